```python
import jax, jax.numpy as jnp
from jax import lax
import numpy as np

D_MODEL = 1024
BATCH = 8
SEQ = 4096
DEPTH = 1

MIX_WIDTH = D_MODEL
POOL_WIDTH = MIX_WIDTH // 2
POOL_WINDOWS = (2, 4, 8, 16)
N_POOL_GROUPS = len(POOL_WINDOWS)
POOL_GROUP_DIM = POOL_WIDTH // N_POOL_GROUPS
RET_WIDTH = MIX_WIDTH - POOL_WIDTH
RET_HEADS = 4
RET_HEAD_DIM = RET_WIDTH // RET_HEADS
RET_CHUNK = 128
ROPE_BASE = 10000.0
IN_WIDTH = POOL_WIDTH + 4 * RET_WIDTH
D_FF = 2816
EPS = 1e-6

kernel_name = "macaron_pool_retention_hybrid"


def rmsnorm(x, gain):
    xf = x.astype(jnp.float32)
    y = xf * lax.rsqrt(jnp.mean(xf * xf, axis=-1, keepdims=True) + EPS)
    return (y * gain.astype(jnp.float32)).astype(x.dtype)


def swiglu(x, w1, w3, w2):
    return (jax.nn.silu(x @ w1) * (x @ w3)) @ w2


def pool_mixer(u, pool_w, pool_scale):
    B, S, _ = u.shape
    ug = u.reshape(B, S, N_POOL_GROUPS, POOL_GROUP_DIM)
    t = jnp.arange(1, S + 1, dtype=jnp.float32)
    outs = []
    for gi, w in enumerate(POOL_WINDOWS):
        xg = ug[:, :, gi, :]
        cs = lax.cumsum(xg.astype(jnp.float32), axis=1)
        csp = jnp.pad(cs, ((0, 0), (w, 0), (0, 0)))
        window_sum = csp[:, w:, :] - csp[:, :S, :]
        count = jnp.minimum(t, float(w))[None, :, None]
        pooled = (window_sum / count).astype(u.dtype) - xg
        outs.append(pooled @ pool_w[gi])
    return jnp.concatenate(outs, axis=-1) * pool_scale


def rope_tables(S, D):
    inv_freq = 1.0 / (ROPE_BASE ** (jnp.arange(0, D, 2, dtype=jnp.float32) / D))
    ang = jnp.arange(S, dtype=jnp.float32)[:, None] * inv_freq[None, :]
    return jnp.cos(ang), jnp.sin(ang)


def apply_rope(t, cos, sin):
    t1, t2 = jnp.split(t, 2, axis=-1)
    c = cos[None, :, None, :]
    s = sin[None, :, None, :]
    return jnp.concatenate([t1 * c - t2 * s, t1 * s + t2 * c], axis=-1)


def retention(q, k, v, g, gain):
    B, S, _ = q.shape
    H, D, C = RET_HEADS, RET_HEAD_DIM, RET_CHUNK
    NC = S // C
    dt = q.dtype

    def heads(t):
        return t.astype(jnp.float32).reshape(B, S, H, D)

    cos, sin = rope_tables(S, D)
    qh = apply_rope(heads(q), cos, sin)
    kh = apply_rope(heads(k), cos, sin) * (D ** -0.5)
    vh = heads(v)

    def chunks(t):
        return t.reshape(B, NC, C, H, D).transpose(0, 3, 1, 2, 4)

    qc, kc, vc = chunks(qh), chunks(kh), chunks(vh)

    log_gamma = jnp.log1p(-jnp.exp2(-5.0 - jnp.arange(H, dtype=jnp.float32)))
    pos = jnp.arange(C, dtype=jnp.float32)
    rel = pos[:, None] - pos[None, :]
    intra_decay = jnp.where(rel[None] >= 0,
                            jnp.exp(log_gamma[:, None, None] * jnp.maximum(rel, 0.0)[None]),
                            0.0)

    scores = jnp.einsum('bhncd,bhnsd->bhncs', qc, kc) * intra_decay[None, :, None]
    o_intra = jnp.einsum('bhncs,bhnse->bhnce', scores, vc)

    k_tail = jnp.exp(log_gamma[:, None] * (C - 1 - pos)[None, :])
    kv = jnp.einsum('bhnsd,bhnse->nbhde', kc * k_tail[None, :, None, :, None], vc)
    chunk_decay = jnp.exp(log_gamma * C)[None, :, None, None]

    def step(R, kv_n):
        return chunk_decay * R + kv_n, R

    _, R_prev = lax.scan(step, jnp.zeros((B, H, D, D), jnp.float32), kv)

    q_head = jnp.exp(log_gamma[:, None] * (pos + 1.0)[None, :])
    o_cross = jnp.einsum('bhncd,nbhde->bhnce', qc * q_head[None, :, None, :, None], R_prev)

    o = (o_intra + o_cross).transpose(0, 2, 3, 1, 4).reshape(B, S, H, D)
    o = o * lax.rsqrt(jnp.mean(o * o, axis=-1, keepdims=True) + EPS)
    o = o.reshape(B, S, H * D) * gain.astype(jnp.float32)
    return (jax.nn.silu(g.astype(jnp.float32)) * o).astype(dt)


def _fwd_setup_inputs(seed: int = 0) -> dict:
    key = jax.random.key(seed)
    ks = jax.random.split(key, 20)
    f32 = jnp.float32

    def nrm(k, shape, fan_in):
        return jax.random.normal(k, shape, f32) * (fan_in ** -0.5)

    def gain(k, shape):
        return 1.0 + 0.05 * jax.random.normal(k, shape, f32)

    L = DEPTH
    return {
        "x": jax.random.normal(ks[0], (BATCH, SEQ, D_MODEL), f32),
        "ffn1_norm": gain(ks[1], (L, D_MODEL)),
        "ffn1_w1": nrm(ks[2], (L, D_MODEL, D_FF), D_MODEL),
        "ffn1_w3": nrm(ks[3], (L, D_MODEL, D_FF), D_MODEL),
        "ffn1_w2": nrm(ks[4], (L, D_FF, D_MODEL), D_FF),
        "mix_norm": gain(ks[5], (L, D_MODEL)),
        "w_in": nrm(ks[6], (L, D_MODEL, IN_WIDTH), D_MODEL),
        "pool_w": nrm(ks[7], (L, N_POOL_GROUPS, POOL_GROUP_DIM, POOL_GROUP_DIM), POOL_GROUP_DIM),
        "pool_scale": gain(ks[8], (L, POOL_WIDTH)),
        "ret_norm": gain(ks[9], (L, RET_WIDTH)),
        "w_out": nrm(ks[10], (L, MIX_WIDTH, D_MODEL), MIX_WIDTH),
        "ffn2_norm": gain(ks[11], (L, D_MODEL)),
        "ffn2_w1": nrm(ks[12], (L, D_MODEL, D_FF), D_MODEL),
        "ffn2_w3": nrm(ks[13], (L, D_MODEL, D_FF), D_MODEL),
        "ffn2_w2": nrm(ks[14], (L, D_FF, D_MODEL), D_FF),
        "final_norm": gain(ks[15], (D_MODEL,)),
    }


def _fwd_reference(x, ffn1_norm, ffn1_w1, ffn1_w3, ffn1_w2, mix_norm, w_in, pool_w, pool_scale,
              ret_norm, w_out, ffn2_norm, ffn2_w1, ffn2_w3, ffn2_w2, final_norm):
    h = x
    for l in range(DEPTH):
        h = h + 0.5 * swiglu(rmsnorm(h, ffn1_norm[l]), ffn1_w1[l], ffn1_w3[l], ffn1_w2[l])

        u = rmsnorm(h, mix_norm[l])
        proj = u @ w_in[l]
        u_pool, q, k, v, g = jnp.split(
            proj, [POOL_WIDTH, POOL_WIDTH + RET_WIDTH, POOL_WIDTH + 2 * RET_WIDTH,
                   POOL_WIDTH + 3 * RET_WIDTH], axis=-1)
        a = pool_mixer(u_pool, pool_w[l], pool_scale[l])
        b = retention(q, k, v, g, ret_norm[l])
        h = h + jnp.concatenate([a, b], axis=-1) @ w_out[l]

        h = h + 0.5 * swiglu(rmsnorm(h, ffn2_norm[l]), ffn2_w1[l], ffn2_w3[l], ffn2_w2[l])
    return rmsnorm(h, final_norm)


import jax as _jax
import jax.numpy as _jnp

TWIN_FORMAT = 'train_step'
FWD_PARAMS = ['x', 'ffn1_norm', 'ffn1_w1', 'ffn1_w3', 'ffn1_w2', 'mix_norm', 'w_in', 'pool_w', 'pool_scale', 'ret_norm', 'w_out', 'ffn2_norm', 'ffn2_w1', 'ffn2_w3', 'ffn2_w2', 'final_norm']
TWIN_WEIGHTS = ['ffn1_norm', 'ffn1_w1', 'ffn1_w3', 'ffn1_w2', 'mix_norm', 'w_in', 'pool_w', 'pool_scale', 'ret_norm', 'w_out', 'ffn2_norm', 'ffn2_w1', 'ffn2_w3', 'ffn2_w2', 'final_norm']
TWIN_DIFF_INPUT = 'x'
TWIN_INPUTS = ['x', 'ffn1_norm', 'ffn1_w1', 'ffn1_w3', 'ffn1_w2', 'mix_norm', 'w_in', 'pool_w', 'pool_scale', 'ret_norm', 'w_out', 'ffn2_norm', 'ffn2_w1', 'ffn2_w3', 'ffn2_w2', 'final_norm', 'loss_target', 'm_ffn1_norm', 'm_ffn1_w1', 'm_ffn1_w3', 'm_ffn1_w2', 'm_mix_norm', 'm_w_in', 'm_pool_w', 'm_pool_scale', 'm_ret_norm', 'm_w_out', 'm_ffn2_norm', 'm_ffn2_w1', 'm_ffn2_w3', 'm_ffn2_w2', 'm_final_norm', 'v_ffn1_norm', 'v_ffn1_w1', 'v_ffn1_w3', 'v_ffn1_w2', 'v_mix_norm', 'v_w_in', 'v_pool_w', 'v_pool_scale', 'v_ret_norm', 'v_w_out', 'v_ffn2_norm', 'v_ffn2_w1', 'v_ffn2_w3', 'v_ffn2_w2', 'v_final_norm']
TWIN_OUTPUTS = ['loss', 'grad_x', 'grad_ffn1_norm', 'grad_ffn1_w1', 'grad_ffn1_w3', 'grad_ffn1_w2', 'grad_mix_norm', 'grad_w_in', 'grad_pool_w', 'grad_pool_scale', 'grad_ret_norm', 'grad_w_out', 'grad_ffn2_norm', 'grad_ffn2_w1', 'grad_ffn2_w3', 'grad_ffn2_w2', 'grad_final_norm', 'delta_ffn1_norm', 'delta_ffn1_w1', 'delta_ffn1_w3', 'delta_ffn1_w2', 'delta_mix_norm', 'delta_w_in', 'delta_pool_w', 'delta_pool_scale', 'delta_ret_norm', 'delta_w_out', 'delta_ffn2_norm', 'delta_ffn2_w1', 'delta_ffn2_w3', 'delta_ffn2_w2', 'delta_final_norm', 'new_m_ffn1_norm', 'new_m_ffn1_w1', 'new_m_ffn1_w3', 'new_m_ffn1_w2', 'new_m_mix_norm', 'new_m_w_in', 'new_m_pool_w', 'new_m_pool_scale', 'new_m_ret_norm', 'new_m_w_out', 'new_m_ffn2_norm', 'new_m_ffn2_w1', 'new_m_ffn2_w3', 'new_m_ffn2_w2', 'new_m_final_norm', 'new_v_ffn1_norm', 'new_v_ffn1_w1', 'new_v_ffn1_w3', 'new_v_ffn1_w2', 'new_v_mix_norm', 'new_v_w_in', 'new_v_pool_w', 'new_v_pool_scale', 'new_v_ret_norm', 'new_v_w_out', 'new_v_ffn2_norm', 'new_v_ffn2_w1', 'new_v_ffn2_w3', 'new_v_ffn2_w2', 'new_v_final_norm']
TWIN_LEAF_KINDS = {'loss': 'loss', 'grad_x': 'grad_x', 'grad_ffn1_norm': 'grad_w', 'grad_ffn1_w1': 'grad_w', 'grad_ffn1_w3': 'grad_w', 'grad_ffn1_w2': 'grad_w', 'grad_mix_norm': 'grad_w', 'grad_w_in': 'grad_w', 'grad_pool_w': 'grad_w', 'grad_pool_scale': 'grad_w', 'grad_ret_norm': 'grad_w', 'grad_w_out': 'grad_w', 'grad_ffn2_norm': 'grad_w', 'grad_ffn2_w1': 'grad_w', 'grad_ffn2_w3': 'grad_w', 'grad_ffn2_w2': 'grad_w', 'grad_final_norm': 'grad_w', 'delta_ffn1_norm': 'delta_w', 'delta_ffn1_w1': 'delta_w', 'delta_ffn1_w3': 'delta_w', 'delta_ffn1_w2': 'delta_w', 'delta_mix_norm': 'delta_w', 'delta_w_in': 'delta_w', 'delta_pool_w': 'delta_w', 'delta_pool_scale': 'delta_w', 'delta_ret_norm': 'delta_w', 'delta_w_out': 'delta_w', 'delta_ffn2_norm': 'delta_w', 'delta_ffn2_w1': 'delta_w', 'delta_ffn2_w3': 'delta_w', 'delta_ffn2_w2': 'delta_w', 'delta_final_norm': 'delta_w', 'new_m_ffn1_norm': 'new_m', 'new_m_ffn1_w1': 'new_m', 'new_m_ffn1_w3': 'new_m', 'new_m_ffn1_w2': 'new_m', 'new_m_mix_norm': 'new_m', 'new_m_w_in': 'new_m', 'new_m_pool_w': 'new_m', 'new_m_pool_scale': 'new_m', 'new_m_ret_norm': 'new_m', 'new_m_w_out': 'new_m', 'new_m_ffn2_norm': 'new_m', 'new_m_ffn2_w1': 'new_m', 'new_m_ffn2_w3': 'new_m', 'new_m_ffn2_w2': 'new_m', 'new_m_final_norm': 'new_m', 'new_v_ffn1_norm': 'new_v', 'new_v_ffn1_w1': 'new_v', 'new_v_ffn1_w3': 'new_v', 'new_v_ffn1_w2': 'new_v', 'new_v_mix_norm': 'new_v', 'new_v_w_in': 'new_v', 'new_v_pool_w': 'new_v', 'new_v_pool_scale': 'new_v', 'new_v_ret_norm': 'new_v', 'new_v_w_out': 'new_v', 'new_v_ffn2_norm': 'new_v', 'new_v_ffn2_w1': 'new_v', 'new_v_ffn2_w3': 'new_v', 'new_v_ffn2_w2': 'new_v', 'new_v_final_norm': 'new_v'}


def _forward(args):
    return _fwd_reference(*[args[k] for k in FWD_PARAMS])


def _output_shape():
    out = _jax.eval_shape(lambda: _forward(_fwd_setup_inputs(0)))
    return out.shape, out.dtype

N_MICROBATCH = 1
ADAM_LR = 0.001
ADAM_B1 = 0.9
ADAM_B2 = 0.999
ADAM_EPS = 1e-08
ADAM_WD = 0.01
ADAM_STEP = 10
PER_EXAMPLE_BATCH_AXIS = {'x': 0, 'loss_target': 0}
SHARED_INPUTS = []
_WEIGHT_DTYPES = {'ffn1_norm': _jnp.float32, 'ffn1_w1': _jnp.float32, 'ffn1_w3': _jnp.float32, 'ffn1_w2': _jnp.float32, 'mix_norm': _jnp.float32, 'w_in': _jnp.float32, 'pool_w': _jnp.float32, 'pool_scale': _jnp.float32, 'ret_norm': _jnp.float32, 'w_out': _jnp.float32, 'ffn2_norm': _jnp.float32, 'ffn2_w1': _jnp.float32, 'ffn2_w3': _jnp.float32, 'ffn2_w2': _jnp.float32, 'final_norm': _jnp.float32}
MOMENT_SCALE = {'ffn1_norm': 9.208341e-02, 'ffn1_w1': 3.772909e-02, 'ffn1_w3': 3.643938e-02, 'ffn1_w2': 6.055299e-02, 'mix_norm': 1.509607e-01, 'w_in': 9.583619e-02, 'pool_w': 1.313277e-01, 'pool_scale': 1.293913e-01, 'ret_norm': 9.378186e-02, 'w_out': 1.118390e-01, 'ffn2_norm': 5.961791e-02, 'ffn2_w1': 2.518456e-02, 'ffn2_w3': 2.451336e-02, 'ffn2_w2': 4.066266e-02, 'final_norm': 3.200890e+01}


def _to_microbatches(a, axis):
    t = _jnp.moveaxis(a, axis, 0)
    t = t.reshape((N_MICROBATCH, t.shape[0] // N_MICROBATCH) + t.shape[1:])
    return _jnp.moveaxis(t, 1, axis + 1)


def setup_inputs(seed: int = 0) -> dict:
    inp = _fwd_setup_inputs(seed)
    key = _jax.random.fold_in(_jax.random.key(seed), 7919)
    shape, _ = _output_shape()
    out = dict(inp)
    out["loss_target"] = _jax.random.normal(_jax.random.fold_in(key, 0), shape, _jnp.float32)
    for i, name in enumerate(TWIN_WEIGHTS):
        w = inp[name].astype(_jnp.float32)
        if MOMENT_SCALE is None:
            s = _jnp.sqrt(_jnp.mean(_jnp.square(w)) + 1e-30)
        else:
            s = MOMENT_SCALE[name]
        km, kv = _jax.random.split(_jax.random.fold_in(key, i + 1))
        out[name] = w
        out["m_" + name] = s * _jax.random.normal(km, w.shape, _jnp.float32)
        out["v_" + name] = (s * s) * _jax.random.uniform(kv, w.shape, _jnp.float32, 0.5, 1.5)
    if N_MICROBATCH > 1:
        for name, axis in PER_EXAMPLE_BATCH_AXIS.items():
            out[name] = _to_microbatches(out[name], axis)
    return {'x': out['x'], 'ffn1_norm': out['ffn1_norm'], 'ffn1_w1': out['ffn1_w1'], 'ffn1_w3': out['ffn1_w3'], 'ffn1_w2': out['ffn1_w2'], 'mix_norm': out['mix_norm'], 'w_in': out['w_in'], 'pool_w': out['pool_w'], 'pool_scale': out['pool_scale'], 'ret_norm': out['ret_norm'], 'w_out': out['w_out'], 'ffn2_norm': out['ffn2_norm'], 'ffn2_w1': out['ffn2_w1'], 'ffn2_w3': out['ffn2_w3'], 'ffn2_w2': out['ffn2_w2'], 'final_norm': out['final_norm'], 'loss_target': out['loss_target'], 'm_ffn1_norm': out['m_ffn1_norm'], 'm_ffn1_w1': out['m_ffn1_w1'], 'm_ffn1_w3': out['m_ffn1_w3'], 'm_ffn1_w2': out['m_ffn1_w2'], 'm_mix_norm': out['m_mix_norm'], 'm_w_in': out['m_w_in'], 'm_pool_w': out['m_pool_w'], 'm_pool_scale': out['m_pool_scale'], 'm_ret_norm': out['m_ret_norm'], 'm_w_out': out['m_w_out'], 'm_ffn2_norm': out['m_ffn2_norm'], 'm_ffn2_w1': out['m_ffn2_w1'], 'm_ffn2_w3': out['m_ffn2_w3'], 'm_ffn2_w2': out['m_ffn2_w2'], 'm_final_norm': out['m_final_norm'], 'v_ffn1_norm': out['v_ffn1_norm'], 'v_ffn1_w1': out['v_ffn1_w1'], 'v_ffn1_w3': out['v_ffn1_w3'], 'v_ffn1_w2': out['v_ffn1_w2'], 'v_mix_norm': out['v_mix_norm'], 'v_w_in': out['v_w_in'], 'v_pool_w': out['v_pool_w'], 'v_pool_scale': out['v_pool_scale'], 'v_ret_norm': out['v_ret_norm'], 'v_w_out': out['v_w_out'], 'v_ffn2_norm': out['v_ffn2_norm'], 'v_ffn2_w1': out['v_ffn2_w1'], 'v_ffn2_w3': out['v_ffn2_w3'], 'v_ffn2_w2': out['v_ffn2_w2'], 'v_final_norm': out['v_final_norm']}


def _loss(weights, diff, rest, loss_target):
    with _jax.named_scope("forward"):
        args = {**rest, TWIN_DIFF_INPUT: diff, **{k: w.astype(_WEIGHT_DTYPES[k]) for k, w in weights.items()}}
        y = _forward(args)
    with _jax.named_scope("loss_head"):
        err = _jnp.square(y.astype(_jnp.float32) - loss_target)
        return 0.5 * _jnp.sum(_jnp.mean(err, axis=-1)) if err.ndim else 0.5 * err


def _adamw(w, g, m, v):
    m = ADAM_B1 * m + (1.0 - ADAM_B1) * g
    v = ADAM_B2 * v + (1.0 - ADAM_B2) * _jnp.square(g)
    m_hat = m / (1.0 - ADAM_B1 ** ADAM_STEP)
    v_hat = v / (1.0 - ADAM_B2 ** ADAM_STEP)
    delta = -ADAM_LR * (m_hat / (_jnp.sqrt(v_hat) + ADAM_EPS) + ADAM_WD * w)
    return delta, m, v


def reference(x, ffn1_norm, ffn1_w1, ffn1_w3, ffn1_w2, mix_norm, w_in, pool_w, pool_scale, ret_norm, w_out, ffn2_norm, ffn2_w1, ffn2_w3, ffn2_w2, final_norm, loss_target, m_ffn1_norm, m_ffn1_w1, m_ffn1_w3, m_ffn1_w2, m_mix_norm, m_w_in, m_pool_w, m_pool_scale, m_ret_norm, m_w_out, m_ffn2_norm, m_ffn2_w1, m_ffn2_w3, m_ffn2_w2, m_final_norm, v_ffn1_norm, v_ffn1_w1, v_ffn1_w3, v_ffn1_w2, v_mix_norm, v_w_in, v_pool_w, v_pool_scale, v_ret_norm, v_w_out, v_ffn2_norm, v_ffn2_w1, v_ffn2_w3, v_ffn2_w2, v_final_norm):
    given = dict(x=x, ffn1_norm=ffn1_norm, ffn1_w1=ffn1_w1, ffn1_w3=ffn1_w3, ffn1_w2=ffn1_w2, mix_norm=mix_norm, w_in=w_in, pool_w=pool_w, pool_scale=pool_scale, ret_norm=ret_norm, w_out=w_out, ffn2_norm=ffn2_norm, ffn2_w1=ffn2_w1, ffn2_w3=ffn2_w3, ffn2_w2=ffn2_w2, final_norm=final_norm, loss_target=loss_target, m_ffn1_norm=m_ffn1_norm, m_ffn1_w1=m_ffn1_w1, m_ffn1_w3=m_ffn1_w3, m_ffn1_w2=m_ffn1_w2, m_mix_norm=m_mix_norm, m_w_in=m_w_in, m_pool_w=m_pool_w, m_pool_scale=m_pool_scale, m_ret_norm=m_ret_norm, m_w_out=m_w_out, m_ffn2_norm=m_ffn2_norm, m_ffn2_w1=m_ffn2_w1, m_ffn2_w3=m_ffn2_w3, m_ffn2_w2=m_ffn2_w2, m_final_norm=m_final_norm, v_ffn1_norm=v_ffn1_norm, v_ffn1_w1=v_ffn1_w1, v_ffn1_w3=v_ffn1_w3, v_ffn1_w2=v_ffn1_w2, v_mix_norm=v_mix_norm, v_w_in=v_w_in, v_pool_w=v_pool_w, v_pool_scale=v_pool_scale, v_ret_norm=v_ret_norm, v_w_out=v_w_out, v_ffn2_norm=v_ffn2_norm, v_ffn2_w1=v_ffn2_w1, v_ffn2_w3=v_ffn2_w3, v_ffn2_w2=v_ffn2_w2, v_final_norm=v_final_norm)
    weights = {n: given[n] for n in TWIN_WEIGHTS}
    shared = {n: given[n] for n in SHARED_INPUTS}
    per_example = {n: given[n] for n in ['x']}
    grad_fn = _jax.value_and_grad(_loss, argnums=(0, 1))

    def one_microbatch(ex, loss_target):
        ex = dict(ex)
        diff = ex.pop(TWIN_DIFF_INPUT)
        return grad_fn(weights, diff, {**shared, **ex}, loss_target)

    if N_MICROBATCH == 1:
        loss, (grad_w, grad_x) = one_microbatch(per_example, given["loss_target"])
    else:
        def body(carry, xs):
            loss_sum, grad_sum = carry
            l_k, (gw_k, gx_k) = one_microbatch(xs[0], xs[1])
            with _jax.named_scope("update"):
                return (loss_sum + l_k, _jax.tree.map(_jnp.add, grad_sum, gw_k)), gx_k

        init = (_jnp.zeros((), _jnp.float32), _jax.tree.map(_jnp.zeros_like, weights))
        (loss, grad_w), grad_x = _jax.lax.scan(body, init, (per_example, given["loss_target"]))
    with _jax.named_scope("update"):
        delta_w, new_m, new_v = {}, {}, {}
        for n in TWIN_WEIGHTS:
            delta_w[n], new_m[n], new_v[n] = _adamw(weights[n], grad_w[n], given["m_" + n], given["v_" + n])
    return (loss, grad_x, *[grad_w[n] for n in TWIN_WEIGHTS], *[delta_w[n] for n in TWIN_WEIGHTS],
            *[new_m[n] for n in TWIN_WEIGHTS], *[new_v[n] for n in TWIN_WEIGHTS])
```

```python
import jax
import jax.numpy as jnp
from jax import lax
from jax.experimental import pallas as pl
from jax.experimental.pallas import tpu as pltpu

F32 = jnp.float32
BF16 = jnp.bfloat16

EPS = 1e-6
N_CHIPS = 4
POOL_WINDOWS = (2, 4, 8, 16)
N_GROUPS = 4
HEAD_DIM = 128
RET_CHUNK = 128
ROPE_BASE = 10000.0
ADAM_LR, ADAM_B1, ADAM_B2, ADAM_EPS, ADAM_WD, ADAM_STEP = 0.001, 0.9, 0.999, 1e-08, 0.01, 10
VMEM_LIMIT_V7X = 56 * 1024 * 1024
MESH = pl.DeviceIdType.MESH


def _dot(a, b):
    return jnp.dot(a, b, preferred_element_type=F32)


def _dot_nt(a, b):
    return lax.dot_general(a, b, (((1,), (1,)), ((), ())), preferred_element_type=F32)


def _dot_tn(a, b):
    return lax.dot_general(a, b, (((0,), (0,)), ((), ())), preferred_element_type=F32)


def _params():
    return pltpu.CompilerParams(vmem_limit_bytes=VMEM_LIMIT_V7X)


def _rstd(h):
    return lax.rsqrt(jnp.mean(h * h, axis=-1, keepdims=True) + EPS)


def _rmsnorm_bwd(dn, h, gain):
    r = _rstd(h)
    nh = h * r
    dnh = dn * gain
    dh = r * (dnh - nh * jnp.mean(dnh * nh, axis=-1, keepdims=True))
    return dh, dn * nh


def _silu_parts(a):
    sig = jax.nn.sigmoid(a)
    silu = a * sig
    return silu, sig + silu * (1.0 - sig)


def ffn_up(h, gain, w1g, w3g, name):
    T, D = h.shape
    nsh, _, Fs = w1g.shape
    tm = min(T, 1024)

    def body(h_ref, g_ref, w1_ref, w3_ref, n_ref, a_ref, b_ref, s_ref):
        @pl.when(pl.program_id(1) == 0)
        def _():
            hh = h_ref[...]
            n_ref[...] = (hh * _rstd(hh) * g_ref[...]).astype(BF16)

        n = n_ref[...]
        a = _dot(n, w1_ref[0])
        b = _dot(n, w3_ref[0])
        a_ref[0] = a.astype(BF16)
        b_ref[0] = b.astype(BF16)
        s_ref[0] = (a * jax.nn.sigmoid(a) * b).astype(BF16)

    act = jax.ShapeDtypeStruct((nsh, T, Fs), BF16)
    act_spec = pl.BlockSpec((1, tm, Fs), lambda i, j: (j, i, 0))
    w_spec = pl.BlockSpec((1, D, Fs), lambda i, j: (j, 0, 0))
    return pl.pallas_call(
        body, name=name, grid=(T // tm, nsh),
        in_specs=[pl.BlockSpec((tm, D), lambda i, j: (i, 0)), pl.BlockSpec((1, D), lambda i, j: (0, 0)), w_spec, w_spec],
        out_specs=[pl.BlockSpec((tm, D), lambda i, j: (i, 0)), act_spec, act_spec, act_spec],
        out_shape=[jax.ShapeDtypeStruct((T, D), BF16), act, act, act],
        compiler_params=_params(),
    )(h, gain, w1g, w3g)


def ffn_down(s, w2g, h, name):
    nsh, T, Fs = s.shape
    D = h.shape[1]
    tm = min(T, 512)

    def body(s_ref, w2_ref, h_ref, o_ref):
        f = _dot(s_ref[0], w2_ref[0])
        for j in range(1, nsh):
            f += _dot(s_ref[j], w2_ref[j])
        o_ref[...] = h_ref[...] + 0.5 * f

    return pl.pallas_call(
        body, name=name, grid=(T // tm,),
        in_specs=[pl.BlockSpec((nsh, tm, Fs), lambda i: (0, i, 0)), pl.BlockSpec((nsh, Fs, D), lambda i: (0, 0, 0)),
                  pl.BlockSpec((tm, D), lambda i: (i, 0))],
        out_specs=pl.BlockSpec((tm, D), lambda i: (i, 0)),
        out_shape=jax.ShapeDtypeStruct((T, D), F32),
        compiler_params=_params(),
    )(s, w2g, h)


def ffn_bwd_act(dh, w2g, a, b, name):
    T, D = dh.shape
    nsh, Fs, _ = w2g.shape
    tm = min(T, 1024)

    def body(dh_ref, w2_ref, a_ref, b_ref, da_ref, db_ref):
        df = (0.5 * dh_ref[...]).astype(BF16)
        ds = _dot_nt(df, w2_ref[0])
        silu, dsilu = _silu_parts(a_ref[0].astype(F32))
        da_ref[0] = (ds * b_ref[0].astype(F32) * dsilu).astype(BF16)
        db_ref[0] = (ds * silu).astype(BF16)

    act = jax.ShapeDtypeStruct((nsh, T, Fs), BF16)
    act_spec = pl.BlockSpec((1, tm, Fs), lambda j, i: (j, i, 0))
    return pl.pallas_call(
        body, name=name, grid=(nsh, T // tm),
        in_specs=[pl.BlockSpec((tm, D), lambda j, i: (i, 0)), pl.BlockSpec((1, Fs, D), lambda j, i: (j, 0, 0)), act_spec, act_spec],
        out_specs=[act_spec, act_spec],
        out_shape=[act, act],
        compiler_params=_params(),
    )(dh, w2g, a, b)


def ffn_dw2(s, dh, name):
    nsh, T, Fs = s.shape
    D = dh.shape[1]
    tk = min(T, 512)
    nk = T // tk

    def body(s_ref, dh_ref, o_ref, acc):
        k = pl.program_id(1)

        @pl.when(k == 0)
        def _():
            acc[...] = jnp.zeros_like(acc)

        acc[...] += _dot_tn(s_ref[0], (0.5 * dh_ref[...]).astype(BF16))

        @pl.when(k == nk - 1)
        def _():
            o_ref[0] = acc[...].astype(BF16)

    return pl.pallas_call(
        body, name=name, grid=(nsh, nk),
        in_specs=[pl.BlockSpec((1, tk, Fs), lambda j, k: (j, k, 0)), pl.BlockSpec((tk, D), lambda j, k: (k, 0))],
        out_specs=pl.BlockSpec((1, Fs, D), lambda j, k: (j, 0, 0)),
        out_shape=jax.ShapeDtypeStruct((nsh, Fs, D), BF16),
        scratch_shapes=[pltpu.VMEM((Fs, D), F32)],
        compiler_params=_params(),
    )(s, dh)


def ffn_dw13(n, da, db, name):
    T, D = n.shape
    nsh, _, Fs = da.shape
    tk = min(T, 512)
    nk = T // tk

    def body(n_ref, da_ref, db_ref, o1_ref, o3_ref, acc1, acc3):
        k = pl.program_id(1)

        @pl.when(k == 0)
        def _():
            acc1[...] = jnp.zeros_like(acc1)
            acc3[...] = jnp.zeros_like(acc3)

        nn = n_ref[...]
        acc1[...] += _dot_tn(nn, da_ref[0])
        acc3[...] += _dot_tn(nn, db_ref[0])

        @pl.when(k == nk - 1)
        def _():
            o1_ref[0] = acc1[...].astype(BF16)
            o3_ref[0] = acc3[...].astype(BF16)

    act_spec = pl.BlockSpec((1, tk, Fs), lambda j, k: (j, k, 0))
    out = jax.ShapeDtypeStruct((nsh, D, Fs), BF16)
    out_spec = pl.BlockSpec((1, D, Fs), lambda j, k: (j, 0, 0))
    return pl.pallas_call(
        body, name=name, grid=(nsh, nk),
        in_specs=[pl.BlockSpec((tk, D), lambda j, k: (k, 0)), act_spec, act_spec],
        out_specs=[out_spec, out_spec],
        out_shape=[out, out],
        scratch_shapes=[pltpu.VMEM((D, Fs), F32), pltpu.VMEM((D, Fs), F32)],
        compiler_params=_params(),
    )(n, da, db)


def ffn_bwd_in(da, db, w1g, w3g, h, gain, dh, name):
    nsh, T, Fs = da.shape
    D = h.shape[1]
    tm = min(T, 256)

    def body(da_ref, db_ref, w1_ref, w3_ref, h_ref, g_ref, dh_ref, o_ref, dg_ref):
        dn = _dot_nt(da_ref[0], w1_ref[0]) + _dot_nt(db_ref[0], w3_ref[0])
        for j in range(1, nsh):
            dn += _dot_nt(da_ref[j], w1_ref[j]) + _dot_nt(db_ref[j], w3_ref[j])
        dhn, dg = _rmsnorm_bwd(dn, h_ref[...], g_ref[...])
        o_ref[...] = dh_ref[...] + dhn

        @pl.when(pl.program_id(0) == 0)
        def _():
            dg_ref[...] = jnp.zeros_like(dg_ref)

        dg_ref[...] += jnp.sum(dg, axis=0, keepdims=True)

    act_spec = pl.BlockSpec((nsh, tm, Fs), lambda i: (0, i, 0))
    w_spec = pl.BlockSpec((nsh, D, Fs), lambda i: (0, 0, 0))
    row_spec = pl.BlockSpec((tm, D), lambda i: (i, 0))
    vec_spec = pl.BlockSpec((1, D), lambda i: (0, 0))
    return pl.pallas_call(
        body, name=name, grid=(T // tm,),
        in_specs=[act_spec, act_spec, w_spec, w_spec, row_spec, vec_spec, row_spec],
        out_specs=[row_spec, vec_spec],
        out_shape=[jax.ShapeDtypeStruct((T, D), F32), jax.ShapeDtypeStruct((1, D), F32)],
        compiler_params=_params(),
    )(da, db, w1g, w3g, h, gain, dh)


def mix_in(h, gain, wing, name):
    T, D = h.shape
    nsh, _, Cs = wing.shape
    tm = min(T, 512)

    def body(h_ref, g_ref, w_ref, u_ref, p_ref):
        hh = h_ref[...]
        u = (hh * _rstd(hh) * g_ref[...]).astype(BF16)
        u_ref[...] = u
        for j in range(nsh):
            p_ref[:, j * Cs:(j + 1) * Cs] = _dot(u, w_ref[j])

    return pl.pallas_call(
        body, name=name, grid=(T // tm,),
        in_specs=[pl.BlockSpec((tm, D), lambda i: (i, 0)), pl.BlockSpec((1, D), lambda i: (0, 0)),
                  pl.BlockSpec((nsh, D, Cs), lambda i: (0, 0, 0))],
        out_specs=[pl.BlockSpec((tm, D), lambda i: (i, 0)), pl.BlockSpec((tm, nsh * Cs), lambda i: (i, 0))],
        out_shape=[jax.ShapeDtypeStruct((T, D), BF16), jax.ShapeDtypeStruct((T, nsh * Cs), F32)],
        compiler_params=_params(),
    )(h, gain, wing)


def mix_out(a, b, woutg, h, name):
    T, W = a.shape
    D = h.shape[1]
    wout = woutg.reshape(2, W, D)
    tm = min(T, 512)

    def body(a_ref, b_ref, w_ref, h_ref, o_ref):
        o_ref[...] = h_ref[...] + _dot(a_ref[...], w_ref[0]) + _dot(b_ref[...], w_ref[1])

    return pl.pallas_call(
        body, name=name, grid=(T // tm,),
        in_specs=[pl.BlockSpec((tm, W), lambda i: (i, 0)), pl.BlockSpec((tm, W), lambda i: (i, 0)),
                  pl.BlockSpec((2, W, D), lambda i: (0, 0, 0)), pl.BlockSpec((tm, D), lambda i: (i, 0))],
        out_specs=pl.BlockSpec((tm, D), lambda i: (i, 0)),
        out_shape=jax.ShapeDtypeStruct((T, D), F32),
        compiler_params=_params(),
    )(a, b, wout, h)


def mix_out_bwd(dh, woutg, a, b, name):
    T, D = dh.shape
    W = a.shape[1]
    nsh, Rs, _ = woutg.shape
    wout = woutg.reshape(2, W, D)
    tk = min(T, 512)
    nk = T // tk

    def body(dh_ref, w_ref, a_ref, b_ref, da_ref, db_ref, dw_ref, acc):
        k = pl.program_id(0)

        @pl.when(k == 0)
        def _():
            acc[...] = jnp.zeros_like(acc)

        dhb = dh_ref[...].astype(BF16)
        da_ref[...] = _dot_nt(dhb, w_ref[0])
        db_ref[...] = _dot_nt(dhb, w_ref[1])
        acc[0:W, :] += _dot_tn(a_ref[...], dhb)
        acc[W:2 * W, :] += _dot_tn(b_ref[...], dhb)

        @pl.when(k == nk - 1)
        def _():
            for j in range(nsh):
                dw_ref[j] = acc[j * Rs:(j + 1) * Rs, :].astype(BF16)

    return pl.pallas_call(
        body, name=name, grid=(nk,),
        in_specs=[pl.BlockSpec((tk, D), lambda k: (k, 0)), pl.BlockSpec((2, W, D), lambda k: (0, 0, 0)),
                  pl.BlockSpec((tk, W), lambda k: (k, 0)), pl.BlockSpec((tk, W), lambda k: (k, 0))],
        out_specs=[pl.BlockSpec((tk, W), lambda k: (k, 0)), pl.BlockSpec((tk, W), lambda k: (k, 0)),
                   pl.BlockSpec((nsh, Rs, D), lambda k: (0, 0, 0))],
        out_shape=[jax.ShapeDtypeStruct((T, W), F32), jax.ShapeDtypeStruct((T, W), F32),
                   jax.ShapeDtypeStruct((nsh, Rs, D), BF16)],
        scratch_shapes=[pltpu.VMEM((2 * W, D), F32)],
        compiler_params=_params(),
    )(dh, wout, a, b)


def mix_dwin(u, d, nsh, name):
    T, D = u.shape
    Cs = d.shape[1] // nsh
    tk = min(T, 512)
    nk = T // tk

    def body(u_ref, d_ref, o_ref, acc):
        k = pl.program_id(1)

        @pl.when(k == 0)
        def _():
            acc[...] = jnp.zeros_like(acc)

        acc[...] += _dot_tn(u_ref[...], d_ref[...])

        @pl.when(k == nk - 1)
        def _():
            o_ref[0] = acc[...].astype(BF16)

    return pl.pallas_call(
        body, name=name, grid=(nsh, nk),
        in_specs=[pl.BlockSpec((tk, D), lambda j, k: (k, 0)), pl.BlockSpec((tk, Cs), lambda j, k: (k, j))],
        out_specs=pl.BlockSpec((1, D, Cs), lambda j, k: (j, 0, 0)),
        out_shape=jax.ShapeDtypeStruct((nsh, D, Cs), BF16),
        scratch_shapes=[pltpu.VMEM((D, Cs), F32)],
        compiler_params=_params(),
    )(u, d)


def mix_in_bwd(d, wing, h, gain, dh, name):
    T, D = h.shape
    nsh, _, Cs = wing.shape
    tm = min(T, 512)

    def body(d_ref, w_ref, h_ref, g_ref, dh_ref, o_ref, dg_ref):
        du = _dot_nt(d_ref[:, 0:Cs], w_ref[0])
        for j in range(1, nsh):
            du += _dot_nt(d_ref[:, j * Cs:(j + 1) * Cs], w_ref[j])
        dhn, dg = _rmsnorm_bwd(du, h_ref[...], g_ref[...])
        o_ref[...] = dh_ref[...] + dhn

        @pl.when(pl.program_id(0) == 0)
        def _():
            dg_ref[...] = jnp.zeros_like(dg_ref)

        dg_ref[...] += jnp.sum(dg, axis=0, keepdims=True)

    row_spec = pl.BlockSpec((tm, D), lambda i: (i, 0))
    vec_spec = pl.BlockSpec((1, D), lambda i: (0, 0))
    return pl.pallas_call(
        body, name=name, grid=(T // tm,),
        in_specs=[pl.BlockSpec((tm, nsh * Cs), lambda i: (i, 0)), pl.BlockSpec((nsh, D, Cs), lambda i: (0, 0, 0)),
                  row_spec, vec_spec, row_spec],
        out_specs=[row_spec, vec_spec],
        out_shape=[jax.ShapeDtypeStruct((T, D), F32), jax.ShapeDtypeStruct((1, D), F32)],
        compiler_params=_params(),
    )(d, wing, h, gain, dh)


def _pool_window(x, group, T, trailing):
    rows = lax.broadcasted_iota(jnp.int32, x.shape, 0)

    def shifted(z, k):
        if trailing:
            return jnp.where(rows >= k, pltpu.roll(z, k, 0), 0.0)
        return jnp.where(rows < T - k, pltpu.roll(z, T - k, 0), 0.0)

    s2 = x + shifted(x, 1)
    s4 = s2 + shifted(s2, 2)
    s8 = s4 + shifted(s4, 4)
    s16 = s8 + shifted(s8, 8)
    return jnp.where(group == 0, s2, jnp.where(group == 1, s4, jnp.where(group == 2, s8, s16)))


def _pool_count(group, shape):
    rows = lax.broadcasted_iota(jnp.int32, shape, 0)
    w = jnp.where(group == 0, 2, jnp.where(group == 1, 4, jnp.where(group == 2, 8, 16)))
    return jnp.minimum(rows + 1, w).astype(F32)


def pool_fwd(proj, pool_w, pool_scale, name):
    T = proj.shape[0]
    Hd = HEAD_DIM

    def body(x_ref, w_ref, sc_ref, a_ref):
        g = pl.program_id(0)
        x = x_ref[...]
        pooled = _pool_window(x, g, T, True) / _pool_count(g, x.shape) - x
        a_ref[...] = (_dot(pooled.astype(BF16), w_ref[0].astype(BF16)) * sc_ref[...]).astype(BF16)

    return pl.pallas_call(
        body, name=name, grid=(N_GROUPS,),
        in_specs=[pl.BlockSpec((T, Hd), lambda g: (0, g)), pl.BlockSpec((1, Hd, Hd), lambda g: (g, 0, 0)),
                  pl.BlockSpec((1, Hd), lambda g: (0, g))],
        out_specs=pl.BlockSpec((T, Hd), lambda g: (0, g)),
        out_shape=jax.ShapeDtypeStruct((T, N_GROUPS * Hd), BF16),
        compiler_params=_params(),
    )(proj, pool_w, pool_scale)


def pool_bwd(proj, da, pool_w, pool_scale, name):
    T = proj.shape[0]
    Hd = HEAD_DIM

    def body(x_ref, da_ref, w_ref, sc_ref, dx_ref, dw_ref, dsc_ref):
        g = pl.program_id(0)
        x = x_ref[...]
        cnt = _pool_count(g, x.shape)
        pooled = (_pool_window(x, g, T, True) / cnt - x).astype(BF16)
        wb = w_ref[0].astype(BF16)
        dav = da_ref[...]
        dsc_ref[...] = jnp.sum(dav * _dot(pooled, wb), axis=0, keepdims=True)
        dout = (dav * sc_ref[...]).astype(BF16)
        dw_ref[0] = _dot_tn(pooled, dout)
        dpooled = _dot_nt(dout, wb)
        dx_ref[...] = (_pool_window(dpooled / cnt, g, T, False) - dpooled).astype(BF16)

    col_spec = pl.BlockSpec((T, Hd), lambda g: (0, g))
    return pl.pallas_call(
        body, name=name, grid=(N_GROUPS,),
        in_specs=[col_spec, col_spec, pl.BlockSpec((1, Hd, Hd), lambda g: (g, 0, 0)), pl.BlockSpec((1, Hd), lambda g: (0, g))],
        out_specs=[col_spec, pl.BlockSpec((1, Hd, Hd), lambda g: (g, 0, 0)), pl.BlockSpec((1, Hd), lambda g: (0, g))],
        out_shape=[jax.ShapeDtypeStruct((T, N_GROUPS * Hd), BF16), jax.ShapeDtypeStruct((N_GROUPS, Hd, Hd), F32),
                   jax.ShapeDtypeStruct((1, N_GROUPS * Hd), F32)],
        compiler_params=_params(),
    )(proj, da, pool_w, pool_scale)


def _ret_tables(T):
    Hd, C = HEAD_DIM, RET_CHUNK
    inv_freq = 1.0 / (ROPE_BASE ** (jnp.arange(0, Hd, 2, dtype=F32) / Hd))
    ang = jnp.arange(T, dtype=F32)[:, None] * inv_freq[None, :]
    cos, sin = jnp.cos(ang), jnp.sin(ang)
    cos2 = jnp.concatenate([cos, cos], axis=-1)
    sin2 = jnp.concatenate([-sin, sin], axis=-1)
    log_gamma = jnp.log1p(-jnp.exp2(-5.0 - jnp.arange(N_GROUPS, dtype=F32)))
    pos = jnp.arange(C, dtype=F32)
    rel = pos[:, None] - pos[None, :]
    intra = jnp.where(rel[None] >= 0, jnp.exp(log_gamma[:, None, None] * jnp.maximum(rel, 0.0)[None]), 0.0)
    k_tail = jnp.exp(log_gamma[:, None] * (C - 1 - pos)[None, :])
    q_head = jnp.exp(log_gamma[:, None] * (pos + 1.0)[None, :])
    chunk_decay = jnp.exp(log_gamma * C)
    wide = lambda t: jnp.broadcast_to(t[:, :, None], (N_GROUPS, C, Hd))
    return cos2, sin2, intra, wide(k_tail), wide(q_head), jnp.broadcast_to(chunk_decay[:, None, None], (N_GROUPS, 1, Hd))


def _rope(x, cos2, sin2):
    return x * cos2 + pltpu.roll(x, HEAD_DIM // 2, 1) * sin2


def _rope_t(d, cos2, sin2):
    return d * cos2 + pltpu.roll(d * sin2, HEAD_DIM // 2, 1)


def _ret_specs(T, tseg, seg_of):
    Hd, G = HEAD_DIM, N_GROUPS
    col = lambda kind: pl.BlockSpec((tseg, Hd), lambda h, s: (seg_of(s), G * kind + h))
    tab = pl.BlockSpec((tseg, Hd), lambda h, s: (seg_of(s), 0))
    head = pl.BlockSpec((1, RET_CHUNK, Hd), lambda h, s: (h, 0, 0))
    cd = pl.BlockSpec((1, 1, Hd), lambda h, s: (h, 0, 0))
    gain = pl.BlockSpec((1, Hd), lambda h, s: (0, h))
    return col, tab, head, cd, gain


def ret_fwd(proj, ret_norm, tables, name):
    T = proj.shape[0]
    Hd, C, G = HEAD_DIM, RET_CHUNK, N_GROUPS
    tseg = min(T, 1024)
    nseg, nck = T // tseg, tseg // C
    scale = Hd ** -0.5
    cos2, sin2, intra, k_tail, q_head, chunk_decay = tables

    def body(q_ref, k_ref, v_ref, g_ref, gain_ref, cos_ref, sin_ref, m_ref, kt_ref, qh_ref, cd_ref,
             b_ref, o_ref, rp_ref, state):
        @pl.when(pl.program_id(1) == 0)
        def _():
            state[...] = jnp.zeros_like(state)

        def chunk(ci, carry):
            rows = pl.ds(pl.multiple_of(ci * C, C), C)
            cos, sin = cos_ref[rows, :], sin_ref[rows, :]
            qr = _rope(q_ref[rows, :], cos, sin)
            kr = _rope(k_ref[rows, :], cos, sin) * scale
            qb, kb, vb = qr.astype(BF16), kr.astype(BF16), v_ref[rows, :].astype(BF16)
            r = state[...]
            rp_ref[0, ci] = r.astype(BF16)
            sc = _dot_nt(qb, kb) * m_ref[0]
            o = _dot(sc.astype(BF16), vb) + _dot((qr * qh_ref[0]).astype(BF16), r.astype(BF16))
            state[...] = cd_ref[0] * r + _dot_tn((kr * kt_ref[0]).astype(BF16), vb)
            o_ref[rows, :] = o
            on = o * _rstd(o)
            b_ref[rows, :] = (jax.nn.silu(g_ref[rows, :]) * (on * gain_ref[...])).astype(BF16)
            return carry

        lax.fori_loop(0, nck, chunk, 0)

    col, tab, head, cd, gain = _ret_specs(T, tseg, lambda s: s)
    out_col = pl.BlockSpec((tseg, Hd), lambda h, s: (s, h))
    return pl.pallas_call(
        body, name=name, grid=(G, nseg),
        in_specs=[col(1), col(2), col(3), col(4), gain, tab, tab, head, head, head, cd],
        out_specs=[out_col, out_col, pl.BlockSpec((1, nck, Hd, Hd), lambda h, s: (h, s, 0, 0))],
        out_shape=[jax.ShapeDtypeStruct((T, G * Hd), BF16), jax.ShapeDtypeStruct((T, G * Hd), F32),
                   jax.ShapeDtypeStruct((G, T // C, Hd, Hd), BF16)],
        scratch_shapes=[pltpu.VMEM((Hd, Hd), F32)],
        compiler_params=_params(),
    )(proj, proj, proj, proj, ret_norm, cos2, sin2, intra, k_tail, q_head, chunk_decay)


def ret_bwd(proj, db, o_pre, r_prev, ret_norm, tables, name):
    T = proj.shape[0]
    Hd, C, G = HEAD_DIM, RET_CHUNK, N_GROUPS
    tseg = min(T, 1024)
    nseg, nck = T // tseg, tseg // C
    scale = Hd ** -0.5
    cos2, sin2, intra, k_tail, q_head, chunk_decay = tables

    def body(q_ref, k_ref, v_ref, g_ref, db_ref, o_ref, rp_ref, gain_ref, cos_ref, sin_ref, m_ref, kt_ref, qh_ref, cd_ref,
             d_ref, dgain_ref, gstate):
        @pl.when(pl.program_id(1) == 0)
        def _():
            gstate[...] = jnp.zeros_like(gstate)
            dgain_ref[...] = jnp.zeros_like(dgain_ref)

        def chunk(t, carry):
            ci = nck - 1 - t
            rows = pl.ds(pl.multiple_of(ci * C, C), C)
            cos, sin = cos_ref[rows, :], sin_ref[rows, :]
            qr = _rope(q_ref[rows, :], cos, sin)
            kr = _rope(k_ref[rows, :], cos, sin) * scale
            qb, kb, vb = qr.astype(BF16), kr.astype(BF16), v_ref[rows, :].astype(BF16)
            qhb, ktb = (qr * qh_ref[0]).astype(BF16), (kr * kt_ref[0]).astype(BF16)
            sc = (_dot_nt(qb, kb) * m_ref[0]).astype(BF16)
            o = o_ref[rows, :]
            rstd = _rstd(o)
            on = o * rstd
            gain = gain_ref[...]
            silu, dsilu = _silu_parts(g_ref[rows, :])
            dy = db_ref[rows, :]
            dgain_ref[...] += jnp.sum(dy * silu * on, axis=0, keepdims=True)
            dg = dy * on * gain * dsilu
            don = dy * silu * gain
            dob = (rstd * (don - on * jnp.mean(don * on, axis=-1, keepdims=True))).astype(BF16)
            gn = gstate[...]
            gb = gn.astype(BF16)
            da = (_dot_nt(dob, vb) * m_ref[0]).astype(BF16)
            dq = _dot(da, kb) + _dot_nt(dob, rp_ref[0, ci]) * qh_ref[0]
            dk = _dot_tn(da, qb) + _dot_nt(vb, gb) * kt_ref[0]
            dv = _dot_tn(sc, dob) + _dot(ktb, gb)
            gstate[...] = cd_ref[0] * gn + _dot_tn(qhb, dob)
            d_ref[0, rows, :] = _rope_t(dq, cos, sin).astype(BF16)
            d_ref[1, rows, :] = _rope_t(dk * scale, cos, sin).astype(BF16)
            d_ref[2, rows, :] = dv.astype(BF16)
            d_ref[3, rows, :] = dg.astype(BF16)
            return carry

        lax.fori_loop(0, nck, chunk, 0)

    rev = lambda s: nseg - 1 - s
    col, tab, head, cd, gain = _ret_specs(T, tseg, rev)
    act = pl.BlockSpec((tseg, Hd), lambda h, s: (rev(s), h))
    return pl.pallas_call(
        body, name=name, grid=(G, nseg),
        in_specs=[col(1), col(2), col(3), col(4), act, act, pl.BlockSpec((1, nck, Hd, Hd), lambda h, s: (h, rev(s), 0, 0)),
                  gain, tab, tab, head, head, head, cd],
        out_specs=[pl.BlockSpec((4, tseg, Hd), lambda h, s: (0, rev(s), h)), gain],
        out_shape=[jax.ShapeDtypeStruct((4, T, G * Hd), BF16), jax.ShapeDtypeStruct((1, G * Hd), F32)],
        scratch_shapes=[pltpu.VMEM((Hd, Hd), F32)],
        compiler_params=_params(),
    )(proj, proj, proj, proj, db, o_pre, r_prev, ret_norm, cos2, sin2, intra, k_tail, q_head, chunk_decay)


def final_loss(h, gain, target, name):
    T, D = h.shape
    tm = min(T, 512)

    def body(h_ref, g_ref, t_ref, dh_ref, loss_ref, dg_ref):
        @pl.when(pl.program_id(0) == 0)
        def _():
            loss_ref[...] = jnp.zeros_like(loss_ref)
            dg_ref[...] = jnp.zeros_like(dg_ref)

        hh = h_ref[...]
        gain_v = g_ref[...]
        err = hh * _rstd(hh) * gain_v - t_ref[...]
        loss_ref[...] += 0.5 * jnp.sum(jnp.mean(err * err, axis=-1, keepdims=True), axis=0, keepdims=True)
        dhn, dg = _rmsnorm_bwd(err * (1.0 / D), hh, gain_v)
        dh_ref[...] = dhn
        dg_ref[...] += jnp.sum(dg, axis=0, keepdims=True)

    row_spec = pl.BlockSpec((tm, D), lambda i: (i, 0))
    vec_spec = pl.BlockSpec((1, D), lambda i: (0, 0))
    return pl.pallas_call(
        body, name=name, grid=(T // tm,),
        in_specs=[row_spec, vec_spec, row_spec],
        out_specs=[row_spec, pl.BlockSpec((1, 128), lambda i: (0, 0)), vec_spec],
        out_shape=[jax.ShapeDtypeStruct((T, D), F32), jax.ShapeDtypeStruct((1, 128), F32), jax.ShapeDtypeStruct((1, D), F32)],
        compiler_params=_params(),
    )(h, gain, target)


def local_step(x, target, gains, wg, pool_w, pool_scale, ret_norm):
    T = x.shape[0]
    tables = _ret_tables(T)
    n1, a1, b1, s1 = ffn_up(x, gains["ffn1"], wg["ffn1_w1"], wg["ffn1_w3"], "ffn1_up")
    h1 = ffn_down(s1, wg["ffn1_w2"], x, "ffn1_down")
    u, proj = mix_in(h1, gains["mix"], wg["w_in"], "mix_in")
    pa = pool_fwd(proj, pool_w, pool_scale, "pool_fwd")
    rb, o_pre, r_prev = ret_fwd(proj, ret_norm, tables, "ret_fwd")
    h2 = mix_out(pa, rb, wg["w_out"], h1, "mix_out")
    n2, a2, b2, s2 = ffn_up(h2, gains["ffn2"], wg["ffn2_w1"], wg["ffn2_w3"], "ffn2_up")
    h3 = ffn_down(s2, wg["ffn2_w2"], h2, "ffn2_down")
    dh3, loss, d_final = final_loss(h3, gains["final"], target, "final_loss")
    g = {}
    da2, db2 = ffn_bwd_act(dh3, wg["ffn2_w2"], a2, b2, "ffn2_bwd_act")
    g["ffn2_w2"] = ffn_dw2(s2, dh3, "ffn2_dw2")
    g["ffn2_w1"], g["ffn2_w3"] = ffn_dw13(n2, da2, db2, "ffn2_dw13")
    dh2, d_ffn2 = ffn_bwd_in(da2, db2, wg["ffn2_w1"], wg["ffn2_w3"], h2, gains["ffn2"], dh3, "ffn2_bwd_in")
    dpa, drb, g["w_out"] = mix_out_bwd(dh2, wg["w_out"], pa, rb, "mix_out_bwd")
    dpool, d_pool_w, d_pool_scale = pool_bwd(proj, dpa, pool_w, pool_scale, "pool_bwd")
    dqkvg, d_ret_norm = ret_bwd(proj, drb, o_pre, r_prev, ret_norm, tables, "ret_bwd")
    d = jnp.concatenate([dpool, dqkvg[0], dqkvg[1], dqkvg[2], dqkvg[3]], axis=1)
    g["w_in"] = mix_dwin(u, d, N_CHIPS, "mix_dwin")
    dh1, d_mix = mix_in_bwd(d, wg["w_in"], h1, gains["mix"], dh2, "mix_in_bwd")
    da1, db1 = ffn_bwd_act(dh1, wg["ffn1_w2"], a1, b1, "ffn1_bwd_act")
    g["ffn1_w2"] = ffn_dw2(s1, dh1, "ffn1_dw2")
    g["ffn1_w1"], g["ffn1_w3"] = ffn_dw13(n1, da1, db1, "ffn1_dw13")
    dx, d_ffn1 = ffn_bwd_in(da1, db1, wg["ffn1_w1"], wg["ffn1_w3"], x, gains["ffn1"], dh1, "ffn1_bwd_in")
    small = {"ffn1_norm": d_ffn1, "mix_norm": d_mix, "pool_w": d_pool_w, "pool_scale": d_pool_scale,
             "ret_norm": d_ret_norm, "ffn2_norm": d_ffn2, "final_norm": d_final}
    return loss, dx, g, small


def _mesh_pos():
    return lax.axis_index("x"), lax.axis_index("y"), lax.axis_index("c")


def _any_specs(n):
    return [pl.BlockSpec(memory_space=pl.ANY)] * n


def chip_exchange(srcs, scatter, name):
    n = len(srcs)
    half_rows = [s.shape[1] if scatter else s.shape[0] // 2 for s in srcs]
    out_shape = [jax.ShapeDtypeStruct((N_CHIPS, 2 * rh, s.shape[-1]), s.dtype) for s, rh in zip(srcs, half_rows)]

    def body(*refs):
        src, out = refs[:n], refs[n:2 * n]
        local_sem, ici_send, ici_recv, d2d_send, d2d_recv = refs[2 * n:]
        x, y, c = _mesh_pos()
        me = 2 * x + y
        sibling = (x, y, 1 - c)
        peers = [(1 - x, y), (x, 1 - y), (1 - x, 1 - y)]

        def slot(t, chip, core):
            return out[t].at[chip, pl.ds(core * half_rows[t], half_rows[t]), :]

        def remote(s, d, sems, k, to):
            return pltpu.make_async_remote_copy(src_ref=s, dst_ref=d, send_sem=sems[0].at[k], recv_sem=sems[1].at[k],
                                                device_id=to, device_id_type=MESH)

        ici, d2d = (ici_send, ici_recv), (d2d_send, d2d_recv)
        started = []
        local = []
        for t in range(n):
            if scatter:
                own = src[t].at[me]
                local.append(pltpu.make_async_copy(own, slot(t, me, c), local_sem.at[t]))
                started.append(remote(own, slot(t, me, c), d2d, 4 * t + 3, sibling))
            else:
                local.append(pltpu.make_async_copy(src[t], out[t].at[me], local_sem.at[t]))
            local[-1].start()
        for t in range(n):
            for p, (px, py) in enumerate(peers):
                piece = src[t].at[2 * px + py] if scatter else src[t].at[pl.ds(c * half_rows[t], half_rows[t]), :]
                started.append(remote(piece, slot(t, me, c), ici, 3 * t + p, (px, py, c)))
        for cp in started:
            cp.start()
        for t in range(n):
            for p, (px, py) in enumerate(peers):
                landed = slot(t, 2 * px + py, c)
                remote(landed, landed, ici, 3 * t + p, (px, py, c)).wait_recv()
                onward = remote(landed, landed, d2d, 4 * t + p, sibling)
                onward.start()
                started.append(onward)
        for t in range(n):
            for p, (px, py) in enumerate(peers):
                landed = slot(t, 2 * px + py, 1 - c)
                remote(landed, landed, d2d, 4 * t + p, sibling).wait_recv()
            if scatter:
                landed = slot(t, me, 1 - c)
                remote(landed, landed, d2d, 4 * t + 3, sibling).wait_recv()
        for cp in started:
            cp.wait_send()
        for cp in local:
            cp.wait()

    return pl.pallas_call(
        body, name=name, in_specs=_any_specs(n), out_specs=_any_specs(n), out_shape=out_shape,
        scratch_shapes=[pltpu.SemaphoreType.DMA((n,)), pltpu.SemaphoreType.DMA((3 * n,)), pltpu.SemaphoreType.DMA((3 * n,)),
                        pltpu.SemaphoreType.DMA((4 * n,)), pltpu.SemaphoreType.DMA((4 * n,))],
    )(*srcs)


def sibling_exchange(grads, name):
    n = len(grads)
    half_rows = [g.shape[1] // 2 for g in grads]
    out_shape = [jax.ShapeDtypeStruct((g.shape[0], rh, g.shape[2]), g.dtype) for g, rh in zip(grads, half_rows)]

    def body(*refs):
        src, out = refs[:n], refs[n:2 * n]
        send_sem, recv_sem = refs[2 * n:]
        x, y, c = _mesh_pos()
        copies = [pltpu.make_async_remote_copy(
            src_ref=src[t].at[:, pl.ds((1 - c) * half_rows[t], half_rows[t]), :], dst_ref=out[t],
            send_sem=send_sem.at[t], recv_sem=recv_sem.at[t], device_id=(x, y, 1 - c), device_id_type=MESH) for t in range(n)]
        for cp in copies:
            cp.start()
        for cp in copies:
            cp.wait()

    return pl.pallas_call(
        body, name=name, in_specs=_any_specs(n), out_specs=_any_specs(n), out_shape=out_shape,
        scratch_shapes=[pltpu.SemaphoreType.DMA((n,)), pltpu.SemaphoreType.DMA((n,))],
    )(*grads)


def prereduce(grad, recv, core, name):
    nsh, R, C = grad.shape
    rh = R // 2

    def body(c_ref, g_ref, r_ref, o_ref):
        o_ref[...] = (g_ref[...].astype(F32) + r_ref[...].astype(F32)).astype(BF16)

    return pl.pallas_call(
        body, name=name,
        grid_spec=pltpu.PrefetchScalarGridSpec(
            num_scalar_prefetch=1, grid=(nsh,),
            in_specs=[pl.BlockSpec((1, rh, C), lambda j, c_ref: (j, c_ref[0], 0)), pl.BlockSpec((1, rh, C), lambda j, c_ref: (j, 0, 0))],
            out_specs=pl.BlockSpec((1, rh, C), lambda j, c_ref: (j, 0, 0))),
        out_shape=jax.ShapeDtypeStruct((nsh, rh, C), BF16),
        compiler_params=_params(),
    )(core, grad, recv)


def all_exchange_small(pack, name):
    R, L = pack.shape
    flips = [(dx, dy, dc) for dx in (0, 1) for dy in (0, 1) for dc in (0, 1)][1:]

    def body(src, out, local_sem, send_sem, recv_sem):
        x, y, c = _mesh_pos()
        me = 4 * x + 2 * y + c
        local = pltpu.make_async_copy(src, out.at[me], local_sem)
        local.start()
        copies = []
        for k, (dx, dy, dc) in enumerate(flips):
            copies.append(pltpu.make_async_remote_copy(
                src_ref=src, dst_ref=out.at[me], send_sem=send_sem.at[k], recv_sem=recv_sem.at[k],
                device_id=(x ^ dx, y ^ dy, c ^ dc), device_id_type=MESH))
        for cp in copies:
            cp.start()
        for k, (dx, dy, dc) in enumerate(flips):
            landed = out.at[4 * (x ^ dx) + 2 * (y ^ dy) + (c ^ dc)]
            pltpu.make_async_remote_copy(src_ref=landed, dst_ref=landed, send_sem=send_sem.at[k], recv_sem=recv_sem.at[k],
                                         device_id=(x ^ dx, y ^ dy, c ^ dc), device_id_type=MESH).wait_recv()
        for cp in copies:
            cp.wait_send()
        local.wait()

    return pl.pallas_call(
        body, name=name, in_specs=_any_specs(1), out_specs=pl.BlockSpec(memory_space=pl.ANY),
        out_shape=jax.ShapeDtypeStruct((2 * N_CHIPS, R, L), pack.dtype),
        scratch_shapes=[pltpu.SemaphoreType.DMA, pltpu.SemaphoreType.DMA((7,)), pltpu.SemaphoreType.DMA((7,))],
    )(pack)


def _adamw(w, g, m, v):
    m = ADAM_B1 * m + (1.0 - ADAM_B1) * g
    v = ADAM_B2 * v + (1.0 - ADAM_B2) * (g * g)
    m_hat = m / (1.0 - ADAM_B1 ** ADAM_STEP)
    v_hat = v / (1.0 - ADAM_B2 ** ADAM_STEP)
    return -ADAM_LR * (m_hat / (jnp.sqrt(v_hat) + ADAM_EPS) + ADAM_WD * w), m, v


def adamw_sharded(pieces, w, m, v, name):
    nsh, R, C = pieces.shape
    tr = 256 if R % 256 == 0 else R // 2

    def body(p_ref, w_ref, m_ref, v_ref, g_ref, d_ref, nm_ref, nv_ref):
        g = p_ref[0].astype(F32)
        for i in range(1, nsh):
            g += p_ref[i].astype(F32)
        g_ref[...] = g
        d_ref[...], nm_ref[...], nv_ref[...] = _adamw(w_ref[...], g, m_ref[...], v_ref[...])

    spec = pl.BlockSpec((tr, C), lambda i: (i, 0))
    out = jax.ShapeDtypeStruct((R, C), F32)
    return pl.pallas_call(
        body, name=name, grid=(R // tr,),
        in_specs=[pl.BlockSpec((nsh, tr, C), lambda i: (0, i, 0)), spec, spec, spec],
        out_specs=[spec] * 4, out_shape=[out] * 4,
        compiler_params=_params(),
    )(pieces, w, m, v)


def adamw_small(packs, w, m, v, name):
    ndev, R, L = packs.shape

    def body(p_ref, w_ref, m_ref, v_ref, g_ref, d_ref, nm_ref, nv_ref):
        g = p_ref[0]
        for i in range(1, ndev):
            g += p_ref[i]
        g_ref[...] = g
        d_ref[...], nm_ref[...], nv_ref[...] = _adamw(w_ref[...], g, m_ref[...], v_ref[...])

    out = jax.ShapeDtypeStruct((R, L), F32)
    return pl.pallas_call(body, name=name, out_shape=[out] * 4, compiler_params=_params())(packs, w, m, v)


BIG = ("ffn1_w1", "ffn1_w3", "ffn1_w2", "w_in", "w_out", "ffn2_w1", "ffn2_w3", "ffn2_w2")
SMALL = ("ffn1_norm", "mix_norm", "pool_w", "pool_scale", "ret_norm", "ffn2_norm", "final_norm")
WEIGHTS = ("ffn1_norm", "ffn1_w1", "ffn1_w3", "ffn1_w2", "mix_norm", "w_in", "pool_w", "pool_scale", "ret_norm", "w_out",
           "ffn2_norm", "ffn2_w1", "ffn2_w3", "ffn2_w2", "final_norm")


def _pack(parts):
    return jnp.concatenate([parts[k].reshape(-1, 128) for k in SMALL], axis=0)


def _unpack(pack, like):
    out, row = {}, 0
    for k in SMALL:
        rows = like[k].size // 128
        out[k] = pack[row:row + rows].reshape(like[k].shape)
        row += rows
    return out


def kernel(x, ffn1_norm, ffn1_w1, ffn1_w3, ffn1_w2, mix_norm, w_in, pool_w, pool_scale, ret_norm, w_out, ffn2_norm, ffn2_w1, ffn2_w3, ffn2_w2, final_norm, loss_target, m_ffn1_norm, m_ffn1_w1, m_ffn1_w3, m_ffn1_w2, m_mix_norm, m_w_in, m_pool_w, m_pool_scale, m_ret_norm, m_w_out, m_ffn2_norm, m_ffn2_w1, m_ffn2_w3, m_ffn2_w2, m_final_norm, v_ffn1_norm, v_ffn1_w1, v_ffn1_w3, v_ffn1_w2, v_mix_norm, v_w_in, v_pool_w, v_pool_scale, v_ret_norm, v_w_out, v_ffn2_norm, v_ffn2_w1, v_ffn2_w3, v_ffn2_w2, v_final_norm):
    w = dict(ffn1_norm=ffn1_norm, ffn1_w1=ffn1_w1, ffn1_w3=ffn1_w3, ffn1_w2=ffn1_w2, mix_norm=mix_norm, w_in=w_in, pool_w=pool_w,
             pool_scale=pool_scale, ret_norm=ret_norm, w_out=w_out, ffn2_norm=ffn2_norm, ffn2_w1=ffn2_w1, ffn2_w3=ffn2_w3,
             ffn2_w2=ffn2_w2, final_norm=final_norm)
    m = dict(ffn1_norm=m_ffn1_norm, ffn1_w1=m_ffn1_w1, ffn1_w3=m_ffn1_w3, ffn1_w2=m_ffn1_w2, mix_norm=m_mix_norm, w_in=m_w_in,
             pool_w=m_pool_w, pool_scale=m_pool_scale, ret_norm=m_ret_norm, w_out=m_w_out, ffn2_norm=m_ffn2_norm, ffn2_w1=m_ffn2_w1,
             ffn2_w3=m_ffn2_w3, ffn2_w2=m_ffn2_w2, final_norm=m_final_norm)
    v = dict(ffn1_norm=v_ffn1_norm, ffn1_w1=v_ffn1_w1, ffn1_w3=v_ffn1_w3, ffn1_w2=v_ffn1_w2, mix_norm=v_mix_norm, w_in=v_w_in,
             pool_w=v_pool_w, pool_scale=v_pool_scale, ret_norm=v_ret_norm, w_out=v_w_out, ffn2_norm=v_ffn2_norm, ffn2_w1=v_ffn2_w1,
             ffn2_w3=v_ffn2_w3, ffn2_w2=v_ffn2_w2, final_norm=v_final_norm)

    shards = [w[k][0].astype(BF16) for k in BIG]
    gathered = chip_exchange(shards[:3], False, "gather_ffn1") + chip_exchange(shards[3:], False, "gather_rest")
    wg = dict(zip(BIG, gathered))

    gains = {"ffn1": ffn1_norm, "mix": mix_norm, "ffn2": ffn2_norm, "final": final_norm[None]}
    loss, dx, g, small = local_step(x[0], loss_target[0], gains, wg, pool_w[0], pool_scale, ret_norm)
    loss = lax.psum(loss[0, 0], ("x", "y", "c"))

    core = lax.axis_index("c").astype(jnp.int32).reshape(1)
    partial = [g[k] for k in BIG]
    from_sibling = sibling_exchange(partial, "rs_sibling")
    reduced = [prereduce(p, r, core, "rs_prereduce_" + k) for k, p, r in zip(BIG, partial, from_sibling)]
    pieces = chip_exchange(reduced, True, "rs_scatter")
    grad, delta, new_m, new_v = {}, {}, {}, {}
    for k, p in zip(BIG, pieces):
        outs = adamw_sharded(p, w[k][0], m[k][0], v[k][0], "adamw_" + k)
        grad[k], delta[k], new_m[k], new_v[k] = [o[None] for o in outs]

    packs = all_exchange_small(_pack(small), "gather_small")
    outs = adamw_small(packs, _pack(w), _pack(m), _pack(v), "adamw_small")
    for res, pack in zip((grad, delta, new_m, new_v), outs):
        res.update(_unpack(pack, w))

    return (loss, dx[None], *[grad[k] for k in WEIGHTS], *[delta[k] for k in WEIGHTS],
            *[new_m[k] for k in WEIGHTS], *[new_v[k] for k in WEIGHTS])
```

```python
import math

import jax
import jax.numpy as jnp
from jax import lax
from jax.experimental import pallas as pl
from jax.experimental.pallas import tpu as pltpu

F32 = jnp.float32
BF16 = jnp.bfloat16

EPS = 1e-6
N_CHIPS = 4
N_GROUPS = 4
HEAD_DIM = 128
RET_CHUNK = 128
ROPE_BASE = 10000.0
ADAM_LR, ADAM_B1, ADAM_B2, ADAM_EPS, ADAM_WD, ADAM_STEP = 0.001, 0.9, 0.999, 1e-08, 0.01, 10
VMEM_LIMIT_V7X = 56 * 1024 * 1024
MESH = pl.DeviceIdType.MESH
ANY = pl.BlockSpec(memory_space=pl.ANY)


def _dot(a, b):
    return jnp.dot(a, b, preferred_element_type=F32)


def _dot_nt(a, b):
    return lax.dot_general(a, b, (((1,), (1,)), ((), ())), preferred_element_type=F32)


def _dot_tn(a, b):
    return lax.dot_general(a, b, (((0,), (0,)), ((), ())), preferred_element_type=F32)


def _rstd(h):
    return lax.rsqrt(jnp.mean(h * h, axis=-1, keepdims=True) + EPS)


def _rmsnorm_bwd(dn, h, gain):
    r = _rstd(h)
    nh = h * r
    dnh = dn * gain
    dh = r * (dnh - nh * jnp.mean(dnh * nh, axis=-1, keepdims=True))
    return dh, dn * nh


def _silu_parts(a):
    sig = jax.nn.sigmoid(a)
    silu = a * sig
    return silu, sig + silu * (1.0 - sig)


def _mesh_pos():
    return lax.axis_index("x"), lax.axis_index("y"), lax.axis_index("c")


class ChipExchange:
    def __init__(self, srcs, scatter):
        n = len(srcs)
        self.srcs, self.scatter, self.n = list(srcs), scatter, n
        self.half_rows = [s.shape[1] if scatter else s.shape[0] // 2 for s in srcs]
        self.out_shape = [jax.ShapeDtypeStruct((N_CHIPS, 2 * rh, s.shape[-1]), s.dtype) for s, rh in zip(srcs, self.half_rows)]
        self.sems = [pltpu.SemaphoreType.DMA((n,)), pltpu.SemaphoreType.DMA((3 * n,)), pltpu.SemaphoreType.DMA((3 * n,)),
                     pltpu.SemaphoreType.DMA((4 * n,)), pltpu.SemaphoreType.DMA((4 * n,))]

    def _plan(self, src, out, sems):
        local_sem, ici_send, ici_recv, d2d_send, d2d_recv = sems
        x, y, c = _mesh_pos()
        me = 2 * x + y
        sibling = (x, y, 1 - c)
        peers = [(1 - x, y), (x, 1 - y), (1 - x, 1 - y)]
        rows = self.half_rows

        def slot(t, chip, core):
            return out[t].at[chip, pl.ds(core * rows[t], rows[t]), :]

        def remote(s, d, pair, k, to):
            return pltpu.make_async_remote_copy(src_ref=s, dst_ref=d, send_sem=pair[0].at[k], recv_sem=pair[1].at[k],
                                                device_id=to, device_id_type=MESH)

        ici, d2d = (ici_send, ici_recv), (d2d_send, d2d_recv)
        local, own, sends, landed, onward, arrivals = [], [], [], [], [], []
        for t in range(self.n):
            if self.scatter:
                piece = src[t].at[me]
                local.append(pltpu.make_async_copy(piece, slot(t, me, c), local_sem.at[t]))
                own.append(remote(piece, slot(t, me, c), d2d, 4 * t + 3, sibling))
                there = slot(t, me, 1 - c)
                arrivals.append(remote(there, there, d2d, 4 * t + 3, sibling))
            else:
                local.append(pltpu.make_async_copy(src[t], out[t].at[me], local_sem.at[t]))
            for p, (px, py) in enumerate(peers):
                piece = src[t].at[2 * px + py] if self.scatter else src[t].at[pl.ds(c * rows[t], rows[t]), :]
                sends.append(remote(piece, slot(t, me, c), ici, 3 * t + p, (px, py, c)))
                here = slot(t, 2 * px + py, c)
                landed.append(remote(here, here, ici, 3 * t + p, (px, py, c)))
                onward.append(remote(here, here, d2d, 4 * t + p, sibling))
                there = slot(t, 2 * px + py, 1 - c)
                arrivals.append(remote(there, there, d2d, 4 * t + p, sibling))
        return local, own, sends, landed, onward, arrivals

    def start(self, src, out, sems):
        local, own, sends, _, _, _ = self._plan(src, out, sems)
        for cp in local + own + sends:
            cp.start()

    def mid(self, src, out, sems):
        _, _, _, landed, onward, _ = self._plan(src, out, sems)
        for arrived, cp in zip(landed, onward):
            arrived.wait_recv()
            cp.start()

    def finish(self, src, out, sems):
        local, own, sends, _, onward, arrivals = self._plan(src, out, sems)
        for cp in arrivals:
            cp.wait_recv()
        for cp in own + sends + onward:
            cp.wait_send()
        for cp in local:
            cp.wait()


class SiblingExchange:
    def __init__(self, grads):
        self.srcs, self.n = list(grads), len(grads)
        self.half_rows = [g.shape[1] // 2 for g in grads]
        self.out_shape = [jax.ShapeDtypeStruct((g.shape[0], rh, g.shape[2]), g.dtype) for g, rh in zip(grads, self.half_rows)]
        self.sems = [pltpu.SemaphoreType.DMA((self.n,)), pltpu.SemaphoreType.DMA((self.n,))]

    def _plan(self, src, out, sems):
        x, y, c = _mesh_pos()
        return [pltpu.make_async_remote_copy(
            src_ref=src[t].at[:, pl.ds((1 - c) * self.half_rows[t], self.half_rows[t]), :], dst_ref=out[t],
            send_sem=sems[0].at[t], recv_sem=sems[1].at[t], device_id=(x, y, 1 - c), device_id_type=MESH) for t in range(self.n)]

    def start(self, src, out, sems):
        for cp in self._plan(src, out, sems):
            cp.start()

    def mid(self, src, out, sems):
        pass

    def finish(self, src, out, sems):
        for cp in self._plan(src, out, sems):
            cp.wait()


def _call(body, hosted=(), *, name, in_specs, out_specs, out_shape, args, grid=(), scratch_shapes=()):
    n_in, n_out, n_scr = len(in_specs), len(out_specs), len(scratch_shapes)
    total = math.prod(grid)
    mid_step = max(0, (3 * total) // 4 - 1)

    def full(*refs):
        pos = [0]

        def take(k):
            pos[0] += k
            return refs[pos[0] - k:pos[0]]

        ins, h_in = take(n_in), [take(h.n) for h in hosted]
        outs, h_out = take(n_out), [take(h.n) for h in hosted]
        scr, h_sem = take(n_scr), [take(len(h.sems)) for h in hosted]
        step = 0
        for axis, size in enumerate(grid):
            step = step * size + pl.program_id(axis)

        def phase(at, method):
            if not hosted:
                return
            if total == 1:
                for h, s, o, m in zip(hosted, h_in, h_out, h_sem):
                    getattr(h, method)(s, o, m)
                return

            @pl.when(step == at)
            def _():
                for h, s, o, m in zip(hosted, h_in, h_out, h_sem):
                    getattr(h, method)(s, o, m)

        phase(0, "start")
        body(*ins, *outs, *scr)
        phase(mid_step, "mid")
        phase(total - 1, "finish")

    n_h = sum(h.n for h in hosted)
    results = pl.pallas_call(
        full, name=name, grid=grid,
        in_specs=list(in_specs) + [ANY] * n_h, out_specs=list(out_specs) + [ANY] * n_h,
        out_shape=list(out_shape) + [s for h in hosted for s in h.out_shape],
        scratch_shapes=list(scratch_shapes) + [s for h in hosted for s in h.sems],
        compiler_params=pltpu.CompilerParams(vmem_limit_bytes=VMEM_LIMIT_V7X),
    )(*args, *[s for h in hosted for s in h.srcs])
    outs, extras, pos = list(results[:n_out]), [], n_out
    for h in hosted:
        extras.append(list(results[pos:pos + h.n]))
        pos += h.n
    return outs, extras


def exchange(hosted, name):
    return _call(lambda: None, hosted, name=name, in_specs=[], out_specs=[], out_shape=[], args=[])[1]


def all_exchange_small(pack, name):
    R, L = pack.shape
    flips = [(dx, dy, dc) for dx in (0, 1) for dy in (0, 1) for dc in (0, 1)][1:]

    def body(src, out, local_sem, send_sem, recv_sem):
        x, y, c = _mesh_pos()
        me = 4 * x + 2 * y + c
        local = pltpu.make_async_copy(src, out.at[me], local_sem)
        local.start()
        copies = []
        for k, (dx, dy, dc) in enumerate(flips):
            copies.append(pltpu.make_async_remote_copy(
                src_ref=src, dst_ref=out.at[me], send_sem=send_sem.at[k], recv_sem=recv_sem.at[k],
                device_id=(x ^ dx, y ^ dy, c ^ dc), device_id_type=MESH))
        for cp in copies:
            cp.start()
        for k, (dx, dy, dc) in enumerate(flips):
            landed = out.at[4 * (x ^ dx) + 2 * (y ^ dy) + (c ^ dc)]
            pltpu.make_async_remote_copy(src_ref=landed, dst_ref=landed, send_sem=send_sem.at[k], recv_sem=recv_sem.at[k],
                                         device_id=(x ^ dx, y ^ dy, c ^ dc), device_id_type=MESH).wait_recv()
        for cp in copies:
            cp.wait_send()
        local.wait()

    return pl.pallas_call(
        body, name=name, in_specs=[ANY], out_specs=ANY,
        out_shape=jax.ShapeDtypeStruct((2 * N_CHIPS, R, L), pack.dtype),
        scratch_shapes=[pltpu.SemaphoreType.DMA, pltpu.SemaphoreType.DMA((7,)), pltpu.SemaphoreType.DMA((7,))],
    )(pack)


def ffn_up(h, gain, w1g, w3g, name, hosted=()):
    T, D = h.shape
    nsh, _, Fs = w1g.shape
    tm = min(T, 1024)

    def body(h_ref, g_ref, w1_ref, w3_ref, n_ref, a_ref, b_ref, s_ref):
        @pl.when(pl.program_id(1) == 0)
        def _():
            hh = h_ref[...]
            n_ref[...] = (hh * _rstd(hh) * g_ref[...]).astype(BF16)

        n = n_ref[...]
        a = _dot(n, w1_ref[0])
        b = _dot(n, w3_ref[0])
        a_ref[0] = a.astype(BF16)
        b_ref[0] = b.astype(BF16)
        s_ref[0] = (a * jax.nn.sigmoid(a) * b).astype(BF16)

    act = jax.ShapeDtypeStruct((nsh, T, Fs), BF16)
    act_spec = pl.BlockSpec((1, tm, Fs), lambda i, j: (j, i, 0))
    w_spec = pl.BlockSpec((1, D, Fs), lambda i, j: (j, 0, 0))
    return _call(
        body, hosted, name=name, grid=(T // tm, nsh),
        in_specs=[pl.BlockSpec((tm, D), lambda i, j: (i, 0)), pl.BlockSpec((1, D), lambda i, j: (0, 0)), w_spec, w_spec],
        out_specs=[pl.BlockSpec((tm, D), lambda i, j: (i, 0)), act_spec, act_spec, act_spec],
        out_shape=[jax.ShapeDtypeStruct((T, D), BF16), act, act, act],
        args=[h, gain, w1g, w3g])


def ffn_down(s, w2g, h, name, hosted=()):
    nsh, T, Fs = s.shape
    D = h.shape[1]
    tm = min(T, 512)

    def body(s_ref, w2_ref, h_ref, o_ref):
        f = _dot(s_ref[0], w2_ref[0])
        for j in range(1, nsh):
            f += _dot(s_ref[j], w2_ref[j])
        o_ref[...] = h_ref[...] + 0.5 * f

    return _call(
        body, hosted, name=name, grid=(T // tm,),
        in_specs=[pl.BlockSpec((nsh, tm, Fs), lambda i: (0, i, 0)), pl.BlockSpec((nsh, Fs, D), lambda i: (0, 0, 0)),
                  pl.BlockSpec((tm, D), lambda i: (i, 0))],
        out_specs=[pl.BlockSpec((tm, D), lambda i: (i, 0))],
        out_shape=[jax.ShapeDtypeStruct((T, D), F32)],
        args=[s, w2g, h])


def ffn_bwd_act(dh, w2g, a, b, name, hosted=()):
    T, D = dh.shape
    nsh, Fs, _ = w2g.shape
    tm = min(T, 1024)

    def body(dh_ref, w2_ref, a_ref, b_ref, da_ref, db_ref):
        df = (0.5 * dh_ref[...]).astype(BF16)
        ds = _dot_nt(df, w2_ref[0])
        silu, dsilu = _silu_parts(a_ref[0].astype(F32))
        da_ref[0] = (ds * b_ref[0].astype(F32) * dsilu).astype(BF16)
        db_ref[0] = (ds * silu).astype(BF16)

    act = jax.ShapeDtypeStruct((nsh, T, Fs), BF16)
    act_spec = pl.BlockSpec((1, tm, Fs), lambda j, i: (j, i, 0))
    return _call(
        body, hosted, name=name, grid=(nsh, T // tm),
        in_specs=[pl.BlockSpec((tm, D), lambda j, i: (i, 0)), pl.BlockSpec((1, Fs, D), lambda j, i: (j, 0, 0)), act_spec, act_spec],
        out_specs=[act_spec, act_spec],
        out_shape=[act, act],
        args=[dh, w2g, a, b])


def ffn_dw2(s, dh, name, hosted=()):
    nsh, T, Fs = s.shape
    D = dh.shape[1]
    tk = min(T, 512)
    nk = T // tk

    def body(s_ref, dh_ref, o_ref, acc):
        k = pl.program_id(1)

        @pl.when(k == 0)
        def _():
            acc[...] = jnp.zeros_like(acc)

        acc[...] += _dot_tn(s_ref[0], (0.5 * dh_ref[...]).astype(BF16))

        @pl.when(k == nk - 1)
        def _():
            o_ref[0] = acc[...].astype(BF16)

    return _call(
        body, hosted, name=name, grid=(nsh, nk),
        in_specs=[pl.BlockSpec((1, tk, Fs), lambda j, k: (j, k, 0)), pl.BlockSpec((tk, D), lambda j, k: (k, 0))],
        out_specs=[pl.BlockSpec((1, Fs, D), lambda j, k: (j, 0, 0))],
        out_shape=[jax.ShapeDtypeStruct((nsh, Fs, D), BF16)],
        scratch_shapes=[pltpu.VMEM((Fs, D), F32)],
        args=[s, dh])


def ffn_dw13(n, da, db, name, hosted=()):
    T, D = n.shape
    nsh, _, Fs = da.shape
    tk = min(T, 512)
    nk = T // tk

    def body(n_ref, da_ref, db_ref, o1_ref, o3_ref, acc1, acc3):
        k = pl.program_id(1)

        @pl.when(k == 0)
        def _():
            acc1[...] = jnp.zeros_like(acc1)
            acc3[...] = jnp.zeros_like(acc3)

        nn = n_ref[...]
        acc1[...] += _dot_tn(nn, da_ref[0])
        acc3[...] += _dot_tn(nn, db_ref[0])

        @pl.when(k == nk - 1)
        def _():
            o1_ref[0] = acc1[...].astype(BF16)
            o3_ref[0] = acc3[...].astype(BF16)

    act_spec = pl.BlockSpec((1, tk, Fs), lambda j, k: (j, k, 0))
    out = jax.ShapeDtypeStruct((nsh, D, Fs), BF16)
    out_spec = pl.BlockSpec((1, D, Fs), lambda j, k: (j, 0, 0))
    return _call(
        body, hosted, name=name, grid=(nsh, nk),
        in_specs=[pl.BlockSpec((tk, D), lambda j, k: (k, 0)), act_spec, act_spec],
        out_specs=[out_spec, out_spec],
        out_shape=[out, out],
        scratch_shapes=[pltpu.VMEM((D, Fs), F32), pltpu.VMEM((D, Fs), F32)],
        args=[n, da, db])


def ffn_bwd_in(da, db, w1g, w3g, h, gain, dh, name, hosted=()):
    nsh, T, Fs = da.shape
    D = h.shape[1]
    tm = min(T, 256)

    def body(da_ref, db_ref, w1_ref, w3_ref, h_ref, g_ref, dh_ref, o_ref, dg_ref):
        dn = _dot_nt(da_ref[0], w1_ref[0]) + _dot_nt(db_ref[0], w3_ref[0])
        for j in range(1, nsh):
            dn += _dot_nt(da_ref[j], w1_ref[j]) + _dot_nt(db_ref[j], w3_ref[j])
        dhn, dg = _rmsnorm_bwd(dn, h_ref[...], g_ref[...])
        o_ref[...] = dh_ref[...] + dhn

        @pl.when(pl.program_id(0) == 0)
        def _():
            dg_ref[...] = jnp.zeros_like(dg_ref)

        dg_ref[...] += jnp.sum(dg, axis=0, keepdims=True)

    act_spec = pl.BlockSpec((nsh, tm, Fs), lambda i: (0, i, 0))
    w_spec = pl.BlockSpec((nsh, D, Fs), lambda i: (0, 0, 0))
    row_spec = pl.BlockSpec((tm, D), lambda i: (i, 0))
    vec_spec = pl.BlockSpec((1, D), lambda i: (0, 0))
    return _call(
        body, hosted, name=name, grid=(T // tm,),
        in_specs=[act_spec, act_spec, w_spec, w_spec, row_spec, vec_spec, row_spec],
        out_specs=[row_spec, vec_spec],
        out_shape=[jax.ShapeDtypeStruct((T, D), F32), jax.ShapeDtypeStruct((1, D), F32)],
        args=[da, db, w1g, w3g, h, gain, dh])


def mix_in(h, gain, wing, name, hosted=()):
    T, D = h.shape
    nsh, _, Cs = wing.shape
    tm = min(T, 512)

    def body(h_ref, g_ref, w_ref, u_ref, p_ref):
        hh = h_ref[...]
        u = (hh * _rstd(hh) * g_ref[...]).astype(BF16)
        u_ref[...] = u
        for j in range(nsh):
            p_ref[:, j * Cs:(j + 1) * Cs] = _dot(u, w_ref[j])

    return _call(
        body, hosted, name=name, grid=(T // tm,),
        in_specs=[pl.BlockSpec((tm, D), lambda i: (i, 0)), pl.BlockSpec((1, D), lambda i: (0, 0)),
                  pl.BlockSpec((nsh, D, Cs), lambda i: (0, 0, 0))],
        out_specs=[pl.BlockSpec((tm, D), lambda i: (i, 0)), pl.BlockSpec((tm, nsh * Cs), lambda i: (i, 0))],
        out_shape=[jax.ShapeDtypeStruct((T, D), BF16), jax.ShapeDtypeStruct((T, nsh * Cs), F32)],
        args=[h, gain, wing])


def mix_out(a, b, woutg, h, name, hosted=()):
    T, W = a.shape
    D = h.shape[1]
    wout = woutg.reshape(2, W, D)
    tm = min(T, 512)

    def body(a_ref, b_ref, w_ref, h_ref, o_ref):
        o_ref[...] = h_ref[...] + _dot(a_ref[...], w_ref[0]) + _dot(b_ref[...], w_ref[1])

    return _call(
        body, hosted, name=name, grid=(T // tm,),
        in_specs=[pl.BlockSpec((tm, W), lambda i: (i, 0)), pl.BlockSpec((tm, W), lambda i: (i, 0)),
                  pl.BlockSpec((2, W, D), lambda i: (0, 0, 0)), pl.BlockSpec((tm, D), lambda i: (i, 0))],
        out_specs=[pl.BlockSpec((tm, D), lambda i: (i, 0))],
        out_shape=[jax.ShapeDtypeStruct((T, D), F32)],
        args=[a, b, wout, h])


def mix_out_bwd(dh, woutg, a, b, name, hosted=()):
    T, D = dh.shape
    W = a.shape[1]
    nsh, Rs, _ = woutg.shape
    wout = woutg.reshape(2, W, D)
    tk = min(T, 512)
    nk = T // tk

    def body(dh_ref, w_ref, a_ref, b_ref, da_ref, db_ref, dw_ref, acc):
        k = pl.program_id(0)

        @pl.when(k == 0)
        def _():
            acc[...] = jnp.zeros_like(acc)

        dhb = dh_ref[...].astype(BF16)
        da_ref[...] = _dot_nt(dhb, w_ref[0])
        db_ref[...] = _dot_nt(dhb, w_ref[1])
        acc[0:W, :] += _dot_tn(a_ref[...], dhb)
        acc[W:2 * W, :] += _dot_tn(b_ref[...], dhb)

        @pl.when(k == nk - 1)
        def _():
            for j in range(nsh):
                dw_ref[j] = acc[j * Rs:(j + 1) * Rs, :].astype(BF16)

    return _call(
        body, hosted, name=name, grid=(nk,),
        in_specs=[pl.BlockSpec((tk, D), lambda k: (k, 0)), pl.BlockSpec((2, W, D), lambda k: (0, 0, 0)),
                  pl.BlockSpec((tk, W), lambda k: (k, 0)), pl.BlockSpec((tk, W), lambda k: (k, 0))],
        out_specs=[pl.BlockSpec((tk, W), lambda k: (k, 0)), pl.BlockSpec((tk, W), lambda k: (k, 0)),
                   pl.BlockSpec((nsh, Rs, D), lambda k: (0, 0, 0))],
        out_shape=[jax.ShapeDtypeStruct((T, W), F32), jax.ShapeDtypeStruct((T, W), F32),
                   jax.ShapeDtypeStruct((nsh, Rs, D), BF16)],
        scratch_shapes=[pltpu.VMEM((2 * W, D), F32)],
        args=[dh, wout, a, b])


def mix_dwin(u, d, nsh, name, hosted=()):
    T, D = u.shape
    Cs = d.shape[1] // nsh
    tk = min(T, 512)
    nk = T // tk

    def body(u_ref, d_ref, o_ref, acc):
        k = pl.program_id(1)

        @pl.when(k == 0)
        def _():
            acc[...] = jnp.zeros_like(acc)

        acc[...] += _dot_tn(u_ref[...], d_ref[...])

        @pl.when(k == nk - 1)
        def _():
            o_ref[0] = acc[...].astype(BF16)

    return _call(
        body, hosted, name=name, grid=(nsh, nk),
        in_specs=[pl.BlockSpec((tk, D), lambda j, k: (k, 0)), pl.BlockSpec((tk, Cs), lambda j, k: (k, j))],
        out_specs=[pl.BlockSpec((1, D, Cs), lambda j, k: (j, 0, 0))],
        out_shape=[jax.ShapeDtypeStruct((nsh, D, Cs), BF16)],
        scratch_shapes=[pltpu.VMEM((D, Cs), F32)],
        args=[u, d])


def mix_in_bwd(d, wing, h, gain, dh, name, hosted=()):
    T, D = h.shape
    nsh, _, Cs = wing.shape
    tm = min(T, 512)

    def body(d_ref, w_ref, h_ref, g_ref, dh_ref, o_ref, dg_ref):
        du = _dot_nt(d_ref[:, 0:Cs], w_ref[0])
        for j in range(1, nsh):
            du += _dot_nt(d_ref[:, j * Cs:(j + 1) * Cs], w_ref[j])
        dhn, dg = _rmsnorm_bwd(du, h_ref[...], g_ref[...])
        o_ref[...] = dh_ref[...] + dhn

        @pl.when(pl.program_id(0) == 0)
        def _():
            dg_ref[...] = jnp.zeros_like(dg_ref)

        dg_ref[...] += jnp.sum(dg, axis=0, keepdims=True)

    row_spec = pl.BlockSpec((tm, D), lambda i: (i, 0))
    vec_spec = pl.BlockSpec((1, D), lambda i: (0, 0))
    return _call(
        body, hosted, name=name, grid=(T // tm,),
        in_specs=[pl.BlockSpec((tm, nsh * Cs), lambda i: (i, 0)), pl.BlockSpec((nsh, D, Cs), lambda i: (0, 0, 0)),
                  row_spec, vec_spec, row_spec],
        out_specs=[row_spec, vec_spec],
        out_shape=[jax.ShapeDtypeStruct((T, D), F32), jax.ShapeDtypeStruct((1, D), F32)],
        args=[d, wing, h, gain, dh])


def _pool_window(x, group, T, trailing):
    rows = lax.broadcasted_iota(jnp.int32, x.shape, 0)

    def shifted(z, k):
        if trailing:
            return jnp.where(rows >= k, pltpu.roll(z, k, 0), 0.0)
        return jnp.where(rows < T - k, pltpu.roll(z, T - k, 0), 0.0)

    s2 = x + shifted(x, 1)
    s4 = s2 + shifted(s2, 2)
    s8 = s4 + shifted(s4, 4)
    s16 = s8 + shifted(s8, 8)
    return jnp.where(group == 0, s2, jnp.where(group == 1, s4, jnp.where(group == 2, s8, s16)))


def _pool_count(group, shape):
    rows = lax.broadcasted_iota(jnp.int32, shape, 0)
    w = jnp.where(group == 0, 2, jnp.where(group == 1, 4, jnp.where(group == 2, 8, 16)))
    return jnp.minimum(rows + 1, w).astype(F32)


def pool_fwd(proj, pool_w, pool_scale, name, hosted=()):
    T = proj.shape[0]
    Hd = HEAD_DIM

    def body(x_ref, w_ref, sc_ref, a_ref):
        g = pl.program_id(0)
        x = x_ref[...]
        pooled = _pool_window(x, g, T, True) / _pool_count(g, x.shape) - x
        a_ref[...] = (_dot(pooled.astype(BF16), w_ref[0].astype(BF16)) * sc_ref[...]).astype(BF16)

    return _call(
        body, hosted, name=name, grid=(N_GROUPS,),
        in_specs=[pl.BlockSpec((T, Hd), lambda g: (0, g)), pl.BlockSpec((1, Hd, Hd), lambda g: (g, 0, 0)),
                  pl.BlockSpec((1, Hd), lambda g: (0, g))],
        out_specs=[pl.BlockSpec((T, Hd), lambda g: (0, g))],
        out_shape=[jax.ShapeDtypeStruct((T, N_GROUPS * Hd), BF16)],
        args=[proj, pool_w, pool_scale])


def pool_bwd(proj, da, pool_w, pool_scale, name, hosted=()):
    T = proj.shape[0]
    Hd = HEAD_DIM

    def body(x_ref, da_ref, w_ref, sc_ref, dx_ref, dw_ref, dsc_ref):
        g = pl.program_id(0)
        x = x_ref[...]
        cnt = _pool_count(g, x.shape)
        pooled = (_pool_window(x, g, T, True) / cnt - x).astype(BF16)
        wb = w_ref[0].astype(BF16)
        dav = da_ref[...]
        dsc_ref[...] = jnp.sum(dav * _dot(pooled, wb), axis=0, keepdims=True)
        dout = (dav * sc_ref[...]).astype(BF16)
        dw_ref[0] = _dot_tn(pooled, dout)
        dpooled = _dot_nt(dout, wb)
        dx_ref[...] = (_pool_window(dpooled / cnt, g, T, False) - dpooled).astype(BF16)

    col_spec = pl.BlockSpec((T, Hd), lambda g: (0, g))
    return _call(
        body, hosted, name=name, grid=(N_GROUPS,),
        in_specs=[col_spec, col_spec, pl.BlockSpec((1, Hd, Hd), lambda g: (g, 0, 0)), pl.BlockSpec((1, Hd), lambda g: (0, g))],
        out_specs=[col_spec, pl.BlockSpec((1, Hd, Hd), lambda g: (g, 0, 0)), pl.BlockSpec((1, Hd), lambda g: (0, g))],
        out_shape=[jax.ShapeDtypeStruct((T, N_GROUPS * Hd), BF16), jax.ShapeDtypeStruct((N_GROUPS, Hd, Hd), F32),
                   jax.ShapeDtypeStruct((1, N_GROUPS * Hd), F32)],
        args=[proj, da, pool_w, pool_scale])


def _ret_tables(T):
    Hd, C = HEAD_DIM, RET_CHUNK
    inv_freq = 1.0 / (ROPE_BASE ** (jnp.arange(0, Hd, 2, dtype=F32) / Hd))
    ang = jnp.arange(T, dtype=F32)[:, None] * inv_freq[None, :]
    cos, sin = jnp.cos(ang), jnp.sin(ang)
    cos2 = jnp.concatenate([cos, cos], axis=-1)
    sin2 = jnp.concatenate([-sin, sin], axis=-1)
    log_gamma = jnp.log1p(-jnp.exp2(-5.0 - jnp.arange(N_GROUPS, dtype=F32)))
    pos = jnp.arange(C, dtype=F32)
    rel = pos[:, None] - pos[None, :]
    intra = jnp.where(rel[None] >= 0, jnp.exp(log_gamma[:, None, None] * jnp.maximum(rel, 0.0)[None]), 0.0)
    k_tail = jnp.exp(log_gamma[:, None] * (C - 1 - pos)[None, :])
    q_head = jnp.exp(log_gamma[:, None] * (pos + 1.0)[None, :])
    chunk_decay = jnp.exp(log_gamma * C)
    wide = lambda t: jnp.broadcast_to(t[:, :, None], (N_GROUPS, C, Hd))
    return cos2, sin2, intra, wide(k_tail), wide(q_head), jnp.broadcast_to(chunk_decay[:, None, None], (N_GROUPS, 1, Hd))


def _rope(x, cos2, sin2):
    return x * cos2 + pltpu.roll(x, HEAD_DIM // 2, 1) * sin2


def _rope_t(d, cos2, sin2):
    return d * cos2 + pltpu.roll(d * sin2, HEAD_DIM // 2, 1)


def _ret_specs(tseg, seg_of):
    Hd, G = HEAD_DIM, N_GROUPS
    col = lambda kind: pl.BlockSpec((tseg, Hd), lambda h, s: (seg_of(s), G * kind + h))
    tab = pl.BlockSpec((tseg, Hd), lambda h, s: (seg_of(s), 0))
    head = pl.BlockSpec((1, RET_CHUNK, Hd), lambda h, s: (h, 0, 0))
    cd = pl.BlockSpec((1, 1, Hd), lambda h, s: (h, 0, 0))
    gain = pl.BlockSpec((1, Hd), lambda h, s: (0, h))
    return col, tab, head, cd, gain


def ret_fwd(proj, ret_norm, tables, name, hosted=()):
    T = proj.shape[0]
    Hd, C, G = HEAD_DIM, RET_CHUNK, N_GROUPS
    tseg = min(T, 1024)
    nseg, nck = T // tseg, tseg // C
    scale = Hd ** -0.5
    cos2, sin2, intra, k_tail, q_head, chunk_decay = tables

    def body(q_ref, k_ref, v_ref, g_ref, gain_ref, cos_ref, sin_ref, m_ref, kt_ref, qh_ref, cd_ref,
             b_ref, o_ref, rp_ref, state):
        @pl.when(pl.program_id(1) == 0)
        def _():
            state[...] = jnp.zeros_like(state)

        def chunk(ci, carry):
            rows = pl.ds(pl.multiple_of(ci * C, C), C)
            cos, sin = cos_ref[rows, :], sin_ref[rows, :]
            qr = _rope(q_ref[rows, :], cos, sin)
            kr = _rope(k_ref[rows, :], cos, sin) * scale
            qb, kb, vb = qr.astype(BF16), kr.astype(BF16), v_ref[rows, :].astype(BF16)
            r = state[...]
            rp_ref[0, ci] = r.astype(BF16)
            sc = _dot_nt(qb, kb) * m_ref[0]
            o = _dot(sc.astype(BF16), vb) + _dot((qr * qh_ref[0]).astype(BF16), r.astype(BF16))
            state[...] = cd_ref[0] * r + _dot_tn((kr * kt_ref[0]).astype(BF16), vb)
            o_ref[rows, :] = o
            on = o * _rstd(o)
            b_ref[rows, :] = (jax.nn.silu(g_ref[rows, :]) * (on * gain_ref[...])).astype(BF16)
            return carry

        lax.fori_loop(0, nck, chunk, 0)

    col, tab, head, cd, gain = _ret_specs(tseg, lambda s: s)
    out_col = pl.BlockSpec((tseg, Hd), lambda h, s: (s, h))
    return _call(
        body, hosted, name=name, grid=(G, nseg),
        in_specs=[col(1), col(2), col(3), col(4), gain, tab, tab, head, head, head, cd],
        out_specs=[out_col, out_col, pl.BlockSpec((1, nck, Hd, Hd), lambda h, s: (h, s, 0, 0))],
        out_shape=[jax.ShapeDtypeStruct((T, G * Hd), BF16), jax.ShapeDtypeStruct((T, G * Hd), F32),
                   jax.ShapeDtypeStruct((G, T // C, Hd, Hd), BF16)],
        scratch_shapes=[pltpu.VMEM((Hd, Hd), F32)],
        args=[proj, proj, proj, proj, ret_norm, cos2, sin2, intra, k_tail, q_head, chunk_decay])


def ret_bwd(proj, db, o_pre, r_prev, ret_norm, tables, name, hosted=()):
    T = proj.shape[0]
    Hd, C, G = HEAD_DIM, RET_CHUNK, N_GROUPS
    tseg = min(T, 1024)
    nseg, nck = T // tseg, tseg // C
    scale = Hd ** -0.5
    cos2, sin2, intra, k_tail, q_head, chunk_decay = tables

    def body(q_ref, k_ref, v_ref, g_ref, db_ref, o_ref, rp_ref, gain_ref, cos_ref, sin_ref, m_ref, kt_ref, qh_ref, cd_ref,
             d_ref, dgain_ref, gstate):
        @pl.when(pl.program_id(1) == 0)
        def _():
            gstate[...] = jnp.zeros_like(gstate)
            dgain_ref[...] = jnp.zeros_like(dgain_ref)

        def chunk(t, carry):
            ci = nck - 1 - t
            rows = pl.ds(pl.multiple_of(ci * C, C), C)
            cos, sin = cos_ref[rows, :], sin_ref[rows, :]
            qr = _rope(q_ref[rows, :], cos, sin)
            kr = _rope(k_ref[rows, :], cos, sin) * scale
            qb, kb, vb = qr.astype(BF16), kr.astype(BF16), v_ref[rows, :].astype(BF16)
            qhb, ktb = (qr * qh_ref[0]).astype(BF16), (kr * kt_ref[0]).astype(BF16)
            sc = (_dot_nt(qb, kb) * m_ref[0]).astype(BF16)
            o = o_ref[rows, :]
            rstd = _rstd(o)
            on = o * rstd
            gain = gain_ref[...]
            silu, dsilu = _silu_parts(g_ref[rows, :])
            dy = db_ref[rows, :]
            dgain_ref[...] += jnp.sum(dy * silu * on, axis=0, keepdims=True)
            dg = dy * on * gain * dsilu
            don = dy * silu * gain
            dob = (rstd * (don - on * jnp.mean(don * on, axis=-1, keepdims=True))).astype(BF16)
            gn = gstate[...]
            gb = gn.astype(BF16)
            da = (_dot_nt(dob, vb) * m_ref[0]).astype(BF16)
            dq = _dot(da, kb) + _dot_nt(dob, rp_ref[0, ci]) * qh_ref[0]
            dk = _dot_tn(da, qb) + _dot_nt(vb, gb) * kt_ref[0]
            dv = _dot_tn(sc, dob) + _dot(ktb, gb)
            gstate[...] = cd_ref[0] * gn + _dot_tn(qhb, dob)
            d_ref[0, rows, :] = _rope_t(dq, cos, sin).astype(BF16)
            d_ref[1, rows, :] = _rope_t(dk * scale, cos, sin).astype(BF16)
            d_ref[2, rows, :] = dv.astype(BF16)
            d_ref[3, rows, :] = dg.astype(BF16)
            return carry

        lax.fori_loop(0, nck, chunk, 0)

    rev = lambda s: nseg - 1 - s
    col, tab, head, cd, gain = _ret_specs(tseg, rev)
    act = pl.BlockSpec((tseg, Hd), lambda h, s: (rev(s), h))
    return _call(
        body, hosted, name=name, grid=(G, nseg),
        in_specs=[col(1), col(2), col(3), col(4), act, act, pl.BlockSpec((1, nck, Hd, Hd), lambda h, s: (h, rev(s), 0, 0)),
                  gain, tab, tab, head, head, head, cd],
        out_specs=[pl.BlockSpec((4, tseg, Hd), lambda h, s: (0, rev(s), h)), gain],
        out_shape=[jax.ShapeDtypeStruct((4, T, G * Hd), BF16), jax.ShapeDtypeStruct((1, G * Hd), F32)],
        scratch_shapes=[pltpu.VMEM((Hd, Hd), F32)],
        args=[proj, proj, proj, proj, db, o_pre, r_prev, ret_norm, cos2, sin2, intra, k_tail, q_head, chunk_decay])


def final_loss(h, gain, target, name, hosted=()):
    T, D = h.shape
    tm = min(T, 512)

    def body(h_ref, g_ref, t_ref, dh_ref, loss_ref, dg_ref):
        @pl.when(pl.program_id(0) == 0)
        def _():
            loss_ref[...] = jnp.zeros_like(loss_ref)
            dg_ref[...] = jnp.zeros_like(dg_ref)

        hh = h_ref[...]
        gain_v = g_ref[...]
        err = hh * _rstd(hh) * gain_v - t_ref[...]
        loss_ref[...] += 0.5 * jnp.sum(jnp.mean(err * err, axis=-1, keepdims=True), axis=0, keepdims=True)
        dhn, dg = _rmsnorm_bwd(err * (1.0 / D), hh, gain_v)
        dh_ref[...] = dhn
        dg_ref[...] += jnp.sum(dg, axis=0, keepdims=True)

    row_spec = pl.BlockSpec((tm, D), lambda i: (i, 0))
    vec_spec = pl.BlockSpec((1, D), lambda i: (0, 0))
    return _call(
        body, hosted, name=name, grid=(T // tm,),
        in_specs=[row_spec, vec_spec, row_spec],
        out_specs=[row_spec, pl.BlockSpec((1, 128), lambda i: (0, 0)), vec_spec],
        out_shape=[jax.ShapeDtypeStruct((T, D), F32), jax.ShapeDtypeStruct((1, 128), F32), jax.ShapeDtypeStruct((1, D), F32)],
        args=[h, gain, target])


def prereduce(grad, recv, core, name):
    nsh, R, C = grad.shape
    rh = R // 2

    def body(c_ref, g_ref, r_ref, o_ref):
        o_ref[...] = (g_ref[...].astype(F32) + r_ref[...].astype(F32)).astype(BF16)

    return pl.pallas_call(
        body, name=name,
        grid_spec=pltpu.PrefetchScalarGridSpec(
            num_scalar_prefetch=1, grid=(nsh,),
            in_specs=[pl.BlockSpec((1, rh, C), lambda j, c_ref: (j, c_ref[0], 0)), pl.BlockSpec((1, rh, C), lambda j, c_ref: (j, 0, 0))],
            out_specs=pl.BlockSpec((1, rh, C), lambda j, c_ref: (j, 0, 0))),
        out_shape=jax.ShapeDtypeStruct((nsh, rh, C), BF16),
        compiler_params=pltpu.CompilerParams(vmem_limit_bytes=VMEM_LIMIT_V7X),
    )(core, grad, recv)


def _adamw(w, g, m, v):
    m = ADAM_B1 * m + (1.0 - ADAM_B1) * g
    v = ADAM_B2 * v + (1.0 - ADAM_B2) * (g * g)
    m_hat = m / (1.0 - ADAM_B1 ** ADAM_STEP)
    v_hat = v / (1.0 - ADAM_B2 ** ADAM_STEP)
    return -ADAM_LR * (m_hat / (jnp.sqrt(v_hat) + ADAM_EPS) + ADAM_WD * w), m, v


def adamw_sharded(tensors, name, hosted=()):
    nt = len(tensors)
    nsh, R, C = tensors[0][0].shape
    tr = 256 if R % 256 == 0 else R // 2

    def body(*refs):
        ins, outs = refs[:4 * nt], refs[4 * nt:]
        for t in range(nt):
            p_ref, w_ref, m_ref, v_ref = ins[4 * t:4 * t + 4]
            g_ref, d_ref, nm_ref, nv_ref = outs[4 * t:4 * t + 4]
            g = p_ref[0].astype(F32)
            for i in range(1, nsh):
                g += p_ref[i].astype(F32)
            g_ref[...] = g
            d_ref[...], nm_ref[...], nv_ref[...] = _adamw(w_ref[...], g, m_ref[...], v_ref[...])

    spec = pl.BlockSpec((tr, C), lambda i: (i, 0))
    out = jax.ShapeDtypeStruct((R, C), F32)
    return _call(
        body, hosted, name=name, grid=(R // tr,),
        in_specs=[pl.BlockSpec((nsh, tr, C), lambda i: (0, i, 0)), spec, spec, spec] * nt,
        out_specs=[spec] * (4 * nt), out_shape=[out] * (4 * nt),
        args=[a for tensor in tensors for a in tensor])


def adamw_small(packs, w, m, v, name):
    ndev, R, L = packs.shape

    def body(p_ref, w_ref, m_ref, v_ref, g_ref, d_ref, nm_ref, nv_ref):
        g = p_ref[0]
        for i in range(1, ndev):
            g += p_ref[i]
        g_ref[...] = g
        d_ref[...], nm_ref[...], nv_ref[...] = _adamw(w_ref[...], g, m_ref[...], v_ref[...])

    out = jax.ShapeDtypeStruct((R, L), F32)
    return pl.pallas_call(body, name=name, out_shape=[out] * 4,
                          compiler_params=pltpu.CompilerParams(vmem_limit_bytes=VMEM_LIMIT_V7X))(packs, w, m, v)


BIG = ("ffn1_w1", "ffn1_w3", "ffn1_w2", "w_in", "w_out", "ffn2_w1", "ffn2_w3", "ffn2_w2")
SMALL = ("ffn1_norm", "mix_norm", "pool_w", "pool_scale", "ret_norm", "ffn2_norm", "final_norm")
WEIGHTS = ("ffn1_norm", "ffn1_w1", "ffn1_w3", "ffn1_w2", "mix_norm", "w_in", "pool_w", "pool_scale", "ret_norm", "w_out",
           "ffn2_norm", "ffn2_w1", "ffn2_w3", "ffn2_w2", "final_norm")


def _pack(parts):
    return jnp.concatenate([parts[k].reshape(-1, 128) for k in SMALL], axis=0)


def _unpack(pack, like):
    out, row = {}, 0
    for k in SMALL:
        rows = like[k].size // 128
        out[k] = pack[row:row + rows].reshape(like[k].shape)
        row += rows
    return out


def kernel(x, ffn1_norm, ffn1_w1, ffn1_w3, ffn1_w2, mix_norm, w_in, pool_w, pool_scale, ret_norm, w_out, ffn2_norm, ffn2_w1, ffn2_w3, ffn2_w2, final_norm, loss_target, m_ffn1_norm, m_ffn1_w1, m_ffn1_w3, m_ffn1_w2, m_mix_norm, m_w_in, m_pool_w, m_pool_scale, m_ret_norm, m_w_out, m_ffn2_norm, m_ffn2_w1, m_ffn2_w3, m_ffn2_w2, m_final_norm, v_ffn1_norm, v_ffn1_w1, v_ffn1_w3, v_ffn1_w2, v_mix_norm, v_w_in, v_pool_w, v_pool_scale, v_ret_norm, v_w_out, v_ffn2_norm, v_ffn2_w1, v_ffn2_w3, v_ffn2_w2, v_final_norm):
    w = dict(ffn1_norm=ffn1_norm, ffn1_w1=ffn1_w1, ffn1_w3=ffn1_w3, ffn1_w2=ffn1_w2, mix_norm=mix_norm, w_in=w_in, pool_w=pool_w,
             pool_scale=pool_scale, ret_norm=ret_norm, w_out=w_out, ffn2_norm=ffn2_norm, ffn2_w1=ffn2_w1, ffn2_w3=ffn2_w3,
             ffn2_w2=ffn2_w2, final_norm=final_norm)
    m = dict(ffn1_norm=m_ffn1_norm, ffn1_w1=m_ffn1_w1, ffn1_w3=m_ffn1_w3, ffn1_w2=m_ffn1_w2, mix_norm=m_mix_norm, w_in=m_w_in,
             pool_w=m_pool_w, pool_scale=m_pool_scale, ret_norm=m_ret_norm, w_out=m_w_out, ffn2_norm=m_ffn2_norm, ffn2_w1=m_ffn2_w1,
             ffn2_w3=m_ffn2_w3, ffn2_w2=m_ffn2_w2, final_norm=m_final_norm)
    v = dict(ffn1_norm=v_ffn1_norm, ffn1_w1=v_ffn1_w1, ffn1_w3=v_ffn1_w3, ffn1_w2=v_ffn1_w2, mix_norm=v_mix_norm, w_in=v_w_in,
             pool_w=v_pool_w, pool_scale=v_pool_scale, ret_norm=v_ret_norm, w_out=v_w_out, ffn2_norm=v_ffn2_norm, ffn2_w1=v_ffn2_w1,
             ffn2_w3=v_ffn2_w3, ffn2_w2=v_ffn2_w2, final_norm=v_final_norm)
    xs, target = x[0], loss_target[0]
    T = xs.shape[0]
    tables = _ret_tables(T)
    core = lax.axis_index("c").astype(jnp.int32).reshape(1)
    sh = {k: w[k][0].astype(BF16) for k in BIG}
    gather = lambda *names: [ChipExchange([sh[k] for k in names], False)]
    wg, grad, delta, new_m, new_v = {}, {}, {}, {}, {}

    def update(names, pieces, name, hosted=()):
        outs, extras = adamw_sharded([(p, w[k][0], m[k][0], v[k][0]) for k, p in zip(names, pieces)], name, hosted)
        for t, k in enumerate(names):
            grad[k], delta[k], new_m[k], new_v[k] = [o[None] for o in outs[4 * t:4 * t + 4]]
        return extras

    def reduce_in_chip(name, partial, recv):
        return prereduce(partial, recv, core, "prereduce_" + name)

    (wg["ffn1_w1"], wg["ffn1_w3"]), = exchange(gather("ffn1_w1", "ffn1_w3"), "gather_ffn1")
    (n1, a1, b1, s1), ((wg["ffn1_w2"], wg["w_in"]),) = ffn_up(
        xs, ffn1_norm, wg["ffn1_w1"], wg["ffn1_w3"], "ffn1_up", gather("ffn1_w2", "w_in"))
    (h1,), ((wg["w_out"],),) = ffn_down(s1, wg["ffn1_w2"], xs, "ffn1_down", gather("w_out"))
    (u, proj), ((wg["ffn2_w1"],),) = mix_in(h1, mix_norm, wg["w_in"], "mix_in", gather("ffn2_w1"))
    (pa,), _ = pool_fwd(proj, pool_w[0], pool_scale, "pool_fwd")
    (rb, o_pre, r_prev), ((wg["ffn2_w3"],),) = ret_fwd(proj, ret_norm, tables, "ret_fwd", gather("ffn2_w3"))
    (h2,), _ = mix_out(pa, rb, wg["w_out"], h1, "mix_out")
    (n2, a2, b2, s2), ((wg["ffn2_w2"],),) = ffn_up(
        h2, ffn2_norm, wg["ffn2_w1"], wg["ffn2_w3"], "ffn2_up", gather("ffn2_w2"))
    (h3,), _ = ffn_down(s2, wg["ffn2_w2"], h2, "ffn2_down")
    (dh3, loss, d_final), _ = final_loss(h3, final_norm[None], target, "final_loss")
    loss = lax.psum(loss[0, 0], ("x", "y", "c"))

    (da2, db2), _ = ffn_bwd_act(dh3, wg["ffn2_w2"], a2, b2, "ffn2_bwd_act")
    (g_f2w2,), _ = ffn_dw2(s2, dh3, "ffn2_dw2")
    (g_f2w1, g_f2w3), ((r_f2w2,),) = ffn_dw13(n2, da2, db2, "ffn2_dw13", [SiblingExchange([g_f2w2])])
    p_f2w2 = reduce_in_chip("ffn2_w2", g_f2w2, r_f2w2)
    (dh2, d_ffn2), ((q_f2w2,), (r_f2w1, r_f2w3)) = ffn_bwd_in(
        da2, db2, wg["ffn2_w1"], wg["ffn2_w3"], h2, ffn2_norm, dh3, "ffn2_bwd_in",
        [ChipExchange([p_f2w2], True), SiblingExchange([g_f2w1, g_f2w3])])
    p_f2w1 = reduce_in_chip("ffn2_w1", g_f2w1, r_f2w1)
    p_f2w3 = reduce_in_chip("ffn2_w3", g_f2w3, r_f2w3)
    (dpa, drb, g_wout), ((q_f2w1,),) = mix_out_bwd(dh2, wg["w_out"], pa, rb, "mix_out_bwd", [ChipExchange([p_f2w1], True)])
    (dpool, d_pool_w, d_pool_scale), _ = pool_bwd(proj, dpa, pool_w[0], pool_scale, "pool_bwd")
    (dqkvg, d_ret_norm), ((q_f2w3,), (r_wout,)) = ret_bwd(
        proj, drb, o_pre, r_prev, ret_norm, tables, "ret_bwd", [ChipExchange([p_f2w3], True), SiblingExchange([g_wout])])
    p_wout = reduce_in_chip("w_out", g_wout, r_wout)
    d = jnp.concatenate([dpool, dqkvg[0], dqkvg[1], dqkvg[2], dqkvg[3]], axis=1)
    (g_win,), ((q_wout,),) = mix_dwin(u, d, N_CHIPS, "mix_dwin", [ChipExchange([p_wout], True)])
    (dh1, d_mix), ((r_win,),) = mix_in_bwd(d, wg["w_in"], h1, mix_norm, dh2, "mix_in_bwd", [SiblingExchange([g_win])])
    p_win = reduce_in_chip("w_in", g_win, r_win)
    (da1, db1), ((q_win,),) = ffn_bwd_act(dh1, wg["ffn1_w2"], a1, b1, "ffn1_bwd_act", [ChipExchange([p_win], True)])
    (g_f1w1, g_f1w3), _ = ffn_dw13(n1, da1, db1, "ffn1_dw13")
    (g_f1w2,), ((r_f1w1, r_f1w3),) = ffn_dw2(s1, dh1, "ffn1_dw2", [SiblingExchange([g_f1w1, g_f1w3])])
    p_f1w1 = reduce_in_chip("ffn1_w1", g_f1w1, r_f1w1)
    p_f1w3 = reduce_in_chip("ffn1_w3", g_f1w3, r_f1w3)
    (dx, d_ffn1), ((q_f1w1, q_f1w3), (r_f1w2,)) = ffn_bwd_in(
        da1, db1, wg["ffn1_w1"], wg["ffn1_w3"], xs, ffn1_norm, dh1, "ffn1_bwd_in",
        [ChipExchange([p_f1w1, p_f1w3], True), SiblingExchange([g_f1w2])])
    p_f1w2 = reduce_in_chip("ffn1_w2", g_f1w2, r_f1w2)

    (q_f1w2,), = update(["ffn2_w1", "ffn2_w3"], [q_f2w1, q_f2w3], "adamw_ffn2_w13", [ChipExchange([p_f1w2], True)])
    update(["ffn2_w2"], [q_f2w2], "adamw_ffn2_w2")
    update(["w_in"], [q_win], "adamw_w_in")
    update(["w_out"], [q_wout], "adamw_w_out")
    update(["ffn1_w1", "ffn1_w3"], [q_f1w1, q_f1w3], "adamw_ffn1_w13")
    update(["ffn1_w2"], [q_f1w2], "adamw_ffn1_w2")

    small = {"ffn1_norm": d_ffn1, "mix_norm": d_mix, "pool_w": d_pool_w, "pool_scale": d_pool_scale,
             "ret_norm": d_ret_norm, "ffn2_norm": d_ffn2, "final_norm": d_final}
    packs = all_exchange_small(_pack(small), "gather_small")
    outs = adamw_small(packs, _pack(w), _pack(m), _pack(v), "adamw_small")
    for res, pack in zip((grad, delta, new_m, new_v), outs):
        res.update(_unpack(pack, w))

    return (loss, dx[None], *[grad[k] for k in WEIGHTS], *[delta[k] for k in WEIGHTS],
            *[new_m[k] for k in WEIGHTS], *[new_v[k] for k in WEIGHTS])
```

```python
import math

import jax
import jax.numpy as jnp
from jax import lax
from jax.experimental import pallas as pl
from jax.experimental.pallas import tpu as pltpu

F32 = jnp.float32
BF16 = jnp.bfloat16

EPS = 1e-6
N_CHIPS = 4
N_GROUPS = 4
HEAD_DIM = 128
RET_CHUNK = 128
ROPE_BASE = 10000.0
ADAM_LR, ADAM_B1, ADAM_B2, ADAM_EPS, ADAM_WD, ADAM_STEP = 0.001, 0.9, 0.999, 1e-08, 0.01, 10
VMEM_LIMIT_V7X = 56 * 1024 * 1024
MESH = pl.DeviceIdType.MESH
ANY = pl.BlockSpec(memory_space=pl.ANY)


def _dot(a, b):
    return jnp.dot(a, b, preferred_element_type=F32)


def _dot_nt(a, b):
    return lax.dot_general(a, b, (((1,), (1,)), ((), ())), preferred_element_type=F32)


def _dot_tn(a, b):
    return lax.dot_general(a, b, (((0,), (0,)), ((), ())), preferred_element_type=F32)


def _rstd(h):
    return lax.rsqrt(jnp.mean(h * h, axis=-1, keepdims=True) + EPS)


def _rmsnorm_bwd(dn, h, gain):
    r = _rstd(h)
    nh = h * r
    dnh = dn * gain
    dh = r * (dnh - nh * jnp.mean(dnh * nh, axis=-1, keepdims=True))
    return dh, dn * nh


def _silu_parts(a):
    sig = jax.nn.sigmoid(a)
    silu = a * sig
    return silu, sig + silu * (1.0 - sig)


def _mesh_pos():
    return lax.axis_index("x"), lax.axis_index("y"), lax.axis_index("c")


class ChipExchange:
    def __init__(self, srcs, scatter):
        n = len(srcs)
        self.srcs, self.scatter, self.n = list(srcs), scatter, n
        self.half_rows = [s.shape[1] if scatter else s.shape[0] // 2 for s in srcs]
        self.out_shape = [jax.ShapeDtypeStruct((N_CHIPS, 2 * rh, s.shape[-1]), s.dtype) for s, rh in zip(srcs, self.half_rows)]
        if scatter:
            self.out_shape += [jax.ShapeDtypeStruct((2, rh // 2, s.shape[-1]), s.dtype) for s, rh in zip(srcs, self.half_rows)]
        dma = pltpu.SemaphoreType.DMA
        self.sems = [dma((n,)), dma((4 * n,)), dma((4 * n,)), dma((2 * n,)), dma((2 * n,)), dma((4 * n,)), dma((4 * n,))]

    def _copies(self, src, out, sems):
        local_sem, hop1_send, hop1_recv, hop2_send, hop2_recv, d2d_send, d2d_recv = sems
        x, y, c = _mesh_pos()
        me, xn, yn, dg = 2 * x + y, 2 * (1 - x) + y, 2 * x + (1 - y), 2 * (1 - x) + (1 - y)
        to_xn, to_yn, sibling = (1 - x, y, c), (x, 1 - y, c), (x, y, 1 - c)
        n = self.n

        def remote(s, d, send, recv, k, to):
            return pltpu.make_async_remote_copy(src_ref=s, dst_ref=d, send_sem=send.at[k], recv_sem=recv.at[k],
                                                device_id=to, device_id_type=MESH)

        class Copies:
            def slot(_, t, chip, core):
                rh = self.half_rows[t]
                return out[t].at[chip, pl.ds(core * rh, rh), :]

            def quarter(_, t, chip, q):
                qh = self.half_rows[t] // 2
                return out[t].at[chip, pl.ds(c * 2 * qh + q * qh, qh), :]

            def local(k, t):
                if self.scatter:
                    return pltpu.make_async_copy(src[t].at[me], k.slot(t, me, c), local_sem.at[t])
                return pltpu.make_async_copy(src[t], out[t].at[me], local_sem.at[t])

            def hop1(k, t, p):
                rh = self.half_rows[t]
                to, nb = (to_xn, xn) if p % 2 == 0 else (to_yn, yn)
                if p < 2:
                    piece = src[t].at[nb] if self.scatter else src[t].at[pl.ds(c * rh, rh), :]
                    return remote(piece, k.slot(t, me, c), hop1_send, hop1_recv, 4 * t + p, to)
                piece = src[t].at[dg, pl.ds((p - 2) * (rh // 2), rh // 2), :]
                return remote(piece, out[n + t].at[p - 2], hop1_send, hop1_recv, 4 * t + p, to)

            def landed1(k, t, p):
                here = k.slot(t, xn if p % 2 == 0 else yn, c) if p < 2 else out[n + t].at[p - 2]
                return remote(here, here, hop1_send, hop1_recv, 4 * t + p, to_xn)

            def hop2(k, t, q):
                origin, to = (xn, to_yn) if q == 0 else (yn, to_xn)
                piece = out[n + t].at[q] if self.scatter else k.quarter(t, origin, q)
                return remote(piece, k.quarter(t, origin, q), hop2_send, hop2_recv, 2 * t + q, to)

            def landed2(k, t, q):
                here = k.quarter(t, dg, q)
                return remote(here, here, hop2_send, hop2_recv, 2 * t + q, to_xn)

            def d2d(k, t, p, arriving=False):
                chip = (xn, yn, dg, me)[p]
                if arriving:
                    there = k.slot(t, chip, 1 - c)
                    return remote(there, there, d2d_send, d2d_recv, 4 * t + p, sibling)
                piece = src[t].at[me] if p == 3 else k.slot(t, chip, c)
                return remote(piece, k.slot(t, chip, c), d2d_send, d2d_recv, 4 * t + p, sibling)

        return Copies()

    def start(self, src, out, sems):
        k = self._copies(src, out, sems)
        for t in range(self.n):
            k.local(t).start()
            for p in range(4 if self.scatter else 2):
                k.hop1(t, p).start()
            if self.scatter:
                k.d2d(t, 3).start()

    def mid(self, src, out, sems):
        k = self._copies(src, out, sems)
        for t in range(self.n):
            if self.scatter:
                for q in range(2):
                    k.landed1(t, 2 + q).wait_recv()
                    k.hop2(t, q).start()
            for p in range(2):
                k.landed1(t, p).wait_recv()
                if not self.scatter:
                    k.hop2(t, p).start()
                k.d2d(t, p).start()

    def finish(self, src, out, sems):
        k = self._copies(src, out, sems)
        for t in range(self.n):
            for q in range(2):
                k.landed2(t, q).wait_recv()
            k.d2d(t, 2).start()
        for t in range(self.n):
            for p in range(4 if self.scatter else 3):
                k.d2d(t, p, arriving=True).wait_recv()
        for t in range(self.n):
            k.local(t).wait()
            for p in range(4 if self.scatter else 2):
                k.hop1(t, p).wait_send()
            for q in range(2):
                k.hop2(t, q).wait_send()
            for p in range(4 if self.scatter else 3):
                k.d2d(t, p).wait_send()


class SiblingExchange:
    def __init__(self, grads):
        self.srcs, self.n = list(grads), len(grads)
        self.half_rows = [g.shape[1] // 2 for g in grads]
        self.out_shape = [jax.ShapeDtypeStruct((g.shape[0], rh, g.shape[2]), g.dtype) for g, rh in zip(grads, self.half_rows)]
        self.sems = [pltpu.SemaphoreType.DMA((self.n,)), pltpu.SemaphoreType.DMA((self.n,))]

    def _plan(self, src, out, sems):
        x, y, c = _mesh_pos()
        return [pltpu.make_async_remote_copy(
            src_ref=src[t].at[:, pl.ds((1 - c) * self.half_rows[t], self.half_rows[t]), :], dst_ref=out[t],
            send_sem=sems[0].at[t], recv_sem=sems[1].at[t], device_id=(x, y, 1 - c), device_id_type=MESH) for t in range(self.n)]

    def start(self, src, out, sems):
        for cp in self._plan(src, out, sems):
            cp.start()

    def mid(self, src, out, sems):
        pass

    def finish(self, src, out, sems):
        for cp in self._plan(src, out, sems):
            cp.wait()


def _call(body, hosted=(), *, name, in_specs, out_specs, out_shape, args, grid=(), scratch_shapes=()):
    n_in, n_out, n_scr = len(in_specs), len(out_specs), len(scratch_shapes)
    total = math.prod(grid)
    mid_step = max(0, (5 * total) // 8 - 1)

    def full(*refs):
        pos = [0]

        def take(k):
            pos[0] += k
            return refs[pos[0] - k:pos[0]]

        ins, h_in = take(n_in), [take(h.n) for h in hosted]
        outs, h_out = take(n_out), [take(len(h.out_shape)) for h in hosted]
        scr, h_sem = take(n_scr), [take(len(h.sems)) for h in hosted]
        step = 0
        for axis, size in enumerate(grid):
            step = step * size + pl.program_id(axis)

        def phase(at, method):
            if not hosted:
                return
            if total == 1:
                for h, s, o, m in zip(hosted, h_in, h_out, h_sem):
                    getattr(h, method)(s, o, m)
                return

            @pl.when(step == at)
            def _():
                for h, s, o, m in zip(hosted, h_in, h_out, h_sem):
                    getattr(h, method)(s, o, m)

        phase(0, "start")
        body(*ins, *outs, *scr)
        phase(mid_step, "mid")
        phase(total - 1, "finish")

    results = pl.pallas_call(
        full, name=name, grid=grid,
        in_specs=list(in_specs) + [ANY] * sum(h.n for h in hosted),
        out_specs=list(out_specs) + [ANY] * sum(len(h.out_shape) for h in hosted),
        out_shape=list(out_shape) + [s for h in hosted for s in h.out_shape],
        scratch_shapes=list(scratch_shapes) + [s for h in hosted for s in h.sems],
        compiler_params=pltpu.CompilerParams(vmem_limit_bytes=VMEM_LIMIT_V7X),
    )(*args, *[s for h in hosted for s in h.srcs])
    outs, extras, pos = list(results[:n_out]), [], n_out
    for h in hosted:
        extras.append(list(results[pos:pos + h.n]))
        pos += len(h.out_shape)
    return outs, extras


def exchange(hosted, name):
    return _call(lambda: None, hosted, name=name, in_specs=[], out_specs=[], out_shape=[], args=[])[1]


def all_exchange_small(pack, name):
    R, L = pack.shape
    flips = [(dx, dy, dc) for dx in (0, 1) for dy in (0, 1) for dc in (0, 1)][1:]

    def body(src, out, local_sem, send_sem, recv_sem):
        x, y, c = _mesh_pos()
        me = 4 * x + 2 * y + c
        local = pltpu.make_async_copy(src, out.at[me], local_sem)
        local.start()
        copies = []
        for k, (dx, dy, dc) in enumerate(flips):
            copies.append(pltpu.make_async_remote_copy(
                src_ref=src, dst_ref=out.at[me], send_sem=send_sem.at[k], recv_sem=recv_sem.at[k],
                device_id=(x ^ dx, y ^ dy, c ^ dc), device_id_type=MESH))
        for cp in copies:
            cp.start()
        for k, (dx, dy, dc) in enumerate(flips):
            landed = out.at[4 * (x ^ dx) + 2 * (y ^ dy) + (c ^ dc)]
            pltpu.make_async_remote_copy(src_ref=landed, dst_ref=landed, send_sem=send_sem.at[k], recv_sem=recv_sem.at[k],
                                         device_id=(x ^ dx, y ^ dy, c ^ dc), device_id_type=MESH).wait_recv()
        for cp in copies:
            cp.wait_send()
        local.wait()

    return pl.pallas_call(
        body, name=name, in_specs=[ANY], out_specs=ANY,
        out_shape=jax.ShapeDtypeStruct((2 * N_CHIPS, R, L), pack.dtype),
        scratch_shapes=[pltpu.SemaphoreType.DMA, pltpu.SemaphoreType.DMA((7,)), pltpu.SemaphoreType.DMA((7,))],
    )(pack)


def ffn_up(h, gain, w1g, w3g, name, hosted=()):
    T, D = h.shape
    nsh, _, Fs = w1g.shape
    tm = min(T, 1024)

    def body(h_ref, g_ref, w1_ref, w3_ref, n_ref, a_ref, b_ref, s_ref):
        @pl.when(pl.program_id(1) == 0)
        def _():
            hh = h_ref[...]
            n_ref[...] = (hh * _rstd(hh) * g_ref[...]).astype(BF16)

        n = n_ref[...]
        a = _dot(n, w1_ref[0])
        b = _dot(n, w3_ref[0])
        a_ref[0] = a.astype(BF16)
        b_ref[0] = b.astype(BF16)
        s_ref[0] = (a * jax.nn.sigmoid(a) * b).astype(BF16)

    act = jax.ShapeDtypeStruct((nsh, T, Fs), BF16)
    act_spec = pl.BlockSpec((1, tm, Fs), lambda i, j: (j, i, 0))
    w_spec = pl.BlockSpec((1, D, Fs), lambda i, j: (j, 0, 0))
    return _call(
        body, hosted, name=name, grid=(T // tm, nsh),
        in_specs=[pl.BlockSpec((tm, D), lambda i, j: (i, 0)), pl.BlockSpec((1, D), lambda i, j: (0, 0)), w_spec, w_spec],
        out_specs=[pl.BlockSpec((tm, D), lambda i, j: (i, 0)), act_spec, act_spec, act_spec],
        out_shape=[jax.ShapeDtypeStruct((T, D), BF16), act, act, act],
        args=[h, gain, w1g, w3g])


def ffn_down(s, w2g, h, name, hosted=()):
    nsh, T, Fs = s.shape
    D = h.shape[1]
    tm = min(T, 512)

    def body(s_ref, w2_ref, h_ref, o_ref):
        f = _dot(s_ref[0], w2_ref[0])
        for j in range(1, nsh):
            f += _dot(s_ref[j], w2_ref[j])
        o_ref[...] = h_ref[...] + 0.5 * f

    return _call(
        body, hosted, name=name, grid=(T // tm,),
        in_specs=[pl.BlockSpec((nsh, tm, Fs), lambda i: (0, i, 0)), pl.BlockSpec((nsh, Fs, D), lambda i: (0, 0, 0)),
                  pl.BlockSpec((tm, D), lambda i: (i, 0))],
        out_specs=[pl.BlockSpec((tm, D), lambda i: (i, 0))],
        out_shape=[jax.ShapeDtypeStruct((T, D), F32)],
        args=[s, w2g, h])


def ffn_bwd_act(dh, w2g, a, b, name, hosted=()):
    T, D = dh.shape
    nsh, Fs, _ = w2g.shape
    tm = min(T, 1024)

    def body(dh_ref, w2_ref, a_ref, b_ref, da_ref, db_ref):
        df = (0.5 * dh_ref[...]).astype(BF16)
        ds = _dot_nt(df, w2_ref[0])
        silu, dsilu = _silu_parts(a_ref[0].astype(F32))
        da_ref[0] = (ds * b_ref[0].astype(F32) * dsilu).astype(BF16)
        db_ref[0] = (ds * silu).astype(BF16)

    act = jax.ShapeDtypeStruct((nsh, T, Fs), BF16)
    act_spec = pl.BlockSpec((1, tm, Fs), lambda j, i: (j, i, 0))
    return _call(
        body, hosted, name=name, grid=(nsh, T // tm),
        in_specs=[pl.BlockSpec((tm, D), lambda j, i: (i, 0)), pl.BlockSpec((1, Fs, D), lambda j, i: (j, 0, 0)), act_spec, act_spec],
        out_specs=[act_spec, act_spec],
        out_shape=[act, act],
        args=[dh, w2g, a, b])


def ffn_dw2(s, dh, name, hosted=()):
    nsh, T, Fs = s.shape
    D = dh.shape[1]
    tk = min(T, 512)
    nk = T // tk

    def body(s_ref, dh_ref, o_ref, acc):
        k = pl.program_id(1)

        @pl.when(k == 0)
        def _():
            acc[...] = jnp.zeros_like(acc)

        acc[...] += _dot_tn(s_ref[0], (0.5 * dh_ref[...]).astype(BF16))

        @pl.when(k == nk - 1)
        def _():
            o_ref[0] = acc[...].astype(BF16)

    return _call(
        body, hosted, name=name, grid=(nsh, nk),
        in_specs=[pl.BlockSpec((1, tk, Fs), lambda j, k: (j, k, 0)), pl.BlockSpec((tk, D), lambda j, k: (k, 0))],
        out_specs=[pl.BlockSpec((1, Fs, D), lambda j, k: (j, 0, 0))],
        out_shape=[jax.ShapeDtypeStruct((nsh, Fs, D), BF16)],
        scratch_shapes=[pltpu.VMEM((Fs, D), F32)],
        args=[s, dh])


def ffn_dw13(n, da, db, name, hosted=()):
    T, D = n.shape
    nsh, _, Fs = da.shape
    tk = min(T, 512)
    nk = T // tk

    def body(n_ref, da_ref, db_ref, o1_ref, o3_ref, acc1, acc3):
        k = pl.program_id(1)

        @pl.when(k == 0)
        def _():
            acc1[...] = jnp.zeros_like(acc1)
            acc3[...] = jnp.zeros_like(acc3)

        nn = n_ref[...]
        acc1[...] += _dot_tn(nn, da_ref[0])
        acc3[...] += _dot_tn(nn, db_ref[0])

        @pl.when(k == nk - 1)
        def _():
            o1_ref[0] = acc1[...].astype(BF16)
            o3_ref[0] = acc3[...].astype(BF16)

    act_spec = pl.BlockSpec((1, tk, Fs), lambda j, k: (j, k, 0))
    out = jax.ShapeDtypeStruct((nsh, D, Fs), BF16)
    out_spec = pl.BlockSpec((1, D, Fs), lambda j, k: (j, 0, 0))
    return _call(
        body, hosted, name=name, grid=(nsh, nk),
        in_specs=[pl.BlockSpec((tk, D), lambda j, k: (k, 0)), act_spec, act_spec],
        out_specs=[out_spec, out_spec],
        out_shape=[out, out],
        scratch_shapes=[pltpu.VMEM((D, Fs), F32), pltpu.VMEM((D, Fs), F32)],
        args=[n, da, db])


def ffn_bwd_in(da, db, w1g, w3g, h, gain, dh, name, hosted=()):
    nsh, T, Fs = da.shape
    D = h.shape[1]
    tm = min(T, 256)

    def body(da_ref, db_ref, w1_ref, w3_ref, h_ref, g_ref, dh_ref, o_ref, dg_ref):
        dn = _dot_nt(da_ref[0], w1_ref[0]) + _dot_nt(db_ref[0], w3_ref[0])
        for j in range(1, nsh):
            dn += _dot_nt(da_ref[j], w1_ref[j]) + _dot_nt(db_ref[j], w3_ref[j])
        dhn, dg = _rmsnorm_bwd(dn, h_ref[...], g_ref[...])
        o_ref[...] = dh_ref[...] + dhn

        @pl.when(pl.program_id(0) == 0)
        def _():
            dg_ref[...] = jnp.zeros_like(dg_ref)

        dg_ref[...] += jnp.sum(dg, axis=0, keepdims=True)

    act_spec = pl.BlockSpec((nsh, tm, Fs), lambda i: (0, i, 0))
    w_spec = pl.BlockSpec((nsh, D, Fs), lambda i: (0, 0, 0))
    row_spec = pl.BlockSpec((tm, D), lambda i: (i, 0))
    vec_spec = pl.BlockSpec((1, D), lambda i: (0, 0))
    return _call(
        body, hosted, name=name, grid=(T // tm,),
        in_specs=[act_spec, act_spec, w_spec, w_spec, row_spec, vec_spec, row_spec],
        out_specs=[row_spec, vec_spec],
        out_shape=[jax.ShapeDtypeStruct((T, D), F32), jax.ShapeDtypeStruct((1, D), F32)],
        args=[da, db, w1g, w3g, h, gain, dh])


def mix_in(h, gain, wing, name, hosted=()):
    T, D = h.shape
    nsh, _, Cs = wing.shape
    tm = min(T, 512)

    def body(h_ref, g_ref, w_ref, u_ref, p_ref):
        hh = h_ref[...]
        u = (hh * _rstd(hh) * g_ref[...]).astype(BF16)
        u_ref[...] = u
        for j in range(nsh):
            p_ref[:, j * Cs:(j + 1) * Cs] = _dot(u, w_ref[j])

    return _call(
        body, hosted, name=name, grid=(T // tm,),
        in_specs=[pl.BlockSpec((tm, D), lambda i: (i, 0)), pl.BlockSpec((1, D), lambda i: (0, 0)),
                  pl.BlockSpec((nsh, D, Cs), lambda i: (0, 0, 0))],
        out_specs=[pl.BlockSpec((tm, D), lambda i: (i, 0)), pl.BlockSpec((tm, nsh * Cs), lambda i: (i, 0))],
        out_shape=[jax.ShapeDtypeStruct((T, D), BF16), jax.ShapeDtypeStruct((T, nsh * Cs), F32)],
        args=[h, gain, wing])


def mix_out(a, b, woutg, h, name, hosted=()):
    T, W = a.shape
    D = h.shape[1]
    wout = woutg.reshape(2, W, D)
    tm = min(T, 512)

    def body(a_ref, b_ref, w_ref, h_ref, o_ref):
        o_ref[...] = h_ref[...] + _dot(a_ref[...], w_ref[0]) + _dot(b_ref[...], w_ref[1])

    return _call(
        body, hosted, name=name, grid=(T // tm,),
        in_specs=[pl.BlockSpec((tm, W), lambda i: (i, 0)), pl.BlockSpec((tm, W), lambda i: (i, 0)),
                  pl.BlockSpec((2, W, D), lambda i: (0, 0, 0)), pl.BlockSpec((tm, D), lambda i: (i, 0))],
        out_specs=[pl.BlockSpec((tm, D), lambda i: (i, 0))],
        out_shape=[jax.ShapeDtypeStruct((T, D), F32)],
        args=[a, b, wout, h])


def mix_out_bwd(dh, woutg, a, b, name, hosted=()):
    T, D = dh.shape
    W = a.shape[1]
    nsh, Rs, _ = woutg.shape
    wout = woutg.reshape(2, W, D)
    tk = min(T, 512)
    nk = T // tk

    def body(dh_ref, w_ref, a_ref, b_ref, da_ref, db_ref, dw_ref, acc):
        k = pl.program_id(0)

        @pl.when(k == 0)
        def _():
            acc[...] = jnp.zeros_like(acc)

        dhb = dh_ref[...].astype(BF16)
        da_ref[...] = _dot_nt(dhb, w_ref[0])
        db_ref[...] = _dot_nt(dhb, w_ref[1])
        acc[0:W, :] += _dot_tn(a_ref[...], dhb)
        acc[W:2 * W, :] += _dot_tn(b_ref[...], dhb)

        @pl.when(k == nk - 1)
        def _():
            for j in range(nsh):
                dw_ref[j] = acc[j * Rs:(j + 1) * Rs, :].astype(BF16)

    return _call(
        body, hosted, name=name, grid=(nk,),
        in_specs=[pl.BlockSpec((tk, D), lambda k: (k, 0)), pl.BlockSpec((2, W, D), lambda k: (0, 0, 0)),
                  pl.BlockSpec((tk, W), lambda k: (k, 0)), pl.BlockSpec((tk, W), lambda k: (k, 0))],
        out_specs=[pl.BlockSpec((tk, W), lambda k: (k, 0)), pl.BlockSpec((tk, W), lambda k: (k, 0)),
                   pl.BlockSpec((nsh, Rs, D), lambda k: (0, 0, 0))],
        out_shape=[jax.ShapeDtypeStruct((T, W), F32), jax.ShapeDtypeStruct((T, W), F32),
                   jax.ShapeDtypeStruct((nsh, Rs, D), BF16)],
        scratch_shapes=[pltpu.VMEM((2 * W, D), F32)],
        args=[dh, wout, a, b])


def mix_dwin(u, d, nsh, name, hosted=()):
    T, D = u.shape
    Cs = d.shape[1] // nsh
    tk = min(T, 512)
    nk = T // tk

    def body(u_ref, d_ref, o_ref, acc):
        k = pl.program_id(1)

        @pl.when(k == 0)
        def _():
            acc[...] = jnp.zeros_like(acc)

        acc[...] += _dot_tn(u_ref[...], d_ref[...])

        @pl.when(k == nk - 1)
        def _():
            o_ref[0] = acc[...].astype(BF16)

    return _call(
        body, hosted, name=name, grid=(nsh, nk),
        in_specs=[pl.BlockSpec((tk, D), lambda j, k: (k, 0)), pl.BlockSpec((tk, Cs), lambda j, k: (k, j))],
        out_specs=[pl.BlockSpec((1, D, Cs), lambda j, k: (j, 0, 0))],
        out_shape=[jax.ShapeDtypeStruct((nsh, D, Cs), BF16)],
        scratch_shapes=[pltpu.VMEM((D, Cs), F32)],
        args=[u, d])


def mix_in_bwd(d, wing, h, gain, dh, name, hosted=()):
    T, D = h.shape
    nsh, _, Cs = wing.shape
    tm = min(T, 512)

    def body(d_ref, w_ref, h_ref, g_ref, dh_ref, o_ref, dg_ref):
        du = _dot_nt(d_ref[:, 0:Cs], w_ref[0])
        for j in range(1, nsh):
            du += _dot_nt(d_ref[:, j * Cs:(j + 1) * Cs], w_ref[j])
        dhn, dg = _rmsnorm_bwd(du, h_ref[...], g_ref[...])
        o_ref[...] = dh_ref[...] + dhn

        @pl.when(pl.program_id(0) == 0)
        def _():
            dg_ref[...] = jnp.zeros_like(dg_ref)

        dg_ref[...] += jnp.sum(dg, axis=0, keepdims=True)

    row_spec = pl.BlockSpec((tm, D), lambda i: (i, 0))
    vec_spec = pl.BlockSpec((1, D), lambda i: (0, 0))
    return _call(
        body, hosted, name=name, grid=(T // tm,),
        in_specs=[pl.BlockSpec((tm, nsh * Cs), lambda i: (i, 0)), pl.BlockSpec((nsh, D, Cs), lambda i: (0, 0, 0)),
                  row_spec, vec_spec, row_spec],
        out_specs=[row_spec, vec_spec],
        out_shape=[jax.ShapeDtypeStruct((T, D), F32), jax.ShapeDtypeStruct((1, D), F32)],
        args=[d, wing, h, gain, dh])


def _pool_window(x, group, T, trailing):
    rows = lax.broadcasted_iota(jnp.int32, x.shape, 0)

    def shifted(z, k):
        if trailing:
            return jnp.where(rows >= k, pltpu.roll(z, k, 0), 0.0)
        return jnp.where(rows < T - k, pltpu.roll(z, T - k, 0), 0.0)

    s2 = x + shifted(x, 1)
    s4 = s2 + shifted(s2, 2)
    s8 = s4 + shifted(s4, 4)
    s16 = s8 + shifted(s8, 8)
    return jnp.where(group == 0, s2, jnp.where(group == 1, s4, jnp.where(group == 2, s8, s16)))


def _pool_count(group, shape):
    rows = lax.broadcasted_iota(jnp.int32, shape, 0)
    w = jnp.where(group == 0, 2, jnp.where(group == 1, 4, jnp.where(group == 2, 8, 16)))
    return jnp.minimum(rows + 1, w).astype(F32)


def pool_fwd(proj, pool_w, pool_scale, name, hosted=()):
    T = proj.shape[0]
    Hd = HEAD_DIM

    def body(x_ref, w_ref, sc_ref, a_ref):
        g = pl.program_id(0)
        x = x_ref[...]
        pooled = _pool_window(x, g, T, True) / _pool_count(g, x.shape) - x
        a_ref[...] = (_dot(pooled.astype(BF16), w_ref[0].astype(BF16)) * sc_ref[...]).astype(BF16)

    return _call(
        body, hosted, name=name, grid=(N_GROUPS,),
        in_specs=[pl.BlockSpec((T, Hd), lambda g: (0, g)), pl.BlockSpec((1, Hd, Hd), lambda g: (g, 0, 0)),
                  pl.BlockSpec((1, Hd), lambda g: (0, g))],
        out_specs=[pl.BlockSpec((T, Hd), lambda g: (0, g))],
        out_shape=[jax.ShapeDtypeStruct((T, N_GROUPS * Hd), BF16)],
        args=[proj, pool_w, pool_scale])


def pool_bwd(proj, da, pool_w, pool_scale, name, hosted=()):
    T = proj.shape[0]
    Hd = HEAD_DIM

    def body(x_ref, da_ref, w_ref, sc_ref, dx_ref, dw_ref, dsc_ref):
        g = pl.program_id(0)
        x = x_ref[...]
        cnt = _pool_count(g, x.shape)
        pooled = (_pool_window(x, g, T, True) / cnt - x).astype(BF16)
        wb = w_ref[0].astype(BF16)
        dav = da_ref[...]
        dsc_ref[...] = jnp.sum(dav * _dot(pooled, wb), axis=0, keepdims=True)
        dout = (dav * sc_ref[...]).astype(BF16)
        dw_ref[0] = _dot_tn(pooled, dout)
        dpooled = _dot_nt(dout, wb)
        dx_ref[...] = (_pool_window(dpooled / cnt, g, T, False) - dpooled).astype(BF16)

    col_spec = pl.BlockSpec((T, Hd), lambda g: (0, g))
    return _call(
        body, hosted, name=name, grid=(N_GROUPS,),
        in_specs=[col_spec, col_spec, pl.BlockSpec((1, Hd, Hd), lambda g: (g, 0, 0)), pl.BlockSpec((1, Hd), lambda g: (0, g))],
        out_specs=[col_spec, pl.BlockSpec((1, Hd, Hd), lambda g: (g, 0, 0)), pl.BlockSpec((1, Hd), lambda g: (0, g))],
        out_shape=[jax.ShapeDtypeStruct((T, N_GROUPS * Hd), BF16), jax.ShapeDtypeStruct((N_GROUPS, Hd, Hd), F32),
                   jax.ShapeDtypeStruct((1, N_GROUPS * Hd), F32)],
        args=[proj, da, pool_w, pool_scale])


def _ret_tables(T):
    Hd, C = HEAD_DIM, RET_CHUNK
    inv_freq = 1.0 / (ROPE_BASE ** (jnp.arange(0, Hd, 2, dtype=F32) / Hd))
    ang = jnp.arange(T, dtype=F32)[:, None] * inv_freq[None, :]
    cos, sin = jnp.cos(ang), jnp.sin(ang)
    cos2 = jnp.concatenate([cos, cos], axis=-1)
    sin2 = jnp.concatenate([-sin, sin], axis=-1)
    log_gamma = jnp.log1p(-jnp.exp2(-5.0 - jnp.arange(N_GROUPS, dtype=F32)))
    pos = jnp.arange(C, dtype=F32)
    rel = pos[:, None] - pos[None, :]
    intra = jnp.where(rel[None] >= 0, jnp.exp(log_gamma[:, None, None] * jnp.maximum(rel, 0.0)[None]), 0.0)
    k_tail = jnp.exp(log_gamma[:, None] * (C - 1 - pos)[None, :])
    q_head = jnp.exp(log_gamma[:, None] * (pos + 1.0)[None, :])
    chunk_decay = jnp.exp(log_gamma * C)
    wide = lambda t: jnp.broadcast_to(t[:, :, None], (N_GROUPS, C, Hd))
    return cos2, sin2, intra, wide(k_tail), wide(q_head), jnp.broadcast_to(chunk_decay[:, None, None], (N_GROUPS, 1, Hd))


def _rope(x, cos2, sin2):
    return x * cos2 + pltpu.roll(x, HEAD_DIM // 2, 1) * sin2


def _rope_t(d, cos2, sin2):
    return d * cos2 + pltpu.roll(d * sin2, HEAD_DIM // 2, 1)


def _ret_specs(tseg, seg_of):
    Hd, G = HEAD_DIM, N_GROUPS
    col = lambda kind: pl.BlockSpec((tseg, Hd), lambda h, s: (seg_of(s), G * kind + h))
    tab = pl.BlockSpec((tseg, Hd), lambda h, s: (seg_of(s), 0))
    head = pl.BlockSpec((1, RET_CHUNK, Hd), lambda h, s: (h, 0, 0))
    cd = pl.BlockSpec((1, 1, Hd), lambda h, s: (h, 0, 0))
    gain = pl.BlockSpec((1, Hd), lambda h, s: (0, h))
    return col, tab, head, cd, gain


def ret_fwd(proj, ret_norm, tables, name, hosted=()):
    T = proj.shape[0]
    Hd, C, G = HEAD_DIM, RET_CHUNK, N_GROUPS
    tseg = min(T, 1024)
    nseg, nck = T // tseg, tseg // C
    scale = Hd ** -0.5
    cos2, sin2, intra, k_tail, q_head, chunk_decay = tables

    def body(q_ref, k_ref, v_ref, g_ref, gain_ref, cos_ref, sin_ref, m_ref, kt_ref, qh_ref, cd_ref,
             b_ref, o_ref, rp_ref, state):
        @pl.when(pl.program_id(1) == 0)
        def _():
            state[...] = jnp.zeros_like(state)

        def chunk(ci, carry):
            rows = pl.ds(pl.multiple_of(ci * C, C), C)
            cos, sin = cos_ref[rows, :], sin_ref[rows, :]
            qr = _rope(q_ref[rows, :], cos, sin)
            kr = _rope(k_ref[rows, :], cos, sin) * scale
            qb, kb, vb = qr.astype(BF16), kr.astype(BF16), v_ref[rows, :].astype(BF16)
            r = state[...]
            rp_ref[0, ci] = r.astype(BF16)
            sc = _dot_nt(qb, kb) * m_ref[0]
            o = _dot(sc.astype(BF16), vb) + _dot((qr * qh_ref[0]).astype(BF16), r.astype(BF16))
            state[...] = cd_ref[0] * r + _dot_tn((kr * kt_ref[0]).astype(BF16), vb)
            o_ref[rows, :] = o
            on = o * _rstd(o)
            b_ref[rows, :] = (jax.nn.silu(g_ref[rows, :]) * (on * gain_ref[...])).astype(BF16)
            return carry

        lax.fori_loop(0, nck, chunk, 0)

    col, tab, head, cd, gain = _ret_specs(tseg, lambda s: s)
    out_col = pl.BlockSpec((tseg, Hd), lambda h, s: (s, h))
    return _call(
        body, hosted, name=name, grid=(G, nseg),
        in_specs=[col(1), col(2), col(3), col(4), gain, tab, tab, head, head, head, cd],
        out_specs=[out_col, out_col, pl.BlockSpec((1, nck, Hd, Hd), lambda h, s: (h, s, 0, 0))],
        out_shape=[jax.ShapeDtypeStruct((T, G * Hd), BF16), jax.ShapeDtypeStruct((T, G * Hd), F32),
                   jax.ShapeDtypeStruct((G, T // C, Hd, Hd), BF16)],
        scratch_shapes=[pltpu.VMEM((Hd, Hd), F32)],
        args=[proj, proj, proj, proj, ret_norm, cos2, sin2, intra, k_tail, q_head, chunk_decay])


def ret_bwd(proj, db, o_pre, r_prev, ret_norm, tables, name, hosted=()):
    T = proj.shape[0]
    Hd, C, G = HEAD_DIM, RET_CHUNK, N_GROUPS
    tseg = min(T, 1024)
    nseg, nck = T // tseg, tseg // C
    scale = Hd ** -0.5
    cos2, sin2, intra, k_tail, q_head, chunk_decay = tables

    def body(q_ref, k_ref, v_ref, g_ref, db_ref, o_ref, rp_ref, gain_ref, cos_ref, sin_ref, m_ref, kt_ref, qh_ref, cd_ref,
             d_ref, dgain_ref, gstate):
        @pl.when(pl.program_id(1) == 0)
        def _():
            gstate[...] = jnp.zeros_like(gstate)
            dgain_ref[...] = jnp.zeros_like(dgain_ref)

        def chunk(t, carry):
            ci = nck - 1 - t
            rows = pl.ds(pl.multiple_of(ci * C, C), C)
            cos, sin = cos_ref[rows, :], sin_ref[rows, :]
            qr = _rope(q_ref[rows, :], cos, sin)
            kr = _rope(k_ref[rows, :], cos, sin) * scale
            qb, kb, vb = qr.astype(BF16), kr.astype(BF16), v_ref[rows, :].astype(BF16)
            qhb, ktb = (qr * qh_ref[0]).astype(BF16), (kr * kt_ref[0]).astype(BF16)
            sc = (_dot_nt(qb, kb) * m_ref[0]).astype(BF16)
            o = o_ref[rows, :]
            rstd = _rstd(o)
            on = o * rstd
            gain = gain_ref[...]
            silu, dsilu = _silu_parts(g_ref[rows, :])
            dy = db_ref[rows, :]
            dgain_ref[...] += jnp.sum(dy * silu * on, axis=0, keepdims=True)
            dg = dy * on * gain * dsilu
            don = dy * silu * gain
            dob = (rstd * (don - on * jnp.mean(don * on, axis=-1, keepdims=True))).astype(BF16)
            gn = gstate[...]
            gb = gn.astype(BF16)
            da = (_dot_nt(dob, vb) * m_ref[0]).astype(BF16)
            dq = _dot(da, kb) + _dot_nt(dob, rp_ref[0, ci]) * qh_ref[0]
            dk = _dot_tn(da, qb) + _dot_nt(vb, gb) * kt_ref[0]
            dv = _dot_tn(sc, dob) + _dot(ktb, gb)
            gstate[...] = cd_ref[0] * gn + _dot_tn(qhb, dob)
            d_ref[0, rows, :] = _rope_t(dq, cos, sin).astype(BF16)
            d_ref[1, rows, :] = _rope_t(dk * scale, cos, sin).astype(BF16)
            d_ref[2, rows, :] = dv.astype(BF16)
            d_ref[3, rows, :] = dg.astype(BF16)
            return carry

        lax.fori_loop(0, nck, chunk, 0)

    rev = lambda s: nseg - 1 - s
    col, tab, head, cd, gain = _ret_specs(tseg, rev)
    act = pl.BlockSpec((tseg, Hd), lambda h, s: (rev(s), h))
    return _call(
        body, hosted, name=name, grid=(G, nseg),
        in_specs=[col(1), col(2), col(3), col(4), act, act, pl.BlockSpec((1, nck, Hd, Hd), lambda h, s: (h, rev(s), 0, 0)),
                  gain, tab, tab, head, head, head, cd],
        out_specs=[pl.BlockSpec((4, tseg, Hd), lambda h, s: (0, rev(s), h)), gain],
        out_shape=[jax.ShapeDtypeStruct((4, T, G * Hd), BF16), jax.ShapeDtypeStruct((1, G * Hd), F32)],
        scratch_shapes=[pltpu.VMEM((Hd, Hd), F32)],
        args=[proj, proj, proj, proj, db, o_pre, r_prev, ret_norm, cos2, sin2, intra, k_tail, q_head, chunk_decay])


def final_loss(h, gain, target, name, hosted=()):
    T, D = h.shape
    tm = min(T, 512)

    def body(h_ref, g_ref, t_ref, dh_ref, loss_ref, dg_ref):
        @pl.when(pl.program_id(0) == 0)
        def _():
            loss_ref[...] = jnp.zeros_like(loss_ref)
            dg_ref[...] = jnp.zeros_like(dg_ref)

        hh = h_ref[...]
        gain_v = g_ref[...]
        err = hh * _rstd(hh) * gain_v - t_ref[...]
        loss_ref[...] += 0.5 * jnp.sum(jnp.mean(err * err, axis=-1, keepdims=True), axis=0, keepdims=True)
        dhn, dg = _rmsnorm_bwd(err * (1.0 / D), hh, gain_v)
        dh_ref[...] = dhn
        dg_ref[...] += jnp.sum(dg, axis=0, keepdims=True)

    row_spec = pl.BlockSpec((tm, D), lambda i: (i, 0))
    vec_spec = pl.BlockSpec((1, D), lambda i: (0, 0))
    return _call(
        body, hosted, name=name, grid=(T // tm,),
        in_specs=[row_spec, vec_spec, row_spec],
        out_specs=[row_spec, pl.BlockSpec((1, 128), lambda i: (0, 0)), vec_spec],
        out_shape=[jax.ShapeDtypeStruct((T, D), F32), jax.ShapeDtypeStruct((1, 128), F32), jax.ShapeDtypeStruct((1, D), F32)],
        args=[h, gain, target])


def prereduce(grad, recv, core, name):
    nsh, R, C = grad.shape
    rh = R // 2

    def body(c_ref, g_ref, r_ref, o_ref):
        o_ref[...] = (g_ref[...].astype(F32) + r_ref[...].astype(F32)).astype(BF16)

    return pl.pallas_call(
        body, name=name,
        grid_spec=pltpu.PrefetchScalarGridSpec(
            num_scalar_prefetch=1, grid=(nsh,),
            in_specs=[pl.BlockSpec((1, rh, C), lambda j, c_ref: (j, c_ref[0], 0)), pl.BlockSpec((1, rh, C), lambda j, c_ref: (j, 0, 0))],
            out_specs=pl.BlockSpec((1, rh, C), lambda j, c_ref: (j, 0, 0))),
        out_shape=jax.ShapeDtypeStruct((nsh, rh, C), BF16),
        compiler_params=pltpu.CompilerParams(vmem_limit_bytes=VMEM_LIMIT_V7X),
    )(core, grad, recv)


def _adamw(w, g, m, v):
    m = ADAM_B1 * m + (1.0 - ADAM_B1) * g
    v = ADAM_B2 * v + (1.0 - ADAM_B2) * (g * g)
    m_hat = m / (1.0 - ADAM_B1 ** ADAM_STEP)
    v_hat = v / (1.0 - ADAM_B2 ** ADAM_STEP)
    return -ADAM_LR * (m_hat / (jnp.sqrt(v_hat) + ADAM_EPS) + ADAM_WD * w), m, v


def adamw_sharded(tensors, name, hosted=()):
    nt = len(tensors)
    nsh, R, C = tensors[0][0].shape
    tr = 256 if R % 256 == 0 else R // 2

    def body(*refs):
        ins, outs = refs[:4 * nt], refs[4 * nt:]
        for t in range(nt):
            p_ref, w_ref, m_ref, v_ref = ins[4 * t:4 * t + 4]
            g_ref, d_ref, nm_ref, nv_ref = outs[4 * t:4 * t + 4]
            g = p_ref[0].astype(F32)
            for i in range(1, nsh):
                g += p_ref[i].astype(F32)
            g_ref[...] = g
            d_ref[...], nm_ref[...], nv_ref[...] = _adamw(w_ref[...], g, m_ref[...], v_ref[...])

    spec = pl.BlockSpec((tr, C), lambda i: (i, 0))
    out = jax.ShapeDtypeStruct((R, C), F32)
    return _call(
        body, hosted, name=name, grid=(R // tr,),
        in_specs=[pl.BlockSpec((nsh, tr, C), lambda i: (0, i, 0)), spec, spec, spec] * nt,
        out_specs=[spec] * (4 * nt), out_shape=[out] * (4 * nt),
        args=[a for tensor in tensors for a in tensor])


def adamw_small(packs, w, m, v, name):
    ndev, R, L = packs.shape

    def body(p_ref, w_ref, m_ref, v_ref, g_ref, d_ref, nm_ref, nv_ref):
        g = p_ref[0]
        for i in range(1, ndev):
            g += p_ref[i]
        g_ref[...] = g
        d_ref[...], nm_ref[...], nv_ref[...] = _adamw(w_ref[...], g, m_ref[...], v_ref[...])

    out = jax.ShapeDtypeStruct((R, L), F32)
    return pl.pallas_call(body, name=name, out_shape=[out] * 4,
                          compiler_params=pltpu.CompilerParams(vmem_limit_bytes=VMEM_LIMIT_V7X))(packs, w, m, v)


BIG = ("ffn1_w1", "ffn1_w3", "ffn1_w2", "w_in", "w_out", "ffn2_w1", "ffn2_w3", "ffn2_w2")
SMALL = ("ffn1_norm", "mix_norm", "pool_w", "pool_scale", "ret_norm", "ffn2_norm", "final_norm")
WEIGHTS = ("ffn1_norm", "ffn1_w1", "ffn1_w3", "ffn1_w2", "mix_norm", "w_in", "pool_w", "pool_scale", "ret_norm", "w_out",
           "ffn2_norm", "ffn2_w1", "ffn2_w3", "ffn2_w2", "final_norm")


def _pack(parts):
    return jnp.concatenate([parts[k].reshape(-1, 128) for k in SMALL], axis=0)


def _unpack(pack, like):
    out, row = {}, 0
    for k in SMALL:
        rows = like[k].size // 128
        out[k] = pack[row:row + rows].reshape(like[k].shape)
        row += rows
    return out


def kernel(x, ffn1_norm, ffn1_w1, ffn1_w3, ffn1_w2, mix_norm, w_in, pool_w, pool_scale, ret_norm, w_out, ffn2_norm, ffn2_w1, ffn2_w3, ffn2_w2, final_norm, loss_target, m_ffn1_norm, m_ffn1_w1, m_ffn1_w3, m_ffn1_w2, m_mix_norm, m_w_in, m_pool_w, m_pool_scale, m_ret_norm, m_w_out, m_ffn2_norm, m_ffn2_w1, m_ffn2_w3, m_ffn2_w2, m_final_norm, v_ffn1_norm, v_ffn1_w1, v_ffn1_w3, v_ffn1_w2, v_mix_norm, v_w_in, v_pool_w, v_pool_scale, v_ret_norm, v_w_out, v_ffn2_norm, v_ffn2_w1, v_ffn2_w3, v_ffn2_w2, v_final_norm):
    w = dict(ffn1_norm=ffn1_norm, ffn1_w1=ffn1_w1, ffn1_w3=ffn1_w3, ffn1_w2=ffn1_w2, mix_norm=mix_norm, w_in=w_in, pool_w=pool_w,
             pool_scale=pool_scale, ret_norm=ret_norm, w_out=w_out, ffn2_norm=ffn2_norm, ffn2_w1=ffn2_w1, ffn2_w3=ffn2_w3,
             ffn2_w2=ffn2_w2, final_norm=final_norm)
    m = dict(ffn1_norm=m_ffn1_norm, ffn1_w1=m_ffn1_w1, ffn1_w3=m_ffn1_w3, ffn1_w2=m_ffn1_w2, mix_norm=m_mix_norm, w_in=m_w_in,
             pool_w=m_pool_w, pool_scale=m_pool_scale, ret_norm=m_ret_norm, w_out=m_w_out, ffn2_norm=m_ffn2_norm, ffn2_w1=m_ffn2_w1,
             ffn2_w3=m_ffn2_w3, ffn2_w2=m_ffn2_w2, final_norm=m_final_norm)
    v = dict(ffn1_norm=v_ffn1_norm, ffn1_w1=v_ffn1_w1, ffn1_w3=v_ffn1_w3, ffn1_w2=v_ffn1_w2, mix_norm=v_mix_norm, w_in=v_w_in,
             pool_w=v_pool_w, pool_scale=v_pool_scale, ret_norm=v_ret_norm, w_out=v_w_out, ffn2_norm=v_ffn2_norm, ffn2_w1=v_ffn2_w1,
             ffn2_w3=v_ffn2_w3, ffn2_w2=v_ffn2_w2, final_norm=v_final_norm)
    xs, target = x[0], loss_target[0]
    T = xs.shape[0]
    tables = _ret_tables(T)
    core = lax.axis_index("c").astype(jnp.int32).reshape(1)
    sh = {k: w[k][0].astype(BF16) for k in BIG}
    gather = lambda *names: [ChipExchange([sh[k] for k in names], False)]
    wg, grad, delta, new_m, new_v = {}, {}, {}, {}, {}

    def update(names, pieces, name, hosted=()):
        outs, extras = adamw_sharded([(p, w[k][0], m[k][0], v[k][0]) for k, p in zip(names, pieces)], name, hosted)
        for t, k in enumerate(names):
            grad[k], delta[k], new_m[k], new_v[k] = [o[None] for o in outs[4 * t:4 * t + 4]]
        return extras

    def reduce_in_chip(name, partial, recv):
        return prereduce(partial, recv, core, "prereduce_" + name)

    (wg["ffn1_w1"], wg["ffn1_w3"]), = exchange(gather("ffn1_w1", "ffn1_w3"), "gather_ffn1")
    (n1, a1, b1, s1), ((wg["ffn1_w2"], wg["w_in"]),) = ffn_up(
        xs, ffn1_norm, wg["ffn1_w1"], wg["ffn1_w3"], "ffn1_up", gather("ffn1_w2", "w_in"))
    (h1,), ((wg["w_out"],),) = ffn_down(s1, wg["ffn1_w2"], xs, "ffn1_down", gather("w_out"))
    (u, proj), ((wg["ffn2_w1"],),) = mix_in(h1, mix_norm, wg["w_in"], "mix_in", gather("ffn2_w1"))
    (pa,), _ = pool_fwd(proj, pool_w[0], pool_scale, "pool_fwd")
    (rb, o_pre, r_prev), ((wg["ffn2_w3"],),) = ret_fwd(proj, ret_norm, tables, "ret_fwd", gather("ffn2_w3"))
    (h2,), _ = mix_out(pa, rb, wg["w_out"], h1, "mix_out")
    (n2, a2, b2, s2), ((wg["ffn2_w2"],),) = ffn_up(
        h2, ffn2_norm, wg["ffn2_w1"], wg["ffn2_w3"], "ffn2_up", gather("ffn2_w2"))
    (h3,), _ = ffn_down(s2, wg["ffn2_w2"], h2, "ffn2_down")
    (dh3, loss, d_final), _ = final_loss(h3, final_norm[None], target, "final_loss")
    loss = lax.psum(loss[0, 0], ("x", "y", "c"))

    (da2, db2), _ = ffn_bwd_act(dh3, wg["ffn2_w2"], a2, b2, "ffn2_bwd_act")
    (g_f2w2,), _ = ffn_dw2(s2, dh3, "ffn2_dw2")
    (g_f2w1, g_f2w3), ((r_f2w2,),) = ffn_dw13(n2, da2, db2, "ffn2_dw13", [SiblingExchange([g_f2w2])])
    p_f2w2 = reduce_in_chip("ffn2_w2", g_f2w2, r_f2w2)
    (dh2, d_ffn2), ((q_f2w2,), (r_f2w1, r_f2w3)) = ffn_bwd_in(
        da2, db2, wg["ffn2_w1"], wg["ffn2_w3"], h2, ffn2_norm, dh3, "ffn2_bwd_in",
        [ChipExchange([p_f2w2], True), SiblingExchange([g_f2w1, g_f2w3])])
    p_f2w1 = reduce_in_chip("ffn2_w1", g_f2w1, r_f2w1)
    p_f2w3 = reduce_in_chip("ffn2_w3", g_f2w3, r_f2w3)
    (dpa, drb, g_wout), ((q_f2w1,),) = mix_out_bwd(dh2, wg["w_out"], pa, rb, "mix_out_bwd", [ChipExchange([p_f2w1], True)])
    (dpool, d_pool_w, d_pool_scale), _ = pool_bwd(proj, dpa, pool_w[0], pool_scale, "pool_bwd")
    (dqkvg, d_ret_norm), ((q_f2w3,), (r_wout,)) = ret_bwd(
        proj, drb, o_pre, r_prev, ret_norm, tables, "ret_bwd", [ChipExchange([p_f2w3], True), SiblingExchange([g_wout])])
    p_wout = reduce_in_chip("w_out", g_wout, r_wout)
    d = jnp.concatenate([dpool, dqkvg[0], dqkvg[1], dqkvg[2], dqkvg[3]], axis=1)
    (g_win,), ((q_wout,),) = mix_dwin(u, d, N_CHIPS, "mix_dwin", [ChipExchange([p_wout], True)])
    (dh1, d_mix), ((r_win,),) = mix_in_bwd(d, wg["w_in"], h1, mix_norm, dh2, "mix_in_bwd", [SiblingExchange([g_win])])
    p_win = reduce_in_chip("w_in", g_win, r_win)
    (da1, db1), ((q_win,),) = ffn_bwd_act(dh1, wg["ffn1_w2"], a1, b1, "ffn1_bwd_act", [ChipExchange([p_win], True)])
    (g_f1w1, g_f1w3), _ = ffn_dw13(n1, da1, db1, "ffn1_dw13")
    (g_f1w2,), ((r_f1w1, r_f1w3),) = ffn_dw2(s1, dh1, "ffn1_dw2", [SiblingExchange([g_f1w1, g_f1w3])])
    p_f1w1 = reduce_in_chip("ffn1_w1", g_f1w1, r_f1w1)
    p_f1w3 = reduce_in_chip("ffn1_w3", g_f1w3, r_f1w3)
    (dx, d_ffn1), ((q_f1w1, q_f1w3), (r_f1w2,)) = ffn_bwd_in(
        da1, db1, wg["ffn1_w1"], wg["ffn1_w3"], xs, ffn1_norm, dh1, "ffn1_bwd_in",
        [ChipExchange([p_f1w1, p_f1w3], True), SiblingExchange([g_f1w2])])
    p_f1w2 = reduce_in_chip("ffn1_w2", g_f1w2, r_f1w2)

    (q_f1w2,), = update(["ffn2_w1", "ffn2_w3"], [q_f2w1, q_f2w3], "adamw_ffn2_w13", [ChipExchange([p_f1w2], True)])
    update(["ffn2_w2"], [q_f2w2], "adamw_ffn2_w2")
    update(["w_in"], [q_win], "adamw_w_in")
    update(["w_out"], [q_wout], "adamw_w_out")
    update(["ffn1_w1", "ffn1_w3"], [q_f1w1, q_f1w3], "adamw_ffn1_w13")
    update(["ffn1_w2"], [q_f1w2], "adamw_ffn1_w2")

    small = {"ffn1_norm": d_ffn1, "mix_norm": d_mix, "pool_w": d_pool_w, "pool_scale": d_pool_scale,
             "ret_norm": d_ret_norm, "ffn2_norm": d_ffn2, "final_norm": d_final}
    packs = all_exchange_small(_pack(small), "gather_small")
    outs = adamw_small(packs, _pack(w), _pack(m), _pack(v), "adamw_small")
    for res, pack in zip((grad, delta, new_m, new_v), outs):
        res.update(_unpack(pack, w))

    return (loss, dx[None], *[grad[k] for k in WEIGHTS], *[delta[k] for k in WEIGHTS],
            *[new_m[k] for k in WEIGHTS], *[new_v[k] for k in WEIGHTS])
```

```python
import math

import jax
import jax.numpy as jnp
from jax import lax
from jax.experimental import pallas as pl
from jax.experimental.pallas import tpu as pltpu

F32 = jnp.float32
BF16 = jnp.bfloat16

EPS = 1e-6
N_CHIPS = 4
N_GROUPS = 4
HEAD_DIM = 128
RET_CHUNK = 128
ROPE_BASE = 10000.0
ADAM_LR, ADAM_B1, ADAM_B2, ADAM_EPS, ADAM_WD, ADAM_STEP = 0.001, 0.9, 0.999, 1e-08, 0.01, 10
VMEM_LIMIT_V7X = 56 * 1024 * 1024
MESH = pl.DeviceIdType.MESH
ANY = pl.BlockSpec(memory_space=pl.ANY)


def _dot(a, b):
    return jnp.dot(a, b, preferred_element_type=F32)


def _dot_nt(a, b):
    return lax.dot_general(a, b, (((1,), (1,)), ((), ())), preferred_element_type=F32)


def _dot_tn(a, b):
    return lax.dot_general(a, b, (((0,), (0,)), ((), ())), preferred_element_type=F32)


def _rstd(h):
    return lax.rsqrt(jnp.mean(h * h, axis=-1, keepdims=True) + EPS)


def _rmsnorm_bwd(dn, h, gain):
    r = _rstd(h)
    nh = h * r
    dnh = dn * gain
    dh = r * (dnh - nh * jnp.mean(dnh * nh, axis=-1, keepdims=True))
    return dh, dn * nh


def _silu_parts(a):
    sig = jax.nn.sigmoid(a)
    silu = a * sig
    return silu, sig + silu * (1.0 - sig)


def _mesh_pos():
    return lax.axis_index("x"), lax.axis_index("y"), lax.axis_index("c")


class ChipExchange:
    def __init__(self, srcs, scatter):
        n = len(srcs)
        self.srcs, self.scatter, self.n = list(srcs), scatter, n
        self.half_rows = [s.shape[1] if scatter else s.shape[0] // 2 for s in srcs]
        self.out_shape = [jax.ShapeDtypeStruct((N_CHIPS, 2 * rh, s.shape[-1]), s.dtype) for s, rh in zip(srcs, self.half_rows)]
        if scatter:
            self.out_shape += [jax.ShapeDtypeStruct((2, rh // 2, s.shape[-1]), s.dtype) for s, rh in zip(srcs, self.half_rows)]
        dma = pltpu.SemaphoreType.DMA
        self.sems = [dma((n,)), dma((4 * n,)), dma((4 * n,)), dma((2 * n,)), dma((2 * n,)), dma((4 * n,)), dma((4 * n,))]

    def _copies(self, src, out, sems):
        local_sem, hop1_send, hop1_recv, hop2_send, hop2_recv, d2d_send, d2d_recv = sems
        x, y, c = _mesh_pos()
        me, dg = 2 * x + y, 2 * (1 - x) + (1 - y)
        sibling = (x, y, 1 - c)
        n = self.n

        def nb(a):
            nx, ny = x ^ (1 - a), y ^ a
            return 2 * nx + ny, (nx, ny, c)

        def remote(s, d, send, recv, k, to):
            return pltpu.make_async_remote_copy(src_ref=s, dst_ref=d, send_sem=send.at[k], recv_sem=recv.at[k],
                                                device_id=to, device_id_type=MESH)

        class Copies:
            def slot(_, t, chip, core):
                rh = self.half_rows[t]
                return out[t].at[chip, pl.ds(core * rh, rh), :]

            def quarter(_, t, chip, q):
                qh = self.half_rows[t] // 2
                return out[t].at[chip, pl.ds(c * 2 * qh + q * qh, qh), :]

            def local(k, t):
                if self.scatter:
                    return pltpu.make_async_copy(src[t].at[me], k.slot(t, me, c), local_sem.at[t])
                return pltpu.make_async_copy(src[t], out[t].at[me], local_sem.at[t])

            def hop1(k, t, a, transit=False):
                rh = self.half_rows[t]
                chip, to = nb(a)
                if transit:
                    piece = src[t].at[dg, pl.ds(a * (rh // 2), rh // 2), :]
                    return remote(piece, out[n + t].at[a], hop1_send, hop1_recv, 4 * t + 2 + a, to)
                piece = src[t].at[chip] if self.scatter else src[t].at[pl.ds(c * rh, rh), :]
                return remote(piece, k.slot(t, me, c), hop1_send, hop1_recv, 4 * t + a, to)

            def landed1(k, t, a, transit=False):
                here = out[n + t].at[a] if transit else k.slot(t, nb(a)[0], c)
                return remote(here, here, hop1_send, hop1_recv, 4 * t + (2 if transit else 0) + a, sibling)

            def hop2(k, t, q):
                origin, to = nb(q)[0], nb(1 - q)[1]
                piece = out[n + t].at[q] if self.scatter else k.quarter(t, origin, q)
                return remote(piece, k.quarter(t, origin, q), hop2_send, hop2_recv, 2 * t + q, to)

            def landed2(k, t, q):
                here = k.quarter(t, dg, q)
                return remote(here, here, hop2_send, hop2_recv, 2 * t + q, sibling)

            def d2d(k, t, p, chip, own=False, arriving=False):
                if arriving:
                    there = k.slot(t, chip, 1 - c)
                    return remote(there, there, d2d_send, d2d_recv, 4 * t + p, sibling)
                piece = src[t].at[me] if own else k.slot(t, chip, c)
                return remote(piece, k.slot(t, chip, c), d2d_send, d2d_recv, 4 * t + p, sibling)

        return Copies(), nb, me, dg, c

    def start(self, src, out, sems):
        k, nb, me, dg, c = self._copies(src, out, sems)
        for t in range(self.n):
            k.local(t).start()
            for first in range(2):
                a = first ^ c
                k.hop1(t, a).start()
                if self.scatter:
                    k.hop1(t, a, transit=True).start()
            if self.scatter:
                k.d2d(t, 3, me, own=True).start()

    def mid(self, src, out, sems):
        k, nb, me, dg, c = self._copies(src, out, sems)
        for t in range(self.n):
            for first in range(2):
                a = first ^ c
                if self.scatter:
                    k.landed1(t, a, transit=True).wait_recv()
                    k.hop2(t, a).start()
                k.landed1(t, a).wait_recv()
                if not self.scatter:
                    k.hop2(t, a).start()
                k.d2d(t, a, nb(a)[0]).start()

    def finish(self, src, out, sems):
        k, nb, me, dg, c = self._copies(src, out, sems)
        for t in range(self.n):
            for q in range(2):
                k.landed2(t, q).wait_recv()
            k.d2d(t, 2, dg).start()
        for t in range(self.n):
            for a in range(2):
                k.d2d(t, a, nb(a)[0], arriving=True).wait_recv()
            k.d2d(t, 2, dg, arriving=True).wait_recv()
            if self.scatter:
                k.d2d(t, 3, me, arriving=True).wait_recv()
        for t in range(self.n):
            k.local(t).wait()
            for a in range(2):
                k.hop1(t, a).wait_send()
                if self.scatter:
                    k.hop1(t, a, transit=True).wait_send()
                k.hop2(t, a).wait_send()
                k.d2d(t, a, nb(a)[0]).wait_send()
            k.d2d(t, 2, dg).wait_send()
            if self.scatter:
                k.d2d(t, 3, me, own=True).wait_send()


class SiblingExchange:
    def __init__(self, grads):
        self.srcs, self.n = list(grads), len(grads)
        self.half_rows = [g.shape[1] // 2 for g in grads]
        self.out_shape = [jax.ShapeDtypeStruct((g.shape[0], rh, g.shape[2]), g.dtype) for g, rh in zip(grads, self.half_rows)]
        self.sems = [pltpu.SemaphoreType.DMA((self.n,)), pltpu.SemaphoreType.DMA((self.n,))]

    def _plan(self, src, out, sems):
        x, y, c = _mesh_pos()
        return [pltpu.make_async_remote_copy(
            src_ref=src[t].at[:, pl.ds((1 - c) * self.half_rows[t], self.half_rows[t]), :], dst_ref=out[t],
            send_sem=sems[0].at[t], recv_sem=sems[1].at[t], device_id=(x, y, 1 - c), device_id_type=MESH) for t in range(self.n)]

    def start(self, src, out, sems):
        for cp in self._plan(src, out, sems):
            cp.start()

    def mid(self, src, out, sems):
        pass

    def finish(self, src, out, sems):
        for cp in self._plan(src, out, sems):
            cp.wait()


def _call(body, hosted=(), *, name, in_specs, out_specs, out_shape, args, grid=(), scratch_shapes=()):
    n_in, n_out, n_scr = len(in_specs), len(out_specs), len(scratch_shapes)
    total = math.prod(grid)
    mid_step = max(0, (5 * total) // 8 - 1)

    def full(*refs):
        pos = [0]

        def take(k):
            pos[0] += k
            return refs[pos[0] - k:pos[0]]

        ins, h_in = take(n_in), [take(h.n) for h in hosted]
        outs, h_out = take(n_out), [take(len(h.out_shape)) for h in hosted]
        scr, h_sem = take(n_scr), [take(len(h.sems)) for h in hosted]
        step = 0
        for axis, size in enumerate(grid):
            step = step * size + pl.program_id(axis)

        def phase(at, method):
            if not hosted:
                return
            if total == 1:
                for h, s, o, m in zip(hosted, h_in, h_out, h_sem):
                    getattr(h, method)(s, o, m)
                return

            @pl.when(step == at)
            def _():
                for h, s, o, m in zip(hosted, h_in, h_out, h_sem):
                    getattr(h, method)(s, o, m)

        phase(0, "start")
        body(*ins, *outs, *scr)
        phase(mid_step, "mid")
        phase(total - 1, "finish")

    results = pl.pallas_call(
        full, name=name, grid=grid,
        in_specs=list(in_specs) + [ANY] * sum(h.n for h in hosted),
        out_specs=list(out_specs) + [ANY] * sum(len(h.out_shape) for h in hosted),
        out_shape=list(out_shape) + [s for h in hosted for s in h.out_shape],
        scratch_shapes=list(scratch_shapes) + [s for h in hosted for s in h.sems],
        compiler_params=pltpu.CompilerParams(vmem_limit_bytes=VMEM_LIMIT_V7X),
    )(*args, *[s for h in hosted for s in h.srcs])
    outs, extras, pos = list(results[:n_out]), [], n_out
    for h in hosted:
        extras.append(list(results[pos:pos + h.n]))
        pos += len(h.out_shape)
    return outs, extras


def exchange(hosted, name):
    return _call(lambda: None, hosted, name=name, in_specs=[], out_specs=[], out_shape=[], args=[])[1]


def all_exchange_small(pack, name):
    R, L = pack.shape
    flips = [(dx, dy, dc) for dx in (0, 1) for dy in (0, 1) for dc in (0, 1)][1:]

    def body(src, out, local_sem, send_sem, recv_sem):
        x, y, c = _mesh_pos()
        me = 4 * x + 2 * y + c
        local = pltpu.make_async_copy(src, out.at[me], local_sem)
        local.start()
        copies = []
        for k, (dx, dy, dc) in enumerate(flips):
            copies.append(pltpu.make_async_remote_copy(
                src_ref=src, dst_ref=out.at[me], send_sem=send_sem.at[k], recv_sem=recv_sem.at[k],
                device_id=(x ^ dx, y ^ dy, c ^ dc), device_id_type=MESH))
        for cp in copies:
            cp.start()
        for k, (dx, dy, dc) in enumerate(flips):
            landed = out.at[4 * (x ^ dx) + 2 * (y ^ dy) + (c ^ dc)]
            pltpu.make_async_remote_copy(src_ref=landed, dst_ref=landed, send_sem=send_sem.at[k], recv_sem=recv_sem.at[k],
                                         device_id=(x ^ dx, y ^ dy, c ^ dc), device_id_type=MESH).wait_recv()
        for cp in copies:
            cp.wait_send()
        local.wait()

    return pl.pallas_call(
        body, name=name, in_specs=[ANY], out_specs=ANY,
        out_shape=jax.ShapeDtypeStruct((2 * N_CHIPS, R, L), pack.dtype),
        scratch_shapes=[pltpu.SemaphoreType.DMA, pltpu.SemaphoreType.DMA((7,)), pltpu.SemaphoreType.DMA((7,))],
    )(pack)


def ffn_up(h, gain, w1g, w3g, name, hosted=()):
    T, D = h.shape
    nsh, _, Fs = w1g.shape
    tm = min(T, 1024)

    def body(h_ref, g_ref, w1_ref, w3_ref, n_ref, a_ref, b_ref, s_ref):
        @pl.when(pl.program_id(1) == 0)
        def _():
            hh = h_ref[...]
            n_ref[...] = (hh * _rstd(hh) * g_ref[...]).astype(BF16)

        n = n_ref[...]
        a = _dot(n, w1_ref[0])
        b = _dot(n, w3_ref[0])
        a_ref[0] = a.astype(BF16)
        b_ref[0] = b.astype(BF16)
        s_ref[0] = (a * jax.nn.sigmoid(a) * b).astype(BF16)

    act = jax.ShapeDtypeStruct((nsh, T, Fs), BF16)
    act_spec = pl.BlockSpec((1, tm, Fs), lambda i, j: (j, i, 0))
    w_spec = pl.BlockSpec((1, D, Fs), lambda i, j: (j, 0, 0))
    return _call(
        body, hosted, name=name, grid=(T // tm, nsh),
        in_specs=[pl.BlockSpec((tm, D), lambda i, j: (i, 0)), pl.BlockSpec((1, D), lambda i, j: (0, 0)), w_spec, w_spec],
        out_specs=[pl.BlockSpec((tm, D), lambda i, j: (i, 0)), act_spec, act_spec, act_spec],
        out_shape=[jax.ShapeDtypeStruct((T, D), BF16), act, act, act],
        args=[h, gain, w1g, w3g])


def ffn_down(s, w2g, h, name, hosted=()):
    nsh, T, Fs = s.shape
    D = h.shape[1]
    tm = min(T, 512)

    def body(s_ref, w2_ref, h_ref, o_ref):
        f = _dot(s_ref[0], w2_ref[0])
        for j in range(1, nsh):
            f += _dot(s_ref[j], w2_ref[j])
        o_ref[...] = h_ref[...] + 0.5 * f

    return _call(
        body, hosted, name=name, grid=(T // tm,),
        in_specs=[pl.BlockSpec((nsh, tm, Fs), lambda i: (0, i, 0)), pl.BlockSpec((nsh, Fs, D), lambda i: (0, 0, 0)),
                  pl.BlockSpec((tm, D), lambda i: (i, 0))],
        out_specs=[pl.BlockSpec((tm, D), lambda i: (i, 0))],
        out_shape=[jax.ShapeDtypeStruct((T, D), F32)],
        args=[s, w2g, h])


def ffn_bwd_act(dh, w2g, a, b, name, hosted=()):
    T, D = dh.shape
    nsh, Fs, _ = w2g.shape
    tm = min(T, 1024)

    def body(dh_ref, w2_ref, a_ref, b_ref, da_ref, db_ref):
        df = (0.5 * dh_ref[...]).astype(BF16)
        ds = _dot_nt(df, w2_ref[0])
        silu, dsilu = _silu_parts(a_ref[0].astype(F32))
        da_ref[0] = (ds * b_ref[0].astype(F32) * dsilu).astype(BF16)
        db_ref[0] = (ds * silu).astype(BF16)

    act = jax.ShapeDtypeStruct((nsh, T, Fs), BF16)
    act_spec = pl.BlockSpec((1, tm, Fs), lambda j, i: (j, i, 0))
    return _call(
        body, hosted, name=name, grid=(nsh, T // tm),
        in_specs=[pl.BlockSpec((tm, D), lambda j, i: (i, 0)), pl.BlockSpec((1, Fs, D), lambda j, i: (j, 0, 0)), act_spec, act_spec],
        out_specs=[act_spec, act_spec],
        out_shape=[act, act],
        args=[dh, w2g, a, b])


def ffn_dw2(s, dh, name, hosted=()):
    nsh, T, Fs = s.shape
    D = dh.shape[1]
    tk = min(T, 512)
    nk = T // tk

    def body(s_ref, dh_ref, o_ref, acc):
        k = pl.program_id(1)

        @pl.when(k == 0)
        def _():
            acc[...] = jnp.zeros_like(acc)

        acc[...] += _dot_tn(s_ref[0], (0.5 * dh_ref[...]).astype(BF16))

        @pl.when(k == nk - 1)
        def _():
            o_ref[0] = acc[...].astype(BF16)

    return _call(
        body, hosted, name=name, grid=(nsh, nk),
        in_specs=[pl.BlockSpec((1, tk, Fs), lambda j, k: (j, k, 0)), pl.BlockSpec((tk, D), lambda j, k: (k, 0))],
        out_specs=[pl.BlockSpec((1, Fs, D), lambda j, k: (j, 0, 0))],
        out_shape=[jax.ShapeDtypeStruct((nsh, Fs, D), BF16)],
        scratch_shapes=[pltpu.VMEM((Fs, D), F32)],
        args=[s, dh])


def ffn_dw13(n, da, db, name, hosted=()):
    T, D = n.shape
    nsh, _, Fs = da.shape
    tk = min(T, 512)
    nk = T // tk

    def body(n_ref, da_ref, db_ref, o1_ref, o3_ref, acc1, acc3):
        k = pl.program_id(1)

        @pl.when(k == 0)
        def _():
            acc1[...] = jnp.zeros_like(acc1)
            acc3[...] = jnp.zeros_like(acc3)

        nn = n_ref[...]
        acc1[...] += _dot_tn(nn, da_ref[0])
        acc3[...] += _dot_tn(nn, db_ref[0])

        @pl.when(k == nk - 1)
        def _():
            o1_ref[0] = acc1[...].astype(BF16)
            o3_ref[0] = acc3[...].astype(BF16)

    act_spec = pl.BlockSpec((1, tk, Fs), lambda j, k: (j, k, 0))
    out = jax.ShapeDtypeStruct((nsh, D, Fs), BF16)
    out_spec = pl.BlockSpec((1, D, Fs), lambda j, k: (j, 0, 0))
    return _call(
        body, hosted, name=name, grid=(nsh, nk),
        in_specs=[pl.BlockSpec((tk, D), lambda j, k: (k, 0)), act_spec, act_spec],
        out_specs=[out_spec, out_spec],
        out_shape=[out, out],
        scratch_shapes=[pltpu.VMEM((D, Fs), F32), pltpu.VMEM((D, Fs), F32)],
        args=[n, da, db])


def ffn_bwd_in(da, db, w1g, w3g, h, gain, dh, name, hosted=()):
    nsh, T, Fs = da.shape
    D = h.shape[1]
    tm = min(T, 256)

    def body(da_ref, db_ref, w1_ref, w3_ref, h_ref, g_ref, dh_ref, o_ref, dg_ref):
        dn = _dot_nt(da_ref[0], w1_ref[0]) + _dot_nt(db_ref[0], w3_ref[0])
        for j in range(1, nsh):
            dn += _dot_nt(da_ref[j], w1_ref[j]) + _dot_nt(db_ref[j], w3_ref[j])
        dhn, dg = _rmsnorm_bwd(dn, h_ref[...], g_ref[...])
        o_ref[...] = dh_ref[...] + dhn

        @pl.when(pl.program_id(0) == 0)
        def _():
            dg_ref[...] = jnp.zeros_like(dg_ref)

        dg_ref[...] += jnp.sum(dg, axis=0, keepdims=True)

    act_spec = pl.BlockSpec((nsh, tm, Fs), lambda i: (0, i, 0))
    w_spec = pl.BlockSpec((nsh, D, Fs), lambda i: (0, 0, 0))
    row_spec = pl.BlockSpec((tm, D), lambda i: (i, 0))
    vec_spec = pl.BlockSpec((1, D), lambda i: (0, 0))
    return _call(
        body, hosted, name=name, grid=(T // tm,),
        in_specs=[act_spec, act_spec, w_spec, w_spec, row_spec, vec_spec, row_spec],
        out_specs=[row_spec, vec_spec],
        out_shape=[jax.ShapeDtypeStruct((T, D), F32), jax.ShapeDtypeStruct((1, D), F32)],
        args=[da, db, w1g, w3g, h, gain, dh])


def mix_in(h, gain, wing, name, hosted=()):
    T, D = h.shape
    nsh, _, Cs = wing.shape
    tm = min(T, 512)

    def body(h_ref, g_ref, w_ref, u_ref, p_ref):
        hh = h_ref[...]
        u = (hh * _rstd(hh) * g_ref[...]).astype(BF16)
        u_ref[...] = u
        for j in range(nsh):
            p_ref[:, j * Cs:(j + 1) * Cs] = _dot(u, w_ref[j])

    return _call(
        body, hosted, name=name, grid=(T // tm,),
        in_specs=[pl.BlockSpec((tm, D), lambda i: (i, 0)), pl.BlockSpec((1, D), lambda i: (0, 0)),
                  pl.BlockSpec((nsh, D, Cs), lambda i: (0, 0, 0))],
        out_specs=[pl.BlockSpec((tm, D), lambda i: (i, 0)), pl.BlockSpec((tm, nsh * Cs), lambda i: (i, 0))],
        out_shape=[jax.ShapeDtypeStruct((T, D), BF16), jax.ShapeDtypeStruct((T, nsh * Cs), F32)],
        args=[h, gain, wing])


def mix_out(a, b, woutg, h, name, hosted=()):
    T, W = a.shape
    D = h.shape[1]
    wout = woutg.reshape(2, W, D)
    tm = min(T, 512)

    def body(a_ref, b_ref, w_ref, h_ref, o_ref):
        o_ref[...] = h_ref[...] + _dot(a_ref[...], w_ref[0]) + _dot(b_ref[...], w_ref[1])

    return _call(
        body, hosted, name=name, grid=(T // tm,),
        in_specs=[pl.BlockSpec((tm, W), lambda i: (i, 0)), pl.BlockSpec((tm, W), lambda i: (i, 0)),
                  pl.BlockSpec((2, W, D), lambda i: (0, 0, 0)), pl.BlockSpec((tm, D), lambda i: (i, 0))],
        out_specs=[pl.BlockSpec((tm, D), lambda i: (i, 0))],
        out_shape=[jax.ShapeDtypeStruct((T, D), F32)],
        args=[a, b, wout, h])


def mix_out_bwd(dh, woutg, a, b, name, hosted=()):
    T, D = dh.shape
    W = a.shape[1]
    nsh, Rs, _ = woutg.shape
    wout = woutg.reshape(2, W, D)
    tk = min(T, 512)
    nk = T // tk

    def body(dh_ref, w_ref, a_ref, b_ref, da_ref, db_ref, dw_ref, acc):
        k = pl.program_id(0)

        @pl.when(k == 0)
        def _():
            acc[...] = jnp.zeros_like(acc)

        dhb = dh_ref[...].astype(BF16)
        da_ref[...] = _dot_nt(dhb, w_ref[0])
        db_ref[...] = _dot_nt(dhb, w_ref[1])
        acc[0:W, :] += _dot_tn(a_ref[...], dhb)
        acc[W:2 * W, :] += _dot_tn(b_ref[...], dhb)

        @pl.when(k == nk - 1)
        def _():
            for j in range(nsh):
                dw_ref[j] = acc[j * Rs:(j + 1) * Rs, :].astype(BF16)

    return _call(
        body, hosted, name=name, grid=(nk,),
        in_specs=[pl.BlockSpec((tk, D), lambda k: (k, 0)), pl.BlockSpec((2, W, D), lambda k: (0, 0, 0)),
                  pl.BlockSpec((tk, W), lambda k: (k, 0)), pl.BlockSpec((tk, W), lambda k: (k, 0))],
        out_specs=[pl.BlockSpec((tk, W), lambda k: (k, 0)), pl.BlockSpec((tk, W), lambda k: (k, 0)),
                   pl.BlockSpec((nsh, Rs, D), lambda k: (0, 0, 0))],
        out_shape=[jax.ShapeDtypeStruct((T, W), F32), jax.ShapeDtypeStruct((T, W), F32),
                   jax.ShapeDtypeStruct((nsh, Rs, D), BF16)],
        scratch_shapes=[pltpu.VMEM((2 * W, D), F32)],
        args=[dh, wout, a, b])


def mix_dwin(u, d, nsh, name, hosted=()):
    T, D = u.shape
    Cs = d.shape[1] // nsh
    tk = min(T, 512)
    nk = T // tk

    def body(u_ref, d_ref, o_ref, acc):
        k = pl.program_id(1)

        @pl.when(k == 0)
        def _():
            acc[...] = jnp.zeros_like(acc)

        acc[...] += _dot_tn(u_ref[...], d_ref[...])

        @pl.when(k == nk - 1)
        def _():
            o_ref[0] = acc[...].astype(BF16)

    return _call(
        body, hosted, name=name, grid=(nsh, nk),
        in_specs=[pl.BlockSpec((tk, D), lambda j, k: (k, 0)), pl.BlockSpec((tk, Cs), lambda j, k: (k, j))],
        out_specs=[pl.BlockSpec((1, D, Cs), lambda j, k: (j, 0, 0))],
        out_shape=[jax.ShapeDtypeStruct((nsh, D, Cs), BF16)],
        scratch_shapes=[pltpu.VMEM((D, Cs), F32)],
        args=[u, d])


def mix_in_bwd(d, wing, h, gain, dh, name, hosted=()):
    T, D = h.shape
    nsh, _, Cs = wing.shape
    tm = min(T, 512)

    def body(d_ref, w_ref, h_ref, g_ref, dh_ref, o_ref, dg_ref):
        du = _dot_nt(d_ref[:, 0:Cs], w_ref[0])
        for j in range(1, nsh):
            du += _dot_nt(d_ref[:, j * Cs:(j + 1) * Cs], w_ref[j])
        dhn, dg = _rmsnorm_bwd(du, h_ref[...], g_ref[...])
        o_ref[...] = dh_ref[...] + dhn

        @pl.when(pl.program_id(0) == 0)
        def _():
            dg_ref[...] = jnp.zeros_like(dg_ref)

        dg_ref[...] += jnp.sum(dg, axis=0, keepdims=True)

    row_spec = pl.BlockSpec((tm, D), lambda i: (i, 0))
    vec_spec = pl.BlockSpec((1, D), lambda i: (0, 0))
    return _call(
        body, hosted, name=name, grid=(T // tm,),
        in_specs=[pl.BlockSpec((tm, nsh * Cs), lambda i: (i, 0)), pl.BlockSpec((nsh, D, Cs), lambda i: (0, 0, 0)),
                  row_spec, vec_spec, row_spec],
        out_specs=[row_spec, vec_spec],
        out_shape=[jax.ShapeDtypeStruct((T, D), F32), jax.ShapeDtypeStruct((1, D), F32)],
        args=[d, wing, h, gain, dh])


def _pool_window(x, group, T, trailing):
    rows = lax.broadcasted_iota(jnp.int32, x.shape, 0)

    def shifted(z, k):
        if trailing:
            return jnp.where(rows >= k, pltpu.roll(z, k, 0), 0.0)
        return jnp.where(rows < T - k, pltpu.roll(z, T - k, 0), 0.0)

    s2 = x + shifted(x, 1)
    s4 = s2 + shifted(s2, 2)
    s8 = s4 + shifted(s4, 4)
    s16 = s8 + shifted(s8, 8)
    return jnp.where(group == 0, s2, jnp.where(group == 1, s4, jnp.where(group == 2, s8, s16)))


def _pool_count(group, shape):
    rows = lax.broadcasted_iota(jnp.int32, shape, 0)
    w = jnp.where(group == 0, 2, jnp.where(group == 1, 4, jnp.where(group == 2, 8, 16)))
    return jnp.minimum(rows + 1, w).astype(F32)


def pool_fwd(proj, pool_w, pool_scale, name, hosted=()):
    T = proj.shape[0]
    Hd = HEAD_DIM

    def body(x_ref, w_ref, sc_ref, a_ref):
        g = pl.program_id(0)
        x = x_ref[...]
        pooled = _pool_window(x, g, T, True) / _pool_count(g, x.shape) - x
        a_ref[...] = (_dot(pooled.astype(BF16), w_ref[0].astype(BF16)) * sc_ref[...]).astype(BF16)

    return _call(
        body, hosted, name=name, grid=(N_GROUPS,),
        in_specs=[pl.BlockSpec((T, Hd), lambda g: (0, g)), pl.BlockSpec((1, Hd, Hd), lambda g: (g, 0, 0)),
                  pl.BlockSpec((1, Hd), lambda g: (0, g))],
        out_specs=[pl.BlockSpec((T, Hd), lambda g: (0, g))],
        out_shape=[jax.ShapeDtypeStruct((T, N_GROUPS * Hd), BF16)],
        args=[proj, pool_w, pool_scale])


def pool_bwd(proj, da, pool_w, pool_scale, name, hosted=()):
    T = proj.shape[0]
    Hd = HEAD_DIM

    def body(x_ref, da_ref, w_ref, sc_ref, dx_ref, dw_ref, dsc_ref):
        g = pl.program_id(0)
        x = x_ref[...]
        cnt = _pool_count(g, x.shape)
        pooled = (_pool_window(x, g, T, True) / cnt - x).astype(BF16)
        wb = w_ref[0].astype(BF16)
        dav = da_ref[...]
        dsc_ref[...] = jnp.sum(dav * _dot(pooled, wb), axis=0, keepdims=True)
        dout = (dav * sc_ref[...]).astype(BF16)
        dw_ref[0] = _dot_tn(pooled, dout)
        dpooled = _dot_nt(dout, wb)
        dx_ref[...] = (_pool_window(dpooled / cnt, g, T, False) - dpooled).astype(BF16)

    col_spec = pl.BlockSpec((T, Hd), lambda g: (0, g))
    return _call(
        body, hosted, name=name, grid=(N_GROUPS,),
        in_specs=[col_spec, col_spec, pl.BlockSpec((1, Hd, Hd), lambda g: (g, 0, 0)), pl.BlockSpec((1, Hd), lambda g: (0, g))],
        out_specs=[col_spec, pl.BlockSpec((1, Hd, Hd), lambda g: (g, 0, 0)), pl.BlockSpec((1, Hd), lambda g: (0, g))],
        out_shape=[jax.ShapeDtypeStruct((T, N_GROUPS * Hd), BF16), jax.ShapeDtypeStruct((N_GROUPS, Hd, Hd), F32),
                   jax.ShapeDtypeStruct((1, N_GROUPS * Hd), F32)],
        args=[proj, da, pool_w, pool_scale])


def _ret_tables(T):
    Hd, C = HEAD_DIM, RET_CHUNK
    inv_freq = 1.0 / (ROPE_BASE ** (jnp.arange(0, Hd, 2, dtype=F32) / Hd))
    ang = jnp.arange(T, dtype=F32)[:, None] * inv_freq[None, :]
    cos, sin = jnp.cos(ang), jnp.sin(ang)
    cos2 = jnp.concatenate([cos, cos], axis=-1)
    sin2 = jnp.concatenate([-sin, sin], axis=-1)
    log_gamma = jnp.log1p(-jnp.exp2(-5.0 - jnp.arange(N_GROUPS, dtype=F32)))
    pos = jnp.arange(C, dtype=F32)
    rel = pos[:, None] - pos[None, :]
    intra = jnp.where(rel[None] >= 0, jnp.exp(log_gamma[:, None, None] * jnp.maximum(rel, 0.0)[None]), 0.0)
    k_tail = jnp.exp(log_gamma[:, None] * (C - 1 - pos)[None, :])
    q_head = jnp.exp(log_gamma[:, None] * (pos + 1.0)[None, :])
    chunk_decay = jnp.exp(log_gamma * C)
    wide = lambda t: jnp.broadcast_to(t[:, :, None], (N_GROUPS, C, Hd))
    return cos2, sin2, intra, wide(k_tail), wide(q_head), jnp.broadcast_to(chunk_decay[:, None, None], (N_GROUPS, 1, Hd))


def _rope(x, cos2, sin2):
    return x * cos2 + pltpu.roll(x, HEAD_DIM // 2, 1) * sin2


def _rope_t(d, cos2, sin2):
    return d * cos2 + pltpu.roll(d * sin2, HEAD_DIM // 2, 1)


def _ret_specs(tseg, seg_of):
    Hd, G = HEAD_DIM, N_GROUPS
    col = lambda kind: pl.BlockSpec((tseg, Hd), lambda h, s: (seg_of(s), G * kind + h))
    tab = pl.BlockSpec((tseg, Hd), lambda h, s: (seg_of(s), 0))
    head = pl.BlockSpec((1, RET_CHUNK, Hd), lambda h, s: (h, 0, 0))
    cd = pl.BlockSpec((1, 1, Hd), lambda h, s: (h, 0, 0))
    gain = pl.BlockSpec((1, Hd), lambda h, s: (0, h))
    return col, tab, head, cd, gain


def ret_fwd(proj, ret_norm, tables, name, hosted=()):
    T = proj.shape[0]
    Hd, C, G = HEAD_DIM, RET_CHUNK, N_GROUPS
    tseg = min(T, 1024)
    nseg, nck = T // tseg, tseg // C
    scale = Hd ** -0.5
    cos2, sin2, intra, k_tail, q_head, chunk_decay = tables

    def body(q_ref, k_ref, v_ref, g_ref, gain_ref, cos_ref, sin_ref, m_ref, kt_ref, qh_ref, cd_ref,
             b_ref, o_ref, rp_ref, state):
        @pl.when(pl.program_id(1) == 0)
        def _():
            state[...] = jnp.zeros_like(state)

        def chunk(ci, carry):
            rows = pl.ds(pl.multiple_of(ci * C, C), C)
            cos, sin = cos_ref[rows, :], sin_ref[rows, :]
            qr = _rope(q_ref[rows, :], cos, sin)
            kr = _rope(k_ref[rows, :], cos, sin) * scale
            qb, kb, vb = qr.astype(BF16), kr.astype(BF16), v_ref[rows, :].astype(BF16)
            r = state[...]
            rp_ref[0, ci] = r.astype(BF16)
            sc = _dot_nt(qb, kb) * m_ref[0]
            o = _dot(sc.astype(BF16), vb) + _dot((qr * qh_ref[0]).astype(BF16), r.astype(BF16))
            state[...] = cd_ref[0] * r + _dot_tn((kr * kt_ref[0]).astype(BF16), vb)
            o_ref[rows, :] = o
            on = o * _rstd(o)
            b_ref[rows, :] = (jax.nn.silu(g_ref[rows, :]) * (on * gain_ref[...])).astype(BF16)
            return carry

        lax.fori_loop(0, nck, chunk, 0)

    col, tab, head, cd, gain = _ret_specs(tseg, lambda s: s)
    out_col = pl.BlockSpec((tseg, Hd), lambda h, s: (s, h))
    return _call(
        body, hosted, name=name, grid=(G, nseg),
        in_specs=[col(1), col(2), col(3), col(4), gain, tab, tab, head, head, head, cd],
        out_specs=[out_col, out_col, pl.BlockSpec((1, nck, Hd, Hd), lambda h, s: (h, s, 0, 0))],
        out_shape=[jax.ShapeDtypeStruct((T, G * Hd), BF16), jax.ShapeDtypeStruct((T, G * Hd), F32),
                   jax.ShapeDtypeStruct((G, T // C, Hd, Hd), BF16)],
        scratch_shapes=[pltpu.VMEM((Hd, Hd), F32)],
        args=[proj, proj, proj, proj, ret_norm, cos2, sin2, intra, k_tail, q_head, chunk_decay])


def ret_bwd(proj, db, o_pre, r_prev, ret_norm, tables, name, hosted=()):
    T = proj.shape[0]
    Hd, C, G = HEAD_DIM, RET_CHUNK, N_GROUPS
    tseg = min(T, 1024)
    nseg, nck = T // tseg, tseg // C
    scale = Hd ** -0.5
    cos2, sin2, intra, k_tail, q_head, chunk_decay = tables

    def body(q_ref, k_ref, v_ref, g_ref, db_ref, o_ref, rp_ref, gain_ref, cos_ref, sin_ref, m_ref, kt_ref, qh_ref, cd_ref,
             d_ref, dgain_ref, gstate):
        @pl.when(pl.program_id(1) == 0)
        def _():
            gstate[...] = jnp.zeros_like(gstate)
            dgain_ref[...] = jnp.zeros_like(dgain_ref)

        def chunk(t, carry):
            ci = nck - 1 - t
            rows = pl.ds(pl.multiple_of(ci * C, C), C)
            cos, sin = cos_ref[rows, :], sin_ref[rows, :]
            qr = _rope(q_ref[rows, :], cos, sin)
            kr = _rope(k_ref[rows, :], cos, sin) * scale
            qb, kb, vb = qr.astype(BF16), kr.astype(BF16), v_ref[rows, :].astype(BF16)
            qhb, ktb = (qr * qh_ref[0]).astype(BF16), (kr * kt_ref[0]).astype(BF16)
            sc = (_dot_nt(qb, kb) * m_ref[0]).astype(BF16)
            o = o_ref[rows, :]
            rstd = _rstd(o)
            on = o * rstd
            gain = gain_ref[...]
            silu, dsilu = _silu_parts(g_ref[rows, :])
            dy = db_ref[rows, :]
            dgain_ref[...] += jnp.sum(dy * silu * on, axis=0, keepdims=True)
            dg = dy * on * gain * dsilu
            don = dy * silu * gain
            dob = (rstd * (don - on * jnp.mean(don * on, axis=-1, keepdims=True))).astype(BF16)
            gn = gstate[...]
            gb = gn.astype(BF16)
            da = (_dot_nt(dob, vb) * m_ref[0]).astype(BF16)
            dq = _dot(da, kb) + _dot_nt(dob, rp_ref[0, ci]) * qh_ref[0]
            dk = _dot_tn(da, qb) + _dot_nt(vb, gb) * kt_ref[0]
            dv = _dot_tn(sc, dob) + _dot(ktb, gb)
            gstate[...] = cd_ref[0] * gn + _dot_tn(qhb, dob)
            d_ref[0, rows, :] = _rope_t(dq, cos, sin).astype(BF16)
            d_ref[1, rows, :] = _rope_t(dk * scale, cos, sin).astype(BF16)
            d_ref[2, rows, :] = dv.astype(BF16)
            d_ref[3, rows, :] = dg.astype(BF16)
            return carry

        lax.fori_loop(0, nck, chunk, 0)

    rev = lambda s: nseg - 1 - s
    col, tab, head, cd, gain = _ret_specs(tseg, rev)
    act = pl.BlockSpec((tseg, Hd), lambda h, s: (rev(s), h))
    return _call(
        body, hosted, name=name, grid=(G, nseg),
        in_specs=[col(1), col(2), col(3), col(4), act, act, pl.BlockSpec((1, nck, Hd, Hd), lambda h, s: (h, rev(s), 0, 0)),
                  gain, tab, tab, head, head, head, cd],
        out_specs=[pl.BlockSpec((4, tseg, Hd), lambda h, s: (0, rev(s), h)), gain],
        out_shape=[jax.ShapeDtypeStruct((4, T, G * Hd), BF16), jax.ShapeDtypeStruct((1, G * Hd), F32)],
        scratch_shapes=[pltpu.VMEM((Hd, Hd), F32)],
        args=[proj, proj, proj, proj, db, o_pre, r_prev, ret_norm, cos2, sin2, intra, k_tail, q_head, chunk_decay])


def final_loss(h, gain, target, name, hosted=()):
    T, D = h.shape
    tm = min(T, 512)

    def body(h_ref, g_ref, t_ref, dh_ref, loss_ref, dg_ref):
        @pl.when(pl.program_id(0) == 0)
        def _():
            loss_ref[...] = jnp.zeros_like(loss_ref)
            dg_ref[...] = jnp.zeros_like(dg_ref)

        hh = h_ref[...]
        gain_v = g_ref[...]
        err = hh * _rstd(hh) * gain_v - t_ref[...]
        loss_ref[...] += 0.5 * jnp.sum(jnp.mean(err * err, axis=-1, keepdims=True), axis=0, keepdims=True)
        dhn, dg = _rmsnorm_bwd(err * (1.0 / D), hh, gain_v)
        dh_ref[...] = dhn
        dg_ref[...] += jnp.sum(dg, axis=0, keepdims=True)

    row_spec = pl.BlockSpec((tm, D), lambda i: (i, 0))
    vec_spec = pl.BlockSpec((1, D), lambda i: (0, 0))
    return _call(
        body, hosted, name=name, grid=(T // tm,),
        in_specs=[row_spec, vec_spec, row_spec],
        out_specs=[row_spec, pl.BlockSpec((1, 128), lambda i: (0, 0)), vec_spec],
        out_shape=[jax.ShapeDtypeStruct((T, D), F32), jax.ShapeDtypeStruct((1, 128), F32), jax.ShapeDtypeStruct((1, D), F32)],
        args=[h, gain, target])


def prereduce(grad, recv, core, name):
    nsh, R, C = grad.shape
    rh = R // 2

    def body(c_ref, g_ref, r_ref, o_ref):
        o_ref[...] = (g_ref[...].astype(F32) + r_ref[...].astype(F32)).astype(BF16)

    return pl.pallas_call(
        body, name=name,
        grid_spec=pltpu.PrefetchScalarGridSpec(
            num_scalar_prefetch=1, grid=(nsh,),
            in_specs=[pl.BlockSpec((1, rh, C), lambda j, c_ref: (j, c_ref[0], 0)), pl.BlockSpec((1, rh, C), lambda j, c_ref: (j, 0, 0))],
            out_specs=pl.BlockSpec((1, rh, C), lambda j, c_ref: (j, 0, 0))),
        out_shape=jax.ShapeDtypeStruct((nsh, rh, C), BF16),
        compiler_params=pltpu.CompilerParams(vmem_limit_bytes=VMEM_LIMIT_V7X),
    )(core, grad, recv)


def _adamw(w, g, m, v):
    m = ADAM_B1 * m + (1.0 - ADAM_B1) * g
    v = ADAM_B2 * v + (1.0 - ADAM_B2) * (g * g)
    m_hat = m / (1.0 - ADAM_B1 ** ADAM_STEP)
    v_hat = v / (1.0 - ADAM_B2 ** ADAM_STEP)
    return -ADAM_LR * (m_hat / (jnp.sqrt(v_hat) + ADAM_EPS) + ADAM_WD * w), m, v


def adamw_sharded(tensors, name, hosted=()):
    nt = len(tensors)
    nsh, R, C = tensors[0][0].shape
    tr = 256 if R % 256 == 0 else R // 2

    def body(*refs):
        ins, outs = refs[:4 * nt], refs[4 * nt:]
        for t in range(nt):
            p_ref, w_ref, m_ref, v_ref = ins[4 * t:4 * t + 4]
            g_ref, d_ref, nm_ref, nv_ref = outs[4 * t:4 * t + 4]
            g = p_ref[0].astype(F32)
            for i in range(1, nsh):
                g += p_ref[i].astype(F32)
            g_ref[...] = g
            d_ref[...], nm_ref[...], nv_ref[...] = _adamw(w_ref[...], g, m_ref[...], v_ref[...])

    spec = pl.BlockSpec((tr, C), lambda i: (i, 0))
    out = jax.ShapeDtypeStruct((R, C), F32)
    return _call(
        body, hosted, name=name, grid=(R // tr,),
        in_specs=[pl.BlockSpec((nsh, tr, C), lambda i: (0, i, 0)), spec, spec, spec] * nt,
        out_specs=[spec] * (4 * nt), out_shape=[out] * (4 * nt),
        args=[a for tensor in tensors for a in tensor])


def adamw_small(packs, w, m, v, name):
    ndev, R, L = packs.shape

    def body(p_ref, w_ref, m_ref, v_ref, g_ref, d_ref, nm_ref, nv_ref):
        g = p_ref[0]
        for i in range(1, ndev):
            g += p_ref[i]
        g_ref[...] = g
        d_ref[...], nm_ref[...], nv_ref[...] = _adamw(w_ref[...], g, m_ref[...], v_ref[...])

    out = jax.ShapeDtypeStruct((R, L), F32)
    return pl.pallas_call(body, name=name, out_shape=[out] * 4,
                          compiler_params=pltpu.CompilerParams(vmem_limit_bytes=VMEM_LIMIT_V7X))(packs, w, m, v)


BIG = ("ffn1_w1", "ffn1_w3", "ffn1_w2", "w_in", "w_out", "ffn2_w1", "ffn2_w3", "ffn2_w2")
SMALL = ("ffn1_norm", "mix_norm", "pool_w", "pool_scale", "ret_norm", "ffn2_norm", "final_norm")
WEIGHTS = ("ffn1_norm", "ffn1_w1", "ffn1_w3", "ffn1_w2", "mix_norm", "w_in", "pool_w", "pool_scale", "ret_norm", "w_out",
           "ffn2_norm", "ffn2_w1", "ffn2_w3", "ffn2_w2", "final_norm")


def _pack(parts):
    return jnp.concatenate([parts[k].reshape(-1, 128) for k in SMALL], axis=0)


def _unpack(pack, like):
    out, row = {}, 0
    for k in SMALL:
        rows = like[k].size // 128
        out[k] = pack[row:row + rows].reshape(like[k].shape)
        row += rows
    return out


def kernel(x, ffn1_norm, ffn1_w1, ffn1_w3, ffn1_w2, mix_norm, w_in, pool_w, pool_scale, ret_norm, w_out, ffn2_norm, ffn2_w1, ffn2_w3, ffn2_w2, final_norm, loss_target, m_ffn1_norm, m_ffn1_w1, m_ffn1_w3, m_ffn1_w2, m_mix_norm, m_w_in, m_pool_w, m_pool_scale, m_ret_norm, m_w_out, m_ffn2_norm, m_ffn2_w1, m_ffn2_w3, m_ffn2_w2, m_final_norm, v_ffn1_norm, v_ffn1_w1, v_ffn1_w3, v_ffn1_w2, v_mix_norm, v_w_in, v_pool_w, v_pool_scale, v_ret_norm, v_w_out, v_ffn2_norm, v_ffn2_w1, v_ffn2_w3, v_ffn2_w2, v_final_norm):
    w = dict(ffn1_norm=ffn1_norm, ffn1_w1=ffn1_w1, ffn1_w3=ffn1_w3, ffn1_w2=ffn1_w2, mix_norm=mix_norm, w_in=w_in, pool_w=pool_w,
             pool_scale=pool_scale, ret_norm=ret_norm, w_out=w_out, ffn2_norm=ffn2_norm, ffn2_w1=ffn2_w1, ffn2_w3=ffn2_w3,
             ffn2_w2=ffn2_w2, final_norm=final_norm)
    m = dict(ffn1_norm=m_ffn1_norm, ffn1_w1=m_ffn1_w1, ffn1_w3=m_ffn1_w3, ffn1_w2=m_ffn1_w2, mix_norm=m_mix_norm, w_in=m_w_in,
             pool_w=m_pool_w, pool_scale=m_pool_scale, ret_norm=m_ret_norm, w_out=m_w_out, ffn2_norm=m_ffn2_norm, ffn2_w1=m_ffn2_w1,
             ffn2_w3=m_ffn2_w3, ffn2_w2=m_ffn2_w2, final_norm=m_final_norm)
    v = dict(ffn1_norm=v_ffn1_norm, ffn1_w1=v_ffn1_w1, ffn1_w3=v_ffn1_w3, ffn1_w2=v_ffn1_w2, mix_norm=v_mix_norm, w_in=v_w_in,
             pool_w=v_pool_w, pool_scale=v_pool_scale, ret_norm=v_ret_norm, w_out=v_w_out, ffn2_norm=v_ffn2_norm, ffn2_w1=v_ffn2_w1,
             ffn2_w3=v_ffn2_w3, ffn2_w2=v_ffn2_w2, final_norm=v_final_norm)
    xs, target = x[0], loss_target[0]
    T = xs.shape[0]
    tables = _ret_tables(T)
    core = lax.axis_index("c").astype(jnp.int32).reshape(1)
    sh = {k: w[k][0].astype(BF16) for k in BIG}
    gather = lambda *names: [ChipExchange([sh[k] for k in names], False)]
    wg, grad, delta, new_m, new_v = {}, {}, {}, {}, {}

    def update(names, pieces, name, hosted=()):
        outs, extras = adamw_sharded([(p, w[k][0], m[k][0], v[k][0]) for k, p in zip(names, pieces)], name, hosted)
        for t, k in enumerate(names):
            grad[k], delta[k], new_m[k], new_v[k] = [o[None] for o in outs[4 * t:4 * t + 4]]
        return extras

    def reduce_in_chip(name, partial, recv):
        return prereduce(partial, recv, core, "prereduce_" + name)

    (wg["ffn1_w1"], wg["ffn1_w3"]), = exchange(gather("ffn1_w1", "ffn1_w3"), "gather_ffn1")
    (n1, a1, b1, s1), ((wg["ffn1_w2"], wg["w_in"]),) = ffn_up(
        xs, ffn1_norm, wg["ffn1_w1"], wg["ffn1_w3"], "ffn1_up", gather("ffn1_w2", "w_in"))
    (h1,), ((wg["w_out"],),) = ffn_down(s1, wg["ffn1_w2"], xs, "ffn1_down", gather("w_out"))
    (u, proj), ((wg["ffn2_w1"],),) = mix_in(h1, mix_norm, wg["w_in"], "mix_in", gather("ffn2_w1"))
    (pa,), _ = pool_fwd(proj, pool_w[0], pool_scale, "pool_fwd")
    (rb, o_pre, r_prev), ((wg["ffn2_w3"],),) = ret_fwd(proj, ret_norm, tables, "ret_fwd", gather("ffn2_w3"))
    (h2,), _ = mix_out(pa, rb, wg["w_out"], h1, "mix_out")
    (n2, a2, b2, s2), ((wg["ffn2_w2"],),) = ffn_up(
        h2, ffn2_norm, wg["ffn2_w1"], wg["ffn2_w3"], "ffn2_up", gather("ffn2_w2"))
    (h3,), _ = ffn_down(s2, wg["ffn2_w2"], h2, "ffn2_down")
    (dh3, loss, d_final), _ = final_loss(h3, final_norm[None], target, "final_loss")
    loss = lax.psum(loss[0, 0], ("x", "y", "c"))

    (da2, db2), _ = ffn_bwd_act(dh3, wg["ffn2_w2"], a2, b2, "ffn2_bwd_act")
    (g_f2w2,), _ = ffn_dw2(s2, dh3, "ffn2_dw2")
    (g_f2w1, g_f2w3), ((r_f2w2,),) = ffn_dw13(n2, da2, db2, "ffn2_dw13", [SiblingExchange([g_f2w2])])
    p_f2w2 = reduce_in_chip("ffn2_w2", g_f2w2, r_f2w2)
    (dh2, d_ffn2), ((q_f2w2,), (r_f2w1, r_f2w3)) = ffn_bwd_in(
        da2, db2, wg["ffn2_w1"], wg["ffn2_w3"], h2, ffn2_norm, dh3, "ffn2_bwd_in",
        [ChipExchange([p_f2w2], True), SiblingExchange([g_f2w1, g_f2w3])])
    p_f2w1 = reduce_in_chip("ffn2_w1", g_f2w1, r_f2w1)
    p_f2w3 = reduce_in_chip("ffn2_w3", g_f2w3, r_f2w3)
    (dpa, drb, g_wout), ((q_f2w1,),) = mix_out_bwd(dh2, wg["w_out"], pa, rb, "mix_out_bwd", [ChipExchange([p_f2w1], True)])
    (dpool, d_pool_w, d_pool_scale), _ = pool_bwd(proj, dpa, pool_w[0], pool_scale, "pool_bwd")
    (dqkvg, d_ret_norm), ((q_f2w3,), (r_wout,)) = ret_bwd(
        proj, drb, o_pre, r_prev, ret_norm, tables, "ret_bwd", [ChipExchange([p_f2w3], True), SiblingExchange([g_wout])])
    p_wout = reduce_in_chip("w_out", g_wout, r_wout)
    d = jnp.concatenate([dpool, dqkvg[0], dqkvg[1], dqkvg[2], dqkvg[3]], axis=1)
    (g_win,), ((q_wout,),) = mix_dwin(u, d, N_CHIPS, "mix_dwin", [ChipExchange([p_wout], True)])
    (dh1, d_mix), ((r_win,),) = mix_in_bwd(d, wg["w_in"], h1, mix_norm, dh2, "mix_in_bwd", [SiblingExchange([g_win])])
    p_win = reduce_in_chip("w_in", g_win, r_win)
    (da1, db1), ((q_win,),) = ffn_bwd_act(dh1, wg["ffn1_w2"], a1, b1, "ffn1_bwd_act", [ChipExchange([p_win], True)])
    (g_f1w1, g_f1w3), _ = ffn_dw13(n1, da1, db1, "ffn1_dw13")
    (g_f1w2,), ((r_f1w1, r_f1w3),) = ffn_dw2(s1, dh1, "ffn1_dw2", [SiblingExchange([g_f1w1, g_f1w3])])
    p_f1w1 = reduce_in_chip("ffn1_w1", g_f1w1, r_f1w1)
    p_f1w3 = reduce_in_chip("ffn1_w3", g_f1w3, r_f1w3)
    (dx, d_ffn1), ((q_f1w1, q_f1w3), (r_f1w2,)) = ffn_bwd_in(
        da1, db1, wg["ffn1_w1"], wg["ffn1_w3"], xs, ffn1_norm, dh1, "ffn1_bwd_in",
        [ChipExchange([p_f1w1, p_f1w3], True), SiblingExchange([g_f1w2])])
    p_f1w2 = reduce_in_chip("ffn1_w2", g_f1w2, r_f1w2)

    (q_f1w2,), = update(["ffn2_w1", "ffn2_w3"], [q_f2w1, q_f2w3], "adamw_ffn2_w13", [ChipExchange([p_f1w2], True)])
    update(["ffn2_w2"], [q_f2w2], "adamw_ffn2_w2")
    update(["w_in"], [q_win], "adamw_w_in")
    update(["w_out"], [q_wout], "adamw_w_out")
    update(["ffn1_w1", "ffn1_w3"], [q_f1w1, q_f1w3], "adamw_ffn1_w13")
    update(["ffn1_w2"], [q_f1w2], "adamw_ffn1_w2")

    small = {"ffn1_norm": d_ffn1, "mix_norm": d_mix, "pool_w": d_pool_w, "pool_scale": d_pool_scale,
             "ret_norm": d_ret_norm, "ffn2_norm": d_ffn2, "final_norm": d_final}
    packs = all_exchange_small(_pack(small), "gather_small")
    outs = adamw_small(packs, _pack(w), _pack(m), _pack(v), "adamw_small")
    for res, pack in zip((grad, delta, new_m, new_v), outs):
        res.update(_unpack(pack, w))

    return (loss, dx[None], *[grad[k] for k in WEIGHTS], *[delta[k] for k in WEIGHTS],
            *[new_m[k] for k in WEIGHTS], *[new_v[k] for k in WEIGHTS])
```

```python
import math

import jax
import jax.numpy as jnp
from jax import lax
from jax.experimental import pallas as pl
from jax.experimental.pallas import tpu as pltpu

F32 = jnp.float32
BF16 = jnp.bfloat16

EPS = 1e-6
N_CHIPS = 4
N_GROUPS = 4
HEAD_DIM = 128
RET_CHUNK = 128
ROPE_BASE = 10000.0
ADAM_LR, ADAM_B1, ADAM_B2, ADAM_EPS, ADAM_WD, ADAM_STEP = 0.001, 0.9, 0.999, 1e-08, 0.01, 10
VMEM_LIMIT_V7X = 56 * 1024 * 1024
MESH = pl.DeviceIdType.MESH
ANY = pl.BlockSpec(memory_space=pl.ANY)


def _dot(a, b):
    return jnp.dot(a, b, preferred_element_type=F32)


def _dot_nt(a, b):
    return lax.dot_general(a, b, (((1,), (1,)), ((), ())), preferred_element_type=F32)


def _dot_tn(a, b):
    return lax.dot_general(a, b, (((0,), (0,)), ((), ())), preferred_element_type=F32)


def _rstd(h):
    return lax.rsqrt(jnp.mean(h * h, axis=-1, keepdims=True) + EPS)


def _rmsnorm_bwd(dn, h, gain):
    r = _rstd(h)
    nh = h * r
    dnh = dn * gain
    dh = r * (dnh - nh * jnp.mean(dnh * nh, axis=-1, keepdims=True))
    return dh, dn * nh


def _silu_parts(a):
    sig = jax.nn.sigmoid(a)
    silu = a * sig
    return silu, sig + silu * (1.0 - sig)


def _mesh_pos():
    return lax.axis_index("x"), lax.axis_index("y"), lax.axis_index("c")


class ChipExchange:
    def __init__(self, srcs, scatter, north_only=False):
        assert not (scatter and north_only)
        n = len(srcs)
        self.srcs, self.scatter, self.north_only, self.n = list(srcs), scatter, north_only, n
        self.half_rows = [s.shape[1] if scatter else s.shape[0] // (1 if north_only else 2) for s in srcs]
        self.out_shape = [jax.ShapeDtypeStruct((N_CHIPS, (1 if north_only else 2) * rh, s.shape[-1]), s.dtype)
                          for s, rh in zip(srcs, self.half_rows)]
        if scatter:
            self.out_shape += [jax.ShapeDtypeStruct((2, rh // 2, s.shape[-1]), s.dtype) for s, rh in zip(srcs, self.half_rows)]
        dma = pltpu.SemaphoreType.DMA
        self.sems = [dma((n,)), dma((4 * n,)), dma((4 * n,)), dma((2 * n,)), dma((2 * n,)), dma((4 * n,)), dma((4 * n,))]

    def _copies(self, src, out, sems):
        local_sem, hop1_send, hop1_recv, hop2_send, hop2_recv, d2d_send, d2d_recv = sems
        x, y, c = _mesh_pos()
        me, dg = 2 * x + y, 2 * (1 - x) + (1 - y)
        sibling = (x, y, 1 - c)
        n = self.n
        mine, theirs = (0, 0) if self.north_only else (c, 1 - c)

        def nb(a):
            nx, ny = x ^ (1 - a), y ^ a
            return 2 * nx + ny, (nx, ny, c)

        def remote(s, d, send, recv, k, to):
            return pltpu.make_async_remote_copy(src_ref=s, dst_ref=d, send_sem=send.at[k], recv_sem=recv.at[k],
                                                device_id=to, device_id_type=MESH)

        class Copies:
            def slot(_, t, chip, unit):
                rh = self.half_rows[t]
                return out[t].at[chip, pl.ds(unit * rh, rh), :]

            def quarter(_, t, chip, q):
                qh = self.half_rows[t] // 2
                return out[t].at[chip, pl.ds(mine * 2 * qh + q * qh, qh), :]

            def local(k, t):
                if self.scatter:
                    return pltpu.make_async_copy(src[t].at[me], k.slot(t, me, mine), local_sem.at[t])
                return pltpu.make_async_copy(src[t], out[t].at[me], local_sem.at[t])

            def hop1(k, t, a, transit=False):
                rh = self.half_rows[t]
                chip, to = nb(a)
                if transit:
                    piece = src[t].at[dg, pl.ds(a * (rh // 2), rh // 2), :]
                    return remote(piece, out[n + t].at[a], hop1_send, hop1_recv, 4 * t + 2 + a, to)
                piece = src[t].at[chip] if self.scatter else src[t].at[pl.ds(mine * rh, rh), :]
                return remote(piece, k.slot(t, me, mine), hop1_send, hop1_recv, 4 * t + a, to)

            def landed1(k, t, a, transit=False):
                here = out[n + t].at[a] if transit else k.slot(t, nb(a)[0], mine)
                return remote(here, here, hop1_send, hop1_recv, 4 * t + (2 if transit else 0) + a, sibling)

            def hop2(k, t, q):
                origin, to = nb(q)[0], nb(1 - q)[1]
                piece = out[n + t].at[q] if self.scatter else k.quarter(t, origin, q)
                return remote(piece, k.quarter(t, origin, q), hop2_send, hop2_recv, 2 * t + q, to)

            def landed2(k, t, q):
                here = k.quarter(t, dg, q)
                return remote(here, here, hop2_send, hop2_recv, 2 * t + q, sibling)

            def d2d(k, t, p, chip, own=False, arriving=False):
                if arriving:
                    there = k.slot(t, chip, theirs)
                    return remote(there, there, d2d_send, d2d_recv, 4 * t + p, sibling)
                piece = src[t].at[me] if own else k.slot(t, chip, mine)
                return remote(piece, k.slot(t, chip, mine), d2d_send, d2d_recv, 4 * t + p, sibling)

        return Copies(), nb, me, dg, c

    def _by(self, sender, c, fn):
        if self.north_only:
            pl.when(c == (1 if sender else 0))(fn)
        else:
            fn()

    def start(self, src, out, sems):
        k, nb, me, dg, c = self._copies(src, out, sems)
        for t in range(self.n):
            k.local(t).start()

        def send():
            for t in range(self.n):
                for first in range(2):
                    a = first ^ c
                    k.hop1(t, a).start()
                    if self.scatter:
                        k.hop1(t, a, transit=True).start()
                if self.scatter:
                    k.d2d(t, 3, me, own=True).start()

        self._by(True, c, send)

    def mid(self, src, out, sems):
        k, nb, me, dg, c = self._copies(src, out, sems)

        def forward():
            for t in range(self.n):
                for first in range(2):
                    a = first ^ c
                    if self.scatter:
                        k.landed1(t, a, transit=True).wait_recv()
                        k.hop2(t, a).start()
                    k.landed1(t, a).wait_recv()
                    if not self.scatter:
                        k.hop2(t, a).start()
                    k.d2d(t, a, nb(a)[0]).start()

        self._by(True, c, forward)

    def finish(self, src, out, sems):
        k, nb, me, dg, c = self._copies(src, out, sems)

        def last_forward():
            for t in range(self.n):
                for q in range(2):
                    k.landed2(t, q).wait_recv()
                k.d2d(t, 2, dg).start()

        def arrivals():
            for t in range(self.n):
                for a in range(2):
                    k.d2d(t, a, nb(a)[0], arriving=True).wait_recv()
                k.d2d(t, 2, dg, arriving=True).wait_recv()
                if self.scatter:
                    k.d2d(t, 3, me, arriving=True).wait_recv()

        def sent():
            for t in range(self.n):
                for a in range(2):
                    k.hop1(t, a).wait_send()
                    if self.scatter:
                        k.hop1(t, a, transit=True).wait_send()
                    k.hop2(t, a).wait_send()
                    k.d2d(t, a, nb(a)[0]).wait_send()
                k.d2d(t, 2, dg).wait_send()
                if self.scatter:
                    k.d2d(t, 3, me, own=True).wait_send()

        self._by(True, c, last_forward)
        self._by(False, c, arrivals)
        self._by(True, c, sent)
        for t in range(self.n):
            k.local(t).wait()


class SiblingExchange:
    def __init__(self, grads):
        self.srcs, self.n = list(grads), len(grads)
        self.half_rows = [g.shape[1] // 2 for g in grads]
        self.out_shape = [jax.ShapeDtypeStruct((g.shape[0], rh, g.shape[2]), g.dtype) for g, rh in zip(grads, self.half_rows)]
        self.sems = [pltpu.SemaphoreType.DMA((self.n,)), pltpu.SemaphoreType.DMA((self.n,))]

    def _plan(self, src, out, sems):
        x, y, c = _mesh_pos()
        return [pltpu.make_async_remote_copy(
            src_ref=src[t].at[:, pl.ds((1 - c) * self.half_rows[t], self.half_rows[t]), :], dst_ref=out[t],
            send_sem=sems[0].at[t], recv_sem=sems[1].at[t], device_id=(x, y, 1 - c), device_id_type=MESH) for t in range(self.n)]

    def start(self, src, out, sems):
        for cp in self._plan(src, out, sems):
            cp.start()

    def mid(self, src, out, sems):
        pass

    def finish(self, src, out, sems):
        for cp in self._plan(src, out, sems):
            cp.wait()


def _call(body, hosted=(), *, name, in_specs, out_specs, out_shape, args, grid=(), scratch_shapes=()):
    n_in, n_out, n_scr = len(in_specs), len(out_specs), len(scratch_shapes)
    total = math.prod(grid)
    mid_step = max(0, (5 * total) // 8 - 1)

    def full(*refs):
        pos = [0]

        def take(k):
            pos[0] += k
            return refs[pos[0] - k:pos[0]]

        ins, h_in = take(n_in), [take(h.n) for h in hosted]
        outs, h_out = take(n_out), [take(len(h.out_shape)) for h in hosted]
        scr, h_sem = take(n_scr), [take(len(h.sems)) for h in hosted]
        step = 0
        for axis, size in enumerate(grid):
            step = step * size + pl.program_id(axis)

        def phase(at, method):
            if not hosted:
                return
            if total == 1:
                for h, s, o, m in zip(hosted, h_in, h_out, h_sem):
                    getattr(h, method)(s, o, m)
                return

            @pl.when(step == at)
            def _():
                for h, s, o, m in zip(hosted, h_in, h_out, h_sem):
                    getattr(h, method)(s, o, m)

        phase(0, "start")
        body(*ins, *outs, *scr)
        phase(mid_step, "mid")
        phase(total - 1, "finish")

    results = pl.pallas_call(
        full, name=name, grid=grid,
        in_specs=list(in_specs) + [ANY] * sum(h.n for h in hosted),
        out_specs=list(out_specs) + [ANY] * sum(len(h.out_shape) for h in hosted),
        out_shape=list(out_shape) + [s for h in hosted for s in h.out_shape],
        scratch_shapes=list(scratch_shapes) + [s for h in hosted for s in h.sems],
        compiler_params=pltpu.CompilerParams(vmem_limit_bytes=VMEM_LIMIT_V7X),
    )(*args, *[s for h in hosted for s in h.srcs])
    outs, extras, pos = list(results[:n_out]), [], n_out
    for h in hosted:
        extras.append(list(results[pos:pos + h.n]))
        pos += len(h.out_shape)
    return outs, extras


def exchange(hosted, name):
    return _call(lambda: None, hosted, name=name, in_specs=[], out_specs=[], out_shape=[], args=[])[1]


def all_exchange_small(pack, name):
    R, L = pack.shape
    flips = [(dx, dy, dc) for dx in (0, 1) for dy in (0, 1) for dc in (0, 1)][1:]

    def body(src, out, local_sem, send_sem, recv_sem):
        x, y, c = _mesh_pos()
        me = 4 * x + 2 * y + c
        local = pltpu.make_async_copy(src, out.at[me], local_sem)
        local.start()
        copies = []
        for k, (dx, dy, dc) in enumerate(flips):
            copies.append(pltpu.make_async_remote_copy(
                src_ref=src, dst_ref=out.at[me], send_sem=send_sem.at[k], recv_sem=recv_sem.at[k],
                device_id=(x ^ dx, y ^ dy, c ^ dc), device_id_type=MESH))
        for cp in copies:
            cp.start()
        for k, (dx, dy, dc) in enumerate(flips):
            landed = out.at[4 * (x ^ dx) + 2 * (y ^ dy) + (c ^ dc)]
            pltpu.make_async_remote_copy(src_ref=landed, dst_ref=landed, send_sem=send_sem.at[k], recv_sem=recv_sem.at[k],
                                         device_id=(x ^ dx, y ^ dy, c ^ dc), device_id_type=MESH).wait_recv()
        for cp in copies:
            cp.wait_send()
        local.wait()

    return pl.pallas_call(
        body, name=name, in_specs=[ANY], out_specs=ANY,
        out_shape=jax.ShapeDtypeStruct((2 * N_CHIPS, R, L), pack.dtype),
        scratch_shapes=[pltpu.SemaphoreType.DMA, pltpu.SemaphoreType.DMA((7,)), pltpu.SemaphoreType.DMA((7,))],
    )(pack)


def ffn_up(h, gain, w1g, w3g, name, hosted=()):
    T, D = h.shape
    nsh, _, Fs = w1g.shape
    tm = min(T, 1024)

    def body(h_ref, g_ref, w1_ref, w3_ref, n_ref, a_ref, b_ref, s_ref):
        @pl.when(pl.program_id(1) == 0)
        def _():
            hh = h_ref[...]
            n_ref[...] = (hh * _rstd(hh) * g_ref[...]).astype(BF16)

        n = n_ref[...]
        a = _dot(n, w1_ref[0])
        b = _dot(n, w3_ref[0])
        a_ref[0] = a.astype(BF16)
        b_ref[0] = b.astype(BF16)
        s_ref[0] = (a * jax.nn.sigmoid(a) * b).astype(BF16)

    act = jax.ShapeDtypeStruct((nsh, T, Fs), BF16)
    act_spec = pl.BlockSpec((1, tm, Fs), lambda i, j: (j, i, 0))
    w_spec = pl.BlockSpec((1, D, Fs), lambda i, j: (j, 0, 0))
    return _call(
        body, hosted, name=name, grid=(T // tm, nsh),
        in_specs=[pl.BlockSpec((tm, D), lambda i, j: (i, 0)), pl.BlockSpec((1, D), lambda i, j: (0, 0)), w_spec, w_spec],
        out_specs=[pl.BlockSpec((tm, D), lambda i, j: (i, 0)), act_spec, act_spec, act_spec],
        out_shape=[jax.ShapeDtypeStruct((T, D), BF16), act, act, act],
        args=[h, gain, w1g, w3g])


def ffn_down(s, w2g, h, name, hosted=()):
    nsh, T, Fs = s.shape
    D = h.shape[1]
    tm = min(T, 512)

    def body(s_ref, w2_ref, h_ref, o_ref):
        f = _dot(s_ref[0], w2_ref[0])
        for j in range(1, nsh):
            f += _dot(s_ref[j], w2_ref[j])
        o_ref[...] = h_ref[...] + 0.5 * f

    return _call(
        body, hosted, name=name, grid=(T // tm,),
        in_specs=[pl.BlockSpec((nsh, tm, Fs), lambda i: (0, i, 0)), pl.BlockSpec((nsh, Fs, D), lambda i: (0, 0, 0)),
                  pl.BlockSpec((tm, D), lambda i: (i, 0))],
        out_specs=[pl.BlockSpec((tm, D), lambda i: (i, 0))],
        out_shape=[jax.ShapeDtypeStruct((T, D), F32)],
        args=[s, w2g, h])


def ffn_bwd_act(dh, w2g, a, b, name, hosted=()):
    T, D = dh.shape
    nsh, Fs, _ = w2g.shape
    tm = min(T, 1024)

    def body(dh_ref, w2_ref, a_ref, b_ref, da_ref, db_ref):
        df = (0.5 * dh_ref[...]).astype(BF16)
        ds = _dot_nt(df, w2_ref[0])
        silu, dsilu = _silu_parts(a_ref[0].astype(F32))
        da_ref[0] = (ds * b_ref[0].astype(F32) * dsilu).astype(BF16)
        db_ref[0] = (ds * silu).astype(BF16)

    act = jax.ShapeDtypeStruct((nsh, T, Fs), BF16)
    act_spec = pl.BlockSpec((1, tm, Fs), lambda j, i: (j, i, 0))
    return _call(
        body, hosted, name=name, grid=(nsh, T // tm),
        in_specs=[pl.BlockSpec((tm, D), lambda j, i: (i, 0)), pl.BlockSpec((1, Fs, D), lambda j, i: (j, 0, 0)), act_spec, act_spec],
        out_specs=[act_spec, act_spec],
        out_shape=[act, act],
        args=[dh, w2g, a, b])


def ffn_dw2(s, dh, name, hosted=()):
    nsh, T, Fs = s.shape
    D = dh.shape[1]
    tk = min(T, 512)
    nk = T // tk

    def body(s_ref, dh_ref, o_ref, acc):
        k = pl.program_id(1)

        @pl.when(k == 0)
        def _():
            acc[...] = jnp.zeros_like(acc)

        acc[...] += _dot_tn(s_ref[0], (0.5 * dh_ref[...]).astype(BF16))

        @pl.when(k == nk - 1)
        def _():
            o_ref[0] = acc[...].astype(BF16)

    return _call(
        body, hosted, name=name, grid=(nsh, nk),
        in_specs=[pl.BlockSpec((1, tk, Fs), lambda j, k: (j, k, 0)), pl.BlockSpec((tk, D), lambda j, k: (k, 0))],
        out_specs=[pl.BlockSpec((1, Fs, D), lambda j, k: (j, 0, 0))],
        out_shape=[jax.ShapeDtypeStruct((nsh, Fs, D), BF16)],
        scratch_shapes=[pltpu.VMEM((Fs, D), F32)],
        args=[s, dh])


def ffn_dw13(n, da, db, name, hosted=()):
    T, D = n.shape
    nsh, _, Fs = da.shape
    tk = min(T, 512)
    nk = T // tk

    def body(n_ref, da_ref, db_ref, o1_ref, o3_ref, acc1, acc3):
        k = pl.program_id(1)

        @pl.when(k == 0)
        def _():
            acc1[...] = jnp.zeros_like(acc1)
            acc3[...] = jnp.zeros_like(acc3)

        nn = n_ref[...]
        acc1[...] += _dot_tn(nn, da_ref[0])
        acc3[...] += _dot_tn(nn, db_ref[0])

        @pl.when(k == nk - 1)
        def _():
            o1_ref[0] = acc1[...].astype(BF16)
            o3_ref[0] = acc3[...].astype(BF16)

    act_spec = pl.BlockSpec((1, tk, Fs), lambda j, k: (j, k, 0))
    out = jax.ShapeDtypeStruct((nsh, D, Fs), BF16)
    out_spec = pl.BlockSpec((1, D, Fs), lambda j, k: (j, 0, 0))
    return _call(
        body, hosted, name=name, grid=(nsh, nk),
        in_specs=[pl.BlockSpec((tk, D), lambda j, k: (k, 0)), act_spec, act_spec],
        out_specs=[out_spec, out_spec],
        out_shape=[out, out],
        scratch_shapes=[pltpu.VMEM((D, Fs), F32), pltpu.VMEM((D, Fs), F32)],
        args=[n, da, db])


def ffn_bwd_in(da, db, w1g, w3g, h, gain, dh, name, hosted=()):
    nsh, T, Fs = da.shape
    D = h.shape[1]
    tm = min(T, 256)

    def body(da_ref, db_ref, w1_ref, w3_ref, h_ref, g_ref, dh_ref, o_ref, dg_ref):
        dn = _dot_nt(da_ref[0], w1_ref[0]) + _dot_nt(db_ref[0], w3_ref[0])
        for j in range(1, nsh):
            dn += _dot_nt(da_ref[j], w1_ref[j]) + _dot_nt(db_ref[j], w3_ref[j])
        dhn, dg = _rmsnorm_bwd(dn, h_ref[...], g_ref[...])
        o_ref[...] = dh_ref[...] + dhn

        @pl.when(pl.program_id(0) == 0)
        def _():
            dg_ref[...] = jnp.zeros_like(dg_ref)

        dg_ref[...] += jnp.sum(dg, axis=0, keepdims=True)

    act_spec = pl.BlockSpec((nsh, tm, Fs), lambda i: (0, i, 0))
    w_spec = pl.BlockSpec((nsh, D, Fs), lambda i: (0, 0, 0))
    row_spec = pl.BlockSpec((tm, D), lambda i: (i, 0))
    vec_spec = pl.BlockSpec((1, D), lambda i: (0, 0))
    return _call(
        body, hosted, name=name, grid=(T // tm,),
        in_specs=[act_spec, act_spec, w_spec, w_spec, row_spec, vec_spec, row_spec],
        out_specs=[row_spec, vec_spec],
        out_shape=[jax.ShapeDtypeStruct((T, D), F32), jax.ShapeDtypeStruct((1, D), F32)],
        args=[da, db, w1g, w3g, h, gain, dh])


def mix_in(h, gain, wing, name, hosted=()):
    T, D = h.shape
    nsh, _, Cs = wing.shape
    tm = min(T, 512)

    def body(h_ref, g_ref, w_ref, u_ref, p_ref):
        hh = h_ref[...]
        u = (hh * _rstd(hh) * g_ref[...]).astype(BF16)
        u_ref[...] = u
        for j in range(nsh):
            p_ref[:, j * Cs:(j + 1) * Cs] = _dot(u, w_ref[j])

    return _call(
        body, hosted, name=name, grid=(T // tm,),
        in_specs=[pl.BlockSpec((tm, D), lambda i: (i, 0)), pl.BlockSpec((1, D), lambda i: (0, 0)),
                  pl.BlockSpec((nsh, D, Cs), lambda i: (0, 0, 0))],
        out_specs=[pl.BlockSpec((tm, D), lambda i: (i, 0)), pl.BlockSpec((tm, nsh * Cs), lambda i: (i, 0))],
        out_shape=[jax.ShapeDtypeStruct((T, D), BF16), jax.ShapeDtypeStruct((T, nsh * Cs), F32)],
        args=[h, gain, wing])


def mix_out(a, b, woutg, h, name, hosted=()):
    T, W = a.shape
    D = h.shape[1]
    wout = woutg.reshape(2, W, D)
    tm = min(T, 512)

    def body(a_ref, b_ref, w_ref, h_ref, o_ref):
        o_ref[...] = h_ref[...] + _dot(a_ref[...], w_ref[0]) + _dot(b_ref[...], w_ref[1])

    return _call(
        body, hosted, name=name, grid=(T // tm,),
        in_specs=[pl.BlockSpec((tm, W), lambda i: (i, 0)), pl.BlockSpec((tm, W), lambda i: (i, 0)),
                  pl.BlockSpec((2, W, D), lambda i: (0, 0, 0)), pl.BlockSpec((tm, D), lambda i: (i, 0))],
        out_specs=[pl.BlockSpec((tm, D), lambda i: (i, 0))],
        out_shape=[jax.ShapeDtypeStruct((T, D), F32)],
        args=[a, b, wout, h])


def mix_out_bwd(dh, woutg, a, b, name, hosted=()):
    T, D = dh.shape
    W = a.shape[1]
    nsh, Rs, _ = woutg.shape
    wout = woutg.reshape(2, W, D)
    tk = min(T, 512)
    nk = T // tk

    def body(dh_ref, w_ref, a_ref, b_ref, da_ref, db_ref, dw_ref, acc):
        k = pl.program_id(0)

        @pl.when(k == 0)
        def _():
            acc[...] = jnp.zeros_like(acc)

        dhb = dh_ref[...].astype(BF16)
        da_ref[...] = _dot_nt(dhb, w_ref[0])
        db_ref[...] = _dot_nt(dhb, w_ref[1])
        acc[0:W, :] += _dot_tn(a_ref[...], dhb)
        acc[W:2 * W, :] += _dot_tn(b_ref[...], dhb)

        @pl.when(k == nk - 1)
        def _():
            for j in range(nsh):
                dw_ref[j] = acc[j * Rs:(j + 1) * Rs, :].astype(BF16)

    return _call(
        body, hosted, name=name, grid=(nk,),
        in_specs=[pl.BlockSpec((tk, D), lambda k: (k, 0)), pl.BlockSpec((2, W, D), lambda k: (0, 0, 0)),
                  pl.BlockSpec((tk, W), lambda k: (k, 0)), pl.BlockSpec((tk, W), lambda k: (k, 0))],
        out_specs=[pl.BlockSpec((tk, W), lambda k: (k, 0)), pl.BlockSpec((tk, W), lambda k: (k, 0)),
                   pl.BlockSpec((nsh, Rs, D), lambda k: (0, 0, 0))],
        out_shape=[jax.ShapeDtypeStruct((T, W), F32), jax.ShapeDtypeStruct((T, W), F32),
                   jax.ShapeDtypeStruct((nsh, Rs, D), BF16)],
        scratch_shapes=[pltpu.VMEM((2 * W, D), F32)],
        args=[dh, wout, a, b])


def mix_dwin(u, d, nsh, name, hosted=()):
    T, D = u.shape
    Cs = d.shape[1] // nsh
    tk = min(T, 512)
    nk = T // tk

    def body(u_ref, d_ref, o_ref, acc):
        k = pl.program_id(1)

        @pl.when(k == 0)
        def _():
            acc[...] = jnp.zeros_like(acc)

        acc[...] += _dot_tn(u_ref[...], d_ref[...])

        @pl.when(k == nk - 1)
        def _():
            o_ref[0] = acc[...].astype(BF16)

    return _call(
        body, hosted, name=name, grid=(nsh, nk),
        in_specs=[pl.BlockSpec((tk, D), lambda j, k: (k, 0)), pl.BlockSpec((tk, Cs), lambda j, k: (k, j))],
        out_specs=[pl.BlockSpec((1, D, Cs), lambda j, k: (j, 0, 0))],
        out_shape=[jax.ShapeDtypeStruct((nsh, D, Cs), BF16)],
        scratch_shapes=[pltpu.VMEM((D, Cs), F32)],
        args=[u, d])


def mix_in_bwd(d, wing, h, gain, dh, name, hosted=()):
    T, D = h.shape
    nsh, _, Cs = wing.shape
    tm = min(T, 512)

    def body(d_ref, w_ref, h_ref, g_ref, dh_ref, o_ref, dg_ref):
        du = _dot_nt(d_ref[:, 0:Cs], w_ref[0])
        for j in range(1, nsh):
            du += _dot_nt(d_ref[:, j * Cs:(j + 1) * Cs], w_ref[j])
        dhn, dg = _rmsnorm_bwd(du, h_ref[...], g_ref[...])
        o_ref[...] = dh_ref[...] + dhn

        @pl.when(pl.program_id(0) == 0)
        def _():
            dg_ref[...] = jnp.zeros_like(dg_ref)

        dg_ref[...] += jnp.sum(dg, axis=0, keepdims=True)

    row_spec = pl.BlockSpec((tm, D), lambda i: (i, 0))
    vec_spec = pl.BlockSpec((1, D), lambda i: (0, 0))
    return _call(
        body, hosted, name=name, grid=(T // tm,),
        in_specs=[pl.BlockSpec((tm, nsh * Cs), lambda i: (i, 0)), pl.BlockSpec((nsh, D, Cs), lambda i: (0, 0, 0)),
                  row_spec, vec_spec, row_spec],
        out_specs=[row_spec, vec_spec],
        out_shape=[jax.ShapeDtypeStruct((T, D), F32), jax.ShapeDtypeStruct((1, D), F32)],
        args=[d, wing, h, gain, dh])


def _pool_window(x, group, T, trailing):
    rows = lax.broadcasted_iota(jnp.int32, x.shape, 0)

    def shifted(z, k):
        if trailing:
            return jnp.where(rows >= k, pltpu.roll(z, k, 0), 0.0)
        return jnp.where(rows < T - k, pltpu.roll(z, T - k, 0), 0.0)

    s2 = x + shifted(x, 1)
    s4 = s2 + shifted(s2, 2)
    s8 = s4 + shifted(s4, 4)
    s16 = s8 + shifted(s8, 8)
    return jnp.where(group == 0, s2, jnp.where(group == 1, s4, jnp.where(group == 2, s8, s16)))


def _pool_count(group, shape):
    rows = lax.broadcasted_iota(jnp.int32, shape, 0)
    w = jnp.where(group == 0, 2, jnp.where(group == 1, 4, jnp.where(group == 2, 8, 16)))
    return jnp.minimum(rows + 1, w).astype(F32)


def pool_fwd(proj, pool_w, pool_scale, name, hosted=()):
    T = proj.shape[0]
    Hd = HEAD_DIM

    def body(x_ref, w_ref, sc_ref, a_ref):
        g = pl.program_id(0)
        x = x_ref[...]
        pooled = _pool_window(x, g, T, True) / _pool_count(g, x.shape) - x
        a_ref[...] = (_dot(pooled.astype(BF16), w_ref[0].astype(BF16)) * sc_ref[...]).astype(BF16)

    return _call(
        body, hosted, name=name, grid=(N_GROUPS,),
        in_specs=[pl.BlockSpec((T, Hd), lambda g: (0, g)), pl.BlockSpec((1, Hd, Hd), lambda g: (g, 0, 0)),
                  pl.BlockSpec((1, Hd), lambda g: (0, g))],
        out_specs=[pl.BlockSpec((T, Hd), lambda g: (0, g))],
        out_shape=[jax.ShapeDtypeStruct((T, N_GROUPS * Hd), BF16)],
        args=[proj, pool_w, pool_scale])


def pool_bwd(proj, da, pool_w, pool_scale, name, hosted=()):
    T = proj.shape[0]
    Hd = HEAD_DIM

    def body(x_ref, da_ref, w_ref, sc_ref, dx_ref, dw_ref, dsc_ref):
        g = pl.program_id(0)
        x = x_ref[...]
        cnt = _pool_count(g, x.shape)
        pooled = (_pool_window(x, g, T, True) / cnt - x).astype(BF16)
        wb = w_ref[0].astype(BF16)
        dav = da_ref[...]
        dsc_ref[...] = jnp.sum(dav * _dot(pooled, wb), axis=0, keepdims=True)
        dout = (dav * sc_ref[...]).astype(BF16)
        dw_ref[0] = _dot_tn(pooled, dout)
        dpooled = _dot_nt(dout, wb)
        dx_ref[...] = (_pool_window(dpooled / cnt, g, T, False) - dpooled).astype(BF16)

    col_spec = pl.BlockSpec((T, Hd), lambda g: (0, g))
    return _call(
        body, hosted, name=name, grid=(N_GROUPS,),
        in_specs=[col_spec, col_spec, pl.BlockSpec((1, Hd, Hd), lambda g: (g, 0, 0)), pl.BlockSpec((1, Hd), lambda g: (0, g))],
        out_specs=[col_spec, pl.BlockSpec((1, Hd, Hd), lambda g: (g, 0, 0)), pl.BlockSpec((1, Hd), lambda g: (0, g))],
        out_shape=[jax.ShapeDtypeStruct((T, N_GROUPS * Hd), BF16), jax.ShapeDtypeStruct((N_GROUPS, Hd, Hd), F32),
                   jax.ShapeDtypeStruct((1, N_GROUPS * Hd), F32)],
        args=[proj, da, pool_w, pool_scale])


def _ret_tables(T):
    Hd, C = HEAD_DIM, RET_CHUNK
    inv_freq = 1.0 / (ROPE_BASE ** (jnp.arange(0, Hd, 2, dtype=F32) / Hd))
    ang = jnp.arange(T, dtype=F32)[:, None] * inv_freq[None, :]
    cos, sin = jnp.cos(ang), jnp.sin(ang)
    cos2 = jnp.concatenate([cos, cos], axis=-1)
    sin2 = jnp.concatenate([-sin, sin], axis=-1)
    log_gamma = jnp.log1p(-jnp.exp2(-5.0 - jnp.arange(N_GROUPS, dtype=F32)))
    pos = jnp.arange(C, dtype=F32)
    rel = pos[:, None] - pos[None, :]
    intra = jnp.where(rel[None] >= 0, jnp.exp(log_gamma[:, None, None] * jnp.maximum(rel, 0.0)[None]), 0.0)
    k_tail = jnp.exp(log_gamma[:, None] * (C - 1 - pos)[None, :])
    q_head = jnp.exp(log_gamma[:, None] * (pos + 1.0)[None, :])
    chunk_decay = jnp.exp(log_gamma * C)
    wide = lambda t: jnp.broadcast_to(t[:, :, None], (N_GROUPS, C, Hd))
    return cos2, sin2, intra, wide(k_tail), wide(q_head), jnp.broadcast_to(chunk_decay[:, None, None], (N_GROUPS, 1, Hd))


def _rope(x, cos2, sin2):
    return x * cos2 + pltpu.roll(x, HEAD_DIM // 2, 1) * sin2


def _rope_t(d, cos2, sin2):
    return d * cos2 + pltpu.roll(d * sin2, HEAD_DIM // 2, 1)


def _ret_specs(tseg, seg_of):
    Hd, G = HEAD_DIM, N_GROUPS
    col = lambda kind: pl.BlockSpec((tseg, Hd), lambda h, s: (seg_of(s), G * kind + h))
    tab = pl.BlockSpec((tseg, Hd), lambda h, s: (seg_of(s), 0))
    head = pl.BlockSpec((1, RET_CHUNK, Hd), lambda h, s: (h, 0, 0))
    cd = pl.BlockSpec((1, 1, Hd), lambda h, s: (h, 0, 0))
    gain = pl.BlockSpec((1, Hd), lambda h, s: (0, h))
    return col, tab, head, cd, gain


def ret_fwd(proj, ret_norm, tables, name, hosted=()):
    T = proj.shape[0]
    Hd, C, G = HEAD_DIM, RET_CHUNK, N_GROUPS
    tseg = min(T, 1024)
    nseg, nck = T // tseg, tseg // C
    scale = Hd ** -0.5
    cos2, sin2, intra, k_tail, q_head, chunk_decay = tables

    def body(q_ref, k_ref, v_ref, g_ref, gain_ref, cos_ref, sin_ref, m_ref, kt_ref, qh_ref, cd_ref,
             b_ref, o_ref, rp_ref, state):
        @pl.when(pl.program_id(1) == 0)
        def _():
            state[...] = jnp.zeros_like(state)

        def chunk(ci, carry):
            rows = pl.ds(pl.multiple_of(ci * C, C), C)
            cos, sin = cos_ref[rows, :], sin_ref[rows, :]
            qr = _rope(q_ref[rows, :], cos, sin)
            kr = _rope(k_ref[rows, :], cos, sin) * scale
            qb, kb, vb = qr.astype(BF16), kr.astype(BF16), v_ref[rows, :].astype(BF16)
            r = state[...]
            rp_ref[0, ci] = r.astype(BF16)
            sc = _dot_nt(qb, kb) * m_ref[0]
            o = _dot(sc.astype(BF16), vb) + _dot((qr * qh_ref[0]).astype(BF16), r.astype(BF16))
            state[...] = cd_ref[0] * r + _dot_tn((kr * kt_ref[0]).astype(BF16), vb)
            o_ref[rows, :] = o
            on = o * _rstd(o)
            b_ref[rows, :] = (jax.nn.silu(g_ref[rows, :]) * (on * gain_ref[...])).astype(BF16)
            return carry

        lax.fori_loop(0, nck, chunk, 0)

    col, tab, head, cd, gain = _ret_specs(tseg, lambda s: s)
    out_col = pl.BlockSpec((tseg, Hd), lambda h, s: (s, h))
    return _call(
        body, hosted, name=name, grid=(G, nseg),
        in_specs=[col(1), col(2), col(3), col(4), gain, tab, tab, head, head, head, cd],
        out_specs=[out_col, out_col, pl.BlockSpec((1, nck, Hd, Hd), lambda h, s: (h, s, 0, 0))],
        out_shape=[jax.ShapeDtypeStruct((T, G * Hd), BF16), jax.ShapeDtypeStruct((T, G * Hd), F32),
                   jax.ShapeDtypeStruct((G, T // C, Hd, Hd), BF16)],
        scratch_shapes=[pltpu.VMEM((Hd, Hd), F32)],
        args=[proj, proj, proj, proj, ret_norm, cos2, sin2, intra, k_tail, q_head, chunk_decay])


def ret_bwd(proj, db, o_pre, r_prev, ret_norm, tables, name, hosted=()):
    T = proj.shape[0]
    Hd, C, G = HEAD_DIM, RET_CHUNK, N_GROUPS
    tseg = min(T, 1024)
    nseg, nck = T // tseg, tseg // C
    scale = Hd ** -0.5
    cos2, sin2, intra, k_tail, q_head, chunk_decay = tables

    def body(q_ref, k_ref, v_ref, g_ref, db_ref, o_ref, rp_ref, gain_ref, cos_ref, sin_ref, m_ref, kt_ref, qh_ref, cd_ref,
             d_ref, dgain_ref, gstate):
        @pl.when(pl.program_id(1) == 0)
        def _():
            gstate[...] = jnp.zeros_like(gstate)
            dgain_ref[...] = jnp.zeros_like(dgain_ref)

        def chunk(t, carry):
            ci = nck - 1 - t
            rows = pl.ds(pl.multiple_of(ci * C, C), C)
            cos, sin = cos_ref[rows, :], sin_ref[rows, :]
            qr = _rope(q_ref[rows, :], cos, sin)
            kr = _rope(k_ref[rows, :], cos, sin) * scale
            qb, kb, vb = qr.astype(BF16), kr.astype(BF16), v_ref[rows, :].astype(BF16)
            qhb, ktb = (qr * qh_ref[0]).astype(BF16), (kr * kt_ref[0]).astype(BF16)
            sc = (_dot_nt(qb, kb) * m_ref[0]).astype(BF16)
            o = o_ref[rows, :]
            rstd = _rstd(o)
            on = o * rstd
            gain = gain_ref[...]
            silu, dsilu = _silu_parts(g_ref[rows, :])
            dy = db_ref[rows, :]
            dgain_ref[...] += jnp.sum(dy * silu * on, axis=0, keepdims=True)
            dg = dy * on * gain * dsilu
            don = dy * silu * gain
            dob = (rstd * (don - on * jnp.mean(don * on, axis=-1, keepdims=True))).astype(BF16)
            gn = gstate[...]
            gb = gn.astype(BF16)
            da = (_dot_nt(dob, vb) * m_ref[0]).astype(BF16)
            dq = _dot(da, kb) + _dot_nt(dob, rp_ref[0, ci]) * qh_ref[0]
            dk = _dot_tn(da, qb) + _dot_nt(vb, gb) * kt_ref[0]
            dv = _dot_tn(sc, dob) + _dot(ktb, gb)
            gstate[...] = cd_ref[0] * gn + _dot_tn(qhb, dob)
            d_ref[0, rows, :] = _rope_t(dq, cos, sin).astype(BF16)
            d_ref[1, rows, :] = _rope_t(dk * scale, cos, sin).astype(BF16)
            d_ref[2, rows, :] = dv.astype(BF16)
            d_ref[3, rows, :] = dg.astype(BF16)
            return carry

        lax.fori_loop(0, nck, chunk, 0)

    rev = lambda s: nseg - 1 - s
    col, tab, head, cd, gain = _ret_specs(tseg, rev)
    act = pl.BlockSpec((tseg, Hd), lambda h, s: (rev(s), h))
    return _call(
        body, hosted, name=name, grid=(G, nseg),
        in_specs=[col(1), col(2), col(3), col(4), act, act, pl.BlockSpec((1, nck, Hd, Hd), lambda h, s: (h, rev(s), 0, 0)),
                  gain, tab, tab, head, head, head, cd],
        out_specs=[pl.BlockSpec((4, tseg, Hd), lambda h, s: (0, rev(s), h)), gain],
        out_shape=[jax.ShapeDtypeStruct((4, T, G * Hd), BF16), jax.ShapeDtypeStruct((1, G * Hd), F32)],
        scratch_shapes=[pltpu.VMEM((Hd, Hd), F32)],
        args=[proj, proj, proj, proj, db, o_pre, r_prev, ret_norm, cos2, sin2, intra, k_tail, q_head, chunk_decay])


def final_loss(h, gain, target, name, hosted=()):
    T, D = h.shape
    tm = min(T, 512)

    def body(h_ref, g_ref, t_ref, dh_ref, loss_ref, dg_ref):
        @pl.when(pl.program_id(0) == 0)
        def _():
            loss_ref[...] = jnp.zeros_like(loss_ref)
            dg_ref[...] = jnp.zeros_like(dg_ref)

        hh = h_ref[...]
        gain_v = g_ref[...]
        err = hh * _rstd(hh) * gain_v - t_ref[...]
        loss_ref[...] += 0.5 * jnp.sum(jnp.mean(err * err, axis=-1, keepdims=True), axis=0, keepdims=True)
        dhn, dg = _rmsnorm_bwd(err * (1.0 / D), hh, gain_v)
        dh_ref[...] = dhn
        dg_ref[...] += jnp.sum(dg, axis=0, keepdims=True)

    row_spec = pl.BlockSpec((tm, D), lambda i: (i, 0))
    vec_spec = pl.BlockSpec((1, D), lambda i: (0, 0))
    return _call(
        body, hosted, name=name, grid=(T // tm,),
        in_specs=[row_spec, vec_spec, row_spec],
        out_specs=[row_spec, pl.BlockSpec((1, 128), lambda i: (0, 0)), vec_spec],
        out_shape=[jax.ShapeDtypeStruct((T, D), F32), jax.ShapeDtypeStruct((1, 128), F32), jax.ShapeDtypeStruct((1, D), F32)],
        args=[h, gain, target])


def prereduce(grad, recv, core, name):
    nsh, R, C = grad.shape
    rh = R // 2

    def body(c_ref, g_ref, r_ref, o_ref):
        o_ref[...] = (g_ref[...].astype(F32) + r_ref[...].astype(F32)).astype(BF16)

    return pl.pallas_call(
        body, name=name,
        grid_spec=pltpu.PrefetchScalarGridSpec(
            num_scalar_prefetch=1, grid=(nsh,),
            in_specs=[pl.BlockSpec((1, rh, C), lambda j, c_ref: (j, c_ref[0], 0)), pl.BlockSpec((1, rh, C), lambda j, c_ref: (j, 0, 0))],
            out_specs=pl.BlockSpec((1, rh, C), lambda j, c_ref: (j, 0, 0))),
        out_shape=jax.ShapeDtypeStruct((nsh, rh, C), BF16),
        compiler_params=pltpu.CompilerParams(vmem_limit_bytes=VMEM_LIMIT_V7X),
    )(core, grad, recv)


def _adamw(w, g, m, v):
    m = ADAM_B1 * m + (1.0 - ADAM_B1) * g
    v = ADAM_B2 * v + (1.0 - ADAM_B2) * (g * g)
    m_hat = m / (1.0 - ADAM_B1 ** ADAM_STEP)
    v_hat = v / (1.0 - ADAM_B2 ** ADAM_STEP)
    return -ADAM_LR * (m_hat / (jnp.sqrt(v_hat) + ADAM_EPS) + ADAM_WD * w), m, v


def adamw_sharded(tensors, name, hosted=()):
    nt = len(tensors)
    nsh, R, C = tensors[0][0].shape
    tr = 256 if R % 256 == 0 else R // 2

    def body(*refs):
        ins, outs = refs[:4 * nt], refs[4 * nt:]
        for t in range(nt):
            p_ref, w_ref, m_ref, v_ref = ins[4 * t:4 * t + 4]
            g_ref, d_ref, nm_ref, nv_ref = outs[4 * t:4 * t + 4]
            g = p_ref[0].astype(F32)
            for i in range(1, nsh):
                g += p_ref[i].astype(F32)
            g_ref[...] = g
            d_ref[...], nm_ref[...], nv_ref[...] = _adamw(w_ref[...], g, m_ref[...], v_ref[...])

    spec = pl.BlockSpec((tr, C), lambda i: (i, 0))
    out = jax.ShapeDtypeStruct((R, C), F32)
    return _call(
        body, hosted, name=name, grid=(R // tr,),
        in_specs=[pl.BlockSpec((nsh, tr, C), lambda i: (0, i, 0)), spec, spec, spec] * nt,
        out_specs=[spec] * (4 * nt), out_shape=[out] * (4 * nt),
        args=[a for tensor in tensors for a in tensor])


def adamw_small(packs, w, m, v, name):
    ndev, R, L = packs.shape

    def body(p_ref, w_ref, m_ref, v_ref, g_ref, d_ref, nm_ref, nv_ref):
        g = p_ref[0]
        for i in range(1, ndev):
            g += p_ref[i]
        g_ref[...] = g
        d_ref[...], nm_ref[...], nv_ref[...] = _adamw(w_ref[...], g, m_ref[...], v_ref[...])

    out = jax.ShapeDtypeStruct((R, L), F32)
    return pl.pallas_call(body, name=name, out_shape=[out] * 4,
                          compiler_params=pltpu.CompilerParams(vmem_limit_bytes=VMEM_LIMIT_V7X))(packs, w, m, v)


BIG = ("ffn1_w1", "ffn1_w3", "ffn1_w2", "w_in", "w_out", "ffn2_w1", "ffn2_w3", "ffn2_w2")
SMALL = ("ffn1_norm", "mix_norm", "pool_w", "pool_scale", "ret_norm", "ffn2_norm", "final_norm")
WEIGHTS = ("ffn1_norm", "ffn1_w1", "ffn1_w3", "ffn1_w2", "mix_norm", "w_in", "pool_w", "pool_scale", "ret_norm", "w_out",
           "ffn2_norm", "ffn2_w1", "ffn2_w3", "ffn2_w2", "final_norm")


def _pack(parts):
    return jnp.concatenate([parts[k].reshape(-1, 128) for k in SMALL], axis=0)


def _unpack(pack, like):
    out, row = {}, 0
    for k in SMALL:
        rows = like[k].size // 128
        out[k] = pack[row:row + rows].reshape(like[k].shape)
        row += rows
    return out


def kernel(x, ffn1_norm, ffn1_w1, ffn1_w3, ffn1_w2, mix_norm, w_in, pool_w, pool_scale, ret_norm, w_out, ffn2_norm, ffn2_w1, ffn2_w3, ffn2_w2, final_norm, loss_target, m_ffn1_norm, m_ffn1_w1, m_ffn1_w3, m_ffn1_w2, m_mix_norm, m_w_in, m_pool_w, m_pool_scale, m_ret_norm, m_w_out, m_ffn2_norm, m_ffn2_w1, m_ffn2_w3, m_ffn2_w2, m_final_norm, v_ffn1_norm, v_ffn1_w1, v_ffn1_w3, v_ffn1_w2, v_mix_norm, v_w_in, v_pool_w, v_pool_scale, v_ret_norm, v_w_out, v_ffn2_norm, v_ffn2_w1, v_ffn2_w3, v_ffn2_w2, v_final_norm):
    w = dict(ffn1_norm=ffn1_norm, ffn1_w1=ffn1_w1, ffn1_w3=ffn1_w3, ffn1_w2=ffn1_w2, mix_norm=mix_norm, w_in=w_in, pool_w=pool_w,
             pool_scale=pool_scale, ret_norm=ret_norm, w_out=w_out, ffn2_norm=ffn2_norm, ffn2_w1=ffn2_w1, ffn2_w3=ffn2_w3,
             ffn2_w2=ffn2_w2, final_norm=final_norm)
    m = dict(ffn1_norm=m_ffn1_norm, ffn1_w1=m_ffn1_w1, ffn1_w3=m_ffn1_w3, ffn1_w2=m_ffn1_w2, mix_norm=m_mix_norm, w_in=m_w_in,
             pool_w=m_pool_w, pool_scale=m_pool_scale, ret_norm=m_ret_norm, w_out=m_w_out, ffn2_norm=m_ffn2_norm, ffn2_w1=m_ffn2_w1,
             ffn2_w3=m_ffn2_w3, ffn2_w2=m_ffn2_w2, final_norm=m_final_norm)
    v = dict(ffn1_norm=v_ffn1_norm, ffn1_w1=v_ffn1_w1, ffn1_w3=v_ffn1_w3, ffn1_w2=v_ffn1_w2, mix_norm=v_mix_norm, w_in=v_w_in,
             pool_w=v_pool_w, pool_scale=v_pool_scale, ret_norm=v_ret_norm, w_out=v_w_out, ffn2_norm=v_ffn2_norm, ffn2_w1=v_ffn2_w1,
             ffn2_w3=v_ffn2_w3, ffn2_w2=v_ffn2_w2, final_norm=v_final_norm)
    xs, target = x[0], loss_target[0]
    T = xs.shape[0]
    tables = _ret_tables(T)
    core = lax.axis_index("c").astype(jnp.int32).reshape(1)
    sh = {k: w[k][0].astype(BF16) for k in BIG}
    gather = lambda *names: [ChipExchange([sh[k] for k in names], False, north_only=True)]
    wg, grad, delta, new_m, new_v = {}, {}, {}, {}, {}

    def update(names, pieces, name, hosted=()):
        outs, extras = adamw_sharded([(p, w[k][0], m[k][0], v[k][0]) for k, p in zip(names, pieces)], name, hosted)
        for t, k in enumerate(names):
            grad[k], delta[k], new_m[k], new_v[k] = [o[None] for o in outs[4 * t:4 * t + 4]]
        return extras

    def reduce_in_chip(name, partial, recv):
        return prereduce(partial, recv, core, "prereduce_" + name)

    (wg["ffn1_w1"], wg["ffn1_w3"]), = exchange(gather("ffn1_w1", "ffn1_w3"), "gather_ffn1")
    (n1, a1, b1, s1), ((wg["ffn1_w2"], wg["w_in"]),) = ffn_up(
        xs, ffn1_norm, wg["ffn1_w1"], wg["ffn1_w3"], "ffn1_up", gather("ffn1_w2", "w_in"))
    (h1,), ((wg["w_out"],),) = ffn_down(s1, wg["ffn1_w2"], xs, "ffn1_down", gather("w_out"))
    (u, proj), ((wg["ffn2_w1"],),) = mix_in(h1, mix_norm, wg["w_in"], "mix_in", gather("ffn2_w1"))
    (pa,), _ = pool_fwd(proj, pool_w[0], pool_scale, "pool_fwd")
    (rb, o_pre, r_prev), ((wg["ffn2_w3"],),) = ret_fwd(proj, ret_norm, tables, "ret_fwd", gather("ffn2_w3"))
    (h2,), _ = mix_out(pa, rb, wg["w_out"], h1, "mix_out")
    (n2, a2, b2, s2), ((wg["ffn2_w2"],),) = ffn_up(
        h2, ffn2_norm, wg["ffn2_w1"], wg["ffn2_w3"], "ffn2_up", gather("ffn2_w2"))
    (h3,), _ = ffn_down(s2, wg["ffn2_w2"], h2, "ffn2_down")
    (dh3, loss, d_final), _ = final_loss(h3, final_norm[None], target, "final_loss")
    loss = lax.psum(loss[0, 0], ("x", "y", "c"))

    (da2, db2), _ = ffn_bwd_act(dh3, wg["ffn2_w2"], a2, b2, "ffn2_bwd_act")
    (g_f2w2,), _ = ffn_dw2(s2, dh3, "ffn2_dw2")
    (g_f2w1, g_f2w3), ((r_f2w2,),) = ffn_dw13(n2, da2, db2, "ffn2_dw13", [SiblingExchange([g_f2w2])])
    p_f2w2 = reduce_in_chip("ffn2_w2", g_f2w2, r_f2w2)
    (dh2, d_ffn2), ((q_f2w2,), (r_f2w1, r_f2w3)) = ffn_bwd_in(
        da2, db2, wg["ffn2_w1"], wg["ffn2_w3"], h2, ffn2_norm, dh3, "ffn2_bwd_in",
        [ChipExchange([p_f2w2], True), SiblingExchange([g_f2w1, g_f2w3])])
    p_f2w1 = reduce_in_chip("ffn2_w1", g_f2w1, r_f2w1)
    p_f2w3 = reduce_in_chip("ffn2_w3", g_f2w3, r_f2w3)
    (dpa, drb, g_wout), ((q_f2w1,),) = mix_out_bwd(dh2, wg["w_out"], pa, rb, "mix_out_bwd", [ChipExchange([p_f2w1], True)])
    (dpool, d_pool_w, d_pool_scale), _ = pool_bwd(proj, dpa, pool_w[0], pool_scale, "pool_bwd")
    (dqkvg, d_ret_norm), ((q_f2w3,), (r_wout,)) = ret_bwd(
        proj, drb, o_pre, r_prev, ret_norm, tables, "ret_bwd", [ChipExchange([p_f2w3], True), SiblingExchange([g_wout])])
    p_wout = reduce_in_chip("w_out", g_wout, r_wout)
    d = jnp.concatenate([dpool, dqkvg[0], dqkvg[1], dqkvg[2], dqkvg[3]], axis=1)
    (g_win,), ((q_wout,),) = mix_dwin(u, d, N_CHIPS, "mix_dwin", [ChipExchange([p_wout], True)])
    (dh1, d_mix), ((r_win,),) = mix_in_bwd(d, wg["w_in"], h1, mix_norm, dh2, "mix_in_bwd", [SiblingExchange([g_win])])
    p_win = reduce_in_chip("w_in", g_win, r_win)
    (da1, db1), ((q_win,),) = ffn_bwd_act(dh1, wg["ffn1_w2"], a1, b1, "ffn1_bwd_act", [ChipExchange([p_win], True)])
    (g_f1w1, g_f1w3), _ = ffn_dw13(n1, da1, db1, "ffn1_dw13")
    (g_f1w2,), ((r_f1w1, r_f1w3),) = ffn_dw2(s1, dh1, "ffn1_dw2", [SiblingExchange([g_f1w1, g_f1w3])])
    p_f1w1 = reduce_in_chip("ffn1_w1", g_f1w1, r_f1w1)
    p_f1w3 = reduce_in_chip("ffn1_w3", g_f1w3, r_f1w3)
    (dx, d_ffn1), ((q_f1w1, q_f1w3), (r_f1w2,)) = ffn_bwd_in(
        da1, db1, wg["ffn1_w1"], wg["ffn1_w3"], xs, ffn1_norm, dh1, "ffn1_bwd_in",
        [ChipExchange([p_f1w1, p_f1w3], True), SiblingExchange([g_f1w2])])
    p_f1w2 = reduce_in_chip("ffn1_w2", g_f1w2, r_f1w2)

    (q_f1w2,), = update(["ffn2_w1", "ffn2_w3"], [q_f2w1, q_f2w3], "adamw_ffn2_w13", [ChipExchange([p_f1w2], True)])
    update(["ffn2_w2"], [q_f2w2], "adamw_ffn2_w2")
    update(["w_in"], [q_win], "adamw_w_in")
    update(["w_out"], [q_wout], "adamw_w_out")
    update(["ffn1_w1", "ffn1_w3"], [q_f1w1, q_f1w3], "adamw_ffn1_w13")
    update(["ffn1_w2"], [q_f1w2], "adamw_ffn1_w2")

    small = {"ffn1_norm": d_ffn1, "mix_norm": d_mix, "pool_w": d_pool_w, "pool_scale": d_pool_scale,
             "ret_norm": d_ret_norm, "ffn2_norm": d_ffn2, "final_norm": d_final}
    packs = all_exchange_small(_pack(small), "gather_small")
    outs = adamw_small(packs, _pack(w), _pack(m), _pack(v), "adamw_small")
    for res, pack in zip((grad, delta, new_m, new_v), outs):
        res.update(_unpack(pack, w))

    return (loss, dx[None], *[grad[k] for k in WEIGHTS], *[delta[k] for k in WEIGHTS],
            *[new_m[k] for k in WEIGHTS], *[new_v[k] for k in WEIGHTS])
```

```python
import math

import jax
import jax.numpy as jnp
from jax import lax
from jax.experimental import pallas as pl
from jax.experimental.pallas import tpu as pltpu

F32 = jnp.float32
BF16 = jnp.bfloat16

EPS = 1e-6
N_CHIPS = 4
N_GROUPS = 4
HEAD_DIM = 128
RET_CHUNK = 128
BF16_TILE_ROWS = 16
ROPE_BASE = 10000.0
ADAM_LR, ADAM_B1, ADAM_B2, ADAM_EPS, ADAM_WD, ADAM_STEP = 0.001, 0.9, 0.999, 1e-08, 0.01, 10
VMEM_LIMIT_V7X = 56 * 1024 * 1024
MESH = pl.DeviceIdType.MESH
ANY = pl.BlockSpec(memory_space=pl.ANY)


def _dot(a, b):
    return jnp.dot(a, b, preferred_element_type=F32)


def _dot_nt(a, b):
    return lax.dot_general(a, b, (((1,), (1,)), ((), ())), preferred_element_type=F32)


def _dot_tn(a, b):
    return lax.dot_general(a, b, (((0,), (0,)), ((), ())), preferred_element_type=F32)


def _rstd(h):
    return lax.rsqrt(jnp.mean(h * h, axis=-1, keepdims=True) + EPS)


def _rmsnorm_bwd(dn, h, gain):
    r = _rstd(h)
    nh = h * r
    dnh = dn * gain
    dh = r * (dnh - nh * jnp.mean(dnh * nh, axis=-1, keepdims=True))
    return dh, dn * nh


def _silu_parts(a):
    sig = jax.nn.sigmoid(a)
    silu = a * sig
    return silu, sig + silu * (1.0 - sig)


def _mesh_pos():
    return lax.axis_index("x"), lax.axis_index("y"), lax.axis_index("c")


class ChipExchange:
    def __init__(self, srcs, scatter, north_only=False):
        assert not (scatter and north_only)
        n = len(srcs)
        self.srcs, self.scatter, self.north_only, self.n = list(srcs), scatter, north_only, n
        self.half_rows = [s.shape[1] if scatter else s.shape[0] // (1 if north_only else 2) for s in srcs]
        self.out_shape = [jax.ShapeDtypeStruct((N_CHIPS, (1 if north_only else 2) * rh, s.shape[-1]), s.dtype)
                          for s, rh in zip(srcs, self.half_rows)]
        if scatter:
            self.out_shape += [jax.ShapeDtypeStruct((2, rh // 2, s.shape[-1]), s.dtype) for s, rh in zip(srcs, self.half_rows)]
        dma = pltpu.SemaphoreType.DMA
        self.sems = [dma((n,)), dma((4 * n,)), dma((4 * n,)), dma((2 * n,)), dma((2 * n,)), dma((4 * n,)), dma((4 * n,))]

    def _copies(self, src, out, sems):
        local_sem, hop1_send, hop1_recv, hop2_send, hop2_recv, d2d_send, d2d_recv = sems
        x, y, c = _mesh_pos()
        me, dg = 2 * x + y, 2 * (1 - x) + (1 - y)
        sibling = (x, y, 1 - c)
        n = self.n
        mine, theirs = (0, 0) if self.north_only else (c, 1 - c)

        def nb(a):
            nx, ny = x ^ (1 - a), y ^ a
            return 2 * nx + ny, (nx, ny, c)

        def remote(s, d, send, recv, k, to):
            return pltpu.make_async_remote_copy(src_ref=s, dst_ref=d, send_sem=send.at[k], recv_sem=recv.at[k],
                                                device_id=to, device_id_type=MESH)

        class Copies:
            def slot(_, t, chip, unit):
                rh = self.half_rows[t]
                return out[t].at[chip, pl.ds(unit * rh, rh), :]

            def quarter(_, t, chip, q):
                qh = self.half_rows[t] // 2
                return out[t].at[chip, pl.ds(mine * 2 * qh + q * qh, qh), :]

            def local(k, t, chunk=None):
                s, d = (src[t].at[me], k.slot(t, me, mine)) if self.scatter else (src[t], out[t].at[me])
                if chunk is not None:
                    rows = s.shape[0] // self.local_chunks(t)
                    s, d = s.at[pl.ds(chunk * rows, rows), :], d.at[pl.ds(chunk * rows, rows), :]
                return pltpu.make_async_copy(s, d, local_sem.at[t])

            def hop1(k, t, a, transit=False):
                rh = self.half_rows[t]
                chip, to = nb(a)
                if transit:
                    piece = src[t].at[dg, pl.ds(a * (rh // 2), rh // 2), :]
                    return remote(piece, out[n + t].at[a], hop1_send, hop1_recv, 4 * t + 2 + a, to)
                piece = src[t].at[chip] if self.scatter else src[t].at[pl.ds(mine * rh, rh), :]
                return remote(piece, k.slot(t, me, mine), hop1_send, hop1_recv, 4 * t + a, to)

            def landed1(k, t, a, transit=False):
                here = out[n + t].at[a] if transit else k.slot(t, nb(a)[0], mine)
                return remote(here, here, hop1_send, hop1_recv, 4 * t + (2 if transit else 0) + a, sibling)

            def hop2(k, t, q):
                origin, to = nb(q)[0], nb(1 - q)[1]
                piece = out[n + t].at[q] if self.scatter else k.quarter(t, origin, q)
                return remote(piece, k.quarter(t, origin, q), hop2_send, hop2_recv, 2 * t + q, to)

            def landed2(k, t, q):
                here = k.quarter(t, dg, q)
                return remote(here, here, hop2_send, hop2_recv, 2 * t + q, sibling)

            def d2d(k, t, p, chip, own=False, arriving=False):
                if arriving:
                    there = k.slot(t, chip, theirs)
                    return remote(there, there, d2d_send, d2d_recv, 4 * t + p, sibling)
                piece = src[t].at[me] if own else k.slot(t, chip, mine)
                return remote(piece, k.slot(t, chip, mine), d2d_send, d2d_recv, 4 * t + p, sibling)

        return Copies(), nb, me, dg, c

    def _by(self, sender, c, fn):
        if self.north_only:
            pl.when(c == (1 if sender else 0))(fn)
        else:
            fn()

    def local_chunks(self, t):
        rows = self.half_rows[t] if self.scatter else self.srcs[t].shape[0]
        return max(n for n in (1, 2, 4, 8) if rows % (BF16_TILE_ROWS * n) == 0)

    def start(self, src, out, sems):
        k, nb, me, dg, c = self._copies(src, out, sems)
        for t in range(self.n):
            for chunk in range(self.local_chunks(t)):
                k.local(t, chunk).start()

        def send():
            for t in range(self.n):
                for first in range(2):
                    a = first ^ c
                    k.hop1(t, a).start()
                    if self.scatter:
                        k.hop1(t, a, transit=True).start()
                if self.scatter:
                    k.d2d(t, 3, me, own=True).start()

        self._by(True, c, send)

    def mid(self, src, out, sems):
        k, nb, me, dg, c = self._copies(src, out, sems)

        def forward():
            for t in range(self.n):
                for first in range(2):
                    a = first ^ c
                    if self.scatter:
                        k.landed1(t, a, transit=True).wait_recv()
                        k.hop2(t, a).start()
                    k.landed1(t, a).wait_recv()
                    if not self.scatter:
                        k.hop2(t, a).start()
                    k.d2d(t, a, nb(a)[0]).start()

        self._by(True, c, forward)

    def finish(self, src, out, sems):
        k, nb, me, dg, c = self._copies(src, out, sems)

        def last_forward():
            for t in range(self.n):
                for q in range(2):
                    k.landed2(t, q).wait_recv()
                k.d2d(t, 2, dg).start()

        def arrivals():
            for t in range(self.n):
                for a in range(2):
                    k.d2d(t, a, nb(a)[0], arriving=True).wait_recv()
                k.d2d(t, 2, dg, arriving=True).wait_recv()
                if self.scatter:
                    k.d2d(t, 3, me, arriving=True).wait_recv()

        def sent():
            for t in range(self.n):
                for a in range(2):
                    k.hop1(t, a).wait_send()
                    if self.scatter:
                        k.hop1(t, a, transit=True).wait_send()
                    k.hop2(t, a).wait_send()
                    k.d2d(t, a, nb(a)[0]).wait_send()
                k.d2d(t, 2, dg).wait_send()
                if self.scatter:
                    k.d2d(t, 3, me, own=True).wait_send()

        self._by(True, c, last_forward)
        self._by(False, c, arrivals)
        self._by(True, c, sent)
        for t in range(self.n):
            k.local(t).wait()


class SiblingExchange:
    def __init__(self, grads):
        self.srcs, self.n = list(grads), len(grads)
        self.half_rows = [g.shape[1] // 2 for g in grads]
        self.out_shape = [jax.ShapeDtypeStruct((g.shape[0], rh, g.shape[2]), g.dtype) for g, rh in zip(grads, self.half_rows)]
        self.sems = [pltpu.SemaphoreType.DMA((self.n,)), pltpu.SemaphoreType.DMA((self.n,))]

    def _plan(self, src, out, sems):
        x, y, c = _mesh_pos()
        return [pltpu.make_async_remote_copy(
            src_ref=src[t].at[:, pl.ds((1 - c) * self.half_rows[t], self.half_rows[t]), :], dst_ref=out[t],
            send_sem=sems[0].at[t], recv_sem=sems[1].at[t], device_id=(x, y, 1 - c), device_id_type=MESH) for t in range(self.n)]

    def start(self, src, out, sems):
        for cp in self._plan(src, out, sems):
            cp.start()

    def mid(self, src, out, sems):
        pass

    def finish(self, src, out, sems):
        for cp in self._plan(src, out, sems):
            cp.wait()


def _call(body, hosted=(), *, name, in_specs, out_specs, out_shape, args, grid=(), scratch_shapes=()):
    n_in, n_out, n_scr = len(in_specs), len(out_specs), len(scratch_shapes)
    total = math.prod(grid)
    mid_step = max(0, (5 * total) // 8 - 1)

    def full(*refs):
        pos = [0]

        def take(k):
            pos[0] += k
            return refs[pos[0] - k:pos[0]]

        ins, h_in = take(n_in), [take(h.n) for h in hosted]
        outs, h_out = take(n_out), [take(len(h.out_shape)) for h in hosted]
        scr, h_sem = take(n_scr), [take(len(h.sems)) for h in hosted]
        step = 0
        for axis, size in enumerate(grid):
            step = step * size + pl.program_id(axis)

        def phase(at, method):
            if not hosted:
                return
            if total == 1:
                for h, s, o, m in zip(hosted, h_in, h_out, h_sem):
                    getattr(h, method)(s, o, m)
                return

            @pl.when(step == at)
            def _():
                for h, s, o, m in zip(hosted, h_in, h_out, h_sem):
                    getattr(h, method)(s, o, m)

        phase(0, "start")
        body(*ins, *outs, *scr)
        phase(mid_step, "mid")
        phase(total - 1, "finish")

    results = pl.pallas_call(
        full, name=name, grid=grid,
        in_specs=list(in_specs) + [ANY] * sum(h.n for h in hosted),
        out_specs=list(out_specs) + [ANY] * sum(len(h.out_shape) for h in hosted),
        out_shape=list(out_shape) + [s for h in hosted for s in h.out_shape],
        scratch_shapes=list(scratch_shapes) + [s for h in hosted for s in h.sems],
        compiler_params=pltpu.CompilerParams(vmem_limit_bytes=VMEM_LIMIT_V7X),
    )(*args, *[s for h in hosted for s in h.srcs])
    outs, extras, pos = list(results[:n_out]), [], n_out
    for h in hosted:
        extras.append(list(results[pos:pos + h.n]))
        pos += len(h.out_shape)
    return outs, extras


def exchange(hosted, name):
    return _call(lambda: None, hosted, name=name, in_specs=[], out_specs=[], out_shape=[], args=[])[1]


def all_exchange_small(pack, name):
    R, L = pack.shape
    flips = [(dx, dy, dc) for dx in (0, 1) for dy in (0, 1) for dc in (0, 1)][1:]

    def body(src, out, local_sem, send_sem, recv_sem):
        x, y, c = _mesh_pos()
        me = 4 * x + 2 * y + c
        local = pltpu.make_async_copy(src, out.at[me], local_sem)
        local.start()
        copies = []
        for k, (dx, dy, dc) in enumerate(flips):
            copies.append(pltpu.make_async_remote_copy(
                src_ref=src, dst_ref=out.at[me], send_sem=send_sem.at[k], recv_sem=recv_sem.at[k],
                device_id=(x ^ dx, y ^ dy, c ^ dc), device_id_type=MESH))
        for cp in copies:
            cp.start()
        for k, (dx, dy, dc) in enumerate(flips):
            landed = out.at[4 * (x ^ dx) + 2 * (y ^ dy) + (c ^ dc)]
            pltpu.make_async_remote_copy(src_ref=landed, dst_ref=landed, send_sem=send_sem.at[k], recv_sem=recv_sem.at[k],
                                         device_id=(x ^ dx, y ^ dy, c ^ dc), device_id_type=MESH).wait_recv()
        for cp in copies:
            cp.wait_send()
        local.wait()

    return pl.pallas_call(
        body, name=name, in_specs=[ANY], out_specs=ANY,
        out_shape=jax.ShapeDtypeStruct((2 * N_CHIPS, R, L), pack.dtype),
        scratch_shapes=[pltpu.SemaphoreType.DMA, pltpu.SemaphoreType.DMA((7,)), pltpu.SemaphoreType.DMA((7,))],
    )(pack)


def ffn_up(h, gain, w1g, w3g, name, hosted=()):
    T, D = h.shape
    nsh, _, Fs = w1g.shape
    tm = min(T, 1024)

    def body(h_ref, g_ref, w1_ref, w3_ref, n_ref, a_ref, b_ref, s_ref):
        @pl.when(pl.program_id(1) == 0)
        def _():
            hh = h_ref[...]
            n_ref[...] = (hh * _rstd(hh) * g_ref[...]).astype(BF16)

        n = n_ref[...]
        a = _dot(n, w1_ref[0])
        b = _dot(n, w3_ref[0])
        a_ref[0] = a.astype(BF16)
        b_ref[0] = b.astype(BF16)
        s_ref[0] = (a * jax.nn.sigmoid(a) * b).astype(BF16)

    act = jax.ShapeDtypeStruct((nsh, T, Fs), BF16)
    act_spec = pl.BlockSpec((1, tm, Fs), lambda i, j: (j, i, 0))
    w_spec = pl.BlockSpec((1, D, Fs), lambda i, j: (j, 0, 0))
    return _call(
        body, hosted, name=name, grid=(T // tm, nsh),
        in_specs=[pl.BlockSpec((tm, D), lambda i, j: (i, 0)), pl.BlockSpec((1, D), lambda i, j: (0, 0)), w_spec, w_spec],
        out_specs=[pl.BlockSpec((tm, D), lambda i, j: (i, 0)), act_spec, act_spec, act_spec],
        out_shape=[jax.ShapeDtypeStruct((T, D), BF16), act, act, act],
        args=[h, gain, w1g, w3g])


def ffn_down(s, w2g, h, name, hosted=()):
    nsh, T, Fs = s.shape
    D = h.shape[1]
    tm = min(T, 512)

    def body(s_ref, w2_ref, h_ref, o_ref):
        f = _dot(s_ref[0], w2_ref[0])
        for j in range(1, nsh):
            f += _dot(s_ref[j], w2_ref[j])
        o_ref[...] = h_ref[...] + 0.5 * f

    return _call(
        body, hosted, name=name, grid=(T // tm,),
        in_specs=[pl.BlockSpec((nsh, tm, Fs), lambda i: (0, i, 0)), pl.BlockSpec((nsh, Fs, D), lambda i: (0, 0, 0)),
                  pl.BlockSpec((tm, D), lambda i: (i, 0))],
        out_specs=[pl.BlockSpec((tm, D), lambda i: (i, 0))],
        out_shape=[jax.ShapeDtypeStruct((T, D), F32)],
        args=[s, w2g, h])


def ffn_bwd_act(dh, w2g, a, b, name, hosted=()):
    T, D = dh.shape
    nsh, Fs, _ = w2g.shape
    tm = min(T, 1024)

    def body(dh_ref, w2_ref, a_ref, b_ref, da_ref, db_ref):
        df = (0.5 * dh_ref[...]).astype(BF16)
        ds = _dot_nt(df, w2_ref[0])
        silu, dsilu = _silu_parts(a_ref[0].astype(F32))
        da_ref[0] = (ds * b_ref[0].astype(F32) * dsilu).astype(BF16)
        db_ref[0] = (ds * silu).astype(BF16)

    act = jax.ShapeDtypeStruct((nsh, T, Fs), BF16)
    act_spec = pl.BlockSpec((1, tm, Fs), lambda j, i: (j, i, 0))
    return _call(
        body, hosted, name=name, grid=(nsh, T // tm),
        in_specs=[pl.BlockSpec((tm, D), lambda j, i: (i, 0)), pl.BlockSpec((1, Fs, D), lambda j, i: (j, 0, 0)), act_spec, act_spec],
        out_specs=[act_spec, act_spec],
        out_shape=[act, act],
        args=[dh, w2g, a, b])


def ffn_dw2(s, dh, name, hosted=()):
    nsh, T, Fs = s.shape
    D = dh.shape[1]
    tk = min(T, 512)
    nk = T // tk

    def body(s_ref, dh_ref, o_ref, acc):
        k = pl.program_id(1)

        @pl.when(k == 0)
        def _():
            acc[...] = jnp.zeros_like(acc)

        acc[...] += _dot_tn(s_ref[0], (0.5 * dh_ref[...]).astype(BF16))

        @pl.when(k == nk - 1)
        def _():
            o_ref[0] = acc[...].astype(BF16)

    return _call(
        body, hosted, name=name, grid=(nsh, nk),
        in_specs=[pl.BlockSpec((1, tk, Fs), lambda j, k: (j, k, 0)), pl.BlockSpec((tk, D), lambda j, k: (k, 0))],
        out_specs=[pl.BlockSpec((1, Fs, D), lambda j, k: (j, 0, 0))],
        out_shape=[jax.ShapeDtypeStruct((nsh, Fs, D), BF16)],
        scratch_shapes=[pltpu.VMEM((Fs, D), F32)],
        args=[s, dh])


def ffn_dw13(n, da, db, name, hosted=()):
    T, D = n.shape
    nsh, _, Fs = da.shape
    tk = min(T, 512)
    nk = T // tk

    def body(n_ref, da_ref, db_ref, o1_ref, o3_ref, acc1, acc3):
        k = pl.program_id(1)

        @pl.when(k == 0)
        def _():
            acc1[...] = jnp.zeros_like(acc1)
            acc3[...] = jnp.zeros_like(acc3)

        nn = n_ref[...]
        acc1[...] += _dot_tn(nn, da_ref[0])
        acc3[...] += _dot_tn(nn, db_ref[0])

        @pl.when(k == nk - 1)
        def _():
            o1_ref[0] = acc1[...].astype(BF16)
            o3_ref[0] = acc3[...].astype(BF16)

    act_spec = pl.BlockSpec((1, tk, Fs), lambda j, k: (j, k, 0))
    out = jax.ShapeDtypeStruct((nsh, D, Fs), BF16)
    out_spec = pl.BlockSpec((1, D, Fs), lambda j, k: (j, 0, 0))
    return _call(
        body, hosted, name=name, grid=(nsh, nk),
        in_specs=[pl.BlockSpec((tk, D), lambda j, k: (k, 0)), act_spec, act_spec],
        out_specs=[out_spec, out_spec],
        out_shape=[out, out],
        scratch_shapes=[pltpu.VMEM((D, Fs), F32), pltpu.VMEM((D, Fs), F32)],
        args=[n, da, db])


def ffn_bwd_in(da, db, w1g, w3g, h, gain, dh, name, hosted=()):
    nsh, T, Fs = da.shape
    D = h.shape[1]
    tm = min(T, 256)

    def body(da_ref, db_ref, w1_ref, w3_ref, h_ref, g_ref, dh_ref, o_ref, dg_ref):
        dn = _dot_nt(da_ref[0], w1_ref[0]) + _dot_nt(db_ref[0], w3_ref[0])
        for j in range(1, nsh):
            dn += _dot_nt(da_ref[j], w1_ref[j]) + _dot_nt(db_ref[j], w3_ref[j])
        dhn, dg = _rmsnorm_bwd(dn, h_ref[...], g_ref[...])
        o_ref[...] = dh_ref[...] + dhn

        @pl.when(pl.program_id(0) == 0)
        def _():
            dg_ref[...] = jnp.zeros_like(dg_ref)

        dg_ref[...] += jnp.sum(dg, axis=0, keepdims=True)

    act_spec = pl.BlockSpec((nsh, tm, Fs), lambda i: (0, i, 0))
    w_spec = pl.BlockSpec((nsh, D, Fs), lambda i: (0, 0, 0))
    row_spec = pl.BlockSpec((tm, D), lambda i: (i, 0))
    vec_spec = pl.BlockSpec((1, D), lambda i: (0, 0))
    return _call(
        body, hosted, name=name, grid=(T // tm,),
        in_specs=[act_spec, act_spec, w_spec, w_spec, row_spec, vec_spec, row_spec],
        out_specs=[row_spec, vec_spec],
        out_shape=[jax.ShapeDtypeStruct((T, D), F32), jax.ShapeDtypeStruct((1, D), F32)],
        args=[da, db, w1g, w3g, h, gain, dh])


def mix_in(h, gain, wing, name, hosted=()):
    T, D = h.shape
    nsh, _, Cs = wing.shape
    tm = min(T, 512)

    def body(h_ref, g_ref, w_ref, u_ref, p_ref):
        hh = h_ref[...]
        u = (hh * _rstd(hh) * g_ref[...]).astype(BF16)
        u_ref[...] = u
        for j in range(nsh):
            p_ref[:, j * Cs:(j + 1) * Cs] = _dot(u, w_ref[j])

    return _call(
        body, hosted, name=name, grid=(T // tm,),
        in_specs=[pl.BlockSpec((tm, D), lambda i: (i, 0)), pl.BlockSpec((1, D), lambda i: (0, 0)),
                  pl.BlockSpec((nsh, D, Cs), lambda i: (0, 0, 0))],
        out_specs=[pl.BlockSpec((tm, D), lambda i: (i, 0)), pl.BlockSpec((tm, nsh * Cs), lambda i: (i, 0))],
        out_shape=[jax.ShapeDtypeStruct((T, D), BF16), jax.ShapeDtypeStruct((T, nsh * Cs), F32)],
        args=[h, gain, wing])


def mix_out(a, b, woutg, h, name, hosted=()):
    T, W = a.shape
    D = h.shape[1]
    wout = woutg.reshape(2, W, D)
    tm = min(T, 512)

    def body(a_ref, b_ref, w_ref, h_ref, o_ref):
        o_ref[...] = h_ref[...] + _dot(a_ref[...], w_ref[0]) + _dot(b_ref[...], w_ref[1])

    return _call(
        body, hosted, name=name, grid=(T // tm,),
        in_specs=[pl.BlockSpec((tm, W), lambda i: (i, 0)), pl.BlockSpec((tm, W), lambda i: (i, 0)),
                  pl.BlockSpec((2, W, D), lambda i: (0, 0, 0)), pl.BlockSpec((tm, D), lambda i: (i, 0))],
        out_specs=[pl.BlockSpec((tm, D), lambda i: (i, 0))],
        out_shape=[jax.ShapeDtypeStruct((T, D), F32)],
        args=[a, b, wout, h])


def mix_out_bwd(dh, woutg, a, b, name, hosted=()):
    T, D = dh.shape
    W = a.shape[1]
    nsh, Rs, _ = woutg.shape
    wout = woutg.reshape(2, W, D)
    tk = min(T, 512)
    nk = T // tk

    def body(dh_ref, w_ref, a_ref, b_ref, da_ref, db_ref, dw_ref, acc):
        k = pl.program_id(0)

        @pl.when(k == 0)
        def _():
            acc[...] = jnp.zeros_like(acc)

        dhb = dh_ref[...].astype(BF16)
        da_ref[...] = _dot_nt(dhb, w_ref[0])
        db_ref[...] = _dot_nt(dhb, w_ref[1])
        acc[0:W, :] += _dot_tn(a_ref[...], dhb)
        acc[W:2 * W, :] += _dot_tn(b_ref[...], dhb)

        @pl.when(k == nk - 1)
        def _():
            for j in range(nsh):
                dw_ref[j] = acc[j * Rs:(j + 1) * Rs, :].astype(BF16)

    return _call(
        body, hosted, name=name, grid=(nk,),
        in_specs=[pl.BlockSpec((tk, D), lambda k: (k, 0)), pl.BlockSpec((2, W, D), lambda k: (0, 0, 0)),
                  pl.BlockSpec((tk, W), lambda k: (k, 0)), pl.BlockSpec((tk, W), lambda k: (k, 0))],
        out_specs=[pl.BlockSpec((tk, W), lambda k: (k, 0)), pl.BlockSpec((tk, W), lambda k: (k, 0)),
                   pl.BlockSpec((nsh, Rs, D), lambda k: (0, 0, 0))],
        out_shape=[jax.ShapeDtypeStruct((T, W), F32), jax.ShapeDtypeStruct((T, W), F32),
                   jax.ShapeDtypeStruct((nsh, Rs, D), BF16)],
        scratch_shapes=[pltpu.VMEM((2 * W, D), F32)],
        args=[dh, wout, a, b])


def mix_dwin(u, d, nsh, name, hosted=()):
    T, D = u.shape
    Cs = d.shape[1] // nsh
    tk = min(T, 512)
    nk = T // tk

    def body(u_ref, d_ref, o_ref, acc):
        k = pl.program_id(1)

        @pl.when(k == 0)
        def _():
            acc[...] = jnp.zeros_like(acc)

        acc[...] += _dot_tn(u_ref[...], d_ref[...])

        @pl.when(k == nk - 1)
        def _():
            o_ref[0] = acc[...].astype(BF16)

    return _call(
        body, hosted, name=name, grid=(nsh, nk),
        in_specs=[pl.BlockSpec((tk, D), lambda j, k: (k, 0)), pl.BlockSpec((tk, Cs), lambda j, k: (k, j))],
        out_specs=[pl.BlockSpec((1, D, Cs), lambda j, k: (j, 0, 0))],
        out_shape=[jax.ShapeDtypeStruct((nsh, D, Cs), BF16)],
        scratch_shapes=[pltpu.VMEM((D, Cs), F32)],
        args=[u, d])


def mix_in_bwd(d, wing, h, gain, dh, name, hosted=()):
    T, D = h.shape
    nsh, _, Cs = wing.shape
    tm = min(T, 512)

    def body(d_ref, w_ref, h_ref, g_ref, dh_ref, o_ref, dg_ref):
        du = _dot_nt(d_ref[:, 0:Cs], w_ref[0])
        for j in range(1, nsh):
            du += _dot_nt(d_ref[:, j * Cs:(j + 1) * Cs], w_ref[j])
        dhn, dg = _rmsnorm_bwd(du, h_ref[...], g_ref[...])
        o_ref[...] = dh_ref[...] + dhn

        @pl.when(pl.program_id(0) == 0)
        def _():
            dg_ref[...] = jnp.zeros_like(dg_ref)

        dg_ref[...] += jnp.sum(dg, axis=0, keepdims=True)

    row_spec = pl.BlockSpec((tm, D), lambda i: (i, 0))
    vec_spec = pl.BlockSpec((1, D), lambda i: (0, 0))
    return _call(
        body, hosted, name=name, grid=(T // tm,),
        in_specs=[pl.BlockSpec((tm, nsh * Cs), lambda i: (i, 0)), pl.BlockSpec((nsh, D, Cs), lambda i: (0, 0, 0)),
                  row_spec, vec_spec, row_spec],
        out_specs=[row_spec, vec_spec],
        out_shape=[jax.ShapeDtypeStruct((T, D), F32), jax.ShapeDtypeStruct((1, D), F32)],
        args=[d, wing, h, gain, dh])


def _pool_window(x, group, T, trailing):
    rows = lax.broadcasted_iota(jnp.int32, x.shape, 0)

    def shifted(z, k):
        if trailing:
            return jnp.where(rows >= k, pltpu.roll(z, k, 0), 0.0)
        return jnp.where(rows < T - k, pltpu.roll(z, T - k, 0), 0.0)

    s2 = x + shifted(x, 1)
    s4 = s2 + shifted(s2, 2)
    s8 = s4 + shifted(s4, 4)
    s16 = s8 + shifted(s8, 8)
    return jnp.where(group == 0, s2, jnp.where(group == 1, s4, jnp.where(group == 2, s8, s16)))


def _pool_count(group, shape):
    rows = lax.broadcasted_iota(jnp.int32, shape, 0)
    w = jnp.where(group == 0, 2, jnp.where(group == 1, 4, jnp.where(group == 2, 8, 16)))
    return jnp.minimum(rows + 1, w).astype(F32)


def pool_fwd(proj, pool_w, pool_scale, name, hosted=()):
    T = proj.shape[0]
    Hd = HEAD_DIM

    def body(x_ref, w_ref, sc_ref, a_ref):
        g = pl.program_id(0)
        x = x_ref[...]
        pooled = _pool_window(x, g, T, True) / _pool_count(g, x.shape) - x
        a_ref[...] = (_dot(pooled.astype(BF16), w_ref[0].astype(BF16)) * sc_ref[...]).astype(BF16)

    return _call(
        body, hosted, name=name, grid=(N_GROUPS,),
        in_specs=[pl.BlockSpec((T, Hd), lambda g: (0, g)), pl.BlockSpec((1, Hd, Hd), lambda g: (g, 0, 0)),
                  pl.BlockSpec((1, Hd), lambda g: (0, g))],
        out_specs=[pl.BlockSpec((T, Hd), lambda g: (0, g))],
        out_shape=[jax.ShapeDtypeStruct((T, N_GROUPS * Hd), BF16)],
        args=[proj, pool_w, pool_scale])


def pool_bwd(proj, da, pool_w, pool_scale, name, hosted=()):
    T = proj.shape[0]
    Hd = HEAD_DIM

    def body(x_ref, da_ref, w_ref, sc_ref, dx_ref, dw_ref, dsc_ref):
        g = pl.program_id(0)
        x = x_ref[...]
        cnt = _pool_count(g, x.shape)
        pooled = (_pool_window(x, g, T, True) / cnt - x).astype(BF16)
        wb = w_ref[0].astype(BF16)
        dav = da_ref[...]
        dsc_ref[...] = jnp.sum(dav * _dot(pooled, wb), axis=0, keepdims=True)
        dout = (dav * sc_ref[...]).astype(BF16)
        dw_ref[0] = _dot_tn(pooled, dout)
        dpooled = _dot_nt(dout, wb)
        dx_ref[...] = (_pool_window(dpooled / cnt, g, T, False) - dpooled).astype(BF16)

    col_spec = pl.BlockSpec((T, Hd), lambda g: (0, g))
    return _call(
        body, hosted, name=name, grid=(N_GROUPS,),
        in_specs=[col_spec, col_spec, pl.BlockSpec((1, Hd, Hd), lambda g: (g, 0, 0)), pl.BlockSpec((1, Hd), lambda g: (0, g))],
        out_specs=[col_spec, pl.BlockSpec((1, Hd, Hd), lambda g: (g, 0, 0)), pl.BlockSpec((1, Hd), lambda g: (0, g))],
        out_shape=[jax.ShapeDtypeStruct((T, N_GROUPS * Hd), BF16), jax.ShapeDtypeStruct((N_GROUPS, Hd, Hd), F32),
                   jax.ShapeDtypeStruct((1, N_GROUPS * Hd), F32)],
        args=[proj, da, pool_w, pool_scale])


def _ret_tables(T):
    Hd, C = HEAD_DIM, RET_CHUNK
    inv_freq = 1.0 / (ROPE_BASE ** (jnp.arange(0, Hd, 2, dtype=F32) / Hd))
    ang = jnp.arange(T, dtype=F32)[:, None] * inv_freq[None, :]
    cos, sin = jnp.cos(ang), jnp.sin(ang)
    cos2 = jnp.concatenate([cos, cos], axis=-1)
    sin2 = jnp.concatenate([-sin, sin], axis=-1)
    log_gamma = jnp.log1p(-jnp.exp2(-5.0 - jnp.arange(N_GROUPS, dtype=F32)))
    pos = jnp.arange(C, dtype=F32)
    rel = pos[:, None] - pos[None, :]
    intra = jnp.where(rel[None] >= 0, jnp.exp(log_gamma[:, None, None] * jnp.maximum(rel, 0.0)[None]), 0.0)
    k_tail = jnp.exp(log_gamma[:, None] * (C - 1 - pos)[None, :])
    q_head = jnp.exp(log_gamma[:, None] * (pos + 1.0)[None, :])
    chunk_decay = jnp.exp(log_gamma * C)
    wide = lambda t: jnp.broadcast_to(t[:, :, None], (N_GROUPS, C, Hd))
    return cos2, sin2, intra, wide(k_tail), wide(q_head), jnp.broadcast_to(chunk_decay[:, None, None], (N_GROUPS, 1, Hd))


def _rope(x, cos2, sin2):
    return x * cos2 + pltpu.roll(x, HEAD_DIM // 2, 1) * sin2


def _rope_t(d, cos2, sin2):
    return d * cos2 + pltpu.roll(d * sin2, HEAD_DIM // 2, 1)


def _ret_specs(tseg, seg_of):
    Hd, G = HEAD_DIM, N_GROUPS
    col = lambda kind: pl.BlockSpec((tseg, Hd), lambda h, s: (seg_of(s), G * kind + h))
    tab = pl.BlockSpec((tseg, Hd), lambda h, s: (seg_of(s), 0))
    head = pl.BlockSpec((1, RET_CHUNK, Hd), lambda h, s: (h, 0, 0))
    cd = pl.BlockSpec((1, 1, Hd), lambda h, s: (h, 0, 0))
    gain = pl.BlockSpec((1, Hd), lambda h, s: (0, h))
    return col, tab, head, cd, gain


def ret_fwd(proj, ret_norm, tables, name, hosted=()):
    T = proj.shape[0]
    Hd, C, G = HEAD_DIM, RET_CHUNK, N_GROUPS
    tseg = min(T, 1024)
    nseg, nck = T // tseg, tseg // C
    scale = Hd ** -0.5
    cos2, sin2, intra, k_tail, q_head, chunk_decay = tables

    def body(q_ref, k_ref, v_ref, g_ref, gain_ref, cos_ref, sin_ref, m_ref, kt_ref, qh_ref, cd_ref,
             b_ref, o_ref, rp_ref, state):
        @pl.when(pl.program_id(1) == 0)
        def _():
            state[...] = jnp.zeros_like(state)

        def chunk(ci, carry):
            rows = pl.ds(pl.multiple_of(ci * C, C), C)
            cos, sin = cos_ref[rows, :], sin_ref[rows, :]
            qr = _rope(q_ref[rows, :], cos, sin)
            kr = _rope(k_ref[rows, :], cos, sin) * scale
            qb, kb, vb = qr.astype(BF16), kr.astype(BF16), v_ref[rows, :].astype(BF16)
            r = state[...]
            rp_ref[0, ci] = r.astype(BF16)
            sc = _dot_nt(qb, kb) * m_ref[0]
            o = _dot(sc.astype(BF16), vb) + _dot((qr * qh_ref[0]).astype(BF16), r.astype(BF16))
            state[...] = cd_ref[0] * r + _dot_tn((kr * kt_ref[0]).astype(BF16), vb)
            o_ref[rows, :] = o
            on = o * _rstd(o)
            b_ref[rows, :] = (jax.nn.silu(g_ref[rows, :]) * (on * gain_ref[...])).astype(BF16)
            return carry

        lax.fori_loop(0, nck, chunk, 0)

    col, tab, head, cd, gain = _ret_specs(tseg, lambda s: s)
    out_col = pl.BlockSpec((tseg, Hd), lambda h, s: (s, h))
    return _call(
        body, hosted, name=name, grid=(G, nseg),
        in_specs=[col(1), col(2), col(3), col(4), gain, tab, tab, head, head, head, cd],
        out_specs=[out_col, out_col, pl.BlockSpec((1, nck, Hd, Hd), lambda h, s: (h, s, 0, 0))],
        out_shape=[jax.ShapeDtypeStruct((T, G * Hd), BF16), jax.ShapeDtypeStruct((T, G * Hd), F32),
                   jax.ShapeDtypeStruct((G, T // C, Hd, Hd), BF16)],
        scratch_shapes=[pltpu.VMEM((Hd, Hd), F32)],
        args=[proj, proj, proj, proj, ret_norm, cos2, sin2, intra, k_tail, q_head, chunk_decay])


def ret_bwd(proj, db, o_pre, r_prev, ret_norm, tables, name, hosted=()):
    T = proj.shape[0]
    Hd, C, G = HEAD_DIM, RET_CHUNK, N_GROUPS
    tseg = min(T, 1024)
    nseg, nck = T // tseg, tseg // C
    scale = Hd ** -0.5
    cos2, sin2, intra, k_tail, q_head, chunk_decay = tables

    def body(q_ref, k_ref, v_ref, g_ref, db_ref, o_ref, rp_ref, gain_ref, cos_ref, sin_ref, m_ref, kt_ref, qh_ref, cd_ref,
             d_ref, dgain_ref, gstate):
        @pl.when(pl.program_id(1) == 0)
        def _():
            gstate[...] = jnp.zeros_like(gstate)
            dgain_ref[...] = jnp.zeros_like(dgain_ref)

        def chunk(t, carry):
            ci = nck - 1 - t
            rows = pl.ds(pl.multiple_of(ci * C, C), C)
            cos, sin = cos_ref[rows, :], sin_ref[rows, :]
            qr = _rope(q_ref[rows, :], cos, sin)
            kr = _rope(k_ref[rows, :], cos, sin) * scale
            qb, kb, vb = qr.astype(BF16), kr.astype(BF16), v_ref[rows, :].astype(BF16)
            qhb, ktb = (qr * qh_ref[0]).astype(BF16), (kr * kt_ref[0]).astype(BF16)
            sc = (_dot_nt(qb, kb) * m_ref[0]).astype(BF16)
            o = o_ref[rows, :]
            rstd = _rstd(o)
            on = o * rstd
            gain = gain_ref[...]
            silu, dsilu = _silu_parts(g_ref[rows, :])
            dy = db_ref[rows, :]
            dgain_ref[...] += jnp.sum(dy * silu * on, axis=0, keepdims=True)
            dg = dy * on * gain * dsilu
            don = dy * silu * gain
            dob = (rstd * (don - on * jnp.mean(don * on, axis=-1, keepdims=True))).astype(BF16)
            gn = gstate[...]
            gb = gn.astype(BF16)
            da = (_dot_nt(dob, vb) * m_ref[0]).astype(BF16)
            dq = _dot(da, kb) + _dot_nt(dob, rp_ref[0, ci]) * qh_ref[0]
            dk = _dot_tn(da, qb) + _dot_nt(vb, gb) * kt_ref[0]
            dv = _dot_tn(sc, dob) + _dot(ktb, gb)
            gstate[...] = cd_ref[0] * gn + _dot_tn(qhb, dob)
            d_ref[0, rows, :] = _rope_t(dq, cos, sin).astype(BF16)
            d_ref[1, rows, :] = _rope_t(dk * scale, cos, sin).astype(BF16)
            d_ref[2, rows, :] = dv.astype(BF16)
            d_ref[3, rows, :] = dg.astype(BF16)
            return carry

        lax.fori_loop(0, nck, chunk, 0)

    rev = lambda s: nseg - 1 - s
    col, tab, head, cd, gain = _ret_specs(tseg, rev)
    act = pl.BlockSpec((tseg, Hd), lambda h, s: (rev(s), h))
    return _call(
        body, hosted, name=name, grid=(G, nseg),
        in_specs=[col(1), col(2), col(3), col(4), act, act, pl.BlockSpec((1, nck, Hd, Hd), lambda h, s: (h, rev(s), 0, 0)),
                  gain, tab, tab, head, head, head, cd],
        out_specs=[pl.BlockSpec((4, tseg, Hd), lambda h, s: (0, rev(s), h)), gain],
        out_shape=[jax.ShapeDtypeStruct((4, T, G * Hd), BF16), jax.ShapeDtypeStruct((1, G * Hd), F32)],
        scratch_shapes=[pltpu.VMEM((Hd, Hd), F32)],
        args=[proj, proj, proj, proj, db, o_pre, r_prev, ret_norm, cos2, sin2, intra, k_tail, q_head, chunk_decay])


def final_loss(h, gain, target, name, hosted=()):
    T, D = h.shape
    tm = min(T, 512)

    def body(h_ref, g_ref, t_ref, dh_ref, loss_ref, dg_ref):
        @pl.when(pl.program_id(0) == 0)
        def _():
            loss_ref[...] = jnp.zeros_like(loss_ref)
            dg_ref[...] = jnp.zeros_like(dg_ref)

        hh = h_ref[...]
        gain_v = g_ref[...]
        err = hh * _rstd(hh) * gain_v - t_ref[...]
        loss_ref[...] += 0.5 * jnp.sum(jnp.mean(err * err, axis=-1, keepdims=True), axis=0, keepdims=True)
        dhn, dg = _rmsnorm_bwd(err * (1.0 / D), hh, gain_v)
        dh_ref[...] = dhn
        dg_ref[...] += jnp.sum(dg, axis=0, keepdims=True)

    row_spec = pl.BlockSpec((tm, D), lambda i: (i, 0))
    vec_spec = pl.BlockSpec((1, D), lambda i: (0, 0))
    return _call(
        body, hosted, name=name, grid=(T // tm,),
        in_specs=[row_spec, vec_spec, row_spec],
        out_specs=[row_spec, pl.BlockSpec((1, 128), lambda i: (0, 0)), vec_spec],
        out_shape=[jax.ShapeDtypeStruct((T, D), F32), jax.ShapeDtypeStruct((1, 128), F32), jax.ShapeDtypeStruct((1, D), F32)],
        args=[h, gain, target])


def prereduce(grad, recv, core, name):
    nsh, R, C = grad.shape
    rh = R // 2

    def body(c_ref, g_ref, r_ref, o_ref):
        o_ref[...] = (g_ref[...].astype(F32) + r_ref[...].astype(F32)).astype(BF16)

    return pl.pallas_call(
        body, name=name,
        grid_spec=pltpu.PrefetchScalarGridSpec(
            num_scalar_prefetch=1, grid=(nsh,),
            in_specs=[pl.BlockSpec((1, rh, C), lambda j, c_ref: (j, c_ref[0], 0)), pl.BlockSpec((1, rh, C), lambda j, c_ref: (j, 0, 0))],
            out_specs=pl.BlockSpec((1, rh, C), lambda j, c_ref: (j, 0, 0))),
        out_shape=jax.ShapeDtypeStruct((nsh, rh, C), BF16),
        compiler_params=pltpu.CompilerParams(vmem_limit_bytes=VMEM_LIMIT_V7X),
    )(core, grad, recv)


def _adamw(w, g, m, v):
    m = ADAM_B1 * m + (1.0 - ADAM_B1) * g
    v = ADAM_B2 * v + (1.0 - ADAM_B2) * (g * g)
    m_hat = m / (1.0 - ADAM_B1 ** ADAM_STEP)
    v_hat = v / (1.0 - ADAM_B2 ** ADAM_STEP)
    return -ADAM_LR * (m_hat / (jnp.sqrt(v_hat) + ADAM_EPS) + ADAM_WD * w), m, v


def adamw_sharded(tensors, name, hosted=()):
    nt = len(tensors)
    nsh, R, C = tensors[0][0].shape
    tr = 256 if R % 256 == 0 else R // 2

    def body(*refs):
        ins, outs = refs[:4 * nt], refs[4 * nt:]
        for t in range(nt):
            p_ref, w_ref, m_ref, v_ref = ins[4 * t:4 * t + 4]
            g_ref, d_ref, nm_ref, nv_ref = outs[4 * t:4 * t + 4]
            g = p_ref[0].astype(F32)
            for i in range(1, nsh):
                g += p_ref[i].astype(F32)
            g_ref[...] = g
            d_ref[...], nm_ref[...], nv_ref[...] = _adamw(w_ref[...], g, m_ref[...], v_ref[...])

    spec = pl.BlockSpec((tr, C), lambda i: (i, 0))
    out = jax.ShapeDtypeStruct((R, C), F32)
    return _call(
        body, hosted, name=name, grid=(R // tr,),
        in_specs=[pl.BlockSpec((nsh, tr, C), lambda i: (0, i, 0)), spec, spec, spec] * nt,
        out_specs=[spec] * (4 * nt), out_shape=[out] * (4 * nt),
        args=[a for tensor in tensors for a in tensor])


def adamw_small(packs, w, m, v, name):
    ndev, R, L = packs.shape

    def body(p_ref, w_ref, m_ref, v_ref, g_ref, d_ref, nm_ref, nv_ref):
        g = p_ref[0]
        for i in range(1, ndev):
            g += p_ref[i]
        g_ref[...] = g
        d_ref[...], nm_ref[...], nv_ref[...] = _adamw(w_ref[...], g, m_ref[...], v_ref[...])

    out = jax.ShapeDtypeStruct((R, L), F32)
    return pl.pallas_call(body, name=name, out_shape=[out] * 4,
                          compiler_params=pltpu.CompilerParams(vmem_limit_bytes=VMEM_LIMIT_V7X))(packs, w, m, v)


BIG = ("ffn1_w1", "ffn1_w3", "ffn1_w2", "w_in", "w_out", "ffn2_w1", "ffn2_w3", "ffn2_w2")
SMALL = ("ffn1_norm", "mix_norm", "pool_w", "pool_scale", "ret_norm", "ffn2_norm", "final_norm")
WEIGHTS = ("ffn1_norm", "ffn1_w1", "ffn1_w3", "ffn1_w2", "mix_norm", "w_in", "pool_w", "pool_scale", "ret_norm", "w_out",
           "ffn2_norm", "ffn2_w1", "ffn2_w3", "ffn2_w2", "final_norm")


def _pack(parts):
    return jnp.concatenate([parts[k].reshape(-1, 128) for k in SMALL], axis=0)


def _unpack(pack, like):
    out, row = {}, 0
    for k in SMALL:
        rows = like[k].size // 128
        out[k] = pack[row:row + rows].reshape(like[k].shape)
        row += rows
    return out


def kernel(x, ffn1_norm, ffn1_w1, ffn1_w3, ffn1_w2, mix_norm, w_in, pool_w, pool_scale, ret_norm, w_out, ffn2_norm, ffn2_w1, ffn2_w3, ffn2_w2, final_norm, loss_target, m_ffn1_norm, m_ffn1_w1, m_ffn1_w3, m_ffn1_w2, m_mix_norm, m_w_in, m_pool_w, m_pool_scale, m_ret_norm, m_w_out, m_ffn2_norm, m_ffn2_w1, m_ffn2_w3, m_ffn2_w2, m_final_norm, v_ffn1_norm, v_ffn1_w1, v_ffn1_w3, v_ffn1_w2, v_mix_norm, v_w_in, v_pool_w, v_pool_scale, v_ret_norm, v_w_out, v_ffn2_norm, v_ffn2_w1, v_ffn2_w3, v_ffn2_w2, v_final_norm):
    w = dict(ffn1_norm=ffn1_norm, ffn1_w1=ffn1_w1, ffn1_w3=ffn1_w3, ffn1_w2=ffn1_w2, mix_norm=mix_norm, w_in=w_in, pool_w=pool_w,
             pool_scale=pool_scale, ret_norm=ret_norm, w_out=w_out, ffn2_norm=ffn2_norm, ffn2_w1=ffn2_w1, ffn2_w3=ffn2_w3,
             ffn2_w2=ffn2_w2, final_norm=final_norm)
    m = dict(ffn1_norm=m_ffn1_norm, ffn1_w1=m_ffn1_w1, ffn1_w3=m_ffn1_w3, ffn1_w2=m_ffn1_w2, mix_norm=m_mix_norm, w_in=m_w_in,
             pool_w=m_pool_w, pool_scale=m_pool_scale, ret_norm=m_ret_norm, w_out=m_w_out, ffn2_norm=m_ffn2_norm, ffn2_w1=m_ffn2_w1,
             ffn2_w3=m_ffn2_w3, ffn2_w2=m_ffn2_w2, final_norm=m_final_norm)
    v = dict(ffn1_norm=v_ffn1_norm, ffn1_w1=v_ffn1_w1, ffn1_w3=v_ffn1_w3, ffn1_w2=v_ffn1_w2, mix_norm=v_mix_norm, w_in=v_w_in,
             pool_w=v_pool_w, pool_scale=v_pool_scale, ret_norm=v_ret_norm, w_out=v_w_out, ffn2_norm=v_ffn2_norm, ffn2_w1=v_ffn2_w1,
             ffn2_w3=v_ffn2_w3, ffn2_w2=v_ffn2_w2, final_norm=v_final_norm)
    xs, target = x[0], loss_target[0]
    T = xs.shape[0]
    tables = _ret_tables(T)
    core = lax.axis_index("c").astype(jnp.int32).reshape(1)
    sh = {k: w[k][0].astype(BF16) for k in BIG}
    gather = lambda *names: [ChipExchange([sh[k] for k in names], False)]
    wg, grad, delta, new_m, new_v = {}, {}, {}, {}, {}

    def update(names, pieces, name, hosted=()):
        outs, extras = adamw_sharded([(p, w[k][0], m[k][0], v[k][0]) for k, p in zip(names, pieces)], name, hosted)
        for t, k in enumerate(names):
            grad[k], delta[k], new_m[k], new_v[k] = [o[None] for o in outs[4 * t:4 * t + 4]]
        return extras

    def reduce_in_chip(name, partial, recv):
        return prereduce(partial, recv, core, "prereduce_" + name)

    (wg["ffn1_w1"], wg["ffn1_w3"]), = exchange(gather("ffn1_w1", "ffn1_w3"), "gather_ffn1")
    (n1, a1, b1, s1), ((wg["ffn1_w2"], wg["w_in"]),) = ffn_up(
        xs, ffn1_norm, wg["ffn1_w1"], wg["ffn1_w3"], "ffn1_up", gather("ffn1_w2", "w_in"))
    (h1,), ((wg["w_out"],),) = ffn_down(s1, wg["ffn1_w2"], xs, "ffn1_down", gather("w_out"))
    (u, proj), ((wg["ffn2_w1"],),) = mix_in(h1, mix_norm, wg["w_in"], "mix_in", gather("ffn2_w1"))
    (pa,), _ = pool_fwd(proj, pool_w[0], pool_scale, "pool_fwd")
    (rb, o_pre, r_prev), ((wg["ffn2_w3"],),) = ret_fwd(proj, ret_norm, tables, "ret_fwd", gather("ffn2_w3"))
    (h2,), _ = mix_out(pa, rb, wg["w_out"], h1, "mix_out")
    (n2, a2, b2, s2), ((wg["ffn2_w2"],),) = ffn_up(
        h2, ffn2_norm, wg["ffn2_w1"], wg["ffn2_w3"], "ffn2_up", gather("ffn2_w2"))
    (h3,), _ = ffn_down(s2, wg["ffn2_w2"], h2, "ffn2_down")
    (dh3, loss, d_final), _ = final_loss(h3, final_norm[None], target, "final_loss")
    loss = lax.psum(loss[0, 0], ("x", "y", "c"))

    (da2, db2), _ = ffn_bwd_act(dh3, wg["ffn2_w2"], a2, b2, "ffn2_bwd_act")
    (g_f2w2,), _ = ffn_dw2(s2, dh3, "ffn2_dw2")
    (g_f2w1, g_f2w3), ((r_f2w2,),) = ffn_dw13(n2, da2, db2, "ffn2_dw13", [SiblingExchange([g_f2w2])])
    p_f2w2 = reduce_in_chip("ffn2_w2", g_f2w2, r_f2w2)
    (dh2, d_ffn2), ((q_f2w2,), (r_f2w1, r_f2w3)) = ffn_bwd_in(
        da2, db2, wg["ffn2_w1"], wg["ffn2_w3"], h2, ffn2_norm, dh3, "ffn2_bwd_in",
        [ChipExchange([p_f2w2], True), SiblingExchange([g_f2w1, g_f2w3])])
    p_f2w1 = reduce_in_chip("ffn2_w1", g_f2w1, r_f2w1)
    p_f2w3 = reduce_in_chip("ffn2_w3", g_f2w3, r_f2w3)
    (dpa, drb, g_wout), ((q_f2w1,),) = mix_out_bwd(dh2, wg["w_out"], pa, rb, "mix_out_bwd", [ChipExchange([p_f2w1], True)])
    (dpool, d_pool_w, d_pool_scale), _ = pool_bwd(proj, dpa, pool_w[0], pool_scale, "pool_bwd")
    (dqkvg, d_ret_norm), ((q_f2w3,), (r_wout,)) = ret_bwd(
        proj, drb, o_pre, r_prev, ret_norm, tables, "ret_bwd", [ChipExchange([p_f2w3], True), SiblingExchange([g_wout])])
    p_wout = reduce_in_chip("w_out", g_wout, r_wout)
    d = jnp.concatenate([dpool, dqkvg[0], dqkvg[1], dqkvg[2], dqkvg[3]], axis=1)
    (g_win,), ((q_wout,),) = mix_dwin(u, d, N_CHIPS, "mix_dwin", [ChipExchange([p_wout], True)])
    (dh1, d_mix), ((r_win,),) = mix_in_bwd(d, wg["w_in"], h1, mix_norm, dh2, "mix_in_bwd", [SiblingExchange([g_win])])
    p_win = reduce_in_chip("w_in", g_win, r_win)
    (da1, db1), ((q_win,),) = ffn_bwd_act(dh1, wg["ffn1_w2"], a1, b1, "ffn1_bwd_act", [ChipExchange([p_win], True)])
    (g_f1w1, g_f1w3), _ = ffn_dw13(n1, da1, db1, "ffn1_dw13")
    (g_f1w2,), ((r_f1w1, r_f1w3),) = ffn_dw2(s1, dh1, "ffn1_dw2", [SiblingExchange([g_f1w1, g_f1w3])])
    p_f1w1 = reduce_in_chip("ffn1_w1", g_f1w1, r_f1w1)
    p_f1w3 = reduce_in_chip("ffn1_w3", g_f1w3, r_f1w3)
    (dx, d_ffn1), ((q_f1w1, q_f1w3), (r_f1w2,)) = ffn_bwd_in(
        da1, db1, wg["ffn1_w1"], wg["ffn1_w3"], xs, ffn1_norm, dh1, "ffn1_bwd_in",
        [ChipExchange([p_f1w1, p_f1w3], True), SiblingExchange([g_f1w2])])
    p_f1w2 = reduce_in_chip("ffn1_w2", g_f1w2, r_f1w2)

    (q_f1w2,), = update(["ffn2_w1", "ffn2_w3"], [q_f2w1, q_f2w3], "adamw_ffn2_w13", [ChipExchange([p_f1w2], True)])
    update(["ffn2_w2"], [q_f2w2], "adamw_ffn2_w2")
    update(["w_in"], [q_win], "adamw_w_in")
    update(["w_out"], [q_wout], "adamw_w_out")
    update(["ffn1_w1", "ffn1_w3"], [q_f1w1, q_f1w3], "adamw_ffn1_w13")
    update(["ffn1_w2"], [q_f1w2], "adamw_ffn1_w2")

    small = {"ffn1_norm": d_ffn1, "mix_norm": d_mix, "pool_w": d_pool_w, "pool_scale": d_pool_scale,
             "ret_norm": d_ret_norm, "ffn2_norm": d_ffn2, "final_norm": d_final}
    packs = all_exchange_small(_pack(small), "gather_small")
    outs = adamw_small(packs, _pack(w), _pack(m), _pack(v), "adamw_small")
    for res, pack in zip((grad, delta, new_m, new_v), outs):
        res.update(_unpack(pack, w))

    return (loss, dx[None], *[grad[k] for k in WEIGHTS], *[delta[k] for k in WEIGHTS],
            *[new_m[k] for k in WEIGHTS], *[new_v[k] for k in WEIGHTS])
```

```python
import math

import jax
import jax.numpy as jnp
from jax import lax
from jax.experimental import pallas as pl
from jax.experimental.pallas import tpu as pltpu

F32 = jnp.float32
BF16 = jnp.bfloat16

EPS = 1e-6
N_CHIPS = 4
N_GROUPS = 4
HEAD_DIM = 128
RET_CHUNK = 128
ROPE_BASE = 10000.0
ADAM_LR, ADAM_B1, ADAM_B2, ADAM_EPS, ADAM_WD, ADAM_STEP = 0.001, 0.9, 0.999, 1e-08, 0.01, 10
VMEM_LIMIT_V7X = 56 * 1024 * 1024
MESH = pl.DeviceIdType.MESH
ANY = pl.BlockSpec(memory_space=pl.ANY)


def _dot(a, b):
    return jnp.dot(a, b, preferred_element_type=F32)


def _dot_nt(a, b):
    return lax.dot_general(a, b, (((1,), (1,)), ((), ())), preferred_element_type=F32)


def _dot_tn(a, b):
    return lax.dot_general(a, b, (((0,), (0,)), ((), ())), preferred_element_type=F32)


def _rstd(h):
    return lax.rsqrt(jnp.mean(h * h, axis=-1, keepdims=True) + EPS)


def _rmsnorm_bwd(dn, h, gain):
    r = _rstd(h)
    nh = h * r
    dnh = dn * gain
    dh = r * (dnh - nh * jnp.mean(dnh * nh, axis=-1, keepdims=True))
    return dh, dn * nh


def _silu_parts(a):
    sig = jax.nn.sigmoid(a)
    silu = a * sig
    return silu, sig + silu * (1.0 - sig)


def _mesh_pos():
    return lax.axis_index("x"), lax.axis_index("y"), lax.axis_index("c")


class ChipExchange:
    def __init__(self, srcs, scatter, north_only=False):
        assert not (scatter and north_only)
        n = len(srcs)
        self.srcs, self.scatter, self.north_only, self.n = list(srcs), scatter, north_only, n
        self.half_rows = [s.shape[1] if scatter else s.shape[0] // (1 if north_only else 2) for s in srcs]
        self.out_shape = [jax.ShapeDtypeStruct((N_CHIPS, (1 if north_only else 2) * rh, s.shape[-1]), s.dtype)
                          for s, rh in zip(srcs, self.half_rows)]
        if scatter:
            self.out_shape += [jax.ShapeDtypeStruct((2, rh // 2, s.shape[-1]), s.dtype) for s, rh in zip(srcs, self.half_rows)]
        dma = pltpu.SemaphoreType.DMA
        self.sems = [dma((n,)), dma((4 * n,)), dma((4 * n,)), dma((2 * n,)), dma((2 * n,)), dma((4 * n,)), dma((4 * n,))]

    def _copies(self, src, out, sems):
        local_sem, hop1_send, hop1_recv, hop2_send, hop2_recv, d2d_send, d2d_recv = sems
        x, y, c = _mesh_pos()
        me, dg = 2 * x + y, 2 * (1 - x) + (1 - y)
        sibling = (x, y, 1 - c)
        n = self.n
        mine, theirs = (0, 0) if self.north_only else (c, 1 - c)

        def nb(a):
            nx, ny = x ^ (1 - a), y ^ a
            return 2 * nx + ny, (nx, ny, c)

        def remote(s, d, send, recv, k, to):
            return pltpu.make_async_remote_copy(src_ref=s, dst_ref=d, send_sem=send.at[k], recv_sem=recv.at[k],
                                                device_id=to, device_id_type=MESH)

        class Copies:
            def slot(_, t, chip, unit):
                rh = self.half_rows[t]
                return out[t].at[chip, pl.ds(unit * rh, rh), :]

            def quarter(_, t, chip, q):
                qh = self.half_rows[t] // 2
                return out[t].at[chip, pl.ds(mine * 2 * qh + q * qh, qh), :]

            def local(k, t):
                if self.scatter:
                    return pltpu.make_async_copy(src[t].at[me], k.slot(t, me, mine), local_sem.at[t])
                return remote(src[t], out[t].at[me], d2d_send, d2d_recv, 4 * t + 3, sibling)

            def hop1(k, t, a, transit=False):
                rh = self.half_rows[t]
                chip, to = nb(a)
                if transit:
                    piece = src[t].at[dg, pl.ds(a * (rh // 2), rh // 2), :]
                    return remote(piece, out[n + t].at[a], hop1_send, hop1_recv, 4 * t + 2 + a, to)
                piece = src[t].at[chip] if self.scatter else src[t].at[pl.ds(mine * rh, rh), :]
                return remote(piece, k.slot(t, me, mine), hop1_send, hop1_recv, 4 * t + a, to)

            def landed1(k, t, a, transit=False):
                here = out[n + t].at[a] if transit else k.slot(t, nb(a)[0], mine)
                return remote(here, here, hop1_send, hop1_recv, 4 * t + (2 if transit else 0) + a, sibling)

            def hop2(k, t, q):
                origin, to = nb(q)[0], nb(1 - q)[1]
                piece = out[n + t].at[q] if self.scatter else k.quarter(t, origin, q)
                return remote(piece, k.quarter(t, origin, q), hop2_send, hop2_recv, 2 * t + q, to)

            def landed2(k, t, q):
                here = k.quarter(t, dg, q)
                return remote(here, here, hop2_send, hop2_recv, 2 * t + q, sibling)

            def d2d(k, t, p, chip, own=False, arriving=False):
                if arriving:
                    there = k.slot(t, chip, theirs)
                    return remote(there, there, d2d_send, d2d_recv, 4 * t + p, sibling)
                piece = src[t].at[me] if own else k.slot(t, chip, mine)
                return remote(piece, k.slot(t, chip, mine), d2d_send, d2d_recv, 4 * t + p, sibling)

        return Copies(), nb, me, dg, c

    def _by(self, sender, c, fn):
        if self.north_only:
            pl.when(c == (1 if sender else 0))(fn)
        else:
            fn()

    def start(self, src, out, sems):
        k, nb, me, dg, c = self._copies(src, out, sems)
        for t in range(self.n):
            k.local(t).start()

        def send():
            for t in range(self.n):
                for first in range(2):
                    a = first ^ c
                    k.hop1(t, a).start()
                    if self.scatter:
                        k.hop1(t, a, transit=True).start()
                if self.scatter:
                    k.d2d(t, 3, me, own=True).start()

        self._by(True, c, send)

    def mid(self, src, out, sems):
        k, nb, me, dg, c = self._copies(src, out, sems)

        def forward():
            for t in range(self.n):
                for first in range(2):
                    a = first ^ c
                    if self.scatter:
                        k.landed1(t, a, transit=True).wait_recv()
                        k.hop2(t, a).start()
                    k.landed1(t, a).wait_recv()
                    if not self.scatter:
                        k.hop2(t, a).start()
                    k.d2d(t, a, nb(a)[0]).start()

        self._by(True, c, forward)

    def finish(self, src, out, sems):
        k, nb, me, dg, c = self._copies(src, out, sems)

        def last_forward():
            for t in range(self.n):
                for q in range(2):
                    k.landed2(t, q).wait_recv()
                k.d2d(t, 2, dg).start()

        def arrivals():
            for t in range(self.n):
                for a in range(2):
                    k.d2d(t, a, nb(a)[0], arriving=True).wait_recv()
                k.d2d(t, 2, dg, arriving=True).wait_recv()
                if self.scatter:
                    k.d2d(t, 3, me, arriving=True).wait_recv()

        def sent():
            for t in range(self.n):
                for a in range(2):
                    k.hop1(t, a).wait_send()
                    if self.scatter:
                        k.hop1(t, a, transit=True).wait_send()
                    k.hop2(t, a).wait_send()
                    k.d2d(t, a, nb(a)[0]).wait_send()
                k.d2d(t, 2, dg).wait_send()
                if self.scatter:
                    k.d2d(t, 3, me, own=True).wait_send()

        self._by(True, c, last_forward)
        self._by(False, c, arrivals)
        self._by(True, c, sent)
        for t in range(self.n):
            k.local(t).wait()


class SiblingExchange:
    def __init__(self, grads):
        self.srcs, self.n = list(grads), len(grads)
        self.half_rows = [g.shape[1] // 2 for g in grads]
        self.out_shape = [jax.ShapeDtypeStruct((g.shape[0], rh, g.shape[2]), g.dtype) for g, rh in zip(grads, self.half_rows)]
        self.sems = [pltpu.SemaphoreType.DMA((self.n,)), pltpu.SemaphoreType.DMA((self.n,))]

    def _plan(self, src, out, sems):
        x, y, c = _mesh_pos()
        return [pltpu.make_async_remote_copy(
            src_ref=src[t].at[:, pl.ds((1 - c) * self.half_rows[t], self.half_rows[t]), :], dst_ref=out[t],
            send_sem=sems[0].at[t], recv_sem=sems[1].at[t], device_id=(x, y, 1 - c), device_id_type=MESH) for t in range(self.n)]

    def start(self, src, out, sems):
        for cp in self._plan(src, out, sems):
            cp.start()

    def mid(self, src, out, sems):
        pass

    def finish(self, src, out, sems):
        for cp in self._plan(src, out, sems):
            cp.wait()


def _call(body, hosted=(), *, name, in_specs, out_specs, out_shape, args, grid=(), scratch_shapes=()):
    n_in, n_out, n_scr = len(in_specs), len(out_specs), len(scratch_shapes)
    total = math.prod(grid)
    mid_step = max(0, (5 * total) // 8 - 1)

    def full(*refs):
        pos = [0]

        def take(k):
            pos[0] += k
            return refs[pos[0] - k:pos[0]]

        ins, h_in = take(n_in), [take(h.n) for h in hosted]
        outs, h_out = take(n_out), [take(len(h.out_shape)) for h in hosted]
        scr, h_sem = take(n_scr), [take(len(h.sems)) for h in hosted]
        step = 0
        for axis, size in enumerate(grid):
            step = step * size + pl.program_id(axis)

        def phase(at, method):
            if not hosted:
                return
            if total == 1:
                for h, s, o, m in zip(hosted, h_in, h_out, h_sem):
                    getattr(h, method)(s, o, m)
                return

            @pl.when(step == at)
            def _():
                for h, s, o, m in zip(hosted, h_in, h_out, h_sem):
                    getattr(h, method)(s, o, m)

        phase(0, "start")
        body(*ins, *outs, *scr)
        phase(mid_step, "mid")
        phase(total - 1, "finish")

    results = pl.pallas_call(
        full, name=name, grid=grid,
        in_specs=list(in_specs) + [ANY] * sum(h.n for h in hosted),
        out_specs=list(out_specs) + [ANY] * sum(len(h.out_shape) for h in hosted),
        out_shape=list(out_shape) + [s for h in hosted for s in h.out_shape],
        scratch_shapes=list(scratch_shapes) + [s for h in hosted for s in h.sems],
        compiler_params=pltpu.CompilerParams(vmem_limit_bytes=VMEM_LIMIT_V7X),
    )(*args, *[s for h in hosted for s in h.srcs])
    outs, extras, pos = list(results[:n_out]), [], n_out
    for h in hosted:
        extras.append(list(results[pos:pos + h.n]))
        pos += len(h.out_shape)
    return outs, extras


def exchange(hosted, name):
    return _call(lambda: None, hosted, name=name, in_specs=[], out_specs=[], out_shape=[], args=[])[1]


def all_exchange_small(pack, name):
    R, L = pack.shape
    flips = [(dx, dy, dc) for dx in (0, 1) for dy in (0, 1) for dc in (0, 1)][1:]

    def body(src, out, local_sem, send_sem, recv_sem):
        x, y, c = _mesh_pos()
        me = 4 * x + 2 * y + c
        local = pltpu.make_async_copy(src, out.at[me], local_sem)
        local.start()
        copies = []
        for k, (dx, dy, dc) in enumerate(flips):
            copies.append(pltpu.make_async_remote_copy(
                src_ref=src, dst_ref=out.at[me], send_sem=send_sem.at[k], recv_sem=recv_sem.at[k],
                device_id=(x ^ dx, y ^ dy, c ^ dc), device_id_type=MESH))
        for cp in copies:
            cp.start()
        for k, (dx, dy, dc) in enumerate(flips):
            landed = out.at[4 * (x ^ dx) + 2 * (y ^ dy) + (c ^ dc)]
            pltpu.make_async_remote_copy(src_ref=landed, dst_ref=landed, send_sem=send_sem.at[k], recv_sem=recv_sem.at[k],
                                         device_id=(x ^ dx, y ^ dy, c ^ dc), device_id_type=MESH).wait_recv()
        for cp in copies:
            cp.wait_send()
        local.wait()

    return pl.pallas_call(
        body, name=name, in_specs=[ANY], out_specs=ANY,
        out_shape=jax.ShapeDtypeStruct((2 * N_CHIPS, R, L), pack.dtype),
        scratch_shapes=[pltpu.SemaphoreType.DMA, pltpu.SemaphoreType.DMA((7,)), pltpu.SemaphoreType.DMA((7,))],
    )(pack)


def ffn_up(h, gain, w1g, w3g, name, hosted=()):
    T, D = h.shape
    nsh, _, Fs = w1g.shape
    tm = min(T, 1024)

    def body(h_ref, g_ref, w1_ref, w3_ref, n_ref, a_ref, b_ref, s_ref):
        @pl.when(pl.program_id(1) == 0)
        def _():
            hh = h_ref[...]
            n_ref[...] = (hh * _rstd(hh) * g_ref[...]).astype(BF16)

        n = n_ref[...]
        a = _dot(n, w1_ref[0])
        b = _dot(n, w3_ref[0])
        a_ref[0] = a.astype(BF16)
        b_ref[0] = b.astype(BF16)
        s_ref[0] = (a * jax.nn.sigmoid(a) * b).astype(BF16)

    act = jax.ShapeDtypeStruct((nsh, T, Fs), BF16)
    act_spec = pl.BlockSpec((1, tm, Fs), lambda i, j: (j, i, 0))
    w_spec = pl.BlockSpec((1, D, Fs), lambda i, j: (j, 0, 0))
    return _call(
        body, hosted, name=name, grid=(T // tm, nsh),
        in_specs=[pl.BlockSpec((tm, D), lambda i, j: (i, 0)), pl.BlockSpec((1, D), lambda i, j: (0, 0)), w_spec, w_spec],
        out_specs=[pl.BlockSpec((tm, D), lambda i, j: (i, 0)), act_spec, act_spec, act_spec],
        out_shape=[jax.ShapeDtypeStruct((T, D), BF16), act, act, act],
        args=[h, gain, w1g, w3g])


def ffn_down(s, w2g, h, name, hosted=()):
    nsh, T, Fs = s.shape
    D = h.shape[1]
    tm = min(T, 512)

    def body(s_ref, w2_ref, h_ref, o_ref):
        f = _dot(s_ref[0], w2_ref[0])
        for j in range(1, nsh):
            f += _dot(s_ref[j], w2_ref[j])
        o_ref[...] = h_ref[...] + 0.5 * f

    return _call(
        body, hosted, name=name, grid=(T // tm,),
        in_specs=[pl.BlockSpec((nsh, tm, Fs), lambda i: (0, i, 0)), pl.BlockSpec((nsh, Fs, D), lambda i: (0, 0, 0)),
                  pl.BlockSpec((tm, D), lambda i: (i, 0))],
        out_specs=[pl.BlockSpec((tm, D), lambda i: (i, 0))],
        out_shape=[jax.ShapeDtypeStruct((T, D), F32)],
        args=[s, w2g, h])


def ffn_bwd_act(dh, w2g, a, b, name, hosted=()):
    T, D = dh.shape
    nsh, Fs, _ = w2g.shape
    tm = min(T, 1024)

    def body(dh_ref, w2_ref, a_ref, b_ref, da_ref, db_ref):
        df = (0.5 * dh_ref[...]).astype(BF16)
        ds = _dot_nt(df, w2_ref[0])
        silu, dsilu = _silu_parts(a_ref[0].astype(F32))
        da_ref[0] = (ds * b_ref[0].astype(F32) * dsilu).astype(BF16)
        db_ref[0] = (ds * silu).astype(BF16)

    act = jax.ShapeDtypeStruct((nsh, T, Fs), BF16)
    act_spec = pl.BlockSpec((1, tm, Fs), lambda j, i: (j, i, 0))
    return _call(
        body, hosted, name=name, grid=(nsh, T // tm),
        in_specs=[pl.BlockSpec((tm, D), lambda j, i: (i, 0)), pl.BlockSpec((1, Fs, D), lambda j, i: (j, 0, 0)), act_spec, act_spec],
        out_specs=[act_spec, act_spec],
        out_shape=[act, act],
        args=[dh, w2g, a, b])


def ffn_dw2(s, dh, name, hosted=()):
    nsh, T, Fs = s.shape
    D = dh.shape[1]
    tk = min(T, 512)
    nk = T // tk

    def body(s_ref, dh_ref, o_ref, acc):
        k = pl.program_id(1)

        @pl.when(k == 0)
        def _():
            acc[...] = jnp.zeros_like(acc)

        acc[...] += _dot_tn(s_ref[0], (0.5 * dh_ref[...]).astype(BF16))

        @pl.when(k == nk - 1)
        def _():
            o_ref[0] = acc[...].astype(BF16)

    return _call(
        body, hosted, name=name, grid=(nsh, nk),
        in_specs=[pl.BlockSpec((1, tk, Fs), lambda j, k: (j, k, 0)), pl.BlockSpec((tk, D), lambda j, k: (k, 0))],
        out_specs=[pl.BlockSpec((1, Fs, D), lambda j, k: (j, 0, 0))],
        out_shape=[jax.ShapeDtypeStruct((nsh, Fs, D), BF16)],
        scratch_shapes=[pltpu.VMEM((Fs, D), F32)],
        args=[s, dh])


def ffn_dw13(n, da, db, name, hosted=()):
    T, D = n.shape
    nsh, _, Fs = da.shape
    tk = min(T, 512)
    nk = T // tk

    def body(n_ref, da_ref, db_ref, o1_ref, o3_ref, acc1, acc3):
        k = pl.program_id(1)

        @pl.when(k == 0)
        def _():
            acc1[...] = jnp.zeros_like(acc1)
            acc3[...] = jnp.zeros_like(acc3)

        nn = n_ref[...]
        acc1[...] += _dot_tn(nn, da_ref[0])
        acc3[...] += _dot_tn(nn, db_ref[0])

        @pl.when(k == nk - 1)
        def _():
            o1_ref[0] = acc1[...].astype(BF16)
            o3_ref[0] = acc3[...].astype(BF16)

    act_spec = pl.BlockSpec((1, tk, Fs), lambda j, k: (j, k, 0))
    out = jax.ShapeDtypeStruct((nsh, D, Fs), BF16)
    out_spec = pl.BlockSpec((1, D, Fs), lambda j, k: (j, 0, 0))
    return _call(
        body, hosted, name=name, grid=(nsh, nk),
        in_specs=[pl.BlockSpec((tk, D), lambda j, k: (k, 0)), act_spec, act_spec],
        out_specs=[out_spec, out_spec],
        out_shape=[out, out],
        scratch_shapes=[pltpu.VMEM((D, Fs), F32), pltpu.VMEM((D, Fs), F32)],
        args=[n, da, db])


def ffn_bwd_in(da, db, w1g, w3g, h, gain, dh, name, hosted=()):
    nsh, T, Fs = da.shape
    D = h.shape[1]
    tm = min(T, 256)

    def body(da_ref, db_ref, w1_ref, w3_ref, h_ref, g_ref, dh_ref, o_ref, dg_ref):
        dn = _dot_nt(da_ref[0], w1_ref[0]) + _dot_nt(db_ref[0], w3_ref[0])
        for j in range(1, nsh):
            dn += _dot_nt(da_ref[j], w1_ref[j]) + _dot_nt(db_ref[j], w3_ref[j])
        dhn, dg = _rmsnorm_bwd(dn, h_ref[...], g_ref[...])
        o_ref[...] = dh_ref[...] + dhn

        @pl.when(pl.program_id(0) == 0)
        def _():
            dg_ref[...] = jnp.zeros_like(dg_ref)

        dg_ref[...] += jnp.sum(dg, axis=0, keepdims=True)

    act_spec = pl.BlockSpec((nsh, tm, Fs), lambda i: (0, i, 0))
    w_spec = pl.BlockSpec((nsh, D, Fs), lambda i: (0, 0, 0))
    row_spec = pl.BlockSpec((tm, D), lambda i: (i, 0))
    vec_spec = pl.BlockSpec((1, D), lambda i: (0, 0))
    return _call(
        body, hosted, name=name, grid=(T // tm,),
        in_specs=[act_spec, act_spec, w_spec, w_spec, row_spec, vec_spec, row_spec],
        out_specs=[row_spec, vec_spec],
        out_shape=[jax.ShapeDtypeStruct((T, D), F32), jax.ShapeDtypeStruct((1, D), F32)],
        args=[da, db, w1g, w3g, h, gain, dh])


def mix_in(h, gain, wing, name, hosted=()):
    T, D = h.shape
    nsh, _, Cs = wing.shape
    tm = min(T, 512)

    def body(h_ref, g_ref, w_ref, u_ref, p_ref):
        hh = h_ref[...]
        u = (hh * _rstd(hh) * g_ref[...]).astype(BF16)
        u_ref[...] = u
        for j in range(nsh):
            p_ref[:, j * Cs:(j + 1) * Cs] = _dot(u, w_ref[j])

    return _call(
        body, hosted, name=name, grid=(T // tm,),
        in_specs=[pl.BlockSpec((tm, D), lambda i: (i, 0)), pl.BlockSpec((1, D), lambda i: (0, 0)),
                  pl.BlockSpec((nsh, D, Cs), lambda i: (0, 0, 0))],
        out_specs=[pl.BlockSpec((tm, D), lambda i: (i, 0)), pl.BlockSpec((tm, nsh * Cs), lambda i: (i, 0))],
        out_shape=[jax.ShapeDtypeStruct((T, D), BF16), jax.ShapeDtypeStruct((T, nsh * Cs), F32)],
        args=[h, gain, wing])


def mix_out(a, b, woutg, h, name, hosted=()):
    T, W = a.shape
    D = h.shape[1]
    wout = woutg.reshape(2, W, D)
    tm = min(T, 512)

    def body(a_ref, b_ref, w_ref, h_ref, o_ref):
        o_ref[...] = h_ref[...] + _dot(a_ref[...], w_ref[0]) + _dot(b_ref[...], w_ref[1])

    return _call(
        body, hosted, name=name, grid=(T // tm,),
        in_specs=[pl.BlockSpec((tm, W), lambda i: (i, 0)), pl.BlockSpec((tm, W), lambda i: (i, 0)),
                  pl.BlockSpec((2, W, D), lambda i: (0, 0, 0)), pl.BlockSpec((tm, D), lambda i: (i, 0))],
        out_specs=[pl.BlockSpec((tm, D), lambda i: (i, 0))],
        out_shape=[jax.ShapeDtypeStruct((T, D), F32)],
        args=[a, b, wout, h])


def mix_out_bwd(dh, woutg, a, b, name, hosted=()):
    T, D = dh.shape
    W = a.shape[1]
    nsh, Rs, _ = woutg.shape
    wout = woutg.reshape(2, W, D)
    tk = min(T, 512)
    nk = T // tk

    def body(dh_ref, w_ref, a_ref, b_ref, da_ref, db_ref, dw_ref, acc):
        k = pl.program_id(0)

        @pl.when(k == 0)
        def _():
            acc[...] = jnp.zeros_like(acc)

        dhb = dh_ref[...].astype(BF16)
        da_ref[...] = _dot_nt(dhb, w_ref[0])
        db_ref[...] = _dot_nt(dhb, w_ref[1])
        acc[0:W, :] += _dot_tn(a_ref[...], dhb)
        acc[W:2 * W, :] += _dot_tn(b_ref[...], dhb)

        @pl.when(k == nk - 1)
        def _():
            for j in range(nsh):
                dw_ref[j] = acc[j * Rs:(j + 1) * Rs, :].astype(BF16)

    return _call(
        body, hosted, name=name, grid=(nk,),
        in_specs=[pl.BlockSpec((tk, D), lambda k: (k, 0)), pl.BlockSpec((2, W, D), lambda k: (0, 0, 0)),
                  pl.BlockSpec((tk, W), lambda k: (k, 0)), pl.BlockSpec((tk, W), lambda k: (k, 0))],
        out_specs=[pl.BlockSpec((tk, W), lambda k: (k, 0)), pl.BlockSpec((tk, W), lambda k: (k, 0)),
                   pl.BlockSpec((nsh, Rs, D), lambda k: (0, 0, 0))],
        out_shape=[jax.ShapeDtypeStruct((T, W), F32), jax.ShapeDtypeStruct((T, W), F32),
                   jax.ShapeDtypeStruct((nsh, Rs, D), BF16)],
        scratch_shapes=[pltpu.VMEM((2 * W, D), F32)],
        args=[dh, wout, a, b])


def mix_dwin(u, d, nsh, name, hosted=()):
    T, D = u.shape
    Cs = d.shape[1] // nsh
    tk = min(T, 512)
    nk = T // tk

    def body(u_ref, d_ref, o_ref, acc):
        k = pl.program_id(1)

        @pl.when(k == 0)
        def _():
            acc[...] = jnp.zeros_like(acc)

        acc[...] += _dot_tn(u_ref[...], d_ref[...])

        @pl.when(k == nk - 1)
        def _():
            o_ref[0] = acc[...].astype(BF16)

    return _call(
        body, hosted, name=name, grid=(nsh, nk),
        in_specs=[pl.BlockSpec((tk, D), lambda j, k: (k, 0)), pl.BlockSpec((tk, Cs), lambda j, k: (k, j))],
        out_specs=[pl.BlockSpec((1, D, Cs), lambda j, k: (j, 0, 0))],
        out_shape=[jax.ShapeDtypeStruct((nsh, D, Cs), BF16)],
        scratch_shapes=[pltpu.VMEM((D, Cs), F32)],
        args=[u, d])


def mix_in_bwd(d, wing, h, gain, dh, name, hosted=()):
    T, D = h.shape
    nsh, _, Cs = wing.shape
    tm = min(T, 512)

    def body(d_ref, w_ref, h_ref, g_ref, dh_ref, o_ref, dg_ref):
        du = _dot_nt(d_ref[:, 0:Cs], w_ref[0])
        for j in range(1, nsh):
            du += _dot_nt(d_ref[:, j * Cs:(j + 1) * Cs], w_ref[j])
        dhn, dg = _rmsnorm_bwd(du, h_ref[...], g_ref[...])
        o_ref[...] = dh_ref[...] + dhn

        @pl.when(pl.program_id(0) == 0)
        def _():
            dg_ref[...] = jnp.zeros_like(dg_ref)

        dg_ref[...] += jnp.sum(dg, axis=0, keepdims=True)

    row_spec = pl.BlockSpec((tm, D), lambda i: (i, 0))
    vec_spec = pl.BlockSpec((1, D), lambda i: (0, 0))
    return _call(
        body, hosted, name=name, grid=(T // tm,),
        in_specs=[pl.BlockSpec((tm, nsh * Cs), lambda i: (i, 0)), pl.BlockSpec((nsh, D, Cs), lambda i: (0, 0, 0)),
                  row_spec, vec_spec, row_spec],
        out_specs=[row_spec, vec_spec],
        out_shape=[jax.ShapeDtypeStruct((T, D), F32), jax.ShapeDtypeStruct((1, D), F32)],
        args=[d, wing, h, gain, dh])


def _pool_window(x, group, T, trailing):
    rows = lax.broadcasted_iota(jnp.int32, x.shape, 0)

    def shifted(z, k):
        if trailing:
            return jnp.where(rows >= k, pltpu.roll(z, k, 0), 0.0)
        return jnp.where(rows < T - k, pltpu.roll(z, T - k, 0), 0.0)

    s2 = x + shifted(x, 1)
    s4 = s2 + shifted(s2, 2)
    s8 = s4 + shifted(s4, 4)
    s16 = s8 + shifted(s8, 8)
    return jnp.where(group == 0, s2, jnp.where(group == 1, s4, jnp.where(group == 2, s8, s16)))


def _pool_count(group, shape):
    rows = lax.broadcasted_iota(jnp.int32, shape, 0)
    w = jnp.where(group == 0, 2, jnp.where(group == 1, 4, jnp.where(group == 2, 8, 16)))
    return jnp.minimum(rows + 1, w).astype(F32)


def pool_fwd(proj, pool_w, pool_scale, name, hosted=()):
    T = proj.shape[0]
    Hd = HEAD_DIM

    def body(x_ref, w_ref, sc_ref, a_ref):
        g = pl.program_id(0)
        x = x_ref[...]
        pooled = _pool_window(x, g, T, True) / _pool_count(g, x.shape) - x
        a_ref[...] = (_dot(pooled.astype(BF16), w_ref[0].astype(BF16)) * sc_ref[...]).astype(BF16)

    return _call(
        body, hosted, name=name, grid=(N_GROUPS,),
        in_specs=[pl.BlockSpec((T, Hd), lambda g: (0, g)), pl.BlockSpec((1, Hd, Hd), lambda g: (g, 0, 0)),
                  pl.BlockSpec((1, Hd), lambda g: (0, g))],
        out_specs=[pl.BlockSpec((T, Hd), lambda g: (0, g))],
        out_shape=[jax.ShapeDtypeStruct((T, N_GROUPS * Hd), BF16)],
        args=[proj, pool_w, pool_scale])


def pool_bwd(proj, da, pool_w, pool_scale, name, hosted=()):
    T = proj.shape[0]
    Hd = HEAD_DIM

    def body(x_ref, da_ref, w_ref, sc_ref, dx_ref, dw_ref, dsc_ref):
        g = pl.program_id(0)
        x = x_ref[...]
        cnt = _pool_count(g, x.shape)
        pooled = (_pool_window(x, g, T, True) / cnt - x).astype(BF16)
        wb = w_ref[0].astype(BF16)
        dav = da_ref[...]
        dsc_ref[...] = jnp.sum(dav * _dot(pooled, wb), axis=0, keepdims=True)
        dout = (dav * sc_ref[...]).astype(BF16)
        dw_ref[0] = _dot_tn(pooled, dout)
        dpooled = _dot_nt(dout, wb)
        dx_ref[...] = (_pool_window(dpooled / cnt, g, T, False) - dpooled).astype(BF16)

    col_spec = pl.BlockSpec((T, Hd), lambda g: (0, g))
    return _call(
        body, hosted, name=name, grid=(N_GROUPS,),
        in_specs=[col_spec, col_spec, pl.BlockSpec((1, Hd, Hd), lambda g: (g, 0, 0)), pl.BlockSpec((1, Hd), lambda g: (0, g))],
        out_specs=[col_spec, pl.BlockSpec((1, Hd, Hd), lambda g: (g, 0, 0)), pl.BlockSpec((1, Hd), lambda g: (0, g))],
        out_shape=[jax.ShapeDtypeStruct((T, N_GROUPS * Hd), BF16), jax.ShapeDtypeStruct((N_GROUPS, Hd, Hd), F32),
                   jax.ShapeDtypeStruct((1, N_GROUPS * Hd), F32)],
        args=[proj, da, pool_w, pool_scale])


def _ret_tables(T):
    Hd, C = HEAD_DIM, RET_CHUNK
    inv_freq = 1.0 / (ROPE_BASE ** (jnp.arange(0, Hd, 2, dtype=F32) / Hd))
    ang = jnp.arange(T, dtype=F32)[:, None] * inv_freq[None, :]
    cos, sin = jnp.cos(ang), jnp.sin(ang)
    cos2 = jnp.concatenate([cos, cos], axis=-1)
    sin2 = jnp.concatenate([-sin, sin], axis=-1)
    log_gamma = jnp.log1p(-jnp.exp2(-5.0 - jnp.arange(N_GROUPS, dtype=F32)))
    pos = jnp.arange(C, dtype=F32)
    rel = pos[:, None] - pos[None, :]
    intra = jnp.where(rel[None] >= 0, jnp.exp(log_gamma[:, None, None] * jnp.maximum(rel, 0.0)[None]), 0.0)
    k_tail = jnp.exp(log_gamma[:, None] * (C - 1 - pos)[None, :])
    q_head = jnp.exp(log_gamma[:, None] * (pos + 1.0)[None, :])
    chunk_decay = jnp.exp(log_gamma * C)
    wide = lambda t: jnp.broadcast_to(t[:, :, None], (N_GROUPS, C, Hd))
    return cos2, sin2, intra, wide(k_tail), wide(q_head), jnp.broadcast_to(chunk_decay[:, None, None], (N_GROUPS, 1, Hd))


def _rope(x, cos2, sin2):
    return x * cos2 + pltpu.roll(x, HEAD_DIM // 2, 1) * sin2


def _rope_t(d, cos2, sin2):
    return d * cos2 + pltpu.roll(d * sin2, HEAD_DIM // 2, 1)


def _ret_specs(tseg, seg_of):
    Hd, G = HEAD_DIM, N_GROUPS
    col = lambda kind: pl.BlockSpec((tseg, Hd), lambda h, s: (seg_of(s), G * kind + h))
    tab = pl.BlockSpec((tseg, Hd), lambda h, s: (seg_of(s), 0))
    head = pl.BlockSpec((1, RET_CHUNK, Hd), lambda h, s: (h, 0, 0))
    cd = pl.BlockSpec((1, 1, Hd), lambda h, s: (h, 0, 0))
    gain = pl.BlockSpec((1, Hd), lambda h, s: (0, h))
    return col, tab, head, cd, gain


def ret_fwd(proj, ret_norm, tables, name, hosted=()):
    T = proj.shape[0]
    Hd, C, G = HEAD_DIM, RET_CHUNK, N_GROUPS
    tseg = min(T, 1024)
    nseg, nck = T // tseg, tseg // C
    scale = Hd ** -0.5
    cos2, sin2, intra, k_tail, q_head, chunk_decay = tables

    def body(q_ref, k_ref, v_ref, g_ref, gain_ref, cos_ref, sin_ref, m_ref, kt_ref, qh_ref, cd_ref,
             b_ref, o_ref, rp_ref, state):
        @pl.when(pl.program_id(1) == 0)
        def _():
            state[...] = jnp.zeros_like(state)

        def chunk(ci, carry):
            rows = pl.ds(pl.multiple_of(ci * C, C), C)
            cos, sin = cos_ref[rows, :], sin_ref[rows, :]
            qr = _rope(q_ref[rows, :], cos, sin)
            kr = _rope(k_ref[rows, :], cos, sin) * scale
            qb, kb, vb = qr.astype(BF16), kr.astype(BF16), v_ref[rows, :].astype(BF16)
            r = state[...]
            rp_ref[0, ci] = r.astype(BF16)
            sc = _dot_nt(qb, kb) * m_ref[0]
            o = _dot(sc.astype(BF16), vb) + _dot((qr * qh_ref[0]).astype(BF16), r.astype(BF16))
            state[...] = cd_ref[0] * r + _dot_tn((kr * kt_ref[0]).astype(BF16), vb)
            o_ref[rows, :] = o
            on = o * _rstd(o)
            b_ref[rows, :] = (jax.nn.silu(g_ref[rows, :]) * (on * gain_ref[...])).astype(BF16)
            return carry

        lax.fori_loop(0, nck, chunk, 0)

    col, tab, head, cd, gain = _ret_specs(tseg, lambda s: s)
    out_col = pl.BlockSpec((tseg, Hd), lambda h, s: (s, h))
    return _call(
        body, hosted, name=name, grid=(G, nseg),
        in_specs=[col(1), col(2), col(3), col(4), gain, tab, tab, head, head, head, cd],
        out_specs=[out_col, out_col, pl.BlockSpec((1, nck, Hd, Hd), lambda h, s: (h, s, 0, 0))],
        out_shape=[jax.ShapeDtypeStruct((T, G * Hd), BF16), jax.ShapeDtypeStruct((T, G * Hd), F32),
                   jax.ShapeDtypeStruct((G, T // C, Hd, Hd), BF16)],
        scratch_shapes=[pltpu.VMEM((Hd, Hd), F32)],
        args=[proj, proj, proj, proj, ret_norm, cos2, sin2, intra, k_tail, q_head, chunk_decay])


def ret_bwd(proj, db, o_pre, r_prev, ret_norm, tables, name, hosted=()):
    T = proj.shape[0]
    Hd, C, G = HEAD_DIM, RET_CHUNK, N_GROUPS
    tseg = min(T, 1024)
    nseg, nck = T // tseg, tseg // C
    scale = Hd ** -0.5
    cos2, sin2, intra, k_tail, q_head, chunk_decay = tables

    def body(q_ref, k_ref, v_ref, g_ref, db_ref, o_ref, rp_ref, gain_ref, cos_ref, sin_ref, m_ref, kt_ref, qh_ref, cd_ref,
             d_ref, dgain_ref, gstate):
        @pl.when(pl.program_id(1) == 0)
        def _():
            gstate[...] = jnp.zeros_like(gstate)
            dgain_ref[...] = jnp.zeros_like(dgain_ref)

        def chunk(t, carry):
            ci = nck - 1 - t
            rows = pl.ds(pl.multiple_of(ci * C, C), C)
            cos, sin = cos_ref[rows, :], sin_ref[rows, :]
            qr = _rope(q_ref[rows, :], cos, sin)
            kr = _rope(k_ref[rows, :], cos, sin) * scale
            qb, kb, vb = qr.astype(BF16), kr.astype(BF16), v_ref[rows, :].astype(BF16)
            qhb, ktb = (qr * qh_ref[0]).astype(BF16), (kr * kt_ref[0]).astype(BF16)
            sc = (_dot_nt(qb, kb) * m_ref[0]).astype(BF16)
            o = o_ref[rows, :]
            rstd = _rstd(o)
            on = o * rstd
            gain = gain_ref[...]
            silu, dsilu = _silu_parts(g_ref[rows, :])
            dy = db_ref[rows, :]
            dgain_ref[...] += jnp.sum(dy * silu * on, axis=0, keepdims=True)
            dg = dy * on * gain * dsilu
            don = dy * silu * gain
            dob = (rstd * (don - on * jnp.mean(don * on, axis=-1, keepdims=True))).astype(BF16)
            gn = gstate[...]
            gb = gn.astype(BF16)
            da = (_dot_nt(dob, vb) * m_ref[0]).astype(BF16)
            dq = _dot(da, kb) + _dot_nt(dob, rp_ref[0, ci]) * qh_ref[0]
            dk = _dot_tn(da, qb) + _dot_nt(vb, gb) * kt_ref[0]
            dv = _dot_tn(sc, dob) + _dot(ktb, gb)
            gstate[...] = cd_ref[0] * gn + _dot_tn(qhb, dob)
            d_ref[0, rows, :] = _rope_t(dq, cos, sin).astype(BF16)
            d_ref[1, rows, :] = _rope_t(dk * scale, cos, sin).astype(BF16)
            d_ref[2, rows, :] = dv.astype(BF16)
            d_ref[3, rows, :] = dg.astype(BF16)
            return carry

        lax.fori_loop(0, nck, chunk, 0)

    rev = lambda s: nseg - 1 - s
    col, tab, head, cd, gain = _ret_specs(tseg, rev)
    act = pl.BlockSpec((tseg, Hd), lambda h, s: (rev(s), h))
    return _call(
        body, hosted, name=name, grid=(G, nseg),
        in_specs=[col(1), col(2), col(3), col(4), act, act, pl.BlockSpec((1, nck, Hd, Hd), lambda h, s: (h, rev(s), 0, 0)),
                  gain, tab, tab, head, head, head, cd],
        out_specs=[pl.BlockSpec((4, tseg, Hd), lambda h, s: (0, rev(s), h)), gain],
        out_shape=[jax.ShapeDtypeStruct((4, T, G * Hd), BF16), jax.ShapeDtypeStruct((1, G * Hd), F32)],
        scratch_shapes=[pltpu.VMEM((Hd, Hd), F32)],
        args=[proj, proj, proj, proj, db, o_pre, r_prev, ret_norm, cos2, sin2, intra, k_tail, q_head, chunk_decay])


def final_loss(h, gain, target, name, hosted=()):
    T, D = h.shape
    tm = min(T, 512)

    def body(h_ref, g_ref, t_ref, dh_ref, loss_ref, dg_ref):
        @pl.when(pl.program_id(0) == 0)
        def _():
            loss_ref[...] = jnp.zeros_like(loss_ref)
            dg_ref[...] = jnp.zeros_like(dg_ref)

        hh = h_ref[...]
        gain_v = g_ref[...]
        err = hh * _rstd(hh) * gain_v - t_ref[...]
        loss_ref[...] += 0.5 * jnp.sum(jnp.mean(err * err, axis=-1, keepdims=True), axis=0, keepdims=True)
        dhn, dg = _rmsnorm_bwd(err * (1.0 / D), hh, gain_v)
        dh_ref[...] = dhn
        dg_ref[...] += jnp.sum(dg, axis=0, keepdims=True)

    row_spec = pl.BlockSpec((tm, D), lambda i: (i, 0))
    vec_spec = pl.BlockSpec((1, D), lambda i: (0, 0))
    return _call(
        body, hosted, name=name, grid=(T // tm,),
        in_specs=[row_spec, vec_spec, row_spec],
        out_specs=[row_spec, pl.BlockSpec((1, 128), lambda i: (0, 0)), vec_spec],
        out_shape=[jax.ShapeDtypeStruct((T, D), F32), jax.ShapeDtypeStruct((1, 128), F32), jax.ShapeDtypeStruct((1, D), F32)],
        args=[h, gain, target])


def prereduce(grad, recv, core, name):
    nsh, R, C = grad.shape
    rh = R // 2

    def body(c_ref, g_ref, r_ref, o_ref):
        o_ref[...] = (g_ref[...].astype(F32) + r_ref[...].astype(F32)).astype(BF16)

    return pl.pallas_call(
        body, name=name,
        grid_spec=pltpu.PrefetchScalarGridSpec(
            num_scalar_prefetch=1, grid=(nsh,),
            in_specs=[pl.BlockSpec((1, rh, C), lambda j, c_ref: (j, c_ref[0], 0)), pl.BlockSpec((1, rh, C), lambda j, c_ref: (j, 0, 0))],
            out_specs=pl.BlockSpec((1, rh, C), lambda j, c_ref: (j, 0, 0))),
        out_shape=jax.ShapeDtypeStruct((nsh, rh, C), BF16),
        compiler_params=pltpu.CompilerParams(vmem_limit_bytes=VMEM_LIMIT_V7X),
    )(core, grad, recv)


def _adamw(w, g, m, v):
    m = ADAM_B1 * m + (1.0 - ADAM_B1) * g
    v = ADAM_B2 * v + (1.0 - ADAM_B2) * (g * g)
    m_hat = m / (1.0 - ADAM_B1 ** ADAM_STEP)
    v_hat = v / (1.0 - ADAM_B2 ** ADAM_STEP)
    return -ADAM_LR * (m_hat / (jnp.sqrt(v_hat) + ADAM_EPS) + ADAM_WD * w), m, v


def adamw_sharded(tensors, name, hosted=()):
    nt = len(tensors)
    nsh, R, C = tensors[0][0].shape
    tr = 256 if R % 256 == 0 else R // 2

    def body(*refs):
        ins, outs = refs[:4 * nt], refs[4 * nt:]
        for t in range(nt):
            p_ref, w_ref, m_ref, v_ref = ins[4 * t:4 * t + 4]
            g_ref, d_ref, nm_ref, nv_ref = outs[4 * t:4 * t + 4]
            g = p_ref[0].astype(F32)
            for i in range(1, nsh):
                g += p_ref[i].astype(F32)
            g_ref[...] = g
            d_ref[...], nm_ref[...], nv_ref[...] = _adamw(w_ref[...], g, m_ref[...], v_ref[...])

    spec = pl.BlockSpec((tr, C), lambda i: (i, 0))
    out = jax.ShapeDtypeStruct((R, C), F32)
    return _call(
        body, hosted, name=name, grid=(R // tr,),
        in_specs=[pl.BlockSpec((nsh, tr, C), lambda i: (0, i, 0)), spec, spec, spec] * nt,
        out_specs=[spec] * (4 * nt), out_shape=[out] * (4 * nt),
        args=[a for tensor in tensors for a in tensor])


def adamw_small(packs, w, m, v, name):
    ndev, R, L = packs.shape

    def body(p_ref, w_ref, m_ref, v_ref, g_ref, d_ref, nm_ref, nv_ref):
        g = p_ref[0]
        for i in range(1, ndev):
            g += p_ref[i]
        g_ref[...] = g
        d_ref[...], nm_ref[...], nv_ref[...] = _adamw(w_ref[...], g, m_ref[...], v_ref[...])

    out = jax.ShapeDtypeStruct((R, L), F32)
    return pl.pallas_call(body, name=name, out_shape=[out] * 4,
                          compiler_params=pltpu.CompilerParams(vmem_limit_bytes=VMEM_LIMIT_V7X))(packs, w, m, v)


BIG = ("ffn1_w1", "ffn1_w3", "ffn1_w2", "w_in", "w_out", "ffn2_w1", "ffn2_w3", "ffn2_w2")
SMALL = ("ffn1_norm", "mix_norm", "pool_w", "pool_scale", "ret_norm", "ffn2_norm", "final_norm")
WEIGHTS = ("ffn1_norm", "ffn1_w1", "ffn1_w3", "ffn1_w2", "mix_norm", "w_in", "pool_w", "pool_scale", "ret_norm", "w_out",
           "ffn2_norm", "ffn2_w1", "ffn2_w3", "ffn2_w2", "final_norm")


def _pack(parts):
    return jnp.concatenate([parts[k].reshape(-1, 128) for k in SMALL], axis=0)


def _unpack(pack, like):
    out, row = {}, 0
    for k in SMALL:
        rows = like[k].size // 128
        out[k] = pack[row:row + rows].reshape(like[k].shape)
        row += rows
    return out


def kernel(x, ffn1_norm, ffn1_w1, ffn1_w3, ffn1_w2, mix_norm, w_in, pool_w, pool_scale, ret_norm, w_out, ffn2_norm, ffn2_w1, ffn2_w3, ffn2_w2, final_norm, loss_target, m_ffn1_norm, m_ffn1_w1, m_ffn1_w3, m_ffn1_w2, m_mix_norm, m_w_in, m_pool_w, m_pool_scale, m_ret_norm, m_w_out, m_ffn2_norm, m_ffn2_w1, m_ffn2_w3, m_ffn2_w2, m_final_norm, v_ffn1_norm, v_ffn1_w1, v_ffn1_w3, v_ffn1_w2, v_mix_norm, v_w_in, v_pool_w, v_pool_scale, v_ret_norm, v_w_out, v_ffn2_norm, v_ffn2_w1, v_ffn2_w3, v_ffn2_w2, v_final_norm):
    w = dict(ffn1_norm=ffn1_norm, ffn1_w1=ffn1_w1, ffn1_w3=ffn1_w3, ffn1_w2=ffn1_w2, mix_norm=mix_norm, w_in=w_in, pool_w=pool_w,
             pool_scale=pool_scale, ret_norm=ret_norm, w_out=w_out, ffn2_norm=ffn2_norm, ffn2_w1=ffn2_w1, ffn2_w3=ffn2_w3,
             ffn2_w2=ffn2_w2, final_norm=final_norm)
    m = dict(ffn1_norm=m_ffn1_norm, ffn1_w1=m_ffn1_w1, ffn1_w3=m_ffn1_w3, ffn1_w2=m_ffn1_w2, mix_norm=m_mix_norm, w_in=m_w_in,
             pool_w=m_pool_w, pool_scale=m_pool_scale, ret_norm=m_ret_norm, w_out=m_w_out, ffn2_norm=m_ffn2_norm, ffn2_w1=m_ffn2_w1,
             ffn2_w3=m_ffn2_w3, ffn2_w2=m_ffn2_w2, final_norm=m_final_norm)
    v = dict(ffn1_norm=v_ffn1_norm, ffn1_w1=v_ffn1_w1, ffn1_w3=v_ffn1_w3, ffn1_w2=v_ffn1_w2, mix_norm=v_mix_norm, w_in=v_w_in,
             pool_w=v_pool_w, pool_scale=v_pool_scale, ret_norm=v_ret_norm, w_out=v_w_out, ffn2_norm=v_ffn2_norm, ffn2_w1=v_ffn2_w1,
             ffn2_w3=v_ffn2_w3, ffn2_w2=v_ffn2_w2, final_norm=v_final_norm)
    xs, target = x[0], loss_target[0]
    T = xs.shape[0]
    tables = _ret_tables(T)
    core = lax.axis_index("c").astype(jnp.int32).reshape(1)
    sh = {k: w[k][0].astype(BF16) for k in BIG}
    gather = lambda *names: [ChipExchange([sh[k] for k in names], False)]
    wg, grad, delta, new_m, new_v = {}, {}, {}, {}, {}

    def update(names, pieces, name, hosted=()):
        outs, extras = adamw_sharded([(p, w[k][0], m[k][0], v[k][0]) for k, p in zip(names, pieces)], name, hosted)
        for t, k in enumerate(names):
            grad[k], delta[k], new_m[k], new_v[k] = [o[None] for o in outs[4 * t:4 * t + 4]]
        return extras

    def reduce_in_chip(name, partial, recv):
        return prereduce(partial, recv, core, "prereduce_" + name)

    (wg["ffn1_w1"], wg["ffn1_w3"]), = exchange(gather("ffn1_w1", "ffn1_w3"), "gather_ffn1")
    (n1, a1, b1, s1), ((wg["ffn1_w2"], wg["w_in"]),) = ffn_up(
        xs, ffn1_norm, wg["ffn1_w1"], wg["ffn1_w3"], "ffn1_up", gather("ffn1_w2", "w_in"))
    (h1,), ((wg["w_out"],),) = ffn_down(s1, wg["ffn1_w2"], xs, "ffn1_down", gather("w_out"))
    (u, proj), ((wg["ffn2_w1"],),) = mix_in(h1, mix_norm, wg["w_in"], "mix_in", gather("ffn2_w1"))
    (pa,), _ = pool_fwd(proj, pool_w[0], pool_scale, "pool_fwd")
    (rb, o_pre, r_prev), ((wg["ffn2_w3"],),) = ret_fwd(proj, ret_norm, tables, "ret_fwd", gather("ffn2_w3"))
    (h2,), _ = mix_out(pa, rb, wg["w_out"], h1, "mix_out")
    (n2, a2, b2, s2), ((wg["ffn2_w2"],),) = ffn_up(
        h2, ffn2_norm, wg["ffn2_w1"], wg["ffn2_w3"], "ffn2_up", gather("ffn2_w2"))
    (h3,), _ = ffn_down(s2, wg["ffn2_w2"], h2, "ffn2_down")
    (dh3, loss, d_final), _ = final_loss(h3, final_norm[None], target, "final_loss")
    loss = lax.psum(loss[0, 0], ("x", "y", "c"))

    (da2, db2), _ = ffn_bwd_act(dh3, wg["ffn2_w2"], a2, b2, "ffn2_bwd_act")
    (g_f2w2,), _ = ffn_dw2(s2, dh3, "ffn2_dw2")
    (g_f2w1, g_f2w3), ((r_f2w2,),) = ffn_dw13(n2, da2, db2, "ffn2_dw13", [SiblingExchange([g_f2w2])])
    p_f2w2 = reduce_in_chip("ffn2_w2", g_f2w2, r_f2w2)
    (dh2, d_ffn2), ((q_f2w2,), (r_f2w1, r_f2w3)) = ffn_bwd_in(
        da2, db2, wg["ffn2_w1"], wg["ffn2_w3"], h2, ffn2_norm, dh3, "ffn2_bwd_in",
        [ChipExchange([p_f2w2], True), SiblingExchange([g_f2w1, g_f2w3])])
    p_f2w1 = reduce_in_chip("ffn2_w1", g_f2w1, r_f2w1)
    p_f2w3 = reduce_in_chip("ffn2_w3", g_f2w3, r_f2w3)
    (dpa, drb, g_wout), ((q_f2w1,),) = mix_out_bwd(dh2, wg["w_out"], pa, rb, "mix_out_bwd", [ChipExchange([p_f2w1], True)])
    (dpool, d_pool_w, d_pool_scale), _ = pool_bwd(proj, dpa, pool_w[0], pool_scale, "pool_bwd")
    (dqkvg, d_ret_norm), ((q_f2w3,), (r_wout,)) = ret_bwd(
        proj, drb, o_pre, r_prev, ret_norm, tables, "ret_bwd", [ChipExchange([p_f2w3], True), SiblingExchange([g_wout])])
    p_wout = reduce_in_chip("w_out", g_wout, r_wout)
    d = jnp.concatenate([dpool, dqkvg[0], dqkvg[1], dqkvg[2], dqkvg[3]], axis=1)
    (g_win,), ((q_wout,),) = mix_dwin(u, d, N_CHIPS, "mix_dwin", [ChipExchange([p_wout], True)])
    (dh1, d_mix), ((r_win,),) = mix_in_bwd(d, wg["w_in"], h1, mix_norm, dh2, "mix_in_bwd", [SiblingExchange([g_win])])
    p_win = reduce_in_chip("w_in", g_win, r_win)
    (da1, db1), ((q_win,),) = ffn_bwd_act(dh1, wg["ffn1_w2"], a1, b1, "ffn1_bwd_act", [ChipExchange([p_win], True)])
    (g_f1w1, g_f1w3), _ = ffn_dw13(n1, da1, db1, "ffn1_dw13")
    (g_f1w2,), ((r_f1w1, r_f1w3),) = ffn_dw2(s1, dh1, "ffn1_dw2", [SiblingExchange([g_f1w1, g_f1w3])])
    p_f1w1 = reduce_in_chip("ffn1_w1", g_f1w1, r_f1w1)
    p_f1w3 = reduce_in_chip("ffn1_w3", g_f1w3, r_f1w3)
    (dx, d_ffn1), ((q_f1w1, q_f1w3), (r_f1w2,)) = ffn_bwd_in(
        da1, db1, wg["ffn1_w1"], wg["ffn1_w3"], xs, ffn1_norm, dh1, "ffn1_bwd_in",
        [ChipExchange([p_f1w1, p_f1w3], True), SiblingExchange([g_f1w2])])
    p_f1w2 = reduce_in_chip("ffn1_w2", g_f1w2, r_f1w2)

    (q_f1w2,), = update(["ffn2_w1", "ffn2_w3"], [q_f2w1, q_f2w3], "adamw_ffn2_w13", [ChipExchange([p_f1w2], True)])
    update(["ffn2_w2"], [q_f2w2], "adamw_ffn2_w2")
    update(["w_in"], [q_win], "adamw_w_in")
    update(["w_out"], [q_wout], "adamw_w_out")
    update(["ffn1_w1", "ffn1_w3"], [q_f1w1, q_f1w3], "adamw_ffn1_w13")
    update(["ffn1_w2"], [q_f1w2], "adamw_ffn1_w2")

    small = {"ffn1_norm": d_ffn1, "mix_norm": d_mix, "pool_w": d_pool_w, "pool_scale": d_pool_scale,
             "ret_norm": d_ret_norm, "ffn2_norm": d_ffn2, "final_norm": d_final}
    packs = all_exchange_small(_pack(small), "gather_small")
    outs = adamw_small(packs, _pack(w), _pack(m), _pack(v), "adamw_small")
    for res, pack in zip((grad, delta, new_m, new_v), outs):
        res.update(_unpack(pack, w))

    return (loss, dx[None], *[grad[k] for k in WEIGHTS], *[delta[k] for k in WEIGHTS],
            *[new_m[k] for k in WEIGHTS], *[new_v[k] for k in WEIGHTS])
```

```python
import math

import jax
import jax.numpy as jnp
from jax import lax
from jax.experimental import pallas as pl
from jax.experimental.pallas import tpu as pltpu

F32 = jnp.float32
BF16 = jnp.bfloat16

EPS = 1e-6
N_CHIPS = 4
N_GROUPS = 4
HEAD_DIM = 128
RET_CHUNK = 128
ROPE_BASE = 10000.0
ADAM_LR, ADAM_B1, ADAM_B2, ADAM_EPS, ADAM_WD, ADAM_STEP = 0.001, 0.9, 0.999, 1e-08, 0.01, 10
VMEM_LIMIT_V7X = 56 * 1024 * 1024
MESH = pl.DeviceIdType.MESH
ANY = pl.BlockSpec(memory_space=pl.ANY)


def _dot(a, b):
    return jnp.dot(a, b, preferred_element_type=F32)


def _dot_nt(a, b):
    return lax.dot_general(a, b, (((1,), (1,)), ((), ())), preferred_element_type=F32)


def _dot_tn(a, b):
    return lax.dot_general(a, b, (((0,), (0,)), ((), ())), preferred_element_type=F32)


def _rstd(h):
    return lax.rsqrt(jnp.mean(h * h, axis=-1, keepdims=True) + EPS)


def _rmsnorm_bwd(dn, h, gain):
    r = _rstd(h)
    nh = h * r
    dnh = dn * gain
    dh = r * (dnh - nh * jnp.mean(dnh * nh, axis=-1, keepdims=True))
    return dh, dn * nh


def _silu_parts(a):
    sig = jax.nn.sigmoid(a)
    silu = a * sig
    return silu, sig + silu * (1.0 - sig)


def _mesh_pos():
    return lax.axis_index("x"), lax.axis_index("y"), lax.axis_index("c")


class ChipExchange:
    def __init__(self, srcs, scatter, placed=()):
        n = len(srcs)
        self.inputs, self.scatter, self.n = list(srcs) + list(placed), scatter, n
        self.aliases = {n + t: t for t in range(n)} if scatter else {}
        self.half_rows = [s.shape[1] if scatter else s.shape[0] // 2 for s in srcs]
        self.out_shape = [jax.ShapeDtypeStruct((N_CHIPS, 2 * rh, s.shape[-1]), s.dtype) for s, rh in zip(srcs, self.half_rows)]
        if scatter:
            self.out_shape += [jax.ShapeDtypeStruct((2, rh // 2, s.shape[-1]), s.dtype) for s, rh in zip(srcs, self.half_rows)]
        dma = pltpu.SemaphoreType.DMA
        self.sems = [dma((4 * n,)), dma((4 * n,)), dma((2 * n,)), dma((2 * n,)), dma((4 * n,)), dma((4 * n,))]

    def _copies(self, src, out, sems):
        hop1_send, hop1_recv, hop2_send, hop2_recv, d2d_send, d2d_recv = sems
        x, y, c = _mesh_pos()
        me, dg = 2 * x + y, 2 * (1 - x) + (1 - y)
        sibling = (x, y, 1 - c)
        n = self.n
        mine, theirs = c, 1 - c

        def nb(a):
            nx, ny = x ^ (1 - a), y ^ a
            return 2 * nx + ny, (nx, ny, c)

        def remote(s, d, send, recv, k, to):
            return pltpu.make_async_remote_copy(src_ref=s, dst_ref=d, send_sem=send.at[k], recv_sem=recv.at[k],
                                                device_id=to, device_id_type=MESH)

        class Copies:
            def slot(_, t, chip, half):
                rh = self.half_rows[t]
                return out[t].at[chip, pl.ds(half * rh, rh), :]

            def quarter(_, t, chip, q):
                qh = self.half_rows[t] // 2
                return out[t].at[chip, pl.ds(mine * 2 * qh + q * qh, qh), :]

            def own_shard(k, t):
                return remote(src[t], out[t].at[me], d2d_send, d2d_recv, 4 * t + 3, sibling)

            def hop1(k, t, a, transit=False):
                rh = self.half_rows[t]
                chip, to = nb(a)
                if transit:
                    piece = src[t].at[dg, pl.ds(a * (rh // 2), rh // 2), :]
                    return remote(piece, out[n + t].at[a], hop1_send, hop1_recv, 4 * t + 2 + a, to)
                piece = src[t].at[chip] if self.scatter else src[t].at[pl.ds(mine * rh, rh), :]
                return remote(piece, k.slot(t, me, mine), hop1_send, hop1_recv, 4 * t + a, to)

            def landed1(k, t, a, transit=False):
                here = out[n + t].at[a] if transit else k.slot(t, nb(a)[0], mine)
                return remote(here, here, hop1_send, hop1_recv, 4 * t + (2 if transit else 0) + a, sibling)

            def hop2(k, t, q):
                origin, to = nb(q)[0], nb(1 - q)[1]
                piece = out[n + t].at[q] if self.scatter else k.quarter(t, origin, q)
                return remote(piece, k.quarter(t, origin, q), hop2_send, hop2_recv, 2 * t + q, to)

            def landed2(k, t, q):
                here = k.quarter(t, dg, q)
                return remote(here, here, hop2_send, hop2_recv, 2 * t + q, sibling)

            def d2d(k, t, p, chip, own=False, arriving=False):
                if arriving:
                    there = k.slot(t, chip, theirs)
                    return remote(there, there, d2d_send, d2d_recv, 4 * t + p, sibling)
                piece = src[t].at[me] if own else k.slot(t, chip, mine)
                return remote(piece, k.slot(t, chip, mine), d2d_send, d2d_recv, 4 * t + p, sibling)

        return Copies(), nb, me, dg, c

    def start(self, src, out, sems):
        k, nb, me, dg, c = self._copies(src, out, sems)
        for t in range(self.n):
            for first in range(2):
                a = first ^ c
                k.hop1(t, a).start()
                if self.scatter:
                    k.hop1(t, a, transit=True).start()
            if self.scatter:
                k.d2d(t, 3, me, own=True).start()
            else:
                k.own_shard(t).start()

    def mid(self, src, out, sems):
        k, nb, me, dg, c = self._copies(src, out, sems)
        for t in range(self.n):
            for first in range(2):
                a = first ^ c
                if self.scatter:
                    k.landed1(t, a, transit=True).wait_recv()
                    k.hop2(t, a).start()
                k.landed1(t, a).wait_recv()
                if not self.scatter:
                    k.hop2(t, a).start()
                k.d2d(t, a, nb(a)[0]).start()

    def finish(self, src, out, sems):
        k, nb, me, dg, c = self._copies(src, out, sems)
        for t in range(self.n):
            for q in range(2):
                k.landed2(t, q).wait_recv()
            k.d2d(t, 2, dg).start()
        for t in range(self.n):
            for a in range(2):
                k.d2d(t, a, nb(a)[0], arriving=True).wait_recv()
            k.d2d(t, 2, dg, arriving=True).wait_recv()
            if self.scatter:
                k.d2d(t, 3, me, arriving=True).wait_recv()
        for t in range(self.n):
            for a in range(2):
                k.hop1(t, a).wait_send()
                if self.scatter:
                    k.hop1(t, a, transit=True).wait_send()
                k.hop2(t, a).wait_send()
                k.d2d(t, a, nb(a)[0]).wait_send()
            k.d2d(t, 2, dg).wait_send()
            if self.scatter:
                k.d2d(t, 3, me, own=True).wait_send()
            else:
                k.own_shard(t).wait()


class SiblingExchange:
    def __init__(self, grads):
        self.inputs, self.n, self.aliases = list(grads), len(grads), {}
        self.half_rows = [g.shape[1] // 2 for g in grads]
        self.out_shape = [jax.ShapeDtypeStruct((g.shape[0], rh, g.shape[2]), g.dtype) for g, rh in zip(grads, self.half_rows)]
        self.sems = [pltpu.SemaphoreType.DMA((self.n,)), pltpu.SemaphoreType.DMA((self.n,))]

    def _plan(self, src, out, sems):
        x, y, c = _mesh_pos()
        return [pltpu.make_async_remote_copy(
            src_ref=src[t].at[:, pl.ds((1 - c) * self.half_rows[t], self.half_rows[t]), :], dst_ref=out[t],
            send_sem=sems[0].at[t], recv_sem=sems[1].at[t], device_id=(x, y, 1 - c), device_id_type=MESH) for t in range(self.n)]

    def start(self, src, out, sems):
        for cp in self._plan(src, out, sems):
            cp.start()

    def mid(self, src, out, sems):
        pass

    def finish(self, src, out, sems):
        for cp in self._plan(src, out, sems):
            cp.wait()


def _call(body, hosted=(), *, name, in_specs, out_specs, out_shape, args, grid=(), scratch_shapes=()):
    n_in, n_out, n_scr = len(in_specs), len(out_specs), len(scratch_shapes)
    total = math.prod(grid)
    mid_step = max(0, (5 * total) // 8 - 1)

    def full(*refs):
        pos = [0]

        def take(k):
            pos[0] += k
            return refs[pos[0] - k:pos[0]]

        ins, h_in = take(n_in), [take(len(h.inputs)) for h in hosted]
        outs, h_out = take(n_out), [take(len(h.out_shape)) for h in hosted]
        scr, h_sem = take(n_scr), [take(len(h.sems)) for h in hosted]
        step = 0
        for axis, size in enumerate(grid):
            step = step * size + pl.program_id(axis)

        def phase(at, method):
            if not hosted:
                return
            if total == 1:
                for h, s, o, m in zip(hosted, h_in, h_out, h_sem):
                    getattr(h, method)(s, o, m)
                return

            @pl.when(step == at)
            def _():
                for h, s, o, m in zip(hosted, h_in, h_out, h_sem):
                    getattr(h, method)(s, o, m)

        phase(0, "start")
        body(*ins, *outs, *scr)
        phase(mid_step, "mid")
        phase(total - 1, "finish")

    aliases, i0, o0 = {}, n_in, n_out
    for h in hosted:
        aliases.update({i0 + i: o0 + o for i, o in h.aliases.items()})
        i0, o0 = i0 + len(h.inputs), o0 + len(h.out_shape)
    results = pl.pallas_call(
        full, name=name, grid=grid,
        in_specs=list(in_specs) + [ANY] * (i0 - n_in),
        out_specs=list(out_specs) + [ANY] * (o0 - n_out),
        out_shape=list(out_shape) + [s for h in hosted for s in h.out_shape],
        scratch_shapes=list(scratch_shapes) + [s for h in hosted for s in h.sems],
        input_output_aliases=aliases,
        compiler_params=pltpu.CompilerParams(vmem_limit_bytes=VMEM_LIMIT_V7X),
    )(*args, *[s for h in hosted for s in h.inputs])
    outs, extras, pos = list(results[:n_out]), [], n_out
    for h in hosted:
        extras.append(list(results[pos:pos + h.n]))
        pos += len(h.out_shape)
    return outs, extras


def exchange(hosted, name):
    return _call(lambda: None, hosted, name=name, in_specs=[], out_specs=[], out_shape=[], args=[])[1]


def all_exchange_small(pack, name):
    R, L = pack.shape
    flips = [(dx, dy, dc) for dx in (0, 1) for dy in (0, 1) for dc in (0, 1)][1:]

    def body(src, out, local_sem, send_sem, recv_sem):
        x, y, c = _mesh_pos()
        me = 4 * x + 2 * y + c
        copies = []
        for k, (dx, dy, dc) in enumerate(flips):
            copies.append(pltpu.make_async_remote_copy(
                src_ref=src, dst_ref=out.at[me], send_sem=send_sem.at[k], recv_sem=recv_sem.at[k],
                device_id=(x ^ dx, y ^ dy, c ^ dc), device_id_type=MESH))
        for cp in copies:
            cp.start()
        local = pltpu.make_async_copy(src, out.at[me], local_sem)
        local.start()
        for k, (dx, dy, dc) in enumerate(flips):
            landed = out.at[4 * (x ^ dx) + 2 * (y ^ dy) + (c ^ dc)]
            pltpu.make_async_remote_copy(src_ref=landed, dst_ref=landed, send_sem=send_sem.at[k], recv_sem=recv_sem.at[k],
                                         device_id=(x ^ dx, y ^ dy, c ^ dc), device_id_type=MESH).wait_recv()
        for cp in copies:
            cp.wait_send()
        local.wait()

    return pl.pallas_call(
        body, name=name, in_specs=[ANY], out_specs=ANY,
        out_shape=jax.ShapeDtypeStruct((2 * N_CHIPS, R, L), pack.dtype),
        scratch_shapes=[pltpu.SemaphoreType.DMA, pltpu.SemaphoreType.DMA((7,)), pltpu.SemaphoreType.DMA((7,))],
    )(pack)


def ffn_up(h, gain, w1g, w3g, name, hosted=()):
    T, D = h.shape
    nsh, _, Fs = w1g.shape
    tm = min(T, 1024)

    def body(h_ref, g_ref, w1_ref, w3_ref, n_ref, a_ref, b_ref, s_ref):
        @pl.when(pl.program_id(1) == 0)
        def _():
            hh = h_ref[...]
            n_ref[...] = (hh * _rstd(hh) * g_ref[...]).astype(BF16)

        n = n_ref[...]
        a = _dot(n, w1_ref[0])
        b = _dot(n, w3_ref[0])
        a_ref[0] = a.astype(BF16)
        b_ref[0] = b.astype(BF16)
        s_ref[0] = (a * jax.nn.sigmoid(a) * b).astype(BF16)

    act = jax.ShapeDtypeStruct((nsh, T, Fs), BF16)
    act_spec = pl.BlockSpec((1, tm, Fs), lambda i, j: (j, i, 0))
    w_spec = pl.BlockSpec((1, D, Fs), lambda i, j: (j, 0, 0))
    return _call(
        body, hosted, name=name, grid=(T // tm, nsh),
        in_specs=[pl.BlockSpec((tm, D), lambda i, j: (i, 0)), pl.BlockSpec((1, D), lambda i, j: (0, 0)), w_spec, w_spec],
        out_specs=[pl.BlockSpec((tm, D), lambda i, j: (i, 0)), act_spec, act_spec, act_spec],
        out_shape=[jax.ShapeDtypeStruct((T, D), BF16), act, act, act],
        args=[h, gain, w1g, w3g])


def ffn_down(s, w2g, h, name, hosted=()):
    nsh, T, Fs = s.shape
    D = h.shape[1]
    tm = min(T, 512)

    def body(s_ref, w2_ref, h_ref, o_ref):
        f = _dot(s_ref[0], w2_ref[0])
        for j in range(1, nsh):
            f += _dot(s_ref[j], w2_ref[j])
        o_ref[...] = h_ref[...] + 0.5 * f

    return _call(
        body, hosted, name=name, grid=(T // tm,),
        in_specs=[pl.BlockSpec((nsh, tm, Fs), lambda i: (0, i, 0)), pl.BlockSpec((nsh, Fs, D), lambda i: (0, 0, 0)),
                  pl.BlockSpec((tm, D), lambda i: (i, 0))],
        out_specs=[pl.BlockSpec((tm, D), lambda i: (i, 0))],
        out_shape=[jax.ShapeDtypeStruct((T, D), F32)],
        args=[s, w2g, h])


def ffn_bwd_act(dh, w2g, a, b, name, hosted=()):
    T, D = dh.shape
    nsh, Fs, _ = w2g.shape
    tm = min(T, 1024)

    def body(dh_ref, w2_ref, a_ref, b_ref, da_ref, db_ref):
        df = (0.5 * dh_ref[...]).astype(BF16)
        ds = _dot_nt(df, w2_ref[0])
        silu, dsilu = _silu_parts(a_ref[0].astype(F32))
        da_ref[0] = (ds * b_ref[0].astype(F32) * dsilu).astype(BF16)
        db_ref[0] = (ds * silu).astype(BF16)

    act = jax.ShapeDtypeStruct((nsh, T, Fs), BF16)
    act_spec = pl.BlockSpec((1, tm, Fs), lambda j, i: (j, i, 0))
    return _call(
        body, hosted, name=name, grid=(nsh, T // tm),
        in_specs=[pl.BlockSpec((tm, D), lambda j, i: (i, 0)), pl.BlockSpec((1, Fs, D), lambda j, i: (j, 0, 0)), act_spec, act_spec],
        out_specs=[act_spec, act_spec],
        out_shape=[act, act],
        args=[dh, w2g, a, b])


def ffn_dw2(s, dh, name, hosted=()):
    nsh, T, Fs = s.shape
    D = dh.shape[1]
    tk = min(T, 512)
    nk = T // tk

    def body(s_ref, dh_ref, o_ref, acc):
        k = pl.program_id(1)

        @pl.when(k == 0)
        def _():
            acc[...] = jnp.zeros_like(acc)

        acc[...] += _dot_tn(s_ref[0], (0.5 * dh_ref[...]).astype(BF16))

        @pl.when(k == nk - 1)
        def _():
            o_ref[0] = acc[...].astype(BF16)

    return _call(
        body, hosted, name=name, grid=(nsh, nk),
        in_specs=[pl.BlockSpec((1, tk, Fs), lambda j, k: (j, k, 0)), pl.BlockSpec((tk, D), lambda j, k: (k, 0))],
        out_specs=[pl.BlockSpec((1, Fs, D), lambda j, k: (j, 0, 0))],
        out_shape=[jax.ShapeDtypeStruct((nsh, Fs, D), BF16)],
        scratch_shapes=[pltpu.VMEM((Fs, D), F32)],
        args=[s, dh])


def ffn_dw13(n, da, db, name, hosted=()):
    T, D = n.shape
    nsh, _, Fs = da.shape
    tk = min(T, 512)
    nk = T // tk

    def body(n_ref, da_ref, db_ref, o1_ref, o3_ref, acc1, acc3):
        k = pl.program_id(1)

        @pl.when(k == 0)
        def _():
            acc1[...] = jnp.zeros_like(acc1)
            acc3[...] = jnp.zeros_like(acc3)

        nn = n_ref[...]
        acc1[...] += _dot_tn(nn, da_ref[0])
        acc3[...] += _dot_tn(nn, db_ref[0])

        @pl.when(k == nk - 1)
        def _():
            o1_ref[0] = acc1[...].astype(BF16)
            o3_ref[0] = acc3[...].astype(BF16)

    act_spec = pl.BlockSpec((1, tk, Fs), lambda j, k: (j, k, 0))
    out = jax.ShapeDtypeStruct((nsh, D, Fs), BF16)
    out_spec = pl.BlockSpec((1, D, Fs), lambda j, k: (j, 0, 0))
    return _call(
        body, hosted, name=name, grid=(nsh, nk),
        in_specs=[pl.BlockSpec((tk, D), lambda j, k: (k, 0)), act_spec, act_spec],
        out_specs=[out_spec, out_spec],
        out_shape=[out, out],
        scratch_shapes=[pltpu.VMEM((D, Fs), F32), pltpu.VMEM((D, Fs), F32)],
        args=[n, da, db])


def ffn_bwd_in(da, db, w1g, w3g, h, gain, dh, name, hosted=()):
    nsh, T, Fs = da.shape
    D = h.shape[1]
    tm = min(T, 256)

    def body(da_ref, db_ref, w1_ref, w3_ref, h_ref, g_ref, dh_ref, o_ref, dg_ref):
        dn = _dot_nt(da_ref[0], w1_ref[0]) + _dot_nt(db_ref[0], w3_ref[0])
        for j in range(1, nsh):
            dn += _dot_nt(da_ref[j], w1_ref[j]) + _dot_nt(db_ref[j], w3_ref[j])
        dhn, dg = _rmsnorm_bwd(dn, h_ref[...], g_ref[...])
        o_ref[...] = dh_ref[...] + dhn

        @pl.when(pl.program_id(0) == 0)
        def _():
            dg_ref[...] = jnp.zeros_like(dg_ref)

        dg_ref[...] += jnp.sum(dg, axis=0, keepdims=True)

    act_spec = pl.BlockSpec((nsh, tm, Fs), lambda i: (0, i, 0))
    w_spec = pl.BlockSpec((nsh, D, Fs), lambda i: (0, 0, 0))
    row_spec = pl.BlockSpec((tm, D), lambda i: (i, 0))
    vec_spec = pl.BlockSpec((1, D), lambda i: (0, 0))
    return _call(
        body, hosted, name=name, grid=(T // tm,),
        in_specs=[act_spec, act_spec, w_spec, w_spec, row_spec, vec_spec, row_spec],
        out_specs=[row_spec, vec_spec],
        out_shape=[jax.ShapeDtypeStruct((T, D), F32), jax.ShapeDtypeStruct((1, D), F32)],
        args=[da, db, w1g, w3g, h, gain, dh])


def mix_in(h, gain, wing, name, hosted=()):
    T, D = h.shape
    nsh, _, Cs = wing.shape
    tm = min(T, 512)

    def body(h_ref, g_ref, w_ref, u_ref, p_ref):
        hh = h_ref[...]
        u = (hh * _rstd(hh) * g_ref[...]).astype(BF16)
        u_ref[...] = u
        for j in range(nsh):
            p_ref[:, j * Cs:(j + 1) * Cs] = _dot(u, w_ref[j])

    return _call(
        body, hosted, name=name, grid=(T // tm,),
        in_specs=[pl.BlockSpec((tm, D), lambda i: (i, 0)), pl.BlockSpec((1, D), lambda i: (0, 0)),
                  pl.BlockSpec((nsh, D, Cs), lambda i: (0, 0, 0))],
        out_specs=[pl.BlockSpec((tm, D), lambda i: (i, 0)), pl.BlockSpec((tm, nsh * Cs), lambda i: (i, 0))],
        out_shape=[jax.ShapeDtypeStruct((T, D), BF16), jax.ShapeDtypeStruct((T, nsh * Cs), F32)],
        args=[h, gain, wing])


def mix_out(a, b, woutg, h, name, hosted=()):
    T, W = a.shape
    D = h.shape[1]
    wout = woutg.reshape(2, W, D)
    tm = min(T, 512)

    def body(a_ref, b_ref, w_ref, h_ref, o_ref):
        o_ref[...] = h_ref[...] + _dot(a_ref[...], w_ref[0]) + _dot(b_ref[...], w_ref[1])

    return _call(
        body, hosted, name=name, grid=(T // tm,),
        in_specs=[pl.BlockSpec((tm, W), lambda i: (i, 0)), pl.BlockSpec((tm, W), lambda i: (i, 0)),
                  pl.BlockSpec((2, W, D), lambda i: (0, 0, 0)), pl.BlockSpec((tm, D), lambda i: (i, 0))],
        out_specs=[pl.BlockSpec((tm, D), lambda i: (i, 0))],
        out_shape=[jax.ShapeDtypeStruct((T, D), F32)],
        args=[a, b, wout, h])


def mix_out_bwd(dh, woutg, a, b, name, hosted=()):
    T, D = dh.shape
    W = a.shape[1]
    nsh, Rs, _ = woutg.shape
    wout = woutg.reshape(2, W, D)
    tk = min(T, 512)
    nk = T // tk

    def body(dh_ref, w_ref, a_ref, b_ref, da_ref, db_ref, dw_ref, acc):
        k = pl.program_id(0)

        @pl.when(k == 0)
        def _():
            acc[...] = jnp.zeros_like(acc)

        dhb = dh_ref[...].astype(BF16)
        da_ref[...] = _dot_nt(dhb, w_ref[0])
        db_ref[...] = _dot_nt(dhb, w_ref[1])
        acc[0:W, :] += _dot_tn(a_ref[...], dhb)
        acc[W:2 * W, :] += _dot_tn(b_ref[...], dhb)

        @pl.when(k == nk - 1)
        def _():
            for j in range(nsh):
                dw_ref[j] = acc[j * Rs:(j + 1) * Rs, :].astype(BF16)

    return _call(
        body, hosted, name=name, grid=(nk,),
        in_specs=[pl.BlockSpec((tk, D), lambda k: (k, 0)), pl.BlockSpec((2, W, D), lambda k: (0, 0, 0)),
                  pl.BlockSpec((tk, W), lambda k: (k, 0)), pl.BlockSpec((tk, W), lambda k: (k, 0))],
        out_specs=[pl.BlockSpec((tk, W), lambda k: (k, 0)), pl.BlockSpec((tk, W), lambda k: (k, 0)),
                   pl.BlockSpec((nsh, Rs, D), lambda k: (0, 0, 0))],
        out_shape=[jax.ShapeDtypeStruct((T, W), F32), jax.ShapeDtypeStruct((T, W), F32),
                   jax.ShapeDtypeStruct((nsh, Rs, D), BF16)],
        scratch_shapes=[pltpu.VMEM((2 * W, D), F32)],
        args=[dh, wout, a, b])


def mix_dwin(u, d, nsh, name, hosted=()):
    T, D = u.shape
    Cs = d.shape[1] // nsh
    tk = min(T, 512)
    nk = T // tk

    def body(u_ref, d_ref, o_ref, acc):
        k = pl.program_id(1)

        @pl.when(k == 0)
        def _():
            acc[...] = jnp.zeros_like(acc)

        acc[...] += _dot_tn(u_ref[...], d_ref[...])

        @pl.when(k == nk - 1)
        def _():
            o_ref[0] = acc[...].astype(BF16)

    return _call(
        body, hosted, name=name, grid=(nsh, nk),
        in_specs=[pl.BlockSpec((tk, D), lambda j, k: (k, 0)), pl.BlockSpec((tk, Cs), lambda j, k: (k, j))],
        out_specs=[pl.BlockSpec((1, D, Cs), lambda j, k: (j, 0, 0))],
        out_shape=[jax.ShapeDtypeStruct((nsh, D, Cs), BF16)],
        scratch_shapes=[pltpu.VMEM((D, Cs), F32)],
        args=[u, d])


def mix_in_bwd(d, wing, h, gain, dh, name, hosted=()):
    T, D = h.shape
    nsh, _, Cs = wing.shape
    tm = min(T, 512)

    def body(d_ref, w_ref, h_ref, g_ref, dh_ref, o_ref, dg_ref):
        du = _dot_nt(d_ref[:, 0:Cs], w_ref[0])
        for j in range(1, nsh):
            du += _dot_nt(d_ref[:, j * Cs:(j + 1) * Cs], w_ref[j])
        dhn, dg = _rmsnorm_bwd(du, h_ref[...], g_ref[...])
        o_ref[...] = dh_ref[...] + dhn

        @pl.when(pl.program_id(0) == 0)
        def _():
            dg_ref[...] = jnp.zeros_like(dg_ref)

        dg_ref[...] += jnp.sum(dg, axis=0, keepdims=True)

    row_spec = pl.BlockSpec((tm, D), lambda i: (i, 0))
    vec_spec = pl.BlockSpec((1, D), lambda i: (0, 0))
    return _call(
        body, hosted, name=name, grid=(T // tm,),
        in_specs=[pl.BlockSpec((tm, nsh * Cs), lambda i: (i, 0)), pl.BlockSpec((nsh, D, Cs), lambda i: (0, 0, 0)),
                  row_spec, vec_spec, row_spec],
        out_specs=[row_spec, vec_spec],
        out_shape=[jax.ShapeDtypeStruct((T, D), F32), jax.ShapeDtypeStruct((1, D), F32)],
        args=[d, wing, h, gain, dh])


def _pool_window(x, group, T, trailing):
    rows = lax.broadcasted_iota(jnp.int32, x.shape, 0)

    def shifted(z, k):
        if trailing:
            return jnp.where(rows >= k, pltpu.roll(z, k, 0), 0.0)
        return jnp.where(rows < T - k, pltpu.roll(z, T - k, 0), 0.0)

    s2 = x + shifted(x, 1)
    s4 = s2 + shifted(s2, 2)
    s8 = s4 + shifted(s4, 4)
    s16 = s8 + shifted(s8, 8)
    return jnp.where(group == 0, s2, jnp.where(group == 1, s4, jnp.where(group == 2, s8, s16)))


def _pool_count(group, shape):
    rows = lax.broadcasted_iota(jnp.int32, shape, 0)
    w = jnp.where(group == 0, 2, jnp.where(group == 1, 4, jnp.where(group == 2, 8, 16)))
    return jnp.minimum(rows + 1, w).astype(F32)


def pool_fwd(proj, pool_w, pool_scale, name, hosted=()):
    T = proj.shape[0]
    Hd = HEAD_DIM

    def body(x_ref, w_ref, sc_ref, a_ref):
        g = pl.program_id(0)
        x = x_ref[...]
        pooled = _pool_window(x, g, T, True) / _pool_count(g, x.shape) - x
        a_ref[...] = (_dot(pooled.astype(BF16), w_ref[0].astype(BF16)) * sc_ref[...]).astype(BF16)

    return _call(
        body, hosted, name=name, grid=(N_GROUPS,),
        in_specs=[pl.BlockSpec((T, Hd), lambda g: (0, g)), pl.BlockSpec((1, Hd, Hd), lambda g: (g, 0, 0)),
                  pl.BlockSpec((1, Hd), lambda g: (0, g))],
        out_specs=[pl.BlockSpec((T, Hd), lambda g: (0, g))],
        out_shape=[jax.ShapeDtypeStruct((T, N_GROUPS * Hd), BF16)],
        args=[proj, pool_w, pool_scale])


def pool_bwd(proj, da, pool_w, pool_scale, name, hosted=()):
    T = proj.shape[0]
    Hd = HEAD_DIM

    def body(x_ref, da_ref, w_ref, sc_ref, dx_ref, dw_ref, dsc_ref):
        g = pl.program_id(0)
        x = x_ref[...]
        cnt = _pool_count(g, x.shape)
        pooled = (_pool_window(x, g, T, True) / cnt - x).astype(BF16)
        wb = w_ref[0].astype(BF16)
        dav = da_ref[...]
        dsc_ref[...] = jnp.sum(dav * _dot(pooled, wb), axis=0, keepdims=True)
        dout = (dav * sc_ref[...]).astype(BF16)
        dw_ref[0] = _dot_tn(pooled, dout)
        dpooled = _dot_nt(dout, wb)
        dx_ref[...] = (_pool_window(dpooled / cnt, g, T, False) - dpooled).astype(BF16)

    col_spec = pl.BlockSpec((T, Hd), lambda g: (0, g))
    return _call(
        body, hosted, name=name, grid=(N_GROUPS,),
        in_specs=[col_spec, col_spec, pl.BlockSpec((1, Hd, Hd), lambda g: (g, 0, 0)), pl.BlockSpec((1, Hd), lambda g: (0, g))],
        out_specs=[col_spec, pl.BlockSpec((1, Hd, Hd), lambda g: (g, 0, 0)), pl.BlockSpec((1, Hd), lambda g: (0, g))],
        out_shape=[jax.ShapeDtypeStruct((T, N_GROUPS * Hd), BF16), jax.ShapeDtypeStruct((N_GROUPS, Hd, Hd), F32),
                   jax.ShapeDtypeStruct((1, N_GROUPS * Hd), F32)],
        args=[proj, da, pool_w, pool_scale])


def _ret_tables(T):
    Hd, C = HEAD_DIM, RET_CHUNK
    inv_freq = 1.0 / (ROPE_BASE ** (jnp.arange(0, Hd, 2, dtype=F32) / Hd))
    ang = jnp.arange(T, dtype=F32)[:, None] * inv_freq[None, :]
    cos, sin = jnp.cos(ang), jnp.sin(ang)
    cos2 = jnp.concatenate([cos, cos], axis=-1)
    sin2 = jnp.concatenate([-sin, sin], axis=-1)
    log_gamma = jnp.log1p(-jnp.exp2(-5.0 - jnp.arange(N_GROUPS, dtype=F32)))
    pos = jnp.arange(C, dtype=F32)
    rel = pos[:, None] - pos[None, :]
    intra = jnp.where(rel[None] >= 0, jnp.exp(log_gamma[:, None, None] * jnp.maximum(rel, 0.0)[None]), 0.0)
    k_tail = jnp.exp(log_gamma[:, None] * (C - 1 - pos)[None, :])
    q_head = jnp.exp(log_gamma[:, None] * (pos + 1.0)[None, :])
    chunk_decay = jnp.exp(log_gamma * C)
    wide = lambda t: jnp.broadcast_to(t[:, :, None], (N_GROUPS, C, Hd))
    return cos2, sin2, intra, wide(k_tail), wide(q_head), jnp.broadcast_to(chunk_decay[:, None, None], (N_GROUPS, 1, Hd))


def _rope(x, cos2, sin2):
    return x * cos2 + pltpu.roll(x, HEAD_DIM // 2, 1) * sin2


def _rope_t(d, cos2, sin2):
    return d * cos2 + pltpu.roll(d * sin2, HEAD_DIM // 2, 1)


def _ret_specs(tseg, seg_of):
    Hd, G = HEAD_DIM, N_GROUPS
    col = lambda kind: pl.BlockSpec((tseg, Hd), lambda h, s: (seg_of(s), G * kind + h))
    tab = pl.BlockSpec((tseg, Hd), lambda h, s: (seg_of(s), 0))
    head = pl.BlockSpec((1, RET_CHUNK, Hd), lambda h, s: (h, 0, 0))
    cd = pl.BlockSpec((1, 1, Hd), lambda h, s: (h, 0, 0))
    gain = pl.BlockSpec((1, Hd), lambda h, s: (0, h))
    return col, tab, head, cd, gain


def ret_fwd(proj, ret_norm, tables, name, hosted=()):
    T = proj.shape[0]
    Hd, C, G = HEAD_DIM, RET_CHUNK, N_GROUPS
    tseg = min(T, 1024)
    nseg, nck = T // tseg, tseg // C
    scale = Hd ** -0.5
    cos2, sin2, intra, k_tail, q_head, chunk_decay = tables

    def body(q_ref, k_ref, v_ref, g_ref, gain_ref, cos_ref, sin_ref, m_ref, kt_ref, qh_ref, cd_ref,
             b_ref, o_ref, rp_ref, state):
        @pl.when(pl.program_id(1) == 0)
        def _():
            state[...] = jnp.zeros_like(state)

        def chunk(ci, carry):
            rows = pl.ds(pl.multiple_of(ci * C, C), C)
            cos, sin = cos_ref[rows, :], sin_ref[rows, :]
            qr = _rope(q_ref[rows, :], cos, sin)
            kr = _rope(k_ref[rows, :], cos, sin) * scale
            qb, kb, vb = qr.astype(BF16), kr.astype(BF16), v_ref[rows, :].astype(BF16)
            r = state[...]
            rp_ref[0, ci] = r.astype(BF16)
            sc = _dot_nt(qb, kb) * m_ref[0]
            o = _dot(sc.astype(BF16), vb) + _dot((qr * qh_ref[0]).astype(BF16), r.astype(BF16))
            state[...] = cd_ref[0] * r + _dot_tn((kr * kt_ref[0]).astype(BF16), vb)
            o_ref[rows, :] = o
            on = o * _rstd(o)
            b_ref[rows, :] = (jax.nn.silu(g_ref[rows, :]) * (on * gain_ref[...])).astype(BF16)
            return carry

        lax.fori_loop(0, nck, chunk, 0)

    col, tab, head, cd, gain = _ret_specs(tseg, lambda s: s)
    out_col = pl.BlockSpec((tseg, Hd), lambda h, s: (s, h))
    return _call(
        body, hosted, name=name, grid=(G, nseg),
        in_specs=[col(1), col(2), col(3), col(4), gain, tab, tab, head, head, head, cd],
        out_specs=[out_col, out_col, pl.BlockSpec((1, nck, Hd, Hd), lambda h, s: (h, s, 0, 0))],
        out_shape=[jax.ShapeDtypeStruct((T, G * Hd), BF16), jax.ShapeDtypeStruct((T, G * Hd), F32),
                   jax.ShapeDtypeStruct((G, T // C, Hd, Hd), BF16)],
        scratch_shapes=[pltpu.VMEM((Hd, Hd), F32)],
        args=[proj, proj, proj, proj, ret_norm, cos2, sin2, intra, k_tail, q_head, chunk_decay])


def ret_bwd(proj, db, o_pre, r_prev, ret_norm, tables, name, hosted=()):
    T = proj.shape[0]
    Hd, C, G = HEAD_DIM, RET_CHUNK, N_GROUPS
    tseg = min(T, 1024)
    nseg, nck = T // tseg, tseg // C
    scale = Hd ** -0.5
    cos2, sin2, intra, k_tail, q_head, chunk_decay = tables

    def body(q_ref, k_ref, v_ref, g_ref, db_ref, o_ref, rp_ref, gain_ref, cos_ref, sin_ref, m_ref, kt_ref, qh_ref, cd_ref,
             d_ref, dgain_ref, gstate):
        @pl.when(pl.program_id(1) == 0)
        def _():
            gstate[...] = jnp.zeros_like(gstate)
            dgain_ref[...] = jnp.zeros_like(dgain_ref)

        def chunk(t, carry):
            ci = nck - 1 - t
            rows = pl.ds(pl.multiple_of(ci * C, C), C)
            cos, sin = cos_ref[rows, :], sin_ref[rows, :]
            qr = _rope(q_ref[rows, :], cos, sin)
            kr = _rope(k_ref[rows, :], cos, sin) * scale
            qb, kb, vb = qr.astype(BF16), kr.astype(BF16), v_ref[rows, :].astype(BF16)
            qhb, ktb = (qr * qh_ref[0]).astype(BF16), (kr * kt_ref[0]).astype(BF16)
            sc = (_dot_nt(qb, kb) * m_ref[0]).astype(BF16)
            o = o_ref[rows, :]
            rstd = _rstd(o)
            on = o * rstd
            gain = gain_ref[...]
            silu, dsilu = _silu_parts(g_ref[rows, :])
            dy = db_ref[rows, :]
            dgain_ref[...] += jnp.sum(dy * silu * on, axis=0, keepdims=True)
            dg = dy * on * gain * dsilu
            don = dy * silu * gain
            dob = (rstd * (don - on * jnp.mean(don * on, axis=-1, keepdims=True))).astype(BF16)
            gn = gstate[...]
            gb = gn.astype(BF16)
            da = (_dot_nt(dob, vb) * m_ref[0]).astype(BF16)
            dq = _dot(da, kb) + _dot_nt(dob, rp_ref[0, ci]) * qh_ref[0]
            dk = _dot_tn(da, qb) + _dot_nt(vb, gb) * kt_ref[0]
            dv = _dot_tn(sc, dob) + _dot(ktb, gb)
            gstate[...] = cd_ref[0] * gn + _dot_tn(qhb, dob)
            d_ref[0, rows, :] = _rope_t(dq, cos, sin).astype(BF16)
            d_ref[1, rows, :] = _rope_t(dk * scale, cos, sin).astype(BF16)
            d_ref[2, rows, :] = dv.astype(BF16)
            d_ref[3, rows, :] = dg.astype(BF16)
            return carry

        lax.fori_loop(0, nck, chunk, 0)

    rev = lambda s: nseg - 1 - s
    col, tab, head, cd, gain = _ret_specs(tseg, rev)
    act = pl.BlockSpec((tseg, Hd), lambda h, s: (rev(s), h))
    return _call(
        body, hosted, name=name, grid=(G, nseg),
        in_specs=[col(1), col(2), col(3), col(4), act, act, pl.BlockSpec((1, nck, Hd, Hd), lambda h, s: (h, rev(s), 0, 0)),
                  gain, tab, tab, head, head, head, cd],
        out_specs=[pl.BlockSpec((4, tseg, Hd), lambda h, s: (0, rev(s), h)), gain],
        out_shape=[jax.ShapeDtypeStruct((4, T, G * Hd), BF16), jax.ShapeDtypeStruct((1, G * Hd), F32)],
        scratch_shapes=[pltpu.VMEM((Hd, Hd), F32)],
        args=[proj, proj, proj, proj, db, o_pre, r_prev, ret_norm, cos2, sin2, intra, k_tail, q_head, chunk_decay])


def final_loss(h, gain, target, name, hosted=()):
    T, D = h.shape
    tm = min(T, 512)

    def body(h_ref, g_ref, t_ref, dh_ref, loss_ref, dg_ref):
        @pl.when(pl.program_id(0) == 0)
        def _():
            loss_ref[...] = jnp.zeros_like(loss_ref)
            dg_ref[...] = jnp.zeros_like(dg_ref)

        hh = h_ref[...]
        gain_v = g_ref[...]
        err = hh * _rstd(hh) * gain_v - t_ref[...]
        loss_ref[...] += 0.5 * jnp.sum(jnp.mean(err * err, axis=-1, keepdims=True), axis=0, keepdims=True)
        dhn, dg = _rmsnorm_bwd(err * (1.0 / D), hh, gain_v)
        dh_ref[...] = dhn
        dg_ref[...] += jnp.sum(dg, axis=0, keepdims=True)

    row_spec = pl.BlockSpec((tm, D), lambda i: (i, 0))
    vec_spec = pl.BlockSpec((1, D), lambda i: (0, 0))
    return _call(
        body, hosted, name=name, grid=(T // tm,),
        in_specs=[row_spec, vec_spec, row_spec],
        out_specs=[row_spec, pl.BlockSpec((1, 128), lambda i: (0, 0)), vec_spec],
        out_shape=[jax.ShapeDtypeStruct((T, D), F32), jax.ShapeDtypeStruct((1, 128), F32), jax.ShapeDtypeStruct((1, D), F32)],
        args=[h, gain, target])


def prereduce(grad, recv, place, name):
    nsh, R, C = grad.shape
    rh = R // 2

    def body(place_ref, g_ref, r_ref, o_ref, own_ref):
        piece = (g_ref[...].astype(F32) + r_ref[...].astype(F32)).astype(BF16)
        o_ref[...] = piece

        @pl.when(pl.program_id(0) == place_ref[1])
        def _():
            own_ref[...] = piece

    return pl.pallas_call(
        body, name=name,
        grid_spec=pltpu.PrefetchScalarGridSpec(
            num_scalar_prefetch=1, grid=(nsh,),
            in_specs=[pl.BlockSpec((1, rh, C), lambda j, p: (j, p[0], 0)), pl.BlockSpec((1, rh, C), lambda j, p: (j, 0, 0))],
            out_specs=[pl.BlockSpec((1, rh, C), lambda j, p: (j, 0, 0)), pl.BlockSpec((1, rh, C), lambda j, p: (p[1], p[0], 0))]),
        out_shape=[jax.ShapeDtypeStruct((nsh, rh, C), BF16), jax.ShapeDtypeStruct((nsh, R, C), BF16)],
        compiler_params=pltpu.CompilerParams(vmem_limit_bytes=VMEM_LIMIT_V7X),
    )(place, grad, recv)


def _adamw(w, g, m, v):
    m = ADAM_B1 * m + (1.0 - ADAM_B1) * g
    v = ADAM_B2 * v + (1.0 - ADAM_B2) * (g * g)
    m_hat = m / (1.0 - ADAM_B1 ** ADAM_STEP)
    v_hat = v / (1.0 - ADAM_B2 ** ADAM_STEP)
    return -ADAM_LR * (m_hat / (jnp.sqrt(v_hat) + ADAM_EPS) + ADAM_WD * w), m, v


def adamw_sharded(tensors, name, hosted=()):
    nt = len(tensors)
    nsh, R, C = tensors[0][0].shape
    tr = 256 if R % 256 == 0 else R // 2

    def body(*refs):
        ins, outs = refs[:4 * nt], refs[4 * nt:]
        for t in range(nt):
            p_ref, w_ref, m_ref, v_ref = ins[4 * t:4 * t + 4]
            g_ref, d_ref, nm_ref, nv_ref = outs[4 * t:4 * t + 4]
            g = p_ref[0].astype(F32)
            for i in range(1, nsh):
                g += p_ref[i].astype(F32)
            g_ref[...] = g
            d_ref[...], nm_ref[...], nv_ref[...] = _adamw(w_ref[...], g, m_ref[...], v_ref[...])

    spec = pl.BlockSpec((tr, C), lambda i: (i, 0))
    out = jax.ShapeDtypeStruct((R, C), F32)
    return _call(
        body, hosted, name=name, grid=(R // tr,),
        in_specs=[pl.BlockSpec((nsh, tr, C), lambda i: (0, i, 0)), spec, spec, spec] * nt,
        out_specs=[spec] * (4 * nt), out_shape=[out] * (4 * nt),
        args=[a for tensor in tensors for a in tensor])


def adamw_small(packs, w, m, v, name):
    ndev, R, L = packs.shape

    def body(p_ref, w_ref, m_ref, v_ref, g_ref, d_ref, nm_ref, nv_ref):
        g = p_ref[0]
        for i in range(1, ndev):
            g += p_ref[i]
        g_ref[...] = g
        d_ref[...], nm_ref[...], nv_ref[...] = _adamw(w_ref[...], g, m_ref[...], v_ref[...])

    out = jax.ShapeDtypeStruct((R, L), F32)
    return pl.pallas_call(body, name=name, out_shape=[out] * 4,
                          compiler_params=pltpu.CompilerParams(vmem_limit_bytes=VMEM_LIMIT_V7X))(packs, w, m, v)


BIG = ("ffn1_w1", "ffn1_w3", "ffn1_w2", "w_in", "w_out", "ffn2_w1", "ffn2_w3", "ffn2_w2")
SMALL = ("ffn1_norm", "mix_norm", "pool_w", "pool_scale", "ret_norm", "ffn2_norm", "final_norm")
WEIGHTS = ("ffn1_norm", "ffn1_w1", "ffn1_w3", "ffn1_w2", "mix_norm", "w_in", "pool_w", "pool_scale", "ret_norm", "w_out",
           "ffn2_norm", "ffn2_w1", "ffn2_w3", "ffn2_w2", "final_norm")


def _pack(parts):
    return jnp.concatenate([parts[k].reshape(-1, 128) for k in SMALL], axis=0)


def _unpack(pack, like):
    out, row = {}, 0
    for k in SMALL:
        rows = like[k].size // 128
        out[k] = pack[row:row + rows].reshape(like[k].shape)
        row += rows
    return out


def kernel(x, ffn1_norm, ffn1_w1, ffn1_w3, ffn1_w2, mix_norm, w_in, pool_w, pool_scale, ret_norm, w_out, ffn2_norm, ffn2_w1, ffn2_w3, ffn2_w2, final_norm, loss_target, m_ffn1_norm, m_ffn1_w1, m_ffn1_w3, m_ffn1_w2, m_mix_norm, m_w_in, m_pool_w, m_pool_scale, m_ret_norm, m_w_out, m_ffn2_norm, m_ffn2_w1, m_ffn2_w3, m_ffn2_w2, m_final_norm, v_ffn1_norm, v_ffn1_w1, v_ffn1_w3, v_ffn1_w2, v_mix_norm, v_w_in, v_pool_w, v_pool_scale, v_ret_norm, v_w_out, v_ffn2_norm, v_ffn2_w1, v_ffn2_w3, v_ffn2_w2, v_final_norm):
    w = dict(ffn1_norm=ffn1_norm, ffn1_w1=ffn1_w1, ffn1_w3=ffn1_w3, ffn1_w2=ffn1_w2, mix_norm=mix_norm, w_in=w_in, pool_w=pool_w,
             pool_scale=pool_scale, ret_norm=ret_norm, w_out=w_out, ffn2_norm=ffn2_norm, ffn2_w1=ffn2_w1, ffn2_w3=ffn2_w3,
             ffn2_w2=ffn2_w2, final_norm=final_norm)
    m = dict(ffn1_norm=m_ffn1_norm, ffn1_w1=m_ffn1_w1, ffn1_w3=m_ffn1_w3, ffn1_w2=m_ffn1_w2, mix_norm=m_mix_norm, w_in=m_w_in,
             pool_w=m_pool_w, pool_scale=m_pool_scale, ret_norm=m_ret_norm, w_out=m_w_out, ffn2_norm=m_ffn2_norm, ffn2_w1=m_ffn2_w1,
             ffn2_w3=m_ffn2_w3, ffn2_w2=m_ffn2_w2, final_norm=m_final_norm)
    v = dict(ffn1_norm=v_ffn1_norm, ffn1_w1=v_ffn1_w1, ffn1_w3=v_ffn1_w3, ffn1_w2=v_ffn1_w2, mix_norm=v_mix_norm, w_in=v_w_in,
             pool_w=v_pool_w, pool_scale=v_pool_scale, ret_norm=v_ret_norm, w_out=v_w_out, ffn2_norm=v_ffn2_norm, ffn2_w1=v_ffn2_w1,
             ffn2_w3=v_ffn2_w3, ffn2_w2=v_ffn2_w2, final_norm=v_final_norm)
    xs, target = x[0], loss_target[0]
    T = xs.shape[0]
    tables = _ret_tables(T)
    place = jnp.stack([lax.axis_index("c"), 2 * lax.axis_index("x") + lax.axis_index("y")]).astype(jnp.int32)
    sh = {k: w[k][0].astype(BF16) for k in BIG}
    gather = lambda *names: [ChipExchange([sh[k] for k in names], False)]
    wg, grad, delta, new_m, new_v = {}, {}, {}, {}, {}

    def update(names, pieces, name, hosted=()):
        outs, extras = adamw_sharded([(p, w[k][0], m[k][0], v[k][0]) for k, p in zip(names, pieces)], name, hosted)
        for t, k in enumerate(names):
            grad[k], delta[k], new_m[k], new_v[k] = [o[None] for o in outs[4 * t:4 * t + 4]]
        return extras

    def reduce_in_chip(name, partial, recv):
        return prereduce(partial, recv, place, "prereduce_" + name)

    scatter = lambda *reduced: ChipExchange([r[0] for r in reduced], True, [r[1] for r in reduced])

    (wg["ffn1_w1"], wg["ffn1_w3"]), = exchange(gather("ffn1_w1", "ffn1_w3"), "gather_ffn1")
    (n1, a1, b1, s1), ((wg["ffn1_w2"], wg["w_in"]),) = ffn_up(
        xs, ffn1_norm, wg["ffn1_w1"], wg["ffn1_w3"], "ffn1_up", gather("ffn1_w2", "w_in"))
    (h1,), ((wg["w_out"],),) = ffn_down(s1, wg["ffn1_w2"], xs, "ffn1_down", gather("w_out"))
    (u, proj), ((wg["ffn2_w1"],),) = mix_in(h1, mix_norm, wg["w_in"], "mix_in", gather("ffn2_w1"))
    (pa,), _ = pool_fwd(proj, pool_w[0], pool_scale, "pool_fwd")
    (rb, o_pre, r_prev), ((wg["ffn2_w3"],),) = ret_fwd(proj, ret_norm, tables, "ret_fwd", gather("ffn2_w3"))
    (h2,), _ = mix_out(pa, rb, wg["w_out"], h1, "mix_out")
    (n2, a2, b2, s2), ((wg["ffn2_w2"],),) = ffn_up(
        h2, ffn2_norm, wg["ffn2_w1"], wg["ffn2_w3"], "ffn2_up", gather("ffn2_w2"))
    (h3,), _ = ffn_down(s2, wg["ffn2_w2"], h2, "ffn2_down")
    (dh3, loss, d_final), _ = final_loss(h3, final_norm[None], target, "final_loss")
    loss = lax.psum(loss[0, 0], ("x", "y", "c"))

    (da2, db2), _ = ffn_bwd_act(dh3, wg["ffn2_w2"], a2, b2, "ffn2_bwd_act")
    (g_f2w2,), _ = ffn_dw2(s2, dh3, "ffn2_dw2")
    (g_f2w1, g_f2w3), ((r_f2w2,),) = ffn_dw13(n2, da2, db2, "ffn2_dw13", [SiblingExchange([g_f2w2])])
    p_f2w2 = reduce_in_chip("ffn2_w2", g_f2w2, r_f2w2)
    (dh2, d_ffn2), ((q_f2w2,), (r_f2w1, r_f2w3)) = ffn_bwd_in(
        da2, db2, wg["ffn2_w1"], wg["ffn2_w3"], h2, ffn2_norm, dh3, "ffn2_bwd_in",
        [scatter(p_f2w2), SiblingExchange([g_f2w1, g_f2w3])])
    p_f2w1 = reduce_in_chip("ffn2_w1", g_f2w1, r_f2w1)
    p_f2w3 = reduce_in_chip("ffn2_w3", g_f2w3, r_f2w3)
    (dpa, drb, g_wout), ((q_f2w1,),) = mix_out_bwd(dh2, wg["w_out"], pa, rb, "mix_out_bwd", [scatter(p_f2w1)])
    (dpool, d_pool_w, d_pool_scale), _ = pool_bwd(proj, dpa, pool_w[0], pool_scale, "pool_bwd")
    (dqkvg, d_ret_norm), ((q_f2w3,), (r_wout,)) = ret_bwd(
        proj, drb, o_pre, r_prev, ret_norm, tables, "ret_bwd", [scatter(p_f2w3), SiblingExchange([g_wout])])
    p_wout = reduce_in_chip("w_out", g_wout, r_wout)
    d = jnp.concatenate([dpool, dqkvg[0], dqkvg[1], dqkvg[2], dqkvg[3]], axis=1)
    (g_win,), ((q_wout,),) = mix_dwin(u, d, N_CHIPS, "mix_dwin", [scatter(p_wout)])
    (dh1, d_mix), ((r_win,),) = mix_in_bwd(d, wg["w_in"], h1, mix_norm, dh2, "mix_in_bwd", [SiblingExchange([g_win])])
    p_win = reduce_in_chip("w_in", g_win, r_win)
    (da1, db1), ((q_win,),) = ffn_bwd_act(dh1, wg["ffn1_w2"], a1, b1, "ffn1_bwd_act", [scatter(p_win)])
    (g_f1w1, g_f1w3), _ = ffn_dw13(n1, da1, db1, "ffn1_dw13")
    (g_f1w2,), ((r_f1w1, r_f1w3),) = ffn_dw2(s1, dh1, "ffn1_dw2", [SiblingExchange([g_f1w1, g_f1w3])])
    p_f1w1 = reduce_in_chip("ffn1_w1", g_f1w1, r_f1w1)
    p_f1w3 = reduce_in_chip("ffn1_w3", g_f1w3, r_f1w3)
    (dx, d_ffn1), ((q_f1w1, q_f1w3), (r_f1w2,)) = ffn_bwd_in(
        da1, db1, wg["ffn1_w1"], wg["ffn1_w3"], xs, ffn1_norm, dh1, "ffn1_bwd_in",
        [scatter(p_f1w1, p_f1w3), SiblingExchange([g_f1w2])])
    p_f1w2 = reduce_in_chip("ffn1_w2", g_f1w2, r_f1w2)

    (q_f1w2,), = update(["ffn2_w1", "ffn2_w3"], [q_f2w1, q_f2w3], "adamw_ffn2_w13", [scatter(p_f1w2)])
    update(["ffn2_w2"], [q_f2w2], "adamw_ffn2_w2")
    update(["w_in"], [q_win], "adamw_w_in")
    update(["w_out"], [q_wout], "adamw_w_out")
    update(["ffn1_w1", "ffn1_w3"], [q_f1w1, q_f1w3], "adamw_ffn1_w13")
    update(["ffn1_w2"], [q_f1w2], "adamw_ffn1_w2")

    small = {"ffn1_norm": d_ffn1, "mix_norm": d_mix, "pool_w": d_pool_w, "pool_scale": d_pool_scale,
             "ret_norm": d_ret_norm, "ffn2_norm": d_ffn2, "final_norm": d_final}
    packs = all_exchange_small(_pack(small), "gather_small")
    outs = adamw_small(packs, _pack(w), _pack(m), _pack(v), "adamw_small")
    for res, pack in zip((grad, delta, new_m, new_v), outs):
        res.update(_unpack(pack, w))

    return (loss, dx[None], *[grad[k] for k in WEIGHTS], *[delta[k] for k in WEIGHTS],
            *[new_m[k] for k in WEIGHTS], *[new_v[k] for k in WEIGHTS])
```

```python
import math

import jax
import jax.numpy as jnp
from jax import lax
from jax.experimental import pallas as pl
from jax.experimental.pallas import tpu as pltpu

F32 = jnp.float32
BF16 = jnp.bfloat16

EPS = 1e-6
N_CHIPS = 4
N_GROUPS = 4
HEAD_DIM = 128
RET_CHUNK = 128
ROPE_BASE = 10000.0
ADAM_LR, ADAM_B1, ADAM_B2, ADAM_EPS, ADAM_WD, ADAM_STEP = 0.001, 0.9, 0.999, 1e-08, 0.01, 10
VMEM_LIMIT_V7X = 56 * 1024 * 1024
ADAMW_VMEM_BUDGET = 32 * 1024 * 1024
MESH = pl.DeviceIdType.MESH
ANY = pl.BlockSpec(memory_space=pl.ANY)


def _dot(a, b):
    return jnp.dot(a, b, preferred_element_type=F32)


def _dot_nt(a, b):
    return lax.dot_general(a, b, (((1,), (1,)), ((), ())), preferred_element_type=F32)


def _dot_tn(a, b):
    return lax.dot_general(a, b, (((0,), (0,)), ((), ())), preferred_element_type=F32)


def _rstd(h):
    return lax.rsqrt(jnp.mean(h * h, axis=-1, keepdims=True) + EPS)


def _rmsnorm_bwd(dn, h, gain):
    r = _rstd(h)
    nh = h * r
    dnh = dn * gain
    dh = r * (dnh - nh * jnp.mean(dnh * nh, axis=-1, keepdims=True))
    return dh, dn * nh


def _silu_parts(a):
    sig = jax.nn.sigmoid(a)
    silu = a * sig
    return silu, sig + silu * (1.0 - sig)


def _mesh_pos():
    return lax.axis_index("x"), lax.axis_index("y"), lax.axis_index("c")


class ChipExchange:
    def __init__(self, srcs, scatter, placed=()):
        n = len(srcs)
        self.inputs, self.scatter, self.n = list(srcs) + list(placed), scatter, n
        self.aliases = {n + t: t for t in range(n)} if scatter else {}
        self.half_rows = [s.shape[1] if scatter else s.shape[0] // 2 for s in srcs]
        self.out_shape = [jax.ShapeDtypeStruct((N_CHIPS, 2 * rh, s.shape[-1]), s.dtype) for s, rh in zip(srcs, self.half_rows)]
        if scatter:
            self.out_shape += [jax.ShapeDtypeStruct((2, rh // 2, s.shape[-1]), s.dtype) for s, rh in zip(srcs, self.half_rows)]
        dma = pltpu.SemaphoreType.DMA
        self.sems = [dma((4 * n,)), dma((4 * n,)), dma((2 * n,)), dma((2 * n,)), dma((4 * n,)), dma((4 * n,))]

    def _copies(self, src, out, sems):
        hop1_send, hop1_recv, hop2_send, hop2_recv, d2d_send, d2d_recv = sems
        x, y, c = _mesh_pos()
        me, dg = 2 * x + y, 2 * (1 - x) + (1 - y)
        sibling = (x, y, 1 - c)
        n = self.n
        mine, theirs = c, 1 - c

        def nb(a):
            nx, ny = x ^ (1 - a), y ^ a
            return 2 * nx + ny, (nx, ny, c)

        def remote(s, d, send, recv, k, to):
            return pltpu.make_async_remote_copy(src_ref=s, dst_ref=d, send_sem=send.at[k], recv_sem=recv.at[k],
                                                device_id=to, device_id_type=MESH)

        class Copies:
            def slot(_, t, chip, half):
                rh = self.half_rows[t]
                return out[t].at[chip, pl.ds(half * rh, rh), :]

            def quarter(_, t, chip, q):
                qh = self.half_rows[t] // 2
                return out[t].at[chip, pl.ds(mine * 2 * qh + q * qh, qh), :]

            def own_shard(k, t):
                return remote(src[t], out[t].at[me], d2d_send, d2d_recv, 4 * t + 3, sibling)

            def hop1(k, t, a, transit=False):
                rh = self.half_rows[t]
                chip, to = nb(a)
                if transit:
                    piece = src[t].at[dg, pl.ds(a * (rh // 2), rh // 2), :]
                    return remote(piece, out[n + t].at[a], hop1_send, hop1_recv, 4 * t + 2 + a, to)
                piece = src[t].at[chip] if self.scatter else src[t].at[pl.ds(mine * rh, rh), :]
                return remote(piece, k.slot(t, me, mine), hop1_send, hop1_recv, 4 * t + a, to)

            def landed1(k, t, a, transit=False):
                here = out[n + t].at[a] if transit else k.slot(t, nb(a)[0], mine)
                return remote(here, here, hop1_send, hop1_recv, 4 * t + (2 if transit else 0) + a, sibling)

            def hop2(k, t, q):
                origin, to = nb(q)[0], nb(1 - q)[1]
                piece = out[n + t].at[q] if self.scatter else k.quarter(t, origin, q)
                return remote(piece, k.quarter(t, origin, q), hop2_send, hop2_recv, 2 * t + q, to)

            def landed2(k, t, q):
                here = k.quarter(t, dg, q)
                return remote(here, here, hop2_send, hop2_recv, 2 * t + q, sibling)

            def d2d(k, t, p, chip, own=False, arriving=False):
                if arriving:
                    there = k.slot(t, chip, theirs)
                    return remote(there, there, d2d_send, d2d_recv, 4 * t + p, sibling)
                piece = src[t].at[me] if own else k.slot(t, chip, mine)
                return remote(piece, k.slot(t, chip, mine), d2d_send, d2d_recv, 4 * t + p, sibling)

        return Copies(), nb, me, dg, c

    def start(self, src, out, sems):
        k, nb, me, dg, c = self._copies(src, out, sems)
        for t in range(self.n):
            for first in range(2):
                a = first ^ c
                k.hop1(t, a).start()
                if self.scatter:
                    k.hop1(t, a, transit=True).start()
            if self.scatter:
                k.d2d(t, 3, me, own=True).start()
            else:
                k.own_shard(t).start()

    def mid(self, src, out, sems):
        k, nb, me, dg, c = self._copies(src, out, sems)
        for t in range(self.n):
            for first in range(2):
                a = first ^ c
                if self.scatter:
                    k.landed1(t, a, transit=True).wait_recv()
                    k.hop2(t, a).start()
                k.landed1(t, a).wait_recv()
                if not self.scatter:
                    k.hop2(t, a).start()
                k.d2d(t, a, nb(a)[0]).start()

    def finish(self, src, out, sems):
        k, nb, me, dg, c = self._copies(src, out, sems)
        for t in range(self.n):
            for q in range(2):
                k.landed2(t, q).wait_recv()
            k.d2d(t, 2, dg).start()
        for t in range(self.n):
            for a in range(2):
                k.d2d(t, a, nb(a)[0], arriving=True).wait_recv()
            k.d2d(t, 2, dg, arriving=True).wait_recv()
            if self.scatter:
                k.d2d(t, 3, me, arriving=True).wait_recv()
        for t in range(self.n):
            for a in range(2):
                k.hop1(t, a).wait_send()
                if self.scatter:
                    k.hop1(t, a, transit=True).wait_send()
                k.hop2(t, a).wait_send()
                k.d2d(t, a, nb(a)[0]).wait_send()
            k.d2d(t, 2, dg).wait_send()
            if self.scatter:
                k.d2d(t, 3, me, own=True).wait_send()
            else:
                k.own_shard(t).wait()


class SiblingExchange:
    def __init__(self, grads):
        self.inputs, self.n, self.aliases = list(grads), len(grads), {}
        self.half_rows = [g.shape[1] // 2 for g in grads]
        self.out_shape = [jax.ShapeDtypeStruct((g.shape[0], rh, g.shape[2]), g.dtype) for g, rh in zip(grads, self.half_rows)]
        self.sems = [pltpu.SemaphoreType.DMA((self.n,)), pltpu.SemaphoreType.DMA((self.n,))]

    def _plan(self, src, out, sems):
        x, y, c = _mesh_pos()
        return [pltpu.make_async_remote_copy(
            src_ref=src[t].at[:, pl.ds((1 - c) * self.half_rows[t], self.half_rows[t]), :], dst_ref=out[t],
            send_sem=sems[0].at[t], recv_sem=sems[1].at[t], device_id=(x, y, 1 - c), device_id_type=MESH) for t in range(self.n)]

    def start(self, src, out, sems):
        for cp in self._plan(src, out, sems):
            cp.start()

    def mid(self, src, out, sems):
        pass

    def finish(self, src, out, sems):
        for cp in self._plan(src, out, sems):
            cp.wait()


def _call(body, hosted=(), *, name, in_specs, out_specs, out_shape, args, grid=(), scratch_shapes=()):
    n_in, n_out, n_scr = len(in_specs), len(out_specs), len(scratch_shapes)
    total = math.prod(grid)
    mid_step = max(0, (5 * total) // 8 - 1)

    def full(*refs):
        pos = [0]

        def take(k):
            pos[0] += k
            return refs[pos[0] - k:pos[0]]

        ins, h_in = take(n_in), [take(len(h.inputs)) for h in hosted]
        outs, h_out = take(n_out), [take(len(h.out_shape)) for h in hosted]
        scr, h_sem = take(n_scr), [take(len(h.sems)) for h in hosted]
        step = 0
        for axis, size in enumerate(grid):
            step = step * size + pl.program_id(axis)

        def phase(at, method):
            if not hosted:
                return
            if total == 1:
                for h, s, o, m in zip(hosted, h_in, h_out, h_sem):
                    getattr(h, method)(s, o, m)
                return

            @pl.when(step == at)
            def _():
                for h, s, o, m in zip(hosted, h_in, h_out, h_sem):
                    getattr(h, method)(s, o, m)

        phase(0, "start")
        body(*ins, *outs, *scr)
        phase(mid_step, "mid")
        phase(total - 1, "finish")

    aliases, i0, o0 = {}, n_in, n_out
    for h in hosted:
        aliases.update({i0 + i: o0 + o for i, o in h.aliases.items()})
        i0, o0 = i0 + len(h.inputs), o0 + len(h.out_shape)
    results = pl.pallas_call(
        full, name=name, grid=grid,
        in_specs=list(in_specs) + [ANY] * (i0 - n_in),
        out_specs=list(out_specs) + [ANY] * (o0 - n_out),
        out_shape=list(out_shape) + [s for h in hosted for s in h.out_shape],
        scratch_shapes=list(scratch_shapes) + [s for h in hosted for s in h.sems],
        input_output_aliases=aliases,
        compiler_params=pltpu.CompilerParams(vmem_limit_bytes=VMEM_LIMIT_V7X),
    )(*args, *[s for h in hosted for s in h.inputs])
    outs, extras, pos = list(results[:n_out]), [], n_out
    for h in hosted:
        extras.append(list(results[pos:pos + h.n]))
        pos += len(h.out_shape)
    return outs, extras


def exchange(hosted, name):
    return _call(lambda: None, hosted, name=name, in_specs=[], out_specs=[], out_shape=[], args=[])[1]


class AllExchange:
    def __init__(self, pack):
        self.inputs, self.n, self.aliases = [pack], 1, {}
        self.out_shape = [jax.ShapeDtypeStruct((2 * N_CHIPS,) + pack.shape, pack.dtype)]
        self.sems = [pltpu.SemaphoreType.DMA, pltpu.SemaphoreType.DMA((7,)), pltpu.SemaphoreType.DMA((7,))]

    def _copies(self, src, out, sems):
        local_sem, send_sem, recv_sem = sems
        x, y, c = _mesh_pos()
        flips = [(dx, dy, dc) for dx in (0, 1) for dy in (0, 1) for dc in (0, 1)][1:]
        peers = [(x ^ dx, y ^ dy, c ^ dc) for dx, dy, dc in flips]
        remote = lambda s, d, k: pltpu.make_async_remote_copy(
            src_ref=s, dst_ref=d, send_sem=send_sem.at[k], recv_sem=recv_sem.at[k], device_id=peers[k], device_id_type=MESH)
        sends = [remote(src[0], out[0].at[4 * x + 2 * y + c], k) for k in range(7)]
        landed = [remote(out[0].at[4 * px + 2 * py + pc], out[0].at[4 * px + 2 * py + pc], k) for k, (px, py, pc) in enumerate(peers)]
        return sends, landed, pltpu.make_async_copy(src[0], out[0].at[4 * x + 2 * y + c], local_sem)

    def start(self, src, out, sems):
        sends, _, local = self._copies(src, out, sems)
        for cp in sends:
            cp.start()
        local.start()

    def mid(self, src, out, sems):
        pass

    def finish(self, src, out, sems):
        sends, landed, local = self._copies(src, out, sems)
        for cp in landed:
            cp.wait_recv()
        for cp in sends:
            cp.wait_send()
        local.wait()


def ffn_up(h, gain, w1g, w3g, name, hosted=()):
    T, D = h.shape
    nsh, _, Fs = w1g.shape
    tm = min(T, 1024)

    def body(h_ref, g_ref, w1_ref, w3_ref, n_ref, a_ref, b_ref, s_ref):
        @pl.when(pl.program_id(1) == 0)
        def _():
            hh = h_ref[...]
            n_ref[...] = (hh * _rstd(hh) * g_ref[...]).astype(BF16)

        n = n_ref[...]
        a = _dot(n, w1_ref[0])
        b = _dot(n, w3_ref[0])
        a_ref[0] = a.astype(BF16)
        b_ref[0] = b.astype(BF16)
        s_ref[0] = (a * jax.nn.sigmoid(a) * b).astype(BF16)

    act = jax.ShapeDtypeStruct((nsh, T, Fs), BF16)
    act_spec = pl.BlockSpec((1, tm, Fs), lambda i, j: (j, i, 0))
    w_spec = pl.BlockSpec((1, D, Fs), lambda i, j: (j, 0, 0))
    return _call(
        body, hosted, name=name, grid=(T // tm, nsh),
        in_specs=[pl.BlockSpec((tm, D), lambda i, j: (i, 0)), pl.BlockSpec((1, D), lambda i, j: (0, 0)), w_spec, w_spec],
        out_specs=[pl.BlockSpec((tm, D), lambda i, j: (i, 0)), act_spec, act_spec, act_spec],
        out_shape=[jax.ShapeDtypeStruct((T, D), BF16), act, act, act],
        args=[h, gain, w1g, w3g])


def ffn_down(s, w2g, h, name, hosted=()):
    nsh, T, Fs = s.shape
    D = h.shape[1]
    tm = min(T, 512)

    def body(s_ref, w2_ref, h_ref, o_ref):
        f = _dot(s_ref[0], w2_ref[0])
        for j in range(1, nsh):
            f += _dot(s_ref[j], w2_ref[j])
        o_ref[...] = h_ref[...] + 0.5 * f

    return _call(
        body, hosted, name=name, grid=(T // tm,),
        in_specs=[pl.BlockSpec((nsh, tm, Fs), lambda i: (0, i, 0)), pl.BlockSpec((nsh, Fs, D), lambda i: (0, 0, 0)),
                  pl.BlockSpec((tm, D), lambda i: (i, 0))],
        out_specs=[pl.BlockSpec((tm, D), lambda i: (i, 0))],
        out_shape=[jax.ShapeDtypeStruct((T, D), F32)],
        args=[s, w2g, h])


def ffn_bwd_act(dh, w2g, a, b, name, hosted=()):
    T, D = dh.shape
    nsh, Fs, _ = w2g.shape
    tm = min(T, 1024)

    def body(dh_ref, w2_ref, a_ref, b_ref, da_ref, db_ref):
        df = (0.5 * dh_ref[...]).astype(BF16)
        ds = _dot_nt(df, w2_ref[0])
        silu, dsilu = _silu_parts(a_ref[0].astype(F32))
        da_ref[0] = (ds * b_ref[0].astype(F32) * dsilu).astype(BF16)
        db_ref[0] = (ds * silu).astype(BF16)

    act = jax.ShapeDtypeStruct((nsh, T, Fs), BF16)
    act_spec = pl.BlockSpec((1, tm, Fs), lambda j, i: (j, i, 0))
    return _call(
        body, hosted, name=name, grid=(nsh, T // tm),
        in_specs=[pl.BlockSpec((tm, D), lambda j, i: (i, 0)), pl.BlockSpec((1, Fs, D), lambda j, i: (j, 0, 0)), act_spec, act_spec],
        out_specs=[act_spec, act_spec],
        out_shape=[act, act],
        args=[dh, w2g, a, b])


def ffn_dw2(s, dh, name, hosted=()):
    nsh, T, Fs = s.shape
    D = dh.shape[1]
    tk = min(T, 512)
    nk = T // tk

    def body(s_ref, dh_ref, o_ref, acc):
        k = pl.program_id(1)

        @pl.when(k == 0)
        def _():
            acc[...] = jnp.zeros_like(acc)

        acc[...] += _dot_tn(s_ref[0], (0.5 * dh_ref[...]).astype(BF16))

        @pl.when(k == nk - 1)
        def _():
            o_ref[0] = acc[...].astype(BF16)

    return _call(
        body, hosted, name=name, grid=(nsh, nk),
        in_specs=[pl.BlockSpec((1, tk, Fs), lambda j, k: (j, k, 0)), pl.BlockSpec((tk, D), lambda j, k: (k, 0))],
        out_specs=[pl.BlockSpec((1, Fs, D), lambda j, k: (j, 0, 0))],
        out_shape=[jax.ShapeDtypeStruct((nsh, Fs, D), BF16)],
        scratch_shapes=[pltpu.VMEM((Fs, D), F32)],
        args=[s, dh])


def ffn_dw13(n, da, db, name, hosted=()):
    T, D = n.shape
    nsh, _, Fs = da.shape
    tk = min(T, 512)
    nk = T // tk

    def body(n_ref, da_ref, db_ref, o1_ref, o3_ref, acc1, acc3):
        k = pl.program_id(1)

        @pl.when(k == 0)
        def _():
            acc1[...] = jnp.zeros_like(acc1)
            acc3[...] = jnp.zeros_like(acc3)

        nn = n_ref[...]
        acc1[...] += _dot_tn(nn, da_ref[0])
        acc3[...] += _dot_tn(nn, db_ref[0])

        @pl.when(k == nk - 1)
        def _():
            o1_ref[0] = acc1[...].astype(BF16)
            o3_ref[0] = acc3[...].astype(BF16)

    act_spec = pl.BlockSpec((1, tk, Fs), lambda j, k: (j, k, 0))
    out = jax.ShapeDtypeStruct((nsh, D, Fs), BF16)
    out_spec = pl.BlockSpec((1, D, Fs), lambda j, k: (j, 0, 0))
    return _call(
        body, hosted, name=name, grid=(nsh, nk),
        in_specs=[pl.BlockSpec((tk, D), lambda j, k: (k, 0)), act_spec, act_spec],
        out_specs=[out_spec, out_spec],
        out_shape=[out, out],
        scratch_shapes=[pltpu.VMEM((D, Fs), F32), pltpu.VMEM((D, Fs), F32)],
        args=[n, da, db])


def ffn_bwd_in(da, db, w1g, w3g, h, gain, dh, name, hosted=()):
    nsh, T, Fs = da.shape
    D = h.shape[1]
    tm = min(T, 256)

    def body(da_ref, db_ref, w1_ref, w3_ref, h_ref, g_ref, dh_ref, o_ref, dg_ref):
        dn = _dot_nt(da_ref[0], w1_ref[0]) + _dot_nt(db_ref[0], w3_ref[0])
        for j in range(1, nsh):
            dn += _dot_nt(da_ref[j], w1_ref[j]) + _dot_nt(db_ref[j], w3_ref[j])
        dhn, dg = _rmsnorm_bwd(dn, h_ref[...], g_ref[...])
        o_ref[...] = dh_ref[...] + dhn

        @pl.when(pl.program_id(0) == 0)
        def _():
            dg_ref[...] = jnp.zeros_like(dg_ref)

        dg_ref[...] += jnp.sum(dg, axis=0, keepdims=True)

    act_spec = pl.BlockSpec((nsh, tm, Fs), lambda i: (0, i, 0))
    w_spec = pl.BlockSpec((nsh, D, Fs), lambda i: (0, 0, 0))
    row_spec = pl.BlockSpec((tm, D), lambda i: (i, 0))
    vec_spec = pl.BlockSpec((1, D), lambda i: (0, 0))
    return _call(
        body, hosted, name=name, grid=(T // tm,),
        in_specs=[act_spec, act_spec, w_spec, w_spec, row_spec, vec_spec, row_spec],
        out_specs=[row_spec, vec_spec],
        out_shape=[jax.ShapeDtypeStruct((T, D), F32), jax.ShapeDtypeStruct((1, D), F32)],
        args=[da, db, w1g, w3g, h, gain, dh])


def mix_in(h, gain, wing, name, hosted=()):
    T, D = h.shape
    nsh, _, Cs = wing.shape
    tm = min(T, 512)

    def body(h_ref, g_ref, w_ref, u_ref, p_ref):
        hh = h_ref[...]
        u = (hh * _rstd(hh) * g_ref[...]).astype(BF16)
        u_ref[...] = u
        for j in range(nsh):
            p_ref[:, j * Cs:(j + 1) * Cs] = _dot(u, w_ref[j])

    return _call(
        body, hosted, name=name, grid=(T // tm,),
        in_specs=[pl.BlockSpec((tm, D), lambda i: (i, 0)), pl.BlockSpec((1, D), lambda i: (0, 0)),
                  pl.BlockSpec((nsh, D, Cs), lambda i: (0, 0, 0))],
        out_specs=[pl.BlockSpec((tm, D), lambda i: (i, 0)), pl.BlockSpec((tm, nsh * Cs), lambda i: (i, 0))],
        out_shape=[jax.ShapeDtypeStruct((T, D), BF16), jax.ShapeDtypeStruct((T, nsh * Cs), F32)],
        args=[h, gain, wing])


def mix_out(a, b, woutg, h, name, hosted=()):
    T, W = a.shape
    D = h.shape[1]
    wout = woutg.reshape(2, W, D)
    tm = min(T, 512)

    def body(a_ref, b_ref, w_ref, h_ref, o_ref):
        o_ref[...] = h_ref[...] + _dot(a_ref[...], w_ref[0]) + _dot(b_ref[...], w_ref[1])

    return _call(
        body, hosted, name=name, grid=(T // tm,),
        in_specs=[pl.BlockSpec((tm, W), lambda i: (i, 0)), pl.BlockSpec((tm, W), lambda i: (i, 0)),
                  pl.BlockSpec((2, W, D), lambda i: (0, 0, 0)), pl.BlockSpec((tm, D), lambda i: (i, 0))],
        out_specs=[pl.BlockSpec((tm, D), lambda i: (i, 0))],
        out_shape=[jax.ShapeDtypeStruct((T, D), F32)],
        args=[a, b, wout, h])


def mix_out_bwd(dh, woutg, a, b, name, hosted=()):
    T, D = dh.shape
    W = a.shape[1]
    nsh, Rs, _ = woutg.shape
    wout = woutg.reshape(2, W, D)
    tk = min(T, 512)
    nk = T // tk

    def body(dh_ref, w_ref, a_ref, b_ref, da_ref, db_ref, dw_ref, acc):
        k = pl.program_id(0)

        @pl.when(k == 0)
        def _():
            acc[...] = jnp.zeros_like(acc)

        dhb = dh_ref[...].astype(BF16)
        da_ref[...] = _dot_nt(dhb, w_ref[0])
        db_ref[...] = _dot_nt(dhb, w_ref[1])
        acc[0:W, :] += _dot_tn(a_ref[...], dhb)
        acc[W:2 * W, :] += _dot_tn(b_ref[...], dhb)

        @pl.when(k == nk - 1)
        def _():
            for j in range(nsh):
                dw_ref[j] = acc[j * Rs:(j + 1) * Rs, :].astype(BF16)

    return _call(
        body, hosted, name=name, grid=(nk,),
        in_specs=[pl.BlockSpec((tk, D), lambda k: (k, 0)), pl.BlockSpec((2, W, D), lambda k: (0, 0, 0)),
                  pl.BlockSpec((tk, W), lambda k: (k, 0)), pl.BlockSpec((tk, W), lambda k: (k, 0))],
        out_specs=[pl.BlockSpec((tk, W), lambda k: (k, 0)), pl.BlockSpec((tk, W), lambda k: (k, 0)),
                   pl.BlockSpec((nsh, Rs, D), lambda k: (0, 0, 0))],
        out_shape=[jax.ShapeDtypeStruct((T, W), F32), jax.ShapeDtypeStruct((T, W), F32),
                   jax.ShapeDtypeStruct((nsh, Rs, D), BF16)],
        scratch_shapes=[pltpu.VMEM((2 * W, D), F32)],
        args=[dh, wout, a, b])


def mix_dwin(u, d, nsh, name, hosted=()):
    T, D = u.shape
    Cs = d.shape[1] // nsh
    tk = min(T, 512)
    nk = T // tk

    def body(u_ref, d_ref, o_ref, acc):
        k = pl.program_id(1)

        @pl.when(k == 0)
        def _():
            acc[...] = jnp.zeros_like(acc)

        acc[...] += _dot_tn(u_ref[...], d_ref[...])

        @pl.when(k == nk - 1)
        def _():
            o_ref[0] = acc[...].astype(BF16)

    return _call(
        body, hosted, name=name, grid=(nsh, nk),
        in_specs=[pl.BlockSpec((tk, D), lambda j, k: (k, 0)), pl.BlockSpec((tk, Cs), lambda j, k: (k, j))],
        out_specs=[pl.BlockSpec((1, D, Cs), lambda j, k: (j, 0, 0))],
        out_shape=[jax.ShapeDtypeStruct((nsh, D, Cs), BF16)],
        scratch_shapes=[pltpu.VMEM((D, Cs), F32)],
        args=[u, d])


def mix_in_bwd(d, wing, h, gain, dh, name, hosted=()):
    T, D = h.shape
    nsh, _, Cs = wing.shape
    tm = min(T, 512)

    def body(d_ref, w_ref, h_ref, g_ref, dh_ref, o_ref, dg_ref):
        du = _dot_nt(d_ref[:, 0:Cs], w_ref[0])
        for j in range(1, nsh):
            du += _dot_nt(d_ref[:, j * Cs:(j + 1) * Cs], w_ref[j])
        dhn, dg = _rmsnorm_bwd(du, h_ref[...], g_ref[...])
        o_ref[...] = dh_ref[...] + dhn

        @pl.when(pl.program_id(0) == 0)
        def _():
            dg_ref[...] = jnp.zeros_like(dg_ref)

        dg_ref[...] += jnp.sum(dg, axis=0, keepdims=True)

    row_spec = pl.BlockSpec((tm, D), lambda i: (i, 0))
    vec_spec = pl.BlockSpec((1, D), lambda i: (0, 0))
    return _call(
        body, hosted, name=name, grid=(T // tm,),
        in_specs=[pl.BlockSpec((tm, nsh * Cs), lambda i: (i, 0)), pl.BlockSpec((nsh, D, Cs), lambda i: (0, 0, 0)),
                  row_spec, vec_spec, row_spec],
        out_specs=[row_spec, vec_spec],
        out_shape=[jax.ShapeDtypeStruct((T, D), F32), jax.ShapeDtypeStruct((1, D), F32)],
        args=[d, wing, h, gain, dh])


def _pool_window(x, group, T, trailing):
    rows = lax.broadcasted_iota(jnp.int32, x.shape, 0)

    def shifted(z, k):
        if trailing:
            return jnp.where(rows >= k, pltpu.roll(z, k, 0), 0.0)
        return jnp.where(rows < T - k, pltpu.roll(z, T - k, 0), 0.0)

    s2 = x + shifted(x, 1)
    s4 = s2 + shifted(s2, 2)
    s8 = s4 + shifted(s4, 4)
    s16 = s8 + shifted(s8, 8)
    return jnp.where(group == 0, s2, jnp.where(group == 1, s4, jnp.where(group == 2, s8, s16)))


def _pool_count(group, shape):
    rows = lax.broadcasted_iota(jnp.int32, shape, 0)
    w = jnp.where(group == 0, 2, jnp.where(group == 1, 4, jnp.where(group == 2, 8, 16)))
    return jnp.minimum(rows + 1, w).astype(F32)


def pool_fwd(proj, pool_w, pool_scale, name, hosted=()):
    T = proj.shape[0]
    Hd = HEAD_DIM

    def body(x_ref, w_ref, sc_ref, a_ref):
        g = pl.program_id(0)
        x = x_ref[...]
        pooled = _pool_window(x, g, T, True) / _pool_count(g, x.shape) - x
        a_ref[...] = (_dot(pooled.astype(BF16), w_ref[0].astype(BF16)) * sc_ref[...]).astype(BF16)

    return _call(
        body, hosted, name=name, grid=(N_GROUPS,),
        in_specs=[pl.BlockSpec((T, Hd), lambda g: (0, g)), pl.BlockSpec((1, Hd, Hd), lambda g: (g, 0, 0)),
                  pl.BlockSpec((1, Hd), lambda g: (0, g))],
        out_specs=[pl.BlockSpec((T, Hd), lambda g: (0, g))],
        out_shape=[jax.ShapeDtypeStruct((T, N_GROUPS * Hd), BF16)],
        args=[proj, pool_w, pool_scale])


def pool_bwd(proj, da, pool_w, pool_scale, name, hosted=()):
    T = proj.shape[0]
    Hd = HEAD_DIM

    def body(x_ref, da_ref, w_ref, sc_ref, dx_ref, dw_ref, dsc_ref):
        g = pl.program_id(0)
        x = x_ref[...]
        cnt = _pool_count(g, x.shape)
        pooled = (_pool_window(x, g, T, True) / cnt - x).astype(BF16)
        wb = w_ref[0].astype(BF16)
        dav = da_ref[...]
        dsc_ref[...] = jnp.sum(dav * _dot(pooled, wb), axis=0, keepdims=True)
        dout = (dav * sc_ref[...]).astype(BF16)
        dw_ref[0] = _dot_tn(pooled, dout)
        dpooled = _dot_nt(dout, wb)
        dx_ref[...] = (_pool_window(dpooled / cnt, g, T, False) - dpooled).astype(BF16)

    col_spec = pl.BlockSpec((T, Hd), lambda g: (0, g))
    return _call(
        body, hosted, name=name, grid=(N_GROUPS,),
        in_specs=[col_spec, col_spec, pl.BlockSpec((1, Hd, Hd), lambda g: (g, 0, 0)), pl.BlockSpec((1, Hd), lambda g: (0, g))],
        out_specs=[col_spec, pl.BlockSpec((1, Hd, Hd), lambda g: (g, 0, 0)), pl.BlockSpec((1, Hd), lambda g: (0, g))],
        out_shape=[jax.ShapeDtypeStruct((T, N_GROUPS * Hd), BF16), jax.ShapeDtypeStruct((N_GROUPS, Hd, Hd), F32),
                   jax.ShapeDtypeStruct((1, N_GROUPS * Hd), F32)],
        args=[proj, da, pool_w, pool_scale])


def _ret_tables(T):
    Hd, C = HEAD_DIM, RET_CHUNK
    inv_freq = 1.0 / (ROPE_BASE ** (jnp.arange(0, Hd, 2, dtype=F32) / Hd))
    ang = jnp.arange(T, dtype=F32)[:, None] * inv_freq[None, :]
    cos, sin = jnp.cos(ang), jnp.sin(ang)
    cos2 = jnp.concatenate([cos, cos], axis=-1)
    sin2 = jnp.concatenate([-sin, sin], axis=-1)
    log_gamma = jnp.log1p(-jnp.exp2(-5.0 - jnp.arange(N_GROUPS, dtype=F32)))
    pos = jnp.arange(C, dtype=F32)
    rel = pos[:, None] - pos[None, :]
    intra = jnp.where(rel[None] >= 0, jnp.exp(log_gamma[:, None, None] * jnp.maximum(rel, 0.0)[None]), 0.0)
    k_tail = jnp.exp(log_gamma[:, None] * (C - 1 - pos)[None, :])
    q_head = jnp.exp(log_gamma[:, None] * (pos + 1.0)[None, :])
    chunk_decay = jnp.exp(log_gamma * C)
    wide = lambda t: jnp.broadcast_to(t[:, :, None], (N_GROUPS, C, Hd))
    return cos2, sin2, intra, wide(k_tail), wide(q_head), jnp.broadcast_to(chunk_decay[:, None, None], (N_GROUPS, 1, Hd))


def _rope(x, cos2, sin2):
    return x * cos2 + pltpu.roll(x, HEAD_DIM // 2, 1) * sin2


def _rope_t(d, cos2, sin2):
    return d * cos2 + pltpu.roll(d * sin2, HEAD_DIM // 2, 1)


def _ret_specs(tseg, seg_of):
    Hd, G = HEAD_DIM, N_GROUPS
    col = lambda kind: pl.BlockSpec((tseg, Hd), lambda h, s: (seg_of(s), G * kind + h))
    tab = pl.BlockSpec((tseg, Hd), lambda h, s: (seg_of(s), 0))
    head = pl.BlockSpec((1, RET_CHUNK, Hd), lambda h, s: (h, 0, 0))
    cd = pl.BlockSpec((1, 1, Hd), lambda h, s: (h, 0, 0))
    gain = pl.BlockSpec((1, Hd), lambda h, s: (0, h))
    return col, tab, head, cd, gain


def ret_fwd(proj, ret_norm, tables, name, hosted=()):
    T = proj.shape[0]
    Hd, C, G = HEAD_DIM, RET_CHUNK, N_GROUPS
    tseg = min(T, 1024)
    nseg, nck = T // tseg, tseg // C
    scale = Hd ** -0.5
    cos2, sin2, intra, k_tail, q_head, chunk_decay = tables

    def body(q_ref, k_ref, v_ref, g_ref, gain_ref, cos_ref, sin_ref, m_ref, kt_ref, qh_ref, cd_ref,
             b_ref, o_ref, rp_ref, state):
        @pl.when(pl.program_id(1) == 0)
        def _():
            state[...] = jnp.zeros_like(state)

        def chunk(ci, carry):
            rows = pl.ds(pl.multiple_of(ci * C, C), C)
            cos, sin = cos_ref[rows, :], sin_ref[rows, :]
            qr = _rope(q_ref[rows, :], cos, sin)
            kr = _rope(k_ref[rows, :], cos, sin) * scale
            qb, kb, vb = qr.astype(BF16), kr.astype(BF16), v_ref[rows, :].astype(BF16)
            r = state[...]
            rp_ref[0, ci] = r.astype(BF16)
            sc = _dot_nt(qb, kb) * m_ref[0]
            o = _dot(sc.astype(BF16), vb) + _dot((qr * qh_ref[0]).astype(BF16), r.astype(BF16))
            state[...] = cd_ref[0] * r + _dot_tn((kr * kt_ref[0]).astype(BF16), vb)
            o_ref[rows, :] = o
            on = o * _rstd(o)
            b_ref[rows, :] = (jax.nn.silu(g_ref[rows, :]) * (on * gain_ref[...])).astype(BF16)
            return carry

        lax.fori_loop(0, nck, chunk, 0)

    col, tab, head, cd, gain = _ret_specs(tseg, lambda s: s)
    out_col = pl.BlockSpec((tseg, Hd), lambda h, s: (s, h))
    return _call(
        body, hosted, name=name, grid=(G, nseg),
        in_specs=[col(1), col(2), col(3), col(4), gain, tab, tab, head, head, head, cd],
        out_specs=[out_col, out_col, pl.BlockSpec((1, nck, Hd, Hd), lambda h, s: (h, s, 0, 0))],
        out_shape=[jax.ShapeDtypeStruct((T, G * Hd), BF16), jax.ShapeDtypeStruct((T, G * Hd), F32),
                   jax.ShapeDtypeStruct((G, T // C, Hd, Hd), BF16)],
        scratch_shapes=[pltpu.VMEM((Hd, Hd), F32)],
        args=[proj, proj, proj, proj, ret_norm, cos2, sin2, intra, k_tail, q_head, chunk_decay])


def ret_bwd(proj, db, o_pre, r_prev, ret_norm, tables, name, hosted=()):
    T = proj.shape[0]
    Hd, C, G = HEAD_DIM, RET_CHUNK, N_GROUPS
    tseg = min(T, 1024)
    nseg, nck = T // tseg, tseg // C
    scale = Hd ** -0.5
    cos2, sin2, intra, k_tail, q_head, chunk_decay = tables

    def body(q_ref, k_ref, v_ref, g_ref, db_ref, o_ref, rp_ref, gain_ref, cos_ref, sin_ref, m_ref, kt_ref, qh_ref, cd_ref,
             d_ref, dgain_ref, gstate):
        @pl.when(pl.program_id(1) == 0)
        def _():
            gstate[...] = jnp.zeros_like(gstate)
            dgain_ref[...] = jnp.zeros_like(dgain_ref)

        def chunk(t, carry):
            ci = nck - 1 - t
            rows = pl.ds(pl.multiple_of(ci * C, C), C)
            cos, sin = cos_ref[rows, :], sin_ref[rows, :]
            qr = _rope(q_ref[rows, :], cos, sin)
            kr = _rope(k_ref[rows, :], cos, sin) * scale
            qb, kb, vb = qr.astype(BF16), kr.astype(BF16), v_ref[rows, :].astype(BF16)
            qhb, ktb = (qr * qh_ref[0]).astype(BF16), (kr * kt_ref[0]).astype(BF16)
            sc = (_dot_nt(qb, kb) * m_ref[0]).astype(BF16)
            o = o_ref[rows, :]
            rstd = _rstd(o)
            on = o * rstd
            gain = gain_ref[...]
            silu, dsilu = _silu_parts(g_ref[rows, :])
            dy = db_ref[rows, :]
            dgain_ref[...] += jnp.sum(dy * silu * on, axis=0, keepdims=True)
            dg = dy * on * gain * dsilu
            don = dy * silu * gain
            dob = (rstd * (don - on * jnp.mean(don * on, axis=-1, keepdims=True))).astype(BF16)
            gn = gstate[...]
            gb = gn.astype(BF16)
            da = (_dot_nt(dob, vb) * m_ref[0]).astype(BF16)
            dq = _dot(da, kb) + _dot_nt(dob, rp_ref[0, ci]) * qh_ref[0]
            dk = _dot_tn(da, qb) + _dot_nt(vb, gb) * kt_ref[0]
            dv = _dot_tn(sc, dob) + _dot(ktb, gb)
            gstate[...] = cd_ref[0] * gn + _dot_tn(qhb, dob)
            d_ref[0, rows, :] = _rope_t(dq, cos, sin).astype(BF16)
            d_ref[1, rows, :] = _rope_t(dk * scale, cos, sin).astype(BF16)
            d_ref[2, rows, :] = dv.astype(BF16)
            d_ref[3, rows, :] = dg.astype(BF16)
            return carry

        lax.fori_loop(0, nck, chunk, 0)

    rev = lambda s: nseg - 1 - s
    col, tab, head, cd, gain = _ret_specs(tseg, rev)
    act = pl.BlockSpec((tseg, Hd), lambda h, s: (rev(s), h))
    return _call(
        body, hosted, name=name, grid=(G, nseg),
        in_specs=[col(1), col(2), col(3), col(4), act, act, pl.BlockSpec((1, nck, Hd, Hd), lambda h, s: (h, rev(s), 0, 0)),
                  gain, tab, tab, head, head, head, cd],
        out_specs=[pl.BlockSpec((4, tseg, Hd), lambda h, s: (0, rev(s), h)), gain],
        out_shape=[jax.ShapeDtypeStruct((4, T, G * Hd), BF16), jax.ShapeDtypeStruct((1, G * Hd), F32)],
        scratch_shapes=[pltpu.VMEM((Hd, Hd), F32)],
        args=[proj, proj, proj, proj, db, o_pre, r_prev, ret_norm, cos2, sin2, intra, k_tail, q_head, chunk_decay])


def final_loss(h, gain, target, name, hosted=()):
    T, D = h.shape
    tm = min(T, 512)

    def body(h_ref, g_ref, t_ref, dh_ref, loss_ref, dg_ref):
        @pl.when(pl.program_id(0) == 0)
        def _():
            loss_ref[...] = jnp.zeros_like(loss_ref)
            dg_ref[...] = jnp.zeros_like(dg_ref)

        hh = h_ref[...]
        gain_v = g_ref[...]
        err = hh * _rstd(hh) * gain_v - t_ref[...]
        loss_ref[...] += 0.5 * jnp.sum(jnp.mean(err * err, axis=-1, keepdims=True), axis=0, keepdims=True)
        dhn, dg = _rmsnorm_bwd(err * (1.0 / D), hh, gain_v)
        dh_ref[...] = dhn
        dg_ref[...] += jnp.sum(dg, axis=0, keepdims=True)

    row_spec = pl.BlockSpec((tm, D), lambda i: (i, 0))
    vec_spec = pl.BlockSpec((1, D), lambda i: (0, 0))
    return _call(
        body, hosted, name=name, grid=(T // tm,),
        in_specs=[row_spec, vec_spec, row_spec],
        out_specs=[row_spec, pl.BlockSpec((1, 128), lambda i: (0, 0)), vec_spec],
        out_shape=[jax.ShapeDtypeStruct((T, D), F32), jax.ShapeDtypeStruct((1, 128), F32), jax.ShapeDtypeStruct((1, D), F32)],
        args=[h, gain, target])


def prereduce(grad, recv, place, name):
    nsh, R, C = grad.shape
    rh = R // 2

    def body(place_ref, g_ref, r_ref, o_ref, own_ref):
        piece = (g_ref[...].astype(F32) + r_ref[...].astype(F32)).astype(BF16)
        o_ref[...] = piece

        @pl.when(pl.program_id(0) == place_ref[1])
        def _():
            own_ref[...] = piece

    return pl.pallas_call(
        body, name=name,
        grid_spec=pltpu.PrefetchScalarGridSpec(
            num_scalar_prefetch=1, grid=(nsh,),
            in_specs=[pl.BlockSpec((1, rh, C), lambda j, p: (j, p[0], 0)), pl.BlockSpec((1, rh, C), lambda j, p: (j, 0, 0))],
            out_specs=[pl.BlockSpec((1, rh, C), lambda j, p: (j, 0, 0)), pl.BlockSpec((1, rh, C), lambda j, p: (p[1], p[0], 0))]),
        out_shape=[jax.ShapeDtypeStruct((nsh, rh, C), BF16), jax.ShapeDtypeStruct((nsh, R, C), BF16)],
        compiler_params=pltpu.CompilerParams(vmem_limit_bytes=VMEM_LIMIT_V7X),
    )(place, grad, recv)


def _adamw(w, g, m, v):
    m = ADAM_B1 * m + (1.0 - ADAM_B1) * g
    v = ADAM_B2 * v + (1.0 - ADAM_B2) * (g * g)
    m_hat = m / (1.0 - ADAM_B1 ** ADAM_STEP)
    v_hat = v / (1.0 - ADAM_B2 ** ADAM_STEP)
    return -ADAM_LR * (m_hat / (jnp.sqrt(v_hat) + ADAM_EPS) + ADAM_WD * w), m, v


def adamw_sharded(tensors, name, hosted=()):
    nt = len(tensors)
    nsh, R, C = tensors[0][0].shape
    lanes = -(-C // 128) * 128
    per_row = 2 * nt * lanes * (nsh * 2 + 7 * 4)
    tr = max(r for r in range(16, R + 1, 16) if R % r == 0 and r * per_row <= ADAMW_VMEM_BUDGET)

    def body(*refs):
        ins, outs = refs[:4 * nt], refs[4 * nt:]
        for t in range(nt):
            p_ref, w_ref, m_ref, v_ref = ins[4 * t:4 * t + 4]
            g_ref, d_ref, nm_ref, nv_ref = outs[4 * t:4 * t + 4]
            g = p_ref[0].astype(F32)
            for i in range(1, nsh):
                g += p_ref[i].astype(F32)
            g_ref[...] = g
            d_ref[...], nm_ref[...], nv_ref[...] = _adamw(w_ref[...], g, m_ref[...], v_ref[...])

    spec = pl.BlockSpec((tr, C), lambda i: (i, 0))
    out = jax.ShapeDtypeStruct((R, C), F32)
    return _call(
        body, hosted, name=name, grid=(R // tr,),
        in_specs=[pl.BlockSpec((nsh, tr, C), lambda i: (0, i, 0)), spec, spec, spec] * nt,
        out_specs=[spec] * (4 * nt), out_shape=[out] * (4 * nt),
        args=[a for tensor in tensors for a in tensor])


def adamw_small(packs, w, m, v, name):
    ndev, R, L = packs.shape

    def body(p_ref, w_ref, m_ref, v_ref, g_ref, d_ref, nm_ref, nv_ref):
        g = p_ref[0]
        for i in range(1, ndev):
            g += p_ref[i]
        g_ref[...] = g
        d_ref[...], nm_ref[...], nv_ref[...] = _adamw(w_ref[...], g, m_ref[...], v_ref[...])

    out = jax.ShapeDtypeStruct((R, L), F32)
    return pl.pallas_call(body, name=name, out_shape=[out] * 4,
                          compiler_params=pltpu.CompilerParams(vmem_limit_bytes=VMEM_LIMIT_V7X))(packs, w, m, v)


BIG = ("ffn1_w1", "ffn1_w3", "ffn1_w2", "w_in", "w_out", "ffn2_w1", "ffn2_w3", "ffn2_w2")
SMALL = ("ffn1_norm", "mix_norm", "pool_w", "pool_scale", "ret_norm", "ffn2_norm", "final_norm")
WEIGHTS = ("ffn1_norm", "ffn1_w1", "ffn1_w3", "ffn1_w2", "mix_norm", "w_in", "pool_w", "pool_scale", "ret_norm", "w_out",
           "ffn2_norm", "ffn2_w1", "ffn2_w3", "ffn2_w2", "final_norm")


def _pack(parts):
    return jnp.concatenate([parts[k].reshape(-1, 128) for k in SMALL], axis=0)


def _unpack(pack, like):
    out, row = {}, 0
    for k in SMALL:
        rows = like[k].size // 128
        out[k] = pack[row:row + rows].reshape(like[k].shape)
        row += rows
    return out


def kernel(x, ffn1_norm, ffn1_w1, ffn1_w3, ffn1_w2, mix_norm, w_in, pool_w, pool_scale, ret_norm, w_out, ffn2_norm, ffn2_w1, ffn2_w3, ffn2_w2, final_norm, loss_target, m_ffn1_norm, m_ffn1_w1, m_ffn1_w3, m_ffn1_w2, m_mix_norm, m_w_in, m_pool_w, m_pool_scale, m_ret_norm, m_w_out, m_ffn2_norm, m_ffn2_w1, m_ffn2_w3, m_ffn2_w2, m_final_norm, v_ffn1_norm, v_ffn1_w1, v_ffn1_w3, v_ffn1_w2, v_mix_norm, v_w_in, v_pool_w, v_pool_scale, v_ret_norm, v_w_out, v_ffn2_norm, v_ffn2_w1, v_ffn2_w3, v_ffn2_w2, v_final_norm):
    w = dict(ffn1_norm=ffn1_norm, ffn1_w1=ffn1_w1, ffn1_w3=ffn1_w3, ffn1_w2=ffn1_w2, mix_norm=mix_norm, w_in=w_in, pool_w=pool_w,
             pool_scale=pool_scale, ret_norm=ret_norm, w_out=w_out, ffn2_norm=ffn2_norm, ffn2_w1=ffn2_w1, ffn2_w3=ffn2_w3,
             ffn2_w2=ffn2_w2, final_norm=final_norm)
    m = dict(ffn1_norm=m_ffn1_norm, ffn1_w1=m_ffn1_w1, ffn1_w3=m_ffn1_w3, ffn1_w2=m_ffn1_w2, mix_norm=m_mix_norm, w_in=m_w_in,
             pool_w=m_pool_w, pool_scale=m_pool_scale, ret_norm=m_ret_norm, w_out=m_w_out, ffn2_norm=m_ffn2_norm, ffn2_w1=m_ffn2_w1,
             ffn2_w3=m_ffn2_w3, ffn2_w2=m_ffn2_w2, final_norm=m_final_norm)
    v = dict(ffn1_norm=v_ffn1_norm, ffn1_w1=v_ffn1_w1, ffn1_w3=v_ffn1_w3, ffn1_w2=v_ffn1_w2, mix_norm=v_mix_norm, w_in=v_w_in,
             pool_w=v_pool_w, pool_scale=v_pool_scale, ret_norm=v_ret_norm, w_out=v_w_out, ffn2_norm=v_ffn2_norm, ffn2_w1=v_ffn2_w1,
             ffn2_w3=v_ffn2_w3, ffn2_w2=v_ffn2_w2, final_norm=v_final_norm)
    xs, target = x[0], loss_target[0]
    T = xs.shape[0]
    tables = _ret_tables(T)
    place = jnp.stack([lax.axis_index("c"), 2 * lax.axis_index("x") + lax.axis_index("y")]).astype(jnp.int32)
    sh = {k: w[k][0].astype(BF16) for k in BIG}
    gather = lambda *names: [ChipExchange([sh[k] for k in names], False)]
    wg, grad, delta, new_m, new_v = {}, {}, {}, {}, {}

    def update(names, pieces, name, hosted=()):
        outs, extras = adamw_sharded([(p, w[k][0], m[k][0], v[k][0]) for k, p in zip(names, pieces)], name, hosted)
        for t, k in enumerate(names):
            grad[k], delta[k], new_m[k], new_v[k] = [o[None] for o in outs[4 * t:4 * t + 4]]
        return extras

    def reduce_in_chip(name, partial, recv):
        return prereduce(partial, recv, place, "prereduce_" + name)

    scatter = lambda *reduced: ChipExchange([r[0] for r in reduced], True, [r[1] for r in reduced])

    (wg["ffn1_w1"], wg["ffn1_w3"]), = exchange(gather("ffn1_w1", "ffn1_w3"), "gather_ffn1")
    (n1, a1, b1, s1), ((wg["ffn1_w2"], wg["w_in"]),) = ffn_up(
        xs, ffn1_norm, wg["ffn1_w1"], wg["ffn1_w3"], "ffn1_up", gather("ffn1_w2", "w_in"))
    (h1,), ((wg["w_out"],),) = ffn_down(s1, wg["ffn1_w2"], xs, "ffn1_down", gather("w_out"))
    (u, proj), ((wg["ffn2_w1"],),) = mix_in(h1, mix_norm, wg["w_in"], "mix_in", gather("ffn2_w1"))
    (pa,), _ = pool_fwd(proj, pool_w[0], pool_scale, "pool_fwd")
    (rb, o_pre, r_prev), ((wg["ffn2_w3"],),) = ret_fwd(proj, ret_norm, tables, "ret_fwd", gather("ffn2_w3"))
    (h2,), _ = mix_out(pa, rb, wg["w_out"], h1, "mix_out")
    (n2, a2, b2, s2), ((wg["ffn2_w2"],),) = ffn_up(
        h2, ffn2_norm, wg["ffn2_w1"], wg["ffn2_w3"], "ffn2_up", gather("ffn2_w2"))
    (h3,), _ = ffn_down(s2, wg["ffn2_w2"], h2, "ffn2_down")
    (dh3, loss, d_final), _ = final_loss(h3, final_norm[None], target, "final_loss")
    loss = lax.psum(loss[0, 0], ("x", "y", "c"))

    (da2, db2), _ = ffn_bwd_act(dh3, wg["ffn2_w2"], a2, b2, "ffn2_bwd_act")
    (g_f2w2,), _ = ffn_dw2(s2, dh3, "ffn2_dw2")
    (g_f2w1, g_f2w3), ((r_f2w2,),) = ffn_dw13(n2, da2, db2, "ffn2_dw13", [SiblingExchange([g_f2w2])])
    p_f2w2 = reduce_in_chip("ffn2_w2", g_f2w2, r_f2w2)
    (dh2, d_ffn2), ((q_f2w2,), (r_f2w1, r_f2w3)) = ffn_bwd_in(
        da2, db2, wg["ffn2_w1"], wg["ffn2_w3"], h2, ffn2_norm, dh3, "ffn2_bwd_in",
        [scatter(p_f2w2), SiblingExchange([g_f2w1, g_f2w3])])
    p_f2w1 = reduce_in_chip("ffn2_w1", g_f2w1, r_f2w1)
    p_f2w3 = reduce_in_chip("ffn2_w3", g_f2w3, r_f2w3)
    (dpa, drb, g_wout), _ = mix_out_bwd(dh2, wg["w_out"], pa, rb, "mix_out_bwd")
    (dpool, d_pool_w, d_pool_scale), _ = pool_bwd(proj, dpa, pool_w[0], pool_scale, "pool_bwd")
    (dqkvg, d_ret_norm), ((q_f2w1, q_f2w3), (r_wout,)) = ret_bwd(
        proj, drb, o_pre, r_prev, ret_norm, tables, "ret_bwd", [scatter(p_f2w1, p_f2w3), SiblingExchange([g_wout])])
    p_wout = reduce_in_chip("w_out", g_wout, r_wout)
    d = jnp.concatenate([dpool, dqkvg[0], dqkvg[1], dqkvg[2], dqkvg[3]], axis=1)
    (g_win,), ((q_wout,),) = mix_dwin(u, d, N_CHIPS, "mix_dwin", [scatter(p_wout)])
    (dh1, d_mix), ((r_win,),) = mix_in_bwd(d, wg["w_in"], h1, mix_norm, dh2, "mix_in_bwd", [SiblingExchange([g_win])])
    p_win = reduce_in_chip("w_in", g_win, r_win)
    (da1, db1), ((q_win,),) = ffn_bwd_act(dh1, wg["ffn1_w2"], a1, b1, "ffn1_bwd_act", [scatter(p_win)])
    (g_f1w1, g_f1w3), _ = ffn_dw13(n1, da1, db1, "ffn1_dw13")
    (g_f1w2,), ((r_f1w1, r_f1w3),) = ffn_dw2(s1, dh1, "ffn1_dw2", [SiblingExchange([g_f1w1, g_f1w3])])
    p_f1w1 = reduce_in_chip("ffn1_w1", g_f1w1, r_f1w1)
    p_f1w3 = reduce_in_chip("ffn1_w3", g_f1w3, r_f1w3)
    (dx, d_ffn1), ((q_f1w1, q_f1w3), (r_f1w2,)) = ffn_bwd_in(
        da1, db1, wg["ffn1_w1"], wg["ffn1_w3"], xs, ffn1_norm, dh1, "ffn1_bwd_in",
        [scatter(p_f1w1, p_f1w3), SiblingExchange([g_f1w2])])
    p_f1w2 = reduce_in_chip("ffn1_w2", g_f1w2, r_f1w2)

    small = {"ffn1_norm": d_ffn1, "mix_norm": d_mix, "pool_w": d_pool_w, "pool_scale": d_pool_scale,
             "ret_norm": d_ret_norm, "ffn2_norm": d_ffn2, "final_norm": d_final}
    (q_f1w2,), (packs,) = update(["ffn2_w1", "ffn2_w3", "ffn1_w1", "ffn1_w3"], [q_f2w1, q_f2w3, q_f1w1, q_f1w3], "adamw_w13",
                                 [scatter(p_f1w2), AllExchange(_pack(small))])
    update(["ffn2_w2", "ffn1_w2"], [q_f2w2, q_f1w2], "adamw_w2")
    update(["w_in"], [q_win], "adamw_w_in")
    update(["w_out"], [q_wout], "adamw_w_out")
    outs = adamw_small(packs, _pack(w), _pack(m), _pack(v), "adamw_small")
    for res, pack in zip((grad, delta, new_m, new_v), outs):
        res.update(_unpack(pack, w))

    return (loss, dx[None], *[grad[k] for k in WEIGHTS], *[delta[k] for k in WEIGHTS],
            *[new_m[k] for k in WEIGHTS], *[new_v[k] for k in WEIGHTS])
```

```python
import math

import jax
import jax.numpy as jnp
from jax import lax
from jax.experimental import pallas as pl
from jax.experimental.pallas import tpu as pltpu

F32 = jnp.float32
BF16 = jnp.bfloat16

EPS = 1e-6
N_CHIPS = 4
N_GROUPS = 4
HEAD_DIM = 128
RET_CHUNK = 128
ROPE_BASE = 10000.0
ADAM_LR, ADAM_B1, ADAM_B2, ADAM_EPS, ADAM_WD, ADAM_STEP = 0.001, 0.9, 0.999, 1e-08, 0.01, 10
VMEM_LIMIT_V7X = 56 * 1024 * 1024
ADAMW_VMEM_BUDGET = 32 * 1024 * 1024
MESH = pl.DeviceIdType.MESH
ANY = pl.BlockSpec(memory_space=pl.ANY)


def _dot(a, b):
    return jnp.dot(a, b, preferred_element_type=F32)


def _dot_nt(a, b):
    return lax.dot_general(a, b, (((1,), (1,)), ((), ())), preferred_element_type=F32)


def _dot_tn(a, b):
    return lax.dot_general(a, b, (((0,), (0,)), ((), ())), preferred_element_type=F32)


def _rstd(h):
    return lax.rsqrt(jnp.mean(h * h, axis=-1, keepdims=True) + EPS)


def _rmsnorm_bwd(dn, h, gain):
    r = _rstd(h)
    nh = h * r
    dnh = dn * gain
    dh = r * (dnh - nh * jnp.mean(dnh * nh, axis=-1, keepdims=True))
    return dh, dn * nh


def _silu_parts(a):
    sig = jax.nn.sigmoid(a)
    silu = a * sig
    return silu, sig + silu * (1.0 - sig)


def _mesh_pos():
    return lax.axis_index("x"), lax.axis_index("y"), lax.axis_index("c")


class ChipExchange:
    def __init__(self, srcs, scatter, placed=()):
        n = len(srcs)
        self.inputs, self.scatter, self.n, self.reach = list(srcs) + list(placed), scatter, n, REACH_CHIPS
        self.aliases = {n + t: t for t in range(n)} if scatter else {}
        self.half_rows = [s.shape[1] if scatter else s.shape[0] // 2 for s in srcs]
        self.out_shape = [jax.ShapeDtypeStruct((N_CHIPS, 2 * rh, s.shape[-1]), s.dtype) for s, rh in zip(srcs, self.half_rows)]
        if scatter:
            self.out_shape += [jax.ShapeDtypeStruct((2, rh // 2, s.shape[-1]), s.dtype) for s, rh in zip(srcs, self.half_rows)]
        dma = pltpu.SemaphoreType.DMA
        self.sems = [dma((4 * n,)), dma((4 * n,)), dma((2 * n,)), dma((2 * n,)), dma((4 * n,)), dma((4 * n,))]

    def _copies(self, src, out, sems):
        hop1_send, hop1_recv, hop2_send, hop2_recv, d2d_send, d2d_recv = sems
        x, y, c = _mesh_pos()
        me, dg = 2 * x + y, 2 * (1 - x) + (1 - y)
        sibling = (x, y, 1 - c)
        n = self.n
        mine, theirs = c, 1 - c

        def nb(a):
            nx, ny = x ^ (1 - a), y ^ a
            return 2 * nx + ny, (nx, ny, c)

        def remote(s, d, send, recv, k, to):
            return pltpu.make_async_remote_copy(src_ref=s, dst_ref=d, send_sem=send.at[k], recv_sem=recv.at[k],
                                                device_id=to, device_id_type=MESH)

        class Copies:
            def slot(_, t, chip, half):
                rh = self.half_rows[t]
                return out[t].at[chip, pl.ds(half * rh, rh), :]

            def quarter(_, t, chip, q):
                qh = self.half_rows[t] // 2
                return out[t].at[chip, pl.ds(mine * 2 * qh + q * qh, qh), :]

            def own_shard(k, t):
                return remote(src[t], out[t].at[me], d2d_send, d2d_recv, 4 * t + 3, sibling)

            def hop1(k, t, a, transit=False):
                rh = self.half_rows[t]
                chip, to = nb(a)
                if transit:
                    piece = src[t].at[dg, pl.ds(a * (rh // 2), rh // 2), :]
                    return remote(piece, out[n + t].at[a], hop1_send, hop1_recv, 4 * t + 2 + a, to)
                piece = src[t].at[chip] if self.scatter else src[t].at[pl.ds(mine * rh, rh), :]
                return remote(piece, k.slot(t, me, mine), hop1_send, hop1_recv, 4 * t + a, to)

            def landed1(k, t, a, transit=False):
                here = out[n + t].at[a] if transit else k.slot(t, nb(a)[0], mine)
                return remote(here, here, hop1_send, hop1_recv, 4 * t + (2 if transit else 0) + a, sibling)

            def hop2(k, t, q):
                origin, to = nb(q)[0], nb(1 - q)[1]
                piece = out[n + t].at[q] if self.scatter else k.quarter(t, origin, q)
                return remote(piece, k.quarter(t, origin, q), hop2_send, hop2_recv, 2 * t + q, to)

            def landed2(k, t, q):
                here = k.quarter(t, dg, q)
                return remote(here, here, hop2_send, hop2_recv, 2 * t + q, sibling)

            def d2d(k, t, p, chip, own=False, arriving=False):
                if arriving:
                    there = k.slot(t, chip, theirs)
                    return remote(there, there, d2d_send, d2d_recv, 4 * t + p, sibling)
                piece = src[t].at[me] if own else k.slot(t, chip, mine)
                return remote(piece, k.slot(t, chip, mine), d2d_send, d2d_recv, 4 * t + p, sibling)

        return Copies(), nb, me, dg, c

    def start(self, src, out, sems):
        k, nb, me, dg, c = self._copies(src, out, sems)
        for t in range(self.n):
            for first in range(2):
                a = first ^ c
                k.hop1(t, a).start()
                if self.scatter:
                    k.hop1(t, a, transit=True).start()
            if self.scatter:
                k.d2d(t, 3, me, own=True).start()
            else:
                k.own_shard(t).start()

    def mid(self, src, out, sems):
        k, nb, me, dg, c = self._copies(src, out, sems)
        for t in range(self.n):
            for first in range(2):
                a = first ^ c
                if self.scatter:
                    k.landed1(t, a, transit=True).wait_recv()
                    k.hop2(t, a).start()
                k.landed1(t, a).wait_recv()
                if not self.scatter:
                    k.hop2(t, a).start()
                k.d2d(t, a, nb(a)[0]).start()

    def finish(self, src, out, sems):
        k, nb, me, dg, c = self._copies(src, out, sems)
        for t in range(self.n):
            for q in range(2):
                k.landed2(t, q).wait_recv()
            k.d2d(t, 2, dg).start()
        for t in range(self.n):
            for a in range(2):
                k.d2d(t, a, nb(a)[0], arriving=True).wait_recv()
            k.d2d(t, 2, dg, arriving=True).wait_recv()
            if self.scatter:
                k.d2d(t, 3, me, arriving=True).wait_recv()
        for t in range(self.n):
            for a in range(2):
                k.hop1(t, a).wait_send()
                if self.scatter:
                    k.hop1(t, a, transit=True).wait_send()
                k.hop2(t, a).wait_send()
                k.d2d(t, a, nb(a)[0]).wait_send()
            k.d2d(t, 2, dg).wait_send()
            if self.scatter:
                k.d2d(t, 3, me, own=True).wait_send()
            else:
                k.own_shard(t).wait()


class SiblingExchange:
    def __init__(self, grads):
        self.inputs, self.n, self.aliases, self.reach = list(grads), len(grads), {}, REACH_SIBLING
        self.half_rows = [g.shape[1] // 2 for g in grads]
        self.out_shape = [jax.ShapeDtypeStruct((g.shape[0], rh, g.shape[2]), g.dtype) for g, rh in zip(grads, self.half_rows)]
        self.sems = [pltpu.SemaphoreType.DMA((self.n,)), pltpu.SemaphoreType.DMA((self.n,))]

    def _plan(self, src, out, sems):
        x, y, c = _mesh_pos()
        return [pltpu.make_async_remote_copy(
            src_ref=src[t].at[:, pl.ds((1 - c) * self.half_rows[t], self.half_rows[t]), :], dst_ref=out[t],
            send_sem=sems[0].at[t], recv_sem=sems[1].at[t], device_id=(x, y, 1 - c), device_id_type=MESH) for t in range(self.n)]

    def start(self, src, out, sems):
        for cp in self._plan(src, out, sems):
            cp.start()

    def mid(self, src, out, sems):
        pass

    def finish(self, src, out, sems):
        for cp in self._plan(src, out, sems):
            cp.wait()


REACH_SIBLING, REACH_CHIPS, REACH_ALL = 0, 1, 2


def _entry_barrier(reach):
    x, y, c = _mesh_pos()
    peers = [(x, y, 1 - c)]
    if reach == REACH_CHIPS:
        peers += [(1 - x, y, c), (x, 1 - y, c)]
    elif reach == REACH_ALL:
        peers = [(x ^ dx, y ^ dy, c ^ dc) for dx in (0, 1) for dy in (0, 1) for dc in (0, 1)][1:]
    barrier = pltpu.get_barrier_semaphore()
    for peer in peers:
        pl.semaphore_signal(barrier, inc=1, device_id=peer, device_id_type=MESH)
    pl.semaphore_wait(barrier, len(peers))


def _call(body, hosted=(), *, name, in_specs, out_specs, out_shape, args, grid=(), scratch_shapes=()):
    n_in, n_out, n_scr = len(in_specs), len(out_specs), len(scratch_shapes)
    total = math.prod(grid)
    mid_step = max(0, (5 * total) // 8 - 1)

    def full(*refs):
        pos = [0]

        def take(k):
            pos[0] += k
            return refs[pos[0] - k:pos[0]]

        ins, h_in = take(n_in), [take(len(h.inputs)) for h in hosted]
        outs, h_out = take(n_out), [take(len(h.out_shape)) for h in hosted]
        scr, h_sem = take(n_scr), [take(len(h.sems)) for h in hosted]
        step = 0
        for axis, size in enumerate(grid):
            step = step * size + pl.program_id(axis)

        def phase(at, method):
            if not hosted:
                return

            def run():
                if method == "start":
                    _entry_barrier(reach)
                for h, s, o, m in zip(hosted, h_in, h_out, h_sem):
                    getattr(h, method)(s, o, m)

            if total == 1:
                run()
            else:
                pl.when(step == at)(run)

        phase(0, "start")
        body(*ins, *outs, *scr)
        phase(mid_step, "mid")
        phase(total - 1, "finish")

    aliases, i0, o0 = {}, n_in, n_out
    for h in hosted:
        aliases.update({i0 + i: o0 + o for i, o in h.aliases.items()})
        i0, o0 = i0 + len(h.inputs), o0 + len(h.out_shape)
    reach = max((h.reach for h in hosted), default=None)
    params = dict(vmem_limit_bytes=VMEM_LIMIT_V7X)
    if hosted:
        params["collective_id"] = reach
    results = pl.pallas_call(
        full, name=name, grid=grid,
        in_specs=list(in_specs) + [ANY] * (i0 - n_in),
        out_specs=list(out_specs) + [ANY] * (o0 - n_out),
        out_shape=list(out_shape) + [s for h in hosted for s in h.out_shape],
        scratch_shapes=list(scratch_shapes) + [s for h in hosted for s in h.sems],
        input_output_aliases=aliases,
        compiler_params=pltpu.CompilerParams(**params),
    )(*args, *[s for h in hosted for s in h.inputs])
    outs, extras, pos = list(results[:n_out]), [], n_out
    for h in hosted:
        extras.append(list(results[pos:pos + h.n]))
        pos += len(h.out_shape)
    return outs, extras


def exchange(hosted, name):
    return _call(lambda: None, hosted, name=name, in_specs=[], out_specs=[], out_shape=[], args=[])[1]


class AllExchange:
    def __init__(self, pack):
        self.inputs, self.n, self.aliases, self.reach = [pack], 1, {}, REACH_ALL
        self.out_shape = [jax.ShapeDtypeStruct((2 * N_CHIPS,) + pack.shape, pack.dtype)]
        self.sems = [pltpu.SemaphoreType.DMA, pltpu.SemaphoreType.DMA((7,)), pltpu.SemaphoreType.DMA((7,))]

    def _copies(self, src, out, sems):
        local_sem, send_sem, recv_sem = sems
        x, y, c = _mesh_pos()
        flips = [(dx, dy, dc) for dx in (0, 1) for dy in (0, 1) for dc in (0, 1)][1:]
        peers = [(x ^ dx, y ^ dy, c ^ dc) for dx, dy, dc in flips]
        remote = lambda s, d, k: pltpu.make_async_remote_copy(
            src_ref=s, dst_ref=d, send_sem=send_sem.at[k], recv_sem=recv_sem.at[k], device_id=peers[k], device_id_type=MESH)
        sends = [remote(src[0], out[0].at[4 * x + 2 * y + c], k) for k in range(7)]
        landed = [remote(out[0].at[4 * px + 2 * py + pc], out[0].at[4 * px + 2 * py + pc], k) for k, (px, py, pc) in enumerate(peers)]
        return sends, landed, pltpu.make_async_copy(src[0], out[0].at[4 * x + 2 * y + c], local_sem)

    def start(self, src, out, sems):
        sends, _, local = self._copies(src, out, sems)
        for cp in sends:
            cp.start()
        local.start()

    def mid(self, src, out, sems):
        pass

    def finish(self, src, out, sems):
        sends, landed, local = self._copies(src, out, sems)
        for cp in landed:
            cp.wait_recv()
        for cp in sends:
            cp.wait_send()
        local.wait()


def ffn_up(h, gain, w1g, w3g, name, hosted=()):
    T, D = h.shape
    nsh, _, Fs = w1g.shape
    tm = min(T, 1024)

    def body(h_ref, g_ref, w1_ref, w3_ref, n_ref, a_ref, b_ref, s_ref):
        @pl.when(pl.program_id(1) == 0)
        def _():
            hh = h_ref[...]
            n_ref[...] = (hh * _rstd(hh) * g_ref[...]).astype(BF16)

        n = n_ref[...]
        a = _dot(n, w1_ref[0])
        b = _dot(n, w3_ref[0])
        a_ref[0] = a.astype(BF16)
        b_ref[0] = b.astype(BF16)
        s_ref[0] = (a * jax.nn.sigmoid(a) * b).astype(BF16)

    act = jax.ShapeDtypeStruct((nsh, T, Fs), BF16)
    act_spec = pl.BlockSpec((1, tm, Fs), lambda i, j: (j, i, 0))
    w_spec = pl.BlockSpec((1, D, Fs), lambda i, j: (j, 0, 0))
    return _call(
        body, hosted, name=name, grid=(T // tm, nsh),
        in_specs=[pl.BlockSpec((tm, D), lambda i, j: (i, 0)), pl.BlockSpec((1, D), lambda i, j: (0, 0)), w_spec, w_spec],
        out_specs=[pl.BlockSpec((tm, D), lambda i, j: (i, 0)), act_spec, act_spec, act_spec],
        out_shape=[jax.ShapeDtypeStruct((T, D), BF16), act, act, act],
        args=[h, gain, w1g, w3g])


def ffn_down(s, w2g, h, name, hosted=()):
    nsh, T, Fs = s.shape
    D = h.shape[1]
    tm = min(T, 512)

    def body(s_ref, w2_ref, h_ref, o_ref):
        f = _dot(s_ref[0], w2_ref[0])
        for j in range(1, nsh):
            f += _dot(s_ref[j], w2_ref[j])
        o_ref[...] = h_ref[...] + 0.5 * f

    return _call(
        body, hosted, name=name, grid=(T // tm,),
        in_specs=[pl.BlockSpec((nsh, tm, Fs), lambda i: (0, i, 0)), pl.BlockSpec((nsh, Fs, D), lambda i: (0, 0, 0)),
                  pl.BlockSpec((tm, D), lambda i: (i, 0))],
        out_specs=[pl.BlockSpec((tm, D), lambda i: (i, 0))],
        out_shape=[jax.ShapeDtypeStruct((T, D), F32)],
        args=[s, w2g, h])


def ffn_bwd_act(dh, w2g, a, b, name, hosted=()):
    T, D = dh.shape
    nsh, Fs, _ = w2g.shape
    tm = min(T, 1024)

    def body(dh_ref, w2_ref, a_ref, b_ref, da_ref, db_ref):
        df = (0.5 * dh_ref[...]).astype(BF16)
        ds = _dot_nt(df, w2_ref[0])
        silu, dsilu = _silu_parts(a_ref[0].astype(F32))
        da_ref[0] = (ds * b_ref[0].astype(F32) * dsilu).astype(BF16)
        db_ref[0] = (ds * silu).astype(BF16)

    act = jax.ShapeDtypeStruct((nsh, T, Fs), BF16)
    act_spec = pl.BlockSpec((1, tm, Fs), lambda j, i: (j, i, 0))
    return _call(
        body, hosted, name=name, grid=(nsh, T // tm),
        in_specs=[pl.BlockSpec((tm, D), lambda j, i: (i, 0)), pl.BlockSpec((1, Fs, D), lambda j, i: (j, 0, 0)), act_spec, act_spec],
        out_specs=[act_spec, act_spec],
        out_shape=[act, act],
        args=[dh, w2g, a, b])


def ffn_dw2(s, dh, name, hosted=()):
    nsh, T, Fs = s.shape
    D = dh.shape[1]
    tk = min(T, 512)
    nk = T // tk

    def body(s_ref, dh_ref, o_ref, acc):
        k = pl.program_id(1)

        @pl.when(k == 0)
        def _():
            acc[...] = jnp.zeros_like(acc)

        acc[...] += _dot_tn(s_ref[0], (0.5 * dh_ref[...]).astype(BF16))

        @pl.when(k == nk - 1)
        def _():
            o_ref[0] = acc[...].astype(BF16)

    return _call(
        body, hosted, name=name, grid=(nsh, nk),
        in_specs=[pl.BlockSpec((1, tk, Fs), lambda j, k: (j, k, 0)), pl.BlockSpec((tk, D), lambda j, k: (k, 0))],
        out_specs=[pl.BlockSpec((1, Fs, D), lambda j, k: (j, 0, 0))],
        out_shape=[jax.ShapeDtypeStruct((nsh, Fs, D), BF16)],
        scratch_shapes=[pltpu.VMEM((Fs, D), F32)],
        args=[s, dh])


def ffn_dw13(n, da, db, name, hosted=()):
    T, D = n.shape
    nsh, _, Fs = da.shape
    tk = min(T, 512)
    nk = T // tk

    def body(n_ref, da_ref, db_ref, o1_ref, o3_ref, acc1, acc3):
        k = pl.program_id(1)

        @pl.when(k == 0)
        def _():
            acc1[...] = jnp.zeros_like(acc1)
            acc3[...] = jnp.zeros_like(acc3)

        nn = n_ref[...]
        acc1[...] += _dot_tn(nn, da_ref[0])
        acc3[...] += _dot_tn(nn, db_ref[0])

        @pl.when(k == nk - 1)
        def _():
            o1_ref[0] = acc1[...].astype(BF16)
            o3_ref[0] = acc3[...].astype(BF16)

    act_spec = pl.BlockSpec((1, tk, Fs), lambda j, k: (j, k, 0))
    out = jax.ShapeDtypeStruct((nsh, D, Fs), BF16)
    out_spec = pl.BlockSpec((1, D, Fs), lambda j, k: (j, 0, 0))
    return _call(
        body, hosted, name=name, grid=(nsh, nk),
        in_specs=[pl.BlockSpec((tk, D), lambda j, k: (k, 0)), act_spec, act_spec],
        out_specs=[out_spec, out_spec],
        out_shape=[out, out],
        scratch_shapes=[pltpu.VMEM((D, Fs), F32), pltpu.VMEM((D, Fs), F32)],
        args=[n, da, db])


def ffn_bwd_in(da, db, w1g, w3g, h, gain, dh, name, hosted=()):
    nsh, T, Fs = da.shape
    D = h.shape[1]
    tm = min(T, 256)

    def body(da_ref, db_ref, w1_ref, w3_ref, h_ref, g_ref, dh_ref, o_ref, dg_ref):
        dn = _dot_nt(da_ref[0], w1_ref[0]) + _dot_nt(db_ref[0], w3_ref[0])
        for j in range(1, nsh):
            dn += _dot_nt(da_ref[j], w1_ref[j]) + _dot_nt(db_ref[j], w3_ref[j])
        dhn, dg = _rmsnorm_bwd(dn, h_ref[...], g_ref[...])
        o_ref[...] = dh_ref[...] + dhn

        @pl.when(pl.program_id(0) == 0)
        def _():
            dg_ref[...] = jnp.zeros_like(dg_ref)

        dg_ref[...] += jnp.sum(dg, axis=0, keepdims=True)

    act_spec = pl.BlockSpec((nsh, tm, Fs), lambda i: (0, i, 0))
    w_spec = pl.BlockSpec((nsh, D, Fs), lambda i: (0, 0, 0))
    row_spec = pl.BlockSpec((tm, D), lambda i: (i, 0))
    vec_spec = pl.BlockSpec((1, D), lambda i: (0, 0))
    return _call(
        body, hosted, name=name, grid=(T // tm,),
        in_specs=[act_spec, act_spec, w_spec, w_spec, row_spec, vec_spec, row_spec],
        out_specs=[row_spec, vec_spec],
        out_shape=[jax.ShapeDtypeStruct((T, D), F32), jax.ShapeDtypeStruct((1, D), F32)],
        args=[da, db, w1g, w3g, h, gain, dh])


def mix_in(h, gain, wing, name, hosted=()):
    T, D = h.shape
    nsh, _, Cs = wing.shape
    tm = min(T, 512)

    def body(h_ref, g_ref, w_ref, u_ref, p_ref):
        hh = h_ref[...]
        u = (hh * _rstd(hh) * g_ref[...]).astype(BF16)
        u_ref[...] = u
        for j in range(nsh):
            p_ref[:, j * Cs:(j + 1) * Cs] = _dot(u, w_ref[j])

    return _call(
        body, hosted, name=name, grid=(T // tm,),
        in_specs=[pl.BlockSpec((tm, D), lambda i: (i, 0)), pl.BlockSpec((1, D), lambda i: (0, 0)),
                  pl.BlockSpec((nsh, D, Cs), lambda i: (0, 0, 0))],
        out_specs=[pl.BlockSpec((tm, D), lambda i: (i, 0)), pl.BlockSpec((tm, nsh * Cs), lambda i: (i, 0))],
        out_shape=[jax.ShapeDtypeStruct((T, D), BF16), jax.ShapeDtypeStruct((T, nsh * Cs), F32)],
        args=[h, gain, wing])


def mix_out(a, b, woutg, h, name, hosted=()):
    T, W = a.shape
    D = h.shape[1]
    wout = woutg.reshape(2, W, D)
    tm = min(T, 512)

    def body(a_ref, b_ref, w_ref, h_ref, o_ref):
        o_ref[...] = h_ref[...] + _dot(a_ref[...], w_ref[0]) + _dot(b_ref[...], w_ref[1])

    return _call(
        body, hosted, name=name, grid=(T // tm,),
        in_specs=[pl.BlockSpec((tm, W), lambda i: (i, 0)), pl.BlockSpec((tm, W), lambda i: (i, 0)),
                  pl.BlockSpec((2, W, D), lambda i: (0, 0, 0)), pl.BlockSpec((tm, D), lambda i: (i, 0))],
        out_specs=[pl.BlockSpec((tm, D), lambda i: (i, 0))],
        out_shape=[jax.ShapeDtypeStruct((T, D), F32)],
        args=[a, b, wout, h])


def mix_out_bwd(dh, woutg, a, b, name, hosted=()):
    T, D = dh.shape
    W = a.shape[1]
    nsh, Rs, _ = woutg.shape
    wout = woutg.reshape(2, W, D)
    tk = min(T, 512)
    nk = T // tk

    def body(dh_ref, w_ref, a_ref, b_ref, da_ref, db_ref, dw_ref, acc):
        k = pl.program_id(0)

        @pl.when(k == 0)
        def _():
            acc[...] = jnp.zeros_like(acc)

        dhb = dh_ref[...].astype(BF16)
        da_ref[...] = _dot_nt(dhb, w_ref[0])
        db_ref[...] = _dot_nt(dhb, w_ref[1])
        acc[0:W, :] += _dot_tn(a_ref[...], dhb)
        acc[W:2 * W, :] += _dot_tn(b_ref[...], dhb)

        @pl.when(k == nk - 1)
        def _():
            for j in range(nsh):
                dw_ref[j] = acc[j * Rs:(j + 1) * Rs, :].astype(BF16)

    return _call(
        body, hosted, name=name, grid=(nk,),
        in_specs=[pl.BlockSpec((tk, D), lambda k: (k, 0)), pl.BlockSpec((2, W, D), lambda k: (0, 0, 0)),
                  pl.BlockSpec((tk, W), lambda k: (k, 0)), pl.BlockSpec((tk, W), lambda k: (k, 0))],
        out_specs=[pl.BlockSpec((tk, W), lambda k: (k, 0)), pl.BlockSpec((tk, W), lambda k: (k, 0)),
                   pl.BlockSpec((nsh, Rs, D), lambda k: (0, 0, 0))],
        out_shape=[jax.ShapeDtypeStruct((T, W), F32), jax.ShapeDtypeStruct((T, W), F32),
                   jax.ShapeDtypeStruct((nsh, Rs, D), BF16)],
        scratch_shapes=[pltpu.VMEM((2 * W, D), F32)],
        args=[dh, wout, a, b])


def mix_dwin(u, d, nsh, name, hosted=()):
    T, D = u.shape
    Cs = d.shape[1] // nsh
    tk = min(T, 512)
    nk = T // tk

    def body(u_ref, d_ref, o_ref, acc):
        k = pl.program_id(1)

        @pl.when(k == 0)
        def _():
            acc[...] = jnp.zeros_like(acc)

        acc[...] += _dot_tn(u_ref[...], d_ref[...])

        @pl.when(k == nk - 1)
        def _():
            o_ref[0] = acc[...].astype(BF16)

    return _call(
        body, hosted, name=name, grid=(nsh, nk),
        in_specs=[pl.BlockSpec((tk, D), lambda j, k: (k, 0)), pl.BlockSpec((tk, Cs), lambda j, k: (k, j))],
        out_specs=[pl.BlockSpec((1, D, Cs), lambda j, k: (j, 0, 0))],
        out_shape=[jax.ShapeDtypeStruct((nsh, D, Cs), BF16)],
        scratch_shapes=[pltpu.VMEM((D, Cs), F32)],
        args=[u, d])


def mix_in_bwd(d, wing, h, gain, dh, name, hosted=()):
    T, D = h.shape
    nsh, _, Cs = wing.shape
    tm = min(T, 512)

    def body(d_ref, w_ref, h_ref, g_ref, dh_ref, o_ref, dg_ref):
        du = _dot_nt(d_ref[:, 0:Cs], w_ref[0])
        for j in range(1, nsh):
            du += _dot_nt(d_ref[:, j * Cs:(j + 1) * Cs], w_ref[j])
        dhn, dg = _rmsnorm_bwd(du, h_ref[...], g_ref[...])
        o_ref[...] = dh_ref[...] + dhn

        @pl.when(pl.program_id(0) == 0)
        def _():
            dg_ref[...] = jnp.zeros_like(dg_ref)

        dg_ref[...] += jnp.sum(dg, axis=0, keepdims=True)

    row_spec = pl.BlockSpec((tm, D), lambda i: (i, 0))
    vec_spec = pl.BlockSpec((1, D), lambda i: (0, 0))
    return _call(
        body, hosted, name=name, grid=(T // tm,),
        in_specs=[pl.BlockSpec((tm, nsh * Cs), lambda i: (i, 0)), pl.BlockSpec((nsh, D, Cs), lambda i: (0, 0, 0)),
                  row_spec, vec_spec, row_spec],
        out_specs=[row_spec, vec_spec],
        out_shape=[jax.ShapeDtypeStruct((T, D), F32), jax.ShapeDtypeStruct((1, D), F32)],
        args=[d, wing, h, gain, dh])


def _pool_window(x, group, T, trailing):
    rows = lax.broadcasted_iota(jnp.int32, x.shape, 0)

    def shifted(z, k):
        if trailing:
            return jnp.where(rows >= k, pltpu.roll(z, k, 0), 0.0)
        return jnp.where(rows < T - k, pltpu.roll(z, T - k, 0), 0.0)

    s2 = x + shifted(x, 1)
    s4 = s2 + shifted(s2, 2)
    s8 = s4 + shifted(s4, 4)
    s16 = s8 + shifted(s8, 8)
    return jnp.where(group == 0, s2, jnp.where(group == 1, s4, jnp.where(group == 2, s8, s16)))


def _pool_count(group, shape):
    rows = lax.broadcasted_iota(jnp.int32, shape, 0)
    w = jnp.where(group == 0, 2, jnp.where(group == 1, 4, jnp.where(group == 2, 8, 16)))
    return jnp.minimum(rows + 1, w).astype(F32)


def pool_fwd(proj, pool_w, pool_scale, name, hosted=()):
    T = proj.shape[0]
    Hd = HEAD_DIM

    def body(x_ref, w_ref, sc_ref, a_ref):
        g = pl.program_id(0)
        x = x_ref[...]
        pooled = _pool_window(x, g, T, True) / _pool_count(g, x.shape) - x
        a_ref[...] = (_dot(pooled.astype(BF16), w_ref[0].astype(BF16)) * sc_ref[...]).astype(BF16)

    return _call(
        body, hosted, name=name, grid=(N_GROUPS,),
        in_specs=[pl.BlockSpec((T, Hd), lambda g: (0, g)), pl.BlockSpec((1, Hd, Hd), lambda g: (g, 0, 0)),
                  pl.BlockSpec((1, Hd), lambda g: (0, g))],
        out_specs=[pl.BlockSpec((T, Hd), lambda g: (0, g))],
        out_shape=[jax.ShapeDtypeStruct((T, N_GROUPS * Hd), BF16)],
        args=[proj, pool_w, pool_scale])


def pool_bwd(proj, da, pool_w, pool_scale, name, hosted=()):
    T = proj.shape[0]
    Hd = HEAD_DIM

    def body(x_ref, da_ref, w_ref, sc_ref, dx_ref, dw_ref, dsc_ref):
        g = pl.program_id(0)
        x = x_ref[...]
        cnt = _pool_count(g, x.shape)
        pooled = (_pool_window(x, g, T, True) / cnt - x).astype(BF16)
        wb = w_ref[0].astype(BF16)
        dav = da_ref[...]
        dsc_ref[...] = jnp.sum(dav * _dot(pooled, wb), axis=0, keepdims=True)
        dout = (dav * sc_ref[...]).astype(BF16)
        dw_ref[0] = _dot_tn(pooled, dout)
        dpooled = _dot_nt(dout, wb)
        dx_ref[...] = (_pool_window(dpooled / cnt, g, T, False) - dpooled).astype(BF16)

    col_spec = pl.BlockSpec((T, Hd), lambda g: (0, g))
    return _call(
        body, hosted, name=name, grid=(N_GROUPS,),
        in_specs=[col_spec, col_spec, pl.BlockSpec((1, Hd, Hd), lambda g: (g, 0, 0)), pl.BlockSpec((1, Hd), lambda g: (0, g))],
        out_specs=[col_spec, pl.BlockSpec((1, Hd, Hd), lambda g: (g, 0, 0)), pl.BlockSpec((1, Hd), lambda g: (0, g))],
        out_shape=[jax.ShapeDtypeStruct((T, N_GROUPS * Hd), BF16), jax.ShapeDtypeStruct((N_GROUPS, Hd, Hd), F32),
                   jax.ShapeDtypeStruct((1, N_GROUPS * Hd), F32)],
        args=[proj, da, pool_w, pool_scale])


def _ret_tables(T):
    Hd, C = HEAD_DIM, RET_CHUNK
    inv_freq = 1.0 / (ROPE_BASE ** (jnp.arange(0, Hd, 2, dtype=F32) / Hd))
    ang = jnp.arange(T, dtype=F32)[:, None] * inv_freq[None, :]
    cos, sin = jnp.cos(ang), jnp.sin(ang)
    cos2 = jnp.concatenate([cos, cos], axis=-1)
    sin2 = jnp.concatenate([-sin, sin], axis=-1)
    log_gamma = jnp.log1p(-jnp.exp2(-5.0 - jnp.arange(N_GROUPS, dtype=F32)))
    pos = jnp.arange(C, dtype=F32)
    rel = pos[:, None] - pos[None, :]
    intra = jnp.where(rel[None] >= 0, jnp.exp(log_gamma[:, None, None] * jnp.maximum(rel, 0.0)[None]), 0.0)
    k_tail = jnp.exp(log_gamma[:, None] * (C - 1 - pos)[None, :])
    q_head = jnp.exp(log_gamma[:, None] * (pos + 1.0)[None, :])
    chunk_decay = jnp.exp(log_gamma * C)
    wide = lambda t: jnp.broadcast_to(t[:, :, None], (N_GROUPS, C, Hd))
    return cos2, sin2, intra, wide(k_tail), wide(q_head), jnp.broadcast_to(chunk_decay[:, None, None], (N_GROUPS, 1, Hd))


def _rope(x, cos2, sin2):
    return x * cos2 + pltpu.roll(x, HEAD_DIM // 2, 1) * sin2


def _rope_t(d, cos2, sin2):
    return d * cos2 + pltpu.roll(d * sin2, HEAD_DIM // 2, 1)


def _ret_specs(tseg, seg_of):
    Hd, G = HEAD_DIM, N_GROUPS
    col = lambda kind: pl.BlockSpec((tseg, Hd), lambda h, s: (seg_of(s), G * kind + h))
    tab = pl.BlockSpec((tseg, Hd), lambda h, s: (seg_of(s), 0))
    head = pl.BlockSpec((1, RET_CHUNK, Hd), lambda h, s: (h, 0, 0))
    cd = pl.BlockSpec((1, 1, Hd), lambda h, s: (h, 0, 0))
    gain = pl.BlockSpec((1, Hd), lambda h, s: (0, h))
    return col, tab, head, cd, gain


def ret_fwd(proj, ret_norm, tables, name, hosted=()):
    T = proj.shape[0]
    Hd, C, G = HEAD_DIM, RET_CHUNK, N_GROUPS
    tseg = min(T, 1024)
    nseg, nck = T // tseg, tseg // C
    scale = Hd ** -0.5
    cos2, sin2, intra, k_tail, q_head, chunk_decay = tables

    def body(q_ref, k_ref, v_ref, g_ref, gain_ref, cos_ref, sin_ref, m_ref, kt_ref, qh_ref, cd_ref,
             b_ref, o_ref, rp_ref, state):
        @pl.when(pl.program_id(1) == 0)
        def _():
            state[...] = jnp.zeros_like(state)

        def chunk(ci, carry):
            rows = pl.ds(pl.multiple_of(ci * C, C), C)
            cos, sin = cos_ref[rows, :], sin_ref[rows, :]
            qr = _rope(q_ref[rows, :], cos, sin)
            kr = _rope(k_ref[rows, :], cos, sin) * scale
            qb, kb, vb = qr.astype(BF16), kr.astype(BF16), v_ref[rows, :].astype(BF16)
            r = state[...]
            rp_ref[0, ci] = r.astype(BF16)
            sc = _dot_nt(qb, kb) * m_ref[0]
            o = _dot(sc.astype(BF16), vb) + _dot((qr * qh_ref[0]).astype(BF16), r.astype(BF16))
            state[...] = cd_ref[0] * r + _dot_tn((kr * kt_ref[0]).astype(BF16), vb)
            o_ref[rows, :] = o
            on = o * _rstd(o)
            b_ref[rows, :] = (jax.nn.silu(g_ref[rows, :]) * (on * gain_ref[...])).astype(BF16)
            return carry

        lax.fori_loop(0, nck, chunk, 0)

    col, tab, head, cd, gain = _ret_specs(tseg, lambda s: s)
    out_col = pl.BlockSpec((tseg, Hd), lambda h, s: (s, h))
    return _call(
        body, hosted, name=name, grid=(G, nseg),
        in_specs=[col(1), col(2), col(3), col(4), gain, tab, tab, head, head, head, cd],
        out_specs=[out_col, out_col, pl.BlockSpec((1, nck, Hd, Hd), lambda h, s: (h, s, 0, 0))],
        out_shape=[jax.ShapeDtypeStruct((T, G * Hd), BF16), jax.ShapeDtypeStruct((T, G * Hd), F32),
                   jax.ShapeDtypeStruct((G, T // C, Hd, Hd), BF16)],
        scratch_shapes=[pltpu.VMEM((Hd, Hd), F32)],
        args=[proj, proj, proj, proj, ret_norm, cos2, sin2, intra, k_tail, q_head, chunk_decay])


def ret_bwd(proj, db, o_pre, r_prev, ret_norm, tables, name, hosted=()):
    T = proj.shape[0]
    Hd, C, G = HEAD_DIM, RET_CHUNK, N_GROUPS
    tseg = min(T, 1024)
    nseg, nck = T // tseg, tseg // C
    scale = Hd ** -0.5
    cos2, sin2, intra, k_tail, q_head, chunk_decay = tables

    def body(q_ref, k_ref, v_ref, g_ref, db_ref, o_ref, rp_ref, gain_ref, cos_ref, sin_ref, m_ref, kt_ref, qh_ref, cd_ref,
             d_ref, dgain_ref, gstate):
        @pl.when(pl.program_id(1) == 0)
        def _():
            gstate[...] = jnp.zeros_like(gstate)
            dgain_ref[...] = jnp.zeros_like(dgain_ref)

        def chunk(t, carry):
            ci = nck - 1 - t
            rows = pl.ds(pl.multiple_of(ci * C, C), C)
            cos, sin = cos_ref[rows, :], sin_ref[rows, :]
            qr = _rope(q_ref[rows, :], cos, sin)
            kr = _rope(k_ref[rows, :], cos, sin) * scale
            qb, kb, vb = qr.astype(BF16), kr.astype(BF16), v_ref[rows, :].astype(BF16)
            qhb, ktb = (qr * qh_ref[0]).astype(BF16), (kr * kt_ref[0]).astype(BF16)
            sc = (_dot_nt(qb, kb) * m_ref[0]).astype(BF16)
            o = o_ref[rows, :]
            rstd = _rstd(o)
            on = o * rstd
            gain = gain_ref[...]
            silu, dsilu = _silu_parts(g_ref[rows, :])
            dy = db_ref[rows, :]
            dgain_ref[...] += jnp.sum(dy * silu * on, axis=0, keepdims=True)
            dg = dy * on * gain * dsilu
            don = dy * silu * gain
            dob = (rstd * (don - on * jnp.mean(don * on, axis=-1, keepdims=True))).astype(BF16)
            gn = gstate[...]
            gb = gn.astype(BF16)
            da = (_dot_nt(dob, vb) * m_ref[0]).astype(BF16)
            dq = _dot(da, kb) + _dot_nt(dob, rp_ref[0, ci]) * qh_ref[0]
            dk = _dot_tn(da, qb) + _dot_nt(vb, gb) * kt_ref[0]
            dv = _dot_tn(sc, dob) + _dot(ktb, gb)
            gstate[...] = cd_ref[0] * gn + _dot_tn(qhb, dob)
            d_ref[0, rows, :] = _rope_t(dq, cos, sin).astype(BF16)
            d_ref[1, rows, :] = _rope_t(dk * scale, cos, sin).astype(BF16)
            d_ref[2, rows, :] = dv.astype(BF16)
            d_ref[3, rows, :] = dg.astype(BF16)
            return carry

        lax.fori_loop(0, nck, chunk, 0)

    rev = lambda s: nseg - 1 - s
    col, tab, head, cd, gain = _ret_specs(tseg, rev)
    act = pl.BlockSpec((tseg, Hd), lambda h, s: (rev(s), h))
    return _call(
        body, hosted, name=name, grid=(G, nseg),
        in_specs=[col(1), col(2), col(3), col(4), act, act, pl.BlockSpec((1, nck, Hd, Hd), lambda h, s: (h, rev(s), 0, 0)),
                  gain, tab, tab, head, head, head, cd],
        out_specs=[pl.BlockSpec((4, tseg, Hd), lambda h, s: (0, rev(s), h)), gain],
        out_shape=[jax.ShapeDtypeStruct((4, T, G * Hd), BF16), jax.ShapeDtypeStruct((1, G * Hd), F32)],
        scratch_shapes=[pltpu.VMEM((Hd, Hd), F32)],
        args=[proj, proj, proj, proj, db, o_pre, r_prev, ret_norm, cos2, sin2, intra, k_tail, q_head, chunk_decay])


def final_loss(h, gain, target, name, hosted=()):
    T, D = h.shape
    tm = min(T, 512)

    def body(h_ref, g_ref, t_ref, dh_ref, loss_ref, dg_ref):
        @pl.when(pl.program_id(0) == 0)
        def _():
            loss_ref[...] = jnp.zeros_like(loss_ref)
            dg_ref[...] = jnp.zeros_like(dg_ref)

        hh = h_ref[...]
        gain_v = g_ref[...]
        err = hh * _rstd(hh) * gain_v - t_ref[...]
        loss_ref[...] += 0.5 * jnp.sum(jnp.mean(err * err, axis=-1, keepdims=True), axis=0, keepdims=True)
        dhn, dg = _rmsnorm_bwd(err * (1.0 / D), hh, gain_v)
        dh_ref[...] = dhn
        dg_ref[...] += jnp.sum(dg, axis=0, keepdims=True)

    row_spec = pl.BlockSpec((tm, D), lambda i: (i, 0))
    vec_spec = pl.BlockSpec((1, D), lambda i: (0, 0))
    return _call(
        body, hosted, name=name, grid=(T // tm,),
        in_specs=[row_spec, vec_spec, row_spec],
        out_specs=[row_spec, pl.BlockSpec((1, 128), lambda i: (0, 0)), vec_spec],
        out_shape=[jax.ShapeDtypeStruct((T, D), F32), jax.ShapeDtypeStruct((1, 128), F32), jax.ShapeDtypeStruct((1, D), F32)],
        args=[h, gain, target])


def prereduce(grad, recv, place, name):
    nsh, R, C = grad.shape
    rh = R // 2

    def body(place_ref, g_ref, r_ref, o_ref, own_ref):
        piece = (g_ref[...].astype(F32) + r_ref[...].astype(F32)).astype(BF16)
        o_ref[...] = piece

        @pl.when(pl.program_id(0) == place_ref[1])
        def _():
            own_ref[...] = piece

    return pl.pallas_call(
        body, name=name,
        grid_spec=pltpu.PrefetchScalarGridSpec(
            num_scalar_prefetch=1, grid=(nsh,),
            in_specs=[pl.BlockSpec((1, rh, C), lambda j, p: (j, p[0], 0)), pl.BlockSpec((1, rh, C), lambda j, p: (j, 0, 0))],
            out_specs=[pl.BlockSpec((1, rh, C), lambda j, p: (j, 0, 0)), pl.BlockSpec((1, rh, C), lambda j, p: (p[1], p[0], 0))]),
        out_shape=[jax.ShapeDtypeStruct((nsh, rh, C), BF16), jax.ShapeDtypeStruct((nsh, R, C), BF16)],
        compiler_params=pltpu.CompilerParams(vmem_limit_bytes=VMEM_LIMIT_V7X),
    )(place, grad, recv)


def _adamw(w, g, m, v):
    m = ADAM_B1 * m + (1.0 - ADAM_B1) * g
    v = ADAM_B2 * v + (1.0 - ADAM_B2) * (g * g)
    m_hat = m / (1.0 - ADAM_B1 ** ADAM_STEP)
    v_hat = v / (1.0 - ADAM_B2 ** ADAM_STEP)
    return -ADAM_LR * (m_hat / (jnp.sqrt(v_hat) + ADAM_EPS) + ADAM_WD * w), m, v


def adamw_sharded(tensors, name, hosted=()):
    nt = len(tensors)
    nsh, R, C = tensors[0][0].shape
    lanes = -(-C // 128) * 128
    per_row = 2 * nt * lanes * (nsh * 2 + 7 * 4)
    tr = max(r for r in range(16, R + 1, 16) if R % r == 0 and r * per_row <= ADAMW_VMEM_BUDGET)

    def body(*refs):
        ins, outs = refs[:4 * nt], refs[4 * nt:]
        for t in range(nt):
            p_ref, w_ref, m_ref, v_ref = ins[4 * t:4 * t + 4]
            g_ref, d_ref, nm_ref, nv_ref = outs[4 * t:4 * t + 4]
            g = p_ref[0].astype(F32)
            for i in range(1, nsh):
                g += p_ref[i].astype(F32)
            g_ref[...] = g
            d_ref[...], nm_ref[...], nv_ref[...] = _adamw(w_ref[...], g, m_ref[...], v_ref[...])

    spec = pl.BlockSpec((tr, C), lambda i: (i, 0))
    out = jax.ShapeDtypeStruct((R, C), F32)
    return _call(
        body, hosted, name=name, grid=(R // tr,),
        in_specs=[pl.BlockSpec((nsh, tr, C), lambda i: (0, i, 0)), spec, spec, spec] * nt,
        out_specs=[spec] * (4 * nt), out_shape=[out] * (4 * nt),
        args=[a for tensor in tensors for a in tensor])


def adamw_small(packs, w, m, v, name):
    ndev, R, L = packs.shape

    def body(p_ref, w_ref, m_ref, v_ref, g_ref, d_ref, nm_ref, nv_ref):
        g = p_ref[0]
        for i in range(1, ndev):
            g += p_ref[i]
        g_ref[...] = g
        d_ref[...], nm_ref[...], nv_ref[...] = _adamw(w_ref[...], g, m_ref[...], v_ref[...])

    out = jax.ShapeDtypeStruct((R, L), F32)
    return pl.pallas_call(body, name=name, out_shape=[out] * 4,
                          compiler_params=pltpu.CompilerParams(vmem_limit_bytes=VMEM_LIMIT_V7X))(packs, w, m, v)


BIG = ("ffn1_w1", "ffn1_w3", "ffn1_w2", "w_in", "w_out", "ffn2_w1", "ffn2_w3", "ffn2_w2")
SMALL = ("ffn1_norm", "mix_norm", "pool_w", "pool_scale", "ret_norm", "ffn2_norm", "final_norm")
WEIGHTS = ("ffn1_norm", "ffn1_w1", "ffn1_w3", "ffn1_w2", "mix_norm", "w_in", "pool_w", "pool_scale", "ret_norm", "w_out",
           "ffn2_norm", "ffn2_w1", "ffn2_w3", "ffn2_w2", "final_norm")


def _pack(parts):
    return jnp.concatenate([parts[k].reshape(-1, 128) for k in SMALL], axis=0)


def _unpack(pack, like):
    out, row = {}, 0
    for k in SMALL:
        rows = like[k].size // 128
        out[k] = pack[row:row + rows].reshape(like[k].shape)
        row += rows
    return out


def kernel(x, ffn1_norm, ffn1_w1, ffn1_w3, ffn1_w2, mix_norm, w_in, pool_w, pool_scale, ret_norm, w_out, ffn2_norm, ffn2_w1, ffn2_w3, ffn2_w2, final_norm, loss_target, m_ffn1_norm, m_ffn1_w1, m_ffn1_w3, m_ffn1_w2, m_mix_norm, m_w_in, m_pool_w, m_pool_scale, m_ret_norm, m_w_out, m_ffn2_norm, m_ffn2_w1, m_ffn2_w3, m_ffn2_w2, m_final_norm, v_ffn1_norm, v_ffn1_w1, v_ffn1_w3, v_ffn1_w2, v_mix_norm, v_w_in, v_pool_w, v_pool_scale, v_ret_norm, v_w_out, v_ffn2_norm, v_ffn2_w1, v_ffn2_w3, v_ffn2_w2, v_final_norm):
    w = dict(ffn1_norm=ffn1_norm, ffn1_w1=ffn1_w1, ffn1_w3=ffn1_w3, ffn1_w2=ffn1_w2, mix_norm=mix_norm, w_in=w_in, pool_w=pool_w,
             pool_scale=pool_scale, ret_norm=ret_norm, w_out=w_out, ffn2_norm=ffn2_norm, ffn2_w1=ffn2_w1, ffn2_w3=ffn2_w3,
             ffn2_w2=ffn2_w2, final_norm=final_norm)
    m = dict(ffn1_norm=m_ffn1_norm, ffn1_w1=m_ffn1_w1, ffn1_w3=m_ffn1_w3, ffn1_w2=m_ffn1_w2, mix_norm=m_mix_norm, w_in=m_w_in,
             pool_w=m_pool_w, pool_scale=m_pool_scale, ret_norm=m_ret_norm, w_out=m_w_out, ffn2_norm=m_ffn2_norm, ffn2_w1=m_ffn2_w1,
             ffn2_w3=m_ffn2_w3, ffn2_w2=m_ffn2_w2, final_norm=m_final_norm)
    v = dict(ffn1_norm=v_ffn1_norm, ffn1_w1=v_ffn1_w1, ffn1_w3=v_ffn1_w3, ffn1_w2=v_ffn1_w2, mix_norm=v_mix_norm, w_in=v_w_in,
             pool_w=v_pool_w, pool_scale=v_pool_scale, ret_norm=v_ret_norm, w_out=v_w_out, ffn2_norm=v_ffn2_norm, ffn2_w1=v_ffn2_w1,
             ffn2_w3=v_ffn2_w3, ffn2_w2=v_ffn2_w2, final_norm=v_final_norm)
    xs, target = x[0], loss_target[0]
    T = xs.shape[0]
    tables = _ret_tables(T)
    place = jnp.stack([lax.axis_index("c"), 2 * lax.axis_index("x") + lax.axis_index("y")]).astype(jnp.int32)
    sh = {k: w[k][0].astype(BF16) for k in BIG}
    gather = lambda *names: [ChipExchange([sh[k] for k in names], False)]
    wg, grad, delta, new_m, new_v = {}, {}, {}, {}, {}

    def update(names, pieces, name, hosted=()):
        outs, extras = adamw_sharded([(p, w[k][0], m[k][0], v[k][0]) for k, p in zip(names, pieces)], name, hosted)
        for t, k in enumerate(names):
            grad[k], delta[k], new_m[k], new_v[k] = [o[None] for o in outs[4 * t:4 * t + 4]]
        return extras

    def reduce_in_chip(name, partial, recv):
        return prereduce(partial, recv, place, "prereduce_" + name)

    scatter = lambda *reduced: ChipExchange([r[0] for r in reduced], True, [r[1] for r in reduced])

    (wg["ffn1_w1"], wg["ffn1_w3"]), = exchange(gather("ffn1_w1", "ffn1_w3"), "gather_ffn1")
    (n1, a1, b1, s1), ((wg["ffn1_w2"], wg["w_in"]),) = ffn_up(
        xs, ffn1_norm, wg["ffn1_w1"], wg["ffn1_w3"], "ffn1_up", gather("ffn1_w2", "w_in"))
    (h1,), ((wg["w_out"],),) = ffn_down(s1, wg["ffn1_w2"], xs, "ffn1_down", gather("w_out"))
    (u, proj), ((wg["ffn2_w1"],),) = mix_in(h1, mix_norm, wg["w_in"], "mix_in", gather("ffn2_w1"))
    (pa,), _ = pool_fwd(proj, pool_w[0], pool_scale, "pool_fwd")
    (rb, o_pre, r_prev), ((wg["ffn2_w3"],),) = ret_fwd(proj, ret_norm, tables, "ret_fwd", gather("ffn2_w3"))
    (h2,), _ = mix_out(pa, rb, wg["w_out"], h1, "mix_out")
    (n2, a2, b2, s2), ((wg["ffn2_w2"],),) = ffn_up(
        h2, ffn2_norm, wg["ffn2_w1"], wg["ffn2_w3"], "ffn2_up", gather("ffn2_w2"))
    (h3,), _ = ffn_down(s2, wg["ffn2_w2"], h2, "ffn2_down")
    (dh3, loss, d_final), _ = final_loss(h3, final_norm[None], target, "final_loss")
    loss = lax.psum(loss[0, 0], ("x", "y", "c"))

    (da2, db2), _ = ffn_bwd_act(dh3, wg["ffn2_w2"], a2, b2, "ffn2_bwd_act")
    (g_f2w2,), _ = ffn_dw2(s2, dh3, "ffn2_dw2")
    (g_f2w1, g_f2w3), ((r_f2w2,),) = ffn_dw13(n2, da2, db2, "ffn2_dw13", [SiblingExchange([g_f2w2])])
    p_f2w2 = reduce_in_chip("ffn2_w2", g_f2w2, r_f2w2)
    (dh2, d_ffn2), ((q_f2w2,), (r_f2w1, r_f2w3)) = ffn_bwd_in(
        da2, db2, wg["ffn2_w1"], wg["ffn2_w3"], h2, ffn2_norm, dh3, "ffn2_bwd_in",
        [scatter(p_f2w2), SiblingExchange([g_f2w1, g_f2w3])])
    p_f2w1 = reduce_in_chip("ffn2_w1", g_f2w1, r_f2w1)
    p_f2w3 = reduce_in_chip("ffn2_w3", g_f2w3, r_f2w3)
    (dpa, drb, g_wout), _ = mix_out_bwd(dh2, wg["w_out"], pa, rb, "mix_out_bwd")
    (dpool, d_pool_w, d_pool_scale), _ = pool_bwd(proj, dpa, pool_w[0], pool_scale, "pool_bwd")
    (dqkvg, d_ret_norm), ((q_f2w1, q_f2w3), (r_wout,)) = ret_bwd(
        proj, drb, o_pre, r_prev, ret_norm, tables, "ret_bwd", [scatter(p_f2w1, p_f2w3), SiblingExchange([g_wout])])
    p_wout = reduce_in_chip("w_out", g_wout, r_wout)
    d = jnp.concatenate([dpool, dqkvg[0], dqkvg[1], dqkvg[2], dqkvg[3]], axis=1)
    (g_win,), ((q_wout,),) = mix_dwin(u, d, N_CHIPS, "mix_dwin", [scatter(p_wout)])
    (dh1, d_mix), ((r_win,),) = mix_in_bwd(d, wg["w_in"], h1, mix_norm, dh2, "mix_in_bwd", [SiblingExchange([g_win])])
    p_win = reduce_in_chip("w_in", g_win, r_win)
    (da1, db1), ((q_win,),) = ffn_bwd_act(dh1, wg["ffn1_w2"], a1, b1, "ffn1_bwd_act", [scatter(p_win)])
    (g_f1w1, g_f1w3), _ = ffn_dw13(n1, da1, db1, "ffn1_dw13")
    (g_f1w2,), ((r_f1w1, r_f1w3),) = ffn_dw2(s1, dh1, "ffn1_dw2", [SiblingExchange([g_f1w1, g_f1w3])])
    p_f1w1 = reduce_in_chip("ffn1_w1", g_f1w1, r_f1w1)
    p_f1w3 = reduce_in_chip("ffn1_w3", g_f1w3, r_f1w3)
    (dx, d_ffn1), ((q_f1w1, q_f1w3), (r_f1w2,)) = ffn_bwd_in(
        da1, db1, wg["ffn1_w1"], wg["ffn1_w3"], xs, ffn1_norm, dh1, "ffn1_bwd_in",
        [scatter(p_f1w1, p_f1w3), SiblingExchange([g_f1w2])])
    p_f1w2 = reduce_in_chip("ffn1_w2", g_f1w2, r_f1w2)

    small = {"ffn1_norm": d_ffn1, "mix_norm": d_mix, "pool_w": d_pool_w, "pool_scale": d_pool_scale,
             "ret_norm": d_ret_norm, "ffn2_norm": d_ffn2, "final_norm": d_final}
    (q_f1w2,), (packs,) = update(["ffn2_w1", "ffn2_w3", "ffn1_w1", "ffn1_w3"], [q_f2w1, q_f2w3, q_f1w1, q_f1w3], "adamw_w13",
                                 [scatter(p_f1w2), AllExchange(_pack(small))])
    update(["ffn2_w2", "ffn1_w2"], [q_f2w2, q_f1w2], "adamw_w2")
    update(["w_in"], [q_win], "adamw_w_in")
    update(["w_out"], [q_wout], "adamw_w_out")
    outs = adamw_small(packs, _pack(w), _pack(m), _pack(v), "adamw_small")
    for res, pack in zip((grad, delta, new_m, new_v), outs):
        res.update(_unpack(pack, w))

    return (loss, dx[None], *[grad[k] for k in WEIGHTS], *[delta[k] for k in WEIGHTS],
            *[new_m[k] for k in WEIGHTS], *[new_v[k] for k in WEIGHTS])
```

```python
import math

import jax
import jax.numpy as jnp
from jax import lax
from jax.experimental import pallas as pl
from jax.experimental.pallas import tpu as pltpu

F32 = jnp.float32
BF16 = jnp.bfloat16

EPS = 1e-6
N_CHIPS = 4
N_GROUPS = 4
HEAD_DIM = 128
RET_CHUNK = 128
ROPE_BASE = 10000.0
ADAM_LR, ADAM_B1, ADAM_B2, ADAM_EPS, ADAM_WD, ADAM_STEP = 0.001, 0.9, 0.999, 1e-08, 0.01, 10
VMEM_LIMIT_V7X = 56 * 1024 * 1024
ADAMW_VMEM_BUDGET = 32 * 1024 * 1024
MESH = pl.DeviceIdType.MESH
ANY = pl.BlockSpec(memory_space=pl.ANY)


def _dot(a, b):
    return jnp.dot(a, b, preferred_element_type=F32)


def _dot_nt(a, b):
    return lax.dot_general(a, b, (((1,), (1,)), ((), ())), preferred_element_type=F32)


def _dot_tn(a, b):
    return lax.dot_general(a, b, (((0,), (0,)), ((), ())), preferred_element_type=F32)


def _rstd(h):
    return lax.rsqrt(jnp.mean(h * h, axis=-1, keepdims=True) + EPS)


def _rmsnorm_bwd(dn, h, gain):
    r = _rstd(h)
    nh = h * r
    dnh = dn * gain
    dh = r * (dnh - nh * jnp.mean(dnh * nh, axis=-1, keepdims=True))
    return dh, dn * nh


def _silu_parts(a):
    sig = jax.nn.sigmoid(a)
    silu = a * sig
    return silu, sig + silu * (1.0 - sig)


def _mesh_pos():
    return lax.axis_index("x"), lax.axis_index("y"), lax.axis_index("c")


class ChipExchange:
    def __init__(self, srcs, scatter, placed=()):
        n = len(srcs)
        self.inputs, self.scatter, self.n, self.reach = list(srcs) + list(placed), scatter, n, REACH_CHIPS
        self.aliases = {n + t: t for t in range(n)} if scatter else {}
        self.half_rows = [s.shape[1] if scatter else s.shape[0] // 2 for s in srcs]
        self.out_shape = [jax.ShapeDtypeStruct((N_CHIPS, 2 * rh, s.shape[-1]), s.dtype) for s, rh in zip(srcs, self.half_rows)]
        if scatter:
            self.out_shape += [jax.ShapeDtypeStruct((2, rh // 2, s.shape[-1]), s.dtype) for s, rh in zip(srcs, self.half_rows)]
        dma = pltpu.SemaphoreType.DMA
        self.sems = [dma((4 * n,)), dma((4 * n,)), dma((2 * n,)), dma((2 * n,)), dma((4 * n,)), dma((4 * n,))]

    def _copies(self, src, out, sems):
        hop1_send, hop1_recv, hop2_send, hop2_recv, d2d_send, d2d_recv = sems
        x, y, c = _mesh_pos()
        me, dg = 2 * x + y, 2 * (1 - x) + (1 - y)
        sibling = (x, y, 1 - c)
        n = self.n
        mine, theirs = c, 1 - c

        def nb(a):
            nx, ny = x ^ (1 - a), y ^ a
            return 2 * nx + ny, (nx, ny, c)

        def remote(s, d, send, recv, k, to):
            return pltpu.make_async_remote_copy(src_ref=s, dst_ref=d, send_sem=send.at[k], recv_sem=recv.at[k],
                                                device_id=to, device_id_type=MESH)

        class Copies:
            def slot(_, t, chip, half):
                rh = self.half_rows[t]
                return out[t].at[chip, pl.ds(half * rh, rh), :]

            def quarter(_, t, chip, q):
                qh = self.half_rows[t] // 2
                return out[t].at[chip, pl.ds(mine * 2 * qh + q * qh, qh), :]

            def own_shard(k, t):
                return remote(src[t], out[t].at[me], d2d_send, d2d_recv, 4 * t + 3, sibling)

            def hop1(k, t, a, transit=False):
                rh = self.half_rows[t]
                chip, to = nb(a)
                if transit:
                    piece = src[t].at[dg, pl.ds(a * (rh // 2), rh // 2), :]
                    return remote(piece, out[n + t].at[a], hop1_send, hop1_recv, 4 * t + 2 + a, to)
                piece = src[t].at[chip] if self.scatter else src[t].at[pl.ds(mine * rh, rh), :]
                return remote(piece, k.slot(t, me, mine), hop1_send, hop1_recv, 4 * t + a, to)

            def landed1(k, t, a, transit=False):
                here = out[n + t].at[a] if transit else k.slot(t, nb(a)[0], mine)
                return remote(here, here, hop1_send, hop1_recv, 4 * t + (2 if transit else 0) + a, sibling)

            def hop2(k, t, q):
                origin, to = nb(q)[0], nb(1 - q)[1]
                piece = out[n + t].at[q] if self.scatter else k.quarter(t, origin, q)
                return remote(piece, k.quarter(t, origin, q), hop2_send, hop2_recv, 2 * t + q, to)

            def landed2(k, t, q):
                here = k.quarter(t, dg, q)
                return remote(here, here, hop2_send, hop2_recv, 2 * t + q, sibling)

            def d2d(k, t, p, chip, own=False, arriving=False):
                if arriving:
                    there = k.slot(t, chip, theirs)
                    return remote(there, there, d2d_send, d2d_recv, 4 * t + p, sibling)
                piece = src[t].at[me] if own else k.slot(t, chip, mine)
                return remote(piece, k.slot(t, chip, mine), d2d_send, d2d_recv, 4 * t + p, sibling)

        return Copies(), nb, me, dg, c

    def start(self, src, out, sems):
        k, nb, me, dg, c = self._copies(src, out, sems)
        for t in range(self.n):
            for first in range(2):
                a = first ^ c
                k.hop1(t, a).start()
                if self.scatter:
                    k.hop1(t, a, transit=True).start()
            if self.scatter:
                k.d2d(t, 3, me, own=True).start()
            else:
                k.own_shard(t).start()

    def mid(self, src, out, sems):
        k, nb, me, dg, c = self._copies(src, out, sems)
        for t in range(self.n):
            for first in range(2):
                a = first ^ c
                if self.scatter:
                    k.landed1(t, a, transit=True).wait_recv()
                    k.hop2(t, a).start()
                k.landed1(t, a).wait_recv()
                if not self.scatter:
                    k.hop2(t, a).start()
                k.d2d(t, a, nb(a)[0]).start()

    def finish(self, src, out, sems):
        k, nb, me, dg, c = self._copies(src, out, sems)
        for t in range(self.n):
            for q in range(2):
                k.landed2(t, q).wait_recv()
            k.d2d(t, 2, dg).start()
        for t in range(self.n):
            for a in range(2):
                k.d2d(t, a, nb(a)[0], arriving=True).wait_recv()
            k.d2d(t, 2, dg, arriving=True).wait_recv()
            if self.scatter:
                k.d2d(t, 3, me, arriving=True).wait_recv()
        for t in range(self.n):
            for a in range(2):
                k.hop1(t, a).wait_send()
                if self.scatter:
                    k.hop1(t, a, transit=True).wait_send()
                k.hop2(t, a).wait_send()
                k.d2d(t, a, nb(a)[0]).wait_send()
            k.d2d(t, 2, dg).wait_send()
            if self.scatter:
                k.d2d(t, 3, me, own=True).wait_send()
            else:
                k.own_shard(t).wait()


class SiblingExchange:
    def __init__(self, grads):
        self.inputs, self.n, self.aliases, self.reach = list(grads), len(grads), {}, REACH_SIBLING
        self.half_rows = [g.shape[1] // 2 for g in grads]
        self.out_shape = [jax.ShapeDtypeStruct((g.shape[0], rh, g.shape[2]), g.dtype) for g, rh in zip(grads, self.half_rows)]
        self.sems = [pltpu.SemaphoreType.DMA((self.n,)), pltpu.SemaphoreType.DMA((self.n,))]

    def _plan(self, src, out, sems):
        x, y, c = _mesh_pos()
        return [pltpu.make_async_remote_copy(
            src_ref=src[t].at[:, pl.ds((1 - c) * self.half_rows[t], self.half_rows[t]), :], dst_ref=out[t],
            send_sem=sems[0].at[t], recv_sem=sems[1].at[t], device_id=(x, y, 1 - c), device_id_type=MESH) for t in range(self.n)]

    def start(self, src, out, sems):
        for cp in self._plan(src, out, sems):
            cp.start()

    def mid(self, src, out, sems):
        pass

    def finish(self, src, out, sems):
        for cp in self._plan(src, out, sems):
            cp.wait()


REACH_SIBLING, REACH_CHIPS, REACH_ALL = 0, 1, 2


def _entry_barrier(reach):
    x, y, c = _mesh_pos()
    peers = [(x, y, 1 - c)]
    if reach == REACH_CHIPS:
        peers += [(1 - x, y, c), (x, 1 - y, c)]
    elif reach == REACH_ALL:
        peers = [(x ^ dx, y ^ dy, c ^ dc) for dx in (0, 1) for dy in (0, 1) for dc in (0, 1)][1:]
    barrier = pltpu.get_barrier_semaphore()
    for peer in peers:
        pl.semaphore_signal(barrier, inc=1, device_id=peer, device_id_type=MESH)
    pl.semaphore_wait(barrier, len(peers))


def _call(body, hosted=(), *, name, in_specs, out_specs, out_shape, args, grid=(), scratch_shapes=()):
    n_in, n_out, n_scr = len(in_specs), len(out_specs), len(scratch_shapes)
    total = math.prod(grid)
    mid_step = max(0, (5 * total) // 8 - 1)

    def full(*refs):
        pos = [0]

        def take(k):
            pos[0] += k
            return refs[pos[0] - k:pos[0]]

        ins, h_in = take(n_in), [take(len(h.inputs)) for h in hosted]
        outs, h_out = take(n_out), [take(len(h.out_shape)) for h in hosted]
        scr, h_sem = take(n_scr), [take(len(h.sems)) for h in hosted]
        step = 0
        for axis, size in enumerate(grid):
            step = step * size + pl.program_id(axis)

        def phase(at, method):
            if not hosted:
                return

            def run():
                if method == "start":
                    _entry_barrier(reach)
                for h, s, o, m in zip(hosted, h_in, h_out, h_sem):
                    getattr(h, method)(s, o, m)

            if total == 1:
                run()
            else:
                pl.when(step == at)(run)

        phase(0, "start")
        body(*ins, *outs, *scr)
        phase(mid_step, "mid")
        phase(total - 1, "finish")

    aliases, i0, o0 = {}, n_in, n_out
    for h in hosted:
        aliases.update({i0 + i: o0 + o for i, o in h.aliases.items()})
        i0, o0 = i0 + len(h.inputs), o0 + len(h.out_shape)
    reach = max((h.reach for h in hosted), default=None)
    params = dict(vmem_limit_bytes=VMEM_LIMIT_V7X)
    if hosted:
        params["collective_id"] = reach
    results = pl.pallas_call(
        full, name=name, grid=grid,
        in_specs=list(in_specs) + [ANY] * (i0 - n_in),
        out_specs=list(out_specs) + [ANY] * (o0 - n_out),
        out_shape=list(out_shape) + [s for h in hosted for s in h.out_shape],
        scratch_shapes=list(scratch_shapes) + [s for h in hosted for s in h.sems],
        input_output_aliases=aliases,
        compiler_params=pltpu.CompilerParams(**params),
    )(*args, *[s for h in hosted for s in h.inputs])
    outs, extras, pos = list(results[:n_out]), [], n_out
    for h in hosted:
        extras.append(list(results[pos:pos + h.n]))
        pos += len(h.out_shape)
    return outs, extras


def exchange(hosted, name):
    return _call(lambda: None, hosted, name=name, in_specs=[], out_specs=[], out_shape=[], args=[])[1]


class AllExchange:
    def __init__(self, pack):
        self.inputs, self.n, self.aliases, self.reach = [pack], 1, {}, REACH_ALL
        self.out_shape = [jax.ShapeDtypeStruct((2 * N_CHIPS,) + pack.shape, pack.dtype)]
        self.sems = [pltpu.SemaphoreType.DMA, pltpu.SemaphoreType.DMA((7,)), pltpu.SemaphoreType.DMA((7,))]

    def _copies(self, src, out, sems):
        local_sem, send_sem, recv_sem = sems
        x, y, c = _mesh_pos()
        flips = [(dx, dy, dc) for dx in (0, 1) for dy in (0, 1) for dc in (0, 1)][1:]
        peers = [(x ^ dx, y ^ dy, c ^ dc) for dx, dy, dc in flips]
        remote = lambda s, d, k: pltpu.make_async_remote_copy(
            src_ref=s, dst_ref=d, send_sem=send_sem.at[k], recv_sem=recv_sem.at[k], device_id=peers[k], device_id_type=MESH)
        sends = [remote(src[0], out[0].at[4 * x + 2 * y + c], k) for k in range(7)]
        landed = [remote(out[0].at[4 * px + 2 * py + pc], out[0].at[4 * px + 2 * py + pc], k) for k, (px, py, pc) in enumerate(peers)]
        return sends, landed, pltpu.make_async_copy(src[0], out[0].at[4 * x + 2 * y + c], local_sem)

    def start(self, src, out, sems):
        sends, _, local = self._copies(src, out, sems)
        for cp in sends:
            cp.start()
        local.start()

    def mid(self, src, out, sems):
        pass

    def finish(self, src, out, sems):
        sends, landed, local = self._copies(src, out, sems)
        for cp in landed:
            cp.wait_recv()
        for cp in sends:
            cp.wait_send()
        local.wait()


def ffn_up(h, gain, w1g, w3g, name, hosted=()):
    T, D = h.shape
    nsh, Fs, _ = w1g.shape
    tm = min(T, 1024)

    def body(h_ref, g_ref, w1_ref, w3_ref, n_ref, a_ref, b_ref, s_ref):
        @pl.when(pl.program_id(1) == 0)
        def _():
            hh = h_ref[...]
            n_ref[...] = (hh * _rstd(hh) * g_ref[...]).astype(BF16)

        n = n_ref[...]
        a = _dot_nt(n, w1_ref[0])
        b = _dot_nt(n, w3_ref[0])
        a_ref[0] = a.astype(BF16)
        b_ref[0] = b.astype(BF16)
        s_ref[0] = (a * jax.nn.sigmoid(a) * b).astype(BF16)

    act = jax.ShapeDtypeStruct((nsh, T, Fs), BF16)
    act_spec = pl.BlockSpec((1, tm, Fs), lambda i, j: (j, i, 0))
    w_spec = pl.BlockSpec((1, Fs, D), lambda i, j: (j, 0, 0))
    return _call(
        body, hosted, name=name, grid=(T // tm, nsh),
        in_specs=[pl.BlockSpec((tm, D), lambda i, j: (i, 0)), pl.BlockSpec((1, D), lambda i, j: (0, 0)), w_spec, w_spec],
        out_specs=[pl.BlockSpec((tm, D), lambda i, j: (i, 0)), act_spec, act_spec, act_spec],
        out_shape=[jax.ShapeDtypeStruct((T, D), BF16), act, act, act],
        args=[h, gain, w1g, w3g])


def ffn_down(s, w2g, h, name, hosted=()):
    nsh, T, Fs = s.shape
    D = h.shape[1]
    tm = min(T, 512)

    def body(s_ref, w2_ref, h_ref, o_ref):
        f = _dot(s_ref[0], w2_ref[0])
        for j in range(1, nsh):
            f += _dot(s_ref[j], w2_ref[j])
        o_ref[...] = h_ref[...] + 0.5 * f

    return _call(
        body, hosted, name=name, grid=(T // tm,),
        in_specs=[pl.BlockSpec((nsh, tm, Fs), lambda i: (0, i, 0)), pl.BlockSpec((nsh, Fs, D), lambda i: (0, 0, 0)),
                  pl.BlockSpec((tm, D), lambda i: (i, 0))],
        out_specs=[pl.BlockSpec((tm, D), lambda i: (i, 0))],
        out_shape=[jax.ShapeDtypeStruct((T, D), F32)],
        args=[s, w2g, h])


def ffn_bwd_act(dh, w2g, a, b, name, hosted=()):
    T, D = dh.shape
    nsh, Fs, _ = w2g.shape
    tm = min(T, 1024)

    def body(dh_ref, w2_ref, a_ref, b_ref, da_ref, db_ref):
        df = (0.5 * dh_ref[...]).astype(BF16)
        ds = _dot_nt(df, w2_ref[0])
        silu, dsilu = _silu_parts(a_ref[0].astype(F32))
        da_ref[0] = (ds * b_ref[0].astype(F32) * dsilu).astype(BF16)
        db_ref[0] = (ds * silu).astype(BF16)

    act = jax.ShapeDtypeStruct((nsh, T, Fs), BF16)
    act_spec = pl.BlockSpec((1, tm, Fs), lambda j, i: (j, i, 0))
    return _call(
        body, hosted, name=name, grid=(nsh, T // tm),
        in_specs=[pl.BlockSpec((tm, D), lambda j, i: (i, 0)), pl.BlockSpec((1, Fs, D), lambda j, i: (j, 0, 0)), act_spec, act_spec],
        out_specs=[act_spec, act_spec],
        out_shape=[act, act],
        args=[dh, w2g, a, b])


def ffn_dw2(s, dh, name, hosted=()):
    nsh, T, Fs = s.shape
    D = dh.shape[1]
    tk = min(T, 512)
    nk = T // tk

    def body(s_ref, dh_ref, o_ref, acc):
        k = pl.program_id(1)

        @pl.when(k == 0)
        def _():
            acc[...] = jnp.zeros_like(acc)

        acc[...] += _dot_tn(s_ref[0], (0.5 * dh_ref[...]).astype(BF16))

        @pl.when(k == nk - 1)
        def _():
            o_ref[0] = acc[...].astype(BF16)

    return _call(
        body, hosted, name=name, grid=(nsh, nk),
        in_specs=[pl.BlockSpec((1, tk, Fs), lambda j, k: (j, k, 0)), pl.BlockSpec((tk, D), lambda j, k: (k, 0))],
        out_specs=[pl.BlockSpec((1, Fs, D), lambda j, k: (j, 0, 0))],
        out_shape=[jax.ShapeDtypeStruct((nsh, Fs, D), BF16)],
        scratch_shapes=[pltpu.VMEM((Fs, D), F32)],
        args=[s, dh])


def ffn_dw13(n, da, db, name, hosted=()):
    T, D = n.shape
    nsh, _, Fs = da.shape
    tk = min(T, 512)
    nk = T // tk

    def body(n_ref, da_ref, db_ref, o1_ref, o3_ref, acc1, acc3):
        k = pl.program_id(1)

        @pl.when(k == 0)
        def _():
            acc1[...] = jnp.zeros_like(acc1)
            acc3[...] = jnp.zeros_like(acc3)

        nn = n_ref[...]
        acc1[...] += _dot_tn(da_ref[0], nn)
        acc3[...] += _dot_tn(db_ref[0], nn)

        @pl.when(k == nk - 1)
        def _():
            o1_ref[0] = acc1[...].astype(BF16)
            o3_ref[0] = acc3[...].astype(BF16)

    act_spec = pl.BlockSpec((1, tk, Fs), lambda j, k: (j, k, 0))
    out = jax.ShapeDtypeStruct((nsh, Fs, D), BF16)
    out_spec = pl.BlockSpec((1, Fs, D), lambda j, k: (j, 0, 0))
    return _call(
        body, hosted, name=name, grid=(nsh, nk),
        in_specs=[pl.BlockSpec((tk, D), lambda j, k: (k, 0)), act_spec, act_spec],
        out_specs=[out_spec, out_spec],
        out_shape=[out, out],
        scratch_shapes=[pltpu.VMEM((Fs, D), F32), pltpu.VMEM((Fs, D), F32)],
        args=[n, da, db])


def ffn_bwd_in(da, db, w1g, w3g, h, gain, dh, name, hosted=()):
    nsh, T, Fs = da.shape
    D = h.shape[1]
    tm = min(T, 256)

    def body(da_ref, db_ref, w1_ref, w3_ref, h_ref, g_ref, dh_ref, o_ref, dg_ref):
        dn = _dot(da_ref[0], w1_ref[0]) + _dot(db_ref[0], w3_ref[0])
        for j in range(1, nsh):
            dn += _dot(da_ref[j], w1_ref[j]) + _dot(db_ref[j], w3_ref[j])
        dhn, dg = _rmsnorm_bwd(dn, h_ref[...], g_ref[...])
        o_ref[...] = dh_ref[...] + dhn

        @pl.when(pl.program_id(0) == 0)
        def _():
            dg_ref[...] = jnp.zeros_like(dg_ref)

        dg_ref[...] += jnp.sum(dg, axis=0, keepdims=True)

    act_spec = pl.BlockSpec((nsh, tm, Fs), lambda i: (0, i, 0))
    w_spec = pl.BlockSpec((nsh, Fs, D), lambda i: (0, 0, 0))
    row_spec = pl.BlockSpec((tm, D), lambda i: (i, 0))
    vec_spec = pl.BlockSpec((1, D), lambda i: (0, 0))
    return _call(
        body, hosted, name=name, grid=(T // tm,),
        in_specs=[act_spec, act_spec, w_spec, w_spec, row_spec, vec_spec, row_spec],
        out_specs=[row_spec, vec_spec],
        out_shape=[jax.ShapeDtypeStruct((T, D), F32), jax.ShapeDtypeStruct((1, D), F32)],
        args=[da, db, w1g, w3g, h, gain, dh])


def mix_in(h, gain, wing, name, hosted=()):
    T, D = h.shape
    nsh, _, Cs = wing.shape
    tm = min(T, 512)

    def body(h_ref, g_ref, w_ref, u_ref, p_ref):
        hh = h_ref[...]
        u = (hh * _rstd(hh) * g_ref[...]).astype(BF16)
        u_ref[...] = u
        for j in range(nsh):
            p_ref[:, j * Cs:(j + 1) * Cs] = _dot(u, w_ref[j])

    return _call(
        body, hosted, name=name, grid=(T // tm,),
        in_specs=[pl.BlockSpec((tm, D), lambda i: (i, 0)), pl.BlockSpec((1, D), lambda i: (0, 0)),
                  pl.BlockSpec((nsh, D, Cs), lambda i: (0, 0, 0))],
        out_specs=[pl.BlockSpec((tm, D), lambda i: (i, 0)), pl.BlockSpec((tm, nsh * Cs), lambda i: (i, 0))],
        out_shape=[jax.ShapeDtypeStruct((T, D), BF16), jax.ShapeDtypeStruct((T, nsh * Cs), F32)],
        args=[h, gain, wing])


def mix_out(a, b, woutg, h, name, hosted=()):
    T, W = a.shape
    D = h.shape[1]
    wout = woutg.reshape(2, W, D)
    tm = min(T, 512)

    def body(a_ref, b_ref, w_ref, h_ref, o_ref):
        o_ref[...] = h_ref[...] + _dot(a_ref[...], w_ref[0]) + _dot(b_ref[...], w_ref[1])

    return _call(
        body, hosted, name=name, grid=(T // tm,),
        in_specs=[pl.BlockSpec((tm, W), lambda i: (i, 0)), pl.BlockSpec((tm, W), lambda i: (i, 0)),
                  pl.BlockSpec((2, W, D), lambda i: (0, 0, 0)), pl.BlockSpec((tm, D), lambda i: (i, 0))],
        out_specs=[pl.BlockSpec((tm, D), lambda i: (i, 0))],
        out_shape=[jax.ShapeDtypeStruct((T, D), F32)],
        args=[a, b, wout, h])


def mix_out_bwd(dh, woutg, a, b, name, hosted=()):
    T, D = dh.shape
    W = a.shape[1]
    nsh, Rs, _ = woutg.shape
    wout = woutg.reshape(2, W, D)
    tk = min(T, 512)
    nk = T // tk

    def body(dh_ref, w_ref, a_ref, b_ref, da_ref, db_ref, dw_ref, acc):
        k = pl.program_id(0)

        @pl.when(k == 0)
        def _():
            acc[...] = jnp.zeros_like(acc)

        dhb = dh_ref[...].astype(BF16)
        da_ref[...] = _dot_nt(dhb, w_ref[0])
        db_ref[...] = _dot_nt(dhb, w_ref[1])
        acc[0:W, :] += _dot_tn(a_ref[...], dhb)
        acc[W:2 * W, :] += _dot_tn(b_ref[...], dhb)

        @pl.when(k == nk - 1)
        def _():
            for j in range(nsh):
                dw_ref[j] = acc[j * Rs:(j + 1) * Rs, :].astype(BF16)

    return _call(
        body, hosted, name=name, grid=(nk,),
        in_specs=[pl.BlockSpec((tk, D), lambda k: (k, 0)), pl.BlockSpec((2, W, D), lambda k: (0, 0, 0)),
                  pl.BlockSpec((tk, W), lambda k: (k, 0)), pl.BlockSpec((tk, W), lambda k: (k, 0))],
        out_specs=[pl.BlockSpec((tk, W), lambda k: (k, 0)), pl.BlockSpec((tk, W), lambda k: (k, 0)),
                   pl.BlockSpec((nsh, Rs, D), lambda k: (0, 0, 0))],
        out_shape=[jax.ShapeDtypeStruct((T, W), F32), jax.ShapeDtypeStruct((T, W), F32),
                   jax.ShapeDtypeStruct((nsh, Rs, D), BF16)],
        scratch_shapes=[pltpu.VMEM((2 * W, D), F32)],
        args=[dh, wout, a, b])


def mix_dwin(u, d, nsh, name, hosted=()):
    T, D = u.shape
    Cs = d.shape[1] // nsh
    tk = min(T, 512)
    nk = T // tk

    def body(u_ref, d_ref, o_ref, acc):
        k = pl.program_id(1)

        @pl.when(k == 0)
        def _():
            acc[...] = jnp.zeros_like(acc)

        acc[...] += _dot_tn(u_ref[...], d_ref[...])

        @pl.when(k == nk - 1)
        def _():
            o_ref[0] = acc[...].astype(BF16)

    return _call(
        body, hosted, name=name, grid=(nsh, nk),
        in_specs=[pl.BlockSpec((tk, D), lambda j, k: (k, 0)), pl.BlockSpec((tk, Cs), lambda j, k: (k, j))],
        out_specs=[pl.BlockSpec((1, D, Cs), lambda j, k: (j, 0, 0))],
        out_shape=[jax.ShapeDtypeStruct((nsh, D, Cs), BF16)],
        scratch_shapes=[pltpu.VMEM((D, Cs), F32)],
        args=[u, d])


def mix_in_bwd(d, wing, h, gain, dh, name, hosted=()):
    T, D = h.shape
    nsh, _, Cs = wing.shape
    tm = min(T, 512)

    def body(d_ref, w_ref, h_ref, g_ref, dh_ref, o_ref, dg_ref):
        du = _dot_nt(d_ref[:, 0:Cs], w_ref[0])
        for j in range(1, nsh):
            du += _dot_nt(d_ref[:, j * Cs:(j + 1) * Cs], w_ref[j])
        dhn, dg = _rmsnorm_bwd(du, h_ref[...], g_ref[...])
        o_ref[...] = dh_ref[...] + dhn

        @pl.when(pl.program_id(0) == 0)
        def _():
            dg_ref[...] = jnp.zeros_like(dg_ref)

        dg_ref[...] += jnp.sum(dg, axis=0, keepdims=True)

    row_spec = pl.BlockSpec((tm, D), lambda i: (i, 0))
    vec_spec = pl.BlockSpec((1, D), lambda i: (0, 0))
    return _call(
        body, hosted, name=name, grid=(T // tm,),
        in_specs=[pl.BlockSpec((tm, nsh * Cs), lambda i: (i, 0)), pl.BlockSpec((nsh, D, Cs), lambda i: (0, 0, 0)),
                  row_spec, vec_spec, row_spec],
        out_specs=[row_spec, vec_spec],
        out_shape=[jax.ShapeDtypeStruct((T, D), F32), jax.ShapeDtypeStruct((1, D), F32)],
        args=[d, wing, h, gain, dh])


def _pool_window(x, group, T, trailing):
    rows = lax.broadcasted_iota(jnp.int32, x.shape, 0)

    def shifted(z, k):
        if trailing:
            return jnp.where(rows >= k, pltpu.roll(z, k, 0), 0.0)
        return jnp.where(rows < T - k, pltpu.roll(z, T - k, 0), 0.0)

    s2 = x + shifted(x, 1)
    s4 = s2 + shifted(s2, 2)
    s8 = s4 + shifted(s4, 4)
    s16 = s8 + shifted(s8, 8)
    return jnp.where(group == 0, s2, jnp.where(group == 1, s4, jnp.where(group == 2, s8, s16)))


def _pool_count(group, shape):
    rows = lax.broadcasted_iota(jnp.int32, shape, 0)
    w = jnp.where(group == 0, 2, jnp.where(group == 1, 4, jnp.where(group == 2, 8, 16)))
    return jnp.minimum(rows + 1, w).astype(F32)


def pool_fwd(proj, pool_w, pool_scale, name, hosted=()):
    T = proj.shape[0]
    Hd = HEAD_DIM

    def body(x_ref, w_ref, sc_ref, a_ref):
        g = pl.program_id(0)
        x = x_ref[...]
        pooled = _pool_window(x, g, T, True) / _pool_count(g, x.shape) - x
        a_ref[...] = (_dot(pooled.astype(BF16), w_ref[0].astype(BF16)) * sc_ref[...]).astype(BF16)

    return _call(
        body, hosted, name=name, grid=(N_GROUPS,),
        in_specs=[pl.BlockSpec((T, Hd), lambda g: (0, g)), pl.BlockSpec((1, Hd, Hd), lambda g: (g, 0, 0)),
                  pl.BlockSpec((1, Hd), lambda g: (0, g))],
        out_specs=[pl.BlockSpec((T, Hd), lambda g: (0, g))],
        out_shape=[jax.ShapeDtypeStruct((T, N_GROUPS * Hd), BF16)],
        args=[proj, pool_w, pool_scale])


def pool_bwd(proj, da, pool_w, pool_scale, name, hosted=()):
    T = proj.shape[0]
    Hd = HEAD_DIM

    def body(x_ref, da_ref, w_ref, sc_ref, dx_ref, dw_ref, dsc_ref):
        g = pl.program_id(0)
        x = x_ref[...]
        cnt = _pool_count(g, x.shape)
        pooled = (_pool_window(x, g, T, True) / cnt - x).astype(BF16)
        wb = w_ref[0].astype(BF16)
        dav = da_ref[...]
        dsc_ref[...] = jnp.sum(dav * _dot(pooled, wb), axis=0, keepdims=True)
        dout = (dav * sc_ref[...]).astype(BF16)
        dw_ref[0] = _dot_tn(pooled, dout)
        dpooled = _dot_nt(dout, wb)
        dx_ref[...] = (_pool_window(dpooled / cnt, g, T, False) - dpooled).astype(BF16)

    col_spec = pl.BlockSpec((T, Hd), lambda g: (0, g))
    return _call(
        body, hosted, name=name, grid=(N_GROUPS,),
        in_specs=[col_spec, col_spec, pl.BlockSpec((1, Hd, Hd), lambda g: (g, 0, 0)), pl.BlockSpec((1, Hd), lambda g: (0, g))],
        out_specs=[col_spec, pl.BlockSpec((1, Hd, Hd), lambda g: (g, 0, 0)), pl.BlockSpec((1, Hd), lambda g: (0, g))],
        out_shape=[jax.ShapeDtypeStruct((T, N_GROUPS * Hd), BF16), jax.ShapeDtypeStruct((N_GROUPS, Hd, Hd), F32),
                   jax.ShapeDtypeStruct((1, N_GROUPS * Hd), F32)],
        args=[proj, da, pool_w, pool_scale])


def _ret_tables(T):
    Hd, C = HEAD_DIM, RET_CHUNK
    inv_freq = 1.0 / (ROPE_BASE ** (jnp.arange(0, Hd, 2, dtype=F32) / Hd))
    ang = jnp.arange(T, dtype=F32)[:, None] * inv_freq[None, :]
    cos, sin = jnp.cos(ang), jnp.sin(ang)
    cos2 = jnp.concatenate([cos, cos], axis=-1)
    sin2 = jnp.concatenate([-sin, sin], axis=-1)
    log_gamma = jnp.log1p(-jnp.exp2(-5.0 - jnp.arange(N_GROUPS, dtype=F32)))
    pos = jnp.arange(C, dtype=F32)
    rel = pos[:, None] - pos[None, :]
    intra = jnp.where(rel[None] >= 0, jnp.exp(log_gamma[:, None, None] * jnp.maximum(rel, 0.0)[None]), 0.0)
    k_tail = jnp.exp(log_gamma[:, None] * (C - 1 - pos)[None, :])
    q_head = jnp.exp(log_gamma[:, None] * (pos + 1.0)[None, :])
    chunk_decay = jnp.exp(log_gamma * C)
    wide = lambda t: jnp.broadcast_to(t[:, :, None], (N_GROUPS, C, Hd))
    return cos2, sin2, intra, wide(k_tail), wide(q_head), jnp.broadcast_to(chunk_decay[:, None, None], (N_GROUPS, 1, Hd))


def _rope(x, cos2, sin2):
    return x * cos2 + pltpu.roll(x, HEAD_DIM // 2, 1) * sin2


def _rope_t(d, cos2, sin2):
    return d * cos2 + pltpu.roll(d * sin2, HEAD_DIM // 2, 1)


def _ret_specs(tseg, seg_of):
    Hd, G = HEAD_DIM, N_GROUPS
    col = lambda kind: pl.BlockSpec((tseg, Hd), lambda h, s: (seg_of(s), G * kind + h))
    tab = pl.BlockSpec((tseg, Hd), lambda h, s: (seg_of(s), 0))
    head = pl.BlockSpec((1, RET_CHUNK, Hd), lambda h, s: (h, 0, 0))
    cd = pl.BlockSpec((1, 1, Hd), lambda h, s: (h, 0, 0))
    gain = pl.BlockSpec((1, Hd), lambda h, s: (0, h))
    return col, tab, head, cd, gain


def ret_fwd(proj, ret_norm, tables, name, hosted=()):
    T = proj.shape[0]
    Hd, C, G = HEAD_DIM, RET_CHUNK, N_GROUPS
    tseg = min(T, 1024)
    nseg, nck = T // tseg, tseg // C
    scale = Hd ** -0.5
    cos2, sin2, intra, k_tail, q_head, chunk_decay = tables

    def body(q_ref, k_ref, v_ref, g_ref, gain_ref, cos_ref, sin_ref, m_ref, kt_ref, qh_ref, cd_ref,
             b_ref, o_ref, rp_ref, state):
        @pl.when(pl.program_id(1) == 0)
        def _():
            state[...] = jnp.zeros_like(state)

        def chunk(ci, carry):
            rows = pl.ds(pl.multiple_of(ci * C, C), C)
            cos, sin = cos_ref[rows, :], sin_ref[rows, :]
            qr = _rope(q_ref[rows, :], cos, sin)
            kr = _rope(k_ref[rows, :], cos, sin) * scale
            qb, kb, vb = qr.astype(BF16), kr.astype(BF16), v_ref[rows, :].astype(BF16)
            r = state[...]
            rp_ref[0, ci] = r.astype(BF16)
            sc = _dot_nt(qb, kb) * m_ref[0]
            o = _dot(sc.astype(BF16), vb) + _dot((qr * qh_ref[0]).astype(BF16), r.astype(BF16))
            state[...] = cd_ref[0] * r + _dot_tn((kr * kt_ref[0]).astype(BF16), vb)
            o_ref[rows, :] = o
            on = o * _rstd(o)
            b_ref[rows, :] = (jax.nn.silu(g_ref[rows, :]) * (on * gain_ref[...])).astype(BF16)
            return carry

        lax.fori_loop(0, nck, chunk, 0)

    col, tab, head, cd, gain = _ret_specs(tseg, lambda s: s)
    out_col = pl.BlockSpec((tseg, Hd), lambda h, s: (s, h))
    return _call(
        body, hosted, name=name, grid=(G, nseg),
        in_specs=[col(1), col(2), col(3), col(4), gain, tab, tab, head, head, head, cd],
        out_specs=[out_col, out_col, pl.BlockSpec((1, nck, Hd, Hd), lambda h, s: (h, s, 0, 0))],
        out_shape=[jax.ShapeDtypeStruct((T, G * Hd), BF16), jax.ShapeDtypeStruct((T, G * Hd), F32),
                   jax.ShapeDtypeStruct((G, T // C, Hd, Hd), BF16)],
        scratch_shapes=[pltpu.VMEM((Hd, Hd), F32)],
        args=[proj, proj, proj, proj, ret_norm, cos2, sin2, intra, k_tail, q_head, chunk_decay])


def ret_bwd(proj, db, o_pre, r_prev, ret_norm, tables, name, hosted=()):
    T = proj.shape[0]
    Hd, C, G = HEAD_DIM, RET_CHUNK, N_GROUPS
    tseg = min(T, 1024)
    nseg, nck = T // tseg, tseg // C
    scale = Hd ** -0.5
    cos2, sin2, intra, k_tail, q_head, chunk_decay = tables

    def body(q_ref, k_ref, v_ref, g_ref, db_ref, o_ref, rp_ref, gain_ref, cos_ref, sin_ref, m_ref, kt_ref, qh_ref, cd_ref,
             d_ref, dgain_ref, gstate):
        @pl.when(pl.program_id(1) == 0)
        def _():
            gstate[...] = jnp.zeros_like(gstate)
            dgain_ref[...] = jnp.zeros_like(dgain_ref)

        def chunk(t, carry):
            ci = nck - 1 - t
            rows = pl.ds(pl.multiple_of(ci * C, C), C)
            cos, sin = cos_ref[rows, :], sin_ref[rows, :]
            qr = _rope(q_ref[rows, :], cos, sin)
            kr = _rope(k_ref[rows, :], cos, sin) * scale
            qb, kb, vb = qr.astype(BF16), kr.astype(BF16), v_ref[rows, :].astype(BF16)
            qhb, ktb = (qr * qh_ref[0]).astype(BF16), (kr * kt_ref[0]).astype(BF16)
            sc = (_dot_nt(qb, kb) * m_ref[0]).astype(BF16)
            o = o_ref[rows, :]
            rstd = _rstd(o)
            on = o * rstd
            gain = gain_ref[...]
            silu, dsilu = _silu_parts(g_ref[rows, :])
            dy = db_ref[rows, :]
            dgain_ref[...] += jnp.sum(dy * silu * on, axis=0, keepdims=True)
            dg = dy * on * gain * dsilu
            don = dy * silu * gain
            dob = (rstd * (don - on * jnp.mean(don * on, axis=-1, keepdims=True))).astype(BF16)
            gn = gstate[...]
            gb = gn.astype(BF16)
            da = (_dot_nt(dob, vb) * m_ref[0]).astype(BF16)
            dq = _dot(da, kb) + _dot_nt(dob, rp_ref[0, ci]) * qh_ref[0]
            dk = _dot_tn(da, qb) + _dot_nt(vb, gb) * kt_ref[0]
            dv = _dot_tn(sc, dob) + _dot(ktb, gb)
            gstate[...] = cd_ref[0] * gn + _dot_tn(qhb, dob)
            d_ref[0, rows, :] = _rope_t(dq, cos, sin).astype(BF16)
            d_ref[1, rows, :] = _rope_t(dk * scale, cos, sin).astype(BF16)
            d_ref[2, rows, :] = dv.astype(BF16)
            d_ref[3, rows, :] = dg.astype(BF16)
            return carry

        lax.fori_loop(0, nck, chunk, 0)

    rev = lambda s: nseg - 1 - s
    col, tab, head, cd, gain = _ret_specs(tseg, rev)
    act = pl.BlockSpec((tseg, Hd), lambda h, s: (rev(s), h))
    return _call(
        body, hosted, name=name, grid=(G, nseg),
        in_specs=[col(1), col(2), col(3), col(4), act, act, pl.BlockSpec((1, nck, Hd, Hd), lambda h, s: (h, rev(s), 0, 0)),
                  gain, tab, tab, head, head, head, cd],
        out_specs=[pl.BlockSpec((4, tseg, Hd), lambda h, s: (0, rev(s), h)), gain],
        out_shape=[jax.ShapeDtypeStruct((4, T, G * Hd), BF16), jax.ShapeDtypeStruct((1, G * Hd), F32)],
        scratch_shapes=[pltpu.VMEM((Hd, Hd), F32)],
        args=[proj, proj, proj, proj, db, o_pre, r_prev, ret_norm, cos2, sin2, intra, k_tail, q_head, chunk_decay])


def final_loss(h, gain, target, name, hosted=()):
    T, D = h.shape
    tm = min(T, 512)

    def body(h_ref, g_ref, t_ref, dh_ref, loss_ref, dg_ref):
        @pl.when(pl.program_id(0) == 0)
        def _():
            loss_ref[...] = jnp.zeros_like(loss_ref)
            dg_ref[...] = jnp.zeros_like(dg_ref)

        hh = h_ref[...]
        gain_v = g_ref[...]
        err = hh * _rstd(hh) * gain_v - t_ref[...]
        loss_ref[...] += 0.5 * jnp.sum(jnp.mean(err * err, axis=-1, keepdims=True), axis=0, keepdims=True)
        dhn, dg = _rmsnorm_bwd(err * (1.0 / D), hh, gain_v)
        dh_ref[...] = dhn
        dg_ref[...] += jnp.sum(dg, axis=0, keepdims=True)

    row_spec = pl.BlockSpec((tm, D), lambda i: (i, 0))
    vec_spec = pl.BlockSpec((1, D), lambda i: (0, 0))
    return _call(
        body, hosted, name=name, grid=(T // tm,),
        in_specs=[row_spec, vec_spec, row_spec],
        out_specs=[row_spec, pl.BlockSpec((1, 128), lambda i: (0, 0)), vec_spec],
        out_shape=[jax.ShapeDtypeStruct((T, D), F32), jax.ShapeDtypeStruct((1, 128), F32), jax.ShapeDtypeStruct((1, D), F32)],
        args=[h, gain, target])


def prereduce(grad, recv, place, name):
    nsh, R, C = grad.shape
    rh = R // 2

    def body(place_ref, g_ref, r_ref, o_ref, own_ref):
        piece = (g_ref[...].astype(F32) + r_ref[...].astype(F32)).astype(BF16)
        o_ref[...] = piece

        @pl.when(pl.program_id(0) == place_ref[1])
        def _():
            own_ref[...] = piece

    return pl.pallas_call(
        body, name=name,
        grid_spec=pltpu.PrefetchScalarGridSpec(
            num_scalar_prefetch=1, grid=(nsh,),
            in_specs=[pl.BlockSpec((1, rh, C), lambda j, p: (j, p[0], 0)), pl.BlockSpec((1, rh, C), lambda j, p: (j, 0, 0))],
            out_specs=[pl.BlockSpec((1, rh, C), lambda j, p: (j, 0, 0)), pl.BlockSpec((1, rh, C), lambda j, p: (p[1], p[0], 0))]),
        out_shape=[jax.ShapeDtypeStruct((nsh, rh, C), BF16), jax.ShapeDtypeStruct((nsh, R, C), BF16)],
        compiler_params=pltpu.CompilerParams(vmem_limit_bytes=VMEM_LIMIT_V7X),
    )(place, grad, recv)


def _adamw(w, g, m, v):
    m = ADAM_B1 * m + (1.0 - ADAM_B1) * g
    v = ADAM_B2 * v + (1.0 - ADAM_B2) * (g * g)
    m_hat = m / (1.0 - ADAM_B1 ** ADAM_STEP)
    v_hat = v / (1.0 - ADAM_B2 ** ADAM_STEP)
    return -ADAM_LR * (m_hat / (jnp.sqrt(v_hat) + ADAM_EPS) + ADAM_WD * w), m, v


def adamw_sharded(tensors, name, hosted=()):
    nt = len(tensors)
    nsh, R, C = tensors[0][0].shape
    lanes = -(-C // 128) * 128
    per_row = 2 * nt * lanes * (nsh * 2 + 7 * 4)
    tr = max(r for r in range(16, R + 1, 16) if R % r == 0 and r * per_row <= ADAMW_VMEM_BUDGET)

    def body(*refs):
        ins, outs = refs[:4 * nt], refs[4 * nt:]
        for t in range(nt):
            p_ref, w_ref, m_ref, v_ref = ins[4 * t:4 * t + 4]
            g_ref, d_ref, nm_ref, nv_ref = outs[4 * t:4 * t + 4]
            g = p_ref[0].astype(F32)
            for i in range(1, nsh):
                g += p_ref[i].astype(F32)
            g_ref[...] = g
            d_ref[...], nm_ref[...], nv_ref[...] = _adamw(w_ref[...], g, m_ref[...], v_ref[...])

    spec = pl.BlockSpec((tr, C), lambda i: (i, 0))
    out = jax.ShapeDtypeStruct((R, C), F32)
    return _call(
        body, hosted, name=name, grid=(R // tr,),
        in_specs=[pl.BlockSpec((nsh, tr, C), lambda i: (0, i, 0)), spec, spec, spec] * nt,
        out_specs=[spec] * (4 * nt), out_shape=[out] * (4 * nt),
        args=[a for tensor in tensors for a in tensor])


def adamw_small(packs, w, m, v, name):
    ndev, R, L = packs.shape

    def body(p_ref, w_ref, m_ref, v_ref, g_ref, d_ref, nm_ref, nv_ref):
        g = p_ref[0]
        for i in range(1, ndev):
            g += p_ref[i]
        g_ref[...] = g
        d_ref[...], nm_ref[...], nv_ref[...] = _adamw(w_ref[...], g, m_ref[...], v_ref[...])

    out = jax.ShapeDtypeStruct((R, L), F32)
    return pl.pallas_call(body, name=name, out_shape=[out] * 4,
                          compiler_params=pltpu.CompilerParams(vmem_limit_bytes=VMEM_LIMIT_V7X))(packs, w, m, v)


BIG = ("ffn1_w1", "ffn1_w3", "ffn1_w2", "w_in", "w_out", "ffn2_w1", "ffn2_w3", "ffn2_w2")
TRANSPOSED = ("ffn1_w1", "ffn1_w3", "ffn2_w1", "ffn2_w3")
SMALL = ("ffn1_norm", "mix_norm", "pool_w", "pool_scale", "ret_norm", "ffn2_norm", "final_norm")
WEIGHTS = ("ffn1_norm", "ffn1_w1", "ffn1_w3", "ffn1_w2", "mix_norm", "w_in", "pool_w", "pool_scale", "ret_norm", "w_out",
           "ffn2_norm", "ffn2_w1", "ffn2_w3", "ffn2_w2", "final_norm")


def _pack(parts):
    return jnp.concatenate([parts[k].reshape(-1, 128) for k in SMALL], axis=0)


def _unpack(pack, like):
    out, row = {}, 0
    for k in SMALL:
        rows = like[k].size // 128
        out[k] = pack[row:row + rows].reshape(like[k].shape)
        row += rows
    return out


def kernel(x, ffn1_norm, ffn1_w1, ffn1_w3, ffn1_w2, mix_norm, w_in, pool_w, pool_scale, ret_norm, w_out, ffn2_norm, ffn2_w1, ffn2_w3, ffn2_w2, final_norm, loss_target, m_ffn1_norm, m_ffn1_w1, m_ffn1_w3, m_ffn1_w2, m_mix_norm, m_w_in, m_pool_w, m_pool_scale, m_ret_norm, m_w_out, m_ffn2_norm, m_ffn2_w1, m_ffn2_w3, m_ffn2_w2, m_final_norm, v_ffn1_norm, v_ffn1_w1, v_ffn1_w3, v_ffn1_w2, v_mix_norm, v_w_in, v_pool_w, v_pool_scale, v_ret_norm, v_w_out, v_ffn2_norm, v_ffn2_w1, v_ffn2_w3, v_ffn2_w2, v_final_norm):
    w = dict(ffn1_norm=ffn1_norm, ffn1_w1=ffn1_w1, ffn1_w3=ffn1_w3, ffn1_w2=ffn1_w2, mix_norm=mix_norm, w_in=w_in, pool_w=pool_w,
             pool_scale=pool_scale, ret_norm=ret_norm, w_out=w_out, ffn2_norm=ffn2_norm, ffn2_w1=ffn2_w1, ffn2_w3=ffn2_w3,
             ffn2_w2=ffn2_w2, final_norm=final_norm)
    m = dict(ffn1_norm=m_ffn1_norm, ffn1_w1=m_ffn1_w1, ffn1_w3=m_ffn1_w3, ffn1_w2=m_ffn1_w2, mix_norm=m_mix_norm, w_in=m_w_in,
             pool_w=m_pool_w, pool_scale=m_pool_scale, ret_norm=m_ret_norm, w_out=m_w_out, ffn2_norm=m_ffn2_norm, ffn2_w1=m_ffn2_w1,
             ffn2_w3=m_ffn2_w3, ffn2_w2=m_ffn2_w2, final_norm=m_final_norm)
    v = dict(ffn1_norm=v_ffn1_norm, ffn1_w1=v_ffn1_w1, ffn1_w3=v_ffn1_w3, ffn1_w2=v_ffn1_w2, mix_norm=v_mix_norm, w_in=v_w_in,
             pool_w=v_pool_w, pool_scale=v_pool_scale, ret_norm=v_ret_norm, w_out=v_w_out, ffn2_norm=v_ffn2_norm, ffn2_w1=v_ffn2_w1,
             ffn2_w3=v_ffn2_w3, ffn2_w2=v_ffn2_w2, final_norm=v_final_norm)
    xs, target = x[0], loss_target[0]
    T = xs.shape[0]
    tables = _ret_tables(T)
    place = jnp.stack([lax.axis_index("c"), 2 * lax.axis_index("x") + lax.axis_index("y")]).astype(jnp.int32)
    local = lambda d, k: jnp.transpose(d[k][0]) if k in TRANSPOSED else d[k][0]
    result = lambda o, k: jnp.transpose(o)[None] if k in TRANSPOSED else o[None]
    sh = {k: local(w, k).astype(BF16) for k in BIG}
    gather = lambda *names: [ChipExchange([sh[k] for k in names], False)]
    wg, grad, delta, new_m, new_v = {}, {}, {}, {}, {}

    def update(names, pieces, name, hosted=()):
        outs, extras = adamw_sharded([(p, local(w, k), local(m, k), local(v, k)) for k, p in zip(names, pieces)], name, hosted)
        for t, k in enumerate(names):
            grad[k], delta[k], new_m[k], new_v[k] = [result(o, k) for o in outs[4 * t:4 * t + 4]]
        return extras

    def reduce_in_chip(name, partial, recv):
        return prereduce(partial, recv, place, "prereduce_" + name)

    scatter = lambda *reduced: ChipExchange([r[0] for r in reduced], True, [r[1] for r in reduced])

    (wg["ffn1_w1"], wg["ffn1_w3"]), = exchange(gather("ffn1_w1", "ffn1_w3"), "gather_ffn1")
    (n1, a1, b1, s1), ((wg["ffn1_w2"], wg["w_in"]),) = ffn_up(
        xs, ffn1_norm, wg["ffn1_w1"], wg["ffn1_w3"], "ffn1_up", gather("ffn1_w2", "w_in"))
    (h1,), ((wg["w_out"],),) = ffn_down(s1, wg["ffn1_w2"], xs, "ffn1_down", gather("w_out"))
    (u, proj), ((wg["ffn2_w1"],),) = mix_in(h1, mix_norm, wg["w_in"], "mix_in", gather("ffn2_w1"))
    (pa,), _ = pool_fwd(proj, pool_w[0], pool_scale, "pool_fwd")
    (rb, o_pre, r_prev), ((wg["ffn2_w3"],),) = ret_fwd(proj, ret_norm, tables, "ret_fwd", gather("ffn2_w3"))
    (h2,), _ = mix_out(pa, rb, wg["w_out"], h1, "mix_out")
    (n2, a2, b2, s2), ((wg["ffn2_w2"],),) = ffn_up(
        h2, ffn2_norm, wg["ffn2_w1"], wg["ffn2_w3"], "ffn2_up", gather("ffn2_w2"))
    (h3,), _ = ffn_down(s2, wg["ffn2_w2"], h2, "ffn2_down")
    (dh3, loss, d_final), _ = final_loss(h3, final_norm[None], target, "final_loss")
    loss = lax.psum(loss[0, 0], ("x", "y", "c"))

    (da2, db2), _ = ffn_bwd_act(dh3, wg["ffn2_w2"], a2, b2, "ffn2_bwd_act")
    (g_f2w2,), _ = ffn_dw2(s2, dh3, "ffn2_dw2")
    (g_f2w1, g_f2w3), ((r_f2w2,),) = ffn_dw13(n2, da2, db2, "ffn2_dw13", [SiblingExchange([g_f2w2])])
    p_f2w2 = reduce_in_chip("ffn2_w2", g_f2w2, r_f2w2)
    (dh2, d_ffn2), ((q_f2w2,), (r_f2w1, r_f2w3)) = ffn_bwd_in(
        da2, db2, wg["ffn2_w1"], wg["ffn2_w3"], h2, ffn2_norm, dh3, "ffn2_bwd_in",
        [scatter(p_f2w2), SiblingExchange([g_f2w1, g_f2w3])])
    p_f2w1 = reduce_in_chip("ffn2_w1", g_f2w1, r_f2w1)
    p_f2w3 = reduce_in_chip("ffn2_w3", g_f2w3, r_f2w3)
    (dpa, drb, g_wout), _ = mix_out_bwd(dh2, wg["w_out"], pa, rb, "mix_out_bwd")
    (dpool, d_pool_w, d_pool_scale), _ = pool_bwd(proj, dpa, pool_w[0], pool_scale, "pool_bwd")
    (dqkvg, d_ret_norm), ((q_f2w1, q_f2w3), (r_wout,)) = ret_bwd(
        proj, drb, o_pre, r_prev, ret_norm, tables, "ret_bwd", [scatter(p_f2w1, p_f2w3), SiblingExchange([g_wout])])
    p_wout = reduce_in_chip("w_out", g_wout, r_wout)
    d = jnp.concatenate([dpool, dqkvg[0], dqkvg[1], dqkvg[2], dqkvg[3]], axis=1)
    (g_win,), ((q_wout,),) = mix_dwin(u, d, N_CHIPS, "mix_dwin", [scatter(p_wout)])
    (dh1, d_mix), ((r_win,),) = mix_in_bwd(d, wg["w_in"], h1, mix_norm, dh2, "mix_in_bwd", [SiblingExchange([g_win])])
    p_win = reduce_in_chip("w_in", g_win, r_win)
    (da1, db1), ((q_win,),) = ffn_bwd_act(dh1, wg["ffn1_w2"], a1, b1, "ffn1_bwd_act", [scatter(p_win)])
    (g_f1w1, g_f1w3), _ = ffn_dw13(n1, da1, db1, "ffn1_dw13")
    (g_f1w2,), ((r_f1w1, r_f1w3),) = ffn_dw2(s1, dh1, "ffn1_dw2", [SiblingExchange([g_f1w1, g_f1w3])])
    p_f1w1 = reduce_in_chip("ffn1_w1", g_f1w1, r_f1w1)
    p_f1w3 = reduce_in_chip("ffn1_w3", g_f1w3, r_f1w3)
    (dx, d_ffn1), ((q_f1w1, q_f1w3), (r_f1w2,)) = ffn_bwd_in(
        da1, db1, wg["ffn1_w1"], wg["ffn1_w3"], xs, ffn1_norm, dh1, "ffn1_bwd_in",
        [scatter(p_f1w1, p_f1w3), SiblingExchange([g_f1w2])])
    p_f1w2 = reduce_in_chip("ffn1_w2", g_f1w2, r_f1w2)

    small = {"ffn1_norm": d_ffn1, "mix_norm": d_mix, "pool_w": d_pool_w, "pool_scale": d_pool_scale,
             "ret_norm": d_ret_norm, "ffn2_norm": d_ffn2, "final_norm": d_final}
    (q_f1w2,), (packs,) = update(["ffn2_w1", "ffn2_w3", "ffn1_w1", "ffn1_w3"], [q_f2w1, q_f2w3, q_f1w1, q_f1w3], "adamw_w13",
                                 [scatter(p_f1w2), AllExchange(_pack(small))])
    update(["ffn2_w2", "ffn1_w2"], [q_f2w2, q_f1w2], "adamw_w2")
    update(["w_in"], [q_win], "adamw_w_in")
    update(["w_out"], [q_wout], "adamw_w_out")
    outs = adamw_small(packs, _pack(w), _pack(m), _pack(v), "adamw_small")
    for res, pack in zip((grad, delta, new_m, new_v), outs):
        res.update(_unpack(pack, w))

    return (loss, dx[None], *[grad[k] for k in WEIGHTS], *[delta[k] for k in WEIGHTS],
            *[new_m[k] for k in WEIGHTS], *[new_v[k] for k in WEIGHTS])
```

```python
import math

import jax
import jax.numpy as jnp
from jax import lax
from jax.experimental import pallas as pl
from jax.experimental.pallas import tpu as pltpu

F32 = jnp.float32
BF16 = jnp.bfloat16

EPS = 1e-6
N_CHIPS = 4
N_GROUPS = 4
HEAD_DIM = 128
RET_CHUNK = 128
ROPE_BASE = 10000.0
ADAM_LR, ADAM_B1, ADAM_B2, ADAM_EPS, ADAM_WD, ADAM_STEP = 0.001, 0.9, 0.999, 1e-08, 0.01, 10
VMEM_LIMIT_V7X = 56 * 1024 * 1024
ADAMW_VMEM_BUDGET = 32 * 1024 * 1024
MESH = pl.DeviceIdType.MESH
ANY = pl.BlockSpec(memory_space=pl.ANY)


def _dot(a, b):
    return jnp.dot(a, b, preferred_element_type=F32)


def _dot_nt(a, b):
    return lax.dot_general(a, b, (((1,), (1,)), ((), ())), preferred_element_type=F32)


def _dot_tn(a, b):
    return lax.dot_general(a, b, (((0,), (0,)), ((), ())), preferred_element_type=F32)


def _rstd(h):
    return lax.rsqrt(jnp.mean(h * h, axis=-1, keepdims=True) + EPS)


def _rmsnorm_bwd(dn, h, gain):
    r = _rstd(h)
    nh = h * r
    dnh = dn * gain
    dh = r * (dnh - nh * jnp.mean(dnh * nh, axis=-1, keepdims=True))
    return dh, dn * nh


def _silu_parts(a):
    sig = jax.nn.sigmoid(a)
    silu = a * sig
    return silu, sig + silu * (1.0 - sig)


def _mesh_pos():
    return lax.axis_index("x"), lax.axis_index("y"), lax.axis_index("c")


class ChipExchange:
    def __init__(self, srcs, scatter, placed=()):
        n = len(srcs)
        self.inputs, self.scatter, self.n, self.reach = list(srcs) + list(placed), scatter, n, REACH_CHIPS
        self.aliases = {n + t: t for t in range(n)} if scatter else {}
        self.half_rows = [s.shape[1] if scatter else s.shape[0] // 2 for s in srcs]
        self.out_shape = [jax.ShapeDtypeStruct((N_CHIPS, 2 * rh, s.shape[-1]), s.dtype) for s, rh in zip(srcs, self.half_rows)]
        if scatter:
            self.out_shape += [jax.ShapeDtypeStruct((2, rh // 2, s.shape[-1]), s.dtype) for s, rh in zip(srcs, self.half_rows)]
        dma = pltpu.SemaphoreType.DMA
        self.sems = [dma((4 * n,)), dma((4 * n,)), dma((2 * n,)), dma((2 * n,)), dma((4 * n,)), dma((4 * n,))]

    def _copies(self, src, out, sems):
        hop1_send, hop1_recv, hop2_send, hop2_recv, d2d_send, d2d_recv = sems
        x, y, c = _mesh_pos()
        me, dg = 2 * x + y, 2 * (1 - x) + (1 - y)
        sibling = (x, y, 1 - c)
        n = self.n
        mine, theirs = c, 1 - c

        def nb(a):
            nx, ny = x ^ (1 - a), y ^ a
            return 2 * nx + ny, (nx, ny, c)

        def remote(s, d, send, recv, k, to):
            return pltpu.make_async_remote_copy(src_ref=s, dst_ref=d, send_sem=send.at[k], recv_sem=recv.at[k],
                                                device_id=to, device_id_type=MESH)

        class Copies:
            def slot(_, t, chip, half):
                rh = self.half_rows[t]
                return out[t].at[chip, pl.ds(half * rh, rh), :]

            def quarter(_, t, chip, q):
                qh = self.half_rows[t] // 2
                return out[t].at[chip, pl.ds(mine * 2 * qh + q * qh, qh), :]

            def own_shard(k, t):
                return remote(src[t], out[t].at[me], d2d_send, d2d_recv, 4 * t + 3, sibling)

            def hop1(k, t, a, transit=False):
                rh = self.half_rows[t]
                chip, to = nb(a)
                if transit:
                    piece = src[t].at[dg, pl.ds(a * (rh // 2), rh // 2), :]
                    return remote(piece, out[n + t].at[a], hop1_send, hop1_recv, 4 * t + 2 + a, to)
                piece = src[t].at[chip] if self.scatter else src[t].at[pl.ds(mine * rh, rh), :]
                return remote(piece, k.slot(t, me, mine), hop1_send, hop1_recv, 4 * t + a, to)

            def landed1(k, t, a, transit=False):
                here = out[n + t].at[a] if transit else k.slot(t, nb(a)[0], mine)
                return remote(here, here, hop1_send, hop1_recv, 4 * t + (2 if transit else 0) + a, sibling)

            def hop2(k, t, q):
                origin, to = nb(q)[0], nb(1 - q)[1]
                piece = out[n + t].at[q] if self.scatter else k.quarter(t, origin, q)
                return remote(piece, k.quarter(t, origin, q), hop2_send, hop2_recv, 2 * t + q, to)

            def landed2(k, t, q):
                here = k.quarter(t, dg, q)
                return remote(here, here, hop2_send, hop2_recv, 2 * t + q, sibling)

            def d2d(k, t, p, chip, own=False, arriving=False):
                if arriving:
                    there = k.slot(t, chip, theirs)
                    return remote(there, there, d2d_send, d2d_recv, 4 * t + p, sibling)
                piece = src[t].at[me] if own else k.slot(t, chip, mine)
                return remote(piece, k.slot(t, chip, mine), d2d_send, d2d_recv, 4 * t + p, sibling)

        return Copies(), nb, me, dg, c

    def start(self, src, out, sems):
        k, nb, me, dg, c = self._copies(src, out, sems)
        for t in range(self.n):
            for first in range(2):
                a = first ^ c
                k.hop1(t, a).start()
                if self.scatter:
                    k.hop1(t, a, transit=True).start()
            if self.scatter:
                k.d2d(t, 3, me, own=True).start()
            else:
                k.own_shard(t).start()

    def mid(self, src, out, sems):
        k, nb, me, dg, c = self._copies(src, out, sems)
        for t in range(self.n):
            for first in range(2):
                a = first ^ c
                if self.scatter:
                    k.landed1(t, a, transit=True).wait_recv()
                    k.hop2(t, a).start()
                k.landed1(t, a).wait_recv()
                if not self.scatter:
                    k.hop2(t, a).start()
                k.d2d(t, a, nb(a)[0]).start()

    def finish(self, src, out, sems):
        k, nb, me, dg, c = self._copies(src, out, sems)
        for t in range(self.n):
            for q in range(2):
                k.landed2(t, q).wait_recv()
            k.d2d(t, 2, dg).start()
        for t in range(self.n):
            for a in range(2):
                k.d2d(t, a, nb(a)[0], arriving=True).wait_recv()
            k.d2d(t, 2, dg, arriving=True).wait_recv()
            if self.scatter:
                k.d2d(t, 3, me, arriving=True).wait_recv()
        for t in range(self.n):
            for a in range(2):
                k.hop1(t, a).wait_send()
                if self.scatter:
                    k.hop1(t, a, transit=True).wait_send()
                k.hop2(t, a).wait_send()
                k.d2d(t, a, nb(a)[0]).wait_send()
            k.d2d(t, 2, dg).wait_send()
            if self.scatter:
                k.d2d(t, 3, me, own=True).wait_send()
            else:
                k.own_shard(t).wait()


class SiblingExchange:
    def __init__(self, grads):
        self.inputs, self.n, self.aliases, self.reach = list(grads), len(grads), {}, REACH_SIBLING
        self.half_rows = [g.shape[1] // 2 for g in grads]
        self.out_shape = [jax.ShapeDtypeStruct((g.shape[0], rh, g.shape[2]), g.dtype) for g, rh in zip(grads, self.half_rows)]
        self.sems = [pltpu.SemaphoreType.DMA((self.n,)), pltpu.SemaphoreType.DMA((self.n,))]

    def _plan(self, src, out, sems):
        x, y, c = _mesh_pos()
        return [pltpu.make_async_remote_copy(
            src_ref=src[t].at[:, pl.ds((1 - c) * self.half_rows[t], self.half_rows[t]), :], dst_ref=out[t],
            send_sem=sems[0].at[t], recv_sem=sems[1].at[t], device_id=(x, y, 1 - c), device_id_type=MESH) for t in range(self.n)]

    def start(self, src, out, sems):
        for cp in self._plan(src, out, sems):
            cp.start()

    def mid(self, src, out, sems):
        pass

    def finish(self, src, out, sems):
        for cp in self._plan(src, out, sems):
            cp.wait()


REACH_SIBLING, REACH_CHIPS, REACH_ALL = 0, 1, 2


def _entry_barrier(reach):
    x, y, c = _mesh_pos()
    peers = [(x, y, 1 - c)]
    if reach == REACH_CHIPS:
        peers += [(1 - x, y, c), (x, 1 - y, c)]
    elif reach == REACH_ALL:
        peers = [(x ^ dx, y ^ dy, c ^ dc) for dx in (0, 1) for dy in (0, 1) for dc in (0, 1)][1:]
    barrier = pltpu.get_barrier_semaphore()
    for peer in peers:
        pl.semaphore_signal(barrier, inc=1, device_id=peer, device_id_type=MESH)
    pl.semaphore_wait(barrier, len(peers))


def _call(body, hosted=(), *, name, in_specs, out_specs, out_shape, args, grid=(), scratch_shapes=()):
    n_in, n_out, n_scr = len(in_specs), len(out_specs), len(scratch_shapes)
    total = math.prod(grid)
    mid_step = max(0, (5 * total) // 8 - 1)

    def full(*refs):
        pos = [0]

        def take(k):
            pos[0] += k
            return refs[pos[0] - k:pos[0]]

        ins, h_in = take(n_in), [take(len(h.inputs)) for h in hosted]
        outs, h_out = take(n_out), [take(len(h.out_shape)) for h in hosted]
        scr, h_sem = take(n_scr), [take(len(h.sems)) for h in hosted]
        step = 0
        for axis, size in enumerate(grid):
            step = step * size + pl.program_id(axis)

        def phase(at, method):
            if not hosted:
                return

            def run():
                if method == "start":
                    _entry_barrier(reach)
                for h, s, o, m in zip(hosted, h_in, h_out, h_sem):
                    getattr(h, method)(s, o, m)

            if total == 1:
                run()
            else:
                pl.when(step == at)(run)

        phase(0, "start")
        body(*ins, *outs, *scr)
        phase(mid_step, "mid")
        phase(total - 1, "finish")

    aliases, i0, o0 = {}, n_in, n_out
    for h in hosted:
        aliases.update({i0 + i: o0 + o for i, o in h.aliases.items()})
        i0, o0 = i0 + len(h.inputs), o0 + len(h.out_shape)
    reach = max((h.reach for h in hosted), default=None)
    params = dict(vmem_limit_bytes=VMEM_LIMIT_V7X)
    if hosted:
        params["collective_id"] = reach
    results = pl.pallas_call(
        full, name=name, grid=grid,
        in_specs=list(in_specs) + [ANY] * (i0 - n_in),
        out_specs=list(out_specs) + [ANY] * (o0 - n_out),
        out_shape=list(out_shape) + [s for h in hosted for s in h.out_shape],
        scratch_shapes=list(scratch_shapes) + [s for h in hosted for s in h.sems],
        input_output_aliases=aliases,
        compiler_params=pltpu.CompilerParams(**params),
    )(*args, *[s for h in hosted for s in h.inputs])
    outs, extras, pos = list(results[:n_out]), [], n_out
    for h in hosted:
        extras.append(list(results[pos:pos + h.n]))
        pos += len(h.out_shape)
    return outs, extras


def exchange(hosted, name):
    return _call(lambda: None, hosted, name=name, in_specs=[], out_specs=[], out_shape=[], args=[])[1]


class AllExchange:
    def __init__(self, pack):
        self.inputs, self.n, self.aliases, self.reach = [pack], 1, {}, REACH_ALL
        self.out_shape = [jax.ShapeDtypeStruct((2 * N_CHIPS,) + pack.shape, pack.dtype)]
        self.sems = [pltpu.SemaphoreType.DMA, pltpu.SemaphoreType.DMA((7,)), pltpu.SemaphoreType.DMA((7,))]

    def _copies(self, src, out, sems):
        local_sem, send_sem, recv_sem = sems
        x, y, c = _mesh_pos()
        flips = [(dx, dy, dc) for dx in (0, 1) for dy in (0, 1) for dc in (0, 1)][1:]
        peers = [(x ^ dx, y ^ dy, c ^ dc) for dx, dy, dc in flips]
        remote = lambda s, d, k: pltpu.make_async_remote_copy(
            src_ref=s, dst_ref=d, send_sem=send_sem.at[k], recv_sem=recv_sem.at[k], device_id=peers[k], device_id_type=MESH)
        sends = [remote(src[0], out[0].at[4 * x + 2 * y + c], k) for k in range(7)]
        landed = [remote(out[0].at[4 * px + 2 * py + pc], out[0].at[4 * px + 2 * py + pc], k) for k, (px, py, pc) in enumerate(peers)]
        return sends, landed, pltpu.make_async_copy(src[0], out[0].at[4 * x + 2 * y + c], local_sem)

    def start(self, src, out, sems):
        sends, _, local = self._copies(src, out, sems)
        for cp in sends:
            cp.start()
        local.start()

    def mid(self, src, out, sems):
        pass

    def finish(self, src, out, sems):
        sends, landed, local = self._copies(src, out, sems)
        for cp in landed:
            cp.wait_recv()
        for cp in sends:
            cp.wait_send()
        local.wait()


def ffn_up(h, gain, w1g, w3g, name, hosted=()):
    T, D = h.shape
    nsh, Fs, _ = w1g.shape
    tm = min(T, 1024)

    def body(h_ref, g_ref, w1_ref, w3_ref, n_ref, ga_ref, gb_ref, s_ref):
        @pl.when(pl.program_id(1) == 0)
        def _():
            hh = h_ref[...]
            n_ref[...] = (hh * _rstd(hh) * g_ref[...]).astype(BF16)

        n = n_ref[...]
        a = _dot_nt(n, w1_ref[0])
        b = _dot_nt(n, w3_ref[0])
        silu, dsilu = _silu_parts(a)
        ga_ref[0] = (b * dsilu).astype(BF16)
        gb_ref[0] = silu.astype(BF16)
        s_ref[0] = (silu * b).astype(BF16)

    act = jax.ShapeDtypeStruct((nsh, T, Fs), BF16)
    act_spec = pl.BlockSpec((1, tm, Fs), lambda i, j: (j, i, 0))
    w_spec = pl.BlockSpec((1, Fs, D), lambda i, j: (j, 0, 0))
    return _call(
        body, hosted, name=name, grid=(T // tm, nsh),
        in_specs=[pl.BlockSpec((tm, D), lambda i, j: (i, 0)), pl.BlockSpec((1, D), lambda i, j: (0, 0)), w_spec, w_spec],
        out_specs=[pl.BlockSpec((tm, D), lambda i, j: (i, 0)), act_spec, act_spec, act_spec],
        out_shape=[jax.ShapeDtypeStruct((T, D), BF16), act, act, act],
        args=[h, gain, w1g, w3g])


def ffn_down(s, w2g, h, name, hosted=()):
    nsh, T, Fs = s.shape
    D = h.shape[1]
    tm = min(T, 512)

    def body(s_ref, w2_ref, h_ref, o_ref):
        f = _dot(s_ref[0], w2_ref[0])
        for j in range(1, nsh):
            f += _dot(s_ref[j], w2_ref[j])
        o_ref[...] = h_ref[...] + 0.5 * f

    return _call(
        body, hosted, name=name, grid=(T // tm,),
        in_specs=[pl.BlockSpec((nsh, tm, Fs), lambda i: (0, i, 0)), pl.BlockSpec((nsh, Fs, D), lambda i: (0, 0, 0)),
                  pl.BlockSpec((tm, D), lambda i: (i, 0))],
        out_specs=[pl.BlockSpec((tm, D), lambda i: (i, 0))],
        out_shape=[jax.ShapeDtypeStruct((T, D), F32)],
        args=[s, w2g, h])


def ffn_bwd_act(dh, w2g, ga, gb, name, hosted=()):
    T, D = dh.shape
    nsh, Fs, _ = w2g.shape
    tm = min(T, 1024)

    def body(dh_ref, w2_ref, ga_ref, gb_ref, da_ref, db_ref, df_ref):
        @pl.when(pl.program_id(1) == 0)
        def _():
            df_ref[...] = (0.5 * dh_ref[...]).astype(BF16)

        ds = _dot_nt(df_ref[...], w2_ref[0])
        da_ref[0] = (ds * ga_ref[0].astype(F32)).astype(BF16)
        db_ref[0] = (ds * gb_ref[0].astype(F32)).astype(BF16)

    act = jax.ShapeDtypeStruct((nsh, T, Fs), BF16)
    act_spec = pl.BlockSpec((1, tm, Fs), lambda i, j: (j, i, 0))
    row_spec = pl.BlockSpec((tm, D), lambda i, j: (i, 0))
    return _call(
        body, hosted, name=name, grid=(T // tm, nsh),
        in_specs=[row_spec, pl.BlockSpec((1, Fs, D), lambda i, j: (j, 0, 0)), act_spec, act_spec],
        out_specs=[act_spec, act_spec, row_spec],
        out_shape=[act, act, jax.ShapeDtypeStruct((T, D), BF16)],
        args=[dh, w2g, ga, gb])


def ffn_dw2(s, df, name, hosted=()):
    nsh, T, Fs = s.shape
    D = df.shape[1]
    tk = min(T, 512)
    nk = T // tk

    def body(s_ref, df_ref, o_ref, acc):
        k = pl.program_id(1)

        @pl.when(k == 0)
        def _():
            acc[...] = jnp.zeros_like(acc)

        acc[...] += _dot_tn(s_ref[0], df_ref[...])

        @pl.when(k == nk - 1)
        def _():
            o_ref[0] = acc[...].astype(BF16)

    return _call(
        body, hosted, name=name, grid=(nsh, nk),
        in_specs=[pl.BlockSpec((1, tk, Fs), lambda j, k: (j, k, 0)), pl.BlockSpec((tk, D), lambda j, k: (k, 0))],
        out_specs=[pl.BlockSpec((1, Fs, D), lambda j, k: (j, 0, 0))],
        out_shape=[jax.ShapeDtypeStruct((nsh, Fs, D), BF16)],
        scratch_shapes=[pltpu.VMEM((Fs, D), F32)],
        args=[s, df])


def ffn_dw13(n, da, db, name, hosted=()):
    T, D = n.shape
    nsh, _, Fs = da.shape
    tk = min(T, 512)
    nk = T // tk

    def body(n_ref, da_ref, db_ref, o1_ref, o3_ref, acc1, acc3):
        k = pl.program_id(1)

        @pl.when(k == 0)
        def _():
            acc1[...] = jnp.zeros_like(acc1)
            acc3[...] = jnp.zeros_like(acc3)

        nn = n_ref[...]
        acc1[...] += _dot_tn(da_ref[0], nn)
        acc3[...] += _dot_tn(db_ref[0], nn)

        @pl.when(k == nk - 1)
        def _():
            o1_ref[0] = acc1[...].astype(BF16)
            o3_ref[0] = acc3[...].astype(BF16)

    act_spec = pl.BlockSpec((1, tk, Fs), lambda j, k: (j, k, 0))
    out = jax.ShapeDtypeStruct((nsh, Fs, D), BF16)
    out_spec = pl.BlockSpec((1, Fs, D), lambda j, k: (j, 0, 0))
    return _call(
        body, hosted, name=name, grid=(nsh, nk),
        in_specs=[pl.BlockSpec((tk, D), lambda j, k: (k, 0)), act_spec, act_spec],
        out_specs=[out_spec, out_spec],
        out_shape=[out, out],
        scratch_shapes=[pltpu.VMEM((Fs, D), F32), pltpu.VMEM((Fs, D), F32)],
        args=[n, da, db])


def ffn_bwd_in(da, db, w1g, w3g, h, gain, dh, name, hosted=()):
    nsh, T, Fs = da.shape
    D = h.shape[1]
    tm = min(T, 256)

    def body(da_ref, db_ref, w1_ref, w3_ref, h_ref, g_ref, dh_ref, o_ref, dg_ref):
        dn = _dot(da_ref[0], w1_ref[0]) + _dot(db_ref[0], w3_ref[0])
        for j in range(1, nsh):
            dn += _dot(da_ref[j], w1_ref[j]) + _dot(db_ref[j], w3_ref[j])
        dhn, dg = _rmsnorm_bwd(dn, h_ref[...], g_ref[...])
        o_ref[...] = dh_ref[...] + dhn

        @pl.when(pl.program_id(0) == 0)
        def _():
            dg_ref[...] = jnp.zeros_like(dg_ref)

        dg_ref[...] += jnp.sum(dg, axis=0, keepdims=True)

    act_spec = pl.BlockSpec((nsh, tm, Fs), lambda i: (0, i, 0))
    w_spec = pl.BlockSpec((nsh, Fs, D), lambda i: (0, 0, 0))
    row_spec = pl.BlockSpec((tm, D), lambda i: (i, 0))
    vec_spec = pl.BlockSpec((1, D), lambda i: (0, 0))
    return _call(
        body, hosted, name=name, grid=(T // tm,),
        in_specs=[act_spec, act_spec, w_spec, w_spec, row_spec, vec_spec, row_spec],
        out_specs=[row_spec, vec_spec],
        out_shape=[jax.ShapeDtypeStruct((T, D), F32), jax.ShapeDtypeStruct((1, D), F32)],
        args=[da, db, w1g, w3g, h, gain, dh])


def mix_in(h, gain, wing, name, hosted=()):
    T, D = h.shape
    nsh, _, Cs = wing.shape
    tm = min(T, 512)

    def body(h_ref, g_ref, w_ref, u_ref, p_ref):
        hh = h_ref[...]
        u = (hh * _rstd(hh) * g_ref[...]).astype(BF16)
        u_ref[...] = u
        for j in range(nsh):
            p_ref[:, j * Cs:(j + 1) * Cs] = _dot(u, w_ref[j])

    return _call(
        body, hosted, name=name, grid=(T // tm,),
        in_specs=[pl.BlockSpec((tm, D), lambda i: (i, 0)), pl.BlockSpec((1, D), lambda i: (0, 0)),
                  pl.BlockSpec((nsh, D, Cs), lambda i: (0, 0, 0))],
        out_specs=[pl.BlockSpec((tm, D), lambda i: (i, 0)), pl.BlockSpec((tm, nsh * Cs), lambda i: (i, 0))],
        out_shape=[jax.ShapeDtypeStruct((T, D), BF16), jax.ShapeDtypeStruct((T, nsh * Cs), F32)],
        args=[h, gain, wing])


def mix_out(a, b, woutg, h, name, hosted=()):
    T, W = a.shape
    D = h.shape[1]
    wout = woutg.reshape(2, W, D)
    tm = min(T, 512)

    def body(a_ref, b_ref, w_ref, h_ref, o_ref):
        o_ref[...] = h_ref[...] + _dot(a_ref[...], w_ref[0]) + _dot(b_ref[...], w_ref[1])

    return _call(
        body, hosted, name=name, grid=(T // tm,),
        in_specs=[pl.BlockSpec((tm, W), lambda i: (i, 0)), pl.BlockSpec((tm, W), lambda i: (i, 0)),
                  pl.BlockSpec((2, W, D), lambda i: (0, 0, 0)), pl.BlockSpec((tm, D), lambda i: (i, 0))],
        out_specs=[pl.BlockSpec((tm, D), lambda i: (i, 0))],
        out_shape=[jax.ShapeDtypeStruct((T, D), F32)],
        args=[a, b, wout, h])


def mix_out_bwd(dh, woutg, a, b, name, hosted=()):
    T, D = dh.shape
    W = a.shape[1]
    nsh, Rs, _ = woutg.shape
    wout = woutg.reshape(2, W, D)
    tk = min(T, 512)
    nk = T // tk

    def body(dh_ref, w_ref, a_ref, b_ref, da_ref, db_ref, dw_ref, acc):
        k = pl.program_id(0)

        @pl.when(k == 0)
        def _():
            acc[...] = jnp.zeros_like(acc)

        dhb = dh_ref[...].astype(BF16)
        da_ref[...] = _dot_nt(dhb, w_ref[0])
        db_ref[...] = _dot_nt(dhb, w_ref[1])
        acc[0:W, :] += _dot_tn(a_ref[...], dhb)
        acc[W:2 * W, :] += _dot_tn(b_ref[...], dhb)

        @pl.when(k == nk - 1)
        def _():
            for j in range(nsh):
                dw_ref[j] = acc[j * Rs:(j + 1) * Rs, :].astype(BF16)

    return _call(
        body, hosted, name=name, grid=(nk,),
        in_specs=[pl.BlockSpec((tk, D), lambda k: (k, 0)), pl.BlockSpec((2, W, D), lambda k: (0, 0, 0)),
                  pl.BlockSpec((tk, W), lambda k: (k, 0)), pl.BlockSpec((tk, W), lambda k: (k, 0))],
        out_specs=[pl.BlockSpec((tk, W), lambda k: (k, 0)), pl.BlockSpec((tk, W), lambda k: (k, 0)),
                   pl.BlockSpec((nsh, Rs, D), lambda k: (0, 0, 0))],
        out_shape=[jax.ShapeDtypeStruct((T, W), F32), jax.ShapeDtypeStruct((T, W), F32),
                   jax.ShapeDtypeStruct((nsh, Rs, D), BF16)],
        scratch_shapes=[pltpu.VMEM((2 * W, D), F32)],
        args=[dh, wout, a, b])


def mix_dwin(u, d, nsh, name, hosted=()):
    T, D = u.shape
    Cs = d.shape[1] // nsh
    tk = min(T, 512)
    nk = T // tk

    def body(u_ref, d_ref, o_ref, acc):
        k = pl.program_id(1)

        @pl.when(k == 0)
        def _():
            acc[...] = jnp.zeros_like(acc)

        acc[...] += _dot_tn(u_ref[...], d_ref[...])

        @pl.when(k == nk - 1)
        def _():
            o_ref[0] = acc[...].astype(BF16)

    return _call(
        body, hosted, name=name, grid=(nsh, nk),
        in_specs=[pl.BlockSpec((tk, D), lambda j, k: (k, 0)), pl.BlockSpec((tk, Cs), lambda j, k: (k, j))],
        out_specs=[pl.BlockSpec((1, D, Cs), lambda j, k: (j, 0, 0))],
        out_shape=[jax.ShapeDtypeStruct((nsh, D, Cs), BF16)],
        scratch_shapes=[pltpu.VMEM((D, Cs), F32)],
        args=[u, d])


def mix_in_bwd(d, wing, h, gain, dh, name, hosted=()):
    T, D = h.shape
    nsh, _, Cs = wing.shape
    tm = min(T, 512)

    def body(d_ref, w_ref, h_ref, g_ref, dh_ref, o_ref, dg_ref):
        du = _dot_nt(d_ref[:, 0:Cs], w_ref[0])
        for j in range(1, nsh):
            du += _dot_nt(d_ref[:, j * Cs:(j + 1) * Cs], w_ref[j])
        dhn, dg = _rmsnorm_bwd(du, h_ref[...], g_ref[...])
        o_ref[...] = dh_ref[...] + dhn

        @pl.when(pl.program_id(0) == 0)
        def _():
            dg_ref[...] = jnp.zeros_like(dg_ref)

        dg_ref[...] += jnp.sum(dg, axis=0, keepdims=True)

    row_spec = pl.BlockSpec((tm, D), lambda i: (i, 0))
    vec_spec = pl.BlockSpec((1, D), lambda i: (0, 0))
    return _call(
        body, hosted, name=name, grid=(T // tm,),
        in_specs=[pl.BlockSpec((tm, nsh * Cs), lambda i: (i, 0)), pl.BlockSpec((nsh, D, Cs), lambda i: (0, 0, 0)),
                  row_spec, vec_spec, row_spec],
        out_specs=[row_spec, vec_spec],
        out_shape=[jax.ShapeDtypeStruct((T, D), F32), jax.ShapeDtypeStruct((1, D), F32)],
        args=[d, wing, h, gain, dh])


def _pool_window(x, group, T, trailing):
    rows = lax.broadcasted_iota(jnp.int32, x.shape, 0)

    def shifted(z, k):
        if trailing:
            return jnp.where(rows >= k, pltpu.roll(z, k, 0), 0.0)
        return jnp.where(rows < T - k, pltpu.roll(z, T - k, 0), 0.0)

    s2 = x + shifted(x, 1)
    s4 = s2 + shifted(s2, 2)
    s8 = s4 + shifted(s4, 4)
    s16 = s8 + shifted(s8, 8)
    return jnp.where(group == 0, s2, jnp.where(group == 1, s4, jnp.where(group == 2, s8, s16)))


def _pool_count(group, shape):
    rows = lax.broadcasted_iota(jnp.int32, shape, 0)
    w = jnp.where(group == 0, 2, jnp.where(group == 1, 4, jnp.where(group == 2, 8, 16)))
    return jnp.minimum(rows + 1, w).astype(F32)


def pool_fwd(proj, pool_w, pool_scale, name, hosted=()):
    T = proj.shape[0]
    Hd = HEAD_DIM

    def body(x_ref, w_ref, sc_ref, a_ref):
        g = pl.program_id(0)
        x = x_ref[...]
        pooled = _pool_window(x, g, T, True) / _pool_count(g, x.shape) - x
        a_ref[...] = (_dot(pooled.astype(BF16), w_ref[0].astype(BF16)) * sc_ref[...]).astype(BF16)

    return _call(
        body, hosted, name=name, grid=(N_GROUPS,),
        in_specs=[pl.BlockSpec((T, Hd), lambda g: (0, g)), pl.BlockSpec((1, Hd, Hd), lambda g: (g, 0, 0)),
                  pl.BlockSpec((1, Hd), lambda g: (0, g))],
        out_specs=[pl.BlockSpec((T, Hd), lambda g: (0, g))],
        out_shape=[jax.ShapeDtypeStruct((T, N_GROUPS * Hd), BF16)],
        args=[proj, pool_w, pool_scale])


def pool_bwd(proj, da, pool_w, pool_scale, name, hosted=()):
    T = proj.shape[0]
    Hd = HEAD_DIM

    def body(x_ref, da_ref, w_ref, sc_ref, dx_ref, dw_ref, dsc_ref):
        g = pl.program_id(0)
        x = x_ref[...]
        cnt = _pool_count(g, x.shape)
        pooled = (_pool_window(x, g, T, True) / cnt - x).astype(BF16)
        wb = w_ref[0].astype(BF16)
        dav = da_ref[...]
        dsc_ref[...] = jnp.sum(dav * _dot(pooled, wb), axis=0, keepdims=True)
        dout = (dav * sc_ref[...]).astype(BF16)
        dw_ref[0] = _dot_tn(pooled, dout)
        dpooled = _dot_nt(dout, wb)
        dx_ref[...] = (_pool_window(dpooled / cnt, g, T, False) - dpooled).astype(BF16)

    col_spec = pl.BlockSpec((T, Hd), lambda g: (0, g))
    return _call(
        body, hosted, name=name, grid=(N_GROUPS,),
        in_specs=[col_spec, col_spec, pl.BlockSpec((1, Hd, Hd), lambda g: (g, 0, 0)), pl.BlockSpec((1, Hd), lambda g: (0, g))],
        out_specs=[col_spec, pl.BlockSpec((1, Hd, Hd), lambda g: (g, 0, 0)), pl.BlockSpec((1, Hd), lambda g: (0, g))],
        out_shape=[jax.ShapeDtypeStruct((T, N_GROUPS * Hd), BF16), jax.ShapeDtypeStruct((N_GROUPS, Hd, Hd), F32),
                   jax.ShapeDtypeStruct((1, N_GROUPS * Hd), F32)],
        args=[proj, da, pool_w, pool_scale])


def _ret_tables(T):
    Hd, C = HEAD_DIM, RET_CHUNK
    inv_freq = 1.0 / (ROPE_BASE ** (jnp.arange(0, Hd, 2, dtype=F32) / Hd))
    ang = jnp.arange(T, dtype=F32)[:, None] * inv_freq[None, :]
    cos, sin = jnp.cos(ang), jnp.sin(ang)
    cos2 = jnp.concatenate([cos, cos], axis=-1)
    sin2 = jnp.concatenate([-sin, sin], axis=-1)
    log_gamma = jnp.log1p(-jnp.exp2(-5.0 - jnp.arange(N_GROUPS, dtype=F32)))
    pos = jnp.arange(C, dtype=F32)
    rel = pos[:, None] - pos[None, :]
    intra = jnp.where(rel[None] >= 0, jnp.exp(log_gamma[:, None, None] * jnp.maximum(rel, 0.0)[None]), 0.0)
    k_tail = jnp.exp(log_gamma[:, None] * (C - 1 - pos)[None, :])
    q_head = jnp.exp(log_gamma[:, None] * (pos + 1.0)[None, :])
    chunk_decay = jnp.exp(log_gamma * C)
    wide = lambda t: jnp.broadcast_to(t[:, :, None], (N_GROUPS, C, Hd))
    return cos2, sin2, intra, wide(k_tail), wide(q_head), jnp.broadcast_to(chunk_decay[:, None, None], (N_GROUPS, 1, Hd))


def _rope(x, cos2, sin2):
    return x * cos2 + pltpu.roll(x, HEAD_DIM // 2, 1) * sin2


def _rope_t(d, cos2, sin2):
    return d * cos2 + pltpu.roll(d * sin2, HEAD_DIM // 2, 1)


def _ret_specs(tseg, seg_of):
    Hd, G = HEAD_DIM, N_GROUPS
    col = lambda kind: pl.BlockSpec((tseg, Hd), lambda h, s: (seg_of(s), G * kind + h))
    tab = pl.BlockSpec((tseg, Hd), lambda h, s: (seg_of(s), 0))
    head = pl.BlockSpec((1, RET_CHUNK, Hd), lambda h, s: (h, 0, 0))
    cd = pl.BlockSpec((1, 1, Hd), lambda h, s: (h, 0, 0))
    gain = pl.BlockSpec((1, Hd), lambda h, s: (0, h))
    return col, tab, head, cd, gain


def ret_fwd(proj, ret_norm, tables, name, hosted=()):
    T = proj.shape[0]
    Hd, C, G = HEAD_DIM, RET_CHUNK, N_GROUPS
    tseg = min(T, 1024)
    nseg, nck = T // tseg, tseg // C
    scale = Hd ** -0.5
    cos2, sin2, intra, k_tail, q_head, chunk_decay = tables

    def body(q_ref, k_ref, v_ref, g_ref, gain_ref, cos_ref, sin_ref, m_ref, kt_ref, qh_ref, cd_ref,
             b_ref, o_ref, rp_ref, state):
        @pl.when(pl.program_id(1) == 0)
        def _():
            state[...] = jnp.zeros_like(state)

        def chunk(ci, carry):
            rows = pl.ds(pl.multiple_of(ci * C, C), C)
            cos, sin = cos_ref[rows, :], sin_ref[rows, :]
            qr = _rope(q_ref[rows, :], cos, sin)
            kr = _rope(k_ref[rows, :], cos, sin) * scale
            qb, kb, vb = qr.astype(BF16), kr.astype(BF16), v_ref[rows, :].astype(BF16)
            r = state[...]
            rp_ref[0, ci] = r.astype(BF16)
            sc = _dot_nt(qb, kb) * m_ref[0]
            o = _dot(sc.astype(BF16), vb) + _dot((qr * qh_ref[0]).astype(BF16), r.astype(BF16))
            state[...] = cd_ref[0] * r + _dot_tn((kr * kt_ref[0]).astype(BF16), vb)
            o_ref[rows, :] = o
            on = o * _rstd(o)
            b_ref[rows, :] = (jax.nn.silu(g_ref[rows, :]) * (on * gain_ref[...])).astype(BF16)
            return carry

        lax.fori_loop(0, nck, chunk, 0)

    col, tab, head, cd, gain = _ret_specs(tseg, lambda s: s)
    out_col = pl.BlockSpec((tseg, Hd), lambda h, s: (s, h))
    return _call(
        body, hosted, name=name, grid=(G, nseg),
        in_specs=[col(1), col(2), col(3), col(4), gain, tab, tab, head, head, head, cd],
        out_specs=[out_col, out_col, pl.BlockSpec((1, nck, Hd, Hd), lambda h, s: (h, s, 0, 0))],
        out_shape=[jax.ShapeDtypeStruct((T, G * Hd), BF16), jax.ShapeDtypeStruct((T, G * Hd), F32),
                   jax.ShapeDtypeStruct((G, T // C, Hd, Hd), BF16)],
        scratch_shapes=[pltpu.VMEM((Hd, Hd), F32)],
        args=[proj, proj, proj, proj, ret_norm, cos2, sin2, intra, k_tail, q_head, chunk_decay])


def ret_bwd(proj, db, o_pre, r_prev, ret_norm, tables, name, hosted=()):
    T = proj.shape[0]
    Hd, C, G = HEAD_DIM, RET_CHUNK, N_GROUPS
    tseg = min(T, 1024)
    nseg, nck = T // tseg, tseg // C
    scale = Hd ** -0.5
    cos2, sin2, intra, k_tail, q_head, chunk_decay = tables

    def body(q_ref, k_ref, v_ref, g_ref, db_ref, o_ref, rp_ref, gain_ref, cos_ref, sin_ref, m_ref, kt_ref, qh_ref, cd_ref,
             d_ref, dgain_ref, gstate):
        @pl.when(pl.program_id(1) == 0)
        def _():
            gstate[...] = jnp.zeros_like(gstate)
            dgain_ref[...] = jnp.zeros_like(dgain_ref)

        def chunk(t, carry):
            ci = nck - 1 - t
            rows = pl.ds(pl.multiple_of(ci * C, C), C)
            cos, sin = cos_ref[rows, :], sin_ref[rows, :]
            qr = _rope(q_ref[rows, :], cos, sin)
            kr = _rope(k_ref[rows, :], cos, sin) * scale
            qb, kb, vb = qr.astype(BF16), kr.astype(BF16), v_ref[rows, :].astype(BF16)
            qhb, ktb = (qr * qh_ref[0]).astype(BF16), (kr * kt_ref[0]).astype(BF16)
            sc = (_dot_nt(qb, kb) * m_ref[0]).astype(BF16)
            o = o_ref[rows, :]
            rstd = _rstd(o)
            on = o * rstd
            gain = gain_ref[...]
            silu, dsilu = _silu_parts(g_ref[rows, :])
            dy = db_ref[rows, :]
            dgain_ref[...] += jnp.sum(dy * silu * on, axis=0, keepdims=True)
            dg = dy * on * gain * dsilu
            don = dy * silu * gain
            dob = (rstd * (don - on * jnp.mean(don * on, axis=-1, keepdims=True))).astype(BF16)
            gn = gstate[...]
            gb = gn.astype(BF16)
            da = (_dot_nt(dob, vb) * m_ref[0]).astype(BF16)
            dq = _dot(da, kb) + _dot_nt(dob, rp_ref[0, ci]) * qh_ref[0]
            dk = _dot_tn(da, qb) + _dot_nt(vb, gb) * kt_ref[0]
            dv = _dot_tn(sc, dob) + _dot(ktb, gb)
            gstate[...] = cd_ref[0] * gn + _dot_tn(qhb, dob)
            d_ref[0, rows, :] = _rope_t(dq, cos, sin).astype(BF16)
            d_ref[1, rows, :] = _rope_t(dk * scale, cos, sin).astype(BF16)
            d_ref[2, rows, :] = dv.astype(BF16)
            d_ref[3, rows, :] = dg.astype(BF16)
            return carry

        lax.fori_loop(0, nck, chunk, 0)

    rev = lambda s: nseg - 1 - s
    col, tab, head, cd, gain = _ret_specs(tseg, rev)
    act = pl.BlockSpec((tseg, Hd), lambda h, s: (rev(s), h))
    return _call(
        body, hosted, name=name, grid=(G, nseg),
        in_specs=[col(1), col(2), col(3), col(4), act, act, pl.BlockSpec((1, nck, Hd, Hd), lambda h, s: (h, rev(s), 0, 0)),
                  gain, tab, tab, head, head, head, cd],
        out_specs=[pl.BlockSpec((4, tseg, Hd), lambda h, s: (0, rev(s), h)), gain],
        out_shape=[jax.ShapeDtypeStruct((4, T, G * Hd), BF16), jax.ShapeDtypeStruct((1, G * Hd), F32)],
        scratch_shapes=[pltpu.VMEM((Hd, Hd), F32)],
        args=[proj, proj, proj, proj, db, o_pre, r_prev, ret_norm, cos2, sin2, intra, k_tail, q_head, chunk_decay])


def final_loss(h, gain, target, name, hosted=()):
    T, D = h.shape
    tm = min(T, 512)

    def body(h_ref, g_ref, t_ref, dh_ref, loss_ref, dg_ref):
        @pl.when(pl.program_id(0) == 0)
        def _():
            loss_ref[...] = jnp.zeros_like(loss_ref)
            dg_ref[...] = jnp.zeros_like(dg_ref)

        hh = h_ref[...]
        gain_v = g_ref[...]
        err = hh * _rstd(hh) * gain_v - t_ref[...]
        loss_ref[...] += 0.5 * jnp.sum(jnp.mean(err * err, axis=-1, keepdims=True), axis=0, keepdims=True)
        dhn, dg = _rmsnorm_bwd(err * (1.0 / D), hh, gain_v)
        dh_ref[...] = dhn
        dg_ref[...] += jnp.sum(dg, axis=0, keepdims=True)

    row_spec = pl.BlockSpec((tm, D), lambda i: (i, 0))
    vec_spec = pl.BlockSpec((1, D), lambda i: (0, 0))
    return _call(
        body, hosted, name=name, grid=(T // tm,),
        in_specs=[row_spec, vec_spec, row_spec],
        out_specs=[row_spec, pl.BlockSpec((1, 128), lambda i: (0, 0)), vec_spec],
        out_shape=[jax.ShapeDtypeStruct((T, D), F32), jax.ShapeDtypeStruct((1, 128), F32), jax.ShapeDtypeStruct((1, D), F32)],
        args=[h, gain, target])


def prereduce(grad, recv, place, name):
    nsh, R, C = grad.shape
    rh = R // 2

    def body(place_ref, g_ref, r_ref, o_ref, own_ref):
        piece = (g_ref[...].astype(F32) + r_ref[...].astype(F32)).astype(BF16)
        o_ref[...] = piece

        @pl.when(pl.program_id(0) == place_ref[1])
        def _():
            own_ref[...] = piece

    return pl.pallas_call(
        body, name=name,
        grid_spec=pltpu.PrefetchScalarGridSpec(
            num_scalar_prefetch=1, grid=(nsh,),
            in_specs=[pl.BlockSpec((1, rh, C), lambda j, p: (j, p[0], 0)), pl.BlockSpec((1, rh, C), lambda j, p: (j, 0, 0))],
            out_specs=[pl.BlockSpec((1, rh, C), lambda j, p: (j, 0, 0)), pl.BlockSpec((1, rh, C), lambda j, p: (p[1], p[0], 0))]),
        out_shape=[jax.ShapeDtypeStruct((nsh, rh, C), BF16), jax.ShapeDtypeStruct((nsh, R, C), BF16)],
        compiler_params=pltpu.CompilerParams(vmem_limit_bytes=VMEM_LIMIT_V7X),
    )(place, grad, recv)


def _adamw(w, g, m, v):
    m = ADAM_B1 * m + (1.0 - ADAM_B1) * g
    v = ADAM_B2 * v + (1.0 - ADAM_B2) * (g * g)
    m_hat = m / (1.0 - ADAM_B1 ** ADAM_STEP)
    v_hat = v / (1.0 - ADAM_B2 ** ADAM_STEP)
    return -ADAM_LR * (m_hat / (jnp.sqrt(v_hat) + ADAM_EPS) + ADAM_WD * w), m, v


def adamw_sharded(tensors, name, hosted=()):
    nt = len(tensors)
    nsh, R, C = tensors[0][0].shape
    lanes = -(-C // 128) * 128
    per_row = 2 * nt * lanes * (nsh * 2 + 7 * 4)
    tr = max(r for r in range(16, R + 1, 16) if R % r == 0 and r * per_row <= ADAMW_VMEM_BUDGET)

    def body(*refs):
        ins, outs = refs[:4 * nt], refs[4 * nt:]
        for t in range(nt):
            p_ref, w_ref, m_ref, v_ref = ins[4 * t:4 * t + 4]
            g_ref, d_ref, nm_ref, nv_ref = outs[4 * t:4 * t + 4]
            g = p_ref[0].astype(F32)
            for i in range(1, nsh):
                g += p_ref[i].astype(F32)
            g_ref[...] = g
            d_ref[...], nm_ref[...], nv_ref[...] = _adamw(w_ref[...], g, m_ref[...], v_ref[...])

    spec = pl.BlockSpec((tr, C), lambda i: (i, 0))
    out = jax.ShapeDtypeStruct((R, C), F32)
    return _call(
        body, hosted, name=name, grid=(R // tr,),
        in_specs=[pl.BlockSpec((nsh, tr, C), lambda i: (0, i, 0)), spec, spec, spec] * nt,
        out_specs=[spec] * (4 * nt), out_shape=[out] * (4 * nt),
        args=[a for tensor in tensors for a in tensor])


def adamw_small(packs, w, m, v, name):
    ndev, R, L = packs.shape

    def body(p_ref, w_ref, m_ref, v_ref, g_ref, d_ref, nm_ref, nv_ref):
        g = p_ref[0]
        for i in range(1, ndev):
            g += p_ref[i]
        g_ref[...] = g
        d_ref[...], nm_ref[...], nv_ref[...] = _adamw(w_ref[...], g, m_ref[...], v_ref[...])

    out = jax.ShapeDtypeStruct((R, L), F32)
    return pl.pallas_call(body, name=name, out_shape=[out] * 4,
                          compiler_params=pltpu.CompilerParams(vmem_limit_bytes=VMEM_LIMIT_V7X))(packs, w, m, v)


BIG = ("ffn1_w1", "ffn1_w3", "ffn1_w2", "w_in", "w_out", "ffn2_w1", "ffn2_w3", "ffn2_w2")
TRANSPOSED = ("ffn1_w1", "ffn1_w3", "ffn2_w1", "ffn2_w3")
SMALL = ("ffn1_norm", "mix_norm", "pool_w", "pool_scale", "ret_norm", "ffn2_norm", "final_norm")
WEIGHTS = ("ffn1_norm", "ffn1_w1", "ffn1_w3", "ffn1_w2", "mix_norm", "w_in", "pool_w", "pool_scale", "ret_norm", "w_out",
           "ffn2_norm", "ffn2_w1", "ffn2_w3", "ffn2_w2", "final_norm")


def _pack(parts):
    return jnp.concatenate([parts[k].reshape(-1, 128) for k in SMALL], axis=0)


def _unpack(pack, like):
    out, row = {}, 0
    for k in SMALL:
        rows = like[k].size // 128
        out[k] = pack[row:row + rows].reshape(like[k].shape)
        row += rows
    return out


def kernel(x, ffn1_norm, ffn1_w1, ffn1_w3, ffn1_w2, mix_norm, w_in, pool_w, pool_scale, ret_norm, w_out, ffn2_norm, ffn2_w1, ffn2_w3, ffn2_w2, final_norm, loss_target, m_ffn1_norm, m_ffn1_w1, m_ffn1_w3, m_ffn1_w2, m_mix_norm, m_w_in, m_pool_w, m_pool_scale, m_ret_norm, m_w_out, m_ffn2_norm, m_ffn2_w1, m_ffn2_w3, m_ffn2_w2, m_final_norm, v_ffn1_norm, v_ffn1_w1, v_ffn1_w3, v_ffn1_w2, v_mix_norm, v_w_in, v_pool_w, v_pool_scale, v_ret_norm, v_w_out, v_ffn2_norm, v_ffn2_w1, v_ffn2_w3, v_ffn2_w2, v_final_norm):
    w = dict(ffn1_norm=ffn1_norm, ffn1_w1=ffn1_w1, ffn1_w3=ffn1_w3, ffn1_w2=ffn1_w2, mix_norm=mix_norm, w_in=w_in, pool_w=pool_w,
             pool_scale=pool_scale, ret_norm=ret_norm, w_out=w_out, ffn2_norm=ffn2_norm, ffn2_w1=ffn2_w1, ffn2_w3=ffn2_w3,
             ffn2_w2=ffn2_w2, final_norm=final_norm)
    m = dict(ffn1_norm=m_ffn1_norm, ffn1_w1=m_ffn1_w1, ffn1_w3=m_ffn1_w3, ffn1_w2=m_ffn1_w2, mix_norm=m_mix_norm, w_in=m_w_in,
             pool_w=m_pool_w, pool_scale=m_pool_scale, ret_norm=m_ret_norm, w_out=m_w_out, ffn2_norm=m_ffn2_norm, ffn2_w1=m_ffn2_w1,
             ffn2_w3=m_ffn2_w3, ffn2_w2=m_ffn2_w2, final_norm=m_final_norm)
    v = dict(ffn1_norm=v_ffn1_norm, ffn1_w1=v_ffn1_w1, ffn1_w3=v_ffn1_w3, ffn1_w2=v_ffn1_w2, mix_norm=v_mix_norm, w_in=v_w_in,
             pool_w=v_pool_w, pool_scale=v_pool_scale, ret_norm=v_ret_norm, w_out=v_w_out, ffn2_norm=v_ffn2_norm, ffn2_w1=v_ffn2_w1,
             ffn2_w3=v_ffn2_w3, ffn2_w2=v_ffn2_w2, final_norm=v_final_norm)
    xs, target = x[0], loss_target[0]
    T = xs.shape[0]
    tables = _ret_tables(T)
    place = jnp.stack([lax.axis_index("c"), 2 * lax.axis_index("x") + lax.axis_index("y")]).astype(jnp.int32)
    local = lambda d, k: jnp.transpose(d[k][0]) if k in TRANSPOSED else d[k][0]
    result = lambda o, k: jnp.transpose(o)[None] if k in TRANSPOSED else o[None]
    sh = {k: local(w, k).astype(BF16) for k in BIG}
    gather = lambda *names: [ChipExchange([sh[k] for k in names], False)]
    wg, grad, delta, new_m, new_v = {}, {}, {}, {}, {}

    def update(names, pieces, name, hosted=()):
        outs, extras = adamw_sharded([(p, local(w, k), local(m, k), local(v, k)) for k, p in zip(names, pieces)], name, hosted)
        for t, k in enumerate(names):
            grad[k], delta[k], new_m[k], new_v[k] = [result(o, k) for o in outs[4 * t:4 * t + 4]]
        return extras

    def reduce_in_chip(name, partial, recv):
        return prereduce(partial, recv, place, "prereduce_" + name)

    scatter = lambda *reduced: ChipExchange([r[0] for r in reduced], True, [r[1] for r in reduced])

    (wg["ffn1_w1"], wg["ffn1_w3"]), = exchange(gather("ffn1_w1", "ffn1_w3"), "gather_ffn1")
    (n1, ga1, gb1, s1), ((wg["ffn1_w2"], wg["w_in"]),) = ffn_up(
        xs, ffn1_norm, wg["ffn1_w1"], wg["ffn1_w3"], "ffn1_up", gather("ffn1_w2", "w_in"))
    (h1,), ((wg["w_out"],),) = ffn_down(s1, wg["ffn1_w2"], xs, "ffn1_down", gather("w_out"))
    (u, proj), ((wg["ffn2_w1"],),) = mix_in(h1, mix_norm, wg["w_in"], "mix_in", gather("ffn2_w1"))
    (pa,), _ = pool_fwd(proj, pool_w[0], pool_scale, "pool_fwd")
    (rb, o_pre, r_prev), ((wg["ffn2_w3"],),) = ret_fwd(proj, ret_norm, tables, "ret_fwd", gather("ffn2_w3"))
    (h2,), _ = mix_out(pa, rb, wg["w_out"], h1, "mix_out")
    (n2, ga2, gb2, s2), ((wg["ffn2_w2"],),) = ffn_up(
        h2, ffn2_norm, wg["ffn2_w1"], wg["ffn2_w3"], "ffn2_up", gather("ffn2_w2"))
    (h3,), _ = ffn_down(s2, wg["ffn2_w2"], h2, "ffn2_down")
    (dh3, loss, d_final), _ = final_loss(h3, final_norm[None], target, "final_loss")
    loss = lax.psum(loss[0, 0], ("x", "y", "c"))

    (da2, db2, df2), _ = ffn_bwd_act(dh3, wg["ffn2_w2"], ga2, gb2, "ffn2_bwd_act")
    (g_f2w2,), _ = ffn_dw2(s2, df2, "ffn2_dw2")
    (g_f2w1, g_f2w3), ((r_f2w2,),) = ffn_dw13(n2, da2, db2, "ffn2_dw13", [SiblingExchange([g_f2w2])])
    p_f2w2 = reduce_in_chip("ffn2_w2", g_f2w2, r_f2w2)
    (dh2, d_ffn2), ((q_f2w2,), (r_f2w1, r_f2w3)) = ffn_bwd_in(
        da2, db2, wg["ffn2_w1"], wg["ffn2_w3"], h2, ffn2_norm, dh3, "ffn2_bwd_in",
        [scatter(p_f2w2), SiblingExchange([g_f2w1, g_f2w3])])
    p_f2w1 = reduce_in_chip("ffn2_w1", g_f2w1, r_f2w1)
    p_f2w3 = reduce_in_chip("ffn2_w3", g_f2w3, r_f2w3)
    (dpa, drb, g_wout), _ = mix_out_bwd(dh2, wg["w_out"], pa, rb, "mix_out_bwd")
    (dpool, d_pool_w, d_pool_scale), _ = pool_bwd(proj, dpa, pool_w[0], pool_scale, "pool_bwd")
    (dqkvg, d_ret_norm), ((q_f2w1, q_f2w3), (r_wout,)) = ret_bwd(
        proj, drb, o_pre, r_prev, ret_norm, tables, "ret_bwd", [scatter(p_f2w1, p_f2w3), SiblingExchange([g_wout])])
    p_wout = reduce_in_chip("w_out", g_wout, r_wout)
    d = jnp.concatenate([dpool, dqkvg[0], dqkvg[1], dqkvg[2], dqkvg[3]], axis=1)
    (g_win,), ((q_wout,),) = mix_dwin(u, d, N_CHIPS, "mix_dwin", [scatter(p_wout)])
    (dh1, d_mix), ((r_win,),) = mix_in_bwd(d, wg["w_in"], h1, mix_norm, dh2, "mix_in_bwd", [SiblingExchange([g_win])])
    p_win = reduce_in_chip("w_in", g_win, r_win)
    (da1, db1, df1), ((q_win,),) = ffn_bwd_act(dh1, wg["ffn1_w2"], ga1, gb1, "ffn1_bwd_act", [scatter(p_win)])
    (g_f1w1, g_f1w3), _ = ffn_dw13(n1, da1, db1, "ffn1_dw13")
    (g_f1w2,), ((r_f1w1, r_f1w3),) = ffn_dw2(s1, df1, "ffn1_dw2", [SiblingExchange([g_f1w1, g_f1w3])])
    p_f1w1 = reduce_in_chip("ffn1_w1", g_f1w1, r_f1w1)
    p_f1w3 = reduce_in_chip("ffn1_w3", g_f1w3, r_f1w3)
    (dx, d_ffn1), ((q_f1w1, q_f1w3), (r_f1w2,)) = ffn_bwd_in(
        da1, db1, wg["ffn1_w1"], wg["ffn1_w3"], xs, ffn1_norm, dh1, "ffn1_bwd_in",
        [scatter(p_f1w1, p_f1w3), SiblingExchange([g_f1w2])])
    p_f1w2 = reduce_in_chip("ffn1_w2", g_f1w2, r_f1w2)

    small = {"ffn1_norm": d_ffn1, "mix_norm": d_mix, "pool_w": d_pool_w, "pool_scale": d_pool_scale,
             "ret_norm": d_ret_norm, "ffn2_norm": d_ffn2, "final_norm": d_final}
    (q_f1w2,), (packs,) = update(["ffn2_w1", "ffn2_w3", "ffn1_w1", "ffn1_w3"], [q_f2w1, q_f2w3, q_f1w1, q_f1w3], "adamw_w13",
                                 [scatter(p_f1w2), AllExchange(_pack(small))])
    update(["ffn2_w2", "ffn1_w2"], [q_f2w2, q_f1w2], "adamw_w2")
    update(["w_in"], [q_win], "adamw_w_in")
    update(["w_out"], [q_wout], "adamw_w_out")
    outs = adamw_small(packs, _pack(w), _pack(m), _pack(v), "adamw_small")
    for res, pack in zip((grad, delta, new_m, new_v), outs):
        res.update(_unpack(pack, w))

    return (loss, dx[None], *[grad[k] for k in WEIGHTS], *[delta[k] for k in WEIGHTS],
            *[new_m[k] for k in WEIGHTS], *[new_v[k] for k in WEIGHTS])
```

```python
import math

import jax
import jax.numpy as jnp
from jax import lax
from jax.experimental import pallas as pl
from jax.experimental.pallas import tpu as pltpu

F32 = jnp.float32
BF16 = jnp.bfloat16

EPS = 1e-6
N_CHIPS = 4
N_GROUPS = 4
HEAD_DIM = 128
RET_CHUNK = 128
ROPE_BASE = 10000.0
ADAM_LR, ADAM_B1, ADAM_B2, ADAM_EPS, ADAM_WD, ADAM_STEP = 0.001, 0.9, 0.999, 1e-08, 0.01, 10
VMEM_LIMIT_V7X = 56 * 1024 * 1024
ADAMW_VMEM_BUDGET = 32 * 1024 * 1024
MESH = pl.DeviceIdType.MESH
ANY = pl.BlockSpec(memory_space=pl.ANY)


def _dot(a, b):
    return jnp.dot(a, b, preferred_element_type=F32)


def _dot_nt(a, b):
    return lax.dot_general(a, b, (((1,), (1,)), ((), ())), preferred_element_type=F32)


def _dot_tn(a, b):
    return lax.dot_general(a, b, (((0,), (0,)), ((), ())), preferred_element_type=F32)


def _rstd(h):
    return lax.rsqrt(jnp.mean(h * h, axis=-1, keepdims=True) + EPS)


def _rmsnorm_bwd(dn, h, gain):
    r = _rstd(h)
    nh = h * r
    dnh = dn * gain
    dh = r * (dnh - nh * jnp.mean(dnh * nh, axis=-1, keepdims=True))
    return dh, dn * nh


def _silu_parts(a):
    sig = jax.nn.sigmoid(a)
    silu = a * sig
    return silu, sig + silu * (1.0 - sig)


def _mesh_pos():
    return lax.axis_index("x"), lax.axis_index("y"), lax.axis_index("c")


class ChipExchange:
    def __init__(self, srcs, scatter, placed=()):
        n = len(srcs)
        self.inputs, self.scatter, self.n, self.reach = list(srcs) + list(placed), scatter, n, REACH_CHIPS
        self.aliases = {n + t: t for t in range(n)} if scatter else {}
        self.half_rows = [s.shape[1] if scatter else s.shape[0] // 2 for s in srcs]
        self.out_shape = [jax.ShapeDtypeStruct((N_CHIPS, 2 * rh, s.shape[-1]), s.dtype) for s, rh in zip(srcs, self.half_rows)]
        if scatter:
            self.out_shape += [jax.ShapeDtypeStruct((2, rh // 2, s.shape[-1]), s.dtype) for s, rh in zip(srcs, self.half_rows)]
        dma = pltpu.SemaphoreType.DMA
        self.sems = [dma((4 * n,)), dma((4 * n,)), dma((2 * n,)), dma((2 * n,)), dma((4 * n,)), dma((4 * n,))]

    def _copies(self, src, out, sems):
        hop1_send, hop1_recv, hop2_send, hop2_recv, d2d_send, d2d_recv = sems
        x, y, c = _mesh_pos()
        me, dg = 2 * x + y, 2 * (1 - x) + (1 - y)
        sibling = (x, y, 1 - c)
        n = self.n
        mine, theirs = c, 1 - c

        def nb(a):
            nx, ny = x ^ (1 - a), y ^ a
            return 2 * nx + ny, (nx, ny, c)

        def remote(s, d, send, recv, k, to):
            return pltpu.make_async_remote_copy(src_ref=s, dst_ref=d, send_sem=send.at[k], recv_sem=recv.at[k],
                                                device_id=to, device_id_type=MESH)

        class Copies:
            def slot(_, t, chip, half):
                rh = self.half_rows[t]
                return out[t].at[chip, pl.ds(half * rh, rh), :]

            def quarter(_, t, chip, q):
                qh = self.half_rows[t] // 2
                return out[t].at[chip, pl.ds(mine * 2 * qh + q * qh, qh), :]

            def own_shard(k, t):
                return remote(src[t], out[t].at[me], d2d_send, d2d_recv, 4 * t + 3, sibling)

            def hop1(k, t, a, transit=False):
                rh = self.half_rows[t]
                chip, to = nb(a)
                if transit:
                    piece = src[t].at[dg, pl.ds(a * (rh // 2), rh // 2), :]
                    return remote(piece, out[n + t].at[a], hop1_send, hop1_recv, 4 * t + 2 + a, to)
                piece = src[t].at[chip] if self.scatter else src[t].at[pl.ds(mine * rh, rh), :]
                return remote(piece, k.slot(t, me, mine), hop1_send, hop1_recv, 4 * t + a, to)

            def landed1(k, t, a, transit=False):
                here = out[n + t].at[a] if transit else k.slot(t, nb(a)[0], mine)
                return remote(here, here, hop1_send, hop1_recv, 4 * t + (2 if transit else 0) + a, sibling)

            def hop2(k, t, q):
                origin, to = nb(q)[0], nb(1 - q)[1]
                piece = out[n + t].at[q] if self.scatter else k.quarter(t, origin, q)
                return remote(piece, k.quarter(t, origin, q), hop2_send, hop2_recv, 2 * t + q, to)

            def landed2(k, t, q):
                here = k.quarter(t, dg, q)
                return remote(here, here, hop2_send, hop2_recv, 2 * t + q, sibling)

            def d2d(k, t, p, chip, own=False, arriving=False):
                if arriving:
                    there = k.slot(t, chip, theirs)
                    return remote(there, there, d2d_send, d2d_recv, 4 * t + p, sibling)
                piece = src[t].at[me] if own else k.slot(t, chip, mine)
                return remote(piece, k.slot(t, chip, mine), d2d_send, d2d_recv, 4 * t + p, sibling)

        return Copies(), nb, me, dg, c

    def start(self, src, out, sems):
        k, nb, me, dg, c = self._copies(src, out, sems)
        for t in range(self.n):
            for first in range(2):
                a = first ^ c
                k.hop1(t, a).start()
                if self.scatter:
                    k.hop1(t, a, transit=True).start()
            if self.scatter:
                k.d2d(t, 3, me, own=True).start()
            else:
                k.own_shard(t).start()

    def mid(self, src, out, sems):
        k, nb, me, dg, c = self._copies(src, out, sems)
        for t in range(self.n):
            for first in range(2):
                a = first ^ c
                if self.scatter:
                    k.landed1(t, a, transit=True).wait_recv()
                    k.hop2(t, a).start()
                k.landed1(t, a).wait_recv()
                if not self.scatter:
                    k.hop2(t, a).start()
                k.d2d(t, a, nb(a)[0]).start()

    def finish(self, src, out, sems):
        k, nb, me, dg, c = self._copies(src, out, sems)
        for t in range(self.n):
            for q in range(2):
                k.landed2(t, q).wait_recv()
            k.d2d(t, 2, dg).start()
        for t in range(self.n):
            for a in range(2):
                k.d2d(t, a, nb(a)[0], arriving=True).wait_recv()
            k.d2d(t, 2, dg, arriving=True).wait_recv()
            if self.scatter:
                k.d2d(t, 3, me, arriving=True).wait_recv()
        for t in range(self.n):
            for a in range(2):
                k.hop1(t, a).wait_send()
                if self.scatter:
                    k.hop1(t, a, transit=True).wait_send()
                k.hop2(t, a).wait_send()
                k.d2d(t, a, nb(a)[0]).wait_send()
            k.d2d(t, 2, dg).wait_send()
            if self.scatter:
                k.d2d(t, 3, me, own=True).wait_send()
            else:
                k.own_shard(t).wait()


class SiblingExchange:
    def __init__(self, grads):
        self.inputs, self.n, self.aliases, self.reach = list(grads), len(grads), {}, REACH_SIBLING
        self.half_rows = [g.shape[1] // 2 for g in grads]
        self.out_shape = [jax.ShapeDtypeStruct((g.shape[0], rh, g.shape[2]), g.dtype) for g, rh in zip(grads, self.half_rows)]
        self.sems = [pltpu.SemaphoreType.DMA((self.n,)), pltpu.SemaphoreType.DMA((self.n,))]

    def _plan(self, src, out, sems):
        x, y, c = _mesh_pos()
        return [pltpu.make_async_remote_copy(
            src_ref=src[t].at[:, pl.ds((1 - c) * self.half_rows[t], self.half_rows[t]), :], dst_ref=out[t],
            send_sem=sems[0].at[t], recv_sem=sems[1].at[t], device_id=(x, y, 1 - c), device_id_type=MESH) for t in range(self.n)]

    def start(self, src, out, sems):
        for cp in self._plan(src, out, sems):
            cp.start()

    def mid(self, src, out, sems):
        pass

    def finish(self, src, out, sems):
        for cp in self._plan(src, out, sems):
            cp.wait()


REACH_SIBLING, REACH_CHIPS, REACH_ALL = 0, 1, 2


def _entry_barrier(reach):
    x, y, c = _mesh_pos()
    peers = [(x, y, 1 - c)]
    if reach == REACH_CHIPS:
        peers += [(1 - x, y, c), (x, 1 - y, c)]
    elif reach == REACH_ALL:
        peers = [(x ^ dx, y ^ dy, c ^ dc) for dx in (0, 1) for dy in (0, 1) for dc in (0, 1)][1:]
    barrier = pltpu.get_barrier_semaphore()
    for peer in peers:
        pl.semaphore_signal(barrier, inc=1, device_id=peer, device_id_type=MESH)
    pl.semaphore_wait(barrier, len(peers))


def _call(body, hosted=(), *, name, in_specs, out_specs, out_shape, args, grid=(), scratch_shapes=()):
    n_in, n_out, n_scr = len(in_specs), len(out_specs), len(scratch_shapes)
    total = math.prod(grid)
    mid_step = max(0, (5 * total) // 8 - 1)

    def full(*refs):
        pos = [0]

        def take(k):
            pos[0] += k
            return refs[pos[0] - k:pos[0]]

        ins, h_in = take(n_in), [take(len(h.inputs)) for h in hosted]
        outs, h_out = take(n_out), [take(len(h.out_shape)) for h in hosted]
        scr, h_sem = take(n_scr), [take(len(h.sems)) for h in hosted]
        step = 0
        for axis, size in enumerate(grid):
            step = step * size + pl.program_id(axis)

        def phase(at, method):
            if not hosted:
                return

            def run():
                if method == "start":
                    _entry_barrier(reach)
                for h, s, o, m in zip(hosted, h_in, h_out, h_sem):
                    getattr(h, method)(s, o, m)

            if total == 1:
                run()
            else:
                pl.when(step == at)(run)

        phase(0, "start")
        body(*ins, *outs, *scr)
        phase(mid_step, "mid")
        phase(total - 1, "finish")

    aliases, i0, o0 = {}, n_in, n_out
    for h in hosted:
        aliases.update({i0 + i: o0 + o for i, o in h.aliases.items()})
        i0, o0 = i0 + len(h.inputs), o0 + len(h.out_shape)
    reach = max((h.reach for h in hosted), default=None)
    params = dict(vmem_limit_bytes=VMEM_LIMIT_V7X)
    if hosted:
        params["collective_id"] = reach
    results = pl.pallas_call(
        full, name=name, grid=grid,
        in_specs=list(in_specs) + [ANY] * (i0 - n_in),
        out_specs=list(out_specs) + [ANY] * (o0 - n_out),
        out_shape=list(out_shape) + [s for h in hosted for s in h.out_shape],
        scratch_shapes=list(scratch_shapes) + [s for h in hosted for s in h.sems],
        input_output_aliases=aliases,
        compiler_params=pltpu.CompilerParams(**params),
    )(*args, *[s for h in hosted for s in h.inputs])
    outs, extras, pos = list(results[:n_out]), [], n_out
    for h in hosted:
        extras.append(list(results[pos:pos + h.n]))
        pos += len(h.out_shape)
    return outs, extras


def exchange(hosted, name):
    return _call(lambda: None, hosted, name=name, in_specs=[], out_specs=[], out_shape=[], args=[])[1]


class AllExchange:
    def __init__(self, pack):
        self.inputs, self.n, self.aliases, self.reach = [pack], 1, {}, REACH_ALL
        self.out_shape = [jax.ShapeDtypeStruct((2 * N_CHIPS,) + pack.shape, pack.dtype)]
        self.sems = [pltpu.SemaphoreType.DMA, pltpu.SemaphoreType.DMA((7,)), pltpu.SemaphoreType.DMA((7,))]

    def _copies(self, src, out, sems):
        local_sem, send_sem, recv_sem = sems
        x, y, c = _mesh_pos()
        flips = [(dx, dy, dc) for dx in (0, 1) for dy in (0, 1) for dc in (0, 1)][1:]
        peers = [(x ^ dx, y ^ dy, c ^ dc) for dx, dy, dc in flips]
        remote = lambda s, d, k: pltpu.make_async_remote_copy(
            src_ref=s, dst_ref=d, send_sem=send_sem.at[k], recv_sem=recv_sem.at[k], device_id=peers[k], device_id_type=MESH)
        sends = [remote(src[0], out[0].at[4 * x + 2 * y + c], k) for k in range(7)]
        landed = [remote(out[0].at[4 * px + 2 * py + pc], out[0].at[4 * px + 2 * py + pc], k) for k, (px, py, pc) in enumerate(peers)]
        return sends, landed, pltpu.make_async_copy(src[0], out[0].at[4 * x + 2 * y + c], local_sem)

    def start(self, src, out, sems):
        sends, _, local = self._copies(src, out, sems)
        for cp in sends:
            cp.start()
        local.start()

    def mid(self, src, out, sems):
        pass

    def finish(self, src, out, sems):
        sends, landed, local = self._copies(src, out, sems)
        for cp in landed:
            cp.wait_recv()
        for cp in sends:
            cp.wait_send()
        local.wait()


def ffn_up(h, gain, w1g, w3g, name, hosted=()):
    T, D = h.shape
    nsh, Fs, _ = w1g.shape
    tm = min(T, 1024)

    def body(h_ref, g_ref, w1_ref, w3_ref, n_ref, ga_ref, gb_ref, s_ref):
        @pl.when(pl.program_id(1) == 0)
        def _():
            hh = h_ref[...]
            n_ref[...] = (hh * _rstd(hh) * g_ref[...]).astype(BF16)

        n = n_ref[...]
        a = _dot_nt(n, w1_ref[0])
        b = _dot_nt(n, w3_ref[0])
        silu, dsilu = _silu_parts(a)
        ga_ref[0] = (b * dsilu).astype(BF16)
        gb_ref[0] = silu.astype(BF16)
        s_ref[0] = (silu * b).astype(BF16)

    act = jax.ShapeDtypeStruct((nsh, T, Fs), BF16)
    act_spec = pl.BlockSpec((1, tm, Fs), lambda i, j: (j, i, 0))
    w_spec = pl.BlockSpec((1, Fs, D), lambda i, j: (j, 0, 0))
    return _call(
        body, hosted, name=name, grid=(T // tm, nsh),
        in_specs=[pl.BlockSpec((tm, D), lambda i, j: (i, 0)), pl.BlockSpec((1, D), lambda i, j: (0, 0)), w_spec, w_spec],
        out_specs=[pl.BlockSpec((tm, D), lambda i, j: (i, 0)), act_spec, act_spec, act_spec],
        out_shape=[jax.ShapeDtypeStruct((T, D), BF16), act, act, act],
        args=[h, gain, w1g, w3g])


def ffn_down(s, w2g, h, name, hosted=()):
    nsh, T, Fs = s.shape
    D = h.shape[1]
    tm = min(T, 512)

    def body(s_ref, w2_ref, h_ref, o_ref):
        f = _dot(s_ref[0], w2_ref[0])
        for j in range(1, nsh):
            f += _dot(s_ref[j], w2_ref[j])
        o_ref[...] = h_ref[...] + 0.5 * f

    return _call(
        body, hosted, name=name, grid=(T // tm,),
        in_specs=[pl.BlockSpec((nsh, tm, Fs), lambda i: (0, i, 0)), pl.BlockSpec((nsh, Fs, D), lambda i: (0, 0, 0)),
                  pl.BlockSpec((tm, D), lambda i: (i, 0))],
        out_specs=[pl.BlockSpec((tm, D), lambda i: (i, 0))],
        out_shape=[jax.ShapeDtypeStruct((T, D), F32)],
        args=[s, w2g, h])


def ffn_bwd_act(dh, w2g, ga, gb, name, hosted=()):
    T, D = dh.shape
    nsh, Fs, _ = w2g.shape
    tm = min(T, 1024)

    def body(dh_ref, w2_ref, ga_ref, gb_ref, da_ref, db_ref, df_ref):
        @pl.when(pl.program_id(1) == 0)
        def _():
            df_ref[...] = (0.5 * dh_ref[...]).astype(BF16)

        ds = _dot_nt(df_ref[...], w2_ref[0])
        da_ref[0] = (ds * ga_ref[0].astype(F32)).astype(BF16)
        db_ref[0] = (ds * gb_ref[0].astype(F32)).astype(BF16)

    act = jax.ShapeDtypeStruct((nsh, T, Fs), BF16)
    act_spec = pl.BlockSpec((1, tm, Fs), lambda i, j: (j, i, 0))
    row_spec = pl.BlockSpec((tm, D), lambda i, j: (i, 0))
    return _call(
        body, hosted, name=name, grid=(T // tm, nsh),
        in_specs=[row_spec, pl.BlockSpec((1, Fs, D), lambda i, j: (j, 0, 0)), act_spec, act_spec],
        out_specs=[act_spec, act_spec, row_spec],
        out_shape=[act, act, jax.ShapeDtypeStruct((T, D), BF16)],
        args=[dh, w2g, ga, gb])


def ffn_dw2(s, df, name, hosted=()):
    nsh, T, Fs = s.shape
    D = df.shape[1]
    tk = min(T, 512)
    nk = T // tk

    def body(s_ref, df_ref, o_ref, acc):
        k = pl.program_id(1)

        @pl.when(k == 0)
        def _():
            acc[...] = jnp.zeros_like(acc)

        acc[...] += _dot_tn(s_ref[0], df_ref[...])

        @pl.when(k == nk - 1)
        def _():
            o_ref[0] = acc[...].astype(BF16)

    return _call(
        body, hosted, name=name, grid=(nsh, nk),
        in_specs=[pl.BlockSpec((1, tk, Fs), lambda j, k: (j, k, 0)), pl.BlockSpec((tk, D), lambda j, k: (k, 0))],
        out_specs=[pl.BlockSpec((1, Fs, D), lambda j, k: (j, 0, 0))],
        out_shape=[jax.ShapeDtypeStruct((nsh, Fs, D), BF16)],
        scratch_shapes=[pltpu.VMEM((Fs, D), F32)],
        args=[s, df])


def ffn_dw13(n, da, db, name, hosted=()):
    T, D = n.shape
    nsh, _, Fs = da.shape
    tk = min(T, 512)
    nk = T // tk

    def body(n_ref, da_ref, db_ref, o1_ref, o3_ref, acc1, acc3):
        k = pl.program_id(1)

        @pl.when(k == 0)
        def _():
            acc1[...] = jnp.zeros_like(acc1)
            acc3[...] = jnp.zeros_like(acc3)

        nn = n_ref[...]
        acc1[...] += _dot_tn(da_ref[0], nn)
        acc3[...] += _dot_tn(db_ref[0], nn)

        @pl.when(k == nk - 1)
        def _():
            o1_ref[0] = acc1[...].astype(BF16)
            o3_ref[0] = acc3[...].astype(BF16)

    act_spec = pl.BlockSpec((1, tk, Fs), lambda j, k: (j, k, 0))
    out = jax.ShapeDtypeStruct((nsh, Fs, D), BF16)
    out_spec = pl.BlockSpec((1, Fs, D), lambda j, k: (j, 0, 0))
    return _call(
        body, hosted, name=name, grid=(nsh, nk),
        in_specs=[pl.BlockSpec((tk, D), lambda j, k: (k, 0)), act_spec, act_spec],
        out_specs=[out_spec, out_spec],
        out_shape=[out, out],
        scratch_shapes=[pltpu.VMEM((Fs, D), F32), pltpu.VMEM((Fs, D), F32)],
        args=[n, da, db])


def ffn_bwd_in(da, db, w1g, w3g, h, gain, dh, name, hosted=()):
    nsh, T, Fs = da.shape
    D = h.shape[1]
    tm = min(T, 256)

    def body(da_ref, db_ref, w1_ref, w3_ref, h_ref, g_ref, dh_ref, o_ref, dg_ref):
        dn = _dot(da_ref[0], w1_ref[0]) + _dot(db_ref[0], w3_ref[0])
        for j in range(1, nsh):
            dn += _dot(da_ref[j], w1_ref[j]) + _dot(db_ref[j], w3_ref[j])
        dhn, dg = _rmsnorm_bwd(dn, h_ref[...], g_ref[...])
        o_ref[...] = dh_ref[...] + dhn

        @pl.when(pl.program_id(0) == 0)
        def _():
            dg_ref[...] = jnp.zeros_like(dg_ref)

        dg_ref[...] += jnp.sum(dg, axis=0, keepdims=True)

    act_spec = pl.BlockSpec((nsh, tm, Fs), lambda i: (0, i, 0))
    w_spec = pl.BlockSpec((nsh, Fs, D), lambda i: (0, 0, 0))
    row_spec = pl.BlockSpec((tm, D), lambda i: (i, 0))
    vec_spec = pl.BlockSpec((1, D), lambda i: (0, 0))
    return _call(
        body, hosted, name=name, grid=(T // tm,),
        in_specs=[act_spec, act_spec, w_spec, w_spec, row_spec, vec_spec, row_spec],
        out_specs=[row_spec, vec_spec],
        out_shape=[jax.ShapeDtypeStruct((T, D), F32), jax.ShapeDtypeStruct((1, D), F32)],
        args=[da, db, w1g, w3g, h, gain, dh])


def mix_in(h, gain, wing, name, hosted=()):
    T, D = h.shape
    nsh, _, Cs = wing.shape
    tm = min(T, 512)

    def body(h_ref, g_ref, w_ref, u_ref, p_ref):
        hh = h_ref[...]
        u = (hh * _rstd(hh) * g_ref[...]).astype(BF16)
        u_ref[...] = u
        for j in range(nsh):
            p_ref[:, j * Cs:(j + 1) * Cs] = _dot(u, w_ref[j])

    return _call(
        body, hosted, name=name, grid=(T // tm,),
        in_specs=[pl.BlockSpec((tm, D), lambda i: (i, 0)), pl.BlockSpec((1, D), lambda i: (0, 0)),
                  pl.BlockSpec((nsh, D, Cs), lambda i: (0, 0, 0))],
        out_specs=[pl.BlockSpec((tm, D), lambda i: (i, 0)), pl.BlockSpec((tm, nsh * Cs), lambda i: (i, 0))],
        out_shape=[jax.ShapeDtypeStruct((T, D), BF16), jax.ShapeDtypeStruct((T, nsh * Cs), F32)],
        args=[h, gain, wing])


def mix_out(a, b, woutg, h, name, hosted=()):
    T, W = a.shape
    D = h.shape[1]
    wout = woutg.reshape(2, W, D)
    tm = min(T, 512)

    def body(a_ref, b_ref, w_ref, h_ref, o_ref):
        o_ref[...] = h_ref[...] + _dot(a_ref[...], w_ref[0]) + _dot(b_ref[...], w_ref[1])

    return _call(
        body, hosted, name=name, grid=(T // tm,),
        in_specs=[pl.BlockSpec((tm, W), lambda i: (i, 0)), pl.BlockSpec((tm, W), lambda i: (i, 0)),
                  pl.BlockSpec((2, W, D), lambda i: (0, 0, 0)), pl.BlockSpec((tm, D), lambda i: (i, 0))],
        out_specs=[pl.BlockSpec((tm, D), lambda i: (i, 0))],
        out_shape=[jax.ShapeDtypeStruct((T, D), F32)],
        args=[a, b, wout, h])


def mix_out_bwd(dh, woutg, a, b, name, hosted=()):
    T, D = dh.shape
    W = a.shape[1]
    nsh, Rs, _ = woutg.shape
    wout = woutg.reshape(2, W, D)
    tk = min(T, 512)
    nk = T // tk

    def body(dh_ref, w_ref, a_ref, b_ref, da_ref, db_ref, dw_ref, acc):
        k = pl.program_id(0)

        @pl.when(k == 0)
        def _():
            acc[...] = jnp.zeros_like(acc)

        dhb = dh_ref[...].astype(BF16)
        da_ref[...] = _dot_nt(dhb, w_ref[0])
        db_ref[...] = _dot_nt(dhb, w_ref[1])
        acc[0:W, :] += _dot_tn(a_ref[...], dhb)
        acc[W:2 * W, :] += _dot_tn(b_ref[...], dhb)

        @pl.when(k == nk - 1)
        def _():
            for j in range(nsh):
                dw_ref[j] = acc[j * Rs:(j + 1) * Rs, :].astype(BF16)

    return _call(
        body, hosted, name=name, grid=(nk,),
        in_specs=[pl.BlockSpec((tk, D), lambda k: (k, 0)), pl.BlockSpec((2, W, D), lambda k: (0, 0, 0)),
                  pl.BlockSpec((tk, W), lambda k: (k, 0)), pl.BlockSpec((tk, W), lambda k: (k, 0))],
        out_specs=[pl.BlockSpec((tk, W), lambda k: (k, 0)), pl.BlockSpec((tk, W), lambda k: (k, 0)),
                   pl.BlockSpec((nsh, Rs, D), lambda k: (0, 0, 0))],
        out_shape=[jax.ShapeDtypeStruct((T, W), F32), jax.ShapeDtypeStruct((T, W), F32),
                   jax.ShapeDtypeStruct((nsh, Rs, D), BF16)],
        scratch_shapes=[pltpu.VMEM((2 * W, D), F32)],
        args=[dh, wout, a, b])


def mix_dwin(u, d, nsh, name, hosted=()):
    T, D = u.shape
    Cs = d.shape[1] // nsh
    tk = min(T, 512)
    nk = T // tk

    def body(u_ref, d_ref, o_ref, acc):
        k = pl.program_id(1)

        @pl.when(k == 0)
        def _():
            acc[...] = jnp.zeros_like(acc)

        acc[...] += _dot_tn(u_ref[...], d_ref[...])

        @pl.when(k == nk - 1)
        def _():
            o_ref[0] = acc[...].astype(BF16)

    return _call(
        body, hosted, name=name, grid=(nsh, nk),
        in_specs=[pl.BlockSpec((tk, D), lambda j, k: (k, 0)), pl.BlockSpec((tk, Cs), lambda j, k: (k, j))],
        out_specs=[pl.BlockSpec((1, D, Cs), lambda j, k: (j, 0, 0))],
        out_shape=[jax.ShapeDtypeStruct((nsh, D, Cs), BF16)],
        scratch_shapes=[pltpu.VMEM((D, Cs), F32)],
        args=[u, d])


def mix_in_bwd(d, wing, h, gain, dh, name, hosted=()):
    T, D = h.shape
    nsh, _, Cs = wing.shape
    tm = min(T, 512)

    def body(d_ref, w_ref, h_ref, g_ref, dh_ref, o_ref, dg_ref):
        du = _dot_nt(d_ref[:, 0:Cs], w_ref[0])
        for j in range(1, nsh):
            du += _dot_nt(d_ref[:, j * Cs:(j + 1) * Cs], w_ref[j])
        dhn, dg = _rmsnorm_bwd(du, h_ref[...], g_ref[...])
        o_ref[...] = dh_ref[...] + dhn

        @pl.when(pl.program_id(0) == 0)
        def _():
            dg_ref[...] = jnp.zeros_like(dg_ref)

        dg_ref[...] += jnp.sum(dg, axis=0, keepdims=True)

    row_spec = pl.BlockSpec((tm, D), lambda i: (i, 0))
    vec_spec = pl.BlockSpec((1, D), lambda i: (0, 0))
    return _call(
        body, hosted, name=name, grid=(T // tm,),
        in_specs=[pl.BlockSpec((tm, nsh * Cs), lambda i: (i, 0)), pl.BlockSpec((nsh, D, Cs), lambda i: (0, 0, 0)),
                  row_spec, vec_spec, row_spec],
        out_specs=[row_spec, vec_spec],
        out_shape=[jax.ShapeDtypeStruct((T, D), F32), jax.ShapeDtypeStruct((1, D), F32)],
        args=[d, wing, h, gain, dh])


def _pool_window(x, group, T, trailing):
    rows = lax.broadcasted_iota(jnp.int32, x.shape, 0)

    def shifted(z, k):
        if trailing:
            return jnp.where(rows >= k, pltpu.roll(z, k, 0), 0.0)
        return jnp.where(rows < T - k, pltpu.roll(z, T - k, 0), 0.0)

    s2 = x + shifted(x, 1)
    s4 = s2 + shifted(s2, 2)
    s8 = s4 + shifted(s4, 4)
    s16 = s8 + shifted(s8, 8)
    return jnp.where(group == 0, s2, jnp.where(group == 1, s4, jnp.where(group == 2, s8, s16)))


def _pool_count(group, shape):
    rows = lax.broadcasted_iota(jnp.int32, shape, 0)
    w = jnp.where(group == 0, 2, jnp.where(group == 1, 4, jnp.where(group == 2, 8, 16)))
    return jnp.minimum(rows + 1, w).astype(F32)


def pool_fwd(proj, pool_w, pool_scale, name, hosted=()):
    T = proj.shape[0]
    Hd = HEAD_DIM

    def body(x_ref, w_ref, sc_ref, a_ref):
        g = pl.program_id(0)
        x = x_ref[...]
        pooled = _pool_window(x, g, T, True) / _pool_count(g, x.shape) - x
        a_ref[...] = (_dot(pooled.astype(BF16), w_ref[0].astype(BF16)) * sc_ref[...]).astype(BF16)

    return _call(
        body, hosted, name=name, grid=(N_GROUPS,),
        in_specs=[pl.BlockSpec((T, Hd), lambda g: (0, g)), pl.BlockSpec((1, Hd, Hd), lambda g: (g, 0, 0)),
                  pl.BlockSpec((1, Hd), lambda g: (0, g))],
        out_specs=[pl.BlockSpec((T, Hd), lambda g: (0, g))],
        out_shape=[jax.ShapeDtypeStruct((T, N_GROUPS * Hd), BF16)],
        args=[proj, pool_w, pool_scale])


def pool_bwd(proj, da, pool_w, pool_scale, name, hosted=()):
    T = proj.shape[0]
    Hd = HEAD_DIM

    def body(x_ref, da_ref, w_ref, sc_ref, dx_ref, dw_ref, dsc_ref):
        g = pl.program_id(0)
        x = x_ref[...]
        cnt = _pool_count(g, x.shape)
        pooled = (_pool_window(x, g, T, True) / cnt - x).astype(BF16)
        wb = w_ref[0].astype(BF16)
        dav = da_ref[...]
        dsc_ref[...] = jnp.sum(dav * _dot(pooled, wb), axis=0, keepdims=True)
        dout = (dav * sc_ref[...]).astype(BF16)
        dw_ref[0] = _dot_tn(pooled, dout)
        dpooled = _dot_nt(dout, wb)
        dx_ref[...] = (_pool_window(dpooled / cnt, g, T, False) - dpooled).astype(BF16)

    col_spec = pl.BlockSpec((T, Hd), lambda g: (0, g))
    return _call(
        body, hosted, name=name, grid=(N_GROUPS,),
        in_specs=[col_spec, col_spec, pl.BlockSpec((1, Hd, Hd), lambda g: (g, 0, 0)), pl.BlockSpec((1, Hd), lambda g: (0, g))],
        out_specs=[col_spec, pl.BlockSpec((1, Hd, Hd), lambda g: (g, 0, 0)), pl.BlockSpec((1, Hd), lambda g: (0, g))],
        out_shape=[jax.ShapeDtypeStruct((T, N_GROUPS * Hd), BF16), jax.ShapeDtypeStruct((N_GROUPS, Hd, Hd), F32),
                   jax.ShapeDtypeStruct((1, N_GROUPS * Hd), F32)],
        args=[proj, da, pool_w, pool_scale])


def _ret_tables(T):
    Hd, C = HEAD_DIM, RET_CHUNK
    inv_freq = 1.0 / (ROPE_BASE ** (jnp.arange(0, Hd, 2, dtype=F32) / Hd))
    ang = jnp.arange(T, dtype=F32)[:, None] * inv_freq[None, :]
    cos, sin = jnp.cos(ang), jnp.sin(ang)
    cos2 = jnp.concatenate([cos, cos], axis=-1)
    sin2 = jnp.concatenate([-sin, sin], axis=-1)
    log_gamma = jnp.log1p(-jnp.exp2(-5.0 - jnp.arange(N_GROUPS, dtype=F32)))
    pos = jnp.arange(C, dtype=F32)
    rel = pos[:, None] - pos[None, :]
    intra = jnp.where(rel[None] >= 0, jnp.exp(log_gamma[:, None, None] * jnp.maximum(rel, 0.0)[None]), 0.0)
    k_tail = jnp.exp(log_gamma[:, None] * (C - 1 - pos)[None, :])
    q_head = jnp.exp(log_gamma[:, None] * (pos + 1.0)[None, :])
    chunk_decay = jnp.exp(log_gamma * C)
    wide = lambda t: jnp.broadcast_to(t[:, :, None], (N_GROUPS, C, Hd))
    return cos2, sin2, intra, wide(k_tail), wide(q_head), jnp.broadcast_to(chunk_decay[:, None, None], (N_GROUPS, 1, Hd))


def _rope(x, cos2, sin2):
    return x * cos2 + pltpu.roll(x, HEAD_DIM // 2, 1) * sin2


def _rope_t(d, cos2, sin2):
    return d * cos2 + pltpu.roll(d * sin2, HEAD_DIM // 2, 1)


def _ret_specs(tseg, seg_of):
    Hd, G = HEAD_DIM, N_GROUPS
    col = lambda kind: pl.BlockSpec((tseg, Hd), lambda h, s: (seg_of(s), G * kind + h))
    tab = pl.BlockSpec((tseg, Hd), lambda h, s: (seg_of(s), 0))
    head = pl.BlockSpec((1, RET_CHUNK, Hd), lambda h, s: (h, 0, 0))
    cd = pl.BlockSpec((1, 1, Hd), lambda h, s: (h, 0, 0))
    gain = pl.BlockSpec((1, Hd), lambda h, s: (0, h))
    return col, tab, head, cd, gain


def ret_fwd(proj, ret_norm, tables, name, hosted=()):
    T = proj.shape[0]
    Hd, C, G = HEAD_DIM, RET_CHUNK, N_GROUPS
    tseg = min(T, 1024)
    nseg, nck = T // tseg, tseg // C
    scale = Hd ** -0.5
    cos2, sin2, intra, k_tail, q_head, chunk_decay = tables

    def body(q_ref, k_ref, v_ref, g_ref, gain_ref, cos_ref, sin_ref, m_ref, kt_ref, qh_ref, cd_ref,
             b_ref, o_ref, rp_ref, state):
        @pl.when(pl.program_id(1) == 0)
        def _():
            state[...] = jnp.zeros_like(state)

        def chunk(ci, carry):
            rows = pl.ds(pl.multiple_of(ci * C, C), C)
            cos, sin = cos_ref[rows, :], sin_ref[rows, :]
            qr = _rope(q_ref[rows, :], cos, sin)
            kr = _rope(k_ref[rows, :], cos, sin) * scale
            qb, kb, vb = qr.astype(BF16), kr.astype(BF16), v_ref[rows, :].astype(BF16)
            r = state[...]
            rp_ref[0, ci] = r.astype(BF16)
            sc = _dot_nt(qb, kb) * m_ref[0]
            o = _dot(sc.astype(BF16), vb) + _dot((qr * qh_ref[0]).astype(BF16), r.astype(BF16))
            state[...] = cd_ref[0] * r + _dot_tn((kr * kt_ref[0]).astype(BF16), vb)
            o_ref[rows, :] = o
            on = o * _rstd(o)
            b_ref[rows, :] = (jax.nn.silu(g_ref[rows, :]) * (on * gain_ref[...])).astype(BF16)
            return carry

        lax.fori_loop(0, nck, chunk, 0, unroll=True)

    col, tab, head, cd, gain = _ret_specs(tseg, lambda s: s)
    out_col = pl.BlockSpec((tseg, Hd), lambda h, s: (s, h))
    return _call(
        body, hosted, name=name, grid=(G, nseg),
        in_specs=[col(1), col(2), col(3), col(4), gain, tab, tab, head, head, head, cd],
        out_specs=[out_col, out_col, pl.BlockSpec((1, nck, Hd, Hd), lambda h, s: (h, s, 0, 0))],
        out_shape=[jax.ShapeDtypeStruct((T, G * Hd), BF16), jax.ShapeDtypeStruct((T, G * Hd), F32),
                   jax.ShapeDtypeStruct((G, T // C, Hd, Hd), BF16)],
        scratch_shapes=[pltpu.VMEM((Hd, Hd), F32)],
        args=[proj, proj, proj, proj, ret_norm, cos2, sin2, intra, k_tail, q_head, chunk_decay])


def ret_bwd(proj, db, o_pre, r_prev, ret_norm, tables, name, hosted=()):
    T = proj.shape[0]
    Hd, C, G = HEAD_DIM, RET_CHUNK, N_GROUPS
    tseg = min(T, 1024)
    nseg, nck = T // tseg, tseg // C
    scale = Hd ** -0.5
    cos2, sin2, intra, k_tail, q_head, chunk_decay = tables

    def body(q_ref, k_ref, v_ref, g_ref, db_ref, o_ref, rp_ref, gain_ref, cos_ref, sin_ref, m_ref, kt_ref, qh_ref, cd_ref,
             d_ref, dgain_ref, gstate):
        @pl.when(pl.program_id(1) == 0)
        def _():
            gstate[...] = jnp.zeros_like(gstate)
            dgain_ref[...] = jnp.zeros_like(dgain_ref)

        def chunk(t, carry):
            ci = nck - 1 - t
            rows = pl.ds(pl.multiple_of(ci * C, C), C)
            cos, sin = cos_ref[rows, :], sin_ref[rows, :]
            qr = _rope(q_ref[rows, :], cos, sin)
            kr = _rope(k_ref[rows, :], cos, sin) * scale
            qb, kb, vb = qr.astype(BF16), kr.astype(BF16), v_ref[rows, :].astype(BF16)
            qhb, ktb = (qr * qh_ref[0]).astype(BF16), (kr * kt_ref[0]).astype(BF16)
            sc = (_dot_nt(qb, kb) * m_ref[0]).astype(BF16)
            o = o_ref[rows, :]
            rstd = _rstd(o)
            on = o * rstd
            gain = gain_ref[...]
            silu, dsilu = _silu_parts(g_ref[rows, :])
            dy = db_ref[rows, :]
            dgain_ref[...] += jnp.sum(dy * silu * on, axis=0, keepdims=True)
            dg = dy * on * gain * dsilu
            don = dy * silu * gain
            dob = (rstd * (don - on * jnp.mean(don * on, axis=-1, keepdims=True))).astype(BF16)
            gn = gstate[...]
            gb = gn.astype(BF16)
            da = (_dot_nt(dob, vb) * m_ref[0]).astype(BF16)
            dq = _dot(da, kb) + _dot_nt(dob, rp_ref[0, ci]) * qh_ref[0]
            dk = _dot_tn(da, qb) + _dot_nt(vb, gb) * kt_ref[0]
            dv = _dot_tn(sc, dob) + _dot(ktb, gb)
            gstate[...] = cd_ref[0] * gn + _dot_tn(qhb, dob)
            d_ref[0, rows, :] = _rope_t(dq, cos, sin).astype(BF16)
            d_ref[1, rows, :] = _rope_t(dk * scale, cos, sin).astype(BF16)
            d_ref[2, rows, :] = dv.astype(BF16)
            d_ref[3, rows, :] = dg.astype(BF16)
            return carry

        lax.fori_loop(0, nck, chunk, 0, unroll=True)

    rev = lambda s: nseg - 1 - s
    col, tab, head, cd, gain = _ret_specs(tseg, rev)
    act = pl.BlockSpec((tseg, Hd), lambda h, s: (rev(s), h))
    return _call(
        body, hosted, name=name, grid=(G, nseg),
        in_specs=[col(1), col(2), col(3), col(4), act, act, pl.BlockSpec((1, nck, Hd, Hd), lambda h, s: (h, rev(s), 0, 0)),
                  gain, tab, tab, head, head, head, cd],
        out_specs=[pl.BlockSpec((4, tseg, Hd), lambda h, s: (0, rev(s), h)), gain],
        out_shape=[jax.ShapeDtypeStruct((4, T, G * Hd), BF16), jax.ShapeDtypeStruct((1, G * Hd), F32)],
        scratch_shapes=[pltpu.VMEM((Hd, Hd), F32)],
        args=[proj, proj, proj, proj, db, o_pre, r_prev, ret_norm, cos2, sin2, intra, k_tail, q_head, chunk_decay])


def final_loss(h, gain, target, name, hosted=()):
    T, D = h.shape
    tm = min(T, 512)

    def body(h_ref, g_ref, t_ref, dh_ref, loss_ref, dg_ref):
        @pl.when(pl.program_id(0) == 0)
        def _():
            loss_ref[...] = jnp.zeros_like(loss_ref)
            dg_ref[...] = jnp.zeros_like(dg_ref)

        hh = h_ref[...]
        gain_v = g_ref[...]
        err = hh * _rstd(hh) * gain_v - t_ref[...]
        loss_ref[...] += 0.5 * jnp.sum(jnp.mean(err * err, axis=-1, keepdims=True), axis=0, keepdims=True)
        dhn, dg = _rmsnorm_bwd(err * (1.0 / D), hh, gain_v)
        dh_ref[...] = dhn
        dg_ref[...] += jnp.sum(dg, axis=0, keepdims=True)

    row_spec = pl.BlockSpec((tm, D), lambda i: (i, 0))
    vec_spec = pl.BlockSpec((1, D), lambda i: (0, 0))
    return _call(
        body, hosted, name=name, grid=(T // tm,),
        in_specs=[row_spec, vec_spec, row_spec],
        out_specs=[row_spec, pl.BlockSpec((1, 128), lambda i: (0, 0)), vec_spec],
        out_shape=[jax.ShapeDtypeStruct((T, D), F32), jax.ShapeDtypeStruct((1, 128), F32), jax.ShapeDtypeStruct((1, D), F32)],
        args=[h, gain, target])


def prereduce(grad, recv, place, name):
    nsh, R, C = grad.shape
    rh = R // 2

    def body(place_ref, g_ref, r_ref, o_ref, own_ref):
        piece = (g_ref[...].astype(F32) + r_ref[...].astype(F32)).astype(BF16)
        o_ref[...] = piece

        @pl.when(pl.program_id(0) == place_ref[1])
        def _():
            own_ref[...] = piece

    return pl.pallas_call(
        body, name=name,
        grid_spec=pltpu.PrefetchScalarGridSpec(
            num_scalar_prefetch=1, grid=(nsh,),
            in_specs=[pl.BlockSpec((1, rh, C), lambda j, p: (j, p[0], 0)), pl.BlockSpec((1, rh, C), lambda j, p: (j, 0, 0))],
            out_specs=[pl.BlockSpec((1, rh, C), lambda j, p: (j, 0, 0)), pl.BlockSpec((1, rh, C), lambda j, p: (p[1], p[0], 0))]),
        out_shape=[jax.ShapeDtypeStruct((nsh, rh, C), BF16), jax.ShapeDtypeStruct((nsh, R, C), BF16)],
        compiler_params=pltpu.CompilerParams(vmem_limit_bytes=VMEM_LIMIT_V7X),
    )(place, grad, recv)


def _adamw(w, g, m, v):
    m = ADAM_B1 * m + (1.0 - ADAM_B1) * g
    v = ADAM_B2 * v + (1.0 - ADAM_B2) * (g * g)
    m_hat = m / (1.0 - ADAM_B1 ** ADAM_STEP)
    v_hat = v / (1.0 - ADAM_B2 ** ADAM_STEP)
    return -ADAM_LR * (m_hat / (jnp.sqrt(v_hat) + ADAM_EPS) + ADAM_WD * w), m, v


def adamw_sharded(tensors, name, hosted=()):
    nt = len(tensors)
    nsh, R, C = tensors[0][0].shape
    lanes = -(-C // 128) * 128
    per_row = 2 * nt * lanes * (nsh * 2 + 7 * 4)
    tr = max(r for r in range(16, R + 1, 16) if R % r == 0 and r * per_row <= ADAMW_VMEM_BUDGET)

    def body(*refs):
        ins, outs = refs[:4 * nt], refs[4 * nt:]
        for t in range(nt):
            p_ref, w_ref, m_ref, v_ref = ins[4 * t:4 * t + 4]
            g_ref, d_ref, nm_ref, nv_ref = outs[4 * t:4 * t + 4]
            g = p_ref[0].astype(F32)
            for i in range(1, nsh):
                g += p_ref[i].astype(F32)
            g_ref[...] = g
            d_ref[...], nm_ref[...], nv_ref[...] = _adamw(w_ref[...], g, m_ref[...], v_ref[...])

    spec = pl.BlockSpec((tr, C), lambda i: (i, 0))
    out = jax.ShapeDtypeStruct((R, C), F32)
    return _call(
        body, hosted, name=name, grid=(R // tr,),
        in_specs=[pl.BlockSpec((nsh, tr, C), lambda i: (0, i, 0)), spec, spec, spec] * nt,
        out_specs=[spec] * (4 * nt), out_shape=[out] * (4 * nt),
        args=[a for tensor in tensors for a in tensor])


def adamw_small(packs, pool, vectors, name):
    ndev = packs.shape[0]
    rp, rv = pool[0].shape[0], vectors[0].shape[0]

    def body(p_ref, wp, mp, vp, wv, mv, vv, gp, dp, nmp, nvp, gv, dv, nmv, nvv, loss_ref):
        g = p_ref[0]
        for i in range(1, ndev):
            g += p_ref[i]
        gp[...], gv[...], loss_ref[...] = g[0:rp], g[rp:rp + rv], g[rp + rv:rp + rv + 8]
        dp[...], nmp[...], nvp[...] = _adamw(wp[...], g[0:rp], mp[...], vp[...])
        dv[...], nmv[...], nvv[...] = _adamw(wv[...], g[rp:rp + rv], mv[...], vv[...])

    shape = lambda rows: jax.ShapeDtypeStruct((rows, 128), F32)
    outs = pl.pallas_call(body, name=name, out_shape=[shape(rp)] * 4 + [shape(rv)] * 4 + [shape(8)],
                          compiler_params=pltpu.CompilerParams(vmem_limit_bytes=VMEM_LIMIT_V7X))(packs, *pool, *vectors)
    return outs[0:4], outs[4:8], outs[8]


BIG = ("ffn1_w1", "ffn1_w3", "ffn1_w2", "w_in", "w_out", "ffn2_w1", "ffn2_w3", "ffn2_w2")
TRANSPOSED = ("ffn1_w1", "ffn1_w3", "ffn2_w1", "ffn2_w3")
VECTORS = ("ffn1_norm", "mix_norm", "pool_scale", "ret_norm", "ffn2_norm", "final_norm")
WEIGHTS = ("ffn1_norm", "ffn1_w1", "ffn1_w3", "ffn1_w2", "mix_norm", "w_in", "pool_w", "pool_scale", "ret_norm", "w_out",
           "ffn2_norm", "ffn2_w1", "ffn2_w3", "ffn2_w2", "final_norm")


def _pack_vectors(parts):
    return jnp.concatenate([parts[k].reshape(-1, 128) for k in VECTORS], axis=0)


def _unpack_vectors(pack, like):
    out, row = {}, 0
    for k in VECTORS:
        rows = like[k].size // 128
        out[k] = pack[row:row + rows].reshape(like[k].shape)
        row += rows
    return out


def kernel(x, ffn1_norm, ffn1_w1, ffn1_w3, ffn1_w2, mix_norm, w_in, pool_w, pool_scale, ret_norm, w_out, ffn2_norm, ffn2_w1, ffn2_w3, ffn2_w2, final_norm, loss_target, m_ffn1_norm, m_ffn1_w1, m_ffn1_w3, m_ffn1_w2, m_mix_norm, m_w_in, m_pool_w, m_pool_scale, m_ret_norm, m_w_out, m_ffn2_norm, m_ffn2_w1, m_ffn2_w3, m_ffn2_w2, m_final_norm, v_ffn1_norm, v_ffn1_w1, v_ffn1_w3, v_ffn1_w2, v_mix_norm, v_w_in, v_pool_w, v_pool_scale, v_ret_norm, v_w_out, v_ffn2_norm, v_ffn2_w1, v_ffn2_w3, v_ffn2_w2, v_final_norm):
    w = dict(ffn1_norm=ffn1_norm, ffn1_w1=ffn1_w1, ffn1_w3=ffn1_w3, ffn1_w2=ffn1_w2, mix_norm=mix_norm, w_in=w_in, pool_w=pool_w,
             pool_scale=pool_scale, ret_norm=ret_norm, w_out=w_out, ffn2_norm=ffn2_norm, ffn2_w1=ffn2_w1, ffn2_w3=ffn2_w3,
             ffn2_w2=ffn2_w2, final_norm=final_norm)
    m = dict(ffn1_norm=m_ffn1_norm, ffn1_w1=m_ffn1_w1, ffn1_w3=m_ffn1_w3, ffn1_w2=m_ffn1_w2, mix_norm=m_mix_norm, w_in=m_w_in,
             pool_w=m_pool_w, pool_scale=m_pool_scale, ret_norm=m_ret_norm, w_out=m_w_out, ffn2_norm=m_ffn2_norm, ffn2_w1=m_ffn2_w1,
             ffn2_w3=m_ffn2_w3, ffn2_w2=m_ffn2_w2, final_norm=m_final_norm)
    v = dict(ffn1_norm=v_ffn1_norm, ffn1_w1=v_ffn1_w1, ffn1_w3=v_ffn1_w3, ffn1_w2=v_ffn1_w2, mix_norm=v_mix_norm, w_in=v_w_in,
             pool_w=v_pool_w, pool_scale=v_pool_scale, ret_norm=v_ret_norm, w_out=v_w_out, ffn2_norm=v_ffn2_norm, ffn2_w1=v_ffn2_w1,
             ffn2_w3=v_ffn2_w3, ffn2_w2=v_ffn2_w2, final_norm=v_final_norm)
    xs, target = x[0], loss_target[0]
    T = xs.shape[0]
    tables = _ret_tables(T)
    place = jnp.stack([lax.axis_index("c"), 2 * lax.axis_index("x") + lax.axis_index("y")]).astype(jnp.int32)
    local = lambda d, k: jnp.transpose(d[k][0]) if k in TRANSPOSED else d[k][0]
    result = lambda o, k: jnp.transpose(o)[None] if k in TRANSPOSED else o[None]
    sh = {k: local(w, k).astype(BF16) for k in BIG}
    gather = lambda *names: [ChipExchange([sh[k] for k in names], False)]
    wg, grad, delta, new_m, new_v = {}, {}, {}, {}, {}

    def update(names, pieces, name, hosted=()):
        outs, extras = adamw_sharded([(p, local(w, k), local(m, k), local(v, k)) for k, p in zip(names, pieces)], name, hosted)
        for t, k in enumerate(names):
            grad[k], delta[k], new_m[k], new_v[k] = [result(o, k) for o in outs[4 * t:4 * t + 4]]
        return extras

    def reduce_in_chip(name, partial, recv):
        return prereduce(partial, recv, place, "prereduce_" + name)

    scatter = lambda *reduced: ChipExchange([r[0] for r in reduced], True, [r[1] for r in reduced])

    (wg["ffn1_w1"], wg["ffn1_w3"]), = exchange(gather("ffn1_w1", "ffn1_w3"), "gather_ffn1")
    (n1, ga1, gb1, s1), ((wg["ffn1_w2"], wg["w_in"]),) = ffn_up(
        xs, ffn1_norm, wg["ffn1_w1"], wg["ffn1_w3"], "ffn1_up", gather("ffn1_w2", "w_in"))
    (h1,), ((wg["w_out"],),) = ffn_down(s1, wg["ffn1_w2"], xs, "ffn1_down", gather("w_out"))
    (u, proj), ((wg["ffn2_w1"],),) = mix_in(h1, mix_norm, wg["w_in"], "mix_in", gather("ffn2_w1"))
    (pa,), _ = pool_fwd(proj, pool_w[0], pool_scale, "pool_fwd")
    (rb, o_pre, r_prev), ((wg["ffn2_w3"],),) = ret_fwd(proj, ret_norm, tables, "ret_fwd", gather("ffn2_w3"))
    (h2,), _ = mix_out(pa, rb, wg["w_out"], h1, "mix_out")
    (n2, ga2, gb2, s2), ((wg["ffn2_w2"],),) = ffn_up(
        h2, ffn2_norm, wg["ffn2_w1"], wg["ffn2_w3"], "ffn2_up", gather("ffn2_w2"))
    (h3,), _ = ffn_down(s2, wg["ffn2_w2"], h2, "ffn2_down")
    (dh3, loss, d_final), _ = final_loss(h3, final_norm[None], target, "final_loss")

    (da2, db2, df2), _ = ffn_bwd_act(dh3, wg["ffn2_w2"], ga2, gb2, "ffn2_bwd_act")
    (g_f2w2,), _ = ffn_dw2(s2, df2, "ffn2_dw2")
    (g_f2w1, g_f2w3), ((r_f2w2,),) = ffn_dw13(n2, da2, db2, "ffn2_dw13", [SiblingExchange([g_f2w2])])
    p_f2w2 = reduce_in_chip("ffn2_w2", g_f2w2, r_f2w2)
    (dh2, d_ffn2), ((q_f2w2,), (r_f2w1, r_f2w3)) = ffn_bwd_in(
        da2, db2, wg["ffn2_w1"], wg["ffn2_w3"], h2, ffn2_norm, dh3, "ffn2_bwd_in",
        [scatter(p_f2w2), SiblingExchange([g_f2w1, g_f2w3])])
    p_f2w1 = reduce_in_chip("ffn2_w1", g_f2w1, r_f2w1)
    p_f2w3 = reduce_in_chip("ffn2_w3", g_f2w3, r_f2w3)
    (dpa, drb, g_wout), _ = mix_out_bwd(dh2, wg["w_out"], pa, rb, "mix_out_bwd")
    (dpool, d_pool_w, d_pool_scale), _ = pool_bwd(proj, dpa, pool_w[0], pool_scale, "pool_bwd")
    (dqkvg, d_ret_norm), ((q_f2w1, q_f2w3), (r_wout,)) = ret_bwd(
        proj, drb, o_pre, r_prev, ret_norm, tables, "ret_bwd", [scatter(p_f2w1, p_f2w3), SiblingExchange([g_wout])])
    p_wout = reduce_in_chip("w_out", g_wout, r_wout)
    d = jnp.concatenate([dpool, dqkvg[0], dqkvg[1], dqkvg[2], dqkvg[3]], axis=1)
    (g_win,), ((q_wout,),) = mix_dwin(u, d, N_CHIPS, "mix_dwin", [scatter(p_wout)])
    (dh1, d_mix), ((r_win,),) = mix_in_bwd(d, wg["w_in"], h1, mix_norm, dh2, "mix_in_bwd", [SiblingExchange([g_win])])
    p_win = reduce_in_chip("w_in", g_win, r_win)
    (da1, db1, df1), ((q_win,),) = ffn_bwd_act(dh1, wg["ffn1_w2"], ga1, gb1, "ffn1_bwd_act", [scatter(p_win)])
    (g_f1w1, g_f1w3), _ = ffn_dw13(n1, da1, db1, "ffn1_dw13")
    (g_f1w2,), ((r_f1w1, r_f1w3),) = ffn_dw2(s1, df1, "ffn1_dw2", [SiblingExchange([g_f1w1, g_f1w3])])
    p_f1w1 = reduce_in_chip("ffn1_w1", g_f1w1, r_f1w1)
    p_f1w3 = reduce_in_chip("ffn1_w3", g_f1w3, r_f1w3)
    (dx, d_ffn1), ((q_f1w1, q_f1w3), (r_f1w2,)) = ffn_bwd_in(
        da1, db1, wg["ffn1_w1"], wg["ffn1_w3"], xs, ffn1_norm, dh1, "ffn1_bwd_in",
        [scatter(p_f1w1, p_f1w3), SiblingExchange([g_f1w2])])
    p_f1w2 = reduce_in_chip("ffn1_w2", g_f1w2, r_f1w2)

    d_vectors = {"ffn1_norm": d_ffn1, "mix_norm": d_mix, "pool_scale": d_pool_scale, "ret_norm": d_ret_norm,
                 "ffn2_norm": d_ffn2, "final_norm": d_final}
    pack = jnp.concatenate([d_pool_w.reshape(-1, 128), _pack_vectors(d_vectors), jnp.broadcast_to(loss, (8, 128))], axis=0)
    (q_f1w2,), (packs,) = update(["ffn2_w1", "ffn2_w3", "ffn1_w1", "ffn1_w3"], [q_f2w1, q_f2w3, q_f1w1, q_f1w3], "adamw_w13",
                                 [scatter(p_f1w2), AllExchange(pack)])
    update(["ffn2_w2", "ffn1_w2"], [q_f2w2, q_f1w2], "adamw_w2")
    update(["w_in"], [q_win], "adamw_w_in")
    update(["w_out"], [q_wout], "adamw_w_out")
    of_pool, of_vectors, loss_sum = adamw_small(packs, [t["pool_w"].reshape(-1, 128) for t in (w, m, v)],
                                                [_pack_vectors(t) for t in (w, m, v)], "adamw_small")
    for res, pool_part, vector_part in zip((grad, delta, new_m, new_v), of_pool, of_vectors):
        res["pool_w"] = pool_part.reshape(pool_w.shape)
        res.update(_unpack_vectors(vector_part, w))
    loss = loss_sum[0, 0]

    return (loss, dx[None], *[grad[k] for k in WEIGHTS], *[delta[k] for k in WEIGHTS],
            *[new_m[k] for k in WEIGHTS], *[new_v[k] for k in WEIGHTS])
```

```python
import math

import jax
import jax.numpy as jnp
from jax import lax
from jax.experimental import pallas as pl
from jax.experimental.pallas import tpu as pltpu

F32 = jnp.float32
BF16 = jnp.bfloat16

EPS = 1e-6
N_CHIPS = 4
N_GROUPS = 4
HEAD_DIM = 128
RET_CHUNK = 128
ROPE_BASE = 10000.0
ADAM_LR, ADAM_B1, ADAM_B2, ADAM_EPS, ADAM_WD, ADAM_STEP = 0.001, 0.9, 0.999, 1e-08, 0.01, 10
VMEM_LIMIT_V7X = 56 * 1024 * 1024
ADAMW_VMEM_BUDGET = 32 * 1024 * 1024
MESH = pl.DeviceIdType.MESH
ANY = pl.BlockSpec(memory_space=pl.ANY)


def _dot(a, b):
    return jnp.dot(a, b, preferred_element_type=F32)


def _dot_nt(a, b):
    return lax.dot_general(a, b, (((1,), (1,)), ((), ())), preferred_element_type=F32)


def _dot_tn(a, b):
    return lax.dot_general(a, b, (((0,), (0,)), ((), ())), preferred_element_type=F32)


def _rstd(h):
    return lax.rsqrt(jnp.mean(h * h, axis=-1, keepdims=True) + EPS)


def _rmsnorm_bwd(dn, h, gain):
    r = _rstd(h)
    nh = h * r
    dnh = dn * gain
    dh = r * (dnh - nh * jnp.mean(dnh * nh, axis=-1, keepdims=True))
    return dh, dn * nh


def _silu_parts(a):
    sig = jax.nn.sigmoid(a)
    silu = a * sig
    return silu, sig + silu * (1.0 - sig)


def _mesh_pos():
    return lax.axis_index("x"), lax.axis_index("y"), lax.axis_index("c")


class ChipExchange:
    def __init__(self, srcs, scatter, placed=()):
        n = len(srcs)
        self.inputs, self.scatter, self.n, self.reach = list(srcs) + list(placed), scatter, n, REACH_CHIPS
        self.aliases = {n + t: t for t in range(n)} if scatter else {}
        self.half_rows = [s.shape[1] if scatter else s.shape[0] // 2 for s in srcs]
        self.out_shape = [jax.ShapeDtypeStruct((N_CHIPS, 2 * rh, s.shape[-1]), s.dtype) for s, rh in zip(srcs, self.half_rows)]
        if scatter:
            self.out_shape += [jax.ShapeDtypeStruct((2, rh // 2, s.shape[-1]), s.dtype) for s, rh in zip(srcs, self.half_rows)]
        dma = pltpu.SemaphoreType.DMA
        self.sems = [dma((4 * n,)), dma((4 * n,)), dma((2 * n,)), dma((2 * n,)), dma((4 * n,)), dma((4 * n,))]

    def _copies(self, src, out, sems):
        hop1_send, hop1_recv, hop2_send, hop2_recv, d2d_send, d2d_recv = sems
        x, y, c = _mesh_pos()
        me, dg = 2 * x + y, 2 * (1 - x) + (1 - y)
        sibling = (x, y, 1 - c)
        n = self.n
        mine, theirs = c, 1 - c

        def nb(a):
            nx, ny = x ^ (1 - a), y ^ a
            return 2 * nx + ny, (nx, ny, c)

        def remote(s, d, send, recv, k, to):
            return pltpu.make_async_remote_copy(src_ref=s, dst_ref=d, send_sem=send.at[k], recv_sem=recv.at[k],
                                                device_id=to, device_id_type=MESH)

        class Copies:
            def slot(_, t, chip, half):
                rh = self.half_rows[t]
                return out[t].at[chip, pl.ds(half * rh, rh), :]

            def quarter(_, t, chip, q):
                qh = self.half_rows[t] // 2
                return out[t].at[chip, pl.ds(mine * 2 * qh + q * qh, qh), :]

            def own_shard(k, t):
                return remote(src[t], out[t].at[me], d2d_send, d2d_recv, 4 * t + 3, sibling)

            def hop1(k, t, a, transit=False):
                rh = self.half_rows[t]
                chip, to = nb(a)
                if transit:
                    piece = src[t].at[dg, pl.ds(a * (rh // 2), rh // 2), :]
                    return remote(piece, out[n + t].at[a], hop1_send, hop1_recv, 4 * t + 2 + a, to)
                piece = src[t].at[chip] if self.scatter else src[t].at[pl.ds(mine * rh, rh), :]
                return remote(piece, k.slot(t, me, mine), hop1_send, hop1_recv, 4 * t + a, to)

            def landed1(k, t, a, transit=False):
                here = out[n + t].at[a] if transit else k.slot(t, nb(a)[0], mine)
                return remote(here, here, hop1_send, hop1_recv, 4 * t + (2 if transit else 0) + a, sibling)

            def hop2(k, t, q):
                origin, to = nb(q)[0], nb(1 - q)[1]
                piece = out[n + t].at[q] if self.scatter else k.quarter(t, origin, q)
                return remote(piece, k.quarter(t, origin, q), hop2_send, hop2_recv, 2 * t + q, to)

            def landed2(k, t, q):
                here = k.quarter(t, dg, q)
                return remote(here, here, hop2_send, hop2_recv, 2 * t + q, sibling)

            def d2d(k, t, p, chip, own=False, arriving=False):
                if arriving:
                    there = k.slot(t, chip, theirs)
                    return remote(there, there, d2d_send, d2d_recv, 4 * t + p, sibling)
                piece = src[t].at[me] if own else k.slot(t, chip, mine)
                return remote(piece, k.slot(t, chip, mine), d2d_send, d2d_recv, 4 * t + p, sibling)

        return Copies(), nb, me, dg, c

    def start(self, src, out, sems):
        k, nb, me, dg, c = self._copies(src, out, sems)
        for t in range(self.n):
            for first in range(2):
                a = first ^ c
                k.hop1(t, a).start()
                if self.scatter:
                    k.hop1(t, a, transit=True).start()
            if self.scatter:
                k.d2d(t, 3, me, own=True).start()
            else:
                k.own_shard(t).start()

    def mid(self, src, out, sems):
        k, nb, me, dg, c = self._copies(src, out, sems)
        for t in range(self.n):
            for first in range(2):
                a = first ^ c
                if self.scatter:
                    k.landed1(t, a, transit=True).wait_recv()
                    k.hop2(t, a).start()
                k.landed1(t, a).wait_recv()
                if not self.scatter:
                    k.hop2(t, a).start()
                k.d2d(t, a, nb(a)[0]).start()

    def finish(self, src, out, sems):
        k, nb, me, dg, c = self._copies(src, out, sems)
        for t in range(self.n):
            for q in range(2):
                k.landed2(t, q).wait_recv()
            k.d2d(t, 2, dg).start()
        for t in range(self.n):
            for a in range(2):
                k.d2d(t, a, nb(a)[0], arriving=True).wait_recv()
            k.d2d(t, 2, dg, arriving=True).wait_recv()
            if self.scatter:
                k.d2d(t, 3, me, arriving=True).wait_recv()
        for t in range(self.n):
            for a in range(2):
                k.hop1(t, a).wait_send()
                if self.scatter:
                    k.hop1(t, a, transit=True).wait_send()
                k.hop2(t, a).wait_send()
                k.d2d(t, a, nb(a)[0]).wait_send()
            k.d2d(t, 2, dg).wait_send()
            if self.scatter:
                k.d2d(t, 3, me, own=True).wait_send()
            else:
                k.own_shard(t).wait()


class SiblingExchange:
    def __init__(self, grads):
        self.inputs, self.n, self.aliases, self.reach = list(grads), len(grads), {}, REACH_SIBLING
        self.half_rows = [g.shape[1] // 2 for g in grads]
        self.out_shape = [jax.ShapeDtypeStruct((g.shape[0], rh, g.shape[2]), g.dtype) for g, rh in zip(grads, self.half_rows)]
        self.sems = [pltpu.SemaphoreType.DMA((self.n,)), pltpu.SemaphoreType.DMA((self.n,))]

    def _plan(self, src, out, sems):
        x, y, c = _mesh_pos()
        return [pltpu.make_async_remote_copy(
            src_ref=src[t].at[:, pl.ds((1 - c) * self.half_rows[t], self.half_rows[t]), :], dst_ref=out[t],
            send_sem=sems[0].at[t], recv_sem=sems[1].at[t], device_id=(x, y, 1 - c), device_id_type=MESH) for t in range(self.n)]

    def start(self, src, out, sems):
        for cp in self._plan(src, out, sems):
            cp.start()

    def mid(self, src, out, sems):
        pass

    def finish(self, src, out, sems):
        for cp in self._plan(src, out, sems):
            cp.wait()


REACH_SIBLING, REACH_CHIPS, REACH_ALL = 0, 1, 2


def _entry_barrier(reach):
    x, y, c = _mesh_pos()
    peers = [(x, y, 1 - c)]
    if reach == REACH_CHIPS:
        peers += [(1 - x, y, c), (x, 1 - y, c)]
    elif reach == REACH_ALL:
        peers = [(x ^ dx, y ^ dy, c ^ dc) for dx in (0, 1) for dy in (0, 1) for dc in (0, 1)][1:]
    barrier = pltpu.get_barrier_semaphore()
    for peer in peers:
        pl.semaphore_signal(barrier, inc=1, device_id=peer, device_id_type=MESH)
    pl.semaphore_wait(barrier, len(peers))


def _call(body, hosted=(), *, name, in_specs, out_specs, out_shape, args, grid=(), scratch_shapes=()):
    n_in, n_out, n_scr = len(in_specs), len(out_specs), len(scratch_shapes)
    total = math.prod(grid)
    mid_step = max(0, (5 * total) // 8 - 1)

    def full(*refs):
        pos = [0]

        def take(k):
            pos[0] += k
            return refs[pos[0] - k:pos[0]]

        ins, h_in = take(n_in), [take(len(h.inputs)) for h in hosted]
        outs, h_out = take(n_out), [take(len(h.out_shape)) for h in hosted]
        scr, h_sem = take(n_scr), [take(len(h.sems)) for h in hosted]
        step = 0
        for axis, size in enumerate(grid):
            step = step * size + pl.program_id(axis)

        def phase(at, method):
            if not hosted:
                return

            def run():
                if method == "start":
                    _entry_barrier(reach)
                for h, s, o, m in zip(hosted, h_in, h_out, h_sem):
                    getattr(h, method)(s, o, m)

            if total == 1:
                run()
            else:
                pl.when(step == at)(run)

        phase(0, "start")
        body(*ins, *outs, *scr)
        phase(mid_step, "mid")
        phase(total - 1, "finish")

    aliases, i0, o0 = {}, n_in, n_out
    for h in hosted:
        aliases.update({i0 + i: o0 + o for i, o in h.aliases.items()})
        i0, o0 = i0 + len(h.inputs), o0 + len(h.out_shape)
    reach = max((h.reach for h in hosted), default=None)
    params = dict(vmem_limit_bytes=VMEM_LIMIT_V7X)
    if hosted:
        params["collective_id"] = reach
    results = pl.pallas_call(
        full, name=name, grid=grid,
        in_specs=list(in_specs) + [ANY] * (i0 - n_in),
        out_specs=list(out_specs) + [ANY] * (o0 - n_out),
        out_shape=list(out_shape) + [s for h in hosted for s in h.out_shape],
        scratch_shapes=list(scratch_shapes) + [s for h in hosted for s in h.sems],
        input_output_aliases=aliases,
        compiler_params=pltpu.CompilerParams(**params),
    )(*args, *[s for h in hosted for s in h.inputs])
    outs, extras, pos = list(results[:n_out]), [], n_out
    for h in hosted:
        extras.append(list(results[pos:pos + h.n]))
        pos += len(h.out_shape)
    return outs, extras


def exchange(hosted, name):
    return _call(lambda: None, hosted, name=name, in_specs=[], out_specs=[], out_shape=[], args=[])[1]


class AllExchange:
    def __init__(self, pack):
        self.inputs, self.n, self.aliases, self.reach = [pack], 1, {}, REACH_ALL
        self.out_shape = [jax.ShapeDtypeStruct((2 * N_CHIPS,) + pack.shape, pack.dtype)]
        self.sems = [pltpu.SemaphoreType.DMA, pltpu.SemaphoreType.DMA((7,)), pltpu.SemaphoreType.DMA((7,))]

    def _copies(self, src, out, sems):
        local_sem, send_sem, recv_sem = sems
        x, y, c = _mesh_pos()
        flips = [(dx, dy, dc) for dx in (0, 1) for dy in (0, 1) for dc in (0, 1)][1:]
        peers = [(x ^ dx, y ^ dy, c ^ dc) for dx, dy, dc in flips]
        remote = lambda s, d, k: pltpu.make_async_remote_copy(
            src_ref=s, dst_ref=d, send_sem=send_sem.at[k], recv_sem=recv_sem.at[k], device_id=peers[k], device_id_type=MESH)
        sends = [remote(src[0], out[0].at[4 * x + 2 * y + c], k) for k in range(7)]
        landed = [remote(out[0].at[4 * px + 2 * py + pc], out[0].at[4 * px + 2 * py + pc], k) for k, (px, py, pc) in enumerate(peers)]
        return sends, landed, pltpu.make_async_copy(src[0], out[0].at[4 * x + 2 * y + c], local_sem)

    def start(self, src, out, sems):
        sends, _, local = self._copies(src, out, sems)
        for cp in sends:
            cp.start()
        local.start()

    def mid(self, src, out, sems):
        pass

    def finish(self, src, out, sems):
        sends, landed, local = self._copies(src, out, sems)
        for cp in landed:
            cp.wait_recv()
        for cp in sends:
            cp.wait_send()
        local.wait()


MXU_COLS = 256


def _resident(shape):
    return pl.BlockSpec(shape, lambda *_: (0,) * len(shape), pipeline_mode=pl.Buffered(1))


def ffn_up(h, gain, w1, w3, name, hosted=()):
    T, D = h.shape
    F = w1.shape[0]
    tm = min(T, 512)

    def body(h_ref, g_ref, w1_ref, w3_ref, n_ref, ga_ref, gb_ref, s_ref):
        hh = h_ref[...]
        n = (hh * _rstd(hh) * g_ref[...]).astype(BF16)
        n_ref[...] = n
        for c in range(0, F, MXU_COLS):
            cols = slice(c, c + MXU_COLS)
            a = _dot_nt(n, w1_ref[cols, :])
            b = _dot_nt(n, w3_ref[cols, :])
            silu, dsilu = _silu_parts(a)
            ga_ref[:, cols] = (b * dsilu).astype(BF16)
            gb_ref[:, cols] = silu.astype(BF16)
            s_ref[:, cols] = (silu * b).astype(BF16)

    act = jax.ShapeDtypeStruct((T, F), BF16)
    act_spec = pl.BlockSpec((tm, F), lambda i: (i, 0))
    row_spec = pl.BlockSpec((tm, D), lambda i: (i, 0))
    return _call(
        body, hosted, name=name, grid=(T // tm,),
        in_specs=[row_spec, pl.BlockSpec((1, D), lambda i: (0, 0)), _resident((F, D)), _resident((F, D))],
        out_specs=[row_spec, act_spec, act_spec, act_spec],
        out_shape=[jax.ShapeDtypeStruct((T, D), BF16), act, act, act],
        args=[h, gain, w1, w3])


def ffn_down(s, w2, h, name, hosted=()):
    T, F = s.shape
    D = h.shape[1]
    tm = min(T, 512)

    def body(s_ref, w2_ref, h_ref, o_ref):
        o_ref[...] = h_ref[...] + 0.5 * _dot(s_ref[...], w2_ref[...])

    row_spec = pl.BlockSpec((tm, D), lambda i: (i, 0))
    return _call(
        body, hosted, name=name, grid=(T // tm,),
        in_specs=[pl.BlockSpec((tm, F), lambda i: (i, 0)), _resident((F, D)), row_spec],
        out_specs=[row_spec],
        out_shape=[jax.ShapeDtypeStruct((T, D), F32)],
        args=[s, w2, h])


def ffn_bwd_act(dh, w2, ga, gb, name, hosted=()):
    T, D = dh.shape
    F = w2.shape[0]
    tm = min(T, 512)

    def body(dh_ref, w2_ref, ga_ref, gb_ref, da_ref, db_ref, df_ref):
        df = (0.5 * dh_ref[...]).astype(BF16)
        df_ref[...] = df
        for c in range(0, F, MXU_COLS):
            cols = slice(c, c + MXU_COLS)
            ds = _dot_nt(df, w2_ref[cols, :])
            da_ref[:, cols] = (ds * ga_ref[:, cols].astype(F32)).astype(BF16)
            db_ref[:, cols] = (ds * gb_ref[:, cols].astype(F32)).astype(BF16)

    act = jax.ShapeDtypeStruct((T, F), BF16)
    act_spec = pl.BlockSpec((tm, F), lambda i: (i, 0))
    row_spec = pl.BlockSpec((tm, D), lambda i: (i, 0))
    return _call(
        body, hosted, name=name, grid=(T // tm,),
        in_specs=[row_spec, _resident((F, D)), act_spec, act_spec],
        out_specs=[act_spec, act_spec, row_spec],
        out_shape=[act, act, jax.ShapeDtypeStruct((T, D), BF16)],
        args=[dh, w2, ga, gb])


def ffn_dw(xs, y, halves, name, hosted=()):
    T, F = xs[0].shape
    D = y.shape[1]
    nx, fh = len(xs), F // halves
    tk = min(T, 512)
    nk = T // tk

    def body(*refs):
        y_ref, x_refs, o_refs, accs = refs[0], refs[1:1 + nx], refs[1 + nx:1 + 2 * nx], refs[1 + 2 * nx:]
        k = pl.program_id(1)

        @pl.when(k == 0)
        def _():
            for acc in accs:
                acc[...] = jnp.zeros_like(acc)

        yy = y_ref[...]
        for x_ref, acc in zip(x_refs, accs):
            acc[...] += _dot_tn(x_ref[...], yy)

        @pl.when(k == nk - 1)
        def _():
            for o_ref, acc in zip(o_refs, accs):
                o_ref[...] = acc[...].astype(BF16)

    out = jax.ShapeDtypeStruct((F, D), BF16)
    return _call(
        body, hosted, name=name, grid=(halves, nk),
        in_specs=[pl.BlockSpec((tk, D), lambda j, k: (k, 0))] + [pl.BlockSpec((tk, fh), lambda j, k: (k, j))] * nx,
        out_specs=[pl.BlockSpec((fh, D), lambda j, k: (j, 0))] * nx,
        out_shape=[out] * nx,
        scratch_shapes=[pltpu.VMEM((fh, D), F32)] * nx,
        args=[y] + list(xs))


def ffn_bwd_in(da, db, w1, w3, h, gain, dh, name, hosted=()):
    T, F = da.shape
    D = h.shape[1]
    tm = min(T, 512)

    def body(da_ref, db_ref, w1_ref, w3_ref, h_ref, g_ref, dh_ref, o_ref, dg_ref):
        dn = _dot(da_ref[...], w1_ref[...]) + _dot(db_ref[...], w3_ref[...])
        dhn, dg = _rmsnorm_bwd(dn, h_ref[...], g_ref[...])
        o_ref[...] = dh_ref[...] + dhn

        @pl.when(pl.program_id(0) == 0)
        def _():
            dg_ref[...] = jnp.zeros_like(dg_ref)

        dg_ref[...] += jnp.sum(dg, axis=0, keepdims=True)

    act_spec = pl.BlockSpec((tm, F), lambda i: (i, 0))
    row_spec = pl.BlockSpec((tm, D), lambda i: (i, 0))
    vec_spec = pl.BlockSpec((1, D), lambda i: (0, 0))
    return _call(
        body, hosted, name=name, grid=(T // tm,),
        in_specs=[act_spec, act_spec, _resident((F, D)), _resident((F, D)), row_spec, vec_spec, row_spec],
        out_specs=[row_spec, vec_spec],
        out_shape=[jax.ShapeDtypeStruct((T, D), F32), jax.ShapeDtypeStruct((1, D), F32)],
        args=[da, db, w1, w3, h, gain, dh])


def mix_in(h, gain, wing, name, hosted=()):
    T, D = h.shape
    nsh, _, Cs = wing.shape
    tm = min(T, 512)

    def body(h_ref, g_ref, w_ref, u_ref, p_ref):
        hh = h_ref[...]
        u = (hh * _rstd(hh) * g_ref[...]).astype(BF16)
        u_ref[...] = u
        for j in range(nsh):
            p_ref[:, j * Cs:(j + 1) * Cs] = _dot(u, w_ref[j])

    return _call(
        body, hosted, name=name, grid=(T // tm,),
        in_specs=[pl.BlockSpec((tm, D), lambda i: (i, 0)), pl.BlockSpec((1, D), lambda i: (0, 0)),
                  pl.BlockSpec((nsh, D, Cs), lambda i: (0, 0, 0))],
        out_specs=[pl.BlockSpec((tm, D), lambda i: (i, 0)), pl.BlockSpec((tm, nsh * Cs), lambda i: (i, 0))],
        out_shape=[jax.ShapeDtypeStruct((T, D), BF16), jax.ShapeDtypeStruct((T, nsh * Cs), F32)],
        args=[h, gain, wing])


def mix_out(a, b, woutg, h, name, hosted=()):
    T, W = a.shape
    D = h.shape[1]
    wout = woutg.reshape(2, W, D)
    tm = min(T, 512)

    def body(a_ref, b_ref, w_ref, h_ref, o_ref):
        o_ref[...] = h_ref[...] + _dot(a_ref[...], w_ref[0]) + _dot(b_ref[...], w_ref[1])

    return _call(
        body, hosted, name=name, grid=(T // tm,),
        in_specs=[pl.BlockSpec((tm, W), lambda i: (i, 0)), pl.BlockSpec((tm, W), lambda i: (i, 0)),
                  pl.BlockSpec((2, W, D), lambda i: (0, 0, 0)), pl.BlockSpec((tm, D), lambda i: (i, 0))],
        out_specs=[pl.BlockSpec((tm, D), lambda i: (i, 0))],
        out_shape=[jax.ShapeDtypeStruct((T, D), F32)],
        args=[a, b, wout, h])


def mix_out_bwd(dh, woutg, a, b, name, hosted=()):
    T, D = dh.shape
    W = a.shape[1]
    nsh, Rs, _ = woutg.shape
    wout = woutg.reshape(2, W, D)
    tk = min(T, 512)
    nk = T // tk

    def body(dh_ref, w_ref, a_ref, b_ref, da_ref, db_ref, dw_ref, acc):
        k = pl.program_id(0)

        @pl.when(k == 0)
        def _():
            acc[...] = jnp.zeros_like(acc)

        dhb = dh_ref[...].astype(BF16)
        da_ref[...] = _dot_nt(dhb, w_ref[0])
        db_ref[...] = _dot_nt(dhb, w_ref[1])
        acc[0:W, :] += _dot_tn(a_ref[...], dhb)
        acc[W:2 * W, :] += _dot_tn(b_ref[...], dhb)

        @pl.when(k == nk - 1)
        def _():
            for j in range(nsh):
                dw_ref[j] = acc[j * Rs:(j + 1) * Rs, :].astype(BF16)

    return _call(
        body, hosted, name=name, grid=(nk,),
        in_specs=[pl.BlockSpec((tk, D), lambda k: (k, 0)), pl.BlockSpec((2, W, D), lambda k: (0, 0, 0)),
                  pl.BlockSpec((tk, W), lambda k: (k, 0)), pl.BlockSpec((tk, W), lambda k: (k, 0))],
        out_specs=[pl.BlockSpec((tk, W), lambda k: (k, 0)), pl.BlockSpec((tk, W), lambda k: (k, 0)),
                   pl.BlockSpec((nsh, Rs, D), lambda k: (0, 0, 0))],
        out_shape=[jax.ShapeDtypeStruct((T, W), F32), jax.ShapeDtypeStruct((T, W), F32),
                   jax.ShapeDtypeStruct((nsh, Rs, D), BF16)],
        scratch_shapes=[pltpu.VMEM((2 * W, D), F32)],
        args=[dh, wout, a, b])


def mix_dwin(u, d, nsh, name, hosted=()):
    T, D = u.shape
    Cs = d.shape[1] // nsh
    tk = min(T, 512)
    nk = T // tk

    def body(u_ref, d_ref, o_ref, acc):
        k = pl.program_id(1)

        @pl.when(k == 0)
        def _():
            acc[...] = jnp.zeros_like(acc)

        acc[...] += _dot_tn(u_ref[...], d_ref[...])

        @pl.when(k == nk - 1)
        def _():
            o_ref[0] = acc[...].astype(BF16)

    return _call(
        body, hosted, name=name, grid=(nsh, nk),
        in_specs=[pl.BlockSpec((tk, D), lambda j, k: (k, 0)), pl.BlockSpec((tk, Cs), lambda j, k: (k, j))],
        out_specs=[pl.BlockSpec((1, D, Cs), lambda j, k: (j, 0, 0))],
        out_shape=[jax.ShapeDtypeStruct((nsh, D, Cs), BF16)],
        scratch_shapes=[pltpu.VMEM((D, Cs), F32)],
        args=[u, d])


def mix_in_bwd(d, wing, h, gain, dh, name, hosted=()):
    T, D = h.shape
    nsh, _, Cs = wing.shape
    tm = min(T, 512)

    def body(d_ref, w_ref, h_ref, g_ref, dh_ref, o_ref, dg_ref):
        du = _dot_nt(d_ref[:, 0:Cs], w_ref[0])
        for j in range(1, nsh):
            du += _dot_nt(d_ref[:, j * Cs:(j + 1) * Cs], w_ref[j])
        dhn, dg = _rmsnorm_bwd(du, h_ref[...], g_ref[...])
        o_ref[...] = dh_ref[...] + dhn

        @pl.when(pl.program_id(0) == 0)
        def _():
            dg_ref[...] = jnp.zeros_like(dg_ref)

        dg_ref[...] += jnp.sum(dg, axis=0, keepdims=True)

    row_spec = pl.BlockSpec((tm, D), lambda i: (i, 0))
    vec_spec = pl.BlockSpec((1, D), lambda i: (0, 0))
    return _call(
        body, hosted, name=name, grid=(T // tm,),
        in_specs=[pl.BlockSpec((tm, nsh * Cs), lambda i: (i, 0)), pl.BlockSpec((nsh, D, Cs), lambda i: (0, 0, 0)),
                  row_spec, vec_spec, row_spec],
        out_specs=[row_spec, vec_spec],
        out_shape=[jax.ShapeDtypeStruct((T, D), F32), jax.ShapeDtypeStruct((1, D), F32)],
        args=[d, wing, h, gain, dh])


def _pool_window(x, group, T, trailing):
    rows = lax.broadcasted_iota(jnp.int32, x.shape, 0)

    def shifted(z, k):
        if trailing:
            return jnp.where(rows >= k, pltpu.roll(z, k, 0), 0.0)
        return jnp.where(rows < T - k, pltpu.roll(z, T - k, 0), 0.0)

    s2 = x + shifted(x, 1)
    s4 = s2 + shifted(s2, 2)
    s8 = s4 + shifted(s4, 4)
    s16 = s8 + shifted(s8, 8)
    return jnp.where(group == 0, s2, jnp.where(group == 1, s4, jnp.where(group == 2, s8, s16)))


def _pool_count(group, shape):
    rows = lax.broadcasted_iota(jnp.int32, shape, 0)
    w = jnp.where(group == 0, 2, jnp.where(group == 1, 4, jnp.where(group == 2, 8, 16)))
    return jnp.minimum(rows + 1, w).astype(F32)


def pool_fwd(proj, pool_w, pool_scale, name, hosted=()):
    T = proj.shape[0]
    Hd = HEAD_DIM

    def body(x_ref, w_ref, sc_ref, a_ref):
        g = pl.program_id(0)
        x = x_ref[...]
        pooled = _pool_window(x, g, T, True) / _pool_count(g, x.shape) - x
        a_ref[...] = (_dot(pooled.astype(BF16), w_ref[0].astype(BF16)) * sc_ref[...]).astype(BF16)

    return _call(
        body, hosted, name=name, grid=(N_GROUPS,),
        in_specs=[pl.BlockSpec((T, Hd), lambda g: (0, g)), pl.BlockSpec((1, Hd, Hd), lambda g: (g, 0, 0)),
                  pl.BlockSpec((1, Hd), lambda g: (0, g))],
        out_specs=[pl.BlockSpec((T, Hd), lambda g: (0, g))],
        out_shape=[jax.ShapeDtypeStruct((T, N_GROUPS * Hd), BF16)],
        args=[proj, pool_w, pool_scale])


def pool_bwd(proj, da, pool_w, pool_scale, name, hosted=()):
    T = proj.shape[0]
    Hd = HEAD_DIM

    def body(x_ref, da_ref, w_ref, sc_ref, dx_ref, dw_ref, dsc_ref):
        g = pl.program_id(0)
        x = x_ref[...]
        cnt = _pool_count(g, x.shape)
        pooled = (_pool_window(x, g, T, True) / cnt - x).astype(BF16)
        wb = w_ref[0].astype(BF16)
        dav = da_ref[...]
        dsc_ref[...] = jnp.sum(dav * _dot(pooled, wb), axis=0, keepdims=True)
        dout = (dav * sc_ref[...]).astype(BF16)
        dw_ref[0] = _dot_tn(pooled, dout)
        dpooled = _dot_nt(dout, wb)
        dx_ref[...] = (_pool_window(dpooled / cnt, g, T, False) - dpooled).astype(BF16)

    col_spec = pl.BlockSpec((T, Hd), lambda g: (0, g))
    return _call(
        body, hosted, name=name, grid=(N_GROUPS,),
        in_specs=[col_spec, col_spec, pl.BlockSpec((1, Hd, Hd), lambda g: (g, 0, 0)), pl.BlockSpec((1, Hd), lambda g: (0, g))],
        out_specs=[col_spec, pl.BlockSpec((1, Hd, Hd), lambda g: (g, 0, 0)), pl.BlockSpec((1, Hd), lambda g: (0, g))],
        out_shape=[jax.ShapeDtypeStruct((T, N_GROUPS * Hd), BF16), jax.ShapeDtypeStruct((N_GROUPS, Hd, Hd), F32),
                   jax.ShapeDtypeStruct((1, N_GROUPS * Hd), F32)],
        args=[proj, da, pool_w, pool_scale])


def _ret_tables(T):
    Hd, C = HEAD_DIM, RET_CHUNK
    inv_freq = 1.0 / (ROPE_BASE ** (jnp.arange(0, Hd, 2, dtype=F32) / Hd))
    ang = jnp.arange(T, dtype=F32)[:, None] * inv_freq[None, :]
    cos, sin = jnp.cos(ang), jnp.sin(ang)
    cos2 = jnp.concatenate([cos, cos], axis=-1)
    sin2 = jnp.concatenate([-sin, sin], axis=-1)
    log_gamma = jnp.log1p(-jnp.exp2(-5.0 - jnp.arange(N_GROUPS, dtype=F32)))
    pos = jnp.arange(C, dtype=F32)
    rel = pos[:, None] - pos[None, :]
    intra = jnp.where(rel[None] >= 0, jnp.exp(log_gamma[:, None, None] * jnp.maximum(rel, 0.0)[None]), 0.0)
    k_tail = jnp.exp(log_gamma[:, None] * (C - 1 - pos)[None, :])
    q_head = jnp.exp(log_gamma[:, None] * (pos + 1.0)[None, :])
    chunk_decay = jnp.exp(log_gamma * C)
    wide = lambda t: jnp.broadcast_to(t[:, :, None], (N_GROUPS, C, Hd))
    return cos2, sin2, intra, wide(k_tail), wide(q_head), jnp.broadcast_to(chunk_decay[:, None, None], (N_GROUPS, 1, Hd))


def _rope(x, cos2, sin2):
    return x * cos2 + pltpu.roll(x, HEAD_DIM // 2, 1) * sin2


def _rope_t(d, cos2, sin2):
    return d * cos2 + pltpu.roll(d * sin2, HEAD_DIM // 2, 1)


def _ret_specs(tseg, seg_of):
    Hd, G = HEAD_DIM, N_GROUPS
    col = lambda kind: pl.BlockSpec((tseg, Hd), lambda h, s: (seg_of(s), G * kind + h))
    tab = pl.BlockSpec((tseg, Hd), lambda h, s: (seg_of(s), 0))
    head = pl.BlockSpec((1, RET_CHUNK, Hd), lambda h, s: (h, 0, 0))
    cd = pl.BlockSpec((1, 1, Hd), lambda h, s: (h, 0, 0))
    gain = pl.BlockSpec((1, Hd), lambda h, s: (0, h))
    return col, tab, head, cd, gain


def ret_fwd(proj, ret_norm, tables, name, hosted=()):
    T = proj.shape[0]
    Hd, C, G = HEAD_DIM, RET_CHUNK, N_GROUPS
    tseg = min(T, 1024)
    nseg, nck = T // tseg, tseg // C
    scale = Hd ** -0.5
    cos2, sin2, intra, k_tail, q_head, chunk_decay = tables

    def body(q_ref, k_ref, v_ref, g_ref, gain_ref, cos_ref, sin_ref, m_ref, kt_ref, qh_ref, cd_ref,
             b_ref, o_ref, rp_ref, state):
        @pl.when(pl.program_id(1) == 0)
        def _():
            state[...] = jnp.zeros_like(state)

        def chunk(ci, carry):
            rows = pl.ds(pl.multiple_of(ci * C, C), C)
            cos, sin = cos_ref[rows, :], sin_ref[rows, :]
            qr = _rope(q_ref[rows, :], cos, sin)
            kr = _rope(k_ref[rows, :], cos, sin) * scale
            qb, kb, vb = qr.astype(BF16), kr.astype(BF16), v_ref[rows, :].astype(BF16)
            r = state[...]
            rp_ref[0, ci] = r.astype(BF16)
            sc = _dot_nt(qb, kb) * m_ref[0]
            o = _dot(sc.astype(BF16), vb) + _dot((qr * qh_ref[0]).astype(BF16), r.astype(BF16))
            state[...] = cd_ref[0] * r + _dot_tn((kr * kt_ref[0]).astype(BF16), vb)
            o_ref[rows, :] = o
            on = o * _rstd(o)
            b_ref[rows, :] = (jax.nn.silu(g_ref[rows, :]) * (on * gain_ref[...])).astype(BF16)
            return carry

        lax.fori_loop(0, nck, chunk, 0, unroll=True)

    col, tab, head, cd, gain = _ret_specs(tseg, lambda s: s)
    out_col = pl.BlockSpec((tseg, Hd), lambda h, s: (s, h))
    return _call(
        body, hosted, name=name, grid=(G, nseg),
        in_specs=[col(1), col(2), col(3), col(4), gain, tab, tab, head, head, head, cd],
        out_specs=[out_col, out_col, pl.BlockSpec((1, nck, Hd, Hd), lambda h, s: (h, s, 0, 0))],
        out_shape=[jax.ShapeDtypeStruct((T, G * Hd), BF16), jax.ShapeDtypeStruct((T, G * Hd), F32),
                   jax.ShapeDtypeStruct((G, T // C, Hd, Hd), BF16)],
        scratch_shapes=[pltpu.VMEM((Hd, Hd), F32)],
        args=[proj, proj, proj, proj, ret_norm, cos2, sin2, intra, k_tail, q_head, chunk_decay])


def ret_bwd(proj, db, o_pre, r_prev, ret_norm, tables, name, hosted=()):
    T = proj.shape[0]
    Hd, C, G = HEAD_DIM, RET_CHUNK, N_GROUPS
    tseg = min(T, 1024)
    nseg, nck = T // tseg, tseg // C
    scale = Hd ** -0.5
    cos2, sin2, intra, k_tail, q_head, chunk_decay = tables

    def body(q_ref, k_ref, v_ref, g_ref, db_ref, o_ref, rp_ref, gain_ref, cos_ref, sin_ref, m_ref, kt_ref, qh_ref, cd_ref,
             d_ref, dgain_ref, gstate):
        @pl.when(pl.program_id(1) == 0)
        def _():
            gstate[...] = jnp.zeros_like(gstate)
            dgain_ref[...] = jnp.zeros_like(dgain_ref)

        def chunk(t, carry):
            ci = nck - 1 - t
            rows = pl.ds(pl.multiple_of(ci * C, C), C)
            cos, sin = cos_ref[rows, :], sin_ref[rows, :]
            qr = _rope(q_ref[rows, :], cos, sin)
            kr = _rope(k_ref[rows, :], cos, sin) * scale
            qb, kb, vb = qr.astype(BF16), kr.astype(BF16), v_ref[rows, :].astype(BF16)
            qhb, ktb = (qr * qh_ref[0]).astype(BF16), (kr * kt_ref[0]).astype(BF16)
            sc = (_dot_nt(qb, kb) * m_ref[0]).astype(BF16)
            o = o_ref[rows, :]
            rstd = _rstd(o)
            on = o * rstd
            gain = gain_ref[...]
            silu, dsilu = _silu_parts(g_ref[rows, :])
            dy = db_ref[rows, :]
            dgain_ref[...] += jnp.sum(dy * silu * on, axis=0, keepdims=True)
            dg = dy * on * gain * dsilu
            don = dy * silu * gain
            dob = (rstd * (don - on * jnp.mean(don * on, axis=-1, keepdims=True))).astype(BF16)
            gn = gstate[...]
            gb = gn.astype(BF16)
            da = (_dot_nt(dob, vb) * m_ref[0]).astype(BF16)
            dq = _dot(da, kb) + _dot_nt(dob, rp_ref[0, ci]) * qh_ref[0]
            dk = _dot_tn(da, qb) + _dot_nt(vb, gb) * kt_ref[0]
            dv = _dot_tn(sc, dob) + _dot(ktb, gb)
            gstate[...] = cd_ref[0] * gn + _dot_tn(qhb, dob)
            d_ref[0, rows, :] = _rope_t(dq, cos, sin).astype(BF16)
            d_ref[1, rows, :] = _rope_t(dk * scale, cos, sin).astype(BF16)
            d_ref[2, rows, :] = dv.astype(BF16)
            d_ref[3, rows, :] = dg.astype(BF16)
            return carry

        lax.fori_loop(0, nck, chunk, 0, unroll=True)

    rev = lambda s: nseg - 1 - s
    col, tab, head, cd, gain = _ret_specs(tseg, rev)
    act = pl.BlockSpec((tseg, Hd), lambda h, s: (rev(s), h))
    return _call(
        body, hosted, name=name, grid=(G, nseg),
        in_specs=[col(1), col(2), col(3), col(4), act, act, pl.BlockSpec((1, nck, Hd, Hd), lambda h, s: (h, rev(s), 0, 0)),
                  gain, tab, tab, head, head, head, cd],
        out_specs=[pl.BlockSpec((4, tseg, Hd), lambda h, s: (0, rev(s), h)), gain],
        out_shape=[jax.ShapeDtypeStruct((4, T, G * Hd), BF16), jax.ShapeDtypeStruct((1, G * Hd), F32)],
        scratch_shapes=[pltpu.VMEM((Hd, Hd), F32)],
        args=[proj, proj, proj, proj, db, o_pre, r_prev, ret_norm, cos2, sin2, intra, k_tail, q_head, chunk_decay])


def final_loss(h, gain, target, name, hosted=()):
    T, D = h.shape
    tm = min(T, 512)

    def body(h_ref, g_ref, t_ref, dh_ref, loss_ref, dg_ref):
        @pl.when(pl.program_id(0) == 0)
        def _():
            loss_ref[...] = jnp.zeros_like(loss_ref)
            dg_ref[...] = jnp.zeros_like(dg_ref)

        hh = h_ref[...]
        gain_v = g_ref[...]
        err = hh * _rstd(hh) * gain_v - t_ref[...]
        loss_ref[...] += 0.5 * jnp.sum(jnp.mean(err * err, axis=-1, keepdims=True), axis=0, keepdims=True)
        dhn, dg = _rmsnorm_bwd(err * (1.0 / D), hh, gain_v)
        dh_ref[...] = dhn
        dg_ref[...] += jnp.sum(dg, axis=0, keepdims=True)

    row_spec = pl.BlockSpec((tm, D), lambda i: (i, 0))
    vec_spec = pl.BlockSpec((1, D), lambda i: (0, 0))
    return _call(
        body, hosted, name=name, grid=(T // tm,),
        in_specs=[row_spec, vec_spec, row_spec],
        out_specs=[row_spec, pl.BlockSpec((1, 128), lambda i: (0, 0)), vec_spec],
        out_shape=[jax.ShapeDtypeStruct((T, D), F32), jax.ShapeDtypeStruct((1, 128), F32), jax.ShapeDtypeStruct((1, D), F32)],
        args=[h, gain, target])


def prereduce(grad, recv, place, name):
    nsh, R, C = grad.shape
    rh = R // 2

    def body(place_ref, g_ref, r_ref, o_ref, own_ref):
        piece = (g_ref[...].astype(F32) + r_ref[...].astype(F32)).astype(BF16)
        o_ref[...] = piece

        @pl.when(pl.program_id(0) == place_ref[1])
        def _():
            own_ref[...] = piece

    return pl.pallas_call(
        body, name=name,
        grid_spec=pltpu.PrefetchScalarGridSpec(
            num_scalar_prefetch=1, grid=(nsh,),
            in_specs=[pl.BlockSpec((1, rh, C), lambda j, p: (j, p[0], 0)), pl.BlockSpec((1, rh, C), lambda j, p: (j, 0, 0))],
            out_specs=[pl.BlockSpec((1, rh, C), lambda j, p: (j, 0, 0)), pl.BlockSpec((1, rh, C), lambda j, p: (p[1], p[0], 0))]),
        out_shape=[jax.ShapeDtypeStruct((nsh, rh, C), BF16), jax.ShapeDtypeStruct((nsh, R, C), BF16)],
        compiler_params=pltpu.CompilerParams(vmem_limit_bytes=VMEM_LIMIT_V7X),
    )(place, grad, recv)


def _adamw(w, g, m, v):
    m = ADAM_B1 * m + (1.0 - ADAM_B1) * g
    v = ADAM_B2 * v + (1.0 - ADAM_B2) * (g * g)
    m_hat = m / (1.0 - ADAM_B1 ** ADAM_STEP)
    v_hat = v / (1.0 - ADAM_B2 ** ADAM_STEP)
    return -ADAM_LR * (m_hat / (jnp.sqrt(v_hat) + ADAM_EPS) + ADAM_WD * w), m, v


def adamw_sharded(tensors, name, hosted=()):
    nt = len(tensors)
    nsh, R, C = tensors[0][0].shape
    lanes = -(-C // 128) * 128
    per_row = 2 * nt * lanes * (nsh * 2 + 7 * 4)
    tr = max(r for r in range(16, R + 1, 16) if R % r == 0 and r * per_row <= ADAMW_VMEM_BUDGET)

    def body(*refs):
        ins, outs = refs[:4 * nt], refs[4 * nt:]
        for t in range(nt):
            p_ref, w_ref, m_ref, v_ref = ins[4 * t:4 * t + 4]
            g_ref, d_ref, nm_ref, nv_ref = outs[4 * t:4 * t + 4]
            g = p_ref[0].astype(F32)
            for i in range(1, nsh):
                g += p_ref[i].astype(F32)
            g_ref[...] = g
            d_ref[...], nm_ref[...], nv_ref[...] = _adamw(w_ref[...], g, m_ref[...], v_ref[...])

    spec = pl.BlockSpec((tr, C), lambda i: (i, 0))
    out = jax.ShapeDtypeStruct((R, C), F32)
    return _call(
        body, hosted, name=name, grid=(R // tr,),
        in_specs=[pl.BlockSpec((nsh, tr, C), lambda i: (0, i, 0)), spec, spec, spec] * nt,
        out_specs=[spec] * (4 * nt), out_shape=[out] * (4 * nt),
        args=[a for tensor in tensors for a in tensor])


def adamw_small(packs, pool, vectors, name):
    ndev = packs.shape[0]
    rp, rv = pool[0].shape[0], vectors[0].shape[0]

    def body(p_ref, wp, mp, vp, wv, mv, vv, gp, dp, nmp, nvp, gv, dv, nmv, nvv, loss_ref):
        g = p_ref[0]
        for i in range(1, ndev):
            g += p_ref[i]
        gp[...], gv[...], loss_ref[...] = g[0:rp], g[rp:rp + rv], g[rp + rv:rp + rv + 8]
        dp[...], nmp[...], nvp[...] = _adamw(wp[...], g[0:rp], mp[...], vp[...])
        dv[...], nmv[...], nvv[...] = _adamw(wv[...], g[rp:rp + rv], mv[...], vv[...])

    shape = lambda rows: jax.ShapeDtypeStruct((rows, 128), F32)
    outs = pl.pallas_call(body, name=name, out_shape=[shape(rp)] * 4 + [shape(rv)] * 4 + [shape(8)],
                          compiler_params=pltpu.CompilerParams(vmem_limit_bytes=VMEM_LIMIT_V7X))(packs, *pool, *vectors)
    return outs[0:4], outs[4:8], outs[8]


BIG = ("ffn1_w1", "ffn1_w3", "ffn1_w2", "w_in", "w_out", "ffn2_w1", "ffn2_w3", "ffn2_w2")
TRANSPOSED = ("ffn1_w1", "ffn1_w3", "ffn2_w1", "ffn2_w3")
VECTORS = ("ffn1_norm", "mix_norm", "pool_scale", "ret_norm", "ffn2_norm", "final_norm")
WEIGHTS = ("ffn1_norm", "ffn1_w1", "ffn1_w3", "ffn1_w2", "mix_norm", "w_in", "pool_w", "pool_scale", "ret_norm", "w_out",
           "ffn2_norm", "ffn2_w1", "ffn2_w3", "ffn2_w2", "final_norm")


def _pack_vectors(parts):
    return jnp.concatenate([parts[k].reshape(-1, 128) for k in VECTORS], axis=0)


def _unpack_vectors(pack, like):
    out, row = {}, 0
    for k in VECTORS:
        rows = like[k].size // 128
        out[k] = pack[row:row + rows].reshape(like[k].shape)
        row += rows
    return out


def kernel(x, ffn1_norm, ffn1_w1, ffn1_w3, ffn1_w2, mix_norm, w_in, pool_w, pool_scale, ret_norm, w_out, ffn2_norm, ffn2_w1, ffn2_w3, ffn2_w2, final_norm, loss_target, m_ffn1_norm, m_ffn1_w1, m_ffn1_w3, m_ffn1_w2, m_mix_norm, m_w_in, m_pool_w, m_pool_scale, m_ret_norm, m_w_out, m_ffn2_norm, m_ffn2_w1, m_ffn2_w3, m_ffn2_w2, m_final_norm, v_ffn1_norm, v_ffn1_w1, v_ffn1_w3, v_ffn1_w2, v_mix_norm, v_w_in, v_pool_w, v_pool_scale, v_ret_norm, v_w_out, v_ffn2_norm, v_ffn2_w1, v_ffn2_w3, v_ffn2_w2, v_final_norm):
    w = dict(ffn1_norm=ffn1_norm, ffn1_w1=ffn1_w1, ffn1_w3=ffn1_w3, ffn1_w2=ffn1_w2, mix_norm=mix_norm, w_in=w_in, pool_w=pool_w,
             pool_scale=pool_scale, ret_norm=ret_norm, w_out=w_out, ffn2_norm=ffn2_norm, ffn2_w1=ffn2_w1, ffn2_w3=ffn2_w3,
             ffn2_w2=ffn2_w2, final_norm=final_norm)
    m = dict(ffn1_norm=m_ffn1_norm, ffn1_w1=m_ffn1_w1, ffn1_w3=m_ffn1_w3, ffn1_w2=m_ffn1_w2, mix_norm=m_mix_norm, w_in=m_w_in,
             pool_w=m_pool_w, pool_scale=m_pool_scale, ret_norm=m_ret_norm, w_out=m_w_out, ffn2_norm=m_ffn2_norm, ffn2_w1=m_ffn2_w1,
             ffn2_w3=m_ffn2_w3, ffn2_w2=m_ffn2_w2, final_norm=m_final_norm)
    v = dict(ffn1_norm=v_ffn1_norm, ffn1_w1=v_ffn1_w1, ffn1_w3=v_ffn1_w3, ffn1_w2=v_ffn1_w2, mix_norm=v_mix_norm, w_in=v_w_in,
             pool_w=v_pool_w, pool_scale=v_pool_scale, ret_norm=v_ret_norm, w_out=v_w_out, ffn2_norm=v_ffn2_norm, ffn2_w1=v_ffn2_w1,
             ffn2_w3=v_ffn2_w3, ffn2_w2=v_ffn2_w2, final_norm=v_final_norm)
    xs, target = x[0], loss_target[0]
    T = xs.shape[0]
    tables = _ret_tables(T)
    place = jnp.stack([lax.axis_index("c"), 2 * lax.axis_index("x") + lax.axis_index("y")]).astype(jnp.int32)
    local = lambda d, k: jnp.transpose(d[k][0]) if k in TRANSPOSED else d[k][0]
    result = lambda o, k: jnp.transpose(o)[None] if k in TRANSPOSED else o[None]
    sh = {k: local(w, k).astype(BF16) for k in BIG}
    gather = lambda *names: [ChipExchange([sh[k] for k in names], False)]
    wg, grad, delta, new_m, new_v = {}, {}, {}, {}, {}

    def update(names, pieces, name, hosted=()):
        outs, extras = adamw_sharded([(p, local(w, k), local(m, k), local(v, k)) for k, p in zip(names, pieces)], name, hosted)
        for t, k in enumerate(names):
            grad[k], delta[k], new_m[k], new_v[k] = [result(o, k) for o in outs[4 * t:4 * t + 4]]
        return extras

    def reduce_in_chip(name, partial, recv):
        return prereduce(partial, recv, place, "prereduce_" + name)

    scatter = lambda *reduced: ChipExchange([r[0] for r in reduced], True, [r[1] for r in reduced])
    whole = lambda k: wg[k].reshape(-1, wg[k].shape[-1])
    sharded = lambda g: g.reshape(N_CHIPS, -1, g.shape[-1])

    (wg["ffn1_w1"], wg["ffn1_w3"]), = exchange(gather("ffn1_w1", "ffn1_w3"), "gather_ffn1")
    (n1, ga1, gb1, s1), ((wg["ffn1_w2"], wg["w_in"]),) = ffn_up(
        xs, ffn1_norm, whole("ffn1_w1"), whole("ffn1_w3"), "ffn1_up", gather("ffn1_w2", "w_in"))
    (h1,), ((wg["w_out"],),) = ffn_down(s1, whole("ffn1_w2"), xs, "ffn1_down", gather("w_out"))
    (u, proj), ((wg["ffn2_w1"],),) = mix_in(h1, mix_norm, wg["w_in"], "mix_in", gather("ffn2_w1"))
    (pa,), _ = pool_fwd(proj, pool_w[0], pool_scale, "pool_fwd")
    (rb, o_pre, r_prev), ((wg["ffn2_w3"],),) = ret_fwd(proj, ret_norm, tables, "ret_fwd", gather("ffn2_w3"))
    (h2,), _ = mix_out(pa, rb, wg["w_out"], h1, "mix_out")
    (n2, ga2, gb2, s2), ((wg["ffn2_w2"],),) = ffn_up(
        h2, ffn2_norm, whole("ffn2_w1"), whole("ffn2_w3"), "ffn2_up", gather("ffn2_w2"))
    (h3,), _ = ffn_down(s2, whole("ffn2_w2"), h2, "ffn2_down")
    (dh3, loss, d_final), _ = final_loss(h3, final_norm[None], target, "final_loss")

    (da2, db2, df2), _ = ffn_bwd_act(dh3, whole("ffn2_w2"), ga2, gb2, "ffn2_bwd_act")
    (g_f2w2,), _ = ffn_dw([s2], df2, 1, "ffn2_dw2")
    g_f2w2 = sharded(g_f2w2)
    (g_f2w1, g_f2w3), ((r_f2w2,),) = ffn_dw([da2, db2], n2, 2, "ffn2_dw13", [SiblingExchange([g_f2w2])])
    g_f2w1, g_f2w3 = sharded(g_f2w1), sharded(g_f2w3)
    p_f2w2 = reduce_in_chip("ffn2_w2", g_f2w2, r_f2w2)
    (dh2, d_ffn2), ((q_f2w2,), (r_f2w1, r_f2w3)) = ffn_bwd_in(
        da2, db2, whole("ffn2_w1"), whole("ffn2_w3"), h2, ffn2_norm, dh3, "ffn2_bwd_in",
        [scatter(p_f2w2), SiblingExchange([g_f2w1, g_f2w3])])
    p_f2w1 = reduce_in_chip("ffn2_w1", g_f2w1, r_f2w1)
    p_f2w3 = reduce_in_chip("ffn2_w3", g_f2w3, r_f2w3)
    (dpa, drb, g_wout), _ = mix_out_bwd(dh2, wg["w_out"], pa, rb, "mix_out_bwd")
    (dpool, d_pool_w, d_pool_scale), _ = pool_bwd(proj, dpa, pool_w[0], pool_scale, "pool_bwd")
    (dqkvg, d_ret_norm), ((q_f2w1, q_f2w3), (r_wout,)) = ret_bwd(
        proj, drb, o_pre, r_prev, ret_norm, tables, "ret_bwd", [scatter(p_f2w1, p_f2w3), SiblingExchange([g_wout])])
    p_wout = reduce_in_chip("w_out", g_wout, r_wout)
    d = jnp.concatenate([dpool, dqkvg[0], dqkvg[1], dqkvg[2], dqkvg[3]], axis=1)
    (g_win,), ((q_wout,),) = mix_dwin(u, d, N_CHIPS, "mix_dwin", [scatter(p_wout)])
    (dh1, d_mix), ((r_win,),) = mix_in_bwd(d, wg["w_in"], h1, mix_norm, dh2, "mix_in_bwd", [SiblingExchange([g_win])])
    p_win = reduce_in_chip("w_in", g_win, r_win)
    (da1, db1, df1), ((q_win,),) = ffn_bwd_act(dh1, whole("ffn1_w2"), ga1, gb1, "ffn1_bwd_act", [scatter(p_win)])
    (g_f1w1, g_f1w3), _ = ffn_dw([da1, db1], n1, 2, "ffn1_dw13")
    g_f1w1, g_f1w3 = sharded(g_f1w1), sharded(g_f1w3)
    (g_f1w2,), ((r_f1w1, r_f1w3),) = ffn_dw([s1], df1, 1, "ffn1_dw2", [SiblingExchange([g_f1w1, g_f1w3])])
    g_f1w2 = sharded(g_f1w2)
    p_f1w1 = reduce_in_chip("ffn1_w1", g_f1w1, r_f1w1)
    p_f1w3 = reduce_in_chip("ffn1_w3", g_f1w3, r_f1w3)
    (dx, d_ffn1), ((q_f1w1, q_f1w3), (r_f1w2,)) = ffn_bwd_in(
        da1, db1, whole("ffn1_w1"), whole("ffn1_w3"), xs, ffn1_norm, dh1, "ffn1_bwd_in",
        [scatter(p_f1w1, p_f1w3), SiblingExchange([g_f1w2])])
    p_f1w2 = reduce_in_chip("ffn1_w2", g_f1w2, r_f1w2)

    d_vectors = {"ffn1_norm": d_ffn1, "mix_norm": d_mix, "pool_scale": d_pool_scale, "ret_norm": d_ret_norm,
                 "ffn2_norm": d_ffn2, "final_norm": d_final}
    pack = jnp.concatenate([d_pool_w.reshape(-1, 128), _pack_vectors(d_vectors), jnp.broadcast_to(loss, (8, 128))], axis=0)
    (q_f1w2,), (packs,) = update(["ffn2_w1", "ffn2_w3", "ffn1_w1", "ffn1_w3"], [q_f2w1, q_f2w3, q_f1w1, q_f1w3], "adamw_w13",
                                 [scatter(p_f1w2), AllExchange(pack)])
    update(["ffn2_w2", "ffn1_w2"], [q_f2w2, q_f1w2], "adamw_w2")
    update(["w_in"], [q_win], "adamw_w_in")
    update(["w_out"], [q_wout], "adamw_w_out")
    of_pool, of_vectors, loss_sum = adamw_small(packs, [t["pool_w"].reshape(-1, 128) for t in (w, m, v)],
                                                [_pack_vectors(t) for t in (w, m, v)], "adamw_small")
    for res, pool_part, vector_part in zip((grad, delta, new_m, new_v), of_pool, of_vectors):
        res["pool_w"] = pool_part.reshape(pool_w.shape)
        res.update(_unpack_vectors(vector_part, w))
    loss = loss_sum[0, 0]

    return (loss, dx[None], *[grad[k] for k in WEIGHTS], *[delta[k] for k in WEIGHTS],
            *[new_m[k] for k in WEIGHTS], *[new_v[k] for k in WEIGHTS])
```

```python
import math

import jax
import jax.numpy as jnp
from jax import lax
from jax.experimental import pallas as pl
from jax.experimental.pallas import tpu as pltpu

F32 = jnp.float32
BF16 = jnp.bfloat16

EPS = 1e-6
N_CHIPS = 4
N_GROUPS = 4
HEAD_DIM = 128
RET_CHUNK = 128
ROPE_BASE = 10000.0
ADAM_LR, ADAM_B1, ADAM_B2, ADAM_EPS, ADAM_WD, ADAM_STEP = 0.001, 0.9, 0.999, 1e-08, 0.01, 10
VMEM_LIMIT_V7X = 56 * 1024 * 1024
ADAMW_VMEM_BUDGET = 32 * 1024 * 1024
MESH = pl.DeviceIdType.MESH
ANY = pl.BlockSpec(memory_space=pl.ANY)


def _dot(a, b):
    return jnp.dot(a, b, preferred_element_type=F32)


def _dot_nt(a, b):
    return lax.dot_general(a, b, (((1,), (1,)), ((), ())), preferred_element_type=F32)


def _dot_tn(a, b):
    return lax.dot_general(a, b, (((0,), (0,)), ((), ())), preferred_element_type=F32)


def _rstd(h):
    return lax.rsqrt(jnp.mean(h * h, axis=-1, keepdims=True) + EPS)


def _rmsnorm_bwd(dn, h, gain):
    r = _rstd(h)
    nh = h * r
    dnh = dn * gain
    dh = r * (dnh - nh * jnp.mean(dnh * nh, axis=-1, keepdims=True))
    return dh, dn * nh


def _silu_parts(a):
    sig = jax.nn.sigmoid(a)
    silu = a * sig
    return silu, sig + silu * (1.0 - sig)


def _mesh_pos():
    return lax.axis_index("x"), lax.axis_index("y"), lax.axis_index("c")


class ChipExchange:
    def __init__(self, srcs, scatter, placed=()):
        n = len(srcs)
        self.inputs, self.scatter, self.n, self.reach = list(srcs) + list(placed), scatter, n, REACH_CHIPS
        self.aliases = {n + t: t for t in range(n)} if scatter else {}
        self.half_rows = [s.shape[1] if scatter else s.shape[0] // 2 for s in srcs]
        self.out_shape = [jax.ShapeDtypeStruct((N_CHIPS, 2 * rh, s.shape[-1]), s.dtype) for s, rh in zip(srcs, self.half_rows)]
        if scatter:
            self.out_shape += [jax.ShapeDtypeStruct((2, rh // 2, s.shape[-1]), s.dtype) for s, rh in zip(srcs, self.half_rows)]
        dma = pltpu.SemaphoreType.DMA
        self.sems = [dma((4 * n,)), dma((4 * n,)), dma((2 * n,)), dma((2 * n,)), dma((4 * n,)), dma((4 * n,))]

    def _copies(self, src, out, sems):
        hop1_send, hop1_recv, hop2_send, hop2_recv, d2d_send, d2d_recv = sems
        x, y, c = _mesh_pos()
        me, dg = 2 * x + y, 2 * (1 - x) + (1 - y)
        sibling = (x, y, 1 - c)
        n = self.n
        mine, theirs = c, 1 - c

        def nb(a):
            nx, ny = x ^ (1 - a), y ^ a
            return 2 * nx + ny, (nx, ny, c)

        def remote(s, d, send, recv, k, to):
            return pltpu.make_async_remote_copy(src_ref=s, dst_ref=d, send_sem=send.at[k], recv_sem=recv.at[k],
                                                device_id=to, device_id_type=MESH)

        class Copies:
            def slot(_, t, chip, half):
                rh = self.half_rows[t]
                return out[t].at[chip, pl.ds(half * rh, rh), :]

            def quarter(_, t, chip, q):
                qh = self.half_rows[t] // 2
                return out[t].at[chip, pl.ds(mine * 2 * qh + q * qh, qh), :]

            def own_shard(k, t):
                return remote(src[t], out[t].at[me], d2d_send, d2d_recv, 4 * t + 3, sibling)

            def hop1(k, t, a, transit=False):
                rh = self.half_rows[t]
                chip, to = nb(a)
                if transit:
                    piece = src[t].at[dg, pl.ds(a * (rh // 2), rh // 2), :]
                    return remote(piece, out[n + t].at[a], hop1_send, hop1_recv, 4 * t + 2 + a, to)
                piece = src[t].at[chip] if self.scatter else src[t].at[pl.ds(mine * rh, rh), :]
                return remote(piece, k.slot(t, me, mine), hop1_send, hop1_recv, 4 * t + a, to)

            def landed1(k, t, a, transit=False):
                here = out[n + t].at[a] if transit else k.slot(t, nb(a)[0], mine)
                return remote(here, here, hop1_send, hop1_recv, 4 * t + (2 if transit else 0) + a, sibling)

            def hop2(k, t, q):
                origin, to = nb(q)[0], nb(1 - q)[1]
                piece = out[n + t].at[q] if self.scatter else k.quarter(t, origin, q)
                return remote(piece, k.quarter(t, origin, q), hop2_send, hop2_recv, 2 * t + q, to)

            def landed2(k, t, q):
                here = k.quarter(t, dg, q)
                return remote(here, here, hop2_send, hop2_recv, 2 * t + q, sibling)

            def d2d(k, t, p, chip, own=False, arriving=False):
                if arriving:
                    there = k.slot(t, chip, theirs)
                    return remote(there, there, d2d_send, d2d_recv, 4 * t + p, sibling)
                piece = src[t].at[me] if own else k.slot(t, chip, mine)
                return remote(piece, k.slot(t, chip, mine), d2d_send, d2d_recv, 4 * t + p, sibling)

        return Copies(), nb, me, dg, c

    def start(self, src, out, sems):
        k, nb, me, dg, c = self._copies(src, out, sems)
        for t in range(self.n):
            for first in range(2):
                a = first ^ c
                k.hop1(t, a).start()
                if self.scatter:
                    k.hop1(t, a, transit=True).start()
            if self.scatter:
                k.d2d(t, 3, me, own=True).start()
            else:
                k.own_shard(t).start()

    def mid(self, src, out, sems):
        k, nb, me, dg, c = self._copies(src, out, sems)
        for t in range(self.n):
            for first in range(2):
                a = first ^ c
                if self.scatter:
                    k.landed1(t, a, transit=True).wait_recv()
                    k.hop2(t, a).start()
                k.landed1(t, a).wait_recv()
                if not self.scatter:
                    k.hop2(t, a).start()
                k.d2d(t, a, nb(a)[0]).start()

    def finish(self, src, out, sems):
        k, nb, me, dg, c = self._copies(src, out, sems)
        for t in range(self.n):
            for q in range(2):
                k.landed2(t, q).wait_recv()
            k.d2d(t, 2, dg).start()
        for t in range(self.n):
            for a in range(2):
                k.d2d(t, a, nb(a)[0], arriving=True).wait_recv()
            k.d2d(t, 2, dg, arriving=True).wait_recv()
            if self.scatter:
                k.d2d(t, 3, me, arriving=True).wait_recv()
        for t in range(self.n):
            for a in range(2):
                k.hop1(t, a).wait_send()
                if self.scatter:
                    k.hop1(t, a, transit=True).wait_send()
                k.hop2(t, a).wait_send()
                k.d2d(t, a, nb(a)[0]).wait_send()
            k.d2d(t, 2, dg).wait_send()
            if self.scatter:
                k.d2d(t, 3, me, own=True).wait_send()
            else:
                k.own_shard(t).wait()


class SiblingExchange:
    def __init__(self, grads):
        self.inputs, self.n, self.aliases, self.reach = list(grads), len(grads), {}, REACH_SIBLING
        self.half_rows = [g.shape[1] // 2 for g in grads]
        self.out_shape = [jax.ShapeDtypeStruct((g.shape[0], rh, g.shape[2]), g.dtype) for g, rh in zip(grads, self.half_rows)]
        self.sems = [pltpu.SemaphoreType.DMA((self.n,)), pltpu.SemaphoreType.DMA((self.n,))]

    def _plan(self, src, out, sems):
        x, y, c = _mesh_pos()
        return [pltpu.make_async_remote_copy(
            src_ref=src[t].at[:, pl.ds((1 - c) * self.half_rows[t], self.half_rows[t]), :], dst_ref=out[t],
            send_sem=sems[0].at[t], recv_sem=sems[1].at[t], device_id=(x, y, 1 - c), device_id_type=MESH) for t in range(self.n)]

    def start(self, src, out, sems):
        for cp in self._plan(src, out, sems):
            cp.start()

    def mid(self, src, out, sems):
        pass

    def finish(self, src, out, sems):
        for cp in self._plan(src, out, sems):
            cp.wait()


REACH_SIBLING, REACH_CHIPS, REACH_ALL = 0, 1, 2


def _entry_barrier(reach):
    x, y, c = _mesh_pos()
    peers = [(x, y, 1 - c)]
    if reach == REACH_CHIPS:
        peers += [(1 - x, y, c), (x, 1 - y, c)]
    elif reach == REACH_ALL:
        peers = [(x ^ dx, y ^ dy, c ^ dc) for dx in (0, 1) for dy in (0, 1) for dc in (0, 1)][1:]
    barrier = pltpu.get_barrier_semaphore()
    for peer in peers:
        pl.semaphore_signal(barrier, inc=1, device_id=peer, device_id_type=MESH)
    pl.semaphore_wait(barrier, len(peers))


def _call(body, hosted=(), *, name, in_specs, out_specs, out_shape, args, grid=(), scratch_shapes=(), aliased=None):
    n_in, n_out, n_scr = len(in_specs), len(out_specs), len(scratch_shapes)
    total = math.prod(grid)
    mid_step = max(0, (5 * total) // 8 - 1)

    def full(*refs):
        pos = [0]

        def take(k):
            pos[0] += k
            return refs[pos[0] - k:pos[0]]

        ins, h_in = take(n_in), [take(len(h.inputs)) for h in hosted]
        outs, h_out = take(n_out), [take(len(h.out_shape)) for h in hosted]
        scr, h_sem = take(n_scr), [take(len(h.sems)) for h in hosted]
        step = 0
        for axis, size in enumerate(grid):
            step = step * size + pl.program_id(axis)

        def phase(at, method):
            if not hosted:
                return

            def run():
                if method == "start":
                    _entry_barrier(reach)
                for h, s, o, m in zip(hosted, h_in, h_out, h_sem):
                    getattr(h, method)(s, o, m)

            if total == 1:
                run()
            else:
                pl.when(step == at)(run)

        phase(0, "start")
        body(*ins, *outs, *scr)
        phase(mid_step, "mid")
        phase(total - 1, "finish")

    aliases, i0, o0 = dict(aliased or {}), n_in, n_out
    for h in hosted:
        aliases.update({i0 + i: o0 + o for i, o in h.aliases.items()})
        i0, o0 = i0 + len(h.inputs), o0 + len(h.out_shape)
    reach = max((h.reach for h in hosted), default=None)
    params = dict(vmem_limit_bytes=VMEM_LIMIT_V7X)
    if hosted:
        params["collective_id"] = reach
    results = pl.pallas_call(
        full, name=name, grid=grid,
        in_specs=list(in_specs) + [ANY] * (i0 - n_in),
        out_specs=list(out_specs) + [ANY] * (o0 - n_out),
        out_shape=list(out_shape) + [s for h in hosted for s in h.out_shape],
        scratch_shapes=list(scratch_shapes) + [s for h in hosted for s in h.sems],
        input_output_aliases=aliases,
        compiler_params=pltpu.CompilerParams(**params),
    )(*args, *[s for h in hosted for s in h.inputs])
    outs, extras, pos = list(results[:n_out]), [], n_out
    for h in hosted:
        extras.append(list(results[pos:pos + h.n]))
        pos += len(h.out_shape)
    return outs, extras


def exchange(hosted, name):
    return _call(lambda: None, hosted, name=name, in_specs=[], out_specs=[], out_shape=[], args=[])[1]


class AllExchange:
    def __init__(self, pack):
        self.inputs, self.n, self.aliases, self.reach = [pack], 1, {}, REACH_ALL
        self.out_shape = [jax.ShapeDtypeStruct((2 * N_CHIPS,) + pack.shape, pack.dtype)]
        self.sems = [pltpu.SemaphoreType.DMA, pltpu.SemaphoreType.DMA((7,)), pltpu.SemaphoreType.DMA((7,))]

    def _copies(self, src, out, sems):
        local_sem, send_sem, recv_sem = sems
        x, y, c = _mesh_pos()
        flips = [(dx, dy, dc) for dx in (0, 1) for dy in (0, 1) for dc in (0, 1)][1:]
        peers = [(x ^ dx, y ^ dy, c ^ dc) for dx, dy, dc in flips]
        remote = lambda s, d, k: pltpu.make_async_remote_copy(
            src_ref=s, dst_ref=d, send_sem=send_sem.at[k], recv_sem=recv_sem.at[k], device_id=peers[k], device_id_type=MESH)
        sends = [remote(src[0], out[0].at[4 * x + 2 * y + c], k) for k in range(7)]
        landed = [remote(out[0].at[4 * px + 2 * py + pc], out[0].at[4 * px + 2 * py + pc], k) for k, (px, py, pc) in enumerate(peers)]
        return sends, landed, pltpu.make_async_copy(src[0], out[0].at[4 * x + 2 * y + c], local_sem)

    def start(self, src, out, sems):
        sends, _, local = self._copies(src, out, sems)
        for cp in sends:
            cp.start()
        local.start()

    def mid(self, src, out, sems):
        pass

    def finish(self, src, out, sems):
        sends, landed, local = self._copies(src, out, sems)
        for cp in landed:
            cp.wait_recv()
        for cp in sends:
            cp.wait_send()
        local.wait()


MXU_COLS = 256


def _resident(shape):
    return pl.BlockSpec(shape, lambda *_: (0,) * len(shape), pipeline_mode=pl.Buffered(1))


def ffn_up(h, gain, w1, w3, name, hosted=()):
    T, D = h.shape
    F = w1.shape[0]
    tm = min(T, 512)

    def body(h_ref, g_ref, w1_ref, w3_ref, n_ref, ga_ref, gb_ref, s_ref):
        hh = h_ref[...]
        n = (hh * _rstd(hh) * g_ref[...]).astype(BF16)
        n_ref[...] = n
        for c in range(0, F, MXU_COLS):
            cols = slice(c, c + MXU_COLS)
            a = _dot_nt(n, w1_ref[cols, :])
            b = _dot_nt(n, w3_ref[cols, :])
            silu, dsilu = _silu_parts(a)
            ga_ref[:, cols] = (b * dsilu).astype(BF16)
            gb_ref[:, cols] = silu.astype(BF16)
            s_ref[:, cols] = (silu * b).astype(BF16)

    act = jax.ShapeDtypeStruct((T, F), BF16)
    act_spec = pl.BlockSpec((tm, F), lambda i: (i, 0))
    row_spec = pl.BlockSpec((tm, D), lambda i: (i, 0))
    return _call(
        body, hosted, name=name, grid=(T // tm,),
        in_specs=[row_spec, pl.BlockSpec((1, D), lambda i: (0, 0)), _resident((F, D)), _resident((F, D))],
        out_specs=[row_spec, act_spec, act_spec, act_spec],
        out_shape=[jax.ShapeDtypeStruct((T, D), BF16), act, act, act],
        args=[h, gain, w1, w3])


def ffn_down(s, w2, h, name, hosted=()):
    T, F = s.shape
    D = h.shape[1]
    tm = min(T, 512)

    def body(s_ref, w2_ref, h_ref, o_ref):
        o_ref[...] = h_ref[...] + 0.5 * _dot(s_ref[...], w2_ref[...])

    row_spec = pl.BlockSpec((tm, D), lambda i: (i, 0))
    return _call(
        body, hosted, name=name, grid=(T // tm,),
        in_specs=[pl.BlockSpec((tm, F), lambda i: (i, 0)), pl.BlockSpec((F, D), lambda i: (0, 0)), row_spec],
        out_specs=[row_spec],
        out_shape=[jax.ShapeDtypeStruct((T, D), F32)],
        args=[s, w2, h])


def ffn_bwd_act(dh, w2, ga, gb, name, hosted=()):
    T, D = dh.shape
    F = w2.shape[0]
    tm = min(T, 512)

    def body(dh_ref, w2_ref, ga_ref, gb_ref, da_ref, db_ref, df_ref):
        df = (0.5 * dh_ref[...]).astype(BF16)
        df_ref[...] = df
        for c in range(0, F, MXU_COLS):
            cols = slice(c, c + MXU_COLS)
            ds = _dot_nt(df, w2_ref[cols, :])
            da_ref[:, cols] = (ds * ga_ref[:, cols].astype(F32)).astype(BF16)
            db_ref[:, cols] = (ds * gb_ref[:, cols].astype(F32)).astype(BF16)

    act = jax.ShapeDtypeStruct((T, F), BF16)
    act_spec = pl.BlockSpec((tm, F), lambda i: (i, 0))
    row_spec = pl.BlockSpec((tm, D), lambda i: (i, 0))
    return _call(
        body, hosted, name=name, grid=(T // tm,),
        in_specs=[row_spec, _resident((F, D)), act_spec, act_spec],
        out_specs=[act_spec, act_spec, row_spec],
        out_shape=[act, act, jax.ShapeDtypeStruct((T, D), BF16)],
        args=[dh, w2, ga, gb])


def ffn_dw(xs, y, halves, name, hosted=()):
    T, F = xs[0].shape
    D = y.shape[1]
    nx, fh = len(xs), F // halves
    tk = min(T, 512)
    nk = T // tk

    def body(*refs):
        y_ref, x_refs, o_refs, accs = refs[0], refs[1:1 + nx], refs[1 + nx:1 + 2 * nx], refs[1 + 2 * nx:]
        k = pl.program_id(1)

        @pl.when(k == 0)
        def _():
            for acc in accs:
                acc[...] = jnp.zeros_like(acc)

        yy = y_ref[...]
        for x_ref, acc in zip(x_refs, accs):
            acc[...] += _dot_tn(x_ref[...], yy)

        @pl.when(k == nk - 1)
        def _():
            for o_ref, acc in zip(o_refs, accs):
                o_ref[...] = acc[...].astype(BF16)

    out = jax.ShapeDtypeStruct((F, D), BF16)
    return _call(
        body, hosted, name=name, grid=(halves, nk),
        in_specs=[pl.BlockSpec((tk, D), lambda j, k: (k, 0))] + [pl.BlockSpec((tk, fh), lambda j, k: (k, j))] * nx,
        out_specs=[pl.BlockSpec((fh, D), lambda j, k: (j, 0))] * nx,
        out_shape=[out] * nx,
        scratch_shapes=[pltpu.VMEM((fh, D), F32)] * nx,
        args=[y] + list(xs))


def ffn_bwd_in(da, db, w1, w3, h, gain, dh, name, hosted=()):
    T, F = da.shape
    D = h.shape[1]
    tm = min(T, 512)

    def body(da_ref, db_ref, w1_ref, w3_ref, h_ref, g_ref, dh_ref, o_ref, dg_ref):
        dn = _dot(da_ref[...], w1_ref[...]) + _dot(db_ref[...], w3_ref[...])
        dhn, dg = _rmsnorm_bwd(dn, h_ref[...], g_ref[...])
        o_ref[...] = dh_ref[...] + dhn

        @pl.when(pl.program_id(0) == 0)
        def _():
            dg_ref[...] = jnp.zeros_like(dg_ref)

        dg_ref[...] += jnp.sum(dg, axis=0, keepdims=True)

    act_spec = pl.BlockSpec((tm, F), lambda i: (i, 0))
    row_spec = pl.BlockSpec((tm, D), lambda i: (i, 0))
    vec_spec = pl.BlockSpec((1, D), lambda i: (0, 0))
    return _call(
        body, hosted, name=name, grid=(T // tm,),
        in_specs=[act_spec, act_spec, _resident((F, D)), _resident((F, D)), row_spec, vec_spec, row_spec],
        out_specs=[row_spec, vec_spec],
        out_shape=[jax.ShapeDtypeStruct((T, D), F32), jax.ShapeDtypeStruct((1, D), F32)],
        args=[da, db, w1, w3, h, gain, dh])


def mix_in(h, gain, wing, name, hosted=()):
    T, D = h.shape
    nsh, _, Cs = wing.shape
    tm = min(T, 512)

    def body(h_ref, g_ref, w_ref, u_ref, p_ref):
        hh = h_ref[...]
        u = (hh * _rstd(hh) * g_ref[...]).astype(BF16)
        u_ref[...] = u
        for j in range(nsh):
            p_ref[:, j * Cs:(j + 1) * Cs] = _dot(u, w_ref[j])

    return _call(
        body, hosted, name=name, grid=(T // tm,),
        in_specs=[pl.BlockSpec((tm, D), lambda i: (i, 0)), pl.BlockSpec((1, D), lambda i: (0, 0)),
                  pl.BlockSpec((nsh, D, Cs), lambda i: (0, 0, 0))],
        out_specs=[pl.BlockSpec((tm, D), lambda i: (i, 0)), pl.BlockSpec((tm, nsh * Cs), lambda i: (i, 0))],
        out_shape=[jax.ShapeDtypeStruct((T, D), BF16), jax.ShapeDtypeStruct((T, nsh * Cs), F32)],
        args=[h, gain, wing])


def mix_out(a, b, woutg, h, name, hosted=()):
    T, W = a.shape
    D = h.shape[1]
    wout = woutg.reshape(2, W, D)
    tm = min(T, 512)

    def body(a_ref, b_ref, w_ref, h_ref, o_ref):
        o_ref[...] = h_ref[...] + _dot(a_ref[...], w_ref[0]) + _dot(b_ref[...], w_ref[1])

    return _call(
        body, hosted, name=name, grid=(T // tm,),
        in_specs=[pl.BlockSpec((tm, W), lambda i: (i, 0)), pl.BlockSpec((tm, W), lambda i: (i, 0)),
                  pl.BlockSpec((2, W, D), lambda i: (0, 0, 0)), pl.BlockSpec((tm, D), lambda i: (i, 0))],
        out_specs=[pl.BlockSpec((tm, D), lambda i: (i, 0))],
        out_shape=[jax.ShapeDtypeStruct((T, D), F32)],
        args=[a, b, wout, h])


def mix_out_bwd(dh, woutg, a, b, name, hosted=()):
    T, D = dh.shape
    W = a.shape[1]
    nsh, Rs, _ = woutg.shape
    wout = woutg.reshape(2, W, D)
    tk = min(T, 512)
    nk = T // tk

    def body(dh_ref, w_ref, a_ref, b_ref, da_ref, db_ref, dw_ref, acc):
        k = pl.program_id(0)

        @pl.when(k == 0)
        def _():
            acc[...] = jnp.zeros_like(acc)

        dhb = dh_ref[...].astype(BF16)
        da_ref[...] = _dot_nt(dhb, w_ref[0])
        db_ref[...] = _dot_nt(dhb, w_ref[1])
        acc[0:W, :] += _dot_tn(a_ref[...], dhb)
        acc[W:2 * W, :] += _dot_tn(b_ref[...], dhb)

        @pl.when(k == nk - 1)
        def _():
            for j in range(nsh):
                dw_ref[j] = acc[j * Rs:(j + 1) * Rs, :].astype(BF16)

    return _call(
        body, hosted, name=name, grid=(nk,),
        in_specs=[pl.BlockSpec((tk, D), lambda k: (k, 0)), pl.BlockSpec((2, W, D), lambda k: (0, 0, 0)),
                  pl.BlockSpec((tk, W), lambda k: (k, 0)), pl.BlockSpec((tk, W), lambda k: (k, 0))],
        out_specs=[pl.BlockSpec((tk, W), lambda k: (k, 0)), pl.BlockSpec((tk, W), lambda k: (k, 0)),
                   pl.BlockSpec((nsh, Rs, D), lambda k: (0, 0, 0))],
        out_shape=[jax.ShapeDtypeStruct((T, W), F32), jax.ShapeDtypeStruct((T, W), F32),
                   jax.ShapeDtypeStruct((nsh, Rs, D), BF16)],
        scratch_shapes=[pltpu.VMEM((2 * W, D), F32)],
        args=[dh, wout, a, b])


def _dproj_block(g):
    return (g // N_GROUPS + N_GROUPS) % (N_GROUPS + 1), g % N_GROUPS


def mix_dwin(u, dproj, nsh, name, hosted=()):
    T, D = u.shape
    Hd = HEAD_DIM
    per = dproj.shape[0] * dproj.shape[2] // (nsh * Hd)
    Cs = per * Hd
    tk = min(T, 512)
    nk = T // tk

    def body(u_ref, *refs):
        d_refs, o_ref, acc = refs[:per], refs[per], refs[per + 1]
        k = pl.program_id(1)

        @pl.when(k == 0)
        def _():
            acc[...] = jnp.zeros_like(acc)

        d = jnp.concatenate([r[0] for r in d_refs], axis=1)
        acc[...] += _dot_tn(u_ref[...], d)

        @pl.when(k == nk - 1)
        def _():
            o_ref[0] = acc[...].astype(BF16)

    def d_spec(i):
        def index(j, k):
            slab, col = _dproj_block(per * j + i)
            return slab, k, col
        return pl.BlockSpec((1, tk, Hd), index)

    return _call(
        body, hosted, name=name, grid=(nsh, nk),
        in_specs=[pl.BlockSpec((tk, D), lambda j, k: (k, 0))] + [d_spec(i) for i in range(per)],
        out_specs=[pl.BlockSpec((1, D, Cs), lambda j, k: (j, 0, 0))],
        out_shape=[jax.ShapeDtypeStruct((nsh, D, Cs), BF16)],
        scratch_shapes=[pltpu.VMEM((D, Cs), F32)],
        args=[u] + [dproj] * per)


def mix_in_bwd(dproj, wing, h, gain, dh, name, hosted=()):
    T, D = h.shape
    nsh, _, Cs = wing.shape
    Hd = HEAD_DIM
    per = Cs // Hd
    tm = min(T, 512)

    def body(d_ref, w_ref, h_ref, g_ref, dh_ref, o_ref, dg_ref):
        def shard(j):
            blocks = [_dproj_block(per * j + i) for i in range(per)]
            return jnp.concatenate([d_ref[slab, :, col * Hd:(col + 1) * Hd] for slab, col in blocks], axis=1)

        du = _dot_nt(shard(0), w_ref[0])
        for j in range(1, nsh):
            du += _dot_nt(shard(j), w_ref[j])
        dhn, dg = _rmsnorm_bwd(du, h_ref[...], g_ref[...])
        o_ref[...] = dh_ref[...] + dhn

        @pl.when(pl.program_id(0) == 0)
        def _():
            dg_ref[...] = jnp.zeros_like(dg_ref)

        dg_ref[...] += jnp.sum(dg, axis=0, keepdims=True)

    row_spec = pl.BlockSpec((tm, D), lambda i: (i, 0))
    vec_spec = pl.BlockSpec((1, D), lambda i: (0, 0))
    return _call(
        body, hosted, name=name, grid=(T // tm,),
        in_specs=[pl.BlockSpec((dproj.shape[0], tm, dproj.shape[2]), lambda i: (0, i, 0)),
                  pl.BlockSpec((nsh, D, Cs), lambda i: (0, 0, 0)), row_spec, vec_spec, row_spec],
        out_specs=[row_spec, vec_spec],
        out_shape=[jax.ShapeDtypeStruct((T, D), F32), jax.ShapeDtypeStruct((1, D), F32)],
        args=[dproj, wing, h, gain, dh])


def _pool_window(x, group, T, trailing):
    rows = lax.broadcasted_iota(jnp.int32, x.shape, 0)

    def shifted(z, k):
        if trailing:
            return jnp.where(rows >= k, pltpu.roll(z, k, 0), 0.0)
        return jnp.where(rows < T - k, pltpu.roll(z, T - k, 0), 0.0)

    s2 = x + shifted(x, 1)
    s4 = s2 + shifted(s2, 2)
    s8 = s4 + shifted(s4, 4)
    s16 = s8 + shifted(s8, 8)
    return jnp.where(group == 0, s2, jnp.where(group == 1, s4, jnp.where(group == 2, s8, s16)))


def _pool_count(group, shape):
    rows = lax.broadcasted_iota(jnp.int32, shape, 0)
    w = jnp.where(group == 0, 2, jnp.where(group == 1, 4, jnp.where(group == 2, 8, 16)))
    return jnp.minimum(rows + 1, w).astype(F32)


def pool_fwd(proj, pool_w, pool_scale, name, hosted=()):
    T = proj.shape[0]
    Hd = HEAD_DIM

    def body(x_ref, w_ref, sc_ref, a_ref):
        g = pl.program_id(0)
        x = x_ref[...]
        pooled = _pool_window(x, g, T, True) / _pool_count(g, x.shape) - x
        a_ref[...] = (_dot(pooled.astype(BF16), w_ref[0].astype(BF16)) * sc_ref[...]).astype(BF16)

    return _call(
        body, hosted, name=name, grid=(N_GROUPS,),
        in_specs=[pl.BlockSpec((T, Hd), lambda g: (0, g)), pl.BlockSpec((1, Hd, Hd), lambda g: (g, 0, 0)),
                  pl.BlockSpec((1, Hd), lambda g: (0, g))],
        out_specs=[pl.BlockSpec((T, Hd), lambda g: (0, g))],
        out_shape=[jax.ShapeDtypeStruct((T, N_GROUPS * Hd), BF16)],
        args=[proj, pool_w, pool_scale])


def pool_bwd(proj, da, pool_w, pool_scale, name, hosted=()):
    T = proj.shape[0]
    Hd = HEAD_DIM

    def body(x_ref, da_ref, w_ref, sc_ref, dx_ref, dw_ref, dsc_ref):
        g = pl.program_id(0)
        x = x_ref[...]
        cnt = _pool_count(g, x.shape)
        pooled = (_pool_window(x, g, T, True) / cnt - x).astype(BF16)
        wb = w_ref[0].astype(BF16)
        dav = da_ref[...]
        dsc_ref[...] = jnp.sum(dav * _dot(pooled, wb), axis=0, keepdims=True)
        dout = (dav * sc_ref[...]).astype(BF16)
        dw_ref[0] = _dot_tn(pooled, dout)
        dpooled = _dot_nt(dout, wb)
        dx_ref[0] = (_pool_window(dpooled / cnt, g, T, False) - dpooled).astype(BF16)

    col_spec = pl.BlockSpec((T, Hd), lambda g: (0, g))
    return _call(
        body, hosted, name=name, grid=(N_GROUPS,),
        in_specs=[col_spec, col_spec, pl.BlockSpec((1, Hd, Hd), lambda g: (g, 0, 0)), pl.BlockSpec((1, Hd), lambda g: (0, g))],
        out_specs=[pl.BlockSpec((1, T, Hd), lambda g: (N_GROUPS, 0, g)), pl.BlockSpec((1, Hd, Hd), lambda g: (g, 0, 0)),
                   pl.BlockSpec((1, Hd), lambda g: (0, g))],
        out_shape=[jax.ShapeDtypeStruct((N_GROUPS + 1, T, N_GROUPS * Hd), BF16), jax.ShapeDtypeStruct((N_GROUPS, Hd, Hd), F32),
                   jax.ShapeDtypeStruct((1, N_GROUPS * Hd), F32)],
        args=[proj, da, pool_w, pool_scale])


def _ret_tables(T):
    Hd, C = HEAD_DIM, RET_CHUNK
    inv_freq = 1.0 / (ROPE_BASE ** (jnp.arange(0, Hd, 2, dtype=F32) / Hd))
    ang = jnp.arange(T, dtype=F32)[:, None] * inv_freq[None, :]
    cos, sin = jnp.cos(ang), jnp.sin(ang)
    cos2 = jnp.concatenate([cos, cos], axis=-1)
    sin2 = jnp.concatenate([-sin, sin], axis=-1)
    log_gamma = jnp.log1p(-jnp.exp2(-5.0 - jnp.arange(N_GROUPS, dtype=F32)))
    pos = jnp.arange(C, dtype=F32)
    rel = pos[:, None] - pos[None, :]
    intra = jnp.where(rel[None] >= 0, jnp.exp(log_gamma[:, None, None] * jnp.maximum(rel, 0.0)[None]), 0.0)
    k_tail = jnp.exp(log_gamma[:, None] * (C - 1 - pos)[None, :])
    q_head = jnp.exp(log_gamma[:, None] * (pos + 1.0)[None, :])
    chunk_decay = jnp.exp(log_gamma * C)
    wide = lambda t: jnp.broadcast_to(t[:, :, None], (N_GROUPS, C, Hd))
    return cos2, sin2, intra, wide(k_tail), wide(q_head), jnp.broadcast_to(chunk_decay[:, None, None], (N_GROUPS, 1, Hd))


def _rope(x, cos2, sin2):
    return x * cos2 + pltpu.roll(x, HEAD_DIM // 2, 1) * sin2


def _rope_t(d, cos2, sin2):
    return d * cos2 + pltpu.roll(d * sin2, HEAD_DIM // 2, 1)


def _ret_specs(tseg, seg_of):
    Hd, G = HEAD_DIM, N_GROUPS
    col = lambda kind: pl.BlockSpec((tseg, Hd), lambda h, s: (seg_of(s), G * kind + h))
    tab = pl.BlockSpec((tseg, Hd), lambda h, s: (seg_of(s), 0))
    head = pl.BlockSpec((1, RET_CHUNK, Hd), lambda h, s: (h, 0, 0))
    cd = pl.BlockSpec((1, 1, Hd), lambda h, s: (h, 0, 0))
    gain = pl.BlockSpec((1, Hd), lambda h, s: (0, h))
    return col, tab, head, cd, gain


def ret_fwd(proj, ret_norm, tables, name, hosted=()):
    T = proj.shape[0]
    Hd, C, G = HEAD_DIM, RET_CHUNK, N_GROUPS
    tseg = min(T, 1024)
    nseg, nck = T // tseg, tseg // C
    scale = Hd ** -0.5
    cos2, sin2, intra, k_tail, q_head, chunk_decay = tables

    def body(q_ref, k_ref, v_ref, g_ref, gain_ref, cos_ref, sin_ref, m_ref, kt_ref, qh_ref, cd_ref,
             b_ref, o_ref, rp_ref, state):
        @pl.when(pl.program_id(1) == 0)
        def _():
            state[...] = jnp.zeros_like(state)

        def chunk(ci, carry):
            rows = pl.ds(pl.multiple_of(ci * C, C), C)
            cos, sin = cos_ref[rows, :], sin_ref[rows, :]
            qr = _rope(q_ref[rows, :], cos, sin)
            kr = _rope(k_ref[rows, :], cos, sin) * scale
            qb, kb, vb = qr.astype(BF16), kr.astype(BF16), v_ref[rows, :].astype(BF16)
            r = state[...]
            rp_ref[0, ci] = r.astype(BF16)
            sc = _dot_nt(qb, kb) * m_ref[0]
            o = _dot(sc.astype(BF16), vb) + _dot((qr * qh_ref[0]).astype(BF16), r.astype(BF16))
            state[...] = cd_ref[0] * r + _dot_tn((kr * kt_ref[0]).astype(BF16), vb)
            o_ref[rows, :] = o
            on = o * _rstd(o)
            b_ref[rows, :] = (jax.nn.silu(g_ref[rows, :]) * (on * gain_ref[...])).astype(BF16)
            return carry

        lax.fori_loop(0, nck, chunk, 0, unroll=True)

    col, tab, head, cd, gain = _ret_specs(tseg, lambda s: s)
    out_col = pl.BlockSpec((tseg, Hd), lambda h, s: (s, h))
    return _call(
        body, hosted, name=name, grid=(G, nseg),
        in_specs=[col(1), col(2), col(3), col(4), gain, tab, tab, head, head, head, cd],
        out_specs=[out_col, out_col, pl.BlockSpec((1, nck, Hd, Hd), lambda h, s: (h, s, 0, 0))],
        out_shape=[jax.ShapeDtypeStruct((T, G * Hd), BF16), jax.ShapeDtypeStruct((T, G * Hd), F32),
                   jax.ShapeDtypeStruct((G, T // C, Hd, Hd), BF16)],
        scratch_shapes=[pltpu.VMEM((Hd, Hd), F32)],
        args=[proj, proj, proj, proj, ret_norm, cos2, sin2, intra, k_tail, q_head, chunk_decay])


def ret_bwd(proj, db, o_pre, r_prev, ret_norm, tables, dproj, name, hosted=()):
    T = proj.shape[0]
    Hd, C, G = HEAD_DIM, RET_CHUNK, N_GROUPS
    tseg = min(T, 1024)
    nseg, nck = T // tseg, tseg // C
    scale = Hd ** -0.5
    cos2, sin2, intra, k_tail, q_head, chunk_decay = tables

    def body(q_ref, k_ref, v_ref, g_ref, db_ref, o_ref, rp_ref, gain_ref, cos_ref, sin_ref, m_ref, kt_ref, qh_ref, cd_ref,
             _, d_ref, dgain_ref, gstate):
        @pl.when(pl.program_id(1) == 0)
        def _():
            gstate[...] = jnp.zeros_like(gstate)
            dgain_ref[...] = jnp.zeros_like(dgain_ref)

        def chunk(t, carry):
            ci = nck - 1 - t
            rows = pl.ds(pl.multiple_of(ci * C, C), C)
            cos, sin = cos_ref[rows, :], sin_ref[rows, :]
            qr = _rope(q_ref[rows, :], cos, sin)
            kr = _rope(k_ref[rows, :], cos, sin) * scale
            qb, kb, vb = qr.astype(BF16), kr.astype(BF16), v_ref[rows, :].astype(BF16)
            qhb, ktb = (qr * qh_ref[0]).astype(BF16), (kr * kt_ref[0]).astype(BF16)
            sc = (_dot_nt(qb, kb) * m_ref[0]).astype(BF16)
            o = o_ref[rows, :]
            rstd = _rstd(o)
            on = o * rstd
            gain = gain_ref[...]
            silu, dsilu = _silu_parts(g_ref[rows, :])
            dy = db_ref[rows, :]
            dgain_ref[...] += jnp.sum(dy * silu * on, axis=0, keepdims=True)
            dg = dy * on * gain * dsilu
            don = dy * silu * gain
            dob = (rstd * (don - on * jnp.mean(don * on, axis=-1, keepdims=True))).astype(BF16)
            gn = gstate[...]
            gb = gn.astype(BF16)
            da = (_dot_nt(dob, vb) * m_ref[0]).astype(BF16)
            dq = _dot(da, kb) + _dot_nt(dob, rp_ref[0, ci]) * qh_ref[0]
            dk = _dot_tn(da, qb) + _dot_nt(vb, gb) * kt_ref[0]
            dv = _dot_tn(sc, dob) + _dot(ktb, gb)
            gstate[...] = cd_ref[0] * gn + _dot_tn(qhb, dob)
            d_ref[0, rows, :] = _rope_t(dq, cos, sin).astype(BF16)
            d_ref[1, rows, :] = _rope_t(dk * scale, cos, sin).astype(BF16)
            d_ref[2, rows, :] = dv.astype(BF16)
            d_ref[3, rows, :] = dg.astype(BF16)
            return carry

        lax.fori_loop(0, nck, chunk, 0, unroll=True)

    rev = lambda s: nseg - 1 - s
    col, tab, head, cd, gain = _ret_specs(tseg, rev)
    act = pl.BlockSpec((tseg, Hd), lambda h, s: (rev(s), h))
    return _call(
        body, hosted, name=name, grid=(G, nseg),
        in_specs=[col(1), col(2), col(3), col(4), act, act, pl.BlockSpec((1, nck, Hd, Hd), lambda h, s: (h, rev(s), 0, 0)),
                  gain, tab, tab, head, head, head, cd, ANY],
        out_specs=[pl.BlockSpec((4, tseg, Hd), lambda h, s: (0, rev(s), h)), gain],
        out_shape=[jax.ShapeDtypeStruct(dproj.shape, BF16), jax.ShapeDtypeStruct((1, G * Hd), F32)],
        scratch_shapes=[pltpu.VMEM((Hd, Hd), F32)], aliased={14: 0},
        args=[proj, proj, proj, proj, db, o_pre, r_prev, ret_norm, cos2, sin2, intra, k_tail, q_head, chunk_decay, dproj])


def final_loss(h, gain, target, name, hosted=()):
    T, D = h.shape
    tm = min(T, 512)

    def body(h_ref, g_ref, t_ref, dh_ref, loss_ref, dg_ref):
        @pl.when(pl.program_id(0) == 0)
        def _():
            loss_ref[...] = jnp.zeros_like(loss_ref)
            dg_ref[...] = jnp.zeros_like(dg_ref)

        hh = h_ref[...]
        gain_v = g_ref[...]
        err = hh * _rstd(hh) * gain_v - t_ref[...]
        loss_ref[...] += 0.5 * jnp.sum(jnp.mean(err * err, axis=-1, keepdims=True), axis=0, keepdims=True)
        dhn, dg = _rmsnorm_bwd(err * (1.0 / D), hh, gain_v)
        dh_ref[...] = dhn
        dg_ref[...] += jnp.sum(dg, axis=0, keepdims=True)

    row_spec = pl.BlockSpec((tm, D), lambda i: (i, 0))
    vec_spec = pl.BlockSpec((1, D), lambda i: (0, 0))
    return _call(
        body, hosted, name=name, grid=(T // tm,),
        in_specs=[row_spec, vec_spec, row_spec],
        out_specs=[row_spec, pl.BlockSpec((1, 128), lambda i: (0, 0)), vec_spec],
        out_shape=[jax.ShapeDtypeStruct((T, D), F32), jax.ShapeDtypeStruct((1, 128), F32), jax.ShapeDtypeStruct((1, D), F32)],
        args=[h, gain, target])


def prereduce(grad, recv, place, name):
    nsh, R, C = grad.shape
    rh = R // 2

    def body(place_ref, g_ref, r_ref, o_ref, own_ref):
        piece = (g_ref[...].astype(F32) + r_ref[...].astype(F32)).astype(BF16)
        o_ref[...] = piece

        @pl.when(pl.program_id(0) == place_ref[1])
        def _():
            own_ref[...] = piece

    return pl.pallas_call(
        body, name=name,
        grid_spec=pltpu.PrefetchScalarGridSpec(
            num_scalar_prefetch=1, grid=(nsh,),
            in_specs=[pl.BlockSpec((1, rh, C), lambda j, p: (j, p[0], 0)), pl.BlockSpec((1, rh, C), lambda j, p: (j, 0, 0))],
            out_specs=[pl.BlockSpec((1, rh, C), lambda j, p: (j, 0, 0)), pl.BlockSpec((1, rh, C), lambda j, p: (p[1], p[0], 0))]),
        out_shape=[jax.ShapeDtypeStruct((nsh, rh, C), BF16), jax.ShapeDtypeStruct((nsh, R, C), BF16)],
        compiler_params=pltpu.CompilerParams(vmem_limit_bytes=VMEM_LIMIT_V7X),
    )(place, grad, recv)


def _adamw(w, g, m, v):
    m = ADAM_B1 * m + (1.0 - ADAM_B1) * g
    v = ADAM_B2 * v + (1.0 - ADAM_B2) * (g * g)
    m_hat = m / (1.0 - ADAM_B1 ** ADAM_STEP)
    v_hat = v / (1.0 - ADAM_B2 ** ADAM_STEP)
    return -ADAM_LR * (m_hat / (jnp.sqrt(v_hat) + ADAM_EPS) + ADAM_WD * w), m, v


def adamw_sharded(tensors, name, hosted=()):
    nt = len(tensors)
    nsh, R, C = tensors[0][0].shape
    lanes = -(-C // 128) * 128
    per_row = 2 * nt * lanes * (nsh * 2 + 7 * 4)
    tr = max(r for r in range(16, R + 1, 16) if R % r == 0 and r * per_row <= ADAMW_VMEM_BUDGET)

    def body(*refs):
        ins, outs = refs[:4 * nt], refs[4 * nt:]
        for t in range(nt):
            p_ref, w_ref, m_ref, v_ref = ins[4 * t:4 * t + 4]
            g_ref, d_ref, nm_ref, nv_ref = outs[4 * t:4 * t + 4]
            g = p_ref[0].astype(F32)
            for i in range(1, nsh):
                g += p_ref[i].astype(F32)
            g_ref[...] = g
            d_ref[...], nm_ref[...], nv_ref[...] = _adamw(w_ref[...], g, m_ref[...], v_ref[...])

    spec = pl.BlockSpec((tr, C), lambda i: (i, 0))
    out = jax.ShapeDtypeStruct((R, C), F32)
    return _call(
        body, hosted, name=name, grid=(R // tr,),
        in_specs=[pl.BlockSpec((nsh, tr, C), lambda i: (0, i, 0)), spec, spec, spec] * nt,
        out_specs=[spec] * (4 * nt), out_shape=[out] * (4 * nt),
        args=[a for tensor in tensors for a in tensor])


def adamw_small(packs, pool, vectors, name):
    ndev = packs.shape[0]
    rp, rv = pool[0].shape[0], vectors[0].shape[0]

    def body(p_ref, wp, mp, vp, wv, mv, vv, gp, dp, nmp, nvp, gv, dv, nmv, nvv, loss_ref):
        g = p_ref[0]
        for i in range(1, ndev):
            g += p_ref[i]
        gp[...], gv[...], loss_ref[...] = g[0:rp], g[rp:rp + rv], g[rp + rv:rp + rv + 8]
        dp[...], nmp[...], nvp[...] = _adamw(wp[...], g[0:rp], mp[...], vp[...])
        dv[...], nmv[...], nvv[...] = _adamw(wv[...], g[rp:rp + rv], mv[...], vv[...])

    shape = lambda rows: jax.ShapeDtypeStruct((rows, 128), F32)
    outs = pl.pallas_call(body, name=name, out_shape=[shape(rp)] * 4 + [shape(rv)] * 4 + [shape(8)],
                          compiler_params=pltpu.CompilerParams(vmem_limit_bytes=VMEM_LIMIT_V7X))(packs, *pool, *vectors)
    return outs[0:4], outs[4:8], outs[8]


BIG = ("ffn1_w1", "ffn1_w3", "ffn1_w2", "w_in", "w_out", "ffn2_w1", "ffn2_w3", "ffn2_w2")
TRANSPOSED = ("ffn1_w1", "ffn1_w3", "ffn2_w1", "ffn2_w3")
VECTORS = ("ffn1_norm", "mix_norm", "pool_scale", "ret_norm", "ffn2_norm", "final_norm")
WEIGHTS = ("ffn1_norm", "ffn1_w1", "ffn1_w3", "ffn1_w2", "mix_norm", "w_in", "pool_w", "pool_scale", "ret_norm", "w_out",
           "ffn2_norm", "ffn2_w1", "ffn2_w3", "ffn2_w2", "final_norm")


def _pack_vectors(parts):
    return jnp.concatenate([parts[k].reshape(-1, 128) for k in VECTORS], axis=0)


def _unpack_vectors(pack, like):
    out, row = {}, 0
    for k in VECTORS:
        rows = like[k].size // 128
        out[k] = pack[row:row + rows].reshape(like[k].shape)
        row += rows
    return out


def kernel(x, ffn1_norm, ffn1_w1, ffn1_w3, ffn1_w2, mix_norm, w_in, pool_w, pool_scale, ret_norm, w_out, ffn2_norm, ffn2_w1, ffn2_w3, ffn2_w2, final_norm, loss_target, m_ffn1_norm, m_ffn1_w1, m_ffn1_w3, m_ffn1_w2, m_mix_norm, m_w_in, m_pool_w, m_pool_scale, m_ret_norm, m_w_out, m_ffn2_norm, m_ffn2_w1, m_ffn2_w3, m_ffn2_w2, m_final_norm, v_ffn1_norm, v_ffn1_w1, v_ffn1_w3, v_ffn1_w2, v_mix_norm, v_w_in, v_pool_w, v_pool_scale, v_ret_norm, v_w_out, v_ffn2_norm, v_ffn2_w1, v_ffn2_w3, v_ffn2_w2, v_final_norm):
    w = dict(ffn1_norm=ffn1_norm, ffn1_w1=ffn1_w1, ffn1_w3=ffn1_w3, ffn1_w2=ffn1_w2, mix_norm=mix_norm, w_in=w_in, pool_w=pool_w,
             pool_scale=pool_scale, ret_norm=ret_norm, w_out=w_out, ffn2_norm=ffn2_norm, ffn2_w1=ffn2_w1, ffn2_w3=ffn2_w3,
             ffn2_w2=ffn2_w2, final_norm=final_norm)
    m = dict(ffn1_norm=m_ffn1_norm, ffn1_w1=m_ffn1_w1, ffn1_w3=m_ffn1_w3, ffn1_w2=m_ffn1_w2, mix_norm=m_mix_norm, w_in=m_w_in,
             pool_w=m_pool_w, pool_scale=m_pool_scale, ret_norm=m_ret_norm, w_out=m_w_out, ffn2_norm=m_ffn2_norm, ffn2_w1=m_ffn2_w1,
             ffn2_w3=m_ffn2_w3, ffn2_w2=m_ffn2_w2, final_norm=m_final_norm)
    v = dict(ffn1_norm=v_ffn1_norm, ffn1_w1=v_ffn1_w1, ffn1_w3=v_ffn1_w3, ffn1_w2=v_ffn1_w2, mix_norm=v_mix_norm, w_in=v_w_in,
             pool_w=v_pool_w, pool_scale=v_pool_scale, ret_norm=v_ret_norm, w_out=v_w_out, ffn2_norm=v_ffn2_norm, ffn2_w1=v_ffn2_w1,
             ffn2_w3=v_ffn2_w3, ffn2_w2=v_ffn2_w2, final_norm=v_final_norm)
    xs, target = x[0], loss_target[0]
    T = xs.shape[0]
    tables = _ret_tables(T)
    place = jnp.stack([lax.axis_index("c"), 2 * lax.axis_index("x") + lax.axis_index("y")]).astype(jnp.int32)
    local = lambda d, k: jnp.transpose(d[k][0]) if k in TRANSPOSED else d[k][0]
    result = lambda o, k: jnp.transpose(o)[None] if k in TRANSPOSED else o[None]
    sh = {k: local(w, k).astype(BF16) for k in BIG}
    gather = lambda *names: [ChipExchange([sh[k] for k in names], False)]
    wg, grad, delta, new_m, new_v = {}, {}, {}, {}, {}

    def update(names, pieces, name, hosted=()):
        outs, extras = adamw_sharded([(p, local(w, k), local(m, k), local(v, k)) for k, p in zip(names, pieces)], name, hosted)
        for t, k in enumerate(names):
            grad[k], delta[k], new_m[k], new_v[k] = [result(o, k) for o in outs[4 * t:4 * t + 4]]
        return extras

    def reduce_in_chip(name, partial, recv):
        return prereduce(partial, recv, place, "prereduce_" + name)

    scatter = lambda *reduced: ChipExchange([r[0] for r in reduced], True, [r[1] for r in reduced])
    whole = lambda k: wg[k].reshape(-1, wg[k].shape[-1])
    sharded = lambda g: g.reshape(N_CHIPS, -1, g.shape[-1])

    (wg["ffn1_w1"], wg["ffn1_w3"]), = exchange(gather("ffn1_w1", "ffn1_w3"), "gather_ffn1")
    (n1, ga1, gb1, s1), ((wg["ffn1_w2"], wg["w_in"]),) = ffn_up(
        xs, ffn1_norm, whole("ffn1_w1"), whole("ffn1_w3"), "ffn1_up", gather("ffn1_w2", "w_in"))
    (h1,), ((wg["w_out"],),) = ffn_down(s1, whole("ffn1_w2"), xs, "ffn1_down", gather("w_out"))
    (u, proj), ((wg["ffn2_w1"],),) = mix_in(h1, mix_norm, wg["w_in"], "mix_in", gather("ffn2_w1"))
    (pa,), _ = pool_fwd(proj, pool_w[0], pool_scale, "pool_fwd")
    (rb, o_pre, r_prev), ((wg["ffn2_w3"],),) = ret_fwd(proj, ret_norm, tables, "ret_fwd", gather("ffn2_w3"))
    (h2,), _ = mix_out(pa, rb, wg["w_out"], h1, "mix_out")
    (n2, ga2, gb2, s2), ((wg["ffn2_w2"],),) = ffn_up(
        h2, ffn2_norm, whole("ffn2_w1"), whole("ffn2_w3"), "ffn2_up", gather("ffn2_w2"))
    (h3,), _ = ffn_down(s2, whole("ffn2_w2"), h2, "ffn2_down")
    (dh3, loss, d_final), _ = final_loss(h3, final_norm[None], target, "final_loss")

    (da2, db2, df2), _ = ffn_bwd_act(dh3, whole("ffn2_w2"), ga2, gb2, "ffn2_bwd_act")
    (g_f2w2,), _ = ffn_dw([s2], df2, 1, "ffn2_dw2")
    g_f2w2 = sharded(g_f2w2)
    (g_f2w1, g_f2w3), ((r_f2w2,),) = ffn_dw([da2, db2], n2, 2, "ffn2_dw13", [SiblingExchange([g_f2w2])])
    g_f2w1, g_f2w3 = sharded(g_f2w1), sharded(g_f2w3)
    p_f2w2 = reduce_in_chip("ffn2_w2", g_f2w2, r_f2w2)
    (dh2, d_ffn2), ((q_f2w2,), (r_f2w1, r_f2w3)) = ffn_bwd_in(
        da2, db2, whole("ffn2_w1"), whole("ffn2_w3"), h2, ffn2_norm, dh3, "ffn2_bwd_in",
        [scatter(p_f2w2), SiblingExchange([g_f2w1, g_f2w3])])
    p_f2w1 = reduce_in_chip("ffn2_w1", g_f2w1, r_f2w1)
    p_f2w3 = reduce_in_chip("ffn2_w3", g_f2w3, r_f2w3)
    (dpa, drb, g_wout), _ = mix_out_bwd(dh2, wg["w_out"], pa, rb, "mix_out_bwd")
    (dproj, d_pool_w, d_pool_scale), _ = pool_bwd(proj, dpa, pool_w[0], pool_scale, "pool_bwd")
    (dproj, d_ret_norm), ((q_f2w1, q_f2w3), (r_wout,)) = ret_bwd(
        proj, drb, o_pre, r_prev, ret_norm, tables, dproj, "ret_bwd", [scatter(p_f2w1, p_f2w3), SiblingExchange([g_wout])])
    p_wout = reduce_in_chip("w_out", g_wout, r_wout)
    (g_win,), ((q_wout,),) = mix_dwin(u, dproj, N_CHIPS, "mix_dwin", [scatter(p_wout)])
    (dh1, d_mix), ((r_win,),) = mix_in_bwd(dproj, wg["w_in"], h1, mix_norm, dh2, "mix_in_bwd", [SiblingExchange([g_win])])
    p_win = reduce_in_chip("w_in", g_win, r_win)
    (da1, db1, df1), ((q_win,),) = ffn_bwd_act(dh1, whole("ffn1_w2"), ga1, gb1, "ffn1_bwd_act", [scatter(p_win)])
    (g_f1w1, g_f1w3), _ = ffn_dw([da1, db1], n1, 2, "ffn1_dw13")
    g_f1w1, g_f1w3 = sharded(g_f1w1), sharded(g_f1w3)
    (g_f1w2,), ((r_f1w1, r_f1w3),) = ffn_dw([s1], df1, 1, "ffn1_dw2", [SiblingExchange([g_f1w1, g_f1w3])])
    g_f1w2 = sharded(g_f1w2)
    p_f1w1 = reduce_in_chip("ffn1_w1", g_f1w1, r_f1w1)
    p_f1w3 = reduce_in_chip("ffn1_w3", g_f1w3, r_f1w3)
    (dx, d_ffn1), ((q_f1w1, q_f1w3), (r_f1w2,)) = ffn_bwd_in(
        da1, db1, whole("ffn1_w1"), whole("ffn1_w3"), xs, ffn1_norm, dh1, "ffn1_bwd_in",
        [scatter(p_f1w1, p_f1w3), SiblingExchange([g_f1w2])])
    p_f1w2 = reduce_in_chip("ffn1_w2", g_f1w2, r_f1w2)

    d_vectors = {"ffn1_norm": d_ffn1, "mix_norm": d_mix, "pool_scale": d_pool_scale, "ret_norm": d_ret_norm,
                 "ffn2_norm": d_ffn2, "final_norm": d_final}
    pack = jnp.concatenate([d_pool_w.reshape(-1, 128), _pack_vectors(d_vectors), jnp.broadcast_to(loss, (8, 128))], axis=0)
    (q_f1w2,), (packs,) = update(["ffn2_w1", "ffn2_w3", "ffn1_w1", "ffn1_w3"], [q_f2w1, q_f2w3, q_f1w1, q_f1w3], "adamw_w13",
                                 [scatter(p_f1w2), AllExchange(pack)])
    update(["ffn2_w2", "ffn1_w2"], [q_f2w2, q_f1w2], "adamw_w2")
    update(["w_in"], [q_win], "adamw_w_in")
    update(["w_out"], [q_wout], "adamw_w_out")
    of_pool, of_vectors, loss_sum = adamw_small(packs, [t["pool_w"].reshape(-1, 128) for t in (w, m, v)],
                                                [_pack_vectors(t) for t in (w, m, v)], "adamw_small")
    for res, pool_part, vector_part in zip((grad, delta, new_m, new_v), of_pool, of_vectors):
        res["pool_w"] = pool_part.reshape(pool_w.shape)
        res.update(_unpack_vectors(vector_part, w))
    loss = loss_sum[0, 0]

    return (loss, dx[None], *[grad[k] for k in WEIGHTS], *[delta[k] for k in WEIGHTS],
            *[new_m[k] for k in WEIGHTS], *[new_v[k] for k in WEIGHTS])
```

```python
import math

import jax
import jax.numpy as jnp
from jax import lax
from jax.experimental import pallas as pl
from jax.experimental.pallas import tpu as pltpu

F32 = jnp.float32
BF16 = jnp.bfloat16

EPS = 1e-6
N_CHIPS = 4
N_GROUPS = 4
HEAD_DIM = 128
RET_CHUNK = 128
ROPE_BASE = 10000.0
ADAM_LR, ADAM_B1, ADAM_B2, ADAM_EPS, ADAM_WD, ADAM_STEP = 0.001, 0.9, 0.999, 1e-08, 0.01, 10
VMEM_LIMIT_V7X = 56 * 1024 * 1024
ADAMW_VMEM_BUDGET = 32 * 1024 * 1024
MESH = pl.DeviceIdType.MESH
ANY = pl.BlockSpec(memory_space=pl.ANY)


def _dot(a, b):
    return jnp.dot(a, b, preferred_element_type=F32)


def _dot_nt(a, b):
    return lax.dot_general(a, b, (((1,), (1,)), ((), ())), preferred_element_type=F32)


def _dot_tn(a, b):
    return lax.dot_general(a, b, (((0,), (0,)), ((), ())), preferred_element_type=F32)


def _rstd(h):
    return lax.rsqrt(jnp.mean(h * h, axis=-1, keepdims=True) + EPS)


def _rmsnorm_bwd(dn, h, gain):
    r = _rstd(h)
    nh = h * r
    dnh = dn * gain
    dh = r * (dnh - nh * jnp.mean(dnh * nh, axis=-1, keepdims=True))
    return dh, dn * nh


def _silu_parts(a):
    sig = jax.nn.sigmoid(a)
    silu = a * sig
    return silu, sig + silu * (1.0 - sig)


def _mesh_pos():
    return lax.axis_index("x"), lax.axis_index("y"), lax.axis_index("c")


class ChipExchange:
    def __init__(self, srcs, scatter, placed=()):
        n = len(srcs)
        self.inputs, self.scatter, self.n, self.reach = list(srcs) + list(placed), scatter, n, REACH_CHIPS
        self.aliases = {n + t: t for t in range(n)} if scatter else {}
        self.half_rows = [s.shape[1] if scatter else s.shape[0] // 2 for s in srcs]
        self.out_shape = [jax.ShapeDtypeStruct((N_CHIPS, 2 * rh, s.shape[-1]), s.dtype) for s, rh in zip(srcs, self.half_rows)]
        if scatter:
            self.out_shape += [jax.ShapeDtypeStruct((2, rh // 2, s.shape[-1]), s.dtype) for s, rh in zip(srcs, self.half_rows)]
        dma = pltpu.SemaphoreType.DMA
        self.sems = [dma((4 * n,)), dma((4 * n,)), dma((2 * n,)), dma((2 * n,)), dma((4 * n,)), dma((4 * n,))]

    def _copies(self, src, out, sems):
        hop1_send, hop1_recv, hop2_send, hop2_recv, d2d_send, d2d_recv = sems
        x, y, c = _mesh_pos()
        me, dg = 2 * x + y, 2 * (1 - x) + (1 - y)
        sibling = (x, y, 1 - c)
        n = self.n
        mine, theirs = c, 1 - c

        def nb(a):
            nx, ny = x ^ (1 - a), y ^ a
            return 2 * nx + ny, (nx, ny, c)

        def remote(s, d, send, recv, k, to):
            return pltpu.make_async_remote_copy(src_ref=s, dst_ref=d, send_sem=send.at[k], recv_sem=recv.at[k],
                                                device_id=to, device_id_type=MESH)

        class Copies:
            def slot(_, t, chip, half):
                rh = self.half_rows[t]
                return out[t].at[chip, pl.ds(half * rh, rh), :]

            def quarter(_, t, chip, q):
                qh = self.half_rows[t] // 2
                return out[t].at[chip, pl.ds(mine * 2 * qh + q * qh, qh), :]

            def own_shard(k, t):
                return remote(src[t], out[t].at[me], d2d_send, d2d_recv, 4 * t + 3, sibling)

            def hop1(k, t, a, transit=False):
                rh = self.half_rows[t]
                chip, to = nb(a)
                if transit:
                    piece = src[t].at[dg, pl.ds(a * (rh // 2), rh // 2), :]
                    return remote(piece, out[n + t].at[a], hop1_send, hop1_recv, 4 * t + 2 + a, to)
                piece = src[t].at[chip] if self.scatter else src[t].at[pl.ds(mine * rh, rh), :]
                return remote(piece, k.slot(t, me, mine), hop1_send, hop1_recv, 4 * t + a, to)

            def landed1(k, t, a, transit=False):
                here = out[n + t].at[a] if transit else k.slot(t, nb(a)[0], mine)
                return remote(here, here, hop1_send, hop1_recv, 4 * t + (2 if transit else 0) + a, sibling)

            def hop2(k, t, q):
                origin, to = nb(q)[0], nb(1 - q)[1]
                piece = out[n + t].at[q] if self.scatter else k.quarter(t, origin, q)
                return remote(piece, k.quarter(t, origin, q), hop2_send, hop2_recv, 2 * t + q, to)

            def landed2(k, t, q):
                here = k.quarter(t, dg, q)
                return remote(here, here, hop2_send, hop2_recv, 2 * t + q, sibling)

            def d2d(k, t, p, chip, own=False, arriving=False):
                if arriving:
                    there = k.slot(t, chip, theirs)
                    return remote(there, there, d2d_send, d2d_recv, 4 * t + p, sibling)
                piece = src[t].at[me] if own else k.slot(t, chip, mine)
                return remote(piece, k.slot(t, chip, mine), d2d_send, d2d_recv, 4 * t + p, sibling)

        return Copies(), nb, me, dg, c

    def start(self, src, out, sems):
        k, nb, me, dg, c = self._copies(src, out, sems)
        for t in range(self.n):
            for first in range(2):
                a = first ^ c
                k.hop1(t, a).start()
                if self.scatter:
                    k.hop1(t, a, transit=True).start()
            if self.scatter:
                k.d2d(t, 3, me, own=True).start()
            else:
                k.own_shard(t).start()

    def mid(self, src, out, sems):
        k, nb, me, dg, c = self._copies(src, out, sems)
        for t in range(self.n):
            for first in range(2):
                a = first ^ c
                if self.scatter:
                    k.landed1(t, a, transit=True).wait_recv()
                    k.hop2(t, a).start()
                k.landed1(t, a).wait_recv()
                if not self.scatter:
                    k.hop2(t, a).start()
                k.d2d(t, a, nb(a)[0]).start()

    def finish(self, src, out, sems):
        k, nb, me, dg, c = self._copies(src, out, sems)
        for t in range(self.n):
            for q in range(2):
                k.landed2(t, q).wait_recv()
            k.d2d(t, 2, dg).start()
        for t in range(self.n):
            for a in range(2):
                k.d2d(t, a, nb(a)[0], arriving=True).wait_recv()
            k.d2d(t, 2, dg, arriving=True).wait_recv()
            if self.scatter:
                k.d2d(t, 3, me, arriving=True).wait_recv()
        for t in range(self.n):
            for a in range(2):
                k.hop1(t, a).wait_send()
                if self.scatter:
                    k.hop1(t, a, transit=True).wait_send()
                k.hop2(t, a).wait_send()
                k.d2d(t, a, nb(a)[0]).wait_send()
            k.d2d(t, 2, dg).wait_send()
            if self.scatter:
                k.d2d(t, 3, me, own=True).wait_send()
            else:
                k.own_shard(t).wait()


class SiblingExchange:
    def __init__(self, grads):
        self.inputs, self.n, self.aliases, self.reach = list(grads), len(grads), {}, REACH_SIBLING
        self.half_rows = [g.shape[1] // 2 for g in grads]
        self.out_shape = [jax.ShapeDtypeStruct((g.shape[0], rh, g.shape[2]), g.dtype) for g, rh in zip(grads, self.half_rows)]
        self.sems = [pltpu.SemaphoreType.DMA((self.n,)), pltpu.SemaphoreType.DMA((self.n,))]

    def _plan(self, src, out, sems):
        x, y, c = _mesh_pos()
        return [pltpu.make_async_remote_copy(
            src_ref=src[t].at[:, pl.ds((1 - c) * self.half_rows[t], self.half_rows[t]), :], dst_ref=out[t],
            send_sem=sems[0].at[t], recv_sem=sems[1].at[t], device_id=(x, y, 1 - c), device_id_type=MESH) for t in range(self.n)]

    def start(self, src, out, sems):
        for cp in self._plan(src, out, sems):
            cp.start()

    def mid(self, src, out, sems):
        pass

    def finish(self, src, out, sems):
        for cp in self._plan(src, out, sems):
            cp.wait()


REACH_SIBLING, REACH_CHIPS, REACH_ALL = 0, 1, 2


def _entry_barrier(reach):
    x, y, c = _mesh_pos()
    peers = [(x, y, 1 - c)]
    if reach == REACH_CHIPS:
        peers += [(1 - x, y, c), (x, 1 - y, c)]
    elif reach == REACH_ALL:
        peers = [(x ^ dx, y ^ dy, c ^ dc) for dx in (0, 1) for dy in (0, 1) for dc in (0, 1)][1:]
    barrier = pltpu.get_barrier_semaphore()
    for peer in peers:
        pl.semaphore_signal(barrier, inc=1, device_id=peer, device_id_type=MESH)
    pl.semaphore_wait(barrier, len(peers))


def _call(body, hosted=(), *, name, in_specs, out_specs, out_shape, args, grid=(), scratch_shapes=(), aliased=None):
    n_in, n_out, n_scr = len(in_specs), len(out_specs), len(scratch_shapes)
    total = math.prod(grid)
    mid_step = max(0, (5 * total) // 8 - 1)

    def full(*refs):
        pos = [0]

        def take(k):
            pos[0] += k
            return refs[pos[0] - k:pos[0]]

        ins, h_in = take(n_in), [take(len(h.inputs)) for h in hosted]
        outs, h_out = take(n_out), [take(len(h.out_shape)) for h in hosted]
        scr, h_sem = take(n_scr), [take(len(h.sems)) for h in hosted]
        step = 0
        for axis, size in enumerate(grid):
            step = step * size + pl.program_id(axis)

        def phase(at, method):
            if not hosted:
                return

            def run():
                if method == "start":
                    _entry_barrier(reach)
                for h, s, o, m in zip(hosted, h_in, h_out, h_sem):
                    getattr(h, method)(s, o, m)

            if total == 1:
                run()
            else:
                pl.when(step == at)(run)

        phase(0, "start")
        body(*ins, *outs, *scr)
        phase(mid_step, "mid")
        phase(total - 1, "finish")

    aliases, i0, o0 = dict(aliased or {}), n_in, n_out
    for h in hosted:
        aliases.update({i0 + i: o0 + o for i, o in h.aliases.items()})
        i0, o0 = i0 + len(h.inputs), o0 + len(h.out_shape)
    reach = max((h.reach for h in hosted), default=None)
    params = dict(vmem_limit_bytes=VMEM_LIMIT_V7X)
    if hosted:
        params["collective_id"] = reach
    results = pl.pallas_call(
        full, name=name, grid=grid,
        in_specs=list(in_specs) + [ANY] * (i0 - n_in),
        out_specs=list(out_specs) + [ANY] * (o0 - n_out),
        out_shape=list(out_shape) + [s for h in hosted for s in h.out_shape],
        scratch_shapes=list(scratch_shapes) + [s for h in hosted for s in h.sems],
        input_output_aliases=aliases,
        compiler_params=pltpu.CompilerParams(**params),
    )(*args, *[s for h in hosted for s in h.inputs])
    outs, extras, pos = list(results[:n_out]), [], n_out
    for h in hosted:
        extras.append(list(results[pos:pos + h.n]))
        pos += len(h.out_shape)
    return outs, extras


def exchange(hosted, name):
    return _call(lambda: None, hosted, name=name, in_specs=[], out_specs=[], out_shape=[], args=[])[1]


class AllExchange:
    def __init__(self, pack):
        self.inputs, self.n, self.aliases, self.reach = [pack], 1, {}, REACH_ALL
        self.out_shape = [jax.ShapeDtypeStruct((2 * N_CHIPS,) + pack.shape, pack.dtype)]
        self.sems = [pltpu.SemaphoreType.DMA, pltpu.SemaphoreType.DMA((7,)), pltpu.SemaphoreType.DMA((7,))]

    def _copies(self, src, out, sems):
        local_sem, send_sem, recv_sem = sems
        x, y, c = _mesh_pos()
        flips = [(dx, dy, dc) for dx in (0, 1) for dy in (0, 1) for dc in (0, 1)][1:]
        peers = [(x ^ dx, y ^ dy, c ^ dc) for dx, dy, dc in flips]
        remote = lambda s, d, k: pltpu.make_async_remote_copy(
            src_ref=s, dst_ref=d, send_sem=send_sem.at[k], recv_sem=recv_sem.at[k], device_id=peers[k], device_id_type=MESH)
        sends = [remote(src[0], out[0].at[4 * x + 2 * y + c], k) for k in range(7)]
        landed = [remote(out[0].at[4 * px + 2 * py + pc], out[0].at[4 * px + 2 * py + pc], k) for k, (px, py, pc) in enumerate(peers)]
        return sends, landed, pltpu.make_async_copy(src[0], out[0].at[4 * x + 2 * y + c], local_sem)

    def start(self, src, out, sems):
        sends, _, local = self._copies(src, out, sems)
        for cp in sends:
            cp.start()
        local.start()

    def mid(self, src, out, sems):
        pass

    def finish(self, src, out, sems):
        sends, landed, local = self._copies(src, out, sems)
        for cp in landed:
            cp.wait_recv()
        for cp in sends:
            cp.wait_send()
        local.wait()


MXU_COLS = 256


def _resident(shape):
    return pl.BlockSpec(shape, lambda *_: (0,) * len(shape), pipeline_mode=pl.Buffered(1))


def ffn_up(h, gain, w1, w3, name, hosted=()):
    T, D = h.shape
    F = w1.shape[0]
    tm = min(T, 512)

    def body(h_ref, g_ref, w1_ref, w3_ref, n_ref, ga_ref, gb_ref, s_ref):
        hh = h_ref[...]
        n = (hh * _rstd(hh) * g_ref[...]).astype(BF16)
        n_ref[...] = n
        for c in range(0, F, MXU_COLS):
            cols = slice(c, c + MXU_COLS)
            a = _dot_nt(n, w1_ref[cols, :])
            b = _dot_nt(n, w3_ref[cols, :])
            silu, dsilu = _silu_parts(a)
            ga_ref[:, cols] = (b * dsilu).astype(BF16)
            gb_ref[:, cols] = silu.astype(BF16)
            s_ref[:, cols] = (silu * b).astype(BF16)

    act = jax.ShapeDtypeStruct((T, F), BF16)
    act_spec = pl.BlockSpec((tm, F), lambda i: (i, 0))
    row_spec = pl.BlockSpec((tm, D), lambda i: (i, 0))
    return _call(
        body, hosted, name=name, grid=(T // tm,),
        in_specs=[row_spec, pl.BlockSpec((1, D), lambda i: (0, 0)), _resident((F, D)), _resident((F, D))],
        out_specs=[row_spec, act_spec, act_spec, act_spec],
        out_shape=[jax.ShapeDtypeStruct((T, D), BF16), act, act, act],
        args=[h, gain, w1, w3])


def ffn_down(s, w2, h, name, hosted=()):
    T, F = s.shape
    D = h.shape[1]
    tm = min(T, 512)

    def body(s_ref, w2_ref, h_ref, o_ref):
        o_ref[...] = h_ref[...] + 0.5 * _dot(s_ref[...], w2_ref[...])

    row_spec = pl.BlockSpec((tm, D), lambda i: (i, 0))
    return _call(
        body, hosted, name=name, grid=(T // tm,),
        in_specs=[pl.BlockSpec((tm, F), lambda i: (i, 0)), pl.BlockSpec((F, D), lambda i: (0, 0)), row_spec],
        out_specs=[row_spec],
        out_shape=[jax.ShapeDtypeStruct((T, D), F32)],
        args=[s, w2, h])


def ffn_bwd_act(dh, w2, ga, gb, name, hosted=()):
    T, D = dh.shape
    F = w2.shape[0]
    tm = min(T, 512)

    def body(dh_ref, w2_ref, ga_ref, gb_ref, da_ref, db_ref, df_ref):
        df = (0.5 * dh_ref[...]).astype(BF16)
        df_ref[...] = df
        for c in range(0, F, MXU_COLS):
            cols = slice(c, c + MXU_COLS)
            ds = _dot_nt(df, w2_ref[cols, :])
            da_ref[:, cols] = (ds * ga_ref[:, cols].astype(F32)).astype(BF16)
            db_ref[:, cols] = (ds * gb_ref[:, cols].astype(F32)).astype(BF16)

    act = jax.ShapeDtypeStruct((T, F), BF16)
    act_spec = pl.BlockSpec((tm, F), lambda i: (i, 0))
    row_spec = pl.BlockSpec((tm, D), lambda i: (i, 0))
    return _call(
        body, hosted, name=name, grid=(T // tm,),
        in_specs=[row_spec, _resident((F, D)), act_spec, act_spec],
        out_specs=[act_spec, act_spec, row_spec],
        out_shape=[act, act, jax.ShapeDtypeStruct((T, D), BF16)],
        args=[dh, w2, ga, gb])


def ffn_dw(xs, y, halves, name, hosted=()):
    T, F = xs[0].shape
    D = y.shape[1]
    nx, fh = len(xs), F // halves
    tk = min(T, 512)
    nk = T // tk

    def body(*refs):
        y_ref, x_refs, o_refs, accs = refs[0], refs[1:1 + nx], refs[1 + nx:1 + 2 * nx], refs[1 + 2 * nx:]
        k = pl.program_id(1)

        @pl.when(k == 0)
        def _():
            for acc in accs:
                acc[...] = jnp.zeros_like(acc)

        yy = y_ref[...]
        for x_ref, acc in zip(x_refs, accs):
            acc[...] += _dot_tn(x_ref[...], yy)

        @pl.when(k == nk - 1)
        def _():
            for o_ref, acc in zip(o_refs, accs):
                o_ref[...] = acc[...].astype(BF16)

    out = jax.ShapeDtypeStruct((F, D), BF16)
    return _call(
        body, hosted, name=name, grid=(halves, nk),
        in_specs=[pl.BlockSpec((tk, D), lambda j, k: (k, 0))] + [pl.BlockSpec((tk, fh), lambda j, k: (k, j))] * nx,
        out_specs=[pl.BlockSpec((fh, D), lambda j, k: (j, 0))] * nx,
        out_shape=[out] * nx,
        scratch_shapes=[pltpu.VMEM((fh, D), F32)] * nx,
        args=[y] + list(xs))


def ffn_bwd_in(da, db, w1, w3, h, gain, dh, name, hosted=()):
    T, F = da.shape
    D = h.shape[1]
    tm = min(T, 512)

    def body(da_ref, db_ref, w1_ref, w3_ref, h_ref, g_ref, dh_ref, o_ref, dg_ref):
        dn = _dot(da_ref[...], w1_ref[...]) + _dot(db_ref[...], w3_ref[...])
        dhn, dg = _rmsnorm_bwd(dn, h_ref[...], g_ref[...])
        o_ref[...] = dh_ref[...] + dhn

        @pl.when(pl.program_id(0) == 0)
        def _():
            dg_ref[...] = jnp.zeros_like(dg_ref)

        dg_ref[...] += jnp.sum(dg, axis=0, keepdims=True)

    act_spec = pl.BlockSpec((tm, F), lambda i: (i, 0))
    row_spec = pl.BlockSpec((tm, D), lambda i: (i, 0))
    vec_spec = pl.BlockSpec((1, D), lambda i: (0, 0))
    return _call(
        body, hosted, name=name, grid=(T // tm,),
        in_specs=[act_spec, act_spec, _resident((F, D)), _resident((F, D)), row_spec, vec_spec, row_spec],
        out_specs=[row_spec, vec_spec],
        out_shape=[jax.ShapeDtypeStruct((T, D), F32), jax.ShapeDtypeStruct((1, D), F32)],
        args=[da, db, w1, w3, h, gain, dh])


def mix_in(h, gain, wing, name, hosted=()):
    T, D = h.shape
    nsh, _, Cs = wing.shape
    tm = min(T, 512)

    def body(h_ref, g_ref, w_ref, u_ref, p_ref):
        hh = h_ref[...]
        u = (hh * _rstd(hh) * g_ref[...]).astype(BF16)
        u_ref[...] = u
        for j in range(nsh):
            p_ref[:, j * Cs:(j + 1) * Cs] = _dot(u, w_ref[j])

    return _call(
        body, hosted, name=name, grid=(T // tm,),
        in_specs=[pl.BlockSpec((tm, D), lambda i: (i, 0)), pl.BlockSpec((1, D), lambda i: (0, 0)),
                  pl.BlockSpec((nsh, D, Cs), lambda i: (0, 0, 0))],
        out_specs=[pl.BlockSpec((tm, D), lambda i: (i, 0)), pl.BlockSpec((tm, nsh * Cs), lambda i: (i, 0))],
        out_shape=[jax.ShapeDtypeStruct((T, D), BF16), jax.ShapeDtypeStruct((T, nsh * Cs), F32)],
        args=[h, gain, wing])


def mix_out(a, b, woutg, h, name, hosted=()):
    T, W = a.shape
    D = h.shape[1]
    wout = woutg.reshape(2, W, D)
    tm = min(T, 512)

    def body(a_ref, b_ref, w_ref, h_ref, o_ref):
        o_ref[...] = h_ref[...] + _dot(a_ref[...], w_ref[0]) + _dot(b_ref[...], w_ref[1])

    return _call(
        body, hosted, name=name, grid=(T // tm,),
        in_specs=[pl.BlockSpec((tm, W), lambda i: (i, 0)), pl.BlockSpec((tm, W), lambda i: (i, 0)),
                  pl.BlockSpec((2, W, D), lambda i: (0, 0, 0)), pl.BlockSpec((tm, D), lambda i: (i, 0))],
        out_specs=[pl.BlockSpec((tm, D), lambda i: (i, 0))],
        out_shape=[jax.ShapeDtypeStruct((T, D), F32)],
        args=[a, b, wout, h])


def mix_out_bwd(dh, woutg, a, b, name, hosted=()):
    T, D = dh.shape
    W = a.shape[1]
    nsh, Rs, _ = woutg.shape
    wout = woutg.reshape(2, W, D)
    tk = min(T, 512)
    nk = T // tk

    def body(dh_ref, w_ref, a_ref, b_ref, da_ref, db_ref, dw_ref, acc):
        k = pl.program_id(0)

        @pl.when(k == 0)
        def _():
            acc[...] = jnp.zeros_like(acc)

        dhb = dh_ref[...].astype(BF16)
        da_ref[...] = _dot_nt(dhb, w_ref[0])
        db_ref[...] = _dot_nt(dhb, w_ref[1])
        acc[0:W, :] += _dot_tn(a_ref[...], dhb)
        acc[W:2 * W, :] += _dot_tn(b_ref[...], dhb)

        @pl.when(k == nk - 1)
        def _():
            for j in range(nsh):
                dw_ref[j] = acc[j * Rs:(j + 1) * Rs, :].astype(BF16)

    return _call(
        body, hosted, name=name, grid=(nk,),
        in_specs=[pl.BlockSpec((tk, D), lambda k: (k, 0)), pl.BlockSpec((2, W, D), lambda k: (0, 0, 0)),
                  pl.BlockSpec((tk, W), lambda k: (k, 0)), pl.BlockSpec((tk, W), lambda k: (k, 0))],
        out_specs=[pl.BlockSpec((tk, W), lambda k: (k, 0)), pl.BlockSpec((tk, W), lambda k: (k, 0)),
                   pl.BlockSpec((nsh, Rs, D), lambda k: (0, 0, 0))],
        out_shape=[jax.ShapeDtypeStruct((T, W), F32), jax.ShapeDtypeStruct((T, W), F32),
                   jax.ShapeDtypeStruct((nsh, Rs, D), BF16)],
        scratch_shapes=[pltpu.VMEM((2 * W, D), F32)],
        args=[dh, wout, a, b])


def _dproj_block(g):
    return (g // N_GROUPS + N_GROUPS) % (N_GROUPS + 1), g % N_GROUPS


def mix_dwin(u, dproj, nsh, name, hosted=()):
    T, D = u.shape
    Hd = HEAD_DIM
    slabs, _, width = dproj.shape
    blocks = slabs * width // Hd
    Cs = blocks * Hd // nsh
    tk = min(T, 512)
    nk = T // tk

    def body(u_ref, d_ref, o_ref, acc):
        k = pl.program_id(0)

        @pl.when(k == 0)
        def _():
            acc[...] = jnp.zeros_like(acc)

        where = [_dproj_block(g) for g in range(blocks)]
        d = jnp.concatenate([d_ref[slab, :, col * Hd:(col + 1) * Hd] for slab, col in where], axis=1)
        acc[...] += _dot_tn(u_ref[...], d)

        @pl.when(k == nk - 1)
        def _():
            for j in range(nsh):
                o_ref[j] = acc[:, j * Cs:(j + 1) * Cs].astype(BF16)

    return _call(
        body, hosted, name=name, grid=(nk,),
        in_specs=[pl.BlockSpec((tk, D), lambda k: (k, 0)), pl.BlockSpec((slabs, tk, width), lambda k: (0, k, 0))],
        out_specs=[pl.BlockSpec((nsh, D, Cs), lambda k: (0, 0, 0))],
        out_shape=[jax.ShapeDtypeStruct((nsh, D, Cs), BF16)],
        scratch_shapes=[pltpu.VMEM((D, blocks * Hd), F32)],
        args=[u, dproj])


def mix_in_bwd(dproj, wing, h, gain, dh, name, hosted=()):
    T, D = h.shape
    nsh, _, Cs = wing.shape
    Hd = HEAD_DIM
    per = Cs // Hd
    tm = min(T, 512)

    def body(d_ref, w_ref, h_ref, g_ref, dh_ref, o_ref, dg_ref):
        def shard(j):
            blocks = [_dproj_block(per * j + i) for i in range(per)]
            return jnp.concatenate([d_ref[slab, :, col * Hd:(col + 1) * Hd] for slab, col in blocks], axis=1)

        du = _dot_nt(shard(0), w_ref[0])
        for j in range(1, nsh):
            du += _dot_nt(shard(j), w_ref[j])
        dhn, dg = _rmsnorm_bwd(du, h_ref[...], g_ref[...])
        o_ref[...] = dh_ref[...] + dhn

        @pl.when(pl.program_id(0) == 0)
        def _():
            dg_ref[...] = jnp.zeros_like(dg_ref)

        dg_ref[...] += jnp.sum(dg, axis=0, keepdims=True)

    row_spec = pl.BlockSpec((tm, D), lambda i: (i, 0))
    vec_spec = pl.BlockSpec((1, D), lambda i: (0, 0))
    return _call(
        body, hosted, name=name, grid=(T // tm,),
        in_specs=[pl.BlockSpec((dproj.shape[0], tm, dproj.shape[2]), lambda i: (0, i, 0)),
                  pl.BlockSpec((nsh, D, Cs), lambda i: (0, 0, 0)), row_spec, vec_spec, row_spec],
        out_specs=[row_spec, vec_spec],
        out_shape=[jax.ShapeDtypeStruct((T, D), F32), jax.ShapeDtypeStruct((1, D), F32)],
        args=[dproj, wing, h, gain, dh])


def _pool_window(x, group, T, trailing):
    rows = lax.broadcasted_iota(jnp.int32, x.shape, 0)

    def shifted(z, k):
        if trailing:
            return jnp.where(rows >= k, pltpu.roll(z, k, 0), 0.0)
        return jnp.where(rows < T - k, pltpu.roll(z, T - k, 0), 0.0)

    s2 = x + shifted(x, 1)
    s4 = s2 + shifted(s2, 2)
    s8 = s4 + shifted(s4, 4)
    s16 = s8 + shifted(s8, 8)
    return jnp.where(group == 0, s2, jnp.where(group == 1, s4, jnp.where(group == 2, s8, s16)))


def _pool_count(group, shape):
    rows = lax.broadcasted_iota(jnp.int32, shape, 0)
    w = jnp.where(group == 0, 2, jnp.where(group == 1, 4, jnp.where(group == 2, 8, 16)))
    return jnp.minimum(rows + 1, w).astype(F32)


def pool_fwd(proj, pool_w, pool_scale, name, hosted=()):
    T = proj.shape[0]
    Hd = HEAD_DIM

    def body(x_ref, w_ref, sc_ref, a_ref):
        g = pl.program_id(0)
        x = x_ref[...]
        pooled = _pool_window(x, g, T, True) / _pool_count(g, x.shape) - x
        a_ref[...] = (_dot(pooled.astype(BF16), w_ref[0].astype(BF16)) * sc_ref[...]).astype(BF16)

    return _call(
        body, hosted, name=name, grid=(N_GROUPS,),
        in_specs=[pl.BlockSpec((T, Hd), lambda g: (0, g)), pl.BlockSpec((1, Hd, Hd), lambda g: (g, 0, 0)),
                  pl.BlockSpec((1, Hd), lambda g: (0, g))],
        out_specs=[pl.BlockSpec((T, Hd), lambda g: (0, g))],
        out_shape=[jax.ShapeDtypeStruct((T, N_GROUPS * Hd), BF16)],
        args=[proj, pool_w, pool_scale])


def pool_bwd(proj, da, pool_w, pool_scale, name, hosted=()):
    T = proj.shape[0]
    Hd = HEAD_DIM

    def body(x_ref, da_ref, w_ref, sc_ref, dx_ref, dw_ref, dsc_ref):
        g = pl.program_id(0)
        x = x_ref[...]
        cnt = _pool_count(g, x.shape)
        pooled = (_pool_window(x, g, T, True) / cnt - x).astype(BF16)
        wb = w_ref[0].astype(BF16)
        dav = da_ref[...]
        dsc_ref[...] = jnp.sum(dav * _dot(pooled, wb), axis=0, keepdims=True)
        dout = (dav * sc_ref[...]).astype(BF16)
        dw_ref[0] = _dot_tn(pooled, dout)
        dpooled = _dot_nt(dout, wb)
        dx_ref[0] = (_pool_window(dpooled / cnt, g, T, False) - dpooled).astype(BF16)

    col_spec = pl.BlockSpec((T, Hd), lambda g: (0, g))
    return _call(
        body, hosted, name=name, grid=(N_GROUPS,),
        in_specs=[col_spec, col_spec, pl.BlockSpec((1, Hd, Hd), lambda g: (g, 0, 0)), pl.BlockSpec((1, Hd), lambda g: (0, g))],
        out_specs=[pl.BlockSpec((1, T, Hd), lambda g: (N_GROUPS, 0, g)), pl.BlockSpec((1, Hd, Hd), lambda g: (g, 0, 0)),
                   pl.BlockSpec((1, Hd), lambda g: (0, g))],
        out_shape=[jax.ShapeDtypeStruct((N_GROUPS + 1, T, N_GROUPS * Hd), BF16), jax.ShapeDtypeStruct((N_GROUPS, Hd, Hd), F32),
                   jax.ShapeDtypeStruct((1, N_GROUPS * Hd), F32)],
        args=[proj, da, pool_w, pool_scale])


def _ret_tables(T):
    Hd, C = HEAD_DIM, RET_CHUNK
    inv_freq = 1.0 / (ROPE_BASE ** (jnp.arange(0, Hd, 2, dtype=F32) / Hd))
    ang = jnp.arange(T, dtype=F32)[:, None] * inv_freq[None, :]
    cos, sin = jnp.cos(ang), jnp.sin(ang)
    cos2 = jnp.concatenate([cos, cos], axis=-1)
    sin2 = jnp.concatenate([-sin, sin], axis=-1)
    log_gamma = jnp.log1p(-jnp.exp2(-5.0 - jnp.arange(N_GROUPS, dtype=F32)))
    pos = jnp.arange(C, dtype=F32)
    rel = pos[:, None] - pos[None, :]
    intra = jnp.where(rel[None] >= 0, jnp.exp(log_gamma[:, None, None] * jnp.maximum(rel, 0.0)[None]), 0.0)
    k_tail = jnp.exp(log_gamma[:, None] * (C - 1 - pos)[None, :])
    q_head = jnp.exp(log_gamma[:, None] * (pos + 1.0)[None, :])
    chunk_decay = jnp.exp(log_gamma * C)
    wide = lambda t: jnp.broadcast_to(t[:, :, None], (N_GROUPS, C, Hd))
    return cos2, sin2, intra, wide(k_tail), wide(q_head), jnp.broadcast_to(chunk_decay[:, None, None], (N_GROUPS, 1, Hd))


def _rope(x, cos2, sin2):
    return x * cos2 + pltpu.roll(x, HEAD_DIM // 2, 1) * sin2


def _rope_t(d, cos2, sin2):
    return d * cos2 + pltpu.roll(d * sin2, HEAD_DIM // 2, 1)


def _ret_specs(tseg, seg_of):
    Hd, G = HEAD_DIM, N_GROUPS
    col = lambda kind: pl.BlockSpec((tseg, Hd), lambda h, s: (seg_of(s), G * kind + h))
    tab = pl.BlockSpec((tseg, Hd), lambda h, s: (seg_of(s), 0))
    head = pl.BlockSpec((1, RET_CHUNK, Hd), lambda h, s: (h, 0, 0))
    cd = pl.BlockSpec((1, 1, Hd), lambda h, s: (h, 0, 0))
    gain = pl.BlockSpec((1, Hd), lambda h, s: (0, h))
    return col, tab, head, cd, gain


def ret_fwd(proj, ret_norm, tables, name, hosted=()):
    T = proj.shape[0]
    Hd, C, G = HEAD_DIM, RET_CHUNK, N_GROUPS
    tseg = min(T, 1024)
    nseg, nck = T // tseg, tseg // C
    scale = Hd ** -0.5
    cos2, sin2, intra, k_tail, q_head, chunk_decay = tables

    def body(q_ref, k_ref, v_ref, g_ref, gain_ref, cos_ref, sin_ref, m_ref, kt_ref, qh_ref, cd_ref,
             b_ref, o_ref, rp_ref, state):
        @pl.when(pl.program_id(1) == 0)
        def _():
            state[...] = jnp.zeros_like(state)

        def chunk(ci, carry):
            rows = pl.ds(pl.multiple_of(ci * C, C), C)
            cos, sin = cos_ref[rows, :], sin_ref[rows, :]
            qr = _rope(q_ref[rows, :], cos, sin)
            kr = _rope(k_ref[rows, :], cos, sin) * scale
            qb, kb, vb = qr.astype(BF16), kr.astype(BF16), v_ref[rows, :].astype(BF16)
            r = state[...]
            rp_ref[0, ci] = r.astype(BF16)
            sc = _dot_nt(qb, kb) * m_ref[0]
            o = _dot(sc.astype(BF16), vb) + _dot((qr * qh_ref[0]).astype(BF16), r.astype(BF16))
            state[...] = cd_ref[0] * r + _dot_tn((kr * kt_ref[0]).astype(BF16), vb)
            o_ref[rows, :] = o
            on = o * _rstd(o)
            b_ref[rows, :] = (jax.nn.silu(g_ref[rows, :]) * (on * gain_ref[...])).astype(BF16)
            return carry

        lax.fori_loop(0, nck, chunk, 0, unroll=True)

    col, tab, head, cd, gain = _ret_specs(tseg, lambda s: s)
    out_col = pl.BlockSpec((tseg, Hd), lambda h, s: (s, h))
    return _call(
        body, hosted, name=name, grid=(G, nseg),
        in_specs=[col(1), col(2), col(3), col(4), gain, tab, tab, head, head, head, cd],
        out_specs=[out_col, out_col, pl.BlockSpec((1, nck, Hd, Hd), lambda h, s: (h, s, 0, 0))],
        out_shape=[jax.ShapeDtypeStruct((T, G * Hd), BF16), jax.ShapeDtypeStruct((T, G * Hd), F32),
                   jax.ShapeDtypeStruct((G, T // C, Hd, Hd), BF16)],
        scratch_shapes=[pltpu.VMEM((Hd, Hd), F32)],
        args=[proj, proj, proj, proj, ret_norm, cos2, sin2, intra, k_tail, q_head, chunk_decay])


def ret_bwd(proj, db, o_pre, r_prev, ret_norm, tables, dproj, name, hosted=()):
    T = proj.shape[0]
    Hd, C, G = HEAD_DIM, RET_CHUNK, N_GROUPS
    tseg = min(T, 1024)
    nseg, nck = T // tseg, tseg // C
    scale = Hd ** -0.5
    cos2, sin2, intra, k_tail, q_head, chunk_decay = tables

    def body(q_ref, k_ref, v_ref, g_ref, db_ref, o_ref, rp_ref, gain_ref, cos_ref, sin_ref, m_ref, kt_ref, qh_ref, cd_ref,
             _, d_ref, dgain_ref, gstate):
        @pl.when(pl.program_id(1) == 0)
        def _():
            gstate[...] = jnp.zeros_like(gstate)
            dgain_ref[...] = jnp.zeros_like(dgain_ref)

        def chunk(t, carry):
            ci = nck - 1 - t
            rows = pl.ds(pl.multiple_of(ci * C, C), C)
            cos, sin = cos_ref[rows, :], sin_ref[rows, :]
            qr = _rope(q_ref[rows, :], cos, sin)
            kr = _rope(k_ref[rows, :], cos, sin) * scale
            qb, kb, vb = qr.astype(BF16), kr.astype(BF16), v_ref[rows, :].astype(BF16)
            qhb, ktb = (qr * qh_ref[0]).astype(BF16), (kr * kt_ref[0]).astype(BF16)
            sc = (_dot_nt(qb, kb) * m_ref[0]).astype(BF16)
            o = o_ref[rows, :]
            rstd = _rstd(o)
            on = o * rstd
            gain = gain_ref[...]
            silu, dsilu = _silu_parts(g_ref[rows, :])
            dy = db_ref[rows, :]
            dgain_ref[...] += jnp.sum(dy * silu * on, axis=0, keepdims=True)
            dg = dy * on * gain * dsilu
            don = dy * silu * gain
            dob = (rstd * (don - on * jnp.mean(don * on, axis=-1, keepdims=True))).astype(BF16)
            gn = gstate[...]
            gb = gn.astype(BF16)
            da = (_dot_nt(dob, vb) * m_ref[0]).astype(BF16)
            dq = _dot(da, kb) + _dot_nt(dob, rp_ref[0, ci]) * qh_ref[0]
            dk = _dot_tn(da, qb) + _dot_nt(vb, gb) * kt_ref[0]
            dv = _dot_tn(sc, dob) + _dot(ktb, gb)
            gstate[...] = cd_ref[0] * gn + _dot_tn(qhb, dob)
            d_ref[0, rows, :] = _rope_t(dq, cos, sin).astype(BF16)
            d_ref[1, rows, :] = _rope_t(dk * scale, cos, sin).astype(BF16)
            d_ref[2, rows, :] = dv.astype(BF16)
            d_ref[3, rows, :] = dg.astype(BF16)
            return carry

        lax.fori_loop(0, nck, chunk, 0, unroll=True)

    rev = lambda s: nseg - 1 - s
    col, tab, head, cd, gain = _ret_specs(tseg, rev)
    act = pl.BlockSpec((tseg, Hd), lambda h, s: (rev(s), h))
    return _call(
        body, hosted, name=name, grid=(G, nseg),
        in_specs=[col(1), col(2), col(3), col(4), act, act, pl.BlockSpec((1, nck, Hd, Hd), lambda h, s: (h, rev(s), 0, 0)),
                  gain, tab, tab, head, head, head, cd, ANY],
        out_specs=[pl.BlockSpec((4, tseg, Hd), lambda h, s: (0, rev(s), h)), gain],
        out_shape=[jax.ShapeDtypeStruct(dproj.shape, BF16), jax.ShapeDtypeStruct((1, G * Hd), F32)],
        scratch_shapes=[pltpu.VMEM((Hd, Hd), F32)], aliased={14: 0},
        args=[proj, proj, proj, proj, db, o_pre, r_prev, ret_norm, cos2, sin2, intra, k_tail, q_head, chunk_decay, dproj])


def final_loss(h, gain, target, name, hosted=()):
    T, D = h.shape
    tm = min(T, 512)

    def body(h_ref, g_ref, t_ref, dh_ref, loss_ref, dg_ref):
        @pl.when(pl.program_id(0) == 0)
        def _():
            loss_ref[...] = jnp.zeros_like(loss_ref)
            dg_ref[...] = jnp.zeros_like(dg_ref)

        hh = h_ref[...]
        gain_v = g_ref[...]
        err = hh * _rstd(hh) * gain_v - t_ref[...]
        loss_ref[...] += 0.5 * jnp.sum(jnp.mean(err * err, axis=-1, keepdims=True), axis=0, keepdims=True)
        dhn, dg = _rmsnorm_bwd(err * (1.0 / D), hh, gain_v)
        dh_ref[...] = dhn
        dg_ref[...] += jnp.sum(dg, axis=0, keepdims=True)

    row_spec = pl.BlockSpec((tm, D), lambda i: (i, 0))
    vec_spec = pl.BlockSpec((1, D), lambda i: (0, 0))
    return _call(
        body, hosted, name=name, grid=(T // tm,),
        in_specs=[row_spec, vec_spec, row_spec],
        out_specs=[row_spec, pl.BlockSpec((1, 128), lambda i: (0, 0)), vec_spec],
        out_shape=[jax.ShapeDtypeStruct((T, D), F32), jax.ShapeDtypeStruct((1, 128), F32), jax.ShapeDtypeStruct((1, D), F32)],
        args=[h, gain, target])


def prereduce(grad, recv, place, name):
    nsh, R, C = grad.shape
    rh = R // 2

    def body(place_ref, g_ref, r_ref, o_ref, own_ref):
        piece = (g_ref[...].astype(F32) + r_ref[...].astype(F32)).astype(BF16)
        o_ref[...] = piece

        @pl.when(pl.program_id(0) == place_ref[1])
        def _():
            own_ref[...] = piece

    return pl.pallas_call(
        body, name=name,
        grid_spec=pltpu.PrefetchScalarGridSpec(
            num_scalar_prefetch=1, grid=(nsh,),
            in_specs=[pl.BlockSpec((1, rh, C), lambda j, p: (j, p[0], 0)), pl.BlockSpec((1, rh, C), lambda j, p: (j, 0, 0))],
            out_specs=[pl.BlockSpec((1, rh, C), lambda j, p: (j, 0, 0)), pl.BlockSpec((1, rh, C), lambda j, p: (p[1], p[0], 0))]),
        out_shape=[jax.ShapeDtypeStruct((nsh, rh, C), BF16), jax.ShapeDtypeStruct((nsh, R, C), BF16)],
        compiler_params=pltpu.CompilerParams(vmem_limit_bytes=VMEM_LIMIT_V7X),
    )(place, grad, recv)


def _adamw(w, g, m, v):
    m = ADAM_B1 * m + (1.0 - ADAM_B1) * g
    v = ADAM_B2 * v + (1.0 - ADAM_B2) * (g * g)
    m_hat = m / (1.0 - ADAM_B1 ** ADAM_STEP)
    v_hat = v / (1.0 - ADAM_B2 ** ADAM_STEP)
    return -ADAM_LR * (m_hat / (jnp.sqrt(v_hat) + ADAM_EPS) + ADAM_WD * w), m, v


def adamw_sharded(tensors, name, hosted=()):
    nt = len(tensors)
    nsh, R, C = tensors[0][0].shape
    lanes = -(-C // 128) * 128
    per_row = 2 * nt * lanes * (nsh * 2 + 7 * 4)
    tr = max(r for r in range(16, R + 1, 16) if R % r == 0 and r * per_row <= ADAMW_VMEM_BUDGET)

    def body(*refs):
        ins, outs = refs[:4 * nt], refs[4 * nt:]
        for t in range(nt):
            p_ref, w_ref, m_ref, v_ref = ins[4 * t:4 * t + 4]
            g_ref, d_ref, nm_ref, nv_ref = outs[4 * t:4 * t + 4]
            g = p_ref[0].astype(F32)
            for i in range(1, nsh):
                g += p_ref[i].astype(F32)
            g_ref[...] = g
            d_ref[...], nm_ref[...], nv_ref[...] = _adamw(w_ref[...], g, m_ref[...], v_ref[...])

    spec = pl.BlockSpec((tr, C), lambda i: (i, 0))
    out = jax.ShapeDtypeStruct((R, C), F32)
    return _call(
        body, hosted, name=name, grid=(R // tr,),
        in_specs=[pl.BlockSpec((nsh, tr, C), lambda i: (0, i, 0)), spec, spec, spec] * nt,
        out_specs=[spec] * (4 * nt), out_shape=[out] * (4 * nt),
        args=[a for tensor in tensors for a in tensor])


def adamw_small(packs, late, pool, vectors, name):
    ndev = packs.shape[0]
    rp, rv, rl = pool[0].shape[0], vectors[0].shape[0], late.shape[1]

    def body(p_ref, l_ref, wp, mp, vp, wv, mv, vv, gp, dp, nmp, nvp, gv, dv, nmv, nvv, loss_ref):
        g, first = p_ref[0], l_ref[0]
        for i in range(1, ndev):
            g += p_ref[i]
            first += l_ref[i]
        g_pool = g[0:rp]
        g_vec = jnp.concatenate([g[rp:rp + rl] + first, g[rp + rl:rp + rv]], axis=0)
        gp[...], gv[...], loss_ref[...] = g_pool, g_vec, g[rp + rv:rp + rv + 8]
        dp[...], nmp[...], nvp[...] = _adamw(wp[...], g_pool, mp[...], vp[...])
        dv[...], nmv[...], nvv[...] = _adamw(wv[...], g_vec, mv[...], vv[...])

    shape = lambda rows: jax.ShapeDtypeStruct((rows, 128), F32)
    outs = pl.pallas_call(body, name=name, out_shape=[shape(rp)] * 4 + [shape(rv)] * 4 + [shape(8)],
                          compiler_params=pltpu.CompilerParams(vmem_limit_bytes=VMEM_LIMIT_V7X))(packs, late, *pool, *vectors)
    return outs[0:4], outs[4:8], outs[8]


BIG = ("ffn1_w1", "ffn1_w3", "ffn1_w2", "w_in", "w_out", "ffn2_w1", "ffn2_w3", "ffn2_w2")
TRANSPOSED = ("ffn1_w1", "ffn1_w3", "ffn2_w1", "ffn2_w3")
VECTORS = ("ffn1_norm", "mix_norm", "pool_scale", "ret_norm", "ffn2_norm", "final_norm")
WEIGHTS = ("ffn1_norm", "ffn1_w1", "ffn1_w3", "ffn1_w2", "mix_norm", "w_in", "pool_w", "pool_scale", "ret_norm", "w_out",
           "ffn2_norm", "ffn2_w1", "ffn2_w3", "ffn2_w2", "final_norm")


def _pack_vectors(parts):
    return jnp.concatenate([parts[k].reshape(-1, 128) for k in VECTORS], axis=0)


def _unpack_vectors(pack, like):
    out, row = {}, 0
    for k in VECTORS:
        rows = like[k].size // 128
        out[k] = pack[row:row + rows].reshape(like[k].shape)
        row += rows
    return out


def kernel(x, ffn1_norm, ffn1_w1, ffn1_w3, ffn1_w2, mix_norm, w_in, pool_w, pool_scale, ret_norm, w_out, ffn2_norm, ffn2_w1, ffn2_w3, ffn2_w2, final_norm, loss_target, m_ffn1_norm, m_ffn1_w1, m_ffn1_w3, m_ffn1_w2, m_mix_norm, m_w_in, m_pool_w, m_pool_scale, m_ret_norm, m_w_out, m_ffn2_norm, m_ffn2_w1, m_ffn2_w3, m_ffn2_w2, m_final_norm, v_ffn1_norm, v_ffn1_w1, v_ffn1_w3, v_ffn1_w2, v_mix_norm, v_w_in, v_pool_w, v_pool_scale, v_ret_norm, v_w_out, v_ffn2_norm, v_ffn2_w1, v_ffn2_w3, v_ffn2_w2, v_final_norm):
    w = dict(ffn1_norm=ffn1_norm, ffn1_w1=ffn1_w1, ffn1_w3=ffn1_w3, ffn1_w2=ffn1_w2, mix_norm=mix_norm, w_in=w_in, pool_w=pool_w,
             pool_scale=pool_scale, ret_norm=ret_norm, w_out=w_out, ffn2_norm=ffn2_norm, ffn2_w1=ffn2_w1, ffn2_w3=ffn2_w3,
             ffn2_w2=ffn2_w2, final_norm=final_norm)
    m = dict(ffn1_norm=m_ffn1_norm, ffn1_w1=m_ffn1_w1, ffn1_w3=m_ffn1_w3, ffn1_w2=m_ffn1_w2, mix_norm=m_mix_norm, w_in=m_w_in,
             pool_w=m_pool_w, pool_scale=m_pool_scale, ret_norm=m_ret_norm, w_out=m_w_out, ffn2_norm=m_ffn2_norm, ffn2_w1=m_ffn2_w1,
             ffn2_w3=m_ffn2_w3, ffn2_w2=m_ffn2_w2, final_norm=m_final_norm)
    v = dict(ffn1_norm=v_ffn1_norm, ffn1_w1=v_ffn1_w1, ffn1_w3=v_ffn1_w3, ffn1_w2=v_ffn1_w2, mix_norm=v_mix_norm, w_in=v_w_in,
             pool_w=v_pool_w, pool_scale=v_pool_scale, ret_norm=v_ret_norm, w_out=v_w_out, ffn2_norm=v_ffn2_norm, ffn2_w1=v_ffn2_w1,
             ffn2_w3=v_ffn2_w3, ffn2_w2=v_ffn2_w2, final_norm=v_final_norm)
    xs, target = x[0], loss_target[0]
    T = xs.shape[0]
    tables = _ret_tables(T)
    place = jnp.stack([lax.axis_index("c"), 2 * lax.axis_index("x") + lax.axis_index("y")]).astype(jnp.int32)
    local = lambda d, k: jnp.transpose(d[k][0]) if k in TRANSPOSED else d[k][0]
    result = lambda o, k: jnp.transpose(o)[None] if k in TRANSPOSED else o[None]
    sh = {k: local(w, k).astype(BF16) for k in BIG}
    gather = lambda *names: [ChipExchange([sh[k] for k in names], False)]
    wg, grad, delta, new_m, new_v = {}, {}, {}, {}, {}

    def update(names, pieces, name, hosted=()):
        outs, extras = adamw_sharded([(p, local(w, k), local(m, k), local(v, k)) for k, p in zip(names, pieces)], name, hosted)
        for t, k in enumerate(names):
            grad[k], delta[k], new_m[k], new_v[k] = [result(o, k) for o in outs[4 * t:4 * t + 4]]
        return extras

    def reduce_in_chip(name, partial, recv):
        return prereduce(partial, recv, place, "prereduce_" + name)

    scatter = lambda *reduced: ChipExchange([r[0] for r in reduced], True, [r[1] for r in reduced])
    whole = lambda k: wg[k].reshape(-1, wg[k].shape[-1])
    sharded = lambda g: g.reshape(N_CHIPS, -1, g.shape[-1])

    (wg["ffn1_w1"], wg["ffn1_w3"]), = exchange(gather("ffn1_w1", "ffn1_w3"), "gather_ffn1")
    (n1, ga1, gb1, s1), ((wg["ffn1_w2"], wg["w_in"]),) = ffn_up(
        xs, ffn1_norm, whole("ffn1_w1"), whole("ffn1_w3"), "ffn1_up", gather("ffn1_w2", "w_in"))
    (h1,), ((wg["w_out"],),) = ffn_down(s1, whole("ffn1_w2"), xs, "ffn1_down", gather("w_out"))
    (u, proj), ((wg["ffn2_w1"],),) = mix_in(h1, mix_norm, wg["w_in"], "mix_in", gather("ffn2_w1"))
    (pa,), _ = pool_fwd(proj, pool_w[0], pool_scale, "pool_fwd")
    (rb, o_pre, r_prev), ((wg["ffn2_w3"],),) = ret_fwd(proj, ret_norm, tables, "ret_fwd", gather("ffn2_w3"))
    (h2,), _ = mix_out(pa, rb, wg["w_out"], h1, "mix_out")
    (n2, ga2, gb2, s2), ((wg["ffn2_w2"],),) = ffn_up(
        h2, ffn2_norm, whole("ffn2_w1"), whole("ffn2_w3"), "ffn2_up", gather("ffn2_w2"))
    (h3,), _ = ffn_down(s2, whole("ffn2_w2"), h2, "ffn2_down")
    (dh3, loss, d_final), _ = final_loss(h3, final_norm[None], target, "final_loss")

    (da2, db2, df2), _ = ffn_bwd_act(dh3, whole("ffn2_w2"), ga2, gb2, "ffn2_bwd_act")
    (g_f2w2,), _ = ffn_dw([s2], df2, 1, "ffn2_dw2")
    g_f2w2 = sharded(g_f2w2)
    (g_f2w1, g_f2w3), ((r_f2w2,),) = ffn_dw([da2, db2], n2, 2, "ffn2_dw13", [SiblingExchange([g_f2w2])])
    g_f2w1, g_f2w3 = sharded(g_f2w1), sharded(g_f2w3)
    p_f2w2 = reduce_in_chip("ffn2_w2", g_f2w2, r_f2w2)
    (dh2, d_ffn2), ((q_f2w2,), (r_f2w1, r_f2w3)) = ffn_bwd_in(
        da2, db2, whole("ffn2_w1"), whole("ffn2_w3"), h2, ffn2_norm, dh3, "ffn2_bwd_in",
        [scatter(p_f2w2), SiblingExchange([g_f2w1, g_f2w3])])
    p_f2w1 = reduce_in_chip("ffn2_w1", g_f2w1, r_f2w1)
    p_f2w3 = reduce_in_chip("ffn2_w3", g_f2w3, r_f2w3)
    (dpa, drb, g_wout), _ = mix_out_bwd(dh2, wg["w_out"], pa, rb, "mix_out_bwd")
    (dproj, d_pool_w, d_pool_scale), _ = pool_bwd(proj, dpa, pool_w[0], pool_scale, "pool_bwd")
    (dproj, d_ret_norm), ((q_f2w1, q_f2w3), (r_wout,)) = ret_bwd(
        proj, drb, o_pre, r_prev, ret_norm, tables, dproj, "ret_bwd", [scatter(p_f2w1, p_f2w3), SiblingExchange([g_wout])])
    p_wout = reduce_in_chip("w_out", g_wout, r_wout)
    (g_win,), ((q_wout,),) = mix_dwin(u, dproj, N_CHIPS, "mix_dwin", [scatter(p_wout)])
    (dh1, d_mix), ((r_win,),) = mix_in_bwd(dproj, wg["w_in"], h1, mix_norm, dh2, "mix_in_bwd", [SiblingExchange([g_win])])
    p_win = reduce_in_chip("w_in", g_win, r_win)
    (da1, db1, df1), ((q_win,),) = ffn_bwd_act(dh1, whole("ffn1_w2"), ga1, gb1, "ffn1_bwd_act", [scatter(p_win)])
    d_vectors = {"ffn1_norm": jnp.zeros_like(ffn1_norm), "mix_norm": d_mix, "pool_scale": d_pool_scale,
                 "ret_norm": d_ret_norm, "ffn2_norm": d_ffn2, "final_norm": d_final}
    pack = jnp.concatenate([d_pool_w.reshape(-1, 128), _pack_vectors(d_vectors), jnp.broadcast_to(loss, (8, 128))], axis=0)
    (g_f1w1, g_f1w3), ((packs,),) = ffn_dw([da1, db1], n1, 2, "ffn1_dw13", [AllExchange(pack)])
    g_f1w1, g_f1w3 = sharded(g_f1w1), sharded(g_f1w3)
    (g_f1w2,), ((r_f1w1, r_f1w3),) = ffn_dw([s1], df1, 1, "ffn1_dw2", [SiblingExchange([g_f1w1, g_f1w3])])
    g_f1w2 = sharded(g_f1w2)
    p_f1w1 = reduce_in_chip("ffn1_w1", g_f1w1, r_f1w1)
    p_f1w3 = reduce_in_chip("ffn1_w3", g_f1w3, r_f1w3)
    (dx, d_ffn1), ((q_f1w1, q_f1w3), (r_f1w2,)) = ffn_bwd_in(
        da1, db1, whole("ffn1_w1"), whole("ffn1_w3"), xs, ffn1_norm, dh1, "ffn1_bwd_in",
        [scatter(p_f1w1, p_f1w3), SiblingExchange([g_f1w2])])
    p_f1w2 = reduce_in_chip("ffn1_w2", g_f1w2, r_f1w2)

    (q_f1w2,), (late,) = update(["ffn2_w1", "ffn2_w3", "ffn1_w1", "ffn1_w3"], [q_f2w1, q_f2w3, q_f1w1, q_f1w3], "adamw_w13",
                                [scatter(p_f1w2), AllExchange(d_ffn1.reshape(-1, 128))])
    update(["ffn2_w2", "ffn1_w2"], [q_f2w2, q_f1w2], "adamw_w2")
    update(["w_in"], [q_win], "adamw_w_in")
    update(["w_out"], [q_wout], "adamw_w_out")
    of_pool, of_vectors, loss_sum = adamw_small(packs, late, [t["pool_w"].reshape(-1, 128) for t in (w, m, v)],
                                                [_pack_vectors(t) for t in (w, m, v)], "adamw_small")
    for res, pool_part, vector_part in zip((grad, delta, new_m, new_v), of_pool, of_vectors):
        res["pool_w"] = pool_part.reshape(pool_w.shape)
        res.update(_unpack_vectors(vector_part, w))
    loss = loss_sum[0, 0]

    return (loss, dx[None], *[grad[k] for k in WEIGHTS], *[delta[k] for k in WEIGHTS],
            *[new_m[k] for k in WEIGHTS], *[new_v[k] for k in WEIGHTS])
```

```python
import math

import jax
import jax.numpy as jnp
from jax import lax
from jax.experimental import pallas as pl
from jax.experimental.pallas import tpu as pltpu

F32 = jnp.float32
BF16 = jnp.bfloat16

EPS = 1e-6
N_CHIPS = 4
N_GROUPS = 4
HEAD_DIM = 128
RET_CHUNK = 128
ROPE_BASE = 10000.0
ADAM_LR, ADAM_B1, ADAM_B2, ADAM_EPS, ADAM_WD, ADAM_STEP = 0.001, 0.9, 0.999, 1e-08, 0.01, 10
VMEM_LIMIT_V7X = 56 * 1024 * 1024
ADAMW_VMEM_BUDGET = 32 * 1024 * 1024
MESH = pl.DeviceIdType.MESH
ANY = pl.BlockSpec(memory_space=pl.ANY)


def _dot(a, b):
    return jnp.dot(a, b, preferred_element_type=F32)


def _dot_nt(a, b):
    return lax.dot_general(a, b, (((1,), (1,)), ((), ())), preferred_element_type=F32)


def _dot_tn(a, b):
    return lax.dot_general(a, b, (((0,), (0,)), ((), ())), preferred_element_type=F32)


def _rstd(h):
    return lax.rsqrt(jnp.mean(h * h, axis=-1, keepdims=True) + EPS)


def _rmsnorm_bwd(dn, h, gain):
    r = _rstd(h)
    nh = h * r
    dnh = dn * gain
    dh = r * (dnh - nh * jnp.mean(dnh * nh, axis=-1, keepdims=True))
    return dh, dn * nh


def _silu_parts(a):
    sig = jax.nn.sigmoid(a)
    silu = a * sig
    return silu, sig + silu * (1.0 - sig)


def _mesh_pos():
    return lax.axis_index("x"), lax.axis_index("y"), lax.axis_index("c")


class ChipExchange:
    def __init__(self, srcs, scatter, placed=()):
        n = len(srcs)
        self.inputs, self.scatter, self.n, self.reach = list(srcs) + list(placed), scatter, n, REACH_CHIPS
        self.aliases = {n + t: t for t in range(n)} if scatter else {}
        self.half_rows = [s.shape[1] if scatter else s.shape[0] // 2 for s in srcs]
        self.out_shape = [jax.ShapeDtypeStruct((N_CHIPS, 2 * rh, s.shape[-1]), s.dtype) for s, rh in zip(srcs, self.half_rows)]
        if scatter:
            self.out_shape += [jax.ShapeDtypeStruct((2, rh // 2, s.shape[-1]), s.dtype) for s, rh in zip(srcs, self.half_rows)]
        dma = pltpu.SemaphoreType.DMA
        self.sems = [dma((4 * n,)), dma((4 * n,)), dma((2 * n,)), dma((2 * n,)), dma((4 * n,)), dma((4 * n,))]

    def _copies(self, src, out, sems):
        hop1_send, hop1_recv, hop2_send, hop2_recv, d2d_send, d2d_recv = sems
        x, y, c = _mesh_pos()
        me, dg = 2 * x + y, 2 * (1 - x) + (1 - y)
        sibling = (x, y, 1 - c)
        n = self.n
        mine, theirs = c, 1 - c

        def nb(a):
            nx, ny = x ^ (1 - a), y ^ a
            return 2 * nx + ny, (nx, ny, c)

        def remote(s, d, send, recv, k, to):
            return pltpu.make_async_remote_copy(src_ref=s, dst_ref=d, send_sem=send.at[k], recv_sem=recv.at[k],
                                                device_id=to, device_id_type=MESH)

        class Copies:
            def slot(_, t, chip, half):
                rh = self.half_rows[t]
                return out[t].at[chip, pl.ds(half * rh, rh), :]

            def quarter(_, t, chip, q):
                qh = self.half_rows[t] // 2
                return out[t].at[chip, pl.ds(mine * 2 * qh + q * qh, qh), :]

            def own_shard(k, t):
                return remote(src[t], out[t].at[me], d2d_send, d2d_recv, 4 * t + 3, sibling)

            def hop1(k, t, a, transit=False):
                rh = self.half_rows[t]
                chip, to = nb(a)
                if transit:
                    piece = src[t].at[dg, pl.ds(a * (rh // 2), rh // 2), :]
                    return remote(piece, out[n + t].at[a], hop1_send, hop1_recv, 4 * t + 2 + a, to)
                piece = src[t].at[chip] if self.scatter else src[t].at[pl.ds(mine * rh, rh), :]
                return remote(piece, k.slot(t, me, mine), hop1_send, hop1_recv, 4 * t + a, to)

            def landed1(k, t, a, transit=False):
                here = out[n + t].at[a] if transit else k.slot(t, nb(a)[0], mine)
                return remote(here, here, hop1_send, hop1_recv, 4 * t + (2 if transit else 0) + a, sibling)

            def hop2(k, t, q):
                origin, to = nb(q)[0], nb(1 - q)[1]
                piece = out[n + t].at[q] if self.scatter else k.quarter(t, origin, q)
                return remote(piece, k.quarter(t, origin, q), hop2_send, hop2_recv, 2 * t + q, to)

            def landed2(k, t, q):
                here = k.quarter(t, dg, q)
                return remote(here, here, hop2_send, hop2_recv, 2 * t + q, sibling)

            def d2d(k, t, p, chip, own=False, arriving=False):
                if arriving:
                    there = k.slot(t, chip, theirs)
                    return remote(there, there, d2d_send, d2d_recv, 4 * t + p, sibling)
                piece = src[t].at[me] if own else k.slot(t, chip, mine)
                return remote(piece, k.slot(t, chip, mine), d2d_send, d2d_recv, 4 * t + p, sibling)

        return Copies(), nb, me, dg, c

    def start(self, src, out, sems):
        k, nb, me, dg, c = self._copies(src, out, sems)
        for t in range(self.n):
            for first in range(2):
                a = first ^ c
                k.hop1(t, a).start()
                if self.scatter:
                    k.hop1(t, a, transit=True).start()
            if self.scatter:
                k.d2d(t, 3, me, own=True).start()
            else:
                k.own_shard(t).start()

    def mid(self, src, out, sems):
        k, nb, me, dg, c = self._copies(src, out, sems)
        for t in range(self.n):
            for first in range(2):
                a = first ^ c
                if self.scatter:
                    k.landed1(t, a, transit=True).wait_recv()
                    k.hop2(t, a).start()
                k.landed1(t, a).wait_recv()
                if not self.scatter:
                    k.hop2(t, a).start()
                k.d2d(t, a, nb(a)[0]).start()

    def finish(self, src, out, sems):
        k, nb, me, dg, c = self._copies(src, out, sems)
        for t in range(self.n):
            for q in range(2):
                k.landed2(t, q).wait_recv()
            k.d2d(t, 2, dg).start()
        for t in range(self.n):
            for a in range(2):
                k.d2d(t, a, nb(a)[0], arriving=True).wait_recv()
            k.d2d(t, 2, dg, arriving=True).wait_recv()
            if self.scatter:
                k.d2d(t, 3, me, arriving=True).wait_recv()
        for t in range(self.n):
            for a in range(2):
                k.hop1(t, a).wait_send()
                if self.scatter:
                    k.hop1(t, a, transit=True).wait_send()
                k.hop2(t, a).wait_send()
                k.d2d(t, a, nb(a)[0]).wait_send()
            k.d2d(t, 2, dg).wait_send()
            if self.scatter:
                k.d2d(t, 3, me, own=True).wait_send()
            else:
                k.own_shard(t).wait()


class SiblingExchange:
    def __init__(self, grads):
        self.inputs, self.n, self.aliases, self.reach = list(grads), len(grads), {}, REACH_SIBLING
        self.half_rows = [g.shape[1] // 2 for g in grads]
        self.out_shape = [jax.ShapeDtypeStruct((g.shape[0], rh, g.shape[2]), g.dtype) for g, rh in zip(grads, self.half_rows)]
        self.sems = [pltpu.SemaphoreType.DMA((self.n,)), pltpu.SemaphoreType.DMA((self.n,))]

    def _plan(self, src, out, sems):
        x, y, c = _mesh_pos()
        return [pltpu.make_async_remote_copy(
            src_ref=src[t].at[:, pl.ds((1 - c) * self.half_rows[t], self.half_rows[t]), :], dst_ref=out[t],
            send_sem=sems[0].at[t], recv_sem=sems[1].at[t], device_id=(x, y, 1 - c), device_id_type=MESH) for t in range(self.n)]

    def start(self, src, out, sems):
        for cp in self._plan(src, out, sems):
            cp.start()

    def mid(self, src, out, sems):
        pass

    def finish(self, src, out, sems):
        for cp in self._plan(src, out, sems):
            cp.wait()


REACH_SIBLING, REACH_CHIPS, REACH_ALL = 0, 1, 2


def _entry_barrier(reach):
    x, y, c = _mesh_pos()
    peers = [(x, y, 1 - c)]
    if reach == REACH_CHIPS:
        peers += [(1 - x, y, c), (x, 1 - y, c)]
    elif reach == REACH_ALL:
        peers = [(x ^ dx, y ^ dy, c ^ dc) for dx in (0, 1) for dy in (0, 1) for dc in (0, 1)][1:]
    barrier = pltpu.get_barrier_semaphore()
    for peer in peers:
        pl.semaphore_signal(barrier, inc=1, device_id=peer, device_id_type=MESH)
    pl.semaphore_wait(barrier, len(peers))


def _call(body, hosted=(), *, name, in_specs, out_specs, out_shape, args, grid=(), scratch_shapes=(), aliased=None):
    n_in, n_out, n_scr = len(in_specs), len(out_specs), len(scratch_shapes)
    total = math.prod(grid)
    mid_step = max(0, (5 * total) // 8 - 1)

    def full(*refs):
        pos = [0]

        def take(k):
            pos[0] += k
            return refs[pos[0] - k:pos[0]]

        ins, h_in = take(n_in), [take(len(h.inputs)) for h in hosted]
        outs, h_out = take(n_out), [take(len(h.out_shape)) for h in hosted]
        scr, h_sem = take(n_scr), [take(len(h.sems)) for h in hosted]
        step = 0
        for axis, size in enumerate(grid):
            step = step * size + pl.program_id(axis)

        def phase(at, method):
            if not hosted:
                return

            def run():
                if method == "start":
                    _entry_barrier(reach)
                for h, s, o, m in zip(hosted, h_in, h_out, h_sem):
                    getattr(h, method)(s, o, m)

            if total == 1:
                run()
            else:
                pl.when(step == at)(run)

        phase(0, "start")
        body(*ins, *outs, *scr)
        phase(mid_step, "mid")
        phase(total - 1, "finish")

    aliases, i0, o0 = dict(aliased or {}), n_in, n_out
    for h in hosted:
        aliases.update({i0 + i: o0 + o for i, o in h.aliases.items()})
        i0, o0 = i0 + len(h.inputs), o0 + len(h.out_shape)
    reach = max((h.reach for h in hosted), default=None)
    params = dict(vmem_limit_bytes=VMEM_LIMIT_V7X)
    if hosted:
        params["collective_id"] = reach
    results = pl.pallas_call(
        full, name=name, grid=grid,
        in_specs=list(in_specs) + [ANY] * (i0 - n_in),
        out_specs=list(out_specs) + [ANY] * (o0 - n_out),
        out_shape=list(out_shape) + [s for h in hosted for s in h.out_shape],
        scratch_shapes=list(scratch_shapes) + [s for h in hosted for s in h.sems],
        input_output_aliases=aliases,
        compiler_params=pltpu.CompilerParams(**params),
    )(*args, *[s for h in hosted for s in h.inputs])
    outs, extras, pos = list(results[:n_out]), [], n_out
    for h in hosted:
        extras.append(list(results[pos:pos + h.n]))
        pos += len(h.out_shape)
    return outs, extras


def cast_shards(shards, name, hosted=()):
    n = len(shards)

    def body(*refs):
        for x_ref, o_ref in zip(refs[:n], refs[n:]):
            o_ref[...] = x_ref[...].astype(BF16)

    whole = lambda s: pl.BlockSpec(s.shape, lambda: (0,) * s.ndim)
    return _call(body, hosted, name=name, in_specs=[whole(s) for s in shards], out_specs=[whole(s) for s in shards],
                 out_shape=[jax.ShapeDtypeStruct(s.shape, BF16) for s in shards], args=list(shards))


class AllExchange:
    def __init__(self, pack):
        self.inputs, self.n, self.aliases, self.reach = [pack], 1, {}, REACH_ALL
        self.out_shape = [jax.ShapeDtypeStruct((2 * N_CHIPS,) + pack.shape, pack.dtype)]
        self.sems = [pltpu.SemaphoreType.DMA, pltpu.SemaphoreType.DMA((7,)), pltpu.SemaphoreType.DMA((7,))]

    def _copies(self, src, out, sems):
        local_sem, send_sem, recv_sem = sems
        x, y, c = _mesh_pos()
        flips = [(dx, dy, dc) for dx in (0, 1) for dy in (0, 1) for dc in (0, 1)][1:]
        peers = [(x ^ dx, y ^ dy, c ^ dc) for dx, dy, dc in flips]
        remote = lambda s, d, k: pltpu.make_async_remote_copy(
            src_ref=s, dst_ref=d, send_sem=send_sem.at[k], recv_sem=recv_sem.at[k], device_id=peers[k], device_id_type=MESH)
        sends = [remote(src[0], out[0].at[4 * x + 2 * y + c], k) for k in range(7)]
        landed = [remote(out[0].at[4 * px + 2 * py + pc], out[0].at[4 * px + 2 * py + pc], k) for k, (px, py, pc) in enumerate(peers)]
        return sends, landed, pltpu.make_async_copy(src[0], out[0].at[4 * x + 2 * y + c], local_sem)

    def start(self, src, out, sems):
        sends, _, local = self._copies(src, out, sems)
        for cp in sends:
            cp.start()
        local.start()

    def mid(self, src, out, sems):
        pass

    def finish(self, src, out, sems):
        sends, landed, local = self._copies(src, out, sems)
        for cp in landed:
            cp.wait_recv()
        for cp in sends:
            cp.wait_send()
        local.wait()


MXU_COLS = 256


def _resident(shape):
    return pl.BlockSpec(shape, lambda *_: (0,) * len(shape), pipeline_mode=pl.Buffered(1))


def ffn_up(h, gain, w1, w3, name, hosted=()):
    T, D = h.shape
    F = w1.shape[0]
    tm = min(T, 512)

    def body(h_ref, g_ref, w1_ref, w3_ref, n_ref, ga_ref, gb_ref, s_ref):
        hh = h_ref[...]
        n = (hh * _rstd(hh) * g_ref[...]).astype(BF16)
        n_ref[...] = n
        for c in range(0, F, MXU_COLS):
            cols = slice(c, c + MXU_COLS)
            a = _dot_nt(n, w1_ref[cols, :])
            b = _dot_nt(n, w3_ref[cols, :])
            silu, dsilu = _silu_parts(a)
            ga_ref[:, cols] = (b * dsilu).astype(BF16)
            gb_ref[:, cols] = silu.astype(BF16)
            s_ref[:, cols] = (silu * b).astype(BF16)

    act = jax.ShapeDtypeStruct((T, F), BF16)
    act_spec = pl.BlockSpec((tm, F), lambda i: (i, 0))
    row_spec = pl.BlockSpec((tm, D), lambda i: (i, 0))
    return _call(
        body, hosted, name=name, grid=(T // tm,),
        in_specs=[row_spec, pl.BlockSpec((1, D), lambda i: (0, 0)), _resident((F, D)), _resident((F, D))],
        out_specs=[row_spec, act_spec, act_spec, act_spec],
        out_shape=[jax.ShapeDtypeStruct((T, D), BF16), act, act, act],
        args=[h, gain, w1, w3])


def ffn_down(s, w2, h, name, hosted=()):
    T, F = s.shape
    D = h.shape[1]
    tm = min(T, 512)

    def body(s_ref, w2_ref, h_ref, o_ref):
        o_ref[...] = h_ref[...] + 0.5 * _dot(s_ref[...], w2_ref[...])

    row_spec = pl.BlockSpec((tm, D), lambda i: (i, 0))
    return _call(
        body, hosted, name=name, grid=(T // tm,),
        in_specs=[pl.BlockSpec((tm, F), lambda i: (i, 0)), pl.BlockSpec((F, D), lambda i: (0, 0)), row_spec],
        out_specs=[row_spec],
        out_shape=[jax.ShapeDtypeStruct((T, D), F32)],
        args=[s, w2, h])


def ffn_bwd_act(dh, w2, ga, gb, name, hosted=()):
    T, D = dh.shape
    F = w2.shape[0]
    tm = min(T, 512)

    def body(dh_ref, w2_ref, ga_ref, gb_ref, da_ref, db_ref, df_ref):
        df = (0.5 * dh_ref[...]).astype(BF16)
        df_ref[...] = df
        for c in range(0, F, MXU_COLS):
            cols = slice(c, c + MXU_COLS)
            ds = _dot_nt(df, w2_ref[cols, :])
            da_ref[:, cols] = (ds * ga_ref[:, cols].astype(F32)).astype(BF16)
            db_ref[:, cols] = (ds * gb_ref[:, cols].astype(F32)).astype(BF16)

    act = jax.ShapeDtypeStruct((T, F), BF16)
    act_spec = pl.BlockSpec((tm, F), lambda i: (i, 0))
    row_spec = pl.BlockSpec((tm, D), lambda i: (i, 0))
    return _call(
        body, hosted, name=name, grid=(T // tm,),
        in_specs=[row_spec, _resident((F, D)), act_spec, act_spec],
        out_specs=[act_spec, act_spec, row_spec],
        out_shape=[act, act, jax.ShapeDtypeStruct((T, D), BF16)],
        args=[dh, w2, ga, gb])


def ffn_dw(xs, y, halves, name, hosted=()):
    T, F = xs[0].shape
    D = y.shape[1]
    nx, fh = len(xs), F // halves
    tk = min(T, 512)
    nk = T // tk

    def body(*refs):
        y_ref, x_refs, o_refs, accs = refs[0], refs[1:1 + nx], refs[1 + nx:1 + 2 * nx], refs[1 + 2 * nx:]
        k = pl.program_id(1)

        @pl.when(k == 0)
        def _():
            for acc in accs:
                acc[...] = jnp.zeros_like(acc)

        yy = y_ref[...]
        for x_ref, acc in zip(x_refs, accs):
            acc[...] += _dot_tn(x_ref[...], yy)

        @pl.when(k == nk - 1)
        def _():
            for o_ref, acc in zip(o_refs, accs):
                o_ref[...] = acc[...].astype(BF16)

    out = jax.ShapeDtypeStruct((F, D), BF16)
    return _call(
        body, hosted, name=name, grid=(halves, nk),
        in_specs=[pl.BlockSpec((tk, D), lambda j, k: (k, 0))] + [pl.BlockSpec((tk, fh), lambda j, k: (k, j))] * nx,
        out_specs=[pl.BlockSpec((fh, D), lambda j, k: (j, 0))] * nx,
        out_shape=[out] * nx,
        scratch_shapes=[pltpu.VMEM((fh, D), F32)] * nx,
        args=[y] + list(xs))


def ffn_bwd_in(da, db, w1, w3, h, gain, dh, name, hosted=()):
    T, F = da.shape
    D = h.shape[1]
    tm = min(T, 512)

    def body(da_ref, db_ref, w1_ref, w3_ref, h_ref, g_ref, dh_ref, o_ref, dg_ref):
        dn = _dot(da_ref[...], w1_ref[...]) + _dot(db_ref[...], w3_ref[...])
        dhn, dg = _rmsnorm_bwd(dn, h_ref[...], g_ref[...])
        o_ref[...] = dh_ref[...] + dhn

        @pl.when(pl.program_id(0) == 0)
        def _():
            dg_ref[...] = jnp.zeros_like(dg_ref)

        dg_ref[...] += jnp.sum(dg, axis=0, keepdims=True)

    act_spec = pl.BlockSpec((tm, F), lambda i: (i, 0))
    row_spec = pl.BlockSpec((tm, D), lambda i: (i, 0))
    vec_spec = pl.BlockSpec((1, D), lambda i: (0, 0))
    return _call(
        body, hosted, name=name, grid=(T // tm,),
        in_specs=[act_spec, act_spec, _resident((F, D)), _resident((F, D)), row_spec, vec_spec, row_spec],
        out_specs=[row_spec, vec_spec],
        out_shape=[jax.ShapeDtypeStruct((T, D), F32), jax.ShapeDtypeStruct((1, D), F32)],
        args=[da, db, w1, w3, h, gain, dh])


def mix_in(h, gain, wing, name, hosted=()):
    T, D = h.shape
    nsh, _, Cs = wing.shape
    tm = min(T, 512)

    def body(h_ref, g_ref, w_ref, u_ref, p_ref):
        hh = h_ref[...]
        u = (hh * _rstd(hh) * g_ref[...]).astype(BF16)
        u_ref[...] = u
        for j in range(nsh):
            p_ref[:, j * Cs:(j + 1) * Cs] = _dot(u, w_ref[j])

    return _call(
        body, hosted, name=name, grid=(T // tm,),
        in_specs=[pl.BlockSpec((tm, D), lambda i: (i, 0)), pl.BlockSpec((1, D), lambda i: (0, 0)),
                  pl.BlockSpec((nsh, D, Cs), lambda i: (0, 0, 0))],
        out_specs=[pl.BlockSpec((tm, D), lambda i: (i, 0)), pl.BlockSpec((tm, nsh * Cs), lambda i: (i, 0))],
        out_shape=[jax.ShapeDtypeStruct((T, D), BF16), jax.ShapeDtypeStruct((T, nsh * Cs), F32)],
        args=[h, gain, wing])


def mix_out(a, b, woutg, h, name, hosted=()):
    T, W = a.shape
    D = h.shape[1]
    wout = woutg.reshape(2, W, D)
    tm = min(T, 512)

    def body(a_ref, b_ref, w_ref, h_ref, o_ref):
        o_ref[...] = h_ref[...] + _dot(a_ref[...], w_ref[0]) + _dot(b_ref[...], w_ref[1])

    return _call(
        body, hosted, name=name, grid=(T // tm,),
        in_specs=[pl.BlockSpec((tm, W), lambda i: (i, 0)), pl.BlockSpec((tm, W), lambda i: (i, 0)),
                  pl.BlockSpec((2, W, D), lambda i: (0, 0, 0)), pl.BlockSpec((tm, D), lambda i: (i, 0))],
        out_specs=[pl.BlockSpec((tm, D), lambda i: (i, 0))],
        out_shape=[jax.ShapeDtypeStruct((T, D), F32)],
        args=[a, b, wout, h])


def mix_out_bwd(dh, woutg, a, b, name, hosted=()):
    T, D = dh.shape
    W = a.shape[1]
    nsh, Rs, _ = woutg.shape
    wout = woutg.reshape(2, W, D)
    tk = min(T, 512)
    nk = T // tk

    def body(dh_ref, w_ref, a_ref, b_ref, da_ref, db_ref, dw_ref, acc):
        k = pl.program_id(0)

        @pl.when(k == 0)
        def _():
            acc[...] = jnp.zeros_like(acc)

        dhb = dh_ref[...].astype(BF16)
        da_ref[...] = _dot_nt(dhb, w_ref[0])
        db_ref[...] = _dot_nt(dhb, w_ref[1])
        acc[0:W, :] += _dot_tn(a_ref[...], dhb)
        acc[W:2 * W, :] += _dot_tn(b_ref[...], dhb)

        @pl.when(k == nk - 1)
        def _():
            for j in range(nsh):
                dw_ref[j] = acc[j * Rs:(j + 1) * Rs, :].astype(BF16)

    return _call(
        body, hosted, name=name, grid=(nk,),
        in_specs=[pl.BlockSpec((tk, D), lambda k: (k, 0)), pl.BlockSpec((2, W, D), lambda k: (0, 0, 0)),
                  pl.BlockSpec((tk, W), lambda k: (k, 0)), pl.BlockSpec((tk, W), lambda k: (k, 0))],
        out_specs=[pl.BlockSpec((tk, W), lambda k: (k, 0)), pl.BlockSpec((tk, W), lambda k: (k, 0)),
                   pl.BlockSpec((nsh, Rs, D), lambda k: (0, 0, 0))],
        out_shape=[jax.ShapeDtypeStruct((T, W), F32), jax.ShapeDtypeStruct((T, W), F32),
                   jax.ShapeDtypeStruct((nsh, Rs, D), BF16)],
        scratch_shapes=[pltpu.VMEM((2 * W, D), F32)],
        args=[dh, wout, a, b])


def _dproj_block(g):
    return (g // N_GROUPS + N_GROUPS) % (N_GROUPS + 1), g % N_GROUPS


def mix_dwin(u, dproj, nsh, name, hosted=()):
    T, D = u.shape
    Hd = HEAD_DIM
    slabs, _, width = dproj.shape
    blocks = slabs * width // Hd
    Cs = blocks * Hd // nsh
    tk = min(T, 512)
    nk = T // tk

    def body(u_ref, d_ref, o_ref, acc):
        k = pl.program_id(0)

        @pl.when(k == 0)
        def _():
            acc[...] = jnp.zeros_like(acc)

        where = [_dproj_block(g) for g in range(blocks)]
        d = jnp.concatenate([d_ref[slab, :, col * Hd:(col + 1) * Hd] for slab, col in where], axis=1)
        acc[...] += _dot_tn(u_ref[...], d)

        @pl.when(k == nk - 1)
        def _():
            for j in range(nsh):
                o_ref[j] = acc[:, j * Cs:(j + 1) * Cs].astype(BF16)

    return _call(
        body, hosted, name=name, grid=(nk,),
        in_specs=[pl.BlockSpec((tk, D), lambda k: (k, 0)), pl.BlockSpec((slabs, tk, width), lambda k: (0, k, 0))],
        out_specs=[pl.BlockSpec((nsh, D, Cs), lambda k: (0, 0, 0))],
        out_shape=[jax.ShapeDtypeStruct((nsh, D, Cs), BF16)],
        scratch_shapes=[pltpu.VMEM((D, blocks * Hd), F32)],
        args=[u, dproj])


def mix_in_bwd(dproj, wing, h, gain, dh, name, hosted=()):
    T, D = h.shape
    nsh, _, Cs = wing.shape
    Hd = HEAD_DIM
    per = Cs // Hd
    tm = min(T, 512)

    def body(d_ref, w_ref, h_ref, g_ref, dh_ref, o_ref, dg_ref):
        def shard(j):
            blocks = [_dproj_block(per * j + i) for i in range(per)]
            return jnp.concatenate([d_ref[slab, :, col * Hd:(col + 1) * Hd] for slab, col in blocks], axis=1)

        du = _dot_nt(shard(0), w_ref[0])
        for j in range(1, nsh):
            du += _dot_nt(shard(j), w_ref[j])
        dhn, dg = _rmsnorm_bwd(du, h_ref[...], g_ref[...])
        o_ref[...] = dh_ref[...] + dhn

        @pl.when(pl.program_id(0) == 0)
        def _():
            dg_ref[...] = jnp.zeros_like(dg_ref)

        dg_ref[...] += jnp.sum(dg, axis=0, keepdims=True)

    row_spec = pl.BlockSpec((tm, D), lambda i: (i, 0))
    vec_spec = pl.BlockSpec((1, D), lambda i: (0, 0))
    return _call(
        body, hosted, name=name, grid=(T // tm,),
        in_specs=[pl.BlockSpec((dproj.shape[0], tm, dproj.shape[2]), lambda i: (0, i, 0)),
                  pl.BlockSpec((nsh, D, Cs), lambda i: (0, 0, 0)), row_spec, vec_spec, row_spec],
        out_specs=[row_spec, vec_spec],
        out_shape=[jax.ShapeDtypeStruct((T, D), F32), jax.ShapeDtypeStruct((1, D), F32)],
        args=[dproj, wing, h, gain, dh])


def _pool_window(x, group, T, trailing):
    rows = lax.broadcasted_iota(jnp.int32, x.shape, 0)

    def shifted(z, k):
        if trailing:
            return jnp.where(rows >= k, pltpu.roll(z, k, 0), 0.0)
        return jnp.where(rows < T - k, pltpu.roll(z, T - k, 0), 0.0)

    s2 = x + shifted(x, 1)
    s4 = s2 + shifted(s2, 2)
    s8 = s4 + shifted(s4, 4)
    s16 = s8 + shifted(s8, 8)
    return jnp.where(group == 0, s2, jnp.where(group == 1, s4, jnp.where(group == 2, s8, s16)))


def _pool_count(group, shape):
    rows = lax.broadcasted_iota(jnp.int32, shape, 0)
    w = jnp.where(group == 0, 2, jnp.where(group == 1, 4, jnp.where(group == 2, 8, 16)))
    return jnp.minimum(rows + 1, w).astype(F32)


def pool_fwd(proj, pool_w, pool_scale, name, hosted=()):
    T = proj.shape[0]
    Hd = HEAD_DIM

    def body(x_ref, w_ref, sc_ref, a_ref):
        g = pl.program_id(0)
        x = x_ref[...]
        pooled = _pool_window(x, g, T, True) / _pool_count(g, x.shape) - x
        a_ref[...] = (_dot(pooled.astype(BF16), w_ref[0].astype(BF16)) * sc_ref[...]).astype(BF16)

    return _call(
        body, hosted, name=name, grid=(N_GROUPS,),
        in_specs=[pl.BlockSpec((T, Hd), lambda g: (0, g)), pl.BlockSpec((1, Hd, Hd), lambda g: (g, 0, 0)),
                  pl.BlockSpec((1, Hd), lambda g: (0, g))],
        out_specs=[pl.BlockSpec((T, Hd), lambda g: (0, g))],
        out_shape=[jax.ShapeDtypeStruct((T, N_GROUPS * Hd), BF16)],
        args=[proj, pool_w, pool_scale])


def pool_bwd(proj, da, pool_w, pool_scale, name, hosted=()):
    T = proj.shape[0]
    Hd = HEAD_DIM

    def body(x_ref, da_ref, w_ref, sc_ref, dx_ref, dw_ref, dsc_ref):
        g = pl.program_id(0)
        x = x_ref[...]
        cnt = _pool_count(g, x.shape)
        pooled = (_pool_window(x, g, T, True) / cnt - x).astype(BF16)
        wb = w_ref[0].astype(BF16)
        dav = da_ref[...]
        dsc_ref[...] = jnp.sum(dav * _dot(pooled, wb), axis=0, keepdims=True)
        dout = (dav * sc_ref[...]).astype(BF16)
        dw_ref[0] = _dot_tn(pooled, dout)
        dpooled = _dot_nt(dout, wb)
        dx_ref[0] = (_pool_window(dpooled / cnt, g, T, False) - dpooled).astype(BF16)

    col_spec = pl.BlockSpec((T, Hd), lambda g: (0, g))
    return _call(
        body, hosted, name=name, grid=(N_GROUPS,),
        in_specs=[col_spec, col_spec, pl.BlockSpec((1, Hd, Hd), lambda g: (g, 0, 0)), pl.BlockSpec((1, Hd), lambda g: (0, g))],
        out_specs=[pl.BlockSpec((1, T, Hd), lambda g: (N_GROUPS, 0, g)), pl.BlockSpec((1, Hd, Hd), lambda g: (g, 0, 0)),
                   pl.BlockSpec((1, Hd), lambda g: (0, g))],
        out_shape=[jax.ShapeDtypeStruct((N_GROUPS + 1, T, N_GROUPS * Hd), BF16), jax.ShapeDtypeStruct((N_GROUPS, Hd, Hd), F32),
                   jax.ShapeDtypeStruct((1, N_GROUPS * Hd), F32)],
        args=[proj, da, pool_w, pool_scale])


def _ret_tables(T):
    Hd, C = HEAD_DIM, RET_CHUNK
    inv_freq = 1.0 / (ROPE_BASE ** (jnp.arange(0, Hd, 2, dtype=F32) / Hd))
    ang = jnp.arange(T, dtype=F32)[:, None] * inv_freq[None, :]
    cos, sin = jnp.cos(ang), jnp.sin(ang)
    cos2 = jnp.concatenate([cos, cos], axis=-1)
    sin2 = jnp.concatenate([-sin, sin], axis=-1)
    log_gamma = jnp.log1p(-jnp.exp2(-5.0 - jnp.arange(N_GROUPS, dtype=F32)))
    pos = jnp.arange(C, dtype=F32)
    rel = pos[:, None] - pos[None, :]
    intra = jnp.where(rel[None] >= 0, jnp.exp(log_gamma[:, None, None] * jnp.maximum(rel, 0.0)[None]), 0.0)
    k_tail = jnp.exp(log_gamma[:, None] * (C - 1 - pos)[None, :])
    q_head = jnp.exp(log_gamma[:, None] * (pos + 1.0)[None, :])
    chunk_decay = jnp.exp(log_gamma * C)
    wide = lambda t: jnp.broadcast_to(t[:, :, None], (N_GROUPS, C, Hd))
    return cos2, sin2, intra, wide(k_tail), wide(q_head), jnp.broadcast_to(chunk_decay[:, None, None], (N_GROUPS, 1, Hd))


def _rope(x, cos2, sin2):
    return x * cos2 + pltpu.roll(x, HEAD_DIM // 2, 1) * sin2


def _rope_t(d, cos2, sin2):
    return d * cos2 + pltpu.roll(d * sin2, HEAD_DIM // 2, 1)


def _ret_specs(tseg, seg_of):
    Hd, G = HEAD_DIM, N_GROUPS
    col = lambda kind: pl.BlockSpec((tseg, Hd), lambda h, s: (seg_of(s), G * kind + h))
    tab = pl.BlockSpec((tseg, Hd), lambda h, s: (seg_of(s), 0))
    head = pl.BlockSpec((1, RET_CHUNK, Hd), lambda h, s: (h, 0, 0))
    cd = pl.BlockSpec((1, 1, Hd), lambda h, s: (h, 0, 0))
    gain = pl.BlockSpec((1, Hd), lambda h, s: (0, h))
    return col, tab, head, cd, gain


def ret_fwd(proj, ret_norm, tables, name, hosted=()):
    T = proj.shape[0]
    Hd, C, G = HEAD_DIM, RET_CHUNK, N_GROUPS
    tseg = min(T, 1024)
    nseg, nck = T // tseg, tseg // C
    scale = Hd ** -0.5
    cos2, sin2, intra, k_tail, q_head, chunk_decay = tables

    def body(q_ref, k_ref, v_ref, g_ref, gain_ref, cos_ref, sin_ref, m_ref, kt_ref, qh_ref, cd_ref,
             b_ref, o_ref, rp_ref, state):
        @pl.when(pl.program_id(1) == 0)
        def _():
            state[...] = jnp.zeros_like(state)

        def chunk(ci, carry):
            rows = pl.ds(pl.multiple_of(ci * C, C), C)
            cos, sin = cos_ref[rows, :], sin_ref[rows, :]
            qr = _rope(q_ref[rows, :], cos, sin)
            kr = _rope(k_ref[rows, :], cos, sin) * scale
            qb, kb, vb = qr.astype(BF16), kr.astype(BF16), v_ref[rows, :].astype(BF16)
            r = state[...]
            rp_ref[0, ci] = r.astype(BF16)
            sc = _dot_nt(qb, kb) * m_ref[0]
            o = _dot(sc.astype(BF16), vb) + _dot((qr * qh_ref[0]).astype(BF16), r.astype(BF16))
            state[...] = cd_ref[0] * r + _dot_tn((kr * kt_ref[0]).astype(BF16), vb)
            o_ref[rows, :] = o
            on = o * _rstd(o)
            b_ref[rows, :] = (jax.nn.silu(g_ref[rows, :]) * (on * gain_ref[...])).astype(BF16)
            return carry

        lax.fori_loop(0, nck, chunk, 0, unroll=True)

    col, tab, head, cd, gain = _ret_specs(tseg, lambda s: s)
    out_col = pl.BlockSpec((tseg, Hd), lambda h, s: (s, h))
    return _call(
        body, hosted, name=name, grid=(G, nseg),
        in_specs=[col(1), col(2), col(3), col(4), gain, tab, tab, head, head, head, cd],
        out_specs=[out_col, out_col, pl.BlockSpec((1, nck, Hd, Hd), lambda h, s: (h, s, 0, 0))],
        out_shape=[jax.ShapeDtypeStruct((T, G * Hd), BF16), jax.ShapeDtypeStruct((T, G * Hd), F32),
                   jax.ShapeDtypeStruct((G, T // C, Hd, Hd), BF16)],
        scratch_shapes=[pltpu.VMEM((Hd, Hd), F32)],
        args=[proj, proj, proj, proj, ret_norm, cos2, sin2, intra, k_tail, q_head, chunk_decay])


def ret_bwd(proj, db, o_pre, r_prev, ret_norm, tables, dproj, name, hosted=()):
    T = proj.shape[0]
    Hd, C, G = HEAD_DIM, RET_CHUNK, N_GROUPS
    tseg = min(T, 1024)
    nseg, nck = T // tseg, tseg // C
    scale = Hd ** -0.5
    cos2, sin2, intra, k_tail, q_head, chunk_decay = tables

    def body(q_ref, k_ref, v_ref, g_ref, db_ref, o_ref, rp_ref, gain_ref, cos_ref, sin_ref, m_ref, kt_ref, qh_ref, cd_ref,
             _, d_ref, dgain_ref, gstate):
        @pl.when(pl.program_id(1) == 0)
        def _():
            gstate[...] = jnp.zeros_like(gstate)
            dgain_ref[...] = jnp.zeros_like(dgain_ref)

        def chunk(t, carry):
            ci = nck - 1 - t
            rows = pl.ds(pl.multiple_of(ci * C, C), C)
            cos, sin = cos_ref[rows, :], sin_ref[rows, :]
            qr = _rope(q_ref[rows, :], cos, sin)
            kr = _rope(k_ref[rows, :], cos, sin) * scale
            qb, kb, vb = qr.astype(BF16), kr.astype(BF16), v_ref[rows, :].astype(BF16)
            qhb, ktb = (qr * qh_ref[0]).astype(BF16), (kr * kt_ref[0]).astype(BF16)
            sc = (_dot_nt(qb, kb) * m_ref[0]).astype(BF16)
            o = o_ref[rows, :]
            rstd = _rstd(o)
            on = o * rstd
            gain = gain_ref[...]
            silu, dsilu = _silu_parts(g_ref[rows, :])
            dy = db_ref[rows, :]
            dgain_ref[...] += jnp.sum(dy * silu * on, axis=0, keepdims=True)
            dg = dy * on * gain * dsilu
            don = dy * silu * gain
            dob = (rstd * (don - on * jnp.mean(don * on, axis=-1, keepdims=True))).astype(BF16)
            gn = gstate[...]
            gb = gn.astype(BF16)
            da = (_dot_nt(dob, vb) * m_ref[0]).astype(BF16)
            dq = _dot(da, kb) + _dot_nt(dob, rp_ref[0, ci]) * qh_ref[0]
            dk = _dot_tn(da, qb) + _dot_nt(vb, gb) * kt_ref[0]
            dv = _dot_tn(sc, dob) + _dot(ktb, gb)
            gstate[...] = cd_ref[0] * gn + _dot_tn(qhb, dob)
            d_ref[0, rows, :] = _rope_t(dq, cos, sin).astype(BF16)
            d_ref[1, rows, :] = _rope_t(dk * scale, cos, sin).astype(BF16)
            d_ref[2, rows, :] = dv.astype(BF16)
            d_ref[3, rows, :] = dg.astype(BF16)
            return carry

        lax.fori_loop(0, nck, chunk, 0, unroll=True)

    rev = lambda s: nseg - 1 - s
    col, tab, head, cd, gain = _ret_specs(tseg, rev)
    act = pl.BlockSpec((tseg, Hd), lambda h, s: (rev(s), h))
    return _call(
        body, hosted, name=name, grid=(G, nseg),
        in_specs=[col(1), col(2), col(3), col(4), act, act, pl.BlockSpec((1, nck, Hd, Hd), lambda h, s: (h, rev(s), 0, 0)),
                  gain, tab, tab, head, head, head, cd, ANY],
        out_specs=[pl.BlockSpec((4, tseg, Hd), lambda h, s: (0, rev(s), h)), gain],
        out_shape=[jax.ShapeDtypeStruct(dproj.shape, BF16), jax.ShapeDtypeStruct((1, G * Hd), F32)],
        scratch_shapes=[pltpu.VMEM((Hd, Hd), F32)], aliased={14: 0},
        args=[proj, proj, proj, proj, db, o_pre, r_prev, ret_norm, cos2, sin2, intra, k_tail, q_head, chunk_decay, dproj])


def final_loss(h, gain, target, name, hosted=()):
    T, D = h.shape
    tm = min(T, 512)

    def body(h_ref, g_ref, t_ref, dh_ref, loss_ref, dg_ref):
        @pl.when(pl.program_id(0) == 0)
        def _():
            loss_ref[...] = jnp.zeros_like(loss_ref)
            dg_ref[...] = jnp.zeros_like(dg_ref)

        hh = h_ref[...]
        gain_v = g_ref[...]
        err = hh * _rstd(hh) * gain_v - t_ref[...]
        loss_ref[...] += 0.5 * jnp.sum(jnp.mean(err * err, axis=-1, keepdims=True), axis=0, keepdims=True)
        dhn, dg = _rmsnorm_bwd(err * (1.0 / D), hh, gain_v)
        dh_ref[...] = dhn
        dg_ref[...] += jnp.sum(dg, axis=0, keepdims=True)

    row_spec = pl.BlockSpec((tm, D), lambda i: (i, 0))
    vec_spec = pl.BlockSpec((1, D), lambda i: (0, 0))
    return _call(
        body, hosted, name=name, grid=(T // tm,),
        in_specs=[row_spec, vec_spec, row_spec],
        out_specs=[row_spec, pl.BlockSpec((1, 128), lambda i: (0, 0)), vec_spec],
        out_shape=[jax.ShapeDtypeStruct((T, D), F32), jax.ShapeDtypeStruct((1, 128), F32), jax.ShapeDtypeStruct((1, D), F32)],
        args=[h, gain, target])


def prereduce(grads, recvs, place, name):
    nt = len(grads)
    nsh, R, C = grads[0].shape
    rh = R // 2

    def body(place_ref, *refs):
        for t in range(nt):
            g_ref, r_ref, o_ref, own_ref = refs[2 * t], refs[2 * t + 1], refs[2 * nt + 2 * t], refs[2 * nt + 2 * t + 1]
            piece = (g_ref[...].astype(F32) + r_ref[...].astype(F32)).astype(BF16)
            o_ref[...] = piece

            @pl.when(pl.program_id(0) == place_ref[1])
            def _():
                own_ref[...] = piece

    outs = pl.pallas_call(
        body, name=name,
        grid_spec=pltpu.PrefetchScalarGridSpec(
            num_scalar_prefetch=1, grid=(nsh,),
            in_specs=[pl.BlockSpec((1, rh, C), lambda j, p: (j, p[0], 0)), pl.BlockSpec((1, rh, C), lambda j, p: (j, 0, 0))] * nt,
            out_specs=[pl.BlockSpec((1, rh, C), lambda j, p: (j, 0, 0)),
                       pl.BlockSpec((1, rh, C), lambda j, p: (p[1], p[0], 0))] * nt),
        out_shape=[jax.ShapeDtypeStruct((nsh, rh, C), BF16), jax.ShapeDtypeStruct((nsh, R, C), BF16)] * nt,
        compiler_params=pltpu.CompilerParams(vmem_limit_bytes=VMEM_LIMIT_V7X),
    )(place, *[a for pair in zip(grads, recvs) for a in pair])
    return [(outs[2 * t], outs[2 * t + 1]) for t in range(nt)]


def _adamw(w, g, m, v):
    m = ADAM_B1 * m + (1.0 - ADAM_B1) * g
    v = ADAM_B2 * v + (1.0 - ADAM_B2) * (g * g)
    m_hat = m / (1.0 - ADAM_B1 ** ADAM_STEP)
    v_hat = v / (1.0 - ADAM_B2 ** ADAM_STEP)
    return -ADAM_LR * (m_hat / (jnp.sqrt(v_hat) + ADAM_EPS) + ADAM_WD * w), m, v


def adamw_sharded(tensors, name, hosted=()):
    nt = len(tensors)
    nsh, R, C = tensors[0][0].shape
    lanes = -(-C // 128) * 128
    per_row = 2 * nt * lanes * (nsh * 2 + 7 * 4)
    tr = max(r for r in range(16, R + 1, 16) if R % r == 0 and r * per_row <= ADAMW_VMEM_BUDGET)

    def body(*refs):
        ins, outs = refs[:4 * nt], refs[4 * nt:]
        for t in range(nt):
            p_ref, w_ref, m_ref, v_ref = ins[4 * t:4 * t + 4]
            g_ref, d_ref, nm_ref, nv_ref = outs[4 * t:4 * t + 4]
            g = p_ref[0].astype(F32)
            for i in range(1, nsh):
                g += p_ref[i].astype(F32)
            g_ref[...] = g
            d_ref[...], nm_ref[...], nv_ref[...] = _adamw(w_ref[...], g, m_ref[...], v_ref[...])

    spec = pl.BlockSpec((tr, C), lambda i: (i, 0))
    out = jax.ShapeDtypeStruct((R, C), F32)
    return _call(
        body, hosted, name=name, grid=(R // tr,),
        in_specs=[pl.BlockSpec((nsh, tr, C), lambda i: (0, i, 0)), spec, spec, spec] * nt,
        out_specs=[spec] * (4 * nt), out_shape=[out] * (4 * nt),
        args=[a for tensor in tensors for a in tensor])


def adamw_small(packs, late, pool, vectors, name):
    ndev = packs.shape[0]
    rp, rv, rl = pool[0].shape[0], vectors.shape[0] // 3, late.shape[1]

    def body(p_ref, l_ref, wp, mp, vp, wmv, gp, dp, nmp, nvp, gv, dv, nmv, nvv, loss_ref):
        wv, mv, vv = wmv.at[0:rv], wmv.at[rv:2 * rv], wmv.at[2 * rv:3 * rv]
        g, first = p_ref[0], l_ref[0]
        for i in range(1, ndev):
            g += p_ref[i]
            first += l_ref[i]
        g_pool = g[0:rp]
        g_vec = jnp.concatenate([g[rp:rp + rl] + first, g[rp + rl:rp + rv]], axis=0)
        gp[...], gv[...], loss_ref[...] = g_pool, g_vec, g[rp + rv:rp + rv + 8]
        dp[...], nmp[...], nvp[...] = _adamw(wp[...], g_pool, mp[...], vp[...])
        dv[...], nmv[...], nvv[...] = _adamw(wv[...], g_vec, mv[...], vv[...])

    shape = lambda rows: jax.ShapeDtypeStruct((rows, 128), F32)
    outs = pl.pallas_call(body, name=name, out_shape=[shape(rp)] * 4 + [shape(rv)] * 4 + [shape(8)],
                          compiler_params=pltpu.CompilerParams(vmem_limit_bytes=VMEM_LIMIT_V7X))(packs, late, *pool, vectors)
    return outs[0:4], outs[4:8], outs[8]


BIG = ("ffn1_w1", "ffn1_w3", "ffn1_w2", "w_in", "w_out", "ffn2_w1", "ffn2_w3", "ffn2_w2")
TRANSPOSED = ("ffn1_w1", "ffn1_w3", "ffn2_w1", "ffn2_w3")
VECTORS = ("ffn1_norm", "mix_norm", "pool_scale", "ret_norm", "ffn2_norm", "final_norm")
WEIGHTS = ("ffn1_norm", "ffn1_w1", "ffn1_w3", "ffn1_w2", "mix_norm", "w_in", "pool_w", "pool_scale", "ret_norm", "w_out",
           "ffn2_norm", "ffn2_w1", "ffn2_w3", "ffn2_w2", "final_norm")


def _pack_vectors(parts):
    return jnp.concatenate([parts[k].reshape(-1, 128) for k in VECTORS], axis=0)


def _unpack_vectors(pack, like):
    out, row = {}, 0
    for k in VECTORS:
        rows = like[k].size // 128
        out[k] = pack[row:row + rows].reshape(like[k].shape)
        row += rows
    return out


def kernel(x, ffn1_norm, ffn1_w1, ffn1_w3, ffn1_w2, mix_norm, w_in, pool_w, pool_scale, ret_norm, w_out, ffn2_norm, ffn2_w1, ffn2_w3, ffn2_w2, final_norm, loss_target, m_ffn1_norm, m_ffn1_w1, m_ffn1_w3, m_ffn1_w2, m_mix_norm, m_w_in, m_pool_w, m_pool_scale, m_ret_norm, m_w_out, m_ffn2_norm, m_ffn2_w1, m_ffn2_w3, m_ffn2_w2, m_final_norm, v_ffn1_norm, v_ffn1_w1, v_ffn1_w3, v_ffn1_w2, v_mix_norm, v_w_in, v_pool_w, v_pool_scale, v_ret_norm, v_w_out, v_ffn2_norm, v_ffn2_w1, v_ffn2_w3, v_ffn2_w2, v_final_norm):
    w = dict(ffn1_norm=ffn1_norm, ffn1_w1=ffn1_w1, ffn1_w3=ffn1_w3, ffn1_w2=ffn1_w2, mix_norm=mix_norm, w_in=w_in, pool_w=pool_w,
             pool_scale=pool_scale, ret_norm=ret_norm, w_out=w_out, ffn2_norm=ffn2_norm, ffn2_w1=ffn2_w1, ffn2_w3=ffn2_w3,
             ffn2_w2=ffn2_w2, final_norm=final_norm)
    m = dict(ffn1_norm=m_ffn1_norm, ffn1_w1=m_ffn1_w1, ffn1_w3=m_ffn1_w3, ffn1_w2=m_ffn1_w2, mix_norm=m_mix_norm, w_in=m_w_in,
             pool_w=m_pool_w, pool_scale=m_pool_scale, ret_norm=m_ret_norm, w_out=m_w_out, ffn2_norm=m_ffn2_norm, ffn2_w1=m_ffn2_w1,
             ffn2_w3=m_ffn2_w3, ffn2_w2=m_ffn2_w2, final_norm=m_final_norm)
    v = dict(ffn1_norm=v_ffn1_norm, ffn1_w1=v_ffn1_w1, ffn1_w3=v_ffn1_w3, ffn1_w2=v_ffn1_w2, mix_norm=v_mix_norm, w_in=v_w_in,
             pool_w=v_pool_w, pool_scale=v_pool_scale, ret_norm=v_ret_norm, w_out=v_w_out, ffn2_norm=v_ffn2_norm, ffn2_w1=v_ffn2_w1,
             ffn2_w3=v_ffn2_w3, ffn2_w2=v_ffn2_w2, final_norm=v_final_norm)
    xs, target = x[0], loss_target[0]
    T = xs.shape[0]
    tables = _ret_tables(T)
    place = jnp.stack([lax.axis_index("c"), 2 * lax.axis_index("x") + lax.axis_index("y")]).astype(jnp.int32)
    local = lambda d, k: jnp.transpose(d[k][0]) if k in TRANSPOSED else d[k][0]
    result = lambda o, k: jnp.transpose(o)[None] if k in TRANSPOSED else o[None]
    first = ("ffn1_w1", "ffn1_w3")
    sh = {k: local(w, k).astype(BF16) for k in first}
    gather = lambda *names: [ChipExchange([sh[k] for k in names], False)]
    wg, grad, delta, new_m, new_v = {}, {}, {}, {}, {}

    def update(names, pieces, name, hosted=()):
        outs, extras = adamw_sharded([(p, local(w, k), local(m, k), local(v, k)) for k, p in zip(names, pieces)], name, hosted)
        for t, k in enumerate(names):
            grad[k], delta[k], new_m[k], new_v[k] = [result(o, k) for o in outs[4 * t:4 * t + 4]]
        return extras

    def reduce_in_chip(name, *pairs):
        reduced = prereduce([p for p, _ in pairs], [r for _, r in pairs], place, "prereduce_" + name)
        return reduced[0] if len(pairs) == 1 else reduced

    scatter = lambda *reduced: ChipExchange([r[0] for r in reduced], True, [r[1] for r in reduced])
    whole = lambda k: wg[k].reshape(-1, wg[k].shape[-1])
    sharded = lambda g: g.reshape(N_CHIPS, -1, g.shape[-1])

    later = [k for k in BIG if k not in first]
    casts, ((wg["ffn1_w1"], wg["ffn1_w3"]),) = cast_shards([local(w, k) for k in later], "cast_gather_ffn1", gather(*first))
    sh.update(zip(later, casts))
    (n1, ga1, gb1, s1), ((wg["ffn1_w2"], wg["w_in"]),) = ffn_up(
        xs, ffn1_norm, whole("ffn1_w1"), whole("ffn1_w3"), "ffn1_up", gather("ffn1_w2", "w_in"))
    (h1,), ((wg["w_out"],),) = ffn_down(s1, whole("ffn1_w2"), xs, "ffn1_down", gather("w_out"))
    (u, proj), ((wg["ffn2_w1"],),) = mix_in(h1, mix_norm, wg["w_in"], "mix_in", gather("ffn2_w1"))
    (pa,), _ = pool_fwd(proj, pool_w[0], pool_scale, "pool_fwd")
    (rb, o_pre, r_prev), ((wg["ffn2_w3"],),) = ret_fwd(proj, ret_norm, tables, "ret_fwd", gather("ffn2_w3"))
    (h2,), _ = mix_out(pa, rb, wg["w_out"], h1, "mix_out")
    (n2, ga2, gb2, s2), ((wg["ffn2_w2"],),) = ffn_up(
        h2, ffn2_norm, whole("ffn2_w1"), whole("ffn2_w3"), "ffn2_up", gather("ffn2_w2"))
    (h3,), _ = ffn_down(s2, whole("ffn2_w2"), h2, "ffn2_down")
    (dh3, loss, d_final), _ = final_loss(h3, final_norm[None], target, "final_loss")

    (da2, db2, df2), _ = ffn_bwd_act(dh3, whole("ffn2_w2"), ga2, gb2, "ffn2_bwd_act")
    (g_f2w2,), _ = ffn_dw([s2], df2, 1, "ffn2_dw2")
    g_f2w2 = sharded(g_f2w2)
    (g_f2w1, g_f2w3), ((r_f2w2,),) = ffn_dw([da2, db2], n2, 2, "ffn2_dw13", [SiblingExchange([g_f2w2])])
    g_f2w1, g_f2w3 = sharded(g_f2w1), sharded(g_f2w3)
    p_f2w2 = reduce_in_chip("ffn2_w2", (g_f2w2, r_f2w2))
    (dh2, d_ffn2), ((q_f2w2,), (r_f2w1, r_f2w3)) = ffn_bwd_in(
        da2, db2, whole("ffn2_w1"), whole("ffn2_w3"), h2, ffn2_norm, dh3, "ffn2_bwd_in",
        [scatter(p_f2w2), SiblingExchange([g_f2w1, g_f2w3])])
    p_f2w1, p_f2w3 = reduce_in_chip("ffn2_w13", (g_f2w1, r_f2w1), (g_f2w3, r_f2w3))
    (dpa, drb, g_wout), _ = mix_out_bwd(dh2, wg["w_out"], pa, rb, "mix_out_bwd")
    (dproj, d_pool_w, d_pool_scale), _ = pool_bwd(proj, dpa, pool_w[0], pool_scale, "pool_bwd")
    (dproj, d_ret_norm), ((q_f2w1, q_f2w3), (r_wout,)) = ret_bwd(
        proj, drb, o_pre, r_prev, ret_norm, tables, dproj, "ret_bwd", [scatter(p_f2w1, p_f2w3), SiblingExchange([g_wout])])
    p_wout = reduce_in_chip("w_out", (g_wout, r_wout))
    (g_win,), ((q_wout,),) = mix_dwin(u, dproj, N_CHIPS, "mix_dwin", [scatter(p_wout)])
    (dh1, d_mix), ((r_win,),) = mix_in_bwd(dproj, wg["w_in"], h1, mix_norm, dh2, "mix_in_bwd", [SiblingExchange([g_win])])
    p_win = reduce_in_chip("w_in", (g_win, r_win))
    (da1, db1, df1), ((q_win,),) = ffn_bwd_act(dh1, whole("ffn1_w2"), ga1, gb1, "ffn1_bwd_act", [scatter(p_win)])
    d_vectors = {"ffn1_norm": jnp.zeros_like(ffn1_norm), "mix_norm": d_mix, "pool_scale": d_pool_scale,
                 "ret_norm": d_ret_norm, "ffn2_norm": d_ffn2, "final_norm": d_final}
    pack = jnp.concatenate([d_pool_w.reshape(-1, 128), _pack_vectors(d_vectors), jnp.broadcast_to(loss, (8, 128))], axis=0)
    (g_f1w1, g_f1w3), ((packs,),) = ffn_dw([da1, db1], n1, 2, "ffn1_dw13", [AllExchange(pack)])
    g_f1w1, g_f1w3 = sharded(g_f1w1), sharded(g_f1w3)
    (g_f1w2,), ((r_f1w1, r_f1w3),) = ffn_dw([s1], df1, 1, "ffn1_dw2", [SiblingExchange([g_f1w1, g_f1w3])])
    g_f1w2 = sharded(g_f1w2)
    p_f1w1, p_f1w3 = reduce_in_chip("ffn1_w13", (g_f1w1, r_f1w1), (g_f1w3, r_f1w3))
    (dx, d_ffn1), ((q_f1w1, q_f1w3), (r_f1w2,)) = ffn_bwd_in(
        da1, db1, whole("ffn1_w1"), whole("ffn1_w3"), xs, ffn1_norm, dh1, "ffn1_bwd_in",
        [scatter(p_f1w1, p_f1w3), SiblingExchange([g_f1w2])])
    p_f1w2 = reduce_in_chip("ffn1_w2", (g_f1w2, r_f1w2))

    (q_f1w2,), (late,) = update(["ffn2_w1", "ffn2_w3", "ffn1_w1", "ffn1_w3"], [q_f2w1, q_f2w3, q_f1w1, q_f1w3], "adamw_w13",
                                [scatter(p_f1w2), AllExchange(d_ffn1.reshape(-1, 128))])
    update(["ffn2_w2", "ffn1_w2"], [q_f2w2, q_f1w2], "adamw_w2")
    update(["w_in"], [q_win], "adamw_w_in")
    update(["w_out"], [q_wout], "adamw_w_out")
    of_pool, of_vectors, loss_sum = adamw_small(packs, late, [t["pool_w"].reshape(-1, 128) for t in (w, m, v)],
                                                jnp.concatenate([t[k].reshape(-1, 128) for t in (w, m, v) for k in VECTORS], axis=0),
                                                "adamw_small")
    for res, pool_part, vector_part in zip((grad, delta, new_m, new_v), of_pool, of_vectors):
        res["pool_w"] = pool_part.reshape(pool_w.shape)
        res.update(_unpack_vectors(vector_part, w))
    loss = loss_sum[0, 0]

    return (loss, dx[None], *[grad[k] for k in WEIGHTS], *[delta[k] for k in WEIGHTS],
            *[new_m[k] for k in WEIGHTS], *[new_v[k] for k in WEIGHTS])
```

```python
import math

import jax
import jax.numpy as jnp
from jax import lax
from jax.experimental import pallas as pl
from jax.experimental.pallas import tpu as pltpu

F32 = jnp.float32
BF16 = jnp.bfloat16

EPS = 1e-6
N_CHIPS = 4
N_GROUPS = 4
HEAD_DIM = 128
RET_CHUNK = 128
ROPE_BASE = 10000.0
ADAM_LR, ADAM_B1, ADAM_B2, ADAM_EPS, ADAM_WD, ADAM_STEP = 0.001, 0.9, 0.999, 1e-08, 0.01, 10
VMEM_LIMIT_V7X = 56 * 1024 * 1024
ADAMW_VMEM_BUDGET = 32 * 1024 * 1024
MESH = pl.DeviceIdType.MESH
ANY = pl.BlockSpec(memory_space=pl.ANY)


def _dot(a, b):
    return jnp.dot(a, b, preferred_element_type=F32)


def _dot_nt(a, b):
    return lax.dot_general(a, b, (((1,), (1,)), ((), ())), preferred_element_type=F32)


def _dot_tn(a, b):
    return lax.dot_general(a, b, (((0,), (0,)), ((), ())), preferred_element_type=F32)


def _rstd(h):
    return lax.rsqrt(jnp.mean(h * h, axis=-1, keepdims=True) + EPS)


def _rmsnorm_bwd(dn, h, gain):
    r = _rstd(h)
    nh = h * r
    dnh = dn * gain
    dh = r * (dnh - nh * jnp.mean(dnh * nh, axis=-1, keepdims=True))
    return dh, dn * nh


def _silu_parts(a):
    sig = jax.nn.sigmoid(a)
    silu = a * sig
    return silu, sig + silu * (1.0 - sig)


def _mesh_pos():
    return lax.axis_index("x"), lax.axis_index("y"), lax.axis_index("c")


class ChipExchange:
    def __init__(self, srcs, scatter, placed=()):
        n = len(srcs)
        self.inputs, self.scatter, self.n, self.reach = list(srcs) + list(placed), scatter, n, REACH_CHIPS
        self.aliases = {n + t: t for t in range(n)} if scatter else {}
        self.half_rows = [s.shape[1] if scatter else s.shape[0] // 2 for s in srcs]
        self.out_shape = [jax.ShapeDtypeStruct((N_CHIPS, 2 * rh, s.shape[-1]), s.dtype) for s, rh in zip(srcs, self.half_rows)]
        if scatter:
            self.out_shape += [jax.ShapeDtypeStruct((2, rh // 2, s.shape[-1]), s.dtype) for s, rh in zip(srcs, self.half_rows)]
        dma = pltpu.SemaphoreType.DMA
        self.sems = [dma((4 * n,)), dma((4 * n,)), dma((2 * n,)), dma((2 * n,)), dma((4 * n,)), dma((4 * n,))]

    def _copies(self, src, out, sems):
        hop1_send, hop1_recv, hop2_send, hop2_recv, d2d_send, d2d_recv = sems
        x, y, c = _mesh_pos()
        me, dg = 2 * x + y, 2 * (1 - x) + (1 - y)
        sibling = (x, y, 1 - c)
        n = self.n
        mine, theirs = c, 1 - c

        def nb(a):
            nx, ny = x ^ (1 - a), y ^ a
            return 2 * nx + ny, (nx, ny, c)

        def remote(s, d, send, recv, k, to):
            return pltpu.make_async_remote_copy(src_ref=s, dst_ref=d, send_sem=send.at[k], recv_sem=recv.at[k],
                                                device_id=to, device_id_type=MESH)

        class Copies:
            def slot(_, t, chip, half):
                rh = self.half_rows[t]
                return out[t].at[chip, pl.ds(half * rh, rh), :]

            def quarter(_, t, chip, q):
                qh = self.half_rows[t] // 2
                return out[t].at[chip, pl.ds(mine * 2 * qh + q * qh, qh), :]

            def own_shard(k, t):
                return remote(src[t], out[t].at[me], d2d_send, d2d_recv, 4 * t + 3, sibling)

            def hop1(k, t, a, transit=False):
                rh = self.half_rows[t]
                chip, to = nb(a)
                if transit:
                    piece = src[t].at[dg, pl.ds(a * (rh // 2), rh // 2), :]
                    return remote(piece, out[n + t].at[a], hop1_send, hop1_recv, 4 * t + 2 + a, to)
                piece = src[t].at[chip] if self.scatter else src[t].at[pl.ds(mine * rh, rh), :]
                return remote(piece, k.slot(t, me, mine), hop1_send, hop1_recv, 4 * t + a, to)

            def landed1(k, t, a, transit=False):
                here = out[n + t].at[a] if transit else k.slot(t, nb(a)[0], mine)
                return remote(here, here, hop1_send, hop1_recv, 4 * t + (2 if transit else 0) + a, sibling)

            def hop2(k, t, q):
                origin, to = nb(q)[0], nb(1 - q)[1]
                piece = out[n + t].at[q] if self.scatter else k.quarter(t, origin, q)
                return remote(piece, k.quarter(t, origin, q), hop2_send, hop2_recv, 2 * t + q, to)

            def landed2(k, t, q):
                here = k.quarter(t, dg, q)
                return remote(here, here, hop2_send, hop2_recv, 2 * t + q, sibling)

            def d2d(k, t, p, chip, own=False, arriving=False):
                if arriving:
                    there = k.slot(t, chip, theirs)
                    return remote(there, there, d2d_send, d2d_recv, 4 * t + p, sibling)
                piece = src[t].at[me] if own else k.slot(t, chip, mine)
                return remote(piece, k.slot(t, chip, mine), d2d_send, d2d_recv, 4 * t + p, sibling)

        return Copies(), nb, me, dg, c

    def start(self, src, out, sems):
        k, nb, me, dg, c = self._copies(src, out, sems)
        for t in range(self.n):
            for first in range(2):
                a = first ^ c
                k.hop1(t, a).start()
                if self.scatter:
                    k.hop1(t, a, transit=True).start()
            if self.scatter:
                k.d2d(t, 3, me, own=True).start()
            else:
                k.own_shard(t).start()

    def mid(self, src, out, sems):
        k, nb, me, dg, c = self._copies(src, out, sems)
        for t in range(self.n):
            for first in range(2):
                a = first ^ c
                if self.scatter:
                    k.landed1(t, a, transit=True).wait_recv()
                    k.hop2(t, a).start()
                k.landed1(t, a).wait_recv()
                if not self.scatter:
                    k.hop2(t, a).start()
                k.d2d(t, a, nb(a)[0]).start()

    def finish(self, src, out, sems):
        k, nb, me, dg, c = self._copies(src, out, sems)
        for t in range(self.n):
            for q in range(2):
                k.landed2(t, q).wait_recv()
            k.d2d(t, 2, dg).start()
        for t in range(self.n):
            for a in range(2):
                k.d2d(t, a, nb(a)[0], arriving=True).wait_recv()
            k.d2d(t, 2, dg, arriving=True).wait_recv()
            if self.scatter:
                k.d2d(t, 3, me, arriving=True).wait_recv()
        for t in range(self.n):
            for a in range(2):
                k.hop1(t, a).wait_send()
                if self.scatter:
                    k.hop1(t, a, transit=True).wait_send()
                k.hop2(t, a).wait_send()
                k.d2d(t, a, nb(a)[0]).wait_send()
            k.d2d(t, 2, dg).wait_send()
            if self.scatter:
                k.d2d(t, 3, me, own=True).wait_send()
            else:
                k.own_shard(t).wait()


class SiblingExchange:
    def __init__(self, grads):
        self.inputs, self.n, self.aliases, self.reach = list(grads), len(grads), {}, REACH_SIBLING
        self.half_rows = [g.shape[1] // 2 for g in grads]
        self.out_shape = [jax.ShapeDtypeStruct((g.shape[0], rh, g.shape[2]), g.dtype) for g, rh in zip(grads, self.half_rows)]
        self.sems = [pltpu.SemaphoreType.DMA((self.n,)), pltpu.SemaphoreType.DMA((self.n,))]

    def _plan(self, src, out, sems):
        x, y, c = _mesh_pos()
        return [pltpu.make_async_remote_copy(
            src_ref=src[t].at[:, pl.ds((1 - c) * self.half_rows[t], self.half_rows[t]), :], dst_ref=out[t],
            send_sem=sems[0].at[t], recv_sem=sems[1].at[t], device_id=(x, y, 1 - c), device_id_type=MESH) for t in range(self.n)]

    def start(self, src, out, sems):
        for cp in self._plan(src, out, sems):
            cp.start()

    def mid(self, src, out, sems):
        pass

    def finish(self, src, out, sems):
        for cp in self._plan(src, out, sems):
            cp.wait()


REACH_SIBLING, REACH_CHIPS, REACH_ALL = 0, 1, 2


def _entry_barrier(reach):
    x, y, c = _mesh_pos()
    peers = [(x, y, 1 - c)]
    if reach == REACH_CHIPS:
        peers += [(1 - x, y, c), (x, 1 - y, c)]
    elif reach == REACH_ALL:
        peers = [(x ^ dx, y ^ dy, c ^ dc) for dx in (0, 1) for dy in (0, 1) for dc in (0, 1)][1:]
    barrier = pltpu.get_barrier_semaphore()
    for peer in peers:
        pl.semaphore_signal(barrier, inc=1, device_id=peer, device_id_type=MESH)
    pl.semaphore_wait(barrier, len(peers))


def _call(body, hosted=(), *, name, in_specs, out_specs, out_shape, args, grid=(), scratch_shapes=(), aliased=None):
    n_in, n_out, n_scr = len(in_specs), len(out_specs), len(scratch_shapes)
    total = math.prod(grid)
    mid_step = max(0, (5 * total) // 8 - 1)

    def full(*refs):
        pos = [0]

        def take(k):
            pos[0] += k
            return refs[pos[0] - k:pos[0]]

        ins, h_in = take(n_in), [take(len(h.inputs)) for h in hosted]
        outs, h_out = take(n_out), [take(len(h.out_shape)) for h in hosted]
        scr, h_sem = take(n_scr), [take(len(h.sems)) for h in hosted]
        step = 0
        for axis, size in enumerate(grid):
            step = step * size + pl.program_id(axis)

        def phase(at, method):
            if not hosted:
                return

            def run():
                if method == "start":
                    _entry_barrier(reach)
                for h, s, o, m in zip(hosted, h_in, h_out, h_sem):
                    getattr(h, method)(s, o, m)

            if total == 1:
                run()
            else:
                pl.when(step == at)(run)

        phase(0, "start")
        body(*ins, *outs, *scr)
        phase(mid_step, "mid")
        phase(total - 1, "finish")

    aliases, i0, o0 = dict(aliased or {}), n_in, n_out
    for h in hosted:
        aliases.update({i0 + i: o0 + o for i, o in h.aliases.items()})
        i0, o0 = i0 + len(h.inputs), o0 + len(h.out_shape)
    reach = max((h.reach for h in hosted), default=None)
    params = dict(vmem_limit_bytes=VMEM_LIMIT_V7X)
    if hosted:
        params["collective_id"] = reach
    results = pl.pallas_call(
        full, name=name, grid=grid,
        in_specs=list(in_specs) + [ANY] * (i0 - n_in),
        out_specs=list(out_specs) + [ANY] * (o0 - n_out),
        out_shape=list(out_shape) + [s for h in hosted for s in h.out_shape],
        scratch_shapes=list(scratch_shapes) + [s for h in hosted for s in h.sems],
        input_output_aliases=aliases,
        compiler_params=pltpu.CompilerParams(**params),
    )(*args, *[s for h in hosted for s in h.inputs])
    outs, extras, pos = list(results[:n_out]), [], n_out
    for h in hosted:
        extras.append(list(results[pos:pos + h.n]))
        pos += len(h.out_shape)
    return outs, extras


def cast_shards(shards, name, hosted=()):
    n = len(shards)

    def body(*refs):
        for x_ref, o_ref in zip(refs[:n], refs[n:]):
            o_ref[...] = x_ref[...].astype(BF16)

    whole = lambda s: pl.BlockSpec(s.shape, lambda: (0,) * s.ndim)
    return _call(body, hosted, name=name, in_specs=[whole(s) for s in shards], out_specs=[whole(s) for s in shards],
                 out_shape=[jax.ShapeDtypeStruct(s.shape, BF16) for s in shards], args=list(shards))


class AllExchange:
    def __init__(self, pack):
        self.inputs, self.n, self.aliases, self.reach = [pack], 1, {}, REACH_ALL
        self.out_shape = [jax.ShapeDtypeStruct((2 * N_CHIPS,) + pack.shape, pack.dtype)]
        self.sems = [pltpu.SemaphoreType.DMA, pltpu.SemaphoreType.DMA((7,)), pltpu.SemaphoreType.DMA((7,))]

    def _copies(self, src, out, sems):
        local_sem, send_sem, recv_sem = sems
        x, y, c = _mesh_pos()
        flips = [(dx, dy, dc) for dx in (0, 1) for dy in (0, 1) for dc in (0, 1)][1:]
        peers = [(x ^ dx, y ^ dy, c ^ dc) for dx, dy, dc in flips]
        remote = lambda s, d, k: pltpu.make_async_remote_copy(
            src_ref=s, dst_ref=d, send_sem=send_sem.at[k], recv_sem=recv_sem.at[k], device_id=peers[k], device_id_type=MESH)
        sends = [remote(src[0], out[0].at[4 * x + 2 * y + c], k) for k in range(7)]
        landed = [remote(out[0].at[4 * px + 2 * py + pc], out[0].at[4 * px + 2 * py + pc], k) for k, (px, py, pc) in enumerate(peers)]
        return sends, landed, pltpu.make_async_copy(src[0], out[0].at[4 * x + 2 * y + c], local_sem)

    def start(self, src, out, sems):
        sends, _, local = self._copies(src, out, sems)
        for cp in sends:
            cp.start()
        local.start()

    def mid(self, src, out, sems):
        pass

    def finish(self, src, out, sems):
        sends, landed, local = self._copies(src, out, sems)
        for cp in landed:
            cp.wait_recv()
        for cp in sends:
            cp.wait_send()
        local.wait()


MXU_COLS = 256


def _resident(shape):
    return pl.BlockSpec(shape, lambda *_: (0,) * len(shape), pipeline_mode=pl.Buffered(1))


def ffn_up(h, gain, w1, w3, name, hosted=()):
    T, D = h.shape
    F = w1.shape[0]
    tm = min(T, 512)

    def body(h_ref, g_ref, w1_ref, w3_ref, n_ref, ga_ref, gb_ref, s_ref):
        hh = h_ref[...]
        n = (hh * _rstd(hh) * g_ref[...]).astype(BF16)
        n_ref[...] = n
        for c in range(0, F, MXU_COLS):
            cols = slice(c, c + MXU_COLS)
            a = _dot_nt(n, w1_ref[cols, :])
            b = _dot_nt(n, w3_ref[cols, :])
            silu, dsilu = _silu_parts(a)
            ga_ref[:, cols] = (b * dsilu).astype(BF16)
            gb_ref[:, cols] = silu.astype(BF16)
            s_ref[:, cols] = (silu * b).astype(BF16)

    act = jax.ShapeDtypeStruct((T, F), BF16)
    act_spec = pl.BlockSpec((tm, F), lambda i: (i, 0))
    row_spec = pl.BlockSpec((tm, D), lambda i: (i, 0))
    return _call(
        body, hosted, name=name, grid=(T // tm,),
        in_specs=[row_spec, pl.BlockSpec((1, D), lambda i: (0, 0)), _resident((F, D)), _resident((F, D))],
        out_specs=[row_spec, act_spec, act_spec, act_spec],
        out_shape=[jax.ShapeDtypeStruct((T, D), BF16), act, act, act],
        args=[h, gain, w1, w3])


def ffn_down(s, w2, h, name, hosted=()):
    T, F = s.shape
    D = h.shape[1]
    tm = min(T, 1024)

    def body(s_ref, w2_ref, h_ref, o_ref):
        o_ref[...] = h_ref[...] + 0.5 * _dot(s_ref[...], w2_ref[...])

    row_spec = pl.BlockSpec((tm, D), lambda i: (i, 0))
    return _call(
        body, hosted, name=name, grid=(T // tm,),
        in_specs=[pl.BlockSpec((tm, F), lambda i: (i, 0)), pl.BlockSpec((F, D), lambda i: (0, 0)), row_spec],
        out_specs=[row_spec],
        out_shape=[jax.ShapeDtypeStruct((T, D), F32)],
        args=[s, w2, h])


def ffn_bwd_act(dh, w2, ga, gb, name, hosted=()):
    T, D = dh.shape
    F = w2.shape[0]
    tm = min(T, 512)

    def body(dh_ref, w2_ref, ga_ref, gb_ref, da_ref, db_ref, df_ref):
        df = (0.5 * dh_ref[...]).astype(BF16)
        df_ref[...] = df
        for c in range(0, F, MXU_COLS):
            cols = slice(c, c + MXU_COLS)
            ds = _dot_nt(df, w2_ref[cols, :])
            da_ref[:, cols] = (ds * ga_ref[:, cols].astype(F32)).astype(BF16)
            db_ref[:, cols] = (ds * gb_ref[:, cols].astype(F32)).astype(BF16)

    act = jax.ShapeDtypeStruct((T, F), BF16)
    act_spec = pl.BlockSpec((tm, F), lambda i: (i, 0))
    row_spec = pl.BlockSpec((tm, D), lambda i: (i, 0))
    return _call(
        body, hosted, name=name, grid=(T // tm,),
        in_specs=[row_spec, _resident((F, D)), act_spec, act_spec],
        out_specs=[act_spec, act_spec, row_spec],
        out_shape=[act, act, jax.ShapeDtypeStruct((T, D), BF16)],
        args=[dh, w2, ga, gb])


def ffn_dw(xs, y, halves, name, hosted=()):
    T, F = xs[0].shape
    D = y.shape[1]
    nx, fh = len(xs), F // halves
    tk = min(T, 512)
    nk = T // tk

    def body(*refs):
        y_ref, x_refs, o_refs, accs = refs[0], refs[1:1 + nx], refs[1 + nx:1 + 2 * nx], refs[1 + 2 * nx:]
        k = pl.program_id(1)

        @pl.when(k == 0)
        def _():
            for acc in accs:
                acc[...] = jnp.zeros_like(acc)

        yy = y_ref[...]
        for x_ref, acc in zip(x_refs, accs):
            acc[...] += _dot_tn(x_ref[...], yy)

        @pl.when(k == nk - 1)
        def _():
            for o_ref, acc in zip(o_refs, accs):
                o_ref[...] = acc[...].astype(BF16)

    out = jax.ShapeDtypeStruct((F, D), BF16)
    return _call(
        body, hosted, name=name, grid=(halves, nk),
        in_specs=[pl.BlockSpec((tk, D), lambda j, k: (k, 0))] + [pl.BlockSpec((tk, fh), lambda j, k: (k, j))] * nx,
        out_specs=[pl.BlockSpec((fh, D), lambda j, k: (j, 0))] * nx,
        out_shape=[out] * nx,
        scratch_shapes=[pltpu.VMEM((fh, D), F32)] * nx,
        args=[y] + list(xs))


def ffn_bwd_in(da, db, w1, w3, h, gain, dh, name, hosted=()):
    T, F = da.shape
    D = h.shape[1]
    tm = min(T, 512)

    def body(da_ref, db_ref, w1_ref, w3_ref, h_ref, g_ref, dh_ref, o_ref, dg_ref):
        dn = _dot(da_ref[...], w1_ref[...]) + _dot(db_ref[...], w3_ref[...])
        dhn, dg = _rmsnorm_bwd(dn, h_ref[...], g_ref[...])
        o_ref[...] = dh_ref[...] + dhn

        @pl.when(pl.program_id(0) == 0)
        def _():
            dg_ref[...] = jnp.zeros_like(dg_ref)

        dg_ref[...] += jnp.sum(dg, axis=0, keepdims=True)

    act_spec = pl.BlockSpec((tm, F), lambda i: (i, 0))
    row_spec = pl.BlockSpec((tm, D), lambda i: (i, 0))
    vec_spec = pl.BlockSpec((1, D), lambda i: (0, 0))
    return _call(
        body, hosted, name=name, grid=(T // tm,),
        in_specs=[act_spec, act_spec, _resident((F, D)), _resident((F, D)), row_spec, vec_spec, row_spec],
        out_specs=[row_spec, vec_spec],
        out_shape=[jax.ShapeDtypeStruct((T, D), F32), jax.ShapeDtypeStruct((1, D), F32)],
        args=[da, db, w1, w3, h, gain, dh])


def mix_in(h, gain, wing, name, hosted=()):
    T, D = h.shape
    nsh, _, Cs = wing.shape
    tm = min(T, 1024)

    def body(h_ref, g_ref, w_ref, u_ref, p_ref):
        hh = h_ref[...]
        u = (hh * _rstd(hh) * g_ref[...]).astype(BF16)
        u_ref[...] = u
        for j in range(nsh):
            p_ref[:, j * Cs:(j + 1) * Cs] = _dot(u, w_ref[j])

    return _call(
        body, hosted, name=name, grid=(T // tm,),
        in_specs=[pl.BlockSpec((tm, D), lambda i: (i, 0)), pl.BlockSpec((1, D), lambda i: (0, 0)),
                  pl.BlockSpec((nsh, D, Cs), lambda i: (0, 0, 0))],
        out_specs=[pl.BlockSpec((tm, D), lambda i: (i, 0)), pl.BlockSpec((tm, nsh * Cs), lambda i: (i, 0))],
        out_shape=[jax.ShapeDtypeStruct((T, D), BF16), jax.ShapeDtypeStruct((T, nsh * Cs), F32)],
        args=[h, gain, wing])


def mix_out(a, b, woutg, h, name, hosted=()):
    T, W = a.shape
    D = h.shape[1]
    wout = woutg.reshape(2, W, D)
    tm = min(T, 1024)

    def body(a_ref, b_ref, w_ref, h_ref, o_ref):
        o_ref[...] = h_ref[...] + _dot(a_ref[...], w_ref[0]) + _dot(b_ref[...], w_ref[1])

    return _call(
        body, hosted, name=name, grid=(T // tm,),
        in_specs=[pl.BlockSpec((tm, W), lambda i: (i, 0)), pl.BlockSpec((tm, W), lambda i: (i, 0)),
                  pl.BlockSpec((2, W, D), lambda i: (0, 0, 0)), pl.BlockSpec((tm, D), lambda i: (i, 0))],
        out_specs=[pl.BlockSpec((tm, D), lambda i: (i, 0))],
        out_shape=[jax.ShapeDtypeStruct((T, D), F32)],
        args=[a, b, wout, h])


def mix_out_bwd(dh, woutg, a, b, name, hosted=()):
    T, D = dh.shape
    W = a.shape[1]
    nsh, Rs, _ = woutg.shape
    wout = woutg.reshape(2, W, D)
    tk = min(T, 512)
    nk = T // tk

    def body(dh_ref, w_ref, a_ref, b_ref, da_ref, db_ref, dw_ref, acc):
        k = pl.program_id(0)

        @pl.when(k == 0)
        def _():
            acc[...] = jnp.zeros_like(acc)

        dhb = dh_ref[...].astype(BF16)
        da_ref[...] = _dot_nt(dhb, w_ref[0])
        db_ref[...] = _dot_nt(dhb, w_ref[1])
        acc[0:W, :] += _dot_tn(a_ref[...], dhb)
        acc[W:2 * W, :] += _dot_tn(b_ref[...], dhb)

        @pl.when(k == nk - 1)
        def _():
            for j in range(nsh):
                dw_ref[j] = acc[j * Rs:(j + 1) * Rs, :].astype(BF16)

    return _call(
        body, hosted, name=name, grid=(nk,),
        in_specs=[pl.BlockSpec((tk, D), lambda k: (k, 0)), pl.BlockSpec((2, W, D), lambda k: (0, 0, 0)),
                  pl.BlockSpec((tk, W), lambda k: (k, 0)), pl.BlockSpec((tk, W), lambda k: (k, 0))],
        out_specs=[pl.BlockSpec((tk, W), lambda k: (k, 0)), pl.BlockSpec((tk, W), lambda k: (k, 0)),
                   pl.BlockSpec((nsh, Rs, D), lambda k: (0, 0, 0))],
        out_shape=[jax.ShapeDtypeStruct((T, W), F32), jax.ShapeDtypeStruct((T, W), F32),
                   jax.ShapeDtypeStruct((nsh, Rs, D), BF16)],
        scratch_shapes=[pltpu.VMEM((2 * W, D), F32)],
        args=[dh, wout, a, b])


def _dproj_block(g):
    return (g // N_GROUPS + N_GROUPS) % (N_GROUPS + 1), g % N_GROUPS


def mix_dwin(u, dproj, nsh, name, hosted=()):
    T, D = u.shape
    Hd = HEAD_DIM
    slabs, _, width = dproj.shape
    blocks = slabs * width // Hd
    Cs = blocks * Hd // nsh
    tk = min(T, 512)
    nk = T // tk

    def body(u_ref, d_ref, o_ref, acc):
        k = pl.program_id(0)

        @pl.when(k == 0)
        def _():
            acc[...] = jnp.zeros_like(acc)

        where = [_dproj_block(g) for g in range(blocks)]
        d = jnp.concatenate([d_ref[slab, :, col * Hd:(col + 1) * Hd] for slab, col in where], axis=1)
        acc[...] += _dot_tn(u_ref[...], d)

        @pl.when(k == nk - 1)
        def _():
            for j in range(nsh):
                o_ref[j] = acc[:, j * Cs:(j + 1) * Cs].astype(BF16)

    return _call(
        body, hosted, name=name, grid=(nk,),
        in_specs=[pl.BlockSpec((tk, D), lambda k: (k, 0)), pl.BlockSpec((slabs, tk, width), lambda k: (0, k, 0))],
        out_specs=[pl.BlockSpec((nsh, D, Cs), lambda k: (0, 0, 0))],
        out_shape=[jax.ShapeDtypeStruct((nsh, D, Cs), BF16)],
        scratch_shapes=[pltpu.VMEM((D, blocks * Hd), F32)],
        args=[u, dproj])


def mix_in_bwd(dproj, wing, h, gain, dh, name, hosted=()):
    T, D = h.shape
    nsh, _, Cs = wing.shape
    Hd = HEAD_DIM
    per = Cs // Hd
    tm = min(T, 1024)

    def body(d_ref, w_ref, h_ref, g_ref, dh_ref, o_ref, dg_ref):
        def shard(j):
            blocks = [_dproj_block(per * j + i) for i in range(per)]
            return jnp.concatenate([d_ref[slab, :, col * Hd:(col + 1) * Hd] for slab, col in blocks], axis=1)

        du = _dot_nt(shard(0), w_ref[0])
        for j in range(1, nsh):
            du += _dot_nt(shard(j), w_ref[j])
        dhn, dg = _rmsnorm_bwd(du, h_ref[...], g_ref[...])
        o_ref[...] = dh_ref[...] + dhn

        @pl.when(pl.program_id(0) == 0)
        def _():
            dg_ref[...] = jnp.zeros_like(dg_ref)

        dg_ref[...] += jnp.sum(dg, axis=0, keepdims=True)

    row_spec = pl.BlockSpec((tm, D), lambda i: (i, 0))
    vec_spec = pl.BlockSpec((1, D), lambda i: (0, 0))
    return _call(
        body, hosted, name=name, grid=(T // tm,),
        in_specs=[pl.BlockSpec((dproj.shape[0], tm, dproj.shape[2]), lambda i: (0, i, 0)),
                  pl.BlockSpec((nsh, D, Cs), lambda i: (0, 0, 0)), row_spec, vec_spec, row_spec],
        out_specs=[row_spec, vec_spec],
        out_shape=[jax.ShapeDtypeStruct((T, D), F32), jax.ShapeDtypeStruct((1, D), F32)],
        args=[dproj, wing, h, gain, dh])


def _pool_window(x, group, T, trailing):
    rows = lax.broadcasted_iota(jnp.int32, x.shape, 0)

    def shifted(z, k):
        if trailing:
            return jnp.where(rows >= k, pltpu.roll(z, k, 0), 0.0)
        return jnp.where(rows < T - k, pltpu.roll(z, T - k, 0), 0.0)

    s2 = x + shifted(x, 1)
    s4 = s2 + shifted(s2, 2)
    s8 = s4 + shifted(s4, 4)
    s16 = s8 + shifted(s8, 8)
    return jnp.where(group == 0, s2, jnp.where(group == 1, s4, jnp.where(group == 2, s8, s16)))


def _pool_count(group, shape):
    rows = lax.broadcasted_iota(jnp.int32, shape, 0)
    w = jnp.where(group == 0, 2, jnp.where(group == 1, 4, jnp.where(group == 2, 8, 16)))
    return jnp.minimum(rows + 1, w).astype(F32)


def pool_fwd(proj, pool_w, pool_scale, name, hosted=()):
    T = proj.shape[0]
    Hd = HEAD_DIM

    def body(x_ref, w_ref, sc_ref, a_ref):
        g = pl.program_id(0)
        x = x_ref[...]
        pooled = _pool_window(x, g, T, True) / _pool_count(g, x.shape) - x
        a_ref[...] = (_dot(pooled.astype(BF16), w_ref[0].astype(BF16)) * sc_ref[...]).astype(BF16)

    return _call(
        body, hosted, name=name, grid=(N_GROUPS,),
        in_specs=[pl.BlockSpec((T, Hd), lambda g: (0, g)), pl.BlockSpec((1, Hd, Hd), lambda g: (g, 0, 0)),
                  pl.BlockSpec((1, Hd), lambda g: (0, g))],
        out_specs=[pl.BlockSpec((T, Hd), lambda g: (0, g))],
        out_shape=[jax.ShapeDtypeStruct((T, N_GROUPS * Hd), BF16)],
        args=[proj, pool_w, pool_scale])


def pool_bwd(proj, da, pool_w, pool_scale, name, hosted=()):
    T = proj.shape[0]
    Hd = HEAD_DIM

    def body(x_ref, da_ref, w_ref, sc_ref, dx_ref, dw_ref, dsc_ref):
        g = pl.program_id(0)
        x = x_ref[...]
        cnt = _pool_count(g, x.shape)
        pooled = (_pool_window(x, g, T, True) / cnt - x).astype(BF16)
        wb = w_ref[0].astype(BF16)
        dav = da_ref[...]
        dsc_ref[...] = jnp.sum(dav * _dot(pooled, wb), axis=0, keepdims=True)
        dout = (dav * sc_ref[...]).astype(BF16)
        dw_ref[0] = _dot_tn(pooled, dout)
        dpooled = _dot_nt(dout, wb)
        dx_ref[0] = (_pool_window(dpooled / cnt, g, T, False) - dpooled).astype(BF16)

    col_spec = pl.BlockSpec((T, Hd), lambda g: (0, g))
    return _call(
        body, hosted, name=name, grid=(N_GROUPS,),
        in_specs=[col_spec, col_spec, pl.BlockSpec((1, Hd, Hd), lambda g: (g, 0, 0)), pl.BlockSpec((1, Hd), lambda g: (0, g))],
        out_specs=[pl.BlockSpec((1, T, Hd), lambda g: (N_GROUPS, 0, g)), pl.BlockSpec((1, Hd, Hd), lambda g: (g, 0, 0)),
                   pl.BlockSpec((1, Hd), lambda g: (0, g))],
        out_shape=[jax.ShapeDtypeStruct((N_GROUPS + 1, T, N_GROUPS * Hd), BF16), jax.ShapeDtypeStruct((N_GROUPS, Hd, Hd), F32),
                   jax.ShapeDtypeStruct((1, N_GROUPS * Hd), F32)],
        args=[proj, da, pool_w, pool_scale])


def _ret_tables(T):
    Hd, C = HEAD_DIM, RET_CHUNK
    inv_freq = 1.0 / (ROPE_BASE ** (jnp.arange(0, Hd, 2, dtype=F32) / Hd))
    ang = jnp.arange(T, dtype=F32)[:, None] * inv_freq[None, :]
    cos, sin = jnp.cos(ang), jnp.sin(ang)
    cos2 = jnp.concatenate([cos, cos], axis=-1)
    sin2 = jnp.concatenate([-sin, sin], axis=-1)
    log_gamma = jnp.log1p(-jnp.exp2(-5.0 - jnp.arange(N_GROUPS, dtype=F32)))
    pos = jnp.arange(C, dtype=F32)
    rel = pos[:, None] - pos[None, :]
    intra = jnp.where(rel[None] >= 0, jnp.exp(log_gamma[:, None, None] * jnp.maximum(rel, 0.0)[None]), 0.0)
    k_tail = jnp.exp(log_gamma[:, None] * (C - 1 - pos)[None, :])
    q_head = jnp.exp(log_gamma[:, None] * (pos + 1.0)[None, :])
    chunk_decay = jnp.exp(log_gamma * C)
    wide = lambda t: jnp.broadcast_to(t[:, :, None], (N_GROUPS, C, Hd))
    return cos2, sin2, intra, wide(k_tail), wide(q_head), jnp.broadcast_to(chunk_decay[:, None, None], (N_GROUPS, 1, Hd))


def _rope(x, cos2, sin2):
    return x * cos2 + pltpu.roll(x, HEAD_DIM // 2, 1) * sin2


def _rope_t(d, cos2, sin2):
    return d * cos2 + pltpu.roll(d * sin2, HEAD_DIM // 2, 1)


def _ret_specs(tseg, seg_of):
    Hd, G = HEAD_DIM, N_GROUPS
    col = lambda kind: pl.BlockSpec((tseg, Hd), lambda h, s: (seg_of(s), G * kind + h))
    tab = pl.BlockSpec((tseg, Hd), lambda h, s: (seg_of(s), 0))
    head = pl.BlockSpec((1, RET_CHUNK, Hd), lambda h, s: (h, 0, 0))
    cd = pl.BlockSpec((1, 1, Hd), lambda h, s: (h, 0, 0))
    gain = pl.BlockSpec((1, Hd), lambda h, s: (0, h))
    return col, tab, head, cd, gain


def ret_fwd(proj, ret_norm, tables, name, hosted=()):
    T = proj.shape[0]
    Hd, C, G = HEAD_DIM, RET_CHUNK, N_GROUPS
    tseg = min(T, 1024)
    nseg, nck = T // tseg, tseg // C
    scale = Hd ** -0.5
    cos2, sin2, intra, k_tail, q_head, chunk_decay = tables

    def body(q_ref, k_ref, v_ref, g_ref, gain_ref, cos_ref, sin_ref, m_ref, kt_ref, qh_ref, cd_ref,
             b_ref, o_ref, rp_ref, state):
        @pl.when(pl.program_id(1) == 0)
        def _():
            state[...] = jnp.zeros_like(state)

        def chunk(ci, carry):
            rows = pl.ds(pl.multiple_of(ci * C, C), C)
            cos, sin = cos_ref[rows, :], sin_ref[rows, :]
            qr = _rope(q_ref[rows, :], cos, sin)
            kr = _rope(k_ref[rows, :], cos, sin) * scale
            qb, kb, vb = qr.astype(BF16), kr.astype(BF16), v_ref[rows, :].astype(BF16)
            r = state[...]
            rp_ref[0, ci] = r.astype(BF16)
            sc = _dot_nt(qb, kb) * m_ref[0]
            o = _dot(sc.astype(BF16), vb) + _dot((qr * qh_ref[0]).astype(BF16), r.astype(BF16))
            state[...] = cd_ref[0] * r + _dot_tn((kr * kt_ref[0]).astype(BF16), vb)
            o_ref[rows, :] = o
            on = o * _rstd(o)
            b_ref[rows, :] = (jax.nn.silu(g_ref[rows, :]) * (on * gain_ref[...])).astype(BF16)
            return carry

        lax.fori_loop(0, nck, chunk, 0, unroll=True)

    col, tab, head, cd, gain = _ret_specs(tseg, lambda s: s)
    out_col = pl.BlockSpec((tseg, Hd), lambda h, s: (s, h))
    return _call(
        body, hosted, name=name, grid=(G, nseg),
        in_specs=[col(1), col(2), col(3), col(4), gain, tab, tab, head, head, head, cd],
        out_specs=[out_col, out_col, pl.BlockSpec((1, nck, Hd, Hd), lambda h, s: (h, s, 0, 0))],
        out_shape=[jax.ShapeDtypeStruct((T, G * Hd), BF16), jax.ShapeDtypeStruct((T, G * Hd), F32),
                   jax.ShapeDtypeStruct((G, T // C, Hd, Hd), BF16)],
        scratch_shapes=[pltpu.VMEM((Hd, Hd), F32)],
        args=[proj, proj, proj, proj, ret_norm, cos2, sin2, intra, k_tail, q_head, chunk_decay])


def ret_bwd(proj, db, o_pre, r_prev, ret_norm, tables, dproj, name, hosted=()):
    T = proj.shape[0]
    Hd, C, G = HEAD_DIM, RET_CHUNK, N_GROUPS
    tseg = min(T, 1024)
    nseg, nck = T // tseg, tseg // C
    scale = Hd ** -0.5
    cos2, sin2, intra, k_tail, q_head, chunk_decay = tables

    def body(q_ref, k_ref, v_ref, g_ref, db_ref, o_ref, rp_ref, gain_ref, cos_ref, sin_ref, m_ref, kt_ref, qh_ref, cd_ref,
             _, d_ref, dgain_ref, gstate):
        @pl.when(pl.program_id(1) == 0)
        def _():
            gstate[...] = jnp.zeros_like(gstate)
            dgain_ref[...] = jnp.zeros_like(dgain_ref)

        def chunk(t, carry):
            ci = nck - 1 - t
            rows = pl.ds(pl.multiple_of(ci * C, C), C)
            cos, sin = cos_ref[rows, :], sin_ref[rows, :]
            qr = _rope(q_ref[rows, :], cos, sin)
            kr = _rope(k_ref[rows, :], cos, sin) * scale
            qb, kb, vb = qr.astype(BF16), kr.astype(BF16), v_ref[rows, :].astype(BF16)
            qhb, ktb = (qr * qh_ref[0]).astype(BF16), (kr * kt_ref[0]).astype(BF16)
            sc = (_dot_nt(qb, kb) * m_ref[0]).astype(BF16)
            o = o_ref[rows, :]
            rstd = _rstd(o)
            on = o * rstd
            gain = gain_ref[...]
            silu, dsilu = _silu_parts(g_ref[rows, :])
            dy = db_ref[rows, :]
            dgain_ref[...] += jnp.sum(dy * silu * on, axis=0, keepdims=True)
            dg = dy * on * gain * dsilu
            don = dy * silu * gain
            dob = (rstd * (don - on * jnp.mean(don * on, axis=-1, keepdims=True))).astype(BF16)
            gn = gstate[...]
            gb = gn.astype(BF16)
            da = (_dot_nt(dob, vb) * m_ref[0]).astype(BF16)
            dq = _dot(da, kb) + _dot_nt(dob, rp_ref[0, ci]) * qh_ref[0]
            dk = _dot_tn(da, qb) + _dot_nt(vb, gb) * kt_ref[0]
            dv = _dot_tn(sc, dob) + _dot(ktb, gb)
            gstate[...] = cd_ref[0] * gn + _dot_tn(qhb, dob)
            d_ref[0, rows, :] = _rope_t(dq, cos, sin).astype(BF16)
            d_ref[1, rows, :] = _rope_t(dk * scale, cos, sin).astype(BF16)
            d_ref[2, rows, :] = dv.astype(BF16)
            d_ref[3, rows, :] = dg.astype(BF16)
            return carry

        lax.fori_loop(0, nck, chunk, 0, unroll=True)

    rev = lambda s: nseg - 1 - s
    col, tab, head, cd, gain = _ret_specs(tseg, rev)
    act = pl.BlockSpec((tseg, Hd), lambda h, s: (rev(s), h))
    return _call(
        body, hosted, name=name, grid=(G, nseg),
        in_specs=[col(1), col(2), col(3), col(4), act, act, pl.BlockSpec((1, nck, Hd, Hd), lambda h, s: (h, rev(s), 0, 0)),
                  gain, tab, tab, head, head, head, cd, ANY],
        out_specs=[pl.BlockSpec((4, tseg, Hd), lambda h, s: (0, rev(s), h)), gain],
        out_shape=[jax.ShapeDtypeStruct(dproj.shape, BF16), jax.ShapeDtypeStruct((1, G * Hd), F32)],
        scratch_shapes=[pltpu.VMEM((Hd, Hd), F32)], aliased={14: 0},
        args=[proj, proj, proj, proj, db, o_pre, r_prev, ret_norm, cos2, sin2, intra, k_tail, q_head, chunk_decay, dproj])


def final_loss(h, gain, target, name, hosted=()):
    T, D = h.shape
    tm = min(T, 512)

    def body(h_ref, g_ref, t_ref, dh_ref, loss_ref, dg_ref):
        @pl.when(pl.program_id(0) == 0)
        def _():
            loss_ref[...] = jnp.zeros_like(loss_ref)
            dg_ref[...] = jnp.zeros_like(dg_ref)

        hh = h_ref[...]
        gain_v = g_ref[...]
        err = hh * _rstd(hh) * gain_v - t_ref[...]
        loss_ref[...] += 0.5 * jnp.sum(jnp.mean(err * err, axis=-1, keepdims=True), axis=0, keepdims=True)
        dhn, dg = _rmsnorm_bwd(err * (1.0 / D), hh, gain_v)
        dh_ref[...] = dhn
        dg_ref[...] += jnp.sum(dg, axis=0, keepdims=True)

    row_spec = pl.BlockSpec((tm, D), lambda i: (i, 0))
    vec_spec = pl.BlockSpec((1, D), lambda i: (0, 0))
    return _call(
        body, hosted, name=name, grid=(T // tm,),
        in_specs=[row_spec, vec_spec, row_spec],
        out_specs=[row_spec, pl.BlockSpec((1, 128), lambda i: (0, 0)), vec_spec],
        out_shape=[jax.ShapeDtypeStruct((T, D), F32), jax.ShapeDtypeStruct((1, 128), F32), jax.ShapeDtypeStruct((1, D), F32)],
        args=[h, gain, target])


def prereduce(grads, recvs, place, name):
    nt = len(grads)
    nsh, R, C = grads[0].shape
    rh = R // 2

    def body(place_ref, *refs):
        for t in range(nt):
            g_ref, r_ref, o_ref, own_ref = refs[2 * t], refs[2 * t + 1], refs[2 * nt + 2 * t], refs[2 * nt + 2 * t + 1]
            piece = (g_ref[...].astype(F32) + r_ref[...].astype(F32)).astype(BF16)
            o_ref[...] = piece

            @pl.when(pl.program_id(0) == place_ref[1])
            def _():
                own_ref[...] = piece

    outs = pl.pallas_call(
        body, name=name,
        grid_spec=pltpu.PrefetchScalarGridSpec(
            num_scalar_prefetch=1, grid=(nsh,),
            in_specs=[pl.BlockSpec((1, rh, C), lambda j, p: (j, p[0], 0)), pl.BlockSpec((1, rh, C), lambda j, p: (j, 0, 0))] * nt,
            out_specs=[pl.BlockSpec((1, rh, C), lambda j, p: (j, 0, 0)),
                       pl.BlockSpec((1, rh, C), lambda j, p: (p[1], p[0], 0))] * nt),
        out_shape=[jax.ShapeDtypeStruct((nsh, rh, C), BF16), jax.ShapeDtypeStruct((nsh, R, C), BF16)] * nt,
        compiler_params=pltpu.CompilerParams(vmem_limit_bytes=VMEM_LIMIT_V7X),
    )(place, *[a for pair in zip(grads, recvs) for a in pair])
    return [(outs[2 * t], outs[2 * t + 1]) for t in range(nt)]


def _adamw(w, g, m, v):
    m = ADAM_B1 * m + (1.0 - ADAM_B1) * g
    v = ADAM_B2 * v + (1.0 - ADAM_B2) * (g * g)
    m_hat = m / (1.0 - ADAM_B1 ** ADAM_STEP)
    v_hat = v / (1.0 - ADAM_B2 ** ADAM_STEP)
    return -ADAM_LR * (m_hat / (jnp.sqrt(v_hat) + ADAM_EPS) + ADAM_WD * w), m, v


def adamw_sharded(tensors, name, hosted=()):
    nt = len(tensors)
    nsh, R, C = tensors[0][0].shape
    lanes = -(-C // 128) * 128
    per_row = 2 * nt * lanes * (nsh * 2 + 7 * 4)
    tr = max(r for r in range(16, R + 1, 16) if R % r == 0 and r * per_row <= ADAMW_VMEM_BUDGET)

    def body(*refs):
        ins, outs = refs[:4 * nt], refs[4 * nt:]
        for t in range(nt):
            p_ref, w_ref, m_ref, v_ref = ins[4 * t:4 * t + 4]
            g_ref, d_ref, nm_ref, nv_ref = outs[4 * t:4 * t + 4]
            g = p_ref[0].astype(F32)
            for i in range(1, nsh):
                g += p_ref[i].astype(F32)
            g_ref[...] = g
            d_ref[...], nm_ref[...], nv_ref[...] = _adamw(w_ref[...], g, m_ref[...], v_ref[...])

    spec = pl.BlockSpec((tr, C), lambda i: (i, 0))
    out = jax.ShapeDtypeStruct((R, C), F32)
    return _call(
        body, hosted, name=name, grid=(R // tr,),
        in_specs=[pl.BlockSpec((nsh, tr, C), lambda i: (0, i, 0)), spec, spec, spec] * nt,
        out_specs=[spec] * (4 * nt), out_shape=[out] * (4 * nt),
        args=[a for tensor in tensors for a in tensor])


def adamw_small(packs, late, pool, vectors, name):
    ndev = packs.shape[0]
    rp, rv, rl = pool[0].shape[0], vectors.shape[0] // 3, late.shape[1]

    def body(p_ref, l_ref, wp, mp, vp, wmv, gp, dp, nmp, nvp, gv, dv, nmv, nvv, loss_ref):
        wv, mv, vv = wmv.at[0:rv], wmv.at[rv:2 * rv], wmv.at[2 * rv:3 * rv]
        g, first = p_ref[0], l_ref[0]
        for i in range(1, ndev):
            g += p_ref[i]
            first += l_ref[i]
        g_pool = g[0:rp]
        g_vec = jnp.concatenate([g[rp:rp + rl] + first, g[rp + rl:rp + rv]], axis=0)
        gp[...], gv[...], loss_ref[...] = g_pool, g_vec, g[rp + rv:rp + rv + 8]
        dp[...], nmp[...], nvp[...] = _adamw(wp[...], g_pool, mp[...], vp[...])
        dv[...], nmv[...], nvv[...] = _adamw(wv[...], g_vec, mv[...], vv[...])

    shape = lambda rows: jax.ShapeDtypeStruct((rows, 128), F32)
    outs = pl.pallas_call(body, name=name, out_shape=[shape(rp)] * 4 + [shape(rv)] * 4 + [shape(8)],
                          compiler_params=pltpu.CompilerParams(vmem_limit_bytes=VMEM_LIMIT_V7X))(packs, late, *pool, vectors)
    return outs[0:4], outs[4:8], outs[8]


BIG = ("ffn1_w1", "ffn1_w3", "ffn1_w2", "w_in", "w_out", "ffn2_w1", "ffn2_w3", "ffn2_w2")
TRANSPOSED = ("ffn1_w1", "ffn1_w3", "ffn2_w1", "ffn2_w3")
VECTORS = ("ffn1_norm", "mix_norm", "pool_scale", "ret_norm", "ffn2_norm", "final_norm")
WEIGHTS = ("ffn1_norm", "ffn1_w1", "ffn1_w3", "ffn1_w2", "mix_norm", "w_in", "pool_w", "pool_scale", "ret_norm", "w_out",
           "ffn2_norm", "ffn2_w1", "ffn2_w3", "ffn2_w2", "final_norm")


def _pack_vectors(parts):
    return jnp.concatenate([parts[k].reshape(-1, 128) for k in VECTORS], axis=0)


def _unpack_vectors(pack, like):
    out, row = {}, 0
    for k in VECTORS:
        rows = like[k].size // 128
        out[k] = pack[row:row + rows].reshape(like[k].shape)
        row += rows
    return out


def kernel(x, ffn1_norm, ffn1_w1, ffn1_w3, ffn1_w2, mix_norm, w_in, pool_w, pool_scale, ret_norm, w_out, ffn2_norm, ffn2_w1, ffn2_w3, ffn2_w2, final_norm, loss_target, m_ffn1_norm, m_ffn1_w1, m_ffn1_w3, m_ffn1_w2, m_mix_norm, m_w_in, m_pool_w, m_pool_scale, m_ret_norm, m_w_out, m_ffn2_norm, m_ffn2_w1, m_ffn2_w3, m_ffn2_w2, m_final_norm, v_ffn1_norm, v_ffn1_w1, v_ffn1_w3, v_ffn1_w2, v_mix_norm, v_w_in, v_pool_w, v_pool_scale, v_ret_norm, v_w_out, v_ffn2_norm, v_ffn2_w1, v_ffn2_w3, v_ffn2_w2, v_final_norm):
    w = dict(ffn1_norm=ffn1_norm, ffn1_w1=ffn1_w1, ffn1_w3=ffn1_w3, ffn1_w2=ffn1_w2, mix_norm=mix_norm, w_in=w_in, pool_w=pool_w,
             pool_scale=pool_scale, ret_norm=ret_norm, w_out=w_out, ffn2_norm=ffn2_norm, ffn2_w1=ffn2_w1, ffn2_w3=ffn2_w3,
             ffn2_w2=ffn2_w2, final_norm=final_norm)
    m = dict(ffn1_norm=m_ffn1_norm, ffn1_w1=m_ffn1_w1, ffn1_w3=m_ffn1_w3, ffn1_w2=m_ffn1_w2, mix_norm=m_mix_norm, w_in=m_w_in,
             pool_w=m_pool_w, pool_scale=m_pool_scale, ret_norm=m_ret_norm, w_out=m_w_out, ffn2_norm=m_ffn2_norm, ffn2_w1=m_ffn2_w1,
             ffn2_w3=m_ffn2_w3, ffn2_w2=m_ffn2_w2, final_norm=m_final_norm)
    v = dict(ffn1_norm=v_ffn1_norm, ffn1_w1=v_ffn1_w1, ffn1_w3=v_ffn1_w3, ffn1_w2=v_ffn1_w2, mix_norm=v_mix_norm, w_in=v_w_in,
             pool_w=v_pool_w, pool_scale=v_pool_scale, ret_norm=v_ret_norm, w_out=v_w_out, ffn2_norm=v_ffn2_norm, ffn2_w1=v_ffn2_w1,
             ffn2_w3=v_ffn2_w3, ffn2_w2=v_ffn2_w2, final_norm=v_final_norm)
    xs, target = x[0], loss_target[0]
    T = xs.shape[0]
    tables = _ret_tables(T)
    place = jnp.stack([lax.axis_index("c"), 2 * lax.axis_index("x") + lax.axis_index("y")]).astype(jnp.int32)
    local = lambda d, k: jnp.transpose(d[k][0]) if k in TRANSPOSED else d[k][0]
    result = lambda o, k: jnp.transpose(o)[None] if k in TRANSPOSED else o[None]
    first = ("ffn1_w1", "ffn1_w3")
    sh = {k: local(w, k).astype(BF16) for k in first}
    gather = lambda *names: [ChipExchange([sh[k] for k in names], False)]
    wg, grad, delta, new_m, new_v = {}, {}, {}, {}, {}

    def update(names, pieces, name, hosted=()):
        outs, extras = adamw_sharded([(p, local(w, k), local(m, k), local(v, k)) for k, p in zip(names, pieces)], name, hosted)
        for t, k in enumerate(names):
            grad[k], delta[k], new_m[k], new_v[k] = [result(o, k) for o in outs[4 * t:4 * t + 4]]
        return extras

    def reduce_in_chip(name, *pairs):
        reduced = prereduce([p for p, _ in pairs], [r for _, r in pairs], place, "prereduce_" + name)
        return reduced[0] if len(pairs) == 1 else reduced

    scatter = lambda *reduced: ChipExchange([r[0] for r in reduced], True, [r[1] for r in reduced])
    whole = lambda k: wg[k].reshape(-1, wg[k].shape[-1])
    sharded = lambda g: g.reshape(N_CHIPS, -1, g.shape[-1])

    later = [k for k in BIG if k not in first]
    casts, ((wg["ffn1_w1"], wg["ffn1_w3"]),) = cast_shards([local(w, k) for k in later], "cast_gather_ffn1", gather(*first))
    sh.update(zip(later, casts))
    (n1, ga1, gb1, s1), ((wg["ffn1_w2"], wg["w_in"]),) = ffn_up(
        xs, ffn1_norm, whole("ffn1_w1"), whole("ffn1_w3"), "ffn1_up", gather("ffn1_w2", "w_in"))
    (h1,), ((wg["w_out"],),) = ffn_down(s1, whole("ffn1_w2"), xs, "ffn1_down", gather("w_out"))
    (u, proj), ((wg["ffn2_w1"],),) = mix_in(h1, mix_norm, wg["w_in"], "mix_in", gather("ffn2_w1"))
    (pa,), _ = pool_fwd(proj, pool_w[0], pool_scale, "pool_fwd")
    (rb, o_pre, r_prev), ((wg["ffn2_w3"],),) = ret_fwd(proj, ret_norm, tables, "ret_fwd", gather("ffn2_w3"))
    (h2,), _ = mix_out(pa, rb, wg["w_out"], h1, "mix_out")
    (n2, ga2, gb2, s2), ((wg["ffn2_w2"],),) = ffn_up(
        h2, ffn2_norm, whole("ffn2_w1"), whole("ffn2_w3"), "ffn2_up", gather("ffn2_w2"))
    (h3,), _ = ffn_down(s2, whole("ffn2_w2"), h2, "ffn2_down")
    (dh3, loss, d_final), _ = final_loss(h3, final_norm[None], target, "final_loss")

    (da2, db2, df2), _ = ffn_bwd_act(dh3, whole("ffn2_w2"), ga2, gb2, "ffn2_bwd_act")
    (g_f2w2,), _ = ffn_dw([s2], df2, 1, "ffn2_dw2")
    g_f2w2 = sharded(g_f2w2)
    (g_f2w1, g_f2w3), ((r_f2w2,),) = ffn_dw([da2, db2], n2, 2, "ffn2_dw13", [SiblingExchange([g_f2w2])])
    g_f2w1, g_f2w3 = sharded(g_f2w1), sharded(g_f2w3)
    p_f2w2 = reduce_in_chip("ffn2_w2", (g_f2w2, r_f2w2))
    (dh2, d_ffn2), ((q_f2w2,), (r_f2w1, r_f2w3)) = ffn_bwd_in(
        da2, db2, whole("ffn2_w1"), whole("ffn2_w3"), h2, ffn2_norm, dh3, "ffn2_bwd_in",
        [scatter(p_f2w2), SiblingExchange([g_f2w1, g_f2w3])])
    p_f2w1, p_f2w3 = reduce_in_chip("ffn2_w13", (g_f2w1, r_f2w1), (g_f2w3, r_f2w3))
    (dpa, drb, g_wout), _ = mix_out_bwd(dh2, wg["w_out"], pa, rb, "mix_out_bwd")
    (dproj, d_pool_w, d_pool_scale), _ = pool_bwd(proj, dpa, pool_w[0], pool_scale, "pool_bwd")
    (dproj, d_ret_norm), ((q_f2w1, q_f2w3), (r_wout,)) = ret_bwd(
        proj, drb, o_pre, r_prev, ret_norm, tables, dproj, "ret_bwd", [scatter(p_f2w1, p_f2w3), SiblingExchange([g_wout])])
    p_wout = reduce_in_chip("w_out", (g_wout, r_wout))
    (g_win,), ((q_wout,),) = mix_dwin(u, dproj, N_CHIPS, "mix_dwin", [scatter(p_wout)])
    (dh1, d_mix), ((r_win,),) = mix_in_bwd(dproj, wg["w_in"], h1, mix_norm, dh2, "mix_in_bwd", [SiblingExchange([g_win])])
    p_win = reduce_in_chip("w_in", (g_win, r_win))
    (da1, db1, df1), ((q_win,),) = ffn_bwd_act(dh1, whole("ffn1_w2"), ga1, gb1, "ffn1_bwd_act", [scatter(p_win)])
    d_vectors = {"ffn1_norm": jnp.zeros_like(ffn1_norm), "mix_norm": d_mix, "pool_scale": d_pool_scale,
                 "ret_norm": d_ret_norm, "ffn2_norm": d_ffn2, "final_norm": d_final}
    pack = jnp.concatenate([d_pool_w.reshape(-1, 128), _pack_vectors(d_vectors), jnp.broadcast_to(loss, (8, 128))], axis=0)
    (g_f1w1, g_f1w3), ((packs,),) = ffn_dw([da1, db1], n1, 2, "ffn1_dw13", [AllExchange(pack)])
    g_f1w1, g_f1w3 = sharded(g_f1w1), sharded(g_f1w3)
    (g_f1w2,), ((r_f1w1, r_f1w3),) = ffn_dw([s1], df1, 1, "ffn1_dw2", [SiblingExchange([g_f1w1, g_f1w3])])
    g_f1w2 = sharded(g_f1w2)
    p_f1w1, p_f1w3 = reduce_in_chip("ffn1_w13", (g_f1w1, r_f1w1), (g_f1w3, r_f1w3))
    (dx, d_ffn1), ((q_f1w1, q_f1w3), (r_f1w2,)) = ffn_bwd_in(
        da1, db1, whole("ffn1_w1"), whole("ffn1_w3"), xs, ffn1_norm, dh1, "ffn1_bwd_in",
        [scatter(p_f1w1, p_f1w3), SiblingExchange([g_f1w2])])
    p_f1w2 = reduce_in_chip("ffn1_w2", (g_f1w2, r_f1w2))

    (q_f1w2,), (late,) = update(["ffn2_w1", "ffn2_w3", "ffn1_w1", "ffn1_w3"], [q_f2w1, q_f2w3, q_f1w1, q_f1w3], "adamw_w13",
                                [scatter(p_f1w2), AllExchange(d_ffn1.reshape(-1, 128))])
    update(["ffn2_w2", "ffn1_w2"], [q_f2w2, q_f1w2], "adamw_w2")
    update(["w_in"], [q_win], "adamw_w_in")
    update(["w_out"], [q_wout], "adamw_w_out")
    of_pool, of_vectors, loss_sum = adamw_small(packs, late, [t["pool_w"].reshape(-1, 128) for t in (w, m, v)],
                                                jnp.concatenate([t[k].reshape(-1, 128) for t in (w, m, v) for k in VECTORS], axis=0),
                                                "adamw_small")
    for res, pool_part, vector_part in zip((grad, delta, new_m, new_v), of_pool, of_vectors):
        res["pool_w"] = pool_part.reshape(pool_w.shape)
        res.update(_unpack_vectors(vector_part, w))
    loss = loss_sum[0, 0]

    return (loss, dx[None], *[grad[k] for k in WEIGHTS], *[delta[k] for k in WEIGHTS],
            *[new_m[k] for k in WEIGHTS], *[new_v[k] for k in WEIGHTS])
```

```python
import math

import jax
import jax.numpy as jnp
from jax import lax
from jax.experimental import pallas as pl
from jax.experimental.pallas import tpu as pltpu

F32 = jnp.float32
BF16 = jnp.bfloat16

EPS = 1e-6
N_CHIPS = 4
N_GROUPS = 4
HEAD_DIM = 128
RET_CHUNK = 128
ROPE_BASE = 10000.0
ADAM_LR, ADAM_B1, ADAM_B2, ADAM_EPS, ADAM_WD, ADAM_STEP = 0.001, 0.9, 0.999, 1e-08, 0.01, 10
VMEM_LIMIT_V7X = 56 * 1024 * 1024
ADAMW_VMEM_BUDGET = 32 * 1024 * 1024
MESH = pl.DeviceIdType.MESH
ANY = pl.BlockSpec(memory_space=pl.ANY)


def _dot(a, b):
    return jnp.dot(a, b, preferred_element_type=F32)


def _dot_nt(a, b):
    return lax.dot_general(a, b, (((1,), (1,)), ((), ())), preferred_element_type=F32)


def _dot_tn(a, b):
    return lax.dot_general(a, b, (((0,), (0,)), ((), ())), preferred_element_type=F32)


def _rstd(h):
    return lax.rsqrt(jnp.mean(h * h, axis=-1, keepdims=True) + EPS)


def _rmsnorm_bwd(dn, h, gain):
    r = _rstd(h)
    nh = h * r
    dnh = dn * gain
    dh = r * (dnh - nh * jnp.mean(dnh * nh, axis=-1, keepdims=True))
    return dh, dn * nh


def _silu_parts(a):
    sig = jax.nn.sigmoid(a)
    silu = a * sig
    return silu, sig + silu * (1.0 - sig)


def _mesh_pos():
    return lax.axis_index("x"), lax.axis_index("y"), lax.axis_index("c")


class ChipExchange:
    def __init__(self, srcs, scatter, placed=()):
        n = len(srcs)
        self.inputs, self.scatter, self.n, self.reach = list(srcs) + list(placed), scatter, n, REACH_CHIPS
        self.aliases = {n + t: t for t in range(n)} if scatter else {}
        self.half_rows = [s.shape[1] if scatter else s.shape[0] // 2 for s in srcs]
        self.out_shape = [jax.ShapeDtypeStruct((N_CHIPS, 2 * rh, s.shape[-1]), s.dtype) for s, rh in zip(srcs, self.half_rows)]
        if scatter:
            self.out_shape += [jax.ShapeDtypeStruct((2, rh // 2, s.shape[-1]), s.dtype) for s, rh in zip(srcs, self.half_rows)]
        dma = pltpu.SemaphoreType.DMA
        self.sems = [dma((4 * n,)), dma((4 * n,)), dma((2 * n,)), dma((2 * n,)), dma((4 * n,)), dma((4 * n,))]

    def _copies(self, src, out, sems):
        hop1_send, hop1_recv, hop2_send, hop2_recv, d2d_send, d2d_recv = sems
        x, y, c = _mesh_pos()
        me, dg = 2 * x + y, 2 * (1 - x) + (1 - y)
        sibling = (x, y, 1 - c)
        n = self.n
        mine, theirs = c, 1 - c

        def nb(a):
            nx, ny = x ^ (1 - a), y ^ a
            return 2 * nx + ny, (nx, ny, c)

        def remote(s, d, send, recv, k, to):
            return pltpu.make_async_remote_copy(src_ref=s, dst_ref=d, send_sem=send.at[k], recv_sem=recv.at[k],
                                                device_id=to, device_id_type=MESH)

        class Copies:
            def slot(_, t, chip, half):
                rh = self.half_rows[t]
                return out[t].at[chip, pl.ds(half * rh, rh), :]

            def quarter(_, t, chip, q):
                qh = self.half_rows[t] // 2
                return out[t].at[chip, pl.ds(mine * 2 * qh + q * qh, qh), :]

            def own_shard(k, t):
                return remote(src[t], out[t].at[me], d2d_send, d2d_recv, 4 * t + 3, sibling)

            def hop1(k, t, a, transit=False):
                rh = self.half_rows[t]
                chip, to = nb(a)
                if transit:
                    piece = src[t].at[dg, pl.ds(a * (rh // 2), rh // 2), :]
                    return remote(piece, out[n + t].at[a], hop1_send, hop1_recv, 4 * t + 2 + a, to)
                piece = src[t].at[chip] if self.scatter else src[t].at[pl.ds(mine * rh, rh), :]
                return remote(piece, k.slot(t, me, mine), hop1_send, hop1_recv, 4 * t + a, to)

            def landed1(k, t, a, transit=False):
                here = out[n + t].at[a] if transit else k.slot(t, nb(a)[0], mine)
                return remote(here, here, hop1_send, hop1_recv, 4 * t + (2 if transit else 0) + a, sibling)

            def hop2(k, t, q):
                origin, to = nb(q)[0], nb(1 - q)[1]
                piece = out[n + t].at[q] if self.scatter else k.quarter(t, origin, q)
                return remote(piece, k.quarter(t, origin, q), hop2_send, hop2_recv, 2 * t + q, to)

            def landed2(k, t, q):
                here = k.quarter(t, dg, q)
                return remote(here, here, hop2_send, hop2_recv, 2 * t + q, sibling)

            def d2d(k, t, p, chip, own=False, arriving=False):
                if arriving:
                    there = k.slot(t, chip, theirs)
                    return remote(there, there, d2d_send, d2d_recv, 4 * t + p, sibling)
                piece = src[t].at[me] if own else k.slot(t, chip, mine)
                return remote(piece, k.slot(t, chip, mine), d2d_send, d2d_recv, 4 * t + p, sibling)

        return Copies(), nb, me, dg, c

    def start(self, src, out, sems):
        k, nb, me, dg, c = self._copies(src, out, sems)
        for t in range(self.n):
            for first in range(2):
                a = first ^ c
                k.hop1(t, a).start()
                if self.scatter:
                    k.hop1(t, a, transit=True).start()
            if self.scatter:
                k.d2d(t, 3, me, own=True).start()
            else:
                k.own_shard(t).start()

    def mid(self, src, out, sems):
        k, nb, me, dg, c = self._copies(src, out, sems)
        for t in range(self.n):
            for first in range(2):
                a = first ^ c
                if self.scatter:
                    k.landed1(t, a, transit=True).wait_recv()
                    k.hop2(t, a).start()
                k.landed1(t, a).wait_recv()
                if not self.scatter:
                    k.hop2(t, a).start()
                k.d2d(t, a, nb(a)[0]).start()

    def finish(self, src, out, sems):
        k, nb, me, dg, c = self._copies(src, out, sems)
        for t in range(self.n):
            for q in range(2):
                k.landed2(t, q).wait_recv()
            k.d2d(t, 2, dg).start()
        for t in range(self.n):
            for a in range(2):
                k.d2d(t, a, nb(a)[0], arriving=True).wait_recv()
            k.d2d(t, 2, dg, arriving=True).wait_recv()
            if self.scatter:
                k.d2d(t, 3, me, arriving=True).wait_recv()
        for t in range(self.n):
            for a in range(2):
                k.hop1(t, a).wait_send()
                if self.scatter:
                    k.hop1(t, a, transit=True).wait_send()
                k.hop2(t, a).wait_send()
                k.d2d(t, a, nb(a)[0]).wait_send()
            k.d2d(t, 2, dg).wait_send()
            if self.scatter:
                k.d2d(t, 3, me, own=True).wait_send()
            else:
                k.own_shard(t).wait()


class SiblingExchange:
    def __init__(self, grads):
        self.inputs, self.n, self.aliases, self.reach = list(grads), len(grads), {}, REACH_SIBLING
        self.half_rows = [g.shape[1] // 2 for g in grads]
        self.out_shape = [jax.ShapeDtypeStruct((g.shape[0], rh, g.shape[2]), g.dtype) for g, rh in zip(grads, self.half_rows)]
        self.sems = [pltpu.SemaphoreType.DMA((self.n,)), pltpu.SemaphoreType.DMA((self.n,))]

    def _plan(self, src, out, sems):
        x, y, c = _mesh_pos()
        return [pltpu.make_async_remote_copy(
            src_ref=src[t].at[:, pl.ds((1 - c) * self.half_rows[t], self.half_rows[t]), :], dst_ref=out[t],
            send_sem=sems[0].at[t], recv_sem=sems[1].at[t], device_id=(x, y, 1 - c), device_id_type=MESH) for t in range(self.n)]

    def start(self, src, out, sems):
        for cp in self._plan(src, out, sems):
            cp.start()

    def mid(self, src, out, sems):
        pass

    def finish(self, src, out, sems):
        for cp in self._plan(src, out, sems):
            cp.wait()


REACH_SIBLING, REACH_CHIPS, REACH_ALL = 0, 1, 2


def _entry_barrier(reach):
    x, y, c = _mesh_pos()
    peers = [(x, y, 1 - c)]
    if reach == REACH_CHIPS:
        peers += [(1 - x, y, c), (x, 1 - y, c)]
    elif reach == REACH_ALL:
        peers = [(x ^ dx, y ^ dy, c ^ dc) for dx in (0, 1) for dy in (0, 1) for dc in (0, 1)][1:]
    barrier = pltpu.get_barrier_semaphore()
    for peer in peers:
        pl.semaphore_signal(barrier, inc=1, device_id=peer, device_id_type=MESH)
    pl.semaphore_wait(barrier, len(peers))


def _call(body, hosted=(), *, name, in_specs, out_specs, out_shape, args, grid=(), scratch_shapes=(), aliased=None):
    n_in, n_out, n_scr = len(in_specs), len(out_specs), len(scratch_shapes)
    total = math.prod(grid)
    mid_step = max(0, (5 * total) // 8 - 1)

    def full(*refs):
        pos = [0]

        def take(k):
            pos[0] += k
            return refs[pos[0] - k:pos[0]]

        ins, h_in = take(n_in), [take(len(h.inputs)) for h in hosted]
        outs, h_out = take(n_out), [take(len(h.out_shape)) for h in hosted]
        scr, h_sem = take(n_scr), [take(len(h.sems)) for h in hosted]
        step = 0
        for axis, size in enumerate(grid):
            step = step * size + pl.program_id(axis)

        def phase(at, method):
            if not hosted:
                return

            def run():
                if method == "start":
                    _entry_barrier(reach)
                for h, s, o, m in zip(hosted, h_in, h_out, h_sem):
                    getattr(h, method)(s, o, m)

            if total == 1:
                run()
            else:
                pl.when(step == at)(run)

        phase(0, "start")
        body(*ins, *outs, *scr)
        phase(mid_step, "mid")
        phase(total - 1, "finish")

    aliases, i0, o0 = dict(aliased or {}), n_in, n_out
    for h in hosted:
        aliases.update({i0 + i: o0 + o for i, o in h.aliases.items()})
        i0, o0 = i0 + len(h.inputs), o0 + len(h.out_shape)
    reach = max((h.reach for h in hosted), default=None)
    params = dict(vmem_limit_bytes=VMEM_LIMIT_V7X)
    if hosted:
        params["collective_id"] = reach
    results = pl.pallas_call(
        full, name=name, grid=grid,
        in_specs=list(in_specs) + [ANY] * (i0 - n_in),
        out_specs=list(out_specs) + [ANY] * (o0 - n_out),
        out_shape=list(out_shape) + [s for h in hosted for s in h.out_shape],
        scratch_shapes=list(scratch_shapes) + [s for h in hosted for s in h.sems],
        input_output_aliases=aliases,
        compiler_params=pltpu.CompilerParams(**params),
    )(*args, *[s for h in hosted for s in h.inputs])
    outs, extras, pos = list(results[:n_out]), [], n_out
    for h in hosted:
        extras.append(list(results[pos:pos + h.n]))
        pos += len(h.out_shape)
    return outs, extras


def cast_shards(shards, name, hosted=()):
    n = len(shards)

    def body(*refs):
        for x_ref, o_ref in zip(refs[:n], refs[n:]):
            o_ref[...] = x_ref[...].astype(BF16)

    whole = lambda s: pl.BlockSpec(s.shape, lambda: (0,) * s.ndim)
    return _call(body, hosted, name=name, in_specs=[whole(s) for s in shards], out_specs=[whole(s) for s in shards],
                 out_shape=[jax.ShapeDtypeStruct(s.shape, BF16) for s in shards], args=list(shards))


class AllExchange:
    def __init__(self, pack):
        self.inputs, self.n, self.aliases, self.reach = [pack], 1, {}, REACH_ALL
        self.out_shape = [jax.ShapeDtypeStruct((2 * N_CHIPS,) + pack.shape, pack.dtype)]
        self.sems = [pltpu.SemaphoreType.DMA, pltpu.SemaphoreType.DMA((7,)), pltpu.SemaphoreType.DMA((7,))]

    def _copies(self, src, out, sems):
        local_sem, send_sem, recv_sem = sems
        x, y, c = _mesh_pos()
        flips = [(dx, dy, dc) for dx in (0, 1) for dy in (0, 1) for dc in (0, 1)][1:]
        peers = [(x ^ dx, y ^ dy, c ^ dc) for dx, dy, dc in flips]
        remote = lambda s, d, k: pltpu.make_async_remote_copy(
            src_ref=s, dst_ref=d, send_sem=send_sem.at[k], recv_sem=recv_sem.at[k], device_id=peers[k], device_id_type=MESH)
        sends = [remote(src[0], out[0].at[4 * x + 2 * y + c], k) for k in range(7)]
        landed = [remote(out[0].at[4 * px + 2 * py + pc], out[0].at[4 * px + 2 * py + pc], k) for k, (px, py, pc) in enumerate(peers)]
        return sends, landed, pltpu.make_async_copy(src[0], out[0].at[4 * x + 2 * y + c], local_sem)

    def start(self, src, out, sems):
        sends, _, local = self._copies(src, out, sems)
        for cp in sends:
            cp.start()
        local.start()

    def mid(self, src, out, sems):
        pass

    def finish(self, src, out, sems):
        sends, landed, local = self._copies(src, out, sems)
        for cp in landed:
            cp.wait_recv()
        for cp in sends:
            cp.wait_send()
        local.wait()


MXU_COLS = 256


class _Streamed:
    def __init__(self, hbm, vmem, sems):
        self.hbm, self.vmem, self.sems = hbm, vmem, sems
        self.first = pl.program_id(0) == 0
        self.chunks = range(0, vmem.shape[0], MXU_COLS)

    @staticmethod
    def scratch(shape):
        return [pltpu.VMEM(shape, BF16), pltpu.SemaphoreType.DMA((shape[0] // MXU_COLS,))]

    def _copy(self, c):
        rows = pl.ds(c, MXU_COLS)
        return pltpu.make_async_copy(self.hbm.at[rows, :], self.vmem.at[rows, :], self.sems.at[c // MXU_COLS])

    def start(self):
        @pl.when(self.first)
        def _():
            for c in self.chunks:
                self._copy(c).start()

    def rows(self, c):
        pl.when(self.first)(lambda: self._copy(c).wait())
        return self.vmem[c:c + MXU_COLS, :]

    def all(self):
        @pl.when(self.first)
        def _():
            for c in self.chunks:
                self._copy(c).wait()
        return self.vmem[...]


def ffn_up(h, gain, w1, w3, name, hosted=()):
    T, D = h.shape
    F = w1.shape[0]
    tm = min(T, 512)

    def body(h_ref, g_ref, w1_hbm, w3_hbm, n_ref, ga_ref, gb_ref, s_ref, *scratch):
        w1, w3 = _Streamed(w1_hbm, *scratch[0:2]), _Streamed(w3_hbm, *scratch[2:4])
        w1.start()
        w3.start()
        hh = h_ref[...]
        n = (hh * _rstd(hh) * g_ref[...]).astype(BF16)
        n_ref[...] = n
        for c in range(0, F, MXU_COLS):
            cols = slice(c, c + MXU_COLS)
            a = _dot_nt(n, w1.rows(c))
            b = _dot_nt(n, w3.rows(c))
            silu, dsilu = _silu_parts(a)
            ga_ref[:, cols] = (b * dsilu).astype(BF16)
            gb_ref[:, cols] = silu.astype(BF16)
            s_ref[:, cols] = (silu * b).astype(BF16)

    act = jax.ShapeDtypeStruct((T, F), BF16)
    act_spec = pl.BlockSpec((tm, F), lambda i: (i, 0))
    row_spec = pl.BlockSpec((tm, D), lambda i: (i, 0))
    return _call(
        body, hosted, name=name, grid=(T // tm,),
        in_specs=[row_spec, pl.BlockSpec((1, D), lambda i: (0, 0)), ANY, ANY],
        out_specs=[row_spec, act_spec, act_spec, act_spec],
        out_shape=[jax.ShapeDtypeStruct((T, D), BF16), act, act, act],
        scratch_shapes=_Streamed.scratch((F, D)) * 2,
        args=[h, gain, w1, w3])


def ffn_down(s, w2, h, name, hosted=()):
    T, F = s.shape
    D = h.shape[1]
    tm = min(T, 512)

    def body(s_ref, w2_hbm, h_ref, o_ref, *scratch):
        w2 = _Streamed(w2_hbm, *scratch)
        w2.start()
        o_ref[...] = h_ref[...] + 0.5 * _dot(s_ref[...], w2.all())

    row_spec = pl.BlockSpec((tm, D), lambda i: (i, 0))
    return _call(
        body, hosted, name=name, grid=(T // tm,),
        in_specs=[pl.BlockSpec((tm, F), lambda i: (i, 0)), ANY, row_spec],
        out_specs=[row_spec],
        out_shape=[jax.ShapeDtypeStruct((T, D), F32)],
        scratch_shapes=_Streamed.scratch((F, D)),
        args=[s, w2, h])


def ffn_bwd_act(dh, w2, ga, gb, name, hosted=()):
    T, D = dh.shape
    F = w2.shape[0]
    tm = min(T, 512)

    def body(dh_ref, w2_hbm, ga_ref, gb_ref, da_ref, db_ref, df_ref, *scratch):
        w2 = _Streamed(w2_hbm, *scratch)
        w2.start()
        df = (0.5 * dh_ref[...]).astype(BF16)
        df_ref[...] = df
        for c in range(0, F, MXU_COLS):
            cols = slice(c, c + MXU_COLS)
            ds = _dot_nt(df, w2.rows(c))
            da_ref[:, cols] = (ds * ga_ref[:, cols].astype(F32)).astype(BF16)
            db_ref[:, cols] = (ds * gb_ref[:, cols].astype(F32)).astype(BF16)

    act = jax.ShapeDtypeStruct((T, F), BF16)
    act_spec = pl.BlockSpec((tm, F), lambda i: (i, 0))
    row_spec = pl.BlockSpec((tm, D), lambda i: (i, 0))
    return _call(
        body, hosted, name=name, grid=(T // tm,),
        in_specs=[row_spec, ANY, act_spec, act_spec],
        out_specs=[act_spec, act_spec, row_spec],
        out_shape=[act, act, jax.ShapeDtypeStruct((T, D), BF16)],
        scratch_shapes=_Streamed.scratch((F, D)),
        args=[dh, w2, ga, gb])


def ffn_dw(xs, y, halves, name, hosted=()):
    T, F = xs[0].shape
    D = y.shape[1]
    nx, fh = len(xs), F // halves
    tk = min(T, 512)
    nk = T // tk

    def body(*refs):
        y_ref, x_refs, o_refs, accs = refs[0], refs[1:1 + nx], refs[1 + nx:1 + 2 * nx], refs[1 + 2 * nx:]
        k = pl.program_id(1)

        @pl.when(k == 0)
        def _():
            for acc in accs:
                acc[...] = jnp.zeros_like(acc)

        yy = y_ref[...]
        for x_ref, acc in zip(x_refs, accs):
            acc[...] += _dot_tn(x_ref[...], yy)

        @pl.when(k == nk - 1)
        def _():
            for o_ref, acc in zip(o_refs, accs):
                o_ref[...] = acc[...].astype(BF16)

    out = jax.ShapeDtypeStruct((F, D), BF16)
    return _call(
        body, hosted, name=name, grid=(halves, nk),
        in_specs=[pl.BlockSpec((tk, D), lambda j, k: (k, 0))] + [pl.BlockSpec((tk, fh), lambda j, k: (k, j))] * nx,
        out_specs=[pl.BlockSpec((fh, D), lambda j, k: (j, 0))] * nx,
        out_shape=[out] * nx,
        scratch_shapes=[pltpu.VMEM((fh, D), F32)] * nx,
        args=[y] + list(xs))


def ffn_bwd_in(da, db, w1, w3, h, gain, dh, name, hosted=()):
    T, F = da.shape
    D = h.shape[1]
    tm = min(T, 512)

    def body(da_ref, db_ref, w1_hbm, w3_hbm, h_ref, g_ref, dh_ref, o_ref, dg_ref, *scratch):
        w1, w3 = _Streamed(w1_hbm, *scratch[0:2]), _Streamed(w3_hbm, *scratch[2:4])
        w1.start()
        w3.start()
        dn = _dot(da_ref[...], w1.all()) + _dot(db_ref[...], w3.all())
        dhn, dg = _rmsnorm_bwd(dn, h_ref[...], g_ref[...])
        o_ref[...] = dh_ref[...] + dhn

        @pl.when(pl.program_id(0) == 0)
        def _():
            dg_ref[...] = jnp.zeros_like(dg_ref)

        dg_ref[...] += jnp.sum(dg, axis=0, keepdims=True)

    act_spec = pl.BlockSpec((tm, F), lambda i: (i, 0))
    row_spec = pl.BlockSpec((tm, D), lambda i: (i, 0))
    vec_spec = pl.BlockSpec((1, D), lambda i: (0, 0))
    return _call(
        body, hosted, name=name, grid=(T // tm,),
        in_specs=[act_spec, act_spec, ANY, ANY, row_spec, vec_spec, row_spec],
        out_specs=[row_spec, vec_spec],
        out_shape=[jax.ShapeDtypeStruct((T, D), F32), jax.ShapeDtypeStruct((1, D), F32)],
        scratch_shapes=_Streamed.scratch((F, D)) * 2,
        args=[da, db, w1, w3, h, gain, dh])


def mix_in(h, gain, wing, name, hosted=()):
    T, D = h.shape
    nsh, _, Cs = wing.shape
    tm = min(T, 512)

    def body(h_ref, g_ref, w_ref, u_ref, p_ref):
        hh = h_ref[...]
        u = (hh * _rstd(hh) * g_ref[...]).astype(BF16)
        u_ref[...] = u
        for j in range(nsh):
            p_ref[:, j * Cs:(j + 1) * Cs] = _dot(u, w_ref[j])

    return _call(
        body, hosted, name=name, grid=(T // tm,),
        in_specs=[pl.BlockSpec((tm, D), lambda i: (i, 0)), pl.BlockSpec((1, D), lambda i: (0, 0)),
                  pl.BlockSpec((nsh, D, Cs), lambda i: (0, 0, 0))],
        out_specs=[pl.BlockSpec((tm, D), lambda i: (i, 0)), pl.BlockSpec((tm, nsh * Cs), lambda i: (i, 0))],
        out_shape=[jax.ShapeDtypeStruct((T, D), BF16), jax.ShapeDtypeStruct((T, nsh * Cs), F32)],
        args=[h, gain, wing])


def mix_out(a, b, woutg, h, name, hosted=()):
    T, W = a.shape
    D = h.shape[1]
    wout = woutg.reshape(2, W, D)
    tm = min(T, 512)

    def body(a_ref, b_ref, w_ref, h_ref, o_ref):
        o_ref[...] = h_ref[...] + _dot(a_ref[...], w_ref[0]) + _dot(b_ref[...], w_ref[1])

    return _call(
        body, hosted, name=name, grid=(T // tm,),
        in_specs=[pl.BlockSpec((tm, W), lambda i: (i, 0)), pl.BlockSpec((tm, W), lambda i: (i, 0)),
                  pl.BlockSpec((2, W, D), lambda i: (0, 0, 0)), pl.BlockSpec((tm, D), lambda i: (i, 0))],
        out_specs=[pl.BlockSpec((tm, D), lambda i: (i, 0))],
        out_shape=[jax.ShapeDtypeStruct((T, D), F32)],
        args=[a, b, wout, h])


def mix_out_bwd(dh, woutg, a, b, name, hosted=()):
    T, D = dh.shape
    W = a.shape[1]
    nsh, Rs, _ = woutg.shape
    wout = woutg.reshape(2, W, D)
    tk = min(T, 512)
    nk = T // tk

    def body(dh_ref, w_ref, a_ref, b_ref, da_ref, db_ref, dw_ref, acc):
        k = pl.program_id(0)

        @pl.when(k == 0)
        def _():
            acc[...] = jnp.zeros_like(acc)

        dhb = dh_ref[...].astype(BF16)
        da_ref[...] = _dot_nt(dhb, w_ref[0])
        db_ref[...] = _dot_nt(dhb, w_ref[1])
        acc[0:W, :] += _dot_tn(a_ref[...], dhb)
        acc[W:2 * W, :] += _dot_tn(b_ref[...], dhb)

        @pl.when(k == nk - 1)
        def _():
            for j in range(nsh):
                dw_ref[j] = acc[j * Rs:(j + 1) * Rs, :].astype(BF16)

    return _call(
        body, hosted, name=name, grid=(nk,),
        in_specs=[pl.BlockSpec((tk, D), lambda k: (k, 0)), pl.BlockSpec((2, W, D), lambda k: (0, 0, 0)),
                  pl.BlockSpec((tk, W), lambda k: (k, 0)), pl.BlockSpec((tk, W), lambda k: (k, 0))],
        out_specs=[pl.BlockSpec((tk, W), lambda k: (k, 0)), pl.BlockSpec((tk, W), lambda k: (k, 0)),
                   pl.BlockSpec((nsh, Rs, D), lambda k: (0, 0, 0))],
        out_shape=[jax.ShapeDtypeStruct((T, W), F32), jax.ShapeDtypeStruct((T, W), F32),
                   jax.ShapeDtypeStruct((nsh, Rs, D), BF16)],
        scratch_shapes=[pltpu.VMEM((2 * W, D), F32)],
        args=[dh, wout, a, b])


def _dproj_block(g):
    return (g // N_GROUPS + N_GROUPS) % (N_GROUPS + 1), g % N_GROUPS


def mix_dwin(u, dproj, nsh, name, hosted=()):
    T, D = u.shape
    Hd = HEAD_DIM
    slabs, _, width = dproj.shape
    blocks = slabs * width // Hd
    Cs = blocks * Hd // nsh
    tk = min(T, 512)
    nk = T // tk

    def body(u_ref, d_ref, o_ref, acc):
        k = pl.program_id(0)

        @pl.when(k == 0)
        def _():
            acc[...] = jnp.zeros_like(acc)

        where = [_dproj_block(g) for g in range(blocks)]
        d = jnp.concatenate([d_ref[slab, :, col * Hd:(col + 1) * Hd] for slab, col in where], axis=1)
        acc[...] += _dot_tn(u_ref[...], d)

        @pl.when(k == nk - 1)
        def _():
            for j in range(nsh):
                o_ref[j] = acc[:, j * Cs:(j + 1) * Cs].astype(BF16)

    return _call(
        body, hosted, name=name, grid=(nk,),
        in_specs=[pl.BlockSpec((tk, D), lambda k: (k, 0)), pl.BlockSpec((slabs, tk, width), lambda k: (0, k, 0))],
        out_specs=[pl.BlockSpec((nsh, D, Cs), lambda k: (0, 0, 0))],
        out_shape=[jax.ShapeDtypeStruct((nsh, D, Cs), BF16)],
        scratch_shapes=[pltpu.VMEM((D, blocks * Hd), F32)],
        args=[u, dproj])


def mix_in_bwd(dproj, wing, h, gain, dh, name, hosted=()):
    T, D = h.shape
    nsh, _, Cs = wing.shape
    Hd = HEAD_DIM
    per = Cs // Hd
    tm = min(T, 512)

    def body(d_ref, w_ref, h_ref, g_ref, dh_ref, o_ref, dg_ref):
        def shard(j):
            blocks = [_dproj_block(per * j + i) for i in range(per)]
            return jnp.concatenate([d_ref[slab, :, col * Hd:(col + 1) * Hd] for slab, col in blocks], axis=1)

        du = _dot_nt(shard(0), w_ref[0])
        for j in range(1, nsh):
            du += _dot_nt(shard(j), w_ref[j])
        dhn, dg = _rmsnorm_bwd(du, h_ref[...], g_ref[...])
        o_ref[...] = dh_ref[...] + dhn

        @pl.when(pl.program_id(0) == 0)
        def _():
            dg_ref[...] = jnp.zeros_like(dg_ref)

        dg_ref[...] += jnp.sum(dg, axis=0, keepdims=True)

    row_spec = pl.BlockSpec((tm, D), lambda i: (i, 0))
    vec_spec = pl.BlockSpec((1, D), lambda i: (0, 0))
    return _call(
        body, hosted, name=name, grid=(T // tm,),
        in_specs=[pl.BlockSpec((dproj.shape[0], tm, dproj.shape[2]), lambda i: (0, i, 0)),
                  pl.BlockSpec((nsh, D, Cs), lambda i: (0, 0, 0)), row_spec, vec_spec, row_spec],
        out_specs=[row_spec, vec_spec],
        out_shape=[jax.ShapeDtypeStruct((T, D), F32), jax.ShapeDtypeStruct((1, D), F32)],
        args=[dproj, wing, h, gain, dh])


def _pool_window(x, group, T, trailing):
    rows = lax.broadcasted_iota(jnp.int32, x.shape, 0)

    def shifted(z, k):
        if trailing:
            return jnp.where(rows >= k, pltpu.roll(z, k, 0), 0.0)
        return jnp.where(rows < T - k, pltpu.roll(z, T - k, 0), 0.0)

    s2 = x + shifted(x, 1)
    s4 = s2 + shifted(s2, 2)
    s8 = s4 + shifted(s4, 4)
    s16 = s8 + shifted(s8, 8)
    return jnp.where(group == 0, s2, jnp.where(group == 1, s4, jnp.where(group == 2, s8, s16)))


def _pool_count(group, shape):
    rows = lax.broadcasted_iota(jnp.int32, shape, 0)
    w = jnp.where(group == 0, 2, jnp.where(group == 1, 4, jnp.where(group == 2, 8, 16)))
    return jnp.minimum(rows + 1, w).astype(F32)


def pool_fwd(proj, pool_w, pool_scale, name, hosted=()):
    T = proj.shape[0]
    Hd = HEAD_DIM

    def body(x_ref, w_ref, sc_ref, a_ref):
        g = pl.program_id(0)
        x = x_ref[...]
        pooled = _pool_window(x, g, T, True) / _pool_count(g, x.shape) - x
        a_ref[...] = (_dot(pooled.astype(BF16), w_ref[0].astype(BF16)) * sc_ref[...]).astype(BF16)

    return _call(
        body, hosted, name=name, grid=(N_GROUPS,),
        in_specs=[pl.BlockSpec((T, Hd), lambda g: (0, g)), pl.BlockSpec((1, Hd, Hd), lambda g: (g, 0, 0)),
                  pl.BlockSpec((1, Hd), lambda g: (0, g))],
        out_specs=[pl.BlockSpec((T, Hd), lambda g: (0, g))],
        out_shape=[jax.ShapeDtypeStruct((T, N_GROUPS * Hd), BF16)],
        args=[proj, pool_w, pool_scale])


def pool_bwd(proj, da, pool_w, pool_scale, name, hosted=()):
    T = proj.shape[0]
    Hd = HEAD_DIM

    def body(x_ref, da_ref, w_ref, sc_ref, dx_ref, dw_ref, dsc_ref):
        g = pl.program_id(0)
        x = x_ref[...]
        cnt = _pool_count(g, x.shape)
        pooled = (_pool_window(x, g, T, True) / cnt - x).astype(BF16)
        wb = w_ref[0].astype(BF16)
        dav = da_ref[...]
        dsc_ref[...] = jnp.sum(dav * _dot(pooled, wb), axis=0, keepdims=True)
        dout = (dav * sc_ref[...]).astype(BF16)
        dw_ref[0] = _dot_tn(pooled, dout)
        dpooled = _dot_nt(dout, wb)
        dx_ref[0] = (_pool_window(dpooled / cnt, g, T, False) - dpooled).astype(BF16)

    col_spec = pl.BlockSpec((T, Hd), lambda g: (0, g))
    return _call(
        body, hosted, name=name, grid=(N_GROUPS,),
        in_specs=[col_spec, col_spec, pl.BlockSpec((1, Hd, Hd), lambda g: (g, 0, 0)), pl.BlockSpec((1, Hd), lambda g: (0, g))],
        out_specs=[pl.BlockSpec((1, T, Hd), lambda g: (N_GROUPS, 0, g)), pl.BlockSpec((1, Hd, Hd), lambda g: (g, 0, 0)),
                   pl.BlockSpec((1, Hd), lambda g: (0, g))],
        out_shape=[jax.ShapeDtypeStruct((N_GROUPS + 1, T, N_GROUPS * Hd), BF16), jax.ShapeDtypeStruct((N_GROUPS, Hd, Hd), F32),
                   jax.ShapeDtypeStruct((1, N_GROUPS * Hd), F32)],
        args=[proj, da, pool_w, pool_scale])


def _ret_tables(T):
    Hd, C = HEAD_DIM, RET_CHUNK
    inv_freq = 1.0 / (ROPE_BASE ** (jnp.arange(0, Hd, 2, dtype=F32) / Hd))
    ang = jnp.arange(T, dtype=F32)[:, None] * inv_freq[None, :]
    cos, sin = jnp.cos(ang), jnp.sin(ang)
    cos2 = jnp.concatenate([cos, cos], axis=-1)
    sin2 = jnp.concatenate([-sin, sin], axis=-1)
    log_gamma = jnp.log1p(-jnp.exp2(-5.0 - jnp.arange(N_GROUPS, dtype=F32)))
    pos = jnp.arange(C, dtype=F32)
    rel = pos[:, None] - pos[None, :]
    intra = jnp.where(rel[None] >= 0, jnp.exp(log_gamma[:, None, None] * jnp.maximum(rel, 0.0)[None]), 0.0)
    k_tail = jnp.exp(log_gamma[:, None] * (C - 1 - pos)[None, :])
    q_head = jnp.exp(log_gamma[:, None] * (pos + 1.0)[None, :])
    chunk_decay = jnp.exp(log_gamma * C)
    wide = lambda t: jnp.broadcast_to(t[:, :, None], (N_GROUPS, C, Hd))
    return cos2, sin2, intra, wide(k_tail), wide(q_head), jnp.broadcast_to(chunk_decay[:, None, None], (N_GROUPS, 1, Hd))


def _rope(x, cos2, sin2):
    return x * cos2 + pltpu.roll(x, HEAD_DIM // 2, 1) * sin2


def _rope_t(d, cos2, sin2):
    return d * cos2 + pltpu.roll(d * sin2, HEAD_DIM // 2, 1)


def _ret_specs(tseg, seg_of):
    Hd, G = HEAD_DIM, N_GROUPS
    col = lambda kind: pl.BlockSpec((tseg, Hd), lambda h, s: (seg_of(s), G * kind + h))
    tab = pl.BlockSpec((tseg, Hd), lambda h, s: (seg_of(s), 0))
    head = pl.BlockSpec((1, RET_CHUNK, Hd), lambda h, s: (h, 0, 0))
    cd = pl.BlockSpec((1, 1, Hd), lambda h, s: (h, 0, 0))
    gain = pl.BlockSpec((1, Hd), lambda h, s: (0, h))
    return col, tab, head, cd, gain


def ret_fwd(proj, ret_norm, tables, name, hosted=()):
    T = proj.shape[0]
    Hd, C, G = HEAD_DIM, RET_CHUNK, N_GROUPS
    tseg = min(T, 1024)
    nseg, nck = T // tseg, tseg // C
    scale = Hd ** -0.5
    cos2, sin2, intra, k_tail, q_head, chunk_decay = tables

    def body(q_ref, k_ref, v_ref, g_ref, gain_ref, cos_ref, sin_ref, m_ref, kt_ref, qh_ref, cd_ref,
             b_ref, o_ref, rp_ref, state):
        @pl.when(pl.program_id(1) == 0)
        def _():
            state[...] = jnp.zeros_like(state)

        def chunk(ci, carry):
            rows = pl.ds(pl.multiple_of(ci * C, C), C)
            cos, sin = cos_ref[rows, :], sin_ref[rows, :]
            qr = _rope(q_ref[rows, :], cos, sin)
            kr = _rope(k_ref[rows, :], cos, sin) * scale
            qb, kb, vb = qr.astype(BF16), kr.astype(BF16), v_ref[rows, :].astype(BF16)
            r = state[...]
            rp_ref[0, ci] = r.astype(BF16)
            sc = _dot_nt(qb, kb) * m_ref[0]
            o = _dot(sc.astype(BF16), vb) + _dot((qr * qh_ref[0]).astype(BF16), r.astype(BF16))
            state[...] = cd_ref[0] * r + _dot_tn((kr * kt_ref[0]).astype(BF16), vb)
            o_ref[rows, :] = o
            on = o * _rstd(o)
            b_ref[rows, :] = (jax.nn.silu(g_ref[rows, :]) * (on * gain_ref[...])).astype(BF16)
            return carry

        lax.fori_loop(0, nck, chunk, 0, unroll=True)

    col, tab, head, cd, gain = _ret_specs(tseg, lambda s: s)
    out_col = pl.BlockSpec((tseg, Hd), lambda h, s: (s, h))
    return _call(
        body, hosted, name=name, grid=(G, nseg),
        in_specs=[col(1), col(2), col(3), col(4), gain, tab, tab, head, head, head, cd],
        out_specs=[out_col, out_col, pl.BlockSpec((1, nck, Hd, Hd), lambda h, s: (h, s, 0, 0))],
        out_shape=[jax.ShapeDtypeStruct((T, G * Hd), BF16), jax.ShapeDtypeStruct((T, G * Hd), F32),
                   jax.ShapeDtypeStruct((G, T // C, Hd, Hd), BF16)],
        scratch_shapes=[pltpu.VMEM((Hd, Hd), F32)],
        args=[proj, proj, proj, proj, ret_norm, cos2, sin2, intra, k_tail, q_head, chunk_decay])


def ret_bwd(proj, db, o_pre, r_prev, ret_norm, tables, dproj, name, hosted=()):
    T = proj.shape[0]
    Hd, C, G = HEAD_DIM, RET_CHUNK, N_GROUPS
    tseg = min(T, 1024)
    nseg, nck = T // tseg, tseg // C
    scale = Hd ** -0.5
    cos2, sin2, intra, k_tail, q_head, chunk_decay = tables

    def body(q_ref, k_ref, v_ref, g_ref, db_ref, o_ref, rp_ref, gain_ref, cos_ref, sin_ref, m_ref, kt_ref, qh_ref, cd_ref,
             _, d_ref, dgain_ref, gstate):
        @pl.when(pl.program_id(1) == 0)
        def _():
            gstate[...] = jnp.zeros_like(gstate)
            dgain_ref[...] = jnp.zeros_like(dgain_ref)

        def chunk(t, carry):
            ci = nck - 1 - t
            rows = pl.ds(pl.multiple_of(ci * C, C), C)
            cos, sin = cos_ref[rows, :], sin_ref[rows, :]
            qr = _rope(q_ref[rows, :], cos, sin)
            kr = _rope(k_ref[rows, :], cos, sin) * scale
            qb, kb, vb = qr.astype(BF16), kr.astype(BF16), v_ref[rows, :].astype(BF16)
            qhb, ktb = (qr * qh_ref[0]).astype(BF16), (kr * kt_ref[0]).astype(BF16)
            sc = (_dot_nt(qb, kb) * m_ref[0]).astype(BF16)
            o = o_ref[rows, :]
            rstd = _rstd(o)
            on = o * rstd
            gain = gain_ref[...]
            silu, dsilu = _silu_parts(g_ref[rows, :])
            dy = db_ref[rows, :]
            dgain_ref[...] += jnp.sum(dy * silu * on, axis=0, keepdims=True)
            dg = dy * on * gain * dsilu
            don = dy * silu * gain
            dob = (rstd * (don - on * jnp.mean(don * on, axis=-1, keepdims=True))).astype(BF16)
            gn = gstate[...]
            gb = gn.astype(BF16)
            da = (_dot_nt(dob, vb) * m_ref[0]).astype(BF16)
            dq = _dot(da, kb) + _dot_nt(dob, rp_ref[0, ci]) * qh_ref[0]
            dk = _dot_tn(da, qb) + _dot_nt(vb, gb) * kt_ref[0]
            dv = _dot_tn(sc, dob) + _dot(ktb, gb)
            gstate[...] = cd_ref[0] * gn + _dot_tn(qhb, dob)
            d_ref[0, rows, :] = _rope_t(dq, cos, sin).astype(BF16)
            d_ref[1, rows, :] = _rope_t(dk * scale, cos, sin).astype(BF16)
            d_ref[2, rows, :] = dv.astype(BF16)
            d_ref[3, rows, :] = dg.astype(BF16)
            return carry

        lax.fori_loop(0, nck, chunk, 0, unroll=True)

    rev = lambda s: nseg - 1 - s
    col, tab, head, cd, gain = _ret_specs(tseg, rev)
    act = pl.BlockSpec((tseg, Hd), lambda h, s: (rev(s), h))
    return _call(
        body, hosted, name=name, grid=(G, nseg),
        in_specs=[col(1), col(2), col(3), col(4), act, act, pl.BlockSpec((1, nck, Hd, Hd), lambda h, s: (h, rev(s), 0, 0)),
                  gain, tab, tab, head, head, head, cd, ANY],
        out_specs=[pl.BlockSpec((4, tseg, Hd), lambda h, s: (0, rev(s), h)), gain],
        out_shape=[jax.ShapeDtypeStruct(dproj.shape, BF16), jax.ShapeDtypeStruct((1, G * Hd), F32)],
        scratch_shapes=[pltpu.VMEM((Hd, Hd), F32)], aliased={14: 0},
        args=[proj, proj, proj, proj, db, o_pre, r_prev, ret_norm, cos2, sin2, intra, k_tail, q_head, chunk_decay, dproj])


def final_loss(h, gain, target, name, hosted=()):
    T, D = h.shape
    tm = min(T, 512)

    def body(h_ref, g_ref, t_ref, dh_ref, loss_ref, dg_ref):
        @pl.when(pl.program_id(0) == 0)
        def _():
            loss_ref[...] = jnp.zeros_like(loss_ref)
            dg_ref[...] = jnp.zeros_like(dg_ref)

        hh = h_ref[...]
        gain_v = g_ref[...]
        err = hh * _rstd(hh) * gain_v - t_ref[...]
        loss_ref[...] += 0.5 * jnp.sum(jnp.mean(err * err, axis=-1, keepdims=True), axis=0, keepdims=True)
        dhn, dg = _rmsnorm_bwd(err * (1.0 / D), hh, gain_v)
        dh_ref[...] = dhn
        dg_ref[...] += jnp.sum(dg, axis=0, keepdims=True)

    row_spec = pl.BlockSpec((tm, D), lambda i: (i, 0))
    vec_spec = pl.BlockSpec((1, D), lambda i: (0, 0))
    return _call(
        body, hosted, name=name, grid=(T // tm,),
        in_specs=[row_spec, vec_spec, row_spec],
        out_specs=[row_spec, pl.BlockSpec((1, 128), lambda i: (0, 0)), vec_spec],
        out_shape=[jax.ShapeDtypeStruct((T, D), F32), jax.ShapeDtypeStruct((1, 128), F32), jax.ShapeDtypeStruct((1, D), F32)],
        args=[h, gain, target])


def prereduce(grads, recvs, place, name):
    nt = len(grads)
    nsh, R, C = grads[0].shape
    rh = R // 2

    def body(place_ref, *refs):
        for t in range(nt):
            g_ref, r_ref, o_ref, own_ref = refs[2 * t], refs[2 * t + 1], refs[2 * nt + 2 * t], refs[2 * nt + 2 * t + 1]
            piece = (g_ref[...].astype(F32) + r_ref[...].astype(F32)).astype(BF16)
            o_ref[...] = piece

            @pl.when(pl.program_id(0) == place_ref[1])
            def _():
                own_ref[...] = piece

    outs = pl.pallas_call(
        body, name=name,
        grid_spec=pltpu.PrefetchScalarGridSpec(
            num_scalar_prefetch=1, grid=(nsh,),
            in_specs=[pl.BlockSpec((1, rh, C), lambda j, p: (j, p[0], 0)), pl.BlockSpec((1, rh, C), lambda j, p: (j, 0, 0))] * nt,
            out_specs=[pl.BlockSpec((1, rh, C), lambda j, p: (j, 0, 0)),
                       pl.BlockSpec((1, rh, C), lambda j, p: (p[1], p[0], 0))] * nt),
        out_shape=[jax.ShapeDtypeStruct((nsh, rh, C), BF16), jax.ShapeDtypeStruct((nsh, R, C), BF16)] * nt,
        compiler_params=pltpu.CompilerParams(vmem_limit_bytes=VMEM_LIMIT_V7X),
    )(place, *[a for pair in zip(grads, recvs) for a in pair])
    return [(outs[2 * t], outs[2 * t + 1]) for t in range(nt)]


def _adamw(w, g, m, v):
    m = ADAM_B1 * m + (1.0 - ADAM_B1) * g
    v = ADAM_B2 * v + (1.0 - ADAM_B2) * (g * g)
    m_hat = m / (1.0 - ADAM_B1 ** ADAM_STEP)
    v_hat = v / (1.0 - ADAM_B2 ** ADAM_STEP)
    return -ADAM_LR * (m_hat / (jnp.sqrt(v_hat) + ADAM_EPS) + ADAM_WD * w), m, v


def adamw_sharded(tensors, name, hosted=()):
    nt = len(tensors)
    nsh, R, C = tensors[0][0].shape
    lanes = -(-C // 128) * 128
    per_row = 2 * nt * lanes * (nsh * 2 + 7 * 4)
    tr = max(r for r in range(16, R + 1, 16) if R % r == 0 and r * per_row <= ADAMW_VMEM_BUDGET)

    def body(*refs):
        ins, outs = refs[:4 * nt], refs[4 * nt:]
        for t in range(nt):
            p_ref, w_ref, m_ref, v_ref = ins[4 * t:4 * t + 4]
            g_ref, d_ref, nm_ref, nv_ref = outs[4 * t:4 * t + 4]
            g = p_ref[0].astype(F32)
            for i in range(1, nsh):
                g += p_ref[i].astype(F32)
            g_ref[...] = g
            d_ref[...], nm_ref[...], nv_ref[...] = _adamw(w_ref[...], g, m_ref[...], v_ref[...])

    spec = pl.BlockSpec((tr, C), lambda i: (i, 0))
    out = jax.ShapeDtypeStruct((R, C), F32)
    return _call(
        body, hosted, name=name, grid=(R // tr,),
        in_specs=[pl.BlockSpec((nsh, tr, C), lambda i: (0, i, 0)), spec, spec, spec] * nt,
        out_specs=[spec] * (4 * nt), out_shape=[out] * (4 * nt),
        args=[a for tensor in tensors for a in tensor])


def adamw_small(packs, late, pool, vectors, name):
    ndev = packs.shape[0]
    rp, rv, rl = pool[0].shape[0], vectors.shape[0] // 3, late.shape[1]

    def body(p_ref, l_ref, wp, mp, vp, wmv, gp, dp, nmp, nvp, gv, dv, nmv, nvv, loss_ref):
        wv, mv, vv = wmv.at[0:rv], wmv.at[rv:2 * rv], wmv.at[2 * rv:3 * rv]
        g, first = p_ref[0], l_ref[0]
        for i in range(1, ndev):
            g += p_ref[i]
            first += l_ref[i]
        g_pool = g[0:rp]
        g_vec = jnp.concatenate([g[rp:rp + rl] + first, g[rp + rl:rp + rv]], axis=0)
        gp[...], gv[...], loss_ref[...] = g_pool, g_vec, g[rp + rv:rp + rv + 8]
        dp[...], nmp[...], nvp[...] = _adamw(wp[...], g_pool, mp[...], vp[...])
        dv[...], nmv[...], nvv[...] = _adamw(wv[...], g_vec, mv[...], vv[...])

    shape = lambda rows: jax.ShapeDtypeStruct((rows, 128), F32)
    outs = pl.pallas_call(body, name=name, out_shape=[shape(rp)] * 4 + [shape(rv)] * 4 + [shape(8)],
                          compiler_params=pltpu.CompilerParams(vmem_limit_bytes=VMEM_LIMIT_V7X))(packs, late, *pool, vectors)
    return outs[0:4], outs[4:8], outs[8]


BIG = ("ffn1_w1", "ffn1_w3", "ffn1_w2", "w_in", "w_out", "ffn2_w1", "ffn2_w3", "ffn2_w2")
TRANSPOSED = ("ffn1_w1", "ffn1_w3", "ffn2_w1", "ffn2_w3")
VECTORS = ("ffn1_norm", "mix_norm", "pool_scale", "ret_norm", "ffn2_norm", "final_norm")
WEIGHTS = ("ffn1_norm", "ffn1_w1", "ffn1_w3", "ffn1_w2", "mix_norm", "w_in", "pool_w", "pool_scale", "ret_norm", "w_out",
           "ffn2_norm", "ffn2_w1", "ffn2_w3", "ffn2_w2", "final_norm")


def _pack_vectors(parts):
    return jnp.concatenate([parts[k].reshape(-1, 128) for k in VECTORS], axis=0)


def _unpack_vectors(pack, like):
    out, row = {}, 0
    for k in VECTORS:
        rows = like[k].size // 128
        out[k] = pack[row:row + rows].reshape(like[k].shape)
        row += rows
    return out


def kernel(x, ffn1_norm, ffn1_w1, ffn1_w3, ffn1_w2, mix_norm, w_in, pool_w, pool_scale, ret_norm, w_out, ffn2_norm, ffn2_w1, ffn2_w3, ffn2_w2, final_norm, loss_target, m_ffn1_norm, m_ffn1_w1, m_ffn1_w3, m_ffn1_w2, m_mix_norm, m_w_in, m_pool_w, m_pool_scale, m_ret_norm, m_w_out, m_ffn2_norm, m_ffn2_w1, m_ffn2_w3, m_ffn2_w2, m_final_norm, v_ffn1_norm, v_ffn1_w1, v_ffn1_w3, v_ffn1_w2, v_mix_norm, v_w_in, v_pool_w, v_pool_scale, v_ret_norm, v_w_out, v_ffn2_norm, v_ffn2_w1, v_ffn2_w3, v_ffn2_w2, v_final_norm):
    w = dict(ffn1_norm=ffn1_norm, ffn1_w1=ffn1_w1, ffn1_w3=ffn1_w3, ffn1_w2=ffn1_w2, mix_norm=mix_norm, w_in=w_in, pool_w=pool_w,
             pool_scale=pool_scale, ret_norm=ret_norm, w_out=w_out, ffn2_norm=ffn2_norm, ffn2_w1=ffn2_w1, ffn2_w3=ffn2_w3,
             ffn2_w2=ffn2_w2, final_norm=final_norm)
    m = dict(ffn1_norm=m_ffn1_norm, ffn1_w1=m_ffn1_w1, ffn1_w3=m_ffn1_w3, ffn1_w2=m_ffn1_w2, mix_norm=m_mix_norm, w_in=m_w_in,
             pool_w=m_pool_w, pool_scale=m_pool_scale, ret_norm=m_ret_norm, w_out=m_w_out, ffn2_norm=m_ffn2_norm, ffn2_w1=m_ffn2_w1,
             ffn2_w3=m_ffn2_w3, ffn2_w2=m_ffn2_w2, final_norm=m_final_norm)
    v = dict(ffn1_norm=v_ffn1_norm, ffn1_w1=v_ffn1_w1, ffn1_w3=v_ffn1_w3, ffn1_w2=v_ffn1_w2, mix_norm=v_mix_norm, w_in=v_w_in,
             pool_w=v_pool_w, pool_scale=v_pool_scale, ret_norm=v_ret_norm, w_out=v_w_out, ffn2_norm=v_ffn2_norm, ffn2_w1=v_ffn2_w1,
             ffn2_w3=v_ffn2_w3, ffn2_w2=v_ffn2_w2, final_norm=v_final_norm)
    xs, target = x[0], loss_target[0]
    T = xs.shape[0]
    tables = _ret_tables(T)
    place = jnp.stack([lax.axis_index("c"), 2 * lax.axis_index("x") + lax.axis_index("y")]).astype(jnp.int32)
    local = lambda d, k: jnp.transpose(d[k][0]) if k in TRANSPOSED else d[k][0]
    result = lambda o, k: jnp.transpose(o)[None] if k in TRANSPOSED else o[None]
    first = ("ffn1_w1", "ffn1_w3")
    sh = {k: local(w, k).astype(BF16) for k in first}
    gather = lambda *names: [ChipExchange([sh[k] for k in names], False)]
    wg, grad, delta, new_m, new_v = {}, {}, {}, {}, {}

    def update(names, pieces, name, hosted=()):
        outs, extras = adamw_sharded([(p, local(w, k), local(m, k), local(v, k)) for k, p in zip(names, pieces)], name, hosted)
        for t, k in enumerate(names):
            grad[k], delta[k], new_m[k], new_v[k] = [result(o, k) for o in outs[4 * t:4 * t + 4]]
        return extras

    def reduce_in_chip(name, *pairs):
        reduced = prereduce([p for p, _ in pairs], [r for _, r in pairs], place, "prereduce_" + name)
        return reduced[0] if len(pairs) == 1 else reduced

    scatter = lambda *reduced: ChipExchange([r[0] for r in reduced], True, [r[1] for r in reduced])
    whole = lambda k: wg[k].reshape(-1, wg[k].shape[-1])
    sharded = lambda g: g.reshape(N_CHIPS, -1, g.shape[-1])

    later = [k for k in BIG if k not in first]
    casts, ((wg["ffn1_w1"], wg["ffn1_w3"]),) = cast_shards([local(w, k) for k in later], "cast_gather_ffn1", gather(*first))
    sh.update(zip(later, casts))
    (n1, ga1, gb1, s1), ((wg["ffn1_w2"], wg["w_in"]),) = ffn_up(
        xs, ffn1_norm, whole("ffn1_w1"), whole("ffn1_w3"), "ffn1_up", gather("ffn1_w2", "w_in"))
    (h1,), ((wg["w_out"],),) = ffn_down(s1, whole("ffn1_w2"), xs, "ffn1_down", gather("w_out"))
    (u, proj), ((wg["ffn2_w1"],),) = mix_in(h1, mix_norm, wg["w_in"], "mix_in", gather("ffn2_w1"))
    (pa,), _ = pool_fwd(proj, pool_w[0], pool_scale, "pool_fwd")
    (rb, o_pre, r_prev), ((wg["ffn2_w3"],),) = ret_fwd(proj, ret_norm, tables, "ret_fwd", gather("ffn2_w3"))
    (h2,), _ = mix_out(pa, rb, wg["w_out"], h1, "mix_out")
    (n2, ga2, gb2, s2), ((wg["ffn2_w2"],),) = ffn_up(
        h2, ffn2_norm, whole("ffn2_w1"), whole("ffn2_w3"), "ffn2_up", gather("ffn2_w2"))
    (h3,), _ = ffn_down(s2, whole("ffn2_w2"), h2, "ffn2_down")
    (dh3, loss, d_final), _ = final_loss(h3, final_norm[None], target, "final_loss")

    (da2, db2, df2), _ = ffn_bwd_act(dh3, whole("ffn2_w2"), ga2, gb2, "ffn2_bwd_act")
    (g_f2w2,), _ = ffn_dw([s2], df2, 1, "ffn2_dw2")
    g_f2w2 = sharded(g_f2w2)
    (g_f2w1, g_f2w3), ((r_f2w2,),) = ffn_dw([da2, db2], n2, 2, "ffn2_dw13", [SiblingExchange([g_f2w2])])
    g_f2w1, g_f2w3 = sharded(g_f2w1), sharded(g_f2w3)
    p_f2w2 = reduce_in_chip("ffn2_w2", (g_f2w2, r_f2w2))
    (dh2, d_ffn2), ((q_f2w2,), (r_f2w1, r_f2w3)) = ffn_bwd_in(
        da2, db2, whole("ffn2_w1"), whole("ffn2_w3"), h2, ffn2_norm, dh3, "ffn2_bwd_in",
        [scatter(p_f2w2), SiblingExchange([g_f2w1, g_f2w3])])
    p_f2w1, p_f2w3 = reduce_in_chip("ffn2_w13", (g_f2w1, r_f2w1), (g_f2w3, r_f2w3))
    (dpa, drb, g_wout), _ = mix_out_bwd(dh2, wg["w_out"], pa, rb, "mix_out_bwd")
    (dproj, d_pool_w, d_pool_scale), _ = pool_bwd(proj, dpa, pool_w[0], pool_scale, "pool_bwd")
    (dproj, d_ret_norm), ((q_f2w1, q_f2w3), (r_wout,)) = ret_bwd(
        proj, drb, o_pre, r_prev, ret_norm, tables, dproj, "ret_bwd", [scatter(p_f2w1, p_f2w3), SiblingExchange([g_wout])])
    p_wout = reduce_in_chip("w_out", (g_wout, r_wout))
    (g_win,), ((q_wout,),) = mix_dwin(u, dproj, N_CHIPS, "mix_dwin", [scatter(p_wout)])
    (dh1, d_mix), ((r_win,),) = mix_in_bwd(dproj, wg["w_in"], h1, mix_norm, dh2, "mix_in_bwd", [SiblingExchange([g_win])])
    p_win = reduce_in_chip("w_in", (g_win, r_win))
    (da1, db1, df1), ((q_win,),) = ffn_bwd_act(dh1, whole("ffn1_w2"), ga1, gb1, "ffn1_bwd_act", [scatter(p_win)])
    d_vectors = {"ffn1_norm": jnp.zeros_like(ffn1_norm), "mix_norm": d_mix, "pool_scale": d_pool_scale,
                 "ret_norm": d_ret_norm, "ffn2_norm": d_ffn2, "final_norm": d_final}
    pack = jnp.concatenate([d_pool_w.reshape(-1, 128), _pack_vectors(d_vectors), jnp.broadcast_to(loss, (8, 128))], axis=0)
    (g_f1w1, g_f1w3), ((packs,),) = ffn_dw([da1, db1], n1, 2, "ffn1_dw13", [AllExchange(pack)])
    g_f1w1, g_f1w3 = sharded(g_f1w1), sharded(g_f1w3)
    (g_f1w2,), ((r_f1w1, r_f1w3),) = ffn_dw([s1], df1, 1, "ffn1_dw2", [SiblingExchange([g_f1w1, g_f1w3])])
    g_f1w2 = sharded(g_f1w2)
    p_f1w1, p_f1w3 = reduce_in_chip("ffn1_w13", (g_f1w1, r_f1w1), (g_f1w3, r_f1w3))
    (dx, d_ffn1), ((q_f1w1, q_f1w3), (r_f1w2,)) = ffn_bwd_in(
        da1, db1, whole("ffn1_w1"), whole("ffn1_w3"), xs, ffn1_norm, dh1, "ffn1_bwd_in",
        [scatter(p_f1w1, p_f1w3), SiblingExchange([g_f1w2])])
    p_f1w2 = reduce_in_chip("ffn1_w2", (g_f1w2, r_f1w2))

    (q_f1w2,), (late,) = update(["ffn2_w1", "ffn2_w3", "ffn1_w1", "ffn1_w3"], [q_f2w1, q_f2w3, q_f1w1, q_f1w3], "adamw_w13",
                                [scatter(p_f1w2), AllExchange(d_ffn1.reshape(-1, 128))])
    update(["ffn2_w2", "ffn1_w2"], [q_f2w2, q_f1w2], "adamw_w2")
    update(["w_in"], [q_win], "adamw_w_in")
    update(["w_out"], [q_wout], "adamw_w_out")
    of_pool, of_vectors, loss_sum = adamw_small(packs, late, [t["pool_w"].reshape(-1, 128) for t in (w, m, v)],
                                                jnp.concatenate([t[k].reshape(-1, 128) for t in (w, m, v) for k in VECTORS], axis=0),
                                                "adamw_small")
    for res, pool_part, vector_part in zip((grad, delta, new_m, new_v), of_pool, of_vectors):
        res["pool_w"] = pool_part.reshape(pool_w.shape)
        res.update(_unpack_vectors(vector_part, w))
    loss = loss_sum[0, 0]

    return (loss, dx[None], *[grad[k] for k in WEIGHTS], *[delta[k] for k in WEIGHTS],
            *[new_m[k] for k in WEIGHTS], *[new_v[k] for k in WEIGHTS])
```

```python
import math

import jax
import jax.numpy as jnp
from jax import lax
from jax.experimental import pallas as pl
from jax.experimental.pallas import tpu as pltpu

F32 = jnp.float32
BF16 = jnp.bfloat16

EPS = 1e-6
N_CHIPS = 4
N_GROUPS = 4
HEAD_DIM = 128
RET_CHUNK = 128
ROPE_BASE = 10000.0
ADAM_LR, ADAM_B1, ADAM_B2, ADAM_EPS, ADAM_WD, ADAM_STEP = 0.001, 0.9, 0.999, 1e-08, 0.01, 10
VMEM_LIMIT_V7X = 56 * 1024 * 1024
ADAMW_VMEM_BUDGET = 32 * 1024 * 1024
MESH = pl.DeviceIdType.MESH
ANY = pl.BlockSpec(memory_space=pl.ANY)


def _dot(a, b):
    return jnp.dot(a, b, preferred_element_type=F32)


def _dot_nt(a, b):
    return lax.dot_general(a, b, (((1,), (1,)), ((), ())), preferred_element_type=F32)


def _dot_tn(a, b):
    return lax.dot_general(a, b, (((0,), (0,)), ((), ())), preferred_element_type=F32)


def _rstd(h):
    return lax.rsqrt(jnp.mean(h * h, axis=-1, keepdims=True) + EPS)


def _rmsnorm_bwd(dn, h, gain):
    r = _rstd(h)
    nh = h * r
    dnh = dn * gain
    dh = r * (dnh - nh * jnp.mean(dnh * nh, axis=-1, keepdims=True))
    return dh, dn * nh


def _silu_parts(a):
    sig = jax.nn.sigmoid(a)
    silu = a * sig
    return silu, sig + silu * (1.0 - sig)


def _mesh_pos():
    return lax.axis_index("x"), lax.axis_index("y"), lax.axis_index("c")


class ChipExchange:
    def __init__(self, srcs, scatter, placed=()):
        n = len(srcs)
        self.inputs, self.scatter, self.n, self.reach = list(srcs) + list(placed), scatter, n, REACH_CHIPS
        self.aliases = {n + t: t for t in range(n)} if scatter else {}
        self.half_rows = [s.shape[1] if scatter else s.shape[0] // 2 for s in srcs]
        self.out_shape = [jax.ShapeDtypeStruct((N_CHIPS, 2 * rh, s.shape[-1]), s.dtype) for s, rh in zip(srcs, self.half_rows)]
        if scatter:
            self.out_shape += [jax.ShapeDtypeStruct((2, rh // 2, s.shape[-1]), s.dtype) for s, rh in zip(srcs, self.half_rows)]
        dma = pltpu.SemaphoreType.DMA
        self.sems = [dma((4 * n,)), dma((4 * n,)), dma((2 * n,)), dma((2 * n,)), dma((4 * n,)), dma((4 * n,))]

    def _copies(self, src, out, sems):
        hop1_send, hop1_recv, hop2_send, hop2_recv, d2d_send, d2d_recv = sems
        x, y, c = _mesh_pos()
        me, dg = 2 * x + y, 2 * (1 - x) + (1 - y)
        sibling = (x, y, 1 - c)
        n = self.n
        mine, theirs = c, 1 - c

        def nb(a):
            nx, ny = x ^ (1 - a), y ^ a
            return 2 * nx + ny, (nx, ny, c)

        def remote(s, d, send, recv, k, to):
            return pltpu.make_async_remote_copy(src_ref=s, dst_ref=d, send_sem=send.at[k], recv_sem=recv.at[k],
                                                device_id=to, device_id_type=MESH)

        class Copies:
            def slot(_, t, chip, half):
                rh = self.half_rows[t]
                return out[t].at[chip, pl.ds(half * rh, rh), :]

            def quarter(_, t, chip, q):
                qh = self.half_rows[t] // 2
                return out[t].at[chip, pl.ds(mine * 2 * qh + q * qh, qh), :]

            def own_shard(k, t):
                return remote(src[t], out[t].at[me], d2d_send, d2d_recv, 4 * t + 3, sibling)

            def hop1(k, t, a, transit=False):
                rh = self.half_rows[t]
                chip, to = nb(a)
                if transit:
                    piece = src[t].at[dg, pl.ds(a * (rh // 2), rh // 2), :]
                    return remote(piece, out[n + t].at[a], hop1_send, hop1_recv, 4 * t + 2 + a, to)
                piece = src[t].at[chip] if self.scatter else src[t].at[pl.ds(mine * rh, rh), :]
                return remote(piece, k.slot(t, me, mine), hop1_send, hop1_recv, 4 * t + a, to)

            def landed1(k, t, a, transit=False):
                here = out[n + t].at[a] if transit else k.slot(t, nb(a)[0], mine)
                return remote(here, here, hop1_send, hop1_recv, 4 * t + (2 if transit else 0) + a, sibling)

            def hop2(k, t, q):
                origin, to = nb(q)[0], nb(1 - q)[1]
                piece = out[n + t].at[q] if self.scatter else k.quarter(t, origin, q)
                return remote(piece, k.quarter(t, origin, q), hop2_send, hop2_recv, 2 * t + q, to)

            def landed2(k, t, q):
                here = k.quarter(t, dg, q)
                return remote(here, here, hop2_send, hop2_recv, 2 * t + q, sibling)

            def d2d(k, t, p, chip, own=False, arriving=False):
                if arriving:
                    there = k.slot(t, chip, theirs)
                    return remote(there, there, d2d_send, d2d_recv, 4 * t + p, sibling)
                piece = src[t].at[me] if own else k.slot(t, chip, mine)
                return remote(piece, k.slot(t, chip, mine), d2d_send, d2d_recv, 4 * t + p, sibling)

        return Copies(), nb, me, dg, c

    def start(self, src, out, sems):
        k, nb, me, dg, c = self._copies(src, out, sems)
        for t in range(self.n):
            for first in range(2):
                a = first ^ c
                k.hop1(t, a).start()
                if self.scatter:
                    k.hop1(t, a, transit=True).start()
            if self.scatter:
                k.d2d(t, 3, me, own=True).start()
            else:
                k.own_shard(t).start()

    def mid(self, src, out, sems):
        k, nb, me, dg, c = self._copies(src, out, sems)
        for t in range(self.n):
            for first in range(2):
                a = first ^ c
                if self.scatter:
                    k.landed1(t, a, transit=True).wait_recv()
                    k.hop2(t, a).start()
                k.landed1(t, a).wait_recv()
                if not self.scatter:
                    k.hop2(t, a).start()
                k.d2d(t, a, nb(a)[0]).start()

    def finish(self, src, out, sems):
        k, nb, me, dg, c = self._copies(src, out, sems)
        for t in range(self.n):
            for q in range(2):
                k.landed2(t, q).wait_recv()
            k.d2d(t, 2, dg).start()
        for t in range(self.n):
            for a in range(2):
                k.d2d(t, a, nb(a)[0], arriving=True).wait_recv()
            k.d2d(t, 2, dg, arriving=True).wait_recv()
            if self.scatter:
                k.d2d(t, 3, me, arriving=True).wait_recv()
        for t in range(self.n):
            for a in range(2):
                k.hop1(t, a).wait_send()
                if self.scatter:
                    k.hop1(t, a, transit=True).wait_send()
                k.hop2(t, a).wait_send()
                k.d2d(t, a, nb(a)[0]).wait_send()
            k.d2d(t, 2, dg).wait_send()
            if self.scatter:
                k.d2d(t, 3, me, own=True).wait_send()
            else:
                k.own_shard(t).wait()


class SiblingExchange:
    def __init__(self, grads):
        self.inputs, self.n, self.aliases, self.reach = list(grads), len(grads), {}, REACH_SIBLING
        self.half_rows = [g.shape[1] // 2 for g in grads]
        self.out_shape = [jax.ShapeDtypeStruct((g.shape[0], rh, g.shape[2]), g.dtype) for g, rh in zip(grads, self.half_rows)]
        self.sems = [pltpu.SemaphoreType.DMA((self.n,)), pltpu.SemaphoreType.DMA((self.n,))]

    def _plan(self, src, out, sems):
        x, y, c = _mesh_pos()
        return [pltpu.make_async_remote_copy(
            src_ref=src[t].at[:, pl.ds((1 - c) * self.half_rows[t], self.half_rows[t]), :], dst_ref=out[t],
            send_sem=sems[0].at[t], recv_sem=sems[1].at[t], device_id=(x, y, 1 - c), device_id_type=MESH) for t in range(self.n)]

    def start(self, src, out, sems):
        for cp in self._plan(src, out, sems):
            cp.start()

    def mid(self, src, out, sems):
        pass

    def finish(self, src, out, sems):
        for cp in self._plan(src, out, sems):
            cp.wait()


REACH_SIBLING, REACH_CHIPS, REACH_ALL = 0, 1, 2


def _entry_barrier(reach):
    x, y, c = _mesh_pos()
    peers = [(x, y, 1 - c)]
    if reach == REACH_CHIPS:
        peers += [(1 - x, y, c), (x, 1 - y, c)]
    elif reach == REACH_ALL:
        peers = [(x ^ dx, y ^ dy, c ^ dc) for dx in (0, 1) for dy in (0, 1) for dc in (0, 1)][1:]
    barrier = pltpu.get_barrier_semaphore()
    for peer in peers:
        pl.semaphore_signal(barrier, inc=1, device_id=peer, device_id_type=MESH)
    pl.semaphore_wait(barrier, len(peers))


def _call(body, hosted=(), *, name, in_specs, out_specs, out_shape, args, grid=(), scratch_shapes=(), aliased=None):
    n_in, n_out, n_scr = len(in_specs), len(out_specs), len(scratch_shapes)
    total = math.prod(grid)
    mid_step = max(0, (5 * total) // 8 - 1)

    def full(*refs):
        pos = [0]

        def take(k):
            pos[0] += k
            return refs[pos[0] - k:pos[0]]

        ins, h_in = take(n_in), [take(len(h.inputs)) for h in hosted]
        outs, h_out = take(n_out), [take(len(h.out_shape)) for h in hosted]
        scr, h_sem = take(n_scr), [take(len(h.sems)) for h in hosted]
        step = 0
        for axis, size in enumerate(grid):
            step = step * size + pl.program_id(axis)

        def phase(at, method):
            if not hosted:
                return

            def run():
                if method == "start":
                    _entry_barrier(reach)
                for h, s, o, m in zip(hosted, h_in, h_out, h_sem):
                    getattr(h, method)(s, o, m)

            if total == 1:
                run()
            else:
                pl.when(step == at)(run)

        phase(0, "start")
        body(*ins, *outs, *scr)
        phase(mid_step, "mid")
        phase(total - 1, "finish")

    aliases, i0, o0 = dict(aliased or {}), n_in, n_out
    for h in hosted:
        aliases.update({i0 + i: o0 + o for i, o in h.aliases.items()})
        i0, o0 = i0 + len(h.inputs), o0 + len(h.out_shape)
    reach = max((h.reach for h in hosted), default=None)
    params = dict(vmem_limit_bytes=VMEM_LIMIT_V7X)
    if hosted:
        params["collective_id"] = reach
    results = pl.pallas_call(
        full, name=name, grid=grid,
        in_specs=list(in_specs) + [ANY] * (i0 - n_in),
        out_specs=list(out_specs) + [ANY] * (o0 - n_out),
        out_shape=list(out_shape) + [s for h in hosted for s in h.out_shape],
        scratch_shapes=list(scratch_shapes) + [s for h in hosted for s in h.sems],
        input_output_aliases=aliases,
        compiler_params=pltpu.CompilerParams(**params),
    )(*args, *[s for h in hosted for s in h.inputs])
    outs, extras, pos = list(results[:n_out]), [], n_out
    for h in hosted:
        extras.append(list(results[pos:pos + h.n]))
        pos += len(h.out_shape)
    return outs, extras


def cast_shards(shards, name, hosted=()):
    n = len(shards)

    def body(*refs):
        for x_ref, o_ref in zip(refs[:n], refs[n:]):
            o_ref[...] = x_ref[...].astype(BF16)

    whole = lambda s: pl.BlockSpec(s.shape, lambda: (0,) * s.ndim)
    return _call(body, hosted, name=name, in_specs=[whole(s) for s in shards], out_specs=[whole(s) for s in shards],
                 out_shape=[jax.ShapeDtypeStruct(s.shape, BF16) for s in shards], args=list(shards))


class AllExchange:
    def __init__(self, pack):
        self.inputs, self.n, self.aliases, self.reach = [pack], 1, {}, REACH_ALL
        self.out_shape = [jax.ShapeDtypeStruct((2 * N_CHIPS,) + pack.shape, pack.dtype)]
        self.sems = [pltpu.SemaphoreType.DMA, pltpu.SemaphoreType.DMA((7,)), pltpu.SemaphoreType.DMA((7,))]

    def _copies(self, src, out, sems):
        local_sem, send_sem, recv_sem = sems
        x, y, c = _mesh_pos()
        flips = [(dx, dy, dc) for dx in (0, 1) for dy in (0, 1) for dc in (0, 1)][1:]
        peers = [(x ^ dx, y ^ dy, c ^ dc) for dx, dy, dc in flips]
        remote = lambda s, d, k: pltpu.make_async_remote_copy(
            src_ref=s, dst_ref=d, send_sem=send_sem.at[k], recv_sem=recv_sem.at[k], device_id=peers[k], device_id_type=MESH)
        sends = [remote(src[0], out[0].at[4 * x + 2 * y + c], k) for k in range(7)]
        landed = [remote(out[0].at[4 * px + 2 * py + pc], out[0].at[4 * px + 2 * py + pc], k) for k, (px, py, pc) in enumerate(peers)]
        return sends, landed, pltpu.make_async_copy(src[0], out[0].at[4 * x + 2 * y + c], local_sem)

    def start(self, src, out, sems):
        sends, _, local = self._copies(src, out, sems)
        for cp in sends:
            cp.start()
        local.start()

    def mid(self, src, out, sems):
        pass

    def finish(self, src, out, sems):
        sends, landed, local = self._copies(src, out, sems)
        for cp in landed:
            cp.wait_recv()
        for cp in sends:
            cp.wait_send()
        local.wait()


MXU_COLS = 256


def _resident(shape):
    return pl.BlockSpec(shape, lambda *_: (0,) * len(shape), pipeline_mode=pl.Buffered(1))


def ffn_up(h, gain, w1, w3, name, hosted=()):
    T, D = h.shape
    F = w1.shape[0]
    tm = min(T, 512)

    def body(h_ref, g_ref, w1_ref, w3_ref, n_ref, ga_ref, gb_ref, s_ref):
        hh = h_ref[...]
        n = (hh * _rstd(hh) * g_ref[...]).astype(BF16)
        n_ref[...] = n
        for c in range(0, F, MXU_COLS):
            cols = slice(c, c + MXU_COLS)
            a = _dot_nt(n, w1_ref[cols, :])
            b = _dot_nt(n, w3_ref[cols, :])
            silu, dsilu = _silu_parts(a)
            ga_ref[:, cols] = (b * dsilu).astype(BF16)
            gb_ref[:, cols] = silu.astype(BF16)
            s_ref[:, cols] = (silu * b).astype(BF16)

    act = jax.ShapeDtypeStruct((T, F), BF16)
    act_spec = pl.BlockSpec((tm, F), lambda i: (i, 0))
    row_spec = pl.BlockSpec((tm, D), lambda i: (i, 0))
    return _call(
        body, hosted, name=name, grid=(T // tm,),
        in_specs=[row_spec, pl.BlockSpec((1, D), lambda i: (0, 0)), _resident((F, D)), _resident((F, D))],
        out_specs=[row_spec, act_spec, act_spec, act_spec],
        out_shape=[jax.ShapeDtypeStruct((T, D), BF16), act, act, act],
        args=[h, gain, w1, w3])


def ffn_down(s, w2, h, name, hosted=()):
    T, F = s.shape
    D = h.shape[1]
    tm = min(T, 512)

    def body(s_ref, w2_ref, h_ref, o_ref):
        o_ref[...] = h_ref[...] + 0.5 * _dot(s_ref[...], w2_ref[...])

    row_spec = pl.BlockSpec((tm, D), lambda i: (i, 0))
    return _call(
        body, hosted, name=name, grid=(T // tm,),
        in_specs=[pl.BlockSpec((tm, F), lambda i: (i, 0)), pl.BlockSpec((F, D), lambda i: (0, 0)), row_spec],
        out_specs=[row_spec],
        out_shape=[jax.ShapeDtypeStruct((T, D), F32)],
        args=[s, w2, h])


def ffn_bwd_act(dh, w2, ga, gb, name, hosted=()):
    T, D = dh.shape
    F = w2.shape[0]
    tm = min(T, 512)

    nsteps = T // tm

    def body(dh_ref, w2_ref, ga_hbm, gb_hbm, da_ref, db_ref, df_ref, ga_buf, gb_buf, sems):
        i = pl.program_id(0)
        slot = i % 2

        def tile(step, to, start):
            rows = pl.ds(pl.multiple_of(step * tm, tm), tm)
            for k, (hbm, buf) in enumerate(((ga_hbm, ga_buf), (gb_hbm, gb_buf))):
                if start:
                    pltpu.async_copy(hbm.at[rows, :], buf.at[to], sems.at[2 * k + to], priority=1)
                else:
                    pltpu.make_async_copy(hbm.at[rows, :], buf.at[to], sems.at[2 * k + to]).wait()

        pl.when(i == 0)(lambda: tile(0, 0, True))
        pl.when(i + 1 < nsteps)(lambda: tile(i + 1, 1 - slot, True))
        tile(i, slot, False)
        df = (0.5 * dh_ref[...]).astype(BF16)
        df_ref[...] = df
        for c in range(0, F, MXU_COLS):
            cols = slice(c, c + MXU_COLS)
            ds = _dot_nt(df, w2_ref[cols, :])
            da_ref[:, cols] = (ds * ga_buf[slot, :, cols].astype(F32)).astype(BF16)
            db_ref[:, cols] = (ds * gb_buf[slot, :, cols].astype(F32)).astype(BF16)

    act = jax.ShapeDtypeStruct((T, F), BF16)
    act_spec = pl.BlockSpec((tm, F), lambda i: (i, 0))
    row_spec = pl.BlockSpec((tm, D), lambda i: (i, 0))
    return _call(
        body, hosted, name=name, grid=(nsteps,),
        in_specs=[row_spec, _resident((F, D)), ANY, ANY],
        out_specs=[act_spec, act_spec, row_spec],
        out_shape=[act, act, jax.ShapeDtypeStruct((T, D), BF16)],
        scratch_shapes=[pltpu.VMEM((2, tm, F), BF16), pltpu.VMEM((2, tm, F), BF16), pltpu.SemaphoreType.DMA((4,))],
        args=[dh, w2, ga, gb])


def ffn_dw(xs, y, halves, name, hosted=()):
    T, F = xs[0].shape
    D = y.shape[1]
    nx, fh = len(xs), F // halves
    tk = min(T, 512)
    nk = T // tk

    def body(*refs):
        y_ref, x_refs, o_refs, accs = refs[0], refs[1:1 + nx], refs[1 + nx:1 + 2 * nx], refs[1 + 2 * nx:]
        k = pl.program_id(1)

        @pl.when(k == 0)
        def _():
            for acc in accs:
                acc[...] = jnp.zeros_like(acc)

        yy = y_ref[...]
        for x_ref, acc in zip(x_refs, accs):
            acc[...] += _dot_tn(x_ref[...], yy)

        @pl.when(k == nk - 1)
        def _():
            for o_ref, acc in zip(o_refs, accs):
                o_ref[...] = acc[...].astype(BF16)

    out = jax.ShapeDtypeStruct((F, D), BF16)
    return _call(
        body, hosted, name=name, grid=(halves, nk),
        in_specs=[pl.BlockSpec((tk, D), lambda j, k: (k, 0))] + [pl.BlockSpec((tk, fh), lambda j, k: (k, j))] * nx,
        out_specs=[pl.BlockSpec((fh, D), lambda j, k: (j, 0))] * nx,
        out_shape=[out] * nx,
        scratch_shapes=[pltpu.VMEM((fh, D), F32)] * nx,
        args=[y] + list(xs))


def ffn_bwd_in(da, db, w1, w3, h, gain, dh, name, hosted=()):
    T, F = da.shape
    D = h.shape[1]
    tm = min(T, 512)

    def body(da_ref, db_ref, w1_ref, w3_ref, h_ref, g_ref, dh_ref, o_ref, dg_ref):
        dn = _dot(da_ref[...], w1_ref[...]) + _dot(db_ref[...], w3_ref[...])
        dhn, dg = _rmsnorm_bwd(dn, h_ref[...], g_ref[...])
        o_ref[...] = dh_ref[...] + dhn

        @pl.when(pl.program_id(0) == 0)
        def _():
            dg_ref[...] = jnp.zeros_like(dg_ref)

        dg_ref[...] += jnp.sum(dg, axis=0, keepdims=True)

    act_spec = pl.BlockSpec((tm, F), lambda i: (i, 0))
    row_spec = pl.BlockSpec((tm, D), lambda i: (i, 0))
    vec_spec = pl.BlockSpec((1, D), lambda i: (0, 0))
    return _call(
        body, hosted, name=name, grid=(T // tm,),
        in_specs=[act_spec, act_spec, _resident((F, D)), _resident((F, D)), row_spec, vec_spec, row_spec],
        out_specs=[row_spec, vec_spec],
        out_shape=[jax.ShapeDtypeStruct((T, D), F32), jax.ShapeDtypeStruct((1, D), F32)],
        args=[da, db, w1, w3, h, gain, dh])


def mix_in(h, gain, wing, name, hosted=()):
    T, D = h.shape
    nsh, _, Cs = wing.shape
    tm = min(T, 512)

    def body(h_ref, g_ref, w_ref, u_ref, p_ref):
        hh = h_ref[...]
        u = (hh * _rstd(hh) * g_ref[...]).astype(BF16)
        u_ref[...] = u
        for j in range(nsh):
            p_ref[:, j * Cs:(j + 1) * Cs] = _dot(u, w_ref[j])

    return _call(
        body, hosted, name=name, grid=(T // tm,),
        in_specs=[pl.BlockSpec((tm, D), lambda i: (i, 0)), pl.BlockSpec((1, D), lambda i: (0, 0)),
                  pl.BlockSpec((nsh, D, Cs), lambda i: (0, 0, 0))],
        out_specs=[pl.BlockSpec((tm, D), lambda i: (i, 0)), pl.BlockSpec((tm, nsh * Cs), lambda i: (i, 0))],
        out_shape=[jax.ShapeDtypeStruct((T, D), BF16), jax.ShapeDtypeStruct((T, nsh * Cs), F32)],
        args=[h, gain, wing])


def mix_out(a, b, woutg, h, name, hosted=()):
    T, W = a.shape
    D = h.shape[1]
    wout = woutg.reshape(2, W, D)
    tm = min(T, 512)

    def body(a_ref, b_ref, w_ref, h_ref, o_ref):
        o_ref[...] = h_ref[...] + _dot(a_ref[...], w_ref[0]) + _dot(b_ref[...], w_ref[1])

    return _call(
        body, hosted, name=name, grid=(T // tm,),
        in_specs=[pl.BlockSpec((tm, W), lambda i: (i, 0)), pl.BlockSpec((tm, W), lambda i: (i, 0)),
                  pl.BlockSpec((2, W, D), lambda i: (0, 0, 0)), pl.BlockSpec((tm, D), lambda i: (i, 0))],
        out_specs=[pl.BlockSpec((tm, D), lambda i: (i, 0))],
        out_shape=[jax.ShapeDtypeStruct((T, D), F32)],
        args=[a, b, wout, h])


def mix_out_bwd(dh, woutg, a, b, name, hosted=()):
    T, D = dh.shape
    W = a.shape[1]
    nsh, Rs, _ = woutg.shape
    wout = woutg.reshape(2, W, D)
    tk = min(T, 512)
    nk = T // tk

    def body(dh_ref, w_ref, a_ref, b_ref, da_ref, db_ref, dw_ref, acc):
        k = pl.program_id(0)

        @pl.when(k == 0)
        def _():
            acc[...] = jnp.zeros_like(acc)

        dhb = dh_ref[...].astype(BF16)
        da_ref[...] = _dot_nt(dhb, w_ref[0])
        db_ref[...] = _dot_nt(dhb, w_ref[1])
        acc[0:W, :] += _dot_tn(a_ref[...], dhb)
        acc[W:2 * W, :] += _dot_tn(b_ref[...], dhb)

        @pl.when(k == nk - 1)
        def _():
            for j in range(nsh):
                dw_ref[j] = acc[j * Rs:(j + 1) * Rs, :].astype(BF16)

    return _call(
        body, hosted, name=name, grid=(nk,),
        in_specs=[pl.BlockSpec((tk, D), lambda k: (k, 0)), pl.BlockSpec((2, W, D), lambda k: (0, 0, 0)),
                  pl.BlockSpec((tk, W), lambda k: (k, 0)), pl.BlockSpec((tk, W), lambda k: (k, 0))],
        out_specs=[pl.BlockSpec((tk, W), lambda k: (k, 0)), pl.BlockSpec((tk, W), lambda k: (k, 0)),
                   pl.BlockSpec((nsh, Rs, D), lambda k: (0, 0, 0))],
        out_shape=[jax.ShapeDtypeStruct((T, W), F32), jax.ShapeDtypeStruct((T, W), F32),
                   jax.ShapeDtypeStruct((nsh, Rs, D), BF16)],
        scratch_shapes=[pltpu.VMEM((2 * W, D), F32)],
        args=[dh, wout, a, b])


def _dproj_block(g):
    return (g // N_GROUPS + N_GROUPS) % (N_GROUPS + 1), g % N_GROUPS


def mix_dwin(u, dproj, nsh, name, hosted=()):
    T, D = u.shape
    Hd = HEAD_DIM
    slabs, _, width = dproj.shape
    blocks = slabs * width // Hd
    Cs = blocks * Hd // nsh
    tk = min(T, 512)
    nk = T // tk

    def body(u_ref, d_ref, o_ref, acc):
        k = pl.program_id(0)

        @pl.when(k == 0)
        def _():
            acc[...] = jnp.zeros_like(acc)

        where = [_dproj_block(g) for g in range(blocks)]
        d = jnp.concatenate([d_ref[slab, :, col * Hd:(col + 1) * Hd] for slab, col in where], axis=1)
        acc[...] += _dot_tn(u_ref[...], d)

        @pl.when(k == nk - 1)
        def _():
            for j in range(nsh):
                o_ref[j] = acc[:, j * Cs:(j + 1) * Cs].astype(BF16)

    return _call(
        body, hosted, name=name, grid=(nk,),
        in_specs=[pl.BlockSpec((tk, D), lambda k: (k, 0)), pl.BlockSpec((slabs, tk, width), lambda k: (0, k, 0))],
        out_specs=[pl.BlockSpec((nsh, D, Cs), lambda k: (0, 0, 0))],
        out_shape=[jax.ShapeDtypeStruct((nsh, D, Cs), BF16)],
        scratch_shapes=[pltpu.VMEM((D, blocks * Hd), F32)],
        args=[u, dproj])


def mix_in_bwd(dproj, wing, h, gain, dh, name, hosted=()):
    T, D = h.shape
    nsh, _, Cs = wing.shape
    Hd = HEAD_DIM
    per = Cs // Hd
    tm = min(T, 512)

    def body(d_ref, w_ref, h_ref, g_ref, dh_ref, o_ref, dg_ref):
        def shard(j):
            blocks = [_dproj_block(per * j + i) for i in range(per)]
            return jnp.concatenate([d_ref[slab, :, col * Hd:(col + 1) * Hd] for slab, col in blocks], axis=1)

        du = _dot_nt(shard(0), w_ref[0])
        for j in range(1, nsh):
            du += _dot_nt(shard(j), w_ref[j])
        dhn, dg = _rmsnorm_bwd(du, h_ref[...], g_ref[...])
        o_ref[...] = dh_ref[...] + dhn

        @pl.when(pl.program_id(0) == 0)
        def _():
            dg_ref[...] = jnp.zeros_like(dg_ref)

        dg_ref[...] += jnp.sum(dg, axis=0, keepdims=True)

    row_spec = pl.BlockSpec((tm, D), lambda i: (i, 0))
    vec_spec = pl.BlockSpec((1, D), lambda i: (0, 0))
    return _call(
        body, hosted, name=name, grid=(T // tm,),
        in_specs=[pl.BlockSpec((dproj.shape[0], tm, dproj.shape[2]), lambda i: (0, i, 0)),
                  pl.BlockSpec((nsh, D, Cs), lambda i: (0, 0, 0)), row_spec, vec_spec, row_spec],
        out_specs=[row_spec, vec_spec],
        out_shape=[jax.ShapeDtypeStruct((T, D), F32), jax.ShapeDtypeStruct((1, D), F32)],
        args=[dproj, wing, h, gain, dh])


def _pool_window(x, group, T, trailing):
    rows = lax.broadcasted_iota(jnp.int32, x.shape, 0)

    def shifted(z, k):
        if trailing:
            return jnp.where(rows >= k, pltpu.roll(z, k, 0), 0.0)
        return jnp.where(rows < T - k, pltpu.roll(z, T - k, 0), 0.0)

    s2 = x + shifted(x, 1)
    s4 = s2 + shifted(s2, 2)
    s8 = s4 + shifted(s4, 4)
    s16 = s8 + shifted(s8, 8)
    return jnp.where(group == 0, s2, jnp.where(group == 1, s4, jnp.where(group == 2, s8, s16)))


def _pool_count(group, shape):
    rows = lax.broadcasted_iota(jnp.int32, shape, 0)
    w = jnp.where(group == 0, 2, jnp.where(group == 1, 4, jnp.where(group == 2, 8, 16)))
    return jnp.minimum(rows + 1, w).astype(F32)


def pool_fwd(proj, pool_w, pool_scale, name, hosted=()):
    T = proj.shape[0]
    Hd = HEAD_DIM

    def body(x_ref, w_ref, sc_ref, a_ref):
        g = pl.program_id(0)
        x = x_ref[...]
        pooled = _pool_window(x, g, T, True) / _pool_count(g, x.shape) - x
        a_ref[...] = (_dot(pooled.astype(BF16), w_ref[0].astype(BF16)) * sc_ref[...]).astype(BF16)

    return _call(
        body, hosted, name=name, grid=(N_GROUPS,),
        in_specs=[pl.BlockSpec((T, Hd), lambda g: (0, g)), pl.BlockSpec((1, Hd, Hd), lambda g: (g, 0, 0)),
                  pl.BlockSpec((1, Hd), lambda g: (0, g))],
        out_specs=[pl.BlockSpec((T, Hd), lambda g: (0, g))],
        out_shape=[jax.ShapeDtypeStruct((T, N_GROUPS * Hd), BF16)],
        args=[proj, pool_w, pool_scale])


def pool_bwd(proj, da, pool_w, pool_scale, name, hosted=()):
    T = proj.shape[0]
    Hd = HEAD_DIM

    def body(x_ref, da_ref, w_ref, sc_ref, dx_ref, dw_ref, dsc_ref):
        g = pl.program_id(0)
        x = x_ref[...]
        cnt = _pool_count(g, x.shape)
        pooled = (_pool_window(x, g, T, True) / cnt - x).astype(BF16)
        wb = w_ref[0].astype(BF16)
        dav = da_ref[...]
        dsc_ref[...] = jnp.sum(dav * _dot(pooled, wb), axis=0, keepdims=True)
        dout = (dav * sc_ref[...]).astype(BF16)
        dw_ref[0] = _dot_tn(pooled, dout)
        dpooled = _dot_nt(dout, wb)
        dx_ref[0] = (_pool_window(dpooled / cnt, g, T, False) - dpooled).astype(BF16)

    col_spec = pl.BlockSpec((T, Hd), lambda g: (0, g))
    return _call(
        body, hosted, name=name, grid=(N_GROUPS,),
        in_specs=[col_spec, col_spec, pl.BlockSpec((1, Hd, Hd), lambda g: (g, 0, 0)), pl.BlockSpec((1, Hd), lambda g: (0, g))],
        out_specs=[pl.BlockSpec((1, T, Hd), lambda g: (N_GROUPS, 0, g)), pl.BlockSpec((1, Hd, Hd), lambda g: (g, 0, 0)),
                   pl.BlockSpec((1, Hd), lambda g: (0, g))],
        out_shape=[jax.ShapeDtypeStruct((N_GROUPS + 1, T, N_GROUPS * Hd), BF16), jax.ShapeDtypeStruct((N_GROUPS, Hd, Hd), F32),
                   jax.ShapeDtypeStruct((1, N_GROUPS * Hd), F32)],
        args=[proj, da, pool_w, pool_scale])


def _ret_tables(T):
    Hd, C = HEAD_DIM, RET_CHUNK
    inv_freq = 1.0 / (ROPE_BASE ** (jnp.arange(0, Hd, 2, dtype=F32) / Hd))
    ang = jnp.arange(T, dtype=F32)[:, None] * inv_freq[None, :]
    cos, sin = jnp.cos(ang), jnp.sin(ang)
    cos2 = jnp.concatenate([cos, cos], axis=-1)
    sin2 = jnp.concatenate([-sin, sin], axis=-1)
    log_gamma = jnp.log1p(-jnp.exp2(-5.0 - jnp.arange(N_GROUPS, dtype=F32)))
    pos = jnp.arange(C, dtype=F32)
    rel = pos[:, None] - pos[None, :]
    intra = jnp.where(rel[None] >= 0, jnp.exp(log_gamma[:, None, None] * jnp.maximum(rel, 0.0)[None]), 0.0)
    k_tail = jnp.exp(log_gamma[:, None] * (C - 1 - pos)[None, :])
    q_head = jnp.exp(log_gamma[:, None] * (pos + 1.0)[None, :])
    chunk_decay = jnp.exp(log_gamma * C)
    wide = lambda t: jnp.broadcast_to(t[:, :, None], (N_GROUPS, C, Hd))
    return cos2, sin2, intra, wide(k_tail), wide(q_head), jnp.broadcast_to(chunk_decay[:, None, None], (N_GROUPS, 1, Hd))


def _rope(x, cos2, sin2):
    return x * cos2 + pltpu.roll(x, HEAD_DIM // 2, 1) * sin2


def _rope_t(d, cos2, sin2):
    return d * cos2 + pltpu.roll(d * sin2, HEAD_DIM // 2, 1)


def _ret_specs(tseg, seg_of):
    Hd, G = HEAD_DIM, N_GROUPS
    col = lambda kind: pl.BlockSpec((tseg, Hd), lambda h, s: (seg_of(s), G * kind + h))
    tab = pl.BlockSpec((tseg, Hd), lambda h, s: (seg_of(s), 0))
    head = pl.BlockSpec((1, RET_CHUNK, Hd), lambda h, s: (h, 0, 0))
    cd = pl.BlockSpec((1, 1, Hd), lambda h, s: (h, 0, 0))
    gain = pl.BlockSpec((1, Hd), lambda h, s: (0, h))
    return col, tab, head, cd, gain


def ret_fwd(proj, ret_norm, tables, name, hosted=()):
    T = proj.shape[0]
    Hd, C, G = HEAD_DIM, RET_CHUNK, N_GROUPS
    tseg = min(T, 1024)
    nseg, nck = T // tseg, tseg // C
    scale = Hd ** -0.5
    cos2, sin2, intra, k_tail, q_head, chunk_decay = tables

    def body(q_ref, k_ref, v_ref, g_ref, gain_ref, cos_ref, sin_ref, m_ref, kt_ref, qh_ref, cd_ref,
             b_ref, o_ref, rp_ref, state):
        @pl.when(pl.program_id(1) == 0)
        def _():
            state[...] = jnp.zeros_like(state)

        def chunk(ci, carry):
            rows = pl.ds(pl.multiple_of(ci * C, C), C)
            cos, sin = cos_ref[rows, :], sin_ref[rows, :]
            qr = _rope(q_ref[rows, :], cos, sin)
            kr = _rope(k_ref[rows, :], cos, sin) * scale
            qb, kb, vb = qr.astype(BF16), kr.astype(BF16), v_ref[rows, :].astype(BF16)
            r = state[...]
            rp_ref[0, ci] = r.astype(BF16)
            sc = _dot_nt(qb, kb) * m_ref[0]
            o = _dot(sc.astype(BF16), vb) + _dot((qr * qh_ref[0]).astype(BF16), r.astype(BF16))
            state[...] = cd_ref[0] * r + _dot_tn((kr * kt_ref[0]).astype(BF16), vb)
            o_ref[rows, :] = o
            on = o * _rstd(o)
            b_ref[rows, :] = (jax.nn.silu(g_ref[rows, :]) * (on * gain_ref[...])).astype(BF16)
            return carry

        lax.fori_loop(0, nck, chunk, 0, unroll=True)

    col, tab, head, cd, gain = _ret_specs(tseg, lambda s: s)
    out_col = pl.BlockSpec((tseg, Hd), lambda h, s: (s, h))
    return _call(
        body, hosted, name=name, grid=(G, nseg),
        in_specs=[col(1), col(2), col(3), col(4), gain, tab, tab, head, head, head, cd],
        out_specs=[out_col, out_col, pl.BlockSpec((1, nck, Hd, Hd), lambda h, s: (h, s, 0, 0))],
        out_shape=[jax.ShapeDtypeStruct((T, G * Hd), BF16), jax.ShapeDtypeStruct((T, G * Hd), F32),
                   jax.ShapeDtypeStruct((G, T // C, Hd, Hd), BF16)],
        scratch_shapes=[pltpu.VMEM((Hd, Hd), F32)],
        args=[proj, proj, proj, proj, ret_norm, cos2, sin2, intra, k_tail, q_head, chunk_decay])


def ret_bwd(proj, db, o_pre, r_prev, ret_norm, tables, dproj, name, hosted=()):
    T = proj.shape[0]
    Hd, C, G = HEAD_DIM, RET_CHUNK, N_GROUPS
    tseg = min(T, 1024)
    nseg, nck = T // tseg, tseg // C
    scale = Hd ** -0.5
    cos2, sin2, intra, k_tail, q_head, chunk_decay = tables

    def body(q_ref, k_ref, v_ref, g_ref, db_ref, o_ref, rp_ref, gain_ref, cos_ref, sin_ref, m_ref, kt_ref, qh_ref, cd_ref,
             _, d_ref, dgain_ref, gstate):
        @pl.when(pl.program_id(1) == 0)
        def _():
            gstate[...] = jnp.zeros_like(gstate)
            dgain_ref[...] = jnp.zeros_like(dgain_ref)

        def chunk(t, carry):
            ci = nck - 1 - t
            rows = pl.ds(pl.multiple_of(ci * C, C), C)
            cos, sin = cos_ref[rows, :], sin_ref[rows, :]
            qr = _rope(q_ref[rows, :], cos, sin)
            kr = _rope(k_ref[rows, :], cos, sin) * scale
            qb, kb, vb = qr.astype(BF16), kr.astype(BF16), v_ref[rows, :].astype(BF16)
            qhb, ktb = (qr * qh_ref[0]).astype(BF16), (kr * kt_ref[0]).astype(BF16)
            sc = (_dot_nt(qb, kb) * m_ref[0]).astype(BF16)
            o = o_ref[rows, :]
            rstd = _rstd(o)
            on = o * rstd
            gain = gain_ref[...]
            silu, dsilu = _silu_parts(g_ref[rows, :])
            dy = db_ref[rows, :]
            dgain_ref[...] += jnp.sum(dy * silu * on, axis=0, keepdims=True)
            dg = dy * on * gain * dsilu
            don = dy * silu * gain
            dob = (rstd * (don - on * jnp.mean(don * on, axis=-1, keepdims=True))).astype(BF16)
            gn = gstate[...]
            gb = gn.astype(BF16)
            da = (_dot_nt(dob, vb) * m_ref[0]).astype(BF16)
            dq = _dot(da, kb) + _dot_nt(dob, rp_ref[0, ci]) * qh_ref[0]
            dk = _dot_tn(da, qb) + _dot_nt(vb, gb) * kt_ref[0]
            dv = _dot_tn(sc, dob) + _dot(ktb, gb)
            gstate[...] = cd_ref[0] * gn + _dot_tn(qhb, dob)
            d_ref[0, rows, :] = _rope_t(dq, cos, sin).astype(BF16)
            d_ref[1, rows, :] = _rope_t(dk * scale, cos, sin).astype(BF16)
            d_ref[2, rows, :] = dv.astype(BF16)
            d_ref[3, rows, :] = dg.astype(BF16)
            return carry

        lax.fori_loop(0, nck, chunk, 0, unroll=True)

    rev = lambda s: nseg - 1 - s
    col, tab, head, cd, gain = _ret_specs(tseg, rev)
    act = pl.BlockSpec((tseg, Hd), lambda h, s: (rev(s), h))
    return _call(
        body, hosted, name=name, grid=(G, nseg),
        in_specs=[col(1), col(2), col(3), col(4), act, act, pl.BlockSpec((1, nck, Hd, Hd), lambda h, s: (h, rev(s), 0, 0)),
                  gain, tab, tab, head, head, head, cd, ANY],
        out_specs=[pl.BlockSpec((4, tseg, Hd), lambda h, s: (0, rev(s), h)), gain],
        out_shape=[jax.ShapeDtypeStruct(dproj.shape, BF16), jax.ShapeDtypeStruct((1, G * Hd), F32)],
        scratch_shapes=[pltpu.VMEM((Hd, Hd), F32)], aliased={14: 0},
        args=[proj, proj, proj, proj, db, o_pre, r_prev, ret_norm, cos2, sin2, intra, k_tail, q_head, chunk_decay, dproj])


def final_loss(h, gain, target, name, hosted=()):
    T, D = h.shape
    tm = min(T, 512)

    def body(h_ref, g_ref, t_ref, dh_ref, loss_ref, dg_ref):
        @pl.when(pl.program_id(0) == 0)
        def _():
            loss_ref[...] = jnp.zeros_like(loss_ref)
            dg_ref[...] = jnp.zeros_like(dg_ref)

        hh = h_ref[...]
        gain_v = g_ref[...]
        err = hh * _rstd(hh) * gain_v - t_ref[...]
        loss_ref[...] += 0.5 * jnp.sum(jnp.mean(err * err, axis=-1, keepdims=True), axis=0, keepdims=True)
        dhn, dg = _rmsnorm_bwd(err * (1.0 / D), hh, gain_v)
        dh_ref[...] = dhn
        dg_ref[...] += jnp.sum(dg, axis=0, keepdims=True)

    row_spec = pl.BlockSpec((tm, D), lambda i: (i, 0))
    vec_spec = pl.BlockSpec((1, D), lambda i: (0, 0))
    return _call(
        body, hosted, name=name, grid=(T // tm,),
        in_specs=[row_spec, vec_spec, row_spec],
        out_specs=[row_spec, pl.BlockSpec((1, 128), lambda i: (0, 0)), vec_spec],
        out_shape=[jax.ShapeDtypeStruct((T, D), F32), jax.ShapeDtypeStruct((1, 128), F32), jax.ShapeDtypeStruct((1, D), F32)],
        args=[h, gain, target])


def prereduce(grads, recvs, place, name):
    nt = len(grads)
    nsh, R, C = grads[0].shape
    rh = R // 2

    def body(place_ref, *refs):
        for t in range(nt):
            g_ref, r_ref, o_ref, own_ref = refs[2 * t], refs[2 * t + 1], refs[2 * nt + 2 * t], refs[2 * nt + 2 * t + 1]
            piece = (g_ref[...].astype(F32) + r_ref[...].astype(F32)).astype(BF16)
            o_ref[...] = piece

            @pl.when(pl.program_id(0) == place_ref[1])
            def _():
                own_ref[...] = piece

    outs = pl.pallas_call(
        body, name=name,
        grid_spec=pltpu.PrefetchScalarGridSpec(
            num_scalar_prefetch=1, grid=(nsh,),
            in_specs=[pl.BlockSpec((1, rh, C), lambda j, p: (j, p[0], 0)), pl.BlockSpec((1, rh, C), lambda j, p: (j, 0, 0))] * nt,
            out_specs=[pl.BlockSpec((1, rh, C), lambda j, p: (j, 0, 0)),
                       pl.BlockSpec((1, rh, C), lambda j, p: (p[1], p[0], 0))] * nt),
        out_shape=[jax.ShapeDtypeStruct((nsh, rh, C), BF16), jax.ShapeDtypeStruct((nsh, R, C), BF16)] * nt,
        compiler_params=pltpu.CompilerParams(vmem_limit_bytes=VMEM_LIMIT_V7X),
    )(place, *[a for pair in zip(grads, recvs) for a in pair])
    return [(outs[2 * t], outs[2 * t + 1]) for t in range(nt)]


def _adamw(w, g, m, v):
    m = ADAM_B1 * m + (1.0 - ADAM_B1) * g
    v = ADAM_B2 * v + (1.0 - ADAM_B2) * (g * g)
    m_hat = m / (1.0 - ADAM_B1 ** ADAM_STEP)
    v_hat = v / (1.0 - ADAM_B2 ** ADAM_STEP)
    return -ADAM_LR * (m_hat / (jnp.sqrt(v_hat) + ADAM_EPS) + ADAM_WD * w), m, v


def adamw_sharded(tensors, name, hosted=()):
    nt = len(tensors)
    nsh, R, C = tensors[0][0].shape
    lanes = -(-C // 128) * 128
    per_row = 2 * nt * lanes * (nsh * 2 + 7 * 4)
    tr = max(r for r in range(16, R + 1, 16) if R % r == 0 and r * per_row <= ADAMW_VMEM_BUDGET)

    def body(*refs):
        ins, outs = refs[:4 * nt], refs[4 * nt:]
        for t in range(nt):
            p_ref, w_ref, m_ref, v_ref = ins[4 * t:4 * t + 4]
            g_ref, d_ref, nm_ref, nv_ref = outs[4 * t:4 * t + 4]
            g = p_ref[0].astype(F32)
            for i in range(1, nsh):
                g += p_ref[i].astype(F32)
            g_ref[...] = g
            d_ref[...], nm_ref[...], nv_ref[...] = _adamw(w_ref[...], g, m_ref[...], v_ref[...])

    spec = pl.BlockSpec((tr, C), lambda i: (i, 0))
    out = jax.ShapeDtypeStruct((R, C), F32)
    return _call(
        body, hosted, name=name, grid=(R // tr,),
        in_specs=[pl.BlockSpec((nsh, tr, C), lambda i: (0, i, 0)), spec, spec, spec] * nt,
        out_specs=[spec] * (4 * nt), out_shape=[out] * (4 * nt),
        args=[a for tensor in tensors for a in tensor])


def adamw_small(packs, late, pool, vectors, name):
    ndev = packs.shape[0]
    rp, rv, rl = pool[0].shape[0], vectors.shape[0] // 3, late.shape[1]

    def body(p_ref, l_ref, wp, mp, vp, wmv, gp, dp, nmp, nvp, gv, dv, nmv, nvv, loss_ref):
        wv, mv, vv = wmv.at[0:rv], wmv.at[rv:2 * rv], wmv.at[2 * rv:3 * rv]
        g, first = p_ref[0], l_ref[0]
        for i in range(1, ndev):
            g += p_ref[i]
            first += l_ref[i]
        g_pool = g[0:rp]
        g_vec = jnp.concatenate([g[rp:rp + rl] + first, g[rp + rl:rp + rv]], axis=0)
        gp[...], gv[...], loss_ref[...] = g_pool, g_vec, g[rp + rv:rp + rv + 8]
        dp[...], nmp[...], nvp[...] = _adamw(wp[...], g_pool, mp[...], vp[...])
        dv[...], nmv[...], nvv[...] = _adamw(wv[...], g_vec, mv[...], vv[...])

    shape = lambda rows: jax.ShapeDtypeStruct((rows, 128), F32)
    outs = pl.pallas_call(body, name=name, out_shape=[shape(rp)] * 4 + [shape(rv)] * 4 + [shape(8)],
                          compiler_params=pltpu.CompilerParams(vmem_limit_bytes=VMEM_LIMIT_V7X))(packs, late, *pool, vectors)
    return outs[0:4], outs[4:8], outs[8]


BIG = ("ffn1_w1", "ffn1_w3", "ffn1_w2", "w_in", "w_out", "ffn2_w1", "ffn2_w3", "ffn2_w2")
TRANSPOSED = ("ffn1_w1", "ffn1_w3", "ffn2_w1", "ffn2_w3")
VECTORS = ("ffn1_norm", "mix_norm", "pool_scale", "ret_norm", "ffn2_norm", "final_norm")
WEIGHTS = ("ffn1_norm", "ffn1_w1", "ffn1_w3", "ffn1_w2", "mix_norm", "w_in", "pool_w", "pool_scale", "ret_norm", "w_out",
           "ffn2_norm", "ffn2_w1", "ffn2_w3", "ffn2_w2", "final_norm")


def _pack_vectors(parts):
    return jnp.concatenate([parts[k].reshape(-1, 128) for k in VECTORS], axis=0)


def _unpack_vectors(pack, like):
    out, row = {}, 0
    for k in VECTORS:
        rows = like[k].size // 128
        out[k] = pack[row:row + rows].reshape(like[k].shape)
        row += rows
    return out


def kernel(x, ffn1_norm, ffn1_w1, ffn1_w3, ffn1_w2, mix_norm, w_in, pool_w, pool_scale, ret_norm, w_out, ffn2_norm, ffn2_w1, ffn2_w3, ffn2_w2, final_norm, loss_target, m_ffn1_norm, m_ffn1_w1, m_ffn1_w3, m_ffn1_w2, m_mix_norm, m_w_in, m_pool_w, m_pool_scale, m_ret_norm, m_w_out, m_ffn2_norm, m_ffn2_w1, m_ffn2_w3, m_ffn2_w2, m_final_norm, v_ffn1_norm, v_ffn1_w1, v_ffn1_w3, v_ffn1_w2, v_mix_norm, v_w_in, v_pool_w, v_pool_scale, v_ret_norm, v_w_out, v_ffn2_norm, v_ffn2_w1, v_ffn2_w3, v_ffn2_w2, v_final_norm):
    w = dict(ffn1_norm=ffn1_norm, ffn1_w1=ffn1_w1, ffn1_w3=ffn1_w3, ffn1_w2=ffn1_w2, mix_norm=mix_norm, w_in=w_in, pool_w=pool_w,
             pool_scale=pool_scale, ret_norm=ret_norm, w_out=w_out, ffn2_norm=ffn2_norm, ffn2_w1=ffn2_w1, ffn2_w3=ffn2_w3,
             ffn2_w2=ffn2_w2, final_norm=final_norm)
    m = dict(ffn1_norm=m_ffn1_norm, ffn1_w1=m_ffn1_w1, ffn1_w3=m_ffn1_w3, ffn1_w2=m_ffn1_w2, mix_norm=m_mix_norm, w_in=m_w_in,
             pool_w=m_pool_w, pool_scale=m_pool_scale, ret_norm=m_ret_norm, w_out=m_w_out, ffn2_norm=m_ffn2_norm, ffn2_w1=m_ffn2_w1,
             ffn2_w3=m_ffn2_w3, ffn2_w2=m_ffn2_w2, final_norm=m_final_norm)
    v = dict(ffn1_norm=v_ffn1_norm, ffn1_w1=v_ffn1_w1, ffn1_w3=v_ffn1_w3, ffn1_w2=v_ffn1_w2, mix_norm=v_mix_norm, w_in=v_w_in,
             pool_w=v_pool_w, pool_scale=v_pool_scale, ret_norm=v_ret_norm, w_out=v_w_out, ffn2_norm=v_ffn2_norm, ffn2_w1=v_ffn2_w1,
             ffn2_w3=v_ffn2_w3, ffn2_w2=v_ffn2_w2, final_norm=v_final_norm)
    xs, target = x[0], loss_target[0]
    T = xs.shape[0]
    tables = _ret_tables(T)
    place = jnp.stack([lax.axis_index("c"), 2 * lax.axis_index("x") + lax.axis_index("y")]).astype(jnp.int32)
    local = lambda d, k: jnp.transpose(d[k][0]) if k in TRANSPOSED else d[k][0]
    result = lambda o, k: jnp.transpose(o)[None] if k in TRANSPOSED else o[None]
    first = ("ffn1_w1", "ffn1_w3")
    sh = {k: local(w, k).astype(BF16) for k in first}
    gather = lambda *names: [ChipExchange([sh[k] for k in names], False)]
    wg, grad, delta, new_m, new_v = {}, {}, {}, {}, {}

    def update(names, pieces, name, hosted=()):
        outs, extras = adamw_sharded([(p, local(w, k), local(m, k), local(v, k)) for k, p in zip(names, pieces)], name, hosted)
        for t, k in enumerate(names):
            grad[k], delta[k], new_m[k], new_v[k] = [result(o, k) for o in outs[4 * t:4 * t + 4]]
        return extras

    def reduce_in_chip(name, *pairs):
        reduced = prereduce([p for p, _ in pairs], [r for _, r in pairs], place, "prereduce_" + name)
        return reduced[0] if len(pairs) == 1 else reduced

    scatter = lambda *reduced: ChipExchange([r[0] for r in reduced], True, [r[1] for r in reduced])
    whole = lambda k: wg[k].reshape(-1, wg[k].shape[-1])
    sharded = lambda g: g.reshape(N_CHIPS, -1, g.shape[-1])

    later = [k for k in BIG if k not in first]
    casts, ((wg["ffn1_w1"], wg["ffn1_w3"]),) = cast_shards([local(w, k) for k in later], "cast_gather_ffn1", gather(*first))
    sh.update(zip(later, casts))
    (n1, ga1, gb1, s1), ((wg["ffn1_w2"], wg["w_in"]),) = ffn_up(
        xs, ffn1_norm, whole("ffn1_w1"), whole("ffn1_w3"), "ffn1_up", gather("ffn1_w2", "w_in"))
    (h1,), ((wg["w_out"],),) = ffn_down(s1, whole("ffn1_w2"), xs, "ffn1_down", gather("w_out"))
    (u, proj), ((wg["ffn2_w1"],),) = mix_in(h1, mix_norm, wg["w_in"], "mix_in", gather("ffn2_w1"))
    (pa,), _ = pool_fwd(proj, pool_w[0], pool_scale, "pool_fwd")
    (rb, o_pre, r_prev), ((wg["ffn2_w3"],),) = ret_fwd(proj, ret_norm, tables, "ret_fwd", gather("ffn2_w3"))
    (h2,), _ = mix_out(pa, rb, wg["w_out"], h1, "mix_out")
    (n2, ga2, gb2, s2), ((wg["ffn2_w2"],),) = ffn_up(
        h2, ffn2_norm, whole("ffn2_w1"), whole("ffn2_w3"), "ffn2_up", gather("ffn2_w2"))
    (h3,), _ = ffn_down(s2, whole("ffn2_w2"), h2, "ffn2_down")
    (dh3, loss, d_final), _ = final_loss(h3, final_norm[None], target, "final_loss")

    (da2, db2, df2), _ = ffn_bwd_act(dh3, whole("ffn2_w2"), ga2, gb2, "ffn2_bwd_act")
    (g_f2w2,), _ = ffn_dw([s2], df2, 1, "ffn2_dw2")
    g_f2w2 = sharded(g_f2w2)
    (g_f2w1, g_f2w3), ((r_f2w2,),) = ffn_dw([da2, db2], n2, 2, "ffn2_dw13", [SiblingExchange([g_f2w2])])
    g_f2w1, g_f2w3 = sharded(g_f2w1), sharded(g_f2w3)
    p_f2w2 = reduce_in_chip("ffn2_w2", (g_f2w2, r_f2w2))
    (dh2, d_ffn2), ((q_f2w2,), (r_f2w1, r_f2w3)) = ffn_bwd_in(
        da2, db2, whole("ffn2_w1"), whole("ffn2_w3"), h2, ffn2_norm, dh3, "ffn2_bwd_in",
        [scatter(p_f2w2), SiblingExchange([g_f2w1, g_f2w3])])
    p_f2w1, p_f2w3 = reduce_in_chip("ffn2_w13", (g_f2w1, r_f2w1), (g_f2w3, r_f2w3))
    (dpa, drb, g_wout), _ = mix_out_bwd(dh2, wg["w_out"], pa, rb, "mix_out_bwd")
    (dproj, d_pool_w, d_pool_scale), _ = pool_bwd(proj, dpa, pool_w[0], pool_scale, "pool_bwd")
    (dproj, d_ret_norm), ((q_f2w1, q_f2w3), (r_wout,)) = ret_bwd(
        proj, drb, o_pre, r_prev, ret_norm, tables, dproj, "ret_bwd", [scatter(p_f2w1, p_f2w3), SiblingExchange([g_wout])])
    p_wout = reduce_in_chip("w_out", (g_wout, r_wout))
    (g_win,), ((q_wout,),) = mix_dwin(u, dproj, N_CHIPS, "mix_dwin", [scatter(p_wout)])
    (dh1, d_mix), ((r_win,),) = mix_in_bwd(dproj, wg["w_in"], h1, mix_norm, dh2, "mix_in_bwd", [SiblingExchange([g_win])])
    p_win = reduce_in_chip("w_in", (g_win, r_win))
    (da1, db1, df1), ((q_win,),) = ffn_bwd_act(dh1, whole("ffn1_w2"), ga1, gb1, "ffn1_bwd_act", [scatter(p_win)])
    d_vectors = {"ffn1_norm": jnp.zeros_like(ffn1_norm), "mix_norm": d_mix, "pool_scale": d_pool_scale,
                 "ret_norm": d_ret_norm, "ffn2_norm": d_ffn2, "final_norm": d_final}
    pack = jnp.concatenate([d_pool_w.reshape(-1, 128), _pack_vectors(d_vectors), jnp.broadcast_to(loss, (8, 128))], axis=0)
    (g_f1w1, g_f1w3), ((packs,),) = ffn_dw([da1, db1], n1, 2, "ffn1_dw13", [AllExchange(pack)])
    g_f1w1, g_f1w3 = sharded(g_f1w1), sharded(g_f1w3)
    (g_f1w2,), ((r_f1w1, r_f1w3),) = ffn_dw([s1], df1, 1, "ffn1_dw2", [SiblingExchange([g_f1w1, g_f1w3])])
    g_f1w2 = sharded(g_f1w2)
    p_f1w1, p_f1w3 = reduce_in_chip("ffn1_w13", (g_f1w1, r_f1w1), (g_f1w3, r_f1w3))
    (dx, d_ffn1), ((q_f1w1, q_f1w3), (r_f1w2,)) = ffn_bwd_in(
        da1, db1, whole("ffn1_w1"), whole("ffn1_w3"), xs, ffn1_norm, dh1, "ffn1_bwd_in",
        [scatter(p_f1w1, p_f1w3), SiblingExchange([g_f1w2])])
    p_f1w2 = reduce_in_chip("ffn1_w2", (g_f1w2, r_f1w2))

    (q_f1w2,), (late,) = update(["ffn2_w1", "ffn2_w3", "ffn1_w1", "ffn1_w3"], [q_f2w1, q_f2w3, q_f1w1, q_f1w3], "adamw_w13",
                                [scatter(p_f1w2), AllExchange(d_ffn1.reshape(-1, 128))])
    update(["ffn2_w2", "ffn1_w2"], [q_f2w2, q_f1w2], "adamw_w2")
    update(["w_in"], [q_win], "adamw_w_in")
    update(["w_out"], [q_wout], "adamw_w_out")
    of_pool, of_vectors, loss_sum = adamw_small(packs, late, [t["pool_w"].reshape(-1, 128) for t in (w, m, v)],
                                                jnp.concatenate([t[k].reshape(-1, 128) for t in (w, m, v) for k in VECTORS], axis=0),
                                                "adamw_small")
    for res, pool_part, vector_part in zip((grad, delta, new_m, new_v), of_pool, of_vectors):
        res["pool_w"] = pool_part.reshape(pool_w.shape)
        res.update(_unpack_vectors(vector_part, w))
    loss = loss_sum[0, 0]

    return (loss, dx[None], *[grad[k] for k in WEIGHTS], *[delta[k] for k in WEIGHTS],
            *[new_m[k] for k in WEIGHTS], *[new_v[k] for k in WEIGHTS])
```

```python
import math

import jax
import jax.numpy as jnp
from jax import lax
from jax.experimental import pallas as pl
from jax.experimental.pallas import tpu as pltpu

F32 = jnp.float32
BF16 = jnp.bfloat16

EPS = 1e-6
N_CHIPS = 4
N_GROUPS = 4
HEAD_DIM = 128
RET_CHUNK = 128
ROPE_BASE = 10000.0
ADAM_LR, ADAM_B1, ADAM_B2, ADAM_EPS, ADAM_WD, ADAM_STEP = 0.001, 0.9, 0.999, 1e-08, 0.01, 10
VMEM_LIMIT_V7X = 56 * 1024 * 1024
ADAMW_VMEM_BUDGET = 32 * 1024 * 1024
MESH = pl.DeviceIdType.MESH
ANY = pl.BlockSpec(memory_space=pl.ANY)


def _dot(a, b):
    return jnp.dot(a, b, preferred_element_type=F32)


def _dot_nt(a, b):
    return lax.dot_general(a, b, (((1,), (1,)), ((), ())), preferred_element_type=F32)


def _dot_tn(a, b):
    return lax.dot_general(a, b, (((0,), (0,)), ((), ())), preferred_element_type=F32)


def _rstd(h):
    return lax.rsqrt(jnp.mean(h * h, axis=-1, keepdims=True) + EPS)


def _rmsnorm_bwd(dn, h, gain):
    r = _rstd(h)
    nh = h * r
    dnh = dn * gain
    dh = r * (dnh - nh * jnp.mean(dnh * nh, axis=-1, keepdims=True))
    return dh, dn * nh


def _silu_parts(a):
    sig = jax.nn.sigmoid(a)
    silu = a * sig
    return silu, sig + silu * (1.0 - sig)


def _mesh_pos():
    return lax.axis_index("x"), lax.axis_index("y"), lax.axis_index("c")


class ChipExchange:
    def __init__(self, srcs, scatter, placed=()):
        n = len(srcs)
        self.inputs, self.scatter, self.n, self.reach = list(srcs) + list(placed), scatter, n, REACH_CHIPS
        self.aliases = {n + t: t for t in range(n)} if scatter else {}
        self.half_rows = [s.shape[1] if scatter else s.shape[0] // 2 for s in srcs]
        self.out_shape = [jax.ShapeDtypeStruct((N_CHIPS, 2 * rh, s.shape[-1]), s.dtype) for s, rh in zip(srcs, self.half_rows)]
        if scatter:
            self.out_shape += [jax.ShapeDtypeStruct((2, rh // 2, s.shape[-1]), s.dtype) for s, rh in zip(srcs, self.half_rows)]
        dma = pltpu.SemaphoreType.DMA
        self.sems = [dma((4 * n,)), dma((4 * n,)), dma((2 * n,)), dma((2 * n,)), dma((4 * n,)), dma((4 * n,))]

    def _copies(self, src, out, sems):
        hop1_send, hop1_recv, hop2_send, hop2_recv, d2d_send, d2d_recv = sems
        x, y, c = _mesh_pos()
        me, dg = 2 * x + y, 2 * (1 - x) + (1 - y)
        sibling = (x, y, 1 - c)
        n = self.n
        mine, theirs = c, 1 - c

        def nb(a):
            nx, ny = x ^ (1 - a), y ^ a
            return 2 * nx + ny, (nx, ny, c)

        def remote(s, d, send, recv, k, to):
            return pltpu.make_async_remote_copy(src_ref=s, dst_ref=d, send_sem=send.at[k], recv_sem=recv.at[k],
                                                device_id=to, device_id_type=MESH)

        class Copies:
            def slot(_, t, chip, half):
                rh = self.half_rows[t]
                return out[t].at[chip, pl.ds(half * rh, rh), :]

            def quarter(_, t, chip, q):
                qh = self.half_rows[t] // 2
                return out[t].at[chip, pl.ds(mine * 2 * qh + q * qh, qh), :]

            def own_shard(k, t):
                return remote(src[t], out[t].at[me], d2d_send, d2d_recv, 4 * t + 3, sibling)

            def hop1(k, t, a, transit=False):
                rh = self.half_rows[t]
                chip, to = nb(a)
                if transit:
                    piece = src[t].at[dg, pl.ds(a * (rh // 2), rh // 2), :]
                    return remote(piece, out[n + t].at[a], hop1_send, hop1_recv, 4 * t + 2 + a, to)
                piece = src[t].at[chip] if self.scatter else src[t].at[pl.ds(mine * rh, rh), :]
                return remote(piece, k.slot(t, me, mine), hop1_send, hop1_recv, 4 * t + a, to)

            def landed1(k, t, a, transit=False):
                here = out[n + t].at[a] if transit else k.slot(t, nb(a)[0], mine)
                return remote(here, here, hop1_send, hop1_recv, 4 * t + (2 if transit else 0) + a, sibling)

            def hop2(k, t, q):
                origin, to = nb(q)[0], nb(1 - q)[1]
                piece = out[n + t].at[q] if self.scatter else k.quarter(t, origin, q)
                return remote(piece, k.quarter(t, origin, q), hop2_send, hop2_recv, 2 * t + q, to)

            def landed2(k, t, q):
                here = k.quarter(t, dg, q)
                return remote(here, here, hop2_send, hop2_recv, 2 * t + q, sibling)

            def d2d(k, t, p, chip, own=False, arriving=False):
                if arriving:
                    there = k.slot(t, chip, theirs)
                    return remote(there, there, d2d_send, d2d_recv, 4 * t + p, sibling)
                piece = src[t].at[me] if own else k.slot(t, chip, mine)
                return remote(piece, k.slot(t, chip, mine), d2d_send, d2d_recv, 4 * t + p, sibling)

        return Copies(), nb, me, dg, c

    def start(self, src, out, sems):
        k, nb, me, dg, c = self._copies(src, out, sems)
        for t in range(self.n):
            for first in range(2):
                a = first ^ c
                k.hop1(t, a).start()
                if self.scatter:
                    k.hop1(t, a, transit=True).start()
            if self.scatter:
                k.d2d(t, 3, me, own=True).start()
            else:
                k.own_shard(t).start()

    def mid(self, src, out, sems):
        k, nb, me, dg, c = self._copies(src, out, sems)
        for t in range(self.n):
            for first in range(2):
                a = first ^ c
                if self.scatter:
                    k.landed1(t, a, transit=True).wait_recv()
                    k.hop2(t, a).start()
                k.landed1(t, a).wait_recv()
                if not self.scatter:
                    k.hop2(t, a).start()
                k.d2d(t, a, nb(a)[0]).start()

    def finish(self, src, out, sems):
        k, nb, me, dg, c = self._copies(src, out, sems)
        for t in range(self.n):
            for q in range(2):
                k.landed2(t, q).wait_recv()
            k.d2d(t, 2, dg).start()
        for t in range(self.n):
            for a in range(2):
                k.d2d(t, a, nb(a)[0], arriving=True).wait_recv()
            k.d2d(t, 2, dg, arriving=True).wait_recv()
            if self.scatter:
                k.d2d(t, 3, me, arriving=True).wait_recv()
        for t in range(self.n):
            for a in range(2):
                k.hop1(t, a).wait_send()
                if self.scatter:
                    k.hop1(t, a, transit=True).wait_send()
                k.hop2(t, a).wait_send()
                k.d2d(t, a, nb(a)[0]).wait_send()
            k.d2d(t, 2, dg).wait_send()
            if self.scatter:
                k.d2d(t, 3, me, own=True).wait_send()
            else:
                k.own_shard(t).wait()


class SiblingExchange:
    def __init__(self, grads):
        self.inputs, self.n, self.aliases, self.reach = list(grads), len(grads), {}, REACH_SIBLING
        self.half_rows = [g.shape[1] // 2 for g in grads]
        self.out_shape = [jax.ShapeDtypeStruct((g.shape[0], rh, g.shape[2]), g.dtype) for g, rh in zip(grads, self.half_rows)]
        self.sems = [pltpu.SemaphoreType.DMA((self.n,)), pltpu.SemaphoreType.DMA((self.n,))]

    def _plan(self, src, out, sems):
        x, y, c = _mesh_pos()
        return [pltpu.make_async_remote_copy(
            src_ref=src[t].at[:, pl.ds((1 - c) * self.half_rows[t], self.half_rows[t]), :], dst_ref=out[t],
            send_sem=sems[0].at[t], recv_sem=sems[1].at[t], device_id=(x, y, 1 - c), device_id_type=MESH) for t in range(self.n)]

    def start(self, src, out, sems):
        for cp in self._plan(src, out, sems):
            cp.start()

    def mid(self, src, out, sems):
        pass

    def finish(self, src, out, sems):
        for cp in self._plan(src, out, sems):
            cp.wait()


REACH_SIBLING, REACH_CHIPS, REACH_ALL = 0, 1, 2


def _entry_barrier(reach):
    x, y, c = _mesh_pos()
    peers = [(x, y, 1 - c)]
    if reach == REACH_CHIPS:
        peers += [(1 - x, y, c), (x, 1 - y, c)]
    elif reach == REACH_ALL:
        peers = [(x ^ dx, y ^ dy, c ^ dc) for dx in (0, 1) for dy in (0, 1) for dc in (0, 1)][1:]
    barrier = pltpu.get_barrier_semaphore()
    for peer in peers:
        pl.semaphore_signal(barrier, inc=1, device_id=peer, device_id_type=MESH)
    pl.semaphore_wait(barrier, len(peers))


def _call(body, hosted=(), *, name, in_specs, out_specs, out_shape, args, grid=(), scratch_shapes=(), aliased=None):
    n_in, n_out, n_scr = len(in_specs), len(out_specs), len(scratch_shapes)
    total = math.prod(grid)
    mid_step = max(0, (5 * total) // 8 - 1)

    def full(*refs):
        pos = [0]

        def take(k):
            pos[0] += k
            return refs[pos[0] - k:pos[0]]

        ins, h_in = take(n_in), [take(len(h.inputs)) for h in hosted]
        outs, h_out = take(n_out), [take(len(h.out_shape)) for h in hosted]
        scr, h_sem = take(n_scr), [take(len(h.sems)) for h in hosted]
        step = 0
        for axis, size in enumerate(grid):
            step = step * size + pl.program_id(axis)

        def phase(at, method):
            if not hosted:
                return

            def run():
                if method == "start":
                    _entry_barrier(reach)
                for h, s, o, m in zip(hosted, h_in, h_out, h_sem):
                    getattr(h, method)(s, o, m)

            if total == 1:
                run()
            else:
                pl.when(step == at)(run)

        phase(0, "start")
        body(*ins, *outs, *scr)
        phase(mid_step, "mid")
        phase(total - 1, "finish")

    aliases, i0, o0 = dict(aliased or {}), n_in, n_out
    for h in hosted:
        aliases.update({i0 + i: o0 + o for i, o in h.aliases.items()})
        i0, o0 = i0 + len(h.inputs), o0 + len(h.out_shape)
    reach = max((h.reach for h in hosted), default=None)
    params = dict(vmem_limit_bytes=VMEM_LIMIT_V7X)
    if hosted:
        params["collective_id"] = reach
    results = pl.pallas_call(
        full, name=name, grid=grid,
        in_specs=list(in_specs) + [ANY] * (i0 - n_in),
        out_specs=list(out_specs) + [ANY] * (o0 - n_out),
        out_shape=list(out_shape) + [s for h in hosted for s in h.out_shape],
        scratch_shapes=list(scratch_shapes) + [s for h in hosted for s in h.sems],
        input_output_aliases=aliases,
        compiler_params=pltpu.CompilerParams(**params),
    )(*args, *[s for h in hosted for s in h.inputs])
    outs, extras, pos = list(results[:n_out]), [], n_out
    for h in hosted:
        extras.append(list(results[pos:pos + h.n]))
        pos += len(h.out_shape)
    return outs, extras


def cast_shards(shards, name, hosted=()):
    n = len(shards)

    def body(*refs):
        for x_ref, o_ref in zip(refs[:n], refs[n:]):
            o_ref[...] = x_ref[...].astype(BF16)

    whole = lambda s: pl.BlockSpec(s.shape, lambda: (0,) * s.ndim)
    return _call(body, hosted, name=name, in_specs=[whole(s) for s in shards], out_specs=[whole(s) for s in shards],
                 out_shape=[jax.ShapeDtypeStruct(s.shape, BF16) for s in shards], args=list(shards))


class AllExchange:
    def __init__(self, pack):
        self.inputs, self.n, self.aliases, self.reach = [pack], 1, {}, REACH_ALL
        self.out_shape = [jax.ShapeDtypeStruct((2 * N_CHIPS,) + pack.shape, pack.dtype)]
        self.sems = [pltpu.SemaphoreType.DMA, pltpu.SemaphoreType.DMA((7,)), pltpu.SemaphoreType.DMA((7,))]

    def _copies(self, src, out, sems):
        local_sem, send_sem, recv_sem = sems
        x, y, c = _mesh_pos()
        flips = [(dx, dy, dc) for dx in (0, 1) for dy in (0, 1) for dc in (0, 1)][1:]
        peers = [(x ^ dx, y ^ dy, c ^ dc) for dx, dy, dc in flips]
        remote = lambda s, d, k: pltpu.make_async_remote_copy(
            src_ref=s, dst_ref=d, send_sem=send_sem.at[k], recv_sem=recv_sem.at[k], device_id=peers[k], device_id_type=MESH)
        sends = [remote(src[0], out[0].at[4 * x + 2 * y + c], k) for k in range(7)]
        landed = [remote(out[0].at[4 * px + 2 * py + pc], out[0].at[4 * px + 2 * py + pc], k) for k, (px, py, pc) in enumerate(peers)]
        return sends, landed, pltpu.make_async_copy(src[0], out[0].at[4 * x + 2 * y + c], local_sem)

    def start(self, src, out, sems):
        sends, _, local = self._copies(src, out, sems)
        for cp in sends:
            cp.start()
        local.start()

    def mid(self, src, out, sems):
        pass

    def finish(self, src, out, sems):
        sends, landed, local = self._copies(src, out, sems)
        for cp in landed:
            cp.wait_recv()
        for cp in sends:
            cp.wait_send()
        local.wait()


MXU_COLS = 256


def _resident(shape):
    return pl.BlockSpec(shape, lambda *_: (0,) * len(shape), pipeline_mode=pl.Buffered(1))


def ffn_up(h, gain, w1, w3, name, hosted=()):
    T, D = h.shape
    F = w1.shape[0]
    tm = min(T, 512)

    def body(h_ref, g_ref, w1_ref, w3_ref, n_ref, ga_ref, gb_ref, s_ref):
        hh = h_ref[...]
        n = (hh * _rstd(hh) * g_ref[...]).astype(BF16)
        n_ref[...] = n
        for c in range(0, F, MXU_COLS):
            cols = slice(c, c + MXU_COLS)
            a = _dot_nt(n, w1_ref[cols, :])
            b = _dot_nt(n, w3_ref[cols, :])
            silu, dsilu = _silu_parts(a)
            ga_ref[:, cols] = (b * dsilu).astype(BF16)
            gb_ref[:, cols] = silu.astype(BF16)
            s_ref[:, cols] = (silu * b).astype(BF16)

    act = jax.ShapeDtypeStruct((T, F), BF16)
    act_spec = pl.BlockSpec((tm, F), lambda i: (i, 0))
    row_spec = pl.BlockSpec((tm, D), lambda i: (i, 0))
    return _call(
        body, hosted, name=name, grid=(T // tm,),
        in_specs=[row_spec, pl.BlockSpec((1, D), lambda i: (0, 0)), _resident((F, D)), _resident((F, D))],
        out_specs=[row_spec, act_spec, act_spec, act_spec],
        out_shape=[jax.ShapeDtypeStruct((T, D), BF16), act, act, act],
        args=[h, gain, w1, w3])


def ffn_down(s, w2, h, name, hosted=()):
    T, F = s.shape
    D = h.shape[1]
    tm = min(T, 512)

    def body(s_ref, w2_ref, h_ref, o_ref):
        o_ref[...] = h_ref[...] + 0.5 * _dot(s_ref[...], w2_ref[...])

    row_spec = pl.BlockSpec((tm, D), lambda i: (i, 0))
    return _call(
        body, hosted, name=name, grid=(T // tm,),
        in_specs=[pl.BlockSpec((tm, F), lambda i: (i, 0)), pl.BlockSpec((F, D), lambda i: (0, 0)), row_spec],
        out_specs=[row_spec],
        out_shape=[jax.ShapeDtypeStruct((T, D), F32)],
        args=[s, w2, h])


def ffn_bwd_act(dh, w2, ga, gb, name, hosted=()):
    T, D = dh.shape
    F = w2.shape[0]
    tm = min(T, 512)

    def body(dh_ref, w2_ref, ga_ref, gb_ref, da_ref, db_ref, df_ref):
        df = (0.5 * dh_ref[...]).astype(BF16)
        df_ref[...] = df
        for c in range(0, F, MXU_COLS):
            cols = slice(c, c + MXU_COLS)
            ds = _dot_nt(df, w2_ref[cols, :])
            da_ref[:, cols] = (ds * ga_ref[:, cols].astype(F32)).astype(BF16)
            db_ref[:, cols] = (ds * gb_ref[:, cols].astype(F32)).astype(BF16)

    act = jax.ShapeDtypeStruct((T, F), BF16)
    act_spec = pl.BlockSpec((tm, F), lambda i: (i, 0))
    row_spec = pl.BlockSpec((tm, D), lambda i: (i, 0))
    return _call(
        body, hosted, name=name, grid=(T // tm,),
        in_specs=[row_spec, _resident((F, D)), act_spec, act_spec],
        out_specs=[act_spec, act_spec, row_spec],
        out_shape=[act, act, jax.ShapeDtypeStruct((T, D), BF16)],
        args=[dh, w2, ga, gb])


def ffn_dw(xs, y, halves, name, hosted=()):
    T, F = xs[0].shape
    D = y.shape[1]
    nx, fh = len(xs), F // halves
    tk = min(T, 512)
    nk = T // tk

    def body(*refs):
        y_ref, x_refs, o_refs, accs = refs[0], refs[1:1 + nx], refs[1 + nx:1 + 2 * nx], refs[1 + 2 * nx:]
        k = pl.program_id(1)

        @pl.when(k == 0)
        def _():
            for acc in accs:
                acc[...] = jnp.zeros_like(acc)

        yy = y_ref[...]
        for x_ref, acc in zip(x_refs, accs):
            acc[...] += _dot_tn(x_ref[...], yy)

        @pl.when(k == nk - 1)
        def _():
            for o_ref, acc in zip(o_refs, accs):
                o_ref[...] = acc[...].astype(BF16)

    out = jax.ShapeDtypeStruct((F, D), BF16)
    return _call(
        body, hosted, name=name, grid=(halves, nk),
        in_specs=[pl.BlockSpec((tk, D), lambda j, k: (k, 0))] + [pl.BlockSpec((tk, fh), lambda j, k: (k, j))] * nx,
        out_specs=[pl.BlockSpec((fh, D), lambda j, k: (j, 0))] * nx,
        out_shape=[out] * nx,
        scratch_shapes=[pltpu.VMEM((fh, D), F32)] * nx,
        args=[y] + list(xs))


def ffn_bwd_in(da, db, w1, w3, h, gain, dh, name, hosted=()):
    T, F = da.shape
    D = h.shape[1]
    tm = min(T, 512)

    def body(da_ref, db_ref, w1_ref, w3_ref, h_ref, g_ref, dh_ref, o_ref, dg_ref):
        dn = _dot(da_ref[...], w1_ref[...]) + _dot(db_ref[...], w3_ref[...])
        dhn, dg = _rmsnorm_bwd(dn, h_ref[...], g_ref[...])
        o_ref[...] = dh_ref[...] + dhn

        @pl.when(pl.program_id(0) == 0)
        def _():
            dg_ref[...] = jnp.zeros_like(dg_ref)

        dg_ref[...] += jnp.sum(dg, axis=0, keepdims=True)

    act_spec = pl.BlockSpec((tm, F), lambda i: (i, 0))
    row_spec = pl.BlockSpec((tm, D), lambda i: (i, 0))
    vec_spec = pl.BlockSpec((1, D), lambda i: (0, 0))
    return _call(
        body, hosted, name=name, grid=(T // tm,),
        in_specs=[act_spec, act_spec, _resident((F, D)), _resident((F, D)), row_spec, vec_spec, row_spec],
        out_specs=[row_spec, vec_spec],
        out_shape=[jax.ShapeDtypeStruct((T, D), F32), jax.ShapeDtypeStruct((1, D), F32)],
        args=[da, db, w1, w3, h, gain, dh])


def mix_in(h, gain, wing, name, hosted=()):
    T, D = h.shape
    nsh, _, Cs = wing.shape
    tm = min(T, 512)

    def body(h_ref, g_ref, w_ref, u_ref, p_ref):
        hh = h_ref[...]
        u = (hh * _rstd(hh) * g_ref[...]).astype(BF16)
        u_ref[...] = u
        for j in range(nsh):
            p_ref[:, j * Cs:(j + 1) * Cs] = _dot(u, w_ref[j])

    return _call(
        body, hosted, name=name, grid=(T // tm,),
        in_specs=[pl.BlockSpec((tm, D), lambda i: (i, 0)), pl.BlockSpec((1, D), lambda i: (0, 0)),
                  pl.BlockSpec((nsh, D, Cs), lambda i: (0, 0, 0))],
        out_specs=[pl.BlockSpec((tm, D), lambda i: (i, 0)), pl.BlockSpec((tm, nsh * Cs), lambda i: (i, 0))],
        out_shape=[jax.ShapeDtypeStruct((T, D), BF16), jax.ShapeDtypeStruct((T, nsh * Cs), F32)],
        args=[h, gain, wing])


def mix_out(a, b, woutg, h, name, hosted=()):
    T, W = a.shape
    D = h.shape[1]
    wout = woutg.reshape(2, W, D)
    tm = min(T, 512)

    def body(a_ref, b_ref, w_ref, h_ref, o_ref):
        o_ref[...] = h_ref[...] + _dot(a_ref[...], w_ref[0]) + _dot(b_ref[...], w_ref[1])

    return _call(
        body, hosted, name=name, grid=(T // tm,),
        in_specs=[pl.BlockSpec((tm, W), lambda i: (i, 0)), pl.BlockSpec((tm, W), lambda i: (i, 0)),
                  pl.BlockSpec((2, W, D), lambda i: (0, 0, 0)), pl.BlockSpec((tm, D), lambda i: (i, 0))],
        out_specs=[pl.BlockSpec((tm, D), lambda i: (i, 0))],
        out_shape=[jax.ShapeDtypeStruct((T, D), F32)],
        args=[a, b, wout, h])


def mix_out_bwd(dh, woutg, a, b, name, hosted=()):
    T, D = dh.shape
    W = a.shape[1]
    nsh, Rs, _ = woutg.shape
    wout = woutg.reshape(2, W, D)
    tk = min(T, 512)
    nk = T // tk

    def body(dh_ref, w_ref, a_ref, b_ref, da_ref, db_ref, dw_ref, acc):
        k = pl.program_id(0)

        @pl.when(k == 0)
        def _():
            acc[...] = jnp.zeros_like(acc)

        dhb = dh_ref[...].astype(BF16)
        da_ref[...] = _dot_nt(dhb, w_ref[0])
        db_ref[...] = _dot_nt(dhb, w_ref[1])
        acc[0:W, :] += _dot_tn(a_ref[...], dhb)
        acc[W:2 * W, :] += _dot_tn(b_ref[...], dhb)

        @pl.when(k == nk - 1)
        def _():
            for j in range(nsh):
                dw_ref[j] = acc[j * Rs:(j + 1) * Rs, :].astype(BF16)

    return _call(
        body, hosted, name=name, grid=(nk,),
        in_specs=[pl.BlockSpec((tk, D), lambda k: (k, 0)), pl.BlockSpec((2, W, D), lambda k: (0, 0, 0)),
                  pl.BlockSpec((tk, W), lambda k: (k, 0)), pl.BlockSpec((tk, W), lambda k: (k, 0))],
        out_specs=[pl.BlockSpec((tk, W), lambda k: (k, 0)), pl.BlockSpec((tk, W), lambda k: (k, 0)),
                   pl.BlockSpec((nsh, Rs, D), lambda k: (0, 0, 0))],
        out_shape=[jax.ShapeDtypeStruct((T, W), F32), jax.ShapeDtypeStruct((T, W), F32),
                   jax.ShapeDtypeStruct((nsh, Rs, D), BF16)],
        scratch_shapes=[pltpu.VMEM((2 * W, D), F32)],
        args=[dh, wout, a, b])


def _dproj_block(g):
    return (g // N_GROUPS + N_GROUPS) % (N_GROUPS + 1), g % N_GROUPS


def mix_dwin(u, dproj, nsh, name, hosted=()):
    T, D = u.shape
    Hd = HEAD_DIM
    slabs, _, width = dproj.shape
    blocks = slabs * width // Hd
    Cs = blocks * Hd // nsh
    tk = min(T, 512)
    nk = T // tk

    def body(u_ref, d_ref, o_ref, acc):
        k = pl.program_id(0)

        @pl.when(k == 0)
        def _():
            acc[...] = jnp.zeros_like(acc)

        where = [_dproj_block(g) for g in range(blocks)]
        d = jnp.concatenate([d_ref[slab, :, col * Hd:(col + 1) * Hd] for slab, col in where], axis=1)
        acc[...] += _dot_tn(u_ref[...], d)

        @pl.when(k == nk - 1)
        def _():
            for j in range(nsh):
                o_ref[j] = acc[:, j * Cs:(j + 1) * Cs].astype(BF16)

    return _call(
        body, hosted, name=name, grid=(nk,),
        in_specs=[pl.BlockSpec((tk, D), lambda k: (k, 0)), pl.BlockSpec((slabs, tk, width), lambda k: (0, k, 0))],
        out_specs=[pl.BlockSpec((nsh, D, Cs), lambda k: (0, 0, 0))],
        out_shape=[jax.ShapeDtypeStruct((nsh, D, Cs), BF16)],
        scratch_shapes=[pltpu.VMEM((D, blocks * Hd), F32)],
        args=[u, dproj])


def mix_in_bwd(dproj, wing, h, gain, dh, name, hosted=()):
    T, D = h.shape
    nsh, _, Cs = wing.shape
    Hd = HEAD_DIM
    per = Cs // Hd
    tm = min(T, 512)

    def body(d_ref, w_ref, h_ref, g_ref, dh_ref, o_ref, dg_ref):
        def shard(j):
            blocks = [_dproj_block(per * j + i) for i in range(per)]
            return jnp.concatenate([d_ref[slab, :, col * Hd:(col + 1) * Hd] for slab, col in blocks], axis=1)

        du = _dot_nt(shard(0), w_ref[0])
        for j in range(1, nsh):
            du += _dot_nt(shard(j), w_ref[j])
        dhn, dg = _rmsnorm_bwd(du, h_ref[...], g_ref[...])
        o_ref[...] = dh_ref[...] + dhn

        @pl.when(pl.program_id(0) == 0)
        def _():
            dg_ref[...] = jnp.zeros_like(dg_ref)

        dg_ref[...] += jnp.sum(dg, axis=0, keepdims=True)

    row_spec = pl.BlockSpec((tm, D), lambda i: (i, 0))
    vec_spec = pl.BlockSpec((1, D), lambda i: (0, 0))
    return _call(
        body, hosted, name=name, grid=(T // tm,),
        in_specs=[pl.BlockSpec((dproj.shape[0], tm, dproj.shape[2]), lambda i: (0, i, 0)),
                  pl.BlockSpec((nsh, D, Cs), lambda i: (0, 0, 0)), row_spec, vec_spec, row_spec],
        out_specs=[row_spec, vec_spec],
        out_shape=[jax.ShapeDtypeStruct((T, D), F32), jax.ShapeDtypeStruct((1, D), F32)],
        args=[dproj, wing, h, gain, dh])


def _pool_window(x, group, T, trailing):
    rows = lax.broadcasted_iota(jnp.int32, x.shape, 0)

    def shifted(z, k):
        if trailing:
            return jnp.where(rows >= k, pltpu.roll(z, k, 0), 0.0)
        return jnp.where(rows < T - k, pltpu.roll(z, T - k, 0), 0.0)

    s2 = x + shifted(x, 1)
    s4 = s2 + shifted(s2, 2)
    s8 = s4 + shifted(s4, 4)
    s16 = s8 + shifted(s8, 8)
    return jnp.where(group == 0, s2, jnp.where(group == 1, s4, jnp.where(group == 2, s8, s16)))


def _pool_count(group, shape):
    rows = lax.broadcasted_iota(jnp.int32, shape, 0)
    w = jnp.where(group == 0, 2, jnp.where(group == 1, 4, jnp.where(group == 2, 8, 16)))
    return jnp.minimum(rows + 1, w).astype(F32)


def pool_fwd(proj, pool_w, pool_scale, name, hosted=()):
    T = proj.shape[0]
    Hd = HEAD_DIM

    def body(x_ref, w_ref, sc_ref, a_ref):
        g = pl.program_id(0)
        x = x_ref[...]
        pooled = _pool_window(x, g, T, True) / _pool_count(g, x.shape) - x
        a_ref[...] = (_dot(pooled.astype(BF16), w_ref[0].astype(BF16)) * sc_ref[...]).astype(BF16)

    return _call(
        body, hosted, name=name, grid=(N_GROUPS,),
        in_specs=[pl.BlockSpec((T, Hd), lambda g: (0, g)), pl.BlockSpec((1, Hd, Hd), lambda g: (g, 0, 0)),
                  pl.BlockSpec((1, Hd), lambda g: (0, g))],
        out_specs=[pl.BlockSpec((T, Hd), lambda g: (0, g))],
        out_shape=[jax.ShapeDtypeStruct((T, N_GROUPS * Hd), BF16)],
        args=[proj, pool_w, pool_scale])


def pool_bwd(proj, da, pool_w, pool_scale, name, hosted=()):
    T = proj.shape[0]
    Hd = HEAD_DIM

    def body(x_ref, da_ref, w_ref, sc_ref, dx_ref, dw_ref, dsc_ref):
        g = pl.program_id(0)
        x = x_ref[...]
        cnt = _pool_count(g, x.shape)
        pooled = (_pool_window(x, g, T, True) / cnt - x).astype(BF16)
        wb = w_ref[0].astype(BF16)
        dav = da_ref[...]
        dsc_ref[...] = jnp.sum(dav * _dot(pooled, wb), axis=0, keepdims=True)
        dout = (dav * sc_ref[...]).astype(BF16)
        dw_ref[0] = _dot_tn(pooled, dout)
        dpooled = _dot_nt(dout, wb)
        dx_ref[0] = (_pool_window(dpooled / cnt, g, T, False) - dpooled).astype(BF16)

    col_spec = pl.BlockSpec((T, Hd), lambda g: (0, g))
    return _call(
        body, hosted, name=name, grid=(N_GROUPS,),
        in_specs=[col_spec, col_spec, pl.BlockSpec((1, Hd, Hd), lambda g: (g, 0, 0)), pl.BlockSpec((1, Hd), lambda g: (0, g))],
        out_specs=[pl.BlockSpec((1, T, Hd), lambda g: (N_GROUPS, 0, g)), pl.BlockSpec((1, Hd, Hd), lambda g: (g, 0, 0)),
                   pl.BlockSpec((1, Hd), lambda g: (0, g))],
        out_shape=[jax.ShapeDtypeStruct((N_GROUPS + 1, T, N_GROUPS * Hd), BF16), jax.ShapeDtypeStruct((N_GROUPS, Hd, Hd), F32),
                   jax.ShapeDtypeStruct((1, N_GROUPS * Hd), F32)],
        args=[proj, da, pool_w, pool_scale])


def _ret_tables(T):
    Hd, C = HEAD_DIM, RET_CHUNK
    inv_freq = 1.0 / (ROPE_BASE ** (jnp.arange(0, Hd, 2, dtype=F32) / Hd))
    ang = jnp.arange(T, dtype=F32)[:, None] * inv_freq[None, :]
    cos, sin = jnp.cos(ang), jnp.sin(ang)
    cos2 = jnp.concatenate([cos, cos], axis=-1)
    sin2 = jnp.concatenate([-sin, sin], axis=-1)
    log_gamma = jnp.log1p(-jnp.exp2(-5.0 - jnp.arange(N_GROUPS, dtype=F32)))
    pos = jnp.arange(C, dtype=F32)
    rel = pos[:, None] - pos[None, :]
    intra = jnp.where(rel[None] >= 0, jnp.exp(log_gamma[:, None, None] * jnp.maximum(rel, 0.0)[None]), 0.0)
    k_tail = jnp.exp(log_gamma[:, None] * (C - 1 - pos)[None, :])
    q_head = jnp.exp(log_gamma[:, None] * (pos + 1.0)[None, :])
    chunk_decay = jnp.exp(log_gamma * C)
    wide = lambda t: jnp.broadcast_to(t[:, :, None], (N_GROUPS, C, Hd))
    return cos2, sin2, intra, wide(k_tail), wide(q_head), jnp.broadcast_to(chunk_decay[:, None, None], (N_GROUPS, 1, Hd))


def _rope(x, cos2, sin2):
    return x * cos2 + pltpu.roll(x, HEAD_DIM // 2, 1) * sin2


def _rope_t(d, cos2, sin2):
    return d * cos2 + pltpu.roll(d * sin2, HEAD_DIM // 2, 1)


def _ret_specs(tseg, seg_of):
    Hd, G = HEAD_DIM, N_GROUPS
    col = lambda kind: pl.BlockSpec((tseg, Hd), lambda h, s: (seg_of(s), G * kind + h))
    tab = pl.BlockSpec((tseg, Hd), lambda h, s: (seg_of(s), 0))
    head = pl.BlockSpec((1, RET_CHUNK, Hd), lambda h, s: (h, 0, 0))
    cd = pl.BlockSpec((1, 1, Hd), lambda h, s: (h, 0, 0))
    gain = pl.BlockSpec((1, Hd), lambda h, s: (0, h))
    return col, tab, head, cd, gain


def ret_fwd(proj, ret_norm, tables, name, hosted=()):
    T = proj.shape[0]
    Hd, C, G = HEAD_DIM, RET_CHUNK, N_GROUPS
    tseg = min(T, 1024)
    nseg, nck = T // tseg, tseg // C
    scale = Hd ** -0.5
    cos2, sin2, intra, k_tail, q_head, chunk_decay = tables

    def body(q_ref, k_ref, v_ref, g_ref, gain_ref, cos_ref, sin_ref, m_ref, kt_ref, qh_ref, cd_ref,
             b_ref, o_ref, rp_ref, state):
        @pl.when(pl.program_id(1) == 0)
        def _():
            state[...] = jnp.zeros_like(state)

        def chunk(ci, carry):
            rows = pl.ds(pl.multiple_of(ci * C, C), C)
            cos, sin = cos_ref[rows, :], sin_ref[rows, :]
            qr = _rope(q_ref[rows, :], cos, sin)
            kr = _rope(k_ref[rows, :], cos, sin) * scale
            qb, kb, vb = qr.astype(BF16), kr.astype(BF16), v_ref[rows, :].astype(BF16)
            r = state[...]
            rp_ref[0, ci] = r.astype(BF16)
            sc = _dot_nt(qb, kb) * m_ref[0]
            o = _dot(sc.astype(BF16), vb) + _dot((qr * qh_ref[0]).astype(BF16), r.astype(BF16))
            state[...] = cd_ref[0] * r + _dot_tn((kr * kt_ref[0]).astype(BF16), vb)
            o_ref[rows, :] = o
            on = o * _rstd(o)
            b_ref[rows, :] = (jax.nn.silu(g_ref[rows, :]) * (on * gain_ref[...])).astype(BF16)
            return carry

        lax.fori_loop(0, nck, chunk, 0, unroll=True)

    col, tab, head, cd, gain = _ret_specs(tseg, lambda s: s)
    out_col = pl.BlockSpec((tseg, Hd), lambda h, s: (s, h))
    return _call(
        body, hosted, name=name, grid=(G, nseg),
        in_specs=[col(1), col(2), col(3), col(4), gain, tab, tab, head, head, head, cd],
        out_specs=[out_col, out_col, pl.BlockSpec((1, nck, Hd, Hd), lambda h, s: (h, s, 0, 0))],
        out_shape=[jax.ShapeDtypeStruct((T, G * Hd), BF16), jax.ShapeDtypeStruct((T, G * Hd), F32),
                   jax.ShapeDtypeStruct((G, T // C, Hd, Hd), BF16)],
        scratch_shapes=[pltpu.VMEM((Hd, Hd), F32)],
        args=[proj, proj, proj, proj, ret_norm, cos2, sin2, intra, k_tail, q_head, chunk_decay])


def ret_bwd(proj, db, o_pre, r_prev, ret_norm, tables, dproj, name, hosted=()):
    T = proj.shape[0]
    Hd, C, G = HEAD_DIM, RET_CHUNK, N_GROUPS
    tseg = min(T, 1024)
    nseg, nck = T // tseg, tseg // C
    scale = Hd ** -0.5
    cos2, sin2, intra, k_tail, q_head, chunk_decay = tables

    def body(q_ref, k_ref, v_ref, g_ref, db_ref, o_ref, rp_ref, gain_ref, cos_ref, sin_ref, m_ref, kt_ref, qh_ref, cd_ref,
             _, d_ref, dgain_ref, gstate):
        @pl.when(pl.program_id(1) == 0)
        def _():
            gstate[...] = jnp.zeros_like(gstate)
            dgain_ref[...] = jnp.zeros_like(dgain_ref)

        def chunk(t, carry):
            ci = nck - 1 - t
            rows = pl.ds(pl.multiple_of(ci * C, C), C)
            cos, sin = cos_ref[rows, :], sin_ref[rows, :]
            qr = _rope(q_ref[rows, :], cos, sin)
            kr = _rope(k_ref[rows, :], cos, sin) * scale
            qb, kb, vb = qr.astype(BF16), kr.astype(BF16), v_ref[rows, :].astype(BF16)
            qhb, ktb = (qr * qh_ref[0]).astype(BF16), (kr * kt_ref[0]).astype(BF16)
            sc = (_dot_nt(qb, kb) * m_ref[0]).astype(BF16)
            o = o_ref[rows, :]
            rstd = _rstd(o)
            on = o * rstd
            gain = gain_ref[...]
            silu, dsilu = _silu_parts(g_ref[rows, :])
            dy = db_ref[rows, :]
            dgain_ref[...] += jnp.sum(dy * silu * on, axis=0, keepdims=True)
            dg = dy * on * gain * dsilu
            don = dy * silu * gain
            dob = (rstd * (don - on * jnp.mean(don * on, axis=-1, keepdims=True))).astype(BF16)
            gn = gstate[...]
            gb = gn.astype(BF16)
            da = (_dot_nt(dob, vb) * m_ref[0]).astype(BF16)
            dq = _dot(da, kb) + _dot_nt(dob, rp_ref[0, ci]) * qh_ref[0]
            dk = _dot_tn(da, qb) + _dot_nt(vb, gb) * kt_ref[0]
            dv = _dot_tn(sc, dob) + _dot(ktb, gb)
            gstate[...] = cd_ref[0] * gn + _dot_tn(qhb, dob)
            d_ref[0, rows, :] = _rope_t(dq, cos, sin).astype(BF16)
            d_ref[1, rows, :] = _rope_t(dk * scale, cos, sin).astype(BF16)
            d_ref[2, rows, :] = dv.astype(BF16)
            d_ref[3, rows, :] = dg.astype(BF16)
            return carry

        lax.fori_loop(0, nck, chunk, 0, unroll=True)

    rev = lambda s: nseg - 1 - s
    col, tab, head, cd, gain = _ret_specs(tseg, rev)
    act = pl.BlockSpec((tseg, Hd), lambda h, s: (rev(s), h))
    return _call(
        body, hosted, name=name, grid=(G, nseg),
        in_specs=[col(1), col(2), col(3), col(4), act, act, pl.BlockSpec((1, nck, Hd, Hd), lambda h, s: (h, rev(s), 0, 0)),
                  gain, tab, tab, head, head, head, cd, ANY],
        out_specs=[pl.BlockSpec((4, tseg, Hd), lambda h, s: (0, rev(s), h)), gain],
        out_shape=[jax.ShapeDtypeStruct(dproj.shape, BF16), jax.ShapeDtypeStruct((1, G * Hd), F32)],
        scratch_shapes=[pltpu.VMEM((Hd, Hd), F32)], aliased={14: 0},
        args=[proj, proj, proj, proj, db, o_pre, r_prev, ret_norm, cos2, sin2, intra, k_tail, q_head, chunk_decay, dproj])


def ffn_down_loss(s, w2, h, gain, target, name, hosted=()):
    T, F = s.shape
    D = h.shape[1]
    tm = min(T, 512)

    def body(s_ref, w2_ref, h_ref, g_ref, t_ref, dh_ref, loss_ref, dg_ref):
        @pl.when(pl.program_id(0) == 0)
        def _():
            loss_ref[...] = jnp.zeros_like(loss_ref)
            dg_ref[...] = jnp.zeros_like(dg_ref)

        hh = h_ref[...] + 0.5 * _dot(s_ref[...], w2_ref[...])
        gain_v = g_ref[...]
        err = hh * _rstd(hh) * gain_v - t_ref[...]
        loss_ref[...] += 0.5 * jnp.sum(jnp.mean(err * err, axis=-1, keepdims=True), axis=0, keepdims=True)
        dhn, dg = _rmsnorm_bwd(err * (1.0 / D), hh, gain_v)
        dh_ref[...] = dhn
        dg_ref[...] += jnp.sum(dg, axis=0, keepdims=True)

    row_spec = pl.BlockSpec((tm, D), lambda i: (i, 0))
    vec_spec = pl.BlockSpec((1, D), lambda i: (0, 0))
    return _call(
        body, hosted, name=name, grid=(T // tm,),
        in_specs=[pl.BlockSpec((tm, F), lambda i: (i, 0)), _resident((F, D)), row_spec, vec_spec, row_spec],
        out_specs=[row_spec, pl.BlockSpec((1, 128), lambda i: (0, 0)), vec_spec],
        out_shape=[jax.ShapeDtypeStruct((T, D), F32), jax.ShapeDtypeStruct((1, 128), F32), jax.ShapeDtypeStruct((1, D), F32)],
        args=[s, w2, h, gain, target])


def prereduce(grads, recvs, place, name):
    nt = len(grads)
    nsh, R, C = grads[0].shape
    rh = R // 2

    def body(place_ref, *refs):
        for t in range(nt):
            g_ref, r_ref, o_ref, own_ref = refs[2 * t], refs[2 * t + 1], refs[2 * nt + 2 * t], refs[2 * nt + 2 * t + 1]
            piece = (g_ref[...].astype(F32) + r_ref[...].astype(F32)).astype(BF16)
            o_ref[...] = piece

            @pl.when(pl.program_id(0) == place_ref[1])
            def _():
                own_ref[...] = piece

    outs = pl.pallas_call(
        body, name=name,
        grid_spec=pltpu.PrefetchScalarGridSpec(
            num_scalar_prefetch=1, grid=(nsh,),
            in_specs=[pl.BlockSpec((1, rh, C), lambda j, p: (j, p[0], 0)), pl.BlockSpec((1, rh, C), lambda j, p: (j, 0, 0))] * nt,
            out_specs=[pl.BlockSpec((1, rh, C), lambda j, p: (j, 0, 0)),
                       pl.BlockSpec((1, rh, C), lambda j, p: (p[1], p[0], 0))] * nt),
        out_shape=[jax.ShapeDtypeStruct((nsh, rh, C), BF16), jax.ShapeDtypeStruct((nsh, R, C), BF16)] * nt,
        compiler_params=pltpu.CompilerParams(vmem_limit_bytes=VMEM_LIMIT_V7X),
    )(place, *[a for pair in zip(grads, recvs) for a in pair])
    return [(outs[2 * t], outs[2 * t + 1]) for t in range(nt)]


def _adamw(w, g, m, v):
    m = ADAM_B1 * m + (1.0 - ADAM_B1) * g
    v = ADAM_B2 * v + (1.0 - ADAM_B2) * (g * g)
    m_hat = m / (1.0 - ADAM_B1 ** ADAM_STEP)
    v_hat = v / (1.0 - ADAM_B2 ** ADAM_STEP)
    return -ADAM_LR * (m_hat / (jnp.sqrt(v_hat) + ADAM_EPS) + ADAM_WD * w), m, v


def adamw_sharded(tensors, name, hosted=()):
    nt = len(tensors)
    nsh, R, C = tensors[0][0].shape
    lanes = -(-C // 128) * 128
    per_row = 2 * nt * lanes * (nsh * 2 + 7 * 4)
    tr = max(r for r in range(16, R + 1, 16) if R % r == 0 and r * per_row <= ADAMW_VMEM_BUDGET)

    def body(*refs):
        ins, outs = refs[:4 * nt], refs[4 * nt:]
        for t in range(nt):
            p_ref, w_ref, m_ref, v_ref = ins[4 * t:4 * t + 4]
            g_ref, d_ref, nm_ref, nv_ref = outs[4 * t:4 * t + 4]
            g = p_ref[0].astype(F32)
            for i in range(1, nsh):
                g += p_ref[i].astype(F32)
            g_ref[...] = g
            d_ref[...], nm_ref[...], nv_ref[...] = _adamw(w_ref[...], g, m_ref[...], v_ref[...])

    spec = pl.BlockSpec((tr, C), lambda i: (i, 0))
    out = jax.ShapeDtypeStruct((R, C), F32)
    return _call(
        body, hosted, name=name, grid=(R // tr,),
        in_specs=[pl.BlockSpec((nsh, tr, C), lambda i: (0, i, 0)), spec, spec, spec] * nt,
        out_specs=[spec] * (4 * nt), out_shape=[out] * (4 * nt),
        args=[a for tensor in tensors for a in tensor])


def adamw_small(packs, late, pool, vectors, name):
    ndev = packs.shape[0]
    rp, rv, rl = pool[0].shape[0], vectors.shape[0] // 3, late.shape[1]

    def body(p_ref, l_ref, wp, mp, vp, wmv, gp, dp, nmp, nvp, gv, dv, nmv, nvv, loss_ref):
        wv, mv, vv = wmv.at[0:rv], wmv.at[rv:2 * rv], wmv.at[2 * rv:3 * rv]
        g, first = p_ref[0], l_ref[0]
        for i in range(1, ndev):
            g += p_ref[i]
            first += l_ref[i]
        g_pool = g[0:rp]
        g_vec = jnp.concatenate([g[rp:rp + rl] + first, g[rp + rl:rp + rv]], axis=0)
        gp[...], gv[...], loss_ref[...] = g_pool, g_vec, g[rp + rv:rp + rv + 8]
        dp[...], nmp[...], nvp[...] = _adamw(wp[...], g_pool, mp[...], vp[...])
        dv[...], nmv[...], nvv[...] = _adamw(wv[...], g_vec, mv[...], vv[...])

    shape = lambda rows: jax.ShapeDtypeStruct((rows, 128), F32)
    outs = pl.pallas_call(body, name=name, out_shape=[shape(rp)] * 4 + [shape(rv)] * 4 + [shape(8)],
                          compiler_params=pltpu.CompilerParams(vmem_limit_bytes=VMEM_LIMIT_V7X))(packs, late, *pool, vectors)
    return outs[0:4], outs[4:8], outs[8]


BIG = ("ffn1_w1", "ffn1_w3", "ffn1_w2", "w_in", "w_out", "ffn2_w1", "ffn2_w3", "ffn2_w2")
TRANSPOSED = ("ffn1_w1", "ffn1_w3", "ffn2_w1", "ffn2_w3")
VECTORS = ("ffn1_norm", "mix_norm", "pool_scale", "ret_norm", "ffn2_norm", "final_norm")
WEIGHTS = ("ffn1_norm", "ffn1_w1", "ffn1_w3", "ffn1_w2", "mix_norm", "w_in", "pool_w", "pool_scale", "ret_norm", "w_out",
           "ffn2_norm", "ffn2_w1", "ffn2_w3", "ffn2_w2", "final_norm")


def _pack_vectors(parts):
    return jnp.concatenate([parts[k].reshape(-1, 128) for k in VECTORS], axis=0)


def _unpack_vectors(pack, like):
    out, row = {}, 0
    for k in VECTORS:
        rows = like[k].size // 128
        out[k] = pack[row:row + rows].reshape(like[k].shape)
        row += rows
    return out


def kernel(x, ffn1_norm, ffn1_w1, ffn1_w3, ffn1_w2, mix_norm, w_in, pool_w, pool_scale, ret_norm, w_out, ffn2_norm, ffn2_w1, ffn2_w3, ffn2_w2, final_norm, loss_target, m_ffn1_norm, m_ffn1_w1, m_ffn1_w3, m_ffn1_w2, m_mix_norm, m_w_in, m_pool_w, m_pool_scale, m_ret_norm, m_w_out, m_ffn2_norm, m_ffn2_w1, m_ffn2_w3, m_ffn2_w2, m_final_norm, v_ffn1_norm, v_ffn1_w1, v_ffn1_w3, v_ffn1_w2, v_mix_norm, v_w_in, v_pool_w, v_pool_scale, v_ret_norm, v_w_out, v_ffn2_norm, v_ffn2_w1, v_ffn2_w3, v_ffn2_w2, v_final_norm):
    w = dict(ffn1_norm=ffn1_norm, ffn1_w1=ffn1_w1, ffn1_w3=ffn1_w3, ffn1_w2=ffn1_w2, mix_norm=mix_norm, w_in=w_in, pool_w=pool_w,
             pool_scale=pool_scale, ret_norm=ret_norm, w_out=w_out, ffn2_norm=ffn2_norm, ffn2_w1=ffn2_w1, ffn2_w3=ffn2_w3,
             ffn2_w2=ffn2_w2, final_norm=final_norm)
    m = dict(ffn1_norm=m_ffn1_norm, ffn1_w1=m_ffn1_w1, ffn1_w3=m_ffn1_w3, ffn1_w2=m_ffn1_w2, mix_norm=m_mix_norm, w_in=m_w_in,
             pool_w=m_pool_w, pool_scale=m_pool_scale, ret_norm=m_ret_norm, w_out=m_w_out, ffn2_norm=m_ffn2_norm, ffn2_w1=m_ffn2_w1,
             ffn2_w3=m_ffn2_w3, ffn2_w2=m_ffn2_w2, final_norm=m_final_norm)
    v = dict(ffn1_norm=v_ffn1_norm, ffn1_w1=v_ffn1_w1, ffn1_w3=v_ffn1_w3, ffn1_w2=v_ffn1_w2, mix_norm=v_mix_norm, w_in=v_w_in,
             pool_w=v_pool_w, pool_scale=v_pool_scale, ret_norm=v_ret_norm, w_out=v_w_out, ffn2_norm=v_ffn2_norm, ffn2_w1=v_ffn2_w1,
             ffn2_w3=v_ffn2_w3, ffn2_w2=v_ffn2_w2, final_norm=v_final_norm)
    xs, target = x[0], loss_target[0]
    T = xs.shape[0]
    tables = _ret_tables(T)
    place = jnp.stack([lax.axis_index("c"), 2 * lax.axis_index("x") + lax.axis_index("y")]).astype(jnp.int32)
    local = lambda d, k: jnp.transpose(d[k][0]) if k in TRANSPOSED else d[k][0]
    result = lambda o, k: jnp.transpose(o)[None] if k in TRANSPOSED else o[None]
    first = ("ffn1_w1", "ffn1_w3")
    sh = {k: local(w, k).astype(BF16) for k in first}
    gather = lambda *names: [ChipExchange([sh[k] for k in names], False)]
    wg, grad, delta, new_m, new_v = {}, {}, {}, {}, {}

    def update(names, pieces, name, hosted=()):
        outs, extras = adamw_sharded([(p, local(w, k), local(m, k), local(v, k)) for k, p in zip(names, pieces)], name, hosted)
        for t, k in enumerate(names):
            grad[k], delta[k], new_m[k], new_v[k] = [result(o, k) for o in outs[4 * t:4 * t + 4]]
        return extras

    def reduce_in_chip(name, *pairs):
        reduced = prereduce([p for p, _ in pairs], [r for _, r in pairs], place, "prereduce_" + name)
        return reduced[0] if len(pairs) == 1 else reduced

    scatter = lambda *reduced: ChipExchange([r[0] for r in reduced], True, [r[1] for r in reduced])
    whole = lambda k: wg[k].reshape(-1, wg[k].shape[-1])
    sharded = lambda g: g.reshape(N_CHIPS, -1, g.shape[-1])

    later = [k for k in BIG if k not in first]
    casts, ((wg["ffn1_w1"], wg["ffn1_w3"]),) = cast_shards([local(w, k) for k in later], "cast_gather_ffn1", gather(*first))
    sh.update(zip(later, casts))
    (n1, ga1, gb1, s1), ((wg["ffn1_w2"], wg["w_in"]),) = ffn_up(
        xs, ffn1_norm, whole("ffn1_w1"), whole("ffn1_w3"), "ffn1_up", gather("ffn1_w2", "w_in"))
    (h1,), ((wg["w_out"],),) = ffn_down(s1, whole("ffn1_w2"), xs, "ffn1_down", gather("w_out"))
    (u, proj), ((wg["ffn2_w1"],),) = mix_in(h1, mix_norm, wg["w_in"], "mix_in", gather("ffn2_w1"))
    (pa,), _ = pool_fwd(proj, pool_w[0], pool_scale, "pool_fwd")
    (rb, o_pre, r_prev), ((wg["ffn2_w3"],),) = ret_fwd(proj, ret_norm, tables, "ret_fwd", gather("ffn2_w3"))
    (h2,), _ = mix_out(pa, rb, wg["w_out"], h1, "mix_out")
    (n2, ga2, gb2, s2), ((wg["ffn2_w2"],),) = ffn_up(
        h2, ffn2_norm, whole("ffn2_w1"), whole("ffn2_w3"), "ffn2_up", gather("ffn2_w2"))
    (dh3, loss, d_final), _ = ffn_down_loss(s2, whole("ffn2_w2"), h2, final_norm[None], target, "ffn2_down_loss")

    (da2, db2, df2), _ = ffn_bwd_act(dh3, whole("ffn2_w2"), ga2, gb2, "ffn2_bwd_act")
    (g_f2w2,), _ = ffn_dw([s2], df2, 1, "ffn2_dw2")
    g_f2w2 = sharded(g_f2w2)
    (g_f2w1, g_f2w3), ((r_f2w2,),) = ffn_dw([da2, db2], n2, 2, "ffn2_dw13", [SiblingExchange([g_f2w2])])
    g_f2w1, g_f2w3 = sharded(g_f2w1), sharded(g_f2w3)
    p_f2w2 = reduce_in_chip("ffn2_w2", (g_f2w2, r_f2w2))
    (dh2, d_ffn2), ((q_f2w2,), (r_f2w1, r_f2w3)) = ffn_bwd_in(
        da2, db2, whole("ffn2_w1"), whole("ffn2_w3"), h2, ffn2_norm, dh3, "ffn2_bwd_in",
        [scatter(p_f2w2), SiblingExchange([g_f2w1, g_f2w3])])
    p_f2w1, p_f2w3 = reduce_in_chip("ffn2_w13", (g_f2w1, r_f2w1), (g_f2w3, r_f2w3))
    (dpa, drb, g_wout), _ = mix_out_bwd(dh2, wg["w_out"], pa, rb, "mix_out_bwd")
    (dproj, d_pool_w, d_pool_scale), _ = pool_bwd(proj, dpa, pool_w[0], pool_scale, "pool_bwd")
    (dproj, d_ret_norm), ((q_f2w1, q_f2w3), (r_wout,)) = ret_bwd(
        proj, drb, o_pre, r_prev, ret_norm, tables, dproj, "ret_bwd", [scatter(p_f2w1, p_f2w3), SiblingExchange([g_wout])])
    p_wout = reduce_in_chip("w_out", (g_wout, r_wout))
    (g_win,), ((q_wout,),) = mix_dwin(u, dproj, N_CHIPS, "mix_dwin", [scatter(p_wout)])
    (dh1, d_mix), ((r_win,),) = mix_in_bwd(dproj, wg["w_in"], h1, mix_norm, dh2, "mix_in_bwd", [SiblingExchange([g_win])])
    p_win = reduce_in_chip("w_in", (g_win, r_win))
    (da1, db1, df1), ((q_win,),) = ffn_bwd_act(dh1, whole("ffn1_w2"), ga1, gb1, "ffn1_bwd_act", [scatter(p_win)])
    d_vectors = {"ffn1_norm": jnp.zeros_like(ffn1_norm), "mix_norm": d_mix, "pool_scale": d_pool_scale,
                 "ret_norm": d_ret_norm, "ffn2_norm": d_ffn2, "final_norm": d_final}
    pack = jnp.concatenate([d_pool_w.reshape(-1, 128), _pack_vectors(d_vectors), jnp.broadcast_to(loss, (8, 128))], axis=0)
    (g_f1w1, g_f1w3), ((packs,),) = ffn_dw([da1, db1], n1, 2, "ffn1_dw13", [AllExchange(pack)])
    g_f1w1, g_f1w3 = sharded(g_f1w1), sharded(g_f1w3)
    (g_f1w2,), ((r_f1w1, r_f1w3),) = ffn_dw([s1], df1, 1, "ffn1_dw2", [SiblingExchange([g_f1w1, g_f1w3])])
    g_f1w2 = sharded(g_f1w2)
    p_f1w1, p_f1w3 = reduce_in_chip("ffn1_w13", (g_f1w1, r_f1w1), (g_f1w3, r_f1w3))
    (dx, d_ffn1), ((q_f1w1, q_f1w3), (r_f1w2,)) = ffn_bwd_in(
        da1, db1, whole("ffn1_w1"), whole("ffn1_w3"), xs, ffn1_norm, dh1, "ffn1_bwd_in",
        [scatter(p_f1w1, p_f1w3), SiblingExchange([g_f1w2])])
    p_f1w2 = reduce_in_chip("ffn1_w2", (g_f1w2, r_f1w2))

    (q_f1w2,), (late,) = update(["ffn2_w1", "ffn2_w3", "ffn1_w1", "ffn1_w3"], [q_f2w1, q_f2w3, q_f1w1, q_f1w3], "adamw_w13",
                                [scatter(p_f1w2), AllExchange(d_ffn1.reshape(-1, 128))])
    update(["ffn2_w2", "ffn1_w2"], [q_f2w2, q_f1w2], "adamw_w2")
    update(["w_in"], [q_win], "adamw_w_in")
    update(["w_out"], [q_wout], "adamw_w_out")
    of_pool, of_vectors, loss_sum = adamw_small(packs, late, [t["pool_w"].reshape(-1, 128) for t in (w, m, v)],
                                                jnp.concatenate([t[k].reshape(-1, 128) for t in (w, m, v) for k in VECTORS], axis=0),
                                                "adamw_small")
    for res, pool_part, vector_part in zip((grad, delta, new_m, new_v), of_pool, of_vectors):
        res["pool_w"] = pool_part.reshape(pool_w.shape)
        res.update(_unpack_vectors(vector_part, w))
    loss = loss_sum[0, 0]

    return (loss, dx[None], *[grad[k] for k in WEIGHTS], *[delta[k] for k in WEIGHTS],
            *[new_m[k] for k in WEIGHTS], *[new_v[k] for k in WEIGHTS])
```

```python
import math

import jax
import jax.numpy as jnp
from jax import lax
from jax.experimental import pallas as pl
from jax.experimental.pallas import tpu as pltpu

F32 = jnp.float32
BF16 = jnp.bfloat16

EPS = 1e-6
N_CHIPS = 4
N_GROUPS = 4
HEAD_DIM = 128
RET_CHUNK = 128
ROPE_BASE = 10000.0
ADAM_LR, ADAM_B1, ADAM_B2, ADAM_EPS, ADAM_WD, ADAM_STEP = 0.001, 0.9, 0.999, 1e-08, 0.01, 10
VMEM_LIMIT_V7X = 56 * 1024 * 1024
ADAMW_VMEM_BUDGET = 32 * 1024 * 1024
MESH = pl.DeviceIdType.MESH
ANY = pl.BlockSpec(memory_space=pl.ANY)


def _dot(a, b):
    return jnp.dot(a, b, preferred_element_type=F32)


def _dot_nt(a, b):
    return lax.dot_general(a, b, (((1,), (1,)), ((), ())), preferred_element_type=F32)


def _dot_tn(a, b):
    return lax.dot_general(a, b, (((0,), (0,)), ((), ())), preferred_element_type=F32)


def _rstd(h):
    return lax.rsqrt(jnp.mean(h * h, axis=-1, keepdims=True) + EPS)


def _rmsnorm_bwd(dn, h, gain):
    r = _rstd(h)
    nh = h * r
    dnh = dn * gain
    dh = r * (dnh - nh * jnp.mean(dnh * nh, axis=-1, keepdims=True))
    return dh, dn * nh


def _silu_parts(a):
    sig = jax.nn.sigmoid(a)
    silu = a * sig
    return silu, sig + silu * (1.0 - sig)


def _mesh_pos():
    return lax.axis_index("x"), lax.axis_index("y"), lax.axis_index("c")


class ChipExchange:
    def __init__(self, srcs, scatter, placed=()):
        n = len(srcs)
        self.inputs, self.scatter, self.n, self.reach = list(srcs) + list(placed), scatter, n, REACH_CHIPS
        self.aliases = {n + t: t for t in range(n)} if scatter else {}
        self.half_rows = [s.shape[1] if scatter else s.shape[0] // 2 for s in srcs]
        self.out_shape = [jax.ShapeDtypeStruct((N_CHIPS, 2 * rh, s.shape[-1]), s.dtype) for s, rh in zip(srcs, self.half_rows)]
        if scatter:
            self.out_shape += [jax.ShapeDtypeStruct((2, rh // 2, s.shape[-1]), s.dtype) for s, rh in zip(srcs, self.half_rows)]
        dma = pltpu.SemaphoreType.DMA
        self.sems = [dma((4 * n,)), dma((4 * n,)), dma((2 * n,)), dma((2 * n,)), dma((4 * n,)), dma((4 * n,))]

    def _copies(self, src, out, sems):
        hop1_send, hop1_recv, hop2_send, hop2_recv, d2d_send, d2d_recv = sems
        x, y, c = _mesh_pos()
        me, dg = 2 * x + y, 2 * (1 - x) + (1 - y)
        sibling = (x, y, 1 - c)
        n = self.n
        mine, theirs = c, 1 - c

        def nb(a):
            nx, ny = x ^ (1 - a), y ^ a
            return 2 * nx + ny, (nx, ny, c)

        def remote(s, d, send, recv, k, to):
            return pltpu.make_async_remote_copy(src_ref=s, dst_ref=d, send_sem=send.at[k], recv_sem=recv.at[k],
                                                device_id=to, device_id_type=MESH)

        class Copies:
            def slot(_, t, chip, half):
                rh = self.half_rows[t]
                return out[t].at[chip, pl.ds(half * rh, rh), :]

            def quarter(_, t, chip, q):
                qh = self.half_rows[t] // 2
                return out[t].at[chip, pl.ds(mine * 2 * qh + q * qh, qh), :]

            def own_shard(k, t):
                return remote(src[t], out[t].at[me], d2d_send, d2d_recv, 4 * t + 3, sibling)

            def hop1(k, t, a, transit=False):
                rh = self.half_rows[t]
                chip, to = nb(a)
                if transit:
                    piece = src[t].at[dg, pl.ds(a * (rh // 2), rh // 2), :]
                    return remote(piece, out[n + t].at[a], hop1_send, hop1_recv, 4 * t + 2 + a, to)
                piece = src[t].at[chip] if self.scatter else src[t].at[pl.ds(mine * rh, rh), :]
                return remote(piece, k.slot(t, me, mine), hop1_send, hop1_recv, 4 * t + a, to)

            def landed1(k, t, a, transit=False):
                here = out[n + t].at[a] if transit else k.slot(t, nb(a)[0], mine)
                return remote(here, here, hop1_send, hop1_recv, 4 * t + (2 if transit else 0) + a, sibling)

            def hop2(k, t, q):
                origin, to = nb(q)[0], nb(1 - q)[1]
                piece = out[n + t].at[q] if self.scatter else k.quarter(t, origin, q)
                return remote(piece, k.quarter(t, origin, q), hop2_send, hop2_recv, 2 * t + q, to)

            def landed2(k, t, q):
                here = k.quarter(t, dg, q)
                return remote(here, here, hop2_send, hop2_recv, 2 * t + q, sibling)

            def d2d(k, t, p, chip, own=False, arriving=False):
                if arriving:
                    there = k.slot(t, chip, theirs)
                    return remote(there, there, d2d_send, d2d_recv, 4 * t + p, sibling)
                piece = src[t].at[me] if own else k.slot(t, chip, mine)
                return remote(piece, k.slot(t, chip, mine), d2d_send, d2d_recv, 4 * t + p, sibling)

        return Copies(), nb, me, dg, c

    def start(self, src, out, sems):
        k, nb, me, dg, c = self._copies(src, out, sems)
        for t in range(self.n):
            for first in range(2):
                a = first ^ c
                k.hop1(t, a).start()
                if self.scatter:
                    k.hop1(t, a, transit=True).start()
            if self.scatter:
                k.d2d(t, 3, me, own=True).start()
            else:
                k.own_shard(t).start()

    def mid(self, src, out, sems):
        k, nb, me, dg, c = self._copies(src, out, sems)
        for t in range(self.n):
            for first in range(2):
                a = first ^ c
                if self.scatter:
                    k.landed1(t, a, transit=True).wait_recv()
                    k.hop2(t, a).start()
                k.landed1(t, a).wait_recv()
                if not self.scatter:
                    k.hop2(t, a).start()
                k.d2d(t, a, nb(a)[0]).start()

    def finish(self, src, out, sems):
        k, nb, me, dg, c = self._copies(src, out, sems)
        for t in range(self.n):
            for q in range(2):
                k.landed2(t, q).wait_recv()
            k.d2d(t, 2, dg).start()
        for t in range(self.n):
            for a in range(2):
                k.d2d(t, a, nb(a)[0], arriving=True).wait_recv()
            k.d2d(t, 2, dg, arriving=True).wait_recv()
            if self.scatter:
                k.d2d(t, 3, me, arriving=True).wait_recv()
        for t in range(self.n):
            for a in range(2):
                k.hop1(t, a).wait_send()
                if self.scatter:
                    k.hop1(t, a, transit=True).wait_send()
                k.hop2(t, a).wait_send()
                k.d2d(t, a, nb(a)[0]).wait_send()
            k.d2d(t, 2, dg).wait_send()
            if self.scatter:
                k.d2d(t, 3, me, own=True).wait_send()
            else:
                k.own_shard(t).wait()


class SiblingExchange:
    def __init__(self, grads):
        self.inputs, self.n, self.aliases, self.reach = list(grads), len(grads), {}, REACH_SIBLING
        self.half_rows = [g.shape[1] // 2 for g in grads]
        self.out_shape = [jax.ShapeDtypeStruct((g.shape[0], rh, g.shape[2]), g.dtype) for g, rh in zip(grads, self.half_rows)]
        self.sems = [pltpu.SemaphoreType.DMA((self.n,)), pltpu.SemaphoreType.DMA((self.n,))]

    def _plan(self, src, out, sems):
        x, y, c = _mesh_pos()
        return [pltpu.make_async_remote_copy(
            src_ref=src[t].at[:, pl.ds((1 - c) * self.half_rows[t], self.half_rows[t]), :], dst_ref=out[t],
            send_sem=sems[0].at[t], recv_sem=sems[1].at[t], device_id=(x, y, 1 - c), device_id_type=MESH) for t in range(self.n)]

    def start(self, src, out, sems):
        for cp in self._plan(src, out, sems):
            cp.start()

    def mid(self, src, out, sems):
        pass

    def finish(self, src, out, sems):
        for cp in self._plan(src, out, sems):
            cp.wait()


REACH_SIBLING, REACH_CHIPS, REACH_ALL = 0, 1, 2


def _entry_barrier(reach):
    x, y, c = _mesh_pos()
    peers = [(x, y, 1 - c)]
    if reach == REACH_CHIPS:
        peers += [(1 - x, y, c), (x, 1 - y, c)]
    elif reach == REACH_ALL:
        peers = [(x ^ dx, y ^ dy, c ^ dc) for dx in (0, 1) for dy in (0, 1) for dc in (0, 1)][1:]
    barrier = pltpu.get_barrier_semaphore()
    for peer in peers:
        pl.semaphore_signal(barrier, inc=1, device_id=peer, device_id_type=MESH)
    pl.semaphore_wait(barrier, len(peers))


def _call(body, hosted=(), *, name, in_specs, out_specs, out_shape, args, grid=(), scratch_shapes=(), aliased=None):
    n_in, n_out, n_scr = len(in_specs), len(out_specs), len(scratch_shapes)
    total = math.prod(grid)
    mid_step = max(0, (5 * total) // 8 - 1)

    def full(*refs):
        pos = [0]

        def take(k):
            pos[0] += k
            return refs[pos[0] - k:pos[0]]

        ins, h_in = take(n_in), [take(len(h.inputs)) for h in hosted]
        outs, h_out = take(n_out), [take(len(h.out_shape)) for h in hosted]
        scr, h_sem = take(n_scr), [take(len(h.sems)) for h in hosted]
        step = 0
        for axis, size in enumerate(grid):
            step = step * size + pl.program_id(axis)

        def phase(at, method):
            if not hosted:
                return

            def run():
                if method == "start":
                    _entry_barrier(reach)
                for h, s, o, m in zip(hosted, h_in, h_out, h_sem):
                    getattr(h, method)(s, o, m)

            if total == 1:
                run()
            else:
                pl.when(step == at)(run)

        phase(0, "start")
        body(*ins, *outs, *scr)
        phase(mid_step, "mid")
        phase(total - 1, "finish")

    aliases, i0, o0 = dict(aliased or {}), n_in, n_out
    for h in hosted:
        aliases.update({i0 + i: o0 + o for i, o in h.aliases.items()})
        i0, o0 = i0 + len(h.inputs), o0 + len(h.out_shape)
    reach = max((h.reach for h in hosted), default=None)
    params = dict(vmem_limit_bytes=VMEM_LIMIT_V7X)
    if hosted:
        params["collective_id"] = reach
    results = pl.pallas_call(
        full, name=name, grid=grid,
        in_specs=list(in_specs) + [ANY] * (i0 - n_in),
        out_specs=list(out_specs) + [ANY] * (o0 - n_out),
        out_shape=list(out_shape) + [s for h in hosted for s in h.out_shape],
        scratch_shapes=list(scratch_shapes) + [s for h in hosted for s in h.sems],
        input_output_aliases=aliases,
        compiler_params=pltpu.CompilerParams(**params),
    )(*args, *[s for h in hosted for s in h.inputs])
    outs, extras, pos = list(results[:n_out]), [], n_out
    for h in hosted:
        extras.append(list(results[pos:pos + h.n]))
        pos += len(h.out_shape)
    return outs, extras


def cast_shards(shards, name, hosted=()):
    n = len(shards)

    def body(*refs):
        for x_ref, o_ref in zip(refs[:n], refs[n:]):
            o_ref[...] = x_ref[...].astype(BF16)

    whole = lambda s: pl.BlockSpec(s.shape, lambda: (0,) * s.ndim)
    return _call(body, hosted, name=name, in_specs=[whole(s) for s in shards], out_specs=[whole(s) for s in shards],
                 out_shape=[jax.ShapeDtypeStruct(s.shape, BF16) for s in shards], args=list(shards))


class AllExchange:
    def __init__(self, pack):
        self.inputs, self.n, self.aliases, self.reach = [pack], 1, {}, REACH_ALL
        self.out_shape = [jax.ShapeDtypeStruct((2 * N_CHIPS,) + pack.shape, pack.dtype)]
        self.sems = [pltpu.SemaphoreType.DMA, pltpu.SemaphoreType.DMA((7,)), pltpu.SemaphoreType.DMA((7,))]

    def _copies(self, src, out, sems):
        local_sem, send_sem, recv_sem = sems
        x, y, c = _mesh_pos()
        flips = [(dx, dy, dc) for dx in (0, 1) for dy in (0, 1) for dc in (0, 1)][1:]
        peers = [(x ^ dx, y ^ dy, c ^ dc) for dx, dy, dc in flips]
        remote = lambda s, d, k: pltpu.make_async_remote_copy(
            src_ref=s, dst_ref=d, send_sem=send_sem.at[k], recv_sem=recv_sem.at[k], device_id=peers[k], device_id_type=MESH)
        sends = [remote(src[0], out[0].at[4 * x + 2 * y + c], k) for k in range(7)]
        landed = [remote(out[0].at[4 * px + 2 * py + pc], out[0].at[4 * px + 2 * py + pc], k) for k, (px, py, pc) in enumerate(peers)]
        return sends, landed, pltpu.make_async_copy(src[0], out[0].at[4 * x + 2 * y + c], local_sem)

    def start(self, src, out, sems):
        sends, _, local = self._copies(src, out, sems)
        for cp in sends:
            cp.start()
        local.start()

    def mid(self, src, out, sems):
        pass

    def finish(self, src, out, sems):
        sends, landed, local = self._copies(src, out, sems)
        for cp in landed:
            cp.wait_recv()
        for cp in sends:
            cp.wait_send()
        local.wait()


MXU_COLS = 256


def _resident(shape):
    return pl.BlockSpec(shape, lambda *_: (0,) * len(shape), pipeline_mode=pl.Buffered(1))


def ffn_up(h, gain, w1, w3, name, hosted=(), mixed=None):
    T, D = h.shape
    F = w1.shape[0]
    tm = min(T, 512)

    def body(*refs):
        if mixed is None:
            h_ref, g_ref, w1_ref, w3_ref, n_ref, ga_ref, gb_ref, s_ref = refs
            hh = h_ref[...]
        else:
            pa_ref, rb_ref, wo_ref, h_ref, g_ref, w1_ref, w3_ref, hh_ref, n_ref, ga_ref, gb_ref, s_ref = refs
            hh = h_ref[...] + _dot(pa_ref[...], wo_ref[0]) + _dot(rb_ref[...], wo_ref[1])
            hh_ref[...] = hh
        n = (hh * _rstd(hh) * g_ref[...]).astype(BF16)
        n_ref[...] = n
        for c in range(0, F, MXU_COLS):
            cols = slice(c, c + MXU_COLS)
            a = _dot_nt(n, w1_ref[cols, :])
            b = _dot_nt(n, w3_ref[cols, :])
            silu, dsilu = _silu_parts(a)
            ga_ref[:, cols] = (b * dsilu).astype(BF16)
            gb_ref[:, cols] = silu.astype(BF16)
            s_ref[:, cols] = (silu * b).astype(BF16)

    act = jax.ShapeDtypeStruct((T, F), BF16)
    act_spec = pl.BlockSpec((tm, F), lambda i: (i, 0))
    row_spec = pl.BlockSpec((tm, D), lambda i: (i, 0))
    in_specs = [row_spec, pl.BlockSpec((1, D), lambda i: (0, 0)), _resident((F, D)), _resident((F, D))]
    out_specs, out_shape, args = [row_spec, act_spec, act_spec, act_spec], [jax.ShapeDtypeStruct((T, D), BF16), act, act, act], [h, gain, w1, w3]
    if mixed is not None:
        pa, rb, woutg = mixed
        W = pa.shape[1]
        in_specs = [pl.BlockSpec((tm, W), lambda i: (i, 0))] * 2 + [_resident((2, W, D))] + in_specs
        out_specs, out_shape = [row_spec] + out_specs, [jax.ShapeDtypeStruct((T, D), F32)] + out_shape
        args = [pa, rb, woutg.reshape(2, W, D)] + args
    return _call(body, hosted, name=name, grid=(T // tm,), in_specs=in_specs, out_specs=out_specs, out_shape=out_shape, args=args)


def ffn_bwd_act(dh, w2, ga, gb, name, hosted=()):
    T, D = dh.shape
    F = w2.shape[0]
    tm = min(T, 512)

    def body(dh_ref, w2_ref, ga_ref, gb_ref, da_ref, db_ref, df_ref):
        df = (0.5 * dh_ref[...]).astype(BF16)
        df_ref[...] = df
        for c in range(0, F, MXU_COLS):
            cols = slice(c, c + MXU_COLS)
            ds = _dot_nt(df, w2_ref[cols, :])
            da_ref[:, cols] = (ds * ga_ref[:, cols].astype(F32)).astype(BF16)
            db_ref[:, cols] = (ds * gb_ref[:, cols].astype(F32)).astype(BF16)

    act = jax.ShapeDtypeStruct((T, F), BF16)
    act_spec = pl.BlockSpec((tm, F), lambda i: (i, 0))
    row_spec = pl.BlockSpec((tm, D), lambda i: (i, 0))
    return _call(
        body, hosted, name=name, grid=(T // tm,),
        in_specs=[row_spec, _resident((F, D)), act_spec, act_spec],
        out_specs=[act_spec, act_spec, row_spec],
        out_shape=[act, act, jax.ShapeDtypeStruct((T, D), BF16)],
        args=[dh, w2, ga, gb])


def ffn_dw(xs, y, halves, name, hosted=()):
    T, F = xs[0].shape
    D = y.shape[1]
    nx, fh = len(xs), F // halves
    tk = min(T, 512)
    nk = T // tk

    def body(*refs):
        y_ref, x_refs, o_refs, accs = refs[0], refs[1:1 + nx], refs[1 + nx:1 + 2 * nx], refs[1 + 2 * nx:]
        k = pl.program_id(1)

        @pl.when(k == 0)
        def _():
            for acc in accs:
                acc[...] = jnp.zeros_like(acc)

        yy = y_ref[...]
        for x_ref, acc in zip(x_refs, accs):
            acc[...] += _dot_tn(x_ref[...], yy)

        @pl.when(k == nk - 1)
        def _():
            for o_ref, acc in zip(o_refs, accs):
                o_ref[...] = acc[...].astype(BF16)

    out = jax.ShapeDtypeStruct((F, D), BF16)
    return _call(
        body, hosted, name=name, grid=(halves, nk),
        in_specs=[pl.BlockSpec((tk, D), lambda j, k: (k, 0))] + [pl.BlockSpec((tk, fh), lambda j, k: (k, j))] * nx,
        out_specs=[pl.BlockSpec((fh, D), lambda j, k: (j, 0))] * nx,
        out_shape=[out] * nx,
        scratch_shapes=[pltpu.VMEM((fh, D), F32)] * nx,
        args=[y] + list(xs))


def ffn_bwd_in(da, db, w1, w3, h, gain, dh, name, hosted=()):
    T, F = da.shape
    D = h.shape[1]
    tm = min(T, 512)

    def body(da_ref, db_ref, w1_ref, w3_ref, h_ref, g_ref, dh_ref, o_ref, dg_ref):
        dn = _dot(da_ref[...], w1_ref[...]) + _dot(db_ref[...], w3_ref[...])
        dhn, dg = _rmsnorm_bwd(dn, h_ref[...], g_ref[...])
        o_ref[...] = dh_ref[...] + dhn

        @pl.when(pl.program_id(0) == 0)
        def _():
            dg_ref[...] = jnp.zeros_like(dg_ref)

        dg_ref[...] += jnp.sum(dg, axis=0, keepdims=True)

    act_spec = pl.BlockSpec((tm, F), lambda i: (i, 0))
    row_spec = pl.BlockSpec((tm, D), lambda i: (i, 0))
    vec_spec = pl.BlockSpec((1, D), lambda i: (0, 0))
    return _call(
        body, hosted, name=name, grid=(T // tm,),
        in_specs=[act_spec, act_spec, _resident((F, D)), _resident((F, D)), row_spec, vec_spec, row_spec],
        out_specs=[row_spec, vec_spec],
        out_shape=[jax.ShapeDtypeStruct((T, D), F32), jax.ShapeDtypeStruct((1, D), F32)],
        args=[da, db, w1, w3, h, gain, dh])


def ffn_down_mix_in(s, w2, h, gain, wing, name, hosted=()):
    T, F = s.shape
    D = h.shape[1]
    nsh, _, Cs = wing.shape
    tm = min(T, 512)

    def body(s_ref, w2_ref, h_ref, g_ref, w_ref, hh_ref, u_ref, p_ref):
        hh = h_ref[...] + 0.5 * _dot(s_ref[...], w2_ref[...])
        hh_ref[...] = hh
        u = (hh * _rstd(hh) * g_ref[...]).astype(BF16)
        u_ref[...] = u
        for j in range(nsh):
            p_ref[:, j * Cs:(j + 1) * Cs] = _dot(u, w_ref[j])

    row_spec = pl.BlockSpec((tm, D), lambda i: (i, 0))
    return _call(
        body, hosted, name=name, grid=(T // tm,),
        in_specs=[pl.BlockSpec((tm, F), lambda i: (i, 0)), _resident((F, D)), row_spec, pl.BlockSpec((1, D), lambda i: (0, 0)),
                  _resident((nsh, D, Cs))],
        out_specs=[row_spec, row_spec, pl.BlockSpec((tm, nsh * Cs), lambda i: (i, 0))],
        out_shape=[jax.ShapeDtypeStruct((T, D), F32), jax.ShapeDtypeStruct((T, D), BF16), jax.ShapeDtypeStruct((T, nsh * Cs), F32)],
        args=[s, w2, h, gain, wing])


def mix_out_bwd(dh, woutg, a, b, name, hosted=()):
    T, D = dh.shape
    W = a.shape[1]
    nsh, Rs, _ = woutg.shape
    wout = woutg.reshape(2, W, D)
    tk = min(T, 512)
    nk = T // tk

    def body(dh_ref, w_ref, a_ref, b_ref, da_ref, db_ref, dw_ref, acc):
        k = pl.program_id(0)

        @pl.when(k == 0)
        def _():
            acc[...] = jnp.zeros_like(acc)

        dhb = dh_ref[...].astype(BF16)
        da_ref[...] = _dot_nt(dhb, w_ref[0])
        db_ref[...] = _dot_nt(dhb, w_ref[1])
        acc[0:W, :] += _dot_tn(a_ref[...], dhb)
        acc[W:2 * W, :] += _dot_tn(b_ref[...], dhb)

        @pl.when(k == nk - 1)
        def _():
            for j in range(nsh):
                dw_ref[j] = acc[j * Rs:(j + 1) * Rs, :].astype(BF16)

    return _call(
        body, hosted, name=name, grid=(nk,),
        in_specs=[pl.BlockSpec((tk, D), lambda k: (k, 0)), pl.BlockSpec((2, W, D), lambda k: (0, 0, 0)),
                  pl.BlockSpec((tk, W), lambda k: (k, 0)), pl.BlockSpec((tk, W), lambda k: (k, 0))],
        out_specs=[pl.BlockSpec((tk, W), lambda k: (k, 0)), pl.BlockSpec((tk, W), lambda k: (k, 0)),
                   pl.BlockSpec((nsh, Rs, D), lambda k: (0, 0, 0))],
        out_shape=[jax.ShapeDtypeStruct((T, W), F32), jax.ShapeDtypeStruct((T, W), F32),
                   jax.ShapeDtypeStruct((nsh, Rs, D), BF16)],
        scratch_shapes=[pltpu.VMEM((2 * W, D), F32)],
        args=[dh, wout, a, b])


def _dproj_block(g):
    return (g // N_GROUPS + N_GROUPS) % (N_GROUPS + 1), g % N_GROUPS


def mix_dwin(u, dproj, nsh, name, hosted=()):
    T, D = u.shape
    Hd = HEAD_DIM
    slabs, _, width = dproj.shape
    blocks = slabs * width // Hd
    Cs = blocks * Hd // nsh
    tk = min(T, 512)
    nk = T // tk

    def body(u_ref, d_ref, o_ref, acc):
        k = pl.program_id(0)

        @pl.when(k == 0)
        def _():
            acc[...] = jnp.zeros_like(acc)

        where = [_dproj_block(g) for g in range(blocks)]
        d = jnp.concatenate([d_ref[slab, :, col * Hd:(col + 1) * Hd] for slab, col in where], axis=1)
        acc[...] += _dot_tn(u_ref[...], d)

        @pl.when(k == nk - 1)
        def _():
            for j in range(nsh):
                o_ref[j] = acc[:, j * Cs:(j + 1) * Cs].astype(BF16)

    return _call(
        body, hosted, name=name, grid=(nk,),
        in_specs=[pl.BlockSpec((tk, D), lambda k: (k, 0)), pl.BlockSpec((slabs, tk, width), lambda k: (0, k, 0))],
        out_specs=[pl.BlockSpec((nsh, D, Cs), lambda k: (0, 0, 0))],
        out_shape=[jax.ShapeDtypeStruct((nsh, D, Cs), BF16)],
        scratch_shapes=[pltpu.VMEM((D, blocks * Hd), F32)],
        args=[u, dproj])


def mix_in_bwd(dproj, wing, h, gain, dh, name, hosted=()):
    T, D = h.shape
    nsh, _, Cs = wing.shape
    Hd = HEAD_DIM
    per = Cs // Hd
    tm = min(T, 512)

    def body(d_ref, w_ref, h_ref, g_ref, dh_ref, o_ref, dg_ref):
        def shard(j):
            blocks = [_dproj_block(per * j + i) for i in range(per)]
            return jnp.concatenate([d_ref[slab, :, col * Hd:(col + 1) * Hd] for slab, col in blocks], axis=1)

        du = _dot_nt(shard(0), w_ref[0])
        for j in range(1, nsh):
            du += _dot_nt(shard(j), w_ref[j])
        dhn, dg = _rmsnorm_bwd(du, h_ref[...], g_ref[...])
        o_ref[...] = dh_ref[...] + dhn

        @pl.when(pl.program_id(0) == 0)
        def _():
            dg_ref[...] = jnp.zeros_like(dg_ref)

        dg_ref[...] += jnp.sum(dg, axis=0, keepdims=True)

    row_spec = pl.BlockSpec((tm, D), lambda i: (i, 0))
    vec_spec = pl.BlockSpec((1, D), lambda i: (0, 0))
    return _call(
        body, hosted, name=name, grid=(T // tm,),
        in_specs=[pl.BlockSpec((dproj.shape[0], tm, dproj.shape[2]), lambda i: (0, i, 0)),
                  pl.BlockSpec((nsh, D, Cs), lambda i: (0, 0, 0)), row_spec, vec_spec, row_spec],
        out_specs=[row_spec, vec_spec],
        out_shape=[jax.ShapeDtypeStruct((T, D), F32), jax.ShapeDtypeStruct((1, D), F32)],
        args=[dproj, wing, h, gain, dh])


def _pool_window(x, group, T, trailing):
    rows = lax.broadcasted_iota(jnp.int32, x.shape, 0)

    def shifted(z, k):
        if trailing:
            return jnp.where(rows >= k, pltpu.roll(z, k, 0), 0.0)
        return jnp.where(rows < T - k, pltpu.roll(z, T - k, 0), 0.0)

    s2 = x + shifted(x, 1)
    s4 = s2 + shifted(s2, 2)
    s8 = s4 + shifted(s4, 4)
    s16 = s8 + shifted(s8, 8)
    return jnp.where(group == 0, s2, jnp.where(group == 1, s4, jnp.where(group == 2, s8, s16)))


def _pool_count(group, shape):
    rows = lax.broadcasted_iota(jnp.int32, shape, 0)
    w = jnp.where(group == 0, 2, jnp.where(group == 1, 4, jnp.where(group == 2, 8, 16)))
    return jnp.minimum(rows + 1, w).astype(F32)


def pool_fwd(proj, pool_w, pool_scale, name, hosted=()):
    T = proj.shape[0]
    Hd = HEAD_DIM

    def body(x_ref, w_ref, sc_ref, a_ref):
        g = pl.program_id(0)
        x = x_ref[...]
        pooled = _pool_window(x, g, T, True) / _pool_count(g, x.shape) - x
        a_ref[...] = (_dot(pooled.astype(BF16), w_ref[0].astype(BF16)) * sc_ref[...]).astype(BF16)

    return _call(
        body, hosted, name=name, grid=(N_GROUPS,),
        in_specs=[pl.BlockSpec((T, Hd), lambda g: (0, g)), pl.BlockSpec((1, Hd, Hd), lambda g: (g, 0, 0)),
                  pl.BlockSpec((1, Hd), lambda g: (0, g))],
        out_specs=[pl.BlockSpec((T, Hd), lambda g: (0, g))],
        out_shape=[jax.ShapeDtypeStruct((T, N_GROUPS * Hd), BF16)],
        args=[proj, pool_w, pool_scale])


def pool_bwd(proj, da, pool_w, pool_scale, name, hosted=()):
    T = proj.shape[0]
    Hd = HEAD_DIM

    def body(x_ref, da_ref, w_ref, sc_ref, dx_ref, dw_ref, dsc_ref):
        g = pl.program_id(0)
        x = x_ref[...]
        cnt = _pool_count(g, x.shape)
        pooled = (_pool_window(x, g, T, True) / cnt - x).astype(BF16)
        wb = w_ref[0].astype(BF16)
        dav = da_ref[...]
        dsc_ref[...] = jnp.sum(dav * _dot(pooled, wb), axis=0, keepdims=True)
        dout = (dav * sc_ref[...]).astype(BF16)
        dw_ref[0] = _dot_tn(pooled, dout)
        dpooled = _dot_nt(dout, wb)
        dx_ref[0] = (_pool_window(dpooled / cnt, g, T, False) - dpooled).astype(BF16)

    col_spec = pl.BlockSpec((T, Hd), lambda g: (0, g))
    return _call(
        body, hosted, name=name, grid=(N_GROUPS,),
        in_specs=[col_spec, col_spec, pl.BlockSpec((1, Hd, Hd), lambda g: (g, 0, 0)), pl.BlockSpec((1, Hd), lambda g: (0, g))],
        out_specs=[pl.BlockSpec((1, T, Hd), lambda g: (N_GROUPS, 0, g)), pl.BlockSpec((1, Hd, Hd), lambda g: (g, 0, 0)),
                   pl.BlockSpec((1, Hd), lambda g: (0, g))],
        out_shape=[jax.ShapeDtypeStruct((N_GROUPS + 1, T, N_GROUPS * Hd), BF16), jax.ShapeDtypeStruct((N_GROUPS, Hd, Hd), F32),
                   jax.ShapeDtypeStruct((1, N_GROUPS * Hd), F32)],
        args=[proj, da, pool_w, pool_scale])


def _ret_tables(T):
    Hd, C = HEAD_DIM, RET_CHUNK
    inv_freq = 1.0 / (ROPE_BASE ** (jnp.arange(0, Hd, 2, dtype=F32) / Hd))
    ang = jnp.arange(T, dtype=F32)[:, None] * inv_freq[None, :]
    cos, sin = jnp.cos(ang), jnp.sin(ang)
    cos2 = jnp.concatenate([cos, cos], axis=-1)
    sin2 = jnp.concatenate([-sin, sin], axis=-1)
    log_gamma = jnp.log1p(-jnp.exp2(-5.0 - jnp.arange(N_GROUPS, dtype=F32)))
    pos = jnp.arange(C, dtype=F32)
    rel = pos[:, None] - pos[None, :]
    intra = jnp.where(rel[None] >= 0, jnp.exp(log_gamma[:, None, None] * jnp.maximum(rel, 0.0)[None]), 0.0)
    k_tail = jnp.exp(log_gamma[:, None] * (C - 1 - pos)[None, :])
    q_head = jnp.exp(log_gamma[:, None] * (pos + 1.0)[None, :])
    chunk_decay = jnp.exp(log_gamma * C)
    wide = lambda t: jnp.broadcast_to(t[:, :, None], (N_GROUPS, C, Hd))
    return cos2, sin2, intra, wide(k_tail), wide(q_head), jnp.broadcast_to(chunk_decay[:, None, None], (N_GROUPS, 1, Hd))


def _rope(x, cos2, sin2):
    return x * cos2 + pltpu.roll(x, HEAD_DIM // 2, 1) * sin2


def _rope_t(d, cos2, sin2):
    return d * cos2 + pltpu.roll(d * sin2, HEAD_DIM // 2, 1)


def _ret_specs(tseg, seg_of):
    Hd, G = HEAD_DIM, N_GROUPS
    col = lambda kind: pl.BlockSpec((tseg, Hd), lambda h, s: (seg_of(s), G * kind + h))
    tab = pl.BlockSpec((tseg, Hd), lambda h, s: (seg_of(s), 0))
    head = pl.BlockSpec((1, RET_CHUNK, Hd), lambda h, s: (h, 0, 0))
    cd = pl.BlockSpec((1, 1, Hd), lambda h, s: (h, 0, 0))
    gain = pl.BlockSpec((1, Hd), lambda h, s: (0, h))
    return col, tab, head, cd, gain


def ret_fwd(proj, ret_norm, tables, name, hosted=()):
    T = proj.shape[0]
    Hd, C, G = HEAD_DIM, RET_CHUNK, N_GROUPS
    tseg = min(T, 1024)
    nseg, nck = T // tseg, tseg // C
    scale = Hd ** -0.5
    cos2, sin2, intra, k_tail, q_head, chunk_decay = tables

    def body(q_ref, k_ref, v_ref, g_ref, gain_ref, cos_ref, sin_ref, m_ref, kt_ref, qh_ref, cd_ref,
             b_ref, o_ref, rp_ref, state):
        @pl.when(pl.program_id(1) == 0)
        def _():
            state[...] = jnp.zeros_like(state)

        def chunk(ci, carry):
            rows = pl.ds(pl.multiple_of(ci * C, C), C)
            cos, sin = cos_ref[rows, :], sin_ref[rows, :]
            qr = _rope(q_ref[rows, :], cos, sin)
            kr = _rope(k_ref[rows, :], cos, sin) * scale
            qb, kb, vb = qr.astype(BF16), kr.astype(BF16), v_ref[rows, :].astype(BF16)
            r = state[...]
            rp_ref[0, ci] = r.astype(BF16)
            sc = _dot_nt(qb, kb) * m_ref[0]
            o = _dot(sc.astype(BF16), vb) + _dot((qr * qh_ref[0]).astype(BF16), r.astype(BF16))
            state[...] = cd_ref[0] * r + _dot_tn((kr * kt_ref[0]).astype(BF16), vb)
            o_ref[rows, :] = o
            on = o * _rstd(o)
            b_ref[rows, :] = (jax.nn.silu(g_ref[rows, :]) * (on * gain_ref[...])).astype(BF16)
            return carry

        lax.fori_loop(0, nck, chunk, 0, unroll=True)

    col, tab, head, cd, gain = _ret_specs(tseg, lambda s: s)
    out_col = pl.BlockSpec((tseg, Hd), lambda h, s: (s, h))
    return _call(
        body, hosted, name=name, grid=(G, nseg),
        in_specs=[col(1), col(2), col(3), col(4), gain, tab, tab, head, head, head, cd],
        out_specs=[out_col, out_col, pl.BlockSpec((1, nck, Hd, Hd), lambda h, s: (h, s, 0, 0))],
        out_shape=[jax.ShapeDtypeStruct((T, G * Hd), BF16), jax.ShapeDtypeStruct((T, G * Hd), F32),
                   jax.ShapeDtypeStruct((G, T // C, Hd, Hd), BF16)],
        scratch_shapes=[pltpu.VMEM((Hd, Hd), F32)],
        args=[proj, proj, proj, proj, ret_norm, cos2, sin2, intra, k_tail, q_head, chunk_decay])


def ret_bwd(proj, db, o_pre, r_prev, ret_norm, tables, dproj, name, hosted=()):
    T = proj.shape[0]
    Hd, C, G = HEAD_DIM, RET_CHUNK, N_GROUPS
    tseg = min(T, 1024)
    nseg, nck = T // tseg, tseg // C
    scale = Hd ** -0.5
    cos2, sin2, intra, k_tail, q_head, chunk_decay = tables

    def body(q_ref, k_ref, v_ref, g_ref, db_ref, o_ref, rp_ref, gain_ref, cos_ref, sin_ref, m_ref, kt_ref, qh_ref, cd_ref,
             _, d_ref, dgain_ref, gstate):
        @pl.when(pl.program_id(1) == 0)
        def _():
            gstate[...] = jnp.zeros_like(gstate)
            dgain_ref[...] = jnp.zeros_like(dgain_ref)

        def chunk(t, carry):
            ci = nck - 1 - t
            rows = pl.ds(pl.multiple_of(ci * C, C), C)
            cos, sin = cos_ref[rows, :], sin_ref[rows, :]
            qr = _rope(q_ref[rows, :], cos, sin)
            kr = _rope(k_ref[rows, :], cos, sin) * scale
            qb, kb, vb = qr.astype(BF16), kr.astype(BF16), v_ref[rows, :].astype(BF16)
            qhb, ktb = (qr * qh_ref[0]).astype(BF16), (kr * kt_ref[0]).astype(BF16)
            sc = (_dot_nt(qb, kb) * m_ref[0]).astype(BF16)
            o = o_ref[rows, :]
            rstd = _rstd(o)
            on = o * rstd
            gain = gain_ref[...]
            silu, dsilu = _silu_parts(g_ref[rows, :])
            dy = db_ref[rows, :]
            dgain_ref[...] += jnp.sum(dy * silu * on, axis=0, keepdims=True)
            dg = dy * on * gain * dsilu
            don = dy * silu * gain
            dob = (rstd * (don - on * jnp.mean(don * on, axis=-1, keepdims=True))).astype(BF16)
            gn = gstate[...]
            gb = gn.astype(BF16)
            da = (_dot_nt(dob, vb) * m_ref[0]).astype(BF16)
            dq = _dot(da, kb) + _dot_nt(dob, rp_ref[0, ci]) * qh_ref[0]
            dk = _dot_tn(da, qb) + _dot_nt(vb, gb) * kt_ref[0]
            dv = _dot_tn(sc, dob) + _dot(ktb, gb)
            gstate[...] = cd_ref[0] * gn + _dot_tn(qhb, dob)
            d_ref[0, rows, :] = _rope_t(dq, cos, sin).astype(BF16)
            d_ref[1, rows, :] = _rope_t(dk * scale, cos, sin).astype(BF16)
            d_ref[2, rows, :] = dv.astype(BF16)
            d_ref[3, rows, :] = dg.astype(BF16)
            return carry

        lax.fori_loop(0, nck, chunk, 0, unroll=True)

    rev = lambda s: nseg - 1 - s
    col, tab, head, cd, gain = _ret_specs(tseg, rev)
    act = pl.BlockSpec((tseg, Hd), lambda h, s: (rev(s), h))
    return _call(
        body, hosted, name=name, grid=(G, nseg),
        in_specs=[col(1), col(2), col(3), col(4), act, act, pl.BlockSpec((1, nck, Hd, Hd), lambda h, s: (h, rev(s), 0, 0)),
                  gain, tab, tab, head, head, head, cd, ANY],
        out_specs=[pl.BlockSpec((4, tseg, Hd), lambda h, s: (0, rev(s), h)), gain],
        out_shape=[jax.ShapeDtypeStruct(dproj.shape, BF16), jax.ShapeDtypeStruct((1, G * Hd), F32)],
        scratch_shapes=[pltpu.VMEM((Hd, Hd), F32)], aliased={14: 0},
        args=[proj, proj, proj, proj, db, o_pre, r_prev, ret_norm, cos2, sin2, intra, k_tail, q_head, chunk_decay, dproj])


def ffn_down_loss(s, w2, h, gain, target, name, hosted=()):
    T, F = s.shape
    D = h.shape[1]
    tm = min(T, 512)

    def body(s_ref, w2_ref, h_ref, g_ref, t_ref, dh_ref, loss_ref, dg_ref):
        @pl.when(pl.program_id(0) == 0)
        def _():
            loss_ref[...] = jnp.zeros_like(loss_ref)
            dg_ref[...] = jnp.zeros_like(dg_ref)

        hh = h_ref[...] + 0.5 * _dot(s_ref[...], w2_ref[...])
        gain_v = g_ref[...]
        err = hh * _rstd(hh) * gain_v - t_ref[...]
        loss_ref[...] += 0.5 * jnp.sum(jnp.mean(err * err, axis=-1, keepdims=True), axis=0, keepdims=True)
        dhn, dg = _rmsnorm_bwd(err * (1.0 / D), hh, gain_v)
        dh_ref[...] = dhn
        dg_ref[...] += jnp.sum(dg, axis=0, keepdims=True)

    row_spec = pl.BlockSpec((tm, D), lambda i: (i, 0))
    vec_spec = pl.BlockSpec((1, D), lambda i: (0, 0))
    return _call(
        body, hosted, name=name, grid=(T // tm,),
        in_specs=[pl.BlockSpec((tm, F), lambda i: (i, 0)), _resident((F, D)), row_spec, vec_spec, row_spec],
        out_specs=[row_spec, pl.BlockSpec((1, 128), lambda i: (0, 0)), vec_spec],
        out_shape=[jax.ShapeDtypeStruct((T, D), F32), jax.ShapeDtypeStruct((1, 128), F32), jax.ShapeDtypeStruct((1, D), F32)],
        args=[s, w2, h, gain, target])


def prereduce(grads, recvs, place, name):
    nt = len(grads)
    nsh, R, C = grads[0].shape
    rh = R // 2

    def body(place_ref, *refs):
        for t in range(nt):
            g_ref, r_ref, o_ref, own_ref = refs[2 * t], refs[2 * t + 1], refs[2 * nt + 2 * t], refs[2 * nt + 2 * t + 1]
            piece = (g_ref[...].astype(F32) + r_ref[...].astype(F32)).astype(BF16)
            o_ref[...] = piece

            @pl.when(pl.program_id(0) == place_ref[1])
            def _():
                own_ref[...] = piece

    outs = pl.pallas_call(
        body, name=name,
        grid_spec=pltpu.PrefetchScalarGridSpec(
            num_scalar_prefetch=1, grid=(nsh,),
            in_specs=[pl.BlockSpec((1, rh, C), lambda j, p: (j, p[0], 0)), pl.BlockSpec((1, rh, C), lambda j, p: (j, 0, 0))] * nt,
            out_specs=[pl.BlockSpec((1, rh, C), lambda j, p: (j, 0, 0)),
                       pl.BlockSpec((1, rh, C), lambda j, p: (p[1], p[0], 0))] * nt),
        out_shape=[jax.ShapeDtypeStruct((nsh, rh, C), BF16), jax.ShapeDtypeStruct((nsh, R, C), BF16)] * nt,
        compiler_params=pltpu.CompilerParams(vmem_limit_bytes=VMEM_LIMIT_V7X),
    )(place, *[a for pair in zip(grads, recvs) for a in pair])
    return [(outs[2 * t], outs[2 * t + 1]) for t in range(nt)]


def _adamw(w, g, m, v):
    m = ADAM_B1 * m + (1.0 - ADAM_B1) * g
    v = ADAM_B2 * v + (1.0 - ADAM_B2) * (g * g)
    m_hat = m / (1.0 - ADAM_B1 ** ADAM_STEP)
    v_hat = v / (1.0 - ADAM_B2 ** ADAM_STEP)
    return -ADAM_LR * (m_hat / (jnp.sqrt(v_hat) + ADAM_EPS) + ADAM_WD * w), m, v


def adamw_sharded(tensors, name, hosted=()):
    nt = len(tensors)
    nsh, R, C = tensors[0][0].shape
    lanes = -(-C // 128) * 128
    per_row = 2 * nt * lanes * (nsh * 2 + 7 * 4)
    tr = max(r for r in range(16, R + 1, 16) if R % r == 0 and r * per_row <= ADAMW_VMEM_BUDGET)

    def body(*refs):
        ins, outs = refs[:4 * nt], refs[4 * nt:]
        for t in range(nt):
            p_ref, w_ref, m_ref, v_ref = ins[4 * t:4 * t + 4]
            g_ref, d_ref, nm_ref, nv_ref = outs[4 * t:4 * t + 4]
            g = p_ref[0].astype(F32)
            for i in range(1, nsh):
                g += p_ref[i].astype(F32)
            g_ref[...] = g
            d_ref[...], nm_ref[...], nv_ref[...] = _adamw(w_ref[...], g, m_ref[...], v_ref[...])

    spec = pl.BlockSpec((tr, C), lambda i: (i, 0))
    out = jax.ShapeDtypeStruct((R, C), F32)
    return _call(
        body, hosted, name=name, grid=(R // tr,),
        in_specs=[pl.BlockSpec((nsh, tr, C), lambda i: (0, i, 0)), spec, spec, spec] * nt,
        out_specs=[spec] * (4 * nt), out_shape=[out] * (4 * nt),
        args=[a for tensor in tensors for a in tensor])


def adamw_small(packs, late, pool, vectors, name):
    ndev = packs.shape[0]
    rp, rv, rl = pool[0].shape[0], vectors.shape[0] // 3, late.shape[1]

    def body(p_ref, l_ref, wp, mp, vp, wmv, gp, dp, nmp, nvp, gv, dv, nmv, nvv, loss_ref):
        wv, mv, vv = wmv.at[0:rv], wmv.at[rv:2 * rv], wmv.at[2 * rv:3 * rv]
        g, first = p_ref[0], l_ref[0]
        for i in range(1, ndev):
            g += p_ref[i]
            first += l_ref[i]
        g_pool = g[0:rp]
        g_vec = jnp.concatenate([g[rp:rp + rl] + first, g[rp + rl:rp + rv]], axis=0)
        gp[...], gv[...], loss_ref[...] = g_pool, g_vec, g[rp + rv:rp + rv + 8]
        dp[...], nmp[...], nvp[...] = _adamw(wp[...], g_pool, mp[...], vp[...])
        dv[...], nmv[...], nvv[...] = _adamw(wv[...], g_vec, mv[...], vv[...])

    shape = lambda rows: jax.ShapeDtypeStruct((rows, 128), F32)
    outs = pl.pallas_call(body, name=name, out_shape=[shape(rp)] * 4 + [shape(rv)] * 4 + [shape(8)],
                          compiler_params=pltpu.CompilerParams(vmem_limit_bytes=VMEM_LIMIT_V7X))(packs, late, *pool, vectors)
    return outs[0:4], outs[4:8], outs[8]


BIG = ("ffn1_w1", "ffn1_w3", "ffn1_w2", "w_in", "w_out", "ffn2_w1", "ffn2_w3", "ffn2_w2")
TRANSPOSED = ("ffn1_w1", "ffn1_w3", "ffn2_w1", "ffn2_w3")
VECTORS = ("ffn1_norm", "mix_norm", "pool_scale", "ret_norm", "ffn2_norm", "final_norm")
WEIGHTS = ("ffn1_norm", "ffn1_w1", "ffn1_w3", "ffn1_w2", "mix_norm", "w_in", "pool_w", "pool_scale", "ret_norm", "w_out",
           "ffn2_norm", "ffn2_w1", "ffn2_w3", "ffn2_w2", "final_norm")


def _pack_vectors(parts):
    return jnp.concatenate([parts[k].reshape(-1, 128) for k in VECTORS], axis=0)


def _unpack_vectors(pack, like):
    out, row = {}, 0
    for k in VECTORS:
        rows = like[k].size // 128
        out[k] = pack[row:row + rows].reshape(like[k].shape)
        row += rows
    return out


def kernel(x, ffn1_norm, ffn1_w1, ffn1_w3, ffn1_w2, mix_norm, w_in, pool_w, pool_scale, ret_norm, w_out, ffn2_norm, ffn2_w1, ffn2_w3, ffn2_w2, final_norm, loss_target, m_ffn1_norm, m_ffn1_w1, m_ffn1_w3, m_ffn1_w2, m_mix_norm, m_w_in, m_pool_w, m_pool_scale, m_ret_norm, m_w_out, m_ffn2_norm, m_ffn2_w1, m_ffn2_w3, m_ffn2_w2, m_final_norm, v_ffn1_norm, v_ffn1_w1, v_ffn1_w3, v_ffn1_w2, v_mix_norm, v_w_in, v_pool_w, v_pool_scale, v_ret_norm, v_w_out, v_ffn2_norm, v_ffn2_w1, v_ffn2_w3, v_ffn2_w2, v_final_norm):
    w = dict(ffn1_norm=ffn1_norm, ffn1_w1=ffn1_w1, ffn1_w3=ffn1_w3, ffn1_w2=ffn1_w2, mix_norm=mix_norm, w_in=w_in, pool_w=pool_w,
             pool_scale=pool_scale, ret_norm=ret_norm, w_out=w_out, ffn2_norm=ffn2_norm, ffn2_w1=ffn2_w1, ffn2_w3=ffn2_w3,
             ffn2_w2=ffn2_w2, final_norm=final_norm)
    m = dict(ffn1_norm=m_ffn1_norm, ffn1_w1=m_ffn1_w1, ffn1_w3=m_ffn1_w3, ffn1_w2=m_ffn1_w2, mix_norm=m_mix_norm, w_in=m_w_in,
             pool_w=m_pool_w, pool_scale=m_pool_scale, ret_norm=m_ret_norm, w_out=m_w_out, ffn2_norm=m_ffn2_norm, ffn2_w1=m_ffn2_w1,
             ffn2_w3=m_ffn2_w3, ffn2_w2=m_ffn2_w2, final_norm=m_final_norm)
    v = dict(ffn1_norm=v_ffn1_norm, ffn1_w1=v_ffn1_w1, ffn1_w3=v_ffn1_w3, ffn1_w2=v_ffn1_w2, mix_norm=v_mix_norm, w_in=v_w_in,
             pool_w=v_pool_w, pool_scale=v_pool_scale, ret_norm=v_ret_norm, w_out=v_w_out, ffn2_norm=v_ffn2_norm, ffn2_w1=v_ffn2_w1,
             ffn2_w3=v_ffn2_w3, ffn2_w2=v_ffn2_w2, final_norm=v_final_norm)
    xs, target = x[0], loss_target[0]
    T = xs.shape[0]
    tables = _ret_tables(T)
    place = jnp.stack([lax.axis_index("c"), 2 * lax.axis_index("x") + lax.axis_index("y")]).astype(jnp.int32)
    local = lambda d, k: jnp.transpose(d[k][0]) if k in TRANSPOSED else d[k][0]
    result = lambda o, k: jnp.transpose(o)[None] if k in TRANSPOSED else o[None]
    first = ("ffn1_w1", "ffn1_w3")
    sh = {k: local(w, k).astype(BF16) for k in first}
    gather = lambda *names: [ChipExchange([sh[k] for k in names], False)]
    wg, grad, delta, new_m, new_v = {}, {}, {}, {}, {}

    def update(names, pieces, name, hosted=()):
        outs, extras = adamw_sharded([(p, local(w, k), local(m, k), local(v, k)) for k, p in zip(names, pieces)], name, hosted)
        for t, k in enumerate(names):
            grad[k], delta[k], new_m[k], new_v[k] = [result(o, k) for o in outs[4 * t:4 * t + 4]]
        return extras

    def reduce_in_chip(name, *pairs):
        reduced = prereduce([p for p, _ in pairs], [r for _, r in pairs], place, "prereduce_" + name)
        return reduced[0] if len(pairs) == 1 else reduced

    scatter = lambda *reduced: ChipExchange([r[0] for r in reduced], True, [r[1] for r in reduced])
    whole = lambda k: wg[k].reshape(-1, wg[k].shape[-1])
    sharded = lambda g: g.reshape(N_CHIPS, -1, g.shape[-1])

    later = [k for k in BIG if k not in first]
    casts, ((wg["ffn1_w1"], wg["ffn1_w3"]),) = cast_shards([local(w, k) for k in later], "cast_gather_ffn1", gather(*first))
    sh.update(zip(later, casts))
    (n1, ga1, gb1, s1), ((wg["ffn1_w2"], wg["w_in"]),) = ffn_up(
        xs, ffn1_norm, whole("ffn1_w1"), whole("ffn1_w3"), "ffn1_up", gather("ffn1_w2", "w_in"))
    (h1, u, proj), ((wg["w_out"], wg["ffn2_w1"]),) = ffn_down_mix_in(
        s1, whole("ffn1_w2"), xs, mix_norm, wg["w_in"], "ffn1_down_mix_in", gather("w_out", "ffn2_w1"))
    (pa,), _ = pool_fwd(proj, pool_w[0], pool_scale, "pool_fwd")
    (rb, o_pre, r_prev), ((wg["ffn2_w3"],),) = ret_fwd(proj, ret_norm, tables, "ret_fwd", gather("ffn2_w3"))
    (h2, n2, ga2, gb2, s2), ((wg["ffn2_w2"],),) = ffn_up(
        h1, ffn2_norm, whole("ffn2_w1"), whole("ffn2_w3"), "mix_out_ffn2_up", gather("ffn2_w2"), mixed=(pa, rb, wg["w_out"]))
    (dh3, loss, d_final), _ = ffn_down_loss(s2, whole("ffn2_w2"), h2, final_norm[None], target, "ffn2_down_loss")

    (da2, db2, df2), _ = ffn_bwd_act(dh3, whole("ffn2_w2"), ga2, gb2, "ffn2_bwd_act")
    (g_f2w2,), _ = ffn_dw([s2], df2, 1, "ffn2_dw2")
    g_f2w2 = sharded(g_f2w2)
    (g_f2w1, g_f2w3), ((r_f2w2,),) = ffn_dw([da2, db2], n2, 2, "ffn2_dw13", [SiblingExchange([g_f2w2])])
    g_f2w1, g_f2w3 = sharded(g_f2w1), sharded(g_f2w3)
    p_f2w2 = reduce_in_chip("ffn2_w2", (g_f2w2, r_f2w2))
    (dh2, d_ffn2), ((q_f2w2,), (r_f2w1, r_f2w3)) = ffn_bwd_in(
        da2, db2, whole("ffn2_w1"), whole("ffn2_w3"), h2, ffn2_norm, dh3, "ffn2_bwd_in",
        [scatter(p_f2w2), SiblingExchange([g_f2w1, g_f2w3])])
    p_f2w1, p_f2w3 = reduce_in_chip("ffn2_w13", (g_f2w1, r_f2w1), (g_f2w3, r_f2w3))
    (dpa, drb, g_wout), _ = mix_out_bwd(dh2, wg["w_out"], pa, rb, "mix_out_bwd")
    (dproj, d_pool_w, d_pool_scale), _ = pool_bwd(proj, dpa, pool_w[0], pool_scale, "pool_bwd")
    (dproj, d_ret_norm), ((q_f2w1, q_f2w3), (r_wout,)) = ret_bwd(
        proj, drb, o_pre, r_prev, ret_norm, tables, dproj, "ret_bwd", [scatter(p_f2w1, p_f2w3), SiblingExchange([g_wout])])
    p_wout = reduce_in_chip("w_out", (g_wout, r_wout))
    (g_win,), ((q_wout,),) = mix_dwin(u, dproj, N_CHIPS, "mix_dwin", [scatter(p_wout)])
    (dh1, d_mix), ((r_win,),) = mix_in_bwd(dproj, wg["w_in"], h1, mix_norm, dh2, "mix_in_bwd", [SiblingExchange([g_win])])
    p_win = reduce_in_chip("w_in", (g_win, r_win))
    (da1, db1, df1), ((q_win,),) = ffn_bwd_act(dh1, whole("ffn1_w2"), ga1, gb1, "ffn1_bwd_act", [scatter(p_win)])
    d_vectors = {"ffn1_norm": jnp.zeros_like(ffn1_norm), "mix_norm": d_mix, "pool_scale": d_pool_scale,
                 "ret_norm": d_ret_norm, "ffn2_norm": d_ffn2, "final_norm": d_final}
    pack = jnp.concatenate([d_pool_w.reshape(-1, 128), _pack_vectors(d_vectors), jnp.broadcast_to(loss, (8, 128))], axis=0)
    (g_f1w1, g_f1w3), ((packs,),) = ffn_dw([da1, db1], n1, 2, "ffn1_dw13", [AllExchange(pack)])
    g_f1w1, g_f1w3 = sharded(g_f1w1), sharded(g_f1w3)
    (g_f1w2,), ((r_f1w1, r_f1w3),) = ffn_dw([s1], df1, 1, "ffn1_dw2", [SiblingExchange([g_f1w1, g_f1w3])])
    g_f1w2 = sharded(g_f1w2)
    p_f1w1, p_f1w3 = reduce_in_chip("ffn1_w13", (g_f1w1, r_f1w1), (g_f1w3, r_f1w3))
    (dx, d_ffn1), ((q_f1w1, q_f1w3), (r_f1w2,)) = ffn_bwd_in(
        da1, db1, whole("ffn1_w1"), whole("ffn1_w3"), xs, ffn1_norm, dh1, "ffn1_bwd_in",
        [scatter(p_f1w1, p_f1w3), SiblingExchange([g_f1w2])])
    p_f1w2 = reduce_in_chip("ffn1_w2", (g_f1w2, r_f1w2))

    (q_f1w2,), (late,) = update(["ffn2_w1", "ffn2_w3", "ffn1_w1", "ffn1_w3"], [q_f2w1, q_f2w3, q_f1w1, q_f1w3], "adamw_w13",
                                [scatter(p_f1w2), AllExchange(d_ffn1.reshape(-1, 128))])
    update(["ffn2_w2", "ffn1_w2"], [q_f2w2, q_f1w2], "adamw_w2")
    update(["w_in"], [q_win], "adamw_w_in")
    update(["w_out"], [q_wout], "adamw_w_out")
    of_pool, of_vectors, loss_sum = adamw_small(packs, late, [t["pool_w"].reshape(-1, 128) for t in (w, m, v)],
                                                jnp.concatenate([t[k].reshape(-1, 128) for t in (w, m, v) for k in VECTORS], axis=0),
                                                "adamw_small")
    for res, pool_part, vector_part in zip((grad, delta, new_m, new_v), of_pool, of_vectors):
        res["pool_w"] = pool_part.reshape(pool_w.shape)
        res.update(_unpack_vectors(vector_part, w))
    loss = loss_sum[0, 0]

    return (loss, dx[None], *[grad[k] for k in WEIGHTS], *[delta[k] for k in WEIGHTS],
            *[new_m[k] for k in WEIGHTS], *[new_v[k] for k in WEIGHTS])
```

```python
import math

import jax
import jax.numpy as jnp
from jax import lax
from jax.experimental import pallas as pl
from jax.experimental.pallas import tpu as pltpu

F32 = jnp.float32
BF16 = jnp.bfloat16

EPS = 1e-6
LANES = 128
BF16_TILE_ROWS = 16
N_CHIPS = 4
N_GROUPS = 4
HEAD_DIM = 128
RET_CHUNK = 128
ROPE_BASE = 10000.0
ADAM_LR, ADAM_B1, ADAM_B2, ADAM_EPS, ADAM_WD, ADAM_STEP = 0.001, 0.9, 0.999, 1e-08, 0.01, 10
VMEM_LIMIT_V7X = 56 * 1024 * 1024
ADAMW_VMEM_BUDGET = 32 * 1024 * 1024
MESH = pl.DeviceIdType.MESH
ANY = pl.BlockSpec(memory_space=pl.ANY)


def _dot(a, b):
    return jnp.dot(a, b, preferred_element_type=F32)


def _dot_nt(a, b):
    return lax.dot_general(a, b, (((1,), (1,)), ((), ())), preferred_element_type=F32)


def _dot_tn(a, b):
    return lax.dot_general(a, b, (((0,), (0,)), ((), ())), preferred_element_type=F32)


def _rstd(h):
    return lax.rsqrt(jnp.mean(h * h, axis=-1, keepdims=True) + EPS)


def _rmsnorm_bwd(dn, h, gain):
    r = _rstd(h)
    nh = h * r
    dnh = dn * gain
    dh = r * (dnh - nh * jnp.mean(dnh * nh, axis=-1, keepdims=True))
    return dh, dn * nh


def _silu_parts(a):
    sig = jax.nn.sigmoid(a)
    silu = a * sig
    return silu, sig + silu * (1.0 - sig)


def _mesh_pos():
    return lax.axis_index("x"), lax.axis_index("y"), lax.axis_index("c")


class ChipExchange:
    def __init__(self, srcs, scatter, placed=()):
        n = len(srcs)
        self.inputs, self.scatter, self.n, self.reach = list(srcs) + list(placed), scatter, n, REACH_CHIPS
        self.aliases = {n + t: t for t in range(n)} if scatter else {}
        self.half_rows = [s.shape[1] if scatter else s.shape[0] // 2 for s in srcs]
        self.out_shape = [jax.ShapeDtypeStruct((N_CHIPS, 2 * rh, s.shape[-1]), s.dtype) for s, rh in zip(srcs, self.half_rows)]
        if scatter:
            self.out_shape += [jax.ShapeDtypeStruct((2, rh // 2, s.shape[-1]), s.dtype) for s, rh in zip(srcs, self.half_rows)]
        dma = pltpu.SemaphoreType.DMA
        self.sems = [dma((4 * n,)), dma((4 * n,)), dma((2 * n,)), dma((2 * n,)), dma((4 * n,)), dma((4 * n,))]

    def _copies(self, src, out, sems):
        hop1_send, hop1_recv, hop2_send, hop2_recv, d2d_send, d2d_recv = sems
        x, y, c = _mesh_pos()
        me, dg = 2 * x + y, 2 * (1 - x) + (1 - y)
        sibling = (x, y, 1 - c)
        n = self.n
        mine, theirs = c, 1 - c

        def nb(a):
            nx, ny = x ^ (1 - a), y ^ a
            return 2 * nx + ny, (nx, ny, c)

        def remote(s, d, send, recv, k, to):
            return pltpu.make_async_remote_copy(src_ref=s, dst_ref=d, send_sem=send.at[k], recv_sem=recv.at[k],
                                                device_id=to, device_id_type=MESH)

        class Copies:
            def slot(_, t, chip, half):
                rh = self.half_rows[t]
                return out[t].at[chip, pl.ds(half * rh, rh), :]

            def quarter(_, t, chip, q):
                qh = self.half_rows[t] // 2
                return out[t].at[chip, pl.ds(mine * 2 * qh + q * qh, qh), :]

            def own_shard(k, t):
                return remote(src[t], out[t].at[me], d2d_send, d2d_recv, 4 * t + 3, sibling)

            def hop1(k, t, a, transit=False):
                rh = self.half_rows[t]
                chip, to = nb(a)
                if transit:
                    piece = src[t].at[dg, pl.ds(a * (rh // 2), rh // 2), :]
                    return remote(piece, out[n + t].at[a], hop1_send, hop1_recv, 4 * t + 2 + a, to)
                piece = src[t].at[chip] if self.scatter else src[t].at[pl.ds(mine * rh, rh), :]
                return remote(piece, k.slot(t, me, mine), hop1_send, hop1_recv, 4 * t + a, to)

            def landed1(k, t, a, transit=False):
                here = out[n + t].at[a] if transit else k.slot(t, nb(a)[0], mine)
                return remote(here, here, hop1_send, hop1_recv, 4 * t + (2 if transit else 0) + a, sibling)

            def hop2(k, t, q):
                origin, to = nb(q)[0], nb(1 - q)[1]
                piece = out[n + t].at[q] if self.scatter else k.quarter(t, origin, q)
                return remote(piece, k.quarter(t, origin, q), hop2_send, hop2_recv, 2 * t + q, to)

            def landed2(k, t, q):
                here = k.quarter(t, dg, q)
                return remote(here, here, hop2_send, hop2_recv, 2 * t + q, sibling)

            def d2d(k, t, p, chip, own=False, arriving=False):
                if arriving:
                    there = k.slot(t, chip, theirs)
                    return remote(there, there, d2d_send, d2d_recv, 4 * t + p, sibling)
                piece = src[t].at[me] if own else k.slot(t, chip, mine)
                return remote(piece, k.slot(t, chip, mine), d2d_send, d2d_recv, 4 * t + p, sibling)

        return Copies(), nb, me, dg, c

    def start(self, src, out, sems):
        k, nb, me, dg, c = self._copies(src, out, sems)
        for t in range(self.n):
            for first in range(2):
                a = first ^ c
                k.hop1(t, a).start()
                if self.scatter:
                    k.hop1(t, a, transit=True).start()
            if self.scatter:
                k.d2d(t, 3, me, own=True).start()
            else:
                k.own_shard(t).start()

    def mid(self, src, out, sems):
        k, nb, me, dg, c = self._copies(src, out, sems)
        for t in range(self.n):
            for first in range(2):
                a = first ^ c
                if self.scatter:
                    k.landed1(t, a, transit=True).wait_recv()
                    k.hop2(t, a).start()
                k.landed1(t, a).wait_recv()
                if not self.scatter:
                    k.hop2(t, a).start()
                k.d2d(t, a, nb(a)[0]).start()

    def finish(self, src, out, sems):
        k, nb, me, dg, c = self._copies(src, out, sems)
        for t in range(self.n):
            for q in range(2):
                k.landed2(t, q).wait_recv()
            k.d2d(t, 2, dg).start()
        for t in range(self.n):
            for a in range(2):
                k.d2d(t, a, nb(a)[0], arriving=True).wait_recv()
            k.d2d(t, 2, dg, arriving=True).wait_recv()
            if self.scatter:
                k.d2d(t, 3, me, arriving=True).wait_recv()
        for t in range(self.n):
            for a in range(2):
                k.hop1(t, a).wait_send()
                if self.scatter:
                    k.hop1(t, a, transit=True).wait_send()
                k.hop2(t, a).wait_send()
                k.d2d(t, a, nb(a)[0]).wait_send()
            k.d2d(t, 2, dg).wait_send()
            if self.scatter:
                k.d2d(t, 3, me, own=True).wait_send()
            else:
                k.own_shard(t).wait()


class SiblingExchange:
    def __init__(self, grads):
        self.inputs, self.n, self.aliases, self.reach = list(grads), len(grads), {}, REACH_SIBLING
        self.half_rows = [g.shape[1] // 2 for g in grads]
        self.out_shape = [jax.ShapeDtypeStruct((g.shape[0], rh, g.shape[2]), g.dtype) for g, rh in zip(grads, self.half_rows)]
        self.sems = [pltpu.SemaphoreType.DMA((self.n,)), pltpu.SemaphoreType.DMA((self.n,))]

    def _plan(self, src, out, sems):
        x, y, c = _mesh_pos()
        return [pltpu.make_async_remote_copy(
            src_ref=src[t].at[:, pl.ds((1 - c) * self.half_rows[t], self.half_rows[t]), :], dst_ref=out[t],
            send_sem=sems[0].at[t], recv_sem=sems[1].at[t], device_id=(x, y, 1 - c), device_id_type=MESH) for t in range(self.n)]

    def start(self, src, out, sems):
        for cp in self._plan(src, out, sems):
            cp.start()

    def mid(self, src, out, sems):
        pass

    def finish(self, src, out, sems):
        for cp in self._plan(src, out, sems):
            cp.wait()


REACH_SIBLING, REACH_CHIPS, REACH_ALL = 0, 1, 2


def _entry_barrier(reach):
    x, y, c = _mesh_pos()
    peers = [(x, y, 1 - c)]
    if reach == REACH_CHIPS:
        peers += [(1 - x, y, c), (x, 1 - y, c)]
    elif reach == REACH_ALL:
        peers = [(x ^ dx, y ^ dy, c ^ dc) for dx in (0, 1) for dy in (0, 1) for dc in (0, 1)][1:]
    barrier = pltpu.get_barrier_semaphore()
    for peer in peers:
        pl.semaphore_signal(barrier, inc=1, device_id=peer, device_id_type=MESH)
    pl.semaphore_wait(barrier, len(peers))


def _call(body, hosted=(), *, name, in_specs, out_specs, out_shape, args, grid=(), scratch_shapes=(), aliased=None):
    n_in, n_out, n_scr = len(in_specs), len(out_specs), len(scratch_shapes)
    total = math.prod(grid)
    mid_step = max(0, (5 * total) // 8 - 1)

    def full(*refs):
        pos = [0]

        def take(k):
            pos[0] += k
            return refs[pos[0] - k:pos[0]]

        ins, h_in = take(n_in), [take(len(h.inputs)) for h in hosted]
        outs, h_out = take(n_out), [take(len(h.out_shape)) for h in hosted]
        scr, h_sem = take(n_scr), [take(len(h.sems)) for h in hosted]
        step = 0
        for axis, size in enumerate(grid):
            step = step * size + pl.program_id(axis)

        def phase(at, method):
            if not hosted:
                return

            def run():
                if method == "start":
                    _entry_barrier(reach)
                for h, s, o, m in zip(hosted, h_in, h_out, h_sem):
                    getattr(h, method)(s, o, m)

            if total == 1:
                run()
            else:
                pl.when(step == at)(run)

        phase(0, "start")
        body(*ins, *outs, *scr)
        phase(mid_step, "mid")
        phase(total - 1, "finish")

    aliases, i0, o0 = dict(aliased or {}), n_in, n_out
    for h in hosted:
        aliases.update({i0 + i: o0 + o for i, o in h.aliases.items()})
        i0, o0 = i0 + len(h.inputs), o0 + len(h.out_shape)
    reach = max((h.reach for h in hosted), default=None)
    params = dict(vmem_limit_bytes=VMEM_LIMIT_V7X)
    if hosted:
        params["collective_id"] = reach
    results = pl.pallas_call(
        full, name=name, grid=grid,
        in_specs=list(in_specs) + [ANY] * (i0 - n_in),
        out_specs=list(out_specs) + [ANY] * (o0 - n_out),
        out_shape=list(out_shape) + [s for h in hosted for s in h.out_shape],
        scratch_shapes=list(scratch_shapes) + [s for h in hosted for s in h.sems],
        input_output_aliases=aliases,
        compiler_params=pltpu.CompilerParams(**params),
    )(*args, *[s for h in hosted for s in h.inputs])
    outs, extras, pos = list(results[:n_out]), [], n_out
    for h in hosted:
        extras.append(list(results[pos:pos + h.n]))
        pos += len(h.out_shape)
    return outs, extras


def cast_shards(shards, name, hosted=()):
    n = len(shards)

    def body(*refs):
        for x_ref, o_ref in zip(refs[:n], refs[n:]):
            o_ref[...] = x_ref[...].astype(BF16)

    whole = lambda s: pl.BlockSpec(s.shape, lambda: (0,) * s.ndim)
    return _call(body, hosted, name=name, in_specs=[whole(s) for s in shards], out_specs=[whole(s) for s in shards],
                 out_shape=[jax.ShapeDtypeStruct(s.shape, BF16) for s in shards], args=list(shards))


class AllExchange:
    def __init__(self, pack):
        self.inputs, self.n, self.aliases, self.reach = [pack], 1, {}, REACH_ALL
        self.out_shape = [jax.ShapeDtypeStruct((2 * N_CHIPS,) + pack.shape, pack.dtype)]
        self.sems = [pltpu.SemaphoreType.DMA, pltpu.SemaphoreType.DMA((7,)), pltpu.SemaphoreType.DMA((7,))]

    def _copies(self, src, out, sems):
        local_sem, send_sem, recv_sem = sems
        x, y, c = _mesh_pos()
        flips = [(dx, dy, dc) for dx in (0, 1) for dy in (0, 1) for dc in (0, 1)][1:]
        peers = [(x ^ dx, y ^ dy, c ^ dc) for dx, dy, dc in flips]
        remote = lambda s, d, k: pltpu.make_async_remote_copy(
            src_ref=s, dst_ref=d, send_sem=send_sem.at[k], recv_sem=recv_sem.at[k], device_id=peers[k], device_id_type=MESH)
        sends = [remote(src[0], out[0].at[4 * x + 2 * y + c], k) for k in range(7)]
        landed = [remote(out[0].at[4 * px + 2 * py + pc], out[0].at[4 * px + 2 * py + pc], k) for k, (px, py, pc) in enumerate(peers)]
        return sends, landed, pltpu.make_async_copy(src[0], out[0].at[4 * x + 2 * y + c], local_sem)

    def start(self, src, out, sems):
        sends, _, local = self._copies(src, out, sems)
        for cp in sends:
            cp.start()
        local.start()

    def mid(self, src, out, sems):
        pass

    def finish(self, src, out, sems):
        sends, landed, local = self._copies(src, out, sems)
        for cp in landed:
            cp.wait_recv()
        for cp in sends:
            cp.wait_send()
        local.wait()


MXU_COLS = 256


def _resident(shape):
    return pl.BlockSpec(shape, lambda *_: (0,) * len(shape), pipeline_mode=pl.Buffered(1))


def ffn_up(h, gain, w1, w3, name, hosted=(), mixed=None):
    T, D = h.shape
    F = w1.shape[0]
    tm = min(T, 512)

    def body(*refs):
        if mixed is None:
            h_ref, g_ref, w1_ref, w3_ref, n_ref, ga_ref, gb_ref, s_ref = refs
            hh = h_ref[...]
        else:
            pa_ref, rb_ref, wo_ref, h_ref, g_ref, w1_ref, w3_ref, hh_ref, n_ref, ga_ref, gb_ref, s_ref = refs
            hh = h_ref[...] + _dot(pa_ref[...], wo_ref[0]) + _dot(rb_ref[...], wo_ref[1])
            hh_ref[...] = hh
        n = (hh * _rstd(hh) * g_ref[...]).astype(BF16)
        n_ref[...] = n
        for c in range(0, F, MXU_COLS):
            cols = slice(c, c + MXU_COLS)
            a = _dot_nt(n, w1_ref[cols, :])
            b = _dot_nt(n, w3_ref[cols, :])
            silu, dsilu = _silu_parts(a)
            ga_ref[:, cols] = (b * dsilu).astype(BF16)
            gb_ref[:, cols] = silu.astype(BF16)
            s_ref[:, cols] = (silu * b).astype(BF16)

    act = jax.ShapeDtypeStruct((T, F), BF16)
    act_spec = pl.BlockSpec((tm, F), lambda i: (i, 0))
    row_spec = pl.BlockSpec((tm, D), lambda i: (i, 0))
    in_specs = [row_spec, pl.BlockSpec((1, D), lambda i: (0, 0)), _resident((F, D)), _resident((F, D))]
    out_specs, out_shape, args = [row_spec, act_spec, act_spec, act_spec], [jax.ShapeDtypeStruct((T, D), BF16), act, act, act], [h, gain, w1, w3]
    if mixed is not None:
        pa, rb, woutg = mixed
        W = pa.shape[1]
        in_specs = [pl.BlockSpec((tm, W), lambda i: (i, 0))] * 2 + [_resident((2, W, D))] + in_specs
        out_specs, out_shape = [row_spec] + out_specs, [jax.ShapeDtypeStruct((T, D), F32)] + out_shape
        args = [pa, rb, woutg.reshape(2, W, D)] + args
    return _call(body, hosted, name=name, grid=(T // tm,), in_specs=in_specs, out_specs=out_specs, out_shape=out_shape, args=args)


def ffn_bwd_act(dh, w2, ga, gb, name, hosted=()):
    T, D = dh.shape
    F = w2.shape[0]
    tm = min(T, 512)

    def body(dh_ref, w2_ref, ga_ref, gb_ref, da_ref, db_ref, df_ref):
        df = (0.5 * dh_ref[...]).astype(BF16)
        df_ref[...] = df
        for c in range(0, F, MXU_COLS):
            cols = slice(c, c + MXU_COLS)
            ds = _dot_nt(df, w2_ref[cols, :])
            da_ref[:, cols] = (ds * ga_ref[:, cols].astype(F32)).astype(BF16)
            db_ref[:, cols] = (ds * gb_ref[:, cols].astype(F32)).astype(BF16)

    act = jax.ShapeDtypeStruct((T, F), BF16)
    act_spec = pl.BlockSpec((tm, F), lambda i: (i, 0))
    row_spec = pl.BlockSpec((tm, D), lambda i: (i, 0))
    return _call(
        body, hosted, name=name, grid=(T // tm,),
        in_specs=[row_spec, _resident((F, D)), act_spec, act_spec],
        out_specs=[act_spec, act_spec, row_spec],
        out_shape=[act, act, jax.ShapeDtypeStruct((T, D), BF16)],
        args=[dh, w2, ga, gb])


def ffn_dw(xs, y, halves, name, hosted=()):
    T, F = xs[0].shape
    D = y.shape[1]
    nx, fh = len(xs), F // halves
    tk = min(T, 512)
    nk = T // tk

    def body(*refs):
        y_ref, x_refs, o_refs, accs = refs[0], refs[1:1 + nx], refs[1 + nx:1 + 2 * nx], refs[1 + 2 * nx:]
        k = pl.program_id(1)

        @pl.when(k == 0)
        def _():
            for acc in accs:
                acc[...] = jnp.zeros_like(acc)

        yy = y_ref[...]
        for x_ref, acc in zip(x_refs, accs):
            acc[...] += _dot_tn(x_ref[...], yy)

        @pl.when(k == nk - 1)
        def _():
            for o_ref, acc in zip(o_refs, accs):
                o_ref[...] = acc[...].astype(BF16)

    out = jax.ShapeDtypeStruct((F, D), BF16)
    return _call(
        body, hosted, name=name, grid=(halves, nk),
        in_specs=[pl.BlockSpec((tk, D), lambda j, k: (k, 0))] + [pl.BlockSpec((tk, fh), lambda j, k: (k, j))] * nx,
        out_specs=[pl.BlockSpec((fh, D), lambda j, k: (j, 0))] * nx,
        out_shape=[out] * nx,
        scratch_shapes=[pltpu.VMEM((fh, D), F32)] * nx,
        args=[y] + list(xs))


def ffn_bwd_in(da, db, w1, w3, h, gain, dh, name, hosted=()):
    T, F = da.shape
    D = h.shape[1]
    tm = min(T, 512)

    def body(da_ref, db_ref, w1_ref, w3_ref, h_ref, g_ref, dh_ref, o_ref, dg_ref):
        dn = _dot(da_ref[...], w1_ref[...]) + _dot(db_ref[...], w3_ref[...])
        dhn, dg = _rmsnorm_bwd(dn, h_ref[...], g_ref[...])
        o_ref[...] = dh_ref[...] + dhn

        @pl.when(pl.program_id(0) == 0)
        def _():
            dg_ref[...] = jnp.zeros_like(dg_ref)

        dg_ref[...] += jnp.sum(dg, axis=0, keepdims=True)

    act_spec = pl.BlockSpec((tm, F), lambda i: (i, 0))
    row_spec = pl.BlockSpec((tm, D), lambda i: (i, 0))
    vec_spec = pl.BlockSpec((1, D), lambda i: (0, 0))
    return _call(
        body, hosted, name=name, grid=(T // tm,),
        in_specs=[act_spec, act_spec, _resident((F, D)), _resident((F, D)), row_spec, vec_spec, row_spec],
        out_specs=[row_spec, vec_spec],
        out_shape=[jax.ShapeDtypeStruct((T, D), F32), jax.ShapeDtypeStruct((1, D), F32)],
        args=[da, db, w1, w3, h, gain, dh])


def ffn_down_mix_in(s, w2, h, gain, wing, name, hosted=()):
    T, F = s.shape
    D = h.shape[1]
    nsh, _, Cs = wing.shape
    tm = min(T, 512)

    def body(s_ref, w2_ref, h_ref, g_ref, w_ref, hh_ref, u_ref, p_ref):
        hh = h_ref[...] + 0.5 * _dot(s_ref[...], w2_ref[...])
        hh_ref[...] = hh
        u = (hh * _rstd(hh) * g_ref[...]).astype(BF16)
        u_ref[...] = u
        for j in range(nsh):
            p_ref[:, j * Cs:(j + 1) * Cs] = _dot(u, w_ref[j])

    row_spec = pl.BlockSpec((tm, D), lambda i: (i, 0))
    return _call(
        body, hosted, name=name, grid=(T // tm,),
        in_specs=[pl.BlockSpec((tm, F), lambda i: (i, 0)), _resident((F, D)), row_spec, pl.BlockSpec((1, D), lambda i: (0, 0)),
                  _resident((nsh, D, Cs))],
        out_specs=[row_spec, row_spec, pl.BlockSpec((tm, nsh * Cs), lambda i: (i, 0))],
        out_shape=[jax.ShapeDtypeStruct((T, D), F32), jax.ShapeDtypeStruct((T, D), BF16), jax.ShapeDtypeStruct((T, nsh * Cs), F32)],
        args=[s, w2, h, gain, wing])


def mix_out_bwd(dh, woutg, a, b, name, hosted=()):
    T, D = dh.shape
    W = a.shape[1]
    nsh, Rs, _ = woutg.shape
    wout = woutg.reshape(2, W, D)
    tk = min(T, 512)
    nk = T // tk

    def body(dh_ref, w_ref, a_ref, b_ref, da_ref, db_ref, dw_ref, acc):
        k = pl.program_id(0)

        @pl.when(k == 0)
        def _():
            acc[...] = jnp.zeros_like(acc)

        dhb = dh_ref[...].astype(BF16)
        da_ref[...] = _dot_nt(dhb, w_ref[0])
        db_ref[...] = _dot_nt(dhb, w_ref[1])
        acc[0:W, :] += _dot_tn(a_ref[...], dhb)
        acc[W:2 * W, :] += _dot_tn(b_ref[...], dhb)

        @pl.when(k == nk - 1)
        def _():
            for j in range(nsh):
                dw_ref[j] = acc[j * Rs:(j + 1) * Rs, :].astype(BF16)

    return _call(
        body, hosted, name=name, grid=(nk,),
        in_specs=[pl.BlockSpec((tk, D), lambda k: (k, 0)), pl.BlockSpec((2, W, D), lambda k: (0, 0, 0)),
                  pl.BlockSpec((tk, W), lambda k: (k, 0)), pl.BlockSpec((tk, W), lambda k: (k, 0))],
        out_specs=[pl.BlockSpec((tk, W), lambda k: (k, 0)), pl.BlockSpec((tk, W), lambda k: (k, 0)),
                   pl.BlockSpec((nsh, Rs, D), lambda k: (0, 0, 0))],
        out_shape=[jax.ShapeDtypeStruct((T, W), F32), jax.ShapeDtypeStruct((T, W), F32),
                   jax.ShapeDtypeStruct((nsh, Rs, D), BF16)],
        scratch_shapes=[pltpu.VMEM((2 * W, D), F32)],
        args=[dh, wout, a, b])


def _dproj_block(g):
    return (g // N_GROUPS + N_GROUPS) % (N_GROUPS + 1), g % N_GROUPS


def mix_dwin(u, dproj, nsh, name, hosted=()):
    T, D = u.shape
    Hd = HEAD_DIM
    slabs, _, width = dproj.shape
    blocks = slabs * width // Hd
    Cs = blocks * Hd // nsh
    tk = min(T, 512)
    nk = T // tk

    def body(u_ref, d_ref, o_ref, acc):
        k = pl.program_id(0)

        @pl.when(k == 0)
        def _():
            acc[...] = jnp.zeros_like(acc)

        where = [_dproj_block(g) for g in range(blocks)]
        d = jnp.concatenate([d_ref[slab, :, col * Hd:(col + 1) * Hd] for slab, col in where], axis=1)
        acc[...] += _dot_tn(u_ref[...], d)

        @pl.when(k == nk - 1)
        def _():
            for j in range(nsh):
                o_ref[j] = acc[:, j * Cs:(j + 1) * Cs].astype(BF16)

    return _call(
        body, hosted, name=name, grid=(nk,),
        in_specs=[pl.BlockSpec((tk, D), lambda k: (k, 0)), pl.BlockSpec((slabs, tk, width), lambda k: (0, k, 0))],
        out_specs=[pl.BlockSpec((nsh, D, Cs), lambda k: (0, 0, 0))],
        out_shape=[jax.ShapeDtypeStruct((nsh, D, Cs), BF16)],
        scratch_shapes=[pltpu.VMEM((D, blocks * Hd), F32)],
        args=[u, dproj])


def mix_in_bwd(dproj, wing, h, gain, dh, name, hosted=()):
    T, D = h.shape
    nsh, _, Cs = wing.shape
    Hd = HEAD_DIM
    per = Cs // Hd
    tm = min(T, 512)

    def body(d_ref, w_ref, h_ref, g_ref, dh_ref, o_ref, dg_ref):
        def shard(j):
            blocks = [_dproj_block(per * j + i) for i in range(per)]
            return jnp.concatenate([d_ref[slab, :, col * Hd:(col + 1) * Hd] for slab, col in blocks], axis=1)

        du = _dot_nt(shard(0), w_ref[0])
        for j in range(1, nsh):
            du += _dot_nt(shard(j), w_ref[j])
        dhn, dg = _rmsnorm_bwd(du, h_ref[...], g_ref[...])
        o_ref[...] = dh_ref[...] + dhn

        @pl.when(pl.program_id(0) == 0)
        def _():
            dg_ref[...] = jnp.zeros_like(dg_ref)

        dg_ref[...] += jnp.sum(dg, axis=0, keepdims=True)

    row_spec = pl.BlockSpec((tm, D), lambda i: (i, 0))
    vec_spec = pl.BlockSpec((1, D), lambda i: (0, 0))
    return _call(
        body, hosted, name=name, grid=(T // tm,),
        in_specs=[pl.BlockSpec((dproj.shape[0], tm, dproj.shape[2]), lambda i: (0, i, 0)),
                  pl.BlockSpec((nsh, D, Cs), lambda i: (0, 0, 0)), row_spec, vec_spec, row_spec],
        out_specs=[row_spec, vec_spec],
        out_shape=[jax.ShapeDtypeStruct((T, D), F32), jax.ShapeDtypeStruct((1, D), F32)],
        args=[dproj, wing, h, gain, dh])


def _pool_window(x, group, T, trailing):
    rows = lax.broadcasted_iota(jnp.int32, x.shape, 0)

    def shifted(z, k):
        if trailing:
            return jnp.where(rows >= k, pltpu.roll(z, k, 0), 0.0)
        return jnp.where(rows < T - k, pltpu.roll(z, T - k, 0), 0.0)

    s2 = x + shifted(x, 1)
    s4 = s2 + shifted(s2, 2)
    s8 = s4 + shifted(s4, 4)
    s16 = s8 + shifted(s8, 8)
    return jnp.where(group == 0, s2, jnp.where(group == 1, s4, jnp.where(group == 2, s8, s16)))


def _pool_count(group, shape):
    rows = lax.broadcasted_iota(jnp.int32, shape, 0)
    w = jnp.where(group == 0, 2, jnp.where(group == 1, 4, jnp.where(group == 2, 8, 16)))
    return jnp.minimum(rows + 1, w).astype(F32)


def pool_fwd(proj, pool_w, pool_scale, name, hosted=()):
    T = proj.shape[0]
    Hd = HEAD_DIM

    def body(x_ref, w_ref, sc_ref, a_ref):
        g = pl.program_id(0)
        x = x_ref[...]
        pooled = _pool_window(x, g, T, True) / _pool_count(g, x.shape) - x
        a_ref[...] = (_dot(pooled.astype(BF16), w_ref[0].astype(BF16)) * sc_ref[...]).astype(BF16)

    return _call(
        body, hosted, name=name, grid=(N_GROUPS,),
        in_specs=[pl.BlockSpec((T, Hd), lambda g: (0, g)), pl.BlockSpec((1, Hd, Hd), lambda g: (g, 0, 0)),
                  pl.BlockSpec((1, Hd), lambda g: (0, g))],
        out_specs=[pl.BlockSpec((T, Hd), lambda g: (0, g))],
        out_shape=[jax.ShapeDtypeStruct((T, N_GROUPS * Hd), BF16)],
        args=[proj, pool_w, pool_scale])


def pool_bwd(proj, da, pool_w, pool_scale, name, hosted=()):
    T = proj.shape[0]
    Hd = HEAD_DIM

    def body(x_ref, da_ref, w_ref, sc_ref, dx_ref, dw_ref, dsc_ref):
        g = pl.program_id(0)
        x = x_ref[...]
        cnt = _pool_count(g, x.shape)
        pooled = (_pool_window(x, g, T, True) / cnt - x).astype(BF16)
        wb = w_ref[0].astype(BF16)
        dav = da_ref[...]
        dsc_ref[...] = jnp.sum(dav * _dot(pooled, wb), axis=0, keepdims=True)
        dout = (dav * sc_ref[...]).astype(BF16)
        dw_ref[0] = _dot_tn(pooled, dout)
        dpooled = _dot_nt(dout, wb)
        dx_ref[0] = (_pool_window(dpooled / cnt, g, T, False) - dpooled).astype(BF16)

    col_spec = pl.BlockSpec((T, Hd), lambda g: (0, g))
    return _call(
        body, hosted, name=name, grid=(N_GROUPS,),
        in_specs=[col_spec, col_spec, pl.BlockSpec((1, Hd, Hd), lambda g: (g, 0, 0)), pl.BlockSpec((1, Hd), lambda g: (0, g))],
        out_specs=[pl.BlockSpec((1, T, Hd), lambda g: (N_GROUPS, 0, g)), pl.BlockSpec((1, Hd, Hd), lambda g: (g, 0, 0)),
                   pl.BlockSpec((1, Hd), lambda g: (0, g))],
        out_shape=[jax.ShapeDtypeStruct((N_GROUPS + 1, T, N_GROUPS * Hd), BF16), jax.ShapeDtypeStruct((N_GROUPS, Hd, Hd), F32),
                   jax.ShapeDtypeStruct((1, N_GROUPS * Hd), F32)],
        args=[proj, da, pool_w, pool_scale])


def _ret_tables(T):
    Hd, C = HEAD_DIM, RET_CHUNK
    inv_freq = 1.0 / (ROPE_BASE ** (jnp.arange(0, Hd, 2, dtype=F32) / Hd))
    ang = jnp.arange(T, dtype=F32)[:, None] * inv_freq[None, :]
    cos, sin = jnp.cos(ang), jnp.sin(ang)
    cos2 = jnp.concatenate([cos, cos], axis=-1)
    sin2 = jnp.concatenate([-sin, sin], axis=-1)
    log_gamma = jnp.log1p(-jnp.exp2(-5.0 - jnp.arange(N_GROUPS, dtype=F32)))
    pos = jnp.arange(C, dtype=F32)
    rel = pos[:, None] - pos[None, :]
    intra = jnp.where(rel[None] >= 0, jnp.exp(log_gamma[:, None, None] * jnp.maximum(rel, 0.0)[None]), 0.0)
    k_tail = jnp.exp(log_gamma[:, None] * (C - 1 - pos)[None, :])
    q_head = jnp.exp(log_gamma[:, None] * (pos + 1.0)[None, :])
    chunk_decay = jnp.exp(log_gamma * C)
    wide = lambda t: jnp.broadcast_to(t[:, :, None], (N_GROUPS, C, Hd))
    return cos2, sin2, intra, wide(k_tail), wide(q_head), jnp.broadcast_to(chunk_decay[:, None, None], (N_GROUPS, 1, Hd))


def _rope(x, cos2, sin2):
    return x * cos2 + pltpu.roll(x, HEAD_DIM // 2, 1) * sin2


def _rope_t(d, cos2, sin2):
    return d * cos2 + pltpu.roll(d * sin2, HEAD_DIM // 2, 1)


def _ret_specs(T, tseg, seg_of):
    Hd, G = HEAD_DIM, N_GROUPS
    col = lambda kind: pl.BlockSpec((tseg, Hd), lambda h, s: (seg_of(s), G * kind + h))
    tab = pl.BlockSpec((T, Hd), lambda h, s: (0, 0))
    head = pl.BlockSpec((1, RET_CHUNK, Hd), lambda h, s: (h, 0, 0))
    cd = pl.BlockSpec((1, 1, Hd), lambda h, s: (h, 0, 0))
    gain = pl.BlockSpec((1, Hd), lambda h, s: (0, h))
    return col, tab, head, cd, gain


def ret_fwd(proj, ret_norm, tables, name, hosted=()):
    T = proj.shape[0]
    Hd, C, G = HEAD_DIM, RET_CHUNK, N_GROUPS
    tseg = min(T, 2048)
    nseg, nck = T // tseg, tseg // C
    scale = Hd ** -0.5
    cos2, sin2, intra, k_tail, q_head, chunk_decay = tables

    def body(q_ref, k_ref, v_ref, g_ref, gain_ref, cos_ref, sin_ref, m_ref, kt_ref, qh_ref, cd_ref,
             b_ref, o_ref, rp_ref, state):
        @pl.when(pl.program_id(1) == 0)
        def _():
            state[...] = jnp.zeros_like(state)

        def chunk(ci, carry):
            rows = pl.ds(pl.multiple_of(ci * C, C), C)
            at = pl.ds(pl.multiple_of(pl.program_id(1) * tseg + ci * C, C), C)
            cos, sin = cos_ref[at, :], sin_ref[at, :]
            qr = _rope(q_ref[rows, :], cos, sin)
            kr = _rope(k_ref[rows, :], cos, sin) * scale
            qb, kb, vb = qr.astype(BF16), kr.astype(BF16), v_ref[rows, :].astype(BF16)
            r = state[...]
            rp_ref[0, ci] = r.astype(BF16)
            sc = _dot_nt(qb, kb) * m_ref[0]
            o = _dot(sc.astype(BF16), vb) + _dot((qr * qh_ref[0]).astype(BF16), r.astype(BF16))
            state[...] = cd_ref[0] * r + _dot_tn((kr * kt_ref[0]).astype(BF16), vb)
            o_ref[rows, :] = o
            on = o * _rstd(o)
            b_ref[rows, :] = (jax.nn.silu(g_ref[rows, :]) * (on * gain_ref[...])).astype(BF16)
            return carry

        lax.fori_loop(0, nck, chunk, 0, unroll=True)

    col, tab, head, cd, gain = _ret_specs(T, tseg, lambda s: s)
    out_col = pl.BlockSpec((tseg, Hd), lambda h, s: (s, h))
    return _call(
        body, hosted, name=name, grid=(G, nseg),
        in_specs=[col(1), col(2), col(3), col(4), gain, tab, tab, head, head, head, cd],
        out_specs=[out_col, out_col, pl.BlockSpec((1, nck, Hd, Hd), lambda h, s: (h, s, 0, 0))],
        out_shape=[jax.ShapeDtypeStruct((T, G * Hd), BF16), jax.ShapeDtypeStruct((T, G * Hd), F32),
                   jax.ShapeDtypeStruct((G, T // C, Hd, Hd), BF16)],
        scratch_shapes=[pltpu.VMEM((Hd, Hd), F32)],
        args=[proj, proj, proj, proj, ret_norm, cos2, sin2, intra, k_tail, q_head, chunk_decay])


def ret_bwd(proj, db, o_pre, r_prev, ret_norm, tables, dproj, name, hosted=()):
    T = proj.shape[0]
    Hd, C, G = HEAD_DIM, RET_CHUNK, N_GROUPS
    tseg = min(T, 2048)
    nseg, nck = T // tseg, tseg // C
    scale = Hd ** -0.5
    cos2, sin2, intra, k_tail, q_head, chunk_decay = tables

    def body(q_ref, k_ref, v_ref, g_ref, db_ref, o_ref, rp_ref, gain_ref, cos_ref, sin_ref, m_ref, kt_ref, qh_ref, cd_ref,
             _, d_ref, dgain_ref, gstate):
        @pl.when(pl.program_id(1) == 0)
        def _():
            gstate[...] = jnp.zeros_like(gstate)
            dgain_ref[...] = jnp.zeros_like(dgain_ref)

        def chunk(t, carry):
            ci = nck - 1 - t
            rows = pl.ds(pl.multiple_of(ci * C, C), C)
            at = pl.ds(pl.multiple_of((nseg - 1 - pl.program_id(1)) * tseg + ci * C, C), C)
            cos, sin = cos_ref[at, :], sin_ref[at, :]
            qr = _rope(q_ref[rows, :], cos, sin)
            kr = _rope(k_ref[rows, :], cos, sin) * scale
            qb, kb, vb = qr.astype(BF16), kr.astype(BF16), v_ref[rows, :].astype(BF16)
            qhb, ktb = (qr * qh_ref[0]).astype(BF16), (kr * kt_ref[0]).astype(BF16)
            sc = (_dot_nt(qb, kb) * m_ref[0]).astype(BF16)
            o = o_ref[rows, :]
            rstd = _rstd(o)
            on = o * rstd
            gain = gain_ref[...]
            silu, dsilu = _silu_parts(g_ref[rows, :])
            dy = db_ref[rows, :]
            dgain_ref[...] += jnp.sum(dy * silu * on, axis=0, keepdims=True)
            dg = dy * on * gain * dsilu
            don = dy * silu * gain
            dob = (rstd * (don - on * jnp.mean(don * on, axis=-1, keepdims=True))).astype(BF16)
            gn = gstate[...]
            gb = gn.astype(BF16)
            da = (_dot_nt(dob, vb) * m_ref[0]).astype(BF16)
            dq = _dot(da, kb) + _dot_nt(dob, rp_ref[0, ci]) * qh_ref[0]
            dk = _dot_tn(da, qb) + _dot_nt(vb, gb) * kt_ref[0]
            dv = _dot_tn(sc, dob) + _dot(ktb, gb)
            gstate[...] = cd_ref[0] * gn + _dot_tn(qhb, dob)
            d_ref[0, rows, :] = _rope_t(dq, cos, sin).astype(BF16)
            d_ref[1, rows, :] = _rope_t(dk * scale, cos, sin).astype(BF16)
            d_ref[2, rows, :] = dv.astype(BF16)
            d_ref[3, rows, :] = dg.astype(BF16)
            return carry

        lax.fori_loop(0, nck, chunk, 0, unroll=True)

    rev = lambda s: nseg - 1 - s
    col, tab, head, cd, gain = _ret_specs(T, tseg, rev)
    act = pl.BlockSpec((tseg, Hd), lambda h, s: (rev(s), h))
    return _call(
        body, hosted, name=name, grid=(G, nseg),
        in_specs=[col(1), col(2), col(3), col(4), act, act, pl.BlockSpec((1, nck, Hd, Hd), lambda h, s: (h, rev(s), 0, 0)),
                  gain, tab, tab, head, head, head, cd, ANY],
        out_specs=[pl.BlockSpec((4, tseg, Hd), lambda h, s: (0, rev(s), h)), gain],
        out_shape=[jax.ShapeDtypeStruct(dproj.shape, BF16), jax.ShapeDtypeStruct((1, G * Hd), F32)],
        scratch_shapes=[pltpu.VMEM((Hd, Hd), F32)], aliased={14: 0},
        args=[proj, proj, proj, proj, db, o_pre, r_prev, ret_norm, cos2, sin2, intra, k_tail, q_head, chunk_decay, dproj])


def ffn_down_loss(s, w2, h, gain, target, name, hosted=()):
    T, F = s.shape
    D = h.shape[1]
    tm = min(T, 512)

    def body(s_ref, w2_ref, h_ref, g_ref, t_ref, dh_ref, loss_ref, dg_ref):
        @pl.when(pl.program_id(0) == 0)
        def _():
            loss_ref[...] = jnp.zeros_like(loss_ref)
            dg_ref[...] = jnp.zeros_like(dg_ref)

        hh = h_ref[...] + 0.5 * _dot(s_ref[...], w2_ref[...])
        gain_v = g_ref[...]
        err = hh * _rstd(hh) * gain_v - t_ref[...]
        loss_ref[...] += 0.5 * jnp.sum(jnp.mean(err * err, axis=-1, keepdims=True), axis=0, keepdims=True)
        dhn, dg = _rmsnorm_bwd(err * (1.0 / D), hh, gain_v)
        dh_ref[...] = dhn
        dg_ref[...] += jnp.sum(dg, axis=0, keepdims=True)

    row_spec = pl.BlockSpec((tm, D), lambda i: (i, 0))
    vec_spec = pl.BlockSpec((1, D), lambda i: (0, 0))
    return _call(
        body, hosted, name=name, grid=(T // tm,),
        in_specs=[pl.BlockSpec((tm, F), lambda i: (i, 0)), _resident((F, D)), row_spec, vec_spec, row_spec],
        out_specs=[row_spec, pl.BlockSpec((1, LANES), lambda i: (0, 0)), vec_spec],
        out_shape=[jax.ShapeDtypeStruct((T, D), F32), jax.ShapeDtypeStruct((1, LANES), F32), jax.ShapeDtypeStruct((1, D), F32)],
        args=[s, w2, h, gain, target])


def prereduce(grads, recvs, place, name):
    nt = len(grads)
    nsh, R, C = grads[0].shape
    rh = R // 2

    def body(place_ref, *refs):
        for t in range(nt):
            g_ref, r_ref, o_ref, own_ref = refs[2 * t], refs[2 * t + 1], refs[2 * nt + 2 * t], refs[2 * nt + 2 * t + 1]
            piece = (g_ref[...].astype(F32) + r_ref[...].astype(F32)).astype(BF16)
            o_ref[...] = piece

            @pl.when(pl.program_id(0) == place_ref[1])
            def _():
                own_ref[...] = piece

    outs = pl.pallas_call(
        body, name=name,
        grid_spec=pltpu.PrefetchScalarGridSpec(
            num_scalar_prefetch=1, grid=(nsh,),
            in_specs=[pl.BlockSpec((1, rh, C), lambda j, p: (j, p[0], 0)), pl.BlockSpec((1, rh, C), lambda j, p: (j, 0, 0))] * nt,
            out_specs=[pl.BlockSpec((1, rh, C), lambda j, p: (j, 0, 0)),
                       pl.BlockSpec((1, rh, C), lambda j, p: (p[1], p[0], 0))] * nt),
        out_shape=[jax.ShapeDtypeStruct((nsh, rh, C), BF16), jax.ShapeDtypeStruct((nsh, R, C), BF16)] * nt,
        compiler_params=pltpu.CompilerParams(vmem_limit_bytes=VMEM_LIMIT_V7X),
    )(place, *[a for pair in zip(grads, recvs) for a in pair])
    return [(outs[2 * t], outs[2 * t + 1]) for t in range(nt)]


def _adamw(w, g, m, v):
    m = ADAM_B1 * m + (1.0 - ADAM_B1) * g
    v = ADAM_B2 * v + (1.0 - ADAM_B2) * (g * g)
    m_hat = m / (1.0 - ADAM_B1 ** ADAM_STEP)
    v_hat = v / (1.0 - ADAM_B2 ** ADAM_STEP)
    return -ADAM_LR * (m_hat / (jnp.sqrt(v_hat) + ADAM_EPS) + ADAM_WD * w), m, v


def adamw_sharded(tensors, name, hosted=()):
    nt = len(tensors)
    nsh, R, C = tensors[0][0].shape
    lanes = -(-C // LANES) * LANES
    per_row = 2 * nt * lanes * (nsh * 2 + 7 * 4)
    tr = max(r for r in range(BF16_TILE_ROWS, R + 1, BF16_TILE_ROWS) if R % r == 0 and r * per_row <= ADAMW_VMEM_BUDGET)

    def body(*refs):
        ins, outs = refs[:4 * nt], refs[4 * nt:]
        for t in range(nt):
            p_ref, w_ref, m_ref, v_ref = ins[4 * t:4 * t + 4]
            g_ref, d_ref, nm_ref, nv_ref = outs[4 * t:4 * t + 4]
            g = p_ref[0].astype(F32)
            for i in range(1, nsh):
                g += p_ref[i].astype(F32)
            g_ref[...] = g
            d_ref[...], nm_ref[...], nv_ref[...] = _adamw(w_ref[...], g, m_ref[...], v_ref[...])

    spec = pl.BlockSpec((tr, C), lambda i: (i, 0))
    out = jax.ShapeDtypeStruct((R, C), F32)
    return _call(
        body, hosted, name=name, grid=(R // tr,),
        in_specs=[pl.BlockSpec((nsh, tr, C), lambda i: (0, i, 0)), spec, spec, spec] * nt,
        out_specs=[spec] * (4 * nt), out_shape=[out] * (4 * nt),
        args=[a for tensor in tensors for a in tensor])


def adamw_small(packs, late, pool, vectors, name):
    ndev = packs.shape[0]
    rp, rv, rl = pool[0].shape[0], vectors.shape[0] // 3, late.shape[1]

    def body(p_ref, l_ref, wp, mp, vp, wmv, gp, dp, nmp, nvp, gv, dv, nmv, nvv, loss_ref):
        wv, mv, vv = wmv.at[0:rv], wmv.at[rv:2 * rv], wmv.at[2 * rv:3 * rv]
        g, first = p_ref[0], l_ref[0]
        for i in range(1, ndev):
            g += p_ref[i]
            first += l_ref[i]
        g_pool = g[0:rp]
        g_vec = jnp.concatenate([g[rp:rp + rl] + first, g[rp + rl:rp + rv]], axis=0)
        gp[...], gv[...], loss_ref[...] = g_pool, g_vec, g[rp + rv:rp + rv + 8]
        dp[...], nmp[...], nvp[...] = _adamw(wp[...], g_pool, mp[...], vp[...])
        dv[...], nmv[...], nvv[...] = _adamw(wv[...], g_vec, mv[...], vv[...])

    shape = lambda rows: jax.ShapeDtypeStruct((rows, LANES), F32)
    outs = pl.pallas_call(body, name=name, out_shape=[shape(rp)] * 4 + [shape(rv)] * 4 + [shape(8)],
                          compiler_params=pltpu.CompilerParams(vmem_limit_bytes=VMEM_LIMIT_V7X))(packs, late, *pool, vectors)
    return outs[0:4], outs[4:8], outs[8]


BIG = ("ffn1_w1", "ffn1_w3", "ffn1_w2", "w_in", "w_out", "ffn2_w1", "ffn2_w3", "ffn2_w2")
TRANSPOSED = ("ffn1_w1", "ffn1_w3", "ffn2_w1", "ffn2_w3")
VECTORS = ("ffn1_norm", "mix_norm", "pool_scale", "ret_norm", "ffn2_norm", "final_norm")
WEIGHTS = ("ffn1_norm", "ffn1_w1", "ffn1_w3", "ffn1_w2", "mix_norm", "w_in", "pool_w", "pool_scale", "ret_norm", "w_out",
           "ffn2_norm", "ffn2_w1", "ffn2_w3", "ffn2_w2", "final_norm")


def _pack_vectors(parts):
    return jnp.concatenate([parts[k].reshape(-1, LANES) for k in VECTORS], axis=0)


def _unpack_vectors(pack, like):
    out, row = {}, 0
    for k in VECTORS:
        rows = like[k].size // LANES
        out[k] = pack[row:row + rows].reshape(like[k].shape)
        row += rows
    return out


def kernel(x, ffn1_norm, ffn1_w1, ffn1_w3, ffn1_w2, mix_norm, w_in, pool_w, pool_scale, ret_norm, w_out, ffn2_norm, ffn2_w1, ffn2_w3, ffn2_w2, final_norm, loss_target, m_ffn1_norm, m_ffn1_w1, m_ffn1_w3, m_ffn1_w2, m_mix_norm, m_w_in, m_pool_w, m_pool_scale, m_ret_norm, m_w_out, m_ffn2_norm, m_ffn2_w1, m_ffn2_w3, m_ffn2_w2, m_final_norm, v_ffn1_norm, v_ffn1_w1, v_ffn1_w3, v_ffn1_w2, v_mix_norm, v_w_in, v_pool_w, v_pool_scale, v_ret_norm, v_w_out, v_ffn2_norm, v_ffn2_w1, v_ffn2_w3, v_ffn2_w2, v_final_norm):
    w = dict(ffn1_norm=ffn1_norm, ffn1_w1=ffn1_w1, ffn1_w3=ffn1_w3, ffn1_w2=ffn1_w2, mix_norm=mix_norm, w_in=w_in, pool_w=pool_w,
             pool_scale=pool_scale, ret_norm=ret_norm, w_out=w_out, ffn2_norm=ffn2_norm, ffn2_w1=ffn2_w1, ffn2_w3=ffn2_w3,
             ffn2_w2=ffn2_w2, final_norm=final_norm)
    m = dict(ffn1_norm=m_ffn1_norm, ffn1_w1=m_ffn1_w1, ffn1_w3=m_ffn1_w3, ffn1_w2=m_ffn1_w2, mix_norm=m_mix_norm, w_in=m_w_in,
             pool_w=m_pool_w, pool_scale=m_pool_scale, ret_norm=m_ret_norm, w_out=m_w_out, ffn2_norm=m_ffn2_norm, ffn2_w1=m_ffn2_w1,
             ffn2_w3=m_ffn2_w3, ffn2_w2=m_ffn2_w2, final_norm=m_final_norm)
    v = dict(ffn1_norm=v_ffn1_norm, ffn1_w1=v_ffn1_w1, ffn1_w3=v_ffn1_w3, ffn1_w2=v_ffn1_w2, mix_norm=v_mix_norm, w_in=v_w_in,
             pool_w=v_pool_w, pool_scale=v_pool_scale, ret_norm=v_ret_norm, w_out=v_w_out, ffn2_norm=v_ffn2_norm, ffn2_w1=v_ffn2_w1,
             ffn2_w3=v_ffn2_w3, ffn2_w2=v_ffn2_w2, final_norm=v_final_norm)
    xs, target = x[0], loss_target[0]
    T = xs.shape[0]
    tables = _ret_tables(T)
    place = jnp.stack([lax.axis_index("c"), 2 * lax.axis_index("x") + lax.axis_index("y")]).astype(jnp.int32)
    local = lambda d, k: jnp.transpose(d[k][0]) if k in TRANSPOSED else d[k][0]
    result = lambda o, k: jnp.transpose(o)[None] if k in TRANSPOSED else o[None]
    first = ("ffn1_w1", "ffn1_w3")
    sh = {k: local(w, k).astype(BF16) for k in first}
    gather = lambda *names: [ChipExchange([sh[k] for k in names], False)]
    wg, grad, delta, new_m, new_v = {}, {}, {}, {}, {}

    def update(names, pieces, name, hosted=()):
        outs, extras = adamw_sharded([(p, local(w, k), local(m, k), local(v, k)) for k, p in zip(names, pieces)], name, hosted)
        for t, k in enumerate(names):
            grad[k], delta[k], new_m[k], new_v[k] = [result(o, k) for o in outs[4 * t:4 * t + 4]]
        return extras

    def reduce_in_chip(name, *pairs):
        reduced = prereduce([p for p, _ in pairs], [r for _, r in pairs], place, "prereduce_" + name)
        return reduced[0] if len(pairs) == 1 else reduced

    scatter = lambda *reduced: ChipExchange([r[0] for r in reduced], True, [r[1] for r in reduced])
    whole = lambda k: wg[k].reshape(-1, wg[k].shape[-1])
    sharded = lambda g: g.reshape(N_CHIPS, -1, g.shape[-1])

    later = [k for k in BIG if k not in first]
    casts, ((wg["ffn1_w1"], wg["ffn1_w3"]),) = cast_shards([local(w, k) for k in later], "cast_gather_ffn1", gather(*first))
    sh.update(zip(later, casts))
    (n1, ga1, gb1, s1), ((wg["ffn1_w2"], wg["w_in"]),) = ffn_up(
        xs, ffn1_norm, whole("ffn1_w1"), whole("ffn1_w3"), "ffn1_up", gather("ffn1_w2", "w_in"))
    (h1, u, proj), ((wg["w_out"], wg["ffn2_w1"]),) = ffn_down_mix_in(
        s1, whole("ffn1_w2"), xs, mix_norm, wg["w_in"], "ffn1_down_mix_in", gather("w_out", "ffn2_w1"))
    (pa,), _ = pool_fwd(proj, pool_w[0], pool_scale, "pool_fwd")
    (rb, o_pre, r_prev), ((wg["ffn2_w3"],),) = ret_fwd(proj, ret_norm, tables, "ret_fwd", gather("ffn2_w3"))
    (h2, n2, ga2, gb2, s2), ((wg["ffn2_w2"],),) = ffn_up(
        h1, ffn2_norm, whole("ffn2_w1"), whole("ffn2_w3"), "mix_out_ffn2_up", gather("ffn2_w2"), mixed=(pa, rb, wg["w_out"]))
    (dh3, loss, d_final), _ = ffn_down_loss(s2, whole("ffn2_w2"), h2, final_norm[None], target, "ffn2_down_loss")

    (da2, db2, df2), _ = ffn_bwd_act(dh3, whole("ffn2_w2"), ga2, gb2, "ffn2_bwd_act")
    (g_f2w2,), _ = ffn_dw([s2], df2, 1, "ffn2_dw2")
    g_f2w2 = sharded(g_f2w2)
    (g_f2w1, g_f2w3), ((r_f2w2,),) = ffn_dw([da2, db2], n2, 2, "ffn2_dw13", [SiblingExchange([g_f2w2])])
    g_f2w1, g_f2w3 = sharded(g_f2w1), sharded(g_f2w3)
    p_f2w2 = reduce_in_chip("ffn2_w2", (g_f2w2, r_f2w2))
    (dh2, d_ffn2), ((q_f2w2,), (r_f2w1, r_f2w3)) = ffn_bwd_in(
        da2, db2, whole("ffn2_w1"), whole("ffn2_w3"), h2, ffn2_norm, dh3, "ffn2_bwd_in",
        [scatter(p_f2w2), SiblingExchange([g_f2w1, g_f2w3])])
    p_f2w1, p_f2w3 = reduce_in_chip("ffn2_w13", (g_f2w1, r_f2w1), (g_f2w3, r_f2w3))
    (dpa, drb, g_wout), _ = mix_out_bwd(dh2, wg["w_out"], pa, rb, "mix_out_bwd")
    (dproj, d_pool_w, d_pool_scale), _ = pool_bwd(proj, dpa, pool_w[0], pool_scale, "pool_bwd")
    (dproj, d_ret_norm), ((q_f2w1, q_f2w3), (r_wout,)) = ret_bwd(
        proj, drb, o_pre, r_prev, ret_norm, tables, dproj, "ret_bwd", [scatter(p_f2w1, p_f2w3), SiblingExchange([g_wout])])
    p_wout = reduce_in_chip("w_out", (g_wout, r_wout))
    (g_win,), ((q_wout,),) = mix_dwin(u, dproj, N_CHIPS, "mix_dwin", [scatter(p_wout)])
    (dh1, d_mix), ((r_win,),) = mix_in_bwd(dproj, wg["w_in"], h1, mix_norm, dh2, "mix_in_bwd", [SiblingExchange([g_win])])
    p_win = reduce_in_chip("w_in", (g_win, r_win))
    (da1, db1, df1), ((q_win,),) = ffn_bwd_act(dh1, whole("ffn1_w2"), ga1, gb1, "ffn1_bwd_act", [scatter(p_win)])
    d_vectors = {"ffn1_norm": jnp.zeros_like(ffn1_norm), "mix_norm": d_mix, "pool_scale": d_pool_scale,
                 "ret_norm": d_ret_norm, "ffn2_norm": d_ffn2, "final_norm": d_final}
    pack = jnp.concatenate([d_pool_w.reshape(-1, LANES), _pack_vectors(d_vectors), jnp.broadcast_to(loss, (8, LANES))], axis=0)
    (g_f1w1, g_f1w3), ((packs,),) = ffn_dw([da1, db1], n1, 2, "ffn1_dw13", [AllExchange(pack)])
    g_f1w1, g_f1w3 = sharded(g_f1w1), sharded(g_f1w3)
    (g_f1w2,), ((r_f1w1, r_f1w3),) = ffn_dw([s1], df1, 1, "ffn1_dw2", [SiblingExchange([g_f1w1, g_f1w3])])
    g_f1w2 = sharded(g_f1w2)
    p_f1w1, p_f1w3 = reduce_in_chip("ffn1_w13", (g_f1w1, r_f1w1), (g_f1w3, r_f1w3))
    (dx, d_ffn1), ((q_f1w1, q_f1w3), (r_f1w2,)) = ffn_bwd_in(
        da1, db1, whole("ffn1_w1"), whole("ffn1_w3"), xs, ffn1_norm, dh1, "ffn1_bwd_in",
        [scatter(p_f1w1, p_f1w3), SiblingExchange([g_f1w2])])
    p_f1w2 = reduce_in_chip("ffn1_w2", (g_f1w2, r_f1w2))

    (q_f1w2,), (late,) = update(["ffn2_w1", "ffn2_w3", "ffn1_w1", "ffn1_w3"], [q_f2w1, q_f2w3, q_f1w1, q_f1w3], "adamw_w13",
                                [scatter(p_f1w2), AllExchange(d_ffn1.reshape(-1, LANES))])
    update(["ffn2_w2", "ffn1_w2"], [q_f2w2, q_f1w2], "adamw_w2")
    update(["w_in"], [q_win], "adamw_w_in")
    update(["w_out"], [q_wout], "adamw_w_out")
    of_pool, of_vectors, loss_sum = adamw_small(packs, late, [t["pool_w"].reshape(-1, LANES) for t in (w, m, v)],
                                                jnp.concatenate([t[k].reshape(-1, LANES) for t in (w, m, v) for k in VECTORS], axis=0),
                                                "adamw_small")
    for res, pool_part, vector_part in zip((grad, delta, new_m, new_v), of_pool, of_vectors):
        res["pool_w"] = pool_part.reshape(pool_w.shape)
        res.update(_unpack_vectors(vector_part, w))
    loss = loss_sum[0, 0]

    return (loss, dx[None], *[grad[k] for k in WEIGHTS], *[delta[k] for k in WEIGHTS],
            *[new_m[k] for k in WEIGHTS], *[new_v[k] for k in WEIGHTS])
```

```python
import math

import jax
import jax.numpy as jnp
from jax import lax
from jax.experimental import pallas as pl
from jax.experimental.pallas import tpu as pltpu

F32 = jnp.float32
BF16 = jnp.bfloat16

EPS = 1e-6
LANES = 128
BF16_TILE_ROWS = 16
N_CHIPS = 4
N_GROUPS = 4
HEAD_DIM = 128
RET_CHUNK = 128
ROPE_BASE = 10000.0
ADAM_LR, ADAM_B1, ADAM_B2, ADAM_EPS, ADAM_WD, ADAM_STEP = 0.001, 0.9, 0.999, 1e-08, 0.01, 10
VMEM_LIMIT_V7X = 56 * 1024 * 1024
ADAMW_VMEM_BUDGET = 32 * 1024 * 1024
MESH = pl.DeviceIdType.MESH
ANY = pl.BlockSpec(memory_space=pl.ANY)


def _dot(a, b):
    return jnp.dot(a, b, preferred_element_type=F32)


def _dot_nt(a, b):
    return lax.dot_general(a, b, (((1,), (1,)), ((), ())), preferred_element_type=F32)


def _dot_tn(a, b):
    return lax.dot_general(a, b, (((0,), (0,)), ((), ())), preferred_element_type=F32)


def _rstd(h):
    return lax.rsqrt(jnp.mean(h * h, axis=-1, keepdims=True) + EPS)


def _rmsnorm_bwd(dn, h, gain):
    r = _rstd(h)
    nh = h * r
    dnh = dn * gain
    dh = r * (dnh - nh * jnp.mean(dnh * nh, axis=-1, keepdims=True))
    return dh, dn * nh


def _silu_parts(a):
    sig = jax.nn.sigmoid(a)
    silu = a * sig
    return silu, sig + silu * (1.0 - sig)


def _mesh_pos():
    return lax.axis_index("x"), lax.axis_index("y"), lax.axis_index("c")


class ChipExchange:
    def __init__(self, srcs, scatter, placed=()):
        n = len(srcs)
        self.inputs, self.scatter, self.n, self.reach = list(srcs) + list(placed), scatter, n, REACH_CHIPS
        self.aliases = {n + t: t for t in range(n)} if scatter else {}
        self.half_rows = [s.shape[1] if scatter else s.shape[0] // 2 for s in srcs]
        self.out_shape = [jax.ShapeDtypeStruct((N_CHIPS, 2 * rh, s.shape[-1]), s.dtype) for s, rh in zip(srcs, self.half_rows)]
        if scatter:
            self.out_shape += [jax.ShapeDtypeStruct((2, rh // 2, s.shape[-1]), s.dtype) for s, rh in zip(srcs, self.half_rows)]
        dma = pltpu.SemaphoreType.DMA
        self.sems = [dma((4 * n,)), dma((4 * n,)), dma((2 * n,)), dma((2 * n,)), dma((4 * n,)), dma((4 * n,))]

    def _copies(self, src, out, sems):
        hop1_send, hop1_recv, hop2_send, hop2_recv, d2d_send, d2d_recv = sems
        x, y, c = _mesh_pos()
        me, dg = 2 * x + y, 2 * (1 - x) + (1 - y)
        sibling = (x, y, 1 - c)
        n = self.n
        mine, theirs = c, 1 - c

        def nb(a):
            nx, ny = x ^ (1 - a), y ^ a
            return 2 * nx + ny, (nx, ny, c)

        def remote(s, d, send, recv, k, to):
            return pltpu.make_async_remote_copy(src_ref=s, dst_ref=d, send_sem=send.at[k], recv_sem=recv.at[k],
                                                device_id=to, device_id_type=MESH)

        class Copies:
            def slot(_, t, chip, half):
                rh = self.half_rows[t]
                return out[t].at[chip, pl.ds(half * rh, rh), :]

            def quarter(_, t, chip, q):
                qh = self.half_rows[t] // 2
                return out[t].at[chip, pl.ds(mine * 2 * qh + q * qh, qh), :]

            def own_shard(k, t):
                return remote(src[t], out[t].at[me], d2d_send, d2d_recv, 4 * t + 3, sibling)

            def hop1(k, t, a, transit=False):
                rh = self.half_rows[t]
                chip, to = nb(a)
                if transit:
                    piece = src[t].at[dg, pl.ds(a * (rh // 2), rh // 2), :]
                    return remote(piece, out[n + t].at[a], hop1_send, hop1_recv, 4 * t + 2 + a, to)
                piece = src[t].at[chip] if self.scatter else src[t].at[pl.ds(mine * rh, rh), :]
                return remote(piece, k.slot(t, me, mine), hop1_send, hop1_recv, 4 * t + a, to)

            def landed1(k, t, a, transit=False):
                here = out[n + t].at[a] if transit else k.slot(t, nb(a)[0], mine)
                return remote(here, here, hop1_send, hop1_recv, 4 * t + (2 if transit else 0) + a, sibling)

            def hop2(k, t, q):
                origin, to = nb(q)[0], nb(1 - q)[1]
                piece = out[n + t].at[q] if self.scatter else k.quarter(t, origin, q)
                return remote(piece, k.quarter(t, origin, q), hop2_send, hop2_recv, 2 * t + q, to)

            def landed2(k, t, q):
                here = k.quarter(t, dg, q)
                return remote(here, here, hop2_send, hop2_recv, 2 * t + q, sibling)

            def d2d(k, t, p, chip, own=False, arriving=False):
                if arriving:
                    there = k.slot(t, chip, theirs)
                    return remote(there, there, d2d_send, d2d_recv, 4 * t + p, sibling)
                piece = src[t].at[me] if own else k.slot(t, chip, mine)
                return remote(piece, k.slot(t, chip, mine), d2d_send, d2d_recv, 4 * t + p, sibling)

        return Copies(), nb, me, dg, c

    def start(self, src, out, sems):
        k, nb, me, dg, c = self._copies(src, out, sems)
        for t in range(self.n):
            for first in range(2):
                a = first ^ c
                k.hop1(t, a).start()
                if self.scatter:
                    k.hop1(t, a, transit=True).start()
            if self.scatter:
                k.d2d(t, 3, me, own=True).start()
            else:
                k.own_shard(t).start()

    def mid(self, src, out, sems):
        k, nb, me, dg, c = self._copies(src, out, sems)
        for t in range(self.n):
            for first in range(2):
                a = first ^ c
                if self.scatter:
                    k.landed1(t, a, transit=True).wait_recv()
                    k.hop2(t, a).start()
                k.landed1(t, a).wait_recv()
                if not self.scatter:
                    k.hop2(t, a).start()
                k.d2d(t, a, nb(a)[0]).start()

    def finish(self, src, out, sems):
        k, nb, me, dg, c = self._copies(src, out, sems)
        for t in range(self.n):
            for q in range(2):
                k.landed2(t, q).wait_recv()
            k.d2d(t, 2, dg).start()
        for t in range(self.n):
            for a in range(2):
                k.d2d(t, a, nb(a)[0], arriving=True).wait_recv()
            k.d2d(t, 2, dg, arriving=True).wait_recv()
            if self.scatter:
                k.d2d(t, 3, me, arriving=True).wait_recv()
        for t in range(self.n):
            for a in range(2):
                k.hop1(t, a).wait_send()
                if self.scatter:
                    k.hop1(t, a, transit=True).wait_send()
                k.hop2(t, a).wait_send()
                k.d2d(t, a, nb(a)[0]).wait_send()
            k.d2d(t, 2, dg).wait_send()
            if self.scatter:
                k.d2d(t, 3, me, own=True).wait_send()
            else:
                k.own_shard(t).wait()


class SiblingExchange:
    def __init__(self, grads):
        self.inputs, self.n, self.aliases, self.reach = list(grads), len(grads), {}, REACH_SIBLING
        self.half_rows = [g.shape[1] // 2 for g in grads]
        self.out_shape = [jax.ShapeDtypeStruct((g.shape[0], rh, g.shape[2]), g.dtype) for g, rh in zip(grads, self.half_rows)]
        self.sems = [pltpu.SemaphoreType.DMA((self.n,)), pltpu.SemaphoreType.DMA((self.n,))]

    def _plan(self, src, out, sems):
        x, y, c = _mesh_pos()
        return [pltpu.make_async_remote_copy(
            src_ref=src[t].at[:, pl.ds((1 - c) * self.half_rows[t], self.half_rows[t]), :], dst_ref=out[t],
            send_sem=sems[0].at[t], recv_sem=sems[1].at[t], device_id=(x, y, 1 - c), device_id_type=MESH) for t in range(self.n)]

    def start(self, src, out, sems):
        for cp in self._plan(src, out, sems):
            cp.start()

    def mid(self, src, out, sems):
        pass

    def finish(self, src, out, sems):
        for cp in self._plan(src, out, sems):
            cp.wait()


REACH_SIBLING, REACH_CHIPS, REACH_ALL = 0, 1, 2


def _entry_barrier(reach):
    x, y, c = _mesh_pos()
    peers = [(x, y, 1 - c)]
    if reach == REACH_CHIPS:
        peers += [(1 - x, y, c), (x, 1 - y, c)]
    elif reach == REACH_ALL:
        peers = [(x ^ dx, y ^ dy, c ^ dc) for dx in (0, 1) for dy in (0, 1) for dc in (0, 1)][1:]
    barrier = pltpu.get_barrier_semaphore()
    for peer in peers:
        pl.semaphore_signal(barrier, inc=1, device_id=peer, device_id_type=MESH)
    pl.semaphore_wait(barrier, len(peers))


def _call(body, hosted=(), *, name, in_specs, out_specs, out_shape, args, grid=(), scratch_shapes=(), aliased=None):
    n_in, n_out, n_scr = len(in_specs), len(out_specs), len(scratch_shapes)
    total = math.prod(grid)
    mid_step = max(0, (5 * total) // 8 - 1)

    def full(*refs):
        pos = [0]

        def take(k):
            pos[0] += k
            return refs[pos[0] - k:pos[0]]

        ins, h_in = take(n_in), [take(len(h.inputs)) for h in hosted]
        outs, h_out = take(n_out), [take(len(h.out_shape)) for h in hosted]
        scr, h_sem = take(n_scr), [take(len(h.sems)) for h in hosted]
        step = 0
        for axis, size in enumerate(grid):
            step = step * size + pl.program_id(axis)

        def phase(at, method):
            if not hosted:
                return

            def run():
                if method == "start":
                    _entry_barrier(reach)
                for h, s, o, m in zip(hosted, h_in, h_out, h_sem):
                    getattr(h, method)(s, o, m)

            if total == 1:
                run()
            else:
                pl.when(step == at)(run)

        phase(0, "start")
        body(*ins, *outs, *scr)
        phase(mid_step, "mid")
        phase(total - 1, "finish")

    aliases, i0, o0 = dict(aliased or {}), n_in, n_out
    for h in hosted:
        aliases.update({i0 + i: o0 + o for i, o in h.aliases.items()})
        i0, o0 = i0 + len(h.inputs), o0 + len(h.out_shape)
    reach = max((h.reach for h in hosted), default=None)
    params = dict(vmem_limit_bytes=VMEM_LIMIT_V7X)
    if hosted:
        params["collective_id"] = reach
    results = pl.pallas_call(
        full, name=name, grid=grid,
        in_specs=list(in_specs) + [ANY] * (i0 - n_in),
        out_specs=list(out_specs) + [ANY] * (o0 - n_out),
        out_shape=list(out_shape) + [s for h in hosted for s in h.out_shape],
        scratch_shapes=list(scratch_shapes) + [s for h in hosted for s in h.sems],
        input_output_aliases=aliases,
        compiler_params=pltpu.CompilerParams(**params),
    )(*args, *[s for h in hosted for s in h.inputs])
    outs, extras, pos = list(results[:n_out]), [], n_out
    for h in hosted:
        extras.append(list(results[pos:pos + h.n]))
        pos += len(h.out_shape)
    return outs, extras


def cast_shards(shards, name, hosted=()):
    n = len(shards)

    def body(*refs):
        for x_ref, o_ref in zip(refs[:n], refs[n:]):
            o_ref[...] = x_ref[...].astype(BF16)

    whole = lambda s: pl.BlockSpec(s.shape, lambda: (0,) * s.ndim)
    return _call(body, hosted, name=name, in_specs=[whole(s) for s in shards], out_specs=[whole(s) for s in shards],
                 out_shape=[jax.ShapeDtypeStruct(s.shape, BF16) for s in shards], args=list(shards))


class AllExchange:
    def __init__(self, pack):
        self.inputs, self.n, self.aliases, self.reach = [pack], 1, {}, REACH_ALL
        self.out_shape = [jax.ShapeDtypeStruct((2 * N_CHIPS,) + pack.shape, pack.dtype)]
        self.sems = [pltpu.SemaphoreType.DMA, pltpu.SemaphoreType.DMA((7,)), pltpu.SemaphoreType.DMA((7,))]

    def _copies(self, src, out, sems):
        local_sem, send_sem, recv_sem = sems
        x, y, c = _mesh_pos()
        flips = [(dx, dy, dc) for dx in (0, 1) for dy in (0, 1) for dc in (0, 1)][1:]
        peers = [(x ^ dx, y ^ dy, c ^ dc) for dx, dy, dc in flips]
        remote = lambda s, d, k: pltpu.make_async_remote_copy(
            src_ref=s, dst_ref=d, send_sem=send_sem.at[k], recv_sem=recv_sem.at[k], device_id=peers[k], device_id_type=MESH)
        sends = [remote(src[0], out[0].at[4 * x + 2 * y + c], k) for k in range(7)]
        landed = [remote(out[0].at[4 * px + 2 * py + pc], out[0].at[4 * px + 2 * py + pc], k) for k, (px, py, pc) in enumerate(peers)]
        return sends, landed, pltpu.make_async_copy(src[0], out[0].at[4 * x + 2 * y + c], local_sem)

    def start(self, src, out, sems):
        sends, _, local = self._copies(src, out, sems)
        for cp in sends:
            cp.start()
        local.start()

    def mid(self, src, out, sems):
        pass

    def finish(self, src, out, sems):
        sends, landed, local = self._copies(src, out, sems)
        for cp in landed:
            cp.wait_recv()
        for cp in sends:
            cp.wait_send()
        local.wait()


MXU_COLS = 256


def _resident(shape):
    return pl.BlockSpec(shape, lambda *_: (0,) * len(shape), pipeline_mode=pl.Buffered(1))


def ffn_up(h, gain, w1, w3, name, hosted=(), mixed=None):
    T, D = h.shape
    F = w1.shape[0]
    tm = min(T, 512)

    def body(*refs):
        if mixed is None:
            h_ref, g_ref, w1_ref, w3_ref, n_ref, ga_ref, gb_ref, s_ref = refs
            hh = h_ref[...]
        else:
            pa_ref, rb_ref, wo_ref, h_ref, g_ref, w1_ref, w3_ref, hh_ref, n_ref, ga_ref, gb_ref, s_ref = refs
            hh = h_ref[...] + _dot(pa_ref[...], wo_ref[0]) + _dot(rb_ref[...], wo_ref[1])
            hh_ref[...] = hh
        n = (hh * _rstd(hh) * g_ref[...]).astype(BF16)
        n_ref[...] = n
        for c in range(0, F, MXU_COLS):
            cols = slice(c, c + MXU_COLS)
            a = _dot_nt(n, w1_ref[cols, :])
            b = _dot_nt(n, w3_ref[cols, :])
            silu, dsilu = _silu_parts(a)
            ga_ref[:, cols] = (b * dsilu).astype(BF16)
            gb_ref[:, cols] = silu.astype(BF16)
            s_ref[:, cols] = (silu * b).astype(BF16)

    act = jax.ShapeDtypeStruct((T, F), BF16)
    act_spec = pl.BlockSpec((tm, F), lambda i: (i, 0))
    row_spec = pl.BlockSpec((tm, D), lambda i: (i, 0))
    in_specs = [row_spec, pl.BlockSpec((1, D), lambda i: (0, 0)), _resident((F, D)), _resident((F, D))]
    out_specs, out_shape, args = [row_spec, act_spec, act_spec, act_spec], [jax.ShapeDtypeStruct((T, D), BF16), act, act, act], [h, gain, w1, w3]
    if mixed is not None:
        pa, rb, woutg = mixed
        W = pa.shape[1]
        in_specs = [pl.BlockSpec((tm, W), lambda i: (i, 0))] * 2 + [_resident((2, W, D))] + in_specs
        out_specs, out_shape = [row_spec] + out_specs, [jax.ShapeDtypeStruct((T, D), F32)] + out_shape
        args = [pa, rb, woutg.reshape(2, W, D)] + args
    return _call(body, hosted, name=name, grid=(T // tm,), in_specs=in_specs, out_specs=out_specs, out_shape=out_shape, args=args)


def ffn_bwd_act(dh, w2, ga, gb, name, hosted=()):
    T, D = dh.shape
    F = w2.shape[0]
    tm = min(T, 512)

    def body(dh_ref, w2_ref, ga_ref, gb_ref, da_ref, db_ref, df_ref):
        df = (0.5 * dh_ref[...]).astype(BF16)
        df_ref[...] = df
        for c in range(0, F, MXU_COLS):
            cols = slice(c, c + MXU_COLS)
            ds = _dot_nt(df, w2_ref[cols, :])
            da_ref[:, cols] = (ds * ga_ref[:, cols].astype(F32)).astype(BF16)
            db_ref[:, cols] = (ds * gb_ref[:, cols].astype(F32)).astype(BF16)

    act = jax.ShapeDtypeStruct((T, F), BF16)
    act_spec = pl.BlockSpec((tm, F), lambda i: (i, 0))
    row_spec = pl.BlockSpec((tm, D), lambda i: (i, 0))
    return _call(
        body, hosted, name=name, grid=(T // tm,),
        in_specs=[row_spec, _resident((F, D)), act_spec, act_spec],
        out_specs=[act_spec, act_spec, row_spec],
        out_shape=[act, act, jax.ShapeDtypeStruct((T, D), BF16)],
        args=[dh, w2, ga, gb])


def ffn_dw(xs, y, halves, name, hosted=()):
    T, F = xs[0].shape
    D = y.shape[1]
    nx, fh = len(xs), F // halves
    tk = min(T, 512)
    nk = T // tk

    def body(*refs):
        y_ref, x_refs, o_refs, accs = refs[0], refs[1:1 + nx], refs[1 + nx:1 + 2 * nx], refs[1 + 2 * nx:]
        k = pl.program_id(1)

        @pl.when(k == 0)
        def _():
            for acc in accs:
                acc[...] = jnp.zeros_like(acc)

        yy = y_ref[...]
        for x_ref, acc in zip(x_refs, accs):
            acc[...] += _dot_tn(x_ref[...], yy)

        @pl.when(k == nk - 1)
        def _():
            for o_ref, acc in zip(o_refs, accs):
                o_ref[...] = acc[...].astype(BF16)

    out = jax.ShapeDtypeStruct((F, D), BF16)
    return _call(
        body, hosted, name=name, grid=(halves, nk),
        in_specs=[pl.BlockSpec((tk, D), lambda j, k: (k, 0))] + [pl.BlockSpec((tk, fh), lambda j, k: (k, j))] * nx,
        out_specs=[pl.BlockSpec((fh, D), lambda j, k: (j, 0))] * nx,
        out_shape=[out] * nx,
        scratch_shapes=[pltpu.VMEM((fh, D), F32)] * nx,
        args=[y] + list(xs))


def ffn_bwd_in(da, db, w1, w3, h, gain, dh, name, hosted=()):
    T, F = da.shape
    D = h.shape[1]
    tm = min(T, 512)

    def body(da_ref, db_ref, w1_ref, w3_ref, h_ref, g_ref, dh_ref, o_ref, dg_ref):
        dn = _dot(da_ref[...], w1_ref[...]) + _dot(db_ref[...], w3_ref[...])
        dhn, dg = _rmsnorm_bwd(dn, h_ref[...], g_ref[...])
        o_ref[...] = dh_ref[...] + dhn

        @pl.when(pl.program_id(0) == 0)
        def _():
            dg_ref[...] = jnp.zeros_like(dg_ref)

        dg_ref[...] += jnp.sum(dg, axis=0, keepdims=True)

    act_spec = pl.BlockSpec((tm, F), lambda i: (i, 0))
    row_spec = pl.BlockSpec((tm, D), lambda i: (i, 0))
    vec_spec = pl.BlockSpec((1, D), lambda i: (0, 0))
    return _call(
        body, hosted, name=name, grid=(T // tm,),
        in_specs=[act_spec, act_spec, _resident((F, D)), _resident((F, D)), row_spec, vec_spec, row_spec],
        out_specs=[row_spec, vec_spec],
        out_shape=[jax.ShapeDtypeStruct((T, D), F32), jax.ShapeDtypeStruct((1, D), F32)],
        args=[da, db, w1, w3, h, gain, dh])


def ffn_down_mix_in(s, w2, h, gain, wing, name, hosted=()):
    T, F = s.shape
    D = h.shape[1]
    nsh, _, Cs = wing.shape
    tm = min(T, 512)

    def body(s_ref, w2_ref, h_ref, g_ref, w_ref, hh_ref, u_ref, p_ref):
        hh = h_ref[...] + 0.5 * _dot(s_ref[...], w2_ref[...])
        hh_ref[...] = hh
        u = (hh * _rstd(hh) * g_ref[...]).astype(BF16)
        u_ref[...] = u
        for j in range(nsh):
            p_ref[:, j * Cs:(j + 1) * Cs] = _dot(u, w_ref[j])

    row_spec = pl.BlockSpec((tm, D), lambda i: (i, 0))
    return _call(
        body, hosted, name=name, grid=(T // tm,),
        in_specs=[pl.BlockSpec((tm, F), lambda i: (i, 0)), _resident((F, D)), row_spec, pl.BlockSpec((1, D), lambda i: (0, 0)),
                  _resident((nsh, D, Cs))],
        out_specs=[row_spec, row_spec, pl.BlockSpec((tm, nsh * Cs), lambda i: (i, 0))],
        out_shape=[jax.ShapeDtypeStruct((T, D), F32), jax.ShapeDtypeStruct((T, D), BF16), jax.ShapeDtypeStruct((T, nsh * Cs), F32)],
        args=[s, w2, h, gain, wing])


def mix_out_bwd(dh, woutg, a, b, name, hosted=()):
    T, D = dh.shape
    W = a.shape[1]
    nsh, Rs, _ = woutg.shape
    wout = woutg.reshape(2, W, D)
    tk = min(T, 512)
    nk = T // tk

    def body(dh_ref, w_ref, a_ref, b_ref, da_ref, db_ref, dw_ref, acc):
        k = pl.program_id(0)

        @pl.when(k == 0)
        def _():
            acc[...] = jnp.zeros_like(acc)

        dhb = dh_ref[...].astype(BF16)
        da_ref[...] = _dot_nt(dhb, w_ref[0])
        db_ref[...] = _dot_nt(dhb, w_ref[1])
        acc[0:W, :] += _dot_tn(a_ref[...], dhb)
        acc[W:2 * W, :] += _dot_tn(b_ref[...], dhb)

        @pl.when(k == nk - 1)
        def _():
            for j in range(nsh):
                dw_ref[j] = acc[j * Rs:(j + 1) * Rs, :].astype(BF16)

    return _call(
        body, hosted, name=name, grid=(nk,),
        in_specs=[pl.BlockSpec((tk, D), lambda k: (k, 0)), pl.BlockSpec((2, W, D), lambda k: (0, 0, 0)),
                  pl.BlockSpec((tk, W), lambda k: (k, 0)), pl.BlockSpec((tk, W), lambda k: (k, 0))],
        out_specs=[pl.BlockSpec((tk, W), lambda k: (k, 0)), pl.BlockSpec((tk, W), lambda k: (k, 0)),
                   pl.BlockSpec((nsh, Rs, D), lambda k: (0, 0, 0))],
        out_shape=[jax.ShapeDtypeStruct((T, W), F32), jax.ShapeDtypeStruct((T, W), F32),
                   jax.ShapeDtypeStruct((nsh, Rs, D), BF16)],
        scratch_shapes=[pltpu.VMEM((2 * W, D), F32)],
        args=[dh, wout, a, b])


def _dproj_block(g):
    return (g // N_GROUPS + N_GROUPS) % (N_GROUPS + 1), g % N_GROUPS


def mix_dwin(u, dproj, nsh, name, hosted=()):
    T, D = u.shape
    Hd = HEAD_DIM
    slabs, _, width = dproj.shape
    blocks = slabs * width // Hd
    Cs = blocks * Hd // nsh
    tk = min(T, 512)
    nk = T // tk

    def body(u_ref, d_ref, o_ref, acc):
        k = pl.program_id(0)

        @pl.when(k == 0)
        def _():
            acc[...] = jnp.zeros_like(acc)

        where = [_dproj_block(g) for g in range(blocks)]
        d = jnp.concatenate([d_ref[slab, :, col * Hd:(col + 1) * Hd] for slab, col in where], axis=1)
        acc[...] += _dot_tn(u_ref[...], d)

        @pl.when(k == nk - 1)
        def _():
            for j in range(nsh):
                o_ref[j] = acc[:, j * Cs:(j + 1) * Cs].astype(BF16)

    return _call(
        body, hosted, name=name, grid=(nk,),
        in_specs=[pl.BlockSpec((tk, D), lambda k: (k, 0)), pl.BlockSpec((slabs, tk, width), lambda k: (0, k, 0))],
        out_specs=[pl.BlockSpec((nsh, D, Cs), lambda k: (0, 0, 0))],
        out_shape=[jax.ShapeDtypeStruct((nsh, D, Cs), BF16)],
        scratch_shapes=[pltpu.VMEM((D, blocks * Hd), F32)],
        args=[u, dproj])


def mix_in_bwd(dproj, wing, h, gain, dh, name, hosted=()):
    T, D = h.shape
    nsh, _, Cs = wing.shape
    Hd = HEAD_DIM
    per = Cs // Hd
    tm = min(T, 512)

    def body(d_ref, w_ref, h_ref, g_ref, dh_ref, o_ref, dg_ref):
        def shard(j):
            blocks = [_dproj_block(per * j + i) for i in range(per)]
            return jnp.concatenate([d_ref[slab, :, col * Hd:(col + 1) * Hd] for slab, col in blocks], axis=1)

        du = _dot_nt(shard(0), w_ref[0])
        for j in range(1, nsh):
            du += _dot_nt(shard(j), w_ref[j])
        dhn, dg = _rmsnorm_bwd(du, h_ref[...], g_ref[...])
        o_ref[...] = dh_ref[...] + dhn

        @pl.when(pl.program_id(0) == 0)
        def _():
            dg_ref[...] = jnp.zeros_like(dg_ref)

        dg_ref[...] += jnp.sum(dg, axis=0, keepdims=True)

    row_spec = pl.BlockSpec((tm, D), lambda i: (i, 0))
    vec_spec = pl.BlockSpec((1, D), lambda i: (0, 0))
    return _call(
        body, hosted, name=name, grid=(T // tm,),
        in_specs=[pl.BlockSpec((dproj.shape[0], tm, dproj.shape[2]), lambda i: (0, i, 0)),
                  pl.BlockSpec((nsh, D, Cs), lambda i: (0, 0, 0)), row_spec, vec_spec, row_spec],
        out_specs=[row_spec, vec_spec],
        out_shape=[jax.ShapeDtypeStruct((T, D), F32), jax.ShapeDtypeStruct((1, D), F32)],
        args=[dproj, wing, h, gain, dh])


def _pool_window(x, group, T, trailing):
    rows = lax.broadcasted_iota(jnp.int32, x.shape, 0)

    def shifted(z, k):
        if trailing:
            return jnp.where(rows >= k, pltpu.roll(z, k, 0), 0.0)
        return jnp.where(rows < T - k, pltpu.roll(z, T - k, 0), 0.0)

    s2 = x + shifted(x, 1)
    s4 = s2 + shifted(s2, 2)
    s8 = s4 + shifted(s4, 4)
    s16 = s8 + shifted(s8, 8)
    return jnp.where(group == 0, s2, jnp.where(group == 1, s4, jnp.where(group == 2, s8, s16)))


def _pool_count(group, shape):
    rows = lax.broadcasted_iota(jnp.int32, shape, 0)
    w = jnp.where(group == 0, 2, jnp.where(group == 1, 4, jnp.where(group == 2, 8, 16)))
    return jnp.minimum(rows + 1, w).astype(F32)


def pool_fwd(proj, pool_w, pool_scale, name, hosted=()):
    T = proj.shape[0]
    Hd = HEAD_DIM

    def body(x_ref, w_ref, sc_ref, a_ref):
        g = pl.program_id(0)
        x = x_ref[...]
        pooled = _pool_window(x, g, T, True) / _pool_count(g, x.shape) - x
        a_ref[...] = (_dot(pooled.astype(BF16), w_ref[0].astype(BF16)) * sc_ref[...]).astype(BF16)

    return _call(
        body, hosted, name=name, grid=(N_GROUPS,),
        in_specs=[pl.BlockSpec((T, Hd), lambda g: (0, g)), pl.BlockSpec((1, Hd, Hd), lambda g: (g, 0, 0)),
                  pl.BlockSpec((1, Hd), lambda g: (0, g))],
        out_specs=[pl.BlockSpec((T, Hd), lambda g: (0, g))],
        out_shape=[jax.ShapeDtypeStruct((T, N_GROUPS * Hd), BF16)],
        args=[proj, pool_w, pool_scale])


def pool_bwd(proj, da, pool_w, pool_scale, name, hosted=()):
    T = proj.shape[0]
    Hd = HEAD_DIM

    def body(x_ref, da_ref, w_ref, sc_ref, dx_ref, dw_ref, dsc_ref):
        g = pl.program_id(0)
        x = x_ref[...]
        cnt = _pool_count(g, x.shape)
        pooled = (_pool_window(x, g, T, True) / cnt - x).astype(BF16)
        wb = w_ref[0].astype(BF16)
        dav = da_ref[...]
        dsc_ref[...] = jnp.sum(dav * _dot(pooled, wb), axis=0, keepdims=True)
        dout = (dav * sc_ref[...]).astype(BF16)
        dw_ref[0] = _dot_tn(pooled, dout)
        dpooled = _dot_nt(dout, wb)
        dx_ref[0] = (_pool_window(dpooled / cnt, g, T, False) - dpooled).astype(BF16)

    col_spec = pl.BlockSpec((T, Hd), lambda g: (0, g))
    return _call(
        body, hosted, name=name, grid=(N_GROUPS,),
        in_specs=[col_spec, col_spec, pl.BlockSpec((1, Hd, Hd), lambda g: (g, 0, 0)), pl.BlockSpec((1, Hd), lambda g: (0, g))],
        out_specs=[pl.BlockSpec((1, T, Hd), lambda g: (N_GROUPS, 0, g)), pl.BlockSpec((1, Hd, Hd), lambda g: (g, 0, 0)),
                   pl.BlockSpec((1, Hd), lambda g: (0, g))],
        out_shape=[jax.ShapeDtypeStruct((N_GROUPS + 1, T, N_GROUPS * Hd), BF16), jax.ShapeDtypeStruct((N_GROUPS, Hd, Hd), F32),
                   jax.ShapeDtypeStruct((1, N_GROUPS * Hd), F32)],
        args=[proj, da, pool_w, pool_scale])


def _ret_tables(T):
    Hd, C = HEAD_DIM, RET_CHUNK
    inv_freq = 1.0 / (ROPE_BASE ** (jnp.arange(0, Hd, 2, dtype=F32) / Hd))
    ang = jnp.arange(T, dtype=F32)[:, None] * inv_freq[None, :]
    cos, sin = jnp.cos(ang), jnp.sin(ang)
    cos2 = jnp.concatenate([cos, cos], axis=-1)
    sin2 = jnp.concatenate([-sin, sin], axis=-1)
    log_gamma = jnp.log1p(-jnp.exp2(-5.0 - jnp.arange(N_GROUPS, dtype=F32)))
    pos = jnp.arange(C, dtype=F32)
    rel = pos[:, None] - pos[None, :]
    intra = jnp.where(rel[None] >= 0, jnp.exp(log_gamma[:, None, None] * jnp.maximum(rel, 0.0)[None]), 0.0)
    k_tail = jnp.exp(log_gamma[:, None] * (C - 1 - pos)[None, :])
    q_head = jnp.exp(log_gamma[:, None] * (pos + 1.0)[None, :])
    chunk_decay = jnp.exp(log_gamma * C)
    wide = lambda t: jnp.broadcast_to(t[:, :, None], (N_GROUPS, C, Hd))
    return cos2, sin2, intra, wide(k_tail), wide(q_head), jnp.broadcast_to(chunk_decay[:, None, None], (N_GROUPS, 1, Hd))


def _rope(x, cos2, sin2):
    return x * cos2 + pltpu.roll(x, HEAD_DIM // 2, 1) * sin2


def _rope_t(d, cos2, sin2):
    return d * cos2 + pltpu.roll(d * sin2, HEAD_DIM // 2, 1)


def _ret_specs(T, tseg, seg_of):
    Hd, G = HEAD_DIM, N_GROUPS
    col = lambda kind: pl.BlockSpec((tseg, Hd), lambda h, s: (seg_of(s), G * kind + h))
    tab = pl.BlockSpec((T, Hd), lambda h, s: (0, 0))
    head = pl.BlockSpec((1, RET_CHUNK, Hd), lambda h, s: (h, 0, 0))
    cd = pl.BlockSpec((1, 1, Hd), lambda h, s: (h, 0, 0))
    gain = pl.BlockSpec((1, Hd), lambda h, s: (0, h))
    return col, tab, head, cd, gain


def ret_fwd(proj, ret_norm, tables, name, hosted=()):
    T = proj.shape[0]
    Hd, C, G = HEAD_DIM, RET_CHUNK, N_GROUPS
    tseg = min(T, 2048)
    nseg, nck = T // tseg, tseg // C
    scale = Hd ** -0.5
    cos2, sin2, intra, k_tail, q_head, chunk_decay = tables

    def body(q_ref, k_ref, v_ref, g_ref, gain_ref, cos_ref, sin_ref, m_ref, kt_ref, qh_ref, cd_ref,
             b_ref, o_ref, rp_ref, state):
        @pl.when(pl.program_id(1) == 0)
        def _():
            state[...] = jnp.zeros_like(state)

        def chunk(ci, carry):
            rows = pl.ds(pl.multiple_of(ci * C, C), C)
            at = pl.ds(pl.multiple_of(pl.program_id(1) * tseg + ci * C, C), C)
            cos, sin = cos_ref[at, :], sin_ref[at, :]
            qr = _rope(q_ref[rows, :], cos, sin)
            kr = _rope(k_ref[rows, :], cos, sin) * scale
            qb, kb, vb = qr.astype(BF16), kr.astype(BF16), v_ref[rows, :].astype(BF16)
            r = state[...]
            rp_ref[0, ci] = r.astype(BF16)
            sc = _dot_nt(qb, kb) * m_ref[0]
            o = _dot(sc.astype(BF16), vb) + _dot((qr * qh_ref[0]).astype(BF16), r.astype(BF16))
            state[...] = cd_ref[0] * r + _dot_tn((kr * kt_ref[0]).astype(BF16), vb)
            o_ref[rows, :] = o
            on = o * _rstd(o)
            b_ref[rows, :] = (jax.nn.silu(g_ref[rows, :]) * (on * gain_ref[...])).astype(BF16)
            return carry

        lax.fori_loop(0, nck, chunk, 0, unroll=True)

    col, tab, head, cd, gain = _ret_specs(T, tseg, lambda s: s)
    out_col = pl.BlockSpec((tseg, Hd), lambda h, s: (s, h))
    return _call(
        body, hosted, name=name, grid=(G, nseg),
        in_specs=[col(1), col(2), col(3), col(4), gain, tab, tab, head, head, head, cd],
        out_specs=[out_col, out_col, pl.BlockSpec((1, nck, Hd, Hd), lambda h, s: (h, s, 0, 0))],
        out_shape=[jax.ShapeDtypeStruct((T, G * Hd), BF16), jax.ShapeDtypeStruct((T, G * Hd), F32),
                   jax.ShapeDtypeStruct((G, T // C, Hd, Hd), BF16)],
        scratch_shapes=[pltpu.VMEM((Hd, Hd), F32)],
        args=[proj, proj, proj, proj, ret_norm, cos2, sin2, intra, k_tail, q_head, chunk_decay])


def ret_bwd(proj, db, o_pre, r_prev, ret_norm, tables, dproj, name, hosted=()):
    T = proj.shape[0]
    Hd, C, G = HEAD_DIM, RET_CHUNK, N_GROUPS
    tseg = min(T, 2048)
    nseg, nck = T // tseg, tseg // C
    scale = Hd ** -0.5
    cos2, sin2, intra, k_tail, q_head, chunk_decay = tables

    def body(q_ref, k_ref, v_ref, g_ref, db_ref, o_ref, rp_ref, gain_ref, cos_ref, sin_ref, m_ref, kt_ref, qh_ref, cd_ref,
             _, d_ref, dgain_ref, gstate):
        @pl.when(pl.program_id(1) == 0)
        def _():
            gstate[...] = jnp.zeros_like(gstate)
            dgain_ref[...] = jnp.zeros_like(dgain_ref)

        def chunk(t, carry):
            ci = nck - 1 - t
            rows = pl.ds(pl.multiple_of(ci * C, C), C)
            at = pl.ds(pl.multiple_of((nseg - 1 - pl.program_id(1)) * tseg + ci * C, C), C)
            cos, sin = cos_ref[at, :], sin_ref[at, :]
            qr = _rope(q_ref[rows, :], cos, sin)
            kr = _rope(k_ref[rows, :], cos, sin) * scale
            qb, kb, vb = qr.astype(BF16), kr.astype(BF16), v_ref[rows, :].astype(BF16)
            qhb, ktb = (qr * qh_ref[0]).astype(BF16), (kr * kt_ref[0]).astype(BF16)
            sc = (_dot_nt(qb, kb) * m_ref[0]).astype(BF16)
            o = o_ref[rows, :]
            rstd = _rstd(o)
            on = o * rstd
            gain = gain_ref[...]
            silu, dsilu = _silu_parts(g_ref[rows, :])
            dy = db_ref[rows, :]
            dgain_ref[...] += jnp.sum(dy * silu * on, axis=0, keepdims=True)
            dg = dy * on * gain * dsilu
            don = dy * silu * gain
            dob = (rstd * (don - on * jnp.mean(don * on, axis=-1, keepdims=True))).astype(BF16)
            gn = gstate[...]
            gb = gn.astype(BF16)
            da = (_dot_nt(dob, vb) * m_ref[0]).astype(BF16)
            dq = _dot(da, kb) + _dot_nt(dob, rp_ref[0, ci]) * qh_ref[0]
            dk = _dot_tn(da, qb) + _dot_nt(vb, gb) * kt_ref[0]
            dv = _dot_tn(sc, dob) + _dot(ktb, gb)
            gstate[...] = cd_ref[0] * gn + _dot_tn(qhb, dob)
            d_ref[0, rows, :] = _rope_t(dq, cos, sin).astype(BF16)
            d_ref[1, rows, :] = _rope_t(dk * scale, cos, sin).astype(BF16)
            d_ref[2, rows, :] = dv.astype(BF16)
            d_ref[3, rows, :] = dg.astype(BF16)
            return carry

        lax.fori_loop(0, nck, chunk, 0, unroll=True)

    rev = lambda s: nseg - 1 - s
    col, tab, head, cd, gain = _ret_specs(T, tseg, rev)
    act = pl.BlockSpec((tseg, Hd), lambda h, s: (rev(s), h))
    return _call(
        body, hosted, name=name, grid=(G, nseg),
        in_specs=[col(1), col(2), col(3), col(4), act, act, pl.BlockSpec((1, nck, Hd, Hd), lambda h, s: (h, rev(s), 0, 0)),
                  gain, tab, tab, head, head, head, cd, ANY],
        out_specs=[pl.BlockSpec((4, tseg, Hd), lambda h, s: (0, rev(s), h)), gain],
        out_shape=[jax.ShapeDtypeStruct(dproj.shape, BF16), jax.ShapeDtypeStruct((1, G * Hd), F32)],
        scratch_shapes=[pltpu.VMEM((Hd, Hd), F32)], aliased={14: 0},
        args=[proj, proj, proj, proj, db, o_pre, r_prev, ret_norm, cos2, sin2, intra, k_tail, q_head, chunk_decay, dproj])


def ffn_down_loss(s, w2, h, gain, target, name, hosted=()):
    T, F = s.shape
    D = h.shape[1]
    tm = min(T, 512)

    def body(s_ref, w2_ref, h_ref, g_ref, t_ref, dh_ref, loss_ref, dg_ref):
        @pl.when(pl.program_id(0) == 0)
        def _():
            loss_ref[...] = jnp.zeros_like(loss_ref)
            dg_ref[...] = jnp.zeros_like(dg_ref)

        hh = h_ref[...] + 0.5 * _dot(s_ref[...], w2_ref[...])
        gain_v = g_ref[...]
        err = hh * _rstd(hh) * gain_v - t_ref[...]
        loss_ref[...] += 0.5 * jnp.sum(jnp.mean(err * err, axis=-1, keepdims=True), axis=0, keepdims=True)
        dhn, dg = _rmsnorm_bwd(err * (1.0 / D), hh, gain_v)
        dh_ref[...] = dhn
        dg_ref[...] += jnp.sum(dg, axis=0, keepdims=True)

    row_spec = pl.BlockSpec((tm, D), lambda i: (i, 0))
    vec_spec = pl.BlockSpec((1, D), lambda i: (0, 0))
    return _call(
        body, hosted, name=name, grid=(T // tm,),
        in_specs=[pl.BlockSpec((tm, F), lambda i: (i, 0)), _resident((F, D)), row_spec, vec_spec, row_spec],
        out_specs=[row_spec, pl.BlockSpec((1, LANES), lambda i: (0, 0)), vec_spec],
        out_shape=[jax.ShapeDtypeStruct((T, D), F32), jax.ShapeDtypeStruct((1, LANES), F32), jax.ShapeDtypeStruct((1, D), F32)],
        args=[s, w2, h, gain, target])


def prereduce(grads, recvs, place, name):
    nt = len(grads)
    nsh, R, C = grads[0].shape
    rh = R // 2

    def body(place_ref, *refs):
        for t in range(nt):
            g_ref, r_ref, o_ref, own_ref = refs[2 * t], refs[2 * t + 1], refs[2 * nt + 2 * t], refs[2 * nt + 2 * t + 1]
            piece = (g_ref[...].astype(F32) + r_ref[...].astype(F32)).astype(BF16)
            o_ref[...] = piece

            @pl.when(pl.program_id(0) == place_ref[1])
            def _():
                own_ref[...] = piece

    outs = pl.pallas_call(
        body, name=name,
        grid_spec=pltpu.PrefetchScalarGridSpec(
            num_scalar_prefetch=1, grid=(nsh,),
            in_specs=[pl.BlockSpec((1, rh, C), lambda j, p: (j, p[0], 0)), pl.BlockSpec((1, rh, C), lambda j, p: (j, 0, 0))] * nt,
            out_specs=[pl.BlockSpec((1, rh, C), lambda j, p: (j, 0, 0)),
                       pl.BlockSpec((1, rh, C), lambda j, p: (p[1], p[0], 0))] * nt),
        out_shape=[jax.ShapeDtypeStruct((nsh, rh, C), BF16), jax.ShapeDtypeStruct((nsh, R, C), BF16)] * nt,
        compiler_params=pltpu.CompilerParams(vmem_limit_bytes=VMEM_LIMIT_V7X),
    )(place, *[a for pair in zip(grads, recvs) for a in pair])
    return [(outs[2 * t], outs[2 * t + 1]) for t in range(nt)]


def _adamw(w, g, m, v):
    m = ADAM_B1 * m + (1.0 - ADAM_B1) * g
    v = ADAM_B2 * v + (1.0 - ADAM_B2) * (g * g)
    m_hat = m / (1.0 - ADAM_B1 ** ADAM_STEP)
    v_hat = v / (1.0 - ADAM_B2 ** ADAM_STEP)
    return -ADAM_LR * (m_hat / (jnp.sqrt(v_hat) + ADAM_EPS) + ADAM_WD * w), m, v


def adamw_sharded(tensors, name, hosted=()):
    nt = len(tensors)
    nsh = tensors[0][0].shape[0]
    shapes = [t[0].shape[1:] for t in tensors]

    def fits(steps):
        if any(R % (steps * BF16_TILE_ROWS) for R, _ in shapes):
            return False
        return sum(2 * (R // steps) * -(-C // LANES) * LANES * (nsh * 2 + 7 * 4) for R, C in shapes) <= ADAMW_VMEM_BUDGET

    steps = min(s for s in range(1, min(R for R, _ in shapes) // BF16_TILE_ROWS + 1) if fits(s))

    def body(*refs):
        ins, outs = refs[:4 * nt], refs[4 * nt:]
        for t in range(nt):
            p_ref, w_ref, m_ref, v_ref = ins[4 * t:4 * t + 4]
            g_ref, d_ref, nm_ref, nv_ref = outs[4 * t:4 * t + 4]
            g = p_ref[0].astype(F32)
            for i in range(1, nsh):
                g += p_ref[i].astype(F32)
            g_ref[...] = g
            d_ref[...], nm_ref[...], nv_ref[...] = _adamw(w_ref[...], g, m_ref[...], v_ref[...])

    in_specs, out_specs, out_shape = [], [], []
    for R, C in shapes:
        spec = pl.BlockSpec((R // steps, C), lambda i: (i, 0))
        in_specs += [pl.BlockSpec((nsh, R // steps, C), lambda i: (0, i, 0)), spec, spec, spec]
        out_specs += [spec] * 4
        out_shape += [jax.ShapeDtypeStruct((R, C), F32)] * 4
    return _call(body, hosted, name=name, grid=(steps,), in_specs=in_specs, out_specs=out_specs, out_shape=out_shape,
                 args=[a for tensor in tensors for a in tensor])


def adamw_small(packs, late, pool, vectors, name):
    ndev = packs.shape[0]
    rp, rv, rl = pool[0].shape[0], vectors.shape[0] // 3, late.shape[1]

    def body(p_ref, l_ref, wp, mp, vp, wmv, gp, dp, nmp, nvp, gv, dv, nmv, nvv, loss_ref):
        wv, mv, vv = wmv.at[0:rv], wmv.at[rv:2 * rv], wmv.at[2 * rv:3 * rv]
        g, first = p_ref[0], l_ref[0]
        for i in range(1, ndev):
            g += p_ref[i]
            first += l_ref[i]
        g_pool = g[0:rp]
        g_vec = jnp.concatenate([g[rp:rp + rl] + first, g[rp + rl:rp + rv]], axis=0)
        gp[...], gv[...], loss_ref[...] = g_pool, g_vec, g[rp + rv:rp + rv + 8]
        dp[...], nmp[...], nvp[...] = _adamw(wp[...], g_pool, mp[...], vp[...])
        dv[...], nmv[...], nvv[...] = _adamw(wv[...], g_vec, mv[...], vv[...])

    shape = lambda rows: jax.ShapeDtypeStruct((rows, LANES), F32)
    outs = pl.pallas_call(body, name=name, out_shape=[shape(rp)] * 4 + [shape(rv)] * 4 + [shape(8)],
                          compiler_params=pltpu.CompilerParams(vmem_limit_bytes=VMEM_LIMIT_V7X))(packs, late, *pool, vectors)
    return outs[0:4], outs[4:8], outs[8]


BIG = ("ffn1_w1", "ffn1_w3", "ffn1_w2", "w_in", "w_out", "ffn2_w1", "ffn2_w3", "ffn2_w2")
TRANSPOSED = ("ffn1_w1", "ffn1_w3", "ffn2_w1", "ffn2_w3")
VECTORS = ("ffn1_norm", "mix_norm", "pool_scale", "ret_norm", "ffn2_norm", "final_norm")
WEIGHTS = ("ffn1_norm", "ffn1_w1", "ffn1_w3", "ffn1_w2", "mix_norm", "w_in", "pool_w", "pool_scale", "ret_norm", "w_out",
           "ffn2_norm", "ffn2_w1", "ffn2_w3", "ffn2_w2", "final_norm")


def _pack_vectors(parts):
    return jnp.concatenate([parts[k].reshape(-1, LANES) for k in VECTORS], axis=0)


def _unpack_vectors(pack, like):
    out, row = {}, 0
    for k in VECTORS:
        rows = like[k].size // LANES
        out[k] = pack[row:row + rows].reshape(like[k].shape)
        row += rows
    return out


def kernel(x, ffn1_norm, ffn1_w1, ffn1_w3, ffn1_w2, mix_norm, w_in, pool_w, pool_scale, ret_norm, w_out, ffn2_norm, ffn2_w1, ffn2_w3, ffn2_w2, final_norm, loss_target, m_ffn1_norm, m_ffn1_w1, m_ffn1_w3, m_ffn1_w2, m_mix_norm, m_w_in, m_pool_w, m_pool_scale, m_ret_norm, m_w_out, m_ffn2_norm, m_ffn2_w1, m_ffn2_w3, m_ffn2_w2, m_final_norm, v_ffn1_norm, v_ffn1_w1, v_ffn1_w3, v_ffn1_w2, v_mix_norm, v_w_in, v_pool_w, v_pool_scale, v_ret_norm, v_w_out, v_ffn2_norm, v_ffn2_w1, v_ffn2_w3, v_ffn2_w2, v_final_norm):
    w = dict(ffn1_norm=ffn1_norm, ffn1_w1=ffn1_w1, ffn1_w3=ffn1_w3, ffn1_w2=ffn1_w2, mix_norm=mix_norm, w_in=w_in, pool_w=pool_w,
             pool_scale=pool_scale, ret_norm=ret_norm, w_out=w_out, ffn2_norm=ffn2_norm, ffn2_w1=ffn2_w1, ffn2_w3=ffn2_w3,
             ffn2_w2=ffn2_w2, final_norm=final_norm)
    m = dict(ffn1_norm=m_ffn1_norm, ffn1_w1=m_ffn1_w1, ffn1_w3=m_ffn1_w3, ffn1_w2=m_ffn1_w2, mix_norm=m_mix_norm, w_in=m_w_in,
             pool_w=m_pool_w, pool_scale=m_pool_scale, ret_norm=m_ret_norm, w_out=m_w_out, ffn2_norm=m_ffn2_norm, ffn2_w1=m_ffn2_w1,
             ffn2_w3=m_ffn2_w3, ffn2_w2=m_ffn2_w2, final_norm=m_final_norm)
    v = dict(ffn1_norm=v_ffn1_norm, ffn1_w1=v_ffn1_w1, ffn1_w3=v_ffn1_w3, ffn1_w2=v_ffn1_w2, mix_norm=v_mix_norm, w_in=v_w_in,
             pool_w=v_pool_w, pool_scale=v_pool_scale, ret_norm=v_ret_norm, w_out=v_w_out, ffn2_norm=v_ffn2_norm, ffn2_w1=v_ffn2_w1,
             ffn2_w3=v_ffn2_w3, ffn2_w2=v_ffn2_w2, final_norm=v_final_norm)
    xs, target = x[0], loss_target[0]
    T = xs.shape[0]
    tables = _ret_tables(T)
    place = jnp.stack([lax.axis_index("c"), 2 * lax.axis_index("x") + lax.axis_index("y")]).astype(jnp.int32)
    local = lambda d, k: jnp.transpose(d[k][0]) if k in TRANSPOSED else d[k][0]
    result = lambda o, k: jnp.transpose(o)[None] if k in TRANSPOSED else o[None]
    first = ("ffn1_w1", "ffn1_w3")
    sh = {k: local(w, k).astype(BF16) for k in first}
    gather = lambda *names: [ChipExchange([sh[k] for k in names], False)]
    wg, grad, delta, new_m, new_v = {}, {}, {}, {}, {}

    def update(names, pieces, name, hosted=()):
        outs, extras = adamw_sharded([(p, local(w, k), local(m, k), local(v, k)) for k, p in zip(names, pieces)], name, hosted)
        for t, k in enumerate(names):
            grad[k], delta[k], new_m[k], new_v[k] = [result(o, k) for o in outs[4 * t:4 * t + 4]]
        return extras

    def reduce_in_chip(name, *pairs):
        reduced = prereduce([p for p, _ in pairs], [r for _, r in pairs], place, "prereduce_" + name)
        return reduced[0] if len(pairs) == 1 else reduced

    scatter = lambda *reduced: ChipExchange([r[0] for r in reduced], True, [r[1] for r in reduced])
    whole = lambda k: wg[k].reshape(-1, wg[k].shape[-1])
    sharded = lambda g: g.reshape(N_CHIPS, -1, g.shape[-1])

    later = [k for k in BIG if k not in first]
    casts, ((wg["ffn1_w1"], wg["ffn1_w3"]),) = cast_shards([local(w, k) for k in later], "cast_gather_ffn1", gather(*first))
    sh.update(zip(later, casts))
    (n1, ga1, gb1, s1), ((wg["ffn1_w2"], wg["w_in"]),) = ffn_up(
        xs, ffn1_norm, whole("ffn1_w1"), whole("ffn1_w3"), "ffn1_up", gather("ffn1_w2", "w_in"))
    (h1, u, proj), ((wg["w_out"], wg["ffn2_w1"]),) = ffn_down_mix_in(
        s1, whole("ffn1_w2"), xs, mix_norm, wg["w_in"], "ffn1_down_mix_in", gather("w_out", "ffn2_w1"))
    (pa,), _ = pool_fwd(proj, pool_w[0], pool_scale, "pool_fwd")
    (rb, o_pre, r_prev), ((wg["ffn2_w3"],),) = ret_fwd(proj, ret_norm, tables, "ret_fwd", gather("ffn2_w3"))
    (h2, n2, ga2, gb2, s2), ((wg["ffn2_w2"],),) = ffn_up(
        h1, ffn2_norm, whole("ffn2_w1"), whole("ffn2_w3"), "mix_out_ffn2_up", gather("ffn2_w2"), mixed=(pa, rb, wg["w_out"]))
    (dh3, loss, d_final), _ = ffn_down_loss(s2, whole("ffn2_w2"), h2, final_norm[None], target, "ffn2_down_loss")

    (da2, db2, df2), _ = ffn_bwd_act(dh3, whole("ffn2_w2"), ga2, gb2, "ffn2_bwd_act")
    (g_f2w2,), _ = ffn_dw([s2], df2, 1, "ffn2_dw2")
    g_f2w2 = sharded(g_f2w2)
    (g_f2w1, g_f2w3), ((r_f2w2,),) = ffn_dw([da2, db2], n2, 2, "ffn2_dw13", [SiblingExchange([g_f2w2])])
    g_f2w1, g_f2w3 = sharded(g_f2w1), sharded(g_f2w3)
    p_f2w2 = reduce_in_chip("ffn2_w2", (g_f2w2, r_f2w2))
    (dh2, d_ffn2), ((q_f2w2,), (r_f2w1, r_f2w3)) = ffn_bwd_in(
        da2, db2, whole("ffn2_w1"), whole("ffn2_w3"), h2, ffn2_norm, dh3, "ffn2_bwd_in",
        [scatter(p_f2w2), SiblingExchange([g_f2w1, g_f2w3])])
    p_f2w1, p_f2w3 = reduce_in_chip("ffn2_w13", (g_f2w1, r_f2w1), (g_f2w3, r_f2w3))
    (dpa, drb, g_wout), _ = mix_out_bwd(dh2, wg["w_out"], pa, rb, "mix_out_bwd")
    (dproj, d_pool_w, d_pool_scale), _ = pool_bwd(proj, dpa, pool_w[0], pool_scale, "pool_bwd")
    (dproj, d_ret_norm), ((q_f2w1, q_f2w3), (r_wout,)) = ret_bwd(
        proj, drb, o_pre, r_prev, ret_norm, tables, dproj, "ret_bwd", [scatter(p_f2w1, p_f2w3), SiblingExchange([g_wout])])
    p_wout = reduce_in_chip("w_out", (g_wout, r_wout))
    (g_win,), ((q_wout,),) = mix_dwin(u, dproj, N_CHIPS, "mix_dwin", [scatter(p_wout)])
    (dh1, d_mix), ((r_win,),) = mix_in_bwd(dproj, wg["w_in"], h1, mix_norm, dh2, "mix_in_bwd", [SiblingExchange([g_win])])
    p_win = reduce_in_chip("w_in", (g_win, r_win))
    (da1, db1, df1), ((q_win,),) = ffn_bwd_act(dh1, whole("ffn1_w2"), ga1, gb1, "ffn1_bwd_act", [scatter(p_win)])
    d_vectors = {"ffn1_norm": jnp.zeros_like(ffn1_norm), "mix_norm": d_mix, "pool_scale": d_pool_scale,
                 "ret_norm": d_ret_norm, "ffn2_norm": d_ffn2, "final_norm": d_final}
    pack = jnp.concatenate([d_pool_w.reshape(-1, LANES), _pack_vectors(d_vectors), jnp.broadcast_to(loss, (8, LANES))], axis=0)
    (g_f1w1, g_f1w3), ((packs,),) = ffn_dw([da1, db1], n1, 2, "ffn1_dw13", [AllExchange(pack)])
    g_f1w1, g_f1w3 = sharded(g_f1w1), sharded(g_f1w3)
    (g_f1w2,), ((r_f1w1, r_f1w3),) = ffn_dw([s1], df1, 1, "ffn1_dw2", [SiblingExchange([g_f1w1, g_f1w3])])
    g_f1w2 = sharded(g_f1w2)
    p_f1w1, p_f1w3 = reduce_in_chip("ffn1_w13", (g_f1w1, r_f1w1), (g_f1w3, r_f1w3))
    (dx, d_ffn1), ((q_f1w1, q_f1w3), (r_f1w2,)) = ffn_bwd_in(
        da1, db1, whole("ffn1_w1"), whole("ffn1_w3"), xs, ffn1_norm, dh1, "ffn1_bwd_in",
        [scatter(p_f1w1, p_f1w3), SiblingExchange([g_f1w2])])
    p_f1w2 = reduce_in_chip("ffn1_w2", (g_f1w2, r_f1w2))

    (q_f1w2,), (late,) = update(["w_in", "w_out", "ffn2_w2"], [q_win, q_wout, q_f2w2], "adamw_mix_w2",
                                [scatter(p_f1w2), AllExchange(d_ffn1.reshape(-1, LANES))])
    update(["ffn2_w1", "ffn2_w3", "ffn1_w1", "ffn1_w3"], [q_f2w1, q_f2w3, q_f1w1, q_f1w3], "adamw_w13")
    update(["ffn1_w2"], [q_f1w2], "adamw_ffn1_w2")
    of_pool, of_vectors, loss_sum = adamw_small(packs, late, [t["pool_w"].reshape(-1, LANES) for t in (w, m, v)],
                                                jnp.concatenate([t[k].reshape(-1, LANES) for t in (w, m, v) for k in VECTORS], axis=0),
                                                "adamw_small")
    for res, pool_part, vector_part in zip((grad, delta, new_m, new_v), of_pool, of_vectors):
        res["pool_w"] = pool_part.reshape(pool_w.shape)
        res.update(_unpack_vectors(vector_part, w))
    loss = loss_sum[0, 0]

    return (loss, dx[None], *[grad[k] for k in WEIGHTS], *[delta[k] for k in WEIGHTS],
            *[new_m[k] for k in WEIGHTS], *[new_v[k] for k in WEIGHTS])
```

```python
import math

import jax
import jax.numpy as jnp
from jax import lax
from jax.experimental import pallas as pl
from jax.experimental.pallas import tpu as pltpu

F32 = jnp.float32
BF16 = jnp.bfloat16

EPS = 1e-6
LANES = 128
BF16_TILE_ROWS = 16
N_CHIPS = 4
N_GROUPS = 4
HEAD_DIM = 128
RET_CHUNK = 128
ROPE_BASE = 10000.0
ADAM_LR, ADAM_B1, ADAM_B2, ADAM_EPS, ADAM_WD, ADAM_STEP = 0.001, 0.9, 0.999, 1e-08, 0.01, 10
VMEM_LIMIT_V7X = 56 * 1024 * 1024
ADAMW_VMEM_BUDGET = 32 * 1024 * 1024
MESH = pl.DeviceIdType.MESH
ANY = pl.BlockSpec(memory_space=pl.ANY)


def _dot(a, b):
    return jnp.dot(a, b, preferred_element_type=F32)


def _dot_nt(a, b):
    return lax.dot_general(a, b, (((1,), (1,)), ((), ())), preferred_element_type=F32)


def _dot_tn(a, b):
    return lax.dot_general(a, b, (((0,), (0,)), ((), ())), preferred_element_type=F32)


def _rstd(h):
    return lax.rsqrt(jnp.mean(h * h, axis=-1, keepdims=True) + EPS)


def _rmsnorm_bwd(dn, h, gain):
    r = _rstd(h)
    nh = h * r
    dnh = dn * gain
    dh = r * (dnh - nh * jnp.mean(dnh * nh, axis=-1, keepdims=True))
    return dh, dn * nh


def _silu_parts(a):
    sig = jax.nn.sigmoid(a)
    silu = a * sig
    return silu, sig + silu * (1.0 - sig)


def _mesh_pos():
    return lax.axis_index("x"), lax.axis_index("y"), lax.axis_index("c")


class ChipExchange:
    def __init__(self, srcs, scatter, placed=()):
        n = len(srcs)
        self.inputs, self.scatter, self.n, self.reach = list(srcs) + list(placed), scatter, n, REACH_CHIPS
        self.aliases = {n + t: t for t in range(n)} if scatter else {}
        self.half_rows = [s.shape[1] if scatter else s.shape[0] // 2 for s in srcs]
        self.out_shape = [jax.ShapeDtypeStruct((N_CHIPS, 2 * rh, s.shape[-1]), s.dtype) for s, rh in zip(srcs, self.half_rows)]
        if scatter:
            self.out_shape += [jax.ShapeDtypeStruct((2, rh // 2, s.shape[-1]), s.dtype) for s, rh in zip(srcs, self.half_rows)]
        dma = pltpu.SemaphoreType.DMA
        self.sems = [dma((4 * n,)), dma((4 * n,)), dma((2 * n,)), dma((2 * n,)), dma((4 * n,)), dma((4 * n,))]

    def _copies(self, src, out, sems):
        hop1_send, hop1_recv, hop2_send, hop2_recv, d2d_send, d2d_recv = sems
        x, y, c = _mesh_pos()
        me, dg = 2 * x + y, 2 * (1 - x) + (1 - y)
        sibling = (x, y, 1 - c)
        n = self.n
        mine, theirs = c, 1 - c

        def nb(a):
            nx, ny = x ^ (1 - a), y ^ a
            return 2 * nx + ny, (nx, ny, c)

        def remote(s, d, send, recv, k, to):
            return pltpu.make_async_remote_copy(src_ref=s, dst_ref=d, send_sem=send.at[k], recv_sem=recv.at[k],
                                                device_id=to, device_id_type=MESH)

        class Copies:
            def slot(_, t, chip, half):
                rh = self.half_rows[t]
                return out[t].at[chip, pl.ds(half * rh, rh), :]

            def quarter(_, t, chip, q):
                qh = self.half_rows[t] // 2
                return out[t].at[chip, pl.ds(mine * 2 * qh + q * qh, qh), :]

            def own_shard(k, t):
                return remote(src[t], out[t].at[me], d2d_send, d2d_recv, 4 * t + 3, sibling)

            def hop1(k, t, a, transit=False):
                rh = self.half_rows[t]
                chip, to = nb(a)
                if transit:
                    piece = src[t].at[dg, pl.ds(a * (rh // 2), rh // 2), :]
                    return remote(piece, out[n + t].at[a], hop1_send, hop1_recv, 4 * t + 2 + a, to)
                piece = src[t].at[chip] if self.scatter else src[t].at[pl.ds(mine * rh, rh), :]
                return remote(piece, k.slot(t, me, mine), hop1_send, hop1_recv, 4 * t + a, to)

            def landed1(k, t, a, transit=False):
                here = out[n + t].at[a] if transit else k.slot(t, nb(a)[0], mine)
                return remote(here, here, hop1_send, hop1_recv, 4 * t + (2 if transit else 0) + a, sibling)

            def hop2(k, t, q):
                origin, to = nb(q)[0], nb(1 - q)[1]
                piece = out[n + t].at[q] if self.scatter else k.quarter(t, origin, q)
                return remote(piece, k.quarter(t, origin, q), hop2_send, hop2_recv, 2 * t + q, to)

            def landed2(k, t, q):
                here = k.quarter(t, dg, q)
                return remote(here, here, hop2_send, hop2_recv, 2 * t + q, sibling)

            def d2d(k, t, p, chip, own=False, arriving=False):
                if arriving:
                    there = k.slot(t, chip, theirs)
                    return remote(there, there, d2d_send, d2d_recv, 4 * t + p, sibling)
                piece = src[t].at[me] if own else k.slot(t, chip, mine)
                return remote(piece, k.slot(t, chip, mine), d2d_send, d2d_recv, 4 * t + p, sibling)

        return Copies(), nb, me, dg, c

    def start(self, src, out, sems):
        k, nb, me, dg, c = self._copies(src, out, sems)
        for t in range(self.n):
            for first in range(2):
                a = first ^ c
                k.hop1(t, a).start()
                if self.scatter:
                    k.hop1(t, a, transit=True).start()
            if self.scatter:
                k.d2d(t, 3, me, own=True).start()
            else:
                k.own_shard(t).start()

    def mid(self, src, out, sems):
        k, nb, me, dg, c = self._copies(src, out, sems)
        for t in range(self.n):
            for first in range(2):
                a = first ^ c
                if self.scatter:
                    k.landed1(t, a, transit=True).wait_recv()
                    k.hop2(t, a).start()
                k.landed1(t, a).wait_recv()
                if not self.scatter:
                    k.hop2(t, a).start()
                k.d2d(t, a, nb(a)[0]).start()

    def finish(self, src, out, sems):
        k, nb, me, dg, c = self._copies(src, out, sems)
        for t in range(self.n):
            for q in range(2):
                k.landed2(t, q).wait_recv()
            k.d2d(t, 2, dg).start()
        for t in range(self.n):
            for a in range(2):
                k.d2d(t, a, nb(a)[0], arriving=True).wait_recv()
            k.d2d(t, 2, dg, arriving=True).wait_recv()
            if self.scatter:
                k.d2d(t, 3, me, arriving=True).wait_recv()
        for t in range(self.n):
            for a in range(2):
                k.hop1(t, a).wait_send()
                if self.scatter:
                    k.hop1(t, a, transit=True).wait_send()
                k.hop2(t, a).wait_send()
                k.d2d(t, a, nb(a)[0]).wait_send()
            k.d2d(t, 2, dg).wait_send()
            if self.scatter:
                k.d2d(t, 3, me, own=True).wait_send()
            else:
                k.own_shard(t).wait()


class SiblingExchange:
    def __init__(self, grads):
        self.inputs, self.n, self.aliases, self.reach = list(grads), len(grads), {}, REACH_SIBLING
        self.half_rows = [g.shape[1] // 2 for g in grads]
        self.out_shape = [jax.ShapeDtypeStruct((g.shape[0], rh, g.shape[2]), g.dtype) for g, rh in zip(grads, self.half_rows)]
        self.sems = [pltpu.SemaphoreType.DMA((self.n,)), pltpu.SemaphoreType.DMA((self.n,))]

    def _plan(self, src, out, sems):
        x, y, c = _mesh_pos()
        return [pltpu.make_async_remote_copy(
            src_ref=src[t].at[:, pl.ds((1 - c) * self.half_rows[t], self.half_rows[t]), :], dst_ref=out[t],
            send_sem=sems[0].at[t], recv_sem=sems[1].at[t], device_id=(x, y, 1 - c), device_id_type=MESH) for t in range(self.n)]

    def start(self, src, out, sems):
        for cp in self._plan(src, out, sems):
            cp.start()

    def mid(self, src, out, sems):
        pass

    def finish(self, src, out, sems):
        for cp in self._plan(src, out, sems):
            cp.wait()


REACH_SIBLING, REACH_CHIPS, REACH_ALL = 0, 1, 2


def _entry_barrier(reach):
    x, y, c = _mesh_pos()
    peers = [(x, y, 1 - c)]
    if reach == REACH_CHIPS:
        peers += [(1 - x, y, c), (x, 1 - y, c)]
    elif reach == REACH_ALL:
        peers = [(x ^ dx, y ^ dy, c ^ dc) for dx in (0, 1) for dy in (0, 1) for dc in (0, 1)][1:]
    barrier = pltpu.get_barrier_semaphore()
    for peer in peers:
        pl.semaphore_signal(barrier, inc=1, device_id=peer, device_id_type=MESH)
    pl.semaphore_wait(barrier, len(peers))


def _call(body, hosted=(), *, name, in_specs, out_specs, out_shape, args, grid=(), scratch_shapes=(), aliased=None):
    n_in, n_out, n_scr = len(in_specs), len(out_specs), len(scratch_shapes)
    total = math.prod(grid)
    mid_step = max(0, (5 * total) // 8 - 1)

    def full(*refs):
        pos = [0]

        def take(k):
            pos[0] += k
            return refs[pos[0] - k:pos[0]]

        ins, h_in = take(n_in), [take(len(h.inputs)) for h in hosted]
        outs, h_out = take(n_out), [take(len(h.out_shape)) for h in hosted]
        scr, h_sem = take(n_scr), [take(len(h.sems)) for h in hosted]
        step = 0
        for axis, size in enumerate(grid):
            step = step * size + pl.program_id(axis)

        def phase(at, method):
            if not hosted:
                return

            def run():
                if method == "start":
                    _entry_barrier(reach)
                for h, s, o, m in zip(hosted, h_in, h_out, h_sem):
                    getattr(h, method)(s, o, m)

            if total == 1:
                run()
            else:
                pl.when(step == at)(run)

        phase(0, "start")
        body(*ins, *outs, *scr)
        phase(mid_step, "mid")
        phase(total - 1, "finish")

    aliases, i0, o0 = dict(aliased or {}), n_in, n_out
    for h in hosted:
        aliases.update({i0 + i: o0 + o for i, o in h.aliases.items()})
        i0, o0 = i0 + len(h.inputs), o0 + len(h.out_shape)
    reach = max((h.reach for h in hosted), default=None)
    params = dict(vmem_limit_bytes=VMEM_LIMIT_V7X)
    if hosted:
        params["collective_id"] = reach
    results = pl.pallas_call(
        full, name=name, grid=grid,
        in_specs=list(in_specs) + [ANY] * (i0 - n_in),
        out_specs=list(out_specs) + [ANY] * (o0 - n_out),
        out_shape=list(out_shape) + [s for h in hosted for s in h.out_shape],
        scratch_shapes=list(scratch_shapes) + [s for h in hosted for s in h.sems],
        input_output_aliases=aliases,
        compiler_params=pltpu.CompilerParams(**params),
    )(*args, *[s for h in hosted for s in h.inputs])
    outs, extras, pos = list(results[:n_out]), [], n_out
    for h in hosted:
        extras.append(list(results[pos:pos + h.n]))
        pos += len(h.out_shape)
    return outs, extras


def cast_shards(shards, name, hosted=()):
    n = len(shards)

    def body(*refs):
        for x_ref, o_ref in zip(refs[:n], refs[n:]):
            o_ref[...] = x_ref[...].astype(BF16)

    whole = lambda s: pl.BlockSpec(s.shape, lambda: (0,) * s.ndim)
    return _call(body, hosted, name=name, in_specs=[whole(s) for s in shards], out_specs=[whole(s) for s in shards],
                 out_shape=[jax.ShapeDtypeStruct(s.shape, BF16) for s in shards], args=list(shards))


class AllExchange:
    def __init__(self, arrays):
        n = len(arrays)
        self.inputs, self.n, self.aliases, self.reach = list(arrays), n, {}, REACH_ALL
        self.out_shape = [jax.ShapeDtypeStruct((2 * N_CHIPS,) + a.shape, a.dtype) for a in arrays]
        self.sems = [pltpu.SemaphoreType.DMA((n,)), pltpu.SemaphoreType.DMA((7 * n,)), pltpu.SemaphoreType.DMA((7 * n,))]

    def _copies(self, src, out, sems):
        local_sem, send_sem, recv_sem = sems
        x, y, c = _mesh_pos()
        me = 4 * x + 2 * y + c
        peers = [(x ^ dx, y ^ dy, c ^ dc) for dx in (0, 1) for dy in (0, 1) for dc in (0, 1)][1:]
        remote = lambda s, d, k, to: pltpu.make_async_remote_copy(
            src_ref=s, dst_ref=d, send_sem=send_sem.at[k], recv_sem=recv_sem.at[k], device_id=to, device_id_type=MESH)
        sends, landed, local = [], [], []
        for t in range(self.n):
            local.append(pltpu.make_async_copy(src[t], out[t].at[me], local_sem.at[t]))
            for p, (px, py, pc) in enumerate(peers):
                sends.append(remote(src[t], out[t].at[me], 7 * t + p, (px, py, pc)))
                here = out[t].at[4 * px + 2 * py + pc]
                landed.append(remote(here, here, 7 * t + p, (px, py, pc)))
        return sends, landed, local

    def start(self, src, out, sems):
        sends, _, local = self._copies(src, out, sems)
        for cp in sends + local:
            cp.start()

    def mid(self, src, out, sems):
        pass

    def finish(self, src, out, sems):
        sends, landed, local = self._copies(src, out, sems)
        for cp in landed:
            cp.wait_recv()
        for cp in sends:
            cp.wait_send()
        for cp in local:
            cp.wait()


MXU_COLS = 256


def _resident(shape):
    return pl.BlockSpec(shape, lambda *_: (0,) * len(shape), pipeline_mode=pl.Buffered(1))


def ffn_up(h, gain, w1, w3, name, hosted=(), mixed=None):
    T, D = h.shape
    F = w1.shape[0]
    tm = min(T, 512)

    def body(*refs):
        if mixed is None:
            h_ref, g_ref, w1_ref, w3_ref, n_ref, ga_ref, gb_ref, s_ref = refs
            hh = h_ref[...]
        else:
            pa_ref, rb_ref, wo_ref, h_ref, g_ref, w1_ref, w3_ref, hh_ref, n_ref, ga_ref, gb_ref, s_ref = refs
            hh = h_ref[...] + _dot(pa_ref[...], wo_ref[0]) + _dot(rb_ref[...], wo_ref[1])
            hh_ref[...] = hh
        n = (hh * _rstd(hh) * g_ref[...]).astype(BF16)
        n_ref[...] = n
        for c in range(0, F, MXU_COLS):
            cols = slice(c, c + MXU_COLS)
            a = _dot_nt(n, w1_ref[cols, :])
            b = _dot_nt(n, w3_ref[cols, :])
            silu, dsilu = _silu_parts(a)
            ga_ref[:, cols] = (b * dsilu).astype(BF16)
            gb_ref[:, cols] = silu.astype(BF16)
            s_ref[:, cols] = (silu * b).astype(BF16)

    act = jax.ShapeDtypeStruct((T, F), BF16)
    act_spec = pl.BlockSpec((tm, F), lambda i: (i, 0))
    row_spec = pl.BlockSpec((tm, D), lambda i: (i, 0))
    in_specs = [row_spec, pl.BlockSpec((1, D), lambda i: (0, 0)), _resident((F, D)), _resident((F, D))]
    out_specs, out_shape, args = [row_spec, act_spec, act_spec, act_spec], [jax.ShapeDtypeStruct((T, D), BF16), act, act, act], [h, gain, w1, w3]
    if mixed is not None:
        pa, rb, woutg = mixed
        W = pa.shape[1]
        in_specs = [pl.BlockSpec((tm, W), lambda i: (i, 0))] * 2 + [_resident((2, W, D))] + in_specs
        out_specs, out_shape = [row_spec] + out_specs, [jax.ShapeDtypeStruct((T, D), F32)] + out_shape
        args = [pa, rb, woutg.reshape(2, W, D)] + args
    return _call(body, hosted, name=name, grid=(T // tm,), in_specs=in_specs, out_specs=out_specs, out_shape=out_shape, args=args)


def ffn_bwd_act(dh, w2, ga, gb, name, hosted=()):
    T, D = dh.shape
    F = w2.shape[0]
    tm = min(T, 512)

    def body(dh_ref, w2_ref, ga_ref, gb_ref, da_ref, db_ref, df_ref):
        df = (0.5 * dh_ref[...]).astype(BF16)
        df_ref[...] = df
        for c in range(0, F, MXU_COLS):
            cols = slice(c, c + MXU_COLS)
            ds = _dot_nt(df, w2_ref[cols, :])
            da_ref[:, cols] = (ds * ga_ref[:, cols].astype(F32)).astype(BF16)
            db_ref[:, cols] = (ds * gb_ref[:, cols].astype(F32)).astype(BF16)

    act = jax.ShapeDtypeStruct((T, F), BF16)
    act_spec = pl.BlockSpec((tm, F), lambda i: (i, 0))
    row_spec = pl.BlockSpec((tm, D), lambda i: (i, 0))
    return _call(
        body, hosted, name=name, grid=(T // tm,),
        in_specs=[row_spec, _resident((F, D)), act_spec, act_spec],
        out_specs=[act_spec, act_spec, row_spec],
        out_shape=[act, act, jax.ShapeDtypeStruct((T, D), BF16)],
        args=[dh, w2, ga, gb])


def ffn_dw(xs, y, halves, name, hosted=()):
    T, F = xs[0].shape
    D = y.shape[1]
    nx, fh = len(xs), F // halves
    tk = min(T, 512)
    nk = T // tk

    def body(*refs):
        y_ref, x_refs, o_refs, accs = refs[0], refs[1:1 + nx], refs[1 + nx:1 + 2 * nx], refs[1 + 2 * nx:]
        k = pl.program_id(1)

        @pl.when(k == 0)
        def _():
            for acc in accs:
                acc[...] = jnp.zeros_like(acc)

        yy = y_ref[...]
        for x_ref, acc in zip(x_refs, accs):
            acc[...] += _dot_tn(x_ref[...], yy)

        @pl.when(k == nk - 1)
        def _():
            for o_ref, acc in zip(o_refs, accs):
                o_ref[...] = acc[...].astype(BF16)

    out = jax.ShapeDtypeStruct((F, D), BF16)
    return _call(
        body, hosted, name=name, grid=(halves, nk),
        in_specs=[pl.BlockSpec((tk, D), lambda j, k: (k, 0))] + [pl.BlockSpec((tk, fh), lambda j, k: (k, j))] * nx,
        out_specs=[pl.BlockSpec((fh, D), lambda j, k: (j, 0))] * nx,
        out_shape=[out] * nx,
        scratch_shapes=[pltpu.VMEM((fh, D), F32)] * nx,
        args=[y] + list(xs))


def ffn_bwd_in(da, db, w1, w3, h, gain, dh, name, hosted=()):
    T, F = da.shape
    D = h.shape[1]
    tm = min(T, 512)

    def body(da_ref, db_ref, w1_ref, w3_ref, h_ref, g_ref, dh_ref, o_ref, dg_ref):
        dn = _dot(da_ref[...], w1_ref[...]) + _dot(db_ref[...], w3_ref[...])
        dhn, dg = _rmsnorm_bwd(dn, h_ref[...], g_ref[...])
        o_ref[...] = dh_ref[...] + dhn

        @pl.when(pl.program_id(0) == 0)
        def _():
            dg_ref[...] = jnp.zeros_like(dg_ref)

        dg_ref[...] += jnp.sum(dg, axis=0, keepdims=True)

    act_spec = pl.BlockSpec((tm, F), lambda i: (i, 0))
    row_spec = pl.BlockSpec((tm, D), lambda i: (i, 0))
    vec_spec = pl.BlockSpec((1, D), lambda i: (0, 0))
    return _call(
        body, hosted, name=name, grid=(T // tm,),
        in_specs=[act_spec, act_spec, _resident((F, D)), _resident((F, D)), row_spec, vec_spec, row_spec],
        out_specs=[row_spec, vec_spec],
        out_shape=[jax.ShapeDtypeStruct((T, D), F32), jax.ShapeDtypeStruct((1, D), F32)],
        args=[da, db, w1, w3, h, gain, dh])


def ffn_down_mix_in(s, w2, h, gain, wing, name, hosted=()):
    T, F = s.shape
    D = h.shape[1]
    nsh, _, Cs = wing.shape
    tm = min(T, 512)

    def body(s_ref, w2_ref, h_ref, g_ref, w_ref, hh_ref, u_ref, p_ref):
        hh = h_ref[...] + 0.5 * _dot(s_ref[...], w2_ref[...])
        hh_ref[...] = hh
        u = (hh * _rstd(hh) * g_ref[...]).astype(BF16)
        u_ref[...] = u
        for j in range(nsh):
            p_ref[:, j * Cs:(j + 1) * Cs] = _dot(u, w_ref[j])

    row_spec = pl.BlockSpec((tm, D), lambda i: (i, 0))
    return _call(
        body, hosted, name=name, grid=(T // tm,),
        in_specs=[pl.BlockSpec((tm, F), lambda i: (i, 0)), _resident((F, D)), row_spec, pl.BlockSpec((1, D), lambda i: (0, 0)),
                  _resident((nsh, D, Cs))],
        out_specs=[row_spec, row_spec, pl.BlockSpec((tm, nsh * Cs), lambda i: (i, 0))],
        out_shape=[jax.ShapeDtypeStruct((T, D), F32), jax.ShapeDtypeStruct((T, D), BF16), jax.ShapeDtypeStruct((T, nsh * Cs), F32)],
        args=[s, w2, h, gain, wing])


def mix_out_bwd(dh, woutg, a, b, name, hosted=()):
    T, D = dh.shape
    W = a.shape[1]
    nsh, Rs, _ = woutg.shape
    wout = woutg.reshape(2, W, D)
    tk = min(T, 512)
    nk = T // tk

    def body(dh_ref, w_ref, a_ref, b_ref, da_ref, db_ref, dw_ref, acc):
        k = pl.program_id(0)

        @pl.when(k == 0)
        def _():
            acc[...] = jnp.zeros_like(acc)

        dhb = dh_ref[...].astype(BF16)
        da_ref[...] = _dot_nt(dhb, w_ref[0])
        db_ref[...] = _dot_nt(dhb, w_ref[1])
        acc[0:W, :] += _dot_tn(a_ref[...], dhb)
        acc[W:2 * W, :] += _dot_tn(b_ref[...], dhb)

        @pl.when(k == nk - 1)
        def _():
            for j in range(nsh):
                dw_ref[j] = acc[j * Rs:(j + 1) * Rs, :].astype(BF16)

    return _call(
        body, hosted, name=name, grid=(nk,),
        in_specs=[pl.BlockSpec((tk, D), lambda k: (k, 0)), pl.BlockSpec((2, W, D), lambda k: (0, 0, 0)),
                  pl.BlockSpec((tk, W), lambda k: (k, 0)), pl.BlockSpec((tk, W), lambda k: (k, 0))],
        out_specs=[pl.BlockSpec((tk, W), lambda k: (k, 0)), pl.BlockSpec((tk, W), lambda k: (k, 0)),
                   pl.BlockSpec((nsh, Rs, D), lambda k: (0, 0, 0))],
        out_shape=[jax.ShapeDtypeStruct((T, W), F32), jax.ShapeDtypeStruct((T, W), F32),
                   jax.ShapeDtypeStruct((nsh, Rs, D), BF16)],
        scratch_shapes=[pltpu.VMEM((2 * W, D), F32)],
        args=[dh, wout, a, b])


def _dproj_block(g):
    return (g // N_GROUPS + N_GROUPS) % (N_GROUPS + 1), g % N_GROUPS


def mix_dwin(u, dproj, nsh, name, hosted=()):
    T, D = u.shape
    Hd = HEAD_DIM
    slabs, _, width = dproj.shape
    blocks = slabs * width // Hd
    Cs = blocks * Hd // nsh
    tk = min(T, 512)
    nk = T // tk

    def body(u_ref, d_ref, o_ref, acc):
        k = pl.program_id(0)

        @pl.when(k == 0)
        def _():
            acc[...] = jnp.zeros_like(acc)

        where = [_dproj_block(g) for g in range(blocks)]
        d = jnp.concatenate([d_ref[slab, :, col * Hd:(col + 1) * Hd] for slab, col in where], axis=1)
        acc[...] += _dot_tn(u_ref[...], d)

        @pl.when(k == nk - 1)
        def _():
            for j in range(nsh):
                o_ref[j] = acc[:, j * Cs:(j + 1) * Cs].astype(BF16)

    return _call(
        body, hosted, name=name, grid=(nk,),
        in_specs=[pl.BlockSpec((tk, D), lambda k: (k, 0)), pl.BlockSpec((slabs, tk, width), lambda k: (0, k, 0))],
        out_specs=[pl.BlockSpec((nsh, D, Cs), lambda k: (0, 0, 0))],
        out_shape=[jax.ShapeDtypeStruct((nsh, D, Cs), BF16)],
        scratch_shapes=[pltpu.VMEM((D, blocks * Hd), F32)],
        args=[u, dproj])


def mix_in_bwd(dproj, wing, h, gain, dh, name, hosted=()):
    T, D = h.shape
    nsh, _, Cs = wing.shape
    Hd = HEAD_DIM
    per = Cs // Hd
    tm = min(T, 512)

    def body(d_ref, w_ref, h_ref, g_ref, dh_ref, o_ref, dg_ref):
        def shard(j):
            blocks = [_dproj_block(per * j + i) for i in range(per)]
            return jnp.concatenate([d_ref[slab, :, col * Hd:(col + 1) * Hd] for slab, col in blocks], axis=1)

        du = _dot_nt(shard(0), w_ref[0])
        for j in range(1, nsh):
            du += _dot_nt(shard(j), w_ref[j])
        dhn, dg = _rmsnorm_bwd(du, h_ref[...], g_ref[...])
        o_ref[...] = dh_ref[...] + dhn

        @pl.when(pl.program_id(0) == 0)
        def _():
            dg_ref[...] = jnp.zeros_like(dg_ref)

        dg_ref[...] += jnp.sum(dg, axis=0, keepdims=True)

    row_spec = pl.BlockSpec((tm, D), lambda i: (i, 0))
    vec_spec = pl.BlockSpec((1, D), lambda i: (0, 0))
    return _call(
        body, hosted, name=name, grid=(T // tm,),
        in_specs=[pl.BlockSpec((dproj.shape[0], tm, dproj.shape[2]), lambda i: (0, i, 0)),
                  pl.BlockSpec((nsh, D, Cs), lambda i: (0, 0, 0)), row_spec, vec_spec, row_spec],
        out_specs=[row_spec, vec_spec],
        out_shape=[jax.ShapeDtypeStruct((T, D), F32), jax.ShapeDtypeStruct((1, D), F32)],
        args=[dproj, wing, h, gain, dh])


def _pool_window(x, group, T, trailing):
    rows = lax.broadcasted_iota(jnp.int32, x.shape, 0)

    def shifted(z, k):
        if trailing:
            return jnp.where(rows >= k, pltpu.roll(z, k, 0), 0.0)
        return jnp.where(rows < T - k, pltpu.roll(z, T - k, 0), 0.0)

    s2 = x + shifted(x, 1)
    s4 = s2 + shifted(s2, 2)
    s8 = s4 + shifted(s4, 4)
    s16 = s8 + shifted(s8, 8)
    return jnp.where(group == 0, s2, jnp.where(group == 1, s4, jnp.where(group == 2, s8, s16)))


def _pool_count(group, shape):
    rows = lax.broadcasted_iota(jnp.int32, shape, 0)
    w = jnp.where(group == 0, 2, jnp.where(group == 1, 4, jnp.where(group == 2, 8, 16)))
    return jnp.minimum(rows + 1, w).astype(F32)


def pool_fwd(proj, pool_w, pool_scale, name, hosted=()):
    T = proj.shape[0]
    Hd = HEAD_DIM

    def body(x_ref, w_ref, sc_ref, a_ref):
        g = pl.program_id(0)
        x = x_ref[...]
        pooled = _pool_window(x, g, T, True) / _pool_count(g, x.shape) - x
        a_ref[...] = (_dot(pooled.astype(BF16), w_ref[0].astype(BF16)) * sc_ref[...]).astype(BF16)

    return _call(
        body, hosted, name=name, grid=(N_GROUPS,),
        in_specs=[pl.BlockSpec((T, Hd), lambda g: (0, g)), pl.BlockSpec((1, Hd, Hd), lambda g: (g, 0, 0)),
                  pl.BlockSpec((1, Hd), lambda g: (0, g))],
        out_specs=[pl.BlockSpec((T, Hd), lambda g: (0, g))],
        out_shape=[jax.ShapeDtypeStruct((T, N_GROUPS * Hd), BF16)],
        args=[proj, pool_w, pool_scale])


def pool_bwd(proj, da, pool_w, pool_scale, name, hosted=()):
    T = proj.shape[0]
    Hd = HEAD_DIM

    def body(x_ref, da_ref, w_ref, sc_ref, dx_ref, dw_ref, dsc_ref):
        g = pl.program_id(0)
        x = x_ref[...]
        cnt = _pool_count(g, x.shape)
        pooled = (_pool_window(x, g, T, True) / cnt - x).astype(BF16)
        wb = w_ref[0].astype(BF16)
        dav = da_ref[...]
        dsc_ref[...] = jnp.sum(dav * _dot(pooled, wb), axis=0, keepdims=True)
        dout = (dav * sc_ref[...]).astype(BF16)
        dw_ref[0] = _dot_tn(pooled, dout)
        dpooled = _dot_nt(dout, wb)
        dx_ref[0] = (_pool_window(dpooled / cnt, g, T, False) - dpooled).astype(BF16)

    col_spec = pl.BlockSpec((T, Hd), lambda g: (0, g))
    return _call(
        body, hosted, name=name, grid=(N_GROUPS,),
        in_specs=[col_spec, col_spec, pl.BlockSpec((1, Hd, Hd), lambda g: (g, 0, 0)), pl.BlockSpec((1, Hd), lambda g: (0, g))],
        out_specs=[pl.BlockSpec((1, T, Hd), lambda g: (N_GROUPS, 0, g)), pl.BlockSpec((1, Hd, Hd), lambda g: (g, 0, 0)),
                   pl.BlockSpec((1, Hd), lambda g: (0, g))],
        out_shape=[jax.ShapeDtypeStruct((N_GROUPS + 1, T, N_GROUPS * Hd), BF16), jax.ShapeDtypeStruct((N_GROUPS, Hd, Hd), F32),
                   jax.ShapeDtypeStruct((1, N_GROUPS * Hd), F32)],
        args=[proj, da, pool_w, pool_scale])


def _ret_tables(T):
    Hd, C = HEAD_DIM, RET_CHUNK
    inv_freq = 1.0 / (ROPE_BASE ** (jnp.arange(0, Hd, 2, dtype=F32) / Hd))
    ang = jnp.arange(T, dtype=F32)[:, None] * inv_freq[None, :]
    cos, sin = jnp.cos(ang), jnp.sin(ang)
    cos2 = jnp.concatenate([cos, cos], axis=-1)
    sin2 = jnp.concatenate([-sin, sin], axis=-1)
    log_gamma = jnp.log1p(-jnp.exp2(-5.0 - jnp.arange(N_GROUPS, dtype=F32)))
    pos = jnp.arange(C, dtype=F32)
    rel = pos[:, None] - pos[None, :]
    intra = jnp.where(rel[None] >= 0, jnp.exp(log_gamma[:, None, None] * jnp.maximum(rel, 0.0)[None]), 0.0)
    k_tail = jnp.exp(log_gamma[:, None] * (C - 1 - pos)[None, :])
    q_head = jnp.exp(log_gamma[:, None] * (pos + 1.0)[None, :])
    chunk_decay = jnp.exp(log_gamma * C)
    wide = lambda t: jnp.broadcast_to(t[:, :, None], (N_GROUPS, C, Hd))
    return cos2, sin2, intra, wide(k_tail), wide(q_head), jnp.broadcast_to(chunk_decay[:, None, None], (N_GROUPS, 1, Hd))


def _rope(x, cos2, sin2):
    return x * cos2 + pltpu.roll(x, HEAD_DIM // 2, 1) * sin2


def _rope_t(d, cos2, sin2):
    return d * cos2 + pltpu.roll(d * sin2, HEAD_DIM // 2, 1)


def _ret_specs(T, tseg, seg_of):
    Hd, G = HEAD_DIM, N_GROUPS
    col = lambda kind: pl.BlockSpec((tseg, Hd), lambda h, s: (seg_of(s), G * kind + h))
    tab = pl.BlockSpec((T, Hd), lambda h, s: (0, 0))
    head = pl.BlockSpec((1, RET_CHUNK, Hd), lambda h, s: (h, 0, 0))
    cd = pl.BlockSpec((1, 1, Hd), lambda h, s: (h, 0, 0))
    gain = pl.BlockSpec((1, Hd), lambda h, s: (0, h))
    return col, tab, head, cd, gain


def ret_fwd(proj, ret_norm, tables, name, hosted=()):
    T = proj.shape[0]
    Hd, C, G = HEAD_DIM, RET_CHUNK, N_GROUPS
    tseg = min(T, 2048)
    nseg, nck = T // tseg, tseg // C
    scale = Hd ** -0.5
    cos2, sin2, intra, k_tail, q_head, chunk_decay = tables

    def body(q_ref, k_ref, v_ref, g_ref, gain_ref, cos_ref, sin_ref, m_ref, kt_ref, qh_ref, cd_ref,
             b_ref, o_ref, rp_ref, state):
        @pl.when(pl.program_id(1) == 0)
        def _():
            state[...] = jnp.zeros_like(state)

        def chunk(ci, carry):
            rows = pl.ds(pl.multiple_of(ci * C, C), C)
            at = pl.ds(pl.multiple_of(pl.program_id(1) * tseg + ci * C, C), C)
            cos, sin = cos_ref[at, :], sin_ref[at, :]
            qr = _rope(q_ref[rows, :], cos, sin)
            kr = _rope(k_ref[rows, :], cos, sin) * scale
            qb, kb, vb = qr.astype(BF16), kr.astype(BF16), v_ref[rows, :].astype(BF16)
            r = state[...]
            rp_ref[0, ci] = r.astype(BF16)
            sc = _dot_nt(qb, kb) * m_ref[0]
            o = _dot(sc.astype(BF16), vb) + _dot((qr * qh_ref[0]).astype(BF16), r.astype(BF16))
            state[...] = cd_ref[0] * r + _dot_tn((kr * kt_ref[0]).astype(BF16), vb)
            o_ref[rows, :] = o
            on = o * _rstd(o)
            b_ref[rows, :] = (jax.nn.silu(g_ref[rows, :]) * (on * gain_ref[...])).astype(BF16)
            return carry

        lax.fori_loop(0, nck, chunk, 0, unroll=True)

    col, tab, head, cd, gain = _ret_specs(T, tseg, lambda s: s)
    out_col = pl.BlockSpec((tseg, Hd), lambda h, s: (s, h))
    return _call(
        body, hosted, name=name, grid=(G, nseg),
        in_specs=[col(1), col(2), col(3), col(4), gain, tab, tab, head, head, head, cd],
        out_specs=[out_col, out_col, pl.BlockSpec((1, nck, Hd, Hd), lambda h, s: (h, s, 0, 0))],
        out_shape=[jax.ShapeDtypeStruct((T, G * Hd), BF16), jax.ShapeDtypeStruct((T, G * Hd), F32),
                   jax.ShapeDtypeStruct((G, T // C, Hd, Hd), BF16)],
        scratch_shapes=[pltpu.VMEM((Hd, Hd), F32)],
        args=[proj, proj, proj, proj, ret_norm, cos2, sin2, intra, k_tail, q_head, chunk_decay])


def ret_bwd(proj, db, o_pre, r_prev, ret_norm, tables, dproj, name, hosted=()):
    T = proj.shape[0]
    Hd, C, G = HEAD_DIM, RET_CHUNK, N_GROUPS
    tseg = min(T, 2048)
    nseg, nck = T // tseg, tseg // C
    scale = Hd ** -0.5
    cos2, sin2, intra, k_tail, q_head, chunk_decay = tables

    def body(q_ref, k_ref, v_ref, g_ref, db_ref, o_ref, rp_ref, gain_ref, cos_ref, sin_ref, m_ref, kt_ref, qh_ref, cd_ref,
             _, d_ref, dgain_ref, gstate):
        @pl.when(pl.program_id(1) == 0)
        def _():
            gstate[...] = jnp.zeros_like(gstate)
            dgain_ref[...] = jnp.zeros_like(dgain_ref)

        def chunk(t, carry):
            ci = nck - 1 - t
            rows = pl.ds(pl.multiple_of(ci * C, C), C)
            at = pl.ds(pl.multiple_of((nseg - 1 - pl.program_id(1)) * tseg + ci * C, C), C)
            cos, sin = cos_ref[at, :], sin_ref[at, :]
            qr = _rope(q_ref[rows, :], cos, sin)
            kr = _rope(k_ref[rows, :], cos, sin) * scale
            qb, kb, vb = qr.astype(BF16), kr.astype(BF16), v_ref[rows, :].astype(BF16)
            qhb, ktb = (qr * qh_ref[0]).astype(BF16), (kr * kt_ref[0]).astype(BF16)
            sc = (_dot_nt(qb, kb) * m_ref[0]).astype(BF16)
            o = o_ref[rows, :]
            rstd = _rstd(o)
            on = o * rstd
            gain = gain_ref[...]
            silu, dsilu = _silu_parts(g_ref[rows, :])
            dy = db_ref[rows, :]
            dgain_ref[...] += jnp.sum(dy * silu * on, axis=0, keepdims=True)
            dg = dy * on * gain * dsilu
            don = dy * silu * gain
            dob = (rstd * (don - on * jnp.mean(don * on, axis=-1, keepdims=True))).astype(BF16)
            gn = gstate[...]
            gb = gn.astype(BF16)
            da = (_dot_nt(dob, vb) * m_ref[0]).astype(BF16)
            dq = _dot(da, kb) + _dot_nt(dob, rp_ref[0, ci]) * qh_ref[0]
            dk = _dot_tn(da, qb) + _dot_nt(vb, gb) * kt_ref[0]
            dv = _dot_tn(sc, dob) + _dot(ktb, gb)
            gstate[...] = cd_ref[0] * gn + _dot_tn(qhb, dob)
            d_ref[0, rows, :] = _rope_t(dq, cos, sin).astype(BF16)
            d_ref[1, rows, :] = _rope_t(dk * scale, cos, sin).astype(BF16)
            d_ref[2, rows, :] = dv.astype(BF16)
            d_ref[3, rows, :] = dg.astype(BF16)
            return carry

        lax.fori_loop(0, nck, chunk, 0, unroll=True)

    rev = lambda s: nseg - 1 - s
    col, tab, head, cd, gain = _ret_specs(T, tseg, rev)
    act = pl.BlockSpec((tseg, Hd), lambda h, s: (rev(s), h))
    return _call(
        body, hosted, name=name, grid=(G, nseg),
        in_specs=[col(1), col(2), col(3), col(4), act, act, pl.BlockSpec((1, nck, Hd, Hd), lambda h, s: (h, rev(s), 0, 0)),
                  gain, tab, tab, head, head, head, cd, ANY],
        out_specs=[pl.BlockSpec((4, tseg, Hd), lambda h, s: (0, rev(s), h)), gain],
        out_shape=[jax.ShapeDtypeStruct(dproj.shape, BF16), jax.ShapeDtypeStruct((1, G * Hd), F32)],
        scratch_shapes=[pltpu.VMEM((Hd, Hd), F32)], aliased={14: 0},
        args=[proj, proj, proj, proj, db, o_pre, r_prev, ret_norm, cos2, sin2, intra, k_tail, q_head, chunk_decay, dproj])


def ffn_down_loss(s, w2, h, gain, target, name, hosted=()):
    T, F = s.shape
    D = h.shape[1]
    tm = min(T, 512)

    def body(s_ref, w2_ref, h_ref, g_ref, t_ref, dh_ref, loss_ref, dg_ref):
        @pl.when(pl.program_id(0) == 0)
        def _():
            loss_ref[...] = jnp.zeros_like(loss_ref)
            dg_ref[...] = jnp.zeros_like(dg_ref)

        hh = h_ref[...] + 0.5 * _dot(s_ref[...], w2_ref[...])
        gain_v = g_ref[...]
        err = hh * _rstd(hh) * gain_v - t_ref[...]
        loss_ref[...] += 0.5 * jnp.sum(jnp.mean(err * err, axis=-1, keepdims=True), axis=0, keepdims=True)
        dhn, dg = _rmsnorm_bwd(err * (1.0 / D), hh, gain_v)
        dh_ref[...] = dhn
        dg_ref[...] += jnp.sum(dg, axis=0, keepdims=True)

    row_spec = pl.BlockSpec((tm, D), lambda i: (i, 0))
    vec_spec = pl.BlockSpec((1, D), lambda i: (0, 0))
    return _call(
        body, hosted, name=name, grid=(T // tm,),
        in_specs=[pl.BlockSpec((tm, F), lambda i: (i, 0)), _resident((F, D)), row_spec, vec_spec, row_spec],
        out_specs=[row_spec, pl.BlockSpec((1, LANES), lambda i: (0, 0)), vec_spec],
        out_shape=[jax.ShapeDtypeStruct((T, D), F32), jax.ShapeDtypeStruct((1, LANES), F32), jax.ShapeDtypeStruct((1, D), F32)],
        args=[s, w2, h, gain, target])


def prereduce(grads, recvs, place, name):
    nt = len(grads)
    nsh, R, C = grads[0].shape
    rh = R // 2

    def body(place_ref, *refs):
        for t in range(nt):
            g_ref, r_ref, o_ref, own_ref = refs[2 * t], refs[2 * t + 1], refs[2 * nt + 2 * t], refs[2 * nt + 2 * t + 1]
            piece = (g_ref[...].astype(F32) + r_ref[...].astype(F32)).astype(BF16)
            o_ref[...] = piece

            @pl.when(pl.program_id(0) == place_ref[1])
            def _():
                own_ref[...] = piece

    outs = pl.pallas_call(
        body, name=name,
        grid_spec=pltpu.PrefetchScalarGridSpec(
            num_scalar_prefetch=1, grid=(nsh,),
            in_specs=[pl.BlockSpec((1, rh, C), lambda j, p: (j, p[0], 0)), pl.BlockSpec((1, rh, C), lambda j, p: (j, 0, 0))] * nt,
            out_specs=[pl.BlockSpec((1, rh, C), lambda j, p: (j, 0, 0)),
                       pl.BlockSpec((1, rh, C), lambda j, p: (p[1], p[0], 0))] * nt),
        out_shape=[jax.ShapeDtypeStruct((nsh, rh, C), BF16), jax.ShapeDtypeStruct((nsh, R, C), BF16)] * nt,
        compiler_params=pltpu.CompilerParams(vmem_limit_bytes=VMEM_LIMIT_V7X),
    )(place, *[a for pair in zip(grads, recvs) for a in pair])
    return [(outs[2 * t], outs[2 * t + 1]) for t in range(nt)]


def _adamw(w, g, m, v):
    m = ADAM_B1 * m + (1.0 - ADAM_B1) * g
    v = ADAM_B2 * v + (1.0 - ADAM_B2) * (g * g)
    m_hat = m / (1.0 - ADAM_B1 ** ADAM_STEP)
    v_hat = v / (1.0 - ADAM_B2 ** ADAM_STEP)
    return -ADAM_LR * (m_hat / (jnp.sqrt(v_hat) + ADAM_EPS) + ADAM_WD * w), m, v


def adamw_sharded(tensors, name, hosted=()):
    nt = len(tensors)
    nsh = tensors[0][0].shape[0]
    shapes = [t[0].shape[1:] for t in tensors]

    def fits(steps):
        if any(R % (steps * BF16_TILE_ROWS) for R, _ in shapes):
            return False
        return sum(2 * (R // steps) * -(-C // LANES) * LANES * (nsh * 2 + 7 * 4) for R, C in shapes) <= ADAMW_VMEM_BUDGET

    steps = min(s for s in range(1, min(R for R, _ in shapes) // BF16_TILE_ROWS + 1) if fits(s))

    def body(*refs):
        ins, outs = refs[:4 * nt], refs[4 * nt:]
        for t in range(nt):
            p_ref, w_ref, m_ref, v_ref = ins[4 * t:4 * t + 4]
            g_ref, d_ref, nm_ref, nv_ref = outs[4 * t:4 * t + 4]
            g = p_ref[0].astype(F32)
            for i in range(1, nsh):
                g += p_ref[i].astype(F32)
            g_ref[...] = g
            d_ref[...], nm_ref[...], nv_ref[...] = _adamw(w_ref[...], g, m_ref[...], v_ref[...])

    in_specs, out_specs, out_shape = [], [], []
    for R, C in shapes:
        spec = pl.BlockSpec((R // steps, C), lambda i: (i, 0))
        in_specs += [pl.BlockSpec((nsh, R // steps, C), lambda i: (0, i, 0)), spec, spec, spec]
        out_specs += [spec] * 4
        out_shape += [jax.ShapeDtypeStruct((R, C), F32)] * 4
    return _call(body, hosted, name=name, grid=(steps,), in_specs=in_specs, out_specs=out_specs, out_shape=out_shape,
                 args=[a for tensor in tensors for a in tensor])


def adamw_small(packs, params, loss_packs, name):
    n = len(packs)
    ndev = loss_packs.shape[0]

    def body(*refs):
        p_refs, loss_ref, wmv = refs[:n], refs[n], refs[n + 1:4 * n + 1]
        outs, loss_out = refs[4 * n + 1:8 * n + 1], refs[8 * n + 1]
        total = lambda r: sum((r[i] for i in range(1, ndev)), r[0])
        loss_out[...] = total(loss_ref)
        for k in range(n):
            g = total(p_refs[k])
            outs[4 * k][...] = g
            outs[4 * k + 1][...], outs[4 * k + 2][...], outs[4 * k + 3][...] = _adamw(
                wmv[3 * k][...], g, wmv[3 * k + 1][...], wmv[3 * k + 2][...])

    out_shape = [jax.ShapeDtypeStruct(p[0].shape, F32) for p in params for _ in range(4)]
    outs = pl.pallas_call(body, name=name, out_shape=out_shape + [jax.ShapeDtypeStruct(loss_packs.shape[1:], F32)],
                          compiler_params=pltpu.CompilerParams(vmem_limit_bytes=VMEM_LIMIT_V7X),
                          )(*packs, loss_packs, *[a for p in params for a in p])
    return [outs[4 * k:4 * k + 4] for k in range(n)], outs[4 * n]


BIG = ("ffn1_w1", "ffn1_w3", "ffn1_w2", "w_in", "w_out", "ffn2_w1", "ffn2_w3", "ffn2_w2")
TRANSPOSED = ("ffn1_w1", "ffn1_w3", "ffn2_w1", "ffn2_w3")
SMALL = ("pool_w", "mix_norm", "pool_scale", "ret_norm", "ffn2_norm", "final_norm", "ffn1_norm")
WEIGHTS = ("ffn1_norm", "ffn1_w1", "ffn1_w3", "ffn1_w2", "mix_norm", "w_in", "pool_w", "pool_scale", "ret_norm", "w_out",
           "ffn2_norm", "ffn2_w1", "ffn2_w3", "ffn2_w2", "final_norm")


def kernel(x, ffn1_norm, ffn1_w1, ffn1_w3, ffn1_w2, mix_norm, w_in, pool_w, pool_scale, ret_norm, w_out, ffn2_norm, ffn2_w1, ffn2_w3, ffn2_w2, final_norm, loss_target, m_ffn1_norm, m_ffn1_w1, m_ffn1_w3, m_ffn1_w2, m_mix_norm, m_w_in, m_pool_w, m_pool_scale, m_ret_norm, m_w_out, m_ffn2_norm, m_ffn2_w1, m_ffn2_w3, m_ffn2_w2, m_final_norm, v_ffn1_norm, v_ffn1_w1, v_ffn1_w3, v_ffn1_w2, v_mix_norm, v_w_in, v_pool_w, v_pool_scale, v_ret_norm, v_w_out, v_ffn2_norm, v_ffn2_w1, v_ffn2_w3, v_ffn2_w2, v_final_norm):
    w = dict(ffn1_norm=ffn1_norm, ffn1_w1=ffn1_w1, ffn1_w3=ffn1_w3, ffn1_w2=ffn1_w2, mix_norm=mix_norm, w_in=w_in, pool_w=pool_w,
             pool_scale=pool_scale, ret_norm=ret_norm, w_out=w_out, ffn2_norm=ffn2_norm, ffn2_w1=ffn2_w1, ffn2_w3=ffn2_w3,
             ffn2_w2=ffn2_w2, final_norm=final_norm)
    m = dict(ffn1_norm=m_ffn1_norm, ffn1_w1=m_ffn1_w1, ffn1_w3=m_ffn1_w3, ffn1_w2=m_ffn1_w2, mix_norm=m_mix_norm, w_in=m_w_in,
             pool_w=m_pool_w, pool_scale=m_pool_scale, ret_norm=m_ret_norm, w_out=m_w_out, ffn2_norm=m_ffn2_norm, ffn2_w1=m_ffn2_w1,
             ffn2_w3=m_ffn2_w3, ffn2_w2=m_ffn2_w2, final_norm=m_final_norm)
    v = dict(ffn1_norm=v_ffn1_norm, ffn1_w1=v_ffn1_w1, ffn1_w3=v_ffn1_w3, ffn1_w2=v_ffn1_w2, mix_norm=v_mix_norm, w_in=v_w_in,
             pool_w=v_pool_w, pool_scale=v_pool_scale, ret_norm=v_ret_norm, w_out=v_w_out, ffn2_norm=v_ffn2_norm, ffn2_w1=v_ffn2_w1,
             ffn2_w3=v_ffn2_w3, ffn2_w2=v_ffn2_w2, final_norm=v_final_norm)
    xs, target = x[0], loss_target[0]
    T = xs.shape[0]
    tables = _ret_tables(T)
    place = jnp.stack([lax.axis_index("c"), 2 * lax.axis_index("x") + lax.axis_index("y")]).astype(jnp.int32)
    local = lambda d, k: jnp.transpose(d[k][0]) if k in TRANSPOSED else d[k][0]
    result = lambda o, k: jnp.transpose(o)[None] if k in TRANSPOSED else o[None]
    first = ("ffn1_w1", "ffn1_w3")
    sh = {k: local(w, k).astype(BF16) for k in first}
    gather = lambda *names: [ChipExchange([sh[k] for k in names], False)]
    wg, grad, delta, new_m, new_v = {}, {}, {}, {}, {}

    def update(names, pieces, name, hosted=()):
        outs, extras = adamw_sharded([(p, local(w, k), local(m, k), local(v, k)) for k, p in zip(names, pieces)], name, hosted)
        for t, k in enumerate(names):
            grad[k], delta[k], new_m[k], new_v[k] = [result(o, k) for o in outs[4 * t:4 * t + 4]]
        return extras

    def reduce_in_chip(name, *pairs):
        reduced = prereduce([p for p, _ in pairs], [r for _, r in pairs], place, "prereduce_" + name)
        return reduced[0] if len(pairs) == 1 else reduced

    scatter = lambda *reduced: ChipExchange([r[0] for r in reduced], True, [r[1] for r in reduced])
    whole = lambda k: wg[k].reshape(-1, wg[k].shape[-1])
    sharded = lambda g: g.reshape(N_CHIPS, -1, g.shape[-1])

    later = [k for k in BIG if k not in first]
    casts, ((wg["ffn1_w1"], wg["ffn1_w3"]),) = cast_shards([local(w, k) for k in later], "cast_gather_ffn1", gather(*first))
    sh.update(zip(later, casts))
    (n1, ga1, gb1, s1), ((wg["ffn1_w2"], wg["w_in"]),) = ffn_up(
        xs, ffn1_norm, whole("ffn1_w1"), whole("ffn1_w3"), "ffn1_up", gather("ffn1_w2", "w_in"))
    (h1, u, proj), ((wg["w_out"], wg["ffn2_w1"]),) = ffn_down_mix_in(
        s1, whole("ffn1_w2"), xs, mix_norm, wg["w_in"], "ffn1_down_mix_in", gather("w_out", "ffn2_w1"))
    (pa,), _ = pool_fwd(proj, pool_w[0], pool_scale, "pool_fwd")
    (rb, o_pre, r_prev), ((wg["ffn2_w3"],),) = ret_fwd(proj, ret_norm, tables, "ret_fwd", gather("ffn2_w3"))
    (h2, n2, ga2, gb2, s2), ((wg["ffn2_w2"],),) = ffn_up(
        h1, ffn2_norm, whole("ffn2_w1"), whole("ffn2_w3"), "mix_out_ffn2_up", gather("ffn2_w2"), mixed=(pa, rb, wg["w_out"]))
    (dh3, loss, d_final), _ = ffn_down_loss(s2, whole("ffn2_w2"), h2, final_norm[None], target, "ffn2_down_loss")

    (da2, db2, df2), _ = ffn_bwd_act(dh3, whole("ffn2_w2"), ga2, gb2, "ffn2_bwd_act")
    (g_f2w2,), _ = ffn_dw([s2], df2, 1, "ffn2_dw2")
    g_f2w2 = sharded(g_f2w2)
    (g_f2w1, g_f2w3), ((r_f2w2,),) = ffn_dw([da2, db2], n2, 2, "ffn2_dw13", [SiblingExchange([g_f2w2])])
    g_f2w1, g_f2w3 = sharded(g_f2w1), sharded(g_f2w3)
    p_f2w2 = reduce_in_chip("ffn2_w2", (g_f2w2, r_f2w2))
    (dh2, d_ffn2), ((q_f2w2,), (r_f2w1, r_f2w3)) = ffn_bwd_in(
        da2, db2, whole("ffn2_w1"), whole("ffn2_w3"), h2, ffn2_norm, dh3, "ffn2_bwd_in",
        [scatter(p_f2w2), SiblingExchange([g_f2w1, g_f2w3])])
    p_f2w1, p_f2w3 = reduce_in_chip("ffn2_w13", (g_f2w1, r_f2w1), (g_f2w3, r_f2w3))
    (dpa, drb, g_wout), _ = mix_out_bwd(dh2, wg["w_out"], pa, rb, "mix_out_bwd")
    (dproj, d_pool_w, d_pool_scale), _ = pool_bwd(proj, dpa, pool_w[0], pool_scale, "pool_bwd")
    (dproj, d_ret_norm), ((q_f2w1, q_f2w3), (r_wout,)) = ret_bwd(
        proj, drb, o_pre, r_prev, ret_norm, tables, dproj, "ret_bwd", [scatter(p_f2w1, p_f2w3), SiblingExchange([g_wout])])
    p_wout = reduce_in_chip("w_out", (g_wout, r_wout))
    (g_win,), ((q_wout,),) = mix_dwin(u, dproj, N_CHIPS, "mix_dwin", [scatter(p_wout)])
    (dh1, d_mix), ((r_win,),) = mix_in_bwd(dproj, wg["w_in"], h1, mix_norm, dh2, "mix_in_bwd", [SiblingExchange([g_win])])
    p_win = reduce_in_chip("w_in", (g_win, r_win))
    (da1, db1, df1), ((q_win,),) = ffn_bwd_act(dh1, whole("ffn1_w2"), ga1, gb1, "ffn1_bwd_act", [scatter(p_win)])
    d_small = {"pool_w": d_pool_w.reshape(-1, LANES), "mix_norm": d_mix, "pool_scale": d_pool_scale, "ret_norm": d_ret_norm,
               "ffn2_norm": d_ffn2, "final_norm": d_final}
    (g_f1w1, g_f1w3), (packs,) = ffn_dw([da1, db1], n1, 2, "ffn1_dw13", [AllExchange([d_small[k] for k in SMALL[:-1]] + [loss])])
    g_f1w1, g_f1w3 = sharded(g_f1w1), sharded(g_f1w3)
    (g_f1w2,), ((r_f1w1, r_f1w3),) = ffn_dw([s1], df1, 1, "ffn1_dw2", [SiblingExchange([g_f1w1, g_f1w3])])
    g_f1w2 = sharded(g_f1w2)
    p_f1w1, p_f1w3 = reduce_in_chip("ffn1_w13", (g_f1w1, r_f1w1), (g_f1w3, r_f1w3))
    (dx, d_ffn1), ((q_f1w1, q_f1w3), (r_f1w2,)) = ffn_bwd_in(
        da1, db1, whole("ffn1_w1"), whole("ffn1_w3"), xs, ffn1_norm, dh1, "ffn1_bwd_in",
        [scatter(p_f1w1, p_f1w3), SiblingExchange([g_f1w2])])
    p_f1w2 = reduce_in_chip("ffn1_w2", (g_f1w2, r_f1w2))

    (q_f1w2,), (late,) = update(["w_in", "w_out", "ffn2_w2"], [q_win, q_wout, q_f2w2], "adamw_mix_w2",
                                [scatter(p_f1w2), AllExchange([d_ffn1])])
    update(["ffn2_w1", "ffn2_w3", "ffn1_w1", "ffn1_w3"], [q_f2w1, q_f2w3, q_f1w1, q_f1w3], "adamw_w13")
    update(["ffn1_w2"], [q_f1w2], "adamw_ffn1_w2")
    flat = lambda t, k: t[k].reshape(-1, LANES) if k == "pool_w" else t[k].reshape(1, -1)
    updated, loss_sum = adamw_small(packs[:-1] + [late], [[flat(t, k) for t in (w, m, v)] for k in SMALL], packs[-1], "adamw_small")
    for k, outs in zip(SMALL, updated):
        grad[k], delta[k], new_m[k], new_v[k] = [o.reshape(w[k].shape) for o in outs]
    loss = loss_sum[0, 0]

    return (loss, dx[None], *[grad[k] for k in WEIGHTS], *[delta[k] for k in WEIGHTS],
            *[new_m[k] for k in WEIGHTS], *[new_v[k] for k in WEIGHTS])
```

```python
import math

import jax
import jax.numpy as jnp
from jax import lax
from jax.experimental import pallas as pl
from jax.experimental.pallas import tpu as pltpu

F32 = jnp.float32
BF16 = jnp.bfloat16

EPS = 1e-6
LANES = 128
BF16_TILE_ROWS = 16
N_CHIPS = 4
N_GROUPS = 4
HEAD_DIM = 128
RET_CHUNK = 128
ROPE_BASE = 10000.0
ADAM_LR, ADAM_B1, ADAM_B2, ADAM_EPS, ADAM_WD, ADAM_STEP = 0.001, 0.9, 0.999, 1e-08, 0.01, 10
VMEM_LIMIT_V7X = 56 * 1024 * 1024
ADAMW_VMEM_BUDGET = 32 * 1024 * 1024
MESH = pl.DeviceIdType.MESH
ANY = pl.BlockSpec(memory_space=pl.ANY)


def _dot(a, b):
    return jnp.dot(a, b, preferred_element_type=F32)


def _dot_nt(a, b):
    return lax.dot_general(a, b, (((1,), (1,)), ((), ())), preferred_element_type=F32)


def _dot_tn(a, b):
    return lax.dot_general(a, b, (((0,), (0,)), ((), ())), preferred_element_type=F32)


def _rstd(h):
    return lax.rsqrt(jnp.mean(h * h, axis=-1, keepdims=True) + EPS)


def _rmsnorm_bwd(dn, h, gain):
    r = _rstd(h)
    nh = h * r
    dnh = dn * gain
    dh = r * (dnh - nh * jnp.mean(dnh * nh, axis=-1, keepdims=True))
    return dh, dn * nh


def _silu_parts(a):
    sig = jax.nn.sigmoid(a)
    silu = a * sig
    return silu, sig + silu * (1.0 - sig)


def _mesh_pos():
    return lax.axis_index("x"), lax.axis_index("y"), lax.axis_index("c")


class ChipExchange:
    def __init__(self, srcs, scatter, placed=()):
        n = len(srcs)
        self.inputs, self.scatter, self.n, self.reach = list(srcs) + list(placed), scatter, n, REACH_CHIPS
        self.aliases = {n + t: t for t in range(n)} if scatter else {}
        self.half_rows = [s.shape[1] if scatter else s.shape[0] // 2 for s in srcs]
        self.out_shape = [jax.ShapeDtypeStruct((N_CHIPS, 2 * rh, s.shape[-1]), s.dtype) for s, rh in zip(srcs, self.half_rows)]
        if scatter:
            self.out_shape += [jax.ShapeDtypeStruct((2, rh // 2, s.shape[-1]), s.dtype) for s, rh in zip(srcs, self.half_rows)]
        dma = pltpu.SemaphoreType.DMA
        self.sems = [dma((4 * n,)), dma((4 * n,)), dma((2 * n,)), dma((2 * n,)), dma((4 * n,)), dma((4 * n,))]

    def _copies(self, src, out, sems):
        hop1_send, hop1_recv, hop2_send, hop2_recv, d2d_send, d2d_recv = sems
        x, y, c = _mesh_pos()
        me, dg = 2 * x + y, 2 * (1 - x) + (1 - y)
        sibling = (x, y, 1 - c)
        n = self.n
        mine, theirs = c, 1 - c

        def nb(a):
            nx, ny = x ^ (1 - a), y ^ a
            return 2 * nx + ny, (nx, ny, c)

        def remote(s, d, send, recv, k, to):
            return pltpu.make_async_remote_copy(src_ref=s, dst_ref=d, send_sem=send.at[k], recv_sem=recv.at[k],
                                                device_id=to, device_id_type=MESH)

        class Copies:
            def slot(_, t, chip, half):
                rh = self.half_rows[t]
                return out[t].at[chip, pl.ds(half * rh, rh), :]

            def quarter(_, t, chip, q):
                qh = self.half_rows[t] // 2
                return out[t].at[chip, pl.ds(mine * 2 * qh + q * qh, qh), :]

            def own_shard(k, t):
                return remote(src[t], out[t].at[me], d2d_send, d2d_recv, 4 * t + 3, sibling)

            def hop1(k, t, a, transit=False):
                rh = self.half_rows[t]
                chip, to = nb(a)
                if transit:
                    piece = src[t].at[dg, pl.ds(a * (rh // 2), rh // 2), :]
                    return remote(piece, out[n + t].at[a], hop1_send, hop1_recv, 4 * t + 2 + a, to)
                piece = src[t].at[chip] if self.scatter else src[t].at[pl.ds(mine * rh, rh), :]
                return remote(piece, k.slot(t, me, mine), hop1_send, hop1_recv, 4 * t + a, to)

            def landed1(k, t, a, transit=False):
                here = out[n + t].at[a] if transit else k.slot(t, nb(a)[0], mine)
                return remote(here, here, hop1_send, hop1_recv, 4 * t + (2 if transit else 0) + a, sibling)

            def hop2(k, t, q):
                origin, to = nb(q)[0], nb(1 - q)[1]
                piece = out[n + t].at[q] if self.scatter else k.quarter(t, origin, q)
                return remote(piece, k.quarter(t, origin, q), hop2_send, hop2_recv, 2 * t + q, to)

            def landed2(k, t, q):
                here = k.quarter(t, dg, q)
                return remote(here, here, hop2_send, hop2_recv, 2 * t + q, sibling)

            def d2d(k, t, p, chip, own=False, arriving=False):
                if arriving:
                    there = k.slot(t, chip, theirs)
                    return remote(there, there, d2d_send, d2d_recv, 4 * t + p, sibling)
                piece = src[t].at[me] if own else k.slot(t, chip, mine)
                return remote(piece, k.slot(t, chip, mine), d2d_send, d2d_recv, 4 * t + p, sibling)

        return Copies(), nb, me, dg, c

    def start(self, src, out, sems):
        k, nb, me, dg, c = self._copies(src, out, sems)
        for t in range(self.n):
            for first in range(2):
                a = first ^ c
                k.hop1(t, a).start()
                if self.scatter:
                    k.hop1(t, a, transit=True).start()
            if self.scatter:
                k.d2d(t, 3, me, own=True).start()
            else:
                k.own_shard(t).start()

    def mid(self, src, out, sems):
        k, nb, me, dg, c = self._copies(src, out, sems)
        for t in range(self.n):
            for first in range(2):
                a = first ^ c
                if self.scatter:
                    k.landed1(t, a, transit=True).wait_recv()
                    k.hop2(t, a).start()
                k.landed1(t, a).wait_recv()
                if not self.scatter:
                    k.hop2(t, a).start()
                k.d2d(t, a, nb(a)[0]).start()

    def finish(self, src, out, sems):
        k, nb, me, dg, c = self._copies(src, out, sems)
        for t in range(self.n):
            for q in range(2):
                k.landed2(t, q).wait_recv()
            k.d2d(t, 2, dg).start()
        for t in range(self.n):
            for a in range(2):
                k.d2d(t, a, nb(a)[0], arriving=True).wait_recv()
            k.d2d(t, 2, dg, arriving=True).wait_recv()
            if self.scatter:
                k.d2d(t, 3, me, arriving=True).wait_recv()
        for t in range(self.n):
            for a in range(2):
                k.hop1(t, a).wait_send()
                if self.scatter:
                    k.hop1(t, a, transit=True).wait_send()
                k.hop2(t, a).wait_send()
                k.d2d(t, a, nb(a)[0]).wait_send()
            k.d2d(t, 2, dg).wait_send()
            if self.scatter:
                k.d2d(t, 3, me, own=True).wait_send()
            else:
                k.own_shard(t).wait()


class SiblingExchange:
    def __init__(self, grads):
        self.inputs, self.n, self.aliases, self.reach = list(grads), len(grads), {}, REACH_SIBLING
        self.half_rows = [g.shape[1] // 2 for g in grads]
        self.out_shape = [jax.ShapeDtypeStruct((g.shape[0], rh, g.shape[2]), g.dtype) for g, rh in zip(grads, self.half_rows)]
        self.sems = [pltpu.SemaphoreType.DMA((self.n,)), pltpu.SemaphoreType.DMA((self.n,))]

    def _plan(self, src, out, sems):
        x, y, c = _mesh_pos()
        return [pltpu.make_async_remote_copy(
            src_ref=src[t].at[:, pl.ds((1 - c) * self.half_rows[t], self.half_rows[t]), :], dst_ref=out[t],
            send_sem=sems[0].at[t], recv_sem=sems[1].at[t], device_id=(x, y, 1 - c), device_id_type=MESH) for t in range(self.n)]

    def start(self, src, out, sems):
        for cp in self._plan(src, out, sems):
            cp.start()

    def mid(self, src, out, sems):
        pass

    def finish(self, src, out, sems):
        for cp in self._plan(src, out, sems):
            cp.wait()


REACH_SIBLING, REACH_CHIPS, REACH_ALL = 0, 1, 2


def _entry_barrier(reach):
    x, y, c = _mesh_pos()
    peers = [(x, y, 1 - c)]
    if reach == REACH_CHIPS:
        peers += [(1 - x, y, c), (x, 1 - y, c)]
    elif reach == REACH_ALL:
        peers = [(x ^ dx, y ^ dy, c ^ dc) for dx in (0, 1) for dy in (0, 1) for dc in (0, 1)][1:]
    barrier = pltpu.get_barrier_semaphore()
    for peer in peers:
        pl.semaphore_signal(barrier, inc=1, device_id=peer, device_id_type=MESH)
    pl.semaphore_wait(barrier, len(peers))


def _call(body, hosted=(), *, name, in_specs, out_specs, out_shape, args, grid=(), scratch_shapes=(), aliased=None):
    n_in, n_out, n_scr = len(in_specs), len(out_specs), len(scratch_shapes)
    total = math.prod(grid)
    mid_step = max(0, total // 2 - 1)

    def full(*refs):
        pos = [0]

        def take(k):
            pos[0] += k
            return refs[pos[0] - k:pos[0]]

        ins, h_in = take(n_in), [take(len(h.inputs)) for h in hosted]
        outs, h_out = take(n_out), [take(len(h.out_shape)) for h in hosted]
        scr, h_sem = take(n_scr), [take(len(h.sems)) for h in hosted]
        step = 0
        for axis, size in enumerate(grid):
            step = step * size + pl.program_id(axis)

        def phase(at, method):
            if not hosted:
                return

            def run():
                if method == "start":
                    _entry_barrier(reach)
                for h, s, o, m in zip(hosted, h_in, h_out, h_sem):
                    getattr(h, method)(s, o, m)

            if total == 1:
                run()
            else:
                pl.when(step == at)(run)

        phase(0, "start")
        body(*ins, *outs, *scr)
        phase(mid_step, "mid")
        phase(total - 1, "finish")

    aliases, i0, o0 = dict(aliased or {}), n_in, n_out
    for h in hosted:
        aliases.update({i0 + i: o0 + o for i, o in h.aliases.items()})
        i0, o0 = i0 + len(h.inputs), o0 + len(h.out_shape)
    reach = max((h.reach for h in hosted), default=None)
    params = dict(vmem_limit_bytes=VMEM_LIMIT_V7X)
    if hosted:
        params["collective_id"] = reach
    results = pl.pallas_call(
        full, name=name, grid=grid,
        in_specs=list(in_specs) + [ANY] * (i0 - n_in),
        out_specs=list(out_specs) + [ANY] * (o0 - n_out),
        out_shape=list(out_shape) + [s for h in hosted for s in h.out_shape],
        scratch_shapes=list(scratch_shapes) + [s for h in hosted for s in h.sems],
        input_output_aliases=aliases,
        compiler_params=pltpu.CompilerParams(**params),
    )(*args, *[s for h in hosted for s in h.inputs])
    outs, extras, pos = list(results[:n_out]), [], n_out
    for h in hosted:
        extras.append(list(results[pos:pos + h.n]))
        pos += len(h.out_shape)
    return outs, extras


def cast_shards(shards, name, hosted=()):
    n = len(shards)

    def body(*refs):
        for x_ref, o_ref in zip(refs[:n], refs[n:]):
            o_ref[...] = x_ref[...].astype(BF16)

    whole = lambda s: pl.BlockSpec(s.shape, lambda: (0,) * s.ndim)
    return _call(body, hosted, name=name, in_specs=[whole(s) for s in shards], out_specs=[whole(s) for s in shards],
                 out_shape=[jax.ShapeDtypeStruct(s.shape, BF16) for s in shards], args=list(shards))


class AllExchange:
    def __init__(self, arrays):
        n = len(arrays)
        self.inputs, self.n, self.aliases, self.reach = list(arrays), n, {}, REACH_ALL
        self.out_shape = [jax.ShapeDtypeStruct((2 * N_CHIPS,) + a.shape, a.dtype) for a in arrays]
        self.sems = [pltpu.SemaphoreType.DMA((n,)), pltpu.SemaphoreType.DMA((7 * n,)), pltpu.SemaphoreType.DMA((7 * n,))]

    def _copies(self, src, out, sems):
        local_sem, send_sem, recv_sem = sems
        x, y, c = _mesh_pos()
        me = 4 * x + 2 * y + c
        peers = [(x ^ dx, y ^ dy, c ^ dc) for dx in (0, 1) for dy in (0, 1) for dc in (0, 1)][1:]
        remote = lambda s, d, k, to: pltpu.make_async_remote_copy(
            src_ref=s, dst_ref=d, send_sem=send_sem.at[k], recv_sem=recv_sem.at[k], device_id=to, device_id_type=MESH)
        sends, landed, local = [], [], []
        for t in range(self.n):
            local.append(pltpu.make_async_copy(src[t], out[t].at[me], local_sem.at[t]))
            for p, (px, py, pc) in enumerate(peers):
                sends.append(remote(src[t], out[t].at[me], 7 * t + p, (px, py, pc)))
                here = out[t].at[4 * px + 2 * py + pc]
                landed.append(remote(here, here, 7 * t + p, (px, py, pc)))
        return sends, landed, local

    def start(self, src, out, sems):
        sends, _, local = self._copies(src, out, sems)
        for cp in sends + local:
            cp.start()

    def mid(self, src, out, sems):
        pass

    def finish(self, src, out, sems):
        sends, landed, local = self._copies(src, out, sems)
        for cp in landed:
            cp.wait_recv()
        for cp in sends:
            cp.wait_send()
        for cp in local:
            cp.wait()


MXU_COLS = 256


def _resident(shape):
    return pl.BlockSpec(shape, lambda *_: (0,) * len(shape), pipeline_mode=pl.Buffered(1))


def ffn_up(h, gain, w1, w3, name, hosted=(), mixed=None):
    T, D = h.shape
    F = w1.shape[0]
    tm = min(T, 512)

    def body(*refs):
        if mixed is None:
            h_ref, g_ref, w1_ref, w3_ref, n_ref, ga_ref, gb_ref, s_ref = refs
            hh = h_ref[...]
        else:
            pa_ref, rb_ref, wo_ref, h_ref, g_ref, w1_ref, w3_ref, hh_ref, n_ref, ga_ref, gb_ref, s_ref = refs
            hh = h_ref[...] + _dot(pa_ref[...], wo_ref[0]) + _dot(rb_ref[...], wo_ref[1])
            hh_ref[...] = hh
        n = (hh * _rstd(hh) * g_ref[...]).astype(BF16)
        n_ref[...] = n
        for c in range(0, F, MXU_COLS):
            cols = slice(c, c + MXU_COLS)
            a = _dot_nt(n, w1_ref[cols, :])
            b = _dot_nt(n, w3_ref[cols, :])
            silu, dsilu = _silu_parts(a)
            ga_ref[:, cols] = (b * dsilu).astype(BF16)
            gb_ref[:, cols] = silu.astype(BF16)
            s_ref[:, cols] = (silu * b).astype(BF16)

    act = jax.ShapeDtypeStruct((T, F), BF16)
    act_spec = pl.BlockSpec((tm, F), lambda i: (i, 0))
    row_spec = pl.BlockSpec((tm, D), lambda i: (i, 0))
    in_specs = [row_spec, pl.BlockSpec((1, D), lambda i: (0, 0)), _resident((F, D)), _resident((F, D))]
    out_specs, out_shape, args = [row_spec, act_spec, act_spec, act_spec], [jax.ShapeDtypeStruct((T, D), BF16), act, act, act], [h, gain, w1, w3]
    if mixed is not None:
        pa, rb, woutg = mixed
        W = pa.shape[1]
        in_specs = [pl.BlockSpec((tm, W), lambda i: (i, 0))] * 2 + [_resident((2, W, D))] + in_specs
        out_specs, out_shape = [row_spec] + out_specs, [jax.ShapeDtypeStruct((T, D), F32)] + out_shape
        args = [pa, rb, woutg.reshape(2, W, D)] + args
    return _call(body, hosted, name=name, grid=(T // tm,), in_specs=in_specs, out_specs=out_specs, out_shape=out_shape, args=args)


def ffn_bwd_act(dh, w2, ga, gb, name, hosted=()):
    T, D = dh.shape
    F = w2.shape[0]
    tm = min(T, 512)

    def body(dh_ref, w2_ref, ga_ref, gb_ref, da_ref, db_ref, df_ref):
        df = (0.5 * dh_ref[...]).astype(BF16)
        df_ref[...] = df
        for c in range(0, F, MXU_COLS):
            cols = slice(c, c + MXU_COLS)
            ds = _dot_nt(df, w2_ref[cols, :])
            da_ref[:, cols] = (ds * ga_ref[:, cols].astype(F32)).astype(BF16)
            db_ref[:, cols] = (ds * gb_ref[:, cols].astype(F32)).astype(BF16)

    act = jax.ShapeDtypeStruct((T, F), BF16)
    act_spec = pl.BlockSpec((tm, F), lambda i: (i, 0))
    row_spec = pl.BlockSpec((tm, D), lambda i: (i, 0))
    return _call(
        body, hosted, name=name, grid=(T // tm,),
        in_specs=[row_spec, _resident((F, D)), act_spec, act_spec],
        out_specs=[act_spec, act_spec, row_spec],
        out_shape=[act, act, jax.ShapeDtypeStruct((T, D), BF16)],
        args=[dh, w2, ga, gb])


def ffn_dw(xs, y, halves, name, hosted=()):
    T, F = xs[0].shape
    D = y.shape[1]
    nx, fh = len(xs), F // halves
    tk = min(T, 512)
    nk = T // tk

    def body(*refs):
        y_ref, x_refs, o_refs, accs = refs[0], refs[1:1 + nx], refs[1 + nx:1 + 2 * nx], refs[1 + 2 * nx:]
        k = pl.program_id(1)

        @pl.when(k == 0)
        def _():
            for acc in accs:
                acc[...] = jnp.zeros_like(acc)

        yy = y_ref[...]
        for x_ref, acc in zip(x_refs, accs):
            acc[...] += _dot_tn(x_ref[...], yy)

        @pl.when(k == nk - 1)
        def _():
            for o_ref, acc in zip(o_refs, accs):
                o_ref[...] = acc[...].astype(BF16)

    out = jax.ShapeDtypeStruct((F, D), BF16)
    return _call(
        body, hosted, name=name, grid=(halves, nk),
        in_specs=[pl.BlockSpec((tk, D), lambda j, k: (k, 0))] + [pl.BlockSpec((tk, fh), lambda j, k: (k, j))] * nx,
        out_specs=[pl.BlockSpec((fh, D), lambda j, k: (j, 0))] * nx,
        out_shape=[out] * nx,
        scratch_shapes=[pltpu.VMEM((fh, D), F32)] * nx,
        args=[y] + list(xs))


def ffn_bwd_in(da, db, w1, w3, h, gain, dh, name, hosted=()):
    T, F = da.shape
    D = h.shape[1]
    tm = min(T, 512)

    def body(da_ref, db_ref, w1_ref, w3_ref, h_ref, g_ref, dh_ref, o_ref, dg_ref):
        dn = _dot(da_ref[...], w1_ref[...]) + _dot(db_ref[...], w3_ref[...])
        dhn, dg = _rmsnorm_bwd(dn, h_ref[...], g_ref[...])
        o_ref[...] = dh_ref[...] + dhn

        @pl.when(pl.program_id(0) == 0)
        def _():
            dg_ref[...] = jnp.zeros_like(dg_ref)

        dg_ref[...] += jnp.sum(dg, axis=0, keepdims=True)

    act_spec = pl.BlockSpec((tm, F), lambda i: (i, 0))
    row_spec = pl.BlockSpec((tm, D), lambda i: (i, 0))
    vec_spec = pl.BlockSpec((1, D), lambda i: (0, 0))
    return _call(
        body, hosted, name=name, grid=(T // tm,),
        in_specs=[act_spec, act_spec, _resident((F, D)), _resident((F, D)), row_spec, vec_spec, row_spec],
        out_specs=[row_spec, vec_spec],
        out_shape=[jax.ShapeDtypeStruct((T, D), F32), jax.ShapeDtypeStruct((1, D), F32)],
        args=[da, db, w1, w3, h, gain, dh])


def ffn_down_mix_in(s, w2, h, gain, wing, name, hosted=()):
    T, F = s.shape
    D = h.shape[1]
    nsh, _, Cs = wing.shape
    tm = min(T, 512)

    def body(s_ref, w2_ref, h_ref, g_ref, w_ref, hh_ref, u_ref, p_ref):
        hh = h_ref[...] + 0.5 * _dot(s_ref[...], w2_ref[...])
        hh_ref[...] = hh
        u = (hh * _rstd(hh) * g_ref[...]).astype(BF16)
        u_ref[...] = u
        for j in range(nsh):
            p_ref[:, j * Cs:(j + 1) * Cs] = _dot(u, w_ref[j])

    row_spec = pl.BlockSpec((tm, D), lambda i: (i, 0))
    return _call(
        body, hosted, name=name, grid=(T // tm,),
        in_specs=[pl.BlockSpec((tm, F), lambda i: (i, 0)), _resident((F, D)), row_spec, pl.BlockSpec((1, D), lambda i: (0, 0)),
                  _resident((nsh, D, Cs))],
        out_specs=[row_spec, row_spec, pl.BlockSpec((tm, nsh * Cs), lambda i: (i, 0))],
        out_shape=[jax.ShapeDtypeStruct((T, D), F32), jax.ShapeDtypeStruct((T, D), BF16), jax.ShapeDtypeStruct((T, nsh * Cs), F32)],
        args=[s, w2, h, gain, wing])


def mix_out_bwd(dh, woutg, a, b, name, hosted=()):
    T, D = dh.shape
    W = a.shape[1]
    nsh, Rs, _ = woutg.shape
    wout = woutg.reshape(2, W, D)
    tk = min(T, 512)
    nk = T // tk

    def body(dh_ref, w_ref, a_ref, b_ref, da_ref, db_ref, dw_ref, acc):
        k = pl.program_id(0)

        @pl.when(k == 0)
        def _():
            acc[...] = jnp.zeros_like(acc)

        dhb = dh_ref[...].astype(BF16)
        da_ref[...] = _dot_nt(dhb, w_ref[0])
        db_ref[...] = _dot_nt(dhb, w_ref[1])
        acc[0:W, :] += _dot_tn(a_ref[...], dhb)
        acc[W:2 * W, :] += _dot_tn(b_ref[...], dhb)

        @pl.when(k == nk - 1)
        def _():
            for j in range(nsh):
                dw_ref[j] = acc[j * Rs:(j + 1) * Rs, :].astype(BF16)

    return _call(
        body, hosted, name=name, grid=(nk,),
        in_specs=[pl.BlockSpec((tk, D), lambda k: (k, 0)), pl.BlockSpec((2, W, D), lambda k: (0, 0, 0)),
                  pl.BlockSpec((tk, W), lambda k: (k, 0)), pl.BlockSpec((tk, W), lambda k: (k, 0))],
        out_specs=[pl.BlockSpec((tk, W), lambda k: (k, 0)), pl.BlockSpec((tk, W), lambda k: (k, 0)),
                   pl.BlockSpec((nsh, Rs, D), lambda k: (0, 0, 0))],
        out_shape=[jax.ShapeDtypeStruct((T, W), F32), jax.ShapeDtypeStruct((T, W), F32),
                   jax.ShapeDtypeStruct((nsh, Rs, D), BF16)],
        scratch_shapes=[pltpu.VMEM((2 * W, D), F32)],
        args=[dh, wout, a, b])


def _dproj_block(g):
    return (g // N_GROUPS + N_GROUPS) % (N_GROUPS + 1), g % N_GROUPS


def mix_dwin(u, dproj, nsh, name, hosted=()):
    T, D = u.shape
    Hd = HEAD_DIM
    slabs, _, width = dproj.shape
    blocks = slabs * width // Hd
    Cs = blocks * Hd // nsh
    tk = min(T, 512)
    nk = T // tk

    def body(u_ref, d_ref, o_ref, acc):
        k = pl.program_id(0)

        @pl.when(k == 0)
        def _():
            acc[...] = jnp.zeros_like(acc)

        where = [_dproj_block(g) for g in range(blocks)]
        d = jnp.concatenate([d_ref[slab, :, col * Hd:(col + 1) * Hd] for slab, col in where], axis=1)
        acc[...] += _dot_tn(u_ref[...], d)

        @pl.when(k == nk - 1)
        def _():
            for j in range(nsh):
                o_ref[j] = acc[:, j * Cs:(j + 1) * Cs].astype(BF16)

    return _call(
        body, hosted, name=name, grid=(nk,),
        in_specs=[pl.BlockSpec((tk, D), lambda k: (k, 0)), pl.BlockSpec((slabs, tk, width), lambda k: (0, k, 0))],
        out_specs=[pl.BlockSpec((nsh, D, Cs), lambda k: (0, 0, 0))],
        out_shape=[jax.ShapeDtypeStruct((nsh, D, Cs), BF16)],
        scratch_shapes=[pltpu.VMEM((D, blocks * Hd), F32)],
        args=[u, dproj])


def mix_in_bwd(dproj, wing, h, gain, dh, name, hosted=()):
    T, D = h.shape
    nsh, _, Cs = wing.shape
    Hd = HEAD_DIM
    per = Cs // Hd
    tm = min(T, 512)

    def body(d_ref, w_ref, h_ref, g_ref, dh_ref, o_ref, dg_ref):
        def shard(j):
            blocks = [_dproj_block(per * j + i) for i in range(per)]
            return jnp.concatenate([d_ref[slab, :, col * Hd:(col + 1) * Hd] for slab, col in blocks], axis=1)

        du = _dot_nt(shard(0), w_ref[0])
        for j in range(1, nsh):
            du += _dot_nt(shard(j), w_ref[j])
        dhn, dg = _rmsnorm_bwd(du, h_ref[...], g_ref[...])
        o_ref[...] = dh_ref[...] + dhn

        @pl.when(pl.program_id(0) == 0)
        def _():
            dg_ref[...] = jnp.zeros_like(dg_ref)

        dg_ref[...] += jnp.sum(dg, axis=0, keepdims=True)

    row_spec = pl.BlockSpec((tm, D), lambda i: (i, 0))
    vec_spec = pl.BlockSpec((1, D), lambda i: (0, 0))
    return _call(
        body, hosted, name=name, grid=(T // tm,),
        in_specs=[pl.BlockSpec((dproj.shape[0], tm, dproj.shape[2]), lambda i: (0, i, 0)),
                  pl.BlockSpec((nsh, D, Cs), lambda i: (0, 0, 0)), row_spec, vec_spec, row_spec],
        out_specs=[row_spec, vec_spec],
        out_shape=[jax.ShapeDtypeStruct((T, D), F32), jax.ShapeDtypeStruct((1, D), F32)],
        args=[dproj, wing, h, gain, dh])


def _pool_window(x, group, T, trailing):
    rows = lax.broadcasted_iota(jnp.int32, x.shape, 0)

    def shifted(z, k):
        if trailing:
            return jnp.where(rows >= k, pltpu.roll(z, k, 0), 0.0)
        return jnp.where(rows < T - k, pltpu.roll(z, T - k, 0), 0.0)

    s2 = x + shifted(x, 1)
    s4 = s2 + shifted(s2, 2)
    s8 = s4 + shifted(s4, 4)
    s16 = s8 + shifted(s8, 8)
    return jnp.where(group == 0, s2, jnp.where(group == 1, s4, jnp.where(group == 2, s8, s16)))


def _pool_count(group, shape):
    rows = lax.broadcasted_iota(jnp.int32, shape, 0)
    w = jnp.where(group == 0, 2, jnp.where(group == 1, 4, jnp.where(group == 2, 8, 16)))
    return jnp.minimum(rows + 1, w).astype(F32)


def pool_fwd(proj, pool_w, pool_scale, name, hosted=()):
    T = proj.shape[0]
    Hd = HEAD_DIM

    def body(x_ref, w_ref, sc_ref, a_ref):
        g = pl.program_id(0)
        x = x_ref[...]
        pooled = _pool_window(x, g, T, True) / _pool_count(g, x.shape) - x
        a_ref[...] = (_dot(pooled.astype(BF16), w_ref[0].astype(BF16)) * sc_ref[...]).astype(BF16)

    return _call(
        body, hosted, name=name, grid=(N_GROUPS,),
        in_specs=[pl.BlockSpec((T, Hd), lambda g: (0, g)), pl.BlockSpec((1, Hd, Hd), lambda g: (g, 0, 0)),
                  pl.BlockSpec((1, Hd), lambda g: (0, g))],
        out_specs=[pl.BlockSpec((T, Hd), lambda g: (0, g))],
        out_shape=[jax.ShapeDtypeStruct((T, N_GROUPS * Hd), BF16)],
        args=[proj, pool_w, pool_scale])


def pool_bwd(proj, da, pool_w, pool_scale, name, hosted=()):
    T = proj.shape[0]
    Hd = HEAD_DIM

    def body(x_ref, da_ref, w_ref, sc_ref, dx_ref, dw_ref, dsc_ref):
        g = pl.program_id(0)
        x = x_ref[...]
        cnt = _pool_count(g, x.shape)
        pooled = (_pool_window(x, g, T, True) / cnt - x).astype(BF16)
        wb = w_ref[0].astype(BF16)
        dav = da_ref[...]
        dsc_ref[...] = jnp.sum(dav * _dot(pooled, wb), axis=0, keepdims=True)
        dout = (dav * sc_ref[...]).astype(BF16)
        dw_ref[0] = _dot_tn(pooled, dout)
        dpooled = _dot_nt(dout, wb)
        dx_ref[0] = (_pool_window(dpooled / cnt, g, T, False) - dpooled).astype(BF16)

    col_spec = pl.BlockSpec((T, Hd), lambda g: (0, g))
    return _call(
        body, hosted, name=name, grid=(N_GROUPS,),
        in_specs=[col_spec, col_spec, pl.BlockSpec((1, Hd, Hd), lambda g: (g, 0, 0)), pl.BlockSpec((1, Hd), lambda g: (0, g))],
        out_specs=[pl.BlockSpec((1, T, Hd), lambda g: (N_GROUPS, 0, g)), pl.BlockSpec((1, Hd, Hd), lambda g: (g, 0, 0)),
                   pl.BlockSpec((1, Hd), lambda g: (0, g))],
        out_shape=[jax.ShapeDtypeStruct((N_GROUPS + 1, T, N_GROUPS * Hd), BF16), jax.ShapeDtypeStruct((N_GROUPS, Hd, Hd), F32),
                   jax.ShapeDtypeStruct((1, N_GROUPS * Hd), F32)],
        args=[proj, da, pool_w, pool_scale])


def _ret_tables(T):
    Hd, C = HEAD_DIM, RET_CHUNK
    inv_freq = 1.0 / (ROPE_BASE ** (jnp.arange(0, Hd, 2, dtype=F32) / Hd))
    ang = jnp.arange(T, dtype=F32)[:, None] * inv_freq[None, :]
    cos, sin = jnp.cos(ang), jnp.sin(ang)
    cos2 = jnp.concatenate([cos, cos], axis=-1)
    sin2 = jnp.concatenate([-sin, sin], axis=-1)
    log_gamma = jnp.log1p(-jnp.exp2(-5.0 - jnp.arange(N_GROUPS, dtype=F32)))
    pos = jnp.arange(C, dtype=F32)
    rel = pos[:, None] - pos[None, :]
    intra = jnp.where(rel[None] >= 0, jnp.exp(log_gamma[:, None, None] * jnp.maximum(rel, 0.0)[None]), 0.0)
    k_tail = jnp.exp(log_gamma[:, None] * (C - 1 - pos)[None, :])
    q_head = jnp.exp(log_gamma[:, None] * (pos + 1.0)[None, :])
    chunk_decay = jnp.exp(log_gamma * C)
    wide = lambda t: jnp.broadcast_to(t[:, :, None], (N_GROUPS, C, Hd))
    return cos2, sin2, intra, wide(k_tail), wide(q_head), jnp.broadcast_to(chunk_decay[:, None, None], (N_GROUPS, 1, Hd))


def _rope(x, cos2, sin2):
    return x * cos2 + pltpu.roll(x, HEAD_DIM // 2, 1) * sin2


def _rope_t(d, cos2, sin2):
    return d * cos2 + pltpu.roll(d * sin2, HEAD_DIM // 2, 1)


def _ret_specs(T, tseg, seg_of):
    Hd, G = HEAD_DIM, N_GROUPS
    col = lambda kind: pl.BlockSpec((tseg, Hd), lambda h, s: (seg_of(s), G * kind + h))
    tab = pl.BlockSpec((T, Hd), lambda h, s: (0, 0))
    head = pl.BlockSpec((1, RET_CHUNK, Hd), lambda h, s: (h, 0, 0))
    cd = pl.BlockSpec((1, 1, Hd), lambda h, s: (h, 0, 0))
    gain = pl.BlockSpec((1, Hd), lambda h, s: (0, h))
    return col, tab, head, cd, gain


def ret_fwd(proj, ret_norm, tables, name, hosted=()):
    T = proj.shape[0]
    Hd, C, G = HEAD_DIM, RET_CHUNK, N_GROUPS
    tseg = min(T, 2048)
    nseg, nck = T // tseg, tseg // C
    scale = Hd ** -0.5
    cos2, sin2, intra, k_tail, q_head, chunk_decay = tables

    def body(q_ref, k_ref, v_ref, g_ref, gain_ref, cos_ref, sin_ref, m_ref, kt_ref, qh_ref, cd_ref,
             b_ref, o_ref, rp_ref, state):
        @pl.when(pl.program_id(1) == 0)
        def _():
            state[...] = jnp.zeros_like(state)

        def chunk(ci, carry):
            rows = pl.ds(pl.multiple_of(ci * C, C), C)
            at = pl.ds(pl.multiple_of(pl.program_id(1) * tseg + ci * C, C), C)
            cos, sin = cos_ref[at, :], sin_ref[at, :]
            qr = _rope(q_ref[rows, :], cos, sin)
            kr = _rope(k_ref[rows, :], cos, sin) * scale
            qb, kb, vb = qr.astype(BF16), kr.astype(BF16), v_ref[rows, :].astype(BF16)
            r = state[...]
            rp_ref[0, ci] = r.astype(BF16)
            sc = _dot_nt(qb, kb) * m_ref[0]
            o = _dot(sc.astype(BF16), vb) + _dot((qr * qh_ref[0]).astype(BF16), r.astype(BF16))
            state[...] = cd_ref[0] * r + _dot_tn((kr * kt_ref[0]).astype(BF16), vb)
            o_ref[rows, :] = o
            on = o * _rstd(o)
            b_ref[rows, :] = (jax.nn.silu(g_ref[rows, :]) * (on * gain_ref[...])).astype(BF16)
            return carry

        lax.fori_loop(0, nck, chunk, 0, unroll=True)

    col, tab, head, cd, gain = _ret_specs(T, tseg, lambda s: s)
    out_col = pl.BlockSpec((tseg, Hd), lambda h, s: (s, h))
    return _call(
        body, hosted, name=name, grid=(G, nseg),
        in_specs=[col(1), col(2), col(3), col(4), gain, tab, tab, head, head, head, cd],
        out_specs=[out_col, out_col, pl.BlockSpec((1, nck, Hd, Hd), lambda h, s: (h, s, 0, 0))],
        out_shape=[jax.ShapeDtypeStruct((T, G * Hd), BF16), jax.ShapeDtypeStruct((T, G * Hd), F32),
                   jax.ShapeDtypeStruct((G, T // C, Hd, Hd), BF16)],
        scratch_shapes=[pltpu.VMEM((Hd, Hd), F32)],
        args=[proj, proj, proj, proj, ret_norm, cos2, sin2, intra, k_tail, q_head, chunk_decay])


def ret_bwd(proj, db, o_pre, r_prev, ret_norm, tables, dproj, name, hosted=()):
    T = proj.shape[0]
    Hd, C, G = HEAD_DIM, RET_CHUNK, N_GROUPS
    tseg = min(T, 2048)
    nseg, nck = T // tseg, tseg // C
    scale = Hd ** -0.5
    cos2, sin2, intra, k_tail, q_head, chunk_decay = tables

    def body(q_ref, k_ref, v_ref, g_ref, db_ref, o_ref, rp_ref, gain_ref, cos_ref, sin_ref, m_ref, kt_ref, qh_ref, cd_ref,
             _, d_ref, dgain_ref, gstate):
        @pl.when(pl.program_id(1) == 0)
        def _():
            gstate[...] = jnp.zeros_like(gstate)
            dgain_ref[...] = jnp.zeros_like(dgain_ref)

        def chunk(t, carry):
            ci = nck - 1 - t
            rows = pl.ds(pl.multiple_of(ci * C, C), C)
            at = pl.ds(pl.multiple_of((nseg - 1 - pl.program_id(1)) * tseg + ci * C, C), C)
            cos, sin = cos_ref[at, :], sin_ref[at, :]
            qr = _rope(q_ref[rows, :], cos, sin)
            kr = _rope(k_ref[rows, :], cos, sin) * scale
            qb, kb, vb = qr.astype(BF16), kr.astype(BF16), v_ref[rows, :].astype(BF16)
            qhb, ktb = (qr * qh_ref[0]).astype(BF16), (kr * kt_ref[0]).astype(BF16)
            sc = (_dot_nt(qb, kb) * m_ref[0]).astype(BF16)
            o = o_ref[rows, :]
            rstd = _rstd(o)
            on = o * rstd
            gain = gain_ref[...]
            silu, dsilu = _silu_parts(g_ref[rows, :])
            dy = db_ref[rows, :]
            dgain_ref[...] += jnp.sum(dy * silu * on, axis=0, keepdims=True)
            dg = dy * on * gain * dsilu
            don = dy * silu * gain
            dob = (rstd * (don - on * jnp.mean(don * on, axis=-1, keepdims=True))).astype(BF16)
            gn = gstate[...]
            gb = gn.astype(BF16)
            da = (_dot_nt(dob, vb) * m_ref[0]).astype(BF16)
            dq = _dot(da, kb) + _dot_nt(dob, rp_ref[0, ci]) * qh_ref[0]
            dk = _dot_tn(da, qb) + _dot_nt(vb, gb) * kt_ref[0]
            dv = _dot_tn(sc, dob) + _dot(ktb, gb)
            gstate[...] = cd_ref[0] * gn + _dot_tn(qhb, dob)
            d_ref[0, rows, :] = _rope_t(dq, cos, sin).astype(BF16)
            d_ref[1, rows, :] = _rope_t(dk * scale, cos, sin).astype(BF16)
            d_ref[2, rows, :] = dv.astype(BF16)
            d_ref[3, rows, :] = dg.astype(BF16)
            return carry

        lax.fori_loop(0, nck, chunk, 0, unroll=True)

    rev = lambda s: nseg - 1 - s
    col, tab, head, cd, gain = _ret_specs(T, tseg, rev)
    act = pl.BlockSpec((tseg, Hd), lambda h, s: (rev(s), h))
    return _call(
        body, hosted, name=name, grid=(G, nseg),
        in_specs=[col(1), col(2), col(3), col(4), act, act, pl.BlockSpec((1, nck, Hd, Hd), lambda h, s: (h, rev(s), 0, 0)),
                  gain, tab, tab, head, head, head, cd, ANY],
        out_specs=[pl.BlockSpec((4, tseg, Hd), lambda h, s: (0, rev(s), h)), gain],
        out_shape=[jax.ShapeDtypeStruct(dproj.shape, BF16), jax.ShapeDtypeStruct((1, G * Hd), F32)],
        scratch_shapes=[pltpu.VMEM((Hd, Hd), F32)], aliased={14: 0},
        args=[proj, proj, proj, proj, db, o_pre, r_prev, ret_norm, cos2, sin2, intra, k_tail, q_head, chunk_decay, dproj])


def ffn_down_loss(s, w2, h, gain, target, name, hosted=()):
    T, F = s.shape
    D = h.shape[1]
    tm = min(T, 512)

    def body(s_ref, w2_ref, h_ref, g_ref, t_ref, dh_ref, loss_ref, dg_ref):
        @pl.when(pl.program_id(0) == 0)
        def _():
            loss_ref[...] = jnp.zeros_like(loss_ref)
            dg_ref[...] = jnp.zeros_like(dg_ref)

        hh = h_ref[...] + 0.5 * _dot(s_ref[...], w2_ref[...])
        gain_v = g_ref[...]
        err = hh * _rstd(hh) * gain_v - t_ref[...]
        loss_ref[...] += 0.5 * jnp.sum(jnp.mean(err * err, axis=-1, keepdims=True), axis=0, keepdims=True)
        dhn, dg = _rmsnorm_bwd(err * (1.0 / D), hh, gain_v)
        dh_ref[...] = dhn
        dg_ref[...] += jnp.sum(dg, axis=0, keepdims=True)

    row_spec = pl.BlockSpec((tm, D), lambda i: (i, 0))
    vec_spec = pl.BlockSpec((1, D), lambda i: (0, 0))
    return _call(
        body, hosted, name=name, grid=(T // tm,),
        in_specs=[pl.BlockSpec((tm, F), lambda i: (i, 0)), _resident((F, D)), row_spec, vec_spec, row_spec],
        out_specs=[row_spec, pl.BlockSpec((1, LANES), lambda i: (0, 0)), vec_spec],
        out_shape=[jax.ShapeDtypeStruct((T, D), F32), jax.ShapeDtypeStruct((1, LANES), F32), jax.ShapeDtypeStruct((1, D), F32)],
        args=[s, w2, h, gain, target])


def prereduce(grads, recvs, place, name):
    nt = len(grads)
    nsh, R, C = grads[0].shape
    rh = R // 2

    def body(place_ref, *refs):
        for t in range(nt):
            g_ref, r_ref, o_ref, own_ref = refs[2 * t], refs[2 * t + 1], refs[2 * nt + 2 * t], refs[2 * nt + 2 * t + 1]
            piece = (g_ref[...].astype(F32) + r_ref[...].astype(F32)).astype(BF16)
            o_ref[...] = piece

            @pl.when(pl.program_id(0) == place_ref[1])
            def _():
                own_ref[...] = piece

    outs = pl.pallas_call(
        body, name=name,
        grid_spec=pltpu.PrefetchScalarGridSpec(
            num_scalar_prefetch=1, grid=(nsh,),
            in_specs=[pl.BlockSpec((1, rh, C), lambda j, p: (j, p[0], 0)), pl.BlockSpec((1, rh, C), lambda j, p: (j, 0, 0))] * nt,
            out_specs=[pl.BlockSpec((1, rh, C), lambda j, p: (j, 0, 0)),
                       pl.BlockSpec((1, rh, C), lambda j, p: (p[1], p[0], 0))] * nt),
        out_shape=[jax.ShapeDtypeStruct((nsh, rh, C), BF16), jax.ShapeDtypeStruct((nsh, R, C), BF16)] * nt,
        compiler_params=pltpu.CompilerParams(vmem_limit_bytes=VMEM_LIMIT_V7X),
    )(place, *[a for pair in zip(grads, recvs) for a in pair])
    return [(outs[2 * t], outs[2 * t + 1]) for t in range(nt)]


def _adamw(w, g, m, v):
    m = ADAM_B1 * m + (1.0 - ADAM_B1) * g
    v = ADAM_B2 * v + (1.0 - ADAM_B2) * (g * g)
    m_hat = m / (1.0 - ADAM_B1 ** ADAM_STEP)
    v_hat = v / (1.0 - ADAM_B2 ** ADAM_STEP)
    return -ADAM_LR * (m_hat / (jnp.sqrt(v_hat) + ADAM_EPS) + ADAM_WD * w), m, v


def adamw_sharded(tensors, name, hosted=()):
    nt = len(tensors)
    nsh = tensors[0][0].shape[0]
    shapes = [t[0].shape[1:] for t in tensors]

    def fits(steps):
        if any(R % (steps * BF16_TILE_ROWS) for R, _ in shapes):
            return False
        return sum(2 * (R // steps) * -(-C // LANES) * LANES * (nsh * 2 + 7 * 4) for R, C in shapes) <= ADAMW_VMEM_BUDGET

    steps = min(s for s in range(1, min(R for R, _ in shapes) // BF16_TILE_ROWS + 1) if fits(s))

    def body(*refs):
        ins, outs = refs[:4 * nt], refs[4 * nt:]
        for t in range(nt):
            p_ref, w_ref, m_ref, v_ref = ins[4 * t:4 * t + 4]
            g_ref, d_ref, nm_ref, nv_ref = outs[4 * t:4 * t + 4]
            g = p_ref[0].astype(F32)
            for i in range(1, nsh):
                g += p_ref[i].astype(F32)
            g_ref[...] = g
            d_ref[...], nm_ref[...], nv_ref[...] = _adamw(w_ref[...], g, m_ref[...], v_ref[...])

    in_specs, out_specs, out_shape = [], [], []
    for R, C in shapes:
        spec = pl.BlockSpec((R // steps, C), lambda i: (i, 0))
        in_specs += [pl.BlockSpec((nsh, R // steps, C), lambda i: (0, i, 0)), spec, spec, spec]
        out_specs += [spec] * 4
        out_shape += [jax.ShapeDtypeStruct((R, C), F32)] * 4
    return _call(body, hosted, name=name, grid=(steps,), in_specs=in_specs, out_specs=out_specs, out_shape=out_shape,
                 args=[a for tensor in tensors for a in tensor])


def adamw_small(packs, params, loss_packs, name):
    n = len(packs)
    ndev = loss_packs.shape[0]

    def body(*refs):
        p_refs, loss_ref, wmv = refs[:n], refs[n], refs[n + 1:4 * n + 1]
        outs, loss_out = refs[4 * n + 1:8 * n + 1], refs[8 * n + 1]
        total = lambda r: sum((r[i] for i in range(1, ndev)), r[0])
        loss_out[...] = total(loss_ref)
        for k in range(n):
            g = total(p_refs[k])
            outs[4 * k][...] = g
            outs[4 * k + 1][...], outs[4 * k + 2][...], outs[4 * k + 3][...] = _adamw(
                wmv[3 * k][...], g, wmv[3 * k + 1][...], wmv[3 * k + 2][...])

    out_shape = [jax.ShapeDtypeStruct(p[0].shape, F32) for p in params for _ in range(4)]
    outs = pl.pallas_call(body, name=name, out_shape=out_shape + [jax.ShapeDtypeStruct(loss_packs.shape[1:], F32)],
                          compiler_params=pltpu.CompilerParams(vmem_limit_bytes=VMEM_LIMIT_V7X),
                          )(*packs, loss_packs, *[a for p in params for a in p])
    return [outs[4 * k:4 * k + 4] for k in range(n)], outs[4 * n]


BIG = ("ffn1_w1", "ffn1_w3", "ffn1_w2", "w_in", "w_out", "ffn2_w1", "ffn2_w3", "ffn2_w2")
TRANSPOSED = ("ffn1_w1", "ffn1_w3", "ffn2_w1", "ffn2_w3")
SMALL = ("pool_w", "mix_norm", "pool_scale", "ret_norm", "ffn2_norm", "final_norm", "ffn1_norm")
WEIGHTS = ("ffn1_norm", "ffn1_w1", "ffn1_w3", "ffn1_w2", "mix_norm", "w_in", "pool_w", "pool_scale", "ret_norm", "w_out",
           "ffn2_norm", "ffn2_w1", "ffn2_w3", "ffn2_w2", "final_norm")


def kernel(x, ffn1_norm, ffn1_w1, ffn1_w3, ffn1_w2, mix_norm, w_in, pool_w, pool_scale, ret_norm, w_out, ffn2_norm, ffn2_w1, ffn2_w3, ffn2_w2, final_norm, loss_target, m_ffn1_norm, m_ffn1_w1, m_ffn1_w3, m_ffn1_w2, m_mix_norm, m_w_in, m_pool_w, m_pool_scale, m_ret_norm, m_w_out, m_ffn2_norm, m_ffn2_w1, m_ffn2_w3, m_ffn2_w2, m_final_norm, v_ffn1_norm, v_ffn1_w1, v_ffn1_w3, v_ffn1_w2, v_mix_norm, v_w_in, v_pool_w, v_pool_scale, v_ret_norm, v_w_out, v_ffn2_norm, v_ffn2_w1, v_ffn2_w3, v_ffn2_w2, v_final_norm):
    w = dict(ffn1_norm=ffn1_norm, ffn1_w1=ffn1_w1, ffn1_w3=ffn1_w3, ffn1_w2=ffn1_w2, mix_norm=mix_norm, w_in=w_in, pool_w=pool_w,
             pool_scale=pool_scale, ret_norm=ret_norm, w_out=w_out, ffn2_norm=ffn2_norm, ffn2_w1=ffn2_w1, ffn2_w3=ffn2_w3,
             ffn2_w2=ffn2_w2, final_norm=final_norm)
    m = dict(ffn1_norm=m_ffn1_norm, ffn1_w1=m_ffn1_w1, ffn1_w3=m_ffn1_w3, ffn1_w2=m_ffn1_w2, mix_norm=m_mix_norm, w_in=m_w_in,
             pool_w=m_pool_w, pool_scale=m_pool_scale, ret_norm=m_ret_norm, w_out=m_w_out, ffn2_norm=m_ffn2_norm, ffn2_w1=m_ffn2_w1,
             ffn2_w3=m_ffn2_w3, ffn2_w2=m_ffn2_w2, final_norm=m_final_norm)
    v = dict(ffn1_norm=v_ffn1_norm, ffn1_w1=v_ffn1_w1, ffn1_w3=v_ffn1_w3, ffn1_w2=v_ffn1_w2, mix_norm=v_mix_norm, w_in=v_w_in,
             pool_w=v_pool_w, pool_scale=v_pool_scale, ret_norm=v_ret_norm, w_out=v_w_out, ffn2_norm=v_ffn2_norm, ffn2_w1=v_ffn2_w1,
             ffn2_w3=v_ffn2_w3, ffn2_w2=v_ffn2_w2, final_norm=v_final_norm)
    xs, target = x[0], loss_target[0]
    T = xs.shape[0]
    tables = _ret_tables(T)
    place = jnp.stack([lax.axis_index("c"), 2 * lax.axis_index("x") + lax.axis_index("y")]).astype(jnp.int32)
    local = lambda d, k: jnp.transpose(d[k][0]) if k in TRANSPOSED else d[k][0]
    result = lambda o, k: jnp.transpose(o)[None] if k in TRANSPOSED else o[None]
    first = ("ffn1_w1", "ffn1_w3")
    sh = {k: local(w, k).astype(BF16) for k in first}
    gather = lambda *names: [ChipExchange([sh[k] for k in names], False)]
    wg, grad, delta, new_m, new_v = {}, {}, {}, {}, {}

    def update(names, pieces, name, hosted=()):
        outs, extras = adamw_sharded([(p, local(w, k), local(m, k), local(v, k)) for k, p in zip(names, pieces)], name, hosted)
        for t, k in enumerate(names):
            grad[k], delta[k], new_m[k], new_v[k] = [result(o, k) for o in outs[4 * t:4 * t + 4]]
        return extras

    def reduce_in_chip(name, *pairs):
        reduced = prereduce([p for p, _ in pairs], [r for _, r in pairs], place, "prereduce_" + name)
        return reduced[0] if len(pairs) == 1 else reduced

    scatter = lambda *reduced: ChipExchange([r[0] for r in reduced], True, [r[1] for r in reduced])
    whole = lambda k: wg[k].reshape(-1, wg[k].shape[-1])
    sharded = lambda g: g.reshape(N_CHIPS, -1, g.shape[-1])

    later = [k for k in BIG if k not in first]
    casts, ((wg["ffn1_w1"], wg["ffn1_w3"]),) = cast_shards([local(w, k) for k in later], "cast_gather_ffn1", gather(*first))
    sh.update(zip(later, casts))
    (n1, ga1, gb1, s1), ((wg["ffn1_w2"], wg["w_in"]),) = ffn_up(
        xs, ffn1_norm, whole("ffn1_w1"), whole("ffn1_w3"), "ffn1_up", gather("ffn1_w2", "w_in"))
    (h1, u, proj), ((wg["w_out"], wg["ffn2_w1"]),) = ffn_down_mix_in(
        s1, whole("ffn1_w2"), xs, mix_norm, wg["w_in"], "ffn1_down_mix_in", gather("w_out", "ffn2_w1"))
    (pa,), _ = pool_fwd(proj, pool_w[0], pool_scale, "pool_fwd")
    (rb, o_pre, r_prev), ((wg["ffn2_w3"],),) = ret_fwd(proj, ret_norm, tables, "ret_fwd", gather("ffn2_w3"))
    (h2, n2, ga2, gb2, s2), ((wg["ffn2_w2"],),) = ffn_up(
        h1, ffn2_norm, whole("ffn2_w1"), whole("ffn2_w3"), "mix_out_ffn2_up", gather("ffn2_w2"), mixed=(pa, rb, wg["w_out"]))
    (dh3, loss, d_final), _ = ffn_down_loss(s2, whole("ffn2_w2"), h2, final_norm[None], target, "ffn2_down_loss")

    (da2, db2, df2), _ = ffn_bwd_act(dh3, whole("ffn2_w2"), ga2, gb2, "ffn2_bwd_act")
    (g_f2w2,), _ = ffn_dw([s2], df2, 1, "ffn2_dw2")
    g_f2w2 = sharded(g_f2w2)
    (g_f2w1, g_f2w3), ((r_f2w2,),) = ffn_dw([da2, db2], n2, 2, "ffn2_dw13", [SiblingExchange([g_f2w2])])
    g_f2w1, g_f2w3 = sharded(g_f2w1), sharded(g_f2w3)
    p_f2w2 = reduce_in_chip("ffn2_w2", (g_f2w2, r_f2w2))
    (dh2, d_ffn2), ((q_f2w2,), (r_f2w1, r_f2w3)) = ffn_bwd_in(
        da2, db2, whole("ffn2_w1"), whole("ffn2_w3"), h2, ffn2_norm, dh3, "ffn2_bwd_in",
        [scatter(p_f2w2), SiblingExchange([g_f2w1, g_f2w3])])
    p_f2w1, p_f2w3 = reduce_in_chip("ffn2_w13", (g_f2w1, r_f2w1), (g_f2w3, r_f2w3))
    (dpa, drb, g_wout), _ = mix_out_bwd(dh2, wg["w_out"], pa, rb, "mix_out_bwd")
    (dproj, d_pool_w, d_pool_scale), _ = pool_bwd(proj, dpa, pool_w[0], pool_scale, "pool_bwd")
    (dproj, d_ret_norm), ((q_f2w1, q_f2w3), (r_wout,)) = ret_bwd(
        proj, drb, o_pre, r_prev, ret_norm, tables, dproj, "ret_bwd", [scatter(p_f2w1, p_f2w3), SiblingExchange([g_wout])])
    p_wout = reduce_in_chip("w_out", (g_wout, r_wout))
    (g_win,), ((q_wout,),) = mix_dwin(u, dproj, N_CHIPS, "mix_dwin", [scatter(p_wout)])
    (dh1, d_mix), ((r_win,),) = mix_in_bwd(dproj, wg["w_in"], h1, mix_norm, dh2, "mix_in_bwd", [SiblingExchange([g_win])])
    p_win = reduce_in_chip("w_in", (g_win, r_win))
    (da1, db1, df1), ((q_win,),) = ffn_bwd_act(dh1, whole("ffn1_w2"), ga1, gb1, "ffn1_bwd_act", [scatter(p_win)])
    d_small = {"pool_w": d_pool_w.reshape(-1, LANES), "mix_norm": d_mix, "pool_scale": d_pool_scale, "ret_norm": d_ret_norm,
               "ffn2_norm": d_ffn2, "final_norm": d_final}
    (g_f1w1, g_f1w3), (packs,) = ffn_dw([da1, db1], n1, 2, "ffn1_dw13", [AllExchange([d_small[k] for k in SMALL[:-1]] + [loss])])
    g_f1w1, g_f1w3 = sharded(g_f1w1), sharded(g_f1w3)
    (g_f1w2,), ((r_f1w1, r_f1w3),) = ffn_dw([s1], df1, 1, "ffn1_dw2", [SiblingExchange([g_f1w1, g_f1w3])])
    g_f1w2 = sharded(g_f1w2)
    p_f1w1, p_f1w3 = reduce_in_chip("ffn1_w13", (g_f1w1, r_f1w1), (g_f1w3, r_f1w3))
    (dx, d_ffn1), ((q_f1w1, q_f1w3), (r_f1w2,)) = ffn_bwd_in(
        da1, db1, whole("ffn1_w1"), whole("ffn1_w3"), xs, ffn1_norm, dh1, "ffn1_bwd_in",
        [scatter(p_f1w1, p_f1w3), SiblingExchange([g_f1w2])])
    p_f1w2 = reduce_in_chip("ffn1_w2", (g_f1w2, r_f1w2))

    (q_f1w2,), (late,) = update(["w_in", "w_out", "ffn2_w2"], [q_win, q_wout, q_f2w2], "adamw_mix_w2",
                                [scatter(p_f1w2), AllExchange([d_ffn1])])
    update(["ffn2_w1", "ffn2_w3", "ffn1_w1", "ffn1_w3"], [q_f2w1, q_f2w3, q_f1w1, q_f1w3], "adamw_w13")
    update(["ffn1_w2"], [q_f1w2], "adamw_ffn1_w2")
    flat = lambda t, k: t[k].reshape(-1, LANES) if k == "pool_w" else t[k].reshape(1, -1)
    updated, loss_sum = adamw_small(packs[:-1] + [late], [[flat(t, k) for t in (w, m, v)] for k in SMALL], packs[-1], "adamw_small")
    for k, outs in zip(SMALL, updated):
        grad[k], delta[k], new_m[k], new_v[k] = [o.reshape(w[k].shape) for o in outs]
    loss = loss_sum[0, 0]

    return (loss, dx[None], *[grad[k] for k in WEIGHTS], *[delta[k] for k in WEIGHTS],
            *[new_m[k] for k in WEIGHTS], *[new_v[k] for k in WEIGHTS])
```

```python
import math

import jax
import jax.numpy as jnp
from jax import lax
from jax.experimental import pallas as pl
from jax.experimental.pallas import tpu as pltpu

F32 = jnp.float32
BF16 = jnp.bfloat16

EPS = 1e-6
LANES = 128
BF16_TILE_ROWS = 16
N_CHIPS = 4
N_GROUPS = 4
HEAD_DIM = 128
RET_CHUNK = 128
ROPE_BASE = 10000.0
ADAM_LR, ADAM_B1, ADAM_B2, ADAM_EPS, ADAM_WD, ADAM_STEP = 0.001, 0.9, 0.999, 1e-08, 0.01, 10
VMEM_LIMIT_V7X = 56 * 1024 * 1024
ADAMW_VMEM_BUDGET = 32 * 1024 * 1024
MESH = pl.DeviceIdType.MESH
ANY = pl.BlockSpec(memory_space=pl.ANY)


def _dot(a, b):
    return jnp.dot(a, b, preferred_element_type=F32)


def _dot_nt(a, b):
    return lax.dot_general(a, b, (((1,), (1,)), ((), ())), preferred_element_type=F32)


def _dot_tn(a, b):
    return lax.dot_general(a, b, (((0,), (0,)), ((), ())), preferred_element_type=F32)


def _rstd(h):
    return lax.rsqrt(jnp.mean(h * h, axis=-1, keepdims=True) + EPS)


def _rmsnorm_bwd(dn, h, gain):
    r = _rstd(h)
    nh = h * r
    dnh = dn * gain
    dh = r * (dnh - nh * jnp.mean(dnh * nh, axis=-1, keepdims=True))
    return dh, dn * nh


def _silu_parts(a):
    sig = jax.nn.sigmoid(a)
    silu = a * sig
    return silu, sig + silu * (1.0 - sig)


def _mesh_pos():
    return lax.axis_index("x"), lax.axis_index("y"), lax.axis_index("c")


class ChipExchange:
    def __init__(self, srcs, scatter, placed=()):
        n = len(srcs)
        self.inputs, self.scatter, self.n, self.reach = list(srcs) + list(placed), scatter, n, REACH_CHIPS
        self.aliases = {n + t: t for t in range(n)} if scatter else {}
        self.half_rows = [s.shape[1] if scatter else s.shape[0] // 2 for s in srcs]
        self.out_shape = [jax.ShapeDtypeStruct((N_CHIPS, 2 * rh, s.shape[-1]), s.dtype) for s, rh in zip(srcs, self.half_rows)]
        if scatter:
            self.out_shape += [jax.ShapeDtypeStruct((2, rh // 2, s.shape[-1]), s.dtype) for s, rh in zip(srcs, self.half_rows)]
        dma = pltpu.SemaphoreType.DMA
        self.sems = [dma((4 * n,)), dma((4 * n,)), dma((2 * n,)), dma((2 * n,)), dma((4 * n,)), dma((4 * n,))]

    def _copies(self, src, out, sems):
        hop1_send, hop1_recv, hop2_send, hop2_recv, d2d_send, d2d_recv = sems
        x, y, c = _mesh_pos()
        me, dg = 2 * x + y, 2 * (1 - x) + (1 - y)
        sibling = (x, y, 1 - c)
        n = self.n
        mine, theirs = c, 1 - c

        def nb(a):
            nx, ny = x ^ (1 - a), y ^ a
            return 2 * nx + ny, (nx, ny, c)

        def remote(s, d, send, recv, k, to):
            return pltpu.make_async_remote_copy(src_ref=s, dst_ref=d, send_sem=send.at[k], recv_sem=recv.at[k],
                                                device_id=to, device_id_type=MESH)

        class Copies:
            def slot(_, t, chip, half):
                rh = self.half_rows[t]
                return out[t].at[chip, pl.ds(half * rh, rh), :]

            def quarter(_, t, chip, q):
                qh = self.half_rows[t] // 2
                return out[t].at[chip, pl.ds(mine * 2 * qh + q * qh, qh), :]

            def own_shard(k, t):
                return remote(src[t], out[t].at[me], d2d_send, d2d_recv, 4 * t + 3, sibling)

            def hop1(k, t, a, transit=False):
                rh = self.half_rows[t]
                chip, to = nb(a)
                if transit:
                    piece = src[t].at[dg, pl.ds(a * (rh // 2), rh // 2), :]
                    return remote(piece, out[n + t].at[a], hop1_send, hop1_recv, 4 * t + 2 + a, to)
                piece = src[t].at[chip] if self.scatter else src[t].at[pl.ds(mine * rh, rh), :]
                return remote(piece, k.slot(t, me, mine), hop1_send, hop1_recv, 4 * t + a, to)

            def landed1(k, t, a, transit=False):
                here = out[n + t].at[a] if transit else k.slot(t, nb(a)[0], mine)
                return remote(here, here, hop1_send, hop1_recv, 4 * t + (2 if transit else 0) + a, sibling)

            def hop2(k, t, q):
                origin, to = nb(q)[0], nb(1 - q)[1]
                piece = out[n + t].at[q] if self.scatter else k.quarter(t, origin, q)
                return remote(piece, k.quarter(t, origin, q), hop2_send, hop2_recv, 2 * t + q, to)

            def landed2(k, t, q):
                here = k.quarter(t, dg, q)
                return remote(here, here, hop2_send, hop2_recv, 2 * t + q, sibling)

            def d2d(k, t, p, chip, own=False, arriving=False):
                if arriving:
                    there = k.slot(t, chip, theirs)
                    return remote(there, there, d2d_send, d2d_recv, 4 * t + p, sibling)
                piece = src[t].at[me] if own else k.slot(t, chip, mine)
                return remote(piece, k.slot(t, chip, mine), d2d_send, d2d_recv, 4 * t + p, sibling)

        return Copies(), nb, me, dg, c

    def start(self, src, out, sems):
        k, nb, me, dg, c = self._copies(src, out, sems)
        for t in range(self.n):
            for first in range(2):
                a = first ^ c
                k.hop1(t, a).start()
                if self.scatter:
                    k.hop1(t, a, transit=True).start()
            if self.scatter:
                k.d2d(t, 3, me, own=True).start()
            else:
                k.own_shard(t).start()

    def mid(self, src, out, sems):
        k, nb, me, dg, c = self._copies(src, out, sems)
        for t in range(self.n):
            for first in range(2):
                a = first ^ c
                if self.scatter:
                    k.landed1(t, a, transit=True).wait_recv()
                    k.hop2(t, a).start()
                k.landed1(t, a).wait_recv()
                if not self.scatter:
                    k.hop2(t, a).start()
                k.d2d(t, a, nb(a)[0]).start()

    def finish(self, src, out, sems):
        k, nb, me, dg, c = self._copies(src, out, sems)
        for t in range(self.n):
            for q in range(2):
                k.landed2(t, q).wait_recv()
            k.d2d(t, 2, dg).start()
        for t in range(self.n):
            for a in range(2):
                k.d2d(t, a, nb(a)[0], arriving=True).wait_recv()
            k.d2d(t, 2, dg, arriving=True).wait_recv()
            if self.scatter:
                k.d2d(t, 3, me, arriving=True).wait_recv()
        for t in range(self.n):
            for a in range(2):
                k.hop1(t, a).wait_send()
                if self.scatter:
                    k.hop1(t, a, transit=True).wait_send()
                k.hop2(t, a).wait_send()
                k.d2d(t, a, nb(a)[0]).wait_send()
            k.d2d(t, 2, dg).wait_send()
            if self.scatter:
                k.d2d(t, 3, me, own=True).wait_send()
            else:
                k.own_shard(t).wait()


class SiblingExchange:
    def __init__(self, grads):
        self.inputs, self.n, self.aliases, self.reach = list(grads), len(grads), {}, REACH_SIBLING
        self.half_rows = [g.shape[1] // 2 for g in grads]
        self.out_shape = [jax.ShapeDtypeStruct((g.shape[0], rh, g.shape[2]), g.dtype) for g, rh in zip(grads, self.half_rows)]
        self.sems = [pltpu.SemaphoreType.DMA((self.n,)), pltpu.SemaphoreType.DMA((self.n,))]

    def _plan(self, src, out, sems):
        x, y, c = _mesh_pos()
        return [pltpu.make_async_remote_copy(
            src_ref=src[t].at[:, pl.ds((1 - c) * self.half_rows[t], self.half_rows[t]), :], dst_ref=out[t],
            send_sem=sems[0].at[t], recv_sem=sems[1].at[t], device_id=(x, y, 1 - c), device_id_type=MESH) for t in range(self.n)]

    def start(self, src, out, sems):
        for cp in self._plan(src, out, sems):
            cp.start()

    def mid(self, src, out, sems):
        pass

    def finish(self, src, out, sems):
        for cp in self._plan(src, out, sems):
            cp.wait()


REACH_SIBLING, REACH_CHIPS, REACH_ALL = 0, 1, 2


def _entry_barrier(reach):
    x, y, c = _mesh_pos()
    peers = [(x, y, 1 - c)]
    if reach == REACH_CHIPS:
        peers += [(1 - x, y, c), (x, 1 - y, c)]
    elif reach == REACH_ALL:
        peers = [(x ^ dx, y ^ dy, c ^ dc) for dx in (0, 1) for dy in (0, 1) for dc in (0, 1)][1:]
    barrier = pltpu.get_barrier_semaphore()
    for peer in peers:
        pl.semaphore_signal(barrier, inc=1, device_id=peer, device_id_type=MESH)
    pl.semaphore_wait(barrier, len(peers))


def _call(body, hosted=(), *, name, in_specs, out_specs, out_shape, args, grid=(), scratch_shapes=(), aliased=None):
    n_in, n_out, n_scr = len(in_specs), len(out_specs), len(scratch_shapes)
    total = math.prod(grid)
    mid_step = max(0, (3 * total) // 4 - 1)

    def full(*refs):
        pos = [0]

        def take(k):
            pos[0] += k
            return refs[pos[0] - k:pos[0]]

        ins, h_in = take(n_in), [take(len(h.inputs)) for h in hosted]
        outs, h_out = take(n_out), [take(len(h.out_shape)) for h in hosted]
        scr, h_sem = take(n_scr), [take(len(h.sems)) for h in hosted]
        step = 0
        for axis, size in enumerate(grid):
            step = step * size + pl.program_id(axis)

        def phase(at, method):
            if not hosted:
                return

            def run():
                if method == "start":
                    _entry_barrier(reach)
                for h, s, o, m in zip(hosted, h_in, h_out, h_sem):
                    getattr(h, method)(s, o, m)

            if total == 1:
                run()
            else:
                pl.when(step == at)(run)

        phase(0, "start")
        body(*ins, *outs, *scr)
        phase(mid_step, "mid")
        phase(total - 1, "finish")

    aliases, i0, o0 = dict(aliased or {}), n_in, n_out
    for h in hosted:
        aliases.update({i0 + i: o0 + o for i, o in h.aliases.items()})
        i0, o0 = i0 + len(h.inputs), o0 + len(h.out_shape)
    reach = max((h.reach for h in hosted), default=None)
    params = dict(vmem_limit_bytes=VMEM_LIMIT_V7X)
    if hosted:
        params["collective_id"] = reach
    results = pl.pallas_call(
        full, name=name, grid=grid,
        in_specs=list(in_specs) + [ANY] * (i0 - n_in),
        out_specs=list(out_specs) + [ANY] * (o0 - n_out),
        out_shape=list(out_shape) + [s for h in hosted for s in h.out_shape],
        scratch_shapes=list(scratch_shapes) + [s for h in hosted for s in h.sems],
        input_output_aliases=aliases,
        compiler_params=pltpu.CompilerParams(**params),
    )(*args, *[s for h in hosted for s in h.inputs])
    outs, extras, pos = list(results[:n_out]), [], n_out
    for h in hosted:
        extras.append(list(results[pos:pos + h.n]))
        pos += len(h.out_shape)
    return outs, extras


def cast_shards(shards, name, hosted=()):
    n = len(shards)

    def body(*refs):
        for x_ref, o_ref in zip(refs[:n], refs[n:]):
            o_ref[...] = x_ref[...].astype(BF16)

    whole = lambda s: pl.BlockSpec(s.shape, lambda: (0,) * s.ndim)
    return _call(body, hosted, name=name, in_specs=[whole(s) for s in shards], out_specs=[whole(s) for s in shards],
                 out_shape=[jax.ShapeDtypeStruct(s.shape, BF16) for s in shards], args=list(shards))


class AllExchange:
    def __init__(self, arrays):
        n = len(arrays)
        self.inputs, self.n, self.aliases, self.reach = list(arrays), n, {}, REACH_ALL
        self.out_shape = [jax.ShapeDtypeStruct((2 * N_CHIPS,) + a.shape, a.dtype) for a in arrays]
        self.sems = [pltpu.SemaphoreType.DMA((n,)), pltpu.SemaphoreType.DMA((7 * n,)), pltpu.SemaphoreType.DMA((7 * n,))]

    def _copies(self, src, out, sems):
        local_sem, send_sem, recv_sem = sems
        x, y, c = _mesh_pos()
        me = 4 * x + 2 * y + c
        peers = [(x ^ dx, y ^ dy, c ^ dc) for dx in (0, 1) for dy in (0, 1) for dc in (0, 1)][1:]
        remote = lambda s, d, k, to: pltpu.make_async_remote_copy(
            src_ref=s, dst_ref=d, send_sem=send_sem.at[k], recv_sem=recv_sem.at[k], device_id=to, device_id_type=MESH)
        sends, landed, local = [], [], []
        for t in range(self.n):
            local.append(pltpu.make_async_copy(src[t], out[t].at[me], local_sem.at[t]))
            for p, (px, py, pc) in enumerate(peers):
                sends.append(remote(src[t], out[t].at[me], 7 * t + p, (px, py, pc)))
                here = out[t].at[4 * px + 2 * py + pc]
                landed.append(remote(here, here, 7 * t + p, (px, py, pc)))
        return sends, landed, local

    def start(self, src, out, sems):
        sends, _, local = self._copies(src, out, sems)
        for cp in sends + local:
            cp.start()

    def mid(self, src, out, sems):
        pass

    def finish(self, src, out, sems):
        sends, landed, local = self._copies(src, out, sems)
        for cp in landed:
            cp.wait_recv()
        for cp in sends:
            cp.wait_send()
        for cp in local:
            cp.wait()


MXU_COLS = 256


def _resident(shape):
    return pl.BlockSpec(shape, lambda *_: (0,) * len(shape), pipeline_mode=pl.Buffered(1))


def ffn_up(h, gain, w1, w3, name, hosted=(), mixed=None):
    T, D = h.shape
    F = w1.shape[0]
    tm = min(T, 512)

    def body(*refs):
        if mixed is None:
            h_ref, g_ref, w1_ref, w3_ref, n_ref, ga_ref, gb_ref, s_ref = refs
            hh = h_ref[...]
        else:
            pa_ref, rb_ref, wo_ref, h_ref, g_ref, w1_ref, w3_ref, hh_ref, n_ref, ga_ref, gb_ref, s_ref = refs
            hh = h_ref[...] + _dot(pa_ref[...], wo_ref[0]) + _dot(rb_ref[...], wo_ref[1])
            hh_ref[...] = hh
        n = (hh * _rstd(hh) * g_ref[...]).astype(BF16)
        n_ref[...] = n
        for c in range(0, F, MXU_COLS):
            cols = slice(c, c + MXU_COLS)
            a = _dot_nt(n, w1_ref[cols, :])
            b = _dot_nt(n, w3_ref[cols, :])
            silu, dsilu = _silu_parts(a)
            ga_ref[:, cols] = (b * dsilu).astype(BF16)
            gb_ref[:, cols] = silu.astype(BF16)
            s_ref[:, cols] = (silu * b).astype(BF16)

    act = jax.ShapeDtypeStruct((T, F), BF16)
    act_spec = pl.BlockSpec((tm, F), lambda i: (i, 0))
    row_spec = pl.BlockSpec((tm, D), lambda i: (i, 0))
    in_specs = [row_spec, pl.BlockSpec((1, D), lambda i: (0, 0)), _resident((F, D)), _resident((F, D))]
    out_specs, out_shape, args = [row_spec, act_spec, act_spec, act_spec], [jax.ShapeDtypeStruct((T, D), BF16), act, act, act], [h, gain, w1, w3]
    if mixed is not None:
        pa, rb, woutg = mixed
        W = pa.shape[1]
        in_specs = [pl.BlockSpec((tm, W), lambda i: (i, 0))] * 2 + [_resident((2, W, D))] + in_specs
        out_specs, out_shape = [row_spec] + out_specs, [jax.ShapeDtypeStruct((T, D), F32)] + out_shape
        args = [pa, rb, woutg.reshape(2, W, D)] + args
    return _call(body, hosted, name=name, grid=(T // tm,), in_specs=in_specs, out_specs=out_specs, out_shape=out_shape, args=args)


def ffn_bwd_act(dh, w2, ga, gb, name, hosted=()):
    T, D = dh.shape
    F = w2.shape[0]
    tm = min(T, 512)

    def body(dh_ref, w2_ref, ga_ref, gb_ref, da_ref, db_ref, df_ref):
        df = (0.5 * dh_ref[...]).astype(BF16)
        df_ref[...] = df
        for c in range(0, F, MXU_COLS):
            cols = slice(c, c + MXU_COLS)
            ds = _dot_nt(df, w2_ref[cols, :])
            da_ref[:, cols] = (ds * ga_ref[:, cols].astype(F32)).astype(BF16)
            db_ref[:, cols] = (ds * gb_ref[:, cols].astype(F32)).astype(BF16)

    act = jax.ShapeDtypeStruct((T, F), BF16)
    act_spec = pl.BlockSpec((tm, F), lambda i: (i, 0))
    row_spec = pl.BlockSpec((tm, D), lambda i: (i, 0))
    return _call(
        body, hosted, name=name, grid=(T // tm,),
        in_specs=[row_spec, _resident((F, D)), act_spec, act_spec],
        out_specs=[act_spec, act_spec, row_spec],
        out_shape=[act, act, jax.ShapeDtypeStruct((T, D), BF16)],
        args=[dh, w2, ga, gb])


def ffn_dw(xs, y, halves, name, hosted=()):
    T, F = xs[0].shape
    D = y.shape[1]
    nx, fh = len(xs), F // halves
    tk = min(T, 512)
    nk = T // tk

    def body(*refs):
        y_ref, x_refs, o_refs, accs = refs[0], refs[1:1 + nx], refs[1 + nx:1 + 2 * nx], refs[1 + 2 * nx:]
        k = pl.program_id(1)

        @pl.when(k == 0)
        def _():
            for acc in accs:
                acc[...] = jnp.zeros_like(acc)

        yy = y_ref[...]
        for x_ref, acc in zip(x_refs, accs):
            acc[...] += _dot_tn(x_ref[...], yy)

        @pl.when(k == nk - 1)
        def _():
            for o_ref, acc in zip(o_refs, accs):
                o_ref[...] = acc[...].astype(BF16)

    out = jax.ShapeDtypeStruct((F, D), BF16)
    return _call(
        body, hosted, name=name, grid=(halves, nk),
        in_specs=[pl.BlockSpec((tk, D), lambda j, k: (k, 0))] + [pl.BlockSpec((tk, fh), lambda j, k: (k, j))] * nx,
        out_specs=[pl.BlockSpec((fh, D), lambda j, k: (j, 0))] * nx,
        out_shape=[out] * nx,
        scratch_shapes=[pltpu.VMEM((fh, D), F32)] * nx,
        args=[y] + list(xs))


def ffn_bwd_in(da, db, w1, w3, h, gain, dh, name, hosted=()):
    T, F = da.shape
    D = h.shape[1]
    tm = min(T, 512)

    def body(da_ref, db_ref, w1_ref, w3_ref, h_ref, g_ref, dh_ref, o_ref, dg_ref):
        dn = _dot(da_ref[...], w1_ref[...]) + _dot(db_ref[...], w3_ref[...])
        dhn, dg = _rmsnorm_bwd(dn, h_ref[...], g_ref[...])
        o_ref[...] = dh_ref[...] + dhn

        @pl.when(pl.program_id(0) == 0)
        def _():
            dg_ref[...] = jnp.zeros_like(dg_ref)

        dg_ref[...] += jnp.sum(dg, axis=0, keepdims=True)

    act_spec = pl.BlockSpec((tm, F), lambda i: (i, 0))
    row_spec = pl.BlockSpec((tm, D), lambda i: (i, 0))
    vec_spec = pl.BlockSpec((1, D), lambda i: (0, 0))
    return _call(
        body, hosted, name=name, grid=(T // tm,),
        in_specs=[act_spec, act_spec, _resident((F, D)), _resident((F, D)), row_spec, vec_spec, row_spec],
        out_specs=[row_spec, vec_spec],
        out_shape=[jax.ShapeDtypeStruct((T, D), F32), jax.ShapeDtypeStruct((1, D), F32)],
        args=[da, db, w1, w3, h, gain, dh])


def ffn_down_mix_in(s, w2, h, gain, wing, name, hosted=()):
    T, F = s.shape
    D = h.shape[1]
    nsh, _, Cs = wing.shape
    tm = min(T, 512)

    def body(s_ref, w2_ref, h_ref, g_ref, w_ref, hh_ref, u_ref, p_ref):
        hh = h_ref[...] + 0.5 * _dot(s_ref[...], w2_ref[...])
        hh_ref[...] = hh
        u = (hh * _rstd(hh) * g_ref[...]).astype(BF16)
        u_ref[...] = u
        for j in range(nsh):
            p_ref[:, j * Cs:(j + 1) * Cs] = _dot(u, w_ref[j])

    row_spec = pl.BlockSpec((tm, D), lambda i: (i, 0))
    return _call(
        body, hosted, name=name, grid=(T // tm,),
        in_specs=[pl.BlockSpec((tm, F), lambda i: (i, 0)), _resident((F, D)), row_spec, pl.BlockSpec((1, D), lambda i: (0, 0)),
                  _resident((nsh, D, Cs))],
        out_specs=[row_spec, row_spec, pl.BlockSpec((tm, nsh * Cs), lambda i: (i, 0))],
        out_shape=[jax.ShapeDtypeStruct((T, D), F32), jax.ShapeDtypeStruct((T, D), BF16), jax.ShapeDtypeStruct((T, nsh * Cs), F32)],
        args=[s, w2, h, gain, wing])


def mix_out_bwd(dh, woutg, a, b, name, hosted=()):
    T, D = dh.shape
    W = a.shape[1]
    nsh, Rs, _ = woutg.shape
    wout = woutg.reshape(2, W, D)
    tk = min(T, 512)
    nk = T // tk

    def body(dh_ref, w_ref, a_ref, b_ref, da_ref, db_ref, dw_ref, acc):
        k = pl.program_id(0)

        @pl.when(k == 0)
        def _():
            acc[...] = jnp.zeros_like(acc)

        dhb = dh_ref[...].astype(BF16)
        da_ref[...] = _dot_nt(dhb, w_ref[0])
        db_ref[...] = _dot_nt(dhb, w_ref[1])
        acc[0:W, :] += _dot_tn(a_ref[...], dhb)
        acc[W:2 * W, :] += _dot_tn(b_ref[...], dhb)

        @pl.when(k == nk - 1)
        def _():
            for j in range(nsh):
                dw_ref[j] = acc[j * Rs:(j + 1) * Rs, :].astype(BF16)

    return _call(
        body, hosted, name=name, grid=(nk,),
        in_specs=[pl.BlockSpec((tk, D), lambda k: (k, 0)), pl.BlockSpec((2, W, D), lambda k: (0, 0, 0)),
                  pl.BlockSpec((tk, W), lambda k: (k, 0)), pl.BlockSpec((tk, W), lambda k: (k, 0))],
        out_specs=[pl.BlockSpec((tk, W), lambda k: (k, 0)), pl.BlockSpec((tk, W), lambda k: (k, 0)),
                   pl.BlockSpec((nsh, Rs, D), lambda k: (0, 0, 0))],
        out_shape=[jax.ShapeDtypeStruct((T, W), F32), jax.ShapeDtypeStruct((T, W), F32),
                   jax.ShapeDtypeStruct((nsh, Rs, D), BF16)],
        scratch_shapes=[pltpu.VMEM((2 * W, D), F32)],
        args=[dh, wout, a, b])


def _dproj_block(g):
    return (g // N_GROUPS + N_GROUPS) % (N_GROUPS + 1), g % N_GROUPS


def mix_dwin(u, dproj, nsh, name, hosted=()):
    T, D = u.shape
    Hd = HEAD_DIM
    slabs, _, width = dproj.shape
    blocks = slabs * width // Hd
    Cs = blocks * Hd // nsh
    tk = min(T, 512)
    nk = T // tk

    def body(u_ref, d_ref, o_ref, acc):
        k = pl.program_id(0)

        @pl.when(k == 0)
        def _():
            acc[...] = jnp.zeros_like(acc)

        where = [_dproj_block(g) for g in range(blocks)]
        d = jnp.concatenate([d_ref[slab, :, col * Hd:(col + 1) * Hd] for slab, col in where], axis=1)
        acc[...] += _dot_tn(u_ref[...], d)

        @pl.when(k == nk - 1)
        def _():
            for j in range(nsh):
                o_ref[j] = acc[:, j * Cs:(j + 1) * Cs].astype(BF16)

    return _call(
        body, hosted, name=name, grid=(nk,),
        in_specs=[pl.BlockSpec((tk, D), lambda k: (k, 0)), pl.BlockSpec((slabs, tk, width), lambda k: (0, k, 0))],
        out_specs=[pl.BlockSpec((nsh, D, Cs), lambda k: (0, 0, 0))],
        out_shape=[jax.ShapeDtypeStruct((nsh, D, Cs), BF16)],
        scratch_shapes=[pltpu.VMEM((D, blocks * Hd), F32)],
        args=[u, dproj])


def mix_in_bwd(dproj, wing, h, gain, dh, name, hosted=()):
    T, D = h.shape
    nsh, _, Cs = wing.shape
    Hd = HEAD_DIM
    per = Cs // Hd
    tm = min(T, 512)

    def body(d_ref, w_ref, h_ref, g_ref, dh_ref, o_ref, dg_ref):
        def shard(j):
            blocks = [_dproj_block(per * j + i) for i in range(per)]
            return jnp.concatenate([d_ref[slab, :, col * Hd:(col + 1) * Hd] for slab, col in blocks], axis=1)

        du = _dot_nt(shard(0), w_ref[0])
        for j in range(1, nsh):
            du += _dot_nt(shard(j), w_ref[j])
        dhn, dg = _rmsnorm_bwd(du, h_ref[...], g_ref[...])
        o_ref[...] = dh_ref[...] + dhn

        @pl.when(pl.program_id(0) == 0)
        def _():
            dg_ref[...] = jnp.zeros_like(dg_ref)

        dg_ref[...] += jnp.sum(dg, axis=0, keepdims=True)

    row_spec = pl.BlockSpec((tm, D), lambda i: (i, 0))
    vec_spec = pl.BlockSpec((1, D), lambda i: (0, 0))
    return _call(
        body, hosted, name=name, grid=(T // tm,),
        in_specs=[pl.BlockSpec((dproj.shape[0], tm, dproj.shape[2]), lambda i: (0, i, 0)),
                  pl.BlockSpec((nsh, D, Cs), lambda i: (0, 0, 0)), row_spec, vec_spec, row_spec],
        out_specs=[row_spec, vec_spec],
        out_shape=[jax.ShapeDtypeStruct((T, D), F32), jax.ShapeDtypeStruct((1, D), F32)],
        args=[dproj, wing, h, gain, dh])


def _pool_window(x, group, T, trailing):
    rows = lax.broadcasted_iota(jnp.int32, x.shape, 0)

    def shifted(z, k):
        if trailing:
            return jnp.where(rows >= k, pltpu.roll(z, k, 0), 0.0)
        return jnp.where(rows < T - k, pltpu.roll(z, T - k, 0), 0.0)

    s2 = x + shifted(x, 1)
    s4 = s2 + shifted(s2, 2)
    s8 = s4 + shifted(s4, 4)
    s16 = s8 + shifted(s8, 8)
    return jnp.where(group == 0, s2, jnp.where(group == 1, s4, jnp.where(group == 2, s8, s16)))


def _pool_count(group, shape):
    rows = lax.broadcasted_iota(jnp.int32, shape, 0)
    w = jnp.where(group == 0, 2, jnp.where(group == 1, 4, jnp.where(group == 2, 8, 16)))
    return jnp.minimum(rows + 1, w).astype(F32)


def pool_fwd(proj, pool_w, pool_scale, name, hosted=()):
    T = proj.shape[0]
    Hd = HEAD_DIM

    def body(x_ref, w_ref, sc_ref, a_ref):
        g = pl.program_id(0)
        x = x_ref[...]
        pooled = _pool_window(x, g, T, True) / _pool_count(g, x.shape) - x
        a_ref[...] = (_dot(pooled.astype(BF16), w_ref[0].astype(BF16)) * sc_ref[...]).astype(BF16)

    return _call(
        body, hosted, name=name, grid=(N_GROUPS,),
        in_specs=[pl.BlockSpec((T, Hd), lambda g: (0, g)), pl.BlockSpec((1, Hd, Hd), lambda g: (g, 0, 0)),
                  pl.BlockSpec((1, Hd), lambda g: (0, g))],
        out_specs=[pl.BlockSpec((T, Hd), lambda g: (0, g))],
        out_shape=[jax.ShapeDtypeStruct((T, N_GROUPS * Hd), BF16)],
        args=[proj, pool_w, pool_scale])


def pool_bwd(proj, da, pool_w, pool_scale, name, hosted=()):
    T = proj.shape[0]
    Hd = HEAD_DIM

    def body(x_ref, da_ref, w_ref, sc_ref, dx_ref, dw_ref, dsc_ref):
        g = pl.program_id(0)
        x = x_ref[...]
        cnt = _pool_count(g, x.shape)
        pooled = (_pool_window(x, g, T, True) / cnt - x).astype(BF16)
        wb = w_ref[0].astype(BF16)
        dav = da_ref[...]
        dsc_ref[...] = jnp.sum(dav * _dot(pooled, wb), axis=0, keepdims=True)
        dout = (dav * sc_ref[...]).astype(BF16)
        dw_ref[0] = _dot_tn(pooled, dout)
        dpooled = _dot_nt(dout, wb)
        dx_ref[0] = (_pool_window(dpooled / cnt, g, T, False) - dpooled).astype(BF16)

    col_spec = pl.BlockSpec((T, Hd), lambda g: (0, g))
    return _call(
        body, hosted, name=name, grid=(N_GROUPS,),
        in_specs=[col_spec, col_spec, pl.BlockSpec((1, Hd, Hd), lambda g: (g, 0, 0)), pl.BlockSpec((1, Hd), lambda g: (0, g))],
        out_specs=[pl.BlockSpec((1, T, Hd), lambda g: (N_GROUPS, 0, g)), pl.BlockSpec((1, Hd, Hd), lambda g: (g, 0, 0)),
                   pl.BlockSpec((1, Hd), lambda g: (0, g))],
        out_shape=[jax.ShapeDtypeStruct((N_GROUPS + 1, T, N_GROUPS * Hd), BF16), jax.ShapeDtypeStruct((N_GROUPS, Hd, Hd), F32),
                   jax.ShapeDtypeStruct((1, N_GROUPS * Hd), F32)],
        args=[proj, da, pool_w, pool_scale])


def _ret_tables(T):
    Hd, C = HEAD_DIM, RET_CHUNK
    inv_freq = 1.0 / (ROPE_BASE ** (jnp.arange(0, Hd, 2, dtype=F32) / Hd))
    ang = jnp.arange(T, dtype=F32)[:, None] * inv_freq[None, :]
    cos, sin = jnp.cos(ang), jnp.sin(ang)
    cos2 = jnp.concatenate([cos, cos], axis=-1)
    sin2 = jnp.concatenate([-sin, sin], axis=-1)
    log_gamma = jnp.log1p(-jnp.exp2(-5.0 - jnp.arange(N_GROUPS, dtype=F32)))
    pos = jnp.arange(C, dtype=F32)
    rel = pos[:, None] - pos[None, :]
    intra = jnp.where(rel[None] >= 0, jnp.exp(log_gamma[:, None, None] * jnp.maximum(rel, 0.0)[None]), 0.0)
    k_tail = jnp.exp(log_gamma[:, None] * (C - 1 - pos)[None, :])
    q_head = jnp.exp(log_gamma[:, None] * (pos + 1.0)[None, :])
    chunk_decay = jnp.exp(log_gamma * C)
    wide = lambda t: jnp.broadcast_to(t[:, :, None], (N_GROUPS, C, Hd))
    return cos2, sin2, intra, wide(k_tail), wide(q_head), jnp.broadcast_to(chunk_decay[:, None, None], (N_GROUPS, 1, Hd))


def _rope(x, cos2, sin2):
    return x * cos2 + pltpu.roll(x, HEAD_DIM // 2, 1) * sin2


def _rope_t(d, cos2, sin2):
    return d * cos2 + pltpu.roll(d * sin2, HEAD_DIM // 2, 1)


def _ret_specs(T, tseg, seg_of):
    Hd, G = HEAD_DIM, N_GROUPS
    col = lambda kind: pl.BlockSpec((tseg, Hd), lambda h, s: (seg_of(s), G * kind + h))
    tab = pl.BlockSpec((T, Hd), lambda h, s: (0, 0))
    head = pl.BlockSpec((1, RET_CHUNK, Hd), lambda h, s: (h, 0, 0))
    cd = pl.BlockSpec((1, 1, Hd), lambda h, s: (h, 0, 0))
    gain = pl.BlockSpec((1, Hd), lambda h, s: (0, h))
    return col, tab, head, cd, gain


def ret_fwd(proj, ret_norm, tables, name, hosted=()):
    T = proj.shape[0]
    Hd, C, G = HEAD_DIM, RET_CHUNK, N_GROUPS
    tseg = min(T, 2048)
    nseg, nck = T // tseg, tseg // C
    scale = Hd ** -0.5
    cos2, sin2, intra, k_tail, q_head, chunk_decay = tables

    def body(q_ref, k_ref, v_ref, g_ref, gain_ref, cos_ref, sin_ref, m_ref, kt_ref, qh_ref, cd_ref,
             b_ref, o_ref, rp_ref, state):
        @pl.when(pl.program_id(1) == 0)
        def _():
            state[...] = jnp.zeros_like(state)

        def chunk(ci, carry):
            rows = pl.ds(pl.multiple_of(ci * C, C), C)
            at = pl.ds(pl.multiple_of(pl.program_id(1) * tseg + ci * C, C), C)
            cos, sin = cos_ref[at, :], sin_ref[at, :]
            qr = _rope(q_ref[rows, :], cos, sin)
            kr = _rope(k_ref[rows, :], cos, sin) * scale
            qb, kb, vb = qr.astype(BF16), kr.astype(BF16), v_ref[rows, :].astype(BF16)
            r = state[...]
            rp_ref[0, ci] = r.astype(BF16)
            sc = _dot_nt(qb, kb) * m_ref[0]
            o = _dot(sc.astype(BF16), vb) + _dot((qr * qh_ref[0]).astype(BF16), r.astype(BF16))
            state[...] = cd_ref[0] * r + _dot_tn((kr * kt_ref[0]).astype(BF16), vb)
            o_ref[rows, :] = o
            on = o * _rstd(o)
            b_ref[rows, :] = (jax.nn.silu(g_ref[rows, :]) * (on * gain_ref[...])).astype(BF16)
            return carry

        lax.fori_loop(0, nck, chunk, 0, unroll=True)

    col, tab, head, cd, gain = _ret_specs(T, tseg, lambda s: s)
    out_col = pl.BlockSpec((tseg, Hd), lambda h, s: (s, h))
    return _call(
        body, hosted, name=name, grid=(G, nseg),
        in_specs=[col(1), col(2), col(3), col(4), gain, tab, tab, head, head, head, cd],
        out_specs=[out_col, out_col, pl.BlockSpec((1, nck, Hd, Hd), lambda h, s: (h, s, 0, 0))],
        out_shape=[jax.ShapeDtypeStruct((T, G * Hd), BF16), jax.ShapeDtypeStruct((T, G * Hd), F32),
                   jax.ShapeDtypeStruct((G, T // C, Hd, Hd), BF16)],
        scratch_shapes=[pltpu.VMEM((Hd, Hd), F32)],
        args=[proj, proj, proj, proj, ret_norm, cos2, sin2, intra, k_tail, q_head, chunk_decay])


def ret_bwd(proj, db, o_pre, r_prev, ret_norm, tables, dproj, name, hosted=()):
    T = proj.shape[0]
    Hd, C, G = HEAD_DIM, RET_CHUNK, N_GROUPS
    tseg = min(T, 2048)
    nseg, nck = T // tseg, tseg // C
    scale = Hd ** -0.5
    cos2, sin2, intra, k_tail, q_head, chunk_decay = tables

    def body(q_ref, k_ref, v_ref, g_ref, db_ref, o_ref, rp_ref, gain_ref, cos_ref, sin_ref, m_ref, kt_ref, qh_ref, cd_ref,
             _, d_ref, dgain_ref, gstate):
        @pl.when(pl.program_id(1) == 0)
        def _():
            gstate[...] = jnp.zeros_like(gstate)
            dgain_ref[...] = jnp.zeros_like(dgain_ref)

        def chunk(t, carry):
            ci = nck - 1 - t
            rows = pl.ds(pl.multiple_of(ci * C, C), C)
            at = pl.ds(pl.multiple_of((nseg - 1 - pl.program_id(1)) * tseg + ci * C, C), C)
            cos, sin = cos_ref[at, :], sin_ref[at, :]
            qr = _rope(q_ref[rows, :], cos, sin)
            kr = _rope(k_ref[rows, :], cos, sin) * scale
            qb, kb, vb = qr.astype(BF16), kr.astype(BF16), v_ref[rows, :].astype(BF16)
            qhb, ktb = (qr * qh_ref[0]).astype(BF16), (kr * kt_ref[0]).astype(BF16)
            sc = (_dot_nt(qb, kb) * m_ref[0]).astype(BF16)
            o = o_ref[rows, :]
            rstd = _rstd(o)
            on = o * rstd
            gain = gain_ref[...]
            silu, dsilu = _silu_parts(g_ref[rows, :])
            dy = db_ref[rows, :]
            dgain_ref[...] += jnp.sum(dy * silu * on, axis=0, keepdims=True)
            dg = dy * on * gain * dsilu
            don = dy * silu * gain
            dob = (rstd * (don - on * jnp.mean(don * on, axis=-1, keepdims=True))).astype(BF16)
            gn = gstate[...]
            gb = gn.astype(BF16)
            da = (_dot_nt(dob, vb) * m_ref[0]).astype(BF16)
            dq = _dot(da, kb) + _dot_nt(dob, rp_ref[0, ci]) * qh_ref[0]
            dk = _dot_tn(da, qb) + _dot_nt(vb, gb) * kt_ref[0]
            dv = _dot_tn(sc, dob) + _dot(ktb, gb)
            gstate[...] = cd_ref[0] * gn + _dot_tn(qhb, dob)
            d_ref[0, rows, :] = _rope_t(dq, cos, sin).astype(BF16)
            d_ref[1, rows, :] = _rope_t(dk * scale, cos, sin).astype(BF16)
            d_ref[2, rows, :] = dv.astype(BF16)
            d_ref[3, rows, :] = dg.astype(BF16)
            return carry

        lax.fori_loop(0, nck, chunk, 0, unroll=True)

    rev = lambda s: nseg - 1 - s
    col, tab, head, cd, gain = _ret_specs(T, tseg, rev)
    act = pl.BlockSpec((tseg, Hd), lambda h, s: (rev(s), h))
    return _call(
        body, hosted, name=name, grid=(G, nseg),
        in_specs=[col(1), col(2), col(3), col(4), act, act, pl.BlockSpec((1, nck, Hd, Hd), lambda h, s: (h, rev(s), 0, 0)),
                  gain, tab, tab, head, head, head, cd, ANY],
        out_specs=[pl.BlockSpec((4, tseg, Hd), lambda h, s: (0, rev(s), h)), gain],
        out_shape=[jax.ShapeDtypeStruct(dproj.shape, BF16), jax.ShapeDtypeStruct((1, G * Hd), F32)],
        scratch_shapes=[pltpu.VMEM((Hd, Hd), F32)], aliased={14: 0},
        args=[proj, proj, proj, proj, db, o_pre, r_prev, ret_norm, cos2, sin2, intra, k_tail, q_head, chunk_decay, dproj])


def ffn_down_loss(s, w2, h, gain, target, name, hosted=()):
    T, F = s.shape
    D = h.shape[1]
    tm = min(T, 512)

    def body(s_ref, w2_ref, h_ref, g_ref, t_ref, dh_ref, loss_ref, dg_ref):
        @pl.when(pl.program_id(0) == 0)
        def _():
            loss_ref[...] = jnp.zeros_like(loss_ref)
            dg_ref[...] = jnp.zeros_like(dg_ref)

        hh = h_ref[...] + 0.5 * _dot(s_ref[...], w2_ref[...])
        gain_v = g_ref[...]
        err = hh * _rstd(hh) * gain_v - t_ref[...]
        loss_ref[...] += 0.5 * jnp.sum(jnp.mean(err * err, axis=-1, keepdims=True), axis=0, keepdims=True)
        dhn, dg = _rmsnorm_bwd(err * (1.0 / D), hh, gain_v)
        dh_ref[...] = dhn
        dg_ref[...] += jnp.sum(dg, axis=0, keepdims=True)

    row_spec = pl.BlockSpec((tm, D), lambda i: (i, 0))
    vec_spec = pl.BlockSpec((1, D), lambda i: (0, 0))
    return _call(
        body, hosted, name=name, grid=(T // tm,),
        in_specs=[pl.BlockSpec((tm, F), lambda i: (i, 0)), _resident((F, D)), row_spec, vec_spec, row_spec],
        out_specs=[row_spec, pl.BlockSpec((1, LANES), lambda i: (0, 0)), vec_spec],
        out_shape=[jax.ShapeDtypeStruct((T, D), F32), jax.ShapeDtypeStruct((1, LANES), F32), jax.ShapeDtypeStruct((1, D), F32)],
        args=[s, w2, h, gain, target])


def prereduce(grads, recvs, place, name):
    nt = len(grads)
    nsh, R, C = grads[0].shape
    rh = R // 2

    def body(place_ref, *refs):
        for t in range(nt):
            g_ref, r_ref, o_ref, own_ref = refs[2 * t], refs[2 * t + 1], refs[2 * nt + 2 * t], refs[2 * nt + 2 * t + 1]
            piece = (g_ref[...].astype(F32) + r_ref[...].astype(F32)).astype(BF16)
            o_ref[...] = piece

            @pl.when(pl.program_id(0) == place_ref[1])
            def _():
                own_ref[...] = piece

    outs = pl.pallas_call(
        body, name=name,
        grid_spec=pltpu.PrefetchScalarGridSpec(
            num_scalar_prefetch=1, grid=(nsh,),
            in_specs=[pl.BlockSpec((1, rh, C), lambda j, p: (j, p[0], 0)), pl.BlockSpec((1, rh, C), lambda j, p: (j, 0, 0))] * nt,
            out_specs=[pl.BlockSpec((1, rh, C), lambda j, p: (j, 0, 0)),
                       pl.BlockSpec((1, rh, C), lambda j, p: (p[1], p[0], 0))] * nt),
        out_shape=[jax.ShapeDtypeStruct((nsh, rh, C), BF16), jax.ShapeDtypeStruct((nsh, R, C), BF16)] * nt,
        compiler_params=pltpu.CompilerParams(vmem_limit_bytes=VMEM_LIMIT_V7X),
    )(place, *[a for pair in zip(grads, recvs) for a in pair])
    return [(outs[2 * t], outs[2 * t + 1]) for t in range(nt)]


def _adamw(w, g, m, v):
    m = ADAM_B1 * m + (1.0 - ADAM_B1) * g
    v = ADAM_B2 * v + (1.0 - ADAM_B2) * (g * g)
    m_hat = m / (1.0 - ADAM_B1 ** ADAM_STEP)
    v_hat = v / (1.0 - ADAM_B2 ** ADAM_STEP)
    return -ADAM_LR * (m_hat / (jnp.sqrt(v_hat) + ADAM_EPS) + ADAM_WD * w), m, v


def adamw_sharded(tensors, name, hosted=()):
    nt = len(tensors)
    nsh = tensors[0][0].shape[0]
    shapes = [t[0].shape[1:] for t in tensors]

    def fits(steps):
        if any(R % (steps * BF16_TILE_ROWS) for R, _ in shapes):
            return False
        return sum(2 * (R // steps) * -(-C // LANES) * LANES * (nsh * 2 + 7 * 4) for R, C in shapes) <= ADAMW_VMEM_BUDGET

    steps = min(s for s in range(1, min(R for R, _ in shapes) // BF16_TILE_ROWS + 1) if fits(s))

    def body(*refs):
        ins, outs = refs[:4 * nt], refs[4 * nt:]
        for t in range(nt):
            p_ref, w_ref, m_ref, v_ref = ins[4 * t:4 * t + 4]
            g_ref, d_ref, nm_ref, nv_ref = outs[4 * t:4 * t + 4]
            g = p_ref[0].astype(F32)
            for i in range(1, nsh):
                g += p_ref[i].astype(F32)
            g_ref[...] = g
            d_ref[...], nm_ref[...], nv_ref[...] = _adamw(w_ref[...], g, m_ref[...], v_ref[...])

    in_specs, out_specs, out_shape = [], [], []
    for R, C in shapes:
        spec = pl.BlockSpec((R // steps, C), lambda i: (i, 0))
        in_specs += [pl.BlockSpec((nsh, R // steps, C), lambda i: (0, i, 0)), spec, spec, spec]
        out_specs += [spec] * 4
        out_shape += [jax.ShapeDtypeStruct((R, C), F32)] * 4
    return _call(body, hosted, name=name, grid=(steps,), in_specs=in_specs, out_specs=out_specs, out_shape=out_shape,
                 args=[a for tensor in tensors for a in tensor])


def adamw_small(packs, params, loss_packs, name):
    n = len(packs)
    ndev = loss_packs.shape[0]

    def body(*refs):
        p_refs, loss_ref, wmv = refs[:n], refs[n], refs[n + 1:4 * n + 1]
        outs, loss_out = refs[4 * n + 1:8 * n + 1], refs[8 * n + 1]
        total = lambda r: sum((r[i] for i in range(1, ndev)), r[0])
        loss_out[...] = total(loss_ref)
        for k in range(n):
            g = total(p_refs[k])
            outs[4 * k][...] = g
            outs[4 * k + 1][...], outs[4 * k + 2][...], outs[4 * k + 3][...] = _adamw(
                wmv[3 * k][...], g, wmv[3 * k + 1][...], wmv[3 * k + 2][...])

    out_shape = [jax.ShapeDtypeStruct(p[0].shape, F32) for p in params for _ in range(4)]
    outs = pl.pallas_call(body, name=name, out_shape=out_shape + [jax.ShapeDtypeStruct(loss_packs.shape[1:], F32)],
                          compiler_params=pltpu.CompilerParams(vmem_limit_bytes=VMEM_LIMIT_V7X),
                          )(*packs, loss_packs, *[a for p in params for a in p])
    return [outs[4 * k:4 * k + 4] for k in range(n)], outs[4 * n]


BIG = ("ffn1_w1", "ffn1_w3", "ffn1_w2", "w_in", "w_out", "ffn2_w1", "ffn2_w3", "ffn2_w2")
TRANSPOSED = ("ffn1_w1", "ffn1_w3", "ffn2_w1", "ffn2_w3")
SMALL = ("pool_w", "mix_norm", "pool_scale", "ret_norm", "ffn2_norm", "final_norm", "ffn1_norm")
WEIGHTS = ("ffn1_norm", "ffn1_w1", "ffn1_w3", "ffn1_w2", "mix_norm", "w_in", "pool_w", "pool_scale", "ret_norm", "w_out",
           "ffn2_norm", "ffn2_w1", "ffn2_w3", "ffn2_w2", "final_norm")


def kernel(x, ffn1_norm, ffn1_w1, ffn1_w3, ffn1_w2, mix_norm, w_in, pool_w, pool_scale, ret_norm, w_out, ffn2_norm, ffn2_w1, ffn2_w3, ffn2_w2, final_norm, loss_target, m_ffn1_norm, m_ffn1_w1, m_ffn1_w3, m_ffn1_w2, m_mix_norm, m_w_in, m_pool_w, m_pool_scale, m_ret_norm, m_w_out, m_ffn2_norm, m_ffn2_w1, m_ffn2_w3, m_ffn2_w2, m_final_norm, v_ffn1_norm, v_ffn1_w1, v_ffn1_w3, v_ffn1_w2, v_mix_norm, v_w_in, v_pool_w, v_pool_scale, v_ret_norm, v_w_out, v_ffn2_norm, v_ffn2_w1, v_ffn2_w3, v_ffn2_w2, v_final_norm):
    w = dict(ffn1_norm=ffn1_norm, ffn1_w1=ffn1_w1, ffn1_w3=ffn1_w3, ffn1_w2=ffn1_w2, mix_norm=mix_norm, w_in=w_in, pool_w=pool_w,
             pool_scale=pool_scale, ret_norm=ret_norm, w_out=w_out, ffn2_norm=ffn2_norm, ffn2_w1=ffn2_w1, ffn2_w3=ffn2_w3,
             ffn2_w2=ffn2_w2, final_norm=final_norm)
    m = dict(ffn1_norm=m_ffn1_norm, ffn1_w1=m_ffn1_w1, ffn1_w3=m_ffn1_w3, ffn1_w2=m_ffn1_w2, mix_norm=m_mix_norm, w_in=m_w_in,
             pool_w=m_pool_w, pool_scale=m_pool_scale, ret_norm=m_ret_norm, w_out=m_w_out, ffn2_norm=m_ffn2_norm, ffn2_w1=m_ffn2_w1,
             ffn2_w3=m_ffn2_w3, ffn2_w2=m_ffn2_w2, final_norm=m_final_norm)
    v = dict(ffn1_norm=v_ffn1_norm, ffn1_w1=v_ffn1_w1, ffn1_w3=v_ffn1_w3, ffn1_w2=v_ffn1_w2, mix_norm=v_mix_norm, w_in=v_w_in,
             pool_w=v_pool_w, pool_scale=v_pool_scale, ret_norm=v_ret_norm, w_out=v_w_out, ffn2_norm=v_ffn2_norm, ffn2_w1=v_ffn2_w1,
             ffn2_w3=v_ffn2_w3, ffn2_w2=v_ffn2_w2, final_norm=v_final_norm)
    xs, target = x[0], loss_target[0]
    T = xs.shape[0]
    tables = _ret_tables(T)
    place = jnp.stack([lax.axis_index("c"), 2 * lax.axis_index("x") + lax.axis_index("y")]).astype(jnp.int32)
    local = lambda d, k: jnp.transpose(d[k][0]) if k in TRANSPOSED else d[k][0]
    result = lambda o, k: jnp.transpose(o)[None] if k in TRANSPOSED else o[None]
    first = ("ffn1_w1", "ffn1_w3")
    sh = {k: local(w, k).astype(BF16) for k in first}
    gather = lambda *names: [ChipExchange([sh[k] for k in names], False)]
    wg, grad, delta, new_m, new_v = {}, {}, {}, {}, {}

    def update(names, pieces, name, hosted=()):
        outs, extras = adamw_sharded([(p, local(w, k), local(m, k), local(v, k)) for k, p in zip(names, pieces)], name, hosted)
        for t, k in enumerate(names):
            grad[k], delta[k], new_m[k], new_v[k] = [result(o, k) for o in outs[4 * t:4 * t + 4]]
        return extras

    def reduce_in_chip(name, *pairs):
        reduced = prereduce([p for p, _ in pairs], [r for _, r in pairs], place, "prereduce_" + name)
        return reduced[0] if len(pairs) == 1 else reduced

    scatter = lambda *reduced: ChipExchange([r[0] for r in reduced], True, [r[1] for r in reduced])
    whole = lambda k: wg[k].reshape(-1, wg[k].shape[-1])
    sharded = lambda g: g.reshape(N_CHIPS, -1, g.shape[-1])

    later = [k for k in BIG if k not in first]
    casts, ((wg["ffn1_w1"], wg["ffn1_w3"]),) = cast_shards([local(w, k) for k in later], "cast_gather_ffn1", gather(*first))
    sh.update(zip(later, casts))
    (n1, ga1, gb1, s1), ((wg["ffn1_w2"], wg["w_in"]),) = ffn_up(
        xs, ffn1_norm, whole("ffn1_w1"), whole("ffn1_w3"), "ffn1_up", gather("ffn1_w2", "w_in"))
    (h1, u, proj), ((wg["w_out"], wg["ffn2_w1"]),) = ffn_down_mix_in(
        s1, whole("ffn1_w2"), xs, mix_norm, wg["w_in"], "ffn1_down_mix_in", gather("w_out", "ffn2_w1"))
    (pa,), _ = pool_fwd(proj, pool_w[0], pool_scale, "pool_fwd")
    (rb, o_pre, r_prev), ((wg["ffn2_w3"],),) = ret_fwd(proj, ret_norm, tables, "ret_fwd", gather("ffn2_w3"))
    (h2, n2, ga2, gb2, s2), ((wg["ffn2_w2"],),) = ffn_up(
        h1, ffn2_norm, whole("ffn2_w1"), whole("ffn2_w3"), "mix_out_ffn2_up", gather("ffn2_w2"), mixed=(pa, rb, wg["w_out"]))
    (dh3, loss, d_final), _ = ffn_down_loss(s2, whole("ffn2_w2"), h2, final_norm[None], target, "ffn2_down_loss")

    (da2, db2, df2), _ = ffn_bwd_act(dh3, whole("ffn2_w2"), ga2, gb2, "ffn2_bwd_act")
    (g_f2w2,), _ = ffn_dw([s2], df2, 1, "ffn2_dw2")
    g_f2w2 = sharded(g_f2w2)
    (g_f2w1, g_f2w3), ((r_f2w2,),) = ffn_dw([da2, db2], n2, 2, "ffn2_dw13", [SiblingExchange([g_f2w2])])
    g_f2w1, g_f2w3 = sharded(g_f2w1), sharded(g_f2w3)
    p_f2w2 = reduce_in_chip("ffn2_w2", (g_f2w2, r_f2w2))
    (dh2, d_ffn2), ((q_f2w2,), (r_f2w1, r_f2w3)) = ffn_bwd_in(
        da2, db2, whole("ffn2_w1"), whole("ffn2_w3"), h2, ffn2_norm, dh3, "ffn2_bwd_in",
        [scatter(p_f2w2), SiblingExchange([g_f2w1, g_f2w3])])
    p_f2w1, p_f2w3 = reduce_in_chip("ffn2_w13", (g_f2w1, r_f2w1), (g_f2w3, r_f2w3))
    (dpa, drb, g_wout), _ = mix_out_bwd(dh2, wg["w_out"], pa, rb, "mix_out_bwd")
    (dproj, d_pool_w, d_pool_scale), _ = pool_bwd(proj, dpa, pool_w[0], pool_scale, "pool_bwd")
    (dproj, d_ret_norm), ((q_f2w1, q_f2w3), (r_wout,)) = ret_bwd(
        proj, drb, o_pre, r_prev, ret_norm, tables, dproj, "ret_bwd", [scatter(p_f2w1, p_f2w3), SiblingExchange([g_wout])])
    p_wout = reduce_in_chip("w_out", (g_wout, r_wout))
    (g_win,), ((q_wout,),) = mix_dwin(u, dproj, N_CHIPS, "mix_dwin", [scatter(p_wout)])
    (dh1, d_mix), ((r_win,),) = mix_in_bwd(dproj, wg["w_in"], h1, mix_norm, dh2, "mix_in_bwd", [SiblingExchange([g_win])])
    p_win = reduce_in_chip("w_in", (g_win, r_win))
    (da1, db1, df1), ((q_win,),) = ffn_bwd_act(dh1, whole("ffn1_w2"), ga1, gb1, "ffn1_bwd_act", [scatter(p_win)])
    d_small = {"pool_w": d_pool_w.reshape(-1, LANES), "mix_norm": d_mix, "pool_scale": d_pool_scale, "ret_norm": d_ret_norm,
               "ffn2_norm": d_ffn2, "final_norm": d_final}
    (g_f1w1, g_f1w3), (packs,) = ffn_dw([da1, db1], n1, 2, "ffn1_dw13", [AllExchange([d_small[k] for k in SMALL[:-1]] + [loss])])
    g_f1w1, g_f1w3 = sharded(g_f1w1), sharded(g_f1w3)
    (g_f1w2,), ((r_f1w1, r_f1w3),) = ffn_dw([s1], df1, 1, "ffn1_dw2", [SiblingExchange([g_f1w1, g_f1w3])])
    g_f1w2 = sharded(g_f1w2)
    p_f1w1, p_f1w3 = reduce_in_chip("ffn1_w13", (g_f1w1, r_f1w1), (g_f1w3, r_f1w3))
    (dx, d_ffn1), ((q_f1w1, q_f1w3), (r_f1w2,)) = ffn_bwd_in(
        da1, db1, whole("ffn1_w1"), whole("ffn1_w3"), xs, ffn1_norm, dh1, "ffn1_bwd_in",
        [scatter(p_f1w1, p_f1w3), SiblingExchange([g_f1w2])])
    p_f1w2 = reduce_in_chip("ffn1_w2", (g_f1w2, r_f1w2))

    (q_f1w2,), (late,) = update(["w_in", "w_out", "ffn2_w2"], [q_win, q_wout, q_f2w2], "adamw_mix_w2",
                                [scatter(p_f1w2), AllExchange([d_ffn1])])
    update(["ffn2_w1", "ffn2_w3", "ffn1_w1", "ffn1_w3"], [q_f2w1, q_f2w3, q_f1w1, q_f1w3], "adamw_w13")
    update(["ffn1_w2"], [q_f1w2], "adamw_ffn1_w2")
    flat = lambda t, k: t[k].reshape(-1, LANES) if k == "pool_w" else t[k].reshape(1, -1)
    updated, loss_sum = adamw_small(packs[:-1] + [late], [[flat(t, k) for t in (w, m, v)] for k in SMALL], packs[-1], "adamw_small")
    for k, outs in zip(SMALL, updated):
        grad[k], delta[k], new_m[k], new_v[k] = [o.reshape(w[k].shape) for o in outs]
    loss = loss_sum[0, 0]

    return (loss, dx[None], *[grad[k] for k in WEIGHTS], *[delta[k] for k in WEIGHTS],
            *[new_m[k] for k in WEIGHTS], *[new_v[k] for k in WEIGHTS])
```

```python
import math

import jax
import jax.numpy as jnp
from jax import lax
from jax.experimental import pallas as pl
from jax.experimental.pallas import tpu as pltpu

F32 = jnp.float32
BF16 = jnp.bfloat16

EPS = 1e-6
LANES = 128
BF16_TILE_ROWS = 16
N_CHIPS = 4
N_GROUPS = 4
HEAD_DIM = 128
RET_CHUNK = 128
ROPE_BASE = 10000.0
ADAM_LR, ADAM_B1, ADAM_B2, ADAM_EPS, ADAM_WD, ADAM_STEP = 0.001, 0.9, 0.999, 1e-08, 0.01, 10
VMEM_LIMIT_V7X = 56 * 1024 * 1024
ADAMW_VMEM_BUDGET = 32 * 1024 * 1024
MESH = pl.DeviceIdType.MESH
ANY = pl.BlockSpec(memory_space=pl.ANY)


def _dot(a, b):
    return jnp.dot(a, b, preferred_element_type=F32)


def _dot_nt(a, b):
    return lax.dot_general(a, b, (((1,), (1,)), ((), ())), preferred_element_type=F32)


def _dot_tn(a, b):
    return lax.dot_general(a, b, (((0,), (0,)), ((), ())), preferred_element_type=F32)


def _rstd(h):
    return lax.rsqrt(jnp.mean(h * h, axis=-1, keepdims=True) + EPS)


def _rmsnorm_bwd(dn, h, gain):
    r = _rstd(h)
    nh = h * r
    dnh = dn * gain
    dh = r * (dnh - nh * jnp.mean(dnh * nh, axis=-1, keepdims=True))
    return dh, dn * nh


def _silu_parts(a):
    sig = jax.nn.sigmoid(a)
    silu = a * sig
    return silu, sig + silu * (1.0 - sig)


def _mesh_pos():
    return lax.axis_index("x"), lax.axis_index("y"), lax.axis_index("c")


class ChipExchange:
    def __init__(self, srcs, scatter, placed=()):
        n = len(srcs)
        self.inputs, self.scatter, self.n, self.reach = list(srcs) + list(placed), scatter, n, REACH_CHIPS
        self.aliases = {n + t: t for t in range(n)} if scatter else {}
        self.half_rows = [s.shape[1] if scatter else s.shape[0] // 2 for s in srcs]
        self.out_shape = [jax.ShapeDtypeStruct((N_CHIPS, 2 * rh, s.shape[-1]), s.dtype) for s, rh in zip(srcs, self.half_rows)]
        if scatter:
            self.out_shape += [jax.ShapeDtypeStruct((2, rh // 2, s.shape[-1]), s.dtype) for s, rh in zip(srcs, self.half_rows)]
        dma = pltpu.SemaphoreType.DMA
        self.sems = [dma((4 * n,)), dma((4 * n,)), dma((2 * n,)), dma((2 * n,)), dma((4 * n,)), dma((4 * n,))]

    def _copies(self, src, out, sems):
        hop1_send, hop1_recv, hop2_send, hop2_recv, d2d_send, d2d_recv = sems
        x, y, c = _mesh_pos()
        me, dg = 2 * x + y, 2 * (1 - x) + (1 - y)
        sibling = (x, y, 1 - c)
        n = self.n
        mine, theirs = c, 1 - c

        def nb(a):
            nx, ny = x ^ (1 - a), y ^ a
            return 2 * nx + ny, (nx, ny, c)

        def remote(s, d, send, recv, k, to):
            return pltpu.make_async_remote_copy(src_ref=s, dst_ref=d, send_sem=send.at[k], recv_sem=recv.at[k],
                                                device_id=to, device_id_type=MESH)

        class Copies:
            def slot(_, t, chip, half):
                rh = self.half_rows[t]
                return out[t].at[chip, pl.ds(half * rh, rh), :]

            def quarter(_, t, chip, q):
                qh = self.half_rows[t] // 2
                return out[t].at[chip, pl.ds(mine * 2 * qh + q * qh, qh), :]

            def own_shard(k, t):
                return remote(src[t], out[t].at[me], d2d_send, d2d_recv, 4 * t + 3, sibling)

            def hop1(k, t, a, transit=False):
                rh = self.half_rows[t]
                chip, to = nb(a)
                if transit:
                    piece = src[t].at[dg, pl.ds(a * (rh // 2), rh // 2), :]
                    return remote(piece, out[n + t].at[a], hop1_send, hop1_recv, 4 * t + 2 + a, to)
                piece = src[t].at[chip] if self.scatter else src[t].at[pl.ds(mine * rh, rh), :]
                return remote(piece, k.slot(t, me, mine), hop1_send, hop1_recv, 4 * t + a, to)

            def landed1(k, t, a, transit=False):
                here = out[n + t].at[a] if transit else k.slot(t, nb(a)[0], mine)
                return remote(here, here, hop1_send, hop1_recv, 4 * t + (2 if transit else 0) + a, sibling)

            def hop2(k, t, q):
                origin, to = nb(q)[0], nb(1 - q)[1]
                piece = out[n + t].at[q] if self.scatter else k.quarter(t, origin, q)
                return remote(piece, k.quarter(t, origin, q), hop2_send, hop2_recv, 2 * t + q, to)

            def landed2(k, t, q):
                here = k.quarter(t, dg, q)
                return remote(here, here, hop2_send, hop2_recv, 2 * t + q, sibling)

            def d2d(k, t, p, chip, own=False, arriving=False):
                if arriving:
                    there = k.slot(t, chip, theirs)
                    return remote(there, there, d2d_send, d2d_recv, 4 * t + p, sibling)
                piece = src[t].at[me] if own else k.slot(t, chip, mine)
                return remote(piece, k.slot(t, chip, mine), d2d_send, d2d_recv, 4 * t + p, sibling)

        return Copies(), nb, me, dg, c

    def start(self, src, out, sems):
        k, nb, me, dg, c = self._copies(src, out, sems)
        for t in range(self.n):
            for first in range(2):
                a = first ^ c
                k.hop1(t, a).start()
                if self.scatter:
                    k.hop1(t, a, transit=True).start()
            if self.scatter:
                k.d2d(t, 3, me, own=True).start()
            else:
                k.own_shard(t).start()

    def mid(self, src, out, sems):
        k, nb, me, dg, c = self._copies(src, out, sems)
        for t in range(self.n):
            for first in range(2):
                a = first ^ c
                if self.scatter:
                    k.landed1(t, a, transit=True).wait_recv()
                    k.hop2(t, a).start()
                k.landed1(t, a).wait_recv()
                if not self.scatter:
                    k.hop2(t, a).start()
                k.d2d(t, a, nb(a)[0]).start()

    def finish(self, src, out, sems):
        k, nb, me, dg, c = self._copies(src, out, sems)
        for t in range(self.n):
            for q in range(2):
                k.landed2(t, q).wait_recv()
            k.d2d(t, 2, dg).start()
        for t in range(self.n):
            for a in range(2):
                k.d2d(t, a, nb(a)[0], arriving=True).wait_recv()
            k.d2d(t, 2, dg, arriving=True).wait_recv()
            if self.scatter:
                k.d2d(t, 3, me, arriving=True).wait_recv()
        for t in range(self.n):
            for a in range(2):
                k.hop1(t, a).wait_send()
                if self.scatter:
                    k.hop1(t, a, transit=True).wait_send()
                k.hop2(t, a).wait_send()
                k.d2d(t, a, nb(a)[0]).wait_send()
            k.d2d(t, 2, dg).wait_send()
            if self.scatter:
                k.d2d(t, 3, me, own=True).wait_send()
            else:
                k.own_shard(t).wait()


class SiblingExchange:
    def __init__(self, grads):
        self.inputs, self.n, self.aliases, self.reach = list(grads), len(grads), {}, REACH_SIBLING
        self.half_rows = [g.shape[1] // 2 for g in grads]
        self.out_shape = [jax.ShapeDtypeStruct((g.shape[0], rh, g.shape[2]), g.dtype) for g, rh in zip(grads, self.half_rows)]
        self.sems = [pltpu.SemaphoreType.DMA((self.n,)), pltpu.SemaphoreType.DMA((self.n,))]

    def _plan(self, src, out, sems):
        x, y, c = _mesh_pos()
        return [pltpu.make_async_remote_copy(
            src_ref=src[t].at[:, pl.ds((1 - c) * self.half_rows[t], self.half_rows[t]), :], dst_ref=out[t],
            send_sem=sems[0].at[t], recv_sem=sems[1].at[t], device_id=(x, y, 1 - c), device_id_type=MESH) for t in range(self.n)]

    def start(self, src, out, sems):
        for cp in self._plan(src, out, sems):
            cp.start()

    def mid(self, src, out, sems):
        pass

    def finish(self, src, out, sems):
        for cp in self._plan(src, out, sems):
            cp.wait()


REACH_SIBLING, REACH_CHIPS, REACH_ALL = 0, 1, 2


def _entry_barrier(reach):
    x, y, c = _mesh_pos()
    peers = [(x, y, 1 - c)]
    if reach == REACH_CHIPS:
        peers += [(1 - x, y, c), (x, 1 - y, c)]
    elif reach == REACH_ALL:
        peers = [(x ^ dx, y ^ dy, c ^ dc) for dx in (0, 1) for dy in (0, 1) for dc in (0, 1)][1:]
    barrier = pltpu.get_barrier_semaphore()
    for peer in peers:
        pl.semaphore_signal(barrier, inc=1, device_id=peer, device_id_type=MESH)
    pl.semaphore_wait(barrier, len(peers))


def _call(body, hosted=(), *, name, in_specs, out_specs, out_shape, args, grid=(), scratch_shapes=(), aliased=None):
    n_in, n_out, n_scr = len(in_specs), len(out_specs), len(scratch_shapes)
    total = math.prod(grid)
    eighths = 7 if sum(h.n for h in hosted if isinstance(h, ChipExchange)) > 1 else 6
    mid_step = max(0, (eighths * total) // 8 - 1)

    def full(*refs):
        pos = [0]

        def take(k):
            pos[0] += k
            return refs[pos[0] - k:pos[0]]

        ins, h_in = take(n_in), [take(len(h.inputs)) for h in hosted]
        outs, h_out = take(n_out), [take(len(h.out_shape)) for h in hosted]
        scr, h_sem = take(n_scr), [take(len(h.sems)) for h in hosted]
        step = 0
        for axis, size in enumerate(grid):
            step = step * size + pl.program_id(axis)

        def phase(at, method):
            if not hosted:
                return

            def run():
                if method == "start":
                    _entry_barrier(reach)
                for h, s, o, m in zip(hosted, h_in, h_out, h_sem):
                    getattr(h, method)(s, o, m)

            if total == 1:
                run()
            else:
                pl.when(step == at)(run)

        phase(0, "start")
        body(*ins, *outs, *scr)
        phase(mid_step, "mid")
        phase(total - 1, "finish")

    aliases, i0, o0 = dict(aliased or {}), n_in, n_out
    for h in hosted:
        aliases.update({i0 + i: o0 + o for i, o in h.aliases.items()})
        i0, o0 = i0 + len(h.inputs), o0 + len(h.out_shape)
    reach = max((h.reach for h in hosted), default=None)
    params = dict(vmem_limit_bytes=VMEM_LIMIT_V7X)
    if hosted:
        params["collective_id"] = reach
    results = pl.pallas_call(
        full, name=name, grid=grid,
        in_specs=list(in_specs) + [ANY] * (i0 - n_in),
        out_specs=list(out_specs) + [ANY] * (o0 - n_out),
        out_shape=list(out_shape) + [s for h in hosted for s in h.out_shape],
        scratch_shapes=list(scratch_shapes) + [s for h in hosted for s in h.sems],
        input_output_aliases=aliases,
        compiler_params=pltpu.CompilerParams(**params),
    )(*args, *[s for h in hosted for s in h.inputs])
    outs, extras, pos = list(results[:n_out]), [], n_out
    for h in hosted:
        extras.append(list(results[pos:pos + h.n]))
        pos += len(h.out_shape)
    return outs, extras


def cast_shards(shards, name, hosted=()):
    n = len(shards)

    def body(*refs):
        for x_ref, o_ref in zip(refs[:n], refs[n:]):
            o_ref[...] = x_ref[...].astype(BF16)

    whole = lambda s: pl.BlockSpec(s.shape, lambda: (0,) * s.ndim)
    return _call(body, hosted, name=name, in_specs=[whole(s) for s in shards], out_specs=[whole(s) for s in shards],
                 out_shape=[jax.ShapeDtypeStruct(s.shape, BF16) for s in shards], args=list(shards))


class AllExchange:
    def __init__(self, arrays):
        n = len(arrays)
        self.inputs, self.n, self.aliases, self.reach = list(arrays), n, {}, REACH_ALL
        self.out_shape = [jax.ShapeDtypeStruct((2 * N_CHIPS,) + a.shape, a.dtype) for a in arrays]
        self.sems = [pltpu.SemaphoreType.DMA((n,)), pltpu.SemaphoreType.DMA((7 * n,)), pltpu.SemaphoreType.DMA((7 * n,))]

    def _copies(self, src, out, sems):
        local_sem, send_sem, recv_sem = sems
        x, y, c = _mesh_pos()
        me = 4 * x + 2 * y + c
        peers = [(x ^ dx, y ^ dy, c ^ dc) for dx in (0, 1) for dy in (0, 1) for dc in (0, 1)][1:]
        remote = lambda s, d, k, to: pltpu.make_async_remote_copy(
            src_ref=s, dst_ref=d, send_sem=send_sem.at[k], recv_sem=recv_sem.at[k], device_id=to, device_id_type=MESH)
        sends, landed, local = [], [], []
        for t in range(self.n):
            local.append(pltpu.make_async_copy(src[t], out[t].at[me], local_sem.at[t]))
            for p, (px, py, pc) in enumerate(peers):
                sends.append(remote(src[t], out[t].at[me], 7 * t + p, (px, py, pc)))
                here = out[t].at[4 * px + 2 * py + pc]
                landed.append(remote(here, here, 7 * t + p, (px, py, pc)))
        return sends, landed, local

    def start(self, src, out, sems):
        sends, _, local = self._copies(src, out, sems)
        for cp in sends + local:
            cp.start()

    def mid(self, src, out, sems):
        pass

    def finish(self, src, out, sems):
        sends, landed, local = self._copies(src, out, sems)
        for cp in landed:
            cp.wait_recv()
        for cp in sends:
            cp.wait_send()
        for cp in local:
            cp.wait()


MXU_COLS = 256


def _resident(shape):
    return pl.BlockSpec(shape, lambda *_: (0,) * len(shape), pipeline_mode=pl.Buffered(1))


def ffn_up(h, gain, w1, w3, name, hosted=(), mixed=None):
    T, D = h.shape
    F = w1.shape[0]
    tm = min(T, 512)

    def body(*refs):
        if mixed is None:
            h_ref, g_ref, w1_ref, w3_ref, n_ref, ga_ref, gb_ref, s_ref = refs
            hh = h_ref[...]
        else:
            pa_ref, rb_ref, wo_ref, h_ref, g_ref, w1_ref, w3_ref, hh_ref, n_ref, ga_ref, gb_ref, s_ref = refs
            hh = h_ref[...] + _dot(pa_ref[...], wo_ref[0]) + _dot(rb_ref[...], wo_ref[1])
            hh_ref[...] = hh
        n = (hh * _rstd(hh) * g_ref[...]).astype(BF16)
        n_ref[...] = n
        for c in range(0, F, MXU_COLS):
            cols = slice(c, c + MXU_COLS)
            a = _dot_nt(n, w1_ref[cols, :])
            b = _dot_nt(n, w3_ref[cols, :])
            silu, dsilu = _silu_parts(a)
            ga_ref[:, cols] = (b * dsilu).astype(BF16)
            gb_ref[:, cols] = silu.astype(BF16)
            s_ref[:, cols] = (silu * b).astype(BF16)

    act = jax.ShapeDtypeStruct((T, F), BF16)
    act_spec = pl.BlockSpec((tm, F), lambda i: (i, 0))
    row_spec = pl.BlockSpec((tm, D), lambda i: (i, 0))
    in_specs = [row_spec, pl.BlockSpec((1, D), lambda i: (0, 0)), _resident((F, D)), _resident((F, D))]
    out_specs, out_shape, args = [row_spec, act_spec, act_spec, act_spec], [jax.ShapeDtypeStruct((T, D), BF16), act, act, act], [h, gain, w1, w3]
    if mixed is not None:
        pa, rb, woutg = mixed
        W = pa.shape[1]
        in_specs = [pl.BlockSpec((tm, W), lambda i: (i, 0))] * 2 + [_resident((2, W, D))] + in_specs
        out_specs, out_shape = [row_spec] + out_specs, [jax.ShapeDtypeStruct((T, D), F32)] + out_shape
        args = [pa, rb, woutg.reshape(2, W, D)] + args
    return _call(body, hosted, name=name, grid=(T // tm,), in_specs=in_specs, out_specs=out_specs, out_shape=out_shape, args=args)


def ffn_bwd_act(dh, w2, ga, gb, name, hosted=()):
    T, D = dh.shape
    F = w2.shape[0]
    tm = min(T, 512)

    def body(dh_ref, w2_ref, ga_ref, gb_ref, da_ref, db_ref, df_ref):
        df = (0.5 * dh_ref[...]).astype(BF16)
        df_ref[...] = df
        for c in range(0, F, MXU_COLS):
            cols = slice(c, c + MXU_COLS)
            ds = _dot_nt(df, w2_ref[cols, :])
            da_ref[:, cols] = (ds * ga_ref[:, cols].astype(F32)).astype(BF16)
            db_ref[:, cols] = (ds * gb_ref[:, cols].astype(F32)).astype(BF16)

    act = jax.ShapeDtypeStruct((T, F), BF16)
    act_spec = pl.BlockSpec((tm, F), lambda i: (i, 0))
    row_spec = pl.BlockSpec((tm, D), lambda i: (i, 0))
    return _call(
        body, hosted, name=name, grid=(T // tm,),
        in_specs=[row_spec, _resident((F, D)), act_spec, act_spec],
        out_specs=[act_spec, act_spec, row_spec],
        out_shape=[act, act, jax.ShapeDtypeStruct((T, D), BF16)],
        args=[dh, w2, ga, gb])


def ffn_dw(xs, y, halves, name, hosted=()):
    T, F = xs[0].shape
    D = y.shape[1]
    nx, fh = len(xs), F // halves
    tk = min(T, 512)
    nk = T // tk

    def body(*refs):
        y_ref, x_refs, o_refs, accs = refs[0], refs[1:1 + nx], refs[1 + nx:1 + 2 * nx], refs[1 + 2 * nx:]
        k = pl.program_id(1)

        @pl.when(k == 0)
        def _():
            for acc in accs:
                acc[...] = jnp.zeros_like(acc)

        yy = y_ref[...]
        for x_ref, acc in zip(x_refs, accs):
            acc[...] += _dot_tn(x_ref[...], yy)

        @pl.when(k == nk - 1)
        def _():
            for o_ref, acc in zip(o_refs, accs):
                o_ref[...] = acc[...].astype(BF16)

    out = jax.ShapeDtypeStruct((F, D), BF16)
    return _call(
        body, hosted, name=name, grid=(halves, nk),
        in_specs=[pl.BlockSpec((tk, D), lambda j, k: (k, 0))] + [pl.BlockSpec((tk, fh), lambda j, k: (k, j))] * nx,
        out_specs=[pl.BlockSpec((fh, D), lambda j, k: (j, 0))] * nx,
        out_shape=[out] * nx,
        scratch_shapes=[pltpu.VMEM((fh, D), F32)] * nx,
        args=[y] + list(xs))


def ffn_bwd_in(da, db, w1, w3, h, gain, dh, name, hosted=()):
    T, F = da.shape
    D = h.shape[1]
    tm = min(T, 512)

    def body(da_ref, db_ref, w1_ref, w3_ref, h_ref, g_ref, dh_ref, o_ref, dg_ref):
        dn = _dot(da_ref[...], w1_ref[...]) + _dot(db_ref[...], w3_ref[...])
        dhn, dg = _rmsnorm_bwd(dn, h_ref[...], g_ref[...])
        o_ref[...] = dh_ref[...] + dhn

        @pl.when(pl.program_id(0) == 0)
        def _():
            dg_ref[...] = jnp.zeros_like(dg_ref)

        dg_ref[...] += jnp.sum(dg, axis=0, keepdims=True)

    act_spec = pl.BlockSpec((tm, F), lambda i: (i, 0))
    row_spec = pl.BlockSpec((tm, D), lambda i: (i, 0))
    vec_spec = pl.BlockSpec((1, D), lambda i: (0, 0))
    return _call(
        body, hosted, name=name, grid=(T // tm,),
        in_specs=[act_spec, act_spec, _resident((F, D)), _resident((F, D)), row_spec, vec_spec, row_spec],
        out_specs=[row_spec, vec_spec],
        out_shape=[jax.ShapeDtypeStruct((T, D), F32), jax.ShapeDtypeStruct((1, D), F32)],
        args=[da, db, w1, w3, h, gain, dh])


def ffn_down_mix_in(s, w2, h, gain, wing, name, hosted=()):
    T, F = s.shape
    D = h.shape[1]
    nsh, _, Cs = wing.shape
    tm = min(T, 512)

    def body(s_ref, w2_ref, h_ref, g_ref, w_ref, hh_ref, u_ref, p_ref):
        hh = h_ref[...] + 0.5 * _dot(s_ref[...], w2_ref[...])
        hh_ref[...] = hh
        u = (hh * _rstd(hh) * g_ref[...]).astype(BF16)
        u_ref[...] = u
        for j in range(nsh):
            p_ref[:, j * Cs:(j + 1) * Cs] = _dot(u, w_ref[j])

    row_spec = pl.BlockSpec((tm, D), lambda i: (i, 0))
    return _call(
        body, hosted, name=name, grid=(T // tm,),
        in_specs=[pl.BlockSpec((tm, F), lambda i: (i, 0)), _resident((F, D)), row_spec, pl.BlockSpec((1, D), lambda i: (0, 0)),
                  _resident((nsh, D, Cs))],
        out_specs=[row_spec, row_spec, pl.BlockSpec((tm, nsh * Cs), lambda i: (i, 0))],
        out_shape=[jax.ShapeDtypeStruct((T, D), F32), jax.ShapeDtypeStruct((T, D), BF16), jax.ShapeDtypeStruct((T, nsh * Cs), F32)],
        args=[s, w2, h, gain, wing])


def mix_out_bwd(dh, woutg, a, b, name, hosted=()):
    T, D = dh.shape
    W = a.shape[1]
    nsh, Rs, _ = woutg.shape
    wout = woutg.reshape(2, W, D)
    tk = min(T, 512)
    nk = T // tk

    def body(dh_ref, w_ref, a_ref, b_ref, da_ref, db_ref, dw_ref, acc):
        k = pl.program_id(0)

        @pl.when(k == 0)
        def _():
            acc[...] = jnp.zeros_like(acc)

        dhb = dh_ref[...].astype(BF16)
        da_ref[...] = _dot_nt(dhb, w_ref[0])
        db_ref[...] = _dot_nt(dhb, w_ref[1])
        acc[0:W, :] += _dot_tn(a_ref[...], dhb)
        acc[W:2 * W, :] += _dot_tn(b_ref[...], dhb)

        @pl.when(k == nk - 1)
        def _():
            for j in range(nsh):
                dw_ref[j] = acc[j * Rs:(j + 1) * Rs, :].astype(BF16)

    return _call(
        body, hosted, name=name, grid=(nk,),
        in_specs=[pl.BlockSpec((tk, D), lambda k: (k, 0)), pl.BlockSpec((2, W, D), lambda k: (0, 0, 0)),
                  pl.BlockSpec((tk, W), lambda k: (k, 0)), pl.BlockSpec((tk, W), lambda k: (k, 0))],
        out_specs=[pl.BlockSpec((tk, W), lambda k: (k, 0)), pl.BlockSpec((tk, W), lambda k: (k, 0)),
                   pl.BlockSpec((nsh, Rs, D), lambda k: (0, 0, 0))],
        out_shape=[jax.ShapeDtypeStruct((T, W), F32), jax.ShapeDtypeStruct((T, W), F32),
                   jax.ShapeDtypeStruct((nsh, Rs, D), BF16)],
        scratch_shapes=[pltpu.VMEM((2 * W, D), F32)],
        args=[dh, wout, a, b])


def _dproj_block(g):
    return (g // N_GROUPS + N_GROUPS) % (N_GROUPS + 1), g % N_GROUPS


def mix_dwin(u, dproj, nsh, name, hosted=()):
    T, D = u.shape
    Hd = HEAD_DIM
    slabs, _, width = dproj.shape
    blocks = slabs * width // Hd
    Cs = blocks * Hd // nsh
    tk = min(T, 512)
    nk = T // tk

    def body(u_ref, d_ref, o_ref, acc):
        k = pl.program_id(0)

        @pl.when(k == 0)
        def _():
            acc[...] = jnp.zeros_like(acc)

        where = [_dproj_block(g) for g in range(blocks)]
        d = jnp.concatenate([d_ref[slab, :, col * Hd:(col + 1) * Hd] for slab, col in where], axis=1)
        acc[...] += _dot_tn(u_ref[...], d)

        @pl.when(k == nk - 1)
        def _():
            for j in range(nsh):
                o_ref[j] = acc[:, j * Cs:(j + 1) * Cs].astype(BF16)

    return _call(
        body, hosted, name=name, grid=(nk,),
        in_specs=[pl.BlockSpec((tk, D), lambda k: (k, 0)), pl.BlockSpec((slabs, tk, width), lambda k: (0, k, 0))],
        out_specs=[pl.BlockSpec((nsh, D, Cs), lambda k: (0, 0, 0))],
        out_shape=[jax.ShapeDtypeStruct((nsh, D, Cs), BF16)],
        scratch_shapes=[pltpu.VMEM((D, blocks * Hd), F32)],
        args=[u, dproj])


def mix_in_bwd(dproj, wing, h, gain, dh, name, hosted=()):
    T, D = h.shape
    nsh, _, Cs = wing.shape
    Hd = HEAD_DIM
    per = Cs // Hd
    tm = min(T, 512)

    def body(d_ref, w_ref, h_ref, g_ref, dh_ref, o_ref, dg_ref):
        def shard(j):
            blocks = [_dproj_block(per * j + i) for i in range(per)]
            return jnp.concatenate([d_ref[slab, :, col * Hd:(col + 1) * Hd] for slab, col in blocks], axis=1)

        du = _dot_nt(shard(0), w_ref[0])
        for j in range(1, nsh):
            du += _dot_nt(shard(j), w_ref[j])
        dhn, dg = _rmsnorm_bwd(du, h_ref[...], g_ref[...])
        o_ref[...] = dh_ref[...] + dhn

        @pl.when(pl.program_id(0) == 0)
        def _():
            dg_ref[...] = jnp.zeros_like(dg_ref)

        dg_ref[...] += jnp.sum(dg, axis=0, keepdims=True)

    row_spec = pl.BlockSpec((tm, D), lambda i: (i, 0))
    vec_spec = pl.BlockSpec((1, D), lambda i: (0, 0))
    return _call(
        body, hosted, name=name, grid=(T // tm,),
        in_specs=[pl.BlockSpec((dproj.shape[0], tm, dproj.shape[2]), lambda i: (0, i, 0)),
                  pl.BlockSpec((nsh, D, Cs), lambda i: (0, 0, 0)), row_spec, vec_spec, row_spec],
        out_specs=[row_spec, vec_spec],
        out_shape=[jax.ShapeDtypeStruct((T, D), F32), jax.ShapeDtypeStruct((1, D), F32)],
        args=[dproj, wing, h, gain, dh])


def _pool_window(x, group, T, trailing):
    rows = lax.broadcasted_iota(jnp.int32, x.shape, 0)

    def shifted(z, k):
        if trailing:
            return jnp.where(rows >= k, pltpu.roll(z, k, 0), 0.0)
        return jnp.where(rows < T - k, pltpu.roll(z, T - k, 0), 0.0)

    s2 = x + shifted(x, 1)
    s4 = s2 + shifted(s2, 2)
    s8 = s4 + shifted(s4, 4)
    s16 = s8 + shifted(s8, 8)
    return jnp.where(group == 0, s2, jnp.where(group == 1, s4, jnp.where(group == 2, s8, s16)))


def _pool_count(group, shape):
    rows = lax.broadcasted_iota(jnp.int32, shape, 0)
    w = jnp.where(group == 0, 2, jnp.where(group == 1, 4, jnp.where(group == 2, 8, 16)))
    return jnp.minimum(rows + 1, w).astype(F32)


def pool_fwd(proj, pool_w, pool_scale, name, hosted=()):
    T = proj.shape[0]
    Hd = HEAD_DIM

    def body(x_ref, w_ref, sc_ref, a_ref):
        g = pl.program_id(0)
        x = x_ref[...]
        pooled = _pool_window(x, g, T, True) / _pool_count(g, x.shape) - x
        a_ref[...] = (_dot(pooled.astype(BF16), w_ref[0].astype(BF16)) * sc_ref[...]).astype(BF16)

    return _call(
        body, hosted, name=name, grid=(N_GROUPS,),
        in_specs=[pl.BlockSpec((T, Hd), lambda g: (0, g)), pl.BlockSpec((1, Hd, Hd), lambda g: (g, 0, 0)),
                  pl.BlockSpec((1, Hd), lambda g: (0, g))],
        out_specs=[pl.BlockSpec((T, Hd), lambda g: (0, g))],
        out_shape=[jax.ShapeDtypeStruct((T, N_GROUPS * Hd), BF16)],
        args=[proj, pool_w, pool_scale])


def pool_bwd(proj, da, pool_w, pool_scale, name, hosted=()):
    T = proj.shape[0]
    Hd = HEAD_DIM

    def body(x_ref, da_ref, w_ref, sc_ref, dx_ref, dw_ref, dsc_ref):
        g = pl.program_id(0)
        x = x_ref[...]
        cnt = _pool_count(g, x.shape)
        pooled = (_pool_window(x, g, T, True) / cnt - x).astype(BF16)
        wb = w_ref[0].astype(BF16)
        dav = da_ref[...]
        dsc_ref[...] = jnp.sum(dav * _dot(pooled, wb), axis=0, keepdims=True)
        dout = (dav * sc_ref[...]).astype(BF16)
        dw_ref[0] = _dot_tn(pooled, dout)
        dpooled = _dot_nt(dout, wb)
        dx_ref[0] = (_pool_window(dpooled / cnt, g, T, False) - dpooled).astype(BF16)

    col_spec = pl.BlockSpec((T, Hd), lambda g: (0, g))
    return _call(
        body, hosted, name=name, grid=(N_GROUPS,),
        in_specs=[col_spec, col_spec, pl.BlockSpec((1, Hd, Hd), lambda g: (g, 0, 0)), pl.BlockSpec((1, Hd), lambda g: (0, g))],
        out_specs=[pl.BlockSpec((1, T, Hd), lambda g: (N_GROUPS, 0, g)), pl.BlockSpec((1, Hd, Hd), lambda g: (g, 0, 0)),
                   pl.BlockSpec((1, Hd), lambda g: (0, g))],
        out_shape=[jax.ShapeDtypeStruct((N_GROUPS + 1, T, N_GROUPS * Hd), BF16), jax.ShapeDtypeStruct((N_GROUPS, Hd, Hd), F32),
                   jax.ShapeDtypeStruct((1, N_GROUPS * Hd), F32)],
        args=[proj, da, pool_w, pool_scale])


def _ret_tables(T):
    Hd, C = HEAD_DIM, RET_CHUNK
    inv_freq = 1.0 / (ROPE_BASE ** (jnp.arange(0, Hd, 2, dtype=F32) / Hd))
    ang = jnp.arange(T, dtype=F32)[:, None] * inv_freq[None, :]
    cos, sin = jnp.cos(ang), jnp.sin(ang)
    cos2 = jnp.concatenate([cos, cos], axis=-1)
    sin2 = jnp.concatenate([-sin, sin], axis=-1)
    log_gamma = jnp.log1p(-jnp.exp2(-5.0 - jnp.arange(N_GROUPS, dtype=F32)))
    pos = jnp.arange(C, dtype=F32)
    rel = pos[:, None] - pos[None, :]
    intra = jnp.where(rel[None] >= 0, jnp.exp(log_gamma[:, None, None] * jnp.maximum(rel, 0.0)[None]), 0.0)
    k_tail = jnp.exp(log_gamma[:, None] * (C - 1 - pos)[None, :])
    q_head = jnp.exp(log_gamma[:, None] * (pos + 1.0)[None, :])
    chunk_decay = jnp.exp(log_gamma * C)
    wide = lambda t: jnp.broadcast_to(t[:, :, None], (N_GROUPS, C, Hd))
    return cos2, sin2, intra, wide(k_tail), wide(q_head), jnp.broadcast_to(chunk_decay[:, None, None], (N_GROUPS, 1, Hd))


def _rope(x, cos2, sin2):
    return x * cos2 + pltpu.roll(x, HEAD_DIM // 2, 1) * sin2


def _rope_t(d, cos2, sin2):
    return d * cos2 + pltpu.roll(d * sin2, HEAD_DIM // 2, 1)


def _ret_specs(T, tseg, seg_of):
    Hd, G = HEAD_DIM, N_GROUPS
    col = lambda kind: pl.BlockSpec((tseg, Hd), lambda h, s: (seg_of(s), G * kind + h))
    tab = pl.BlockSpec((T, Hd), lambda h, s: (0, 0))
    head = pl.BlockSpec((1, RET_CHUNK, Hd), lambda h, s: (h, 0, 0))
    cd = pl.BlockSpec((1, 1, Hd), lambda h, s: (h, 0, 0))
    gain = pl.BlockSpec((1, Hd), lambda h, s: (0, h))
    return col, tab, head, cd, gain


def ret_fwd(proj, ret_norm, tables, name, hosted=()):
    T = proj.shape[0]
    Hd, C, G = HEAD_DIM, RET_CHUNK, N_GROUPS
    tseg = min(T, 2048)
    nseg, nck = T // tseg, tseg // C
    scale = Hd ** -0.5
    cos2, sin2, intra, k_tail, q_head, chunk_decay = tables

    def body(q_ref, k_ref, v_ref, g_ref, gain_ref, cos_ref, sin_ref, m_ref, kt_ref, qh_ref, cd_ref,
             b_ref, o_ref, rp_ref, state):
        @pl.when(pl.program_id(1) == 0)
        def _():
            state[...] = jnp.zeros_like(state)

        def chunk(ci, carry):
            rows = pl.ds(pl.multiple_of(ci * C, C), C)
            at = pl.ds(pl.multiple_of(pl.program_id(1) * tseg + ci * C, C), C)
            cos, sin = cos_ref[at, :], sin_ref[at, :]
            qr = _rope(q_ref[rows, :], cos, sin)
            kr = _rope(k_ref[rows, :], cos, sin) * scale
            qb, kb, vb = qr.astype(BF16), kr.astype(BF16), v_ref[rows, :].astype(BF16)
            r = state[...]
            rp_ref[0, ci] = r.astype(BF16)
            sc = _dot_nt(qb, kb) * m_ref[0]
            o = _dot(sc.astype(BF16), vb) + _dot((qr * qh_ref[0]).astype(BF16), r.astype(BF16))
            state[...] = cd_ref[0] * r + _dot_tn((kr * kt_ref[0]).astype(BF16), vb)
            o_ref[rows, :] = o
            on = o * _rstd(o)
            b_ref[rows, :] = (jax.nn.silu(g_ref[rows, :]) * (on * gain_ref[...])).astype(BF16)
            return carry

        lax.fori_loop(0, nck, chunk, 0, unroll=True)

    col, tab, head, cd, gain = _ret_specs(T, tseg, lambda s: s)
    out_col = pl.BlockSpec((tseg, Hd), lambda h, s: (s, h))
    return _call(
        body, hosted, name=name, grid=(G, nseg),
        in_specs=[col(1), col(2), col(3), col(4), gain, tab, tab, head, head, head, cd],
        out_specs=[out_col, out_col, pl.BlockSpec((1, nck, Hd, Hd), lambda h, s: (h, s, 0, 0))],
        out_shape=[jax.ShapeDtypeStruct((T, G * Hd), BF16), jax.ShapeDtypeStruct((T, G * Hd), F32),
                   jax.ShapeDtypeStruct((G, T // C, Hd, Hd), BF16)],
        scratch_shapes=[pltpu.VMEM((Hd, Hd), F32)],
        args=[proj, proj, proj, proj, ret_norm, cos2, sin2, intra, k_tail, q_head, chunk_decay])


def ret_bwd(proj, db, o_pre, r_prev, ret_norm, tables, dproj, name, hosted=()):
    T = proj.shape[0]
    Hd, C, G = HEAD_DIM, RET_CHUNK, N_GROUPS
    tseg = min(T, 2048)
    nseg, nck = T // tseg, tseg // C
    scale = Hd ** -0.5
    cos2, sin2, intra, k_tail, q_head, chunk_decay = tables

    def body(q_ref, k_ref, v_ref, g_ref, db_ref, o_ref, rp_ref, gain_ref, cos_ref, sin_ref, m_ref, kt_ref, qh_ref, cd_ref,
             _, d_ref, dgain_ref, gstate):
        @pl.when(pl.program_id(1) == 0)
        def _():
            gstate[...] = jnp.zeros_like(gstate)
            dgain_ref[...] = jnp.zeros_like(dgain_ref)

        def chunk(t, carry):
            ci = nck - 1 - t
            rows = pl.ds(pl.multiple_of(ci * C, C), C)
            at = pl.ds(pl.multiple_of((nseg - 1 - pl.program_id(1)) * tseg + ci * C, C), C)
            cos, sin = cos_ref[at, :], sin_ref[at, :]
            qr = _rope(q_ref[rows, :], cos, sin)
            kr = _rope(k_ref[rows, :], cos, sin) * scale
            qb, kb, vb = qr.astype(BF16), kr.astype(BF16), v_ref[rows, :].astype(BF16)
            qhb, ktb = (qr * qh_ref[0]).astype(BF16), (kr * kt_ref[0]).astype(BF16)
            sc = (_dot_nt(qb, kb) * m_ref[0]).astype(BF16)
            o = o_ref[rows, :]
            rstd = _rstd(o)
            on = o * rstd
            gain = gain_ref[...]
            silu, dsilu = _silu_parts(g_ref[rows, :])
            dy = db_ref[rows, :]
            dgain_ref[...] += jnp.sum(dy * silu * on, axis=0, keepdims=True)
            dg = dy * on * gain * dsilu
            don = dy * silu * gain
            dob = (rstd * (don - on * jnp.mean(don * on, axis=-1, keepdims=True))).astype(BF16)
            gn = gstate[...]
            gb = gn.astype(BF16)
            da = (_dot_nt(dob, vb) * m_ref[0]).astype(BF16)
            dq = _dot(da, kb) + _dot_nt(dob, rp_ref[0, ci]) * qh_ref[0]
            dk = _dot_tn(da, qb) + _dot_nt(vb, gb) * kt_ref[0]
            dv = _dot_tn(sc, dob) + _dot(ktb, gb)
            gstate[...] = cd_ref[0] * gn + _dot_tn(qhb, dob)
            d_ref[0, rows, :] = _rope_t(dq, cos, sin).astype(BF16)
            d_ref[1, rows, :] = _rope_t(dk * scale, cos, sin).astype(BF16)
            d_ref[2, rows, :] = dv.astype(BF16)
            d_ref[3, rows, :] = dg.astype(BF16)
            return carry

        lax.fori_loop(0, nck, chunk, 0, unroll=True)

    rev = lambda s: nseg - 1 - s
    col, tab, head, cd, gain = _ret_specs(T, tseg, rev)
    act = pl.BlockSpec((tseg, Hd), lambda h, s: (rev(s), h))
    return _call(
        body, hosted, name=name, grid=(G, nseg),
        in_specs=[col(1), col(2), col(3), col(4), act, act, pl.BlockSpec((1, nck, Hd, Hd), lambda h, s: (h, rev(s), 0, 0)),
                  gain, tab, tab, head, head, head, cd, ANY],
        out_specs=[pl.BlockSpec((4, tseg, Hd), lambda h, s: (0, rev(s), h)), gain],
        out_shape=[jax.ShapeDtypeStruct(dproj.shape, BF16), jax.ShapeDtypeStruct((1, G * Hd), F32)],
        scratch_shapes=[pltpu.VMEM((Hd, Hd), F32)], aliased={14: 0},
        args=[proj, proj, proj, proj, db, o_pre, r_prev, ret_norm, cos2, sin2, intra, k_tail, q_head, chunk_decay, dproj])


def ffn_down_loss(s, w2, h, gain, target, name, hosted=()):
    T, F = s.shape
    D = h.shape[1]
    tm = min(T, 512)

    def body(s_ref, w2_ref, h_ref, g_ref, t_ref, dh_ref, loss_ref, dg_ref):
        @pl.when(pl.program_id(0) == 0)
        def _():
            loss_ref[...] = jnp.zeros_like(loss_ref)
            dg_ref[...] = jnp.zeros_like(dg_ref)

        hh = h_ref[...] + 0.5 * _dot(s_ref[...], w2_ref[...])
        gain_v = g_ref[...]
        err = hh * _rstd(hh) * gain_v - t_ref[...]
        loss_ref[...] += 0.5 * jnp.sum(jnp.mean(err * err, axis=-1, keepdims=True), axis=0, keepdims=True)
        dhn, dg = _rmsnorm_bwd(err * (1.0 / D), hh, gain_v)
        dh_ref[...] = dhn
        dg_ref[...] += jnp.sum(dg, axis=0, keepdims=True)

    row_spec = pl.BlockSpec((tm, D), lambda i: (i, 0))
    vec_spec = pl.BlockSpec((1, D), lambda i: (0, 0))
    return _call(
        body, hosted, name=name, grid=(T // tm,),
        in_specs=[pl.BlockSpec((tm, F), lambda i: (i, 0)), _resident((F, D)), row_spec, vec_spec, row_spec],
        out_specs=[row_spec, pl.BlockSpec((1, LANES), lambda i: (0, 0)), vec_spec],
        out_shape=[jax.ShapeDtypeStruct((T, D), F32), jax.ShapeDtypeStruct((1, LANES), F32), jax.ShapeDtypeStruct((1, D), F32)],
        args=[s, w2, h, gain, target])


def prereduce(grads, recvs, place, name):
    nt = len(grads)
    nsh, R, C = grads[0].shape
    rh = R // 2

    def body(place_ref, *refs):
        for t in range(nt):
            g_ref, r_ref, o_ref, own_ref = refs[2 * t], refs[2 * t + 1], refs[2 * nt + 2 * t], refs[2 * nt + 2 * t + 1]
            piece = (g_ref[...].astype(F32) + r_ref[...].astype(F32)).astype(BF16)
            o_ref[...] = piece

            @pl.when(pl.program_id(0) == place_ref[1])
            def _():
                own_ref[...] = piece

    outs = pl.pallas_call(
        body, name=name,
        grid_spec=pltpu.PrefetchScalarGridSpec(
            num_scalar_prefetch=1, grid=(nsh,),
            in_specs=[pl.BlockSpec((1, rh, C), lambda j, p: (j, p[0], 0)), pl.BlockSpec((1, rh, C), lambda j, p: (j, 0, 0))] * nt,
            out_specs=[pl.BlockSpec((1, rh, C), lambda j, p: (j, 0, 0)),
                       pl.BlockSpec((1, rh, C), lambda j, p: (p[1], p[0], 0))] * nt),
        out_shape=[jax.ShapeDtypeStruct((nsh, rh, C), BF16), jax.ShapeDtypeStruct((nsh, R, C), BF16)] * nt,
        compiler_params=pltpu.CompilerParams(vmem_limit_bytes=VMEM_LIMIT_V7X),
    )(place, *[a for pair in zip(grads, recvs) for a in pair])
    return [(outs[2 * t], outs[2 * t + 1]) for t in range(nt)]


def _adamw(w, g, m, v):
    m = ADAM_B1 * m + (1.0 - ADAM_B1) * g
    v = ADAM_B2 * v + (1.0 - ADAM_B2) * (g * g)
    m_hat = m / (1.0 - ADAM_B1 ** ADAM_STEP)
    v_hat = v / (1.0 - ADAM_B2 ** ADAM_STEP)
    return -ADAM_LR * (m_hat / (jnp.sqrt(v_hat) + ADAM_EPS) + ADAM_WD * w), m, v


def adamw_sharded(tensors, name, hosted=()):
    nt = len(tensors)
    nsh = tensors[0][0].shape[0]
    shapes = [t[0].shape[1:] for t in tensors]

    def fits(steps):
        if any(R % (steps * BF16_TILE_ROWS) for R, _ in shapes):
            return False
        return sum(2 * (R // steps) * -(-C // LANES) * LANES * (nsh * 2 + 7 * 4) for R, C in shapes) <= ADAMW_VMEM_BUDGET

    steps = min(s for s in range(1, min(R for R, _ in shapes) // BF16_TILE_ROWS + 1) if fits(s))

    def body(*refs):
        ins, outs = refs[:4 * nt], refs[4 * nt:]
        for t in range(nt):
            p_ref, w_ref, m_ref, v_ref = ins[4 * t:4 * t + 4]
            g_ref, d_ref, nm_ref, nv_ref = outs[4 * t:4 * t + 4]
            g = p_ref[0].astype(F32)
            for i in range(1, nsh):
                g += p_ref[i].astype(F32)
            g_ref[...] = g
            d_ref[...], nm_ref[...], nv_ref[...] = _adamw(w_ref[...], g, m_ref[...], v_ref[...])

    in_specs, out_specs, out_shape = [], [], []
    for R, C in shapes:
        spec = pl.BlockSpec((R // steps, C), lambda i: (i, 0))
        in_specs += [pl.BlockSpec((nsh, R // steps, C), lambda i: (0, i, 0)), spec, spec, spec]
        out_specs += [spec] * 4
        out_shape += [jax.ShapeDtypeStruct((R, C), F32)] * 4
    return _call(body, hosted, name=name, grid=(steps,), in_specs=in_specs, out_specs=out_specs, out_shape=out_shape,
                 args=[a for tensor in tensors for a in tensor])


def adamw_small(packs, params, loss_packs, name):
    n = len(packs)
    ndev = loss_packs.shape[0]

    def body(*refs):
        p_refs, loss_ref, wmv = refs[:n], refs[n], refs[n + 1:4 * n + 1]
        outs, loss_out = refs[4 * n + 1:8 * n + 1], refs[8 * n + 1]
        total = lambda r: sum((r[i] for i in range(1, ndev)), r[0])
        loss_out[...] = total(loss_ref)
        for k in range(n):
            g = total(p_refs[k])
            outs[4 * k][...] = g
            outs[4 * k + 1][...], outs[4 * k + 2][...], outs[4 * k + 3][...] = _adamw(
                wmv[3 * k][...], g, wmv[3 * k + 1][...], wmv[3 * k + 2][...])

    out_shape = [jax.ShapeDtypeStruct(p[0].shape, F32) for p in params for _ in range(4)]
    outs = pl.pallas_call(body, name=name, out_shape=out_shape + [jax.ShapeDtypeStruct(loss_packs.shape[1:], F32)],
                          compiler_params=pltpu.CompilerParams(vmem_limit_bytes=VMEM_LIMIT_V7X),
                          )(*packs, loss_packs, *[a for p in params for a in p])
    return [outs[4 * k:4 * k + 4] for k in range(n)], outs[4 * n]


BIG = ("ffn1_w1", "ffn1_w3", "ffn1_w2", "w_in", "w_out", "ffn2_w1", "ffn2_w3", "ffn2_w2")
TRANSPOSED = ("ffn1_w1", "ffn1_w3", "ffn2_w1", "ffn2_w3")
SMALL = ("pool_w", "mix_norm", "pool_scale", "ret_norm", "ffn2_norm", "final_norm", "ffn1_norm")
WEIGHTS = ("ffn1_norm", "ffn1_w1", "ffn1_w3", "ffn1_w2", "mix_norm", "w_in", "pool_w", "pool_scale", "ret_norm", "w_out",
           "ffn2_norm", "ffn2_w1", "ffn2_w3", "ffn2_w2", "final_norm")


def kernel(x, ffn1_norm, ffn1_w1, ffn1_w3, ffn1_w2, mix_norm, w_in, pool_w, pool_scale, ret_norm, w_out, ffn2_norm, ffn2_w1, ffn2_w3, ffn2_w2, final_norm, loss_target, m_ffn1_norm, m_ffn1_w1, m_ffn1_w3, m_ffn1_w2, m_mix_norm, m_w_in, m_pool_w, m_pool_scale, m_ret_norm, m_w_out, m_ffn2_norm, m_ffn2_w1, m_ffn2_w3, m_ffn2_w2, m_final_norm, v_ffn1_norm, v_ffn1_w1, v_ffn1_w3, v_ffn1_w2, v_mix_norm, v_w_in, v_pool_w, v_pool_scale, v_ret_norm, v_w_out, v_ffn2_norm, v_ffn2_w1, v_ffn2_w3, v_ffn2_w2, v_final_norm):
    w = dict(ffn1_norm=ffn1_norm, ffn1_w1=ffn1_w1, ffn1_w3=ffn1_w3, ffn1_w2=ffn1_w2, mix_norm=mix_norm, w_in=w_in, pool_w=pool_w,
             pool_scale=pool_scale, ret_norm=ret_norm, w_out=w_out, ffn2_norm=ffn2_norm, ffn2_w1=ffn2_w1, ffn2_w3=ffn2_w3,
             ffn2_w2=ffn2_w2, final_norm=final_norm)
    m = dict(ffn1_norm=m_ffn1_norm, ffn1_w1=m_ffn1_w1, ffn1_w3=m_ffn1_w3, ffn1_w2=m_ffn1_w2, mix_norm=m_mix_norm, w_in=m_w_in,
             pool_w=m_pool_w, pool_scale=m_pool_scale, ret_norm=m_ret_norm, w_out=m_w_out, ffn2_norm=m_ffn2_norm, ffn2_w1=m_ffn2_w1,
             ffn2_w3=m_ffn2_w3, ffn2_w2=m_ffn2_w2, final_norm=m_final_norm)
    v = dict(ffn1_norm=v_ffn1_norm, ffn1_w1=v_ffn1_w1, ffn1_w3=v_ffn1_w3, ffn1_w2=v_ffn1_w2, mix_norm=v_mix_norm, w_in=v_w_in,
             pool_w=v_pool_w, pool_scale=v_pool_scale, ret_norm=v_ret_norm, w_out=v_w_out, ffn2_norm=v_ffn2_norm, ffn2_w1=v_ffn2_w1,
             ffn2_w3=v_ffn2_w3, ffn2_w2=v_ffn2_w2, final_norm=v_final_norm)
    xs, target = x[0], loss_target[0]
    T = xs.shape[0]
    tables = _ret_tables(T)
    place = jnp.stack([lax.axis_index("c"), 2 * lax.axis_index("x") + lax.axis_index("y")]).astype(jnp.int32)
    local = lambda d, k: jnp.transpose(d[k][0]) if k in TRANSPOSED else d[k][0]
    result = lambda o, k: jnp.transpose(o)[None] if k in TRANSPOSED else o[None]
    first = ("ffn1_w1", "ffn1_w3")
    sh = {k: local(w, k).astype(BF16) for k in first}
    gather = lambda *names: [ChipExchange([sh[k] for k in names], False)]
    wg, grad, delta, new_m, new_v = {}, {}, {}, {}, {}

    def update(names, pieces, name, hosted=()):
        outs, extras = adamw_sharded([(p, local(w, k), local(m, k), local(v, k)) for k, p in zip(names, pieces)], name, hosted)
        for t, k in enumerate(names):
            grad[k], delta[k], new_m[k], new_v[k] = [result(o, k) for o in outs[4 * t:4 * t + 4]]
        return extras

    def reduce_in_chip(name, *pairs):
        reduced = prereduce([p for p, _ in pairs], [r for _, r in pairs], place, "prereduce_" + name)
        return reduced[0] if len(pairs) == 1 else reduced

    scatter = lambda *reduced: ChipExchange([r[0] for r in reduced], True, [r[1] for r in reduced])
    whole = lambda k: wg[k].reshape(-1, wg[k].shape[-1])
    sharded = lambda g: g.reshape(N_CHIPS, -1, g.shape[-1])

    later = [k for k in BIG if k not in first]
    casts, ((wg["ffn1_w1"], wg["ffn1_w3"]),) = cast_shards([local(w, k) for k in later], "cast_gather_ffn1", gather(*first))
    sh.update(zip(later, casts))
    (n1, ga1, gb1, s1), ((wg["ffn1_w2"], wg["w_in"]),) = ffn_up(
        xs, ffn1_norm, whole("ffn1_w1"), whole("ffn1_w3"), "ffn1_up", gather("ffn1_w2", "w_in"))
    (h1, u, proj), ((wg["w_out"], wg["ffn2_w1"]),) = ffn_down_mix_in(
        s1, whole("ffn1_w2"), xs, mix_norm, wg["w_in"], "ffn1_down_mix_in", gather("w_out", "ffn2_w1"))
    (pa,), _ = pool_fwd(proj, pool_w[0], pool_scale, "pool_fwd")
    (rb, o_pre, r_prev), ((wg["ffn2_w3"],),) = ret_fwd(proj, ret_norm, tables, "ret_fwd", gather("ffn2_w3"))
    (h2, n2, ga2, gb2, s2), ((wg["ffn2_w2"],),) = ffn_up(
        h1, ffn2_norm, whole("ffn2_w1"), whole("ffn2_w3"), "mix_out_ffn2_up", gather("ffn2_w2"), mixed=(pa, rb, wg["w_out"]))
    (dh3, loss, d_final), _ = ffn_down_loss(s2, whole("ffn2_w2"), h2, final_norm[None], target, "ffn2_down_loss")

    (da2, db2, df2), _ = ffn_bwd_act(dh3, whole("ffn2_w2"), ga2, gb2, "ffn2_bwd_act")
    (g_f2w2,), _ = ffn_dw([s2], df2, 1, "ffn2_dw2")
    g_f2w2 = sharded(g_f2w2)
    (g_f2w1, g_f2w3), ((r_f2w2,),) = ffn_dw([da2, db2], n2, 2, "ffn2_dw13", [SiblingExchange([g_f2w2])])
    g_f2w1, g_f2w3 = sharded(g_f2w1), sharded(g_f2w3)
    p_f2w2 = reduce_in_chip("ffn2_w2", (g_f2w2, r_f2w2))
    (dh2, d_ffn2), ((q_f2w2,), (r_f2w1, r_f2w3)) = ffn_bwd_in(
        da2, db2, whole("ffn2_w1"), whole("ffn2_w3"), h2, ffn2_norm, dh3, "ffn2_bwd_in",
        [scatter(p_f2w2), SiblingExchange([g_f2w1, g_f2w3])])
    p_f2w1, p_f2w3 = reduce_in_chip("ffn2_w13", (g_f2w1, r_f2w1), (g_f2w3, r_f2w3))
    (dpa, drb, g_wout), _ = mix_out_bwd(dh2, wg["w_out"], pa, rb, "mix_out_bwd")
    (dproj, d_pool_w, d_pool_scale), _ = pool_bwd(proj, dpa, pool_w[0], pool_scale, "pool_bwd")
    (dproj, d_ret_norm), ((q_f2w1, q_f2w3), (r_wout,)) = ret_bwd(
        proj, drb, o_pre, r_prev, ret_norm, tables, dproj, "ret_bwd", [scatter(p_f2w1, p_f2w3), SiblingExchange([g_wout])])
    p_wout = reduce_in_chip("w_out", (g_wout, r_wout))
    (g_win,), ((q_wout,),) = mix_dwin(u, dproj, N_CHIPS, "mix_dwin", [scatter(p_wout)])
    (dh1, d_mix), ((r_win,),) = mix_in_bwd(dproj, wg["w_in"], h1, mix_norm, dh2, "mix_in_bwd", [SiblingExchange([g_win])])
    p_win = reduce_in_chip("w_in", (g_win, r_win))
    (da1, db1, df1), ((q_win,),) = ffn_bwd_act(dh1, whole("ffn1_w2"), ga1, gb1, "ffn1_bwd_act", [scatter(p_win)])
    d_small = {"pool_w": d_pool_w.reshape(-1, LANES), "mix_norm": d_mix, "pool_scale": d_pool_scale, "ret_norm": d_ret_norm,
               "ffn2_norm": d_ffn2, "final_norm": d_final}
    (g_f1w1, g_f1w3), (packs,) = ffn_dw([da1, db1], n1, 2, "ffn1_dw13", [AllExchange([d_small[k] for k in SMALL[:-1]] + [loss])])
    g_f1w1, g_f1w3 = sharded(g_f1w1), sharded(g_f1w3)
    (g_f1w2,), ((r_f1w1, r_f1w3),) = ffn_dw([s1], df1, 1, "ffn1_dw2", [SiblingExchange([g_f1w1, g_f1w3])])
    g_f1w2 = sharded(g_f1w2)
    p_f1w1, p_f1w3 = reduce_in_chip("ffn1_w13", (g_f1w1, r_f1w1), (g_f1w3, r_f1w3))
    (dx, d_ffn1), ((q_f1w1, q_f1w3), (r_f1w2,)) = ffn_bwd_in(
        da1, db1, whole("ffn1_w1"), whole("ffn1_w3"), xs, ffn1_norm, dh1, "ffn1_bwd_in",
        [scatter(p_f1w1, p_f1w3), SiblingExchange([g_f1w2])])
    p_f1w2 = reduce_in_chip("ffn1_w2", (g_f1w2, r_f1w2))

    (q_f1w2,), (late,) = update(["w_in", "w_out", "ffn2_w2"], [q_win, q_wout, q_f2w2], "adamw_mix_w2",
                                [scatter(p_f1w2), AllExchange([d_ffn1])])
    update(["ffn2_w1", "ffn2_w3", "ffn1_w1", "ffn1_w3"], [q_f2w1, q_f2w3, q_f1w1, q_f1w3], "adamw_w13")
    update(["ffn1_w2"], [q_f1w2], "adamw_ffn1_w2")
    flat = lambda t, k: t[k].reshape(-1, LANES) if k == "pool_w" else t[k].reshape(1, -1)
    updated, loss_sum = adamw_small(packs[:-1] + [late], [[flat(t, k) for t in (w, m, v)] for k in SMALL], packs[-1], "adamw_small")
    for k, outs in zip(SMALL, updated):
        grad[k], delta[k], new_m[k], new_v[k] = [o.reshape(w[k].shape) for o in outs]
    loss = loss_sum[0, 0]

    return (loss, dx[None], *[grad[k] for k in WEIGHTS], *[delta[k] for k in WEIGHTS],
            *[new_m[k] for k in WEIGHTS], *[new_v[k] for k in WEIGHTS])
```

```python
import math

import jax
import jax.numpy as jnp
from jax import lax
from jax.experimental import pallas as pl
from jax.experimental.pallas import tpu as pltpu

F32 = jnp.float32
BF16 = jnp.bfloat16

EPS = 1e-6
LANES = 128
BF16_TILE_ROWS = 16
N_CHIPS = 4
N_GROUPS = 4
HEAD_DIM = 128
RET_CHUNK = 128
ROPE_BASE = 10000.0
ADAM_LR, ADAM_B1, ADAM_B2, ADAM_EPS, ADAM_WD, ADAM_STEP = 0.001, 0.9, 0.999, 1e-08, 0.01, 10
VMEM_LIMIT_V7X = 56 * 1024 * 1024
ADAMW_VMEM_BUDGET = 32 * 1024 * 1024
MESH = pl.DeviceIdType.MESH
ANY = pl.BlockSpec(memory_space=pl.ANY)


def _dot(a, b):
    return jnp.dot(a, b, preferred_element_type=F32)


def _dot_nt(a, b):
    return lax.dot_general(a, b, (((1,), (1,)), ((), ())), preferred_element_type=F32)


def _dot_tn(a, b):
    return lax.dot_general(a, b, (((0,), (0,)), ((), ())), preferred_element_type=F32)


def _rstd(h):
    return lax.rsqrt(jnp.mean(h * h, axis=-1, keepdims=True) + EPS)


def _rmsnorm_bwd(dn, h, gain):
    r = _rstd(h)
    nh = h * r
    dnh = dn * gain
    dh = r * (dnh - nh * jnp.mean(dnh * nh, axis=-1, keepdims=True))
    return dh, dn * nh


def _silu_parts(a):
    sig = jax.nn.sigmoid(a)
    silu = a * sig
    return silu, sig + silu * (1.0 - sig)


def _mesh_pos():
    return lax.axis_index("x"), lax.axis_index("y"), lax.axis_index("c")


class ChipExchange:
    def __init__(self, srcs, scatter, placed=()):
        n = len(srcs)
        self.inputs, self.scatter, self.n, self.reach = list(srcs) + list(placed), scatter, n, REACH_CHIPS
        self.aliases = {n + t: t for t in range(n)} if scatter else {}
        self.half_rows = [s.shape[1] if scatter else s.shape[0] // 2 for s in srcs]
        self.out_shape = [jax.ShapeDtypeStruct((N_CHIPS, 2 * rh, s.shape[-1]), s.dtype) for s, rh in zip(srcs, self.half_rows)]
        if scatter:
            self.out_shape += [jax.ShapeDtypeStruct((2, rh // 2, s.shape[-1]), s.dtype) for s, rh in zip(srcs, self.half_rows)]
        dma = pltpu.SemaphoreType.DMA
        self.sems = [dma((4 * n,)), dma((4 * n,)), dma((2 * n,)), dma((2 * n,)), dma((4 * n,)), dma((4 * n,))]

    def _copies(self, src, out, sems):
        hop1_send, hop1_recv, hop2_send, hop2_recv, d2d_send, d2d_recv = sems
        x, y, c = _mesh_pos()
        me, dg = 2 * x + y, 2 * (1 - x) + (1 - y)
        sibling = (x, y, 1 - c)
        n = self.n
        mine, theirs = c, 1 - c

        def nb(a):
            nx, ny = x ^ (1 - a), y ^ a
            return 2 * nx + ny, (nx, ny, c)

        def remote(s, d, send, recv, k, to):
            return pltpu.make_async_remote_copy(src_ref=s, dst_ref=d, send_sem=send.at[k], recv_sem=recv.at[k],
                                                device_id=to, device_id_type=MESH)

        class Copies:
            def slot(_, t, chip, half):
                rh = self.half_rows[t]
                return out[t].at[chip, pl.ds(half * rh, rh), :]

            def quarter(_, t, chip, q):
                qh = self.half_rows[t] // 2
                return out[t].at[chip, pl.ds(mine * 2 * qh + q * qh, qh), :]

            def own_shard(k, t):
                return remote(src[t], out[t].at[me], d2d_send, d2d_recv, 4 * t + 3, sibling)

            def hop1(k, t, a, transit=False):
                rh = self.half_rows[t]
                chip, to = nb(a)
                if transit:
                    piece = src[t].at[dg, pl.ds(a * (rh // 2), rh // 2), :]
                    return remote(piece, out[n + t].at[a], hop1_send, hop1_recv, 4 * t + 2 + a, to)
                piece = src[t].at[chip] if self.scatter else src[t].at[pl.ds(mine * rh, rh), :]
                return remote(piece, k.slot(t, me, mine), hop1_send, hop1_recv, 4 * t + a, to)

            def landed1(k, t, a, transit=False):
                here = out[n + t].at[a] if transit else k.slot(t, nb(a)[0], mine)
                return remote(here, here, hop1_send, hop1_recv, 4 * t + (2 if transit else 0) + a, sibling)

            def hop2(k, t, q):
                origin, to = nb(q)[0], nb(1 - q)[1]
                piece = out[n + t].at[q] if self.scatter else k.quarter(t, origin, q)
                return remote(piece, k.quarter(t, origin, q), hop2_send, hop2_recv, 2 * t + q, to)

            def landed2(k, t, q):
                here = k.quarter(t, dg, q)
                return remote(here, here, hop2_send, hop2_recv, 2 * t + q, sibling)

            def d2d(k, t, p, chip, own=False, arriving=False):
                if arriving:
                    there = k.slot(t, chip, theirs)
                    return remote(there, there, d2d_send, d2d_recv, 4 * t + p, sibling)
                piece = src[t].at[me] if own else k.slot(t, chip, mine)
                return remote(piece, k.slot(t, chip, mine), d2d_send, d2d_recv, 4 * t + p, sibling)

        return Copies(), nb, me, dg, c

    def start(self, src, out, sems):
        k, nb, me, dg, c = self._copies(src, out, sems)
        for t in range(self.n):
            for first in range(2):
                a = first ^ c
                k.hop1(t, a).start()
                if self.scatter:
                    k.hop1(t, a, transit=True).start()
            if self.scatter:
                k.d2d(t, 3, me, own=True).start()
            else:
                k.own_shard(t).start()

    def mid(self, src, out, sems):
        k, nb, me, dg, c = self._copies(src, out, sems)
        for t in range(self.n):
            for first in range(2):
                a = first ^ c
                if self.scatter:
                    k.landed1(t, a, transit=True).wait_recv()
                    k.hop2(t, a).start()
                k.landed1(t, a).wait_recv()
                if not self.scatter:
                    k.hop2(t, a).start()
                k.d2d(t, a, nb(a)[0]).start()

    def finish(self, src, out, sems):
        k, nb, me, dg, c = self._copies(src, out, sems)
        for t in range(self.n):
            for q in range(2):
                k.landed2(t, q).wait_recv()
            k.d2d(t, 2, dg).start()
        for t in range(self.n):
            for a in range(2):
                k.d2d(t, a, nb(a)[0], arriving=True).wait_recv()
            k.d2d(t, 2, dg, arriving=True).wait_recv()
            if self.scatter:
                k.d2d(t, 3, me, arriving=True).wait_recv()
        for t in range(self.n):
            for a in range(2):
                k.hop1(t, a).wait_send()
                if self.scatter:
                    k.hop1(t, a, transit=True).wait_send()
                k.hop2(t, a).wait_send()
                k.d2d(t, a, nb(a)[0]).wait_send()
            k.d2d(t, 2, dg).wait_send()
            if self.scatter:
                k.d2d(t, 3, me, own=True).wait_send()
            else:
                k.own_shard(t).wait()


class SiblingExchange:
    def __init__(self, grads):
        self.inputs, self.n, self.aliases, self.reach = list(grads), len(grads), {}, REACH_SIBLING
        self.half_rows = [g.shape[1] // 2 for g in grads]
        self.out_shape = [jax.ShapeDtypeStruct((g.shape[0], rh, g.shape[2]), g.dtype) for g, rh in zip(grads, self.half_rows)]
        self.sems = [pltpu.SemaphoreType.DMA((self.n,)), pltpu.SemaphoreType.DMA((self.n,))]

    def _plan(self, src, out, sems):
        x, y, c = _mesh_pos()
        return [pltpu.make_async_remote_copy(
            src_ref=src[t].at[:, pl.ds((1 - c) * self.half_rows[t], self.half_rows[t]), :], dst_ref=out[t],
            send_sem=sems[0].at[t], recv_sem=sems[1].at[t], device_id=(x, y, 1 - c), device_id_type=MESH) for t in range(self.n)]

    def start(self, src, out, sems):
        for cp in self._plan(src, out, sems):
            cp.start()

    def mid(self, src, out, sems):
        pass

    def finish(self, src, out, sems):
        for cp in self._plan(src, out, sems):
            cp.wait()


REACH_SIBLING, REACH_CHIPS, REACH_ALL = 0, 1, 2


def _entry_barrier(reach):
    x, y, c = _mesh_pos()
    peers = [(x, y, 1 - c)]
    if reach == REACH_CHIPS:
        peers += [(1 - x, y, c), (x, 1 - y, c)]
    elif reach == REACH_ALL:
        peers = [(x ^ dx, y ^ dy, c ^ dc) for dx in (0, 1) for dy in (0, 1) for dc in (0, 1)][1:]
    barrier = pltpu.get_barrier_semaphore()
    for peer in peers:
        pl.semaphore_signal(barrier, inc=1, device_id=peer, device_id_type=MESH)
    pl.semaphore_wait(barrier, len(peers))


def _call(body, hosted=(), *, name, in_specs, out_specs, out_shape, args, grid=(), scratch_shapes=(), aliased=None):
    n_in, n_out, n_scr = len(in_specs), len(out_specs), len(scratch_shapes)
    total = math.prod(grid)
    mid_step = max(0, (3 * total) // 4 - 1)

    def full(*refs):
        pos = [0]

        def take(k):
            pos[0] += k
            return refs[pos[0] - k:pos[0]]

        ins, h_in = take(n_in), [take(len(h.inputs)) for h in hosted]
        outs, h_out = take(n_out), [take(len(h.out_shape)) for h in hosted]
        scr, h_sem = take(n_scr), [take(len(h.sems)) for h in hosted]
        step = 0
        for axis, size in enumerate(grid):
            step = step * size + pl.program_id(axis)

        def phase(at, method):
            if not hosted:
                return

            def run():
                if method == "start":
                    _entry_barrier(reach)
                for h, s, o, m in zip(hosted, h_in, h_out, h_sem):
                    getattr(h, method)(s, o, m)

            if total == 1:
                run()
            else:
                pl.when(step == at)(run)

        phase(0, "start")
        body(*ins, *outs, *scr)
        phase(mid_step, "mid")
        phase(total - 1, "finish")

    aliases, i0, o0 = dict(aliased or {}), n_in, n_out
    for h in hosted:
        aliases.update({i0 + i: o0 + o for i, o in h.aliases.items()})
        i0, o0 = i0 + len(h.inputs), o0 + len(h.out_shape)
    reach = max((h.reach for h in hosted), default=None)
    params = dict(vmem_limit_bytes=VMEM_LIMIT_V7X)
    if hosted:
        params["collective_id"] = reach
    results = pl.pallas_call(
        full, name=name, grid=grid,
        in_specs=list(in_specs) + [ANY] * (i0 - n_in),
        out_specs=list(out_specs) + [ANY] * (o0 - n_out),
        out_shape=list(out_shape) + [s for h in hosted for s in h.out_shape],
        scratch_shapes=list(scratch_shapes) + [s for h in hosted for s in h.sems],
        input_output_aliases=aliases,
        compiler_params=pltpu.CompilerParams(**params),
    )(*args, *[s for h in hosted for s in h.inputs])
    outs, extras, pos = list(results[:n_out]), [], n_out
    for h in hosted:
        extras.append(list(results[pos:pos + h.n]))
        pos += len(h.out_shape)
    return outs, extras


def cast_shards(shards, name, hosted=()):
    n = len(shards)

    def body(*refs):
        for x_ref, o_ref in zip(refs[:n], refs[n:]):
            o_ref[...] = x_ref[...].astype(BF16)

    whole = lambda s: pl.BlockSpec(s.shape, lambda: (0,) * s.ndim)
    return _call(body, hosted, name=name, in_specs=[whole(s) for s in shards], out_specs=[whole(s) for s in shards],
                 out_shape=[jax.ShapeDtypeStruct(s.shape, BF16) for s in shards], args=list(shards))


class AllExchange:
    def __init__(self, arrays):
        n = len(arrays)
        self.inputs, self.n, self.aliases, self.reach = list(arrays), n, {}, REACH_ALL
        self.out_shape = [jax.ShapeDtypeStruct((2 * N_CHIPS,) + a.shape, a.dtype) for a in arrays]
        self.sems = [pltpu.SemaphoreType.DMA((n,)), pltpu.SemaphoreType.DMA((7 * n,)), pltpu.SemaphoreType.DMA((7 * n,))]

    def _copies(self, src, out, sems):
        local_sem, send_sem, recv_sem = sems
        x, y, c = _mesh_pos()
        me = 4 * x + 2 * y + c
        peers = [(x ^ dx, y ^ dy, c ^ dc) for dx in (0, 1) for dy in (0, 1) for dc in (0, 1)][1:]
        remote = lambda s, d, k, to: pltpu.make_async_remote_copy(
            src_ref=s, dst_ref=d, send_sem=send_sem.at[k], recv_sem=recv_sem.at[k], device_id=to, device_id_type=MESH)
        sends, landed, local = [], [], []
        for t in range(self.n):
            local.append(pltpu.make_async_copy(src[t], out[t].at[me], local_sem.at[t]))
            for p, (px, py, pc) in enumerate(peers):
                sends.append(remote(src[t], out[t].at[me], 7 * t + p, (px, py, pc)))
                here = out[t].at[4 * px + 2 * py + pc]
                landed.append(remote(here, here, 7 * t + p, (px, py, pc)))
        return sends, landed, local

    def start(self, src, out, sems):
        sends, _, local = self._copies(src, out, sems)
        for cp in sends + local:
            cp.start()

    def mid(self, src, out, sems):
        pass

    def finish(self, src, out, sems):
        sends, landed, local = self._copies(src, out, sems)
        for cp in landed:
            cp.wait_recv()
        for cp in sends:
            cp.wait_send()
        for cp in local:
            cp.wait()


MXU_COLS = 256


def _resident(shape):
    return pl.BlockSpec(shape, lambda *_: (0,) * len(shape), pipeline_mode=pl.Buffered(1))


def ffn_up(h, gain, w1, w3, name, hosted=(), mixed=None):
    T, D = h.shape
    F = w1.shape[0]
    tm = min(T, 512)

    def body(*refs):
        if mixed is None:
            h_ref, g_ref, w1_ref, w3_ref, n_ref, ga_ref, gb_ref, s_ref = refs
            hh = h_ref[...]
        else:
            pa_ref, rb_ref, wo_ref, h_ref, g_ref, w1_ref, w3_ref, hh_ref, n_ref, ga_ref, gb_ref, s_ref = refs
            hh = h_ref[...] + _dot(pa_ref[...], wo_ref[0]) + _dot(rb_ref[...], wo_ref[1])
            hh_ref[...] = hh
        n = (hh * _rstd(hh) * g_ref[...]).astype(BF16)
        n_ref[...] = n
        for c in range(0, F, MXU_COLS):
            cols = slice(c, c + MXU_COLS)
            a = _dot_nt(n, w1_ref[cols, :])
            b = _dot_nt(n, w3_ref[cols, :])
            silu, dsilu = _silu_parts(a)
            ga_ref[:, cols] = (b * dsilu).astype(BF16)
            gb_ref[:, cols] = silu.astype(BF16)
            s_ref[:, cols] = (silu * b).astype(BF16)

    act = jax.ShapeDtypeStruct((T, F), BF16)
    act_spec = pl.BlockSpec((tm, F), lambda i: (i, 0))
    row_spec = pl.BlockSpec((tm, D), lambda i: (i, 0))
    in_specs = [row_spec, pl.BlockSpec((1, D), lambda i: (0, 0)), _resident((F, D)), _resident((F, D))]
    out_specs, out_shape, args = [row_spec, act_spec, act_spec, act_spec], [jax.ShapeDtypeStruct((T, D), BF16), act, act, act], [h, gain, w1, w3]
    if mixed is not None:
        pa, rb, woutg = mixed
        W = pa.shape[1]
        in_specs = [pl.BlockSpec((tm, W), lambda i: (i, 0))] * 2 + [_resident((2, W, D))] + in_specs
        out_specs, out_shape = [row_spec] + out_specs, [jax.ShapeDtypeStruct((T, D), F32)] + out_shape
        args = [pa, rb, woutg.reshape(2, W, D)] + args
    return _call(body, hosted, name=name, grid=(T // tm,), in_specs=in_specs, out_specs=out_specs, out_shape=out_shape, args=args)


def ffn_bwd_act(dh, w2, ga, gb, name, hosted=()):
    T, D = dh.shape
    F = w2.shape[0]
    tm = min(T, 512)

    def body(dh_ref, w2_ref, ga_ref, gb_ref, da_ref, db_ref, df_ref):
        df = (0.5 * dh_ref[...]).astype(BF16)
        df_ref[...] = df
        for c in range(0, F, MXU_COLS):
            cols = slice(c, c + MXU_COLS)
            ds = _dot_nt(df, w2_ref[cols, :])
            da_ref[:, cols] = (ds * ga_ref[:, cols].astype(F32)).astype(BF16)
            db_ref[:, cols] = (ds * gb_ref[:, cols].astype(F32)).astype(BF16)

    act = jax.ShapeDtypeStruct((T, F), BF16)
    act_spec = pl.BlockSpec((tm, F), lambda i: (i, 0))
    row_spec = pl.BlockSpec((tm, D), lambda i: (i, 0))
    return _call(
        body, hosted, name=name, grid=(T // tm,),
        in_specs=[row_spec, _resident((F, D)), act_spec, act_spec],
        out_specs=[act_spec, act_spec, row_spec],
        out_shape=[act, act, jax.ShapeDtypeStruct((T, D), BF16)],
        args=[dh, w2, ga, gb])


def ffn_dw(xs, y, halves, name, hosted=()):
    T, F = xs[0].shape
    D = y.shape[1]
    nx, fh = len(xs), F // halves
    tk = min(T, 512)
    nk = T // tk

    def body(*refs):
        y_ref, x_refs, o_refs, accs = refs[0], refs[1:1 + nx], refs[1 + nx:1 + 2 * nx], refs[1 + 2 * nx:]
        k = pl.program_id(1)

        @pl.when(k == 0)
        def _():
            for acc in accs:
                acc[...] = jnp.zeros_like(acc)

        yy = y_ref[...]
        for x_ref, acc in zip(x_refs, accs):
            acc[...] += _dot_tn(x_ref[...], yy)

        @pl.when(k == nk - 1)
        def _():
            for o_ref, acc in zip(o_refs, accs):
                o_ref[...] = acc[...].astype(BF16)

    out = jax.ShapeDtypeStruct((F, D), BF16)
    return _call(
        body, hosted, name=name, grid=(halves, nk),
        in_specs=[pl.BlockSpec((tk, D), lambda j, k: (k, 0))] + [pl.BlockSpec((tk, fh), lambda j, k: (k, j))] * nx,
        out_specs=[pl.BlockSpec((fh, D), lambda j, k: (j, 0))] * nx,
        out_shape=[out] * nx,
        scratch_shapes=[pltpu.VMEM((fh, D), F32)] * nx,
        args=[y] + list(xs))


def ffn_bwd_in(da, db, w1, w3, h, gain, dh, name, hosted=()):
    T, F = da.shape
    D = h.shape[1]
    tm = min(T, 512)

    def body(da_ref, db_ref, w1_ref, w3_ref, h_ref, g_ref, dh_ref, o_ref, dg_ref):
        dn = _dot(da_ref[...], w1_ref[...]) + _dot(db_ref[...], w3_ref[...])
        dhn, dg = _rmsnorm_bwd(dn, h_ref[...], g_ref[...])
        o_ref[...] = dh_ref[...] + dhn

        @pl.when(pl.program_id(0) == 0)
        def _():
            dg_ref[...] = jnp.zeros_like(dg_ref)

        dg_ref[...] += jnp.sum(dg, axis=0, keepdims=True)

    act_spec = pl.BlockSpec((tm, F), lambda i: (i, 0))
    row_spec = pl.BlockSpec((tm, D), lambda i: (i, 0))
    vec_spec = pl.BlockSpec((1, D), lambda i: (0, 0))
    return _call(
        body, hosted, name=name, grid=(T // tm,),
        in_specs=[act_spec, act_spec, _resident((F, D)), _resident((F, D)), row_spec, vec_spec, row_spec],
        out_specs=[row_spec, vec_spec],
        out_shape=[jax.ShapeDtypeStruct((T, D), F32), jax.ShapeDtypeStruct((1, D), F32)],
        args=[da, db, w1, w3, h, gain, dh])


def ffn_down_mix_in(s, w2, h, gain, wing, name, hosted=()):
    T, F = s.shape
    D = h.shape[1]
    nsh, _, Cs = wing.shape
    tm = min(T, 512)

    def body(s_ref, w2_ref, h_ref, g_ref, w_ref, hh_ref, u_ref, p_ref):
        hh = h_ref[...] + 0.5 * _dot(s_ref[...], w2_ref[...])
        hh_ref[...] = hh
        u = (hh * _rstd(hh) * g_ref[...]).astype(BF16)
        u_ref[...] = u
        for j in range(nsh):
            p_ref[:, j * Cs:(j + 1) * Cs] = _dot(u, w_ref[j])

    row_spec = pl.BlockSpec((tm, D), lambda i: (i, 0))
    return _call(
        body, hosted, name=name, grid=(T // tm,),
        in_specs=[pl.BlockSpec((tm, F), lambda i: (i, 0)), _resident((F, D)), row_spec, pl.BlockSpec((1, D), lambda i: (0, 0)),
                  _resident((nsh, D, Cs))],
        out_specs=[row_spec, row_spec, pl.BlockSpec((tm, nsh * Cs), lambda i: (i, 0))],
        out_shape=[jax.ShapeDtypeStruct((T, D), F32), jax.ShapeDtypeStruct((T, D), BF16), jax.ShapeDtypeStruct((T, nsh * Cs), F32)],
        args=[s, w2, h, gain, wing])


def mix_out_bwd(dh, woutg, a, b, name, hosted=()):
    T, D = dh.shape
    W = a.shape[1]
    nsh, Rs, _ = woutg.shape
    wout = woutg.reshape(2, W, D)
    tk = min(T, 512)
    nk = T // tk

    def body(dh_ref, w_ref, a_ref, b_ref, da_ref, db_ref, dw_ref, acc):
        k = pl.program_id(0)

        @pl.when(k == 0)
        def _():
            acc[...] = jnp.zeros_like(acc)

        dhb = dh_ref[...].astype(BF16)
        da_ref[...] = _dot_nt(dhb, w_ref[0])
        db_ref[...] = _dot_nt(dhb, w_ref[1])
        acc[0:W, :] += _dot_tn(a_ref[...], dhb)
        acc[W:2 * W, :] += _dot_tn(b_ref[...], dhb)

        @pl.when(k == nk - 1)
        def _():
            for j in range(nsh):
                dw_ref[j] = acc[j * Rs:(j + 1) * Rs, :].astype(BF16)

    return _call(
        body, hosted, name=name, grid=(nk,),
        in_specs=[pl.BlockSpec((tk, D), lambda k: (k, 0)), pl.BlockSpec((2, W, D), lambda k: (0, 0, 0)),
                  pl.BlockSpec((tk, W), lambda k: (k, 0)), pl.BlockSpec((tk, W), lambda k: (k, 0))],
        out_specs=[pl.BlockSpec((tk, W), lambda k: (k, 0)), pl.BlockSpec((tk, W), lambda k: (k, 0)),
                   pl.BlockSpec((nsh, Rs, D), lambda k: (0, 0, 0))],
        out_shape=[jax.ShapeDtypeStruct((T, W), F32), jax.ShapeDtypeStruct((T, W), F32),
                   jax.ShapeDtypeStruct((nsh, Rs, D), BF16)],
        scratch_shapes=[pltpu.VMEM((2 * W, D), F32)],
        args=[dh, wout, a, b])


def _dproj_block(g):
    return (g // N_GROUPS + N_GROUPS) % (N_GROUPS + 1), g % N_GROUPS


def mix_dwin(u, dproj, nsh, name, hosted=()):
    T, D = u.shape
    Hd = HEAD_DIM
    slabs, _, width = dproj.shape
    blocks = slabs * width // Hd
    Cs = blocks * Hd // nsh
    tk = min(T, 512)
    nk = T // tk

    def body(u_ref, d_ref, o_ref, acc):
        k = pl.program_id(0)

        @pl.when(k == 0)
        def _():
            acc[...] = jnp.zeros_like(acc)

        where = [_dproj_block(g) for g in range(blocks)]
        d = jnp.concatenate([d_ref[slab, :, col * Hd:(col + 1) * Hd] for slab, col in where], axis=1)
        acc[...] += _dot_tn(u_ref[...], d)

        @pl.when(k == nk - 1)
        def _():
            for j in range(nsh):
                o_ref[j] = acc[:, j * Cs:(j + 1) * Cs].astype(BF16)

    return _call(
        body, hosted, name=name, grid=(nk,),
        in_specs=[pl.BlockSpec((tk, D), lambda k: (k, 0)), pl.BlockSpec((slabs, tk, width), lambda k: (0, k, 0))],
        out_specs=[pl.BlockSpec((nsh, D, Cs), lambda k: (0, 0, 0))],
        out_shape=[jax.ShapeDtypeStruct((nsh, D, Cs), BF16)],
        scratch_shapes=[pltpu.VMEM((D, blocks * Hd), F32)],
        args=[u, dproj])


def mix_in_bwd(dproj, wing, h, gain, dh, name, hosted=()):
    T, D = h.shape
    nsh, _, Cs = wing.shape
    Hd = HEAD_DIM
    per = Cs // Hd
    tm = min(T, 512)

    def body(d_ref, w_ref, h_ref, g_ref, dh_ref, o_ref, dg_ref):
        def shard(j):
            blocks = [_dproj_block(per * j + i) for i in range(per)]
            return jnp.concatenate([d_ref[slab, :, col * Hd:(col + 1) * Hd] for slab, col in blocks], axis=1)

        du = _dot_nt(shard(0), w_ref[0])
        for j in range(1, nsh):
            du += _dot_nt(shard(j), w_ref[j])
        dhn, dg = _rmsnorm_bwd(du, h_ref[...], g_ref[...])
        o_ref[...] = dh_ref[...] + dhn

        @pl.when(pl.program_id(0) == 0)
        def _():
            dg_ref[...] = jnp.zeros_like(dg_ref)

        dg_ref[...] += jnp.sum(dg, axis=0, keepdims=True)

    row_spec = pl.BlockSpec((tm, D), lambda i: (i, 0))
    vec_spec = pl.BlockSpec((1, D), lambda i: (0, 0))
    return _call(
        body, hosted, name=name, grid=(T // tm,),
        in_specs=[pl.BlockSpec((dproj.shape[0], tm, dproj.shape[2]), lambda i: (0, i, 0)),
                  pl.BlockSpec((nsh, D, Cs), lambda i: (0, 0, 0)), row_spec, vec_spec, row_spec],
        out_specs=[row_spec, vec_spec],
        out_shape=[jax.ShapeDtypeStruct((T, D), F32), jax.ShapeDtypeStruct((1, D), F32)],
        args=[dproj, wing, h, gain, dh])


POOL_WINDOWS = (2, 4, 8, 16)


def _pool_window(x, window, T, trailing):
    rows = lax.broadcasted_iota(jnp.int32, x.shape, 0)
    s, k = x, 1
    while k < window:
        if trailing:
            s = s + jnp.where(rows >= k, pltpu.roll(s, k, 0), 0.0)
        else:
            s = s + jnp.where(rows < T - k, pltpu.roll(s, T - k, 0), 0.0)
        k *= 2
    return s


def _pool_count(window, shape):
    rows = lax.broadcasted_iota(jnp.int32, shape, 0)
    return jnp.minimum(rows + 1, window).astype(F32)


def _per_group(work):
    for group, window in enumerate(POOL_WINDOWS):
        pl.when(pl.program_id(0) == group)(lambda window=window: work(window))


def pool_fwd(proj, pool_w, pool_scale, name, hosted=()):
    T = proj.shape[0]
    Hd = HEAD_DIM

    def body(x_ref, w_ref, sc_ref, a_ref):
        def work(window):
            x = x_ref[...]
            pooled = _pool_window(x, window, T, True) / _pool_count(window, x.shape) - x
            a_ref[...] = (_dot(pooled.astype(BF16), w_ref[0].astype(BF16)) * sc_ref[...]).astype(BF16)

        _per_group(work)

    return _call(
        body, hosted, name=name, grid=(N_GROUPS,),
        in_specs=[pl.BlockSpec((T, Hd), lambda g: (0, g)), pl.BlockSpec((1, Hd, Hd), lambda g: (g, 0, 0)),
                  pl.BlockSpec((1, Hd), lambda g: (0, g))],
        out_specs=[pl.BlockSpec((T, Hd), lambda g: (0, g))],
        out_shape=[jax.ShapeDtypeStruct((T, N_GROUPS * Hd), BF16)],
        args=[proj, pool_w, pool_scale])


def pool_bwd(proj, da, pool_w, pool_scale, name, hosted=()):
    T = proj.shape[0]
    Hd = HEAD_DIM

    def body(x_ref, da_ref, w_ref, sc_ref, dx_ref, dw_ref, dsc_ref):
        def work(window):
            x = x_ref[...]
            cnt = _pool_count(window, x.shape)
            pooled = (_pool_window(x, window, T, True) / cnt - x).astype(BF16)
            wb = w_ref[0].astype(BF16)
            dav = da_ref[...]
            dsc_ref[...] = jnp.sum(dav * _dot(pooled, wb), axis=0, keepdims=True)
            dout = (dav * sc_ref[...]).astype(BF16)
            dw_ref[0] = _dot_tn(pooled, dout)
            dpooled = _dot_nt(dout, wb)
            dx_ref[0] = (_pool_window(dpooled / cnt, window, T, False) - dpooled).astype(BF16)

        _per_group(work)

    col_spec = pl.BlockSpec((T, Hd), lambda g: (0, g))
    return _call(
        body, hosted, name=name, grid=(N_GROUPS,),
        in_specs=[col_spec, col_spec, pl.BlockSpec((1, Hd, Hd), lambda g: (g, 0, 0)), pl.BlockSpec((1, Hd), lambda g: (0, g))],
        out_specs=[pl.BlockSpec((1, T, Hd), lambda g: (N_GROUPS, 0, g)), pl.BlockSpec((1, Hd, Hd), lambda g: (g, 0, 0)),
                   pl.BlockSpec((1, Hd), lambda g: (0, g))],
        out_shape=[jax.ShapeDtypeStruct((N_GROUPS + 1, T, N_GROUPS * Hd), BF16), jax.ShapeDtypeStruct((N_GROUPS, Hd, Hd), F32),
                   jax.ShapeDtypeStruct((1, N_GROUPS * Hd), F32)],
        args=[proj, da, pool_w, pool_scale])


def _ret_tables(T):
    Hd, C = HEAD_DIM, RET_CHUNK
    inv_freq = 1.0 / (ROPE_BASE ** (jnp.arange(0, Hd, 2, dtype=F32) / Hd))
    ang = jnp.arange(T, dtype=F32)[:, None] * inv_freq[None, :]
    cos, sin = jnp.cos(ang), jnp.sin(ang)
    cos2 = jnp.concatenate([cos, cos], axis=-1)
    sin2 = jnp.concatenate([-sin, sin], axis=-1)
    log_gamma = jnp.log1p(-jnp.exp2(-5.0 - jnp.arange(N_GROUPS, dtype=F32)))
    pos = jnp.arange(C, dtype=F32)
    rel = pos[:, None] - pos[None, :]
    intra = jnp.where(rel[None] >= 0, jnp.exp(log_gamma[:, None, None] * jnp.maximum(rel, 0.0)[None]), 0.0)
    k_tail = jnp.exp(log_gamma[:, None] * (C - 1 - pos)[None, :])
    q_head = jnp.exp(log_gamma[:, None] * (pos + 1.0)[None, :])
    chunk_decay = jnp.exp(log_gamma * C)
    wide = lambda t: jnp.broadcast_to(t[:, :, None], (N_GROUPS, C, Hd))
    return cos2, sin2, intra, wide(k_tail), wide(q_head), jnp.broadcast_to(chunk_decay[:, None, None], (N_GROUPS, 1, Hd))


def _rope(x, cos2, sin2):
    return x * cos2 + pltpu.roll(x, HEAD_DIM // 2, 1) * sin2


def _rope_t(d, cos2, sin2):
    return d * cos2 + pltpu.roll(d * sin2, HEAD_DIM // 2, 1)


def _ret_specs(T, tseg, seg_of):
    Hd, G = HEAD_DIM, N_GROUPS
    col = lambda kind: pl.BlockSpec((tseg, Hd), lambda h, s: (seg_of(s), G * kind + h))
    tab = pl.BlockSpec((T, Hd), lambda h, s: (0, 0))
    head = pl.BlockSpec((1, RET_CHUNK, Hd), lambda h, s: (h, 0, 0))
    cd = pl.BlockSpec((1, 1, Hd), lambda h, s: (h, 0, 0))
    gain = pl.BlockSpec((1, Hd), lambda h, s: (0, h))
    return col, tab, head, cd, gain


def ret_fwd(proj, ret_norm, tables, name, hosted=()):
    T = proj.shape[0]
    Hd, C, G = HEAD_DIM, RET_CHUNK, N_GROUPS
    tseg = min(T, 2048)
    nseg, nck = T // tseg, tseg // C
    scale = Hd ** -0.5
    cos2, sin2, intra, k_tail, q_head, chunk_decay = tables

    def body(q_ref, k_ref, v_ref, g_ref, gain_ref, cos_ref, sin_ref, m_ref, kt_ref, qh_ref, cd_ref,
             b_ref, o_ref, rp_ref, state):
        @pl.when(pl.program_id(1) == 0)
        def _():
            state[...] = jnp.zeros_like(state)

        def chunk(ci, carry):
            rows = pl.ds(pl.multiple_of(ci * C, C), C)
            at = pl.ds(pl.multiple_of(pl.program_id(1) * tseg + ci * C, C), C)
            cos, sin = cos_ref[at, :], sin_ref[at, :]
            qr = _rope(q_ref[rows, :], cos, sin)
            kr = _rope(k_ref[rows, :], cos, sin) * scale
            qb, kb, vb = qr.astype(BF16), kr.astype(BF16), v_ref[rows, :].astype(BF16)
            r = state[...]
            rp_ref[0, ci] = r.astype(BF16)
            sc = _dot_nt(qb, kb) * m_ref[0]
            o = _dot(sc.astype(BF16), vb) + _dot((qr * qh_ref[0]).astype(BF16), r.astype(BF16))
            state[...] = cd_ref[0] * r + _dot_tn((kr * kt_ref[0]).astype(BF16), vb)
            o_ref[rows, :] = o
            on = o * _rstd(o)
            b_ref[rows, :] = (jax.nn.silu(g_ref[rows, :]) * (on * gain_ref[...])).astype(BF16)
            return carry

        lax.fori_loop(0, nck, chunk, 0, unroll=True)

    col, tab, head, cd, gain = _ret_specs(T, tseg, lambda s: s)
    out_col = pl.BlockSpec((tseg, Hd), lambda h, s: (s, h))
    return _call(
        body, hosted, name=name, grid=(G, nseg),
        in_specs=[col(1), col(2), col(3), col(4), gain, tab, tab, head, head, head, cd],
        out_specs=[out_col, out_col, pl.BlockSpec((1, nck, Hd, Hd), lambda h, s: (h, s, 0, 0))],
        out_shape=[jax.ShapeDtypeStruct((T, G * Hd), BF16), jax.ShapeDtypeStruct((T, G * Hd), F32),
                   jax.ShapeDtypeStruct((G, T // C, Hd, Hd), BF16)],
        scratch_shapes=[pltpu.VMEM((Hd, Hd), F32)],
        args=[proj, proj, proj, proj, ret_norm, cos2, sin2, intra, k_tail, q_head, chunk_decay])


def ret_bwd(proj, db, o_pre, r_prev, ret_norm, tables, dproj, name, hosted=()):
    T = proj.shape[0]
    Hd, C, G = HEAD_DIM, RET_CHUNK, N_GROUPS
    tseg = min(T, 2048)
    nseg, nck = T // tseg, tseg // C
    scale = Hd ** -0.5
    cos2, sin2, intra, k_tail, q_head, chunk_decay = tables

    def body(q_ref, k_ref, v_ref, g_ref, db_ref, o_ref, rp_ref, gain_ref, cos_ref, sin_ref, m_ref, kt_ref, qh_ref, cd_ref,
             _, d_ref, dgain_ref, gstate):
        @pl.when(pl.program_id(1) == 0)
        def _():
            gstate[...] = jnp.zeros_like(gstate)
            dgain_ref[...] = jnp.zeros_like(dgain_ref)

        def chunk(t, carry):
            ci = nck - 1 - t
            rows = pl.ds(pl.multiple_of(ci * C, C), C)
            at = pl.ds(pl.multiple_of((nseg - 1 - pl.program_id(1)) * tseg + ci * C, C), C)
            cos, sin = cos_ref[at, :], sin_ref[at, :]
            qr = _rope(q_ref[rows, :], cos, sin)
            kr = _rope(k_ref[rows, :], cos, sin) * scale
            qb, kb, vb = qr.astype(BF16), kr.astype(BF16), v_ref[rows, :].astype(BF16)
            qhb, ktb = (qr * qh_ref[0]).astype(BF16), (kr * kt_ref[0]).astype(BF16)
            sc = (_dot_nt(qb, kb) * m_ref[0]).astype(BF16)
            o = o_ref[rows, :]
            rstd = _rstd(o)
            on = o * rstd
            gain = gain_ref[...]
            silu, dsilu = _silu_parts(g_ref[rows, :])
            dy = db_ref[rows, :]
            dgain_ref[...] += jnp.sum(dy * silu * on, axis=0, keepdims=True)
            dg = dy * on * gain * dsilu
            don = dy * silu * gain
            dob = (rstd * (don - on * jnp.mean(don * on, axis=-1, keepdims=True))).astype(BF16)
            gn = gstate[...]
            gb = gn.astype(BF16)
            da = (_dot_nt(dob, vb) * m_ref[0]).astype(BF16)
            dq = _dot(da, kb) + _dot_nt(dob, rp_ref[0, ci]) * qh_ref[0]
            dk = _dot_tn(da, qb) + _dot_nt(vb, gb) * kt_ref[0]
            dv = _dot_tn(sc, dob) + _dot(ktb, gb)
            gstate[...] = cd_ref[0] * gn + _dot_tn(qhb, dob)
            d_ref[0, rows, :] = _rope_t(dq, cos, sin).astype(BF16)
            d_ref[1, rows, :] = _rope_t(dk * scale, cos, sin).astype(BF16)
            d_ref[2, rows, :] = dv.astype(BF16)
            d_ref[3, rows, :] = dg.astype(BF16)
            return carry

        lax.fori_loop(0, nck, chunk, 0, unroll=True)

    rev = lambda s: nseg - 1 - s
    col, tab, head, cd, gain = _ret_specs(T, tseg, rev)
    act = pl.BlockSpec((tseg, Hd), lambda h, s: (rev(s), h))
    return _call(
        body, hosted, name=name, grid=(G, nseg),
        in_specs=[col(1), col(2), col(3), col(4), act, act, pl.BlockSpec((1, nck, Hd, Hd), lambda h, s: (h, rev(s), 0, 0)),
                  gain, tab, tab, head, head, head, cd, ANY],
        out_specs=[pl.BlockSpec((4, tseg, Hd), lambda h, s: (0, rev(s), h)), gain],
        out_shape=[jax.ShapeDtypeStruct(dproj.shape, BF16), jax.ShapeDtypeStruct((1, G * Hd), F32)],
        scratch_shapes=[pltpu.VMEM((Hd, Hd), F32)], aliased={14: 0},
        args=[proj, proj, proj, proj, db, o_pre, r_prev, ret_norm, cos2, sin2, intra, k_tail, q_head, chunk_decay, dproj])


def ffn_down_loss(s, w2, h, gain, target, name, hosted=()):
    T, F = s.shape
    D = h.shape[1]
    tm = min(T, 512)

    def body(s_ref, w2_ref, h_ref, g_ref, t_ref, dh_ref, loss_ref, dg_ref):
        @pl.when(pl.program_id(0) == 0)
        def _():
            loss_ref[...] = jnp.zeros_like(loss_ref)
            dg_ref[...] = jnp.zeros_like(dg_ref)

        hh = h_ref[...] + 0.5 * _dot(s_ref[...], w2_ref[...])
        gain_v = g_ref[...]
        err = hh * _rstd(hh) * gain_v - t_ref[...]
        loss_ref[...] += 0.5 * jnp.sum(jnp.mean(err * err, axis=-1, keepdims=True), axis=0, keepdims=True)
        dhn, dg = _rmsnorm_bwd(err * (1.0 / D), hh, gain_v)
        dh_ref[...] = dhn
        dg_ref[...] += jnp.sum(dg, axis=0, keepdims=True)

    row_spec = pl.BlockSpec((tm, D), lambda i: (i, 0))
    vec_spec = pl.BlockSpec((1, D), lambda i: (0, 0))
    return _call(
        body, hosted, name=name, grid=(T // tm,),
        in_specs=[pl.BlockSpec((tm, F), lambda i: (i, 0)), _resident((F, D)), row_spec, vec_spec, row_spec],
        out_specs=[row_spec, pl.BlockSpec((1, LANES), lambda i: (0, 0)), vec_spec],
        out_shape=[jax.ShapeDtypeStruct((T, D), F32), jax.ShapeDtypeStruct((1, LANES), F32), jax.ShapeDtypeStruct((1, D), F32)],
        args=[s, w2, h, gain, target])


def prereduce(grads, recvs, place, name):
    nt = len(grads)
    nsh, R, C = grads[0].shape
    rh = R // 2

    def body(place_ref, *refs):
        for t in range(nt):
            g_ref, r_ref, o_ref, own_ref = refs[2 * t], refs[2 * t + 1], refs[2 * nt + 2 * t], refs[2 * nt + 2 * t + 1]
            piece = (g_ref[...].astype(F32) + r_ref[...].astype(F32)).astype(BF16)
            o_ref[...] = piece

            @pl.when(pl.program_id(0) == place_ref[1])
            def _():
                own_ref[...] = piece

    outs = pl.pallas_call(
        body, name=name,
        grid_spec=pltpu.PrefetchScalarGridSpec(
            num_scalar_prefetch=1, grid=(nsh,),
            in_specs=[pl.BlockSpec((1, rh, C), lambda j, p: (j, p[0], 0)), pl.BlockSpec((1, rh, C), lambda j, p: (j, 0, 0))] * nt,
            out_specs=[pl.BlockSpec((1, rh, C), lambda j, p: (j, 0, 0)),
                       pl.BlockSpec((1, rh, C), lambda j, p: (p[1], p[0], 0))] * nt),
        out_shape=[jax.ShapeDtypeStruct((nsh, rh, C), BF16), jax.ShapeDtypeStruct((nsh, R, C), BF16)] * nt,
        compiler_params=pltpu.CompilerParams(vmem_limit_bytes=VMEM_LIMIT_V7X),
    )(place, *[a for pair in zip(grads, recvs) for a in pair])
    return [(outs[2 * t], outs[2 * t + 1]) for t in range(nt)]


def _adamw(w, g, m, v):
    m = ADAM_B1 * m + (1.0 - ADAM_B1) * g
    v = ADAM_B2 * v + (1.0 - ADAM_B2) * (g * g)
    m_hat = m / (1.0 - ADAM_B1 ** ADAM_STEP)
    v_hat = v / (1.0 - ADAM_B2 ** ADAM_STEP)
    return -ADAM_LR * (m_hat / (jnp.sqrt(v_hat) + ADAM_EPS) + ADAM_WD * w), m, v


def adamw_sharded(tensors, name, hosted=()):
    nt = len(tensors)
    nsh = tensors[0][0].shape[0]
    shapes = [t[0].shape[1:] for t in tensors]

    def fits(steps):
        if any(R % (steps * BF16_TILE_ROWS) for R, _ in shapes):
            return False
        return sum(2 * (R // steps) * -(-C // LANES) * LANES * (nsh * 2 + 7 * 4) for R, C in shapes) <= ADAMW_VMEM_BUDGET

    steps = min(s for s in range(1, min(R for R, _ in shapes) // BF16_TILE_ROWS + 1) if fits(s))

    def body(*refs):
        ins, outs = refs[:4 * nt], refs[4 * nt:]
        for t in range(nt):
            p_ref, w_ref, m_ref, v_ref = ins[4 * t:4 * t + 4]
            g_ref, d_ref, nm_ref, nv_ref = outs[4 * t:4 * t + 4]
            g = p_ref[0].astype(F32)
            for i in range(1, nsh):
                g += p_ref[i].astype(F32)
            g_ref[...] = g
            d_ref[...], nm_ref[...], nv_ref[...] = _adamw(w_ref[...], g, m_ref[...], v_ref[...])

    in_specs, out_specs, out_shape = [], [], []
    for R, C in shapes:
        spec = pl.BlockSpec((R // steps, C), lambda i: (i, 0))
        in_specs += [pl.BlockSpec((nsh, R // steps, C), lambda i: (0, i, 0)), spec, spec, spec]
        out_specs += [spec] * 4
        out_shape += [jax.ShapeDtypeStruct((R, C), F32)] * 4
    return _call(body, hosted, name=name, grid=(steps,), in_specs=in_specs, out_specs=out_specs, out_shape=out_shape,
                 args=[a for tensor in tensors for a in tensor])


def adamw_small(packs, params, loss_packs, name):
    n = len(packs)
    ndev = loss_packs.shape[0]

    def body(*refs):
        p_refs, loss_ref, wmv = refs[:n], refs[n], refs[n + 1:4 * n + 1]
        outs, loss_out = refs[4 * n + 1:8 * n + 1], refs[8 * n + 1]
        total = lambda r: sum((r[i] for i in range(1, ndev)), r[0])
        loss_out[...] = total(loss_ref)
        for k in range(n):
            g = total(p_refs[k])
            outs[4 * k][...] = g
            outs[4 * k + 1][...], outs[4 * k + 2][...], outs[4 * k + 3][...] = _adamw(
                wmv[3 * k][...], g, wmv[3 * k + 1][...], wmv[3 * k + 2][...])

    out_shape = [jax.ShapeDtypeStruct(p[0].shape, F32) for p in params for _ in range(4)]
    outs = pl.pallas_call(body, name=name, out_shape=out_shape + [jax.ShapeDtypeStruct(loss_packs.shape[1:], F32)],
                          compiler_params=pltpu.CompilerParams(vmem_limit_bytes=VMEM_LIMIT_V7X),
                          )(*packs, loss_packs, *[a for p in params for a in p])
    return [outs[4 * k:4 * k + 4] for k in range(n)], outs[4 * n]


BIG = ("ffn1_w1", "ffn1_w3", "ffn1_w2", "w_in", "w_out", "ffn2_w1", "ffn2_w3", "ffn2_w2")
TRANSPOSED = ("ffn1_w1", "ffn1_w3", "ffn2_w1", "ffn2_w3")
SMALL = ("pool_w", "mix_norm", "pool_scale", "ret_norm", "ffn2_norm", "final_norm", "ffn1_norm")
WEIGHTS = ("ffn1_norm", "ffn1_w1", "ffn1_w3", "ffn1_w2", "mix_norm", "w_in", "pool_w", "pool_scale", "ret_norm", "w_out",
           "ffn2_norm", "ffn2_w1", "ffn2_w3", "ffn2_w2", "final_norm")


def kernel(x, ffn1_norm, ffn1_w1, ffn1_w3, ffn1_w2, mix_norm, w_in, pool_w, pool_scale, ret_norm, w_out, ffn2_norm, ffn2_w1, ffn2_w3, ffn2_w2, final_norm, loss_target, m_ffn1_norm, m_ffn1_w1, m_ffn1_w3, m_ffn1_w2, m_mix_norm, m_w_in, m_pool_w, m_pool_scale, m_ret_norm, m_w_out, m_ffn2_norm, m_ffn2_w1, m_ffn2_w3, m_ffn2_w2, m_final_norm, v_ffn1_norm, v_ffn1_w1, v_ffn1_w3, v_ffn1_w2, v_mix_norm, v_w_in, v_pool_w, v_pool_scale, v_ret_norm, v_w_out, v_ffn2_norm, v_ffn2_w1, v_ffn2_w3, v_ffn2_w2, v_final_norm):
    w = dict(ffn1_norm=ffn1_norm, ffn1_w1=ffn1_w1, ffn1_w3=ffn1_w3, ffn1_w2=ffn1_w2, mix_norm=mix_norm, w_in=w_in, pool_w=pool_w,
             pool_scale=pool_scale, ret_norm=ret_norm, w_out=w_out, ffn2_norm=ffn2_norm, ffn2_w1=ffn2_w1, ffn2_w3=ffn2_w3,
             ffn2_w2=ffn2_w2, final_norm=final_norm)
    m = dict(ffn1_norm=m_ffn1_norm, ffn1_w1=m_ffn1_w1, ffn1_w3=m_ffn1_w3, ffn1_w2=m_ffn1_w2, mix_norm=m_mix_norm, w_in=m_w_in,
             pool_w=m_pool_w, pool_scale=m_pool_scale, ret_norm=m_ret_norm, w_out=m_w_out, ffn2_norm=m_ffn2_norm, ffn2_w1=m_ffn2_w1,
             ffn2_w3=m_ffn2_w3, ffn2_w2=m_ffn2_w2, final_norm=m_final_norm)
    v = dict(ffn1_norm=v_ffn1_norm, ffn1_w1=v_ffn1_w1, ffn1_w3=v_ffn1_w3, ffn1_w2=v_ffn1_w2, mix_norm=v_mix_norm, w_in=v_w_in,
             pool_w=v_pool_w, pool_scale=v_pool_scale, ret_norm=v_ret_norm, w_out=v_w_out, ffn2_norm=v_ffn2_norm, ffn2_w1=v_ffn2_w1,
             ffn2_w3=v_ffn2_w3, ffn2_w2=v_ffn2_w2, final_norm=v_final_norm)
    xs, target = x[0], loss_target[0]
    T = xs.shape[0]
    tables = _ret_tables(T)
    place = jnp.stack([lax.axis_index("c"), 2 * lax.axis_index("x") + lax.axis_index("y")]).astype(jnp.int32)
    local = lambda d, k: jnp.transpose(d[k][0]) if k in TRANSPOSED else d[k][0]
    result = lambda o, k: jnp.transpose(o)[None] if k in TRANSPOSED else o[None]
    first = ("ffn1_w1", "ffn1_w3")
    sh = {k: local(w, k).astype(BF16) for k in first}
    gather = lambda *names: [ChipExchange([sh[k] for k in names], False)]
    wg, grad, delta, new_m, new_v = {}, {}, {}, {}, {}

    def update(names, pieces, name, hosted=()):
        outs, extras = adamw_sharded([(p, local(w, k), local(m, k), local(v, k)) for k, p in zip(names, pieces)], name, hosted)
        for t, k in enumerate(names):
            grad[k], delta[k], new_m[k], new_v[k] = [result(o, k) for o in outs[4 * t:4 * t + 4]]
        return extras

    def reduce_in_chip(name, *pairs):
        reduced = prereduce([p for p, _ in pairs], [r for _, r in pairs], place, "prereduce_" + name)
        return reduced[0] if len(pairs) == 1 else reduced

    scatter = lambda *reduced: ChipExchange([r[0] for r in reduced], True, [r[1] for r in reduced])
    whole = lambda k: wg[k].reshape(-1, wg[k].shape[-1])
    sharded = lambda g: g.reshape(N_CHIPS, -1, g.shape[-1])

    later = [k for k in BIG if k not in first]
    casts, ((wg["ffn1_w1"], wg["ffn1_w3"]),) = cast_shards([local(w, k) for k in later], "cast_gather_ffn1", gather(*first))
    sh.update(zip(later, casts))
    (n1, ga1, gb1, s1), ((wg["ffn1_w2"], wg["w_in"]),) = ffn_up(
        xs, ffn1_norm, whole("ffn1_w1"), whole("ffn1_w3"), "ffn1_up", gather("ffn1_w2", "w_in"))
    (h1, u, proj), ((wg["w_out"], wg["ffn2_w1"]),) = ffn_down_mix_in(
        s1, whole("ffn1_w2"), xs, mix_norm, wg["w_in"], "ffn1_down_mix_in", gather("w_out", "ffn2_w1"))
    (pa,), _ = pool_fwd(proj, pool_w[0], pool_scale, "pool_fwd")
    (rb, o_pre, r_prev), ((wg["ffn2_w3"],),) = ret_fwd(proj, ret_norm, tables, "ret_fwd", gather("ffn2_w3"))
    (h2, n2, ga2, gb2, s2), ((wg["ffn2_w2"],),) = ffn_up(
        h1, ffn2_norm, whole("ffn2_w1"), whole("ffn2_w3"), "mix_out_ffn2_up", gather("ffn2_w2"), mixed=(pa, rb, wg["w_out"]))
    (dh3, loss, d_final), _ = ffn_down_loss(s2, whole("ffn2_w2"), h2, final_norm[None], target, "ffn2_down_loss")

    (da2, db2, df2), _ = ffn_bwd_act(dh3, whole("ffn2_w2"), ga2, gb2, "ffn2_bwd_act")
    (g_f2w2,), _ = ffn_dw([s2], df2, 1, "ffn2_dw2")
    g_f2w2 = sharded(g_f2w2)
    (g_f2w1, g_f2w3), ((r_f2w2,),) = ffn_dw([da2, db2], n2, 2, "ffn2_dw13", [SiblingExchange([g_f2w2])])
    g_f2w1, g_f2w3 = sharded(g_f2w1), sharded(g_f2w3)
    p_f2w2 = reduce_in_chip("ffn2_w2", (g_f2w2, r_f2w2))
    (dh2, d_ffn2), ((q_f2w2,), (r_f2w1, r_f2w3)) = ffn_bwd_in(
        da2, db2, whole("ffn2_w1"), whole("ffn2_w3"), h2, ffn2_norm, dh3, "ffn2_bwd_in",
        [scatter(p_f2w2), SiblingExchange([g_f2w1, g_f2w3])])
    p_f2w1, p_f2w3 = reduce_in_chip("ffn2_w13", (g_f2w1, r_f2w1), (g_f2w3, r_f2w3))
    (dpa, drb, g_wout), _ = mix_out_bwd(dh2, wg["w_out"], pa, rb, "mix_out_bwd")
    (dproj, d_pool_w, d_pool_scale), _ = pool_bwd(proj, dpa, pool_w[0], pool_scale, "pool_bwd")
    (dproj, d_ret_norm), ((q_f2w1, q_f2w3), (r_wout,)) = ret_bwd(
        proj, drb, o_pre, r_prev, ret_norm, tables, dproj, "ret_bwd", [scatter(p_f2w1, p_f2w3), SiblingExchange([g_wout])])
    p_wout = reduce_in_chip("w_out", (g_wout, r_wout))
    (g_win,), ((q_wout,),) = mix_dwin(u, dproj, N_CHIPS, "mix_dwin", [scatter(p_wout)])
    (dh1, d_mix), ((r_win,),) = mix_in_bwd(dproj, wg["w_in"], h1, mix_norm, dh2, "mix_in_bwd", [SiblingExchange([g_win])])
    p_win = reduce_in_chip("w_in", (g_win, r_win))
    (da1, db1, df1), ((q_win,),) = ffn_bwd_act(dh1, whole("ffn1_w2"), ga1, gb1, "ffn1_bwd_act", [scatter(p_win)])
    d_small = {"pool_w": d_pool_w.reshape(-1, LANES), "mix_norm": d_mix, "pool_scale": d_pool_scale, "ret_norm": d_ret_norm,
               "ffn2_norm": d_ffn2, "final_norm": d_final}
    (g_f1w1, g_f1w3), (packs,) = ffn_dw([da1, db1], n1, 2, "ffn1_dw13", [AllExchange([d_small[k] for k in SMALL[:-1]] + [loss])])
    g_f1w1, g_f1w3 = sharded(g_f1w1), sharded(g_f1w3)
    (g_f1w2,), ((r_f1w1, r_f1w3),) = ffn_dw([s1], df1, 1, "ffn1_dw2", [SiblingExchange([g_f1w1, g_f1w3])])
    g_f1w2 = sharded(g_f1w2)
    p_f1w1, p_f1w3 = reduce_in_chip("ffn1_w13", (g_f1w1, r_f1w1), (g_f1w3, r_f1w3))
    (dx, d_ffn1), ((q_f1w1, q_f1w3), (r_f1w2,)) = ffn_bwd_in(
        da1, db1, whole("ffn1_w1"), whole("ffn1_w3"), xs, ffn1_norm, dh1, "ffn1_bwd_in",
        [scatter(p_f1w1, p_f1w3), SiblingExchange([g_f1w2])])
    p_f1w2 = reduce_in_chip("ffn1_w2", (g_f1w2, r_f1w2))

    (q_f1w2,), (late,) = update(["w_in", "w_out", "ffn2_w2"], [q_win, q_wout, q_f2w2], "adamw_mix_w2",
                                [scatter(p_f1w2), AllExchange([d_ffn1])])
    update(["ffn2_w1", "ffn2_w3", "ffn1_w1", "ffn1_w3"], [q_f2w1, q_f2w3, q_f1w1, q_f1w3], "adamw_w13")
    update(["ffn1_w2"], [q_f1w2], "adamw_ffn1_w2")
    flat = lambda t, k: t[k].reshape(-1, LANES) if k == "pool_w" else t[k].reshape(1, -1)
    updated, loss_sum = adamw_small(packs[:-1] + [late], [[flat(t, k) for t in (w, m, v)] for k in SMALL], packs[-1], "adamw_small")
    for k, outs in zip(SMALL, updated):
        grad[k], delta[k], new_m[k], new_v[k] = [o.reshape(w[k].shape) for o in outs]
    loss = loss_sum[0, 0]

    return (loss, dx[None], *[grad[k] for k in WEIGHTS], *[delta[k] for k in WEIGHTS],
            *[new_m[k] for k in WEIGHTS], *[new_v[k] for k in WEIGHTS])
```

```python
import math

import jax
import jax.numpy as jnp
from jax import lax
from jax.experimental import pallas as pl
from jax.experimental.pallas import tpu as pltpu

F32 = jnp.float32
BF16 = jnp.bfloat16

EPS = 1e-6
LANES = 128
BF16_TILE_ROWS = 16
N_CHIPS = 4
N_GROUPS = 4
HEAD_DIM = 128
RET_CHUNK = 128
ROPE_BASE = 10000.0
ADAM_LR, ADAM_B1, ADAM_B2, ADAM_EPS, ADAM_WD, ADAM_STEP = 0.001, 0.9, 0.999, 1e-08, 0.01, 10
VMEM_LIMIT_V7X = 56 * 1024 * 1024
ADAMW_VMEM_BUDGET = 32 * 1024 * 1024
MESH = pl.DeviceIdType.MESH
ANY = pl.BlockSpec(memory_space=pl.ANY)


def _dot(a, b):
    return jnp.dot(a, b, preferred_element_type=F32)


def _dot_nt(a, b):
    return lax.dot_general(a, b, (((1,), (1,)), ((), ())), preferred_element_type=F32)


def _dot_tn(a, b):
    return lax.dot_general(a, b, (((0,), (0,)), ((), ())), preferred_element_type=F32)


def _rstd(h):
    return lax.rsqrt(jnp.mean(h * h, axis=-1, keepdims=True) + EPS)


def _rmsnorm_bwd(dn, h, gain):
    r = _rstd(h)
    nh = h * r
    dnh = dn * gain
    dh = r * (dnh - nh * jnp.mean(dnh * nh, axis=-1, keepdims=True))
    return dh, dn * nh


def _silu_parts(a):
    sig = jax.nn.sigmoid(a)
    silu = a * sig
    return silu, sig + silu * (1.0 - sig)


def _mesh_pos():
    return lax.axis_index("x"), lax.axis_index("y"), lax.axis_index("c")


class ChipExchange:
    def __init__(self, srcs, scatter, placed=()):
        n = len(srcs)
        self.inputs, self.scatter, self.n, self.reach = list(srcs) + list(placed), scatter, n, REACH_CHIPS
        self.aliases = {n + t: t for t in range(n)} if scatter else {}
        self.half_rows = [s.shape[1] if scatter else s.shape[0] // 2 for s in srcs]
        self.out_shape = [jax.ShapeDtypeStruct((N_CHIPS, 2 * rh, s.shape[-1]), s.dtype) for s, rh in zip(srcs, self.half_rows)]
        if scatter:
            self.out_shape += [jax.ShapeDtypeStruct((2, rh // 2, s.shape[-1]), s.dtype) for s, rh in zip(srcs, self.half_rows)]
        dma = pltpu.SemaphoreType.DMA
        self.sems = [dma((4 * n,)), dma((4 * n,)), dma((2 * n,)), dma((2 * n,)), dma((4 * n,)), dma((4 * n,))]

    def _copies(self, src, out, sems):
        hop1_send, hop1_recv, hop2_send, hop2_recv, d2d_send, d2d_recv = sems
        x, y, c = _mesh_pos()
        me, dg = 2 * x + y, 2 * (1 - x) + (1 - y)
        sibling = (x, y, 1 - c)
        n = self.n
        mine, theirs = c, 1 - c

        def nb(a):
            nx, ny = x ^ (1 - a), y ^ a
            return 2 * nx + ny, (nx, ny, c)

        def remote(s, d, send, recv, k, to):
            return pltpu.make_async_remote_copy(src_ref=s, dst_ref=d, send_sem=send.at[k], recv_sem=recv.at[k],
                                                device_id=to, device_id_type=MESH)

        class Copies:
            def slot(_, t, chip, half):
                rh = self.half_rows[t]
                return out[t].at[chip, pl.ds(half * rh, rh), :]

            def quarter(_, t, chip, q):
                qh = self.half_rows[t] // 2
                return out[t].at[chip, pl.ds(mine * 2 * qh + q * qh, qh), :]

            def own_shard(k, t):
                return remote(src[t], out[t].at[me], d2d_send, d2d_recv, 4 * t + 3, sibling)

            def hop1(k, t, a, transit=False):
                rh = self.half_rows[t]
                chip, to = nb(a)
                if transit:
                    piece = src[t].at[dg, pl.ds(a * (rh // 2), rh // 2), :]
                    return remote(piece, out[n + t].at[a], hop1_send, hop1_recv, 4 * t + 2 + a, to)
                piece = src[t].at[chip] if self.scatter else src[t].at[pl.ds(mine * rh, rh), :]
                return remote(piece, k.slot(t, me, mine), hop1_send, hop1_recv, 4 * t + a, to)

            def landed1(k, t, a, transit=False):
                here = out[n + t].at[a] if transit else k.slot(t, nb(a)[0], mine)
                return remote(here, here, hop1_send, hop1_recv, 4 * t + (2 if transit else 0) + a, sibling)

            def hop2(k, t, q):
                origin, to = nb(q)[0], nb(1 - q)[1]
                piece = out[n + t].at[q] if self.scatter else k.quarter(t, origin, q)
                return remote(piece, k.quarter(t, origin, q), hop2_send, hop2_recv, 2 * t + q, to)

            def landed2(k, t, q):
                here = k.quarter(t, dg, q)
                return remote(here, here, hop2_send, hop2_recv, 2 * t + q, sibling)

            def d2d(k, t, p, chip, own=False, arriving=False):
                if arriving:
                    there = k.slot(t, chip, theirs)
                    return remote(there, there, d2d_send, d2d_recv, 4 * t + p, sibling)
                piece = src[t].at[me] if own else k.slot(t, chip, mine)
                return remote(piece, k.slot(t, chip, mine), d2d_send, d2d_recv, 4 * t + p, sibling)

        return Copies(), nb, me, dg, c

    def start(self, src, out, sems):
        k, nb, me, dg, c = self._copies(src, out, sems)
        for t in range(self.n):
            for first in range(2):
                a = first ^ c
                k.hop1(t, a).start()
                if self.scatter:
                    k.hop1(t, a, transit=True).start()
            if self.scatter:
                k.d2d(t, 3, me, own=True).start()
            else:
                k.own_shard(t).start()

    def mid(self, src, out, sems):
        k, nb, me, dg, c = self._copies(src, out, sems)
        for t in range(self.n):
            for first in range(2):
                a = first ^ c
                if self.scatter:
                    k.landed1(t, a, transit=True).wait_recv()
                    k.hop2(t, a).start()
                k.landed1(t, a).wait_recv()
                if not self.scatter:
                    k.hop2(t, a).start()
                k.d2d(t, a, nb(a)[0]).start()

    def finish(self, src, out, sems):
        k, nb, me, dg, c = self._copies(src, out, sems)
        for t in range(self.n):
            for q in range(2):
                k.landed2(t, q).wait_recv()
            k.d2d(t, 2, dg).start()
        for t in range(self.n):
            for a in range(2):
                k.d2d(t, a, nb(a)[0], arriving=True).wait_recv()
            k.d2d(t, 2, dg, arriving=True).wait_recv()
            if self.scatter:
                k.d2d(t, 3, me, arriving=True).wait_recv()
        for t in range(self.n):
            for a in range(2):
                k.hop1(t, a).wait_send()
                if self.scatter:
                    k.hop1(t, a, transit=True).wait_send()
                k.hop2(t, a).wait_send()
                k.d2d(t, a, nb(a)[0]).wait_send()
            k.d2d(t, 2, dg).wait_send()
            if self.scatter:
                k.d2d(t, 3, me, own=True).wait_send()
            else:
                k.own_shard(t).wait()


class SiblingExchange:
    def __init__(self, grads):
        self.inputs, self.n, self.aliases, self.reach = list(grads), len(grads), {}, REACH_SIBLING
        self.half_rows = [g.shape[1] // 2 for g in grads]
        self.out_shape = [jax.ShapeDtypeStruct((g.shape[0], rh, g.shape[2]), g.dtype) for g, rh in zip(grads, self.half_rows)]
        self.sems = [pltpu.SemaphoreType.DMA((self.n,)), pltpu.SemaphoreType.DMA((self.n,))]

    def _plan(self, src, out, sems):
        x, y, c = _mesh_pos()
        return [pltpu.make_async_remote_copy(
            src_ref=src[t].at[:, pl.ds((1 - c) * self.half_rows[t], self.half_rows[t]), :], dst_ref=out[t],
            send_sem=sems[0].at[t], recv_sem=sems[1].at[t], device_id=(x, y, 1 - c), device_id_type=MESH) for t in range(self.n)]

    def start(self, src, out, sems):
        for cp in self._plan(src, out, sems):
            cp.start()

    def mid(self, src, out, sems):
        pass

    def finish(self, src, out, sems):
        for cp in self._plan(src, out, sems):
            cp.wait()


REACH_SIBLING, REACH_CHIPS, REACH_ALL = 0, 1, 2


def _entry_barrier(reach):
    x, y, c = _mesh_pos()
    peers = [(x, y, 1 - c)]
    if reach == REACH_CHIPS:
        peers += [(1 - x, y, c), (x, 1 - y, c)]
    elif reach == REACH_ALL:
        peers = [(x ^ dx, y ^ dy, c ^ dc) for dx in (0, 1) for dy in (0, 1) for dc in (0, 1)][1:]
    barrier = pltpu.get_barrier_semaphore()
    for peer in peers:
        pl.semaphore_signal(barrier, inc=1, device_id=peer, device_id_type=MESH)
    pl.semaphore_wait(barrier, len(peers))


def _call(body, hosted=(), *, name, in_specs, out_specs, out_shape, args, grid=(), scratch_shapes=(), aliased=None):
    n_in, n_out, n_scr = len(in_specs), len(out_specs), len(scratch_shapes)
    total = math.prod(grid)
    mid_step = max(0, (3 * total) // 4 - 1)

    def full(*refs):
        pos = [0]

        def take(k):
            pos[0] += k
            return refs[pos[0] - k:pos[0]]

        ins, h_in = take(n_in), [take(len(h.inputs)) for h in hosted]
        outs, h_out = take(n_out), [take(len(h.out_shape)) for h in hosted]
        scr, h_sem = take(n_scr), [take(len(h.sems)) for h in hosted]
        step = 0
        for axis, size in enumerate(grid):
            step = step * size + pl.program_id(axis)

        def phase(at, method):
            if not hosted:
                return

            def run():
                if method == "start":
                    _entry_barrier(reach)
                for h, s, o, m in zip(hosted, h_in, h_out, h_sem):
                    getattr(h, method)(s, o, m)

            if total == 1:
                run()
            else:
                pl.when(step == at)(run)

        phase(0, "start")
        body(*ins, *outs, *scr)
        phase(mid_step, "mid")
        phase(total - 1, "finish")

    aliases, i0, o0 = dict(aliased or {}), n_in, n_out
    for h in hosted:
        aliases.update({i0 + i: o0 + o for i, o in h.aliases.items()})
        i0, o0 = i0 + len(h.inputs), o0 + len(h.out_shape)
    reach = max((h.reach for h in hosted), default=None)
    params = dict(vmem_limit_bytes=VMEM_LIMIT_V7X)
    if hosted:
        params["collective_id"] = reach
    results = pl.pallas_call(
        full, name=name, grid=grid,
        in_specs=list(in_specs) + [ANY] * (i0 - n_in),
        out_specs=list(out_specs) + [ANY] * (o0 - n_out),
        out_shape=list(out_shape) + [s for h in hosted for s in h.out_shape],
        scratch_shapes=list(scratch_shapes) + [s for h in hosted for s in h.sems],
        input_output_aliases=aliases,
        compiler_params=pltpu.CompilerParams(**params),
    )(*args, *[s for h in hosted for s in h.inputs])
    outs, extras, pos = list(results[:n_out]), [], n_out
    for h in hosted:
        extras.append(list(results[pos:pos + h.n]))
        pos += len(h.out_shape)
    return outs, extras


def cast_shards(shards, name, hosted=()):
    n = len(shards)

    def body(*refs):
        for x_ref, o_ref in zip(refs[:n], refs[n:]):
            o_ref[...] = x_ref[...].astype(BF16)

    whole = lambda s: pl.BlockSpec(s.shape, lambda: (0,) * s.ndim)
    return _call(body, hosted, name=name, in_specs=[whole(s) for s in shards], out_specs=[whole(s) for s in shards],
                 out_shape=[jax.ShapeDtypeStruct(s.shape, BF16) for s in shards], args=list(shards))


class AllExchange:
    def __init__(self, arrays):
        n = len(arrays)
        self.inputs, self.n, self.aliases, self.reach = list(arrays), n, {}, REACH_ALL
        self.out_shape = [jax.ShapeDtypeStruct((2 * N_CHIPS,) + a.shape, a.dtype) for a in arrays]
        self.sems = [pltpu.SemaphoreType.DMA((n,)), pltpu.SemaphoreType.DMA((7 * n,)), pltpu.SemaphoreType.DMA((7 * n,))]

    def _copies(self, src, out, sems):
        local_sem, send_sem, recv_sem = sems
        x, y, c = _mesh_pos()
        me = 4 * x + 2 * y + c
        peers = [(x ^ dx, y ^ dy, c ^ dc) for dx in (0, 1) for dy in (0, 1) for dc in (0, 1)][1:]
        remote = lambda s, d, k, to: pltpu.make_async_remote_copy(
            src_ref=s, dst_ref=d, send_sem=send_sem.at[k], recv_sem=recv_sem.at[k], device_id=to, device_id_type=MESH)
        sends, landed, local = [], [], []
        for t in range(self.n):
            local.append(pltpu.make_async_copy(src[t], out[t].at[me], local_sem.at[t]))
            for p, (px, py, pc) in enumerate(peers):
                sends.append(remote(src[t], out[t].at[me], 7 * t + p, (px, py, pc)))
                here = out[t].at[4 * px + 2 * py + pc]
                landed.append(remote(here, here, 7 * t + p, (px, py, pc)))
        return sends, landed, local

    def start(self, src, out, sems):
        sends, _, local = self._copies(src, out, sems)
        for cp in sends + local:
            cp.start()

    def mid(self, src, out, sems):
        pass

    def finish(self, src, out, sems):
        sends, landed, local = self._copies(src, out, sems)
        for cp in landed:
            cp.wait_recv()
        for cp in sends:
            cp.wait_send()
        for cp in local:
            cp.wait()


MXU_COLS = 256


def _resident(shape):
    return pl.BlockSpec(shape, lambda *_: (0,) * len(shape), pipeline_mode=pl.Buffered(1))


def ffn_up(h, gain, w1, w3, name, hosted=(), mixed=None):
    T, D = h.shape
    F = w1.shape[0]
    tm = min(T, 512)

    def body(*refs):
        if mixed is None:
            h_ref, g_ref, w1_ref, w3_ref, n_ref, ga_ref, gb_ref, s_ref = refs
            hh = h_ref[...]
        else:
            pa_ref, rb_ref, wo_ref, h_ref, g_ref, w1_ref, w3_ref, hh_ref, n_ref, ga_ref, gb_ref, s_ref = refs
            hh = h_ref[...] + _dot(pa_ref[...], wo_ref[0]) + _dot(rb_ref[...], wo_ref[1])
            hh_ref[...] = hh
        n = (hh * _rstd(hh) * g_ref[...]).astype(BF16)
        n_ref[...] = n
        for c in range(0, F, MXU_COLS):
            cols = slice(c, c + MXU_COLS)
            a = _dot_nt(n, w1_ref[cols, :])
            b = _dot_nt(n, w3_ref[cols, :])
            silu, dsilu = _silu_parts(a)
            ga_ref[:, cols] = (b * dsilu).astype(BF16)
            gb_ref[:, cols] = silu.astype(BF16)
            s_ref[:, cols] = (silu * b).astype(BF16)

    act = jax.ShapeDtypeStruct((T, F), BF16)
    act_spec = pl.BlockSpec((tm, F), lambda i: (i, 0))
    row_spec = pl.BlockSpec((tm, D), lambda i: (i, 0))
    in_specs = [row_spec, pl.BlockSpec((1, D), lambda i: (0, 0)), _resident((F, D)), _resident((F, D))]
    out_specs, out_shape, args = [row_spec, act_spec, act_spec, act_spec], [jax.ShapeDtypeStruct((T, D), BF16), act, act, act], [h, gain, w1, w3]
    if mixed is not None:
        pa, rb, woutg = mixed
        W = pa.shape[1]
        in_specs = [pl.BlockSpec((tm, W), lambda i: (i, 0))] * 2 + [_resident((2, W, D))] + in_specs
        out_specs, out_shape = [row_spec] + out_specs, [jax.ShapeDtypeStruct((T, D), F32)] + out_shape
        args = [pa, rb, woutg.reshape(2, W, D)] + args
    return _call(body, hosted, name=name, grid=(T // tm,), in_specs=in_specs, out_specs=out_specs, out_shape=out_shape, args=args)


def ffn_bwd_act(dh, w2, ga, gb, name, hosted=()):
    T, D = dh.shape
    F = w2.shape[0]
    tm = min(T, 512)

    def body(dh_ref, w2_ref, ga_ref, gb_ref, da_ref, db_ref, df_ref):
        df = (0.5 * dh_ref[...]).astype(BF16)
        df_ref[...] = df
        for c in range(0, F, MXU_COLS):
            cols = slice(c, c + MXU_COLS)
            ds = _dot_nt(df, w2_ref[cols, :])
            da_ref[:, cols] = (ds * ga_ref[:, cols].astype(F32)).astype(BF16)
            db_ref[:, cols] = (ds * gb_ref[:, cols].astype(F32)).astype(BF16)

    act = jax.ShapeDtypeStruct((T, F), BF16)
    act_spec = pl.BlockSpec((tm, F), lambda i: (i, 0))
    row_spec = pl.BlockSpec((tm, D), lambda i: (i, 0))
    return _call(
        body, hosted, name=name, grid=(T // tm,),
        in_specs=[row_spec, _resident((F, D)), act_spec, act_spec],
        out_specs=[act_spec, act_spec, row_spec],
        out_shape=[act, act, jax.ShapeDtypeStruct((T, D), BF16)],
        args=[dh, w2, ga, gb])


def ffn_dw(xs, y, halves, name, hosted=()):
    T, F = xs[0].shape
    D = y.shape[1]
    nx, fh = len(xs), F // halves
    tk = min(T, 512)
    nk = T // tk

    def body(*refs):
        y_ref, x_refs, o_refs, accs = refs[0], refs[1:1 + nx], refs[1 + nx:1 + 2 * nx], refs[1 + 2 * nx:]
        k = pl.program_id(1)

        @pl.when(k == 0)
        def _():
            for acc in accs:
                acc[...] = jnp.zeros_like(acc)

        yy = y_ref[...]
        for x_ref, acc in zip(x_refs, accs):
            acc[...] += _dot_tn(x_ref[...], yy)

        @pl.when(k == nk - 1)
        def _():
            for o_ref, acc in zip(o_refs, accs):
                o_ref[...] = acc[...].astype(BF16)

    out = jax.ShapeDtypeStruct((F, D), BF16)
    return _call(
        body, hosted, name=name, grid=(halves, nk),
        in_specs=[pl.BlockSpec((tk, D), lambda j, k: (k, 0))] + [pl.BlockSpec((tk, fh), lambda j, k: (k, j))] * nx,
        out_specs=[pl.BlockSpec((fh, D), lambda j, k: (j, 0))] * nx,
        out_shape=[out] * nx,
        scratch_shapes=[pltpu.VMEM((fh, D), F32)] * nx,
        args=[y] + list(xs))


def ffn_bwd_in(da, db, w1, w3, h, gain, dh, name, hosted=()):
    T, F = da.shape
    D = h.shape[1]
    tm = min(T, 512)

    def body(da_ref, db_ref, w1_ref, w3_ref, h_ref, g_ref, dh_ref, o_ref, dg_ref):
        dn = _dot(da_ref[...], w1_ref[...]) + _dot(db_ref[...], w3_ref[...])
        dhn, dg = _rmsnorm_bwd(dn, h_ref[...], g_ref[...])
        o_ref[...] = dh_ref[...] + dhn

        @pl.when(pl.program_id(0) == 0)
        def _():
            dg_ref[...] = jnp.zeros_like(dg_ref)

        dg_ref[...] += jnp.sum(dg, axis=0, keepdims=True)

    act_spec = pl.BlockSpec((tm, F), lambda i: (i, 0))
    row_spec = pl.BlockSpec((tm, D), lambda i: (i, 0))
    vec_spec = pl.BlockSpec((1, D), lambda i: (0, 0))
    return _call(
        body, hosted, name=name, grid=(T // tm,),
        in_specs=[act_spec, act_spec, _resident((F, D)), _resident((F, D)), row_spec, vec_spec, row_spec],
        out_specs=[row_spec, vec_spec],
        out_shape=[jax.ShapeDtypeStruct((T, D), F32), jax.ShapeDtypeStruct((1, D), F32)],
        args=[da, db, w1, w3, h, gain, dh])


def ffn_down_mix_in(s, w2, h, gain, wing, name, hosted=()):
    T, F = s.shape
    D = h.shape[1]
    nsh, _, Cs = wing.shape
    tm = min(T, 512)

    def body(s_ref, w2_ref, h_ref, g_ref, w_ref, hh_ref, u_ref, p_ref):
        hh = h_ref[...] + 0.5 * _dot(s_ref[...], w2_ref[...])
        hh_ref[...] = hh
        u = (hh * _rstd(hh) * g_ref[...]).astype(BF16)
        u_ref[...] = u
        for j in range(nsh):
            p_ref[:, j * Cs:(j + 1) * Cs] = _dot(u, w_ref[j])

    row_spec = pl.BlockSpec((tm, D), lambda i: (i, 0))
    return _call(
        body, hosted, name=name, grid=(T // tm,),
        in_specs=[pl.BlockSpec((tm, F), lambda i: (i, 0)), _resident((F, D)), row_spec, pl.BlockSpec((1, D), lambda i: (0, 0)),
                  _resident((nsh, D, Cs))],
        out_specs=[row_spec, row_spec, pl.BlockSpec((tm, nsh * Cs), lambda i: (i, 0))],
        out_shape=[jax.ShapeDtypeStruct((T, D), F32), jax.ShapeDtypeStruct((T, D), BF16), jax.ShapeDtypeStruct((T, nsh * Cs), F32)],
        args=[s, w2, h, gain, wing])


def mix_out_bwd(dh, woutg, a, b, name, hosted=()):
    T, D = dh.shape
    W = a.shape[1]
    nsh, Rs, _ = woutg.shape
    wout = woutg.reshape(2, W, D)
    tk = min(T, 512)
    nk = T // tk

    def body(dh_ref, w_ref, a_ref, b_ref, da_ref, db_ref, dw_ref, acc):
        k = pl.program_id(0)

        @pl.when(k == 0)
        def _():
            acc[...] = jnp.zeros_like(acc)

        dhb = dh_ref[...].astype(BF16)
        da_ref[...] = _dot_nt(dhb, w_ref[0])
        db_ref[...] = _dot_nt(dhb, w_ref[1])
        acc[0:W, :] += _dot_tn(a_ref[...], dhb)
        acc[W:2 * W, :] += _dot_tn(b_ref[...], dhb)

        @pl.when(k == nk - 1)
        def _():
            for j in range(nsh):
                dw_ref[j] = acc[j * Rs:(j + 1) * Rs, :].astype(BF16)

    return _call(
        body, hosted, name=name, grid=(nk,),
        in_specs=[pl.BlockSpec((tk, D), lambda k: (k, 0)), pl.BlockSpec((2, W, D), lambda k: (0, 0, 0)),
                  pl.BlockSpec((tk, W), lambda k: (k, 0)), pl.BlockSpec((tk, W), lambda k: (k, 0))],
        out_specs=[pl.BlockSpec((tk, W), lambda k: (k, 0)), pl.BlockSpec((tk, W), lambda k: (k, 0)),
                   pl.BlockSpec((nsh, Rs, D), lambda k: (0, 0, 0))],
        out_shape=[jax.ShapeDtypeStruct((T, W), F32), jax.ShapeDtypeStruct((T, W), F32),
                   jax.ShapeDtypeStruct((nsh, Rs, D), BF16)],
        scratch_shapes=[pltpu.VMEM((2 * W, D), F32)],
        args=[dh, wout, a, b])


def _dproj_block(g):
    return (g // N_GROUPS + N_GROUPS) % (N_GROUPS + 1), g % N_GROUPS


def mix_dwin(u, dproj, nsh, name, hosted=()):
    T, D = u.shape
    Hd = HEAD_DIM
    slabs, _, width = dproj.shape
    blocks = slabs * width // Hd
    Cs = blocks * Hd // nsh
    tk = min(T, 512)
    nk = T // tk

    def body(u_ref, d_ref, o_ref, acc):
        k = pl.program_id(0)

        @pl.when(k == 0)
        def _():
            acc[...] = jnp.zeros_like(acc)

        where = [_dproj_block(g) for g in range(blocks)]
        d = jnp.concatenate([d_ref[slab, :, col * Hd:(col + 1) * Hd] for slab, col in where], axis=1)
        acc[...] += _dot_tn(u_ref[...], d)

        @pl.when(k == nk - 1)
        def _():
            for j in range(nsh):
                o_ref[j] = acc[:, j * Cs:(j + 1) * Cs].astype(BF16)

    return _call(
        body, hosted, name=name, grid=(nk,),
        in_specs=[pl.BlockSpec((tk, D), lambda k: (k, 0)), pl.BlockSpec((slabs, tk, width), lambda k: (0, k, 0))],
        out_specs=[pl.BlockSpec((nsh, D, Cs), lambda k: (0, 0, 0))],
        out_shape=[jax.ShapeDtypeStruct((nsh, D, Cs), BF16)],
        scratch_shapes=[pltpu.VMEM((D, blocks * Hd), F32)],
        args=[u, dproj])


def mix_in_bwd(dproj, wing, h, gain, dh, name, hosted=()):
    T, D = h.shape
    nsh, _, Cs = wing.shape
    Hd = HEAD_DIM
    per = Cs // Hd
    tm = min(T, 512)

    def body(d_ref, w_ref, h_ref, g_ref, dh_ref, o_ref, dg_ref):
        def shard(j):
            blocks = [_dproj_block(per * j + i) for i in range(per)]
            return jnp.concatenate([d_ref[slab, :, col * Hd:(col + 1) * Hd] for slab, col in blocks], axis=1)

        du = _dot_nt(shard(0), w_ref[0])
        for j in range(1, nsh):
            du += _dot_nt(shard(j), w_ref[j])
        dhn, dg = _rmsnorm_bwd(du, h_ref[...], g_ref[...])
        o_ref[...] = dh_ref[...] + dhn

        @pl.when(pl.program_id(0) == 0)
        def _():
            dg_ref[...] = jnp.zeros_like(dg_ref)

        dg_ref[...] += jnp.sum(dg, axis=0, keepdims=True)

    row_spec = pl.BlockSpec((tm, D), lambda i: (i, 0))
    vec_spec = pl.BlockSpec((1, D), lambda i: (0, 0))
    return _call(
        body, hosted, name=name, grid=(T // tm,),
        in_specs=[pl.BlockSpec((dproj.shape[0], tm, dproj.shape[2]), lambda i: (0, i, 0)),
                  pl.BlockSpec((nsh, D, Cs), lambda i: (0, 0, 0)), row_spec, vec_spec, row_spec],
        out_specs=[row_spec, vec_spec],
        out_shape=[jax.ShapeDtypeStruct((T, D), F32), jax.ShapeDtypeStruct((1, D), F32)],
        args=[dproj, wing, h, gain, dh])


POOL_WINDOWS = (2, 4, 8, 16)


def _pool_window(x, window, T, trailing):
    rows = lax.broadcasted_iota(jnp.int32, x.shape, 0)
    s, k = x, 1
    while k < window:
        if trailing:
            s = s + jnp.where(rows >= k, pltpu.roll(s, k, 0), 0.0)
        else:
            s = s + jnp.where(rows < T - k, pltpu.roll(s, T - k, 0), 0.0)
        k *= 2
    return s


def _pool_count(window, shape):
    rows = lax.broadcasted_iota(jnp.int32, shape, 0)
    return jnp.minimum(rows + 1, window).astype(F32)


def _per_group(work):
    for group, window in enumerate(POOL_WINDOWS):
        pl.when(pl.program_id(0) == group)(lambda window=window: work(window))


def pool_fwd(proj, pool_w, pool_scale, name, hosted=()):
    T = proj.shape[0]
    Hd = HEAD_DIM

    def body(x_ref, w_ref, sc_ref, a_ref):
        def work(window):
            x = x_ref[...]
            pooled = _pool_window(x, window, T, True) / _pool_count(window, x.shape) - x
            a_ref[...] = (_dot(pooled.astype(BF16), w_ref[0].astype(BF16)) * sc_ref[...]).astype(BF16)

        _per_group(work)

    return _call(
        body, hosted, name=name, grid=(N_GROUPS,),
        in_specs=[pl.BlockSpec((T, Hd), lambda g: (0, g)), pl.BlockSpec((1, Hd, Hd), lambda g: (g, 0, 0)),
                  pl.BlockSpec((1, Hd), lambda g: (0, g))],
        out_specs=[pl.BlockSpec((T, Hd), lambda g: (0, g))],
        out_shape=[jax.ShapeDtypeStruct((T, N_GROUPS * Hd), BF16)],
        args=[proj, pool_w, pool_scale])


def pool_bwd(proj, da, pool_w, pool_scale, name, hosted=()):
    T = proj.shape[0]
    Hd = HEAD_DIM

    def body(x_ref, da_ref, w_ref, sc_ref, dx_ref, dw_ref, dsc_ref):
        def work(window):
            x = x_ref[...]
            cnt = _pool_count(window, x.shape)
            pooled = (_pool_window(x, window, T, True) / cnt - x).astype(BF16)
            wb = w_ref[0].astype(BF16)
            dav = da_ref[...]
            dsc_ref[...] = jnp.sum(dav * _dot(pooled, wb), axis=0, keepdims=True)
            dout = (dav * sc_ref[...]).astype(BF16)
            dw_ref[0] = _dot_tn(pooled, dout)
            dpooled = _dot_nt(dout, wb)
            dx_ref[0] = (_pool_window(dpooled / cnt, window, T, False) - dpooled).astype(BF16)

        _per_group(work)

    col_spec = pl.BlockSpec((T, Hd), lambda g: (0, g))
    return _call(
        body, hosted, name=name, grid=(N_GROUPS,),
        in_specs=[col_spec, col_spec, pl.BlockSpec((1, Hd, Hd), lambda g: (g, 0, 0)), pl.BlockSpec((1, Hd), lambda g: (0, g))],
        out_specs=[pl.BlockSpec((1, T, Hd), lambda g: (N_GROUPS, 0, g)), pl.BlockSpec((1, Hd, Hd), lambda g: (g, 0, 0)),
                   pl.BlockSpec((1, Hd), lambda g: (0, g))],
        out_shape=[jax.ShapeDtypeStruct((N_GROUPS + 1, T, N_GROUPS * Hd), BF16), jax.ShapeDtypeStruct((N_GROUPS, Hd, Hd), F32),
                   jax.ShapeDtypeStruct((1, N_GROUPS * Hd), F32)],
        args=[proj, da, pool_w, pool_scale])


def _ret_tables(T):
    Hd, C = HEAD_DIM, RET_CHUNK
    inv_freq = 1.0 / (ROPE_BASE ** (jnp.arange(0, Hd, 2, dtype=F32) / Hd))
    ang = jnp.arange(T, dtype=F32)[:, None] * inv_freq[None, :]
    cos, sin = jnp.cos(ang), jnp.sin(ang)
    cos2 = jnp.concatenate([cos, cos], axis=-1)
    sin2 = jnp.concatenate([-sin, sin], axis=-1)
    log_gamma = jnp.log1p(-jnp.exp2(-5.0 - jnp.arange(N_GROUPS, dtype=F32)))
    pos = jnp.arange(C, dtype=F32)
    rel = pos[:, None] - pos[None, :]
    intra = jnp.where(rel[None] >= 0, jnp.exp(log_gamma[:, None, None] * jnp.maximum(rel, 0.0)[None]), 0.0)
    k_tail = jnp.exp(log_gamma[:, None] * (C - 1 - pos)[None, :])
    q_head = jnp.exp(log_gamma[:, None] * (pos + 1.0)[None, :])
    chunk_decay = jnp.exp(log_gamma * C)
    wide = lambda t: jnp.broadcast_to(t[:, :, None], (N_GROUPS, C, Hd))
    return cos2, sin2, intra, wide(k_tail), wide(q_head), jnp.broadcast_to(chunk_decay[:, None, None], (N_GROUPS, 1, Hd))


def _rope(x, cos2, sin2):
    return x * cos2 + pltpu.roll(x, HEAD_DIM // 2, 1) * sin2


def _rope_t(d, cos2, sin2):
    return d * cos2 + pltpu.roll(d * sin2, HEAD_DIM // 2, 1)


def _ret_specs(T, tseg, seg_of):
    Hd, G = HEAD_DIM, N_GROUPS
    col = lambda kind: pl.BlockSpec((tseg, Hd), lambda h, s: (seg_of(s), G * kind + h))
    tab = pl.BlockSpec((T, Hd), lambda h, s: (0, 0))
    head = pl.BlockSpec((1, RET_CHUNK, Hd), lambda h, s: (h, 0, 0))
    cd = pl.BlockSpec((1, 1, Hd), lambda h, s: (h, 0, 0))
    gain = pl.BlockSpec((1, Hd), lambda h, s: (0, h))
    return col, tab, head, cd, gain


def ret_fwd(proj, ret_norm, tables, name, hosted=()):
    T = proj.shape[0]
    Hd, C, G = HEAD_DIM, RET_CHUNK, N_GROUPS
    tseg = min(T, 2048)
    nseg, nck = T // tseg, tseg // C
    scale = Hd ** -0.5
    cos2, sin2, intra, k_tail, q_head, chunk_decay = tables

    def body(q_ref, k_ref, v_ref, g_ref, gain_ref, cos_ref, sin_ref, m_ref, kt_ref, qh_ref, cd_ref,
             b_ref, o_ref, rp_ref, state):
        @pl.when(pl.program_id(1) == 0)
        def _():
            state[...] = jnp.zeros_like(state)

        def chunk(ci, carry):
            rows = pl.ds(pl.multiple_of(ci * C, C), C)
            at = pl.ds(pl.multiple_of(pl.program_id(1) * tseg + ci * C, C), C)
            cos, sin = cos_ref[at, :], sin_ref[at, :]
            qr = _rope(q_ref[rows, :], cos, sin)
            kr = _rope(k_ref[rows, :], cos, sin) * scale
            qb, kb, vb = qr.astype(BF16), kr.astype(BF16), v_ref[rows, :].astype(BF16)
            r = state[...]
            rp_ref[0, ci] = r.astype(BF16)
            sc = _dot_nt(qb, kb) * m_ref[0]
            o = _dot(sc.astype(BF16), vb) + _dot((qr * qh_ref[0]).astype(BF16), r.astype(BF16))
            state[...] = cd_ref[0] * r + _dot_tn((kr * kt_ref[0]).astype(BF16), vb)
            o_ref[rows, :] = o
            on = o * _rstd(o)
            b_ref[rows, :] = (jax.nn.silu(g_ref[rows, :]) * (on * gain_ref[...])).astype(BF16)
            return carry

        lax.fori_loop(0, nck, chunk, 0, unroll=True)

    col, tab, head, cd, gain = _ret_specs(T, tseg, lambda s: s)
    out_col = pl.BlockSpec((tseg, Hd), lambda h, s: (s, h))
    return _call(
        body, hosted, name=name, grid=(G, nseg),
        in_specs=[col(1), col(2), col(3), col(4), gain, tab, tab, head, head, head, cd],
        out_specs=[out_col, out_col, pl.BlockSpec((1, nck, Hd, Hd), lambda h, s: (h, s, 0, 0))],
        out_shape=[jax.ShapeDtypeStruct((T, G * Hd), BF16), jax.ShapeDtypeStruct((T, G * Hd), F32),
                   jax.ShapeDtypeStruct((G, T // C, Hd, Hd), BF16)],
        scratch_shapes=[pltpu.VMEM((Hd, Hd), F32)],
        args=[proj, proj, proj, proj, ret_norm, cos2, sin2, intra, k_tail, q_head, chunk_decay])


def ret_bwd(proj, db, o_pre, r_prev, ret_norm, tables, dproj, name, hosted=()):
    T = proj.shape[0]
    Hd, C, G = HEAD_DIM, RET_CHUNK, N_GROUPS
    tseg = min(T, 2048)
    nseg, nck = T // tseg, tseg // C
    scale = Hd ** -0.5
    cos2, sin2, intra, k_tail, q_head, chunk_decay = tables

    def body(q_ref, k_ref, v_ref, g_ref, db_ref, o_ref, rp_ref, gain_ref, cos_ref, sin_ref, m_ref, kt_ref, qh_ref, cd_ref,
             _, d_ref, dgain_ref, gstate):
        @pl.when(pl.program_id(1) == 0)
        def _():
            gstate[...] = jnp.zeros_like(gstate)
            dgain_ref[...] = jnp.zeros_like(dgain_ref)

        def chunk(t, carry):
            ci = nck - 1 - t
            rows = pl.ds(pl.multiple_of(ci * C, C), C)
            at = pl.ds(pl.multiple_of((nseg - 1 - pl.program_id(1)) * tseg + ci * C, C), C)
            cos, sin = cos_ref[at, :], sin_ref[at, :]
            qr = _rope(q_ref[rows, :], cos, sin)
            kr = _rope(k_ref[rows, :], cos, sin) * scale
            qb, kb, vb = qr.astype(BF16), kr.astype(BF16), v_ref[rows, :].astype(BF16)
            qhb, ktb = (qr * qh_ref[0]).astype(BF16), (kr * kt_ref[0]).astype(BF16)
            sc = (_dot_nt(qb, kb) * m_ref[0]).astype(BF16)
            o = o_ref[rows, :]
            rstd = _rstd(o)
            on = o * rstd
            gain = gain_ref[...]
            silu, dsilu = _silu_parts(g_ref[rows, :])
            dy = db_ref[rows, :]
            dgain_ref[...] += jnp.sum(dy * silu * on, axis=0, keepdims=True)
            dg = dy * on * gain * dsilu
            don = dy * silu * gain
            dob = (rstd * (don - on * jnp.mean(don * on, axis=-1, keepdims=True))).astype(BF16)
            gn = gstate[...]
            gb = gn.astype(BF16)
            da = (_dot_nt(dob, vb) * m_ref[0]).astype(BF16)
            dq = _dot(da, kb) + _dot_nt(dob, rp_ref[0, ci]) * qh_ref[0]
            dk = _dot_tn(da, qb) + _dot_nt(vb, gb) * kt_ref[0]
            dv = _dot_tn(sc, dob) + _dot(ktb, gb)
            gstate[...] = cd_ref[0] * gn + _dot_tn(qhb, dob)
            d_ref[0, rows, :] = _rope_t(dq, cos, sin).astype(BF16)
            d_ref[1, rows, :] = _rope_t(dk * scale, cos, sin).astype(BF16)
            d_ref[2, rows, :] = dv.astype(BF16)
            d_ref[3, rows, :] = dg.astype(BF16)
            return carry

        lax.fori_loop(0, nck, chunk, 0, unroll=True)

    rev = lambda s: nseg - 1 - s
    col, tab, head, cd, gain = _ret_specs(T, tseg, rev)
    act = pl.BlockSpec((tseg, Hd), lambda h, s: (rev(s), h))
    return _call(
        body, hosted, name=name, grid=(G, nseg),
        in_specs=[col(1), col(2), col(3), col(4), act, act, pl.BlockSpec((1, nck, Hd, Hd), lambda h, s: (h, rev(s), 0, 0)),
                  gain, tab, tab, head, head, head, cd, ANY],
        out_specs=[pl.BlockSpec((4, tseg, Hd), lambda h, s: (0, rev(s), h)), gain],
        out_shape=[jax.ShapeDtypeStruct(dproj.shape, BF16), jax.ShapeDtypeStruct((1, G * Hd), F32)],
        scratch_shapes=[pltpu.VMEM((Hd, Hd), F32)], aliased={14: 0},
        args=[proj, proj, proj, proj, db, o_pre, r_prev, ret_norm, cos2, sin2, intra, k_tail, q_head, chunk_decay, dproj])


def ffn_down_loss(s, w2, h, gain, target, name, hosted=()):
    T, F = s.shape
    D = h.shape[1]
    tm = min(T, 512)

    def body(s_ref, w2_ref, h_ref, g_ref, t_ref, dh_ref, loss_ref, dg_ref):
        @pl.when(pl.program_id(0) == 0)
        def _():
            loss_ref[...] = jnp.zeros_like(loss_ref)
            dg_ref[...] = jnp.zeros_like(dg_ref)

        hh = h_ref[...] + 0.5 * _dot(s_ref[...], w2_ref[...])
        gain_v = g_ref[...]
        err = hh * _rstd(hh) * gain_v - t_ref[...]
        loss_ref[...] += 0.5 * jnp.sum(jnp.mean(err * err, axis=-1, keepdims=True), axis=0, keepdims=True)
        dhn, dg = _rmsnorm_bwd(err * (1.0 / D), hh, gain_v)
        dh_ref[...] = dhn
        dg_ref[...] += jnp.sum(dg, axis=0, keepdims=True)

    row_spec = pl.BlockSpec((tm, D), lambda i: (i, 0))
    vec_spec = pl.BlockSpec((1, D), lambda i: (0, 0))
    return _call(
        body, hosted, name=name, grid=(T // tm,),
        in_specs=[pl.BlockSpec((tm, F), lambda i: (i, 0)), _resident((F, D)), row_spec, vec_spec, row_spec],
        out_specs=[row_spec, pl.BlockSpec((1, LANES), lambda i: (0, 0)), vec_spec],
        out_shape=[jax.ShapeDtypeStruct((T, D), F32), jax.ShapeDtypeStruct((1, LANES), F32), jax.ShapeDtypeStruct((1, D), F32)],
        args=[s, w2, h, gain, target])


def prereduce(grads, recvs, place, name):
    nt = len(grads)
    nsh, R, C = grads[0].shape
    rh = R // 2

    def body(place_ref, *refs):
        for t in range(nt):
            g_ref, r_ref, o_ref, own_ref = refs[2 * t], refs[2 * t + 1], refs[2 * nt + 2 * t], refs[2 * nt + 2 * t + 1]
            o_ref[...] = (g_ref[...].astype(F32) + r_ref[...].astype(F32)).astype(BF16)
            own_ref[0] = o_ref[place_ref[1]]

    outs = pl.pallas_call(
        body, name=name,
        grid_spec=pltpu.PrefetchScalarGridSpec(
            num_scalar_prefetch=1, grid=(1,),
            in_specs=[pl.BlockSpec((nsh, rh, C), lambda j, p: (0, p[0], 0)), pl.BlockSpec((nsh, rh, C), lambda j, p: (0, 0, 0))] * nt,
            out_specs=[pl.BlockSpec((nsh, rh, C), lambda j, p: (0, 0, 0)),
                       pl.BlockSpec((1, rh, C), lambda j, p: (p[1], p[0], 0))] * nt),
        out_shape=[jax.ShapeDtypeStruct((nsh, rh, C), BF16), jax.ShapeDtypeStruct((nsh, R, C), BF16)] * nt,
        compiler_params=pltpu.CompilerParams(vmem_limit_bytes=VMEM_LIMIT_V7X),
    )(place, *[a for pair in zip(grads, recvs) for a in pair])
    return [(outs[2 * t], outs[2 * t + 1]) for t in range(nt)]


def _adamw(w, g, m, v):
    m = ADAM_B1 * m + (1.0 - ADAM_B1) * g
    v = ADAM_B2 * v + (1.0 - ADAM_B2) * (g * g)
    m_hat = m / (1.0 - ADAM_B1 ** ADAM_STEP)
    v_hat = v / (1.0 - ADAM_B2 ** ADAM_STEP)
    return -ADAM_LR * (m_hat / (jnp.sqrt(v_hat) + ADAM_EPS) + ADAM_WD * w), m, v


def adamw_sharded(tensors, name, hosted=()):
    nt = len(tensors)
    nsh = tensors[0][0].shape[0]
    shapes = [t[0].shape[1:] for t in tensors]

    def fits(steps):
        if any(R % (steps * BF16_TILE_ROWS) for R, _ in shapes):
            return False
        return sum(2 * (R // steps) * -(-C // LANES) * LANES * (nsh * 2 + 7 * 4) for R, C in shapes) <= ADAMW_VMEM_BUDGET

    steps = min(s for s in range(1, min(R for R, _ in shapes) // BF16_TILE_ROWS + 1) if fits(s))

    def body(*refs):
        ins, outs = refs[:4 * nt], refs[4 * nt:]
        for t in range(nt):
            p_ref, w_ref, m_ref, v_ref = ins[4 * t:4 * t + 4]
            g_ref, d_ref, nm_ref, nv_ref = outs[4 * t:4 * t + 4]
            g = p_ref[0].astype(F32)
            for i in range(1, nsh):
                g += p_ref[i].astype(F32)
            g_ref[...] = g
            d_ref[...], nm_ref[...], nv_ref[...] = _adamw(w_ref[...], g, m_ref[...], v_ref[...])

    in_specs, out_specs, out_shape = [], [], []
    for R, C in shapes:
        spec = pl.BlockSpec((R // steps, C), lambda i: (i, 0))
        in_specs += [pl.BlockSpec((nsh, R // steps, C), lambda i: (0, i, 0)), spec, spec, spec]
        out_specs += [spec] * 4
        out_shape += [jax.ShapeDtypeStruct((R, C), F32)] * 4
    return _call(body, hosted, name=name, grid=(steps,), in_specs=in_specs, out_specs=out_specs, out_shape=out_shape,
                 args=[a for tensor in tensors for a in tensor])


def adamw_small(packs, params, loss_packs, name):
    n = len(packs)
    ndev = loss_packs.shape[0]

    def body(*refs):
        p_refs, loss_ref, wmv = refs[:n], refs[n], refs[n + 1:4 * n + 1]
        outs, loss_out = refs[4 * n + 1:8 * n + 1], refs[8 * n + 1]
        total = lambda r: sum((r[i] for i in range(1, ndev)), r[0])
        loss_out[...] = total(loss_ref)
        for k in range(n):
            g = total(p_refs[k])
            outs[4 * k][...] = g
            outs[4 * k + 1][...], outs[4 * k + 2][...], outs[4 * k + 3][...] = _adamw(
                wmv[3 * k][...], g, wmv[3 * k + 1][...], wmv[3 * k + 2][...])

    out_shape = [jax.ShapeDtypeStruct(p[0].shape, F32) for p in params for _ in range(4)]
    outs = pl.pallas_call(body, name=name, out_shape=out_shape + [jax.ShapeDtypeStruct(loss_packs.shape[1:], F32)],
                          compiler_params=pltpu.CompilerParams(vmem_limit_bytes=VMEM_LIMIT_V7X),
                          )(*packs, loss_packs, *[a for p in params for a in p])
    return [outs[4 * k:4 * k + 4] for k in range(n)], outs[4 * n]


BIG = ("ffn1_w1", "ffn1_w3", "ffn1_w2", "w_in", "w_out", "ffn2_w1", "ffn2_w3", "ffn2_w2")
TRANSPOSED = ("ffn1_w1", "ffn1_w3", "ffn2_w1", "ffn2_w3")
SMALL = ("pool_w", "mix_norm", "pool_scale", "ret_norm", "ffn2_norm", "final_norm", "ffn1_norm")
WEIGHTS = ("ffn1_norm", "ffn1_w1", "ffn1_w3", "ffn1_w2", "mix_norm", "w_in", "pool_w", "pool_scale", "ret_norm", "w_out",
           "ffn2_norm", "ffn2_w1", "ffn2_w3", "ffn2_w2", "final_norm")


def kernel(x, ffn1_norm, ffn1_w1, ffn1_w3, ffn1_w2, mix_norm, w_in, pool_w, pool_scale, ret_norm, w_out, ffn2_norm, ffn2_w1, ffn2_w3, ffn2_w2, final_norm, loss_target, m_ffn1_norm, m_ffn1_w1, m_ffn1_w3, m_ffn1_w2, m_mix_norm, m_w_in, m_pool_w, m_pool_scale, m_ret_norm, m_w_out, m_ffn2_norm, m_ffn2_w1, m_ffn2_w3, m_ffn2_w2, m_final_norm, v_ffn1_norm, v_ffn1_w1, v_ffn1_w3, v_ffn1_w2, v_mix_norm, v_w_in, v_pool_w, v_pool_scale, v_ret_norm, v_w_out, v_ffn2_norm, v_ffn2_w1, v_ffn2_w3, v_ffn2_w2, v_final_norm):
    w = dict(ffn1_norm=ffn1_norm, ffn1_w1=ffn1_w1, ffn1_w3=ffn1_w3, ffn1_w2=ffn1_w2, mix_norm=mix_norm, w_in=w_in, pool_w=pool_w,
             pool_scale=pool_scale, ret_norm=ret_norm, w_out=w_out, ffn2_norm=ffn2_norm, ffn2_w1=ffn2_w1, ffn2_w3=ffn2_w3,
             ffn2_w2=ffn2_w2, final_norm=final_norm)
    m = dict(ffn1_norm=m_ffn1_norm, ffn1_w1=m_ffn1_w1, ffn1_w3=m_ffn1_w3, ffn1_w2=m_ffn1_w2, mix_norm=m_mix_norm, w_in=m_w_in,
             pool_w=m_pool_w, pool_scale=m_pool_scale, ret_norm=m_ret_norm, w_out=m_w_out, ffn2_norm=m_ffn2_norm, ffn2_w1=m_ffn2_w1,
             ffn2_w3=m_ffn2_w3, ffn2_w2=m_ffn2_w2, final_norm=m_final_norm)
    v = dict(ffn1_norm=v_ffn1_norm, ffn1_w1=v_ffn1_w1, ffn1_w3=v_ffn1_w3, ffn1_w2=v_ffn1_w2, mix_norm=v_mix_norm, w_in=v_w_in,
             pool_w=v_pool_w, pool_scale=v_pool_scale, ret_norm=v_ret_norm, w_out=v_w_out, ffn2_norm=v_ffn2_norm, ffn2_w1=v_ffn2_w1,
             ffn2_w3=v_ffn2_w3, ffn2_w2=v_ffn2_w2, final_norm=v_final_norm)
    xs, target = x[0], loss_target[0]
    T = xs.shape[0]
    tables = _ret_tables(T)
    place = jnp.stack([lax.axis_index("c"), 2 * lax.axis_index("x") + lax.axis_index("y")]).astype(jnp.int32)
    local = lambda d, k: jnp.transpose(d[k][0]) if k in TRANSPOSED else d[k][0]
    result = lambda o, k: jnp.transpose(o)[None] if k in TRANSPOSED else o[None]
    first = ("ffn1_w1", "ffn1_w3")
    sh = {k: local(w, k).astype(BF16) for k in first}
    gather = lambda *names: [ChipExchange([sh[k] for k in names], False)]
    wg, grad, delta, new_m, new_v = {}, {}, {}, {}, {}

    def update(names, pieces, name, hosted=()):
        outs, extras = adamw_sharded([(p, local(w, k), local(m, k), local(v, k)) for k, p in zip(names, pieces)], name, hosted)
        for t, k in enumerate(names):
            grad[k], delta[k], new_m[k], new_v[k] = [result(o, k) for o in outs[4 * t:4 * t + 4]]
        return extras

    def reduce_in_chip(name, *pairs):
        reduced = prereduce([p for p, _ in pairs], [r for _, r in pairs], place, "prereduce_" + name)
        return reduced[0] if len(pairs) == 1 else reduced

    scatter = lambda *reduced: ChipExchange([r[0] for r in reduced], True, [r[1] for r in reduced])
    whole = lambda k: wg[k].reshape(-1, wg[k].shape[-1])
    sharded = lambda g: g.reshape(N_CHIPS, -1, g.shape[-1])

    later = [k for k in BIG if k not in first]
    casts, ((wg["ffn1_w1"], wg["ffn1_w3"]),) = cast_shards([local(w, k) for k in later], "cast_gather_ffn1", gather(*first))
    sh.update(zip(later, casts))
    (n1, ga1, gb1, s1), ((wg["ffn1_w2"], wg["w_in"]),) = ffn_up(
        xs, ffn1_norm, whole("ffn1_w1"), whole("ffn1_w3"), "ffn1_up", gather("ffn1_w2", "w_in"))
    (h1, u, proj), ((wg["w_out"], wg["ffn2_w1"]),) = ffn_down_mix_in(
        s1, whole("ffn1_w2"), xs, mix_norm, wg["w_in"], "ffn1_down_mix_in", gather("w_out", "ffn2_w1"))
    (pa,), _ = pool_fwd(proj, pool_w[0], pool_scale, "pool_fwd")
    (rb, o_pre, r_prev), ((wg["ffn2_w3"],),) = ret_fwd(proj, ret_norm, tables, "ret_fwd", gather("ffn2_w3"))
    (h2, n2, ga2, gb2, s2), ((wg["ffn2_w2"],),) = ffn_up(
        h1, ffn2_norm, whole("ffn2_w1"), whole("ffn2_w3"), "mix_out_ffn2_up", gather("ffn2_w2"), mixed=(pa, rb, wg["w_out"]))
    (dh3, loss, d_final), _ = ffn_down_loss(s2, whole("ffn2_w2"), h2, final_norm[None], target, "ffn2_down_loss")

    (da2, db2, df2), _ = ffn_bwd_act(dh3, whole("ffn2_w2"), ga2, gb2, "ffn2_bwd_act")
    (g_f2w2,), _ = ffn_dw([s2], df2, 1, "ffn2_dw2")
    g_f2w2 = sharded(g_f2w2)
    (g_f2w1, g_f2w3), ((r_f2w2,),) = ffn_dw([da2, db2], n2, 2, "ffn2_dw13", [SiblingExchange([g_f2w2])])
    g_f2w1, g_f2w3 = sharded(g_f2w1), sharded(g_f2w3)
    p_f2w2 = reduce_in_chip("ffn2_w2", (g_f2w2, r_f2w2))
    (dh2, d_ffn2), ((q_f2w2,), (r_f2w1, r_f2w3)) = ffn_bwd_in(
        da2, db2, whole("ffn2_w1"), whole("ffn2_w3"), h2, ffn2_norm, dh3, "ffn2_bwd_in",
        [scatter(p_f2w2), SiblingExchange([g_f2w1, g_f2w3])])
    p_f2w1, p_f2w3 = reduce_in_chip("ffn2_w13", (g_f2w1, r_f2w1), (g_f2w3, r_f2w3))
    (dpa, drb, g_wout), _ = mix_out_bwd(dh2, wg["w_out"], pa, rb, "mix_out_bwd")
    (dproj, d_pool_w, d_pool_scale), _ = pool_bwd(proj, dpa, pool_w[0], pool_scale, "pool_bwd")
    (dproj, d_ret_norm), ((q_f2w1, q_f2w3), (r_wout,)) = ret_bwd(
        proj, drb, o_pre, r_prev, ret_norm, tables, dproj, "ret_bwd", [scatter(p_f2w1, p_f2w3), SiblingExchange([g_wout])])
    p_wout = reduce_in_chip("w_out", (g_wout, r_wout))
    (g_win,), ((q_wout,),) = mix_dwin(u, dproj, N_CHIPS, "mix_dwin", [scatter(p_wout)])
    (dh1, d_mix), ((r_win,),) = mix_in_bwd(dproj, wg["w_in"], h1, mix_norm, dh2, "mix_in_bwd", [SiblingExchange([g_win])])
    p_win = reduce_in_chip("w_in", (g_win, r_win))
    (da1, db1, df1), ((q_win,),) = ffn_bwd_act(dh1, whole("ffn1_w2"), ga1, gb1, "ffn1_bwd_act", [scatter(p_win)])
    d_small = {"pool_w": d_pool_w.reshape(-1, LANES), "mix_norm": d_mix, "pool_scale": d_pool_scale, "ret_norm": d_ret_norm,
               "ffn2_norm": d_ffn2, "final_norm": d_final}
    (g_f1w1, g_f1w3), (packs,) = ffn_dw([da1, db1], n1, 2, "ffn1_dw13", [AllExchange([d_small[k] for k in SMALL[:-1]] + [loss])])
    g_f1w1, g_f1w3 = sharded(g_f1w1), sharded(g_f1w3)
    (g_f1w2,), ((r_f1w1, r_f1w3),) = ffn_dw([s1], df1, 1, "ffn1_dw2", [SiblingExchange([g_f1w1, g_f1w3])])
    g_f1w2 = sharded(g_f1w2)
    p_f1w1, p_f1w3 = reduce_in_chip("ffn1_w13", (g_f1w1, r_f1w1), (g_f1w3, r_f1w3))
    (dx, d_ffn1), ((q_f1w1, q_f1w3), (r_f1w2,)) = ffn_bwd_in(
        da1, db1, whole("ffn1_w1"), whole("ffn1_w3"), xs, ffn1_norm, dh1, "ffn1_bwd_in",
        [scatter(p_f1w1, p_f1w3), SiblingExchange([g_f1w2])])
    p_f1w2 = reduce_in_chip("ffn1_w2", (g_f1w2, r_f1w2))

    (q_f1w2,), (late,) = update(["w_in", "w_out", "ffn2_w2"], [q_win, q_wout, q_f2w2], "adamw_mix_w2",
                                [scatter(p_f1w2), AllExchange([d_ffn1])])
    update(["ffn2_w1", "ffn2_w3", "ffn1_w1", "ffn1_w3"], [q_f2w1, q_f2w3, q_f1w1, q_f1w3], "adamw_w13")
    update(["ffn1_w2"], [q_f1w2], "adamw_ffn1_w2")
    flat = lambda t, k: t[k].reshape(-1, LANES) if k == "pool_w" else t[k].reshape(1, -1)
    updated, loss_sum = adamw_small(packs[:-1] + [late], [[flat(t, k) for t in (w, m, v)] for k in SMALL], packs[-1], "adamw_small")
    for k, outs in zip(SMALL, updated):
        grad[k], delta[k], new_m[k], new_v[k] = [o.reshape(w[k].shape) for o in outs]
    loss = loss_sum[0, 0]

    return (loss, dx[None], *[grad[k] for k in WEIGHTS], *[delta[k] for k in WEIGHTS],
            *[new_m[k] for k in WEIGHTS], *[new_v[k] for k in WEIGHTS])
```

```python
import math

import jax
import jax.numpy as jnp
from jax import lax
from jax.experimental import pallas as pl
from jax.experimental.pallas import tpu as pltpu

F32 = jnp.float32
BF16 = jnp.bfloat16

EPS = 1e-6
LANES = 128
BF16_TILE_ROWS = 16
N_CHIPS = 4
N_GROUPS = 4
HEAD_DIM = 128
RET_CHUNK = 128
ROPE_BASE = 10000.0
ADAM_LR, ADAM_B1, ADAM_B2, ADAM_EPS, ADAM_WD, ADAM_STEP = 0.001, 0.9, 0.999, 1e-08, 0.01, 10
VMEM_LIMIT_V7X = 56 * 1024 * 1024
ADAMW_VMEM_BUDGET = 32 * 1024 * 1024
MESH = pl.DeviceIdType.MESH
ANY = pl.BlockSpec(memory_space=pl.ANY)


def _dot(a, b):
    return jnp.dot(a, b, preferred_element_type=F32)


def _dot_nt(a, b):
    return lax.dot_general(a, b, (((1,), (1,)), ((), ())), preferred_element_type=F32)


def _dot_tn(a, b):
    return lax.dot_general(a, b, (((0,), (0,)), ((), ())), preferred_element_type=F32)


def _rstd(h):
    return lax.rsqrt(jnp.mean(h * h, axis=-1, keepdims=True) + EPS)


def _rmsnorm_bwd(dn, h, gain):
    r = _rstd(h)
    nh = h * r
    dnh = dn * gain
    dh = r * (dnh - nh * jnp.mean(dnh * nh, axis=-1, keepdims=True))
    return dh, dn * nh


def _silu_parts(a):
    sig = jax.nn.sigmoid(a)
    silu = a * sig
    return silu, sig + silu * (1.0 - sig)


def _mesh_pos():
    return lax.axis_index("x"), lax.axis_index("y"), lax.axis_index("c")


class ChipExchange:
    def __init__(self, srcs, scatter, placed=()):
        n = len(srcs)
        self.inputs, self.scatter, self.n, self.reach = list(srcs) + list(placed), scatter, n, REACH_CHIPS
        self.aliases = {n + t: t for t in range(n)} if scatter else {}
        self.half_rows = [s.shape[1] if scatter else s.shape[0] // 2 for s in srcs]
        self.out_shape = [jax.ShapeDtypeStruct((N_CHIPS, 2 * rh, s.shape[-1]), s.dtype) for s, rh in zip(srcs, self.half_rows)]
        if scatter:
            self.out_shape += [jax.ShapeDtypeStruct((2, rh // 2, s.shape[-1]), s.dtype) for s, rh in zip(srcs, self.half_rows)]
        dma = pltpu.SemaphoreType.DMA
        self.sems = [dma((4 * n,)), dma((4 * n,)), dma((2 * n,)), dma((2 * n,)), dma((4 * n,)), dma((4 * n,))]

    def _copies(self, src, out, sems):
        hop1_send, hop1_recv, hop2_send, hop2_recv, d2d_send, d2d_recv = sems
        x, y, c = _mesh_pos()
        me, dg = 2 * x + y, 2 * (1 - x) + (1 - y)
        sibling = (x, y, 1 - c)
        n = self.n
        mine, theirs = c, 1 - c

        def nb(a):
            nx, ny = x ^ (1 - a), y ^ a
            return 2 * nx + ny, (nx, ny, c)

        def remote(s, d, send, recv, k, to):
            return pltpu.make_async_remote_copy(src_ref=s, dst_ref=d, send_sem=send.at[k], recv_sem=recv.at[k],
                                                device_id=to, device_id_type=MESH)

        class Copies:
            def slot(_, t, chip, half):
                rh = self.half_rows[t]
                return out[t].at[chip, pl.ds(half * rh, rh), :]

            def quarter(_, t, chip, q):
                qh = self.half_rows[t] // 2
                return out[t].at[chip, pl.ds(mine * 2 * qh + q * qh, qh), :]

            def own_shard(k, t):
                return remote(src[t], out[t].at[me], d2d_send, d2d_recv, 4 * t + 3, sibling)

            def hop1(k, t, a, transit=False):
                rh = self.half_rows[t]
                chip, to = nb(a)
                if transit:
                    piece = src[t].at[dg, pl.ds(a * (rh // 2), rh // 2), :]
                    return remote(piece, out[n + t].at[a], hop1_send, hop1_recv, 4 * t + 2 + a, to)
                piece = src[t].at[chip] if self.scatter else src[t].at[pl.ds(mine * rh, rh), :]
                return remote(piece, k.slot(t, me, mine), hop1_send, hop1_recv, 4 * t + a, to)

            def landed1(k, t, a, transit=False):
                here = out[n + t].at[a] if transit else k.slot(t, nb(a)[0], mine)
                return remote(here, here, hop1_send, hop1_recv, 4 * t + (2 if transit else 0) + a, sibling)

            def hop2(k, t, q):
                origin, to = nb(q)[0], nb(1 - q)[1]
                piece = out[n + t].at[q] if self.scatter else k.quarter(t, origin, q)
                return remote(piece, k.quarter(t, origin, q), hop2_send, hop2_recv, 2 * t + q, to)

            def landed2(k, t, q):
                here = k.quarter(t, dg, q)
                return remote(here, here, hop2_send, hop2_recv, 2 * t + q, sibling)

            def d2d(k, t, p, chip, own=False, arriving=False):
                if arriving:
                    there = k.slot(t, chip, theirs)
                    return remote(there, there, d2d_send, d2d_recv, 4 * t + p, sibling)
                piece = src[t].at[me] if own else k.slot(t, chip, mine)
                return remote(piece, k.slot(t, chip, mine), d2d_send, d2d_recv, 4 * t + p, sibling)

        return Copies(), nb, me, dg, c

    def start(self, src, out, sems):
        k, nb, me, dg, c = self._copies(src, out, sems)
        for t in range(self.n):
            for first in range(2):
                a = first ^ c
                k.hop1(t, a).start()
                if self.scatter:
                    k.hop1(t, a, transit=True).start()
            if self.scatter:
                k.d2d(t, 3, me, own=True).start()
            else:
                k.own_shard(t).start()

    def mid(self, src, out, sems):
        k, nb, me, dg, c = self._copies(src, out, sems)
        for t in range(self.n):
            for first in range(2):
                a = first ^ c
                if self.scatter:
                    k.landed1(t, a, transit=True).wait_recv()
                    k.hop2(t, a).start()
                k.landed1(t, a).wait_recv()
                if not self.scatter:
                    k.hop2(t, a).start()
                k.d2d(t, a, nb(a)[0]).start()

    def finish(self, src, out, sems):
        k, nb, me, dg, c = self._copies(src, out, sems)
        for t in range(self.n):
            for q in range(2):
                k.landed2(t, q).wait_recv()
            k.d2d(t, 2, dg).start()
        for t in range(self.n):
            for a in range(2):
                k.d2d(t, a, nb(a)[0], arriving=True).wait_recv()
            k.d2d(t, 2, dg, arriving=True).wait_recv()
            if self.scatter:
                k.d2d(t, 3, me, arriving=True).wait_recv()
        for t in range(self.n):
            for a in range(2):
                k.hop1(t, a).wait_send()
                if self.scatter:
                    k.hop1(t, a, transit=True).wait_send()
                k.hop2(t, a).wait_send()
                k.d2d(t, a, nb(a)[0]).wait_send()
            k.d2d(t, 2, dg).wait_send()
            if self.scatter:
                k.d2d(t, 3, me, own=True).wait_send()
            else:
                k.own_shard(t).wait()


class SiblingExchange:
    def __init__(self, grads):
        self.inputs, self.n, self.aliases, self.reach = list(grads), len(grads), {}, REACH_SIBLING
        self.half_rows = [g.shape[1] // 2 for g in grads]
        self.out_shape = [jax.ShapeDtypeStruct((g.shape[0], rh, g.shape[2]), g.dtype) for g, rh in zip(grads, self.half_rows)]
        self.sems = [pltpu.SemaphoreType.DMA((self.n,)), pltpu.SemaphoreType.DMA((self.n,))]

    def _plan(self, src, out, sems):
        x, y, c = _mesh_pos()
        return [pltpu.make_async_remote_copy(
            src_ref=src[t].at[:, pl.ds((1 - c) * self.half_rows[t], self.half_rows[t]), :], dst_ref=out[t],
            send_sem=sems[0].at[t], recv_sem=sems[1].at[t], device_id=(x, y, 1 - c), device_id_type=MESH) for t in range(self.n)]

    def start(self, src, out, sems):
        for cp in self._plan(src, out, sems):
            cp.start()

    def mid(self, src, out, sems):
        pass

    def finish(self, src, out, sems):
        for cp in self._plan(src, out, sems):
            cp.wait()


REACH_SIBLING, REACH_CHIPS, REACH_ALL = 0, 1, 2


def _entry_barrier(reach):
    x, y, c = _mesh_pos()
    peers = [(x, y, 1 - c)]
    if reach == REACH_CHIPS:
        peers += [(1 - x, y, c), (x, 1 - y, c)]
    elif reach == REACH_ALL:
        peers = [(x ^ dx, y ^ dy, c ^ dc) for dx in (0, 1) for dy in (0, 1) for dc in (0, 1)][1:]
    barrier = pltpu.get_barrier_semaphore()
    for peer in peers:
        pl.semaphore_signal(barrier, inc=1, device_id=peer, device_id_type=MESH)
    pl.semaphore_wait(barrier, len(peers))


def _call(body, hosted=(), *, name, in_specs, out_specs, out_shape, args, grid=(), scratch_shapes=(), aliased=None):
    n_in, n_out, n_scr = len(in_specs), len(out_specs), len(scratch_shapes)
    total = math.prod(grid)
    mid_step = max(0, (3 * total) // 4 - 1)

    def full(*refs):
        pos = [0]

        def take(k):
            pos[0] += k
            return refs[pos[0] - k:pos[0]]

        ins, h_in = take(n_in), [take(len(h.inputs)) for h in hosted]
        outs, h_out = take(n_out), [take(len(h.out_shape)) for h in hosted]
        scr, h_sem = take(n_scr), [take(len(h.sems)) for h in hosted]
        step = 0
        for axis, size in enumerate(grid):
            step = step * size + pl.program_id(axis)

        def phase(at, method):
            if not hosted:
                return

            def run():
                if method == "start":
                    _entry_barrier(reach)
                for h, s, o, m in zip(hosted, h_in, h_out, h_sem):
                    getattr(h, method)(s, o, m)

            if total == 1:
                run()
            else:
                pl.when(step == at)(run)

        phase(0, "start")
        body(*ins, *outs, *scr)
        phase(mid_step, "mid")
        phase(total - 1, "finish")

    aliases, i0, o0 = dict(aliased or {}), n_in, n_out
    for h in hosted:
        aliases.update({i0 + i: o0 + o for i, o in h.aliases.items()})
        i0, o0 = i0 + len(h.inputs), o0 + len(h.out_shape)
    reach = max((h.reach for h in hosted), default=None)
    params = dict(vmem_limit_bytes=VMEM_LIMIT_V7X)
    if hosted:
        params["collective_id"] = reach
    results = pl.pallas_call(
        full, name=name, grid=grid,
        in_specs=list(in_specs) + [ANY] * (i0 - n_in),
        out_specs=list(out_specs) + [ANY] * (o0 - n_out),
        out_shape=list(out_shape) + [s for h in hosted for s in h.out_shape],
        scratch_shapes=list(scratch_shapes) + [s for h in hosted for s in h.sems],
        input_output_aliases=aliases,
        compiler_params=pltpu.CompilerParams(**params),
    )(*args, *[s for h in hosted for s in h.inputs])
    outs, extras, pos = list(results[:n_out]), [], n_out
    for h in hosted:
        extras.append(list(results[pos:pos + h.n]))
        pos += len(h.out_shape)
    return outs, extras


def cast_shards(shards, name, hosted=()):
    n = len(shards)

    def body(*refs):
        for x_ref, o_ref in zip(refs[:n], refs[n:]):
            o_ref[...] = x_ref[...].astype(BF16)

    whole = lambda s: pl.BlockSpec(s.shape, lambda: (0,) * s.ndim)
    return _call(body, hosted, name=name, in_specs=[whole(s) for s in shards], out_specs=[whole(s) for s in shards],
                 out_shape=[jax.ShapeDtypeStruct(s.shape, BF16) for s in shards], args=list(shards))


class AllExchange:
    def __init__(self, arrays):
        n = len(arrays)
        self.inputs, self.n, self.aliases, self.reach = list(arrays), n, {}, REACH_ALL
        self.out_shape = [jax.ShapeDtypeStruct((2 * N_CHIPS,) + a.shape, a.dtype) for a in arrays]
        self.sems = [pltpu.SemaphoreType.DMA((n,)), pltpu.SemaphoreType.DMA((7 * n,)), pltpu.SemaphoreType.DMA((7 * n,))]

    def _copies(self, src, out, sems):
        local_sem, send_sem, recv_sem = sems
        x, y, c = _mesh_pos()
        me = 4 * x + 2 * y + c
        peers = [(x ^ dx, y ^ dy, c ^ dc) for dx in (0, 1) for dy in (0, 1) for dc in (0, 1)][1:]
        remote = lambda s, d, k, to: pltpu.make_async_remote_copy(
            src_ref=s, dst_ref=d, send_sem=send_sem.at[k], recv_sem=recv_sem.at[k], device_id=to, device_id_type=MESH)
        sends, landed, local = [], [], []
        for t in range(self.n):
            local.append(pltpu.make_async_copy(src[t], out[t].at[me], local_sem.at[t]))
            for p, (px, py, pc) in enumerate(peers):
                sends.append(remote(src[t], out[t].at[me], 7 * t + p, (px, py, pc)))
                here = out[t].at[4 * px + 2 * py + pc]
                landed.append(remote(here, here, 7 * t + p, (px, py, pc)))
        return sends, landed, local

    def start(self, src, out, sems):
        sends, _, local = self._copies(src, out, sems)
        for cp in sends + local:
            cp.start()

    def mid(self, src, out, sems):
        pass

    def finish(self, src, out, sems):
        sends, landed, local = self._copies(src, out, sems)
        for cp in landed:
            cp.wait_recv()
        for cp in sends:
            cp.wait_send()
        for cp in local:
            cp.wait()


MXU_COLS = 256


def _resident(shape):
    return pl.BlockSpec(shape, lambda *_: (0,) * len(shape), pipeline_mode=pl.Buffered(1))


def ffn_up(h, gain, w1, w3, name, hosted=(), mixed=None):
    T, D = h.shape
    F = w1.shape[0]
    tm = min(T, 256)

    def body(*refs):
        if mixed is None:
            h_ref, g_ref, w1_ref, w3_ref, n_ref, ga_ref, gb_ref, s_ref = refs
            hh = h_ref[...]
        else:
            pa_ref, rb_ref, wo_ref, h_ref, g_ref, w1_ref, w3_ref, hh_ref, n_ref, ga_ref, gb_ref, s_ref = refs
            hh = h_ref[...] + _dot(pa_ref[...], wo_ref[0]) + _dot(rb_ref[...], wo_ref[1])
            hh_ref[...] = hh
        n = (hh * _rstd(hh) * g_ref[...]).astype(BF16)
        n_ref[...] = n
        for c in range(0, F, MXU_COLS):
            cols = slice(c, c + MXU_COLS)
            a = _dot_nt(n, w1_ref[cols, :])
            b = _dot_nt(n, w3_ref[cols, :])
            silu, dsilu = _silu_parts(a)
            ga_ref[:, cols] = (b * dsilu).astype(BF16)
            gb_ref[:, cols] = silu.astype(BF16)
            s_ref[:, cols] = (silu * b).astype(BF16)

    act = jax.ShapeDtypeStruct((T, F), BF16)
    act_spec = pl.BlockSpec((tm, F), lambda i: (i, 0))
    row_spec = pl.BlockSpec((tm, D), lambda i: (i, 0))
    in_specs = [row_spec, pl.BlockSpec((1, D), lambda i: (0, 0)), _resident((F, D)), _resident((F, D))]
    out_specs, out_shape, args = [row_spec, act_spec, act_spec, act_spec], [jax.ShapeDtypeStruct((T, D), BF16), act, act, act], [h, gain, w1, w3]
    if mixed is not None:
        pa, rb, woutg = mixed
        W = pa.shape[1]
        in_specs = [pl.BlockSpec((tm, W), lambda i: (i, 0))] * 2 + [_resident((2, W, D))] + in_specs
        out_specs, out_shape = [row_spec] + out_specs, [jax.ShapeDtypeStruct((T, D), F32)] + out_shape
        args = [pa, rb, woutg.reshape(2, W, D)] + args
    return _call(body, hosted, name=name, grid=(T // tm,), in_specs=in_specs, out_specs=out_specs, out_shape=out_shape, args=args)


def ffn_bwd_act(dh, w2, ga, gb, name, hosted=()):
    T, D = dh.shape
    F = w2.shape[0]
    tm = min(T, 512)

    def body(dh_ref, w2_ref, ga_ref, gb_ref, da_ref, db_ref, df_ref):
        df = (0.5 * dh_ref[...]).astype(BF16)
        df_ref[...] = df
        for c in range(0, F, MXU_COLS):
            cols = slice(c, c + MXU_COLS)
            ds = _dot_nt(df, w2_ref[cols, :])
            da_ref[:, cols] = (ds * ga_ref[:, cols].astype(F32)).astype(BF16)
            db_ref[:, cols] = (ds * gb_ref[:, cols].astype(F32)).astype(BF16)

    act = jax.ShapeDtypeStruct((T, F), BF16)
    act_spec = pl.BlockSpec((tm, F), lambda i: (i, 0))
    row_spec = pl.BlockSpec((tm, D), lambda i: (i, 0))
    return _call(
        body, hosted, name=name, grid=(T // tm,),
        in_specs=[row_spec, _resident((F, D)), act_spec, act_spec],
        out_specs=[act_spec, act_spec, row_spec],
        out_shape=[act, act, jax.ShapeDtypeStruct((T, D), BF16)],
        args=[dh, w2, ga, gb])


def ffn_dw(xs, y, halves, name, hosted=()):
    T, F = xs[0].shape
    D = y.shape[1]
    nx, fh = len(xs), F // halves
    tk = min(T, 512)
    nk = T // tk

    def body(*refs):
        y_ref, x_refs, o_refs, accs = refs[0], refs[1:1 + nx], refs[1 + nx:1 + 2 * nx], refs[1 + 2 * nx:]
        k = pl.program_id(1)

        @pl.when(k == 0)
        def _():
            for acc in accs:
                acc[...] = jnp.zeros_like(acc)

        yy = y_ref[...]
        for x_ref, acc in zip(x_refs, accs):
            acc[...] += _dot_tn(x_ref[...], yy)

        @pl.when(k == nk - 1)
        def _():
            for o_ref, acc in zip(o_refs, accs):
                o_ref[...] = acc[...].astype(BF16)

    out = jax.ShapeDtypeStruct((F, D), BF16)
    return _call(
        body, hosted, name=name, grid=(halves, nk),
        in_specs=[pl.BlockSpec((tk, D), lambda j, k: (k, 0))] + [pl.BlockSpec((tk, fh), lambda j, k: (k, j))] * nx,
        out_specs=[pl.BlockSpec((fh, D), lambda j, k: (j, 0))] * nx,
        out_shape=[out] * nx,
        scratch_shapes=[pltpu.VMEM((fh, D), F32)] * nx,
        args=[y] + list(xs))


def ffn_bwd_in(da, db, w1, w3, h, gain, dh, name, hosted=()):
    T, F = da.shape
    D = h.shape[1]
    tm = min(T, 256)

    def body(da_ref, db_ref, w1_ref, w3_ref, h_ref, g_ref, dh_ref, o_ref, dg_ref):
        dn = _dot(da_ref[...], w1_ref[...]) + _dot(db_ref[...], w3_ref[...])
        dhn, dg = _rmsnorm_bwd(dn, h_ref[...], g_ref[...])
        o_ref[...] = dh_ref[...] + dhn

        @pl.when(pl.program_id(0) == 0)
        def _():
            dg_ref[...] = jnp.zeros_like(dg_ref)

        dg_ref[...] += jnp.sum(dg, axis=0, keepdims=True)

    act_spec = pl.BlockSpec((tm, F), lambda i: (i, 0))
    row_spec = pl.BlockSpec((tm, D), lambda i: (i, 0))
    vec_spec = pl.BlockSpec((1, D), lambda i: (0, 0))
    return _call(
        body, hosted, name=name, grid=(T // tm,),
        in_specs=[act_spec, act_spec, _resident((F, D)), _resident((F, D)), row_spec, vec_spec, row_spec],
        out_specs=[row_spec, vec_spec],
        out_shape=[jax.ShapeDtypeStruct((T, D), F32), jax.ShapeDtypeStruct((1, D), F32)],
        args=[da, db, w1, w3, h, gain, dh])


def ffn_down_mix_in(s, w2, h, gain, wing, name, hosted=()):
    T, F = s.shape
    D = h.shape[1]
    nsh, _, Cs = wing.shape
    tm = min(T, 512)

    def body(s_ref, w2_ref, h_ref, g_ref, w_ref, hh_ref, u_ref, p_ref):
        hh = h_ref[...] + 0.5 * _dot(s_ref[...], w2_ref[...])
        hh_ref[...] = hh
        u = (hh * _rstd(hh) * g_ref[...]).astype(BF16)
        u_ref[...] = u
        for j in range(nsh):
            p_ref[:, j * Cs:(j + 1) * Cs] = _dot(u, w_ref[j])

    row_spec = pl.BlockSpec((tm, D), lambda i: (i, 0))
    return _call(
        body, hosted, name=name, grid=(T // tm,),
        in_specs=[pl.BlockSpec((tm, F), lambda i: (i, 0)), _resident((F, D)), row_spec, pl.BlockSpec((1, D), lambda i: (0, 0)),
                  _resident((nsh, D, Cs))],
        out_specs=[row_spec, row_spec, pl.BlockSpec((tm, nsh * Cs), lambda i: (i, 0))],
        out_shape=[jax.ShapeDtypeStruct((T, D), F32), jax.ShapeDtypeStruct((T, D), BF16), jax.ShapeDtypeStruct((T, nsh * Cs), F32)],
        args=[s, w2, h, gain, wing])


def mix_out_bwd(dh, woutg, a, b, name, hosted=()):
    T, D = dh.shape
    W = a.shape[1]
    nsh, Rs, _ = woutg.shape
    wout = woutg.reshape(2, W, D)
    tk = min(T, 512)
    nk = T // tk

    def body(dh_ref, w_ref, a_ref, b_ref, da_ref, db_ref, dw_ref, acc):
        k = pl.program_id(0)

        @pl.when(k == 0)
        def _():
            acc[...] = jnp.zeros_like(acc)

        dhb = dh_ref[...].astype(BF16)
        da_ref[...] = _dot_nt(dhb, w_ref[0])
        db_ref[...] = _dot_nt(dhb, w_ref[1])
        acc[0:W, :] += _dot_tn(a_ref[...], dhb)
        acc[W:2 * W, :] += _dot_tn(b_ref[...], dhb)

        @pl.when(k == nk - 1)
        def _():
            for j in range(nsh):
                dw_ref[j] = acc[j * Rs:(j + 1) * Rs, :].astype(BF16)

    return _call(
        body, hosted, name=name, grid=(nk,),
        in_specs=[pl.BlockSpec((tk, D), lambda k: (k, 0)), pl.BlockSpec((2, W, D), lambda k: (0, 0, 0)),
                  pl.BlockSpec((tk, W), lambda k: (k, 0)), pl.BlockSpec((tk, W), lambda k: (k, 0))],
        out_specs=[pl.BlockSpec((tk, W), lambda k: (k, 0)), pl.BlockSpec((tk, W), lambda k: (k, 0)),
                   pl.BlockSpec((nsh, Rs, D), lambda k: (0, 0, 0))],
        out_shape=[jax.ShapeDtypeStruct((T, W), F32), jax.ShapeDtypeStruct((T, W), F32),
                   jax.ShapeDtypeStruct((nsh, Rs, D), BF16)],
        scratch_shapes=[pltpu.VMEM((2 * W, D), F32)],
        args=[dh, wout, a, b])


def _dproj_block(g):
    return (g // N_GROUPS + N_GROUPS) % (N_GROUPS + 1), g % N_GROUPS


def mix_dwin(u, dproj, nsh, name, hosted=()):
    T, D = u.shape
    Hd = HEAD_DIM
    slabs, _, width = dproj.shape
    blocks = slabs * width // Hd
    Cs = blocks * Hd // nsh
    tk = min(T, 512)
    nk = T // tk

    def body(u_ref, d_ref, o_ref, acc):
        k = pl.program_id(0)

        @pl.when(k == 0)
        def _():
            acc[...] = jnp.zeros_like(acc)

        where = [_dproj_block(g) for g in range(blocks)]
        d = jnp.concatenate([d_ref[slab, :, col * Hd:(col + 1) * Hd] for slab, col in where], axis=1)
        acc[...] += _dot_tn(u_ref[...], d)

        @pl.when(k == nk - 1)
        def _():
            for j in range(nsh):
                o_ref[j] = acc[:, j * Cs:(j + 1) * Cs].astype(BF16)

    return _call(
        body, hosted, name=name, grid=(nk,),
        in_specs=[pl.BlockSpec((tk, D), lambda k: (k, 0)), pl.BlockSpec((slabs, tk, width), lambda k: (0, k, 0))],
        out_specs=[pl.BlockSpec((nsh, D, Cs), lambda k: (0, 0, 0))],
        out_shape=[jax.ShapeDtypeStruct((nsh, D, Cs), BF16)],
        scratch_shapes=[pltpu.VMEM((D, blocks * Hd), F32)],
        args=[u, dproj])


def mix_in_bwd(dproj, wing, h, gain, dh, name, hosted=()):
    T, D = h.shape
    nsh, _, Cs = wing.shape
    Hd = HEAD_DIM
    per = Cs // Hd
    tm = min(T, 512)

    def body(d_ref, w_ref, h_ref, g_ref, dh_ref, o_ref, dg_ref):
        def shard(j):
            blocks = [_dproj_block(per * j + i) for i in range(per)]
            return jnp.concatenate([d_ref[slab, :, col * Hd:(col + 1) * Hd] for slab, col in blocks], axis=1)

        du = _dot_nt(shard(0), w_ref[0])
        for j in range(1, nsh):
            du += _dot_nt(shard(j), w_ref[j])
        dhn, dg = _rmsnorm_bwd(du, h_ref[...], g_ref[...])
        o_ref[...] = dh_ref[...] + dhn

        @pl.when(pl.program_id(0) == 0)
        def _():
            dg_ref[...] = jnp.zeros_like(dg_ref)

        dg_ref[...] += jnp.sum(dg, axis=0, keepdims=True)

    row_spec = pl.BlockSpec((tm, D), lambda i: (i, 0))
    vec_spec = pl.BlockSpec((1, D), lambda i: (0, 0))
    return _call(
        body, hosted, name=name, grid=(T // tm,),
        in_specs=[pl.BlockSpec((dproj.shape[0], tm, dproj.shape[2]), lambda i: (0, i, 0)),
                  pl.BlockSpec((nsh, D, Cs), lambda i: (0, 0, 0)), row_spec, vec_spec, row_spec],
        out_specs=[row_spec, vec_spec],
        out_shape=[jax.ShapeDtypeStruct((T, D), F32), jax.ShapeDtypeStruct((1, D), F32)],
        args=[dproj, wing, h, gain, dh])


POOL_WINDOWS = (2, 4, 8, 16)


def _pool_window(x, window, T, trailing):
    rows = lax.broadcasted_iota(jnp.int32, x.shape, 0)
    s, k = x, 1
    while k < window:
        if trailing:
            s = s + jnp.where(rows >= k, pltpu.roll(s, k, 0), 0.0)
        else:
            s = s + jnp.where(rows < T - k, pltpu.roll(s, T - k, 0), 0.0)
        k *= 2
    return s


def _pool_count(window, shape):
    rows = lax.broadcasted_iota(jnp.int32, shape, 0)
    return jnp.minimum(rows + 1, window).astype(F32)


def _per_group(work):
    for group, window in enumerate(POOL_WINDOWS):
        pl.when(pl.program_id(0) == group)(lambda window=window: work(window))


def pool_fwd(proj, pool_w, pool_scale, name, hosted=()):
    T = proj.shape[0]
    Hd = HEAD_DIM

    def body(x_ref, w_ref, sc_ref, a_ref):
        def work(window):
            x = x_ref[...]
            pooled = _pool_window(x, window, T, True) / _pool_count(window, x.shape) - x
            a_ref[...] = (_dot(pooled.astype(BF16), w_ref[0].astype(BF16)) * sc_ref[...]).astype(BF16)

        _per_group(work)

    return _call(
        body, hosted, name=name, grid=(N_GROUPS,),
        in_specs=[pl.BlockSpec((T, Hd), lambda g: (0, g)), pl.BlockSpec((1, Hd, Hd), lambda g: (g, 0, 0)),
                  pl.BlockSpec((1, Hd), lambda g: (0, g))],
        out_specs=[pl.BlockSpec((T, Hd), lambda g: (0, g))],
        out_shape=[jax.ShapeDtypeStruct((T, N_GROUPS * Hd), BF16)],
        args=[proj, pool_w, pool_scale])


def pool_bwd(proj, da, pool_w, pool_scale, name, hosted=()):
    T = proj.shape[0]
    Hd = HEAD_DIM

    def body(x_ref, da_ref, w_ref, sc_ref, dx_ref, dw_ref, dsc_ref):
        def work(window):
            x = x_ref[...]
            cnt = _pool_count(window, x.shape)
            pooled = (_pool_window(x, window, T, True) / cnt - x).astype(BF16)
            wb = w_ref[0].astype(BF16)
            dav = da_ref[...]
            dsc_ref[...] = jnp.sum(dav * _dot(pooled, wb), axis=0, keepdims=True)
            dout = (dav * sc_ref[...]).astype(BF16)
            dw_ref[0] = _dot_tn(pooled, dout)
            dpooled = _dot_nt(dout, wb)
            dx_ref[0] = (_pool_window(dpooled / cnt, window, T, False) - dpooled).astype(BF16)

        _per_group(work)

    col_spec = pl.BlockSpec((T, Hd), lambda g: (0, g))
    return _call(
        body, hosted, name=name, grid=(N_GROUPS,),
        in_specs=[col_spec, col_spec, pl.BlockSpec((1, Hd, Hd), lambda g: (g, 0, 0)), pl.BlockSpec((1, Hd), lambda g: (0, g))],
        out_specs=[pl.BlockSpec((1, T, Hd), lambda g: (N_GROUPS, 0, g)), pl.BlockSpec((1, Hd, Hd), lambda g: (g, 0, 0)),
                   pl.BlockSpec((1, Hd), lambda g: (0, g))],
        out_shape=[jax.ShapeDtypeStruct((N_GROUPS + 1, T, N_GROUPS * Hd), BF16), jax.ShapeDtypeStruct((N_GROUPS, Hd, Hd), F32),
                   jax.ShapeDtypeStruct((1, N_GROUPS * Hd), F32)],
        args=[proj, da, pool_w, pool_scale])


def _ret_tables(T):
    Hd, C = HEAD_DIM, RET_CHUNK
    inv_freq = 1.0 / (ROPE_BASE ** (jnp.arange(0, Hd, 2, dtype=F32) / Hd))
    ang = jnp.arange(T, dtype=F32)[:, None] * inv_freq[None, :]
    cos, sin = jnp.cos(ang), jnp.sin(ang)
    cos2 = jnp.concatenate([cos, cos], axis=-1)
    sin2 = jnp.concatenate([-sin, sin], axis=-1)
    log_gamma = jnp.log1p(-jnp.exp2(-5.0 - jnp.arange(N_GROUPS, dtype=F32)))
    pos = jnp.arange(C, dtype=F32)
    rel = pos[:, None] - pos[None, :]
    intra = jnp.where(rel[None] >= 0, jnp.exp(log_gamma[:, None, None] * jnp.maximum(rel, 0.0)[None]), 0.0)
    k_tail = jnp.exp(log_gamma[:, None] * (C - 1 - pos)[None, :])
    q_head = jnp.exp(log_gamma[:, None] * (pos + 1.0)[None, :])
    chunk_decay = jnp.exp(log_gamma * C)
    wide = lambda t: jnp.broadcast_to(t[:, :, None], (N_GROUPS, C, Hd))
    return cos2, sin2, intra, wide(k_tail), wide(q_head), jnp.broadcast_to(chunk_decay[:, None, None], (N_GROUPS, 1, Hd))


def _rope(x, cos2, sin2):
    return x * cos2 + pltpu.roll(x, HEAD_DIM // 2, 1) * sin2


def _rope_t(d, cos2, sin2):
    return d * cos2 + pltpu.roll(d * sin2, HEAD_DIM // 2, 1)


def _ret_specs(T, tseg, seg_of):
    Hd, G = HEAD_DIM, N_GROUPS
    col = lambda kind: pl.BlockSpec((tseg, Hd), lambda h, s: (seg_of(s), G * kind + h))
    tab = pl.BlockSpec((T, Hd), lambda h, s: (0, 0))
    head = pl.BlockSpec((1, RET_CHUNK, Hd), lambda h, s: (h, 0, 0))
    cd = pl.BlockSpec((1, 1, Hd), lambda h, s: (h, 0, 0))
    gain = pl.BlockSpec((1, Hd), lambda h, s: (0, h))
    return col, tab, head, cd, gain


def ret_fwd(proj, ret_norm, tables, name, hosted=()):
    T = proj.shape[0]
    Hd, C, G = HEAD_DIM, RET_CHUNK, N_GROUPS
    tseg = min(T, 2048)
    nseg, nck = T // tseg, tseg // C
    scale = Hd ** -0.5
    cos2, sin2, intra, k_tail, q_head, chunk_decay = tables

    def body(q_ref, k_ref, v_ref, g_ref, gain_ref, cos_ref, sin_ref, m_ref, kt_ref, qh_ref, cd_ref,
             b_ref, o_ref, rp_ref, state):
        @pl.when(pl.program_id(1) == 0)
        def _():
            state[...] = jnp.zeros_like(state)

        def chunk(ci, carry):
            rows = pl.ds(pl.multiple_of(ci * C, C), C)
            at = pl.ds(pl.multiple_of(pl.program_id(1) * tseg + ci * C, C), C)
            cos, sin = cos_ref[at, :], sin_ref[at, :]
            qr = _rope(q_ref[rows, :], cos, sin)
            kr = _rope(k_ref[rows, :], cos, sin) * scale
            qb, kb, vb = qr.astype(BF16), kr.astype(BF16), v_ref[rows, :].astype(BF16)
            r = state[...]
            rp_ref[0, ci] = r.astype(BF16)
            sc = _dot_nt(qb, kb) * m_ref[0]
            o = _dot(sc.astype(BF16), vb) + _dot((qr * qh_ref[0]).astype(BF16), r.astype(BF16))
            state[...] = cd_ref[0] * r + _dot_tn((kr * kt_ref[0]).astype(BF16), vb)
            o_ref[rows, :] = o
            on = o * _rstd(o)
            b_ref[rows, :] = (jax.nn.silu(g_ref[rows, :]) * (on * gain_ref[...])).astype(BF16)
            return carry

        lax.fori_loop(0, nck, chunk, 0, unroll=True)

    col, tab, head, cd, gain = _ret_specs(T, tseg, lambda s: s)
    out_col = pl.BlockSpec((tseg, Hd), lambda h, s: (s, h))
    return _call(
        body, hosted, name=name, grid=(G, nseg),
        in_specs=[col(1), col(2), col(3), col(4), gain, tab, tab, head, head, head, cd],
        out_specs=[out_col, out_col, pl.BlockSpec((1, nck, Hd, Hd), lambda h, s: (h, s, 0, 0))],
        out_shape=[jax.ShapeDtypeStruct((T, G * Hd), BF16), jax.ShapeDtypeStruct((T, G * Hd), F32),
                   jax.ShapeDtypeStruct((G, T // C, Hd, Hd), BF16)],
        scratch_shapes=[pltpu.VMEM((Hd, Hd), F32)],
        args=[proj, proj, proj, proj, ret_norm, cos2, sin2, intra, k_tail, q_head, chunk_decay])


def ret_bwd(proj, db, o_pre, r_prev, ret_norm, tables, dproj, name, hosted=()):
    T = proj.shape[0]
    Hd, C, G = HEAD_DIM, RET_CHUNK, N_GROUPS
    tseg = min(T, 2048)
    nseg, nck = T // tseg, tseg // C
    scale = Hd ** -0.5
    cos2, sin2, intra, k_tail, q_head, chunk_decay = tables

    def body(q_ref, k_ref, v_ref, g_ref, db_ref, o_ref, rp_ref, gain_ref, cos_ref, sin_ref, m_ref, kt_ref, qh_ref, cd_ref,
             _, d_ref, dgain_ref, gstate):
        @pl.when(pl.program_id(1) == 0)
        def _():
            gstate[...] = jnp.zeros_like(gstate)
            dgain_ref[...] = jnp.zeros_like(dgain_ref)

        def chunk(t, carry):
            ci = nck - 1 - t
            rows = pl.ds(pl.multiple_of(ci * C, C), C)
            at = pl.ds(pl.multiple_of((nseg - 1 - pl.program_id(1)) * tseg + ci * C, C), C)
            cos, sin = cos_ref[at, :], sin_ref[at, :]
            qr = _rope(q_ref[rows, :], cos, sin)
            kr = _rope(k_ref[rows, :], cos, sin) * scale
            qb, kb, vb = qr.astype(BF16), kr.astype(BF16), v_ref[rows, :].astype(BF16)
            qhb, ktb = (qr * qh_ref[0]).astype(BF16), (kr * kt_ref[0]).astype(BF16)
            sc = (_dot_nt(qb, kb) * m_ref[0]).astype(BF16)
            o = o_ref[rows, :]
            rstd = _rstd(o)
            on = o * rstd
            gain = gain_ref[...]
            silu, dsilu = _silu_parts(g_ref[rows, :])
            dy = db_ref[rows, :]
            dgain_ref[...] += jnp.sum(dy * silu * on, axis=0, keepdims=True)
            dg = dy * on * gain * dsilu
            don = dy * silu * gain
            dob = (rstd * (don - on * jnp.mean(don * on, axis=-1, keepdims=True))).astype(BF16)
            gn = gstate[...]
            gb = gn.astype(BF16)
            da = (_dot_nt(dob, vb) * m_ref[0]).astype(BF16)
            dq = _dot(da, kb) + _dot_nt(dob, rp_ref[0, ci]) * qh_ref[0]
            dk = _dot_tn(da, qb) + _dot_nt(vb, gb) * kt_ref[0]
            dv = _dot_tn(sc, dob) + _dot(ktb, gb)
            gstate[...] = cd_ref[0] * gn + _dot_tn(qhb, dob)
            d_ref[0, rows, :] = _rope_t(dq, cos, sin).astype(BF16)
            d_ref[1, rows, :] = _rope_t(dk * scale, cos, sin).astype(BF16)
            d_ref[2, rows, :] = dv.astype(BF16)
            d_ref[3, rows, :] = dg.astype(BF16)
            return carry

        lax.fori_loop(0, nck, chunk, 0, unroll=True)

    rev = lambda s: nseg - 1 - s
    col, tab, head, cd, gain = _ret_specs(T, tseg, rev)
    act = pl.BlockSpec((tseg, Hd), lambda h, s: (rev(s), h))
    return _call(
        body, hosted, name=name, grid=(G, nseg),
        in_specs=[col(1), col(2), col(3), col(4), act, act, pl.BlockSpec((1, nck, Hd, Hd), lambda h, s: (h, rev(s), 0, 0)),
                  gain, tab, tab, head, head, head, cd, ANY],
        out_specs=[pl.BlockSpec((4, tseg, Hd), lambda h, s: (0, rev(s), h)), gain],
        out_shape=[jax.ShapeDtypeStruct(dproj.shape, BF16), jax.ShapeDtypeStruct((1, G * Hd), F32)],
        scratch_shapes=[pltpu.VMEM((Hd, Hd), F32)], aliased={14: 0},
        args=[proj, proj, proj, proj, db, o_pre, r_prev, ret_norm, cos2, sin2, intra, k_tail, q_head, chunk_decay, dproj])


def ffn_down_loss(s, w2, h, gain, target, name, hosted=()):
    T, F = s.shape
    D = h.shape[1]
    tm = min(T, 512)

    def body(s_ref, w2_ref, h_ref, g_ref, t_ref, dh_ref, loss_ref, dg_ref):
        @pl.when(pl.program_id(0) == 0)
        def _():
            loss_ref[...] = jnp.zeros_like(loss_ref)
            dg_ref[...] = jnp.zeros_like(dg_ref)

        hh = h_ref[...] + 0.5 * _dot(s_ref[...], w2_ref[...])
        gain_v = g_ref[...]
        err = hh * _rstd(hh) * gain_v - t_ref[...]
        loss_ref[...] += 0.5 * jnp.sum(jnp.mean(err * err, axis=-1, keepdims=True), axis=0, keepdims=True)
        dhn, dg = _rmsnorm_bwd(err * (1.0 / D), hh, gain_v)
        dh_ref[...] = dhn
        dg_ref[...] += jnp.sum(dg, axis=0, keepdims=True)

    row_spec = pl.BlockSpec((tm, D), lambda i: (i, 0))
    vec_spec = pl.BlockSpec((1, D), lambda i: (0, 0))
    return _call(
        body, hosted, name=name, grid=(T // tm,),
        in_specs=[pl.BlockSpec((tm, F), lambda i: (i, 0)), _resident((F, D)), row_spec, vec_spec, row_spec],
        out_specs=[row_spec, pl.BlockSpec((1, LANES), lambda i: (0, 0)), vec_spec],
        out_shape=[jax.ShapeDtypeStruct((T, D), F32), jax.ShapeDtypeStruct((1, LANES), F32), jax.ShapeDtypeStruct((1, D), F32)],
        args=[s, w2, h, gain, target])


def prereduce(grads, recvs, place, name):
    nt = len(grads)
    nsh, R, C = grads[0].shape
    rh = R // 2

    def body(place_ref, *refs):
        for t in range(nt):
            g_ref, r_ref, o_ref, own_ref = refs[2 * t], refs[2 * t + 1], refs[2 * nt + 2 * t], refs[2 * nt + 2 * t + 1]
            piece = (g_ref[...].astype(F32) + r_ref[...].astype(F32)).astype(BF16)
            o_ref[...] = piece

            @pl.when(pl.program_id(0) == place_ref[1])
            def _():
                own_ref[...] = piece

    outs = pl.pallas_call(
        body, name=name,
        grid_spec=pltpu.PrefetchScalarGridSpec(
            num_scalar_prefetch=1, grid=(nsh,),
            in_specs=[pl.BlockSpec((1, rh, C), lambda j, p: (j, p[0], 0)), pl.BlockSpec((1, rh, C), lambda j, p: (j, 0, 0))] * nt,
            out_specs=[pl.BlockSpec((1, rh, C), lambda j, p: (j, 0, 0)),
                       pl.BlockSpec((1, rh, C), lambda j, p: (p[1], p[0], 0))] * nt),
        out_shape=[jax.ShapeDtypeStruct((nsh, rh, C), BF16), jax.ShapeDtypeStruct((nsh, R, C), BF16)] * nt,
        compiler_params=pltpu.CompilerParams(vmem_limit_bytes=VMEM_LIMIT_V7X),
    )(place, *[a for pair in zip(grads, recvs) for a in pair])
    return [(outs[2 * t], outs[2 * t + 1]) for t in range(nt)]


def _adamw(w, g, m, v):
    m = ADAM_B1 * m + (1.0 - ADAM_B1) * g
    v = ADAM_B2 * v + (1.0 - ADAM_B2) * (g * g)
    m_hat = m / (1.0 - ADAM_B1 ** ADAM_STEP)
    v_hat = v / (1.0 - ADAM_B2 ** ADAM_STEP)
    return -ADAM_LR * (m_hat / (jnp.sqrt(v_hat) + ADAM_EPS) + ADAM_WD * w), m, v


def adamw_sharded(tensors, name, hosted=()):
    nt = len(tensors)
    nsh = tensors[0][0].shape[0]
    shapes = [t[0].shape[1:] for t in tensors]

    def fits(steps):
        if any(R % (steps * BF16_TILE_ROWS) for R, _ in shapes):
            return False
        return sum(2 * (R // steps) * -(-C // LANES) * LANES * (nsh * 2 + 7 * 4) for R, C in shapes) <= ADAMW_VMEM_BUDGET

    steps = min(s for s in range(1, min(R for R, _ in shapes) // BF16_TILE_ROWS + 1) if fits(s))

    def body(*refs):
        ins, outs = refs[:4 * nt], refs[4 * nt:]
        for t in range(nt):
            p_ref, w_ref, m_ref, v_ref = ins[4 * t:4 * t + 4]
            g_ref, d_ref, nm_ref, nv_ref = outs[4 * t:4 * t + 4]
            g = p_ref[0].astype(F32)
            for i in range(1, nsh):
                g += p_ref[i].astype(F32)
            g_ref[...] = g
            d_ref[...], nm_ref[...], nv_ref[...] = _adamw(w_ref[...], g, m_ref[...], v_ref[...])

    in_specs, out_specs, out_shape = [], [], []
    for R, C in shapes:
        spec = pl.BlockSpec((R // steps, C), lambda i: (i, 0))
        in_specs += [pl.BlockSpec((nsh, R // steps, C), lambda i: (0, i, 0)), spec, spec, spec]
        out_specs += [spec] * 4
        out_shape += [jax.ShapeDtypeStruct((R, C), F32)] * 4
    return _call(body, hosted, name=name, grid=(steps,), in_specs=in_specs, out_specs=out_specs, out_shape=out_shape,
                 args=[a for tensor in tensors for a in tensor])


def adamw_small(packs, params, loss_packs, name):
    n = len(packs)
    ndev = loss_packs.shape[0]

    def body(*refs):
        p_refs, loss_ref, wmv = refs[:n], refs[n], refs[n + 1:4 * n + 1]
        outs, loss_out = refs[4 * n + 1:8 * n + 1], refs[8 * n + 1]
        total = lambda r: sum((r[i] for i in range(1, ndev)), r[0])
        loss_out[...] = total(loss_ref)
        for k in range(n):
            g = total(p_refs[k])
            outs[4 * k][...] = g
            outs[4 * k + 1][...], outs[4 * k + 2][...], outs[4 * k + 3][...] = _adamw(
                wmv[3 * k][...], g, wmv[3 * k + 1][...], wmv[3 * k + 2][...])

    out_shape = [jax.ShapeDtypeStruct(p[0].shape, F32) for p in params for _ in range(4)]
    outs = pl.pallas_call(body, name=name, out_shape=out_shape + [jax.ShapeDtypeStruct(loss_packs.shape[1:], F32)],
                          compiler_params=pltpu.CompilerParams(vmem_limit_bytes=VMEM_LIMIT_V7X),
                          )(*packs, loss_packs, *[a for p in params for a in p])
    return [outs[4 * k:4 * k + 4] for k in range(n)], outs[4 * n]


BIG = ("ffn1_w1", "ffn1_w3", "ffn1_w2", "w_in", "w_out", "ffn2_w1", "ffn2_w3", "ffn2_w2")
TRANSPOSED = ("ffn1_w1", "ffn1_w3", "ffn2_w1", "ffn2_w3")
SMALL = ("pool_w", "mix_norm", "pool_scale", "ret_norm", "ffn2_norm", "final_norm", "ffn1_norm")
WEIGHTS = ("ffn1_norm", "ffn1_w1", "ffn1_w3", "ffn1_w2", "mix_norm", "w_in", "pool_w", "pool_scale", "ret_norm", "w_out",
           "ffn2_norm", "ffn2_w1", "ffn2_w3", "ffn2_w2", "final_norm")


def kernel(x, ffn1_norm, ffn1_w1, ffn1_w3, ffn1_w2, mix_norm, w_in, pool_w, pool_scale, ret_norm, w_out, ffn2_norm, ffn2_w1, ffn2_w3, ffn2_w2, final_norm, loss_target, m_ffn1_norm, m_ffn1_w1, m_ffn1_w3, m_ffn1_w2, m_mix_norm, m_w_in, m_pool_w, m_pool_scale, m_ret_norm, m_w_out, m_ffn2_norm, m_ffn2_w1, m_ffn2_w3, m_ffn2_w2, m_final_norm, v_ffn1_norm, v_ffn1_w1, v_ffn1_w3, v_ffn1_w2, v_mix_norm, v_w_in, v_pool_w, v_pool_scale, v_ret_norm, v_w_out, v_ffn2_norm, v_ffn2_w1, v_ffn2_w3, v_ffn2_w2, v_final_norm):
    w = dict(ffn1_norm=ffn1_norm, ffn1_w1=ffn1_w1, ffn1_w3=ffn1_w3, ffn1_w2=ffn1_w2, mix_norm=mix_norm, w_in=w_in, pool_w=pool_w,
             pool_scale=pool_scale, ret_norm=ret_norm, w_out=w_out, ffn2_norm=ffn2_norm, ffn2_w1=ffn2_w1, ffn2_w3=ffn2_w3,
             ffn2_w2=ffn2_w2, final_norm=final_norm)
    m = dict(ffn1_norm=m_ffn1_norm, ffn1_w1=m_ffn1_w1, ffn1_w3=m_ffn1_w3, ffn1_w2=m_ffn1_w2, mix_norm=m_mix_norm, w_in=m_w_in,
             pool_w=m_pool_w, pool_scale=m_pool_scale, ret_norm=m_ret_norm, w_out=m_w_out, ffn2_norm=m_ffn2_norm, ffn2_w1=m_ffn2_w1,
             ffn2_w3=m_ffn2_w3, ffn2_w2=m_ffn2_w2, final_norm=m_final_norm)
    v = dict(ffn1_norm=v_ffn1_norm, ffn1_w1=v_ffn1_w1, ffn1_w3=v_ffn1_w3, ffn1_w2=v_ffn1_w2, mix_norm=v_mix_norm, w_in=v_w_in,
             pool_w=v_pool_w, pool_scale=v_pool_scale, ret_norm=v_ret_norm, w_out=v_w_out, ffn2_norm=v_ffn2_norm, ffn2_w1=v_ffn2_w1,
             ffn2_w3=v_ffn2_w3, ffn2_w2=v_ffn2_w2, final_norm=v_final_norm)
    xs, target = x[0], loss_target[0]
    T = xs.shape[0]
    tables = _ret_tables(T)
    place = jnp.stack([lax.axis_index("c"), 2 * lax.axis_index("x") + lax.axis_index("y")]).astype(jnp.int32)
    local = lambda d, k: jnp.transpose(d[k][0]) if k in TRANSPOSED else d[k][0]
    result = lambda o, k: jnp.transpose(o)[None] if k in TRANSPOSED else o[None]
    first = ("ffn1_w1", "ffn1_w3")
    sh = {k: local(w, k).astype(BF16) for k in first}
    gather = lambda *names: [ChipExchange([sh[k] for k in names], False)]
    wg, grad, delta, new_m, new_v = {}, {}, {}, {}, {}

    def update(names, pieces, name, hosted=()):
        outs, extras = adamw_sharded([(p, local(w, k), local(m, k), local(v, k)) for k, p in zip(names, pieces)], name, hosted)
        for t, k in enumerate(names):
            grad[k], delta[k], new_m[k], new_v[k] = [result(o, k) for o in outs[4 * t:4 * t + 4]]
        return extras

    def reduce_in_chip(name, *pairs):
        reduced = prereduce([p for p, _ in pairs], [r for _, r in pairs], place, "prereduce_" + name)
        return reduced[0] if len(pairs) == 1 else reduced

    scatter = lambda *reduced: ChipExchange([r[0] for r in reduced], True, [r[1] for r in reduced])
    whole = lambda k: wg[k].reshape(-1, wg[k].shape[-1])
    sharded = lambda g: g.reshape(N_CHIPS, -1, g.shape[-1])

    later = [k for k in BIG if k not in first]
    casts, ((wg["ffn1_w1"], wg["ffn1_w3"]),) = cast_shards([local(w, k) for k in later], "cast_gather_ffn1", gather(*first))
    sh.update(zip(later, casts))
    (n1, ga1, gb1, s1), ((wg["ffn1_w2"], wg["w_in"]),) = ffn_up(
        xs, ffn1_norm, whole("ffn1_w1"), whole("ffn1_w3"), "ffn1_up", gather("ffn1_w2", "w_in"))
    (h1, u, proj), ((wg["w_out"], wg["ffn2_w1"]),) = ffn_down_mix_in(
        s1, whole("ffn1_w2"), xs, mix_norm, wg["w_in"], "ffn1_down_mix_in", gather("w_out", "ffn2_w1"))
    (pa,), _ = pool_fwd(proj, pool_w[0], pool_scale, "pool_fwd")
    (rb, o_pre, r_prev), ((wg["ffn2_w3"],),) = ret_fwd(proj, ret_norm, tables, "ret_fwd", gather("ffn2_w3"))
    (h2, n2, ga2, gb2, s2), ((wg["ffn2_w2"],),) = ffn_up(
        h1, ffn2_norm, whole("ffn2_w1"), whole("ffn2_w3"), "mix_out_ffn2_up", gather("ffn2_w2"), mixed=(pa, rb, wg["w_out"]))
    (dh3, loss, d_final), _ = ffn_down_loss(s2, whole("ffn2_w2"), h2, final_norm[None], target, "ffn2_down_loss")

    (da2, db2, df2), _ = ffn_bwd_act(dh3, whole("ffn2_w2"), ga2, gb2, "ffn2_bwd_act")
    (g_f2w2,), _ = ffn_dw([s2], df2, 1, "ffn2_dw2")
    g_f2w2 = sharded(g_f2w2)
    (g_f2w1, g_f2w3), ((r_f2w2,),) = ffn_dw([da2, db2], n2, 2, "ffn2_dw13", [SiblingExchange([g_f2w2])])
    g_f2w1, g_f2w3 = sharded(g_f2w1), sharded(g_f2w3)
    p_f2w2 = reduce_in_chip("ffn2_w2", (g_f2w2, r_f2w2))
    (dh2, d_ffn2), ((q_f2w2,), (r_f2w1, r_f2w3)) = ffn_bwd_in(
        da2, db2, whole("ffn2_w1"), whole("ffn2_w3"), h2, ffn2_norm, dh3, "ffn2_bwd_in",
        [scatter(p_f2w2), SiblingExchange([g_f2w1, g_f2w3])])
    p_f2w1, p_f2w3 = reduce_in_chip("ffn2_w13", (g_f2w1, r_f2w1), (g_f2w3, r_f2w3))
    (dpa, drb, g_wout), _ = mix_out_bwd(dh2, wg["w_out"], pa, rb, "mix_out_bwd")
    (dproj, d_pool_w, d_pool_scale), _ = pool_bwd(proj, dpa, pool_w[0], pool_scale, "pool_bwd")
    (dproj, d_ret_norm), ((q_f2w1, q_f2w3), (r_wout,)) = ret_bwd(
        proj, drb, o_pre, r_prev, ret_norm, tables, dproj, "ret_bwd", [scatter(p_f2w1, p_f2w3), SiblingExchange([g_wout])])
    p_wout = reduce_in_chip("w_out", (g_wout, r_wout))
    (g_win,), ((q_wout,),) = mix_dwin(u, dproj, N_CHIPS, "mix_dwin", [scatter(p_wout)])
    (dh1, d_mix), ((r_win,),) = mix_in_bwd(dproj, wg["w_in"], h1, mix_norm, dh2, "mix_in_bwd", [SiblingExchange([g_win])])
    p_win = reduce_in_chip("w_in", (g_win, r_win))
    (da1, db1, df1), ((q_win,),) = ffn_bwd_act(dh1, whole("ffn1_w2"), ga1, gb1, "ffn1_bwd_act", [scatter(p_win)])
    d_small = {"pool_w": d_pool_w.reshape(-1, LANES), "mix_norm": d_mix, "pool_scale": d_pool_scale, "ret_norm": d_ret_norm,
               "ffn2_norm": d_ffn2, "final_norm": d_final}
    (g_f1w1, g_f1w3), (packs,) = ffn_dw([da1, db1], n1, 2, "ffn1_dw13", [AllExchange([d_small[k] for k in SMALL[:-1]] + [loss])])
    g_f1w1, g_f1w3 = sharded(g_f1w1), sharded(g_f1w3)
    (g_f1w2,), ((r_f1w1, r_f1w3),) = ffn_dw([s1], df1, 1, "ffn1_dw2", [SiblingExchange([g_f1w1, g_f1w3])])
    g_f1w2 = sharded(g_f1w2)
    p_f1w1, p_f1w3 = reduce_in_chip("ffn1_w13", (g_f1w1, r_f1w1), (g_f1w3, r_f1w3))
    (dx, d_ffn1), ((q_f1w1, q_f1w3), (r_f1w2,)) = ffn_bwd_in(
        da1, db1, whole("ffn1_w1"), whole("ffn1_w3"), xs, ffn1_norm, dh1, "ffn1_bwd_in",
        [scatter(p_f1w1, p_f1w3), SiblingExchange([g_f1w2])])
    p_f1w2 = reduce_in_chip("ffn1_w2", (g_f1w2, r_f1w2))

    (q_f1w2,), (late,) = update(["w_in", "w_out", "ffn2_w2"], [q_win, q_wout, q_f2w2], "adamw_mix_w2",
                                [scatter(p_f1w2), AllExchange([d_ffn1])])
    update(["ffn2_w1", "ffn2_w3", "ffn1_w1", "ffn1_w3"], [q_f2w1, q_f2w3, q_f1w1, q_f1w3], "adamw_w13")
    update(["ffn1_w2"], [q_f1w2], "adamw_ffn1_w2")
    flat = lambda t, k: t[k].reshape(-1, LANES) if k == "pool_w" else t[k].reshape(1, -1)
    updated, loss_sum = adamw_small(packs[:-1] + [late], [[flat(t, k) for t in (w, m, v)] for k in SMALL], packs[-1], "adamw_small")
    for k, outs in zip(SMALL, updated):
        grad[k], delta[k], new_m[k], new_v[k] = [o.reshape(w[k].shape) for o in outs]
    loss = loss_sum[0, 0]

    return (loss, dx[None], *[grad[k] for k in WEIGHTS], *[delta[k] for k in WEIGHTS],
            *[new_m[k] for k in WEIGHTS], *[new_v[k] for k in WEIGHTS])
```

```python
import math

import jax
import jax.numpy as jnp
from jax import lax
from jax.experimental import pallas as pl
from jax.experimental.pallas import tpu as pltpu

F32 = jnp.float32
BF16 = jnp.bfloat16

EPS = 1e-6
LANES = 128
BF16_TILE_ROWS = 16
N_CHIPS = 4
N_GROUPS = 4
HEAD_DIM = 128
RET_CHUNK = 128
ROPE_BASE = 10000.0
ADAM_LR, ADAM_B1, ADAM_B2, ADAM_EPS, ADAM_WD, ADAM_STEP = 0.001, 0.9, 0.999, 1e-08, 0.01, 10
VMEM_LIMIT_V7X = 56 * 1024 * 1024
ADAMW_VMEM_BUDGET = 32 * 1024 * 1024
MESH = pl.DeviceIdType.MESH
ANY = pl.BlockSpec(memory_space=pl.ANY)


def _dot(a, b):
    return jnp.dot(a, b, preferred_element_type=F32)


def _dot_nt(a, b):
    return lax.dot_general(a, b, (((1,), (1,)), ((), ())), preferred_element_type=F32)


def _dot_tn(a, b):
    return lax.dot_general(a, b, (((0,), (0,)), ((), ())), preferred_element_type=F32)


def _rstd(h):
    return lax.rsqrt(jnp.mean(h * h, axis=-1, keepdims=True) + EPS)


def _rmsnorm_bwd(dn, h, gain):
    r = _rstd(h)
    nh = h * r
    dnh = dn * gain
    dh = r * (dnh - nh * jnp.mean(dnh * nh, axis=-1, keepdims=True))
    return dh, dn * nh


def _silu_parts(a):
    sig = jax.nn.sigmoid(a)
    silu = a * sig
    return silu, sig + silu * (1.0 - sig)


def _mesh_pos():
    return lax.axis_index("x"), lax.axis_index("y"), lax.axis_index("c")


class ChipExchange:
    def __init__(self, srcs, scatter, placed=()):
        n = len(srcs)
        self.inputs, self.scatter, self.n, self.reach = list(srcs) + list(placed), scatter, n, REACH_CHIPS
        self.aliases = {n + t: t for t in range(n)} if scatter else {}
        self.half_rows = [s.shape[1] if scatter else s.shape[0] // 2 for s in srcs]
        self.out_shape = [jax.ShapeDtypeStruct((N_CHIPS, 2 * rh, s.shape[-1]), s.dtype) for s, rh in zip(srcs, self.half_rows)]
        if scatter:
            self.out_shape += [jax.ShapeDtypeStruct((2, rh // 2, s.shape[-1]), s.dtype) for s, rh in zip(srcs, self.half_rows)]
        dma = pltpu.SemaphoreType.DMA
        self.sems = [dma((4 * n,)), dma((4 * n,)), dma((2 * n,)), dma((2 * n,)), dma((4 * n,)), dma((4 * n,))]

    def _copies(self, src, out, sems):
        hop1_send, hop1_recv, hop2_send, hop2_recv, d2d_send, d2d_recv = sems
        x, y, c = _mesh_pos()
        me, dg = 2 * x + y, 2 * (1 - x) + (1 - y)
        sibling = (x, y, 1 - c)
        n = self.n
        mine, theirs = c, 1 - c

        def nb(a):
            nx, ny = x ^ (1 - a), y ^ a
            return 2 * nx + ny, (nx, ny, c)

        def remote(s, d, send, recv, k, to):
            return pltpu.make_async_remote_copy(src_ref=s, dst_ref=d, send_sem=send.at[k], recv_sem=recv.at[k],
                                                device_id=to, device_id_type=MESH)

        class Copies:
            def slot(_, t, chip, half):
                rh = self.half_rows[t]
                return out[t].at[chip, pl.ds(half * rh, rh), :]

            def quarter(_, t, chip, q):
                qh = self.half_rows[t] // 2
                return out[t].at[chip, pl.ds(mine * 2 * qh + q * qh, qh), :]

            def own_shard(k, t):
                return remote(src[t], out[t].at[me], d2d_send, d2d_recv, 4 * t + 3, sibling)

            def hop1(k, t, a, transit=False):
                rh = self.half_rows[t]
                chip, to = nb(a)
                if transit:
                    piece = src[t].at[dg, pl.ds(a * (rh // 2), rh // 2), :]
                    return remote(piece, out[n + t].at[a], hop1_send, hop1_recv, 4 * t + 2 + a, to)
                piece = src[t].at[chip] if self.scatter else src[t].at[pl.ds(mine * rh, rh), :]
                return remote(piece, k.slot(t, me, mine), hop1_send, hop1_recv, 4 * t + a, to)

            def landed1(k, t, a, transit=False):
                here = out[n + t].at[a] if transit else k.slot(t, nb(a)[0], mine)
                return remote(here, here, hop1_send, hop1_recv, 4 * t + (2 if transit else 0) + a, sibling)

            def hop2(k, t, q):
                origin, to = nb(q)[0], nb(1 - q)[1]
                piece = out[n + t].at[q] if self.scatter else k.quarter(t, origin, q)
                return remote(piece, k.quarter(t, origin, q), hop2_send, hop2_recv, 2 * t + q, to)

            def landed2(k, t, q):
                here = k.quarter(t, dg, q)
                return remote(here, here, hop2_send, hop2_recv, 2 * t + q, sibling)

            def d2d(k, t, p, chip, own=False, arriving=False):
                if arriving:
                    there = k.slot(t, chip, theirs)
                    return remote(there, there, d2d_send, d2d_recv, 4 * t + p, sibling)
                piece = src[t].at[me] if own else k.slot(t, chip, mine)
                return remote(piece, k.slot(t, chip, mine), d2d_send, d2d_recv, 4 * t + p, sibling)

        return Copies(), nb, me, dg, c

    def start(self, src, out, sems):
        k, nb, me, dg, c = self._copies(src, out, sems)
        for t in range(self.n):
            for first in range(2):
                a = first ^ c
                k.hop1(t, a).start()
                if self.scatter:
                    k.hop1(t, a, transit=True).start()
            if self.scatter:
                k.d2d(t, 3, me, own=True).start()
            else:
                k.own_shard(t).start()

    def mid(self, src, out, sems):
        k, nb, me, dg, c = self._copies(src, out, sems)
        for t in range(self.n):
            for first in range(2):
                a = first ^ c
                if self.scatter:
                    k.landed1(t, a, transit=True).wait_recv()
                    k.hop2(t, a).start()
                k.landed1(t, a).wait_recv()
                if not self.scatter:
                    k.hop2(t, a).start()
                k.d2d(t, a, nb(a)[0]).start()

    def finish(self, src, out, sems):
        k, nb, me, dg, c = self._copies(src, out, sems)
        for t in range(self.n):
            for q in range(2):
                k.landed2(t, q).wait_recv()
            k.d2d(t, 2, dg).start()
        for t in range(self.n):
            for a in range(2):
                k.d2d(t, a, nb(a)[0], arriving=True).wait_recv()
            k.d2d(t, 2, dg, arriving=True).wait_recv()
            if self.scatter:
                k.d2d(t, 3, me, arriving=True).wait_recv()
        for t in range(self.n):
            for a in range(2):
                k.hop1(t, a).wait_send()
                if self.scatter:
                    k.hop1(t, a, transit=True).wait_send()
                k.hop2(t, a).wait_send()
                k.d2d(t, a, nb(a)[0]).wait_send()
            k.d2d(t, 2, dg).wait_send()
            if self.scatter:
                k.d2d(t, 3, me, own=True).wait_send()
            else:
                k.own_shard(t).wait()


class SiblingExchange:
    def __init__(self, grads):
        self.inputs, self.n, self.aliases, self.reach = list(grads), len(grads), {}, REACH_SIBLING
        self.half_rows = [g.shape[1] // 2 for g in grads]
        self.out_shape = [jax.ShapeDtypeStruct((g.shape[0], rh, g.shape[2]), g.dtype) for g, rh in zip(grads, self.half_rows)]
        self.sems = [pltpu.SemaphoreType.DMA((self.n,)), pltpu.SemaphoreType.DMA((self.n,))]

    def _plan(self, src, out, sems):
        x, y, c = _mesh_pos()
        return [pltpu.make_async_remote_copy(
            src_ref=src[t].at[:, pl.ds((1 - c) * self.half_rows[t], self.half_rows[t]), :], dst_ref=out[t],
            send_sem=sems[0].at[t], recv_sem=sems[1].at[t], device_id=(x, y, 1 - c), device_id_type=MESH) for t in range(self.n)]

    def start(self, src, out, sems):
        for cp in self._plan(src, out, sems):
            cp.start()

    def mid(self, src, out, sems):
        pass

    def finish(self, src, out, sems):
        for cp in self._plan(src, out, sems):
            cp.wait()


REACH_SIBLING, REACH_CHIPS, REACH_ALL = 0, 1, 2


def _entry_barrier(reach):
    x, y, c = _mesh_pos()
    peers = [(x, y, 1 - c)]
    if reach == REACH_CHIPS:
        peers += [(1 - x, y, c), (x, 1 - y, c)]
    elif reach == REACH_ALL:
        peers = [(x ^ dx, y ^ dy, c ^ dc) for dx in (0, 1) for dy in (0, 1) for dc in (0, 1)][1:]
    barrier = pltpu.get_barrier_semaphore()
    for peer in peers:
        pl.semaphore_signal(barrier, inc=1, device_id=peer, device_id_type=MESH)
    pl.semaphore_wait(barrier, len(peers))


def _call(body, hosted=(), *, name, in_specs, out_specs, out_shape, args, grid=(), scratch_shapes=(), aliased=None):
    n_in, n_out, n_scr = len(in_specs), len(out_specs), len(scratch_shapes)
    total = math.prod(grid)
    mid_step = max(0, (3 * total) // 4 - 1)

    def full(*refs):
        pos = [0]

        def take(k):
            pos[0] += k
            return refs[pos[0] - k:pos[0]]

        ins, h_in = take(n_in), [take(len(h.inputs)) for h in hosted]
        outs, h_out = take(n_out), [take(len(h.out_shape)) for h in hosted]
        scr, h_sem = take(n_scr), [take(len(h.sems)) for h in hosted]
        step = 0
        for axis, size in enumerate(grid):
            step = step * size + pl.program_id(axis)

        def phase(at, method):
            if not hosted:
                return

            def run():
                if method == "start":
                    _entry_barrier(reach)
                for h, s, o, m in zip(hosted, h_in, h_out, h_sem):
                    getattr(h, method)(s, o, m)

            if total == 1:
                run()
            else:
                pl.when(step == at)(run)

        phase(0, "start")
        body(*ins, *outs, *scr)
        phase(mid_step, "mid")
        phase(total - 1, "finish")

    aliases, i0, o0 = dict(aliased or {}), n_in, n_out
    for h in hosted:
        aliases.update({i0 + i: o0 + o for i, o in h.aliases.items()})
        i0, o0 = i0 + len(h.inputs), o0 + len(h.out_shape)
    reach = max((h.reach for h in hosted), default=None)
    params = dict(vmem_limit_bytes=VMEM_LIMIT_V7X)
    if hosted:
        params["collective_id"] = reach
    results = pl.pallas_call(
        full, name=name, grid=grid,
        in_specs=list(in_specs) + [ANY] * (i0 - n_in),
        out_specs=list(out_specs) + [ANY] * (o0 - n_out),
        out_shape=list(out_shape) + [s for h in hosted for s in h.out_shape],
        scratch_shapes=list(scratch_shapes) + [s for h in hosted for s in h.sems],
        input_output_aliases=aliases,
        compiler_params=pltpu.CompilerParams(**params),
    )(*args, *[s for h in hosted for s in h.inputs])
    outs, extras, pos = list(results[:n_out]), [], n_out
    for h in hosted:
        extras.append(list(results[pos:pos + h.n]))
        pos += len(h.out_shape)
    return outs, extras


def cast_shards(shards, name, hosted=()):
    n = len(shards)

    def body(*refs):
        for x_ref, o_ref in zip(refs[:n], refs[n:]):
            o_ref[...] = x_ref[...].astype(BF16)

    whole = lambda s: pl.BlockSpec(s.shape, lambda: (0,) * s.ndim)
    return _call(body, hosted, name=name, in_specs=[whole(s) for s in shards], out_specs=[whole(s) for s in shards],
                 out_shape=[jax.ShapeDtypeStruct(s.shape, BF16) for s in shards], args=list(shards))


class AllExchange:
    def __init__(self, arrays):
        n = len(arrays)
        self.inputs, self.n, self.aliases, self.reach = list(arrays), n, {}, REACH_ALL
        self.out_shape = [jax.ShapeDtypeStruct((2 * N_CHIPS,) + a.shape, a.dtype) for a in arrays]
        self.sems = [pltpu.SemaphoreType.DMA((n,)), pltpu.SemaphoreType.DMA((7 * n,)), pltpu.SemaphoreType.DMA((7 * n,))]

    def _copies(self, src, out, sems):
        local_sem, send_sem, recv_sem = sems
        x, y, c = _mesh_pos()
        me = 4 * x + 2 * y + c
        peers = [(x ^ dx, y ^ dy, c ^ dc) for dx in (0, 1) for dy in (0, 1) for dc in (0, 1)][1:]
        remote = lambda s, d, k, to: pltpu.make_async_remote_copy(
            src_ref=s, dst_ref=d, send_sem=send_sem.at[k], recv_sem=recv_sem.at[k], device_id=to, device_id_type=MESH)
        sends, landed, local = [], [], []
        for t in range(self.n):
            local.append(pltpu.make_async_copy(src[t], out[t].at[me], local_sem.at[t]))
            for p, (px, py, pc) in enumerate(peers):
                sends.append(remote(src[t], out[t].at[me], 7 * t + p, (px, py, pc)))
                here = out[t].at[4 * px + 2 * py + pc]
                landed.append(remote(here, here, 7 * t + p, (px, py, pc)))
        return sends, landed, local

    def start(self, src, out, sems):
        sends, _, local = self._copies(src, out, sems)
        for cp in sends + local:
            cp.start()

    def mid(self, src, out, sems):
        pass

    def finish(self, src, out, sems):
        sends, landed, local = self._copies(src, out, sems)
        for cp in landed:
            cp.wait_recv()
        for cp in sends:
            cp.wait_send()
        for cp in local:
            cp.wait()


MXU_COLS = 256


def _resident(shape):
    return pl.BlockSpec(shape, lambda *_: (0,) * len(shape), pipeline_mode=pl.Buffered(1))


def ffn_up(h, gain, w1, w3, name, hosted=(), mixed=None):
    T, D = h.shape
    F = w1.shape[0]
    tm = min(T, 256)

    def body(*refs):
        if mixed is None:
            h_ref, g_ref, w1_ref, w3_ref, n_ref, ga_ref, gb_ref, s_ref = refs
            hh = h_ref[...]
        else:
            pa_ref, rb_ref, wo_ref, h_ref, g_ref, w1_ref, w3_ref, hh_ref, n_ref, ga_ref, gb_ref, s_ref = refs
            hh = h_ref[...] + _dot(pa_ref[...], wo_ref[0]) + _dot(rb_ref[...], wo_ref[1])
            hh_ref[...] = hh
        n = (hh * _rstd(hh) * g_ref[...]).astype(BF16)
        n_ref[...] = n
        for c in range(0, F, MXU_COLS):
            cols = slice(c, c + MXU_COLS)
            a = _dot_nt(n, w1_ref[cols, :])
            b = _dot_nt(n, w3_ref[cols, :])
            silu, dsilu = _silu_parts(a)
            ga_ref[:, cols] = (b * dsilu).astype(BF16)
            gb_ref[:, cols] = silu.astype(BF16)
            s_ref[:, cols] = (silu * b).astype(BF16)

    act = jax.ShapeDtypeStruct((T, F), BF16)
    act_spec = pl.BlockSpec((tm, F), lambda i: (i, 0))
    row_spec = pl.BlockSpec((tm, D), lambda i: (i, 0))
    in_specs = [row_spec, pl.BlockSpec((1, D), lambda i: (0, 0)), _resident((F, D)), _resident((F, D))]
    out_specs, out_shape, args = [row_spec, act_spec, act_spec, act_spec], [jax.ShapeDtypeStruct((T, D), BF16), act, act, act], [h, gain, w1, w3]
    if mixed is not None:
        pa, rb, woutg = mixed
        W = pa.shape[1]
        in_specs = [pl.BlockSpec((tm, W), lambda i: (i, 0))] * 2 + [_resident((2, W, D))] + in_specs
        out_specs, out_shape = [row_spec] + out_specs, [jax.ShapeDtypeStruct((T, D), F32)] + out_shape
        args = [pa, rb, woutg.reshape(2, W, D)] + args
    return _call(body, hosted, name=name, grid=(T // tm,), in_specs=in_specs, out_specs=out_specs, out_shape=out_shape, args=args)


def ffn_bwd_act(dh, w2, ga, gb, name, hosted=()):
    T, D = dh.shape
    F = w2.shape[0]
    tm = min(T, 256)

    def body(dh_ref, w2_ref, ga_ref, gb_ref, da_ref, db_ref, df_ref):
        df = (0.5 * dh_ref[...]).astype(BF16)
        df_ref[...] = df
        for c in range(0, F, MXU_COLS):
            cols = slice(c, c + MXU_COLS)
            ds = _dot_nt(df, w2_ref[cols, :])
            da_ref[:, cols] = (ds * ga_ref[:, cols].astype(F32)).astype(BF16)
            db_ref[:, cols] = (ds * gb_ref[:, cols].astype(F32)).astype(BF16)

    act = jax.ShapeDtypeStruct((T, F), BF16)
    act_spec = pl.BlockSpec((tm, F), lambda i: (i, 0))
    row_spec = pl.BlockSpec((tm, D), lambda i: (i, 0))
    return _call(
        body, hosted, name=name, grid=(T // tm,),
        in_specs=[row_spec, _resident((F, D)), act_spec, act_spec],
        out_specs=[act_spec, act_spec, row_spec],
        out_shape=[act, act, jax.ShapeDtypeStruct((T, D), BF16)],
        args=[dh, w2, ga, gb])


def ffn_dw(xs, y, halves, name, hosted=()):
    T, F = xs[0].shape
    D = y.shape[1]
    nx, fh = len(xs), F // halves
    tk = min(T, 512)
    nk = T // tk

    def body(*refs):
        y_ref, x_refs, o_refs, accs = refs[0], refs[1:1 + nx], refs[1 + nx:1 + 2 * nx], refs[1 + 2 * nx:]
        k = pl.program_id(1)

        @pl.when(k == 0)
        def _():
            for acc in accs:
                acc[...] = jnp.zeros_like(acc)

        yy = y_ref[...]
        for x_ref, acc in zip(x_refs, accs):
            acc[...] += _dot_tn(x_ref[...], yy)

        @pl.when(k == nk - 1)
        def _():
            for o_ref, acc in zip(o_refs, accs):
                o_ref[...] = acc[...].astype(BF16)

    out = jax.ShapeDtypeStruct((F, D), BF16)
    return _call(
        body, hosted, name=name, grid=(halves, nk),
        in_specs=[pl.BlockSpec((tk, D), lambda j, k: (k, 0))] + [pl.BlockSpec((tk, fh), lambda j, k: (k, j))] * nx,
        out_specs=[pl.BlockSpec((fh, D), lambda j, k: (j, 0))] * nx,
        out_shape=[out] * nx,
        scratch_shapes=[pltpu.VMEM((fh, D), F32)] * nx,
        args=[y] + list(xs))


def ffn_bwd_in(da, db, w1, w3, h, gain, dh, name, hosted=()):
    T, F = da.shape
    D = h.shape[1]
    tm = min(T, 256)

    def body(da_ref, db_ref, w1_ref, w3_ref, h_ref, g_ref, dh_ref, o_ref, dg_ref):
        dn = _dot(da_ref[...], w1_ref[...]) + _dot(db_ref[...], w3_ref[...])
        dhn, dg = _rmsnorm_bwd(dn, h_ref[...], g_ref[...])
        o_ref[...] = dh_ref[...] + dhn

        @pl.when(pl.program_id(0) == 0)
        def _():
            dg_ref[...] = jnp.zeros_like(dg_ref)

        dg_ref[...] += jnp.sum(dg, axis=0, keepdims=True)

    act_spec = pl.BlockSpec((tm, F), lambda i: (i, 0))
    row_spec = pl.BlockSpec((tm, D), lambda i: (i, 0))
    vec_spec = pl.BlockSpec((1, D), lambda i: (0, 0))
    return _call(
        body, hosted, name=name, grid=(T // tm,),
        in_specs=[act_spec, act_spec, _resident((F, D)), _resident((F, D)), row_spec, vec_spec, row_spec],
        out_specs=[row_spec, vec_spec],
        out_shape=[jax.ShapeDtypeStruct((T, D), F32), jax.ShapeDtypeStruct((1, D), F32)],
        args=[da, db, w1, w3, h, gain, dh])


def ffn_down_mix_in(s, w2, h, gain, wing, name, hosted=()):
    T, F = s.shape
    D = h.shape[1]
    nsh, _, Cs = wing.shape
    tm = min(T, 256)

    def body(s_ref, w2_ref, h_ref, g_ref, w_ref, hh_ref, u_ref, p_ref):
        hh = h_ref[...] + 0.5 * _dot(s_ref[...], w2_ref[...])
        hh_ref[...] = hh
        u = (hh * _rstd(hh) * g_ref[...]).astype(BF16)
        u_ref[...] = u
        for j in range(nsh):
            p_ref[:, j * Cs:(j + 1) * Cs] = _dot(u, w_ref[j])

    row_spec = pl.BlockSpec((tm, D), lambda i: (i, 0))
    return _call(
        body, hosted, name=name, grid=(T // tm,),
        in_specs=[pl.BlockSpec((tm, F), lambda i: (i, 0)), _resident((F, D)), row_spec, pl.BlockSpec((1, D), lambda i: (0, 0)),
                  _resident((nsh, D, Cs))],
        out_specs=[row_spec, row_spec, pl.BlockSpec((tm, nsh * Cs), lambda i: (i, 0))],
        out_shape=[jax.ShapeDtypeStruct((T, D), F32), jax.ShapeDtypeStruct((T, D), BF16), jax.ShapeDtypeStruct((T, nsh * Cs), F32)],
        args=[s, w2, h, gain, wing])


def mix_out_bwd(dh, woutg, a, b, name, hosted=()):
    T, D = dh.shape
    W = a.shape[1]
    nsh, Rs, _ = woutg.shape
    wout = woutg.reshape(2, W, D)
    tk = min(T, 512)
    nk = T // tk

    def body(dh_ref, w_ref, a_ref, b_ref, da_ref, db_ref, dw_ref, acc):
        k = pl.program_id(0)

        @pl.when(k == 0)
        def _():
            acc[...] = jnp.zeros_like(acc)

        dhb = dh_ref[...].astype(BF16)
        da_ref[...] = _dot_nt(dhb, w_ref[0])
        db_ref[...] = _dot_nt(dhb, w_ref[1])
        acc[0:W, :] += _dot_tn(a_ref[...], dhb)
        acc[W:2 * W, :] += _dot_tn(b_ref[...], dhb)

        @pl.when(k == nk - 1)
        def _():
            for j in range(nsh):
                dw_ref[j] = acc[j * Rs:(j + 1) * Rs, :].astype(BF16)

    return _call(
        body, hosted, name=name, grid=(nk,),
        in_specs=[pl.BlockSpec((tk, D), lambda k: (k, 0)), pl.BlockSpec((2, W, D), lambda k: (0, 0, 0)),
                  pl.BlockSpec((tk, W), lambda k: (k, 0)), pl.BlockSpec((tk, W), lambda k: (k, 0))],
        out_specs=[pl.BlockSpec((tk, W), lambda k: (k, 0)), pl.BlockSpec((tk, W), lambda k: (k, 0)),
                   pl.BlockSpec((nsh, Rs, D), lambda k: (0, 0, 0))],
        out_shape=[jax.ShapeDtypeStruct((T, W), F32), jax.ShapeDtypeStruct((T, W), F32),
                   jax.ShapeDtypeStruct((nsh, Rs, D), BF16)],
        scratch_shapes=[pltpu.VMEM((2 * W, D), F32)],
        args=[dh, wout, a, b])


def _dproj_block(g):
    return (g // N_GROUPS + N_GROUPS) % (N_GROUPS + 1), g % N_GROUPS


def mix_dwin(u, dproj, nsh, name, hosted=()):
    T, D = u.shape
    Hd = HEAD_DIM
    slabs, _, width = dproj.shape
    blocks = slabs * width // Hd
    Cs = blocks * Hd // nsh
    tk = min(T, 512)
    nk = T // tk

    def body(u_ref, d_ref, o_ref, acc):
        k = pl.program_id(0)

        @pl.when(k == 0)
        def _():
            acc[...] = jnp.zeros_like(acc)

        where = [_dproj_block(g) for g in range(blocks)]
        d = jnp.concatenate([d_ref[slab, :, col * Hd:(col + 1) * Hd] for slab, col in where], axis=1)
        acc[...] += _dot_tn(u_ref[...], d)

        @pl.when(k == nk - 1)
        def _():
            for j in range(nsh):
                o_ref[j] = acc[:, j * Cs:(j + 1) * Cs].astype(BF16)

    return _call(
        body, hosted, name=name, grid=(nk,),
        in_specs=[pl.BlockSpec((tk, D), lambda k: (k, 0)), pl.BlockSpec((slabs, tk, width), lambda k: (0, k, 0))],
        out_specs=[pl.BlockSpec((nsh, D, Cs), lambda k: (0, 0, 0))],
        out_shape=[jax.ShapeDtypeStruct((nsh, D, Cs), BF16)],
        scratch_shapes=[pltpu.VMEM((D, blocks * Hd), F32)],
        args=[u, dproj])


def mix_in_bwd(dproj, wing, h, gain, dh, name, hosted=()):
    T, D = h.shape
    nsh, _, Cs = wing.shape
    Hd = HEAD_DIM
    per = Cs // Hd
    tm = min(T, 256)

    def body(d_ref, w_ref, h_ref, g_ref, dh_ref, o_ref, dg_ref):
        def shard(j):
            blocks = [_dproj_block(per * j + i) for i in range(per)]
            return jnp.concatenate([d_ref[slab, :, col * Hd:(col + 1) * Hd] for slab, col in blocks], axis=1)

        du = _dot_nt(shard(0), w_ref[0])
        for j in range(1, nsh):
            du += _dot_nt(shard(j), w_ref[j])
        dhn, dg = _rmsnorm_bwd(du, h_ref[...], g_ref[...])
        o_ref[...] = dh_ref[...] + dhn

        @pl.when(pl.program_id(0) == 0)
        def _():
            dg_ref[...] = jnp.zeros_like(dg_ref)

        dg_ref[...] += jnp.sum(dg, axis=0, keepdims=True)

    row_spec = pl.BlockSpec((tm, D), lambda i: (i, 0))
    vec_spec = pl.BlockSpec((1, D), lambda i: (0, 0))
    return _call(
        body, hosted, name=name, grid=(T // tm,),
        in_specs=[pl.BlockSpec((dproj.shape[0], tm, dproj.shape[2]), lambda i: (0, i, 0)),
                  pl.BlockSpec((nsh, D, Cs), lambda i: (0, 0, 0)), row_spec, vec_spec, row_spec],
        out_specs=[row_spec, vec_spec],
        out_shape=[jax.ShapeDtypeStruct((T, D), F32), jax.ShapeDtypeStruct((1, D), F32)],
        args=[dproj, wing, h, gain, dh])


POOL_WINDOWS = (2, 4, 8, 16)


def _pool_window(x, window, T, trailing):
    rows = lax.broadcasted_iota(jnp.int32, x.shape, 0)
    s, k = x, 1
    while k < window:
        if trailing:
            s = s + jnp.where(rows >= k, pltpu.roll(s, k, 0), 0.0)
        else:
            s = s + jnp.where(rows < T - k, pltpu.roll(s, T - k, 0), 0.0)
        k *= 2
    return s


def _pool_count(window, shape):
    rows = lax.broadcasted_iota(jnp.int32, shape, 0)
    return jnp.minimum(rows + 1, window).astype(F32)


def _per_group(work):
    for group, window in enumerate(POOL_WINDOWS):
        pl.when(pl.program_id(0) == group)(lambda window=window: work(window))


def pool_fwd(proj, pool_w, pool_scale, name, hosted=()):
    T = proj.shape[0]
    Hd = HEAD_DIM

    def body(x_ref, w_ref, sc_ref, a_ref):
        def work(window):
            x = x_ref[...]
            pooled = _pool_window(x, window, T, True) / _pool_count(window, x.shape) - x
            a_ref[...] = (_dot(pooled.astype(BF16), w_ref[0].astype(BF16)) * sc_ref[...]).astype(BF16)

        _per_group(work)

    return _call(
        body, hosted, name=name, grid=(N_GROUPS,),
        in_specs=[pl.BlockSpec((T, Hd), lambda g: (0, g)), pl.BlockSpec((1, Hd, Hd), lambda g: (g, 0, 0)),
                  pl.BlockSpec((1, Hd), lambda g: (0, g))],
        out_specs=[pl.BlockSpec((T, Hd), lambda g: (0, g))],
        out_shape=[jax.ShapeDtypeStruct((T, N_GROUPS * Hd), BF16)],
        args=[proj, pool_w, pool_scale])


def pool_bwd(proj, da, pool_w, pool_scale, name, hosted=()):
    T = proj.shape[0]
    Hd = HEAD_DIM

    def body(x_ref, da_ref, w_ref, sc_ref, dx_ref, dw_ref, dsc_ref):
        def work(window):
            x = x_ref[...]
            cnt = _pool_count(window, x.shape)
            pooled = (_pool_window(x, window, T, True) / cnt - x).astype(BF16)
            wb = w_ref[0].astype(BF16)
            dav = da_ref[...]
            dsc_ref[...] = jnp.sum(dav * _dot(pooled, wb), axis=0, keepdims=True)
            dout = (dav * sc_ref[...]).astype(BF16)
            dw_ref[0] = _dot_tn(pooled, dout)
            dpooled = _dot_nt(dout, wb)
            dx_ref[0] = (_pool_window(dpooled / cnt, window, T, False) - dpooled).astype(BF16)

        _per_group(work)

    col_spec = pl.BlockSpec((T, Hd), lambda g: (0, g))
    return _call(
        body, hosted, name=name, grid=(N_GROUPS,),
        in_specs=[col_spec, col_spec, pl.BlockSpec((1, Hd, Hd), lambda g: (g, 0, 0)), pl.BlockSpec((1, Hd), lambda g: (0, g))],
        out_specs=[pl.BlockSpec((1, T, Hd), lambda g: (N_GROUPS, 0, g)), pl.BlockSpec((1, Hd, Hd), lambda g: (g, 0, 0)),
                   pl.BlockSpec((1, Hd), lambda g: (0, g))],
        out_shape=[jax.ShapeDtypeStruct((N_GROUPS + 1, T, N_GROUPS * Hd), BF16), jax.ShapeDtypeStruct((N_GROUPS, Hd, Hd), F32),
                   jax.ShapeDtypeStruct((1, N_GROUPS * Hd), F32)],
        args=[proj, da, pool_w, pool_scale])


def _ret_tables(T):
    Hd, C = HEAD_DIM, RET_CHUNK
    inv_freq = 1.0 / (ROPE_BASE ** (jnp.arange(0, Hd, 2, dtype=F32) / Hd))
    ang = jnp.arange(T, dtype=F32)[:, None] * inv_freq[None, :]
    cos, sin = jnp.cos(ang), jnp.sin(ang)
    cos2 = jnp.concatenate([cos, cos], axis=-1)
    sin2 = jnp.concatenate([-sin, sin], axis=-1)
    log_gamma = jnp.log1p(-jnp.exp2(-5.0 - jnp.arange(N_GROUPS, dtype=F32)))
    pos = jnp.arange(C, dtype=F32)
    rel = pos[:, None] - pos[None, :]
    intra = jnp.where(rel[None] >= 0, jnp.exp(log_gamma[:, None, None] * jnp.maximum(rel, 0.0)[None]), 0.0)
    k_tail = jnp.exp(log_gamma[:, None] * (C - 1 - pos)[None, :])
    q_head = jnp.exp(log_gamma[:, None] * (pos + 1.0)[None, :])
    chunk_decay = jnp.exp(log_gamma * C)
    wide = lambda t: jnp.broadcast_to(t[:, :, None], (N_GROUPS, C, Hd))
    return cos2, sin2, intra, wide(k_tail), wide(q_head), jnp.broadcast_to(chunk_decay[:, None, None], (N_GROUPS, 1, Hd))


def _rope(x, cos2, sin2):
    return x * cos2 + pltpu.roll(x, HEAD_DIM // 2, 1) * sin2


def _rope_t(d, cos2, sin2):
    return d * cos2 + pltpu.roll(d * sin2, HEAD_DIM // 2, 1)


def _ret_specs(T, tseg, seg_of):
    Hd, G = HEAD_DIM, N_GROUPS
    col = lambda kind: pl.BlockSpec((tseg, Hd), lambda h, s: (seg_of(s), G * kind + h))
    tab = pl.BlockSpec((T, Hd), lambda h, s: (0, 0))
    head = pl.BlockSpec((1, RET_CHUNK, Hd), lambda h, s: (h, 0, 0))
    cd = pl.BlockSpec((1, 1, Hd), lambda h, s: (h, 0, 0))
    gain = pl.BlockSpec((1, Hd), lambda h, s: (0, h))
    return col, tab, head, cd, gain


def ret_fwd(proj, ret_norm, tables, name, hosted=()):
    T = proj.shape[0]
    Hd, C, G = HEAD_DIM, RET_CHUNK, N_GROUPS
    tseg = min(T, 2048)
    nseg, nck = T // tseg, tseg // C
    scale = Hd ** -0.5
    cos2, sin2, intra, k_tail, q_head, chunk_decay = tables

    def body(q_ref, k_ref, v_ref, g_ref, gain_ref, cos_ref, sin_ref, m_ref, kt_ref, qh_ref, cd_ref,
             b_ref, o_ref, rp_ref, state):
        @pl.when(pl.program_id(1) == 0)
        def _():
            state[...] = jnp.zeros_like(state)

        def chunk(ci, carry):
            rows = pl.ds(pl.multiple_of(ci * C, C), C)
            at = pl.ds(pl.multiple_of(pl.program_id(1) * tseg + ci * C, C), C)
            cos, sin = cos_ref[at, :], sin_ref[at, :]
            qr = _rope(q_ref[rows, :], cos, sin)
            kr = _rope(k_ref[rows, :], cos, sin) * scale
            qb, kb, vb = qr.astype(BF16), kr.astype(BF16), v_ref[rows, :].astype(BF16)
            r = state[...]
            rp_ref[0, ci] = r.astype(BF16)
            sc = _dot_nt(qb, kb) * m_ref[0]
            o = _dot(sc.astype(BF16), vb) + _dot((qr * qh_ref[0]).astype(BF16), r.astype(BF16))
            state[...] = cd_ref[0] * r + _dot_tn((kr * kt_ref[0]).astype(BF16), vb)
            o_ref[rows, :] = o
            on = o * _rstd(o)
            b_ref[rows, :] = (jax.nn.silu(g_ref[rows, :]) * (on * gain_ref[...])).astype(BF16)
            return carry

        lax.fori_loop(0, nck, chunk, 0, unroll=True)

    col, tab, head, cd, gain = _ret_specs(T, tseg, lambda s: s)
    out_col = pl.BlockSpec((tseg, Hd), lambda h, s: (s, h))
    return _call(
        body, hosted, name=name, grid=(G, nseg),
        in_specs=[col(1), col(2), col(3), col(4), gain, tab, tab, head, head, head, cd],
        out_specs=[out_col, out_col, pl.BlockSpec((1, nck, Hd, Hd), lambda h, s: (h, s, 0, 0))],
        out_shape=[jax.ShapeDtypeStruct((T, G * Hd), BF16), jax.ShapeDtypeStruct((T, G * Hd), F32),
                   jax.ShapeDtypeStruct((G, T // C, Hd, Hd), BF16)],
        scratch_shapes=[pltpu.VMEM((Hd, Hd), F32)],
        args=[proj, proj, proj, proj, ret_norm, cos2, sin2, intra, k_tail, q_head, chunk_decay])


def ret_bwd(proj, db, o_pre, r_prev, ret_norm, tables, dproj, name, hosted=()):
    T = proj.shape[0]
    Hd, C, G = HEAD_DIM, RET_CHUNK, N_GROUPS
    tseg = min(T, 2048)
    nseg, nck = T // tseg, tseg // C
    scale = Hd ** -0.5
    cos2, sin2, intra, k_tail, q_head, chunk_decay = tables

    def body(q_ref, k_ref, v_ref, g_ref, db_ref, o_ref, rp_ref, gain_ref, cos_ref, sin_ref, m_ref, kt_ref, qh_ref, cd_ref,
             _, d_ref, dgain_ref, gstate):
        @pl.when(pl.program_id(1) == 0)
        def _():
            gstate[...] = jnp.zeros_like(gstate)
            dgain_ref[...] = jnp.zeros_like(dgain_ref)

        def chunk(t, carry):
            ci = nck - 1 - t
            rows = pl.ds(pl.multiple_of(ci * C, C), C)
            at = pl.ds(pl.multiple_of((nseg - 1 - pl.program_id(1)) * tseg + ci * C, C), C)
            cos, sin = cos_ref[at, :], sin_ref[at, :]
            qr = _rope(q_ref[rows, :], cos, sin)
            kr = _rope(k_ref[rows, :], cos, sin) * scale
            qb, kb, vb = qr.astype(BF16), kr.astype(BF16), v_ref[rows, :].astype(BF16)
            qhb, ktb = (qr * qh_ref[0]).astype(BF16), (kr * kt_ref[0]).astype(BF16)
            sc = (_dot_nt(qb, kb) * m_ref[0]).astype(BF16)
            o = o_ref[rows, :]
            rstd = _rstd(o)
            on = o * rstd
            gain = gain_ref[...]
            silu, dsilu = _silu_parts(g_ref[rows, :])
            dy = db_ref[rows, :]
            dgain_ref[...] += jnp.sum(dy * silu * on, axis=0, keepdims=True)
            dg = dy * on * gain * dsilu
            don = dy * silu * gain
            dob = (rstd * (don - on * jnp.mean(don * on, axis=-1, keepdims=True))).astype(BF16)
            gn = gstate[...]
            gb = gn.astype(BF16)
            da = (_dot_nt(dob, vb) * m_ref[0]).astype(BF16)
            dq = _dot(da, kb) + _dot_nt(dob, rp_ref[0, ci]) * qh_ref[0]
            dk = _dot_tn(da, qb) + _dot_nt(vb, gb) * kt_ref[0]
            dv = _dot_tn(sc, dob) + _dot(ktb, gb)
            gstate[...] = cd_ref[0] * gn + _dot_tn(qhb, dob)
            d_ref[0, rows, :] = _rope_t(dq, cos, sin).astype(BF16)
            d_ref[1, rows, :] = _rope_t(dk * scale, cos, sin).astype(BF16)
            d_ref[2, rows, :] = dv.astype(BF16)
            d_ref[3, rows, :] = dg.astype(BF16)
            return carry

        lax.fori_loop(0, nck, chunk, 0, unroll=True)

    rev = lambda s: nseg - 1 - s
    col, tab, head, cd, gain = _ret_specs(T, tseg, rev)
    act = pl.BlockSpec((tseg, Hd), lambda h, s: (rev(s), h))
    return _call(
        body, hosted, name=name, grid=(G, nseg),
        in_specs=[col(1), col(2), col(3), col(4), act, act, pl.BlockSpec((1, nck, Hd, Hd), lambda h, s: (h, rev(s), 0, 0)),
                  gain, tab, tab, head, head, head, cd, ANY],
        out_specs=[pl.BlockSpec((4, tseg, Hd), lambda h, s: (0, rev(s), h)), gain],
        out_shape=[jax.ShapeDtypeStruct(dproj.shape, BF16), jax.ShapeDtypeStruct((1, G * Hd), F32)],
        scratch_shapes=[pltpu.VMEM((Hd, Hd), F32)], aliased={14: 0},
        args=[proj, proj, proj, proj, db, o_pre, r_prev, ret_norm, cos2, sin2, intra, k_tail, q_head, chunk_decay, dproj])


def ffn_down_loss(s, w2, h, gain, target, name, hosted=()):
    T, F = s.shape
    D = h.shape[1]
    tm = min(T, 256)

    def body(s_ref, w2_ref, h_ref, g_ref, t_ref, dh_ref, loss_ref, dg_ref):
        @pl.when(pl.program_id(0) == 0)
        def _():
            loss_ref[...] = jnp.zeros_like(loss_ref)
            dg_ref[...] = jnp.zeros_like(dg_ref)

        hh = h_ref[...] + 0.5 * _dot(s_ref[...], w2_ref[...])
        gain_v = g_ref[...]
        err = hh * _rstd(hh) * gain_v - t_ref[...]
        loss_ref[...] += 0.5 * jnp.sum(jnp.mean(err * err, axis=-1, keepdims=True), axis=0, keepdims=True)
        dhn, dg = _rmsnorm_bwd(err * (1.0 / D), hh, gain_v)
        dh_ref[...] = dhn
        dg_ref[...] += jnp.sum(dg, axis=0, keepdims=True)

    row_spec = pl.BlockSpec((tm, D), lambda i: (i, 0))
    vec_spec = pl.BlockSpec((1, D), lambda i: (0, 0))
    return _call(
        body, hosted, name=name, grid=(T // tm,),
        in_specs=[pl.BlockSpec((tm, F), lambda i: (i, 0)), _resident((F, D)), row_spec, vec_spec, row_spec],
        out_specs=[row_spec, pl.BlockSpec((1, LANES), lambda i: (0, 0)), vec_spec],
        out_shape=[jax.ShapeDtypeStruct((T, D), F32), jax.ShapeDtypeStruct((1, LANES), F32), jax.ShapeDtypeStruct((1, D), F32)],
        args=[s, w2, h, gain, target])


def prereduce(grads, recvs, place, name):
    nt = len(grads)
    nsh, R, C = grads[0].shape
    rh = R // 2

    def body(place_ref, *refs):
        for t in range(nt):
            g_ref, r_ref, o_ref, own_ref = refs[2 * t], refs[2 * t + 1], refs[2 * nt + 2 * t], refs[2 * nt + 2 * t + 1]
            piece = (g_ref[...].astype(F32) + r_ref[...].astype(F32)).astype(BF16)
            o_ref[...] = piece

            @pl.when(pl.program_id(0) == place_ref[1])
            def _():
                own_ref[...] = piece

    outs = pl.pallas_call(
        body, name=name,
        grid_spec=pltpu.PrefetchScalarGridSpec(
            num_scalar_prefetch=1, grid=(nsh,),
            in_specs=[pl.BlockSpec((1, rh, C), lambda j, p: (j, p[0], 0)), pl.BlockSpec((1, rh, C), lambda j, p: (j, 0, 0))] * nt,
            out_specs=[pl.BlockSpec((1, rh, C), lambda j, p: (j, 0, 0)),
                       pl.BlockSpec((1, rh, C), lambda j, p: (p[1], p[0], 0))] * nt),
        out_shape=[jax.ShapeDtypeStruct((nsh, rh, C), BF16), jax.ShapeDtypeStruct((nsh, R, C), BF16)] * nt,
        compiler_params=pltpu.CompilerParams(vmem_limit_bytes=VMEM_LIMIT_V7X),
    )(place, *[a for pair in zip(grads, recvs) for a in pair])
    return [(outs[2 * t], outs[2 * t + 1]) for t in range(nt)]


def _adamw(w, g, m, v):
    m = ADAM_B1 * m + (1.0 - ADAM_B1) * g
    v = ADAM_B2 * v + (1.0 - ADAM_B2) * (g * g)
    m_hat = m / (1.0 - ADAM_B1 ** ADAM_STEP)
    v_hat = v / (1.0 - ADAM_B2 ** ADAM_STEP)
    return -ADAM_LR * (m_hat / (jnp.sqrt(v_hat) + ADAM_EPS) + ADAM_WD * w), m, v


def adamw_sharded(tensors, name, hosted=()):
    nt = len(tensors)
    nsh = tensors[0][0].shape[0]
    shapes = [t[0].shape[1:] for t in tensors]

    def fits(steps):
        if any(R % (steps * BF16_TILE_ROWS) for R, _ in shapes):
            return False
        return sum(2 * (R // steps) * -(-C // LANES) * LANES * (nsh * 2 + 7 * 4) for R, C in shapes) <= ADAMW_VMEM_BUDGET

    steps = min(s for s in range(1, min(R for R, _ in shapes) // BF16_TILE_ROWS + 1) if fits(s))

    def body(*refs):
        ins, outs = refs[:4 * nt], refs[4 * nt:]
        for t in range(nt):
            p_ref, w_ref, m_ref, v_ref = ins[4 * t:4 * t + 4]
            g_ref, d_ref, nm_ref, nv_ref = outs[4 * t:4 * t + 4]
            g = p_ref[0].astype(F32)
            for i in range(1, nsh):
                g += p_ref[i].astype(F32)
            g_ref[...] = g
            d_ref[...], nm_ref[...], nv_ref[...] = _adamw(w_ref[...], g, m_ref[...], v_ref[...])

    in_specs, out_specs, out_shape = [], [], []
    for R, C in shapes:
        spec = pl.BlockSpec((R // steps, C), lambda i: (i, 0))
        in_specs += [pl.BlockSpec((nsh, R // steps, C), lambda i: (0, i, 0)), spec, spec, spec]
        out_specs += [spec] * 4
        out_shape += [jax.ShapeDtypeStruct((R, C), F32)] * 4
    return _call(body, hosted, name=name, grid=(steps,), in_specs=in_specs, out_specs=out_specs, out_shape=out_shape,
                 args=[a for tensor in tensors for a in tensor])


def adamw_small(packs, params, loss_packs, name):
    n = len(packs)
    ndev = loss_packs.shape[0]

    def body(*refs):
        p_refs, loss_ref, wmv = refs[:n], refs[n], refs[n + 1:4 * n + 1]
        outs, loss_out = refs[4 * n + 1:8 * n + 1], refs[8 * n + 1]
        total = lambda r: sum((r[i] for i in range(1, ndev)), r[0])
        loss_out[...] = total(loss_ref)
        for k in range(n):
            g = total(p_refs[k])
            outs[4 * k][...] = g
            outs[4 * k + 1][...], outs[4 * k + 2][...], outs[4 * k + 3][...] = _adamw(
                wmv[3 * k][...], g, wmv[3 * k + 1][...], wmv[3 * k + 2][...])

    out_shape = [jax.ShapeDtypeStruct(p[0].shape, F32) for p in params for _ in range(4)]
    outs = pl.pallas_call(body, name=name, out_shape=out_shape + [jax.ShapeDtypeStruct(loss_packs.shape[1:], F32)],
                          compiler_params=pltpu.CompilerParams(vmem_limit_bytes=VMEM_LIMIT_V7X),
                          )(*packs, loss_packs, *[a for p in params for a in p])
    return [outs[4 * k:4 * k + 4] for k in range(n)], outs[4 * n]


BIG = ("ffn1_w1", "ffn1_w3", "ffn1_w2", "w_in", "w_out", "ffn2_w1", "ffn2_w3", "ffn2_w2")
TRANSPOSED = ("ffn1_w1", "ffn1_w3", "ffn2_w1", "ffn2_w3")
SMALL = ("pool_w", "mix_norm", "pool_scale", "ret_norm", "ffn2_norm", "final_norm", "ffn1_norm")
WEIGHTS = ("ffn1_norm", "ffn1_w1", "ffn1_w3", "ffn1_w2", "mix_norm", "w_in", "pool_w", "pool_scale", "ret_norm", "w_out",
           "ffn2_norm", "ffn2_w1", "ffn2_w3", "ffn2_w2", "final_norm")


def kernel(x, ffn1_norm, ffn1_w1, ffn1_w3, ffn1_w2, mix_norm, w_in, pool_w, pool_scale, ret_norm, w_out, ffn2_norm, ffn2_w1, ffn2_w3, ffn2_w2, final_norm, loss_target, m_ffn1_norm, m_ffn1_w1, m_ffn1_w3, m_ffn1_w2, m_mix_norm, m_w_in, m_pool_w, m_pool_scale, m_ret_norm, m_w_out, m_ffn2_norm, m_ffn2_w1, m_ffn2_w3, m_ffn2_w2, m_final_norm, v_ffn1_norm, v_ffn1_w1, v_ffn1_w3, v_ffn1_w2, v_mix_norm, v_w_in, v_pool_w, v_pool_scale, v_ret_norm, v_w_out, v_ffn2_norm, v_ffn2_w1, v_ffn2_w3, v_ffn2_w2, v_final_norm):
    w = dict(ffn1_norm=ffn1_norm, ffn1_w1=ffn1_w1, ffn1_w3=ffn1_w3, ffn1_w2=ffn1_w2, mix_norm=mix_norm, w_in=w_in, pool_w=pool_w,
             pool_scale=pool_scale, ret_norm=ret_norm, w_out=w_out, ffn2_norm=ffn2_norm, ffn2_w1=ffn2_w1, ffn2_w3=ffn2_w3,
             ffn2_w2=ffn2_w2, final_norm=final_norm)
    m = dict(ffn1_norm=m_ffn1_norm, ffn1_w1=m_ffn1_w1, ffn1_w3=m_ffn1_w3, ffn1_w2=m_ffn1_w2, mix_norm=m_mix_norm, w_in=m_w_in,
             pool_w=m_pool_w, pool_scale=m_pool_scale, ret_norm=m_ret_norm, w_out=m_w_out, ffn2_norm=m_ffn2_norm, ffn2_w1=m_ffn2_w1,
             ffn2_w3=m_ffn2_w3, ffn2_w2=m_ffn2_w2, final_norm=m_final_norm)
    v = dict(ffn1_norm=v_ffn1_norm, ffn1_w1=v_ffn1_w1, ffn1_w3=v_ffn1_w3, ffn1_w2=v_ffn1_w2, mix_norm=v_mix_norm, w_in=v_w_in,
             pool_w=v_pool_w, pool_scale=v_pool_scale, ret_norm=v_ret_norm, w_out=v_w_out, ffn2_norm=v_ffn2_norm, ffn2_w1=v_ffn2_w1,
             ffn2_w3=v_ffn2_w3, ffn2_w2=v_ffn2_w2, final_norm=v_final_norm)
    xs, target = x[0], loss_target[0]
    T = xs.shape[0]
    tables = _ret_tables(T)
    place = jnp.stack([lax.axis_index("c"), 2 * lax.axis_index("x") + lax.axis_index("y")]).astype(jnp.int32)
    local = lambda d, k: jnp.transpose(d[k][0]) if k in TRANSPOSED else d[k][0]
    result = lambda o, k: jnp.transpose(o)[None] if k in TRANSPOSED else o[None]
    first = ("ffn1_w1", "ffn1_w3")
    sh = {k: local(w, k).astype(BF16) for k in first}
    gather = lambda *names: [ChipExchange([sh[k] for k in names], False)]
    wg, grad, delta, new_m, new_v = {}, {}, {}, {}, {}

    def update(names, pieces, name, hosted=()):
        outs, extras = adamw_sharded([(p, local(w, k), local(m, k), local(v, k)) for k, p in zip(names, pieces)], name, hosted)
        for t, k in enumerate(names):
            grad[k], delta[k], new_m[k], new_v[k] = [result(o, k) for o in outs[4 * t:4 * t + 4]]
        return extras

    def reduce_in_chip(name, *pairs):
        reduced = prereduce([p for p, _ in pairs], [r for _, r in pairs], place, "prereduce_" + name)
        return reduced[0] if len(pairs) == 1 else reduced

    scatter = lambda *reduced: ChipExchange([r[0] for r in reduced], True, [r[1] for r in reduced])
    whole = lambda k: wg[k].reshape(-1, wg[k].shape[-1])
    sharded = lambda g: g.reshape(N_CHIPS, -1, g.shape[-1])

    later = [k for k in BIG if k not in first]
    casts, ((wg["ffn1_w1"], wg["ffn1_w3"]),) = cast_shards([local(w, k) for k in later], "cast_gather_ffn1", gather(*first))
    sh.update(zip(later, casts))
    (n1, ga1, gb1, s1), ((wg["ffn1_w2"], wg["w_in"]),) = ffn_up(
        xs, ffn1_norm, whole("ffn1_w1"), whole("ffn1_w3"), "ffn1_up", gather("ffn1_w2", "w_in"))
    (h1, u, proj), ((wg["w_out"], wg["ffn2_w1"]),) = ffn_down_mix_in(
        s1, whole("ffn1_w2"), xs, mix_norm, wg["w_in"], "ffn1_down_mix_in", gather("w_out", "ffn2_w1"))
    (pa,), _ = pool_fwd(proj, pool_w[0], pool_scale, "pool_fwd")
    (rb, o_pre, r_prev), ((wg["ffn2_w3"],),) = ret_fwd(proj, ret_norm, tables, "ret_fwd", gather("ffn2_w3"))
    (h2, n2, ga2, gb2, s2), ((wg["ffn2_w2"],),) = ffn_up(
        h1, ffn2_norm, whole("ffn2_w1"), whole("ffn2_w3"), "mix_out_ffn2_up", gather("ffn2_w2"), mixed=(pa, rb, wg["w_out"]))
    (dh3, loss, d_final), _ = ffn_down_loss(s2, whole("ffn2_w2"), h2, final_norm[None], target, "ffn2_down_loss")

    (da2, db2, df2), _ = ffn_bwd_act(dh3, whole("ffn2_w2"), ga2, gb2, "ffn2_bwd_act")
    (g_f2w2,), _ = ffn_dw([s2], df2, 1, "ffn2_dw2")
    g_f2w2 = sharded(g_f2w2)
    (g_f2w1, g_f2w3), ((r_f2w2,),) = ffn_dw([da2, db2], n2, 2, "ffn2_dw13", [SiblingExchange([g_f2w2])])
    g_f2w1, g_f2w3 = sharded(g_f2w1), sharded(g_f2w3)
    p_f2w2 = reduce_in_chip("ffn2_w2", (g_f2w2, r_f2w2))
    (dh2, d_ffn2), ((q_f2w2,), (r_f2w1, r_f2w3)) = ffn_bwd_in(
        da2, db2, whole("ffn2_w1"), whole("ffn2_w3"), h2, ffn2_norm, dh3, "ffn2_bwd_in",
        [scatter(p_f2w2), SiblingExchange([g_f2w1, g_f2w3])])
    p_f2w1, p_f2w3 = reduce_in_chip("ffn2_w13", (g_f2w1, r_f2w1), (g_f2w3, r_f2w3))
    (dpa, drb, g_wout), _ = mix_out_bwd(dh2, wg["w_out"], pa, rb, "mix_out_bwd")
    (dproj, d_pool_w, d_pool_scale), _ = pool_bwd(proj, dpa, pool_w[0], pool_scale, "pool_bwd")
    (dproj, d_ret_norm), ((q_f2w1, q_f2w3), (r_wout,)) = ret_bwd(
        proj, drb, o_pre, r_prev, ret_norm, tables, dproj, "ret_bwd", [scatter(p_f2w1, p_f2w3), SiblingExchange([g_wout])])
    p_wout = reduce_in_chip("w_out", (g_wout, r_wout))
    (g_win,), ((q_wout,),) = mix_dwin(u, dproj, N_CHIPS, "mix_dwin", [scatter(p_wout)])
    (dh1, d_mix), ((r_win,),) = mix_in_bwd(dproj, wg["w_in"], h1, mix_norm, dh2, "mix_in_bwd", [SiblingExchange([g_win])])
    p_win = reduce_in_chip("w_in", (g_win, r_win))
    (da1, db1, df1), ((q_win,),) = ffn_bwd_act(dh1, whole("ffn1_w2"), ga1, gb1, "ffn1_bwd_act", [scatter(p_win)])
    d_small = {"pool_w": d_pool_w.reshape(-1, LANES), "mix_norm": d_mix, "pool_scale": d_pool_scale, "ret_norm": d_ret_norm,
               "ffn2_norm": d_ffn2, "final_norm": d_final}
    (g_f1w1, g_f1w3), (packs,) = ffn_dw([da1, db1], n1, 2, "ffn1_dw13", [AllExchange([d_small[k] for k in SMALL[:-1]] + [loss])])
    g_f1w1, g_f1w3 = sharded(g_f1w1), sharded(g_f1w3)
    (g_f1w2,), ((r_f1w1, r_f1w3),) = ffn_dw([s1], df1, 1, "ffn1_dw2", [SiblingExchange([g_f1w1, g_f1w3])])
    g_f1w2 = sharded(g_f1w2)
    p_f1w1, p_f1w3 = reduce_in_chip("ffn1_w13", (g_f1w1, r_f1w1), (g_f1w3, r_f1w3))
    (dx, d_ffn1), ((q_f1w1, q_f1w3), (r_f1w2,)) = ffn_bwd_in(
        da1, db1, whole("ffn1_w1"), whole("ffn1_w3"), xs, ffn1_norm, dh1, "ffn1_bwd_in",
        [scatter(p_f1w1, p_f1w3), SiblingExchange([g_f1w2])])
    p_f1w2 = reduce_in_chip("ffn1_w2", (g_f1w2, r_f1w2))

    (q_f1w2,), (late,) = update(["w_in", "w_out", "ffn2_w2"], [q_win, q_wout, q_f2w2], "adamw_mix_w2",
                                [scatter(p_f1w2), AllExchange([d_ffn1])])
    update(["ffn2_w1", "ffn2_w3", "ffn1_w1", "ffn1_w3"], [q_f2w1, q_f2w3, q_f1w1, q_f1w3], "adamw_w13")
    update(["ffn1_w2"], [q_f1w2], "adamw_ffn1_w2")
    flat = lambda t, k: t[k].reshape(-1, LANES) if k == "pool_w" else t[k].reshape(1, -1)
    updated, loss_sum = adamw_small(packs[:-1] + [late], [[flat(t, k) for t in (w, m, v)] for k in SMALL], packs[-1], "adamw_small")
    for k, outs in zip(SMALL, updated):
        grad[k], delta[k], new_m[k], new_v[k] = [o.reshape(w[k].shape) for o in outs]
    loss = loss_sum[0, 0]

    return (loss, dx[None], *[grad[k] for k in WEIGHTS], *[delta[k] for k in WEIGHTS],
            *[new_m[k] for k in WEIGHTS], *[new_v[k] for k in WEIGHTS])
```

```python
import math

import jax
import jax.numpy as jnp
import numpy as np
from jax import lax
from jax.experimental import pallas as pl
from jax.experimental.pallas import tpu as pltpu

F32 = jnp.float32
BF16 = jnp.bfloat16

EPS = 1e-6
LANES = 128
BF16_TILE_ROWS = 16
N_CHIPS = 4
N_GROUPS = 4
HEAD_DIM = 128
RET_CHUNK = 128
ROPE_BASE = 10000.0
ADAM_LR, ADAM_B1, ADAM_B2, ADAM_EPS, ADAM_WD, ADAM_STEP = 0.001, 0.9, 0.999, 1e-08, 0.01, 10
VMEM_LIMIT_V7X = 56 * 1024 * 1024
ADAMW_VMEM_BUDGET = 32 * 1024 * 1024
MESH = pl.DeviceIdType.MESH
ANY = pl.BlockSpec(memory_space=pl.ANY)


def _dot(a, b):
    return jnp.dot(a, b, preferred_element_type=F32)


def _dot_nt(a, b):
    return lax.dot_general(a, b, (((1,), (1,)), ((), ())), preferred_element_type=F32)


def _dot_tn(a, b):
    return lax.dot_general(a, b, (((0,), (0,)), ((), ())), preferred_element_type=F32)


def _rstd(h):
    return lax.rsqrt(jnp.mean(h * h, axis=-1, keepdims=True) + EPS)


def _rmsnorm_bwd(dn, h, gain):
    r = _rstd(h)
    nh = h * r
    dnh = dn * gain
    dh = r * (dnh - nh * jnp.mean(dnh * nh, axis=-1, keepdims=True))
    return dh, dn * nh


def _silu_parts(a):
    sig = jax.nn.sigmoid(a)
    silu = a * sig
    return silu, sig + silu * (1.0 - sig)


def _mesh_pos():
    return lax.axis_index("x"), lax.axis_index("y"), lax.axis_index("c")


class ChipExchange:
    def __init__(self, srcs, scatter, placed=()):
        n = len(srcs)
        self.inputs, self.scatter, self.n, self.reach = list(srcs) + list(placed), scatter, n, REACH_CHIPS
        self.aliases = {n + t: t for t in range(n)} if scatter else {}
        self.half_rows = [s.shape[1] if scatter else s.shape[0] // 2 for s in srcs]
        self.out_shape = [jax.ShapeDtypeStruct((N_CHIPS, 2 * rh, s.shape[-1]), s.dtype) for s, rh in zip(srcs, self.half_rows)]
        if scatter:
            self.out_shape += [jax.ShapeDtypeStruct((2, rh // 2, s.shape[-1]), s.dtype) for s, rh in zip(srcs, self.half_rows)]
        dma = pltpu.SemaphoreType.DMA
        self.sems = [dma((4 * n,)), dma((4 * n,)), dma((2 * n,)), dma((2 * n,)), dma((4 * n,)), dma((4 * n,))]

    def _copies(self, src, out, sems):
        hop1_send, hop1_recv, hop2_send, hop2_recv, d2d_send, d2d_recv = sems
        x, y, c = _mesh_pos()
        me, dg = 2 * x + y, 2 * (1 - x) + (1 - y)
        sibling = (x, y, 1 - c)
        n = self.n
        mine, theirs = c, 1 - c

        def nb(a):
            nx, ny = x ^ (1 - a), y ^ a
            return 2 * nx + ny, (nx, ny, c)

        def remote(s, d, send, recv, k, to):
            return pltpu.make_async_remote_copy(src_ref=s, dst_ref=d, send_sem=send.at[k], recv_sem=recv.at[k],
                                                device_id=to, device_id_type=MESH)

        class Copies:
            def slot(_, t, chip, half):
                rh = self.half_rows[t]
                return out[t].at[chip, pl.ds(half * rh, rh), :]

            def quarter(_, t, chip, q):
                qh = self.half_rows[t] // 2
                return out[t].at[chip, pl.ds(mine * 2 * qh + q * qh, qh), :]

            def own_shard(k, t):
                return remote(src[t], out[t].at[me], d2d_send, d2d_recv, 4 * t + 3, sibling)

            def hop1(k, t, a, transit=False):
                rh = self.half_rows[t]
                chip, to = nb(a)
                if transit:
                    piece = src[t].at[dg, pl.ds(a * (rh // 2), rh // 2), :]
                    return remote(piece, out[n + t].at[a], hop1_send, hop1_recv, 4 * t + 2 + a, to)
                piece = src[t].at[chip] if self.scatter else src[t].at[pl.ds(mine * rh, rh), :]
                return remote(piece, k.slot(t, me, mine), hop1_send, hop1_recv, 4 * t + a, to)

            def landed1(k, t, a, transit=False):
                here = out[n + t].at[a] if transit else k.slot(t, nb(a)[0], mine)
                return remote(here, here, hop1_send, hop1_recv, 4 * t + (2 if transit else 0) + a, sibling)

            def hop2(k, t, q):
                origin, to = nb(q)[0], nb(1 - q)[1]
                piece = out[n + t].at[q] if self.scatter else k.quarter(t, origin, q)
                return remote(piece, k.quarter(t, origin, q), hop2_send, hop2_recv, 2 * t + q, to)

            def landed2(k, t, q):
                here = k.quarter(t, dg, q)
                return remote(here, here, hop2_send, hop2_recv, 2 * t + q, sibling)

            def d2d(k, t, p, chip, own=False, arriving=False):
                if arriving:
                    there = k.slot(t, chip, theirs)
                    return remote(there, there, d2d_send, d2d_recv, 4 * t + p, sibling)
                piece = src[t].at[me] if own else k.slot(t, chip, mine)
                return remote(piece, k.slot(t, chip, mine), d2d_send, d2d_recv, 4 * t + p, sibling)

        return Copies(), nb, me, dg, c

    def start(self, src, out, sems):
        k, nb, me, dg, c = self._copies(src, out, sems)
        for t in range(self.n):
            for first in range(2):
                a = first ^ c
                k.hop1(t, a).start()
                if self.scatter:
                    k.hop1(t, a, transit=True).start()
            if self.scatter:
                k.d2d(t, 3, me, own=True).start()
            else:
                k.own_shard(t).start()

    def mid(self, src, out, sems):
        k, nb, me, dg, c = self._copies(src, out, sems)
        for t in range(self.n):
            for first in range(2):
                a = first ^ c
                if self.scatter:
                    k.landed1(t, a, transit=True).wait_recv()
                    k.hop2(t, a).start()
                k.landed1(t, a).wait_recv()
                if not self.scatter:
                    k.hop2(t, a).start()
                k.d2d(t, a, nb(a)[0]).start()

    def finish(self, src, out, sems):
        k, nb, me, dg, c = self._copies(src, out, sems)
        for t in range(self.n):
            for q in range(2):
                k.landed2(t, q).wait_recv()
            k.d2d(t, 2, dg).start()
        for t in range(self.n):
            for a in range(2):
                k.d2d(t, a, nb(a)[0], arriving=True).wait_recv()
            k.d2d(t, 2, dg, arriving=True).wait_recv()
            if self.scatter:
                k.d2d(t, 3, me, arriving=True).wait_recv()
        for t in range(self.n):
            for a in range(2):
                k.hop1(t, a).wait_send()
                if self.scatter:
                    k.hop1(t, a, transit=True).wait_send()
                k.hop2(t, a).wait_send()
                k.d2d(t, a, nb(a)[0]).wait_send()
            k.d2d(t, 2, dg).wait_send()
            if self.scatter:
                k.d2d(t, 3, me, own=True).wait_send()
            else:
                k.own_shard(t).wait()


class SiblingExchange:
    def __init__(self, grads):
        self.inputs, self.n, self.aliases, self.reach = list(grads), len(grads), {}, REACH_SIBLING
        self.half_rows = [g.shape[1] // 2 for g in grads]
        self.out_shape = [jax.ShapeDtypeStruct((g.shape[0], rh, g.shape[2]), g.dtype) for g, rh in zip(grads, self.half_rows)]
        self.sems = [pltpu.SemaphoreType.DMA((self.n,)), pltpu.SemaphoreType.DMA((self.n,))]

    def _plan(self, src, out, sems):
        x, y, c = _mesh_pos()
        return [pltpu.make_async_remote_copy(
            src_ref=src[t].at[:, pl.ds((1 - c) * self.half_rows[t], self.half_rows[t]), :], dst_ref=out[t],
            send_sem=sems[0].at[t], recv_sem=sems[1].at[t], device_id=(x, y, 1 - c), device_id_type=MESH) for t in range(self.n)]

    def start(self, src, out, sems):
        for cp in self._plan(src, out, sems):
            cp.start()

    def mid(self, src, out, sems):
        pass

    def finish(self, src, out, sems):
        for cp in self._plan(src, out, sems):
            cp.wait()


REACH_SIBLING, REACH_CHIPS, REACH_ALL = 0, 1, 2


def _entry_barrier(reach):
    x, y, c = _mesh_pos()
    peers = [(x, y, 1 - c)]
    if reach == REACH_CHIPS:
        peers += [(1 - x, y, c), (x, 1 - y, c)]
    elif reach == REACH_ALL:
        peers = [(x ^ dx, y ^ dy, c ^ dc) for dx in (0, 1) for dy in (0, 1) for dc in (0, 1)][1:]
    barrier = pltpu.get_barrier_semaphore()
    for peer in peers:
        pl.semaphore_signal(barrier, inc=1, device_id=peer, device_id_type=MESH)
    pl.semaphore_wait(barrier, len(peers))


def _call(body, hosted=(), *, name, in_specs, out_specs, out_shape, args, grid=(), scratch_shapes=(), aliased=None):
    n_in, n_out, n_scr = len(in_specs), len(out_specs), len(scratch_shapes)
    total = math.prod(grid)
    mid_step = max(0, (3 * total) // 4 - 1)

    def full(*refs):
        pos = [0]

        def take(k):
            pos[0] += k
            return refs[pos[0] - k:pos[0]]

        ins, h_in = take(n_in), [take(len(h.inputs)) for h in hosted]
        outs, h_out = take(n_out), [take(len(h.out_shape)) for h in hosted]
        scr, h_sem = take(n_scr), [take(len(h.sems)) for h in hosted]
        step = 0
        for axis, size in enumerate(grid):
            step = step * size + pl.program_id(axis)

        def phase(at, method):
            if not hosted:
                return

            def run():
                if method == "start":
                    _entry_barrier(reach)
                for h, s, o, m in zip(hosted, h_in, h_out, h_sem):
                    getattr(h, method)(s, o, m)

            if total == 1:
                run()
            else:
                pl.when(step == at)(run)

        phase(0, "start")
        body(*ins, *outs, *scr)
        phase(mid_step, "mid")
        phase(total - 1, "finish")

    aliases, i0, o0 = dict(aliased or {}), n_in, n_out
    for h in hosted:
        aliases.update({i0 + i: o0 + o for i, o in h.aliases.items()})
        i0, o0 = i0 + len(h.inputs), o0 + len(h.out_shape)
    reach = max((h.reach for h in hosted), default=None)
    params = dict(vmem_limit_bytes=VMEM_LIMIT_V7X)
    if hosted:
        params["collective_id"] = reach
    results = pl.pallas_call(
        full, name=name, grid=grid,
        in_specs=list(in_specs) + [ANY] * (i0 - n_in),
        out_specs=list(out_specs) + [ANY] * (o0 - n_out),
        out_shape=list(out_shape) + [s for h in hosted for s in h.out_shape],
        scratch_shapes=list(scratch_shapes) + [s for h in hosted for s in h.sems],
        input_output_aliases=aliases,
        compiler_params=pltpu.CompilerParams(**params),
    )(*args, *[s for h in hosted for s in h.inputs])
    outs, extras, pos = list(results[:n_out]), [], n_out
    for h in hosted:
        extras.append(list(results[pos:pos + h.n]))
        pos += len(h.out_shape)
    return outs, extras


def cast_shards(shards, name, hosted=()):
    n = len(shards)

    def body(*refs):
        for x_ref, o_ref in zip(refs[:n], refs[n:]):
            o_ref[...] = x_ref[...].astype(BF16)

    whole = lambda s: pl.BlockSpec(s.shape, lambda: (0,) * s.ndim)
    return _call(body, hosted, name=name, in_specs=[whole(s) for s in shards], out_specs=[whole(s) for s in shards],
                 out_shape=[jax.ShapeDtypeStruct(s.shape, BF16) for s in shards], args=list(shards))


class AllExchange:
    def __init__(self, arrays):
        n = len(arrays)
        self.inputs, self.n, self.aliases, self.reach = list(arrays), n, {}, REACH_ALL
        self.out_shape = [jax.ShapeDtypeStruct((2 * N_CHIPS,) + a.shape, a.dtype) for a in arrays]
        self.sems = [pltpu.SemaphoreType.DMA((n,)), pltpu.SemaphoreType.DMA((7 * n,)), pltpu.SemaphoreType.DMA((7 * n,))]

    def _copies(self, src, out, sems):
        local_sem, send_sem, recv_sem = sems
        x, y, c = _mesh_pos()
        me = 4 * x + 2 * y + c
        peers = [(x ^ dx, y ^ dy, c ^ dc) for dx in (0, 1) for dy in (0, 1) for dc in (0, 1)][1:]
        remote = lambda s, d, k, to: pltpu.make_async_remote_copy(
            src_ref=s, dst_ref=d, send_sem=send_sem.at[k], recv_sem=recv_sem.at[k], device_id=to, device_id_type=MESH)
        sends, landed, local = [], [], []
        for t in range(self.n):
            local.append(pltpu.make_async_copy(src[t], out[t].at[me], local_sem.at[t]))
            for p, (px, py, pc) in enumerate(peers):
                sends.append(remote(src[t], out[t].at[me], 7 * t + p, (px, py, pc)))
                here = out[t].at[4 * px + 2 * py + pc]
                landed.append(remote(here, here, 7 * t + p, (px, py, pc)))
        return sends, landed, local

    def start(self, src, out, sems):
        sends, _, local = self._copies(src, out, sems)
        for cp in sends + local:
            cp.start()

    def mid(self, src, out, sems):
        pass

    def finish(self, src, out, sems):
        sends, landed, local = self._copies(src, out, sems)
        for cp in landed:
            cp.wait_recv()
        for cp in sends:
            cp.wait_send()
        for cp in local:
            cp.wait()


MXU_COLS = 256


def _resident(shape):
    return pl.BlockSpec(shape, lambda *_: (0,) * len(shape), pipeline_mode=pl.Buffered(1))


def ffn_up(h, gain, w1, w3, name, hosted=(), mixed=None):
    T, D = h.shape
    F = w1.shape[0]
    tm = min(T, 256)

    def body(*refs):
        if mixed is None:
            h_ref, g_ref, w1_ref, w3_ref, n_ref, ga_ref, gb_ref, s_ref = refs
            hh = h_ref[...]
        else:
            pa_ref, rb_ref, wo_ref, h_ref, g_ref, w1_ref, w3_ref, hh_ref, n_ref, ga_ref, gb_ref, s_ref = refs
            hh = h_ref[...] + _dot(pa_ref[...], wo_ref[0]) + _dot(rb_ref[...], wo_ref[1])
            hh_ref[...] = hh
        n = (hh * _rstd(hh) * g_ref[...]).astype(BF16)
        n_ref[...] = n
        for c in range(0, F, MXU_COLS):
            cols = slice(c, c + MXU_COLS)
            a = _dot_nt(n, w1_ref[cols, :])
            b = _dot_nt(n, w3_ref[cols, :])
            silu, dsilu = _silu_parts(a)
            ga_ref[:, cols] = (b * dsilu).astype(BF16)
            gb_ref[:, cols] = silu.astype(BF16)
            s_ref[:, cols] = (silu * b).astype(BF16)

    act = jax.ShapeDtypeStruct((T, F), BF16)
    act_spec = pl.BlockSpec((tm, F), lambda i: (i, 0))
    row_spec = pl.BlockSpec((tm, D), lambda i: (i, 0))
    in_specs = [row_spec, pl.BlockSpec((1, D), lambda i: (0, 0)), _resident((F, D)), _resident((F, D))]
    out_specs, out_shape, args = [row_spec, act_spec, act_spec, act_spec], [jax.ShapeDtypeStruct((T, D), BF16), act, act, act], [h, gain, w1, w3]
    if mixed is not None:
        pa, rb, woutg = mixed
        W = pa.shape[1]
        in_specs = [pl.BlockSpec((tm, W), lambda i: (i, 0))] * 2 + [_resident((2, W, D))] + in_specs
        out_specs, out_shape = [row_spec] + out_specs, [jax.ShapeDtypeStruct((T, D), F32)] + out_shape
        args = [pa, rb, woutg.reshape(2, W, D)] + args
    return _call(body, hosted, name=name, grid=(T // tm,), in_specs=in_specs, out_specs=out_specs, out_shape=out_shape, args=args)


def ffn_bwd_act(dh, w2, ga, gb, name, hosted=()):
    T, D = dh.shape
    F = w2.shape[0]
    tm = min(T, 512)

    def body(dh_ref, w2_ref, ga_ref, gb_ref, da_ref, db_ref, df_ref):
        df = (0.5 * dh_ref[...]).astype(BF16)
        df_ref[...] = df
        for c in range(0, F, MXU_COLS):
            cols = slice(c, c + MXU_COLS)
            ds = _dot_nt(df, w2_ref[cols, :])
            da_ref[:, cols] = (ds * ga_ref[:, cols].astype(F32)).astype(BF16)
            db_ref[:, cols] = (ds * gb_ref[:, cols].astype(F32)).astype(BF16)

    act = jax.ShapeDtypeStruct((T, F), BF16)
    act_spec = pl.BlockSpec((tm, F), lambda i: (i, 0))
    row_spec = pl.BlockSpec((tm, D), lambda i: (i, 0))
    return _call(
        body, hosted, name=name, grid=(T // tm,),
        in_specs=[row_spec, _resident((F, D)), act_spec, act_spec],
        out_specs=[act_spec, act_spec, row_spec],
        out_shape=[act, act, jax.ShapeDtypeStruct((T, D), BF16)],
        args=[dh, w2, ga, gb])


def ffn_dw(xs, y, halves, name, hosted=()):
    T, F = xs[0].shape
    D = y.shape[1]
    nx, fh = len(xs), F // halves
    tk = min(T, 512)
    nk = T // tk

    def body(*refs):
        y_ref, x_refs, o_refs, accs = refs[0], refs[1:1 + nx], refs[1 + nx:1 + 2 * nx], refs[1 + 2 * nx:]
        k = pl.program_id(1)

        @pl.when(k == 0)
        def _():
            for acc in accs:
                acc[...] = jnp.zeros_like(acc)

        yy = y_ref[...]
        for x_ref, acc in zip(x_refs, accs):
            acc[...] += _dot_tn(x_ref[...], yy)

        @pl.when(k == nk - 1)
        def _():
            for o_ref, acc in zip(o_refs, accs):
                o_ref[...] = acc[...].astype(BF16)

    out = jax.ShapeDtypeStruct((F, D), BF16)
    return _call(
        body, hosted, name=name, grid=(halves, nk),
        in_specs=[pl.BlockSpec((tk, D), lambda j, k: (k, 0))] + [pl.BlockSpec((tk, fh), lambda j, k: (k, j))] * nx,
        out_specs=[pl.BlockSpec((fh, D), lambda j, k: (j, 0))] * nx,
        out_shape=[out] * nx,
        scratch_shapes=[pltpu.VMEM((fh, D), F32)] * nx,
        args=[y] + list(xs))


def ffn_bwd_in(da, db, w1, w3, h, gain, dh, name, hosted=()):
    T, F = da.shape
    D = h.shape[1]
    tm = min(T, 256)

    def body(da_ref, db_ref, w1_ref, w3_ref, h_ref, g_ref, dh_ref, o_ref, dg_ref):
        dn = _dot(da_ref[...], w1_ref[...]) + _dot(db_ref[...], w3_ref[...])
        dhn, dg = _rmsnorm_bwd(dn, h_ref[...], g_ref[...])
        o_ref[...] = dh_ref[...] + dhn

        @pl.when(pl.program_id(0) == 0)
        def _():
            dg_ref[...] = jnp.zeros_like(dg_ref)

        dg_ref[...] += jnp.sum(dg, axis=0, keepdims=True)

    act_spec = pl.BlockSpec((tm, F), lambda i: (i, 0))
    row_spec = pl.BlockSpec((tm, D), lambda i: (i, 0))
    vec_spec = pl.BlockSpec((1, D), lambda i: (0, 0))
    return _call(
        body, hosted, name=name, grid=(T // tm,),
        in_specs=[act_spec, act_spec, _resident((F, D)), _resident((F, D)), row_spec, vec_spec, row_spec],
        out_specs=[row_spec, vec_spec],
        out_shape=[jax.ShapeDtypeStruct((T, D), F32), jax.ShapeDtypeStruct((1, D), F32)],
        args=[da, db, w1, w3, h, gain, dh])


def ffn_down_mix_in(s, w2, h, gain, wing, name, hosted=()):
    T, F = s.shape
    D = h.shape[1]
    nsh, _, Cs = wing.shape
    tm = min(T, 512)

    def body(s_ref, w2_ref, h_ref, g_ref, w_ref, hh_ref, u_ref, p_ref):
        hh = h_ref[...] + 0.5 * _dot(s_ref[...], w2_ref[...])
        hh_ref[...] = hh
        u = (hh * _rstd(hh) * g_ref[...]).astype(BF16)
        u_ref[...] = u
        for j in range(nsh):
            p_ref[:, j * Cs:(j + 1) * Cs] = _dot(u, w_ref[j])

    row_spec = pl.BlockSpec((tm, D), lambda i: (i, 0))
    return _call(
        body, hosted, name=name, grid=(T // tm,),
        in_specs=[pl.BlockSpec((tm, F), lambda i: (i, 0)), _resident((F, D)), row_spec, pl.BlockSpec((1, D), lambda i: (0, 0)),
                  _resident((nsh, D, Cs))],
        out_specs=[row_spec, row_spec, pl.BlockSpec((tm, nsh * Cs), lambda i: (i, 0))],
        out_shape=[jax.ShapeDtypeStruct((T, D), F32), jax.ShapeDtypeStruct((T, D), BF16), jax.ShapeDtypeStruct((T, nsh * Cs), F32)],
        args=[s, w2, h, gain, wing])


def mix_out_bwd(dh, woutg, a, b, name, hosted=()):
    T, D = dh.shape
    W = a.shape[1]
    nsh, Rs, _ = woutg.shape
    wout = woutg.reshape(2, W, D)
    tk = min(T, 512)
    nk = T // tk

    def body(dh_ref, w_ref, a_ref, b_ref, da_ref, db_ref, dw_ref, acc):
        k = pl.program_id(0)

        @pl.when(k == 0)
        def _():
            acc[...] = jnp.zeros_like(acc)

        dhb = dh_ref[...].astype(BF16)
        da_ref[...] = _dot_nt(dhb, w_ref[0])
        db_ref[...] = _dot_nt(dhb, w_ref[1])
        acc[0:W, :] += _dot_tn(a_ref[...], dhb)
        acc[W:2 * W, :] += _dot_tn(b_ref[...], dhb)

        @pl.when(k == nk - 1)
        def _():
            for j in range(nsh):
                dw_ref[j] = acc[j * Rs:(j + 1) * Rs, :].astype(BF16)

    return _call(
        body, hosted, name=name, grid=(nk,),
        in_specs=[pl.BlockSpec((tk, D), lambda k: (k, 0)), pl.BlockSpec((2, W, D), lambda k: (0, 0, 0)),
                  pl.BlockSpec((tk, W), lambda k: (k, 0)), pl.BlockSpec((tk, W), lambda k: (k, 0))],
        out_specs=[pl.BlockSpec((tk, W), lambda k: (k, 0)), pl.BlockSpec((tk, W), lambda k: (k, 0)),
                   pl.BlockSpec((nsh, Rs, D), lambda k: (0, 0, 0))],
        out_shape=[jax.ShapeDtypeStruct((T, W), F32), jax.ShapeDtypeStruct((T, W), F32),
                   jax.ShapeDtypeStruct((nsh, Rs, D), BF16)],
        scratch_shapes=[pltpu.VMEM((2 * W, D), F32)],
        args=[dh, wout, a, b])


def _dproj_block(g):
    return (g // N_GROUPS + N_GROUPS) % (N_GROUPS + 1), g % N_GROUPS


def mix_dwin(u, dproj, nsh, name, hosted=()):
    T, D = u.shape
    Hd = HEAD_DIM
    slabs, _, width = dproj.shape
    blocks = slabs * width // Hd
    Cs = blocks * Hd // nsh
    tk = min(T, 512)
    nk = T // tk

    def body(u_ref, d_ref, o_ref, acc):
        k = pl.program_id(0)

        @pl.when(k == 0)
        def _():
            acc[...] = jnp.zeros_like(acc)

        where = [_dproj_block(g) for g in range(blocks)]
        d = jnp.concatenate([d_ref[slab, :, col * Hd:(col + 1) * Hd] for slab, col in where], axis=1)
        acc[...] += _dot_tn(u_ref[...], d)

        @pl.when(k == nk - 1)
        def _():
            for j in range(nsh):
                o_ref[j] = acc[:, j * Cs:(j + 1) * Cs].astype(BF16)

    return _call(
        body, hosted, name=name, grid=(nk,),
        in_specs=[pl.BlockSpec((tk, D), lambda k: (k, 0)), pl.BlockSpec((slabs, tk, width), lambda k: (0, k, 0))],
        out_specs=[pl.BlockSpec((nsh, D, Cs), lambda k: (0, 0, 0))],
        out_shape=[jax.ShapeDtypeStruct((nsh, D, Cs), BF16)],
        scratch_shapes=[pltpu.VMEM((D, blocks * Hd), F32)],
        args=[u, dproj])


def mix_in_bwd(dproj, wing, h, gain, dh, name, hosted=()):
    T, D = h.shape
    nsh, _, Cs = wing.shape
    Hd = HEAD_DIM
    per = Cs // Hd
    tm = min(T, 512)

    def body(d_ref, w_ref, h_ref, g_ref, dh_ref, o_ref, dg_ref):
        def shard(j):
            blocks = [_dproj_block(per * j + i) for i in range(per)]
            return jnp.concatenate([d_ref[slab, :, col * Hd:(col + 1) * Hd] for slab, col in blocks], axis=1)

        du = _dot_nt(shard(0), w_ref[0])
        for j in range(1, nsh):
            du += _dot_nt(shard(j), w_ref[j])
        dhn, dg = _rmsnorm_bwd(du, h_ref[...], g_ref[...])
        o_ref[...] = dh_ref[...] + dhn

        @pl.when(pl.program_id(0) == 0)
        def _():
            dg_ref[...] = jnp.zeros_like(dg_ref)

        dg_ref[...] += jnp.sum(dg, axis=0, keepdims=True)

    row_spec = pl.BlockSpec((tm, D), lambda i: (i, 0))
    vec_spec = pl.BlockSpec((1, D), lambda i: (0, 0))
    return _call(
        body, hosted, name=name, grid=(T // tm,),
        in_specs=[pl.BlockSpec((dproj.shape[0], tm, dproj.shape[2]), lambda i: (0, i, 0)),
                  pl.BlockSpec((nsh, D, Cs), lambda i: (0, 0, 0)), row_spec, vec_spec, row_spec],
        out_specs=[row_spec, vec_spec],
        out_shape=[jax.ShapeDtypeStruct((T, D), F32), jax.ShapeDtypeStruct((1, D), F32)],
        args=[dproj, wing, h, gain, dh])


POOL_WINDOWS = (2, 4, 8, 16)


def _pool_window(x, window, T, trailing):
    rows = lax.broadcasted_iota(jnp.int32, x.shape, 0)
    s, k = x, 1
    while k < window:
        if trailing:
            s = s + jnp.where(rows >= k, pltpu.roll(s, k, 0), 0.0)
        else:
            s = s + jnp.where(rows < T - k, pltpu.roll(s, T - k, 0), 0.0)
        k *= 2
    return s


def _pool_count(window, shape):
    rows = lax.broadcasted_iota(jnp.int32, shape, 0)
    return jnp.minimum(rows + 1, window).astype(F32)


def _per_group(work):
    for group, window in enumerate(POOL_WINDOWS):
        pl.when(pl.program_id(0) == group)(lambda window=window: work(window))


def pool_fwd(proj, pool_w, pool_scale, name, hosted=()):
    T = proj.shape[0]
    Hd = HEAD_DIM

    def body(x_ref, w_ref, sc_ref, a_ref):
        def work(window):
            x = x_ref[...]
            pooled = _pool_window(x, window, T, True) / _pool_count(window, x.shape) - x
            a_ref[...] = (_dot(pooled.astype(BF16), w_ref[0].astype(BF16)) * sc_ref[...]).astype(BF16)

        _per_group(work)

    return _call(
        body, hosted, name=name, grid=(N_GROUPS,),
        in_specs=[pl.BlockSpec((T, Hd), lambda g: (0, g)), pl.BlockSpec((1, Hd, Hd), lambda g: (g, 0, 0)),
                  pl.BlockSpec((1, Hd), lambda g: (0, g))],
        out_specs=[pl.BlockSpec((T, Hd), lambda g: (0, g))],
        out_shape=[jax.ShapeDtypeStruct((T, N_GROUPS * Hd), BF16)],
        args=[proj, pool_w, pool_scale])


def pool_bwd(proj, da, pool_w, pool_scale, name, hosted=()):
    T = proj.shape[0]
    Hd = HEAD_DIM

    def body(x_ref, da_ref, w_ref, sc_ref, dx_ref, dw_ref, dsc_ref):
        def work(window):
            x = x_ref[...]
            cnt = _pool_count(window, x.shape)
            pooled = (_pool_window(x, window, T, True) / cnt - x).astype(BF16)
            wb = w_ref[0].astype(BF16)
            dav = da_ref[...]
            dsc_ref[...] = jnp.sum(dav * _dot(pooled, wb), axis=0, keepdims=True)
            dout = (dav * sc_ref[...]).astype(BF16)
            dw_ref[0] = _dot_tn(pooled, dout)
            dpooled = _dot_nt(dout, wb)
            dx_ref[0] = (_pool_window(dpooled / cnt, window, T, False) - dpooled).astype(BF16)

        _per_group(work)

    col_spec = pl.BlockSpec((T, Hd), lambda g: (0, g))
    return _call(
        body, hosted, name=name, grid=(N_GROUPS,),
        in_specs=[col_spec, col_spec, pl.BlockSpec((1, Hd, Hd), lambda g: (g, 0, 0)), pl.BlockSpec((1, Hd), lambda g: (0, g))],
        out_specs=[pl.BlockSpec((1, T, Hd), lambda g: (N_GROUPS, 0, g)), pl.BlockSpec((1, Hd, Hd), lambda g: (g, 0, 0)),
                   pl.BlockSpec((1, Hd), lambda g: (0, g))],
        out_shape=[jax.ShapeDtypeStruct((N_GROUPS + 1, T, N_GROUPS * Hd), BF16), jax.ShapeDtypeStruct((N_GROUPS, Hd, Hd), F32),
                   jax.ShapeDtypeStruct((1, N_GROUPS * Hd), F32)],
        args=[proj, da, pool_w, pool_scale])


def _ret_tables(T):
    Hd, C, f32 = HEAD_DIM, RET_CHUNK, np.float32
    inv_freq = (1.0 / (ROPE_BASE ** (np.arange(0, Hd, 2, dtype=np.float64) / Hd))).astype(f32)
    ang = np.arange(T, dtype=f32)[:, None] * inv_freq[None, :]
    cos, sin = np.cos(ang), np.sin(ang)
    cos2 = np.concatenate([cos, cos], axis=-1)
    sin2 = np.concatenate([-sin, sin], axis=-1)
    log_gamma = np.log1p(-np.exp2(f32(-5.0) - np.arange(N_GROUPS, dtype=f32)))
    pos = np.arange(C, dtype=f32)
    rel = pos[:, None] - pos[None, :]
    intra = np.where(rel[None] >= 0, np.exp(log_gamma[:, None, None] * np.maximum(rel, f32(0.0))[None]), f32(0.0))
    k_tail = np.exp(log_gamma[:, None] * (f32(C - 1) - pos)[None, :])
    q_head = np.exp(log_gamma[:, None] * (pos + f32(1.0))[None, :])
    chunk_decay = np.exp(log_gamma * f32(C))
    wide = lambda t: np.broadcast_to(t[:, :, None], (N_GROUPS, C, Hd))
    tables = cos2, sin2, intra, wide(k_tail), wide(q_head), np.broadcast_to(chunk_decay[:, None, None], (N_GROUPS, 1, Hd))
    assert all(t.dtype == f32 for t in tables)
    return tuple(jnp.asarray(t) for t in tables)


def _rope(x, cos2, sin2):
    return x * cos2 + pltpu.roll(x, HEAD_DIM // 2, 1) * sin2


def _rope_t(d, cos2, sin2):
    return d * cos2 + pltpu.roll(d * sin2, HEAD_DIM // 2, 1)


def _ret_specs(T, tseg, seg_of):
    Hd, G = HEAD_DIM, N_GROUPS
    col = lambda kind: pl.BlockSpec((tseg, Hd), lambda h, s: (seg_of(s), G * kind + h))
    tab = pl.BlockSpec((T, Hd), lambda h, s: (0, 0))
    head = pl.BlockSpec((1, RET_CHUNK, Hd), lambda h, s: (h, 0, 0))
    cd = pl.BlockSpec((1, 1, Hd), lambda h, s: (h, 0, 0))
    gain = pl.BlockSpec((1, Hd), lambda h, s: (0, h))
    return col, tab, head, cd, gain


def ret_fwd(proj, ret_norm, tables, name, hosted=()):
    T = proj.shape[0]
    Hd, C, G = HEAD_DIM, RET_CHUNK, N_GROUPS
    tseg = min(T, 2048)
    nseg, nck = T // tseg, tseg // C
    scale = Hd ** -0.5
    cos2, sin2, intra, k_tail, q_head, chunk_decay = tables

    def body(q_ref, k_ref, v_ref, g_ref, gain_ref, cos_ref, sin_ref, m_ref, kt_ref, qh_ref, cd_ref,
             b_ref, o_ref, rp_ref, state):
        @pl.when(pl.program_id(1) == 0)
        def _():
            state[...] = jnp.zeros_like(state)

        def chunk(ci, carry):
            rows = pl.ds(pl.multiple_of(ci * C, C), C)
            at = pl.ds(pl.multiple_of(pl.program_id(1) * tseg + ci * C, C), C)
            cos, sin = cos_ref[at, :], sin_ref[at, :]
            qr = _rope(q_ref[rows, :], cos, sin)
            kr = _rope(k_ref[rows, :], cos, sin) * scale
            qb, kb, vb = qr.astype(BF16), kr.astype(BF16), v_ref[rows, :].astype(BF16)
            r = state[...]
            rp_ref[0, ci] = r.astype(BF16)
            sc = _dot_nt(qb, kb) * m_ref[0]
            o = _dot(sc.astype(BF16), vb) + _dot((qr * qh_ref[0]).astype(BF16), r.astype(BF16))
            state[...] = cd_ref[0] * r + _dot_tn((kr * kt_ref[0]).astype(BF16), vb)
            o_ref[rows, :] = o
            on = o * _rstd(o)
            b_ref[rows, :] = (jax.nn.silu(g_ref[rows, :]) * (on * gain_ref[...])).astype(BF16)
            return carry

        lax.fori_loop(0, nck, chunk, 0, unroll=True)

    col, tab, head, cd, gain = _ret_specs(T, tseg, lambda s: s)
    out_col = pl.BlockSpec((tseg, Hd), lambda h, s: (s, h))
    return _call(
        body, hosted, name=name, grid=(G, nseg),
        in_specs=[col(1), col(2), col(3), col(4), gain, tab, tab, head, head, head, cd],
        out_specs=[out_col, out_col, pl.BlockSpec((1, nck, Hd, Hd), lambda h, s: (h, s, 0, 0))],
        out_shape=[jax.ShapeDtypeStruct((T, G * Hd), BF16), jax.ShapeDtypeStruct((T, G * Hd), F32),
                   jax.ShapeDtypeStruct((G, T // C, Hd, Hd), BF16)],
        scratch_shapes=[pltpu.VMEM((Hd, Hd), F32)],
        args=[proj, proj, proj, proj, ret_norm, cos2, sin2, intra, k_tail, q_head, chunk_decay])


def ret_bwd(proj, db, o_pre, r_prev, ret_norm, tables, dproj, name, hosted=()):
    T = proj.shape[0]
    Hd, C, G = HEAD_DIM, RET_CHUNK, N_GROUPS
    tseg = min(T, 2048)
    nseg, nck = T // tseg, tseg // C
    scale = Hd ** -0.5
    cos2, sin2, intra, k_tail, q_head, chunk_decay = tables

    def body(q_ref, k_ref, v_ref, g_ref, db_ref, o_ref, rp_ref, gain_ref, cos_ref, sin_ref, m_ref, kt_ref, qh_ref, cd_ref,
             _, d_ref, dgain_ref, gstate):
        @pl.when(pl.program_id(1) == 0)
        def _():
            gstate[...] = jnp.zeros_like(gstate)
            dgain_ref[...] = jnp.zeros_like(dgain_ref)

        def chunk(t, carry):
            ci = nck - 1 - t
            rows = pl.ds(pl.multiple_of(ci * C, C), C)
            at = pl.ds(pl.multiple_of((nseg - 1 - pl.program_id(1)) * tseg + ci * C, C), C)
            cos, sin = cos_ref[at, :], sin_ref[at, :]
            qr = _rope(q_ref[rows, :], cos, sin)
            kr = _rope(k_ref[rows, :], cos, sin) * scale
            qb, kb, vb = qr.astype(BF16), kr.astype(BF16), v_ref[rows, :].astype(BF16)
            qhb, ktb = (qr * qh_ref[0]).astype(BF16), (kr * kt_ref[0]).astype(BF16)
            sc = (_dot_nt(qb, kb) * m_ref[0]).astype(BF16)
            o = o_ref[rows, :]
            rstd = _rstd(o)
            on = o * rstd
            gain = gain_ref[...]
            silu, dsilu = _silu_parts(g_ref[rows, :])
            dy = db_ref[rows, :]
            dgain_ref[...] += jnp.sum(dy * silu * on, axis=0, keepdims=True)
            dg = dy * on * gain * dsilu
            don = dy * silu * gain
            dob = (rstd * (don - on * jnp.mean(don * on, axis=-1, keepdims=True))).astype(BF16)
            gn = gstate[...]
            gb = gn.astype(BF16)
            da = (_dot_nt(dob, vb) * m_ref[0]).astype(BF16)
            dq = _dot(da, kb) + _dot_nt(dob, rp_ref[0, ci]) * qh_ref[0]
            dk = _dot_tn(da, qb) + _dot_nt(vb, gb) * kt_ref[0]
            dv = _dot_tn(sc, dob) + _dot(ktb, gb)
            gstate[...] = cd_ref[0] * gn + _dot_tn(qhb, dob)
            d_ref[0, rows, :] = _rope_t(dq, cos, sin).astype(BF16)
            d_ref[1, rows, :] = _rope_t(dk * scale, cos, sin).astype(BF16)
            d_ref[2, rows, :] = dv.astype(BF16)
            d_ref[3, rows, :] = dg.astype(BF16)
            return carry

        lax.fori_loop(0, nck, chunk, 0, unroll=True)

    rev = lambda s: nseg - 1 - s
    col, tab, head, cd, gain = _ret_specs(T, tseg, rev)
    act = pl.BlockSpec((tseg, Hd), lambda h, s: (rev(s), h))
    return _call(
        body, hosted, name=name, grid=(G, nseg),
        in_specs=[col(1), col(2), col(3), col(4), act, act, pl.BlockSpec((1, nck, Hd, Hd), lambda h, s: (h, rev(s), 0, 0)),
                  gain, tab, tab, head, head, head, cd, ANY],
        out_specs=[pl.BlockSpec((4, tseg, Hd), lambda h, s: (0, rev(s), h)), gain],
        out_shape=[jax.ShapeDtypeStruct(dproj.shape, BF16), jax.ShapeDtypeStruct((1, G * Hd), F32)],
        scratch_shapes=[pltpu.VMEM((Hd, Hd), F32)], aliased={14: 0},
        args=[proj, proj, proj, proj, db, o_pre, r_prev, ret_norm, cos2, sin2, intra, k_tail, q_head, chunk_decay, dproj])


def ffn_down_loss(s, w2, h, gain, target, name, hosted=()):
    T, F = s.shape
    D = h.shape[1]
    tm = min(T, 512)

    def body(s_ref, w2_ref, h_ref, g_ref, t_ref, dh_ref, loss_ref, dg_ref):
        @pl.when(pl.program_id(0) == 0)
        def _():
            loss_ref[...] = jnp.zeros_like(loss_ref)
            dg_ref[...] = jnp.zeros_like(dg_ref)

        hh = h_ref[...] + 0.5 * _dot(s_ref[...], w2_ref[...])
        gain_v = g_ref[...]
        err = hh * _rstd(hh) * gain_v - t_ref[...]
        loss_ref[...] += 0.5 * jnp.sum(jnp.mean(err * err, axis=-1, keepdims=True), axis=0, keepdims=True)
        dhn, dg = _rmsnorm_bwd(err * (1.0 / D), hh, gain_v)
        dh_ref[...] = dhn
        dg_ref[...] += jnp.sum(dg, axis=0, keepdims=True)

    row_spec = pl.BlockSpec((tm, D), lambda i: (i, 0))
    vec_spec = pl.BlockSpec((1, D), lambda i: (0, 0))
    return _call(
        body, hosted, name=name, grid=(T // tm,),
        in_specs=[pl.BlockSpec((tm, F), lambda i: (i, 0)), _resident((F, D)), row_spec, vec_spec, row_spec],
        out_specs=[row_spec, pl.BlockSpec((1, LANES), lambda i: (0, 0)), vec_spec],
        out_shape=[jax.ShapeDtypeStruct((T, D), F32), jax.ShapeDtypeStruct((1, LANES), F32), jax.ShapeDtypeStruct((1, D), F32)],
        args=[s, w2, h, gain, target])


def prereduce(grads, recvs, place, name):
    nt = len(grads)
    nsh, R, C = grads[0].shape
    rh = R // 2

    def body(place_ref, *refs):
        for t in range(nt):
            g_ref, r_ref, o_ref, own_ref = refs[2 * t], refs[2 * t + 1], refs[2 * nt + 2 * t], refs[2 * nt + 2 * t + 1]
            piece = (g_ref[...].astype(F32) + r_ref[...].astype(F32)).astype(BF16)
            o_ref[...] = piece

            @pl.when(pl.program_id(0) == place_ref[1])
            def _():
                own_ref[...] = piece

    outs = pl.pallas_call(
        body, name=name,
        grid_spec=pltpu.PrefetchScalarGridSpec(
            num_scalar_prefetch=1, grid=(nsh,),
            in_specs=[pl.BlockSpec((1, rh, C), lambda j, p: (j, p[0], 0)), pl.BlockSpec((1, rh, C), lambda j, p: (j, 0, 0))] * nt,
            out_specs=[pl.BlockSpec((1, rh, C), lambda j, p: (j, 0, 0)),
                       pl.BlockSpec((1, rh, C), lambda j, p: (p[1], p[0], 0))] * nt),
        out_shape=[jax.ShapeDtypeStruct((nsh, rh, C), BF16), jax.ShapeDtypeStruct((nsh, R, C), BF16)] * nt,
        compiler_params=pltpu.CompilerParams(vmem_limit_bytes=VMEM_LIMIT_V7X),
    )(place, *[a for pair in zip(grads, recvs) for a in pair])
    return [(outs[2 * t], outs[2 * t + 1]) for t in range(nt)]


def _adamw(w, g, m, v):
    m = ADAM_B1 * m + (1.0 - ADAM_B1) * g
    v = ADAM_B2 * v + (1.0 - ADAM_B2) * (g * g)
    m_hat = m / (1.0 - ADAM_B1 ** ADAM_STEP)
    v_hat = v / (1.0 - ADAM_B2 ** ADAM_STEP)
    return -ADAM_LR * (m_hat / (jnp.sqrt(v_hat) + ADAM_EPS) + ADAM_WD * w), m, v


def adamw_sharded(tensors, name, hosted=()):
    nt = len(tensors)
    nsh = tensors[0][0].shape[0]
    shapes = [t[0].shape[1:] for t in tensors]

    def fits(steps):
        if any(R % (steps * BF16_TILE_ROWS) for R, _ in shapes):
            return False
        return sum(2 * (R // steps) * -(-C // LANES) * LANES * (nsh * 2 + 7 * 4) for R, C in shapes) <= ADAMW_VMEM_BUDGET

    steps = min(s for s in range(1, min(R for R, _ in shapes) // BF16_TILE_ROWS + 1) if fits(s))

    def body(*refs):
        ins, outs = refs[:4 * nt], refs[4 * nt:]
        for t in range(nt):
            p_ref, w_ref, m_ref, v_ref = ins[4 * t:4 * t + 4]
            g_ref, d_ref, nm_ref, nv_ref = outs[4 * t:4 * t + 4]
            g = p_ref[0].astype(F32)
            for i in range(1, nsh):
                g += p_ref[i].astype(F32)
            g_ref[...] = g
            d_ref[...], nm_ref[...], nv_ref[...] = _adamw(w_ref[...], g, m_ref[...], v_ref[...])

    in_specs, out_specs, out_shape = [], [], []
    for R, C in shapes:
        spec = pl.BlockSpec((R // steps, C), lambda i: (i, 0))
        in_specs += [pl.BlockSpec((nsh, R // steps, C), lambda i: (0, i, 0)), spec, spec, spec]
        out_specs += [spec] * 4
        out_shape += [jax.ShapeDtypeStruct((R, C), F32)] * 4
    return _call(body, hosted, name=name, grid=(steps,), in_specs=in_specs, out_specs=out_specs, out_shape=out_shape,
                 args=[a for tensor in tensors for a in tensor])


def adamw_small(packs, params, loss_packs, name):
    n = len(packs)
    ndev = loss_packs.shape[0]

    def body(*refs):
        p_refs, loss_ref, wmv = refs[:n], refs[n], refs[n + 1:4 * n + 1]
        outs, loss_out = refs[4 * n + 1:8 * n + 1], refs[8 * n + 1]
        total = lambda r: sum((r[i] for i in range(1, ndev)), r[0])
        loss_out[...] = total(loss_ref)
        for k in range(n):
            g = total(p_refs[k])
            outs[4 * k][...] = g
            outs[4 * k + 1][...], outs[4 * k + 2][...], outs[4 * k + 3][...] = _adamw(
                wmv[3 * k][...], g, wmv[3 * k + 1][...], wmv[3 * k + 2][...])

    out_shape = [jax.ShapeDtypeStruct(p[0].shape, F32) for p in params for _ in range(4)]
    outs = pl.pallas_call(body, name=name, out_shape=out_shape + [jax.ShapeDtypeStruct(loss_packs.shape[1:], F32)],
                          compiler_params=pltpu.CompilerParams(vmem_limit_bytes=VMEM_LIMIT_V7X),
                          )(*packs, loss_packs, *[a for p in params for a in p])
    return [outs[4 * k:4 * k + 4] for k in range(n)], outs[4 * n]


BIG = ("ffn1_w1", "ffn1_w3", "ffn1_w2", "w_in", "w_out", "ffn2_w1", "ffn2_w3", "ffn2_w2")
TRANSPOSED = ("ffn1_w1", "ffn1_w3", "ffn2_w1", "ffn2_w3")
SMALL = ("pool_w", "mix_norm", "pool_scale", "ret_norm", "ffn2_norm", "final_norm", "ffn1_norm")
WEIGHTS = ("ffn1_norm", "ffn1_w1", "ffn1_w3", "ffn1_w2", "mix_norm", "w_in", "pool_w", "pool_scale", "ret_norm", "w_out",
           "ffn2_norm", "ffn2_w1", "ffn2_w3", "ffn2_w2", "final_norm")


def kernel(x, ffn1_norm, ffn1_w1, ffn1_w3, ffn1_w2, mix_norm, w_in, pool_w, pool_scale, ret_norm, w_out, ffn2_norm, ffn2_w1, ffn2_w3, ffn2_w2, final_norm, loss_target, m_ffn1_norm, m_ffn1_w1, m_ffn1_w3, m_ffn1_w2, m_mix_norm, m_w_in, m_pool_w, m_pool_scale, m_ret_norm, m_w_out, m_ffn2_norm, m_ffn2_w1, m_ffn2_w3, m_ffn2_w2, m_final_norm, v_ffn1_norm, v_ffn1_w1, v_ffn1_w3, v_ffn1_w2, v_mix_norm, v_w_in, v_pool_w, v_pool_scale, v_ret_norm, v_w_out, v_ffn2_norm, v_ffn2_w1, v_ffn2_w3, v_ffn2_w2, v_final_norm):
    w = dict(ffn1_norm=ffn1_norm, ffn1_w1=ffn1_w1, ffn1_w3=ffn1_w3, ffn1_w2=ffn1_w2, mix_norm=mix_norm, w_in=w_in, pool_w=pool_w,
             pool_scale=pool_scale, ret_norm=ret_norm, w_out=w_out, ffn2_norm=ffn2_norm, ffn2_w1=ffn2_w1, ffn2_w3=ffn2_w3,
             ffn2_w2=ffn2_w2, final_norm=final_norm)
    m = dict(ffn1_norm=m_ffn1_norm, ffn1_w1=m_ffn1_w1, ffn1_w3=m_ffn1_w3, ffn1_w2=m_ffn1_w2, mix_norm=m_mix_norm, w_in=m_w_in,
             pool_w=m_pool_w, pool_scale=m_pool_scale, ret_norm=m_ret_norm, w_out=m_w_out, ffn2_norm=m_ffn2_norm, ffn2_w1=m_ffn2_w1,
             ffn2_w3=m_ffn2_w3, ffn2_w2=m_ffn2_w2, final_norm=m_final_norm)
    v = dict(ffn1_norm=v_ffn1_norm, ffn1_w1=v_ffn1_w1, ffn1_w3=v_ffn1_w3, ffn1_w2=v_ffn1_w2, mix_norm=v_mix_norm, w_in=v_w_in,
             pool_w=v_pool_w, pool_scale=v_pool_scale, ret_norm=v_ret_norm, w_out=v_w_out, ffn2_norm=v_ffn2_norm, ffn2_w1=v_ffn2_w1,
             ffn2_w3=v_ffn2_w3, ffn2_w2=v_ffn2_w2, final_norm=v_final_norm)
    xs, target = x[0], loss_target[0]
    T = xs.shape[0]
    tables = _ret_tables(T)
    place = jnp.stack([lax.axis_index("c"), 2 * lax.axis_index("x") + lax.axis_index("y")]).astype(jnp.int32)
    local = lambda d, k: jnp.transpose(d[k][0]) if k in TRANSPOSED else d[k][0]
    result = lambda o, k: jnp.transpose(o)[None] if k in TRANSPOSED else o[None]
    first = ("ffn1_w1", "ffn1_w3")
    sh = {k: local(w, k).astype(BF16) for k in first}
    gather = lambda *names: [ChipExchange([sh[k] for k in names], False)]
    wg, grad, delta, new_m, new_v = {}, {}, {}, {}, {}

    def update(names, pieces, name, hosted=()):
        outs, extras = adamw_sharded([(p, local(w, k), local(m, k), local(v, k)) for k, p in zip(names, pieces)], name, hosted)
        for t, k in enumerate(names):
            grad[k], delta[k], new_m[k], new_v[k] = [result(o, k) for o in outs[4 * t:4 * t + 4]]
        return extras

    def reduce_in_chip(name, *pairs):
        reduced = prereduce([p for p, _ in pairs], [r for _, r in pairs], place, "prereduce_" + name)
        return reduced[0] if len(pairs) == 1 else reduced

    scatter = lambda *reduced: ChipExchange([r[0] for r in reduced], True, [r[1] for r in reduced])
    whole = lambda k: wg[k].reshape(-1, wg[k].shape[-1])
    sharded = lambda g: g.reshape(N_CHIPS, -1, g.shape[-1])

    later = [k for k in BIG if k not in first]
    casts, ((wg["ffn1_w1"], wg["ffn1_w3"]),) = cast_shards([local(w, k) for k in later], "cast_gather_ffn1", gather(*first))
    sh.update(zip(later, casts))
    (n1, ga1, gb1, s1), ((wg["ffn1_w2"], wg["w_in"]),) = ffn_up(
        xs, ffn1_norm, whole("ffn1_w1"), whole("ffn1_w3"), "ffn1_up", gather("ffn1_w2", "w_in"))
    (h1, u, proj), ((wg["w_out"], wg["ffn2_w1"]),) = ffn_down_mix_in(
        s1, whole("ffn1_w2"), xs, mix_norm, wg["w_in"], "ffn1_down_mix_in", gather("w_out", "ffn2_w1"))
    (pa,), _ = pool_fwd(proj, pool_w[0], pool_scale, "pool_fwd")
    (rb, o_pre, r_prev), ((wg["ffn2_w3"],),) = ret_fwd(proj, ret_norm, tables, "ret_fwd", gather("ffn2_w3"))
    (h2, n2, ga2, gb2, s2), ((wg["ffn2_w2"],),) = ffn_up(
        h1, ffn2_norm, whole("ffn2_w1"), whole("ffn2_w3"), "mix_out_ffn2_up", gather("ffn2_w2"), mixed=(pa, rb, wg["w_out"]))
    (dh3, loss, d_final), _ = ffn_down_loss(s2, whole("ffn2_w2"), h2, final_norm[None], target, "ffn2_down_loss")

    (da2, db2, df2), _ = ffn_bwd_act(dh3, whole("ffn2_w2"), ga2, gb2, "ffn2_bwd_act")
    (g_f2w2,), _ = ffn_dw([s2], df2, 1, "ffn2_dw2")
    g_f2w2 = sharded(g_f2w2)
    (g_f2w1, g_f2w3), ((r_f2w2,),) = ffn_dw([da2, db2], n2, 2, "ffn2_dw13", [SiblingExchange([g_f2w2])])
    g_f2w1, g_f2w3 = sharded(g_f2w1), sharded(g_f2w3)
    p_f2w2 = reduce_in_chip("ffn2_w2", (g_f2w2, r_f2w2))
    (dh2, d_ffn2), ((q_f2w2,), (r_f2w1, r_f2w3)) = ffn_bwd_in(
        da2, db2, whole("ffn2_w1"), whole("ffn2_w3"), h2, ffn2_norm, dh3, "ffn2_bwd_in",
        [scatter(p_f2w2), SiblingExchange([g_f2w1, g_f2w3])])
    p_f2w1, p_f2w3 = reduce_in_chip("ffn2_w13", (g_f2w1, r_f2w1), (g_f2w3, r_f2w3))
    (dpa, drb, g_wout), _ = mix_out_bwd(dh2, wg["w_out"], pa, rb, "mix_out_bwd")
    (dproj, d_pool_w, d_pool_scale), _ = pool_bwd(proj, dpa, pool_w[0], pool_scale, "pool_bwd")
    (dproj, d_ret_norm), ((q_f2w1, q_f2w3), (r_wout,)) = ret_bwd(
        proj, drb, o_pre, r_prev, ret_norm, tables, dproj, "ret_bwd", [scatter(p_f2w1, p_f2w3), SiblingExchange([g_wout])])
    p_wout = reduce_in_chip("w_out", (g_wout, r_wout))
    (g_win,), ((q_wout,),) = mix_dwin(u, dproj, N_CHIPS, "mix_dwin", [scatter(p_wout)])
    (dh1, d_mix), ((r_win,),) = mix_in_bwd(dproj, wg["w_in"], h1, mix_norm, dh2, "mix_in_bwd", [SiblingExchange([g_win])])
    p_win = reduce_in_chip("w_in", (g_win, r_win))
    (da1, db1, df1), ((q_win,),) = ffn_bwd_act(dh1, whole("ffn1_w2"), ga1, gb1, "ffn1_bwd_act", [scatter(p_win)])
    d_small = {"pool_w": d_pool_w.reshape(-1, LANES), "mix_norm": d_mix, "pool_scale": d_pool_scale, "ret_norm": d_ret_norm,
               "ffn2_norm": d_ffn2, "final_norm": d_final}
    (g_f1w1, g_f1w3), (packs,) = ffn_dw([da1, db1], n1, 2, "ffn1_dw13", [AllExchange([d_small[k] for k in SMALL[:-1]] + [loss])])
    g_f1w1, g_f1w3 = sharded(g_f1w1), sharded(g_f1w3)
    (g_f1w2,), ((r_f1w1, r_f1w3),) = ffn_dw([s1], df1, 1, "ffn1_dw2", [SiblingExchange([g_f1w1, g_f1w3])])
    g_f1w2 = sharded(g_f1w2)
    p_f1w1, p_f1w3 = reduce_in_chip("ffn1_w13", (g_f1w1, r_f1w1), (g_f1w3, r_f1w3))
    (dx, d_ffn1), ((q_f1w1, q_f1w3), (r_f1w2,)) = ffn_bwd_in(
        da1, db1, whole("ffn1_w1"), whole("ffn1_w3"), xs, ffn1_norm, dh1, "ffn1_bwd_in",
        [scatter(p_f1w1, p_f1w3), SiblingExchange([g_f1w2])])
    p_f1w2 = reduce_in_chip("ffn1_w2", (g_f1w2, r_f1w2))

    (q_f1w2,), (late,) = update(["w_in", "w_out", "ffn2_w2"], [q_win, q_wout, q_f2w2], "adamw_mix_w2",
                                [scatter(p_f1w2), AllExchange([d_ffn1])])
    update(["ffn2_w1", "ffn2_w3", "ffn1_w1", "ffn1_w3"], [q_f2w1, q_f2w3, q_f1w1, q_f1w3], "adamw_w13")
    update(["ffn1_w2"], [q_f1w2], "adamw_ffn1_w2")
    flat = lambda t, k: t[k].reshape(-1, LANES) if k == "pool_w" else t[k].reshape(1, -1)
    updated, loss_sum = adamw_small(packs[:-1] + [late], [[flat(t, k) for t in (w, m, v)] for k in SMALL], packs[-1], "adamw_small")
    for k, outs in zip(SMALL, updated):
        grad[k], delta[k], new_m[k], new_v[k] = [o.reshape(w[k].shape) for o in outs]
    loss = loss_sum[0, 0]

    return (loss, dx[None], *[grad[k] for k in WEIGHTS], *[delta[k] for k in WEIGHTS],
            *[new_m[k] for k in WEIGHTS], *[new_v[k] for k in WEIGHTS])
```

```python
import math

import jax
import jax.numpy as jnp
import numpy as np
from jax import lax
from jax.experimental import pallas as pl
from jax.experimental.pallas import tpu as pltpu

F32 = jnp.float32
BF16 = jnp.bfloat16

EPS = 1e-6
LANES = 128
BF16_TILE_ROWS = 16
N_CHIPS = 4
N_GROUPS = 4
HEAD_DIM = 128
RET_CHUNK = 128
ROPE_BASE = 10000.0
ADAM_LR, ADAM_B1, ADAM_B2, ADAM_EPS, ADAM_WD, ADAM_STEP = 0.001, 0.9, 0.999, 1e-08, 0.01, 10
VMEM_LIMIT_V7X = 56 * 1024 * 1024
ADAMW_VMEM_BUDGET = 32 * 1024 * 1024
MESH = pl.DeviceIdType.MESH
ANY = pl.BlockSpec(memory_space=pl.ANY)


def _dot(a, b):
    return jnp.dot(a, b, preferred_element_type=F32)


def _dot_nt(a, b):
    return lax.dot_general(a, b, (((1,), (1,)), ((), ())), preferred_element_type=F32)


def _dot_tn(a, b):
    return lax.dot_general(a, b, (((0,), (0,)), ((), ())), preferred_element_type=F32)


def _rstd(h):
    return lax.rsqrt(jnp.mean(h * h, axis=-1, keepdims=True) + EPS)


def _rmsnorm_bwd(dn, h, gain):
    r = _rstd(h)
    nh = h * r
    dnh = dn * gain
    dh = r * (dnh - nh * jnp.mean(dnh * nh, axis=-1, keepdims=True))
    return dh, dn * nh


def _silu_parts(a):
    sig = jax.nn.sigmoid(a)
    silu = a * sig
    return silu, sig + silu * (1.0 - sig)


def _mesh_pos():
    return lax.axis_index("x"), lax.axis_index("y"), lax.axis_index("c")


class ChipExchange:
    def __init__(self, srcs, scatter, placed=()):
        n = len(srcs)
        self.inputs, self.scatter, self.n, self.reach = list(srcs) + list(placed), scatter, n, REACH_CHIPS
        self.aliases = {n + t: t for t in range(n)} if scatter else {}
        self.half_rows = [s.shape[1] if scatter else s.shape[0] // 2 for s in srcs]
        self.out_shape = [jax.ShapeDtypeStruct((N_CHIPS, 2 * rh, s.shape[-1]), s.dtype) for s, rh in zip(srcs, self.half_rows)]
        if scatter:
            self.out_shape += [jax.ShapeDtypeStruct((2, rh // 2, s.shape[-1]), s.dtype) for s, rh in zip(srcs, self.half_rows)]
        dma = pltpu.SemaphoreType.DMA
        self.sems = [dma((4 * n,)), dma((4 * n,)), dma((2 * n,)), dma((2 * n,)), dma((4 * n,)), dma((4 * n,))]

    def _copies(self, src, out, sems):
        hop1_send, hop1_recv, hop2_send, hop2_recv, d2d_send, d2d_recv = sems
        x, y, c = _mesh_pos()
        me, dg = 2 * x + y, 2 * (1 - x) + (1 - y)
        sibling = (x, y, 1 - c)
        n = self.n
        mine, theirs = c, 1 - c

        def nb(a):
            nx, ny = x ^ (1 - a), y ^ a
            return 2 * nx + ny, (nx, ny, c)

        def remote(s, d, send, recv, k, to):
            return pltpu.make_async_remote_copy(src_ref=s, dst_ref=d, send_sem=send.at[k], recv_sem=recv.at[k],
                                                device_id=to, device_id_type=MESH)

        class Copies:
            def slot(_, t, chip, half):
                rh = self.half_rows[t]
                return out[t].at[chip, pl.ds(half * rh, rh), :]

            def quarter(_, t, chip, q):
                qh = self.half_rows[t] // 2
                return out[t].at[chip, pl.ds(mine * 2 * qh + q * qh, qh), :]

            def own_shard(k, t):
                return remote(src[t], out[t].at[me], d2d_send, d2d_recv, 4 * t + 3, sibling)

            def hop1(k, t, a, transit=False):
                rh = self.half_rows[t]
                chip, to = nb(a)
                if transit:
                    piece = src[t].at[dg, pl.ds(a * (rh // 2), rh // 2), :]
                    return remote(piece, out[n + t].at[a], hop1_send, hop1_recv, 4 * t + 2 + a, to)
                piece = src[t].at[chip] if self.scatter else src[t].at[pl.ds(mine * rh, rh), :]
                return remote(piece, k.slot(t, me, mine), hop1_send, hop1_recv, 4 * t + a, to)

            def landed1(k, t, a, transit=False):
                here = out[n + t].at[a] if transit else k.slot(t, nb(a)[0], mine)
                return remote(here, here, hop1_send, hop1_recv, 4 * t + (2 if transit else 0) + a, sibling)

            def hop2(k, t, q):
                origin, to = nb(q)[0], nb(1 - q)[1]
                piece = out[n + t].at[q] if self.scatter else k.quarter(t, origin, q)
                return remote(piece, k.quarter(t, origin, q), hop2_send, hop2_recv, 2 * t + q, to)

            def landed2(k, t, q):
                here = k.quarter(t, dg, q)
                return remote(here, here, hop2_send, hop2_recv, 2 * t + q, sibling)

            def d2d(k, t, p, chip, own=False, arriving=False):
                if arriving:
                    there = k.slot(t, chip, theirs)
                    return remote(there, there, d2d_send, d2d_recv, 4 * t + p, sibling)
                piece = src[t].at[me] if own else k.slot(t, chip, mine)
                return remote(piece, k.slot(t, chip, mine), d2d_send, d2d_recv, 4 * t + p, sibling)

        return Copies(), nb, me, dg, c

    def start(self, src, out, sems):
        k, nb, me, dg, c = self._copies(src, out, sems)
        for t in range(self.n):
            for first in range(2):
                a = first ^ c
                k.hop1(t, a).start()
                if self.scatter:
                    k.hop1(t, a, transit=True).start()
            if self.scatter:
                k.d2d(t, 3, me, own=True).start()
            else:
                k.own_shard(t).start()

    def mid(self, src, out, sems):
        k, nb, me, dg, c = self._copies(src, out, sems)
        for t in range(self.n):
            for first in range(2):
                a = first ^ c
                if self.scatter:
                    k.landed1(t, a, transit=True).wait_recv()
                    k.hop2(t, a).start()
                k.landed1(t, a).wait_recv()
                if not self.scatter:
                    k.hop2(t, a).start()
                k.d2d(t, a, nb(a)[0]).start()

    def finish(self, src, out, sems):
        k, nb, me, dg, c = self._copies(src, out, sems)
        for t in range(self.n):
            for q in range(2):
                k.landed2(t, q).wait_recv()
            k.d2d(t, 2, dg).start()
        for t in range(self.n):
            for a in range(2):
                k.d2d(t, a, nb(a)[0], arriving=True).wait_recv()
            k.d2d(t, 2, dg, arriving=True).wait_recv()
            if self.scatter:
                k.d2d(t, 3, me, arriving=True).wait_recv()
        for t in range(self.n):
            for a in range(2):
                k.hop1(t, a).wait_send()
                if self.scatter:
                    k.hop1(t, a, transit=True).wait_send()
                k.hop2(t, a).wait_send()
                k.d2d(t, a, nb(a)[0]).wait_send()
            k.d2d(t, 2, dg).wait_send()
            if self.scatter:
                k.d2d(t, 3, me, own=True).wait_send()
            else:
                k.own_shard(t).wait()


class SiblingExchange:
    def __init__(self, grads):
        self.inputs, self.n, self.aliases, self.reach = list(grads), len(grads), {}, REACH_SIBLING
        self.half_rows = [g.shape[1] // 2 for g in grads]
        self.out_shape = [jax.ShapeDtypeStruct((g.shape[0], rh, g.shape[2]), g.dtype) for g, rh in zip(grads, self.half_rows)]
        self.sems = [pltpu.SemaphoreType.DMA((self.n,)), pltpu.SemaphoreType.DMA((self.n,))]

    def _plan(self, src, out, sems):
        x, y, c = _mesh_pos()
        return [pltpu.make_async_remote_copy(
            src_ref=src[t].at[:, pl.ds((1 - c) * self.half_rows[t], self.half_rows[t]), :], dst_ref=out[t],
            send_sem=sems[0].at[t], recv_sem=sems[1].at[t], device_id=(x, y, 1 - c), device_id_type=MESH) for t in range(self.n)]

    def start(self, src, out, sems):
        for cp in self._plan(src, out, sems):
            cp.start()

    def mid(self, src, out, sems):
        pass

    def finish(self, src, out, sems):
        for cp in self._plan(src, out, sems):
            cp.wait()


REACH_SIBLING, REACH_CHIPS, REACH_ALL = 0, 1, 2


def _entry_barrier(reach):
    x, y, c = _mesh_pos()
    peers = [(x, y, 1 - c)]
    if reach == REACH_CHIPS:
        peers += [(1 - x, y, c), (x, 1 - y, c)]
    elif reach == REACH_ALL:
        peers = [(x ^ dx, y ^ dy, c ^ dc) for dx in (0, 1) for dy in (0, 1) for dc in (0, 1)][1:]
    barrier = pltpu.get_barrier_semaphore()
    for peer in peers:
        pl.semaphore_signal(barrier, inc=1, device_id=peer, device_id_type=MESH)
    pl.semaphore_wait(barrier, len(peers))


def _call(body, hosted=(), *, name, in_specs, out_specs, out_shape, args, grid=(), scratch_shapes=(), aliased=None):
    n_in, n_out, n_scr = len(in_specs), len(out_specs), len(scratch_shapes)
    total = math.prod(grid)
    mid_step = max(0, (3 * total) // 4 - 1)

    def full(*refs):
        pos = [0]

        def take(k):
            pos[0] += k
            return refs[pos[0] - k:pos[0]]

        ins, h_in = take(n_in), [take(len(h.inputs)) for h in hosted]
        outs, h_out = take(n_out), [take(len(h.out_shape)) for h in hosted]
        scr, h_sem = take(n_scr), [take(len(h.sems)) for h in hosted]
        step = 0
        for axis, size in enumerate(grid):
            step = step * size + pl.program_id(axis)

        def phase(at, method):
            if not hosted:
                return

            def run():
                if method == "start":
                    _entry_barrier(reach)
                for h, s, o, m in zip(hosted, h_in, h_out, h_sem):
                    getattr(h, method)(s, o, m)

            if total == 1:
                run()
            else:
                pl.when(step == at)(run)

        phase(0, "start")
        body(*ins, *outs, *scr)
        phase(mid_step, "mid")
        phase(total - 1, "finish")

    aliases, i0, o0 = dict(aliased or {}), n_in, n_out
    for h in hosted:
        aliases.update({i0 + i: o0 + o for i, o in h.aliases.items()})
        i0, o0 = i0 + len(h.inputs), o0 + len(h.out_shape)
    reach = max((h.reach for h in hosted), default=None)
    params = dict(vmem_limit_bytes=VMEM_LIMIT_V7X)
    if hosted:
        params["collective_id"] = reach
    results = pl.pallas_call(
        full, name=name, grid=grid,
        in_specs=list(in_specs) + [ANY] * (i0 - n_in),
        out_specs=list(out_specs) + [ANY] * (o0 - n_out),
        out_shape=list(out_shape) + [s for h in hosted for s in h.out_shape],
        scratch_shapes=list(scratch_shapes) + [s for h in hosted for s in h.sems],
        input_output_aliases=aliases,
        compiler_params=pltpu.CompilerParams(**params),
    )(*args, *[s for h in hosted for s in h.inputs])
    outs, extras, pos = list(results[:n_out]), [], n_out
    for h in hosted:
        extras.append(list(results[pos:pos + h.n]))
        pos += len(h.out_shape)
    return outs, extras


def cast_shards(shards, name, hosted=()):
    n = len(shards)

    def body(*refs):
        for x_ref, o_ref in zip(refs[:n], refs[n:]):
            o_ref[...] = x_ref[...].astype(BF16)

    whole = lambda s: pl.BlockSpec(s.shape, lambda: (0,) * s.ndim)
    return _call(body, hosted, name=name, in_specs=[whole(s) for s in shards], out_specs=[whole(s) for s in shards],
                 out_shape=[jax.ShapeDtypeStruct(s.shape, BF16) for s in shards], args=list(shards))


class AllExchange:
    def __init__(self, arrays):
        n = len(arrays)
        self.inputs, self.n, self.aliases, self.reach = list(arrays), n, {}, REACH_ALL
        self.out_shape = [jax.ShapeDtypeStruct((2 * N_CHIPS,) + a.shape, a.dtype) for a in arrays]
        self.sems = [pltpu.SemaphoreType.DMA((n,)), pltpu.SemaphoreType.DMA((7 * n,)), pltpu.SemaphoreType.DMA((7 * n,))]

    def _copies(self, src, out, sems):
        local_sem, send_sem, recv_sem = sems
        x, y, c = _mesh_pos()
        me = 4 * x + 2 * y + c
        peers = [(x ^ dx, y ^ dy, c ^ dc) for dx in (0, 1) for dy in (0, 1) for dc in (0, 1)][1:]
        remote = lambda s, d, k, to: pltpu.make_async_remote_copy(
            src_ref=s, dst_ref=d, send_sem=send_sem.at[k], recv_sem=recv_sem.at[k], device_id=to, device_id_type=MESH)
        sends, landed, local = [], [], []
        for t in range(self.n):
            local.append(pltpu.make_async_copy(src[t], out[t].at[me], local_sem.at[t]))
            for p, (px, py, pc) in enumerate(peers):
                sends.append(remote(src[t], out[t].at[me], 7 * t + p, (px, py, pc)))
                here = out[t].at[4 * px + 2 * py + pc]
                landed.append(remote(here, here, 7 * t + p, (px, py, pc)))
        return sends, landed, local

    def start(self, src, out, sems):
        sends, _, local = self._copies(src, out, sems)
        for cp in sends + local:
            cp.start()

    def mid(self, src, out, sems):
        pass

    def finish(self, src, out, sems):
        sends, landed, local = self._copies(src, out, sems)
        for cp in landed:
            cp.wait_recv()
        for cp in sends:
            cp.wait_send()
        for cp in local:
            cp.wait()


MXU_COLS = 256


def _resident(shape):
    return pl.BlockSpec(shape, lambda *_: (0,) * len(shape), pipeline_mode=pl.Buffered(1))


def ffn_up(h, gain, w1, w3, name, hosted=(), mixed=None):
    T, D = h.shape
    F = w1.shape[0]
    tm = min(T, 256)

    def body(*refs):
        if mixed is None:
            h_ref, g_ref, w1_ref, w3_ref, n_ref, ga_ref, gb_ref, s_ref = refs
            hh = h_ref[...]
        else:
            pa_ref, rb_ref, wo_ref, h_ref, g_ref, w1_ref, w3_ref, hh_ref, n_ref, ga_ref, gb_ref, s_ref = refs
            hh = h_ref[...] + _dot(pa_ref[...], wo_ref[0]) + _dot(rb_ref[...], wo_ref[1])
            hh_ref[...] = hh
        n = (hh * _rstd(hh) * g_ref[...]).astype(BF16)
        n_ref[...] = n
        for c in range(0, F, MXU_COLS):
            cols = slice(c, c + MXU_COLS)
            a = _dot_nt(n, w1_ref[cols, :])
            b = _dot_nt(n, w3_ref[cols, :])
            silu, dsilu = _silu_parts(a)
            ga_ref[:, cols] = (b * dsilu).astype(BF16)
            gb_ref[:, cols] = silu.astype(BF16)
            s_ref[:, cols] = (silu * b).astype(BF16)

    act = jax.ShapeDtypeStruct((T, F), BF16)
    act_spec = pl.BlockSpec((tm, F), lambda i: (i, 0))
    row_spec = pl.BlockSpec((tm, D), lambda i: (i, 0))
    in_specs = [row_spec, pl.BlockSpec((1, D), lambda i: (0, 0)), _resident((F, D)), _resident((F, D))]
    out_specs, out_shape, args = [row_spec, act_spec, act_spec, act_spec], [jax.ShapeDtypeStruct((T, D), BF16), act, act, act], [h, gain, w1, w3]
    if mixed is not None:
        pa, rb, woutg = mixed
        W = pa.shape[1]
        in_specs = [pl.BlockSpec((tm, W), lambda i: (i, 0))] * 2 + [_resident((2, W, D))] + in_specs
        out_specs, out_shape = [row_spec] + out_specs, [jax.ShapeDtypeStruct((T, D), F32)] + out_shape
        args = [pa, rb, woutg.reshape(2, W, D)] + args
    return _call(body, hosted, name=name, grid=(T // tm,), in_specs=in_specs, out_specs=out_specs, out_shape=out_shape, args=args)


def ffn_bwd_act(dh, w2, ga, gb, name, hosted=()):
    T, D = dh.shape
    F = w2.shape[0]
    tm = min(T, 512)

    def body(dh_ref, w2_ref, ga_ref, gb_ref, da_ref, db_ref, df_ref):
        df = (0.5 * dh_ref[...]).astype(BF16)
        df_ref[...] = df
        for c in range(0, F, MXU_COLS):
            cols = slice(c, c + MXU_COLS)
            ds = _dot_nt(df, w2_ref[cols, :])
            da_ref[:, cols] = (ds * ga_ref[:, cols].astype(F32)).astype(BF16)
            db_ref[:, cols] = (ds * gb_ref[:, cols].astype(F32)).astype(BF16)

    act = jax.ShapeDtypeStruct((T, F), BF16)
    act_spec = pl.BlockSpec((tm, F), lambda i: (i, 0))
    row_spec = pl.BlockSpec((tm, D), lambda i: (i, 0))
    return _call(
        body, hosted, name=name, grid=(T // tm,),
        in_specs=[row_spec, _resident((F, D)), act_spec, act_spec],
        out_specs=[act_spec, act_spec, row_spec],
        out_shape=[act, act, jax.ShapeDtypeStruct((T, D), BF16)],
        args=[dh, w2, ga, gb])


def ffn_dw(xs, y, halves, name, hosted=()):
    T, F = xs[0].shape
    D = y.shape[1]
    nx, fh = len(xs), F // halves
    tk = min(T, 1024)
    nk = T // tk

    def body(*refs):
        y_ref, x_refs, o_refs, accs = refs[0], refs[1:1 + nx], refs[1 + nx:1 + 2 * nx], refs[1 + 2 * nx:]
        k = pl.program_id(1)

        @pl.when(k == 0)
        def _():
            for acc in accs:
                acc[...] = jnp.zeros_like(acc)

        yy = y_ref[...]
        for x_ref, acc in zip(x_refs, accs):
            acc[...] += _dot_tn(x_ref[...], yy)

        @pl.when(k == nk - 1)
        def _():
            for o_ref, acc in zip(o_refs, accs):
                o_ref[...] = acc[...].astype(BF16)

    out = jax.ShapeDtypeStruct((F, D), BF16)
    return _call(
        body, hosted, name=name, grid=(halves, nk),
        in_specs=[pl.BlockSpec((tk, D), lambda j, k: (k, 0))] + [pl.BlockSpec((tk, fh), lambda j, k: (k, j))] * nx,
        out_specs=[pl.BlockSpec((fh, D), lambda j, k: (j, 0))] * nx,
        out_shape=[out] * nx,
        scratch_shapes=[pltpu.VMEM((fh, D), F32)] * nx,
        args=[y] + list(xs))


def ffn_bwd_in(da, db, w1, w3, h, gain, dh, name, hosted=()):
    T, F = da.shape
    D = h.shape[1]
    tm = min(T, 256)

    def body(da_ref, db_ref, w1_ref, w3_ref, h_ref, g_ref, dh_ref, o_ref, dg_ref):
        dn = _dot(da_ref[...], w1_ref[...]) + _dot(db_ref[...], w3_ref[...])
        dhn, dg = _rmsnorm_bwd(dn, h_ref[...], g_ref[...])
        o_ref[...] = dh_ref[...] + dhn

        @pl.when(pl.program_id(0) == 0)
        def _():
            dg_ref[...] = jnp.zeros_like(dg_ref)

        dg_ref[...] += jnp.sum(dg, axis=0, keepdims=True)

    act_spec = pl.BlockSpec((tm, F), lambda i: (i, 0))
    row_spec = pl.BlockSpec((tm, D), lambda i: (i, 0))
    vec_spec = pl.BlockSpec((1, D), lambda i: (0, 0))
    return _call(
        body, hosted, name=name, grid=(T // tm,),
        in_specs=[act_spec, act_spec, _resident((F, D)), _resident((F, D)), row_spec, vec_spec, row_spec],
        out_specs=[row_spec, vec_spec],
        out_shape=[jax.ShapeDtypeStruct((T, D), F32), jax.ShapeDtypeStruct((1, D), F32)],
        args=[da, db, w1, w3, h, gain, dh])


def ffn_down_mix_in(s, w2, h, gain, wing, name, hosted=()):
    T, F = s.shape
    D = h.shape[1]
    nsh, _, Cs = wing.shape
    tm = min(T, 512)

    def body(s_ref, w2_ref, h_ref, g_ref, w_ref, hh_ref, u_ref, p_ref):
        hh = h_ref[...] + 0.5 * _dot(s_ref[...], w2_ref[...])
        hh_ref[...] = hh
        u = (hh * _rstd(hh) * g_ref[...]).astype(BF16)
        u_ref[...] = u
        for j in range(nsh):
            p_ref[:, j * Cs:(j + 1) * Cs] = _dot(u, w_ref[j])

    row_spec = pl.BlockSpec((tm, D), lambda i: (i, 0))
    return _call(
        body, hosted, name=name, grid=(T // tm,),
        in_specs=[pl.BlockSpec((tm, F), lambda i: (i, 0)), _resident((F, D)), row_spec, pl.BlockSpec((1, D), lambda i: (0, 0)),
                  _resident((nsh, D, Cs))],
        out_specs=[row_spec, row_spec, pl.BlockSpec((tm, nsh * Cs), lambda i: (i, 0))],
        out_shape=[jax.ShapeDtypeStruct((T, D), F32), jax.ShapeDtypeStruct((T, D), BF16), jax.ShapeDtypeStruct((T, nsh * Cs), F32)],
        args=[s, w2, h, gain, wing])


def mix_out_bwd(dh, woutg, a, b, name, hosted=()):
    T, D = dh.shape
    W = a.shape[1]
    nsh, Rs, _ = woutg.shape
    wout = woutg.reshape(2, W, D)
    tk = min(T, 512)
    nk = T // tk

    def body(dh_ref, w_ref, a_ref, b_ref, da_ref, db_ref, dw_ref, acc):
        k = pl.program_id(0)

        @pl.when(k == 0)
        def _():
            acc[...] = jnp.zeros_like(acc)

        dhb = dh_ref[...].astype(BF16)
        da_ref[...] = _dot_nt(dhb, w_ref[0])
        db_ref[...] = _dot_nt(dhb, w_ref[1])
        acc[0:W, :] += _dot_tn(a_ref[...], dhb)
        acc[W:2 * W, :] += _dot_tn(b_ref[...], dhb)

        @pl.when(k == nk - 1)
        def _():
            for j in range(nsh):
                dw_ref[j] = acc[j * Rs:(j + 1) * Rs, :].astype(BF16)

    return _call(
        body, hosted, name=name, grid=(nk,),
        in_specs=[pl.BlockSpec((tk, D), lambda k: (k, 0)), pl.BlockSpec((2, W, D), lambda k: (0, 0, 0)),
                  pl.BlockSpec((tk, W), lambda k: (k, 0)), pl.BlockSpec((tk, W), lambda k: (k, 0))],
        out_specs=[pl.BlockSpec((tk, W), lambda k: (k, 0)), pl.BlockSpec((tk, W), lambda k: (k, 0)),
                   pl.BlockSpec((nsh, Rs, D), lambda k: (0, 0, 0))],
        out_shape=[jax.ShapeDtypeStruct((T, W), F32), jax.ShapeDtypeStruct((T, W), F32),
                   jax.ShapeDtypeStruct((nsh, Rs, D), BF16)],
        scratch_shapes=[pltpu.VMEM((2 * W, D), F32)],
        args=[dh, wout, a, b])


def _dproj_block(g):
    return (g // N_GROUPS + N_GROUPS) % (N_GROUPS + 1), g % N_GROUPS


def mix_dwin(u, dproj, nsh, name, hosted=()):
    T, D = u.shape
    Hd = HEAD_DIM
    slabs, _, width = dproj.shape
    blocks = slabs * width // Hd
    Cs = blocks * Hd // nsh
    tk = min(T, 512)
    nk = T // tk

    def body(u_ref, d_ref, o_ref, acc):
        k = pl.program_id(0)

        @pl.when(k == 0)
        def _():
            acc[...] = jnp.zeros_like(acc)

        where = [_dproj_block(g) for g in range(blocks)]
        d = jnp.concatenate([d_ref[slab, :, col * Hd:(col + 1) * Hd] for slab, col in where], axis=1)
        acc[...] += _dot_tn(u_ref[...], d)

        @pl.when(k == nk - 1)
        def _():
            for j in range(nsh):
                o_ref[j] = acc[:, j * Cs:(j + 1) * Cs].astype(BF16)

    return _call(
        body, hosted, name=name, grid=(nk,),
        in_specs=[pl.BlockSpec((tk, D), lambda k: (k, 0)), pl.BlockSpec((slabs, tk, width), lambda k: (0, k, 0))],
        out_specs=[pl.BlockSpec((nsh, D, Cs), lambda k: (0, 0, 0))],
        out_shape=[jax.ShapeDtypeStruct((nsh, D, Cs), BF16)],
        scratch_shapes=[pltpu.VMEM((D, blocks * Hd), F32)],
        args=[u, dproj])


def mix_in_bwd(dproj, wing, h, gain, dh, name, hosted=()):
    T, D = h.shape
    nsh, _, Cs = wing.shape
    Hd = HEAD_DIM
    per = Cs // Hd
    tm = min(T, 512)

    def body(d_ref, w_ref, h_ref, g_ref, dh_ref, o_ref, dg_ref):
        def shard(j):
            blocks = [_dproj_block(per * j + i) for i in range(per)]
            return jnp.concatenate([d_ref[slab, :, col * Hd:(col + 1) * Hd] for slab, col in blocks], axis=1)

        du = _dot_nt(shard(0), w_ref[0])
        for j in range(1, nsh):
            du += _dot_nt(shard(j), w_ref[j])
        dhn, dg = _rmsnorm_bwd(du, h_ref[...], g_ref[...])
        o_ref[...] = dh_ref[...] + dhn

        @pl.when(pl.program_id(0) == 0)
        def _():
            dg_ref[...] = jnp.zeros_like(dg_ref)

        dg_ref[...] += jnp.sum(dg, axis=0, keepdims=True)

    row_spec = pl.BlockSpec((tm, D), lambda i: (i, 0))
    vec_spec = pl.BlockSpec((1, D), lambda i: (0, 0))
    return _call(
        body, hosted, name=name, grid=(T // tm,),
        in_specs=[pl.BlockSpec((dproj.shape[0], tm, dproj.shape[2]), lambda i: (0, i, 0)),
                  pl.BlockSpec((nsh, D, Cs), lambda i: (0, 0, 0)), row_spec, vec_spec, row_spec],
        out_specs=[row_spec, vec_spec],
        out_shape=[jax.ShapeDtypeStruct((T, D), F32), jax.ShapeDtypeStruct((1, D), F32)],
        args=[dproj, wing, h, gain, dh])


POOL_WINDOWS = (2, 4, 8, 16)


def _pool_window(x, window, T, trailing):
    rows = lax.broadcasted_iota(jnp.int32, x.shape, 0)
    s, k = x, 1
    while k < window:
        if trailing:
            s = s + jnp.where(rows >= k, pltpu.roll(s, k, 0), 0.0)
        else:
            s = s + jnp.where(rows < T - k, pltpu.roll(s, T - k, 0), 0.0)
        k *= 2
    return s


def _pool_count(window, shape):
    rows = lax.broadcasted_iota(jnp.int32, shape, 0)
    return jnp.minimum(rows + 1, window).astype(F32)


def _per_group(work):
    for group, window in enumerate(POOL_WINDOWS):
        pl.when(pl.program_id(0) == group)(lambda window=window: work(window))


def pool_fwd(proj, pool_w, pool_scale, name, hosted=()):
    T = proj.shape[0]
    Hd = HEAD_DIM

    def body(x_ref, w_ref, sc_ref, a_ref):
        def work(window):
            x = x_ref[...]
            pooled = _pool_window(x, window, T, True) / _pool_count(window, x.shape) - x
            a_ref[...] = (_dot(pooled.astype(BF16), w_ref[0].astype(BF16)) * sc_ref[...]).astype(BF16)

        _per_group(work)

    return _call(
        body, hosted, name=name, grid=(N_GROUPS,),
        in_specs=[pl.BlockSpec((T, Hd), lambda g: (0, g)), pl.BlockSpec((1, Hd, Hd), lambda g: (g, 0, 0)),
                  pl.BlockSpec((1, Hd), lambda g: (0, g))],
        out_specs=[pl.BlockSpec((T, Hd), lambda g: (0, g))],
        out_shape=[jax.ShapeDtypeStruct((T, N_GROUPS * Hd), BF16)],
        args=[proj, pool_w, pool_scale])


def pool_bwd(proj, da, pool_w, pool_scale, name, hosted=()):
    T = proj.shape[0]
    Hd = HEAD_DIM

    def body(x_ref, da_ref, w_ref, sc_ref, dx_ref, dw_ref, dsc_ref):
        def work(window):
            x = x_ref[...]
            cnt = _pool_count(window, x.shape)
            pooled = (_pool_window(x, window, T, True) / cnt - x).astype(BF16)
            wb = w_ref[0].astype(BF16)
            dav = da_ref[...]
            dsc_ref[...] = jnp.sum(dav * _dot(pooled, wb), axis=0, keepdims=True)
            dout = (dav * sc_ref[...]).astype(BF16)
            dw_ref[0] = _dot_tn(pooled, dout)
            dpooled = _dot_nt(dout, wb)
            dx_ref[0] = (_pool_window(dpooled / cnt, window, T, False) - dpooled).astype(BF16)

        _per_group(work)

    col_spec = pl.BlockSpec((T, Hd), lambda g: (0, g))
    return _call(
        body, hosted, name=name, grid=(N_GROUPS,),
        in_specs=[col_spec, col_spec, pl.BlockSpec((1, Hd, Hd), lambda g: (g, 0, 0)), pl.BlockSpec((1, Hd), lambda g: (0, g))],
        out_specs=[pl.BlockSpec((1, T, Hd), lambda g: (N_GROUPS, 0, g)), pl.BlockSpec((1, Hd, Hd), lambda g: (g, 0, 0)),
                   pl.BlockSpec((1, Hd), lambda g: (0, g))],
        out_shape=[jax.ShapeDtypeStruct((N_GROUPS + 1, T, N_GROUPS * Hd), BF16), jax.ShapeDtypeStruct((N_GROUPS, Hd, Hd), F32),
                   jax.ShapeDtypeStruct((1, N_GROUPS * Hd), F32)],
        args=[proj, da, pool_w, pool_scale])


def _ret_tables(T):
    Hd, C, f32 = HEAD_DIM, RET_CHUNK, np.float32
    inv_freq = (1.0 / (ROPE_BASE ** (np.arange(0, Hd, 2, dtype=np.float64) / Hd))).astype(f32)
    ang = np.arange(T, dtype=f32)[:, None] * inv_freq[None, :]
    cos, sin = np.cos(ang), np.sin(ang)
    cos2 = np.concatenate([cos, cos], axis=-1)
    sin2 = np.concatenate([-sin, sin], axis=-1)
    log_gamma = np.log1p(-np.exp2(f32(-5.0) - np.arange(N_GROUPS, dtype=f32)))
    pos = np.arange(C, dtype=f32)
    rel = pos[:, None] - pos[None, :]
    intra = np.where(rel[None] >= 0, np.exp(log_gamma[:, None, None] * np.maximum(rel, f32(0.0))[None]), f32(0.0))
    k_tail = np.exp(log_gamma[:, None] * (f32(C - 1) - pos)[None, :])
    q_head = np.exp(log_gamma[:, None] * (pos + f32(1.0))[None, :])
    chunk_decay = np.exp(log_gamma * f32(C))
    wide = lambda t: np.broadcast_to(t[:, :, None], (N_GROUPS, C, Hd))
    tables = cos2, sin2, intra, wide(k_tail), wide(q_head), np.broadcast_to(chunk_decay[:, None, None], (N_GROUPS, 1, Hd))
    assert all(t.dtype == f32 for t in tables)
    return tuple(jnp.asarray(t) for t in tables)


def _rope(x, cos2, sin2):
    return x * cos2 + pltpu.roll(x, HEAD_DIM // 2, 1) * sin2


def _rope_t(d, cos2, sin2):
    return d * cos2 + pltpu.roll(d * sin2, HEAD_DIM // 2, 1)


def _ret_specs(T, tseg, seg_of):
    Hd, G = HEAD_DIM, N_GROUPS
    col = lambda kind: pl.BlockSpec((tseg, Hd), lambda h, s: (seg_of(s), G * kind + h))
    tab = pl.BlockSpec((T, Hd), lambda h, s: (0, 0))
    head = pl.BlockSpec((1, RET_CHUNK, Hd), lambda h, s: (h, 0, 0))
    cd = pl.BlockSpec((1, 1, Hd), lambda h, s: (h, 0, 0))
    gain = pl.BlockSpec((1, Hd), lambda h, s: (0, h))
    return col, tab, head, cd, gain


def ret_fwd(proj, ret_norm, tables, name, hosted=()):
    T = proj.shape[0]
    Hd, C, G = HEAD_DIM, RET_CHUNK, N_GROUPS
    tseg = min(T, 2048)
    nseg, nck = T // tseg, tseg // C
    scale = Hd ** -0.5
    cos2, sin2, intra, k_tail, q_head, chunk_decay = tables

    def body(q_ref, k_ref, v_ref, g_ref, gain_ref, cos_ref, sin_ref, m_ref, kt_ref, qh_ref, cd_ref,
             b_ref, o_ref, rp_ref, state):
        @pl.when(pl.program_id(1) == 0)
        def _():
            state[...] = jnp.zeros_like(state)

        def chunk(ci, carry):
            rows = pl.ds(pl.multiple_of(ci * C, C), C)
            at = pl.ds(pl.multiple_of(pl.program_id(1) * tseg + ci * C, C), C)
            cos, sin = cos_ref[at, :], sin_ref[at, :]
            qr = _rope(q_ref[rows, :], cos, sin)
            kr = _rope(k_ref[rows, :], cos, sin) * scale
            qb, kb, vb = qr.astype(BF16), kr.astype(BF16), v_ref[rows, :].astype(BF16)
            r = state[...]
            rp_ref[0, ci] = r.astype(BF16)
            sc = _dot_nt(qb, kb) * m_ref[0]
            o = _dot(sc.astype(BF16), vb) + _dot((qr * qh_ref[0]).astype(BF16), r.astype(BF16))
            state[...] = cd_ref[0] * r + _dot_tn((kr * kt_ref[0]).astype(BF16), vb)
            o_ref[rows, :] = o
            on = o * _rstd(o)
            b_ref[rows, :] = (jax.nn.silu(g_ref[rows, :]) * (on * gain_ref[...])).astype(BF16)
            return carry

        lax.fori_loop(0, nck, chunk, 0, unroll=True)

    col, tab, head, cd, gain = _ret_specs(T, tseg, lambda s: s)
    out_col = pl.BlockSpec((tseg, Hd), lambda h, s: (s, h))
    return _call(
        body, hosted, name=name, grid=(G, nseg),
        in_specs=[col(1), col(2), col(3), col(4), gain, tab, tab, head, head, head, cd],
        out_specs=[out_col, out_col, pl.BlockSpec((1, nck, Hd, Hd), lambda h, s: (h, s, 0, 0))],
        out_shape=[jax.ShapeDtypeStruct((T, G * Hd), BF16), jax.ShapeDtypeStruct((T, G * Hd), F32),
                   jax.ShapeDtypeStruct((G, T // C, Hd, Hd), BF16)],
        scratch_shapes=[pltpu.VMEM((Hd, Hd), F32)],
        args=[proj, proj, proj, proj, ret_norm, cos2, sin2, intra, k_tail, q_head, chunk_decay])


def ret_bwd(proj, db, o_pre, r_prev, ret_norm, tables, dproj, name, hosted=()):
    T = proj.shape[0]
    Hd, C, G = HEAD_DIM, RET_CHUNK, N_GROUPS
    tseg = min(T, 2048)
    nseg, nck = T // tseg, tseg // C
    scale = Hd ** -0.5
    cos2, sin2, intra, k_tail, q_head, chunk_decay = tables

    def body(q_ref, k_ref, v_ref, g_ref, db_ref, o_ref, rp_ref, gain_ref, cos_ref, sin_ref, m_ref, kt_ref, qh_ref, cd_ref,
             _, d_ref, dgain_ref, gstate):
        @pl.when(pl.program_id(1) == 0)
        def _():
            gstate[...] = jnp.zeros_like(gstate)
            dgain_ref[...] = jnp.zeros_like(dgain_ref)

        def chunk(t, carry):
            ci = nck - 1 - t
            rows = pl.ds(pl.multiple_of(ci * C, C), C)
            at = pl.ds(pl.multiple_of((nseg - 1 - pl.program_id(1)) * tseg + ci * C, C), C)
            cos, sin = cos_ref[at, :], sin_ref[at, :]
            qr = _rope(q_ref[rows, :], cos, sin)
            kr = _rope(k_ref[rows, :], cos, sin) * scale
            qb, kb, vb = qr.astype(BF16), kr.astype(BF16), v_ref[rows, :].astype(BF16)
            qhb, ktb = (qr * qh_ref[0]).astype(BF16), (kr * kt_ref[0]).astype(BF16)
            sc = (_dot_nt(qb, kb) * m_ref[0]).astype(BF16)
            o = o_ref[rows, :]
            rstd = _rstd(o)
            on = o * rstd
            gain = gain_ref[...]
            silu, dsilu = _silu_parts(g_ref[rows, :])
            dy = db_ref[rows, :]
            dgain_ref[...] += jnp.sum(dy * silu * on, axis=0, keepdims=True)
            dg = dy * on * gain * dsilu
            don = dy * silu * gain
            dob = (rstd * (don - on * jnp.mean(don * on, axis=-1, keepdims=True))).astype(BF16)
            gn = gstate[...]
            gb = gn.astype(BF16)
            da = (_dot_nt(dob, vb) * m_ref[0]).astype(BF16)
            dq = _dot(da, kb) + _dot_nt(dob, rp_ref[0, ci]) * qh_ref[0]
            dk = _dot_tn(da, qb) + _dot_nt(vb, gb) * kt_ref[0]
            dv = _dot_tn(sc, dob) + _dot(ktb, gb)
            gstate[...] = cd_ref[0] * gn + _dot_tn(qhb, dob)
            d_ref[0, rows, :] = _rope_t(dq, cos, sin).astype(BF16)
            d_ref[1, rows, :] = _rope_t(dk * scale, cos, sin).astype(BF16)
            d_ref[2, rows, :] = dv.astype(BF16)
            d_ref[3, rows, :] = dg.astype(BF16)
            return carry

        lax.fori_loop(0, nck, chunk, 0, unroll=True)

    rev = lambda s: nseg - 1 - s
    col, tab, head, cd, gain = _ret_specs(T, tseg, rev)
    act = pl.BlockSpec((tseg, Hd), lambda h, s: (rev(s), h))
    return _call(
        body, hosted, name=name, grid=(G, nseg),
        in_specs=[col(1), col(2), col(3), col(4), act, act, pl.BlockSpec((1, nck, Hd, Hd), lambda h, s: (h, rev(s), 0, 0)),
                  gain, tab, tab, head, head, head, cd, ANY],
        out_specs=[pl.BlockSpec((4, tseg, Hd), lambda h, s: (0, rev(s), h)), gain],
        out_shape=[jax.ShapeDtypeStruct(dproj.shape, BF16), jax.ShapeDtypeStruct((1, G * Hd), F32)],
        scratch_shapes=[pltpu.VMEM((Hd, Hd), F32)], aliased={14: 0},
        args=[proj, proj, proj, proj, db, o_pre, r_prev, ret_norm, cos2, sin2, intra, k_tail, q_head, chunk_decay, dproj])


def ffn_down_loss(s, w2, h, gain, target, name, hosted=()):
    T, F = s.shape
    D = h.shape[1]
    tm = min(T, 512)

    def body(s_ref, w2_ref, h_ref, g_ref, t_ref, dh_ref, loss_ref, dg_ref):
        @pl.when(pl.program_id(0) == 0)
        def _():
            loss_ref[...] = jnp.zeros_like(loss_ref)
            dg_ref[...] = jnp.zeros_like(dg_ref)

        hh = h_ref[...] + 0.5 * _dot(s_ref[...], w2_ref[...])
        gain_v = g_ref[...]
        err = hh * _rstd(hh) * gain_v - t_ref[...]
        loss_ref[...] += 0.5 * jnp.sum(jnp.mean(err * err, axis=-1, keepdims=True), axis=0, keepdims=True)
        dhn, dg = _rmsnorm_bwd(err * (1.0 / D), hh, gain_v)
        dh_ref[...] = dhn
        dg_ref[...] += jnp.sum(dg, axis=0, keepdims=True)

    row_spec = pl.BlockSpec((tm, D), lambda i: (i, 0))
    vec_spec = pl.BlockSpec((1, D), lambda i: (0, 0))
    return _call(
        body, hosted, name=name, grid=(T // tm,),
        in_specs=[pl.BlockSpec((tm, F), lambda i: (i, 0)), _resident((F, D)), row_spec, vec_spec, row_spec],
        out_specs=[row_spec, pl.BlockSpec((1, LANES), lambda i: (0, 0)), vec_spec],
        out_shape=[jax.ShapeDtypeStruct((T, D), F32), jax.ShapeDtypeStruct((1, LANES), F32), jax.ShapeDtypeStruct((1, D), F32)],
        args=[s, w2, h, gain, target])


def prereduce(grads, recvs, place, name):
    nt = len(grads)
    nsh, R, C = grads[0].shape
    rh = R // 2

    def body(place_ref, *refs):
        for t in range(nt):
            g_ref, r_ref, o_ref, own_ref = refs[2 * t], refs[2 * t + 1], refs[2 * nt + 2 * t], refs[2 * nt + 2 * t + 1]
            piece = (g_ref[...].astype(F32) + r_ref[...].astype(F32)).astype(BF16)
            o_ref[...] = piece

            @pl.when(pl.program_id(0) == place_ref[1])
            def _():
                own_ref[...] = piece

    outs = pl.pallas_call(
        body, name=name,
        grid_spec=pltpu.PrefetchScalarGridSpec(
            num_scalar_prefetch=1, grid=(nsh,),
            in_specs=[pl.BlockSpec((1, rh, C), lambda j, p: (j, p[0], 0)), pl.BlockSpec((1, rh, C), lambda j, p: (j, 0, 0))] * nt,
            out_specs=[pl.BlockSpec((1, rh, C), lambda j, p: (j, 0, 0)),
                       pl.BlockSpec((1, rh, C), lambda j, p: (p[1], p[0], 0))] * nt),
        out_shape=[jax.ShapeDtypeStruct((nsh, rh, C), BF16), jax.ShapeDtypeStruct((nsh, R, C), BF16)] * nt,
        compiler_params=pltpu.CompilerParams(vmem_limit_bytes=VMEM_LIMIT_V7X),
    )(place, *[a for pair in zip(grads, recvs) for a in pair])
    return [(outs[2 * t], outs[2 * t + 1]) for t in range(nt)]


def _adamw(w, g, m, v):
    m = ADAM_B1 * m + (1.0 - ADAM_B1) * g
    v = ADAM_B2 * v + (1.0 - ADAM_B2) * (g * g)
    m_hat = m / (1.0 - ADAM_B1 ** ADAM_STEP)
    v_hat = v / (1.0 - ADAM_B2 ** ADAM_STEP)
    return -ADAM_LR * (m_hat / (jnp.sqrt(v_hat) + ADAM_EPS) + ADAM_WD * w), m, v


def adamw_sharded(tensors, name, hosted=()):
    nt = len(tensors)
    nsh = tensors[0][0].shape[0]
    shapes = [t[0].shape[1:] for t in tensors]

    def fits(steps):
        if any(R % (steps * BF16_TILE_ROWS) for R, _ in shapes):
            return False
        return sum(2 * (R // steps) * -(-C // LANES) * LANES * (nsh * 2 + 7 * 4) for R, C in shapes) <= ADAMW_VMEM_BUDGET

    steps = min(s for s in range(1, min(R for R, _ in shapes) // BF16_TILE_ROWS + 1) if fits(s))

    def body(*refs):
        ins, outs = refs[:4 * nt], refs[4 * nt:]
        for t in range(nt):
            p_ref, w_ref, m_ref, v_ref = ins[4 * t:4 * t + 4]
            g_ref, d_ref, nm_ref, nv_ref = outs[4 * t:4 * t + 4]
            g = p_ref[0].astype(F32)
            for i in range(1, nsh):
                g += p_ref[i].astype(F32)
            g_ref[...] = g
            d_ref[...], nm_ref[...], nv_ref[...] = _adamw(w_ref[...], g, m_ref[...], v_ref[...])

    in_specs, out_specs, out_shape = [], [], []
    for R, C in shapes:
        spec = pl.BlockSpec((R // steps, C), lambda i: (i, 0))
        in_specs += [pl.BlockSpec((nsh, R // steps, C), lambda i: (0, i, 0)), spec, spec, spec]
        out_specs += [spec] * 4
        out_shape += [jax.ShapeDtypeStruct((R, C), F32)] * 4
    return _call(body, hosted, name=name, grid=(steps,), in_specs=in_specs, out_specs=out_specs, out_shape=out_shape,
                 args=[a for tensor in tensors for a in tensor])


def adamw_small(packs, params, loss_packs, name):
    n = len(packs)
    ndev = loss_packs.shape[0]

    def body(*refs):
        p_refs, loss_ref, wmv = refs[:n], refs[n], refs[n + 1:4 * n + 1]
        outs, loss_out = refs[4 * n + 1:8 * n + 1], refs[8 * n + 1]
        total = lambda r: sum((r[i] for i in range(1, ndev)), r[0])
        loss_out[...] = total(loss_ref)
        for k in range(n):
            g = total(p_refs[k])
            outs[4 * k][...] = g
            outs[4 * k + 1][...], outs[4 * k + 2][...], outs[4 * k + 3][...] = _adamw(
                wmv[3 * k][...], g, wmv[3 * k + 1][...], wmv[3 * k + 2][...])

    out_shape = [jax.ShapeDtypeStruct(p[0].shape, F32) for p in params for _ in range(4)]
    outs = pl.pallas_call(body, name=name, out_shape=out_shape + [jax.ShapeDtypeStruct(loss_packs.shape[1:], F32)],
                          compiler_params=pltpu.CompilerParams(vmem_limit_bytes=VMEM_LIMIT_V7X),
                          )(*packs, loss_packs, *[a for p in params for a in p])
    return [outs[4 * k:4 * k + 4] for k in range(n)], outs[4 * n]


BIG = ("ffn1_w1", "ffn1_w3", "ffn1_w2", "w_in", "w_out", "ffn2_w1", "ffn2_w3", "ffn2_w2")
TRANSPOSED = ("ffn1_w1", "ffn1_w3", "ffn2_w1", "ffn2_w3")
SMALL = ("pool_w", "mix_norm", "pool_scale", "ret_norm", "ffn2_norm", "final_norm", "ffn1_norm")
WEIGHTS = ("ffn1_norm", "ffn1_w1", "ffn1_w3", "ffn1_w2", "mix_norm", "w_in", "pool_w", "pool_scale", "ret_norm", "w_out",
           "ffn2_norm", "ffn2_w1", "ffn2_w3", "ffn2_w2", "final_norm")


def kernel(x, ffn1_norm, ffn1_w1, ffn1_w3, ffn1_w2, mix_norm, w_in, pool_w, pool_scale, ret_norm, w_out, ffn2_norm, ffn2_w1, ffn2_w3, ffn2_w2, final_norm, loss_target, m_ffn1_norm, m_ffn1_w1, m_ffn1_w3, m_ffn1_w2, m_mix_norm, m_w_in, m_pool_w, m_pool_scale, m_ret_norm, m_w_out, m_ffn2_norm, m_ffn2_w1, m_ffn2_w3, m_ffn2_w2, m_final_norm, v_ffn1_norm, v_ffn1_w1, v_ffn1_w3, v_ffn1_w2, v_mix_norm, v_w_in, v_pool_w, v_pool_scale, v_ret_norm, v_w_out, v_ffn2_norm, v_ffn2_w1, v_ffn2_w3, v_ffn2_w2, v_final_norm):
    w = dict(ffn1_norm=ffn1_norm, ffn1_w1=ffn1_w1, ffn1_w3=ffn1_w3, ffn1_w2=ffn1_w2, mix_norm=mix_norm, w_in=w_in, pool_w=pool_w,
             pool_scale=pool_scale, ret_norm=ret_norm, w_out=w_out, ffn2_norm=ffn2_norm, ffn2_w1=ffn2_w1, ffn2_w3=ffn2_w3,
             ffn2_w2=ffn2_w2, final_norm=final_norm)
    m = dict(ffn1_norm=m_ffn1_norm, ffn1_w1=m_ffn1_w1, ffn1_w3=m_ffn1_w3, ffn1_w2=m_ffn1_w2, mix_norm=m_mix_norm, w_in=m_w_in,
             pool_w=m_pool_w, pool_scale=m_pool_scale, ret_norm=m_ret_norm, w_out=m_w_out, ffn2_norm=m_ffn2_norm, ffn2_w1=m_ffn2_w1,
             ffn2_w3=m_ffn2_w3, ffn2_w2=m_ffn2_w2, final_norm=m_final_norm)
    v = dict(ffn1_norm=v_ffn1_norm, ffn1_w1=v_ffn1_w1, ffn1_w3=v_ffn1_w3, ffn1_w2=v_ffn1_w2, mix_norm=v_mix_norm, w_in=v_w_in,
             pool_w=v_pool_w, pool_scale=v_pool_scale, ret_norm=v_ret_norm, w_out=v_w_out, ffn2_norm=v_ffn2_norm, ffn2_w1=v_ffn2_w1,
             ffn2_w3=v_ffn2_w3, ffn2_w2=v_ffn2_w2, final_norm=v_final_norm)
    xs, target = x[0], loss_target[0]
    T = xs.shape[0]
    tables = _ret_tables(T)
    place = jnp.stack([lax.axis_index("c"), 2 * lax.axis_index("x") + lax.axis_index("y")]).astype(jnp.int32)
    local = lambda d, k: jnp.transpose(d[k][0]) if k in TRANSPOSED else d[k][0]
    result = lambda o, k: jnp.transpose(o)[None] if k in TRANSPOSED else o[None]
    first = ("ffn1_w1", "ffn1_w3")
    sh = {k: local(w, k).astype(BF16) for k in first}
    gather = lambda *names: [ChipExchange([sh[k] for k in names], False)]
    wg, grad, delta, new_m, new_v = {}, {}, {}, {}, {}

    def update(names, pieces, name, hosted=()):
        outs, extras = adamw_sharded([(p, local(w, k), local(m, k), local(v, k)) for k, p in zip(names, pieces)], name, hosted)
        for t, k in enumerate(names):
            grad[k], delta[k], new_m[k], new_v[k] = [result(o, k) for o in outs[4 * t:4 * t + 4]]
        return extras

    def reduce_in_chip(name, *pairs):
        reduced = prereduce([p for p, _ in pairs], [r for _, r in pairs], place, "prereduce_" + name)
        return reduced[0] if len(pairs) == 1 else reduced

    scatter = lambda *reduced: ChipExchange([r[0] for r in reduced], True, [r[1] for r in reduced])
    whole = lambda k: wg[k].reshape(-1, wg[k].shape[-1])
    sharded = lambda g: g.reshape(N_CHIPS, -1, g.shape[-1])

    later = [k for k in BIG if k not in first]
    casts, ((wg["ffn1_w1"], wg["ffn1_w3"]),) = cast_shards([local(w, k) for k in later], "cast_gather_ffn1", gather(*first))
    sh.update(zip(later, casts))
    (n1, ga1, gb1, s1), ((wg["ffn1_w2"], wg["w_in"]),) = ffn_up(
        xs, ffn1_norm, whole("ffn1_w1"), whole("ffn1_w3"), "ffn1_up", gather("ffn1_w2", "w_in"))
    (h1, u, proj), ((wg["w_out"], wg["ffn2_w1"]),) = ffn_down_mix_in(
        s1, whole("ffn1_w2"), xs, mix_norm, wg["w_in"], "ffn1_down_mix_in", gather("w_out", "ffn2_w1"))
    (pa,), _ = pool_fwd(proj, pool_w[0], pool_scale, "pool_fwd")
    (rb, o_pre, r_prev), ((wg["ffn2_w3"],),) = ret_fwd(proj, ret_norm, tables, "ret_fwd", gather("ffn2_w3"))
    (h2, n2, ga2, gb2, s2), ((wg["ffn2_w2"],),) = ffn_up(
        h1, ffn2_norm, whole("ffn2_w1"), whole("ffn2_w3"), "mix_out_ffn2_up", gather("ffn2_w2"), mixed=(pa, rb, wg["w_out"]))
    (dh3, loss, d_final), _ = ffn_down_loss(s2, whole("ffn2_w2"), h2, final_norm[None], target, "ffn2_down_loss")

    (da2, db2, df2), _ = ffn_bwd_act(dh3, whole("ffn2_w2"), ga2, gb2, "ffn2_bwd_act")
    (g_f2w2,), _ = ffn_dw([s2], df2, 1, "ffn2_dw2")
    g_f2w2 = sharded(g_f2w2)
    (g_f2w1, g_f2w3), ((r_f2w2,),) = ffn_dw([da2, db2], n2, 2, "ffn2_dw13", [SiblingExchange([g_f2w2])])
    g_f2w1, g_f2w3 = sharded(g_f2w1), sharded(g_f2w3)
    p_f2w2 = reduce_in_chip("ffn2_w2", (g_f2w2, r_f2w2))
    (dh2, d_ffn2), ((q_f2w2,), (r_f2w1, r_f2w3)) = ffn_bwd_in(
        da2, db2, whole("ffn2_w1"), whole("ffn2_w3"), h2, ffn2_norm, dh3, "ffn2_bwd_in",
        [scatter(p_f2w2), SiblingExchange([g_f2w1, g_f2w3])])
    p_f2w1, p_f2w3 = reduce_in_chip("ffn2_w13", (g_f2w1, r_f2w1), (g_f2w3, r_f2w3))
    (dpa, drb, g_wout), _ = mix_out_bwd(dh2, wg["w_out"], pa, rb, "mix_out_bwd")
    (dproj, d_pool_w, d_pool_scale), _ = pool_bwd(proj, dpa, pool_w[0], pool_scale, "pool_bwd")
    (dproj, d_ret_norm), ((q_f2w1, q_f2w3), (r_wout,)) = ret_bwd(
        proj, drb, o_pre, r_prev, ret_norm, tables, dproj, "ret_bwd", [scatter(p_f2w1, p_f2w3), SiblingExchange([g_wout])])
    p_wout = reduce_in_chip("w_out", (g_wout, r_wout))
    (g_win,), ((q_wout,),) = mix_dwin(u, dproj, N_CHIPS, "mix_dwin", [scatter(p_wout)])
    (dh1, d_mix), ((r_win,),) = mix_in_bwd(dproj, wg["w_in"], h1, mix_norm, dh2, "mix_in_bwd", [SiblingExchange([g_win])])
    p_win = reduce_in_chip("w_in", (g_win, r_win))
    (da1, db1, df1), ((q_win,),) = ffn_bwd_act(dh1, whole("ffn1_w2"), ga1, gb1, "ffn1_bwd_act", [scatter(p_win)])
    d_small = {"pool_w": d_pool_w.reshape(-1, LANES), "mix_norm": d_mix, "pool_scale": d_pool_scale, "ret_norm": d_ret_norm,
               "ffn2_norm": d_ffn2, "final_norm": d_final}
    (g_f1w1, g_f1w3), (packs,) = ffn_dw([da1, db1], n1, 2, "ffn1_dw13", [AllExchange([d_small[k] for k in SMALL[:-1]] + [loss])])
    g_f1w1, g_f1w3 = sharded(g_f1w1), sharded(g_f1w3)
    (g_f1w2,), ((r_f1w1, r_f1w3),) = ffn_dw([s1], df1, 1, "ffn1_dw2", [SiblingExchange([g_f1w1, g_f1w3])])
    g_f1w2 = sharded(g_f1w2)
    p_f1w1, p_f1w3 = reduce_in_chip("ffn1_w13", (g_f1w1, r_f1w1), (g_f1w3, r_f1w3))
    (dx, d_ffn1), ((q_f1w1, q_f1w3), (r_f1w2,)) = ffn_bwd_in(
        da1, db1, whole("ffn1_w1"), whole("ffn1_w3"), xs, ffn1_norm, dh1, "ffn1_bwd_in",
        [scatter(p_f1w1, p_f1w3), SiblingExchange([g_f1w2])])
    p_f1w2 = reduce_in_chip("ffn1_w2", (g_f1w2, r_f1w2))

    (q_f1w2,), (late,) = update(["w_in", "w_out", "ffn2_w2"], [q_win, q_wout, q_f2w2], "adamw_mix_w2",
                                [scatter(p_f1w2), AllExchange([d_ffn1])])
    update(["ffn2_w1", "ffn2_w3", "ffn1_w1", "ffn1_w3"], [q_f2w1, q_f2w3, q_f1w1, q_f1w3], "adamw_w13")
    update(["ffn1_w2"], [q_f1w2], "adamw_ffn1_w2")
    flat = lambda t, k: t[k].reshape(-1, LANES) if k == "pool_w" else t[k].reshape(1, -1)
    updated, loss_sum = adamw_small(packs[:-1] + [late], [[flat(t, k) for t in (w, m, v)] for k in SMALL], packs[-1], "adamw_small")
    for k, outs in zip(SMALL, updated):
        grad[k], delta[k], new_m[k], new_v[k] = [o.reshape(w[k].shape) for o in outs]
    loss = loss_sum[0, 0]

    return (loss, dx[None], *[grad[k] for k in WEIGHTS], *[delta[k] for k in WEIGHTS],
            *[new_m[k] for k in WEIGHTS], *[new_v[k] for k in WEIGHTS])
```

```python
import math

import jax
import jax.numpy as jnp
import numpy as np
from jax import lax
from jax.experimental import pallas as pl
from jax.experimental.pallas import tpu as pltpu

F32 = jnp.float32
BF16 = jnp.bfloat16

EPS = 1e-6
LANES = 128
BF16_TILE_ROWS = 16
N_CHIPS = 4
N_GROUPS = 4
HEAD_DIM = 128
RET_CHUNK = 128
ROPE_BASE = 10000.0
ADAM_LR, ADAM_B1, ADAM_B2, ADAM_EPS, ADAM_WD, ADAM_STEP = 0.001, 0.9, 0.999, 1e-08, 0.01, 10
VMEM_LIMIT_V7X = 56 * 1024 * 1024
ADAMW_VMEM_BUDGET = 32 * 1024 * 1024
MESH = pl.DeviceIdType.MESH
ANY = pl.BlockSpec(memory_space=pl.ANY)


def _dot(a, b):
    return jnp.dot(a, b, preferred_element_type=F32)


def _dot_nt(a, b):
    return lax.dot_general(a, b, (((1,), (1,)), ((), ())), preferred_element_type=F32)


def _dot_tn(a, b):
    return lax.dot_general(a, b, (((0,), (0,)), ((), ())), preferred_element_type=F32)


def _rstd(h):
    return lax.rsqrt(jnp.mean(h * h, axis=-1, keepdims=True) + EPS)


def _rmsnorm_bwd(dn, h, gain):
    r = _rstd(h)
    nh = h * r
    dnh = dn * gain
    dh = r * (dnh - nh * jnp.mean(dnh * nh, axis=-1, keepdims=True))
    return dh, dn * nh


def _silu_parts(a):
    sig = jax.nn.sigmoid(a)
    silu = a * sig
    return silu, sig + silu * (1.0 - sig)


def _mesh_pos():
    return lax.axis_index("x"), lax.axis_index("y"), lax.axis_index("c")


class ChipExchange:
    def __init__(self, srcs, scatter, placed=()):
        n = len(srcs)
        self.inputs, self.scatter, self.n, self.reach = list(srcs) + list(placed), scatter, n, REACH_CHIPS
        self.aliases = {n + t: t for t in range(n)} if scatter else {}
        self.half_rows = [s.shape[1] if scatter else s.shape[0] // 2 for s in srcs]
        self.out_shape = [jax.ShapeDtypeStruct((N_CHIPS, 2 * rh, s.shape[-1]), s.dtype) for s, rh in zip(srcs, self.half_rows)]
        if scatter:
            self.out_shape += [jax.ShapeDtypeStruct((2, rh // 2, s.shape[-1]), s.dtype) for s, rh in zip(srcs, self.half_rows)]
        dma = pltpu.SemaphoreType.DMA
        self.sems = [dma((4 * n,)), dma((4 * n,)), dma((2 * n,)), dma((2 * n,)), dma((4 * n,)), dma((4 * n,))]

    def _copies(self, src, out, sems):
        hop1_send, hop1_recv, hop2_send, hop2_recv, d2d_send, d2d_recv = sems
        x, y, c = _mesh_pos()
        me, dg = 2 * x + y, 2 * (1 - x) + (1 - y)
        sibling = (x, y, 1 - c)
        n = self.n
        mine, theirs = c, 1 - c

        def nb(a):
            nx, ny = x ^ (1 - a), y ^ a
            return 2 * nx + ny, (nx, ny, c)

        def remote(s, d, send, recv, k, to):
            return pltpu.make_async_remote_copy(src_ref=s, dst_ref=d, send_sem=send.at[k], recv_sem=recv.at[k],
                                                device_id=to, device_id_type=MESH)

        class Copies:
            def slot(_, t, chip, half):
                rh = self.half_rows[t]
                return out[t].at[chip, pl.ds(half * rh, rh), :]

            def quarter(_, t, chip, q):
                qh = self.half_rows[t] // 2
                return out[t].at[chip, pl.ds(mine * 2 * qh + q * qh, qh), :]

            def own_shard(k, t):
                return remote(src[t], out[t].at[me], d2d_send, d2d_recv, 4 * t + 3, sibling)

            def hop1(k, t, a, transit=False):
                rh = self.half_rows[t]
                chip, to = nb(a)
                if transit:
                    piece = src[t].at[dg, pl.ds(a * (rh // 2), rh // 2), :]
                    return remote(piece, out[n + t].at[a], hop1_send, hop1_recv, 4 * t + 2 + a, to)
                piece = src[t].at[chip] if self.scatter else src[t].at[pl.ds(mine * rh, rh), :]
                return remote(piece, k.slot(t, me, mine), hop1_send, hop1_recv, 4 * t + a, to)

            def landed1(k, t, a, transit=False):
                here = out[n + t].at[a] if transit else k.slot(t, nb(a)[0], mine)
                return remote(here, here, hop1_send, hop1_recv, 4 * t + (2 if transit else 0) + a, sibling)

            def hop2(k, t, q):
                origin, to = nb(q)[0], nb(1 - q)[1]
                piece = out[n + t].at[q] if self.scatter else k.quarter(t, origin, q)
                return remote(piece, k.quarter(t, origin, q), hop2_send, hop2_recv, 2 * t + q, to)

            def landed2(k, t, q):
                here = k.quarter(t, dg, q)
                return remote(here, here, hop2_send, hop2_recv, 2 * t + q, sibling)

            def d2d(k, t, p, chip, own=False, arriving=False):
                if arriving:
                    there = k.slot(t, chip, theirs)
                    return remote(there, there, d2d_send, d2d_recv, 4 * t + p, sibling)
                piece = src[t].at[me] if own else k.slot(t, chip, mine)
                return remote(piece, k.slot(t, chip, mine), d2d_send, d2d_recv, 4 * t + p, sibling)

        return Copies(), nb, me, dg, c

    def start(self, src, out, sems):
        k, nb, me, dg, c = self._copies(src, out, sems)
        for t in range(self.n):
            for first in range(2):
                a = first ^ c
                k.hop1(t, a).start()
                if self.scatter:
                    k.hop1(t, a, transit=True).start()
            if self.scatter:
                k.d2d(t, 3, me, own=True).start()
            else:
                k.own_shard(t).start()

    def mid(self, src, out, sems):
        k, nb, me, dg, c = self._copies(src, out, sems)
        for t in range(self.n):
            for first in range(2):
                a = first ^ c
                if self.scatter:
                    k.landed1(t, a, transit=True).wait_recv()
                    k.hop2(t, a).start()
                k.landed1(t, a).wait_recv()
                if not self.scatter:
                    k.hop2(t, a).start()
                k.d2d(t, a, nb(a)[0]).start()

    def finish(self, src, out, sems):
        k, nb, me, dg, c = self._copies(src, out, sems)
        for t in range(self.n):
            for q in range(2):
                k.landed2(t, q).wait_recv()
            k.d2d(t, 2, dg).start()
        for t in range(self.n):
            for a in range(2):
                k.d2d(t, a, nb(a)[0], arriving=True).wait_recv()
            k.d2d(t, 2, dg, arriving=True).wait_recv()
            if self.scatter:
                k.d2d(t, 3, me, arriving=True).wait_recv()
        for t in range(self.n):
            for a in range(2):
                k.hop1(t, a).wait_send()
                if self.scatter:
                    k.hop1(t, a, transit=True).wait_send()
                k.hop2(t, a).wait_send()
                k.d2d(t, a, nb(a)[0]).wait_send()
            k.d2d(t, 2, dg).wait_send()
            if self.scatter:
                k.d2d(t, 3, me, own=True).wait_send()
            else:
                k.own_shard(t).wait()


class SiblingExchange:
    def __init__(self, grads):
        self.inputs, self.n, self.aliases, self.reach = list(grads), len(grads), {}, REACH_SIBLING
        self.half_rows = [g.shape[1] // 2 for g in grads]
        self.out_shape = [jax.ShapeDtypeStruct((g.shape[0], rh, g.shape[2]), g.dtype) for g, rh in zip(grads, self.half_rows)]
        self.sems = [pltpu.SemaphoreType.DMA((self.n,)), pltpu.SemaphoreType.DMA((self.n,))]

    def _plan(self, src, out, sems):
        x, y, c = _mesh_pos()
        return [pltpu.make_async_remote_copy(
            src_ref=src[t].at[:, pl.ds((1 - c) * self.half_rows[t], self.half_rows[t]), :], dst_ref=out[t],
            send_sem=sems[0].at[t], recv_sem=sems[1].at[t], device_id=(x, y, 1 - c), device_id_type=MESH) for t in range(self.n)]

    def start(self, src, out, sems):
        for cp in self._plan(src, out, sems):
            cp.start()

    def mid(self, src, out, sems):
        pass

    def finish(self, src, out, sems):
        for cp in self._plan(src, out, sems):
            cp.wait()


REACH_SIBLING, REACH_CHIPS, REACH_ALL = 0, 1, 2


def _entry_barrier(reach):
    x, y, c = _mesh_pos()
    peers = [(x, y, 1 - c)]
    if reach == REACH_CHIPS:
        peers += [(1 - x, y, c), (x, 1 - y, c)]
    elif reach == REACH_ALL:
        peers = [(x ^ dx, y ^ dy, c ^ dc) for dx in (0, 1) for dy in (0, 1) for dc in (0, 1)][1:]
    barrier = pltpu.get_barrier_semaphore()
    for peer in peers:
        pl.semaphore_signal(barrier, inc=1, device_id=peer, device_id_type=MESH)
    pl.semaphore_wait(barrier, len(peers))


def _call(body, hosted=(), *, name, in_specs, out_specs, out_shape, args, grid=(), scratch_shapes=(), aliased=None):
    n_in, n_out, n_scr = len(in_specs), len(out_specs), len(scratch_shapes)
    total = math.prod(grid)
    mid_step = max(0, (3 * total) // 4 - 1)

    def full(*refs):
        pos = [0]

        def take(k):
            pos[0] += k
            return refs[pos[0] - k:pos[0]]

        ins, h_in = take(n_in), [take(len(h.inputs)) for h in hosted]
        outs, h_out = take(n_out), [take(len(h.out_shape)) for h in hosted]
        scr, h_sem = take(n_scr), [take(len(h.sems)) for h in hosted]
        step = 0
        for axis, size in enumerate(grid):
            step = step * size + pl.program_id(axis)

        def phase(at, method):
            if not hosted:
                return

            def run():
                if method == "start":
                    _entry_barrier(reach)
                for h, s, o, m in zip(hosted, h_in, h_out, h_sem):
                    getattr(h, method)(s, o, m)

            if total == 1:
                run()
            else:
                pl.when(step == at)(run)

        phase(0, "start")
        body(*ins, *outs, *scr)
        phase(mid_step, "mid")
        phase(total - 1, "finish")

    aliases, i0, o0 = dict(aliased or {}), n_in, n_out
    for h in hosted:
        aliases.update({i0 + i: o0 + o for i, o in h.aliases.items()})
        i0, o0 = i0 + len(h.inputs), o0 + len(h.out_shape)
    reach = max((h.reach for h in hosted), default=None)
    params = dict(vmem_limit_bytes=VMEM_LIMIT_V7X)
    if hosted:
        params["collective_id"] = reach
    results = pl.pallas_call(
        full, name=name, grid=grid,
        in_specs=list(in_specs) + [ANY] * (i0 - n_in),
        out_specs=list(out_specs) + [ANY] * (o0 - n_out),
        out_shape=list(out_shape) + [s for h in hosted for s in h.out_shape],
        scratch_shapes=list(scratch_shapes) + [s for h in hosted for s in h.sems],
        input_output_aliases=aliases,
        compiler_params=pltpu.CompilerParams(**params),
    )(*args, *[s for h in hosted for s in h.inputs])
    outs, extras, pos = list(results[:n_out]), [], n_out
    for h in hosted:
        extras.append(list(results[pos:pos + h.n]))
        pos += len(h.out_shape)
    return outs, extras


def cast_shards(shards, name, hosted=()):
    n = len(shards)

    def body(*refs):
        for x_ref, o_ref in zip(refs[:n], refs[n:]):
            o_ref[...] = x_ref[...].astype(BF16)

    whole = lambda s: pl.BlockSpec(s.shape, lambda: (0,) * s.ndim)
    return _call(body, hosted, name=name, in_specs=[whole(s) for s in shards], out_specs=[whole(s) for s in shards],
                 out_shape=[jax.ShapeDtypeStruct(s.shape, BF16) for s in shards], args=list(shards))


class AllExchange:
    def __init__(self, arrays):
        n = len(arrays)
        self.inputs, self.n, self.aliases, self.reach = list(arrays), n, {}, REACH_ALL
        self.out_shape = [jax.ShapeDtypeStruct((2 * N_CHIPS,) + a.shape, a.dtype) for a in arrays]
        self.sems = [pltpu.SemaphoreType.DMA((n,)), pltpu.SemaphoreType.DMA((7 * n,)), pltpu.SemaphoreType.DMA((7 * n,))]

    def _copies(self, src, out, sems):
        local_sem, send_sem, recv_sem = sems
        x, y, c = _mesh_pos()
        me = 4 * x + 2 * y + c
        peers = [(x ^ dx, y ^ dy, c ^ dc) for dx in (0, 1) for dy in (0, 1) for dc in (0, 1)][1:]
        remote = lambda s, d, k, to: pltpu.make_async_remote_copy(
            src_ref=s, dst_ref=d, send_sem=send_sem.at[k], recv_sem=recv_sem.at[k], device_id=to, device_id_type=MESH)
        sends, landed, local = [], [], []
        for t in range(self.n):
            local.append(pltpu.make_async_copy(src[t], out[t].at[me], local_sem.at[t]))
            for p, (px, py, pc) in enumerate(peers):
                sends.append(remote(src[t], out[t].at[me], 7 * t + p, (px, py, pc)))
                here = out[t].at[4 * px + 2 * py + pc]
                landed.append(remote(here, here, 7 * t + p, (px, py, pc)))
        return sends, landed, local

    def start(self, src, out, sems):
        sends, _, local = self._copies(src, out, sems)
        for cp in sends + local:
            cp.start()

    def mid(self, src, out, sems):
        pass

    def finish(self, src, out, sems):
        sends, landed, local = self._copies(src, out, sems)
        for cp in landed:
            cp.wait_recv()
        for cp in sends:
            cp.wait_send()
        for cp in local:
            cp.wait()


MXU_COLS = 256


def _resident(shape):
    return pl.BlockSpec(shape, lambda *_: (0,) * len(shape), pipeline_mode=pl.Buffered(1))


def ffn_up(h, gain, w1, w3, name, hosted=(), mixed=None):
    T, D = h.shape
    F = w1.shape[0]
    tm = min(T, 256)

    def body(*refs):
        if mixed is None:
            h_ref, g_ref, w1_ref, w3_ref, n_ref, ga_ref, gb_ref, s_ref = refs
            hh = h_ref[...]
        else:
            pa_ref, rb_ref, wo_ref, h_ref, g_ref, w1_ref, w3_ref, hh_ref, n_ref, ga_ref, gb_ref, s_ref = refs
            hh = h_ref[...] + _dot(pa_ref[...], wo_ref[0]) + _dot(rb_ref[...], wo_ref[1])
            hh_ref[...] = hh
        n = (hh * _rstd(hh) * g_ref[...]).astype(BF16)
        n_ref[...] = n
        for c in range(0, F, MXU_COLS):
            cols = slice(c, c + MXU_COLS)
            a = _dot_nt(n, w1_ref[cols, :])
            b = _dot_nt(n, w3_ref[cols, :])
            silu, dsilu = _silu_parts(a)
            ga_ref[:, cols] = (b * dsilu).astype(BF16)
            gb_ref[:, cols] = silu.astype(BF16)
            s_ref[:, cols] = (silu * b).astype(BF16)

    act = jax.ShapeDtypeStruct((T, F), BF16)
    act_spec = pl.BlockSpec((tm, F), lambda i: (i, 0))
    row_spec = pl.BlockSpec((tm, D), lambda i: (i, 0))
    in_specs = [row_spec, pl.BlockSpec((1, D), lambda i: (0, 0)), _resident((F, D)), _resident((F, D))]
    out_specs, out_shape, args = [row_spec, act_spec, act_spec, act_spec], [jax.ShapeDtypeStruct((T, D), BF16), act, act, act], [h, gain, w1, w3]
    if mixed is not None:
        pa, rb, woutg = mixed
        W = pa.shape[1]
        in_specs = [pl.BlockSpec((tm, W), lambda i: (i, 0))] * 2 + [_resident((2, W, D))] + in_specs
        out_specs, out_shape = [row_spec] + out_specs, [jax.ShapeDtypeStruct((T, D), F32)] + out_shape
        args = [pa, rb, woutg.reshape(2, W, D)] + args
    return _call(body, hosted, name=name, grid=(T // tm,), in_specs=in_specs, out_specs=out_specs, out_shape=out_shape, args=args)


def ffn_bwd_act(dh, w2, ga, gb, name, hosted=()):
    T, D = dh.shape
    F = w2.shape[0]
    tm = min(T, 512)

    def body(dh_ref, w2_ref, ga_ref, gb_ref, da_ref, db_ref, df_ref):
        df = (0.5 * dh_ref[...]).astype(BF16)
        df_ref[...] = df
        for c in range(0, F, MXU_COLS):
            cols = slice(c, c + MXU_COLS)
            ds = _dot_nt(df, w2_ref[cols, :])
            da_ref[:, cols] = (ds * ga_ref[:, cols].astype(F32)).astype(BF16)
            db_ref[:, cols] = (ds * gb_ref[:, cols].astype(F32)).astype(BF16)

    act = jax.ShapeDtypeStruct((T, F), BF16)
    act_spec = pl.BlockSpec((tm, F), lambda i: (i, 0))
    row_spec = pl.BlockSpec((tm, D), lambda i: (i, 0))
    return _call(
        body, hosted, name=name, grid=(T // tm,),
        in_specs=[row_spec, _resident((F, D)), act_spec, act_spec],
        out_specs=[act_spec, act_spec, row_spec],
        out_shape=[act, act, jax.ShapeDtypeStruct((T, D), BF16)],
        args=[dh, w2, ga, gb])


def ffn_dw(xs, y, halves, name, hosted=()):
    T, F = xs[0].shape
    D = y.shape[1]
    nx, fh = len(xs), F // halves
    tk = min(T, 1024)
    nk = T // tk

    def body(*refs):
        y_ref, x_refs, o_refs, accs = refs[0], refs[1:1 + nx], refs[1 + nx:1 + 2 * nx], refs[1 + 2 * nx:]
        k = pl.program_id(1)

        @pl.when(k == 0)
        def _():
            for acc in accs:
                acc[...] = jnp.zeros_like(acc)

        yy = y_ref[...]
        for x_ref, acc in zip(x_refs, accs):
            acc[...] += _dot_tn(x_ref[...], yy)

        @pl.when(k == nk - 1)
        def _():
            for o_ref, acc in zip(o_refs, accs):
                o_ref[...] = acc[...].astype(BF16)

    out = jax.ShapeDtypeStruct((F, D), BF16)
    return _call(
        body, hosted, name=name, grid=(halves, nk),
        in_specs=[pl.BlockSpec((tk, D), lambda j, k: (k, 0))] + [pl.BlockSpec((tk, fh), lambda j, k: (k, j))] * nx,
        out_specs=[pl.BlockSpec((fh, D), lambda j, k: (j, 0))] * nx,
        out_shape=[out] * nx,
        scratch_shapes=[pltpu.VMEM((fh, D), F32)] * nx,
        args=[y] + list(xs))


def ffn_bwd_in(da, db, w1, w3, h, gain, dh, name, hosted=()):
    T, F = da.shape
    D = h.shape[1]
    tm = min(T, 256)

    def body(da_ref, db_ref, w1_ref, w3_ref, h_ref, g_ref, dh_ref, o_ref, dg_ref):
        dn = _dot(da_ref[...], w1_ref[...]) + _dot(db_ref[...], w3_ref[...])
        dhn, dg = _rmsnorm_bwd(dn, h_ref[...], g_ref[...])
        o_ref[...] = dh_ref[...] + dhn

        @pl.when(pl.program_id(0) == 0)
        def _():
            dg_ref[...] = jnp.zeros_like(dg_ref)

        dg_ref[...] += jnp.sum(dg, axis=0, keepdims=True)

    act_spec = pl.BlockSpec((tm, F), lambda i: (i, 0))
    row_spec = pl.BlockSpec((tm, D), lambda i: (i, 0))
    vec_spec = pl.BlockSpec((1, D), lambda i: (0, 0))
    return _call(
        body, hosted, name=name, grid=(T // tm,),
        in_specs=[act_spec, act_spec, _resident((F, D)), _resident((F, D)), row_spec, vec_spec, row_spec],
        out_specs=[row_spec, vec_spec],
        out_shape=[jax.ShapeDtypeStruct((T, D), F32), jax.ShapeDtypeStruct((1, D), F32)],
        args=[da, db, w1, w3, h, gain, dh])


def ffn_down_mix_in(s, w2, h, gain, wing, name, hosted=()):
    T, F = s.shape
    D = h.shape[1]
    nsh, _, Cs = wing.shape
    tm = min(T, 512)

    def body(s_ref, w2_ref, h_ref, g_ref, w_ref, hh_ref, u_ref, p_ref):
        hh = h_ref[...] + 0.5 * _dot(s_ref[...], w2_ref[...])
        hh_ref[...] = hh
        u = (hh * _rstd(hh) * g_ref[...]).astype(BF16)
        u_ref[...] = u
        for j in range(nsh):
            p_ref[:, j * Cs:(j + 1) * Cs] = _dot(u, w_ref[j])

    row_spec = pl.BlockSpec((tm, D), lambda i: (i, 0))
    return _call(
        body, hosted, name=name, grid=(T // tm,),
        in_specs=[pl.BlockSpec((tm, F), lambda i: (i, 0)), _resident((F, D)), row_spec, pl.BlockSpec((1, D), lambda i: (0, 0)),
                  _resident((nsh, D, Cs))],
        out_specs=[row_spec, row_spec, pl.BlockSpec((tm, nsh * Cs), lambda i: (i, 0))],
        out_shape=[jax.ShapeDtypeStruct((T, D), F32), jax.ShapeDtypeStruct((T, D), BF16), jax.ShapeDtypeStruct((T, nsh * Cs), F32)],
        args=[s, w2, h, gain, wing])


def mix_out_bwd(dh, woutg, a, b, name, hosted=()):
    T, D = dh.shape
    W = a.shape[1]
    nsh, Rs, _ = woutg.shape
    wout = woutg.reshape(2, W, D)
    tk = min(T, 512)
    nk = T // tk

    def body(dh_ref, w_ref, a_ref, b_ref, da_ref, db_ref, dw_ref, acc):
        k = pl.program_id(0)

        @pl.when(k == 0)
        def _():
            acc[...] = jnp.zeros_like(acc)

        dhb = dh_ref[...].astype(BF16)
        da_ref[...] = _dot_nt(dhb, w_ref[0])
        db_ref[...] = _dot_nt(dhb, w_ref[1])
        acc[0:W, :] += _dot_tn(a_ref[...], dhb)
        acc[W:2 * W, :] += _dot_tn(b_ref[...], dhb)

        @pl.when(k == nk - 1)
        def _():
            for j in range(nsh):
                dw_ref[j] = acc[j * Rs:(j + 1) * Rs, :].astype(BF16)

    return _call(
        body, hosted, name=name, grid=(nk,),
        in_specs=[pl.BlockSpec((tk, D), lambda k: (k, 0)), pl.BlockSpec((2, W, D), lambda k: (0, 0, 0)),
                  pl.BlockSpec((tk, W), lambda k: (k, 0)), pl.BlockSpec((tk, W), lambda k: (k, 0))],
        out_specs=[pl.BlockSpec((tk, W), lambda k: (k, 0)), pl.BlockSpec((tk, W), lambda k: (k, 0)),
                   pl.BlockSpec((nsh, Rs, D), lambda k: (0, 0, 0))],
        out_shape=[jax.ShapeDtypeStruct((T, W), F32), jax.ShapeDtypeStruct((T, W), F32),
                   jax.ShapeDtypeStruct((nsh, Rs, D), BF16)],
        scratch_shapes=[pltpu.VMEM((2 * W, D), F32)],
        args=[dh, wout, a, b])


def _dproj_block(g):
    return (g // N_GROUPS + N_GROUPS) % (N_GROUPS + 1), g % N_GROUPS


def mix_dwin(u, dproj, nsh, name, hosted=()):
    T, D = u.shape
    Hd = HEAD_DIM
    slabs, _, width = dproj.shape
    blocks = slabs * width // Hd
    Cs = blocks * Hd // nsh
    tk = min(T, 512)
    nk = T // tk

    def body(u_ref, d_ref, o_ref, acc):
        k = pl.program_id(0)

        @pl.when(k == 0)
        def _():
            acc[...] = jnp.zeros_like(acc)

        where = [_dproj_block(g) for g in range(blocks)]
        d = jnp.concatenate([d_ref[slab, :, col * Hd:(col + 1) * Hd] for slab, col in where], axis=1)
        acc[...] += _dot_tn(u_ref[...], d)

        @pl.when(k == nk - 1)
        def _():
            for j in range(nsh):
                o_ref[j] = acc[:, j * Cs:(j + 1) * Cs].astype(BF16)

    return _call(
        body, hosted, name=name, grid=(nk,),
        in_specs=[pl.BlockSpec((tk, D), lambda k: (k, 0)), pl.BlockSpec((slabs, tk, width), lambda k: (0, k, 0))],
        out_specs=[pl.BlockSpec((nsh, D, Cs), lambda k: (0, 0, 0))],
        out_shape=[jax.ShapeDtypeStruct((nsh, D, Cs), BF16)],
        scratch_shapes=[pltpu.VMEM((D, blocks * Hd), F32)],
        args=[u, dproj])


def mix_in_bwd(dproj, wing, h, gain, dh, name, hosted=()):
    T, D = h.shape
    nsh, _, Cs = wing.shape
    Hd = HEAD_DIM
    per = Cs // Hd
    tm = min(T, 512)

    def body(d_ref, w_ref, h_ref, g_ref, dh_ref, o_ref, dg_ref):
        def shard(j):
            blocks = [_dproj_block(per * j + i) for i in range(per)]
            return jnp.concatenate([d_ref[slab, :, col * Hd:(col + 1) * Hd] for slab, col in blocks], axis=1)

        du = _dot_nt(shard(0), w_ref[0])
        for j in range(1, nsh):
            du += _dot_nt(shard(j), w_ref[j])
        dhn, dg = _rmsnorm_bwd(du, h_ref[...], g_ref[...])
        o_ref[...] = dh_ref[...] + dhn

        @pl.when(pl.program_id(0) == 0)
        def _():
            dg_ref[...] = jnp.zeros_like(dg_ref)

        dg_ref[...] += jnp.sum(dg, axis=0, keepdims=True)

    row_spec = pl.BlockSpec((tm, D), lambda i: (i, 0))
    vec_spec = pl.BlockSpec((1, D), lambda i: (0, 0))
    return _call(
        body, hosted, name=name, grid=(T // tm,),
        in_specs=[pl.BlockSpec((dproj.shape[0], tm, dproj.shape[2]), lambda i: (0, i, 0)),
                  pl.BlockSpec((nsh, D, Cs), lambda i: (0, 0, 0)), row_spec, vec_spec, row_spec],
        out_specs=[row_spec, vec_spec],
        out_shape=[jax.ShapeDtypeStruct((T, D), F32), jax.ShapeDtypeStruct((1, D), F32)],
        args=[dproj, wing, h, gain, dh])


POOL_WINDOWS = (2, 4, 8, 16)


def _pool_window(x, window, T, trailing):
    rows = lax.broadcasted_iota(jnp.int32, x.shape, 0)
    s, k = x, 1
    while k < window:
        if trailing:
            s = s + jnp.where(rows >= k, pltpu.roll(s, k, 0), 0.0)
        else:
            s = s + jnp.where(rows < T - k, pltpu.roll(s, T - k, 0), 0.0)
        k *= 2
    return s


def _pool_count(window, shape):
    rows = lax.broadcasted_iota(jnp.int32, shape, 0)
    return jnp.minimum(rows + 1, window).astype(F32)


def _per_group(work):
    for group, window in enumerate(POOL_WINDOWS):
        pl.when(pl.program_id(0) == group)(lambda window=window: work(window))


def pool_fwd(proj, pool_w, pool_scale, name, hosted=()):
    T = proj.shape[0]
    Hd = HEAD_DIM

    def body(x_ref, w_ref, sc_ref, a_ref):
        def work(window):
            x = x_ref[...]
            pooled = _pool_window(x, window, T, True) / _pool_count(window, x.shape) - x
            a_ref[...] = (_dot(pooled.astype(BF16), w_ref[0].astype(BF16)) * sc_ref[...]).astype(BF16)

        _per_group(work)

    return _call(
        body, hosted, name=name, grid=(N_GROUPS,),
        in_specs=[pl.BlockSpec((T, Hd), lambda g: (0, g)), pl.BlockSpec((1, Hd, Hd), lambda g: (g, 0, 0)),
                  pl.BlockSpec((1, Hd), lambda g: (0, g))],
        out_specs=[pl.BlockSpec((T, Hd), lambda g: (0, g))],
        out_shape=[jax.ShapeDtypeStruct((T, N_GROUPS * Hd), BF16)],
        args=[proj, pool_w, pool_scale])


def pool_bwd(proj, da, pool_w, pool_scale, name, hosted=()):
    T = proj.shape[0]
    Hd = HEAD_DIM

    def body(x_ref, da_ref, w_ref, sc_ref, dx_ref, dw_ref, dsc_ref):
        def work(window):
            x = x_ref[...]
            cnt = _pool_count(window, x.shape)
            pooled = (_pool_window(x, window, T, True) / cnt - x).astype(BF16)
            wb = w_ref[0].astype(BF16)
            dav = da_ref[...]
            dsc_ref[...] = jnp.sum(dav * _dot(pooled, wb), axis=0, keepdims=True)
            dout = (dav * sc_ref[...]).astype(BF16)
            dw_ref[0] = _dot_tn(pooled, dout)
            dpooled = _dot_nt(dout, wb)
            dx_ref[0] = (_pool_window(dpooled / cnt, window, T, False) - dpooled).astype(BF16)

        _per_group(work)

    col_spec = pl.BlockSpec((T, Hd), lambda g: (0, g))
    return _call(
        body, hosted, name=name, grid=(N_GROUPS,),
        in_specs=[col_spec, col_spec, pl.BlockSpec((1, Hd, Hd), lambda g: (g, 0, 0)), pl.BlockSpec((1, Hd), lambda g: (0, g))],
        out_specs=[pl.BlockSpec((1, T, Hd), lambda g: (N_GROUPS, 0, g)), pl.BlockSpec((1, Hd, Hd), lambda g: (g, 0, 0)),
                   pl.BlockSpec((1, Hd), lambda g: (0, g))],
        out_shape=[jax.ShapeDtypeStruct((N_GROUPS + 1, T, N_GROUPS * Hd), BF16), jax.ShapeDtypeStruct((N_GROUPS, Hd, Hd), F32),
                   jax.ShapeDtypeStruct((1, N_GROUPS * Hd), F32)],
        args=[proj, da, pool_w, pool_scale])


def _ret_tables(T):
    Hd, C, f32 = HEAD_DIM, RET_CHUNK, np.float32
    inv_freq = (1.0 / (ROPE_BASE ** (np.arange(0, Hd, 2, dtype=np.float64) / Hd))).astype(f32)
    ang = np.arange(T, dtype=f32)[:, None] * inv_freq[None, :]
    cos, sin = np.cos(ang), np.sin(ang)
    cos2 = np.concatenate([cos, cos], axis=-1)
    sin2 = np.concatenate([-sin, sin], axis=-1)
    log_gamma = np.log1p(-np.exp2(f32(-5.0) - np.arange(N_GROUPS, dtype=f32)))
    pos = np.arange(C, dtype=f32)
    rel = pos[:, None] - pos[None, :]
    intra = np.where(rel[None] >= 0, np.exp(log_gamma[:, None, None] * np.maximum(rel, f32(0.0))[None]), f32(0.0))
    k_tail = np.exp(log_gamma[:, None] * (f32(C - 1) - pos)[None, :])
    q_head = np.exp(log_gamma[:, None] * (pos + f32(1.0))[None, :])
    chunk_decay = np.exp(log_gamma * f32(C))
    wide = lambda t: np.broadcast_to(t[:, :, None], (N_GROUPS, C, Hd))
    tables = cos2, sin2, intra, wide(k_tail), wide(q_head), np.broadcast_to(chunk_decay[:, None, None], (N_GROUPS, 1, Hd))
    assert all(t.dtype == f32 for t in tables)
    return tuple(jnp.asarray(t) for t in tables)


def _rope(x, cos2, sin2):
    return x * cos2 + pltpu.roll(x, HEAD_DIM // 2, 1) * sin2


def _rope_t(d, cos2, sin2):
    return d * cos2 + pltpu.roll(d * sin2, HEAD_DIM // 2, 1)


def _ret_specs(T, tseg, seg_of):
    Hd, G = HEAD_DIM, N_GROUPS
    col = lambda kind: pl.BlockSpec((tseg, Hd), lambda h, s: (seg_of(s), G * kind + h))
    tab = pl.BlockSpec((T, Hd), lambda h, s: (0, 0))
    head = pl.BlockSpec((1, RET_CHUNK, Hd), lambda h, s: (h, 0, 0))
    cd = pl.BlockSpec((1, 1, Hd), lambda h, s: (h, 0, 0))
    gain = pl.BlockSpec((1, Hd), lambda h, s: (0, h))
    return col, tab, head, cd, gain


def ret_fwd(proj, ret_norm, tables, name, hosted=()):
    T = proj.shape[0]
    Hd, C, G = HEAD_DIM, RET_CHUNK, N_GROUPS
    tseg = min(T, 2048)
    nseg, nck = T // tseg, tseg // C
    scale = Hd ** -0.5
    cos2, sin2, intra, k_tail, q_head, chunk_decay = tables

    def body(q_ref, k_ref, v_ref, g_ref, gain_ref, cos_ref, sin_ref, m_ref, kt_ref, qh_ref, cd_ref,
             b_ref, o_ref, rp_ref, state):
        @pl.when(pl.program_id(1) == 0)
        def _():
            state[...] = jnp.zeros_like(state)

        def chunk(ci, carry):
            rows = pl.ds(pl.multiple_of(ci * C, C), C)
            at = pl.ds(pl.multiple_of(pl.program_id(1) * tseg + ci * C, C), C)
            cos, sin = cos_ref[at, :], sin_ref[at, :]
            qr = _rope(q_ref[rows, :], cos, sin)
            kr = _rope(k_ref[rows, :], cos, sin) * scale
            qb, kb, vb = qr.astype(BF16), kr.astype(BF16), v_ref[rows, :].astype(BF16)
            r = state[...]
            rp_ref[0, ci] = r.astype(BF16)
            sc = _dot_nt(qb, kb) * m_ref[0]
            o = _dot(sc.astype(BF16), vb) + _dot((qr * qh_ref[0]).astype(BF16), r.astype(BF16))
            state[...] = cd_ref[0] * r + _dot_tn((kr * kt_ref[0]).astype(BF16), vb)
            o_ref[rows, :] = o
            on = o * _rstd(o)
            b_ref[rows, :] = (jax.nn.silu(g_ref[rows, :]) * (on * gain_ref[...])).astype(BF16)
            return carry

        lax.fori_loop(0, nck, chunk, 0, unroll=True)

    col, tab, head, cd, gain = _ret_specs(T, tseg, lambda s: s)
    out_col = pl.BlockSpec((tseg, Hd), lambda h, s: (s, h))
    return _call(
        body, hosted, name=name, grid=(G, nseg),
        in_specs=[col(1), col(2), col(3), col(4), gain, tab, tab, head, head, head, cd],
        out_specs=[out_col, out_col, pl.BlockSpec((1, nck, Hd, Hd), lambda h, s: (h, s, 0, 0))],
        out_shape=[jax.ShapeDtypeStruct((T, G * Hd), BF16), jax.ShapeDtypeStruct((T, G * Hd), F32),
                   jax.ShapeDtypeStruct((G, T // C, Hd, Hd), BF16)],
        scratch_shapes=[pltpu.VMEM((Hd, Hd), F32)],
        args=[proj, proj, proj, proj, ret_norm, cos2, sin2, intra, k_tail, q_head, chunk_decay])


def ret_bwd(proj, db, o_pre, r_prev, ret_norm, tables, dproj, name, hosted=()):
    T = proj.shape[0]
    Hd, C, G = HEAD_DIM, RET_CHUNK, N_GROUPS
    tseg = min(T, 2048)
    nseg, nck = T // tseg, tseg // C
    scale = Hd ** -0.5
    cos2, sin2, intra, k_tail, q_head, chunk_decay = tables

    def body(q_ref, k_ref, v_ref, g_ref, db_ref, o_ref, rp_ref, gain_ref, cos_ref, sin_ref, m_ref, kt_ref, qh_ref, cd_ref,
             _, d_ref, dgain_ref, gstate):
        @pl.when(pl.program_id(1) == 0)
        def _():
            gstate[...] = jnp.zeros_like(gstate)
            dgain_ref[...] = jnp.zeros_like(dgain_ref)

        def chunk(t, carry):
            ci = nck - 1 - t
            rows = pl.ds(pl.multiple_of(ci * C, C), C)
            at = pl.ds(pl.multiple_of((nseg - 1 - pl.program_id(1)) * tseg + ci * C, C), C)
            cos, sin = cos_ref[at, :], sin_ref[at, :]
            qr = _rope(q_ref[rows, :], cos, sin)
            kr = _rope(k_ref[rows, :], cos, sin) * scale
            qb, kb, vb = qr.astype(BF16), kr.astype(BF16), v_ref[rows, :].astype(BF16)
            qhb, ktb = (qr * qh_ref[0]).astype(BF16), (kr * kt_ref[0]).astype(BF16)
            sc = (_dot_nt(qb, kb) * m_ref[0]).astype(BF16)
            o = o_ref[rows, :]
            rstd = _rstd(o)
            on = o * rstd
            gain = gain_ref[...]
            silu, dsilu = _silu_parts(g_ref[rows, :])
            dy = db_ref[rows, :]
            dgain_ref[...] += jnp.sum(dy * silu * on, axis=0, keepdims=True)
            dg = dy * on * gain * dsilu
            don = dy * silu * gain
            dob = (rstd * (don - on * jnp.mean(don * on, axis=-1, keepdims=True))).astype(BF16)
            gn = gstate[...]
            gb = gn.astype(BF16)
            da = (_dot_nt(dob, vb) * m_ref[0]).astype(BF16)
            dq = _dot(da, kb) + _dot_nt(dob, rp_ref[0, ci]) * qh_ref[0]
            dk = _dot_tn(da, qb) + _dot_nt(vb, gb) * kt_ref[0]
            dv = _dot_tn(sc, dob) + _dot(ktb, gb)
            gstate[...] = cd_ref[0] * gn + _dot_tn(qhb, dob)
            d_ref[0, rows, :] = _rope_t(dq, cos, sin).astype(BF16)
            d_ref[1, rows, :] = _rope_t(dk * scale, cos, sin).astype(BF16)
            d_ref[2, rows, :] = dv.astype(BF16)
            d_ref[3, rows, :] = dg.astype(BF16)
            return carry

        lax.fori_loop(0, nck, chunk, 0, unroll=True)

    rev = lambda s: nseg - 1 - s
    col, tab, head, cd, gain = _ret_specs(T, tseg, rev)
    act = pl.BlockSpec((tseg, Hd), lambda h, s: (rev(s), h))
    return _call(
        body, hosted, name=name, grid=(G, nseg),
        in_specs=[col(1), col(2), col(3), col(4), act, act, pl.BlockSpec((1, nck, Hd, Hd), lambda h, s: (h, rev(s), 0, 0)),
                  gain, tab, tab, head, head, head, cd, ANY],
        out_specs=[pl.BlockSpec((4, tseg, Hd), lambda h, s: (0, rev(s), h)), gain],
        out_shape=[jax.ShapeDtypeStruct(dproj.shape, BF16), jax.ShapeDtypeStruct((1, G * Hd), F32)],
        scratch_shapes=[pltpu.VMEM((Hd, Hd), F32)], aliased={14: 0},
        args=[proj, proj, proj, proj, db, o_pre, r_prev, ret_norm, cos2, sin2, intra, k_tail, q_head, chunk_decay, dproj])


def ffn_down_loss(s, w2, h, gain, target, name, hosted=()):
    T, F = s.shape
    D = h.shape[1]
    tm = min(T, 512)

    def body(s_ref, w2_ref, h_ref, g_ref, t_ref, dh_ref, loss_ref, dg_ref):
        @pl.when(pl.program_id(0) == 0)
        def _():
            loss_ref[...] = jnp.zeros_like(loss_ref)
            dg_ref[...] = jnp.zeros_like(dg_ref)

        hh = h_ref[...] + 0.5 * _dot(s_ref[...], w2_ref[...])
        gain_v = g_ref[...]
        err = hh * _rstd(hh) * gain_v - t_ref[...]
        loss_ref[...] += 0.5 * jnp.sum(jnp.mean(err * err, axis=-1, keepdims=True), axis=0, keepdims=True)
        dhn, dg = _rmsnorm_bwd(err * (1.0 / D), hh, gain_v)
        dh_ref[...] = dhn
        dg_ref[...] += jnp.sum(dg, axis=0, keepdims=True)

    row_spec = pl.BlockSpec((tm, D), lambda i: (i, 0))
    vec_spec = pl.BlockSpec((1, D), lambda i: (0, 0))
    return _call(
        body, hosted, name=name, grid=(T // tm,),
        in_specs=[pl.BlockSpec((tm, F), lambda i: (i, 0)), _resident((F, D)), row_spec, vec_spec, row_spec],
        out_specs=[row_spec, pl.BlockSpec((1, LANES), lambda i: (0, 0)), vec_spec],
        out_shape=[jax.ShapeDtypeStruct((T, D), F32), jax.ShapeDtypeStruct((1, LANES), F32), jax.ShapeDtypeStruct((1, D), F32)],
        args=[s, w2, h, gain, target])


def prereduce(grads, recvs, place, name):
    nt = len(grads)
    nsh, R, C = grads[0].shape
    rh = R // 2

    def body(place_ref, *refs):
        for t in range(nt):
            g_ref, r_ref, o_ref, own_ref = refs[2 * t], refs[2 * t + 1], refs[2 * nt + 2 * t], refs[2 * nt + 2 * t + 1]
            piece = (g_ref[...].astype(F32) + r_ref[...].astype(F32)).astype(BF16)
            o_ref[...] = piece

            @pl.when(pl.program_id(0) == place_ref[1])
            def _():
                own_ref[...] = piece

    outs = pl.pallas_call(
        body, name=name,
        grid_spec=pltpu.PrefetchScalarGridSpec(
            num_scalar_prefetch=1, grid=(nsh,),
            in_specs=[pl.BlockSpec((1, rh, C), lambda j, p: (j, p[0], 0)), pl.BlockSpec((1, rh, C), lambda j, p: (j, 0, 0))] * nt,
            out_specs=[pl.BlockSpec((1, rh, C), lambda j, p: (j, 0, 0)),
                       pl.BlockSpec((1, rh, C), lambda j, p: (p[1], p[0], 0))] * nt),
        out_shape=[jax.ShapeDtypeStruct((nsh, rh, C), BF16), jax.ShapeDtypeStruct((nsh, R, C), BF16)] * nt,
        compiler_params=pltpu.CompilerParams(vmem_limit_bytes=VMEM_LIMIT_V7X),
    )(place, *[a for pair in zip(grads, recvs) for a in pair])
    return [(outs[2 * t], outs[2 * t + 1]) for t in range(nt)]


def _adamw(w, g, m, v):
    m = ADAM_B1 * m + (1.0 - ADAM_B1) * g
    v = ADAM_B2 * v + (1.0 - ADAM_B2) * (g * g)
    m_hat = m / (1.0 - ADAM_B1 ** ADAM_STEP)
    v_hat = v / (1.0 - ADAM_B2 ** ADAM_STEP)
    return -ADAM_LR * (m_hat / (jnp.sqrt(v_hat) + ADAM_EPS) + ADAM_WD * w), m, v


def adamw_sharded(tensors, name, hosted=()):
    nt = len(tensors)
    nsh = tensors[0][0].shape[0]
    shapes = [t[0].shape[1:] for t in tensors]

    def fits(steps):
        if any(R % (steps * BF16_TILE_ROWS) for R, _ in shapes):
            return False
        return sum(2 * (R // steps) * -(-C // LANES) * LANES * (nsh * 2 + 7 * 4) for R, C in shapes) <= ADAMW_VMEM_BUDGET

    steps = min(s for s in range(1, min(R for R, _ in shapes) // BF16_TILE_ROWS + 1) if fits(s))

    def body(*refs):
        ins, outs = refs[:4 * nt], refs[4 * nt:]
        for t in range(nt):
            p_ref, w_ref, m_ref, v_ref = ins[4 * t:4 * t + 4]
            g_ref, d_ref, nm_ref, nv_ref = outs[4 * t:4 * t + 4]
            g = p_ref[0].astype(F32)
            for i in range(1, nsh):
                g += p_ref[i].astype(F32)
            g_ref[...] = g
            d_ref[...], nm_ref[...], nv_ref[...] = _adamw(w_ref[...], g, m_ref[...], v_ref[...])

    in_specs, out_specs, out_shape = [], [], []
    for R, C in shapes:
        spec = pl.BlockSpec((R // steps, C), lambda i: (i, 0))
        in_specs += [pl.BlockSpec((nsh, R // steps, C), lambda i: (0, i, 0)), spec, spec, spec]
        out_specs += [spec] * 4
        out_shape += [jax.ShapeDtypeStruct((R, C), F32)] * 4
    return _call(body, hosted, name=name, grid=(steps,), in_specs=in_specs, out_specs=out_specs, out_shape=out_shape,
                 args=[a for tensor in tensors for a in tensor])


def adamw_small(packs, params, loss_packs, name):
    n = len(packs)
    ndev = loss_packs.shape[0]

    def body(*refs):
        p_refs, loss_ref, wmv = refs[:n], refs[n], refs[n + 1:4 * n + 1]
        outs, loss_out = refs[4 * n + 1:8 * n + 1], refs[8 * n + 1]
        total = lambda r: sum((r[i] for i in range(1, ndev)), r[0])
        loss_out[...] = total(loss_ref)
        for k in range(n):
            g = total(p_refs[k])
            outs[4 * k][...] = g
            outs[4 * k + 1][...], outs[4 * k + 2][...], outs[4 * k + 3][...] = _adamw(
                wmv[3 * k][...], g, wmv[3 * k + 1][...], wmv[3 * k + 2][...])

    out_shape = [jax.ShapeDtypeStruct(p[0].shape, F32) for p in params for _ in range(4)]
    outs = pl.pallas_call(body, name=name, out_shape=out_shape + [jax.ShapeDtypeStruct(loss_packs.shape[1:], F32)],
                          compiler_params=pltpu.CompilerParams(vmem_limit_bytes=VMEM_LIMIT_V7X),
                          )(*packs, loss_packs, *[a for p in params for a in p])
    return [outs[4 * k:4 * k + 4] for k in range(n)], outs[4 * n]


BIG = ("ffn1_w1", "ffn1_w3", "ffn1_w2", "w_in", "w_out", "ffn2_w1", "ffn2_w3", "ffn2_w2")
TRANSPOSED = ("ffn1_w1", "ffn1_w3", "ffn2_w1", "ffn2_w3")
SMALL = ("pool_w", "mix_norm", "pool_scale", "ret_norm", "ffn2_norm", "final_norm", "ffn1_norm")
WEIGHTS = ("ffn1_norm", "ffn1_w1", "ffn1_w3", "ffn1_w2", "mix_norm", "w_in", "pool_w", "pool_scale", "ret_norm", "w_out",
           "ffn2_norm", "ffn2_w1", "ffn2_w3", "ffn2_w2", "final_norm")


def kernel(x, ffn1_norm, ffn1_w1, ffn1_w3, ffn1_w2, mix_norm, w_in, pool_w, pool_scale, ret_norm, w_out, ffn2_norm, ffn2_w1, ffn2_w3, ffn2_w2, final_norm, loss_target, m_ffn1_norm, m_ffn1_w1, m_ffn1_w3, m_ffn1_w2, m_mix_norm, m_w_in, m_pool_w, m_pool_scale, m_ret_norm, m_w_out, m_ffn2_norm, m_ffn2_w1, m_ffn2_w3, m_ffn2_w2, m_final_norm, v_ffn1_norm, v_ffn1_w1, v_ffn1_w3, v_ffn1_w2, v_mix_norm, v_w_in, v_pool_w, v_pool_scale, v_ret_norm, v_w_out, v_ffn2_norm, v_ffn2_w1, v_ffn2_w3, v_ffn2_w2, v_final_norm):
    w = dict(ffn1_norm=ffn1_norm, ffn1_w1=ffn1_w1, ffn1_w3=ffn1_w3, ffn1_w2=ffn1_w2, mix_norm=mix_norm, w_in=w_in, pool_w=pool_w,
             pool_scale=pool_scale, ret_norm=ret_norm, w_out=w_out, ffn2_norm=ffn2_norm, ffn2_w1=ffn2_w1, ffn2_w3=ffn2_w3,
             ffn2_w2=ffn2_w2, final_norm=final_norm)
    m = dict(ffn1_norm=m_ffn1_norm, ffn1_w1=m_ffn1_w1, ffn1_w3=m_ffn1_w3, ffn1_w2=m_ffn1_w2, mix_norm=m_mix_norm, w_in=m_w_in,
             pool_w=m_pool_w, pool_scale=m_pool_scale, ret_norm=m_ret_norm, w_out=m_w_out, ffn2_norm=m_ffn2_norm, ffn2_w1=m_ffn2_w1,
             ffn2_w3=m_ffn2_w3, ffn2_w2=m_ffn2_w2, final_norm=m_final_norm)
    v = dict(ffn1_norm=v_ffn1_norm, ffn1_w1=v_ffn1_w1, ffn1_w3=v_ffn1_w3, ffn1_w2=v_ffn1_w2, mix_norm=v_mix_norm, w_in=v_w_in,
             pool_w=v_pool_w, pool_scale=v_pool_scale, ret_norm=v_ret_norm, w_out=v_w_out, ffn2_norm=v_ffn2_norm, ffn2_w1=v_ffn2_w1,
             ffn2_w3=v_ffn2_w3, ffn2_w2=v_ffn2_w2, final_norm=v_final_norm)
    xs, target = x[0], loss_target[0]
    T = xs.shape[0]
    tables = _ret_tables(T)
    place = jnp.stack([lax.axis_index("c"), 2 * lax.axis_index("x") + lax.axis_index("y")]).astype(jnp.int32)
    local = lambda d, k: jnp.transpose(d[k][0]) if k in TRANSPOSED else d[k][0]
    result = lambda o, k: jnp.transpose(o)[None] if k in TRANSPOSED else o[None]
    first = ("ffn1_w1", "ffn1_w3")
    sh = {k: local(w, k).astype(BF16) for k in first}
    gather = lambda *names: [ChipExchange([sh[k] for k in names], False)]
    wg, grad, delta, new_m, new_v = {}, {}, {}, {}, {}

    def update(names, pieces, name, hosted=()):
        outs, extras = adamw_sharded([(p, local(w, k), local(m, k), local(v, k)) for k, p in zip(names, pieces)], name, hosted)
        for t, k in enumerate(names):
            grad[k], delta[k], new_m[k], new_v[k] = [result(o, k) for o in outs[4 * t:4 * t + 4]]
        return extras

    def reduce_in_chip(name, *pairs):
        reduced = prereduce([p for p, _ in pairs], [r for _, r in pairs], place, "prereduce_" + name)
        return reduced[0] if len(pairs) == 1 else reduced

    scatter = lambda *reduced: ChipExchange([r[0] for r in reduced], True, [r[1] for r in reduced])
    whole = lambda k: wg[k].reshape(-1, wg[k].shape[-1])
    sharded = lambda g: g.reshape(N_CHIPS, -1, g.shape[-1])

    later = [k for k in BIG if k not in first]
    casts, ((wg["ffn1_w1"], wg["ffn1_w3"]),) = cast_shards([local(w, k) for k in later], "cast_gather_ffn1", gather(*first))
    sh.update(zip(later, casts))
    (n1, ga1, gb1, s1), ((wg["ffn1_w2"], wg["w_in"]),) = ffn_up(
        xs, ffn1_norm, whole("ffn1_w1"), whole("ffn1_w3"), "ffn1_up", gather("ffn1_w2", "w_in"))
    (h1, u, proj), ((wg["w_out"], wg["ffn2_w1"]),) = ffn_down_mix_in(
        s1, whole("ffn1_w2"), xs, mix_norm, wg["w_in"], "ffn1_down_mix_in", gather("w_out", "ffn2_w1"))
    (pa,), _ = pool_fwd(proj, pool_w[0], pool_scale, "pool_fwd")
    (rb, o_pre, r_prev), ((wg["ffn2_w3"],),) = ret_fwd(proj, ret_norm, tables, "ret_fwd", gather("ffn2_w3"))
    (h2, n2, ga2, gb2, s2), ((wg["ffn2_w2"],),) = ffn_up(
        h1, ffn2_norm, whole("ffn2_w1"), whole("ffn2_w3"), "mix_out_ffn2_up", gather("ffn2_w2"), mixed=(pa, rb, wg["w_out"]))
    (dh3, loss, d_final), _ = ffn_down_loss(s2, whole("ffn2_w2"), h2, final_norm[None], target, "ffn2_down_loss")

    (da2, db2, df2), _ = ffn_bwd_act(dh3, whole("ffn2_w2"), ga2, gb2, "ffn2_bwd_act")
    (g_f2w2,), _ = ffn_dw([s2], df2, 1, "ffn2_dw2")
    g_f2w2 = sharded(g_f2w2)
    (g_f2w1, g_f2w3), ((r_f2w2,),) = ffn_dw([da2, db2], n2, 2, "ffn2_dw13", [SiblingExchange([g_f2w2])])
    g_f2w1, g_f2w3 = sharded(g_f2w1), sharded(g_f2w3)
    p_f2w2 = reduce_in_chip("ffn2_w2", (g_f2w2, r_f2w2))
    (dh2, d_ffn2), ((q_f2w2,), (r_f2w1, r_f2w3)) = ffn_bwd_in(
        da2, db2, whole("ffn2_w1"), whole("ffn2_w3"), h2, ffn2_norm, dh3, "ffn2_bwd_in",
        [scatter(p_f2w2), SiblingExchange([g_f2w1, g_f2w3])])
    p_f2w1, p_f2w3 = reduce_in_chip("ffn2_w13", (g_f2w1, r_f2w1), (g_f2w3, r_f2w3))
    (dpa, drb, g_wout), _ = mix_out_bwd(dh2, wg["w_out"], pa, rb, "mix_out_bwd")
    (dproj, d_pool_w, d_pool_scale), _ = pool_bwd(proj, dpa, pool_w[0], pool_scale, "pool_bwd")
    (dproj, d_ret_norm), ((q_f2w1, q_f2w3), (r_wout,)) = ret_bwd(
        proj, drb, o_pre, r_prev, ret_norm, tables, dproj, "ret_bwd", [scatter(p_f2w1, p_f2w3), SiblingExchange([g_wout])])
    p_wout = reduce_in_chip("w_out", (g_wout, r_wout))
    (g_win,), ((q_wout,),) = mix_dwin(u, dproj, N_CHIPS, "mix_dwin", [scatter(p_wout)])
    (dh1, d_mix), ((r_win,),) = mix_in_bwd(dproj, wg["w_in"], h1, mix_norm, dh2, "mix_in_bwd", [SiblingExchange([g_win])])
    p_win = reduce_in_chip("w_in", (g_win, r_win))
    (da1, db1, df1), ((q_win,),) = ffn_bwd_act(dh1, whole("ffn1_w2"), ga1, gb1, "ffn1_bwd_act", [scatter(p_win)])
    d_small = {"pool_w": d_pool_w.reshape(-1, LANES), "mix_norm": d_mix, "pool_scale": d_pool_scale, "ret_norm": d_ret_norm,
               "ffn2_norm": d_ffn2, "final_norm": d_final}
    (g_f1w1, g_f1w3), (packs,) = ffn_dw([da1, db1], n1, 2, "ffn1_dw13", [AllExchange([d_small[k] for k in SMALL[:-1]] + [loss])])
    g_f1w1, g_f1w3 = sharded(g_f1w1), sharded(g_f1w3)
    (g_f1w2,), ((r_f1w1, r_f1w3),) = ffn_dw([s1], df1, 1, "ffn1_dw2", [SiblingExchange([g_f1w1, g_f1w3])])
    g_f1w2 = sharded(g_f1w2)
    p_f1w1, p_f1w3 = reduce_in_chip("ffn1_w13", (g_f1w1, r_f1w1), (g_f1w3, r_f1w3))
    (dx, d_ffn1), ((q_f1w1, q_f1w3), (r_f1w2,)) = ffn_bwd_in(
        da1, db1, whole("ffn1_w1"), whole("ffn1_w3"), xs, ffn1_norm, dh1, "ffn1_bwd_in",
        [scatter(p_f1w1, p_f1w3), SiblingExchange([g_f1w2])])
    p_f1w2 = reduce_in_chip("ffn1_w2", (g_f1w2, r_f1w2))

    (q_f1w2,), (late,) = update(["w_in", "w_out"], [q_win, q_wout], "adamw_mix", [scatter(p_f1w2), AllExchange([d_ffn1])])
    update(["ffn2_w2", "ffn2_w1", "ffn2_w3", "ffn1_w1", "ffn1_w3", "ffn1_w2"],
           [q_f2w2, q_f2w1, q_f2w3, q_f1w1, q_f1w3, q_f1w2], "adamw_ffn")
    flat = lambda t, k: t[k].reshape(-1, LANES) if k == "pool_w" else t[k].reshape(1, -1)
    updated, loss_sum = adamw_small(packs[:-1] + [late], [[flat(t, k) for t in (w, m, v)] for k in SMALL], packs[-1], "adamw_small")
    for k, outs in zip(SMALL, updated):
        grad[k], delta[k], new_m[k], new_v[k] = [o.reshape(w[k].shape) for o in outs]
    loss = loss_sum[0, 0]

    return (loss, dx[None], *[grad[k] for k in WEIGHTS], *[delta[k] for k in WEIGHTS],
            *[new_m[k] for k in WEIGHTS], *[new_v[k] for k in WEIGHTS])
```

```python
import math

import jax
import jax.numpy as jnp
import numpy as np
from jax import lax
from jax.experimental import pallas as pl
from jax.experimental.pallas import tpu as pltpu

F32 = jnp.float32
BF16 = jnp.bfloat16

EPS = 1e-6
LANES = 128
BF16_TILE_ROWS = 16
N_CHIPS = 4
N_GROUPS = 4
HEAD_DIM = 128
RET_CHUNK = 128
ROPE_BASE = 10000.0
ADAM_LR, ADAM_B1, ADAM_B2, ADAM_EPS, ADAM_WD, ADAM_STEP = 0.001, 0.9, 0.999, 1e-08, 0.01, 10
VMEM_LIMIT_V7X = 56 * 1024 * 1024
ADAMW_VMEM_BUDGET = 32 * 1024 * 1024
MESH = pl.DeviceIdType.MESH
ANY = pl.BlockSpec(memory_space=pl.ANY)


def _dot(a, b):
    return jnp.dot(a, b, preferred_element_type=F32)


def _dot_nt(a, b):
    return lax.dot_general(a, b, (((1,), (1,)), ((), ())), preferred_element_type=F32)


def _dot_tn(a, b):
    return lax.dot_general(a, b, (((0,), (0,)), ((), ())), preferred_element_type=F32)


def _rstd(h):
    return lax.rsqrt(jnp.mean(h * h, axis=-1, keepdims=True) + EPS)


def _rmsnorm_bwd(dn, h, gain):
    r = _rstd(h)
    nh = h * r
    dnh = dn * gain
    dh = r * (dnh - nh * jnp.mean(dnh * nh, axis=-1, keepdims=True))
    return dh, dn * nh


def _silu_parts(a):
    sig = jax.nn.sigmoid(a)
    silu = a * sig
    return silu, sig + silu * (1.0 - sig)


def _mesh_pos():
    return lax.axis_index("x"), lax.axis_index("y"), lax.axis_index("c")


class ChipExchange:
    def __init__(self, srcs, scatter, placed=()):
        n = len(srcs)
        self.inputs, self.scatter, self.n, self.reach = list(srcs) + list(placed), scatter, n, REACH_CHIPS
        self.aliases = {n + t: t for t in range(n)} if scatter else {}
        self.half_rows = [s.shape[1] if scatter else s.shape[0] // 2 for s in srcs]
        self.out_shape = [jax.ShapeDtypeStruct((N_CHIPS, 2 * rh, s.shape[-1]), s.dtype) for s, rh in zip(srcs, self.half_rows)]
        if scatter:
            self.out_shape += [jax.ShapeDtypeStruct((2, rh // 2, s.shape[-1]), s.dtype) for s, rh in zip(srcs, self.half_rows)]
        dma = pltpu.SemaphoreType.DMA
        self.sems = [dma((4 * n,)), dma((4 * n,)), dma((2 * n,)), dma((2 * n,)), dma((4 * n,)), dma((4 * n,))]

    def _copies(self, src, out, sems):
        hop1_send, hop1_recv, hop2_send, hop2_recv, d2d_send, d2d_recv = sems
        x, y, c = _mesh_pos()
        me, dg = 2 * x + y, 2 * (1 - x) + (1 - y)
        sibling = (x, y, 1 - c)
        n = self.n
        mine, theirs = c, 1 - c

        def nb(a):
            nx, ny = x ^ (1 - a), y ^ a
            return 2 * nx + ny, (nx, ny, c)

        def remote(s, d, send, recv, k, to):
            return pltpu.make_async_remote_copy(src_ref=s, dst_ref=d, send_sem=send.at[k], recv_sem=recv.at[k],
                                                device_id=to, device_id_type=MESH)

        class Copies:
            def slot(_, t, chip, half):
                rh = self.half_rows[t]
                return out[t].at[chip, pl.ds(half * rh, rh), :]

            def quarter(_, t, chip, q):
                qh = self.half_rows[t] // 2
                return out[t].at[chip, pl.ds(mine * 2 * qh + q * qh, qh), :]

            def own_shard(k, t):
                return remote(src[t], out[t].at[me], d2d_send, d2d_recv, 4 * t + 3, sibling)

            def hop1(k, t, a, transit=False):
                rh = self.half_rows[t]
                chip, to = nb(a)
                if transit:
                    piece = src[t].at[dg, pl.ds(a * (rh // 2), rh // 2), :]
                    return remote(piece, out[n + t].at[a], hop1_send, hop1_recv, 4 * t + 2 + a, to)
                piece = src[t].at[chip] if self.scatter else src[t].at[pl.ds(mine * rh, rh), :]
                return remote(piece, k.slot(t, me, mine), hop1_send, hop1_recv, 4 * t + a, to)

            def landed1(k, t, a, transit=False):
                here = out[n + t].at[a] if transit else k.slot(t, nb(a)[0], mine)
                return remote(here, here, hop1_send, hop1_recv, 4 * t + (2 if transit else 0) + a, sibling)

            def hop2(k, t, q):
                origin, to = nb(q)[0], nb(1 - q)[1]
                piece = out[n + t].at[q] if self.scatter else k.quarter(t, origin, q)
                return remote(piece, k.quarter(t, origin, q), hop2_send, hop2_recv, 2 * t + q, to)

            def landed2(k, t, q):
                here = k.quarter(t, dg, q)
                return remote(here, here, hop2_send, hop2_recv, 2 * t + q, sibling)

            def d2d(k, t, p, chip, own=False, arriving=False):
                if arriving:
                    there = k.slot(t, chip, theirs)
                    return remote(there, there, d2d_send, d2d_recv, 4 * t + p, sibling)
                piece = src[t].at[me] if own else k.slot(t, chip, mine)
                return remote(piece, k.slot(t, chip, mine), d2d_send, d2d_recv, 4 * t + p, sibling)

        return Copies(), nb, me, dg, c

    def start(self, src, out, sems):
        k, nb, me, dg, c = self._copies(src, out, sems)
        for t in range(self.n):
            for first in range(2):
                a = first ^ c
                k.hop1(t, a).start()
                if self.scatter:
                    k.hop1(t, a, transit=True).start()
            if self.scatter:
                k.d2d(t, 3, me, own=True).start()
            else:
                k.own_shard(t).start()

    def mid(self, src, out, sems):
        k, nb, me, dg, c = self._copies(src, out, sems)
        for t in range(self.n):
            for first in range(2):
                a = first ^ c
                if self.scatter:
                    k.landed1(t, a, transit=True).wait_recv()
                    k.hop2(t, a).start()
                k.landed1(t, a).wait_recv()
                if not self.scatter:
                    k.hop2(t, a).start()
                k.d2d(t, a, nb(a)[0]).start()

    def finish(self, src, out, sems):
        k, nb, me, dg, c = self._copies(src, out, sems)
        for t in range(self.n):
            for q in range(2):
                k.landed2(t, q).wait_recv()
            k.d2d(t, 2, dg).start()
        for t in range(self.n):
            for a in range(2):
                k.d2d(t, a, nb(a)[0], arriving=True).wait_recv()
            k.d2d(t, 2, dg, arriving=True).wait_recv()
            if self.scatter:
                k.d2d(t, 3, me, arriving=True).wait_recv()
        for t in range(self.n):
            for a in range(2):
                k.hop1(t, a).wait_send()
                if self.scatter:
                    k.hop1(t, a, transit=True).wait_send()
                k.hop2(t, a).wait_send()
                k.d2d(t, a, nb(a)[0]).wait_send()
            k.d2d(t, 2, dg).wait_send()
            if self.scatter:
                k.d2d(t, 3, me, own=True).wait_send()
            else:
                k.own_shard(t).wait()


class SiblingExchange:
    def __init__(self, grads):
        self.inputs, self.n, self.aliases, self.reach = list(grads), len(grads), {}, REACH_SIBLING
        self.half_rows = [g.shape[1] // 2 for g in grads]
        self.out_shape = [jax.ShapeDtypeStruct((g.shape[0], rh, g.shape[2]), g.dtype) for g, rh in zip(grads, self.half_rows)]
        self.sems = [pltpu.SemaphoreType.DMA((self.n,)), pltpu.SemaphoreType.DMA((self.n,))]

    def _plan(self, src, out, sems):
        x, y, c = _mesh_pos()
        return [pltpu.make_async_remote_copy(
            src_ref=src[t].at[:, pl.ds((1 - c) * self.half_rows[t], self.half_rows[t]), :], dst_ref=out[t],
            send_sem=sems[0].at[t], recv_sem=sems[1].at[t], device_id=(x, y, 1 - c), device_id_type=MESH) for t in range(self.n)]

    def start(self, src, out, sems):
        for cp in self._plan(src, out, sems):
            cp.start()

    def mid(self, src, out, sems):
        pass

    def finish(self, src, out, sems):
        for cp in self._plan(src, out, sems):
            cp.wait()


REACH_SIBLING, REACH_CHIPS, REACH_ALL = 0, 1, 2


def _entry_barrier(reach):
    x, y, c = _mesh_pos()
    peers = [(x, y, 1 - c)]
    if reach == REACH_CHIPS:
        peers += [(1 - x, y, c), (x, 1 - y, c)]
    elif reach == REACH_ALL:
        peers = [(x ^ dx, y ^ dy, c ^ dc) for dx in (0, 1) for dy in (0, 1) for dc in (0, 1)][1:]
    barrier = pltpu.get_barrier_semaphore()
    for peer in peers:
        pl.semaphore_signal(barrier, inc=1, device_id=peer, device_id_type=MESH)
    pl.semaphore_wait(barrier, len(peers))


def _call(body, hosted=(), *, name, in_specs, out_specs, out_shape, args, grid=(), scratch_shapes=(), aliased=None):
    n_in, n_out, n_scr = len(in_specs), len(out_specs), len(scratch_shapes)
    total = math.prod(grid)
    mid_step = max(0, (3 * total) // 4 - 1)

    def full(*refs):
        pos = [0]

        def take(k):
            pos[0] += k
            return refs[pos[0] - k:pos[0]]

        ins, h_in = take(n_in), [take(len(h.inputs)) for h in hosted]
        outs, h_out = take(n_out), [take(len(h.out_shape)) for h in hosted]
        scr, h_sem = take(n_scr), [take(len(h.sems)) for h in hosted]
        step = 0
        for axis, size in enumerate(grid):
            step = step * size + pl.program_id(axis)

        def phase(at, method):
            if not hosted:
                return

            def run():
                if method == "start":
                    _entry_barrier(reach)
                for h, s, o, m in zip(hosted, h_in, h_out, h_sem):
                    getattr(h, method)(s, o, m)

            if total == 1:
                run()
            else:
                pl.when(step == at)(run)

        phase(0, "start")
        body(*ins, *outs, *scr)
        phase(mid_step, "mid")
        phase(total - 1, "finish")

    aliases, i0, o0 = dict(aliased or {}), n_in, n_out
    for h in hosted:
        aliases.update({i0 + i: o0 + o for i, o in h.aliases.items()})
        i0, o0 = i0 + len(h.inputs), o0 + len(h.out_shape)
    reach = max((h.reach for h in hosted), default=None)
    params = dict(vmem_limit_bytes=VMEM_LIMIT_V7X)
    if hosted:
        params["collective_id"] = reach
    results = pl.pallas_call(
        full, name=name, grid=grid,
        in_specs=list(in_specs) + [ANY] * (i0 - n_in),
        out_specs=list(out_specs) + [ANY] * (o0 - n_out),
        out_shape=list(out_shape) + [s for h in hosted for s in h.out_shape],
        scratch_shapes=list(scratch_shapes) + [s for h in hosted for s in h.sems],
        input_output_aliases=aliases,
        compiler_params=pltpu.CompilerParams(**params),
    )(*args, *[s for h in hosted for s in h.inputs])
    outs, extras, pos = list(results[:n_out]), [], n_out
    for h in hosted:
        extras.append(list(results[pos:pos + h.n]))
        pos += len(h.out_shape)
    return outs, extras


def cast_shards(shards, name, hosted=()):
    n = len(shards)

    def body(*refs):
        for x_ref, o_ref in zip(refs[:n], refs[n:]):
            o_ref[...] = x_ref[...].astype(BF16)

    whole = lambda s: pl.BlockSpec(s.shape, lambda: (0,) * s.ndim)
    return _call(body, hosted, name=name, in_specs=[whole(s) for s in shards], out_specs=[whole(s) for s in shards],
                 out_shape=[jax.ShapeDtypeStruct(s.shape, BF16) for s in shards], args=list(shards))


class AllExchange:
    def __init__(self, arrays):
        n = len(arrays)
        self.inputs, self.n, self.aliases, self.reach = list(arrays), n, {}, REACH_ALL
        self.out_shape = [jax.ShapeDtypeStruct((2 * N_CHIPS,) + a.shape, a.dtype) for a in arrays]
        self.sems = [pltpu.SemaphoreType.DMA((n,)), pltpu.SemaphoreType.DMA((7 * n,)), pltpu.SemaphoreType.DMA((7 * n,))]

    def _copies(self, src, out, sems):
        local_sem, send_sem, recv_sem = sems
        x, y, c = _mesh_pos()
        me = 4 * x + 2 * y + c
        peers = [(x ^ dx, y ^ dy, c ^ dc) for dx in (0, 1) for dy in (0, 1) for dc in (0, 1)][1:]
        remote = lambda s, d, k, to: pltpu.make_async_remote_copy(
            src_ref=s, dst_ref=d, send_sem=send_sem.at[k], recv_sem=recv_sem.at[k], device_id=to, device_id_type=MESH)
        sends, landed, local = [], [], []
        for t in range(self.n):
            local.append(pltpu.make_async_copy(src[t], out[t].at[me], local_sem.at[t]))
            for p, (px, py, pc) in enumerate(peers):
                sends.append(remote(src[t], out[t].at[me], 7 * t + p, (px, py, pc)))
                here = out[t].at[4 * px + 2 * py + pc]
                landed.append(remote(here, here, 7 * t + p, (px, py, pc)))
        return sends, landed, local

    def start(self, src, out, sems):
        sends, _, local = self._copies(src, out, sems)
        for cp in sends + local:
            cp.start()

    def mid(self, src, out, sems):
        pass

    def finish(self, src, out, sems):
        sends, landed, local = self._copies(src, out, sems)
        for cp in landed:
            cp.wait_recv()
        for cp in sends:
            cp.wait_send()
        for cp in local:
            cp.wait()


MXU_COLS = 256


def _resident(shape):
    return pl.BlockSpec(shape, lambda *_: (0,) * len(shape), pipeline_mode=pl.Buffered(1))


def ffn_up(h, gain, w1, w3, name, hosted=(), mixed=None):
    T, D = h.shape
    F = w1.shape[0]
    tm = min(T, 256)

    def body(*refs):
        if mixed is None:
            h_ref, g_ref, w1_ref, w3_ref, n_ref, ga_ref, gb_ref, s_ref = refs
            hh = h_ref[...]
        else:
            pa_ref, rb_ref, wo_ref, h_ref, g_ref, w1_ref, w3_ref, hh_ref, n_ref, ga_ref, gb_ref, s_ref = refs
            hh = h_ref[...] + _dot(pa_ref[...], wo_ref[0]) + _dot(rb_ref[...], wo_ref[1])
            hh_ref[...] = hh
        n = (hh * _rstd(hh) * g_ref[...]).astype(BF16)
        n_ref[...] = n
        for c in range(0, F, MXU_COLS):
            cols = slice(c, c + MXU_COLS)
            a = _dot_nt(n, w1_ref[cols, :])
            b = _dot_nt(n, w3_ref[cols, :])
            silu, dsilu = _silu_parts(a)
            ga_ref[:, cols] = (b * dsilu).astype(BF16)
            gb_ref[:, cols] = silu.astype(BF16)
            s_ref[:, cols] = (silu * b).astype(BF16)

    act = jax.ShapeDtypeStruct((T, F), BF16)
    act_spec = pl.BlockSpec((tm, F), lambda i: (i, 0))
    row_spec = pl.BlockSpec((tm, D), lambda i: (i, 0))
    in_specs = [row_spec, pl.BlockSpec((1, D), lambda i: (0, 0)), _resident((F, D)), _resident((F, D))]
    out_specs, out_shape, args = [row_spec, act_spec, act_spec, act_spec], [jax.ShapeDtypeStruct((T, D), BF16), act, act, act], [h, gain, w1, w3]
    if mixed is not None:
        pa, rb, woutg = mixed
        W = pa.shape[1]
        in_specs = [pl.BlockSpec((tm, W), lambda i: (i, 0))] * 2 + [_resident((2, W, D))] + in_specs
        out_specs, out_shape = [row_spec] + out_specs, [jax.ShapeDtypeStruct((T, D), F32)] + out_shape
        args = [pa, rb, woutg.reshape(2, W, D)] + args
    return _call(body, hosted, name=name, grid=(T // tm,), in_specs=in_specs, out_specs=out_specs, out_shape=out_shape, args=args)


def ffn_bwd_act(dh, w2, ga, gb, name, hosted=()):
    T, D = dh.shape
    F = w2.shape[0]
    tm = min(T, 512)

    def body(dh_ref, w2_ref, ga_ref, gb_ref, da_ref, db_ref, df_ref):
        df = (0.5 * dh_ref[...]).astype(BF16)
        df_ref[...] = df
        for c in range(0, F, MXU_COLS):
            cols = slice(c, c + MXU_COLS)
            ds = _dot_nt(df, w2_ref[cols, :])
            da_ref[:, cols] = (ds * ga_ref[:, cols].astype(F32)).astype(BF16)
            db_ref[:, cols] = (ds * gb_ref[:, cols].astype(F32)).astype(BF16)

    act = jax.ShapeDtypeStruct((T, F), BF16)
    act_spec = pl.BlockSpec((tm, F), lambda i: (i, 0))
    row_spec = pl.BlockSpec((tm, D), lambda i: (i, 0))
    return _call(
        body, hosted, name=name, grid=(T // tm,),
        in_specs=[row_spec, _resident((F, D)), act_spec, act_spec],
        out_specs=[act_spec, act_spec, row_spec],
        out_shape=[act, act, jax.ShapeDtypeStruct((T, D), BF16)],
        args=[dh, w2, ga, gb])


def ffn_dw(xs, y, halves, name, hosted=()):
    T, F = xs[0].shape
    D = y.shape[1]
    nx, fh = len(xs), F // halves
    tk = min(T, 1024)
    nk = T // tk

    def body(*refs):
        y_ref, x_refs, o_refs, accs = refs[0], refs[1:1 + nx], refs[1 + nx:1 + 2 * nx], refs[1 + 2 * nx:]
        k = pl.program_id(1)

        @pl.when(k == 0)
        def _():
            for acc in accs:
                acc[...] = jnp.zeros_like(acc)

        yy = y_ref[...]
        for x_ref, acc in zip(x_refs, accs):
            acc[...] += _dot_tn(x_ref[...], yy)

        @pl.when(k == nk - 1)
        def _():
            for o_ref, acc in zip(o_refs, accs):
                o_ref[...] = acc[...].astype(BF16)

    out = jax.ShapeDtypeStruct((F, D), BF16)
    return _call(
        body, hosted, name=name, grid=(halves, nk),
        in_specs=[pl.BlockSpec((tk, D), lambda j, k: (k, 0))] + [pl.BlockSpec((tk, fh), lambda j, k: (k, j))] * nx,
        out_specs=[pl.BlockSpec((fh, D), lambda j, k: (j, 0))] * nx,
        out_shape=[out] * nx,
        scratch_shapes=[pltpu.VMEM((fh, D), F32)] * nx,
        args=[y] + list(xs))


def ffn_bwd_in(da, db, w1, w3, h, gain, dh, name, hosted=()):
    T, F = da.shape
    D = h.shape[1]
    tm = min(T, 256)

    def body(da_ref, db_ref, w1_ref, w3_ref, h_ref, g_ref, dh_ref, o_ref, dg_ref):
        dn = _dot(da_ref[...], w1_ref[...]) + _dot(db_ref[...], w3_ref[...])
        dhn, dg = _rmsnorm_bwd(dn, h_ref[...], g_ref[...])
        o_ref[...] = dh_ref[...] + dhn

        @pl.when(pl.program_id(0) == 0)
        def _():
            dg_ref[...] = jnp.zeros_like(dg_ref)

        dg_ref[...] += jnp.sum(dg, axis=0, keepdims=True)

    act_spec = pl.BlockSpec((tm, F), lambda i: (i, 0))
    row_spec = pl.BlockSpec((tm, D), lambda i: (i, 0))
    vec_spec = pl.BlockSpec((1, D), lambda i: (0, 0))
    return _call(
        body, hosted, name=name, grid=(T // tm,),
        in_specs=[act_spec, act_spec, _resident((F, D)), _resident((F, D)), row_spec, vec_spec, row_spec],
        out_specs=[row_spec, vec_spec],
        out_shape=[jax.ShapeDtypeStruct((T, D), F32), jax.ShapeDtypeStruct((1, D), F32)],
        args=[da, db, w1, w3, h, gain, dh])


def ffn_down_mix_in(s, w2, h, gain, wing, name, hosted=()):
    T, F = s.shape
    D = h.shape[1]
    nsh, _, Cs = wing.shape
    tm = min(T, 512)

    def body(s_ref, w2_ref, h_ref, g_ref, w_ref, hh_ref, u_ref, p_ref):
        hh = h_ref[...] + 0.5 * _dot(s_ref[...], w2_ref[...])
        hh_ref[...] = hh
        u = (hh * _rstd(hh) * g_ref[...]).astype(BF16)
        u_ref[...] = u
        for j in range(nsh):
            p_ref[:, j * Cs:(j + 1) * Cs] = _dot(u, w_ref[j])

    row_spec = pl.BlockSpec((tm, D), lambda i: (i, 0))
    return _call(
        body, hosted, name=name, grid=(T // tm,),
        in_specs=[pl.BlockSpec((tm, F), lambda i: (i, 0)), _resident((F, D)), row_spec, pl.BlockSpec((1, D), lambda i: (0, 0)),
                  _resident((nsh, D, Cs))],
        out_specs=[row_spec, row_spec, pl.BlockSpec((tm, nsh * Cs), lambda i: (i, 0))],
        out_shape=[jax.ShapeDtypeStruct((T, D), F32), jax.ShapeDtypeStruct((T, D), BF16), jax.ShapeDtypeStruct((T, nsh * Cs), F32)],
        args=[s, w2, h, gain, wing])


def mix_out_bwd(dh, woutg, a, b, name, hosted=()):
    T, D = dh.shape
    W = a.shape[1]
    nsh, Rs, _ = woutg.shape
    wout = woutg.reshape(2, W, D)
    tk = min(T, 1024)
    nk = T // tk

    def body(dh_ref, w_ref, a_ref, b_ref, da_ref, db_ref, dw_ref, acc):
        k = pl.program_id(0)

        @pl.when(k == 0)
        def _():
            acc[...] = jnp.zeros_like(acc)

        dhb = dh_ref[...].astype(BF16)
        da_ref[...] = _dot_nt(dhb, w_ref[0])
        db_ref[...] = _dot_nt(dhb, w_ref[1])
        acc[0:W, :] += _dot_tn(a_ref[...], dhb)
        acc[W:2 * W, :] += _dot_tn(b_ref[...], dhb)

        @pl.when(k == nk - 1)
        def _():
            for j in range(nsh):
                dw_ref[j] = acc[j * Rs:(j + 1) * Rs, :].astype(BF16)

    return _call(
        body, hosted, name=name, grid=(nk,),
        in_specs=[pl.BlockSpec((tk, D), lambda k: (k, 0)), pl.BlockSpec((2, W, D), lambda k: (0, 0, 0)),
                  pl.BlockSpec((tk, W), lambda k: (k, 0)), pl.BlockSpec((tk, W), lambda k: (k, 0))],
        out_specs=[pl.BlockSpec((tk, W), lambda k: (k, 0)), pl.BlockSpec((tk, W), lambda k: (k, 0)),
                   pl.BlockSpec((nsh, Rs, D), lambda k: (0, 0, 0))],
        out_shape=[jax.ShapeDtypeStruct((T, W), F32), jax.ShapeDtypeStruct((T, W), F32),
                   jax.ShapeDtypeStruct((nsh, Rs, D), BF16)],
        scratch_shapes=[pltpu.VMEM((2 * W, D), F32)],
        args=[dh, wout, a, b])


def _dproj_block(g):
    return (g // N_GROUPS + N_GROUPS) % (N_GROUPS + 1), g % N_GROUPS


def mix_dwin(u, dproj, nsh, name, hosted=()):
    T, D = u.shape
    Hd = HEAD_DIM
    slabs, _, width = dproj.shape
    blocks = slabs * width // Hd
    Cs = blocks * Hd // nsh
    tk = min(T, 512)
    nk = T // tk

    def body(u_ref, d_ref, o_ref, acc):
        k = pl.program_id(0)

        @pl.when(k == 0)
        def _():
            acc[...] = jnp.zeros_like(acc)

        where = [_dproj_block(g) for g in range(blocks)]
        d = jnp.concatenate([d_ref[slab, :, col * Hd:(col + 1) * Hd] for slab, col in where], axis=1)
        acc[...] += _dot_tn(u_ref[...], d)

        @pl.when(k == nk - 1)
        def _():
            for j in range(nsh):
                o_ref[j] = acc[:, j * Cs:(j + 1) * Cs].astype(BF16)

    return _call(
        body, hosted, name=name, grid=(nk,),
        in_specs=[pl.BlockSpec((tk, D), lambda k: (k, 0)), pl.BlockSpec((slabs, tk, width), lambda k: (0, k, 0))],
        out_specs=[pl.BlockSpec((nsh, D, Cs), lambda k: (0, 0, 0))],
        out_shape=[jax.ShapeDtypeStruct((nsh, D, Cs), BF16)],
        scratch_shapes=[pltpu.VMEM((D, blocks * Hd), F32)],
        args=[u, dproj])


def mix_in_bwd(dproj, wing, h, gain, dh, name, hosted=()):
    T, D = h.shape
    nsh, _, Cs = wing.shape
    Hd = HEAD_DIM
    per = Cs // Hd
    tm = min(T, 512)

    def body(d_ref, w_ref, h_ref, g_ref, dh_ref, o_ref, dg_ref):
        def shard(j):
            blocks = [_dproj_block(per * j + i) for i in range(per)]
            return jnp.concatenate([d_ref[slab, :, col * Hd:(col + 1) * Hd] for slab, col in blocks], axis=1)

        du = _dot_nt(shard(0), w_ref[0])
        for j in range(1, nsh):
            du += _dot_nt(shard(j), w_ref[j])
        dhn, dg = _rmsnorm_bwd(du, h_ref[...], g_ref[...])
        o_ref[...] = dh_ref[...] + dhn

        @pl.when(pl.program_id(0) == 0)
        def _():
            dg_ref[...] = jnp.zeros_like(dg_ref)

        dg_ref[...] += jnp.sum(dg, axis=0, keepdims=True)

    row_spec = pl.BlockSpec((tm, D), lambda i: (i, 0))
    vec_spec = pl.BlockSpec((1, D), lambda i: (0, 0))
    return _call(
        body, hosted, name=name, grid=(T // tm,),
        in_specs=[pl.BlockSpec((dproj.shape[0], tm, dproj.shape[2]), lambda i: (0, i, 0)),
                  pl.BlockSpec((nsh, D, Cs), lambda i: (0, 0, 0)), row_spec, vec_spec, row_spec],
        out_specs=[row_spec, vec_spec],
        out_shape=[jax.ShapeDtypeStruct((T, D), F32), jax.ShapeDtypeStruct((1, D), F32)],
        args=[dproj, wing, h, gain, dh])


POOL_WINDOWS = (2, 4, 8, 16)


def _pool_window(x, window, T, trailing):
    rows = lax.broadcasted_iota(jnp.int32, x.shape, 0)
    s, k = x, 1
    while k < window:
        if trailing:
            s = s + jnp.where(rows >= k, pltpu.roll(s, k, 0), 0.0)
        else:
            s = s + jnp.where(rows < T - k, pltpu.roll(s, T - k, 0), 0.0)
        k *= 2
    return s


def _pool_count(window, shape):
    rows = lax.broadcasted_iota(jnp.int32, shape, 0)
    return jnp.minimum(rows + 1, window).astype(F32)


def _per_group(work):
    for group, window in enumerate(POOL_WINDOWS):
        pl.when(pl.program_id(0) == group)(lambda window=window: work(window))


def pool_fwd(proj, pool_w, pool_scale, name, hosted=()):
    T = proj.shape[0]
    Hd = HEAD_DIM

    def body(x_ref, w_ref, sc_ref, a_ref):
        def work(window):
            x = x_ref[...]
            pooled = _pool_window(x, window, T, True) / _pool_count(window, x.shape) - x
            a_ref[...] = (_dot(pooled.astype(BF16), w_ref[0].astype(BF16)) * sc_ref[...]).astype(BF16)

        _per_group(work)

    return _call(
        body, hosted, name=name, grid=(N_GROUPS,),
        in_specs=[pl.BlockSpec((T, Hd), lambda g: (0, g)), pl.BlockSpec((1, Hd, Hd), lambda g: (g, 0, 0)),
                  pl.BlockSpec((1, Hd), lambda g: (0, g))],
        out_specs=[pl.BlockSpec((T, Hd), lambda g: (0, g))],
        out_shape=[jax.ShapeDtypeStruct((T, N_GROUPS * Hd), BF16)],
        args=[proj, pool_w, pool_scale])


def pool_bwd(proj, da, pool_w, pool_scale, name, hosted=()):
    T = proj.shape[0]
    Hd = HEAD_DIM

    def body(x_ref, da_ref, w_ref, sc_ref, dx_ref, dw_ref, dsc_ref):
        def work(window):
            x = x_ref[...]
            cnt = _pool_count(window, x.shape)
            pooled = (_pool_window(x, window, T, True) / cnt - x).astype(BF16)
            wb = w_ref[0].astype(BF16)
            dav = da_ref[...]
            dsc_ref[...] = jnp.sum(dav * _dot(pooled, wb), axis=0, keepdims=True)
            dout = (dav * sc_ref[...]).astype(BF16)
            dw_ref[0] = _dot_tn(pooled, dout)
            dpooled = _dot_nt(dout, wb)
            dx_ref[0] = (_pool_window(dpooled / cnt, window, T, False) - dpooled).astype(BF16)

        _per_group(work)

    col_spec = pl.BlockSpec((T, Hd), lambda g: (0, g))
    return _call(
        body, hosted, name=name, grid=(N_GROUPS,),
        in_specs=[col_spec, col_spec, pl.BlockSpec((1, Hd, Hd), lambda g: (g, 0, 0)), pl.BlockSpec((1, Hd), lambda g: (0, g))],
        out_specs=[pl.BlockSpec((1, T, Hd), lambda g: (N_GROUPS, 0, g)), pl.BlockSpec((1, Hd, Hd), lambda g: (g, 0, 0)),
                   pl.BlockSpec((1, Hd), lambda g: (0, g))],
        out_shape=[jax.ShapeDtypeStruct((N_GROUPS + 1, T, N_GROUPS * Hd), BF16), jax.ShapeDtypeStruct((N_GROUPS, Hd, Hd), F32),
                   jax.ShapeDtypeStruct((1, N_GROUPS * Hd), F32)],
        args=[proj, da, pool_w, pool_scale])


def _ret_tables(T):
    Hd, C, f32 = HEAD_DIM, RET_CHUNK, np.float32
    inv_freq = (1.0 / (ROPE_BASE ** (np.arange(0, Hd, 2, dtype=np.float64) / Hd))).astype(f32)
    ang = np.arange(T, dtype=f32)[:, None] * inv_freq[None, :]
    cos, sin = np.cos(ang), np.sin(ang)
    cos2 = np.concatenate([cos, cos], axis=-1)
    sin2 = np.concatenate([-sin, sin], axis=-1)
    log_gamma = np.log1p(-np.exp2(f32(-5.0) - np.arange(N_GROUPS, dtype=f32)))
    pos = np.arange(C, dtype=f32)
    rel = pos[:, None] - pos[None, :]
    intra = np.where(rel[None] >= 0, np.exp(log_gamma[:, None, None] * np.maximum(rel, f32(0.0))[None]), f32(0.0))
    k_tail = np.exp(log_gamma[:, None] * (f32(C - 1) - pos)[None, :])
    q_head = np.exp(log_gamma[:, None] * (pos + f32(1.0))[None, :])
    chunk_decay = np.exp(log_gamma * f32(C))
    wide = lambda t: np.broadcast_to(t[:, :, None], (N_GROUPS, C, Hd))
    tables = cos2, sin2, intra, wide(k_tail), wide(q_head), np.broadcast_to(chunk_decay[:, None, None], (N_GROUPS, 1, Hd))
    assert all(t.dtype == f32 for t in tables)
    return tuple(jnp.asarray(t) for t in tables)


def _rope(x, cos2, sin2):
    return x * cos2 + pltpu.roll(x, HEAD_DIM // 2, 1) * sin2


def _rope_t(d, cos2, sin2):
    return d * cos2 + pltpu.roll(d * sin2, HEAD_DIM // 2, 1)


def _ret_specs(T, tseg, seg_of):
    Hd, G = HEAD_DIM, N_GROUPS
    col = lambda kind: pl.BlockSpec((tseg, Hd), lambda h, s: (seg_of(s), G * kind + h))
    tab = pl.BlockSpec((T, Hd), lambda h, s: (0, 0))
    head = pl.BlockSpec((1, RET_CHUNK, Hd), lambda h, s: (h, 0, 0))
    cd = pl.BlockSpec((1, 1, Hd), lambda h, s: (h, 0, 0))
    gain = pl.BlockSpec((1, Hd), lambda h, s: (0, h))
    return col, tab, head, cd, gain


def ret_fwd(proj, ret_norm, tables, name, hosted=()):
    T = proj.shape[0]
    Hd, C, G = HEAD_DIM, RET_CHUNK, N_GROUPS
    tseg = min(T, 2048)
    nseg, nck = T // tseg, tseg // C
    scale = Hd ** -0.5
    cos2, sin2, intra, k_tail, q_head, chunk_decay = tables

    def body(q_ref, k_ref, v_ref, g_ref, gain_ref, cos_ref, sin_ref, m_ref, kt_ref, qh_ref, cd_ref,
             b_ref, o_ref, rp_ref, state):
        @pl.when(pl.program_id(1) == 0)
        def _():
            state[...] = jnp.zeros_like(state)

        def chunk(ci, carry):
            rows = pl.ds(pl.multiple_of(ci * C, C), C)
            at = pl.ds(pl.multiple_of(pl.program_id(1) * tseg + ci * C, C), C)
            cos, sin = cos_ref[at, :], sin_ref[at, :]
            qr = _rope(q_ref[rows, :], cos, sin)
            kr = _rope(k_ref[rows, :], cos, sin) * scale
            qb, kb, vb = qr.astype(BF16), kr.astype(BF16), v_ref[rows, :].astype(BF16)
            r = state[...]
            rp_ref[0, ci] = r.astype(BF16)
            sc = _dot_nt(qb, kb) * m_ref[0]
            o = _dot(sc.astype(BF16), vb) + _dot((qr * qh_ref[0]).astype(BF16), r.astype(BF16))
            state[...] = cd_ref[0] * r + _dot_tn((kr * kt_ref[0]).astype(BF16), vb)
            o_ref[rows, :] = o
            on = o * _rstd(o)
            b_ref[rows, :] = (jax.nn.silu(g_ref[rows, :]) * (on * gain_ref[...])).astype(BF16)
            return carry

        lax.fori_loop(0, nck, chunk, 0, unroll=True)

    col, tab, head, cd, gain = _ret_specs(T, tseg, lambda s: s)
    out_col = pl.BlockSpec((tseg, Hd), lambda h, s: (s, h))
    return _call(
        body, hosted, name=name, grid=(G, nseg),
        in_specs=[col(1), col(2), col(3), col(4), gain, tab, tab, head, head, head, cd],
        out_specs=[out_col, out_col, pl.BlockSpec((1, nck, Hd, Hd), lambda h, s: (h, s, 0, 0))],
        out_shape=[jax.ShapeDtypeStruct((T, G * Hd), BF16), jax.ShapeDtypeStruct((T, G * Hd), F32),
                   jax.ShapeDtypeStruct((G, T // C, Hd, Hd), BF16)],
        scratch_shapes=[pltpu.VMEM((Hd, Hd), F32)],
        args=[proj, proj, proj, proj, ret_norm, cos2, sin2, intra, k_tail, q_head, chunk_decay])


def ret_bwd(proj, db, o_pre, r_prev, ret_norm, tables, dproj, name, hosted=()):
    T = proj.shape[0]
    Hd, C, G = HEAD_DIM, RET_CHUNK, N_GROUPS
    tseg = min(T, 2048)
    nseg, nck = T // tseg, tseg // C
    scale = Hd ** -0.5
    cos2, sin2, intra, k_tail, q_head, chunk_decay = tables

    def body(q_ref, k_ref, v_ref, g_ref, db_ref, o_ref, rp_ref, gain_ref, cos_ref, sin_ref, m_ref, kt_ref, qh_ref, cd_ref,
             _, d_ref, dgain_ref, gstate):
        @pl.when(pl.program_id(1) == 0)
        def _():
            gstate[...] = jnp.zeros_like(gstate)
            dgain_ref[...] = jnp.zeros_like(dgain_ref)

        def chunk(t, carry):
            ci = nck - 1 - t
            rows = pl.ds(pl.multiple_of(ci * C, C), C)
            at = pl.ds(pl.multiple_of((nseg - 1 - pl.program_id(1)) * tseg + ci * C, C), C)
            cos, sin = cos_ref[at, :], sin_ref[at, :]
            qr = _rope(q_ref[rows, :], cos, sin)
            kr = _rope(k_ref[rows, :], cos, sin) * scale
            qb, kb, vb = qr.astype(BF16), kr.astype(BF16), v_ref[rows, :].astype(BF16)
            qhb, ktb = (qr * qh_ref[0]).astype(BF16), (kr * kt_ref[0]).astype(BF16)
            sc = (_dot_nt(qb, kb) * m_ref[0]).astype(BF16)
            o = o_ref[rows, :]
            rstd = _rstd(o)
            on = o * rstd
            gain = gain_ref[...]
            silu, dsilu = _silu_parts(g_ref[rows, :])
            dy = db_ref[rows, :]
            dgain_ref[...] += jnp.sum(dy * silu * on, axis=0, keepdims=True)
            dg = dy * on * gain * dsilu
            don = dy * silu * gain
            dob = (rstd * (don - on * jnp.mean(don * on, axis=-1, keepdims=True))).astype(BF16)
            gn = gstate[...]
            gb = gn.astype(BF16)
            da = (_dot_nt(dob, vb) * m_ref[0]).astype(BF16)
            dq = _dot(da, kb) + _dot_nt(dob, rp_ref[0, ci]) * qh_ref[0]
            dk = _dot_tn(da, qb) + _dot_nt(vb, gb) * kt_ref[0]
            dv = _dot_tn(sc, dob) + _dot(ktb, gb)
            gstate[...] = cd_ref[0] * gn + _dot_tn(qhb, dob)
            d_ref[0, rows, :] = _rope_t(dq, cos, sin).astype(BF16)
            d_ref[1, rows, :] = _rope_t(dk * scale, cos, sin).astype(BF16)
            d_ref[2, rows, :] = dv.astype(BF16)
            d_ref[3, rows, :] = dg.astype(BF16)
            return carry

        lax.fori_loop(0, nck, chunk, 0, unroll=True)

    rev = lambda s: nseg - 1 - s
    col, tab, head, cd, gain = _ret_specs(T, tseg, rev)
    act = pl.BlockSpec((tseg, Hd), lambda h, s: (rev(s), h))
    return _call(
        body, hosted, name=name, grid=(G, nseg),
        in_specs=[col(1), col(2), col(3), col(4), act, act, pl.BlockSpec((1, nck, Hd, Hd), lambda h, s: (h, rev(s), 0, 0)),
                  gain, tab, tab, head, head, head, cd, ANY],
        out_specs=[pl.BlockSpec((4, tseg, Hd), lambda h, s: (0, rev(s), h)), gain],
        out_shape=[jax.ShapeDtypeStruct(dproj.shape, BF16), jax.ShapeDtypeStruct((1, G * Hd), F32)],
        scratch_shapes=[pltpu.VMEM((Hd, Hd), F32)], aliased={14: 0},
        args=[proj, proj, proj, proj, db, o_pre, r_prev, ret_norm, cos2, sin2, intra, k_tail, q_head, chunk_decay, dproj])


def ffn_down_loss(s, w2, h, gain, target, name, hosted=()):
    T, F = s.shape
    D = h.shape[1]
    tm = min(T, 512)

    def body(s_ref, w2_ref, h_ref, g_ref, t_ref, dh_ref, loss_ref, dg_ref):
        @pl.when(pl.program_id(0) == 0)
        def _():
            loss_ref[...] = jnp.zeros_like(loss_ref)
            dg_ref[...] = jnp.zeros_like(dg_ref)

        hh = h_ref[...] + 0.5 * _dot(s_ref[...], w2_ref[...])
        gain_v = g_ref[...]
        err = hh * _rstd(hh) * gain_v - t_ref[...]
        loss_ref[...] += 0.5 * jnp.sum(jnp.mean(err * err, axis=-1, keepdims=True), axis=0, keepdims=True)
        dhn, dg = _rmsnorm_bwd(err * (1.0 / D), hh, gain_v)
        dh_ref[...] = dhn
        dg_ref[...] += jnp.sum(dg, axis=0, keepdims=True)

    row_spec = pl.BlockSpec((tm, D), lambda i: (i, 0))
    vec_spec = pl.BlockSpec((1, D), lambda i: (0, 0))
    return _call(
        body, hosted, name=name, grid=(T // tm,),
        in_specs=[pl.BlockSpec((tm, F), lambda i: (i, 0)), _resident((F, D)), row_spec, vec_spec, row_spec],
        out_specs=[row_spec, pl.BlockSpec((1, LANES), lambda i: (0, 0)), vec_spec],
        out_shape=[jax.ShapeDtypeStruct((T, D), F32), jax.ShapeDtypeStruct((1, LANES), F32), jax.ShapeDtypeStruct((1, D), F32)],
        args=[s, w2, h, gain, target])


def prereduce(grads, recvs, place, name):
    nt = len(grads)
    nsh, R, C = grads[0].shape
    rh = R // 2

    def body(place_ref, *refs):
        for t in range(nt):
            g_ref, r_ref, o_ref, own_ref = refs[2 * t], refs[2 * t + 1], refs[2 * nt + 2 * t], refs[2 * nt + 2 * t + 1]
            piece = (g_ref[...].astype(F32) + r_ref[...].astype(F32)).astype(BF16)
            o_ref[...] = piece

            @pl.when(pl.program_id(0) == place_ref[1])
            def _():
                own_ref[...] = piece

    outs = pl.pallas_call(
        body, name=name,
        grid_spec=pltpu.PrefetchScalarGridSpec(
            num_scalar_prefetch=1, grid=(nsh,),
            in_specs=[pl.BlockSpec((1, rh, C), lambda j, p: (j, p[0], 0)), pl.BlockSpec((1, rh, C), lambda j, p: (j, 0, 0))] * nt,
            out_specs=[pl.BlockSpec((1, rh, C), lambda j, p: (j, 0, 0)),
                       pl.BlockSpec((1, rh, C), lambda j, p: (p[1], p[0], 0))] * nt),
        out_shape=[jax.ShapeDtypeStruct((nsh, rh, C), BF16), jax.ShapeDtypeStruct((nsh, R, C), BF16)] * nt,
        compiler_params=pltpu.CompilerParams(vmem_limit_bytes=VMEM_LIMIT_V7X),
    )(place, *[a for pair in zip(grads, recvs) for a in pair])
    return [(outs[2 * t], outs[2 * t + 1]) for t in range(nt)]


def _adamw(w, g, m, v):
    m = ADAM_B1 * m + (1.0 - ADAM_B1) * g
    v = ADAM_B2 * v + (1.0 - ADAM_B2) * (g * g)
    m_hat = m / (1.0 - ADAM_B1 ** ADAM_STEP)
    v_hat = v / (1.0 - ADAM_B2 ** ADAM_STEP)
    return -ADAM_LR * (m_hat / (jnp.sqrt(v_hat) + ADAM_EPS) + ADAM_WD * w), m, v


def adamw_sharded(tensors, name, hosted=()):
    nt = len(tensors)
    nsh = tensors[0][0].shape[0]
    shapes = [t[0].shape[1:] for t in tensors]

    def fits(steps):
        if any(R % (steps * BF16_TILE_ROWS) for R, _ in shapes):
            return False
        return sum(2 * (R // steps) * -(-C // LANES) * LANES * (nsh * 2 + 7 * 4) for R, C in shapes) <= ADAMW_VMEM_BUDGET

    steps = min(s for s in range(1, min(R for R, _ in shapes) // BF16_TILE_ROWS + 1) if fits(s))

    def body(*refs):
        ins, outs = refs[:4 * nt], refs[4 * nt:]
        for t in range(nt):
            p_ref, w_ref, m_ref, v_ref = ins[4 * t:4 * t + 4]
            g_ref, d_ref, nm_ref, nv_ref = outs[4 * t:4 * t + 4]
            g = p_ref[0].astype(F32)
            for i in range(1, nsh):
                g += p_ref[i].astype(F32)
            g_ref[...] = g
            d_ref[...], nm_ref[...], nv_ref[...] = _adamw(w_ref[...], g, m_ref[...], v_ref[...])

    in_specs, out_specs, out_shape = [], [], []
    for R, C in shapes:
        spec = pl.BlockSpec((R // steps, C), lambda i: (i, 0))
        in_specs += [pl.BlockSpec((nsh, R // steps, C), lambda i: (0, i, 0)), spec, spec, spec]
        out_specs += [spec] * 4
        out_shape += [jax.ShapeDtypeStruct((R, C), F32)] * 4
    return _call(body, hosted, name=name, grid=(steps,), in_specs=in_specs, out_specs=out_specs, out_shape=out_shape,
                 args=[a for tensor in tensors for a in tensor])


def adamw_small(packs, params, loss_packs, name):
    n = len(packs)
    ndev = loss_packs.shape[0]

    def body(*refs):
        p_refs, loss_ref, wmv = refs[:n], refs[n], refs[n + 1:4 * n + 1]
        outs, loss_out = refs[4 * n + 1:8 * n + 1], refs[8 * n + 1]
        total = lambda r: sum((r[i] for i in range(1, ndev)), r[0])
        loss_out[...] = total(loss_ref)
        for k in range(n):
            g = total(p_refs[k])
            outs[4 * k][...] = g
            outs[4 * k + 1][...], outs[4 * k + 2][...], outs[4 * k + 3][...] = _adamw(
                wmv[3 * k][...], g, wmv[3 * k + 1][...], wmv[3 * k + 2][...])

    out_shape = [jax.ShapeDtypeStruct(p[0].shape, F32) for p in params for _ in range(4)]
    outs = pl.pallas_call(body, name=name, out_shape=out_shape + [jax.ShapeDtypeStruct(loss_packs.shape[1:], F32)],
                          compiler_params=pltpu.CompilerParams(vmem_limit_bytes=VMEM_LIMIT_V7X),
                          )(*packs, loss_packs, *[a for p in params for a in p])
    return [outs[4 * k:4 * k + 4] for k in range(n)], outs[4 * n]


BIG = ("ffn1_w1", "ffn1_w3", "ffn1_w2", "w_in", "w_out", "ffn2_w1", "ffn2_w3", "ffn2_w2")
TRANSPOSED = ("ffn1_w1", "ffn1_w3", "ffn2_w1", "ffn2_w3")
SMALL = ("pool_w", "mix_norm", "pool_scale", "ret_norm", "ffn2_norm", "final_norm", "ffn1_norm")
WEIGHTS = ("ffn1_norm", "ffn1_w1", "ffn1_w3", "ffn1_w2", "mix_norm", "w_in", "pool_w", "pool_scale", "ret_norm", "w_out",
           "ffn2_norm", "ffn2_w1", "ffn2_w3", "ffn2_w2", "final_norm")


def kernel(x, ffn1_norm, ffn1_w1, ffn1_w3, ffn1_w2, mix_norm, w_in, pool_w, pool_scale, ret_norm, w_out, ffn2_norm, ffn2_w1, ffn2_w3, ffn2_w2, final_norm, loss_target, m_ffn1_norm, m_ffn1_w1, m_ffn1_w3, m_ffn1_w2, m_mix_norm, m_w_in, m_pool_w, m_pool_scale, m_ret_norm, m_w_out, m_ffn2_norm, m_ffn2_w1, m_ffn2_w3, m_ffn2_w2, m_final_norm, v_ffn1_norm, v_ffn1_w1, v_ffn1_w3, v_ffn1_w2, v_mix_norm, v_w_in, v_pool_w, v_pool_scale, v_ret_norm, v_w_out, v_ffn2_norm, v_ffn2_w1, v_ffn2_w3, v_ffn2_w2, v_final_norm):
    w = dict(ffn1_norm=ffn1_norm, ffn1_w1=ffn1_w1, ffn1_w3=ffn1_w3, ffn1_w2=ffn1_w2, mix_norm=mix_norm, w_in=w_in, pool_w=pool_w,
             pool_scale=pool_scale, ret_norm=ret_norm, w_out=w_out, ffn2_norm=ffn2_norm, ffn2_w1=ffn2_w1, ffn2_w3=ffn2_w3,
             ffn2_w2=ffn2_w2, final_norm=final_norm)
    m = dict(ffn1_norm=m_ffn1_norm, ffn1_w1=m_ffn1_w1, ffn1_w3=m_ffn1_w3, ffn1_w2=m_ffn1_w2, mix_norm=m_mix_norm, w_in=m_w_in,
             pool_w=m_pool_w, pool_scale=m_pool_scale, ret_norm=m_ret_norm, w_out=m_w_out, ffn2_norm=m_ffn2_norm, ffn2_w1=m_ffn2_w1,
             ffn2_w3=m_ffn2_w3, ffn2_w2=m_ffn2_w2, final_norm=m_final_norm)
    v = dict(ffn1_norm=v_ffn1_norm, ffn1_w1=v_ffn1_w1, ffn1_w3=v_ffn1_w3, ffn1_w2=v_ffn1_w2, mix_norm=v_mix_norm, w_in=v_w_in,
             pool_w=v_pool_w, pool_scale=v_pool_scale, ret_norm=v_ret_norm, w_out=v_w_out, ffn2_norm=v_ffn2_norm, ffn2_w1=v_ffn2_w1,
             ffn2_w3=v_ffn2_w3, ffn2_w2=v_ffn2_w2, final_norm=v_final_norm)
    xs, target = x[0], loss_target[0]
    T = xs.shape[0]
    tables = _ret_tables(T)
    place = jnp.stack([lax.axis_index("c"), 2 * lax.axis_index("x") + lax.axis_index("y")]).astype(jnp.int32)
    local = lambda d, k: jnp.transpose(d[k][0]) if k in TRANSPOSED else d[k][0]
    result = lambda o, k: jnp.transpose(o)[None] if k in TRANSPOSED else o[None]
    first = ("ffn1_w1", "ffn1_w3")
    sh = {k: local(w, k).astype(BF16) for k in first}
    gather = lambda *names: [ChipExchange([sh[k] for k in names], False)]
    wg, grad, delta, new_m, new_v = {}, {}, {}, {}, {}

    def update(names, pieces, name, hosted=()):
        outs, extras = adamw_sharded([(p, local(w, k), local(m, k), local(v, k)) for k, p in zip(names, pieces)], name, hosted)
        for t, k in enumerate(names):
            grad[k], delta[k], new_m[k], new_v[k] = [result(o, k) for o in outs[4 * t:4 * t + 4]]
        return extras

    def reduce_in_chip(name, *pairs):
        reduced = prereduce([p for p, _ in pairs], [r for _, r in pairs], place, "prereduce_" + name)
        return reduced[0] if len(pairs) == 1 else reduced

    scatter = lambda *reduced: ChipExchange([r[0] for r in reduced], True, [r[1] for r in reduced])
    whole = lambda k: wg[k].reshape(-1, wg[k].shape[-1])
    sharded = lambda g: g.reshape(N_CHIPS, -1, g.shape[-1])

    later = [k for k in BIG if k not in first]
    casts, ((wg["ffn1_w1"], wg["ffn1_w3"]),) = cast_shards([local(w, k) for k in later], "cast_gather_ffn1", gather(*first))
    sh.update(zip(later, casts))
    (n1, ga1, gb1, s1), ((wg["ffn1_w2"], wg["w_in"]),) = ffn_up(
        xs, ffn1_norm, whole("ffn1_w1"), whole("ffn1_w3"), "ffn1_up", gather("ffn1_w2", "w_in"))
    (h1, u, proj), ((wg["w_out"], wg["ffn2_w1"]),) = ffn_down_mix_in(
        s1, whole("ffn1_w2"), xs, mix_norm, wg["w_in"], "ffn1_down_mix_in", gather("w_out", "ffn2_w1"))
    (pa,), _ = pool_fwd(proj, pool_w[0], pool_scale, "pool_fwd")
    (rb, o_pre, r_prev), ((wg["ffn2_w3"],),) = ret_fwd(proj, ret_norm, tables, "ret_fwd", gather("ffn2_w3"))
    (h2, n2, ga2, gb2, s2), ((wg["ffn2_w2"],),) = ffn_up(
        h1, ffn2_norm, whole("ffn2_w1"), whole("ffn2_w3"), "mix_out_ffn2_up", gather("ffn2_w2"), mixed=(pa, rb, wg["w_out"]))
    (dh3, loss, d_final), _ = ffn_down_loss(s2, whole("ffn2_w2"), h2, final_norm[None], target, "ffn2_down_loss")

    (da2, db2, df2), _ = ffn_bwd_act(dh3, whole("ffn2_w2"), ga2, gb2, "ffn2_bwd_act")
    (g_f2w2,), _ = ffn_dw([s2], df2, 1, "ffn2_dw2")
    g_f2w2 = sharded(g_f2w2)
    (g_f2w1, g_f2w3), ((r_f2w2,),) = ffn_dw([da2, db2], n2, 2, "ffn2_dw13", [SiblingExchange([g_f2w2])])
    g_f2w1, g_f2w3 = sharded(g_f2w1), sharded(g_f2w3)
    p_f2w2 = reduce_in_chip("ffn2_w2", (g_f2w2, r_f2w2))
    (dh2, d_ffn2), ((q_f2w2,), (r_f2w1, r_f2w3)) = ffn_bwd_in(
        da2, db2, whole("ffn2_w1"), whole("ffn2_w3"), h2, ffn2_norm, dh3, "ffn2_bwd_in",
        [scatter(p_f2w2), SiblingExchange([g_f2w1, g_f2w3])])
    p_f2w1, p_f2w3 = reduce_in_chip("ffn2_w13", (g_f2w1, r_f2w1), (g_f2w3, r_f2w3))
    (dpa, drb, g_wout), _ = mix_out_bwd(dh2, wg["w_out"], pa, rb, "mix_out_bwd")
    (dproj, d_pool_w, d_pool_scale), _ = pool_bwd(proj, dpa, pool_w[0], pool_scale, "pool_bwd")
    (dproj, d_ret_norm), ((q_f2w1, q_f2w3), (r_wout,)) = ret_bwd(
        proj, drb, o_pre, r_prev, ret_norm, tables, dproj, "ret_bwd", [scatter(p_f2w1, p_f2w3), SiblingExchange([g_wout])])
    p_wout = reduce_in_chip("w_out", (g_wout, r_wout))
    (g_win,), ((q_wout,),) = mix_dwin(u, dproj, N_CHIPS, "mix_dwin", [scatter(p_wout)])
    (dh1, d_mix), ((r_win,),) = mix_in_bwd(dproj, wg["w_in"], h1, mix_norm, dh2, "mix_in_bwd", [SiblingExchange([g_win])])
    p_win = reduce_in_chip("w_in", (g_win, r_win))
    (da1, db1, df1), ((q_win,),) = ffn_bwd_act(dh1, whole("ffn1_w2"), ga1, gb1, "ffn1_bwd_act", [scatter(p_win)])
    d_small = {"pool_w": d_pool_w.reshape(-1, LANES), "mix_norm": d_mix, "pool_scale": d_pool_scale, "ret_norm": d_ret_norm,
               "ffn2_norm": d_ffn2, "final_norm": d_final}
    (g_f1w1, g_f1w3), (packs,) = ffn_dw([da1, db1], n1, 2, "ffn1_dw13", [AllExchange([d_small[k] for k in SMALL[:-1]] + [loss])])
    g_f1w1, g_f1w3 = sharded(g_f1w1), sharded(g_f1w3)
    (g_f1w2,), ((r_f1w1, r_f1w3),) = ffn_dw([s1], df1, 1, "ffn1_dw2", [SiblingExchange([g_f1w1, g_f1w3])])
    g_f1w2 = sharded(g_f1w2)
    p_f1w1, p_f1w3 = reduce_in_chip("ffn1_w13", (g_f1w1, r_f1w1), (g_f1w3, r_f1w3))
    (dx, d_ffn1), ((q_f1w1, q_f1w3), (r_f1w2,)) = ffn_bwd_in(
        da1, db1, whole("ffn1_w1"), whole("ffn1_w3"), xs, ffn1_norm, dh1, "ffn1_bwd_in",
        [scatter(p_f1w1, p_f1w3), SiblingExchange([g_f1w2])])
    p_f1w2 = reduce_in_chip("ffn1_w2", (g_f1w2, r_f1w2))

    (q_f1w2,), (late,) = update(["w_in", "w_out"], [q_win, q_wout], "adamw_mix", [scatter(p_f1w2), AllExchange([d_ffn1])])
    update(["ffn2_w2", "ffn2_w1", "ffn2_w3", "ffn1_w1", "ffn1_w3", "ffn1_w2"],
           [q_f2w2, q_f2w1, q_f2w3, q_f1w1, q_f1w3, q_f1w2], "adamw_ffn")
    flat = lambda t, k: t[k].reshape(-1, LANES) if k == "pool_w" else t[k].reshape(1, -1)
    updated, loss_sum = adamw_small(packs[:-1] + [late], [[flat(t, k) for t in (w, m, v)] for k in SMALL], packs[-1], "adamw_small")
    for k, outs in zip(SMALL, updated):
        grad[k], delta[k], new_m[k], new_v[k] = [o.reshape(w[k].shape) for o in outs]
    loss = loss_sum[0, 0]

    return (loss, dx[None], *[grad[k] for k in WEIGHTS], *[delta[k] for k in WEIGHTS],
            *[new_m[k] for k in WEIGHTS], *[new_v[k] for k in WEIGHTS])
```

```python
import math

import jax
import jax.numpy as jnp
import numpy as np
from jax import lax
from jax.experimental import pallas as pl
from jax.experimental.pallas import tpu as pltpu

F32 = jnp.float32
BF16 = jnp.bfloat16

EPS = 1e-6
LANES = 128
BF16_TILE_ROWS = 16
N_CHIPS = 4
N_GROUPS = 4
HEAD_DIM = 128
RET_CHUNK = 128
ROPE_BASE = 10000.0
ADAM_LR, ADAM_B1, ADAM_B2, ADAM_EPS, ADAM_WD, ADAM_STEP = 0.001, 0.9, 0.999, 1e-08, 0.01, 10
VMEM_LIMIT_V7X = 56 * 1024 * 1024
ADAMW_VMEM_BUDGET = 32 * 1024 * 1024
MESH = pl.DeviceIdType.MESH
ANY = pl.BlockSpec(memory_space=pl.ANY)


def _dot(a, b):
    return jnp.dot(a, b, preferred_element_type=F32)


def _dot_nt(a, b):
    return lax.dot_general(a, b, (((1,), (1,)), ((), ())), preferred_element_type=F32)


def _dot_tn(a, b):
    return lax.dot_general(a, b, (((0,), (0,)), ((), ())), preferred_element_type=F32)


def _rstd(h):
    return lax.rsqrt(jnp.mean(h * h, axis=-1, keepdims=True) + EPS)


def _rmsnorm_bwd(dn, h, gain):
    r = _rstd(h)
    nh = h * r
    dnh = dn * gain
    dh = r * (dnh - nh * jnp.mean(dnh * nh, axis=-1, keepdims=True))
    return dh, dn * nh


def _silu_parts(a):
    sig = jax.nn.sigmoid(a)
    silu = a * sig
    return silu, sig + silu * (1.0 - sig)


def _mesh_pos():
    return lax.axis_index("x"), lax.axis_index("y"), lax.axis_index("c")


class ChipExchange:
    def __init__(self, srcs, scatter, placed=()):
        n = len(srcs)
        self.inputs, self.scatter, self.n, self.reach = list(srcs) + list(placed), scatter, n, REACH_CHIPS
        self.aliases = {n + t: t for t in range(n)} if scatter else {}
        self.half_rows = [s.shape[1] if scatter else s.shape[0] // 2 for s in srcs]
        self.out_shape = [jax.ShapeDtypeStruct((N_CHIPS, 2 * rh, s.shape[-1]), s.dtype) for s, rh in zip(srcs, self.half_rows)]
        if scatter:
            self.out_shape += [jax.ShapeDtypeStruct((2, rh // 2, s.shape[-1]), s.dtype) for s, rh in zip(srcs, self.half_rows)]
        dma = pltpu.SemaphoreType.DMA
        self.sems = [dma((4 * n,)), dma((4 * n,)), dma((2 * n,)), dma((2 * n,)), dma((4 * n,)), dma((4 * n,))]

    def _copies(self, src, out, sems):
        hop1_send, hop1_recv, hop2_send, hop2_recv, d2d_send, d2d_recv = sems
        x, y, c = _mesh_pos()
        me, dg = 2 * x + y, 2 * (1 - x) + (1 - y)
        sibling = (x, y, 1 - c)
        n = self.n
        mine, theirs = c, 1 - c

        def nb(a):
            nx, ny = x ^ (1 - a), y ^ a
            return 2 * nx + ny, (nx, ny, c)

        def remote(s, d, send, recv, k, to):
            return pltpu.make_async_remote_copy(src_ref=s, dst_ref=d, send_sem=send.at[k], recv_sem=recv.at[k],
                                                device_id=to, device_id_type=MESH)

        class Copies:
            def slot(_, t, chip, half):
                rh = self.half_rows[t]
                return out[t].at[chip, pl.ds(half * rh, rh), :]

            def quarter(_, t, chip, q):
                qh = self.half_rows[t] // 2
                return out[t].at[chip, pl.ds(mine * 2 * qh + q * qh, qh), :]

            def own_shard(k, t):
                return remote(src[t], out[t].at[me], d2d_send, d2d_recv, 4 * t + 3, sibling)

            def hop1(k, t, a, transit=False):
                rh = self.half_rows[t]
                chip, to = nb(a)
                if transit:
                    piece = src[t].at[dg, pl.ds(a * (rh // 2), rh // 2), :]
                    return remote(piece, out[n + t].at[a], hop1_send, hop1_recv, 4 * t + 2 + a, to)
                piece = src[t].at[chip] if self.scatter else src[t].at[pl.ds(mine * rh, rh), :]
                return remote(piece, k.slot(t, me, mine), hop1_send, hop1_recv, 4 * t + a, to)

            def landed1(k, t, a, transit=False):
                here = out[n + t].at[a] if transit else k.slot(t, nb(a)[0], mine)
                return remote(here, here, hop1_send, hop1_recv, 4 * t + (2 if transit else 0) + a, sibling)

            def hop2(k, t, q):
                origin, to = nb(q)[0], nb(1 - q)[1]
                piece = out[n + t].at[q] if self.scatter else k.quarter(t, origin, q)
                return remote(piece, k.quarter(t, origin, q), hop2_send, hop2_recv, 2 * t + q, to)

            def landed2(k, t, q):
                here = k.quarter(t, dg, q)
                return remote(here, here, hop2_send, hop2_recv, 2 * t + q, sibling)

            def d2d(k, t, p, chip, own=False, arriving=False):
                if arriving:
                    there = k.slot(t, chip, theirs)
                    return remote(there, there, d2d_send, d2d_recv, 4 * t + p, sibling)
                piece = src[t].at[me] if own else k.slot(t, chip, mine)
                return remote(piece, k.slot(t, chip, mine), d2d_send, d2d_recv, 4 * t + p, sibling)

        return Copies(), nb, me, dg, c

    def start(self, src, out, sems):
        k, nb, me, dg, c = self._copies(src, out, sems)
        for t in range(self.n):
            for first in range(2):
                a = first ^ c
                k.hop1(t, a).start()
                if self.scatter:
                    k.hop1(t, a, transit=True).start()
            if self.scatter:
                k.d2d(t, 3, me, own=True).start()
            else:
                k.own_shard(t).start()

    def mid(self, src, out, sems):
        k, nb, me, dg, c = self._copies(src, out, sems)
        for t in range(self.n):
            for first in range(2):
                a = first ^ c
                if self.scatter:
                    k.landed1(t, a, transit=True).wait_recv()
                    k.hop2(t, a).start()
                k.landed1(t, a).wait_recv()
                if not self.scatter:
                    k.hop2(t, a).start()
                k.d2d(t, a, nb(a)[0]).start()

    def finish(self, src, out, sems):
        k, nb, me, dg, c = self._copies(src, out, sems)
        for t in range(self.n):
            for q in range(2):
                k.landed2(t, q).wait_recv()
            k.d2d(t, 2, dg).start()
        for t in range(self.n):
            for a in range(2):
                k.d2d(t, a, nb(a)[0], arriving=True).wait_recv()
            k.d2d(t, 2, dg, arriving=True).wait_recv()
            if self.scatter:
                k.d2d(t, 3, me, arriving=True).wait_recv()
        for t in range(self.n):
            for a in range(2):
                k.hop1(t, a).wait_send()
                if self.scatter:
                    k.hop1(t, a, transit=True).wait_send()
                k.hop2(t, a).wait_send()
                k.d2d(t, a, nb(a)[0]).wait_send()
            k.d2d(t, 2, dg).wait_send()
            if self.scatter:
                k.d2d(t, 3, me, own=True).wait_send()
            else:
                k.own_shard(t).wait()


class SiblingExchange:
    def __init__(self, grads):
        self.inputs, self.n, self.aliases, self.reach = list(grads), len(grads), {}, REACH_SIBLING
        self.half_rows = [g.shape[1] // 2 for g in grads]
        self.out_shape = [jax.ShapeDtypeStruct((g.shape[0], rh, g.shape[2]), g.dtype) for g, rh in zip(grads, self.half_rows)]
        self.sems = [pltpu.SemaphoreType.DMA((self.n,)), pltpu.SemaphoreType.DMA((self.n,))]

    def _plan(self, src, out, sems):
        x, y, c = _mesh_pos()
        return [pltpu.make_async_remote_copy(
            src_ref=src[t].at[:, pl.ds((1 - c) * self.half_rows[t], self.half_rows[t]), :], dst_ref=out[t],
            send_sem=sems[0].at[t], recv_sem=sems[1].at[t], device_id=(x, y, 1 - c), device_id_type=MESH) for t in range(self.n)]

    def start(self, src, out, sems):
        for cp in self._plan(src, out, sems):
            cp.start()

    def mid(self, src, out, sems):
        pass

    def finish(self, src, out, sems):
        for cp in self._plan(src, out, sems):
            cp.wait()


REACH_SIBLING, REACH_CHIPS, REACH_ALL = 0, 1, 2


def _entry_barrier(reach):
    x, y, c = _mesh_pos()
    peers = [(x, y, 1 - c)]
    if reach == REACH_CHIPS:
        peers += [(1 - x, y, c), (x, 1 - y, c)]
    elif reach == REACH_ALL:
        peers = [(x ^ dx, y ^ dy, c ^ dc) for dx in (0, 1) for dy in (0, 1) for dc in (0, 1)][1:]
    barrier = pltpu.get_barrier_semaphore()
    for peer in peers:
        pl.semaphore_signal(barrier, inc=1, device_id=peer, device_id_type=MESH)
    pl.semaphore_wait(barrier, len(peers))


def _call(body, hosted=(), *, name, in_specs, out_specs, out_shape, args, grid=(), scratch_shapes=(), aliased=None):
    n_in, n_out, n_scr = len(in_specs), len(out_specs), len(scratch_shapes)
    total = math.prod(grid)
    mid_step = max(0, (3 * total) // 4 - 1)

    def full(*refs):
        pos = [0]

        def take(k):
            pos[0] += k
            return refs[pos[0] - k:pos[0]]

        ins, h_in = take(n_in), [take(len(h.inputs)) for h in hosted]
        outs, h_out = take(n_out), [take(len(h.out_shape)) for h in hosted]
        scr, h_sem = take(n_scr), [take(len(h.sems)) for h in hosted]
        step = 0
        for axis, size in enumerate(grid):
            step = step * size + pl.program_id(axis)

        def phase(at, method):
            if not hosted:
                return

            def run():
                if method == "start":
                    _entry_barrier(reach)
                for h, s, o, m in zip(hosted, h_in, h_out, h_sem):
                    getattr(h, method)(s, o, m)

            if total == 1:
                run()
            else:
                pl.when(step == at)(run)

        phase(0, "start")
        body(*ins, *outs, *scr)
        phase(mid_step, "mid")
        phase(total - 1, "finish")

    aliases, i0, o0 = dict(aliased or {}), n_in, n_out
    for h in hosted:
        aliases.update({i0 + i: o0 + o for i, o in h.aliases.items()})
        i0, o0 = i0 + len(h.inputs), o0 + len(h.out_shape)
    reach = max((h.reach for h in hosted), default=None)
    params = dict(vmem_limit_bytes=VMEM_LIMIT_V7X)
    if hosted:
        params["collective_id"] = reach
    results = pl.pallas_call(
        full, name=name, grid=grid,
        in_specs=list(in_specs) + [ANY] * (i0 - n_in),
        out_specs=list(out_specs) + [ANY] * (o0 - n_out),
        out_shape=list(out_shape) + [s for h in hosted for s in h.out_shape],
        scratch_shapes=list(scratch_shapes) + [s for h in hosted for s in h.sems],
        input_output_aliases=aliases,
        compiler_params=pltpu.CompilerParams(**params),
    )(*args, *[s for h in hosted for s in h.inputs])
    outs, extras, pos = list(results[:n_out]), [], n_out
    for h in hosted:
        extras.append(list(results[pos:pos + h.n]))
        pos += len(h.out_shape)
    return outs, extras


def cast_shards(shards, name, hosted=()):
    n = len(shards)

    def body(*refs):
        for x_ref, o_ref in zip(refs[:n], refs[n:]):
            o_ref[...] = x_ref[...].astype(BF16)

    whole = lambda s: pl.BlockSpec(s.shape, lambda: (0,) * s.ndim)
    return _call(body, hosted, name=name, in_specs=[whole(s) for s in shards], out_specs=[whole(s) for s in shards],
                 out_shape=[jax.ShapeDtypeStruct(s.shape, BF16) for s in shards], args=list(shards))


class AllExchange:
    def __init__(self, arrays):
        n = len(arrays)
        self.inputs, self.n, self.aliases, self.reach = list(arrays), n, {}, REACH_ALL
        self.out_shape = [jax.ShapeDtypeStruct((2 * N_CHIPS,) + a.shape, a.dtype) for a in arrays]
        self.sems = [pltpu.SemaphoreType.DMA((n,)), pltpu.SemaphoreType.DMA((7 * n,)), pltpu.SemaphoreType.DMA((7 * n,))]

    def _copies(self, src, out, sems):
        local_sem, send_sem, recv_sem = sems
        x, y, c = _mesh_pos()
        me = 4 * x + 2 * y + c
        peers = [(x ^ dx, y ^ dy, c ^ dc) for dx in (0, 1) for dy in (0, 1) for dc in (0, 1)][1:]
        remote = lambda s, d, k, to: pltpu.make_async_remote_copy(
            src_ref=s, dst_ref=d, send_sem=send_sem.at[k], recv_sem=recv_sem.at[k], device_id=to, device_id_type=MESH)
        sends, landed, local = [], [], []
        for t in range(self.n):
            local.append(pltpu.make_async_copy(src[t], out[t].at[me], local_sem.at[t]))
            for p, (px, py, pc) in enumerate(peers):
                sends.append(remote(src[t], out[t].at[me], 7 * t + p, (px, py, pc)))
                here = out[t].at[4 * px + 2 * py + pc]
                landed.append(remote(here, here, 7 * t + p, (px, py, pc)))
        return sends, landed, local

    def start(self, src, out, sems):
        sends, _, local = self._copies(src, out, sems)
        for cp in sends + local:
            cp.start()

    def mid(self, src, out, sems):
        pass

    def finish(self, src, out, sems):
        sends, landed, local = self._copies(src, out, sems)
        for cp in landed:
            cp.wait_recv()
        for cp in sends:
            cp.wait_send()
        for cp in local:
            cp.wait()


MXU_COLS = 256


def _resident(shape):
    return pl.BlockSpec(shape, lambda *_: (0,) * len(shape), pipeline_mode=pl.Buffered(1))


def ffn_up(h, gain, w1, w3, name, hosted=(), mixed=None):
    T, D = h.shape
    F = w1.shape[0]
    tm = min(T, 256)

    def body(*refs):
        if mixed is None:
            h_ref, g_ref, w1_ref, w3_ref, n_ref, ga_ref, gb_ref, s_ref = refs
            hh = h_ref[...]
        else:
            pa_ref, rb_ref, wo_ref, h_ref, g_ref, w1_ref, w3_ref, hh_ref, n_ref, ga_ref, gb_ref, s_ref = refs
            hh = h_ref[...] + _dot(pa_ref[...], wo_ref[0]) + _dot(rb_ref[...], wo_ref[1])
            hh_ref[...] = hh
        n = (hh * _rstd(hh) * g_ref[...]).astype(BF16)
        n_ref[...] = n
        for c in range(0, F, MXU_COLS):
            cols = slice(c, c + MXU_COLS)
            a = _dot_nt(n, w1_ref[cols, :])
            b = _dot_nt(n, w3_ref[cols, :])
            silu, dsilu = _silu_parts(a)
            ga_ref[:, cols] = (b * dsilu).astype(BF16)
            gb_ref[:, cols] = silu.astype(BF16)
            s_ref[:, cols] = (silu * b).astype(BF16)

    act = jax.ShapeDtypeStruct((T, F), BF16)
    act_spec = pl.BlockSpec((tm, F), lambda i: (i, 0))
    row_spec = pl.BlockSpec((tm, D), lambda i: (i, 0))
    in_specs = [row_spec, pl.BlockSpec((1, D), lambda i: (0, 0)), _resident((F, D)), _resident((F, D))]
    out_specs, out_shape, args = [row_spec, act_spec, act_spec, act_spec], [jax.ShapeDtypeStruct((T, D), BF16), act, act, act], [h, gain, w1, w3]
    if mixed is not None:
        pa, rb, woutg = mixed
        W = pa.shape[1]
        in_specs = [pl.BlockSpec((tm, W), lambda i: (i, 0))] * 2 + [_resident((2, W, D))] + in_specs
        out_specs, out_shape = [row_spec] + out_specs, [jax.ShapeDtypeStruct((T, D), F32)] + out_shape
        args = [pa, rb, woutg.reshape(2, W, D)] + args
    return _call(body, hosted, name=name, grid=(T // tm,), in_specs=in_specs, out_specs=out_specs, out_shape=out_shape, args=args)


def ffn_bwd_act(dh, w2, ga, gb, name, hosted=()):
    T, D = dh.shape
    F = w2.shape[0]
    tm = min(T, 512)

    def body(dh_ref, w2_ref, ga_ref, gb_ref, da_ref, db_ref, df_ref):
        df = (0.5 * dh_ref[...]).astype(BF16)
        df_ref[...] = df
        for c in range(0, F, MXU_COLS):
            cols = slice(c, c + MXU_COLS)
            ds = _dot_nt(df, w2_ref[cols, :])
            da_ref[:, cols] = (ds * ga_ref[:, cols].astype(F32)).astype(BF16)
            db_ref[:, cols] = (ds * gb_ref[:, cols].astype(F32)).astype(BF16)

    act = jax.ShapeDtypeStruct((T, F), BF16)
    act_spec = pl.BlockSpec((tm, F), lambda i: (i, 0))
    row_spec = pl.BlockSpec((tm, D), lambda i: (i, 0))
    return _call(
        body, hosted, name=name, grid=(T // tm,),
        in_specs=[row_spec, _resident((F, D)), act_spec, act_spec],
        out_specs=[act_spec, act_spec, row_spec],
        out_shape=[act, act, jax.ShapeDtypeStruct((T, D), BF16)],
        args=[dh, w2, ga, gb])


def ffn_dw(xs, y, halves, name, hosted=()):
    T, F = xs[0].shape
    D = y.shape[1]
    nx, fh = len(xs), F // halves
    tk = min(T, 1024)
    nk = T // tk

    def body(*refs):
        y_ref, x_refs, o_refs, accs = refs[0], refs[1:1 + nx], refs[1 + nx:1 + 2 * nx], refs[1 + 2 * nx:]
        k = pl.program_id(1)

        @pl.when(k == 0)
        def _():
            for acc in accs:
                acc[...] = jnp.zeros_like(acc)

        yy = y_ref[...]
        for x_ref, acc in zip(x_refs, accs):
            acc[...] += _dot_tn(x_ref[...], yy)

        @pl.when(k == nk - 1)
        def _():
            for o_ref, acc in zip(o_refs, accs):
                o_ref[...] = acc[...].astype(BF16)

    out = jax.ShapeDtypeStruct((F, D), BF16)
    return _call(
        body, hosted, name=name, grid=(halves, nk),
        in_specs=[pl.BlockSpec((tk, D), lambda j, k: (k, 0))] + [pl.BlockSpec((tk, fh), lambda j, k: (k, j))] * nx,
        out_specs=[pl.BlockSpec((fh, D), lambda j, k: (j, 0))] * nx,
        out_shape=[out] * nx,
        scratch_shapes=[pltpu.VMEM((fh, D), F32)] * nx,
        args=[y] + list(xs))


def ffn_bwd_in(da, db, w1, w3, h, gain, dh, name, hosted=()):
    T, F = da.shape
    D = h.shape[1]
    tm = min(T, 256)

    def body(da_ref, db_ref, w1_ref, w3_ref, h_ref, g_ref, dh_ref, o_ref, dg_ref):
        dn = _dot(da_ref[...], w1_ref[...]) + _dot(db_ref[...], w3_ref[...])
        dhn, dg = _rmsnorm_bwd(dn, h_ref[...], g_ref[...])
        o_ref[...] = dh_ref[...] + dhn

        @pl.when(pl.program_id(0) == 0)
        def _():
            dg_ref[...] = jnp.zeros_like(dg_ref)

        dg_ref[...] += jnp.sum(dg, axis=0, keepdims=True)

    act_spec = pl.BlockSpec((tm, F), lambda i: (i, 0))
    row_spec = pl.BlockSpec((tm, D), lambda i: (i, 0))
    vec_spec = pl.BlockSpec((1, D), lambda i: (0, 0))
    return _call(
        body, hosted, name=name, grid=(T // tm,),
        in_specs=[act_spec, act_spec, _resident((F, D)), _resident((F, D)), row_spec, vec_spec, row_spec],
        out_specs=[row_spec, vec_spec],
        out_shape=[jax.ShapeDtypeStruct((T, D), F32), jax.ShapeDtypeStruct((1, D), F32)],
        args=[da, db, w1, w3, h, gain, dh])


def ffn_down_mix_in(s, w2, h, gain, wing, name, hosted=()):
    T, F = s.shape
    D = h.shape[1]
    nsh, _, Cs = wing.shape
    tm = min(T, 512)

    def body(s_ref, w2_ref, h_ref, g_ref, w_ref, hh_ref, u_ref, p_ref):
        hh = h_ref[...] + 0.5 * _dot(s_ref[...], w2_ref[...])
        hh_ref[...] = hh
        u = (hh * _rstd(hh) * g_ref[...]).astype(BF16)
        u_ref[...] = u
        for j in range(nsh):
            p_ref[:, j * Cs:(j + 1) * Cs] = _dot(u, w_ref[j])

    row_spec = pl.BlockSpec((tm, D), lambda i: (i, 0))
    return _call(
        body, hosted, name=name, grid=(T // tm,),
        in_specs=[pl.BlockSpec((tm, F), lambda i: (i, 0)), _resident((F, D)), row_spec, pl.BlockSpec((1, D), lambda i: (0, 0)),
                  _resident((nsh, D, Cs))],
        out_specs=[row_spec, row_spec, pl.BlockSpec((tm, nsh * Cs), lambda i: (i, 0))],
        out_shape=[jax.ShapeDtypeStruct((T, D), F32), jax.ShapeDtypeStruct((T, D), BF16), jax.ShapeDtypeStruct((T, nsh * Cs), F32)],
        args=[s, w2, h, gain, wing])


def mix_out_bwd(dh, woutg, a, b, name, hosted=()):
    T, D = dh.shape
    W = a.shape[1]
    nsh, Rs, _ = woutg.shape
    wout = woutg.reshape(2, W, D)
    tk = min(T, 512)
    nk = T // tk

    def body(dh_ref, w_ref, a_ref, b_ref, da_ref, db_ref, dw_ref, acc):
        k = pl.program_id(0)

        @pl.when(k == 0)
        def _():
            acc[...] = jnp.zeros_like(acc)

        dhb = dh_ref[...].astype(BF16)
        da_ref[...] = _dot_nt(dhb, w_ref[0])
        db_ref[...] = _dot_nt(dhb, w_ref[1])
        acc[0:W, :] += _dot_tn(a_ref[...], dhb)
        acc[W:2 * W, :] += _dot_tn(b_ref[...], dhb)

        @pl.when(k == nk - 1)
        def _():
            for j in range(nsh):
                dw_ref[j] = acc[j * Rs:(j + 1) * Rs, :].astype(BF16)

    return _call(
        body, hosted, name=name, grid=(nk,),
        in_specs=[pl.BlockSpec((tk, D), lambda k: (k, 0)), pl.BlockSpec((2, W, D), lambda k: (0, 0, 0)),
                  pl.BlockSpec((tk, W), lambda k: (k, 0)), pl.BlockSpec((tk, W), lambda k: (k, 0))],
        out_specs=[pl.BlockSpec((tk, W), lambda k: (k, 0)), pl.BlockSpec((tk, W), lambda k: (k, 0)),
                   pl.BlockSpec((nsh, Rs, D), lambda k: (0, 0, 0))],
        out_shape=[jax.ShapeDtypeStruct((T, W), F32), jax.ShapeDtypeStruct((T, W), F32),
                   jax.ShapeDtypeStruct((nsh, Rs, D), BF16)],
        scratch_shapes=[pltpu.VMEM((2 * W, D), F32)],
        args=[dh, wout, a, b])


def _dproj_block(g):
    return (g // N_GROUPS + N_GROUPS) % (N_GROUPS + 1), g % N_GROUPS


def mix_dwin(u, dproj, nsh, name, hosted=()):
    T, D = u.shape
    Hd = HEAD_DIM
    slabs, _, width = dproj.shape
    blocks = slabs * width // Hd
    Cs = blocks * Hd // nsh
    tk = min(T, 512)
    nk = T // tk

    def body(u_ref, d_ref, o_ref, acc):
        k = pl.program_id(0)

        @pl.when(k == 0)
        def _():
            acc[...] = jnp.zeros_like(acc)

        where = [_dproj_block(g) for g in range(blocks)]
        d = jnp.concatenate([d_ref[slab, :, col * Hd:(col + 1) * Hd] for slab, col in where], axis=1)
        acc[...] += _dot_tn(u_ref[...], d)

        @pl.when(k == nk - 1)
        def _():
            for j in range(nsh):
                o_ref[j] = acc[:, j * Cs:(j + 1) * Cs].astype(BF16)

    return _call(
        body, hosted, name=name, grid=(nk,),
        in_specs=[pl.BlockSpec((tk, D), lambda k: (k, 0)), pl.BlockSpec((slabs, tk, width), lambda k: (0, k, 0))],
        out_specs=[pl.BlockSpec((nsh, D, Cs), lambda k: (0, 0, 0))],
        out_shape=[jax.ShapeDtypeStruct((nsh, D, Cs), BF16)],
        scratch_shapes=[pltpu.VMEM((D, blocks * Hd), F32)],
        args=[u, dproj])


def mix_in_bwd(dproj, wing, h, gain, dh, name, hosted=()):
    T, D = h.shape
    nsh, _, Cs = wing.shape
    Hd = HEAD_DIM
    per = Cs // Hd
    tm = min(T, 512)

    def body(d_ref, w_ref, h_ref, g_ref, dh_ref, o_ref, dg_ref):
        def shard(j):
            blocks = [_dproj_block(per * j + i) for i in range(per)]
            return jnp.concatenate([d_ref[slab, :, col * Hd:(col + 1) * Hd] for slab, col in blocks], axis=1)

        du = _dot_nt(shard(0), w_ref[0])
        for j in range(1, nsh):
            du += _dot_nt(shard(j), w_ref[j])
        dhn, dg = _rmsnorm_bwd(du, h_ref[...], g_ref[...])
        o_ref[...] = dh_ref[...] + dhn

        @pl.when(pl.program_id(0) == 0)
        def _():
            dg_ref[...] = jnp.zeros_like(dg_ref)

        dg_ref[...] += jnp.sum(dg, axis=0, keepdims=True)

    row_spec = pl.BlockSpec((tm, D), lambda i: (i, 0))
    vec_spec = pl.BlockSpec((1, D), lambda i: (0, 0))
    return _call(
        body, hosted, name=name, grid=(T // tm,),
        in_specs=[pl.BlockSpec((dproj.shape[0], tm, dproj.shape[2]), lambda i: (0, i, 0)),
                  pl.BlockSpec((nsh, D, Cs), lambda i: (0, 0, 0)), row_spec, vec_spec, row_spec],
        out_specs=[row_spec, vec_spec],
        out_shape=[jax.ShapeDtypeStruct((T, D), F32), jax.ShapeDtypeStruct((1, D), F32)],
        args=[dproj, wing, h, gain, dh])


POOL_WINDOWS = (2, 4, 8, 16)


def _pool_window(x, window, T, trailing):
    rows = lax.broadcasted_iota(jnp.int32, x.shape, 0)
    s, k = x, 1
    while k < window:
        if trailing:
            s = s + jnp.where(rows >= k, pltpu.roll(s, k, 0), 0.0)
        else:
            s = s + jnp.where(rows < T - k, pltpu.roll(s, T - k, 0), 0.0)
        k *= 2
    return s


def _pool_count(window, shape):
    rows = lax.broadcasted_iota(jnp.int32, shape, 0)
    return jnp.minimum(rows + 1, window).astype(F32)


def _per_group(work):
    for group, window in enumerate(POOL_WINDOWS):
        pl.when(pl.program_id(0) == group)(lambda window=window: work(window))


def pool_fwd(proj, pool_w, pool_scale, name, hosted=()):
    T = proj.shape[0]
    Hd = HEAD_DIM

    def body(x_ref, w_ref, sc_ref, a_ref):
        def work(window):
            x = x_ref[...]
            pooled = _pool_window(x, window, T, True) / _pool_count(window, x.shape) - x
            a_ref[...] = (_dot(pooled.astype(BF16), w_ref[0].astype(BF16)) * sc_ref[...]).astype(BF16)

        _per_group(work)

    return _call(
        body, hosted, name=name, grid=(N_GROUPS,),
        in_specs=[pl.BlockSpec((T, Hd), lambda g: (0, g)), pl.BlockSpec((1, Hd, Hd), lambda g: (g, 0, 0)),
                  pl.BlockSpec((1, Hd), lambda g: (0, g))],
        out_specs=[pl.BlockSpec((T, Hd), lambda g: (0, g))],
        out_shape=[jax.ShapeDtypeStruct((T, N_GROUPS * Hd), BF16)],
        args=[proj, pool_w, pool_scale])


def pool_bwd(proj, da, pool_w, pool_scale, name, hosted=()):
    T = proj.shape[0]
    Hd = HEAD_DIM

    def body(x_ref, da_ref, w_ref, sc_ref, dx_ref, dw_ref, dsc_ref):
        def work(window):
            x = x_ref[...]
            cnt = _pool_count(window, x.shape)
            pooled = (_pool_window(x, window, T, True) / cnt - x).astype(BF16)
            wb = w_ref[0].astype(BF16)
            dav = da_ref[...]
            dsc_ref[...] = jnp.sum(dav * _dot(pooled, wb), axis=0, keepdims=True)
            dout = (dav * sc_ref[...]).astype(BF16)
            dw_ref[0] = _dot_tn(pooled, dout)
            dpooled = _dot_nt(dout, wb)
            dx_ref[0] = (_pool_window(dpooled / cnt, window, T, False) - dpooled).astype(BF16)

        _per_group(work)

    col_spec = pl.BlockSpec((T, Hd), lambda g: (0, g))
    return _call(
        body, hosted, name=name, grid=(N_GROUPS,),
        in_specs=[col_spec, col_spec, pl.BlockSpec((1, Hd, Hd), lambda g: (g, 0, 0)), pl.BlockSpec((1, Hd), lambda g: (0, g))],
        out_specs=[pl.BlockSpec((1, T, Hd), lambda g: (N_GROUPS, 0, g)), pl.BlockSpec((1, Hd, Hd), lambda g: (g, 0, 0)),
                   pl.BlockSpec((1, Hd), lambda g: (0, g))],
        out_shape=[jax.ShapeDtypeStruct((N_GROUPS + 1, T, N_GROUPS * Hd), BF16), jax.ShapeDtypeStruct((N_GROUPS, Hd, Hd), F32),
                   jax.ShapeDtypeStruct((1, N_GROUPS * Hd), F32)],
        args=[proj, da, pool_w, pool_scale])


def _ret_tables(T):
    Hd, C, f32 = HEAD_DIM, RET_CHUNK, np.float32
    inv_freq = (1.0 / (ROPE_BASE ** (np.arange(0, Hd, 2, dtype=np.float64) / Hd))).astype(f32)
    ang = np.arange(T, dtype=f32)[:, None] * inv_freq[None, :]
    cos, sin = np.cos(ang), np.sin(ang)
    cos2 = np.concatenate([cos, cos], axis=-1)
    sin2 = np.concatenate([-sin, sin], axis=-1)
    log_gamma = np.log1p(-np.exp2(f32(-5.0) - np.arange(N_GROUPS, dtype=f32)))
    pos = np.arange(C, dtype=f32)
    rel = pos[:, None] - pos[None, :]
    intra = np.where(rel[None] >= 0, np.exp(log_gamma[:, None, None] * np.maximum(rel, f32(0.0))[None]), f32(0.0))
    k_tail = np.exp(log_gamma[:, None] * (f32(C - 1) - pos)[None, :])
    q_head = np.exp(log_gamma[:, None] * (pos + f32(1.0))[None, :])
    chunk_decay = np.exp(log_gamma * f32(C))
    wide = lambda t: np.broadcast_to(t[:, :, None], (N_GROUPS, C, Hd))
    tables = cos2, sin2, intra, wide(k_tail), wide(q_head), np.broadcast_to(chunk_decay[:, None, None], (N_GROUPS, 1, Hd))
    assert all(t.dtype == f32 for t in tables)
    return tuple(jnp.asarray(t) for t in tables)


def _rope(x, cos2, sin2):
    return x * cos2 + pltpu.roll(x, HEAD_DIM // 2, 1) * sin2


def _rope_t(d, cos2, sin2):
    return d * cos2 + pltpu.roll(d * sin2, HEAD_DIM // 2, 1)


def _ret_specs(T, tseg, seg_of):
    Hd, G = HEAD_DIM, N_GROUPS
    col = lambda kind: pl.BlockSpec((tseg, Hd), lambda h, s: (seg_of(s), G * kind + h))
    tab = pl.BlockSpec((T, Hd), lambda h, s: (0, 0))
    head = pl.BlockSpec((1, RET_CHUNK, Hd), lambda h, s: (h, 0, 0))
    cd = pl.BlockSpec((1, 1, Hd), lambda h, s: (h, 0, 0))
    gain = pl.BlockSpec((1, Hd), lambda h, s: (0, h))
    return col, tab, head, cd, gain


def ret_fwd(proj, ret_norm, tables, name, hosted=()):
    T = proj.shape[0]
    Hd, C, G = HEAD_DIM, RET_CHUNK, N_GROUPS
    tseg = min(T, 2048)
    nseg, nck = T // tseg, tseg // C
    scale = Hd ** -0.5
    cos2, sin2, intra, k_tail, q_head, chunk_decay = tables

    def body(q_ref, k_ref, v_ref, g_ref, gain_ref, cos_ref, sin_ref, m_ref, kt_ref, qh_ref, cd_ref,
             b_ref, o_ref, rp_ref, state):
        @pl.when(pl.program_id(1) == 0)
        def _():
            state[...] = jnp.zeros_like(state)

        def chunk(ci, carry):
            rows = pl.ds(pl.multiple_of(ci * C, C), C)
            at = pl.ds(pl.multiple_of(pl.program_id(1) * tseg + ci * C, C), C)
            cos, sin = cos_ref[at, :], sin_ref[at, :]
            qr = _rope(q_ref[rows, :], cos, sin)
            kr = _rope(k_ref[rows, :], cos, sin) * scale
            qb, kb, vb = qr.astype(BF16), kr.astype(BF16), v_ref[rows, :].astype(BF16)
            r = state[...]
            rp_ref[0, ci] = r.astype(BF16)
            sc = _dot_nt(qb, kb) * m_ref[0]
            o = _dot(sc.astype(BF16), vb) + _dot((qr * qh_ref[0]).astype(BF16), r.astype(BF16))
            state[...] = cd_ref[0] * r + _dot_tn((kr * kt_ref[0]).astype(BF16), vb)
            o_ref[rows, :] = o
            on = o * _rstd(o)
            b_ref[rows, :] = (jax.nn.silu(g_ref[rows, :]) * (on * gain_ref[...])).astype(BF16)
            return carry

        lax.fori_loop(0, nck, chunk, 0, unroll=True)

    col, tab, head, cd, gain = _ret_specs(T, tseg, lambda s: s)
    out_col = pl.BlockSpec((tseg, Hd), lambda h, s: (s, h))
    return _call(
        body, hosted, name=name, grid=(G, nseg),
        in_specs=[col(1), col(2), col(3), col(4), gain, tab, tab, head, head, head, cd],
        out_specs=[out_col, out_col, pl.BlockSpec((1, nck, Hd, Hd), lambda h, s: (h, s, 0, 0))],
        out_shape=[jax.ShapeDtypeStruct((T, G * Hd), BF16), jax.ShapeDtypeStruct((T, G * Hd), F32),
                   jax.ShapeDtypeStruct((G, T // C, Hd, Hd), BF16)],
        scratch_shapes=[pltpu.VMEM((Hd, Hd), F32)],
        args=[proj, proj, proj, proj, ret_norm, cos2, sin2, intra, k_tail, q_head, chunk_decay])


def ret_bwd(proj, db, o_pre, r_prev, ret_norm, tables, dproj, name, hosted=()):
    T = proj.shape[0]
    Hd, C, G = HEAD_DIM, RET_CHUNK, N_GROUPS
    tseg = min(T, 2048)
    nseg, nck = T // tseg, tseg // C
    scale = Hd ** -0.5
    cos2, sin2, intra, k_tail, q_head, chunk_decay = tables

    def body(q_ref, k_ref, v_ref, g_ref, db_ref, o_ref, rp_ref, gain_ref, cos_ref, sin_ref, m_ref, kt_ref, qh_ref, cd_ref,
             _, d_ref, dgain_ref, gstate):
        @pl.when(pl.program_id(1) == 0)
        def _():
            gstate[...] = jnp.zeros_like(gstate)
            dgain_ref[...] = jnp.zeros_like(dgain_ref)

        def chunk(t, carry):
            ci = nck - 1 - t
            rows = pl.ds(pl.multiple_of(ci * C, C), C)
            at = pl.ds(pl.multiple_of((nseg - 1 - pl.program_id(1)) * tseg + ci * C, C), C)
            cos, sin = cos_ref[at, :], sin_ref[at, :]
            qr = _rope(q_ref[rows, :], cos, sin)
            kr = _rope(k_ref[rows, :], cos, sin) * scale
            qb, kb, vb = qr.astype(BF16), kr.astype(BF16), v_ref[rows, :].astype(BF16)
            qhb, ktb = (qr * qh_ref[0]).astype(BF16), (kr * kt_ref[0]).astype(BF16)
            sc = (_dot_nt(qb, kb) * m_ref[0]).astype(BF16)
            o = o_ref[rows, :]
            rstd = _rstd(o)
            on = o * rstd
            gain = gain_ref[...]
            silu, dsilu = _silu_parts(g_ref[rows, :])
            dy = db_ref[rows, :]
            dgain_ref[...] += jnp.sum(dy * silu * on, axis=0, keepdims=True)
            dg = dy * on * gain * dsilu
            don = dy * silu * gain
            dob = (rstd * (don - on * jnp.mean(don * on, axis=-1, keepdims=True))).astype(BF16)
            gn = gstate[...]
            gb = gn.astype(BF16)
            da = (_dot_nt(dob, vb) * m_ref[0]).astype(BF16)
            dq = _dot(da, kb) + _dot_nt(dob, rp_ref[0, ci]) * qh_ref[0]
            dk = _dot_tn(da, qb) + _dot_nt(vb, gb) * kt_ref[0]
            dv = _dot_tn(sc, dob) + _dot(ktb, gb)
            gstate[...] = cd_ref[0] * gn + _dot_tn(qhb, dob)
            d_ref[0, rows, :] = _rope_t(dq, cos, sin).astype(BF16)
            d_ref[1, rows, :] = _rope_t(dk * scale, cos, sin).astype(BF16)
            d_ref[2, rows, :] = dv.astype(BF16)
            d_ref[3, rows, :] = dg.astype(BF16)
            return carry

        lax.fori_loop(0, nck, chunk, 0, unroll=True)

    rev = lambda s: nseg - 1 - s
    col, tab, head, cd, gain = _ret_specs(T, tseg, rev)
    act = pl.BlockSpec((tseg, Hd), lambda h, s: (rev(s), h))
    return _call(
        body, hosted, name=name, grid=(G, nseg),
        in_specs=[col(1), col(2), col(3), col(4), act, act, pl.BlockSpec((1, nck, Hd, Hd), lambda h, s: (h, rev(s), 0, 0)),
                  gain, tab, tab, head, head, head, cd, ANY],
        out_specs=[pl.BlockSpec((4, tseg, Hd), lambda h, s: (0, rev(s), h)), gain],
        out_shape=[jax.ShapeDtypeStruct(dproj.shape, BF16), jax.ShapeDtypeStruct((1, G * Hd), F32)],
        scratch_shapes=[pltpu.VMEM((Hd, Hd), F32)], aliased={14: 0},
        args=[proj, proj, proj, proj, db, o_pre, r_prev, ret_norm, cos2, sin2, intra, k_tail, q_head, chunk_decay, dproj])


def ffn_down_loss(s, w2, h, gain, target, name, hosted=()):
    T, F = s.shape
    D = h.shape[1]
    tm = min(T, 512)

    assert not hosted
    deep = pl.Buffered(3)

    def body(s_hbm, w2_ref, h_hbm, g_ref, t_hbm, dh_hbm, loss_ref, dg_ref):
        loss_ref[...] = jnp.zeros_like(loss_ref)
        dg_ref[...] = jnp.zeros_like(dg_ref)

        def step(s_ref, h_ref, t_ref, dh_ref):
            hh = h_ref[...] + 0.5 * _dot(s_ref[...], w2_ref[...])
            gain_v = g_ref[...]
            err = hh * _rstd(hh) * gain_v - t_ref[...]
            loss_ref[...] += 0.5 * jnp.sum(jnp.mean(err * err, axis=-1, keepdims=True), axis=0, keepdims=True)
            dhn, dg = _rmsnorm_bwd(err * (1.0 / D), hh, gain_v)
            dh_ref[...] = dhn
            dg_ref[...] += jnp.sum(dg, axis=0, keepdims=True)

        pltpu.emit_pipeline(
            step, grid=(T // tm,),
            in_specs=[pl.BlockSpec((tm, F), lambda i: (i, 0), pipeline_mode=deep),
                      pl.BlockSpec((tm, D), lambda i: (i, 0), pipeline_mode=deep),
                      pl.BlockSpec((tm, D), lambda i: (i, 0), pipeline_mode=deep)],
            out_specs=[pl.BlockSpec((tm, D), lambda i: (i, 0))],
        )(s_hbm, h_hbm, t_hbm, dh_hbm)

    whole = pl.BlockSpec(memory_space=pltpu.VMEM)
    outs = pl.pallas_call(
        body, name=name,
        in_specs=[ANY, whole, ANY, whole, ANY],
        out_specs=[ANY, whole, whole],
        out_shape=[jax.ShapeDtypeStruct((T, D), F32), jax.ShapeDtypeStruct((1, LANES), F32), jax.ShapeDtypeStruct((1, D), F32)],
        compiler_params=pltpu.CompilerParams(vmem_limit_bytes=VMEM_LIMIT_V7X),
    )(s, w2, h, gain, target)
    return outs, ()


def prereduce(grads, recvs, place, name):
    nt = len(grads)
    nsh, R, C = grads[0].shape
    rh = R // 2

    def body(place_ref, *refs):
        for t in range(nt):
            g_ref, r_ref, o_ref, own_ref = refs[2 * t], refs[2 * t + 1], refs[2 * nt + 2 * t], refs[2 * nt + 2 * t + 1]
            piece = (g_ref[...].astype(F32) + r_ref[...].astype(F32)).astype(BF16)
            o_ref[...] = piece

            @pl.when(pl.program_id(0) == place_ref[1])
            def _():
                own_ref[...] = piece

    outs = pl.pallas_call(
        body, name=name,
        grid_spec=pltpu.PrefetchScalarGridSpec(
            num_scalar_prefetch=1, grid=(nsh,),
            in_specs=[pl.BlockSpec((1, rh, C), lambda j, p: (j, p[0], 0)), pl.BlockSpec((1, rh, C), lambda j, p: (j, 0, 0))] * nt,
            out_specs=[pl.BlockSpec((1, rh, C), lambda j, p: (j, 0, 0)),
                       pl.BlockSpec((1, rh, C), lambda j, p: (p[1], p[0], 0))] * nt),
        out_shape=[jax.ShapeDtypeStruct((nsh, rh, C), BF16), jax.ShapeDtypeStruct((nsh, R, C), BF16)] * nt,
        compiler_params=pltpu.CompilerParams(vmem_limit_bytes=VMEM_LIMIT_V7X),
    )(place, *[a for pair in zip(grads, recvs) for a in pair])
    return [(outs[2 * t], outs[2 * t + 1]) for t in range(nt)]


def _adamw(w, g, m, v):
    m = ADAM_B1 * m + (1.0 - ADAM_B1) * g
    v = ADAM_B2 * v + (1.0 - ADAM_B2) * (g * g)
    m_hat = m / (1.0 - ADAM_B1 ** ADAM_STEP)
    v_hat = v / (1.0 - ADAM_B2 ** ADAM_STEP)
    return -ADAM_LR * (m_hat / (jnp.sqrt(v_hat) + ADAM_EPS) + ADAM_WD * w), m, v


def adamw_sharded(tensors, name, hosted=()):
    nt = len(tensors)
    nsh = tensors[0][0].shape[0]
    shapes = [t[0].shape[1:] for t in tensors]

    def fits(steps):
        if any(R % (steps * BF16_TILE_ROWS) for R, _ in shapes):
            return False
        return sum(2 * (R // steps) * -(-C // LANES) * LANES * (nsh * 2 + 7 * 4) for R, C in shapes) <= ADAMW_VMEM_BUDGET

    steps = min(s for s in range(1, min(R for R, _ in shapes) // BF16_TILE_ROWS + 1) if fits(s))

    def body(*refs):
        ins, outs = refs[:4 * nt], refs[4 * nt:]
        for t in range(nt):
            p_ref, w_ref, m_ref, v_ref = ins[4 * t:4 * t + 4]
            g_ref, d_ref, nm_ref, nv_ref = outs[4 * t:4 * t + 4]
            g = p_ref[0].astype(F32)
            for i in range(1, nsh):
                g += p_ref[i].astype(F32)
            g_ref[...] = g
            d_ref[...], nm_ref[...], nv_ref[...] = _adamw(w_ref[...], g, m_ref[...], v_ref[...])

    in_specs, out_specs, out_shape = [], [], []
    for R, C in shapes:
        spec = pl.BlockSpec((R // steps, C), lambda i: (i, 0))
        in_specs += [pl.BlockSpec((nsh, R // steps, C), lambda i: (0, i, 0)), spec, spec, spec]
        out_specs += [spec] * 4
        out_shape += [jax.ShapeDtypeStruct((R, C), F32)] * 4
    return _call(body, hosted, name=name, grid=(steps,), in_specs=in_specs, out_specs=out_specs, out_shape=out_shape,
                 args=[a for tensor in tensors for a in tensor])


def adamw_small(packs, params, loss_packs, name):
    n = len(packs)
    ndev = loss_packs.shape[0]

    def body(*refs):
        p_refs, loss_ref, wmv = refs[:n], refs[n], refs[n + 1:4 * n + 1]
        outs, loss_out = refs[4 * n + 1:8 * n + 1], refs[8 * n + 1]
        total = lambda r: sum((r[i] for i in range(1, ndev)), r[0])
        loss_out[...] = total(loss_ref)
        for k in range(n):
            g = total(p_refs[k])
            outs[4 * k][...] = g
            outs[4 * k + 1][...], outs[4 * k + 2][...], outs[4 * k + 3][...] = _adamw(
                wmv[3 * k][...], g, wmv[3 * k + 1][...], wmv[3 * k + 2][...])

    out_shape = [jax.ShapeDtypeStruct(p[0].shape, F32) for p in params for _ in range(4)]
    outs = pl.pallas_call(body, name=name, out_shape=out_shape + [jax.ShapeDtypeStruct(loss_packs.shape[1:], F32)],
                          compiler_params=pltpu.CompilerParams(vmem_limit_bytes=VMEM_LIMIT_V7X),
                          )(*packs, loss_packs, *[a for p in params for a in p])
    return [outs[4 * k:4 * k + 4] for k in range(n)], outs[4 * n]


BIG = ("ffn1_w1", "ffn1_w3", "ffn1_w2", "w_in", "w_out", "ffn2_w1", "ffn2_w3", "ffn2_w2")
TRANSPOSED = ("ffn1_w1", "ffn1_w3", "ffn2_w1", "ffn2_w3")
SMALL = ("pool_w", "mix_norm", "pool_scale", "ret_norm", "ffn2_norm", "final_norm", "ffn1_norm")
WEIGHTS = ("ffn1_norm", "ffn1_w1", "ffn1_w3", "ffn1_w2", "mix_norm", "w_in", "pool_w", "pool_scale", "ret_norm", "w_out",
           "ffn2_norm", "ffn2_w1", "ffn2_w3", "ffn2_w2", "final_norm")


def kernel(x, ffn1_norm, ffn1_w1, ffn1_w3, ffn1_w2, mix_norm, w_in, pool_w, pool_scale, ret_norm, w_out, ffn2_norm, ffn2_w1, ffn2_w3, ffn2_w2, final_norm, loss_target, m_ffn1_norm, m_ffn1_w1, m_ffn1_w3, m_ffn1_w2, m_mix_norm, m_w_in, m_pool_w, m_pool_scale, m_ret_norm, m_w_out, m_ffn2_norm, m_ffn2_w1, m_ffn2_w3, m_ffn2_w2, m_final_norm, v_ffn1_norm, v_ffn1_w1, v_ffn1_w3, v_ffn1_w2, v_mix_norm, v_w_in, v_pool_w, v_pool_scale, v_ret_norm, v_w_out, v_ffn2_norm, v_ffn2_w1, v_ffn2_w3, v_ffn2_w2, v_final_norm):
    w = dict(ffn1_norm=ffn1_norm, ffn1_w1=ffn1_w1, ffn1_w3=ffn1_w3, ffn1_w2=ffn1_w2, mix_norm=mix_norm, w_in=w_in, pool_w=pool_w,
             pool_scale=pool_scale, ret_norm=ret_norm, w_out=w_out, ffn2_norm=ffn2_norm, ffn2_w1=ffn2_w1, ffn2_w3=ffn2_w3,
             ffn2_w2=ffn2_w2, final_norm=final_norm)
    m = dict(ffn1_norm=m_ffn1_norm, ffn1_w1=m_ffn1_w1, ffn1_w3=m_ffn1_w3, ffn1_w2=m_ffn1_w2, mix_norm=m_mix_norm, w_in=m_w_in,
             pool_w=m_pool_w, pool_scale=m_pool_scale, ret_norm=m_ret_norm, w_out=m_w_out, ffn2_norm=m_ffn2_norm, ffn2_w1=m_ffn2_w1,
             ffn2_w3=m_ffn2_w3, ffn2_w2=m_ffn2_w2, final_norm=m_final_norm)
    v = dict(ffn1_norm=v_ffn1_norm, ffn1_w1=v_ffn1_w1, ffn1_w3=v_ffn1_w3, ffn1_w2=v_ffn1_w2, mix_norm=v_mix_norm, w_in=v_w_in,
             pool_w=v_pool_w, pool_scale=v_pool_scale, ret_norm=v_ret_norm, w_out=v_w_out, ffn2_norm=v_ffn2_norm, ffn2_w1=v_ffn2_w1,
             ffn2_w3=v_ffn2_w3, ffn2_w2=v_ffn2_w2, final_norm=v_final_norm)
    xs, target = x[0], loss_target[0]
    T = xs.shape[0]
    tables = _ret_tables(T)
    place = jnp.stack([lax.axis_index("c"), 2 * lax.axis_index("x") + lax.axis_index("y")]).astype(jnp.int32)
    local = lambda d, k: jnp.transpose(d[k][0]) if k in TRANSPOSED else d[k][0]
    result = lambda o, k: jnp.transpose(o)[None] if k in TRANSPOSED else o[None]
    first = ("ffn1_w1", "ffn1_w3")
    sh = {k: local(w, k).astype(BF16) for k in first}
    gather = lambda *names: [ChipExchange([sh[k] for k in names], False)]
    wg, grad, delta, new_m, new_v = {}, {}, {}, {}, {}

    def update(names, pieces, name, hosted=()):
        outs, extras = adamw_sharded([(p, local(w, k), local(m, k), local(v, k)) for k, p in zip(names, pieces)], name, hosted)
        for t, k in enumerate(names):
            grad[k], delta[k], new_m[k], new_v[k] = [result(o, k) for o in outs[4 * t:4 * t + 4]]
        return extras

    def reduce_in_chip(name, *pairs):
        reduced = prereduce([p for p, _ in pairs], [r for _, r in pairs], place, "prereduce_" + name)
        return reduced[0] if len(pairs) == 1 else reduced

    scatter = lambda *reduced: ChipExchange([r[0] for r in reduced], True, [r[1] for r in reduced])
    whole = lambda k: wg[k].reshape(-1, wg[k].shape[-1])
    sharded = lambda g: g.reshape(N_CHIPS, -1, g.shape[-1])

    later = [k for k in BIG if k not in first]
    casts, ((wg["ffn1_w1"], wg["ffn1_w3"]),) = cast_shards([local(w, k) for k in later], "cast_gather_ffn1", gather(*first))
    sh.update(zip(later, casts))
    (n1, ga1, gb1, s1), ((wg["ffn1_w2"], wg["w_in"]),) = ffn_up(
        xs, ffn1_norm, whole("ffn1_w1"), whole("ffn1_w3"), "ffn1_up", gather("ffn1_w2", "w_in"))
    (h1, u, proj), ((wg["w_out"], wg["ffn2_w1"]),) = ffn_down_mix_in(
        s1, whole("ffn1_w2"), xs, mix_norm, wg["w_in"], "ffn1_down_mix_in", gather("w_out", "ffn2_w1"))
    (pa,), _ = pool_fwd(proj, pool_w[0], pool_scale, "pool_fwd")
    (rb, o_pre, r_prev), ((wg["ffn2_w3"],),) = ret_fwd(proj, ret_norm, tables, "ret_fwd", gather("ffn2_w3"))
    (h2, n2, ga2, gb2, s2), ((wg["ffn2_w2"],),) = ffn_up(
        h1, ffn2_norm, whole("ffn2_w1"), whole("ffn2_w3"), "mix_out_ffn2_up", gather("ffn2_w2"), mixed=(pa, rb, wg["w_out"]))
    (dh3, loss, d_final), _ = ffn_down_loss(s2, whole("ffn2_w2"), h2, final_norm[None], target, "ffn2_down_loss")

    (da2, db2, df2), _ = ffn_bwd_act(dh3, whole("ffn2_w2"), ga2, gb2, "ffn2_bwd_act")
    (g_f2w2,), _ = ffn_dw([s2], df2, 1, "ffn2_dw2")
    g_f2w2 = sharded(g_f2w2)
    (g_f2w1, g_f2w3), ((r_f2w2,),) = ffn_dw([da2, db2], n2, 2, "ffn2_dw13", [SiblingExchange([g_f2w2])])
    g_f2w1, g_f2w3 = sharded(g_f2w1), sharded(g_f2w3)
    p_f2w2 = reduce_in_chip("ffn2_w2", (g_f2w2, r_f2w2))
    (dh2, d_ffn2), ((q_f2w2,), (r_f2w1, r_f2w3)) = ffn_bwd_in(
        da2, db2, whole("ffn2_w1"), whole("ffn2_w3"), h2, ffn2_norm, dh3, "ffn2_bwd_in",
        [scatter(p_f2w2), SiblingExchange([g_f2w1, g_f2w3])])
    p_f2w1, p_f2w3 = reduce_in_chip("ffn2_w13", (g_f2w1, r_f2w1), (g_f2w3, r_f2w3))
    (dpa, drb, g_wout), _ = mix_out_bwd(dh2, wg["w_out"], pa, rb, "mix_out_bwd")
    (dproj, d_pool_w, d_pool_scale), _ = pool_bwd(proj, dpa, pool_w[0], pool_scale, "pool_bwd")
    (dproj, d_ret_norm), ((q_f2w1, q_f2w3), (r_wout,)) = ret_bwd(
        proj, drb, o_pre, r_prev, ret_norm, tables, dproj, "ret_bwd", [scatter(p_f2w1, p_f2w3), SiblingExchange([g_wout])])
    p_wout = reduce_in_chip("w_out", (g_wout, r_wout))
    (g_win,), ((q_wout,),) = mix_dwin(u, dproj, N_CHIPS, "mix_dwin", [scatter(p_wout)])
    (dh1, d_mix), ((r_win,),) = mix_in_bwd(dproj, wg["w_in"], h1, mix_norm, dh2, "mix_in_bwd", [SiblingExchange([g_win])])
    p_win = reduce_in_chip("w_in", (g_win, r_win))
    (da1, db1, df1), ((q_win,),) = ffn_bwd_act(dh1, whole("ffn1_w2"), ga1, gb1, "ffn1_bwd_act", [scatter(p_win)])
    d_small = {"pool_w": d_pool_w.reshape(-1, LANES), "mix_norm": d_mix, "pool_scale": d_pool_scale, "ret_norm": d_ret_norm,
               "ffn2_norm": d_ffn2, "final_norm": d_final}
    (g_f1w1, g_f1w3), (packs,) = ffn_dw([da1, db1], n1, 2, "ffn1_dw13", [AllExchange([d_small[k] for k in SMALL[:-1]] + [loss])])
    g_f1w1, g_f1w3 = sharded(g_f1w1), sharded(g_f1w3)
    (g_f1w2,), ((r_f1w1, r_f1w3),) = ffn_dw([s1], df1, 1, "ffn1_dw2", [SiblingExchange([g_f1w1, g_f1w3])])
    g_f1w2 = sharded(g_f1w2)
    p_f1w1, p_f1w3 = reduce_in_chip("ffn1_w13", (g_f1w1, r_f1w1), (g_f1w3, r_f1w3))
    (dx, d_ffn1), ((q_f1w1, q_f1w3), (r_f1w2,)) = ffn_bwd_in(
        da1, db1, whole("ffn1_w1"), whole("ffn1_w3"), xs, ffn1_norm, dh1, "ffn1_bwd_in",
        [scatter(p_f1w1, p_f1w3), SiblingExchange([g_f1w2])])
    p_f1w2 = reduce_in_chip("ffn1_w2", (g_f1w2, r_f1w2))

    (q_f1w2,), (late,) = update(["w_in", "w_out"], [q_win, q_wout], "adamw_mix", [scatter(p_f1w2), AllExchange([d_ffn1])])
    update(["ffn2_w2", "ffn2_w1", "ffn2_w3", "ffn1_w1", "ffn1_w3", "ffn1_w2"],
           [q_f2w2, q_f2w1, q_f2w3, q_f1w1, q_f1w3, q_f1w2], "adamw_ffn")
    flat = lambda t, k: t[k].reshape(-1, LANES) if k == "pool_w" else t[k].reshape(1, -1)
    updated, loss_sum = adamw_small(packs[:-1] + [late], [[flat(t, k) for t in (w, m, v)] for k in SMALL], packs[-1], "adamw_small")
    for k, outs in zip(SMALL, updated):
        grad[k], delta[k], new_m[k], new_v[k] = [o.reshape(w[k].shape) for o in outs]
    loss = loss_sum[0, 0]

    return (loss, dx[None], *[grad[k] for k in WEIGHTS], *[delta[k] for k in WEIGHTS],
            *[new_m[k] for k in WEIGHTS], *[new_v[k] for k in WEIGHTS])
```

```python
import math

import jax
import jax.numpy as jnp
import numpy as np
from jax import lax
from jax.experimental import pallas as pl
from jax.experimental.pallas import tpu as pltpu

F32 = jnp.float32
BF16 = jnp.bfloat16

EPS = 1e-6
LANES = 128
BF16_TILE_ROWS = 16
N_CHIPS = 4
N_GROUPS = 4
HEAD_DIM = 128
RET_CHUNK = 128
ROPE_BASE = 10000.0
ADAM_LR, ADAM_B1, ADAM_B2, ADAM_EPS, ADAM_WD, ADAM_STEP = 0.001, 0.9, 0.999, 1e-08, 0.01, 10
VMEM_LIMIT_V7X = 56 * 1024 * 1024
ADAMW_VMEM_BUDGET = 32 * 1024 * 1024
MESH = pl.DeviceIdType.MESH
ANY = pl.BlockSpec(memory_space=pl.ANY)


def _dot(a, b):
    return jnp.dot(a, b, preferred_element_type=F32)


def _dot_nt(a, b):
    return lax.dot_general(a, b, (((1,), (1,)), ((), ())), preferred_element_type=F32)


def _dot_tn(a, b):
    return lax.dot_general(a, b, (((0,), (0,)), ((), ())), preferred_element_type=F32)


def _rstd(h):
    return lax.rsqrt(jnp.mean(h * h, axis=-1, keepdims=True) + EPS)


def _rmsnorm_bwd(dn, h, gain):
    r = _rstd(h)
    nh = h * r
    dnh = dn * gain
    dh = r * (dnh - nh * jnp.mean(dnh * nh, axis=-1, keepdims=True))
    return dh, dn * nh


def _silu_parts(a):
    sig = jax.nn.sigmoid(a)
    silu = a * sig
    return silu, sig + silu * (1.0 - sig)


def _mesh_pos():
    return lax.axis_index("x"), lax.axis_index("y"), lax.axis_index("c")


class ChipExchange:
    def __init__(self, srcs, scatter, placed=()):
        n = len(srcs)
        self.inputs, self.scatter, self.n, self.reach = list(srcs) + list(placed), scatter, n, REACH_CHIPS
        self.aliases = {n + t: t for t in range(n)} if scatter else {}
        self.half_rows = [s.shape[1] if scatter else s.shape[0] // 2 for s in srcs]
        self.out_shape = [jax.ShapeDtypeStruct((N_CHIPS, 2 * rh, s.shape[-1]), s.dtype) for s, rh in zip(srcs, self.half_rows)]
        if scatter:
            self.out_shape += [jax.ShapeDtypeStruct((2, rh // 2, s.shape[-1]), s.dtype) for s, rh in zip(srcs, self.half_rows)]
        dma = pltpu.SemaphoreType.DMA
        self.sems = [dma((4 * n,)), dma((4 * n,)), dma((2 * n,)), dma((2 * n,)), dma((4 * n,)), dma((4 * n,))]

    def _copies(self, src, out, sems):
        hop1_send, hop1_recv, hop2_send, hop2_recv, d2d_send, d2d_recv = sems
        x, y, c = _mesh_pos()
        me, dg = 2 * x + y, 2 * (1 - x) + (1 - y)
        sibling = (x, y, 1 - c)
        n = self.n
        mine, theirs = c, 1 - c

        def nb(a):
            nx, ny = x ^ (1 - a), y ^ a
            return 2 * nx + ny, (nx, ny, c)

        def remote(s, d, send, recv, k, to):
            return pltpu.make_async_remote_copy(src_ref=s, dst_ref=d, send_sem=send.at[k], recv_sem=recv.at[k],
                                                device_id=to, device_id_type=MESH)

        class Copies:
            def slot(_, t, chip, half):
                rh = self.half_rows[t]
                return out[t].at[chip, pl.ds(half * rh, rh), :]

            def quarter(_, t, chip, q):
                qh = self.half_rows[t] // 2
                return out[t].at[chip, pl.ds(mine * 2 * qh + q * qh, qh), :]

            def own_shard(k, t):
                return remote(src[t], out[t].at[me], d2d_send, d2d_recv, 4 * t + 3, sibling)

            def hop1(k, t, a, transit=False):
                rh = self.half_rows[t]
                chip, to = nb(a)
                if transit:
                    piece = src[t].at[dg, pl.ds(a * (rh // 2), rh // 2), :]
                    return remote(piece, out[n + t].at[a], hop1_send, hop1_recv, 4 * t + 2 + a, to)
                piece = src[t].at[chip] if self.scatter else src[t].at[pl.ds(mine * rh, rh), :]
                return remote(piece, k.slot(t, me, mine), hop1_send, hop1_recv, 4 * t + a, to)

            def landed1(k, t, a, transit=False):
                here = out[n + t].at[a] if transit else k.slot(t, nb(a)[0], mine)
                return remote(here, here, hop1_send, hop1_recv, 4 * t + (2 if transit else 0) + a, sibling)

            def hop2(k, t, q):
                origin, to = nb(q)[0], nb(1 - q)[1]
                piece = out[n + t].at[q] if self.scatter else k.quarter(t, origin, q)
                return remote(piece, k.quarter(t, origin, q), hop2_send, hop2_recv, 2 * t + q, to)

            def landed2(k, t, q):
                here = k.quarter(t, dg, q)
                return remote(here, here, hop2_send, hop2_recv, 2 * t + q, sibling)

            def d2d(k, t, p, chip, own=False, arriving=False):
                if arriving:
                    there = k.slot(t, chip, theirs)
                    return remote(there, there, d2d_send, d2d_recv, 4 * t + p, sibling)
                piece = src[t].at[me] if own else k.slot(t, chip, mine)
                return remote(piece, k.slot(t, chip, mine), d2d_send, d2d_recv, 4 * t + p, sibling)

        return Copies(), nb, me, dg, c

    def start(self, src, out, sems):
        k, nb, me, dg, c = self._copies(src, out, sems)
        for t in range(self.n):
            for first in range(2):
                a = first ^ c
                k.hop1(t, a).start()
                if self.scatter:
                    k.hop1(t, a, transit=True).start()
            if self.scatter:
                k.d2d(t, 3, me, own=True).start()
            else:
                k.own_shard(t).start()

    def mid(self, src, out, sems):
        k, nb, me, dg, c = self._copies(src, out, sems)
        for t in range(self.n):
            for first in range(2):
                a = first ^ c
                if self.scatter:
                    k.landed1(t, a, transit=True).wait_recv()
                    k.hop2(t, a).start()
                k.landed1(t, a).wait_recv()
                if not self.scatter:
                    k.hop2(t, a).start()
                k.d2d(t, a, nb(a)[0]).start()

    def finish(self, src, out, sems):
        k, nb, me, dg, c = self._copies(src, out, sems)
        for t in range(self.n):
            for q in range(2):
                k.landed2(t, q).wait_recv()
            k.d2d(t, 2, dg).start()
        for t in range(self.n):
            for a in range(2):
                k.d2d(t, a, nb(a)[0], arriving=True).wait_recv()
            k.d2d(t, 2, dg, arriving=True).wait_recv()
            if self.scatter:
                k.d2d(t, 3, me, arriving=True).wait_recv()
        for t in range(self.n):
            for a in range(2):
                k.hop1(t, a).wait_send()
                if self.scatter:
                    k.hop1(t, a, transit=True).wait_send()
                k.hop2(t, a).wait_send()
                k.d2d(t, a, nb(a)[0]).wait_send()
            k.d2d(t, 2, dg).wait_send()
            if self.scatter:
                k.d2d(t, 3, me, own=True).wait_send()
            else:
                k.own_shard(t).wait()


class SiblingExchange:
    def __init__(self, grads):
        self.inputs, self.n, self.aliases, self.reach = list(grads), len(grads), {}, REACH_SIBLING
        self.half_rows = [g.shape[1] // 2 for g in grads]
        self.out_shape = [jax.ShapeDtypeStruct((g.shape[0], rh, g.shape[2]), g.dtype) for g, rh in zip(grads, self.half_rows)]
        self.sems = [pltpu.SemaphoreType.DMA((self.n,)), pltpu.SemaphoreType.DMA((self.n,))]

    def _plan(self, src, out, sems):
        x, y, c = _mesh_pos()
        return [pltpu.make_async_remote_copy(
            src_ref=src[t].at[:, pl.ds((1 - c) * self.half_rows[t], self.half_rows[t]), :], dst_ref=out[t],
            send_sem=sems[0].at[t], recv_sem=sems[1].at[t], device_id=(x, y, 1 - c), device_id_type=MESH) for t in range(self.n)]

    def start(self, src, out, sems):
        for cp in self._plan(src, out, sems):
            cp.start()

    def mid(self, src, out, sems):
        pass

    def finish(self, src, out, sems):
        for cp in self._plan(src, out, sems):
            cp.wait()


REACH_SIBLING, REACH_CHIPS, REACH_ALL = 0, 1, 2


def _entry_barrier(reach):
    x, y, c = _mesh_pos()
    peers = [(x, y, 1 - c)]
    if reach == REACH_CHIPS:
        peers += [(1 - x, y, c), (x, 1 - y, c)]
    elif reach == REACH_ALL:
        peers = [(x ^ dx, y ^ dy, c ^ dc) for dx in (0, 1) for dy in (0, 1) for dc in (0, 1)][1:]
    barrier = pltpu.get_barrier_semaphore()
    for peer in peers:
        pl.semaphore_signal(barrier, inc=1, device_id=peer, device_id_type=MESH)
    pl.semaphore_wait(barrier, len(peers))


def _call(body, hosted=(), *, name, in_specs, out_specs, out_shape, args, grid=(), scratch_shapes=(), aliased=None):
    n_in, n_out, n_scr = len(in_specs), len(out_specs), len(scratch_shapes)
    total = math.prod(grid)
    mid_step = max(0, (3 * total) // 4 - 1)

    def full(*refs):
        pos = [0]

        def take(k):
            pos[0] += k
            return refs[pos[0] - k:pos[0]]

        ins, h_in = take(n_in), [take(len(h.inputs)) for h in hosted]
        outs, h_out = take(n_out), [take(len(h.out_shape)) for h in hosted]
        scr, h_sem = take(n_scr), [take(len(h.sems)) for h in hosted]
        step = 0
        for axis, size in enumerate(grid):
            step = step * size + pl.program_id(axis)

        def phase(at, method):
            if not hosted:
                return

            def run():
                if method == "start":
                    _entry_barrier(reach)
                for h, s, o, m in zip(hosted, h_in, h_out, h_sem):
                    getattr(h, method)(s, o, m)

            if total == 1:
                run()
            else:
                pl.when(step == at)(run)

        phase(0, "start")
        body(*ins, *outs, *scr)
        phase(mid_step, "mid")
        phase(total - 1, "finish")

    aliases, i0, o0 = dict(aliased or {}), n_in, n_out
    for h in hosted:
        aliases.update({i0 + i: o0 + o for i, o in h.aliases.items()})
        i0, o0 = i0 + len(h.inputs), o0 + len(h.out_shape)
    reach = max((h.reach for h in hosted), default=None)
    params = dict(vmem_limit_bytes=VMEM_LIMIT_V7X)
    if hosted:
        params["collective_id"] = reach
    results = pl.pallas_call(
        full, name=name, grid=grid,
        in_specs=list(in_specs) + [ANY] * (i0 - n_in),
        out_specs=list(out_specs) + [ANY] * (o0 - n_out),
        out_shape=list(out_shape) + [s for h in hosted for s in h.out_shape],
        scratch_shapes=list(scratch_shapes) + [s for h in hosted for s in h.sems],
        input_output_aliases=aliases,
        compiler_params=pltpu.CompilerParams(**params),
    )(*args, *[s for h in hosted for s in h.inputs])
    outs, extras, pos = list(results[:n_out]), [], n_out
    for h in hosted:
        extras.append(list(results[pos:pos + h.n]))
        pos += len(h.out_shape)
    return outs, extras


def cast_shards(shards, name, hosted=()):
    n = len(shards)

    def body(*refs):
        for x_ref, o_ref in zip(refs[:n], refs[n:]):
            o_ref[...] = x_ref[...].astype(BF16)

    whole = lambda s: pl.BlockSpec(s.shape, lambda: (0,) * s.ndim)
    return _call(body, hosted, name=name, in_specs=[whole(s) for s in shards], out_specs=[whole(s) for s in shards],
                 out_shape=[jax.ShapeDtypeStruct(s.shape, BF16) for s in shards], args=list(shards))


class AllExchange:
    def __init__(self, arrays):
        n = len(arrays)
        self.inputs, self.n, self.aliases, self.reach = list(arrays), n, {}, REACH_ALL
        self.out_shape = [jax.ShapeDtypeStruct((2 * N_CHIPS,) + a.shape, a.dtype) for a in arrays]
        self.sems = [pltpu.SemaphoreType.DMA((n,)), pltpu.SemaphoreType.DMA((7 * n,)), pltpu.SemaphoreType.DMA((7 * n,))]

    def _copies(self, src, out, sems):
        local_sem, send_sem, recv_sem = sems
        x, y, c = _mesh_pos()
        me = 4 * x + 2 * y + c
        peers = [(x ^ dx, y ^ dy, c ^ dc) for dx in (0, 1) for dy in (0, 1) for dc in (0, 1)][1:]
        remote = lambda s, d, k, to: pltpu.make_async_remote_copy(
            src_ref=s, dst_ref=d, send_sem=send_sem.at[k], recv_sem=recv_sem.at[k], device_id=to, device_id_type=MESH)
        sends, landed, local = [], [], []
        for t in range(self.n):
            local.append(pltpu.make_async_copy(src[t], out[t].at[me], local_sem.at[t]))
            for p, (px, py, pc) in enumerate(peers):
                sends.append(remote(src[t], out[t].at[me], 7 * t + p, (px, py, pc)))
                here = out[t].at[4 * px + 2 * py + pc]
                landed.append(remote(here, here, 7 * t + p, (px, py, pc)))
        return sends, landed, local

    def start(self, src, out, sems):
        sends, _, local = self._copies(src, out, sems)
        for cp in sends + local:
            cp.start()

    def mid(self, src, out, sems):
        pass

    def finish(self, src, out, sems):
        sends, landed, local = self._copies(src, out, sems)
        for cp in landed:
            cp.wait_recv()
        for cp in sends:
            cp.wait_send()
        for cp in local:
            cp.wait()


MXU_COLS = 256


def _resident(shape):
    return pl.BlockSpec(shape, lambda *_: (0,) * len(shape), pipeline_mode=pl.Buffered(1))


def ffn_up(h, gain, w1, w3, name, hosted=(), mixed=None):
    T, D = h.shape
    F = w1.shape[0]
    tm = min(T, 256)

    def body(*refs):
        if mixed is None:
            h_ref, g_ref, w1_ref, w3_ref, n_ref, ga_ref, gb_ref, s_ref = refs
            hh = h_ref[...]
        else:
            pa_ref, rb_ref, wo_ref, h_ref, g_ref, w1_ref, w3_ref, hh_ref, n_ref, ga_ref, gb_ref, s_ref = refs
            hh = h_ref[...] + _dot(pa_ref[...], wo_ref[0]) + _dot(rb_ref[...], wo_ref[1])
            hh_ref[...] = hh
        n = (hh * _rstd(hh) * g_ref[...]).astype(BF16)
        n_ref[...] = n
        for c in range(0, F, MXU_COLS):
            cols = slice(c, c + MXU_COLS)
            a = _dot_nt(n, w1_ref[cols, :])
            b = _dot_nt(n, w3_ref[cols, :])
            silu, dsilu = _silu_parts(a)
            ga_ref[:, cols] = (b * dsilu).astype(BF16)
            gb_ref[:, cols] = silu.astype(BF16)
            s_ref[:, cols] = (silu * b).astype(BF16)

    act = jax.ShapeDtypeStruct((T, F), BF16)
    act_spec = pl.BlockSpec((tm, F), lambda i: (i, 0))
    row_spec = pl.BlockSpec((tm, D), lambda i: (i, 0))
    in_specs = [row_spec, pl.BlockSpec((1, D), lambda i: (0, 0)), _resident((F, D)), _resident((F, D))]
    out_specs, out_shape, args = [row_spec, act_spec, act_spec, act_spec], [jax.ShapeDtypeStruct((T, D), BF16), act, act, act], [h, gain, w1, w3]
    if mixed is not None:
        pa, rb, woutg = mixed
        W = pa.shape[1]
        in_specs = [pl.BlockSpec((tm, W), lambda i: (i, 0))] * 2 + [_resident((2, W, D))] + in_specs
        out_specs, out_shape = [row_spec] + out_specs, [jax.ShapeDtypeStruct((T, D), F32)] + out_shape
        args = [pa, rb, woutg.reshape(2, W, D)] + args
    return _call(body, hosted, name=name, grid=(T // tm,), in_specs=in_specs, out_specs=out_specs, out_shape=out_shape, args=args)


def ffn_bwd_act(dh, w2, ga, gb, name, hosted=()):
    T, D = dh.shape
    F = w2.shape[0]
    tm = min(T, 512)

    def body(dh_ref, w2_ref, ga_ref, gb_ref, da_ref, db_ref, df_ref):
        df = (0.5 * dh_ref[...]).astype(BF16)
        df_ref[...] = df
        for c in range(0, F, MXU_COLS):
            cols = slice(c, c + MXU_COLS)
            ds = _dot_nt(df, w2_ref[cols, :])
            da_ref[:, cols] = (ds * ga_ref[:, cols].astype(F32)).astype(BF16)
            db_ref[:, cols] = (ds * gb_ref[:, cols].astype(F32)).astype(BF16)

    act = jax.ShapeDtypeStruct((T, F), BF16)
    act_spec = pl.BlockSpec((tm, F), lambda i: (i, 0))
    row_spec = pl.BlockSpec((tm, D), lambda i: (i, 0))
    return _call(
        body, hosted, name=name, grid=(T // tm,),
        in_specs=[row_spec, _resident((F, D)), act_spec, act_spec],
        out_specs=[act_spec, act_spec, row_spec],
        out_shape=[act, act, jax.ShapeDtypeStruct((T, D), BF16)],
        args=[dh, w2, ga, gb])


def ffn_dw(xs, y, halves, name, hosted=()):
    T, F = xs[0].shape
    D = y.shape[1]
    nx, fh = len(xs), F // halves
    tk = min(T, 1024)
    nk = T // tk

    def body(*refs):
        y_ref, x_refs, o_refs, accs = refs[0], refs[1:1 + nx], refs[1 + nx:1 + 2 * nx], refs[1 + 2 * nx:]
        k = pl.program_id(1)

        @pl.when(k == 0)
        def _():
            for acc in accs:
                acc[...] = jnp.zeros_like(acc)

        yy = y_ref[...]
        for x_ref, acc in zip(x_refs, accs):
            acc[...] += _dot_tn(x_ref[...], yy)

        @pl.when(k == nk - 1)
        def _():
            for o_ref, acc in zip(o_refs, accs):
                o_ref[...] = acc[...].astype(BF16)

    out = jax.ShapeDtypeStruct((F, D), BF16)
    return _call(
        body, hosted, name=name, grid=(halves, nk),
        in_specs=[pl.BlockSpec((tk, D), lambda j, k: (k, 0))] + [pl.BlockSpec((tk, fh), lambda j, k: (k, j))] * nx,
        out_specs=[pl.BlockSpec((fh, D), lambda j, k: (j, 0))] * nx,
        out_shape=[out] * nx,
        scratch_shapes=[pltpu.VMEM((fh, D), F32)] * nx,
        args=[y] + list(xs))


def ffn_bwd_in(da, db, w1, w3, h, gain, dh, name, hosted=()):
    T, F = da.shape
    D = h.shape[1]
    tm = min(T, 256)

    def body(da_ref, db_ref, w1_ref, w3_ref, h_ref, g_ref, dh_ref, o_ref, dg_ref):
        dn = _dot(da_ref[...], w1_ref[...]) + _dot(db_ref[...], w3_ref[...])
        dhn, dg = _rmsnorm_bwd(dn, h_ref[...], g_ref[...])
        o_ref[...] = dh_ref[...] + dhn

        @pl.when(pl.program_id(0) == 0)
        def _():
            dg_ref[...] = jnp.zeros_like(dg_ref)

        dg_ref[...] += jnp.sum(dg, axis=0, keepdims=True)

    act_spec = pl.BlockSpec((tm, F), lambda i: (i, 0))
    row_spec = pl.BlockSpec((tm, D), lambda i: (i, 0))
    vec_spec = pl.BlockSpec((1, D), lambda i: (0, 0))
    return _call(
        body, hosted, name=name, grid=(T // tm,),
        in_specs=[act_spec, act_spec, _resident((F, D)), _resident((F, D)), row_spec, vec_spec, row_spec],
        out_specs=[row_spec, vec_spec],
        out_shape=[jax.ShapeDtypeStruct((T, D), F32), jax.ShapeDtypeStruct((1, D), F32)],
        args=[da, db, w1, w3, h, gain, dh])


def ffn_down_mix_in(s, w2, h, gain, wing, name, hosted=()):
    T, F = s.shape
    D = h.shape[1]
    nsh, _, Cs = wing.shape
    tm = min(T, 512)

    def body(s_ref, w2_ref, h_ref, g_ref, w_ref, hh_ref, u_ref, p_ref):
        hh = h_ref[...] + 0.5 * _dot(s_ref[...], w2_ref[...])
        hh_ref[...] = hh
        u = (hh * _rstd(hh) * g_ref[...]).astype(BF16)
        u_ref[...] = u
        for j in range(nsh):
            p_ref[:, j * Cs:(j + 1) * Cs] = _dot(u, w_ref[j])

    row_spec = pl.BlockSpec((tm, D), lambda i: (i, 0))
    return _call(
        body, hosted, name=name, grid=(T // tm,),
        in_specs=[pl.BlockSpec((tm, F), lambda i: (i, 0)), _resident((F, D)), row_spec, pl.BlockSpec((1, D), lambda i: (0, 0)),
                  _resident((nsh, D, Cs))],
        out_specs=[row_spec, row_spec, pl.BlockSpec((tm, nsh * Cs), lambda i: (i, 0))],
        out_shape=[jax.ShapeDtypeStruct((T, D), F32), jax.ShapeDtypeStruct((T, D), BF16), jax.ShapeDtypeStruct((T, nsh * Cs), F32)],
        args=[s, w2, h, gain, wing])


def mix_out_bwd(dh, woutg, a, b, name, hosted=()):
    T, D = dh.shape
    W = a.shape[1]
    nsh, Rs, _ = woutg.shape
    wout = woutg.reshape(2, W, D)
    tk = min(T, 512)
    nk = T // tk

    def body(dh_ref, w_ref, a_ref, b_ref, da_ref, db_ref, dw_ref, acc):
        k = pl.program_id(0)

        @pl.when(k == 0)
        def _():
            acc[...] = jnp.zeros_like(acc)

        dhb = dh_ref[...].astype(BF16)
        da_ref[...] = _dot_nt(dhb, w_ref[0])
        db_ref[...] = _dot_nt(dhb, w_ref[1])
        acc[0:W, :] += _dot_tn(a_ref[...], dhb)
        acc[W:2 * W, :] += _dot_tn(b_ref[...], dhb)

        @pl.when(k == nk - 1)
        def _():
            for j in range(nsh):
                dw_ref[j] = acc[j * Rs:(j + 1) * Rs, :].astype(BF16)

    return _call(
        body, hosted, name=name, grid=(nk,),
        in_specs=[pl.BlockSpec((tk, D), lambda k: (k, 0)), pl.BlockSpec((2, W, D), lambda k: (0, 0, 0)),
                  pl.BlockSpec((tk, W), lambda k: (k, 0)), pl.BlockSpec((tk, W), lambda k: (k, 0))],
        out_specs=[pl.BlockSpec((tk, W), lambda k: (k, 0)), pl.BlockSpec((tk, W), lambda k: (k, 0)),
                   pl.BlockSpec((nsh, Rs, D), lambda k: (0, 0, 0))],
        out_shape=[jax.ShapeDtypeStruct((T, W), F32), jax.ShapeDtypeStruct((T, W), F32),
                   jax.ShapeDtypeStruct((nsh, Rs, D), BF16)],
        scratch_shapes=[pltpu.VMEM((2 * W, D), F32)],
        args=[dh, wout, a, b])


def _dproj_block(g):
    return (g // N_GROUPS + N_GROUPS) % (N_GROUPS + 1), g % N_GROUPS


def mix_dwin(u, dproj, nsh, name, hosted=()):
    T, D = u.shape
    Hd = HEAD_DIM
    slabs, _, width = dproj.shape
    blocks = slabs * width // Hd
    Cs = blocks * Hd // nsh
    tk = min(T, 512)
    nk = T // tk

    def body(u_ref, d_ref, o_ref, acc):
        k = pl.program_id(0)

        @pl.when(k == 0)
        def _():
            acc[...] = jnp.zeros_like(acc)

        where = [_dproj_block(g) for g in range(blocks)]
        d = jnp.concatenate([d_ref[slab, :, col * Hd:(col + 1) * Hd] for slab, col in where], axis=1)
        acc[...] += _dot_tn(u_ref[...], d)

        @pl.when(k == nk - 1)
        def _():
            for j in range(nsh):
                o_ref[j] = acc[:, j * Cs:(j + 1) * Cs].astype(BF16)

    return _call(
        body, hosted, name=name, grid=(nk,),
        in_specs=[pl.BlockSpec((tk, D), lambda k: (k, 0)), pl.BlockSpec((slabs, tk, width), lambda k: (0, k, 0))],
        out_specs=[pl.BlockSpec((nsh, D, Cs), lambda k: (0, 0, 0))],
        out_shape=[jax.ShapeDtypeStruct((nsh, D, Cs), BF16)],
        scratch_shapes=[pltpu.VMEM((D, blocks * Hd), F32)],
        args=[u, dproj])


def mix_in_bwd(dproj, wing, h, gain, dh, name, hosted=()):
    T, D = h.shape
    nsh, _, Cs = wing.shape
    Hd = HEAD_DIM
    per = Cs // Hd
    tm = min(T, 512)

    def body(d_ref, w_ref, h_ref, g_ref, dh_ref, o_ref, dg_ref):
        def shard(j):
            blocks = [_dproj_block(per * j + i) for i in range(per)]
            return jnp.concatenate([d_ref[slab, :, col * Hd:(col + 1) * Hd] for slab, col in blocks], axis=1)

        du = _dot_nt(shard(0), w_ref[0])
        for j in range(1, nsh):
            du += _dot_nt(shard(j), w_ref[j])
        dhn, dg = _rmsnorm_bwd(du, h_ref[...], g_ref[...])
        o_ref[...] = dh_ref[...] + dhn

        @pl.when(pl.program_id(0) == 0)
        def _():
            dg_ref[...] = jnp.zeros_like(dg_ref)

        dg_ref[...] += jnp.sum(dg, axis=0, keepdims=True)

    row_spec = pl.BlockSpec((tm, D), lambda i: (i, 0))
    vec_spec = pl.BlockSpec((1, D), lambda i: (0, 0))
    return _call(
        body, hosted, name=name, grid=(T // tm,),
        in_specs=[pl.BlockSpec((dproj.shape[0], tm, dproj.shape[2]), lambda i: (0, i, 0)),
                  pl.BlockSpec((nsh, D, Cs), lambda i: (0, 0, 0)), row_spec, vec_spec, row_spec],
        out_specs=[row_spec, vec_spec],
        out_shape=[jax.ShapeDtypeStruct((T, D), F32), jax.ShapeDtypeStruct((1, D), F32)],
        args=[dproj, wing, h, gain, dh])


POOL_WINDOWS = (2, 4, 8, 16)


def _pool_window(x, window, T, trailing):
    rows = lax.broadcasted_iota(jnp.int32, x.shape, 0)
    s, k = x, 1
    while k < window:
        if trailing:
            s = s + jnp.where(rows >= k, pltpu.roll(s, k, 0), 0.0)
        else:
            s = s + jnp.where(rows < T - k, pltpu.roll(s, T - k, 0), 0.0)
        k *= 2
    return s


def _pool_count(window, shape):
    rows = lax.broadcasted_iota(jnp.int32, shape, 0)
    return jnp.minimum(rows + 1, window).astype(F32)


def _per_group(work):
    for group, window in enumerate(POOL_WINDOWS):
        pl.when(pl.program_id(0) == group)(lambda window=window: work(window))


def pool_fwd(proj, pool_w, pool_scale, name, hosted=()):
    T = proj.shape[0]
    Hd = HEAD_DIM

    def body(x_ref, w_ref, sc_ref, a_ref):
        def work(window):
            x = x_ref[...]
            pooled = _pool_window(x, window, T, True) / _pool_count(window, x.shape) - x
            a_ref[...] = (_dot(pooled.astype(BF16), w_ref[0].astype(BF16)) * sc_ref[...]).astype(BF16)

        _per_group(work)

    return _call(
        body, hosted, name=name, grid=(N_GROUPS,),
        in_specs=[pl.BlockSpec((T, Hd), lambda g: (0, g)), pl.BlockSpec((1, Hd, Hd), lambda g: (g, 0, 0)),
                  pl.BlockSpec((1, Hd), lambda g: (0, g))],
        out_specs=[pl.BlockSpec((T, Hd), lambda g: (0, g))],
        out_shape=[jax.ShapeDtypeStruct((T, N_GROUPS * Hd), BF16)],
        args=[proj, pool_w, pool_scale])


def pool_bwd(proj, da, pool_w, pool_scale, name, hosted=()):
    T = proj.shape[0]
    Hd = HEAD_DIM

    def body(x_ref, da_ref, w_ref, sc_ref, dx_ref, dw_ref, dsc_ref):
        def work(window):
            x = x_ref[...]
            cnt = _pool_count(window, x.shape)
            pooled = (_pool_window(x, window, T, True) / cnt - x).astype(BF16)
            wb = w_ref[0].astype(BF16)
            dav = da_ref[...]
            dsc_ref[...] = jnp.sum(dav * _dot(pooled, wb), axis=0, keepdims=True)
            dout = (dav * sc_ref[...]).astype(BF16)
            dw_ref[0] = _dot_tn(pooled, dout)
            dpooled = _dot_nt(dout, wb)
            dx_ref[0] = (_pool_window(dpooled / cnt, window, T, False) - dpooled).astype(BF16)

        _per_group(work)

    col_spec = pl.BlockSpec((T, Hd), lambda g: (0, g))
    return _call(
        body, hosted, name=name, grid=(N_GROUPS,),
        in_specs=[col_spec, col_spec, pl.BlockSpec((1, Hd, Hd), lambda g: (g, 0, 0)), pl.BlockSpec((1, Hd), lambda g: (0, g))],
        out_specs=[pl.BlockSpec((1, T, Hd), lambda g: (N_GROUPS, 0, g)), pl.BlockSpec((1, Hd, Hd), lambda g: (g, 0, 0)),
                   pl.BlockSpec((1, Hd), lambda g: (0, g))],
        out_shape=[jax.ShapeDtypeStruct((N_GROUPS + 1, T, N_GROUPS * Hd), BF16), jax.ShapeDtypeStruct((N_GROUPS, Hd, Hd), F32),
                   jax.ShapeDtypeStruct((1, N_GROUPS * Hd), F32)],
        args=[proj, da, pool_w, pool_scale])


def _ret_tables(T):
    Hd, C, f32 = HEAD_DIM, RET_CHUNK, np.float32
    inv_freq = (1.0 / (ROPE_BASE ** (np.arange(0, Hd, 2, dtype=np.float64) / Hd))).astype(f32)
    ang = np.arange(T, dtype=f32)[:, None] * inv_freq[None, :]
    cos, sin = np.cos(ang), np.sin(ang)
    cos2 = np.concatenate([cos, cos], axis=-1)
    sin2 = np.concatenate([-sin, sin], axis=-1)
    log_gamma = np.log1p(-np.exp2(f32(-5.0) - np.arange(N_GROUPS, dtype=f32)))
    pos = np.arange(C, dtype=f32)
    rel = pos[:, None] - pos[None, :]
    intra = np.where(rel[None] >= 0, np.exp(log_gamma[:, None, None] * np.maximum(rel, f32(0.0))[None]), f32(0.0))
    k_tail = np.exp(log_gamma[:, None] * (f32(C - 1) - pos)[None, :])
    q_head = np.exp(log_gamma[:, None] * (pos + f32(1.0))[None, :])
    chunk_decay = np.exp(log_gamma * f32(C))
    wide = lambda t: np.broadcast_to(t[:, :, None], (N_GROUPS, C, Hd))
    tables = cos2, sin2, intra, wide(k_tail), wide(q_head), np.broadcast_to(chunk_decay[:, None, None], (N_GROUPS, 1, Hd))
    assert all(t.dtype == f32 for t in tables)
    return tuple(jnp.asarray(t) for t in tables)


def _rope(x, cos2, sin2):
    return x * cos2 + pltpu.roll(x, HEAD_DIM // 2, 1) * sin2


def _rope_t(d, cos2, sin2):
    return d * cos2 + pltpu.roll(d * sin2, HEAD_DIM // 2, 1)


def _ret_specs(T, tseg, seg_of):
    Hd, G = HEAD_DIM, N_GROUPS
    col = lambda kind: pl.BlockSpec((tseg, Hd), lambda h, s: (seg_of(s), G * kind + h))
    tab = pl.BlockSpec((T, Hd), lambda h, s: (0, 0))
    head = pl.BlockSpec((1, RET_CHUNK, Hd), lambda h, s: (h, 0, 0))
    cd = pl.BlockSpec((1, 1, Hd), lambda h, s: (h, 0, 0))
    gain = pl.BlockSpec((1, Hd), lambda h, s: (0, h))
    return col, tab, head, cd, gain


def ret_fwd(proj, ret_norm, tables, name, hosted=()):
    T = proj.shape[0]
    Hd, C, G = HEAD_DIM, RET_CHUNK, N_GROUPS
    tseg = min(T, 2048)
    nseg, nck = T // tseg, tseg // C
    scale = Hd ** -0.5
    cos2, sin2, intra, k_tail, q_head, chunk_decay = tables

    def body(q_ref, k_ref, v_ref, g_ref, gain_ref, cos_ref, sin_ref, m_ref, kt_ref, qh_ref, cd_ref,
             b_ref, o_ref, rp_ref, state):
        @pl.when(pl.program_id(1) == 0)
        def _():
            state[...] = jnp.zeros_like(state)

        def chunk(ci, carry):
            rows = pl.ds(pl.multiple_of(ci * C, C), C)
            at = pl.ds(pl.multiple_of(pl.program_id(1) * tseg + ci * C, C), C)
            cos, sin = cos_ref[at, :], sin_ref[at, :]
            qr = _rope(q_ref[rows, :], cos, sin)
            kr = _rope(k_ref[rows, :], cos, sin) * scale
            qb, kb, vb = qr.astype(BF16), kr.astype(BF16), v_ref[rows, :].astype(BF16)
            r = state[...]
            rp_ref[0, ci] = r.astype(BF16)
            sc = _dot_nt(qb, kb) * m_ref[0]
            o = _dot(sc.astype(BF16), vb) + _dot((qr * qh_ref[0]).astype(BF16), r.astype(BF16))
            state[...] = cd_ref[0] * r + _dot_tn((kr * kt_ref[0]).astype(BF16), vb)
            o_ref[rows, :] = o
            on = o * _rstd(o)
            b_ref[rows, :] = (jax.nn.silu(g_ref[rows, :]) * (on * gain_ref[...])).astype(BF16)
            return carry

        lax.fori_loop(0, nck, chunk, 0, unroll=True)

    col, tab, head, cd, gain = _ret_specs(T, tseg, lambda s: s)
    out_col = pl.BlockSpec((tseg, Hd), lambda h, s: (s, h))
    return _call(
        body, hosted, name=name, grid=(G, nseg),
        in_specs=[col(1), col(2), col(3), col(4), gain, tab, tab, head, head, head, cd],
        out_specs=[out_col, out_col, pl.BlockSpec((1, nck, Hd, Hd), lambda h, s: (h, s, 0, 0))],
        out_shape=[jax.ShapeDtypeStruct((T, G * Hd), BF16), jax.ShapeDtypeStruct((T, G * Hd), F32),
                   jax.ShapeDtypeStruct((G, T // C, Hd, Hd), BF16)],
        scratch_shapes=[pltpu.VMEM((Hd, Hd), F32)],
        args=[proj, proj, proj, proj, ret_norm, cos2, sin2, intra, k_tail, q_head, chunk_decay])


def ret_bwd(proj, db, o_pre, r_prev, ret_norm, tables, dproj, name, hosted=()):
    T = proj.shape[0]
    Hd, C, G = HEAD_DIM, RET_CHUNK, N_GROUPS
    tseg = min(T, 2048)
    nseg, nck = T // tseg, tseg // C
    scale = Hd ** -0.5
    cos2, sin2, intra, k_tail, q_head, chunk_decay = tables

    def body(q_ref, k_ref, v_ref, g_ref, db_ref, o_ref, rp_ref, gain_ref, cos_ref, sin_ref, m_ref, kt_ref, qh_ref, cd_ref,
             _, d_ref, dgain_ref, gstate):
        @pl.when(pl.program_id(1) == 0)
        def _():
            gstate[...] = jnp.zeros_like(gstate)
            dgain_ref[...] = jnp.zeros_like(dgain_ref)

        def chunk(t, carry):
            ci = nck - 1 - t
            rows = pl.ds(pl.multiple_of(ci * C, C), C)
            at = pl.ds(pl.multiple_of((nseg - 1 - pl.program_id(1)) * tseg + ci * C, C), C)
            cos, sin = cos_ref[at, :], sin_ref[at, :]
            qr = _rope(q_ref[rows, :], cos, sin)
            kr = _rope(k_ref[rows, :], cos, sin) * scale
            qb, kb, vb = qr.astype(BF16), kr.astype(BF16), v_ref[rows, :].astype(BF16)
            qhb, ktb = (qr * qh_ref[0]).astype(BF16), (kr * kt_ref[0]).astype(BF16)
            sc = (_dot_nt(qb, kb) * m_ref[0]).astype(BF16)
            o = o_ref[rows, :]
            rstd = _rstd(o)
            on = o * rstd
            gain = gain_ref[...]
            silu, dsilu = _silu_parts(g_ref[rows, :])
            dy = db_ref[rows, :]
            dgain_ref[...] += jnp.sum(dy * silu * on, axis=0, keepdims=True)
            dg = dy * on * gain * dsilu
            don = dy * silu * gain
            dob = (rstd * (don - on * jnp.mean(don * on, axis=-1, keepdims=True))).astype(BF16)
            gn = gstate[...]
            gb = gn.astype(BF16)
            da = (_dot_nt(dob, vb) * m_ref[0]).astype(BF16)
            dq = _dot(da, kb) + _dot_nt(dob, rp_ref[0, ci]) * qh_ref[0]
            dk = _dot_tn(da, qb) + _dot_nt(vb, gb) * kt_ref[0]
            dv = _dot_tn(sc, dob) + _dot(ktb, gb)
            gstate[...] = cd_ref[0] * gn + _dot_tn(qhb, dob)
            d_ref[0, rows, :] = _rope_t(dq, cos, sin).astype(BF16)
            d_ref[1, rows, :] = _rope_t(dk * scale, cos, sin).astype(BF16)
            d_ref[2, rows, :] = dv.astype(BF16)
            d_ref[3, rows, :] = dg.astype(BF16)
            return carry

        lax.fori_loop(0, nck, chunk, 0, unroll=True)

    rev = lambda s: nseg - 1 - s
    col, tab, head, cd, gain = _ret_specs(T, tseg, rev)
    act = pl.BlockSpec((tseg, Hd), lambda h, s: (rev(s), h))
    return _call(
        body, hosted, name=name, grid=(G, nseg),
        in_specs=[col(1), col(2), col(3), col(4), act, act, pl.BlockSpec((1, nck, Hd, Hd), lambda h, s: (h, rev(s), 0, 0)),
                  gain, tab, tab, head, head, head, cd, ANY],
        out_specs=[pl.BlockSpec((4, tseg, Hd), lambda h, s: (0, rev(s), h)), gain],
        out_shape=[jax.ShapeDtypeStruct(dproj.shape, BF16), jax.ShapeDtypeStruct((1, G * Hd), F32)],
        scratch_shapes=[pltpu.VMEM((Hd, Hd), F32)], aliased={14: 0},
        args=[proj, proj, proj, proj, db, o_pre, r_prev, ret_norm, cos2, sin2, intra, k_tail, q_head, chunk_decay, dproj])


def ffn_down_loss(s, w2, h, gain, target, name, hosted=()):
    T, F = s.shape
    D = h.shape[1]
    tm = min(T, 512)

    assert not hosted

    def body(s_hbm, w2_ref, h_hbm, g_ref, t_hbm, dh_hbm, loss_ref, dg_ref):
        loss_ref[...] = jnp.zeros_like(loss_ref)
        dg_ref[...] = jnp.zeros_like(dg_ref)

        def step(s_ref, h_ref, t_ref, dh_ref):
            hh = h_ref[...] + 0.5 * _dot(s_ref[...], w2_ref[...])
            gain_v = g_ref[...]
            err = hh * _rstd(hh) * gain_v - t_ref[...]
            loss_ref[...] += 0.5 * jnp.sum(jnp.mean(err * err, axis=-1, keepdims=True), axis=0, keepdims=True)
            dhn, dg = _rmsnorm_bwd(err * (1.0 / D), hh, gain_v)
            dh_ref[...] = dhn
            dg_ref[...] += jnp.sum(dg, axis=0, keepdims=True)

        row_spec = pl.BlockSpec((tm, D), lambda i: (i, 0))
        pltpu.emit_pipeline(
            step, grid=(T // tm,),
            in_specs=[pl.BlockSpec((tm, F), lambda i: (i, 0)), row_spec, row_spec],
            out_specs=[row_spec],
        )(s_hbm, h_hbm, t_hbm, dh_hbm)

    whole = pl.BlockSpec(memory_space=pltpu.VMEM)
    outs = pl.pallas_call(
        body, name=name,
        in_specs=[ANY, whole, ANY, whole, ANY],
        out_specs=[ANY, whole, whole],
        out_shape=[jax.ShapeDtypeStruct((T, D), F32), jax.ShapeDtypeStruct((1, LANES), F32), jax.ShapeDtypeStruct((1, D), F32)],
        compiler_params=pltpu.CompilerParams(vmem_limit_bytes=VMEM_LIMIT_V7X),
    )(s, w2, h, gain, target)
    return outs, ()


def prereduce(grads, recvs, place, name):
    nt = len(grads)
    nsh, R, C = grads[0].shape
    rh = R // 2

    def body(place_ref, *refs):
        for t in range(nt):
            g_ref, r_ref, o_ref, own_ref = refs[2 * t], refs[2 * t + 1], refs[2 * nt + 2 * t], refs[2 * nt + 2 * t + 1]
            piece = (g_ref[...].astype(F32) + r_ref[...].astype(F32)).astype(BF16)
            o_ref[...] = piece

            @pl.when(pl.program_id(0) == place_ref[1])
            def _():
                own_ref[...] = piece

    outs = pl.pallas_call(
        body, name=name,
        grid_spec=pltpu.PrefetchScalarGridSpec(
            num_scalar_prefetch=1, grid=(nsh,),
            in_specs=[pl.BlockSpec((1, rh, C), lambda j, p: (j, p[0], 0)), pl.BlockSpec((1, rh, C), lambda j, p: (j, 0, 0))] * nt,
            out_specs=[pl.BlockSpec((1, rh, C), lambda j, p: (j, 0, 0)),
                       pl.BlockSpec((1, rh, C), lambda j, p: (p[1], p[0], 0))] * nt),
        out_shape=[jax.ShapeDtypeStruct((nsh, rh, C), BF16), jax.ShapeDtypeStruct((nsh, R, C), BF16)] * nt,
        compiler_params=pltpu.CompilerParams(vmem_limit_bytes=VMEM_LIMIT_V7X),
    )(place, *[a for pair in zip(grads, recvs) for a in pair])
    return [(outs[2 * t], outs[2 * t + 1]) for t in range(nt)]


def _adamw(w, g, m, v):
    m = ADAM_B1 * m + (1.0 - ADAM_B1) * g
    v = ADAM_B2 * v + (1.0 - ADAM_B2) * (g * g)
    m_hat = m / (1.0 - ADAM_B1 ** ADAM_STEP)
    v_hat = v / (1.0 - ADAM_B2 ** ADAM_STEP)
    return -ADAM_LR * (m_hat / (jnp.sqrt(v_hat) + ADAM_EPS) + ADAM_WD * w), m, v


def adamw_sharded(tensors, name, hosted=()):
    nt = len(tensors)
    nsh = tensors[0][0].shape[0]
    shapes = [t[0].shape[1:] for t in tensors]

    def fits(steps):
        if any(R % (steps * BF16_TILE_ROWS) for R, _ in shapes):
            return False
        return sum(2 * (R // steps) * -(-C // LANES) * LANES * (nsh * 2 + 7 * 4) for R, C in shapes) <= ADAMW_VMEM_BUDGET

    steps = min(s for s in range(1, min(R for R, _ in shapes) // BF16_TILE_ROWS + 1) if fits(s))

    def body(*refs):
        ins, outs = refs[:4 * nt], refs[4 * nt:]
        for t in range(nt):
            p_ref, w_ref, m_ref, v_ref = ins[4 * t:4 * t + 4]
            g_ref, d_ref, nm_ref, nv_ref = outs[4 * t:4 * t + 4]
            g = p_ref[0].astype(F32)
            for i in range(1, nsh):
                g += p_ref[i].astype(F32)
            g_ref[...] = g
            d_ref[...], nm_ref[...], nv_ref[...] = _adamw(w_ref[...], g, m_ref[...], v_ref[...])

    in_specs, out_specs, out_shape = [], [], []
    for R, C in shapes:
        spec = pl.BlockSpec((R // steps, C), lambda i: (i, 0))
        in_specs += [pl.BlockSpec((nsh, R // steps, C), lambda i: (0, i, 0)), spec, spec, spec]
        out_specs += [spec] * 4
        out_shape += [jax.ShapeDtypeStruct((R, C), F32)] * 4
    return _call(body, hosted, name=name, grid=(steps,), in_specs=in_specs, out_specs=out_specs, out_shape=out_shape,
                 args=[a for tensor in tensors for a in tensor])


def adamw_small(packs, params, loss_packs, name):
    n = len(packs)
    ndev = loss_packs.shape[0]

    def body(*refs):
        p_refs, loss_ref, wmv = refs[:n], refs[n], refs[n + 1:4 * n + 1]
        outs, loss_out = refs[4 * n + 1:8 * n + 1], refs[8 * n + 1]
        total = lambda r: sum((r[i] for i in range(1, ndev)), r[0])
        loss_out[...] = total(loss_ref)
        for k in range(n):
            g = total(p_refs[k])
            outs[4 * k][...] = g
            outs[4 * k + 1][...], outs[4 * k + 2][...], outs[4 * k + 3][...] = _adamw(
                wmv[3 * k][...], g, wmv[3 * k + 1][...], wmv[3 * k + 2][...])

    out_shape = [jax.ShapeDtypeStruct(p[0].shape, F32) for p in params for _ in range(4)]
    outs = pl.pallas_call(body, name=name, out_shape=out_shape + [jax.ShapeDtypeStruct(loss_packs.shape[1:], F32)],
                          compiler_params=pltpu.CompilerParams(vmem_limit_bytes=VMEM_LIMIT_V7X),
                          )(*packs, loss_packs, *[a for p in params for a in p])
    return [outs[4 * k:4 * k + 4] for k in range(n)], outs[4 * n]


BIG = ("ffn1_w1", "ffn1_w3", "ffn1_w2", "w_in", "w_out", "ffn2_w1", "ffn2_w3", "ffn2_w2")
TRANSPOSED = ("ffn1_w1", "ffn1_w3", "ffn2_w1", "ffn2_w3")
SMALL = ("pool_w", "mix_norm", "pool_scale", "ret_norm", "ffn2_norm", "final_norm", "ffn1_norm")
WEIGHTS = ("ffn1_norm", "ffn1_w1", "ffn1_w3", "ffn1_w2", "mix_norm", "w_in", "pool_w", "pool_scale", "ret_norm", "w_out",
           "ffn2_norm", "ffn2_w1", "ffn2_w3", "ffn2_w2", "final_norm")


def kernel(x, ffn1_norm, ffn1_w1, ffn1_w3, ffn1_w2, mix_norm, w_in, pool_w, pool_scale, ret_norm, w_out, ffn2_norm, ffn2_w1, ffn2_w3, ffn2_w2, final_norm, loss_target, m_ffn1_norm, m_ffn1_w1, m_ffn1_w3, m_ffn1_w2, m_mix_norm, m_w_in, m_pool_w, m_pool_scale, m_ret_norm, m_w_out, m_ffn2_norm, m_ffn2_w1, m_ffn2_w3, m_ffn2_w2, m_final_norm, v_ffn1_norm, v_ffn1_w1, v_ffn1_w3, v_ffn1_w2, v_mix_norm, v_w_in, v_pool_w, v_pool_scale, v_ret_norm, v_w_out, v_ffn2_norm, v_ffn2_w1, v_ffn2_w3, v_ffn2_w2, v_final_norm):
    w = dict(ffn1_norm=ffn1_norm, ffn1_w1=ffn1_w1, ffn1_w3=ffn1_w3, ffn1_w2=ffn1_w2, mix_norm=mix_norm, w_in=w_in, pool_w=pool_w,
             pool_scale=pool_scale, ret_norm=ret_norm, w_out=w_out, ffn2_norm=ffn2_norm, ffn2_w1=ffn2_w1, ffn2_w3=ffn2_w3,
             ffn2_w2=ffn2_w2, final_norm=final_norm)
    m = dict(ffn1_norm=m_ffn1_norm, ffn1_w1=m_ffn1_w1, ffn1_w3=m_ffn1_w3, ffn1_w2=m_ffn1_w2, mix_norm=m_mix_norm, w_in=m_w_in,
             pool_w=m_pool_w, pool_scale=m_pool_scale, ret_norm=m_ret_norm, w_out=m_w_out, ffn2_norm=m_ffn2_norm, ffn2_w1=m_ffn2_w1,
             ffn2_w3=m_ffn2_w3, ffn2_w2=m_ffn2_w2, final_norm=m_final_norm)
    v = dict(ffn1_norm=v_ffn1_norm, ffn1_w1=v_ffn1_w1, ffn1_w3=v_ffn1_w3, ffn1_w2=v_ffn1_w2, mix_norm=v_mix_norm, w_in=v_w_in,
             pool_w=v_pool_w, pool_scale=v_pool_scale, ret_norm=v_ret_norm, w_out=v_w_out, ffn2_norm=v_ffn2_norm, ffn2_w1=v_ffn2_w1,
             ffn2_w3=v_ffn2_w3, ffn2_w2=v_ffn2_w2, final_norm=v_final_norm)
    xs, target = x[0], loss_target[0]
    T = xs.shape[0]
    tables = _ret_tables(T)
    place = jnp.stack([lax.axis_index("c"), 2 * lax.axis_index("x") + lax.axis_index("y")]).astype(jnp.int32)
    local = lambda d, k: jnp.transpose(d[k][0]) if k in TRANSPOSED else d[k][0]
    result = lambda o, k: jnp.transpose(o)[None] if k in TRANSPOSED else o[None]
    first = ("ffn1_w1", "ffn1_w3")
    sh = {k: local(w, k).astype(BF16) for k in first}
    gather = lambda *names: [ChipExchange([sh[k] for k in names], False)]
    wg, grad, delta, new_m, new_v = {}, {}, {}, {}, {}

    def update(names, pieces, name, hosted=()):
        outs, extras = adamw_sharded([(p, local(w, k), local(m, k), local(v, k)) for k, p in zip(names, pieces)], name, hosted)
        for t, k in enumerate(names):
            grad[k], delta[k], new_m[k], new_v[k] = [result(o, k) for o in outs[4 * t:4 * t + 4]]
        return extras

    def reduce_in_chip(name, *pairs):
        reduced = prereduce([p for p, _ in pairs], [r for _, r in pairs], place, "prereduce_" + name)
        return reduced[0] if len(pairs) == 1 else reduced

    scatter = lambda *reduced: ChipExchange([r[0] for r in reduced], True, [r[1] for r in reduced])
    whole = lambda k: wg[k].reshape(-1, wg[k].shape[-1])
    sharded = lambda g: g.reshape(N_CHIPS, -1, g.shape[-1])

    later = [k for k in BIG if k not in first]
    casts, ((wg["ffn1_w1"], wg["ffn1_w3"]),) = cast_shards([local(w, k) for k in later], "cast_gather_ffn1", gather(*first))
    sh.update(zip(later, casts))
    (n1, ga1, gb1, s1), ((wg["ffn1_w2"], wg["w_in"]),) = ffn_up(
        xs, ffn1_norm, whole("ffn1_w1"), whole("ffn1_w3"), "ffn1_up", gather("ffn1_w2", "w_in"))
    (h1, u, proj), ((wg["w_out"], wg["ffn2_w1"]),) = ffn_down_mix_in(
        s1, whole("ffn1_w2"), xs, mix_norm, wg["w_in"], "ffn1_down_mix_in", gather("w_out", "ffn2_w1"))
    (pa,), _ = pool_fwd(proj, pool_w[0], pool_scale, "pool_fwd")
    (rb, o_pre, r_prev), ((wg["ffn2_w3"],),) = ret_fwd(proj, ret_norm, tables, "ret_fwd", gather("ffn2_w3"))
    (h2, n2, ga2, gb2, s2), ((wg["ffn2_w2"],),) = ffn_up(
        h1, ffn2_norm, whole("ffn2_w1"), whole("ffn2_w3"), "mix_out_ffn2_up", gather("ffn2_w2"), mixed=(pa, rb, wg["w_out"]))
    (dh3, loss, d_final), _ = ffn_down_loss(s2, whole("ffn2_w2"), h2, final_norm[None], target, "ffn2_down_loss")

    (da2, db2, df2), _ = ffn_bwd_act(dh3, whole("ffn2_w2"), ga2, gb2, "ffn2_bwd_act")
    (g_f2w2,), _ = ffn_dw([s2], df2, 1, "ffn2_dw2")
    g_f2w2 = sharded(g_f2w2)
    (g_f2w1, g_f2w3), ((r_f2w2,),) = ffn_dw([da2, db2], n2, 2, "ffn2_dw13", [SiblingExchange([g_f2w2])])
    g_f2w1, g_f2w3 = sharded(g_f2w1), sharded(g_f2w3)
    p_f2w2 = reduce_in_chip("ffn2_w2", (g_f2w2, r_f2w2))
    (dh2, d_ffn2), ((q_f2w2,), (r_f2w1, r_f2w3)) = ffn_bwd_in(
        da2, db2, whole("ffn2_w1"), whole("ffn2_w3"), h2, ffn2_norm, dh3, "ffn2_bwd_in",
        [scatter(p_f2w2), SiblingExchange([g_f2w1, g_f2w3])])
    p_f2w1, p_f2w3 = reduce_in_chip("ffn2_w13", (g_f2w1, r_f2w1), (g_f2w3, r_f2w3))
    (dpa, drb, g_wout), _ = mix_out_bwd(dh2, wg["w_out"], pa, rb, "mix_out_bwd")
    (dproj, d_pool_w, d_pool_scale), _ = pool_bwd(proj, dpa, pool_w[0], pool_scale, "pool_bwd")
    (dproj, d_ret_norm), ((q_f2w1, q_f2w3), (r_wout,)) = ret_bwd(
        proj, drb, o_pre, r_prev, ret_norm, tables, dproj, "ret_bwd", [scatter(p_f2w1, p_f2w3), SiblingExchange([g_wout])])
    p_wout = reduce_in_chip("w_out", (g_wout, r_wout))
    (g_win,), ((q_wout,),) = mix_dwin(u, dproj, N_CHIPS, "mix_dwin", [scatter(p_wout)])
    (dh1, d_mix), ((r_win,),) = mix_in_bwd(dproj, wg["w_in"], h1, mix_norm, dh2, "mix_in_bwd", [SiblingExchange([g_win])])
    p_win = reduce_in_chip("w_in", (g_win, r_win))
    (da1, db1, df1), ((q_win,),) = ffn_bwd_act(dh1, whole("ffn1_w2"), ga1, gb1, "ffn1_bwd_act", [scatter(p_win)])
    d_small = {"pool_w": d_pool_w.reshape(-1, LANES), "mix_norm": d_mix, "pool_scale": d_pool_scale, "ret_norm": d_ret_norm,
               "ffn2_norm": d_ffn2, "final_norm": d_final}
    (g_f1w1, g_f1w3), (packs,) = ffn_dw([da1, db1], n1, 2, "ffn1_dw13", [AllExchange([d_small[k] for k in SMALL[:-1]] + [loss])])
    g_f1w1, g_f1w3 = sharded(g_f1w1), sharded(g_f1w3)
    (g_f1w2,), ((r_f1w1, r_f1w3),) = ffn_dw([s1], df1, 1, "ffn1_dw2", [SiblingExchange([g_f1w1, g_f1w3])])
    g_f1w2 = sharded(g_f1w2)
    p_f1w1, p_f1w3 = reduce_in_chip("ffn1_w13", (g_f1w1, r_f1w1), (g_f1w3, r_f1w3))
    (dx, d_ffn1), ((q_f1w1, q_f1w3), (r_f1w2,)) = ffn_bwd_in(
        da1, db1, whole("ffn1_w1"), whole("ffn1_w3"), xs, ffn1_norm, dh1, "ffn1_bwd_in",
        [scatter(p_f1w1, p_f1w3), SiblingExchange([g_f1w2])])
    p_f1w2 = reduce_in_chip("ffn1_w2", (g_f1w2, r_f1w2))

    (q_f1w2,), (late,) = update(["w_in", "w_out"], [q_win, q_wout], "adamw_mix", [scatter(p_f1w2), AllExchange([d_ffn1])])
    update(["ffn2_w2", "ffn2_w1", "ffn2_w3", "ffn1_w1", "ffn1_w3", "ffn1_w2"],
           [q_f2w2, q_f2w1, q_f2w3, q_f1w1, q_f1w3, q_f1w2], "adamw_ffn")
    flat = lambda t, k: t[k].reshape(-1, LANES) if k == "pool_w" else t[k].reshape(1, -1)
    updated, loss_sum = adamw_small(packs[:-1] + [late], [[flat(t, k) for t in (w, m, v)] for k in SMALL], packs[-1], "adamw_small")
    for k, outs in zip(SMALL, updated):
        grad[k], delta[k], new_m[k], new_v[k] = [o.reshape(w[k].shape) for o in outs]
    loss = loss_sum[0, 0]

    return (loss, dx[None], *[grad[k] for k in WEIGHTS], *[delta[k] for k in WEIGHTS],
            *[new_m[k] for k in WEIGHTS], *[new_v[k] for k in WEIGHTS])
```
